```python
import math
import jax
import jax.numpy as jnp
from jax import lax
import numpy as np

D_MODEL = 1024
BATCH = 16
SEQ = 4096
DEPTH = 2

GRID_W = 64
CTX_LEN = 256
EPS = 1e-6

SSD_HEADS = 6
SSD_HEAD_DIM = 64
SSD_INNER = SSD_HEADS * SSD_HEAD_DIM
SSD_GROUPS = 2
SSD_STATE = 128
SSD_CONV = 4
SSD_CHUNK = 128
XBC_DIM = SSD_INNER + 2 * SSD_GROUPS * SSD_STATE

MLA_HEADS = 6
Q_LORA = 256
KV_LORA = 256
QK_NOPE = 64
QK_ROPE = 32
V_HEAD = 64
QK_DIM = QK_NOPE + QK_ROPE
MLA_OUT = MLA_HEADS * V_HEAD
ROPE_THETA = 10000.0
ROPE_PAIRS = QK_ROPE // 4
Q_BLOCK = 128

POOL_WINDOWS = (2, 4, 8, 16)
POOL_GROUPS = len(POOL_WINDOWS)
POOL_GROUP_DIM = 64
POOL_DIM = POOL_GROUPS * POOL_GROUP_DIM

MIX_DIM = SSD_INNER + MLA_OUT + POOL_DIM
D_FF = 4 * D_MODEL

OFF_Z = 0
OFF_XBC = OFF_Z + SSD_INNER
OFF_DT = OFF_XBC + XBC_DIM
OFF_QA = OFF_DT + 2 * SSD_HEADS
OFF_KVA = OFF_QA + Q_LORA
OFF_KROPE = OFF_KVA + KV_LORA
OFF_POOL = OFF_KROPE + QK_ROPE
IN_COLS = OFF_POOL + POOL_DIM

kernel_name = "hybrid_ssd_mla_pool_diffusion_block"


def rmsnorm(u, w):
    uf = u.astype(jnp.float32)
    y = uf * lax.rsqrt(jnp.mean(uf * uf, axis=-1, keepdims=True) + EPS)
    return (y * w.astype(jnp.float32)).astype(u.dtype)


def adaln(cond, mod_w, mod_b):
    m = jax.nn.silu(cond) @ mod_w + mod_b
    return jnp.split(m[..., None, :], 6, axis=-1)


def modulate(h, shift, scale):
    return h * (1.0 + scale) + shift


def squared_relu_mlp(h, w1, w2):
    return jnp.square(jax.nn.relu(h @ w1)) @ w2


def axial_rope_tables(n):
    rows = n // GRID_W
    row = jnp.repeat(jnp.arange(rows, dtype=jnp.float32), GRID_W)
    col = jnp.tile(jnp.arange(GRID_W, dtype=jnp.float32), rows)
    inv_freq = ROPE_THETA ** (-jnp.arange(ROPE_PAIRS, dtype=jnp.float32) / ROPE_PAIRS)
    ang = jnp.stack([row[:, None] * inv_freq, col[:, None] * inv_freq], axis=1)
    return jnp.cos(ang), jnp.sin(ang)


def apply_axial_rope(u, cos, sin):
    shp = u.shape
    ur = u.astype(jnp.float32).reshape(shp[:-1] + (2, 2, ROPE_PAIRS))
    u1, u2 = ur[..., 0, :], ur[..., 1, :]
    bshape = (shp[1],) + (1,) * (u.ndim - 3) + (2, ROPE_PAIRS)
    cos = cos.reshape(bshape)
    sin = sin.reshape(bshape)
    out = jnp.stack([u1 * cos - u2 * sin, u2 * cos + u1 * sin], axis=-2)
    return out.reshape(shp).astype(u.dtype)


def depthwise_conv_centred(u, w, b):
    k = w.shape[0]
    out = lax.conv_general_dilated(u, w[:, None, :].astype(u.dtype), (1,), [((k - 1) // 2, k // 2)],
                                   dimension_numbers=('NWC', 'WIO', 'NWC'),
                                   feature_group_count=u.shape[-1])
    return out + b


def ssd_inputs(proj, conv_w, conv_b, dt_bias):
    bsz, n = proj.shape[:2]
    z = proj[..., OFF_Z:OFF_XBC]
    xbc = jax.nn.silu(depthwise_conv_centred(proj[..., OFF_XBC:OFF_DT], conv_w, conv_b))
    xs = xbc[..., :SSD_INNER].reshape(bsz, n, SSD_HEADS, SSD_HEAD_DIM)
    bm = xbc[..., SSD_INNER:SSD_INNER + SSD_GROUPS * SSD_STATE].reshape(bsz, n, SSD_GROUPS, SSD_STATE)
    cm = xbc[..., SSD_INNER + SSD_GROUPS * SSD_STATE:].reshape(bsz, n, SSD_GROUPS, SSD_STATE)
    dt_raw = proj[..., OFF_DT:OFF_QA].astype(jnp.float32).reshape(bsz, n, 2, SSD_HEADS)
    dt = jax.nn.softplus(dt_raw + dt_bias.astype(jnp.float32))
    return z, xs, bm, cm, dt


def segsum_exp(a):
    l = a.shape[-1]
    cs = jnp.cumsum(a, axis=-1)
    diff = cs[..., :, None] - cs[..., None, :]
    mask = jnp.tril(jnp.ones((l, l), dtype=bool))
    return jnp.exp(jnp.where(mask, diff, -jnp.inf))


def ssd_chunked_scan(xs, dt, a, b_mat, c_mat, h0):
    f32 = jnp.float32
    bsz, n = xs.shape[:2]
    nc = n // SSD_CHUNK
    r = SSD_HEADS // SSD_GROUPS
    dt = dt.astype(f32)
    x = (xs.astype(f32) * dt[..., None]).reshape(bsz, nc, SSD_CHUNK, SSD_GROUPS, r, SSD_HEAD_DIM)
    adt = jnp.moveaxis((dt * a.astype(f32)).reshape(bsz, nc, SSD_CHUNK, SSD_GROUPS, r), 2, -1)
    bm = b_mat.astype(f32).reshape(bsz, nc, SSD_CHUNK, SSD_GROUPS, SSD_STATE)
    cm = c_mat.astype(f32).reshape(bsz, nc, SSD_CHUNK, SSD_GROUPS, SSD_STATE)
    a_cs = jnp.cumsum(adt, axis=-1)
    l_mat = segsum_exp(adt)
    cb = jnp.einsum('bclgn,bcsgn->bcgls', cm, bm)
    y_diag = jnp.einsum('bcgrls,bcsgrp->bclgrp', cb[:, :, :, None] * l_mat, x)
    decay_to_end = jnp.exp(a_cs[..., -1:] - a_cs)
    chunk_states = jnp.einsum('bclgn,bcgrl,bclgrp->bcgrpn', bm, decay_to_end, x)
    chunk_decay = jnp.exp(a_cs[..., -1])

    def carry_step(h, inp):
        s_c, d_c = inp
        return h * d_c[..., None, None] + s_c, h

    h_init = h0.astype(f32).reshape(bsz, SSD_GROUPS, r, SSD_HEAD_DIM, SSD_STATE)
    h_last, h_in = lax.scan(carry_step, h_init,
                            (jnp.moveaxis(chunk_states, 1, 0), jnp.moveaxis(chunk_decay, 1, 0)))
    h_in = jnp.moveaxis(h_in, 0, 1)
    y_off = jnp.einsum('bclgn,bcgrpn,bcgrl->bclgrp', cm, h_in, jnp.exp(a_cs))
    y = (y_diag + y_off).reshape(bsz, n, SSD_HEADS, SSD_HEAD_DIM)
    return y, h_last.reshape(bsz, SSD_HEADS, SSD_HEAD_DIM, SSD_STATE)


def ssd_bidirectional(xs, bm, cm, dt, a, h0_fwd, h0_bwd):
    flip = lambda t: jnp.flip(t, axis=1)
    y_f, h_f = ssd_chunked_scan(xs, dt[:, :, 0], a[0], bm, cm, h0_fwd)
    y_b, h_b = ssd_chunked_scan(flip(xs), flip(dt[:, :, 1]), a[1], flip(bm), flip(cm), h0_bwd)
    return y_f + flip(y_b), h_f, h_b


def ssd_output(y, xs, z, d_skip, norm_w):
    bsz, n = y.shape[:2]
    y = y + xs.astype(jnp.float32) * d_skip.astype(jnp.float32)[:, None]
    g = (y.reshape(bsz, n, SSD_INNER) * jax.nn.silu(z.astype(jnp.float32)))
    g = g.reshape(bsz, n, SSD_GROUPS, SSD_INNER // SSD_GROUPS)
    g = g * lax.rsqrt(jnp.mean(g * g, axis=-1, keepdims=True) + EPS)
    return (g.reshape(bsz, n, SSD_INNER) * norm_w.astype(jnp.float32)).astype(z.dtype)


def mla_qkv(proj, q_a_norm_w, w_q_b, kv_a_norm_w, w_kv_b, rope):
    bsz, n = proj.shape[:2]
    cq = rmsnorm(proj[..., OFF_QA:OFF_KVA], q_a_norm_w)
    q = (cq @ w_q_b).reshape(bsz, n, MLA_HEADS, QK_DIM)
    ckv = rmsnorm(proj[..., OFF_KVA:OFF_KROPE], kv_a_norm_w)
    k_rope = proj[..., OFF_KROPE:OFF_POOL]
    kv = (ckv @ w_kv_b).reshape(bsz, n, MLA_HEADS, QK_NOPE + V_HEAD)
    k_nope, v = kv[..., :QK_NOPE], kv[..., QK_NOPE:]
    q_nope, q_rope = q[..., :QK_NOPE], q[..., QK_NOPE:]
    if rope is not None:
        cos, sin = rope
        q_rope = apply_axial_rope(q_rope, cos, sin)
        k_rope = apply_axial_rope(k_rope, cos, sin)
    q = jnp.concatenate([q_nope, q_rope], axis=-1)
    k = jnp.concatenate([k_nope, jnp.broadcast_to(k_rope[:, :, None, :], (bsz, n, MLA_HEADS, QK_ROPE))], axis=-1)
    return q, k, v


def attend(q, k, v):
    s = jnp.einsum('bqhd,bkhd->bhqk', q, k, preferred_element_type=jnp.float32) * (QK_DIM ** -0.5)
    p = jax.nn.softmax(s, axis=-1).astype(v.dtype)
    return jnp.einsum('bhqk,bkhd->bqhd', p, v)


def attend_blocked(q, k, v):
    bsz, n, h, dq = q.shape
    qb = jnp.moveaxis(q.reshape(bsz, n // Q_BLOCK, Q_BLOCK, h, dq), 1, 0)
    out = lax.map(lambda qi: attend(qi, k, v), qb)
    return jnp.moveaxis(out, 0, 1).reshape(bsz, n, h, v.shape[-1])


def multiscale_pool(u, pool_w, pool_scale):
    bsz, n, _ = u.shape
    uf = u.astype(jnp.float32)
    cs = jnp.pad(jnp.cumsum(uf, axis=1), ((0, 0), (1, 0), (0, 0)))
    t = jnp.arange(n)
    outs = []
    for gi, w in enumerate(POOL_WINDOWS):
        sl = slice(gi * POOL_GROUP_DIM, (gi + 1) * POOL_GROUP_DIM)
        lo = jnp.clip(t - w // 2, 0, n)
        hi = jnp.clip(t + w - w // 2, 0, n)
        csg = cs[..., sl]
        mean = (jnp.take(csg, hi, axis=1) - jnp.take(csg, lo, axis=1)) / (hi - lo).astype(jnp.float32)[:, None]
        outs.append(mean - uf[..., sl])
    d = jnp.stack(outs, axis=2)
    y = jnp.einsum('blgc,gcd->blgd', d, pool_w.astype(jnp.float32)).reshape(bsz, n, POOL_DIM)
    return (y * pool_scale.astype(jnp.float32)).astype(u.dtype)


def hybrid_layer(x, ctx, c, c_ctx, rope, mod_w, mod_b, norm1_w, norm2_w, w_in, conv_w, conv_b,
                 dt_bias, a_log, ssd_d, ssd_norm_w, q_a_norm_w, w_q_b, kv_a_norm_w, w_kv_b,
                 pool_w, pool_scale, w_out, w_mlp1, w_mlp2, update_ctx):
    bsz, n, _ = x.shape
    m = ctx.shape[1]
    sh1, sc1, g1, sh2, sc2, g2 = adaln(c, mod_w, mod_b)
    csh1, csc1, cg1, csh2, csc2, cg2 = adaln(c_ctx, mod_w, mod_b)
    px = modulate(rmsnorm(x, norm1_w), sh1, sc1) @ w_in
    pc = modulate(rmsnorm(ctx, norm1_w), csh1, csc1) @ w_in

    a = -jnp.exp(a_log.astype(jnp.float32))
    zc, xsc, bmc, cmc, dtc = ssd_inputs(pc, conv_w, conv_b, dt_bias)
    zx, xsx, bmx, cmx, dtx = ssd_inputs(px, conv_w, conv_b, dt_bias)
    h_zero = jnp.zeros((bsz, SSD_HEADS, SSD_HEAD_DIM, SSD_STATE), jnp.float32)
    yc, hf_ctx, hb_ctx = ssd_bidirectional(xsc, bmc, cmc, dtc, a, h_zero, h_zero)
    yx, _, _ = ssd_bidirectional(xsx, bmx, cmx, dtx, a, hf_ctx, hb_ctx)
    ssd_x = ssd_output(yx, xsx, zx, ssd_d, ssd_norm_w)

    qc, kc, vc = mla_qkv(pc, q_a_norm_w, w_q_b, kv_a_norm_w, w_kv_b, None)
    qx, kx, vx = mla_qkv(px, q_a_norm_w, w_q_b, kv_a_norm_w, w_kv_b, rope)
    attn_x = attend_blocked(qx, jnp.concatenate([kx, kc], axis=1), jnp.concatenate([vx, vc], axis=1))

    pool_x = multiscale_pool(px[..., OFF_POOL:], pool_w, pool_scale)

    mix_x = jnp.concatenate([ssd_x, attn_x.reshape(bsz, n, MLA_OUT), pool_x], axis=-1) @ w_out
    x = x + g1 * mix_x
    x = x + g2 * squared_relu_mlp(modulate(rmsnorm(x, norm2_w), sh2, sc2), w_mlp1, w_mlp2)

    if update_ctx:
        ssd_c = ssd_output(yc, xsc, zc, ssd_d, ssd_norm_w)
        attn_c = attend(qc, kc, vc).reshape(bsz, m, MLA_OUT)
        pool_c = multiscale_pool(pc[..., OFF_POOL:], pool_w, pool_scale)
        mix_c = jnp.concatenate([ssd_c, attn_c, pool_c], axis=-1) @ w_out
        ctx = ctx + cg1 * mix_c
        ctx = ctx + cg2 * squared_relu_mlp(modulate(rmsnorm(ctx, norm2_w), csh2, csc2), w_mlp1, w_mlp2)
    return x, ctx


def _fwd_setup_inputs(seed: int = 0) -> dict:
    key = jax.random.key(seed)
    ks = jax.random.split(key, 32)
    f32 = jnp.float32

    def nrm(k, shape, scale):
        return jax.random.normal(k, shape, f32) * scale

    def gain(k, shape):
        return 1.0 + 0.02 * jax.random.normal(k, shape, f32)

    dt0 = jnp.exp(jax.random.uniform(ks[11], (DEPTH, 2, SSD_HEADS), f32, math.log(1e-3), math.log(1e-1)))
    return {
        "x": nrm(ks[0], (BATCH, SEQ, D_MODEL), 1.0),
        "c": nrm(ks[1], (BATCH, D_MODEL), 1.0),
        "ctx": nrm(ks[2], (BATCH, CTX_LEN, D_MODEL), 1.0),
        "c_ctx": nrm(ks[3], (D_MODEL,), 1.0),
        "mod_w": nrm(ks[4], (DEPTH, D_MODEL, 6 * D_MODEL), 0.5 * D_MODEL ** -0.5),
        "mod_b": nrm(ks[5], (DEPTH, 6 * D_MODEL), 0.01),
        "norm1_w": gain(ks[6], (DEPTH, D_MODEL)),
        "norm2_w": gain(ks[7], (DEPTH, D_MODEL)),
        "w_in": nrm(ks[8], (DEPTH, D_MODEL, IN_COLS), D_MODEL ** -0.5),
        "conv_w": nrm(ks[9], (DEPTH, SSD_CONV, XBC_DIM), SSD_CONV ** -0.5),
        "conv_b": nrm(ks[10], (DEPTH, XBC_DIM), 0.01),
        "dt_bias": dt0 + jnp.log(-jnp.expm1(-dt0)),
        "a_log": jnp.log(jax.random.uniform(ks[12], (DEPTH, 2, SSD_HEADS), f32, 1.0, 16.0)),
        "ssd_d": 1.0 + 0.1 * jax.random.normal(ks[13], (DEPTH, SSD_HEADS), f32),
        "ssd_norm_w": gain(ks[14], (DEPTH, SSD_INNER)),
        "q_a_norm_w": gain(ks[15], (DEPTH, Q_LORA)),
        "w_q_b": nrm(ks[16], (DEPTH, Q_LORA, MLA_HEADS * QK_DIM), Q_LORA ** -0.5),
        "kv_a_norm_w": gain(ks[17], (DEPTH, KV_LORA)),
        "w_kv_b": nrm(ks[18], (DEPTH, KV_LORA, MLA_HEADS * (QK_NOPE + V_HEAD)), KV_LORA ** -0.5),
        "pool_w": nrm(ks[19], (DEPTH, POOL_GROUPS, POOL_GROUP_DIM, POOL_GROUP_DIM), POOL_GROUP_DIM ** -0.5),
        "pool_scale": gain(ks[20], (DEPTH, POOL_DIM)),
        "w_out": nrm(ks[21], (DEPTH, MIX_DIM, D_MODEL), MIX_DIM ** -0.5),
        "w_mlp1": nrm(ks[22], (DEPTH, D_MODEL, D_FF), D_MODEL ** -0.5),
        "w_mlp2": nrm(ks[23], (DEPTH, D_FF, D_MODEL), D_FF ** -0.5),
        "final_norm_w": gain(ks[24], (D_MODEL,)),
    }


def _fwd_reference(x, c, ctx, c_ctx, mod_w, mod_b, norm1_w, norm2_w, w_in, conv_w, conv_b, dt_bias, a_log,
              ssd_d, ssd_norm_w, q_a_norm_w, w_q_b, kv_a_norm_w, w_kv_b, pool_w, pool_scale, w_out,
              w_mlp1, w_mlp2, final_norm_w):
    rope = axial_rope_tables(x.shape[1])
    for i in range(DEPTH):
        x, ctx = hybrid_layer(x, ctx, c, c_ctx, rope, mod_w[i], mod_b[i], norm1_w[i], norm2_w[i], w_in[i],
                              conv_w[i], conv_b[i], dt_bias[i], a_log[i], ssd_d[i], ssd_norm_w[i],
                              q_a_norm_w[i], w_q_b[i], kv_a_norm_w[i], w_kv_b[i], pool_w[i], pool_scale[i],
                              w_out[i], w_mlp1[i], w_mlp2[i], update_ctx=(i < DEPTH - 1))
    return rmsnorm(x, final_norm_w)


import jax as _jax
import jax.numpy as _jnp

TWIN_FORMAT = 'train_step'
FWD_PARAMS = ['x', 'c', 'ctx', 'c_ctx', 'mod_w', 'mod_b', 'norm1_w', 'norm2_w', 'w_in', 'conv_w', 'conv_b', 'dt_bias', 'a_log', 'ssd_d', 'ssd_norm_w', 'q_a_norm_w', 'w_q_b', 'kv_a_norm_w', 'w_kv_b', 'pool_w', 'pool_scale', 'w_out', 'w_mlp1', 'w_mlp2', 'final_norm_w']
TWIN_WEIGHTS = ['c_ctx', 'mod_w', 'mod_b', 'norm1_w', 'norm2_w', 'w_in', 'conv_w', 'conv_b', 'dt_bias', 'a_log', 'ssd_d', 'ssd_norm_w', 'q_a_norm_w', 'w_q_b', 'kv_a_norm_w', 'w_kv_b', 'pool_w', 'pool_scale', 'w_out', 'w_mlp1', 'w_mlp2', 'final_norm_w']
TWIN_DIFF_INPUT = 'x'
TWIN_INPUTS = ['x', 'c', 'ctx', 'c_ctx', 'mod_w', 'mod_b', 'norm1_w', 'norm2_w', 'w_in', 'conv_w', 'conv_b', 'dt_bias', 'a_log', 'ssd_d', 'ssd_norm_w', 'q_a_norm_w', 'w_q_b', 'kv_a_norm_w', 'w_kv_b', 'pool_w', 'pool_scale', 'w_out', 'w_mlp1', 'w_mlp2', 'final_norm_w', 'loss_target', 'm_c_ctx', 'm_mod_w', 'm_mod_b', 'm_norm1_w', 'm_norm2_w', 'm_w_in', 'm_conv_w', 'm_conv_b', 'm_dt_bias', 'm_a_log', 'm_ssd_d', 'm_ssd_norm_w', 'm_q_a_norm_w', 'm_w_q_b', 'm_kv_a_norm_w', 'm_w_kv_b', 'm_pool_w', 'm_pool_scale', 'm_w_out', 'm_w_mlp1', 'm_w_mlp2', 'm_final_norm_w', 'v_c_ctx', 'v_mod_w', 'v_mod_b', 'v_norm1_w', 'v_norm2_w', 'v_w_in', 'v_conv_w', 'v_conv_b', 'v_dt_bias', 'v_a_log', 'v_ssd_d', 'v_ssd_norm_w', 'v_q_a_norm_w', 'v_w_q_b', 'v_kv_a_norm_w', 'v_w_kv_b', 'v_pool_w', 'v_pool_scale', 'v_w_out', 'v_w_mlp1', 'v_w_mlp2', 'v_final_norm_w']
TWIN_OUTPUTS = ['loss', 'grad_x', 'grad_c_ctx', 'grad_mod_w', 'grad_mod_b', 'grad_norm1_w', 'grad_norm2_w', 'grad_w_in', 'grad_conv_w', 'grad_conv_b', 'grad_dt_bias', 'grad_a_log', 'grad_ssd_d', 'grad_ssd_norm_w', 'grad_q_a_norm_w', 'grad_w_q_b', 'grad_kv_a_norm_w', 'grad_w_kv_b', 'grad_pool_w', 'grad_pool_scale', 'grad_w_out', 'grad_w_mlp1', 'grad_w_mlp2', 'grad_final_norm_w', 'delta_c_ctx', 'delta_mod_w', 'delta_mod_b', 'delta_norm1_w', 'delta_norm2_w', 'delta_w_in', 'delta_conv_w', 'delta_conv_b', 'delta_dt_bias', 'delta_a_log', 'delta_ssd_d', 'delta_ssd_norm_w', 'delta_q_a_norm_w', 'delta_w_q_b', 'delta_kv_a_norm_w', 'delta_w_kv_b', 'delta_pool_w', 'delta_pool_scale', 'delta_w_out', 'delta_w_mlp1', 'delta_w_mlp2', 'delta_final_norm_w', 'new_m_c_ctx', 'new_m_mod_w', 'new_m_mod_b', 'new_m_norm1_w', 'new_m_norm2_w', 'new_m_w_in', 'new_m_conv_w', 'new_m_conv_b', 'new_m_dt_bias', 'new_m_a_log', 'new_m_ssd_d', 'new_m_ssd_norm_w', 'new_m_q_a_norm_w', 'new_m_w_q_b', 'new_m_kv_a_norm_w', 'new_m_w_kv_b', 'new_m_pool_w', 'new_m_pool_scale', 'new_m_w_out', 'new_m_w_mlp1', 'new_m_w_mlp2', 'new_m_final_norm_w', 'new_v_c_ctx', 'new_v_mod_w', 'new_v_mod_b', 'new_v_norm1_w', 'new_v_norm2_w', 'new_v_w_in', 'new_v_conv_w', 'new_v_conv_b', 'new_v_dt_bias', 'new_v_a_log', 'new_v_ssd_d', 'new_v_ssd_norm_w', 'new_v_q_a_norm_w', 'new_v_w_q_b', 'new_v_kv_a_norm_w', 'new_v_w_kv_b', 'new_v_pool_w', 'new_v_pool_scale', 'new_v_w_out', 'new_v_w_mlp1', 'new_v_w_mlp2', 'new_v_final_norm_w']
TWIN_LEAF_KINDS = {'loss': 'loss', 'grad_x': 'grad_x', 'grad_c_ctx': 'grad_w', 'grad_mod_w': 'grad_w', 'grad_mod_b': 'grad_w', 'grad_norm1_w': 'grad_w', 'grad_norm2_w': 'grad_w', 'grad_w_in': 'grad_w', 'grad_conv_w': 'grad_w', 'grad_conv_b': 'grad_w', 'grad_dt_bias': 'grad_w', 'grad_a_log': 'grad_w', 'grad_ssd_d': 'grad_w', 'grad_ssd_norm_w': 'grad_w', 'grad_q_a_norm_w': 'grad_w', 'grad_w_q_b': 'grad_w', 'grad_kv_a_norm_w': 'grad_w', 'grad_w_kv_b': 'grad_w', 'grad_pool_w': 'grad_w', 'grad_pool_scale': 'grad_w', 'grad_w_out': 'grad_w', 'grad_w_mlp1': 'grad_w', 'grad_w_mlp2': 'grad_w', 'grad_final_norm_w': 'grad_w', 'delta_c_ctx': 'delta_w', 'delta_mod_w': 'delta_w', 'delta_mod_b': 'delta_w', 'delta_norm1_w': 'delta_w', 'delta_norm2_w': 'delta_w', 'delta_w_in': 'delta_w', 'delta_conv_w': 'delta_w', 'delta_conv_b': 'delta_w', 'delta_dt_bias': 'delta_w', 'delta_a_log': 'delta_w', 'delta_ssd_d': 'delta_w', 'delta_ssd_norm_w': 'delta_w', 'delta_q_a_norm_w': 'delta_w', 'delta_w_q_b': 'delta_w', 'delta_kv_a_norm_w': 'delta_w', 'delta_w_kv_b': 'delta_w', 'delta_pool_w': 'delta_w', 'delta_pool_scale': 'delta_w', 'delta_w_out': 'delta_w', 'delta_w_mlp1': 'delta_w', 'delta_w_mlp2': 'delta_w', 'delta_final_norm_w': 'delta_w', 'new_m_c_ctx': 'new_m', 'new_m_mod_w': 'new_m', 'new_m_mod_b': 'new_m', 'new_m_norm1_w': 'new_m', 'new_m_norm2_w': 'new_m', 'new_m_w_in': 'new_m', 'new_m_conv_w': 'new_m', 'new_m_conv_b': 'new_m', 'new_m_dt_bias': 'new_m', 'new_m_a_log': 'new_m', 'new_m_ssd_d': 'new_m', 'new_m_ssd_norm_w': 'new_m', 'new_m_q_a_norm_w': 'new_m', 'new_m_w_q_b': 'new_m', 'new_m_kv_a_norm_w': 'new_m', 'new_m_w_kv_b': 'new_m', 'new_m_pool_w': 'new_m', 'new_m_pool_scale': 'new_m', 'new_m_w_out': 'new_m', 'new_m_w_mlp1': 'new_m', 'new_m_w_mlp2': 'new_m', 'new_m_final_norm_w': 'new_m', 'new_v_c_ctx': 'new_v', 'new_v_mod_w': 'new_v', 'new_v_mod_b': 'new_v', 'new_v_norm1_w': 'new_v', 'new_v_norm2_w': 'new_v', 'new_v_w_in': 'new_v', 'new_v_conv_w': 'new_v', 'new_v_conv_b': 'new_v', 'new_v_dt_bias': 'new_v', 'new_v_a_log': 'new_v', 'new_v_ssd_d': 'new_v', 'new_v_ssd_norm_w': 'new_v', 'new_v_q_a_norm_w': 'new_v', 'new_v_w_q_b': 'new_v', 'new_v_kv_a_norm_w': 'new_v', 'new_v_w_kv_b': 'new_v', 'new_v_pool_w': 'new_v', 'new_v_pool_scale': 'new_v', 'new_v_w_out': 'new_v', 'new_v_w_mlp1': 'new_v', 'new_v_w_mlp2': 'new_v', 'new_v_final_norm_w': 'new_v'}


def _forward(args):
    return _fwd_reference(*[args[k] for k in FWD_PARAMS])


def _output_shape():
    out = _jax.eval_shape(lambda: _forward(_fwd_setup_inputs(0)))
    return out.shape, out.dtype

N_MICROBATCH = 1
ADAM_LR = 0.001
ADAM_B1 = 0.9
ADAM_B2 = 0.999
ADAM_EPS = 1e-08
ADAM_WD = 0.01
ADAM_STEP = 10
PER_EXAMPLE_BATCH_AXIS = {'x': 0, 'c': 0, 'ctx': 0, 'loss_target': 0}
SHARED_INPUTS = []
_WEIGHT_DTYPES = {'c_ctx': _jnp.float32, 'mod_w': _jnp.float32, 'mod_b': _jnp.float32, 'norm1_w': _jnp.float32, 'norm2_w': _jnp.float32, 'w_in': _jnp.float32, 'conv_w': _jnp.float32, 'conv_b': _jnp.float32, 'dt_bias': _jnp.float32, 'a_log': _jnp.float32, 'ssd_d': _jnp.float32, 'ssd_norm_w': _jnp.float32, 'q_a_norm_w': _jnp.float32, 'w_q_b': _jnp.float32, 'kv_a_norm_w': _jnp.float32, 'w_kv_b': _jnp.float32, 'pool_w': _jnp.float32, 'pool_scale': _jnp.float32, 'w_out': _jnp.float32, 'w_mlp1': _jnp.float32, 'w_mlp2': _jnp.float32, 'final_norm_w': _jnp.float32}
MOMENT_SCALE = {'c_ctx': 1.139293e-02, 'mod_w': 1.217795e-01, 'mod_b': 1.979327e-01, 'norm1_w': 7.284417e-02, 'norm2_w': 1.116555e-01, 'w_in': 5.391271e-02, 'conv_w': 5.556804e-02, 'conv_b': 7.565505e-02, 'dt_bias': 2.262561e-01, 'a_log': 2.201985e-01, 'ssd_d': 3.151190e-01, 'ssd_norm_w': 1.019612e-01, 'q_a_norm_w': 8.327058e-03, 'w_q_b': 5.933333e-03, 'kv_a_norm_w': 3.863055e-02, 'w_kv_b': 2.000567e-02, 'pool_w': 6.706835e-02, 'pool_scale': 6.810406e-02, 'w_out': 5.910850e-02, 'w_mlp1': 5.744932e-02, 'w_mlp2': 1.036890e-01, 'final_norm_w': 6.468598e+01}


def _to_microbatches(a, axis):
    t = _jnp.moveaxis(a, axis, 0)
    t = t.reshape((N_MICROBATCH, t.shape[0] // N_MICROBATCH) + t.shape[1:])
    return _jnp.moveaxis(t, 1, axis + 1)


def setup_inputs(seed: int = 0) -> dict:
    inp = _fwd_setup_inputs(seed)
    key = _jax.random.fold_in(_jax.random.key(seed), 7919)
    shape, _ = _output_shape()
    out = dict(inp)
    out["loss_target"] = _jax.random.normal(_jax.random.fold_in(key, 0), shape, _jnp.float32)
    for i, name in enumerate(TWIN_WEIGHTS):
        w = inp[name].astype(_jnp.float32)
        if MOMENT_SCALE is None:
            s = _jnp.sqrt(_jnp.mean(_jnp.square(w)) + 1e-30)
        else:
            s = MOMENT_SCALE[name]
        km, kv = _jax.random.split(_jax.random.fold_in(key, i + 1))
        out[name] = w
        out["m_" + name] = s * _jax.random.normal(km, w.shape, _jnp.float32)
        out["v_" + name] = (s * s) * _jax.random.uniform(kv, w.shape, _jnp.float32, 0.5, 1.5)
    if N_MICROBATCH > 1:
        for name, axis in PER_EXAMPLE_BATCH_AXIS.items():
            out[name] = _to_microbatches(out[name], axis)
    return {'x': out['x'], 'c': out['c'], 'ctx': out['ctx'], 'c_ctx': out['c_ctx'], 'mod_w': out['mod_w'], 'mod_b': out['mod_b'], 'norm1_w': out['norm1_w'], 'norm2_w': out['norm2_w'], 'w_in': out['w_in'], 'conv_w': out['conv_w'], 'conv_b': out['conv_b'], 'dt_bias': out['dt_bias'], 'a_log': out['a_log'], 'ssd_d': out['ssd_d'], 'ssd_norm_w': out['ssd_norm_w'], 'q_a_norm_w': out['q_a_norm_w'], 'w_q_b': out['w_q_b'], 'kv_a_norm_w': out['kv_a_norm_w'], 'w_kv_b': out['w_kv_b'], 'pool_w': out['pool_w'], 'pool_scale': out['pool_scale'], 'w_out': out['w_out'], 'w_mlp1': out['w_mlp1'], 'w_mlp2': out['w_mlp2'], 'final_norm_w': out['final_norm_w'], 'loss_target': out['loss_target'], 'm_c_ctx': out['m_c_ctx'], 'm_mod_w': out['m_mod_w'], 'm_mod_b': out['m_mod_b'], 'm_norm1_w': out['m_norm1_w'], 'm_norm2_w': out['m_norm2_w'], 'm_w_in': out['m_w_in'], 'm_conv_w': out['m_conv_w'], 'm_conv_b': out['m_conv_b'], 'm_dt_bias': out['m_dt_bias'], 'm_a_log': out['m_a_log'], 'm_ssd_d': out['m_ssd_d'], 'm_ssd_norm_w': out['m_ssd_norm_w'], 'm_q_a_norm_w': out['m_q_a_norm_w'], 'm_w_q_b': out['m_w_q_b'], 'm_kv_a_norm_w': out['m_kv_a_norm_w'], 'm_w_kv_b': out['m_w_kv_b'], 'm_pool_w': out['m_pool_w'], 'm_pool_scale': out['m_pool_scale'], 'm_w_out': out['m_w_out'], 'm_w_mlp1': out['m_w_mlp1'], 'm_w_mlp2': out['m_w_mlp2'], 'm_final_norm_w': out['m_final_norm_w'], 'v_c_ctx': out['v_c_ctx'], 'v_mod_w': out['v_mod_w'], 'v_mod_b': out['v_mod_b'], 'v_norm1_w': out['v_norm1_w'], 'v_norm2_w': out['v_norm2_w'], 'v_w_in': out['v_w_in'], 'v_conv_w': out['v_conv_w'], 'v_conv_b': out['v_conv_b'], 'v_dt_bias': out['v_dt_bias'], 'v_a_log': out['v_a_log'], 'v_ssd_d': out['v_ssd_d'], 'v_ssd_norm_w': out['v_ssd_norm_w'], 'v_q_a_norm_w': out['v_q_a_norm_w'], 'v_w_q_b': out['v_w_q_b'], 'v_kv_a_norm_w': out['v_kv_a_norm_w'], 'v_w_kv_b': out['v_w_kv_b'], 'v_pool_w': out['v_pool_w'], 'v_pool_scale': out['v_pool_scale'], 'v_w_out': out['v_w_out'], 'v_w_mlp1': out['v_w_mlp1'], 'v_w_mlp2': out['v_w_mlp2'], 'v_final_norm_w': out['v_final_norm_w']}


def _loss(weights, diff, rest, loss_target):
    with _jax.named_scope("forward"):
        args = {**rest, TWIN_DIFF_INPUT: diff, **{k: w.astype(_WEIGHT_DTYPES[k]) for k, w in weights.items()}}
        y = _forward(args)
    with _jax.named_scope("loss_head"):
        err = _jnp.square(y.astype(_jnp.float32) - loss_target)
        return 0.5 * _jnp.sum(_jnp.mean(err, axis=-1)) if err.ndim else 0.5 * err


def _adamw(w, g, m, v):
    m = ADAM_B1 * m + (1.0 - ADAM_B1) * g
    v = ADAM_B2 * v + (1.0 - ADAM_B2) * _jnp.square(g)
    m_hat = m / (1.0 - ADAM_B1 ** ADAM_STEP)
    v_hat = v / (1.0 - ADAM_B2 ** ADAM_STEP)
    delta = -ADAM_LR * (m_hat / (_jnp.sqrt(v_hat) + ADAM_EPS) + ADAM_WD * w)
    return delta, m, v


def reference(x, c, ctx, c_ctx, mod_w, mod_b, norm1_w, norm2_w, w_in, conv_w, conv_b, dt_bias, a_log, ssd_d, ssd_norm_w, q_a_norm_w, w_q_b, kv_a_norm_w, w_kv_b, pool_w, pool_scale, w_out, w_mlp1, w_mlp2, final_norm_w, loss_target, m_c_ctx, m_mod_w, m_mod_b, m_norm1_w, m_norm2_w, m_w_in, m_conv_w, m_conv_b, m_dt_bias, m_a_log, m_ssd_d, m_ssd_norm_w, m_q_a_norm_w, m_w_q_b, m_kv_a_norm_w, m_w_kv_b, m_pool_w, m_pool_scale, m_w_out, m_w_mlp1, m_w_mlp2, m_final_norm_w, v_c_ctx, v_mod_w, v_mod_b, v_norm1_w, v_norm2_w, v_w_in, v_conv_w, v_conv_b, v_dt_bias, v_a_log, v_ssd_d, v_ssd_norm_w, v_q_a_norm_w, v_w_q_b, v_kv_a_norm_w, v_w_kv_b, v_pool_w, v_pool_scale, v_w_out, v_w_mlp1, v_w_mlp2, v_final_norm_w):
    given = dict(x=x, c=c, ctx=ctx, c_ctx=c_ctx, mod_w=mod_w, mod_b=mod_b, norm1_w=norm1_w, norm2_w=norm2_w, w_in=w_in, conv_w=conv_w, conv_b=conv_b, dt_bias=dt_bias, a_log=a_log, ssd_d=ssd_d, ssd_norm_w=ssd_norm_w, q_a_norm_w=q_a_norm_w, w_q_b=w_q_b, kv_a_norm_w=kv_a_norm_w, w_kv_b=w_kv_b, pool_w=pool_w, pool_scale=pool_scale, w_out=w_out, w_mlp1=w_mlp1, w_mlp2=w_mlp2, final_norm_w=final_norm_w, loss_target=loss_target, m_c_ctx=m_c_ctx, m_mod_w=m_mod_w, m_mod_b=m_mod_b, m_norm1_w=m_norm1_w, m_norm2_w=m_norm2_w, m_w_in=m_w_in, m_conv_w=m_conv_w, m_conv_b=m_conv_b, m_dt_bias=m_dt_bias, m_a_log=m_a_log, m_ssd_d=m_ssd_d, m_ssd_norm_w=m_ssd_norm_w, m_q_a_norm_w=m_q_a_norm_w, m_w_q_b=m_w_q_b, m_kv_a_norm_w=m_kv_a_norm_w, m_w_kv_b=m_w_kv_b, m_pool_w=m_pool_w, m_pool_scale=m_pool_scale, m_w_out=m_w_out, m_w_mlp1=m_w_mlp1, m_w_mlp2=m_w_mlp2, m_final_norm_w=m_final_norm_w, v_c_ctx=v_c_ctx, v_mod_w=v_mod_w, v_mod_b=v_mod_b, v_norm1_w=v_norm1_w, v_norm2_w=v_norm2_w, v_w_in=v_w_in, v_conv_w=v_conv_w, v_conv_b=v_conv_b, v_dt_bias=v_dt_bias, v_a_log=v_a_log, v_ssd_d=v_ssd_d, v_ssd_norm_w=v_ssd_norm_w, v_q_a_norm_w=v_q_a_norm_w, v_w_q_b=v_w_q_b, v_kv_a_norm_w=v_kv_a_norm_w, v_w_kv_b=v_w_kv_b, v_pool_w=v_pool_w, v_pool_scale=v_pool_scale, v_w_out=v_w_out, v_w_mlp1=v_w_mlp1, v_w_mlp2=v_w_mlp2, v_final_norm_w=v_final_norm_w)
    weights = {n: given[n] for n in TWIN_WEIGHTS}
    shared = {n: given[n] for n in SHARED_INPUTS}
    per_example = {n: given[n] for n in ['x', 'c', 'ctx']}
    grad_fn = _jax.value_and_grad(_loss, argnums=(0, 1))

    def one_microbatch(ex, loss_target):
        ex = dict(ex)
        diff = ex.pop(TWIN_DIFF_INPUT)
        return grad_fn(weights, diff, {**shared, **ex}, loss_target)

    if N_MICROBATCH == 1:
        loss, (grad_w, grad_x) = one_microbatch(per_example, given["loss_target"])
    else:
        def body(carry, xs):
            loss_sum, grad_sum = carry
            l_k, (gw_k, gx_k) = one_microbatch(xs[0], xs[1])
            with _jax.named_scope("update"):
                return (loss_sum + l_k, _jax.tree.map(_jnp.add, grad_sum, gw_k)), gx_k

        init = (_jnp.zeros((), _jnp.float32), _jax.tree.map(_jnp.zeros_like, weights))
        (loss, grad_w), grad_x = _jax.lax.scan(body, init, (per_example, given["loss_target"]))
    with _jax.named_scope("update"):
        delta_w, new_m, new_v = {}, {}, {}
        for n in TWIN_WEIGHTS:
            delta_w[n], new_m[n], new_v[n] = _adamw(weights[n], grad_w[n], given["m_" + n], given["v_" + n])
    return (loss, grad_x, *[grad_w[n] for n in TWIN_WEIGHTS], *[delta_w[n] for n in TWIN_WEIGHTS],
            *[new_m[n] for n in TWIN_WEIGHTS], *[new_v[n] for n in TWIN_WEIGHTS])
```

```python
import functools
import math

import jax
import jax.numpy as jnp
from jax import lax
from jax.experimental import pallas as pl
from jax.experimental.pallas import tpu as pltpu

F32 = jnp.float32
BF = jnp.bfloat16
MXU = BF

D = 1024
CTX = 256
GRID_W = 64
EPS = 1e-6
H = 6
P = 64
SSD_IN = 384
NST = 128
XBC = 896
CH = 128
QL = 256
NOPE = 64
ROPE = 32
VH = 64
QK = 96
HP = 128
QW = H * HP
PD = 256
FF = 4096
IN_COLS = 2092
ROPE_THETA = 10000.0
PC = 2304
A_Z, A_XBC, A_QA, A_KVA, A_POOL, A_KR, A_DT = 0, 384, 1280, 1536, 1792, 2048, 2176
MIXW = SSD_IN + QW + PD
NDEV = 8
FSH = FF // NDEV
TM = 256
TQ = 256
TK = 256
VMEM_CAP = 64 * 1024 * 1024
ADAM_LR, ADAM_B1, ADAM_B2, ADAM_EPS, ADAM_WD, ADAM_STEP = 0.001, 0.9, 0.999, 1e-08, 0.01, 10


def _nbytes(shape, dtype):
    n = 1
    for s in shape:
        if s is not None:
            n *= s
    return n * jnp.dtype(dtype).itemsize


def _params(sem, block_bytes, extra=0):
    lim = min(2 * block_bytes + extra + (8 << 20), VMEM_CAP - (6 << 20))
    return pltpu.CompilerParams(dimension_semantics=sem, vmem_limit_bytes=int(lim))


def _dot(a, b):
    return jnp.dot(a.astype(MXU), b.astype(MXU), preferred_element_type=F32)


def _dot_nt(a, b):
    return lax.dot_general(a.astype(MXU), b.astype(MXU), (((1,), (1,)), ((), ())), preferred_element_type=F32)


def _dot_tn(a, b):
    return lax.dot_general(a.astype(MXU), b.astype(MXU), (((0,), (0,)), ((), ())), preferred_element_type=F32)


def _dot01(m01, x):
    b16 = jnp.bfloat16
    m = m01.astype(b16)
    hi = x.astype(b16)
    r1 = x - hi.astype(F32)
    mid = r1.astype(b16)
    lo = (r1 - mid.astype(F32)).astype(b16)
    f = lambda v: jnp.dot(m, v, preferred_element_type=F32)
    return f(hi) + f(mid) + f(lo)


def _sigmoid(x):
    return 1.0 / (1.0 + jnp.exp(-x))


def _silu(x):
    return x * _sigmoid(x)


def _dsilu(x):
    s = _sigmoid(x)
    return s * (1.0 + x * (1.0 - s))


def _iota(shape, dim):
    return lax.broadcasted_iota(jnp.int32, shape, dim)


def _row(ref, k):
    blk = ref[...]
    return jnp.sum(jnp.where(_iota(blk.shape, 0) == k, blk, 0.0), axis=0, keepdims=True)


def _shift_rows(x, k):
    n = x.shape[0]
    return pltpu.roll(x, (-k) % n, axis=0)


class _Rows:
    def __init__(self, B, T):
        self.B, self.T = B, T
        self.nt = T // TM
        self.nct = CTX // TM

    def row(self, F):
        return pl.BlockSpec((None, TM, F), lambda b, i: (b, i, 0))

    def row2(self, F):
        return pl.BlockSpec((None, 2, TM, F), lambda b, i: (b, 0, i, 0))

    def prev8(self, F):
        return pl.BlockSpec((None, 8, F), lambda b, i: (b, jnp.maximum(i * (TM // 8) - 1, 0), 0))

    def next8(self, F):
        last = self.T // 8 - 1
        return pl.BlockSpec((None, 8, F), lambda b, i: (b, jnp.minimum((i + 1) * (TM // 8), last), 0))

    def mod(self):
        nct = self.nct
        return pl.BlockSpec((None, None, 8, D), lambda b, i: (b, jnp.where(i < nct, 0, 1), 0, 0))

    def const(self, shape):
        z = (0,) * len(shape)
        return pl.BlockSpec(tuple(shape), lambda b, i: z)

    def tgt(self, F):
        nct = self.nct
        return pl.BlockSpec((None, TM, F), lambda b, i: (b, jnp.maximum(i - nct, 0), 0))

    def call(self, name, body, ins, outs, scratch=(), extra=0):
        arrays = [a for a, _ in ins]
        in_specs = [s for _, s in ins]
        out_shape = [o for o, _ in outs]
        out_specs = [s for _, s in outs]
        bb = 0
        for a, s in list(ins) + list(outs):
            if s.block_shape is not None:
                bb += _nbytes(s.block_shape, a.dtype)
        return pl.pallas_call(
            functools.partial(body), name=name, grid=(self.B, self.nt),
            in_specs=in_specs, out_specs=out_specs, out_shape=out_shape, scratch_shapes=list(scratch),
            compiler_params=_params(("arbitrary", "arbitrary"), bb, extra),
        )(*arrays)

    def first_of_stream(self, i):
        return jnp.logical_or(i == 0, i == self.nct)

    def last_of_stream(self, i):
        return jnp.logical_or(i == self.nct - 1, i == self.nt - 1)

    def ext(self, i, prev_ref, cur, next_ref):
        pv = prev_ref[...].astype(F32) * jnp.where(self.first_of_stream(i), 0.0, 1.0)
        nx = next_ref[...].astype(F32) * jnp.where(self.last_of_stream(i), 0.0, 1.0)
        return jnp.concatenate([pv, cur, nx], axis=0)

    def stream_pos(self, i, rows):
        start = jnp.where(i < self.nct, 0, CTX)
        n = jnp.where(i < self.nct, CTX, self.T - CTX)
        t = i * TM - 8 - start + _iota((rows, 1), 0)
        return t, n


def _sds(shape, dtype):
    return jax.ShapeDtypeStruct(tuple(shape), dtype)


def _norm_mod(x, nw, sh, sc):
    r = lax.rsqrt(jnp.mean(x * x, axis=-1, keepdims=True) + EPS)
    xn = x * r
    return xn * nw * (1.0 + sc) + sh, xn, r


def _norm_mod_bwd(dh, xn, r, nw, sc):
    dsh = jnp.sum(dh, axis=0, keepdims=True)
    dsc = jnp.sum(dh * (xn * nw), axis=0, keepdims=True)
    dnw = jnp.sum(dh * (1.0 + sc) * xn, axis=0, keepdims=True)
    dxn = dh * nw * (1.0 + sc)
    dx = r * (dxn - xn * jnp.mean(dxn * xn, axis=-1, keepdims=True))
    return dx, dsh, dsc, dnw


def _acc_rows(ref, first, rows):
    rid = _iota(ref.shape, 0)
    upd = jnp.zeros(ref.shape, F32)
    for k, v in rows.items():
        upd = upd + jnp.where(rid == k, v, 0.0)

    @pl.when(first)
    def _():
        ref[...] = upd

    @pl.when(jnp.logical_not(first))
    def _():
        ref[...] += upd


def _in_proj(R, x, mod, nw1, w_arr):
    B, T = R.B, R.T

    def body(x_ref, mod_ref, nw_ref, w_ref, z_ref, xbc_ref, qa_ref, kva_ref, pool_ref, kr_ref, dt_ref, h_ref):
        h, _, _ = _norm_mod(x_ref[...], nw_ref[...], _row(mod_ref, 0), _row(mod_ref, 1))
        hb = h.astype(BF)
        h_ref[...] = hb
        p = jnp.dot(hb, w_ref[...], preferred_element_type=F32)
        z_ref[...] = p[:, A_Z:A_XBC]
        xbc_ref[...] = p[:, A_XBC:A_QA]
        qa_ref[...] = p[:, A_QA:A_KVA]
        kva_ref[...] = p[:, A_KVA:A_POOL]
        pool_ref[...] = p[:, A_POOL:A_KR]
        kr_ref[...] = p[:, A_KR:A_DT]
        dt_ref[...] = p[:, A_DT:PC]

    widths = [SSD_IN, XBC, QL, QL, PD, HP, HP]
    outs = [(_sds((B, T, w), F32), R.row(w)) for w in widths] + [(_sds((B, T, D), BF), R.row(D))]
    return R.call("in_proj", body,
                  [(x, R.row(D)), (mod, R.mod()), (nw1, R.const((1, D))), (w_arr, R.const((D, PC)))],
                  outs, extra=8 << 20)


def _conv_pre(ext, w_ref, b_ref):
    return (_row(w_ref, 0) * _shift_rows(ext, -1)[8:8 + TM] + _row(w_ref, 1) * ext[8:8 + TM]
            + _row(w_ref, 2) * _shift_rows(ext, 1)[8:8 + TM] + _row(w_ref, 3) * _shift_rows(ext, 2)[8:8 + TM]
            + b_ref[...])


def _softplus(x):
    return jnp.maximum(x, 0.0) + jnp.log(1.0 + jnp.exp(-jnp.abs(x)))


def _ssd_prep(R, xbc_raw, dt_raw, conv_w8, conv_b, dtb):
    B, T = R.B, R.T

    def body(raw_ref, pv_ref, nx_ref, dtr_ref, w_ref, b_ref, dtb_ref, xbc_ref, dt_ref):
        i = pl.program_id(1)
        ext = R.ext(i, pv_ref, raw_ref[...], nx_ref)
        xbc_ref[...] = _silu(_conv_pre(ext, w_ref, b_ref))
        lane = _iota((TM, HP), 1)
        dtv = _softplus(dtr_ref[...] + dtb_ref[...])
        keep = lane < H
        dt_ref[0] = jnp.where(keep, dtv, 0.0)
        dt_ref[1] = jnp.where(keep, pltpu.roll(dtv, HP - H, axis=1), 0.0)

    return R.call("ssd_prep", body,
                  [(xbc_raw, R.row(XBC)), (xbc_raw, R.prev8(XBC)), (xbc_raw, R.next8(XBC)), (dt_raw, R.row(HP)),
                   (conv_w8, R.const((8, XBC))), (conv_b, R.const((1, XBC))), (dtb, R.const((1, HP)))],
                  [(_sds((B, T, XBC), F32), R.row(XBC)), (_sds((B, 2, T, HP), F32), R.row2(HP))], extra=12 << 20)


def _chunk_index(d, s, nc, nctc):
    rev = jnp.where(s < nctc, nctc - 1 - s, nc - 1 - (s - nctc))
    return jnp.where(d == 0, s, rev)


def _scan_common(d, xbc, dtv, alog_ref):
    a = -jnp.exp(_row(alog_ref, 0))
    adt = dtv * a
    row = _iota((CH, CH), 0)
    col = _iota((CH, CH), 1)
    sgn = jnp.where(d == 0, 1, -1)
    inc = (row - col) * sgn >= 0
    inc_t = (col - row) * sgn >= 0
    q = _dot01(inc, adt)
    return a, adt, inc, inc_t, q


def _head_cols(h, adt, q, dtv, inc_t):
    onehot = (_iota((1, HP), 1) == h).astype(F32)
    adt_h = jnp.sum(adt * onehot, axis=1, keepdims=True)
    qc = jnp.sum(q * onehot, axis=1, keepdims=True)
    dt_h = jnp.sum(dtv * onehot, axis=1, keepdims=True)
    qr = jnp.sum(adt_h * inc_t.astype(F32), axis=0, keepdims=True)
    qtot = jnp.sum(adt_h, axis=0, keepdims=True)
    return onehot, qc, qr, qtot, dt_h


def _ssd_scan(B, T, xbc, dt2, alog2):
    nc, nctc = T // CH, CTX // CH

    def body(xbc_ref, dt_ref, alog_ref, y_ref, hin_ref, hs):
        d, s = pl.program_id(1), pl.program_id(2)

        @pl.when(s == 0)
        def _():
            hs[...] = jnp.zeros(hs.shape, F32)

        xbc_v = xbc_ref[...]
        dtv = dt_ref[...]
        _, adt, inc, inc_t, q = _scan_common(d, xbc_v, dtv, alog_ref)
        hin_ref[...] = hs[...]
        for g in range(2):
            bg = xbc_v[:, SSD_IN + NST * g:SSD_IN + NST * (g + 1)]
            cg = xbc_v[:, SSD_IN + 2 * NST + NST * g:SSD_IN + 2 * NST + NST * (g + 1)]
            gm = _dot_nt(cg, bg)
            for r in range(3):
                h = 3 * g + r
                _, qc, qr, qtot, dt_h = _head_cols(h, adt, q, dtv, inc_t)
                lm = jnp.where(inc, jnp.exp(qc - qr), 0.0)
                xh = xbc_v[:, P * h:P * (h + 1)] * dt_h
                hprev = hs[P * h:P * (h + 1), :]
                y = _dot(gm * lm, xh) + jnp.exp(qc) * _dot_nt(cg, hprev)
                y_ref[:, P * h:P * (h + 1)] = y
                hs[P * h:P * (h + 1), :] = jnp.exp(qtot) * hprev + _dot_tn(xh * jnp.exp(qtot - qc), bg)

    cidx = lambda d, s: _chunk_index(d, s, nc, nctc)
    bb = _nbytes((CH, XBC), F32) + _nbytes((CH, HP), F32) + _nbytes((CH, SSD_IN), F32) + _nbytes((SSD_IN, NST), F32)
    return pl.pallas_call(
        functools.partial(body), name="ssd_scan", grid=(B, 2, nc),
        in_specs=[pl.BlockSpec((None, CH, XBC), lambda b, d, s: (b, cidx(d, s), 0)),
                  pl.BlockSpec((None, None, CH, HP), lambda b, d, s: (b, d, cidx(d, s), 0)),
                  pl.BlockSpec((None, 8, HP), lambda b, d, s: (d, 0, 0))],
        out_specs=[pl.BlockSpec((None, None, CH, SSD_IN), lambda b, d, s: (b, d, cidx(d, s), 0)),
                   pl.BlockSpec((None, None, None, SSD_IN, NST), lambda b, d, s: (b, d, cidx(d, s), 0, 0))],
        out_shape=[_sds((B, 2, T, SSD_IN), F32), _sds((B, 2, nc, SSD_IN, NST), F32)],
        scratch_shapes=[pltpu.VMEM((SSD_IN, NST), F32)],
        compiler_params=_params(("arbitrary",) * 3, bb, 8 << 20),
    )(xbc, dt2, alog2)


def _swap8(u):
    lane = _iota(u.shape, 1)
    n = u.shape[1]
    return jnp.where((lane & 15) < 8, pltpu.roll(u, n - 8, axis=1), pltpu.roll(u, 8, axis=1))


def _rope(u, cos, sin_signed):
    return u * cos + _swap8(u) * sin_signed


def _rms(x, w):
    r = lax.rsqrt(jnp.mean(x * x, axis=-1, keepdims=True) + EPS)
    xh = x * r
    return xh * w, xh, r


def _rms_bwd(dy, xh, r, w):
    dw = jnp.sum(dy * xh, axis=0, keepdims=True)
    dxh = dy * w
    return r * (dxh - xh * jnp.mean(dxh * xh, axis=-1, keepdims=True)), dw


def _tile6(t):
    return jnp.concatenate([t] * H, axis=1)


def _per_head(fn, u):
    return jnp.concatenate([fn(u[:, HP * h:HP * (h + 1)]) for h in range(H)], axis=1)


def _mla_prep(R, qa, kva, kr, qnw, kvnw, wq, wk, wv, cos, sin):
    B, T = R.B, R.T
    scale = QK ** -0.5

    def body(qa_ref, kva_ref, kr_ref, qnw_ref, kvnw_ref, wq_ref, wk_ref, wv_ref, cos_ref, sin_ref,
             qt_ref, kt_ref, vt_ref, cq_ref, ckv_ref):
        cq, _, _ = _rms(qa_ref[...], qnw_ref[...])
        ckv, _, _ = _rms(kva_ref[...], kvnw_ref[...])
        cqb, ckvb = cq.astype(BF), ckv.astype(BF)
        cq_ref[...] = cqb
        ckv_ref[...] = ckvb
        cos1, sin1 = cos_ref[...], sin_ref[...]
        q = jnp.dot(cqb, wq_ref[...], preferred_element_type=F32)
        qt_ref[...] = (_per_head(lambda u: _rope(u, cos1, sin1), q) * scale).astype(BF)
        kk = _rope(kr_ref[...], cos1, sin1)
        k = jnp.dot(ckvb, wk_ref[...], preferred_element_type=F32)
        kt_ref[...] = (k + _tile6(kk)).astype(BF)
        vt_ref[...] = jnp.dot(ckvb, wv_ref[...], preferred_element_type=F32).astype(BF)

    return R.call("mla_prep", body,
                  [(qa, R.row(QL)), (kva, R.row(QL)), (kr, R.row(HP)), (qnw, R.const((1, QL))), (kvnw, R.const((1, QL))),
                   (wq, R.const((QL, QW))), (wk, R.const((QL, QW))), (wv, R.const((QL, QW))),
                   (cos, pl.BlockSpec((TM, HP), lambda b, i: (i, 0))), (sin, pl.BlockSpec((TM, HP), lambda b, i: (i, 0)))],
                  [(_sds((B, T, QW), BF), R.row(QW))] * 3 + [(_sds((B, T, QL), BF), R.row(QL))] * 2, extra=8 << 20)


def _kv_count(i, T):
    return jnp.where(i < CTX // TQ, CTX // TK, T // TK)


def _flash_fwd(B, T, qt, kt, vt):
    nq = T // TQ

    def body(q_ref, k_ref, v_ref, o_ref, lse_ref):
        i = pl.program_id(2)
        q = q_ref[...]

        def step(j, carry):
            m, l, acc = carry
            k = k_ref[pl.ds(j * TK, TK), :]
            v = v_ref[pl.ds(j * TK, TK), :]
            s = _dot_nt(q, k)
            mn = jnp.maximum(m, jnp.max(s, axis=1, keepdims=True))
            p = jnp.exp(s - mn)
            al = jnp.exp(m - mn)
            return mn, al * l + jnp.sum(p, axis=1, keepdims=True), al * acc + _dot(p, v)

        m0 = jnp.full((TQ, 1), -1e30, F32)
        m, l, acc = lax.fori_loop(0, _kv_count(i, T), step, (m0, jnp.zeros((TQ, 1), F32), jnp.zeros((TQ, HP), F32)))
        o_ref[...] = acc / l
        lse_ref[...] = m + jnp.log(l)

    bb = _nbytes((TQ, HP), BF) + 2 * _nbytes((T, HP), BF) + _nbytes((TQ, HP), F32) + _nbytes((TQ, 1), F32)
    return pl.pallas_call(
        functools.partial(body), name="flash_fwd", grid=(B, H, nq),
        in_specs=[pl.BlockSpec((None, TQ, HP), lambda b, h, i: (b, i, h)),
                  pl.BlockSpec((None, T, HP), lambda b, h, i: (b, 0, h)),
                  pl.BlockSpec((None, T, HP), lambda b, h, i: (b, 0, h))],
        out_specs=[pl.BlockSpec((None, TQ, HP), lambda b, h, i: (b, i, h)),
                   pl.BlockSpec((None, None, TQ, 1), lambda b, h, i: (b, h, i, 0))],
        out_shape=[_sds((B, T, QW), F32), _sds((B, H, T, 1), F32)],
        compiler_params=_params(("arbitrary",) * 3, bb, 8 << 20),
    )(qt, kt, vt)


def _pool_terms(R, i, rows):
    t, n = R.stream_pos(i, rows)
    lane = _iota((1, PD), 1)
    half = jnp.where(lane < 64, 1, jnp.where(lane < 128, 2, jnp.where(lane < 192, 4, 8)))
    cnt = (jnp.minimum(t + half, n) - jnp.maximum(t - half, 0)).astype(F32)
    valid = jnp.logical_and(t >= 0, t < n)
    return jnp.where(valid, cnt, 1.0), valid.astype(F32), lane


def _lane_select(lane, a2, a4, a8, a16):
    return jnp.where(lane < 64, a2, jnp.where(lane < 128, a4, jnp.where(lane < 192, a8, a16)))


def _pool_centred(R, i, ext):
    cnt, valid, lane = _pool_terms(R, i, ext.shape[0])
    s2 = ext + _shift_rows(ext, -1)
    s4 = _shift_rows(s2, -1) + _shift_rows(s2, 1)
    s8 = _shift_rows(s4, -2) + _shift_rows(s4, 2)
    s16 = _shift_rows(s8, -4) + _shift_rows(s8, 4)
    return _lane_select(lane, s2, s4, s8, s16) / cnt - ext, cnt, valid, lane


def _pool_fwd(R, u, wbd, scale):
    B, T = R.B, R.T

    def body(u_ref, pv_ref, nx_ref, w_ref, sc_ref, o_ref):
        i = pl.program_id(1)
        ext = R.ext(i, pv_ref, u_ref[...], nx_ref)
        dm, _, _, _ = _pool_centred(R, i, ext)
        o_ref[...] = _dot(dm[8:8 + TM], w_ref[...]) * sc_ref[...]

    return R.call("pool_fwd", body,
                  [(u, R.row(PD)), (u, R.prev8(PD)), (u, R.next8(PD)), (wbd, R.const((PD, PD))), (scale, R.const((1, PD)))],
                  [(_sds((B, T, PD), F32), R.row(PD))], extra=8 << 20)


def _group_mask():
    return _iota((1, SSD_IN), 1) < SSD_IN // 2


def _ssd_gate(y2_ref, xbc_ref, z_ref, dsk_ref):
    ytot = y2_ref[0] + y2_ref[1] + xbc_ref[:, 0:SSD_IN] * dsk_ref[...]
    z = z_ref[...]
    gz = ytot * _silu(z)
    g0 = _group_mask()
    sq = gz * gz
    s0 = jnp.sum(jnp.where(g0, sq, 0.0), axis=1, keepdims=True)
    s1 = jnp.sum(jnp.where(g0, 0.0, sq), axis=1, keepdims=True)
    half = SSD_IN // 2
    r = jnp.where(g0, lax.rsqrt(s0 / half + EPS), lax.rsqrt(s1 / half + EPS))
    return ytot, z, gz, r


def _out_proj(R, x, mod, y2, xbc, z, o, pool, dsk, snw, wout):
    B, T = R.B, R.T

    def body(x_ref, mod_ref, y2_ref, xbc_ref, z_ref, o_ref, pool_ref, dsk_ref, snw_ref, w_ref, xmid_ref, cat_ref, mix_ref):
        _, _, gz, r = _ssd_gate(y2_ref, xbc_ref, z_ref, dsk_ref)
        cat_ref[:, 0:SSD_IN] = (gz * r * snw_ref[...]).astype(BF)
        cat_ref[:, SSD_IN:SSD_IN + QW] = o_ref[...].astype(BF)
        cat_ref[:, SSD_IN + QW:MIXW] = pool_ref[...].astype(BF)
        mix = jnp.dot(cat_ref[...], w_ref[...], preferred_element_type=F32)
        mix_ref[...] = mix.astype(BF)
        xmid_ref[...] = x_ref[...] + _row(mod_ref, 2) * mix

    return R.call("out_proj", body,
                  [(x, R.row(D)), (mod, R.mod()), (y2, R.row2(SSD_IN)), (xbc, R.row(XBC)), (z, R.row(SSD_IN)), (o, R.row(QW)),
                   (pool, R.row(PD)), (dsk, R.const((1, SSD_IN))), (snw, R.const((1, SSD_IN))), (wout, R.const((MIXW, D)))],
                  [(_sds((B, T, D), F32), R.row(D)), (_sds((B, T, MIXW), BF), R.row(MIXW)), (_sds((B, T, D), BF), R.row(D))],
                  extra=8 << 20)


def _load_once(first, pairs, sem):
    @pl.when(first)
    def _():
        cps = [pltpu.make_async_copy(src, dst, sem.at[k]) for k, (src, dst) in enumerate(pairs)]
        for cp in cps:
            cp.start()
        for cp in cps:
            cp.wait()


ANY = pl.BlockSpec(memory_space=pl.ANY)


def _mlp_fwd(R, xmid, mod, nw2, w1, w2):
    B, T = R.B, R.T

    def body(x_ref, mod_ref, nw_ref, w1_hbm, w2_hbm, xo_ref, h_ref, u_ref, y_ref, w1_v, w2_v, sem):
        first = jnp.logical_and(pl.program_id(0) == 0, pl.program_id(1) == 0)
        _load_once(first, [(w1_hbm, w1_v), (w2_hbm, w2_v)], sem)
        x = x_ref[...]
        h, _, _ = _norm_mod(x, nw_ref[...], _row(mod_ref, 3), _row(mod_ref, 4))
        hb = h.astype(BF)
        h_ref[...] = hb
        y = jnp.zeros((TM, D), F32)
        for j in range(NDEV):
            u = jnp.dot(hb, w1_v[j], preferred_element_type=F32)
            u_ref[:, FSH * j:FSH * (j + 1)] = u.astype(BF)
            a = jnp.square(jnp.maximum(u, 0.0))
            y = y + jnp.dot(a.astype(BF), w2_v[j], preferred_element_type=F32)
        y_ref[...] = y.astype(BF)
        xo_ref[...] = x + _row(mod_ref, 5) * y

    return R.call("mlp_fwd", body,
                  [(xmid, R.row(D)), (mod, R.mod()), (nw2, R.const((1, D))), (w1, ANY), (w2, ANY)],
                  [(_sds((B, T, D), F32), R.row(D)), (_sds((B, T, D), BF), R.row(D)), (_sds((B, T, FF), BF), R.row(FF)),
                   (_sds((B, T, D), BF), R.row(D))],
                  scratch=[pltpu.VMEM((NDEV, D, FSH), w1.dtype), pltpu.VMEM((NDEV, FSH, D), w2.dtype), pltpu.SemaphoreType.DMA((2,))],
                  extra=(2 * _nbytes((NDEV, D, FSH), BF)) + (8 << 20))


def _loss_head(R, x, tgt, fnw):
    B, T = R.B, R.T

    def body(x_ref, t_ref, w_ref, dx_ref, loss_ref, dw_ref):
        b, i = pl.program_id(0), pl.program_id(1)
        live = jnp.where(i >= R.nct, 1.0, 0.0)
        y, xh, r = _rms(x_ref[...], w_ref[...])
        err = (y - t_ref[...]) * live
        dy = err / D
        dxn, dw = _rms_bwd(dy, xh, r, w_ref[...])
        dx_ref[...] = dxn
        first = jnp.logical_and(b == 0, i == 0)
        part = 0.5 * jnp.sum(jnp.sum(err * err, axis=1, keepdims=True), axis=0, keepdims=True) / D
        _acc_rows(loss_ref, first, {0: jnp.broadcast_to(part, (1, HP))})
        _acc_rows(dw_ref, first, {0: dw})

    return R.call("loss_head", body,
                  [(x, R.row(D)), (tgt, R.tgt(D)), (fnw, R.const((1, D)))],
                  [(_sds((B, T, D), F32), R.row(D)), (_sds((8, HP), F32), R.const((8, HP))), (_sds((8, D), F32), R.const((8, D)))],
                  extra=8 << 20)


def _tn_matmul(name, a, b, tn):
    B, T, K = a.shape
    N = b.shape[2]
    nk = 1
    while T % nk or (T // nk) > 1088 or (T // nk) % 16:
        nk += 1
    tk = T // nk
    kt = K if K <= 1536 else 1024
    assert K % kt == 0 and N % tn == 0

    def body(a_ref, b_ref, o_ref):
        first = jnp.logical_and(pl.program_id(2) == 0, pl.program_id(3) == 0)

        @pl.when(first)
        def _():
            o_ref[...] = jnp.zeros(o_ref.shape, F32)

        o_ref[...] += _dot_tn(a_ref[...], b_ref[...])

    bb = _nbytes((tk, kt), a.dtype) + _nbytes((tk, tn), b.dtype) + _nbytes((kt, tn), F32)
    return pl.pallas_call(
        functools.partial(body), name=name, grid=(N // tn, K // kt, B, nk),
        in_specs=[pl.BlockSpec((None, tk, kt), lambda j, kk, bi, t: (bi, t, kk)),
                  pl.BlockSpec((None, tk, tn), lambda j, kk, bi, t: (bi, t, j))],
        out_specs=pl.BlockSpec((None, kt, tn), lambda j, kk, bi, t: (j, kk, 0)),
        out_shape=_sds((N // tn, K, tn), F32),
        compiler_params=_params(("arbitrary",) * 4, bb, 8 << 20),
    )(a, b)


def _mlp_bwd(R, dxo, xmid, ubf, ybf, mod, nw2, w1, w2):
    B, T = R.B, R.T

    def body(dxo_ref, x_ref, u_ref, y_ref, mod_ref, nw_ref, w1_hbm, w2_hbm,
             dxm_ref, du_ref, a_ref, dy_ref, dmod_ref, dnw_ref, w1_v, w2_v, sem):
        b, i = pl.program_id(0), pl.program_id(1)
        _load_once(jnp.logical_and(b == 0, i == 0), [(w1_hbm, w1_v), (w2_hbm, w2_v)], sem)
        dxo = dxo_ref[...]
        _, xn, r = _norm_mod(x_ref[...], nw_ref[...], _row(mod_ref, 3), _row(mod_ref, 4))
        dyb = (dxo * _row(mod_ref, 5)).astype(BF)
        dy_ref[...] = dyb
        dg2 = jnp.sum(dxo * y_ref[...].astype(F32), axis=0, keepdims=True)
        dh = jnp.zeros((TM, D), F32)
        for j in range(NDEV):
            rl = jnp.maximum(u_ref[:, FSH * j:FSH * (j + 1)].astype(F32), 0.0)
            a_ref[:, FSH * j:FSH * (j + 1)] = (rl * rl).astype(BF)
            du = (_dot_nt(dyb, w2_v[j]) * (2.0 * rl)).astype(BF)
            du_ref[:, FSH * j:FSH * (j + 1)] = du
            dh = dh + _dot_nt(du, w1_v[j])
        dx, dsh, dsc, dnw = _norm_mod_bwd(dh, xn, r, nw_ref[...], _row(mod_ref, 4))
        dxm_ref[...] = dxo + dx
        _acc_rows(dmod_ref, R.first_of_stream(i), {3: dsh, 4: dsc, 5: dg2})
        _acc_rows(dnw_ref, jnp.logical_and(b == 0, i == 0), {0: dnw})

    return R.call("mlp_bwd", body,
                  [(dxo, R.row(D)), (xmid, R.row(D)), (ubf, R.row(FF)), (ybf, R.row(D)), (mod, R.mod()), (nw2, R.const((1, D))),
                   (w1, ANY), (w2, ANY)],
                  [(_sds((B, T, D), F32), R.row(D)), (_sds((B, T, FF), BF), R.row(FF)), (_sds((B, T, FF), BF), R.row(FF)),
                   (_sds((B, T, D), BF), R.row(D)), (_sds((B, 2, 8, D), F32), R.mod()), (_sds((8, D), F32), R.const((8, D)))],
                  scratch=[pltpu.VMEM((NDEV, D, FSH), w1.dtype), pltpu.VMEM((NDEV, FSH, D), w2.dtype), pltpu.SemaphoreType.DMA((2,))],
                  extra=(2 * _nbytes((NDEV, D, FSH), BF)) + (8 << 20))


def _out_bwd(R, dxm, mod, mixbf, y2, xbc, z, dsk, snw, wout):
    B, T = R.B, R.T

    def body(dxm_ref, mod_ref, mix_ref, y2_ref, xbc_ref, z_ref, dsk_ref, snw_ref, w_ref,
             dmix_ref, dz_ref, dyt_ref, dxsk_ref, do_ref, dpo_ref, dmod_ref, dvec_ref):
        b, i = pl.program_id(0), pl.program_id(1)
        dxm = dxm_ref[...]
        dmixb = (dxm * _row(mod_ref, 2)).astype(BF)
        dmix_ref[...] = dmixb
        dg1 = jnp.sum(dxm * mix_ref[...].astype(F32), axis=0, keepdims=True)
        dcat = _dot_nt(dmixb, w_ref[...])
        do_ref[...] = dcat[:, SSD_IN:SSD_IN + QW]
        dpo_ref[...] = dcat[:, SSD_IN + QW:MIXW]
        dsn = dcat[:, 0:SSD_IN]
        ytot, zv, gz, r = _ssd_gate(y2_ref, xbc_ref, z_ref, dsk_ref)
        gh = gz * r
        dsnw = jnp.sum(dsn * gh, axis=0, keepdims=True)
        dgh = dsn * snw_ref[...]
        g0 = _group_mask()
        pr = dgh * gh
        half = SSD_IN // 2
        m0 = jnp.sum(jnp.where(g0, pr, 0.0), axis=1, keepdims=True) / half
        m1 = jnp.sum(jnp.where(g0, 0.0, pr), axis=1, keepdims=True) / half
        dgz = r * (dgh - gh * jnp.where(g0, m0, m1))
        dyt = dgz * _silu(zv)
        dz_ref[...] = (dgz * ytot * _dsilu(zv)).astype(BF)
        dyt_ref[...] = dyt
        dxsk_ref[...] = dyt * dsk_ref[...]
        ddsk = jnp.sum(dyt * xbc_ref[:, 0:SSD_IN], axis=0, keepdims=True)
        _acc_rows(dmod_ref, R.first_of_stream(i), {2: dg1})
        _acc_rows(dvec_ref, jnp.logical_and(b == 0, i == 0), {0: dsnw, 1: ddsk})

    return R.call("out_bwd", body,
                  [(dxm, R.row(D)), (mod, R.mod()), (mixbf, R.row(D)), (y2, R.row2(SSD_IN)), (xbc, R.row(XBC)), (z, R.row(SSD_IN)),
                   (dsk, R.const((1, SSD_IN))), (snw, R.const((1, SSD_IN))), (wout, R.const((MIXW, D)))],
                  [(_sds((B, T, D), BF), R.row(D)), (_sds((B, T, SSD_IN), BF), R.row(SSD_IN)), (_sds((B, T, SSD_IN), F32), R.row(SSD_IN)),
                   (_sds((B, T, SSD_IN), F32), R.row(SSD_IN)), (_sds((B, T, QW), F32), R.row(QW)), (_sds((B, T, PD), F32), R.row(PD)),
                   (_sds((B, 2, 8, D), F32), R.mod()), (_sds((8, SSD_IN), F32), R.const((8, SSD_IN)))],
                  extra=8 << 20)


def _pool_bwd(R, dpo, u, wbd, scale):
    B, T = R.B, R.T

    def body(d_ref, dpv_ref, dnx_ref, u_ref, upv_ref, unx_ref, w_ref, sc_ref, du_ref, dw_ref, dsc_ref):
        b, i = pl.program_id(0), pl.program_id(1)
        ext_u = R.ext(i, upv_ref, u_ref[...], unx_ref)
        ext_d = R.ext(i, dpv_ref, d_ref[...], dnx_ref)
        dm, cnt, valid, lane = _pool_centred(R, i, ext_u)
        ddm = _dot_nt(ext_d * sc_ref[...], w_ref[...]) * valid
        e = ddm / cnt
        a2 = e + _shift_rows(e, 1)
        a4 = _shift_rows(a2, -1) + _shift_rows(a2, 1)
        a8 = _shift_rows(a4, -2) + _shift_rows(a4, 2)
        a16 = _shift_rows(a8, -4) + _shift_rows(a8, 4)
        du_ref[...] = (_lane_select(lane, a2, a4, a8, a16) - ddm)[8:8 + TM].astype(BF)
        dmc = dm[8:8 + TM]
        dpo_c = d_ref[...]
        first = jnp.logical_and(b == 0, i == 0)

        @pl.when(first)
        def _():
            dw_ref[...] = jnp.zeros(dw_ref.shape, F32)

        dw_ref[...] += _dot_tn(dmc, dpo_c * sc_ref[...])
        _acc_rows(dsc_ref, first, {0: jnp.sum(dpo_c * _dot(dmc, w_ref[...]), axis=0, keepdims=True)})

    return R.call("pool_bwd", body,
                  [(dpo, R.row(PD)), (dpo, R.prev8(PD)), (dpo, R.next8(PD)), (u, R.row(PD)), (u, R.prev8(PD)), (u, R.next8(PD)),
                   (wbd, R.const((PD, PD))), (scale, R.const((1, PD)))],
                  [(_sds((B, T, PD), BF), R.row(PD)), (_sds((PD, PD), F32), R.const((PD, PD))), (_sds((8, PD), F32), R.const((8, PD)))],
                  extra=8 << 20)


def _flash_bwd_dq(B, T, qt, kt, vt, do, o, lse):
    nq = T // TQ

    def body(q_ref, k_ref, v_ref, do_ref, o_ref, lse_ref, dq_ref, dd_ref):
        i = pl.program_id(2)
        q = q_ref[...]
        do_v = do_ref[...]
        dob = do_v.astype(BF)
        dd = jnp.sum(do_v * o_ref[...], axis=1, keepdims=True)
        dd_ref[...] = dd
        lse_v = lse_ref[...]

        def step(j, dq):
            k = k_ref[pl.ds(j * TK, TK), :]
            v = v_ref[pl.ds(j * TK, TK), :]
            p = jnp.exp(_dot_nt(q, k) - lse_v)
            ds = p * (_dot_nt(dob, v) - dd)
            return dq + _dot(ds, k)

        dq_ref[...] = lax.fori_loop(0, _kv_count(i, T), step, jnp.zeros((TQ, HP), F32))

    qspec = pl.BlockSpec((None, TQ, HP), lambda b, h, i: (b, i, h))
    kspec = pl.BlockSpec((None, T, HP), lambda b, h, i: (b, 0, h))
    cspec = pl.BlockSpec((None, None, TQ, 1), lambda b, h, i: (b, h, i, 0))
    bb = 2 * _nbytes((T, HP), BF) + 4 * _nbytes((TQ, HP), F32)
    return pl.pallas_call(
        functools.partial(body), name="flash_bwd_dq", grid=(B, H, nq),
        in_specs=[qspec, kspec, kspec, qspec, qspec, cspec],
        out_specs=[qspec, cspec],
        out_shape=[_sds((B, T, QW), F32), _sds((B, H, T, 1), F32)],
        compiler_params=_params(("arbitrary",) * 3, bb, 8 << 20),
    )(qt, kt, vt, do, o, lse)


def _flash_bwd_dkv(B, T, qt, kt, vt, do, lse_r, dd_r):
    nk, nq = T // TK, T // TQ

    def body(k_ref, v_ref, q_ref, do_ref, lse_ref, dd_ref, dk_ref, dv_ref):
        j = pl.program_id(2)
        k = k_ref[...]
        v = v_ref[...]
        lo = jnp.where(j < CTX // TK, 0, CTX // TQ)

        def step(i, carry):
            dk, dv = carry
            q = q_ref[pl.ds(i * TQ, TQ), :]
            dob = do_ref[pl.ds(i * TQ, TQ), :].astype(BF)
            pt = jnp.exp(_dot_nt(k, q) - lse_ref[i])
            dst = pt * (_dot_nt(v, dob) - dd_ref[i])
            return dk + _dot(dst, q), dv + _dot(pt, dob)

        dk, dv = lax.fori_loop(lo, nq, step, (jnp.zeros((TK, HP), F32), jnp.zeros((TK, HP), F32)))
        dk_ref[...] = dk
        dv_ref[...] = dv

    tspec = pl.BlockSpec((None, TK, HP), lambda b, h, j: (b, j, h))
    fspec = pl.BlockSpec((None, T, HP), lambda b, h, j: (b, 0, h))
    rspec = pl.BlockSpec((None, None, nq, 1, TQ), lambda b, h, j: (b, h, 0, 0, 0))
    bb = _nbytes((T, HP), BF) + _nbytes((T, HP), F32) + 4 * _nbytes((TK, HP), F32) + 2 * _nbytes((nq, 8, TQ), F32)
    return pl.pallas_call(
        functools.partial(body), name="flash_bwd_dkv", grid=(B, H, nk),
        in_specs=[tspec, tspec, fspec, fspec, rspec, rspec],
        out_specs=[tspec, tspec],
        out_shape=[_sds((B, T, QW), F32), _sds((B, T, QW), F32)],
        compiler_params=_params(("arbitrary",) * 3, bb, 8 << 20),
    )(kt, vt, qt, do, lse_r, dd_r)


def _mla_bwd(R, dqt, dkt, dvt, qa, kva, qnw, kvnw, wq, wk, wv, cos, sin):
    B, T = R.B, R.T
    scale = QK ** -0.5

    def body(dq_ref, dk_ref, dv_ref, qa_ref, kva_ref, qnw_ref, kvnw_ref, wq_ref, wk_ref, wv_ref, cos_ref, sin_ref,
             dqp_ref, dkb_ref, dvb_ref, dqa_ref, dkva_ref, dkr_ref, dnw_ref):
        b, i = pl.program_id(0), pl.program_id(1)
        cos1, sin1 = cos_ref[...], sin_ref[...]
        dq = dq_ref[...] * scale
        dqp = _per_head(lambda g: g * cos1 + _swap8(g * sin1), dq).astype(BF)
        dqp_ref[...] = dqp
        dkv = dk_ref[...]
        dkb = dkv.astype(BF)
        dkb_ref[...] = dkb
        dvb = dv_ref[...].astype(BF)
        dvb_ref[...] = dvb
        dkk = dkv[:, 0:HP]
        for h in range(1, H):
            dkk = dkk + dkv[:, HP * h:HP * (h + 1)]
        lane = _iota((TM, HP), 1)
        rope_lane = jnp.logical_and(lane >= NOPE, lane < NOPE + ROPE)
        dkr_ref[...] = jnp.where(rope_lane, dkk * cos1 + _swap8(dkk * sin1), 0.0).astype(BF)
        _, qh, qr = _rms(qa_ref[...], qnw_ref[...])
        _, kh, kr_ = _rms(kva_ref[...], kvnw_ref[...])
        dcq = _dot_nt(dqp, wq_ref[...])
        dckv = _dot_nt(dkb, wk_ref[...]) + _dot_nt(dvb, wv_ref[...])
        dqa, dqw = _rms_bwd(dcq, qh, qr, qnw_ref[...])
        dkva, dkw = _rms_bwd(dckv, kh, kr_, kvnw_ref[...])
        dqa_ref[...] = dqa.astype(BF)
        dkva_ref[...] = dkva.astype(BF)
        _acc_rows(dnw_ref, jnp.logical_and(b == 0, i == 0), {0: dqw, 1: dkw})

    tab = pl.BlockSpec((TM, HP), lambda b, i: (i, 0))
    return R.call("mla_bwd", body,
                  [(dqt, R.row(QW)), (dkt, R.row(QW)), (dvt, R.row(QW)), (qa, R.row(QL)), (kva, R.row(QL)),
                   (qnw, R.const((1, QL))), (kvnw, R.const((1, QL))), (wq, R.const((QL, QW))), (wk, R.const((QL, QW))),
                   (wv, R.const((QL, QW))), (cos, tab), (sin, tab)],
                  [(_sds((B, T, QW), BF), R.row(QW))] * 3 + [(_sds((B, T, QL), BF), R.row(QL))] * 2
                  + [(_sds((B, T, HP), BF), R.row(HP)), (_sds((8, QL), F32), R.const((8, QL)))], extra=8 << 20)


def _ssd_scan_bwd(B, T, dyt, xbc, dt2, alog2, hin):
    nc, nctc = T // CH, CTX // CH

    def body(dy_ref, xbc_ref, dt_ref, alog_ref, hin_ref, dxbc_ref, ddt_ref, da_ref, dhs):
        d, s = pl.program_id(1), pl.program_id(2)

        @pl.when(s == 0)
        def _():
            dhs[...] = jnp.zeros(dhs.shape, F32)
            da_ref[...] = jnp.zeros(da_ref.shape, F32)

        xbc_v = xbc_ref[...]
        dtv = dt_ref[...]
        dyv = dy_ref[...]
        a, adt, inc, inc_t, q = _scan_common(d, xbc_v, dtv, alog_ref)
        dq_all = jnp.zeros((CH, HP), F32)
        dqtot_all = jnp.zeros((1, HP), F32)
        ddt_x = jnp.zeros((CH, HP), F32)
        for g in range(2):
            bg = xbc_v[:, SSD_IN + NST * g:SSD_IN + NST * (g + 1)]
            cg = xbc_v[:, SSD_IN + 2 * NST + NST * g:SSD_IN + 2 * NST + NST * (g + 1)]
            gm = _dot_nt(cg, bg)
            gm_t = _dot_nt(bg, cg)
            dgm = jnp.zeros((CH, CH), F32)
            dbg = jnp.zeros((CH, NST), F32)
            dcg = jnp.zeros((CH, NST), F32)
            for r in range(3):
                h = 3 * g + r
                onehot, qc, qr, qtot, dt_h = _head_cols(h, adt, q, dtv, inc_t)
                lm = jnp.where(inc, jnp.exp(qc - qr), 0.0)
                xs_h = xbc_v[:, P * h:P * (h + 1)]
                xh = xs_h * dt_h
                sm = gm * lm
                dy_h = dyv[:, P * h:P * (h + 1)]
                hprev = hin_ref[P * h:P * (h + 1), :]
                dho = dhs[P * h:P * (h + 1), :]
                eq = jnp.exp(qc)
                etot = jnp.exp(qtot)
                dte = jnp.exp(qtot - qc)
                ds = _dot_nt(dy_h, xh)
                dx = _dot_tn(sm, dy_h)
                edy = eq * dy_h
                dq_col = jnp.sum(dy_h * (eq * _dot_nt(cg, hprev)), axis=1, keepdims=True)
                dcg = dcg + _dot(edy, hprev)
                dhin = _dot_tn(edy, cg) + etot * dho
                zs = _dot_nt(bg, dho)
                dx = dx + dte * zs
                w = dte * jnp.sum(xh * zs, axis=1, keepdims=True)
                dbg = dbg + _dot(xh * dte, dho)
                dqtot = jnp.sum(jnp.sum(hprev * dho, axis=1, keepdims=True), axis=0, keepdims=True) * etot \
                    + jnp.sum(w, axis=0, keepdims=True)
                dq_col = dq_col - w
                dgm = dgm + ds * lm
                lm_t = jnp.where(inc_t, jnp.exp(qr - qc), 0.0)
                e_t = _dot_nt(xh, dy_h) * gm_t * lm_t
                dq_col = dq_col + jnp.sum(ds * sm, axis=1, keepdims=True) - jnp.sum(e_t, axis=1, keepdims=True)
                dq_all = dq_all + dq_col * onehot
                dqtot_all = dqtot_all + dqtot * onehot
                dxbc_ref[:, P * h:P * (h + 1)] = dx * dt_h
                ddt_x = ddt_x + jnp.sum(dx * xs_h, axis=1, keepdims=True) * onehot
                dhs[P * h:P * (h + 1), :] = dhin
            dcg = dcg + _dot(dgm, bg)
            dbg = dbg + _dot_tn(dgm, cg)
            dxbc_ref[:, SSD_IN + NST * g:SSD_IN + NST * (g + 1)] = dbg
            dxbc_ref[:, SSD_IN + 2 * NST + NST * g:SSD_IN + 2 * NST + NST * (g + 1)] = dcg
        dadt = _dot01(inc_t, dq_all) + dqtot_all
        ddt_ref[...] = ddt_x + dadt * a
        da_ref[...] += jnp.where(_iota((8, HP), 0) == 0, jnp.sum(dadt * dtv, axis=0, keepdims=True), 0.0)

    cidx = lambda d, s: _chunk_index(d, nc - 1 - s, nc, nctc)
    bb = 2 * _nbytes((CH, XBC), F32) + 2 * _nbytes((CH, HP), F32) + _nbytes((CH, SSD_IN), F32) + _nbytes((SSD_IN, NST), F32)
    return pl.pallas_call(
        functools.partial(body), name="ssd_scan_bwd", grid=(B, 2, nc),
        in_specs=[pl.BlockSpec((None, CH, SSD_IN), lambda b, d, s: (b, cidx(d, s), 0)),
                  pl.BlockSpec((None, CH, XBC), lambda b, d, s: (b, cidx(d, s), 0)),
                  pl.BlockSpec((None, None, CH, HP), lambda b, d, s: (b, d, cidx(d, s), 0)),
                  pl.BlockSpec((None, 8, HP), lambda b, d, s: (d, 0, 0)),
                  pl.BlockSpec((None, None, None, SSD_IN, NST), lambda b, d, s: (b, d, cidx(d, s), 0, 0))],
        out_specs=[pl.BlockSpec((None, None, CH, XBC), lambda b, d, s: (b, d, cidx(d, s), 0)),
                   pl.BlockSpec((None, None, CH, HP), lambda b, d, s: (b, d, cidx(d, s), 0)),
                   pl.BlockSpec((None, None, 8, HP), lambda b, d, s: (b, d, 0, 0))],
        out_shape=[_sds((B, 2, T, XBC), F32), _sds((B, 2, T, HP), F32), _sds((B, 2, 8, HP), F32)],
        scratch_shapes=[pltpu.VMEM((SSD_IN, NST), F32)],
        compiler_params=_params(("arbitrary",) * 3, bb, 8 << 20),
    )(dyt, xbc, dt2, alog2, hin)


def _ssd_prep_bwd(R, dxbc2, dxsk, ddt2, xbc_raw, dt_raw, conv_w8, conv_b, dtb):
    B, T = R.B, R.T

    def body(dx2_ref, dsk_ref, ddt_ref, raw_ref, pv_ref, nx_ref, dtr_ref, w_ref, b_ref, dtb_ref,
             dpre_ref, ddtr_ref, dcw_ref, dvec_ref, ddtb_ref):
        b, i = pl.program_id(0), pl.program_id(1)
        ext = R.ext(i, pv_ref, raw_ref[...], nx_ref)
        pre = _conv_pre(ext, w_ref, b_ref)
        dxbc = dx2_ref[0] + dx2_ref[1]
        lane = _iota((TM, XBC), 1)
        skip = jnp.concatenate([dsk_ref[...], jnp.zeros((TM, XBC - SSD_IN), F32)], axis=1)
        dpre = (dxbc + skip) * _dsilu(pre)
        dpre_ref[...] = dpre
        first = jnp.logical_and(b == 0, i == 0)
        taps = {k: jnp.sum(dpre * _shift_rows(ext, k - 1)[8:8 + TM], axis=0, keepdims=True) for k in range(4)}
        _acc_rows(dcw_ref, first, taps)
        _acc_rows(dvec_ref, first, {0: jnp.sum(dpre, axis=0, keepdims=True)})
        ddt = ddt_ref[0] + pltpu.roll(ddt_ref[1], H, axis=1)
        ddtr = ddt * _sigmoid(dtr_ref[...] + dtb_ref[...])
        ddtr = jnp.where(_iota((TM, HP), 1) < 2 * H, ddtr, 0.0)
        ddtr_ref[...] = ddtr.astype(BF)
        _acc_rows(ddtb_ref, first, {0: jnp.sum(ddtr, axis=0, keepdims=True)})
        del lane

    return R.call("ssd_prep_bwd", body,
                  [(dxbc2, R.row2(XBC)), (dxsk, R.row(SSD_IN)), (ddt2, R.row2(HP)), (xbc_raw, R.row(XBC)), (xbc_raw, R.prev8(XBC)),
                   (xbc_raw, R.next8(XBC)), (dt_raw, R.row(HP)), (conv_w8, R.const((8, XBC))), (conv_b, R.const((1, XBC))),
                   (dtb, R.const((1, HP)))],
                  [(_sds((B, T, XBC), F32), R.row(XBC)), (_sds((B, T, HP), BF), R.row(HP)), (_sds((8, XBC), F32), R.const((8, XBC))),
                   (_sds((8, XBC), F32), R.const((8, XBC))), (_sds((8, HP), F32), R.const((8, HP)))], extra=16 << 20)


def _in_bwd(R, dxm, x, mod, nw1, dz, dpre, dqa, dkva, dpool, dkr, ddtr, conv_w8, w_arr):
    B, T = R.B, R.T

    def body(dxm_ref, x_ref, mod_ref, nw_ref, dz_ref, dp_ref, dpp_ref, dpn_ref, dqa_ref, dkva_ref, dpool_ref, dkr_ref, ddt_ref,
             cw_ref, w_ref, dx_ref, dproj_ref, dmod_ref, dnw_ref):
        b, i = pl.program_id(0), pl.program_id(1)
        ext = R.ext(i, dpp_ref, dp_ref[...], dpn_ref)
        draw = (_row(cw_ref, 0) * _shift_rows(ext, 1)[8:8 + TM] + _row(cw_ref, 1) * ext[8:8 + TM]
                + _row(cw_ref, 2) * _shift_rows(ext, -1)[8:8 + TM] + _row(cw_ref, 3) * _shift_rows(ext, -2)[8:8 + TM])
        dproj_ref[:, A_Z:A_XBC] = dz_ref[...]
        dproj_ref[:, A_XBC:A_QA] = draw.astype(BF)
        dproj_ref[:, A_QA:A_KVA] = dqa_ref[...]
        dproj_ref[:, A_KVA:A_POOL] = dkva_ref[...]
        dproj_ref[:, A_POOL:A_KR] = dpool_ref[...]
        dproj_ref[:, A_KR:A_DT] = dkr_ref[...]
        dproj_ref[:, A_DT:PC] = ddt_ref[...]
        dh = _dot_nt(dproj_ref[...], w_ref[...])
        _, xn, r = _norm_mod(x_ref[...], nw_ref[...], _row(mod_ref, 0), _row(mod_ref, 1))
        dx, dsh, dsc, dnw = _norm_mod_bwd(dh, xn, r, nw_ref[...], _row(mod_ref, 1))
        dx_ref[...] = dxm_ref[...] + dx
        _acc_rows(dmod_ref, R.first_of_stream(i), {0: dsh, 1: dsc})
        _acc_rows(dnw_ref, jnp.logical_and(b == 0, i == 0), {0: dnw})

    return R.call("in_bwd", body,
                  [(dxm, R.row(D)), (x, R.row(D)), (mod, R.mod()), (nw1, R.const((1, D))), (dz, R.row(SSD_IN)), (dpre, R.row(XBC)),
                   (dpre, R.prev8(XBC)), (dpre, R.next8(XBC)), (dqa, R.row(QL)), (dkva, R.row(QL)), (dpool, R.row(PD)), (dkr, R.row(HP)),
                   (ddtr, R.row(HP)), (conv_w8, R.const((8, XBC))), (w_arr, R.const((D, PC)))],
                  [(_sds((B, T, D), F32), R.row(D)), (_sds((B, T, PC), BF), R.row(PC)), (_sds((B, 2, 8, D), F32), R.mod()),
                   (_sds((8, D), F32), R.const((8, D)))], extra=12 << 20)


def _adaln_fwd(cs, mod_w):
    L, _, C = mod_w.shape

    def body(c_ref, w_ref, o_ref):
        s = _silu(c_ref[...]).astype(BF)
        for l in range(L):
            o_ref[l] = jnp.dot(s, w_ref[l].astype(BF), preferred_element_type=F32)

    return pl.pallas_call(functools.partial(body), name="adaln_fwd", out_shape=_sds((L, 24, C), F32),
                          compiler_params=_params(None, _nbytes(mod_w.shape, F32) + _nbytes((L, 24, C), F32), 8 << 20))(cs, mod_w)


def _adaln_bwd(cs, dm, mod_w):
    L, _, C = mod_w.shape

    def body(c_ref, dm_ref, w_ref, gw_ref, gc_ref):
        c = c_ref[...]
        s = _silu(c).astype(BF)
        acc = jnp.zeros((24, D), F32)
        for l in range(L):
            dmb = dm_ref[l].astype(BF)
            gw_ref[l] = _dot_tn(s, dmb)
            acc = acc + _dot_nt(dmb, w_ref[l])
        gc_ref[...] = acc * _dsilu(c)

    return pl.pallas_call(functools.partial(body), name="adaln_bwd", out_shape=[_sds((L, D, C), F32), _sds((24, D), F32)],
                          compiler_params=_params(None, 2 * _nbytes(mod_w.shape, F32), 8 << 20))(cs, dm, mod_w)


def _adamw(name, parts, w, m, v, rt):
    Pn, Rr, C = parts.shape
    c1 = 1.0 - ADAM_B1 ** ADAM_STEP
    c2 = 1.0 - ADAM_B2 ** ADAM_STEP

    def body(p_ref, w_ref, m_ref, v_ref, g_ref, d_ref, nm_ref, nv_ref):
        g = p_ref[0].astype(F32)
        for k in range(1, Pn):
            g = g + p_ref[k].astype(F32)
        mn = ADAM_B1 * m_ref[...] + (1.0 - ADAM_B1) * g
        vn = ADAM_B2 * v_ref[...] + (1.0 - ADAM_B2) * jnp.square(g)
        g_ref[...] = g
        nm_ref[...] = mn
        nv_ref[...] = vn
        d_ref[...] = -ADAM_LR * ((mn / c1) / (jnp.sqrt(vn / c2) + ADAM_EPS) + ADAM_WD * w_ref[...])

    spec = pl.BlockSpec((rt, C), lambda i: (i, 0))
    bb = Pn * _nbytes((rt, C), parts.dtype) + 7 * _nbytes((rt, C), F32)
    return pl.pallas_call(
        functools.partial(body), name=name, grid=(Rr // rt,),
        in_specs=[pl.BlockSpec((Pn, rt, C), lambda i: (0, i, 0)), spec, spec, spec],
        out_specs=[spec] * 4, out_shape=[_sds((Rr, C), F32)] * 4,
        compiler_params=_params(("arbitrary",), bb, 4 << 20),
    )(parts, w, m, v)


MESH = pl.DeviceIdType.MESH


def _my_pos():
    return lax.axis_index("x"), lax.axis_index("y"), lax.axis_index("c")


def _dev_index(x, y, c):
    return 4 * x + 2 * y + c


def _all_gather(name, shards):
    n = len(shards)

    def body(*refs):
        ins, outs = refs[:n], refs[n:2 * n]
        send_sems, recv_sems, local_sem = refs[2 * n:]
        x, y, c = _my_pos()
        me, sibling = (x, y, c), (x, y, 1 - c)
        chips = [(1 - x, y), (x, 1 - y), (1 - x, 1 - y)]

        def copy(t, k, block, to, src=None):
            slot = outs[t].at[_dev_index(*block)]
            return pltpu.make_async_remote_copy(
                src_ref=slot if src is None else src, dst_ref=slot,
                send_sem=send_sems.at[t, k], recv_sem=recv_sems.at[t, k], device_id=to, device_id_type=MESH)

        mine = [pltpu.make_async_copy(ins[t], outs[t].at[_dev_index(*me)], local_sem.at[t]) for t in range(n)]
        for cp in mine:
            cp.start()
        first = []
        for t in range(n):
            first.append(copy(t, 0, me, sibling, src=ins[t]))
            first += [copy(t, 1 + j, me, (*chip, c), src=ins[t]) for j, chip in enumerate(chips)]
        for cp in first:
            cp.start()
        passed = []
        for j, chip in enumerate(chips):
            for t in range(n):
                copy(t, 1 + j, (*chip, c), me).wait_recv()
                cp = copy(t, 4 + j, (*chip, c), sibling)
                cp.start()
                passed.append(cp)
        for t in range(n):
            copy(t, 0, sibling, me).wait_recv()
            for j, chip in enumerate(chips):
                copy(t, 4 + j, (*chip, 1 - c), me).wait_recv()
        for cp in first + passed:
            cp.wait_send()
        for cp in mine:
            cp.wait()

    return pl.pallas_call(
        functools.partial(body), name=name,
        in_specs=[ANY] * n, out_specs=[ANY] * n,
        out_shape=[_sds((NDEV,) + s.shape, s.dtype) for s in shards],
        scratch_shapes=[pltpu.SemaphoreType.DMA((n, 7)), pltpu.SemaphoreType.DMA((n, 7)), pltpu.SemaphoreType.DMA((n,))],
    )(*shards)


def _all_to_all(name, parts):
    n = len(parts)

    def body(*refs):
        ins, outs = refs[:n], refs[n:2 * n]
        send_sems, recv_sems, local_sem = refs[2 * n:]
        x, y, c = _my_pos()
        me = _dev_index(x, y, c)
        peers = [(x ^ ((k >> 2) & 1), y ^ ((k >> 1) & 1), c ^ (k & 1)) for k in range(1, NDEV)]
        mine = [pltpu.make_async_copy(ins[t].at[me], outs[t].at[me], local_sem.at[t]) for t in range(n)]
        for cp in mine:
            cp.start()
        sends = []
        for t in range(n):
            for k, peer in enumerate(peers):
                cp = pltpu.make_async_remote_copy(
                    src_ref=ins[t].at[_dev_index(*peer)], dst_ref=outs[t].at[me],
                    send_sem=send_sems.at[t, k], recv_sem=recv_sems.at[t, k], device_id=peer, device_id_type=MESH)
                cp.start()
                sends.append(cp)
        for t in range(n):
            for k, peer in enumerate(peers):
                slot = outs[t].at[_dev_index(*peer)]
                pltpu.make_async_remote_copy(
                    src_ref=slot, dst_ref=slot, send_sem=send_sems.at[t, k], recv_sem=recv_sems.at[t, k],
                    device_id=peer, device_id_type=MESH).wait_recv()
        for cp in sends:
            cp.wait_send()
        for cp in mine:
            cp.wait()

    return pl.pallas_call(
        functools.partial(body), name=name,
        in_specs=[ANY] * n, out_specs=[ANY] * n,
        out_shape=[_sds(p.shape, p.dtype) for p in parts],
        scratch_shapes=[pltpu.SemaphoreType.DMA((n, 7)), pltpu.SemaphoreType.DMA((n, 7)), pltpu.SemaphoreType.DMA((n,))],
    )(*parts)


def _arrange_w_in(w):
    z = lambda n: jnp.zeros((w.shape[0], n), w.dtype)
    return jnp.concatenate([w[:, 0:1280], w[:, 1292:1548], w[:, 1548:1804], w[:, 1836:2092],
                            z(64), w[:, 1804:1836], z(32), w[:, 1280:1292], z(HP - 2 * H)], axis=1)


def _unarrange_w_in(g):
    return jnp.concatenate([g[:, 0:1280], g[:, A_DT:A_DT + 2 * H], g[:, A_QA:A_KVA], g[:, A_KVA:A_POOL],
                            g[:, A_KR + NOPE:A_KR + NOPE + ROPE], g[:, A_POOL:A_KR]], axis=1)


def _pad_heads(w, width):
    k = w.shape[0]
    return jnp.pad(w.reshape(k, H, width), ((0, 0), (0, 0), (0, HP - width))).reshape(k, H * HP)


def _unpad_heads(g, width):
    k = g.shape[0]
    return g.reshape(k, H, HP)[:, :, :width].reshape(k, H * width)


def _arrange_w_out(w):
    att = jnp.pad(w[SSD_IN:2 * SSD_IN].reshape(H, VH, D), ((0, 0), (0, HP - VH), (0, 0))).reshape(QW, D)
    return jnp.concatenate([w[0:SSD_IN], att, w[2 * SSD_IN:]], axis=0)


def _unarrange_w_out(g):
    att = g[SSD_IN:SSD_IN + QW].reshape(H, HP, D)[:, :VH].reshape(SSD_IN, D)
    return jnp.concatenate([g[0:SSD_IN], att, g[SSD_IN + QW:]], axis=0)


def _rope_tables(T):
    n = T - CTX
    t = jnp.arange(n, dtype=F32)
    row, col = jnp.floor(t / GRID_W), t % GRID_W
    pairs = ROPE // 4
    inv = ROPE_THETA ** (-jnp.arange(pairs, dtype=F32) / pairs)
    ar, ac = row[:, None] * inv, col[:, None] * inv
    cos = jnp.concatenate([jnp.cos(ar)] * 2 + [jnp.cos(ac)] * 2, axis=1)
    sin = jnp.concatenate([-jnp.sin(ar), jnp.sin(ar), -jnp.sin(ac), jnp.sin(ac)], axis=1)
    ones, zeros = jnp.ones((n, NOPE), F32), jnp.zeros((n, NOPE), F32)
    cos = jnp.concatenate([ones, cos, ones[:, :HP - QK]], axis=1)
    sin = jnp.concatenate([zeros, sin, zeros[:, :HP - QK]], axis=1)
    return (jnp.concatenate([jnp.ones((CTX, HP), F32), cos], axis=0),
            jnp.concatenate([jnp.zeros((CTX, HP), F32), sin], axis=0))


def _lane_pad(v, n):
    return jnp.pad(v, (0, n - v.shape[0]))[None, :]


def _layer_weights(w_in, w_out, w_q_b, w_kv_b, conv_w, pool_w):
    kv = w_kv_b.reshape(QL, H, NOPE + VH)
    wbd = jnp.zeros((PD, PD), F32)
    for g in range(4):
        wbd = lax.dynamic_update_slice(wbd, pool_w[g], (64 * g, 64 * g))
    return dict(
        w_in=_arrange_w_in(w_in).astype(BF), w_out=_arrange_w_out(w_out).astype(BF),
        wq=_pad_heads(w_q_b, QK).astype(BF),
        wk=_pad_heads(kv[:, :, :NOPE].reshape(QL, H * NOPE), NOPE).astype(BF),
        wv=_pad_heads(kv[:, :, NOPE:].reshape(QL, H * VH), VH).astype(BF),
        conv_w8=jnp.pad(conv_w, ((0, 4), (0, 0))), wbd=wbd)


def _layer_fwd(R, x, mod, lw, sp, w1, w2, cos, sin):
    B, T = R.B, R.T
    z, xbc_raw, qa, kva, pool_in, kr, dt_raw, h1 = _in_proj(R, x, mod, sp["nw1"], lw["w_in"])
    xbc, dt2 = _ssd_prep(R, xbc_raw, dt_raw, lw["conv_w8"], sp["conv_b"], sp["dtb"])
    y2, hin = _ssd_scan(B, T, xbc, dt2, sp["alog2"])
    qt, kt, vt, cq, ckv = _mla_prep(R, qa, kva, kr, sp["qnw"], sp["kvnw"], lw["wq"], lw["wk"], lw["wv"], cos, sin)
    o, lse = _flash_fwd(B, T, qt, kt, vt)
    pool = _pool_fwd(R, pool_in, lw["wbd"], sp["pscale"])[0]
    xmid, cat, mixbf = _out_proj(R, x, mod, y2, xbc, z, o, pool, sp["dsk"], sp["snw"], lw["w_out"])
    xo, h2, ubf, ybf = _mlp_fwd(R, xmid, mod, sp["nw2"], w1, w2)
    saved = dict(x=x, z=z, xbc_raw=xbc_raw, qa=qa, kva=kva, pool_in=pool_in, dt_raw=dt_raw, h1=h1, xbc=xbc, dt2=dt2,
                 y2=y2, hin=hin, qt=qt, kt=kt, vt=vt, cq=cq, ckv=ckv, o=o, lse=lse, cat=cat, mixbf=mixbf, xmid=xmid,
                 h2=h2, ubf=ubf, ybf=ybf)
    return xo, saved


def _layer_bwd(R, dxo, sv, mod, lw, sp, w1, w2, cos, sin):
    B, T = R.B, R.T
    nq = T // TQ
    dxm, du, abf, dyb, dmod_a, dnw2 = _mlp_bwd(R, dxo, sv["xmid"], sv["ubf"], sv["ybf"], mod, sp["nw2"], w1, w2)
    g_w1 = _tn_matmul("dw_mlp1", sv["h2"], du, FSH)
    g_w2 = _tn_matmul("dw_mlp2", abf, dyb, D)[0].reshape(NDEV, FSH, D)
    dmix, dz, dyt, dxsk, do, dpo, dmod_b, dvec_o = _out_bwd(R, dxm, mod, sv["mixbf"], sv["y2"], sv["xbc"], sv["z"],
                                                            sp["dsk"], sp["snw"], lw["w_out"])
    g_wout = _unarrange_w_out(_tn_matmul("dw_out", sv["cat"], dmix, D)[0])
    dpool_in, g_wbd, dpsc = _pool_bwd(R, dpo, sv["pool_in"], lw["wbd"], sp["pscale"])
    dqt, dd = _flash_bwd_dq(B, T, sv["qt"], sv["kt"], sv["vt"], do, sv["o"], sv["lse"])
    lse_r = sv["lse"].reshape(B, H, nq, 1, TQ)
    dd_r = dd.reshape(B, H, nq, 1, TQ)
    dkt, dvt = _flash_bwd_dkv(B, T, sv["qt"], sv["kt"], sv["vt"], do, lse_r, dd_r)
    dqp, dkb, dvb, dqa, dkva, dkr, dnw_qk = _mla_bwd(R, dqt, dkt, dvt, sv["qa"], sv["kva"], sp["qnw"], sp["kvnw"],
                                                     lw["wq"], lw["wk"], lw["wv"], cos, sin)
    g_wq = _unpad_heads(_tn_matmul("dw_q", sv["cq"], dqp, QW)[0], QK)
    g_wk = _unpad_heads(_tn_matmul("dw_k", sv["ckv"], dkb, QW)[0], NOPE).reshape(QL, H, NOPE)
    g_wv = _unpad_heads(_tn_matmul("dw_v", sv["ckv"], dvb, QW)[0], VH).reshape(QL, H, VH)
    g_wkv = jnp.concatenate([g_wk, g_wv], axis=2).reshape(QL, H * (NOPE + VH))
    dxbc2, ddt2, da = _ssd_scan_bwd(B, T, dyt, sv["xbc"], sv["dt2"], sp["alog2"], sv["hin"])
    dpre, ddtr, dcw, dcb, ddtb = _ssd_prep_bwd(R, dxbc2, dxsk, ddt2, sv["xbc_raw"], sv["dt_raw"], lw["conv_w8"],
                                                sp["conv_b"], sp["dtb"])
    dx, dproj, dmod_c, dnw1 = _in_bwd(R, dxm, sv["x"], mod, sp["nw1"], dz, dpre, dqa, dkva, dpool_in, dkr, ddtr,
                                      lw["conv_w8"], lw["w_in"])
    g_win = _unarrange_w_in(jnp.concatenate(list(_tn_matmul("dw_in", sv["h1"], dproj, PC // 3)), axis=1))
    a2 = -jnp.exp(sp["alog2"][:, 0, :H])
    small = dict(
        norm1_w=dnw1[0], norm2_w=dnw2[0], conv_w=dcw[0:4], conv_b=dcb[0], dt_bias=ddtb[0, :2 * H].reshape(2, H),
        a_log=jnp.sum(da[:, :, 0, :H], axis=0) * a2, ssd_d=jnp.sum(dvec_o[1].reshape(H, P), axis=1), ssd_norm_w=dvec_o[0],
        q_a_norm_w=dnw_qk[0], kv_a_norm_w=dnw_qk[1],
        pool_w=jnp.stack([g_wbd[64 * g:64 * (g + 1), 64 * g:64 * (g + 1)] for g in range(4)]), pool_scale=dpsc[0])
    big = dict(w_in=g_win, w_out=g_wout, w_q_b=g_wq, w_kv_b=g_wkv, w_mlp1=g_w1, w_mlp2=g_w2)
    return dx, big, small, dmod_a + dmod_b + dmod_c


def _small_params(l, norm1_w, norm2_w, conv_b, dt_bias, a_log, ssd_d, ssd_norm_w, q_a_norm_w, kv_a_norm_w, pool_scale):
    alog2 = jnp.broadcast_to(jnp.pad(a_log[l], ((0, 0), (0, HP - H)))[:, None, :], (2, 8, HP))
    return dict(nw1=norm1_w[l][None], nw2=norm2_w[l][None], conv_b=conv_b[l][None],
                dtb=_lane_pad(dt_bias[l].reshape(2 * H), HP), alog2=alog2,
                dsk=jnp.repeat(ssd_d[l], P)[None], snw=ssd_norm_w[l][None], qnw=q_a_norm_w[l][None],
                kvnw=kv_a_norm_w[l][None], pscale=pool_scale[l][None])


SMALL_NAMES = ["mod_b", "norm1_w", "norm2_w", "conv_b", "dt_bias", "a_log", "ssd_d", "ssd_norm_w", "q_a_norm_w",
               "kv_a_norm_w", "pool_w", "pool_scale", "final_norm_w"]


def _pack(arrs):
    rows = []
    for a in arrs:
        f = a.reshape(-1).astype(F32)
        n = -(-f.shape[0] // HP) * HP
        rows.append(jnp.pad(f, (0, n - f.shape[0])).reshape(-1, HP))
    out = jnp.concatenate(rows, axis=0)
    pad = (-out.shape[0]) % 8
    return jnp.pad(out, ((0, pad), (0, 0)))


def _unpack(pack, like):
    outs, r = [], 0
    for a in like:
        n = math.prod(a.shape)
        nr = -(-n // HP)
        outs.append(pack[r:r + nr].reshape(-1)[:n].reshape(a.shape))
        r += nr
    return outs


def _local_step(x, ctx, target, mods, full, small_w):
    B, N = x.shape[0], x.shape[1]
    T = CTX + N
    R = _Rows(B, T)
    cos, sin = _rope_tables(T)
    xu = jnp.concatenate([ctx, x], axis=1)
    L = len(mods)
    lws, sps, saves = [], [], []
    for l in range(L):
        f = full[l]
        lws.append(_layer_weights(f["w_in"], f["w_out"], f["w_q_b"], f["w_kv_b"], f["conv_w"], small_w["pool_w"][l]))
        sps.append(_small_params(l, *[small_w[k] for k in ["norm1_w", "norm2_w", "conv_b", "dt_bias", "a_log", "ssd_d",
                                                          "ssd_norm_w", "q_a_norm_w", "kv_a_norm_w", "pool_scale"]]))
        xu, sv = _layer_fwd(R, xu, mods[l], lws[l], sps[l], f["w_mlp1"], f["w_mlp2"], cos, sin)
        saves.append(sv)
    dx, loss8, dfnw = _loss_head(R, xu, target, small_w["final_norm_w"][None])
    bigs, smalls, dmods = [None] * L, [None] * L, [None] * L
    for l in reversed(range(L)):
        f = full[l]
        dx, bigs[l], smalls[l], dmods[l] = _layer_bwd(R, dx, saves[l], mods[l], lws[l], sps[l], f["w_mlp1"], f["w_mlp2"], cos, sin)
    return loss8[0, 0], dx[:, CTX:], bigs, smalls, dfnw[0], dmods


def kernel(x, c, ctx, c_ctx, mod_w, mod_b, norm1_w, norm2_w, w_in, conv_w, conv_b, dt_bias, a_log, ssd_d, ssd_norm_w, q_a_norm_w, w_q_b, kv_a_norm_w, w_kv_b, pool_w, pool_scale, w_out, w_mlp1, w_mlp2, final_norm_w, loss_target, m_c_ctx, m_mod_w, m_mod_b, m_norm1_w, m_norm2_w, m_w_in, m_conv_w, m_conv_b, m_dt_bias, m_a_log, m_ssd_d, m_ssd_norm_w, m_q_a_norm_w, m_w_q_b, m_kv_a_norm_w, m_w_kv_b, m_pool_w, m_pool_scale, m_w_out, m_w_mlp1, m_w_mlp2, m_final_norm_w, v_c_ctx, v_mod_w, v_mod_b, v_norm1_w, v_norm2_w, v_w_in, v_conv_w, v_conv_b, v_dt_bias, v_a_log, v_ssd_d, v_ssd_norm_w, v_q_a_norm_w, v_w_q_b, v_kv_a_norm_w, v_w_kv_b, v_pool_w, v_pool_scale, v_w_out, v_w_mlp1, v_w_mlp2, v_final_norm_w):
    args = dict(locals())
    B = x.shape[0]
    L = mod_w.shape[0]
    me = _dev_index(*_my_pos())
    CS = mod_w.shape[2]

    c_all, convw_all, win_g, wout_g, wq_g, wkv_g, w1_g, w2_g = _all_gather(
        "gather_weights",
        [c, conv_w, w_in.astype(BF), w_out.astype(BF), w_q_b.astype(BF), w_kv_b.astype(BF), w_mlp1.astype(BF), w_mlp2.astype(BF)])
    cs = jnp.concatenate([c_all.reshape(NDEV * B, D), c_ctx[None], jnp.zeros((24 - NDEV * B - 1, D), F32)], axis=0)
    m_loc = _adaln_fwd(cs, mod_w)
    m_all = _all_gather("gather_mod", [m_loc])[0]
    m_full = jnp.moveaxis(m_all, 0, 2).reshape(L, 24, NDEV * CS) + mod_b[:, None, :]
    mods = []
    for l in range(L):
        ex = lax.dynamic_slice(m_full[l], (me * B, 0), (B, 6 * D)).reshape(B, 6, D)
        cc = jnp.broadcast_to(m_full[l, NDEV * B].reshape(1, 6, D), (B, 6, D))
        mods.append(jnp.pad(jnp.stack([cc, ex], axis=1), ((0, 0), (0, 0), (0, 2), (0, 0))))
    full = []
    for l in range(L):
        full.append(dict(
            w_in=win_g[:, l].reshape(D, IN_COLS), w_out=wout_g[:, l].reshape(D, D),
            w_q_b=jnp.moveaxis(wq_g[:, l], 0, 1).reshape(QL, H * QK),
            w_kv_b=jnp.moveaxis(wkv_g[:, l], 0, 1).reshape(QL, H * (NOPE + VH)),
            conv_w=jnp.moveaxis(convw_all[:, l], 0, 1).reshape(4, XBC),
            w_mlp1=w1_g[:, l], w_mlp2=w2_g[:, l]))
    small_w = {k: args[k] for k in SMALL_NAMES if k != "mod_b"}

    loss_part, grad_x, bigs, smalls, dfnw, dmods = _local_step(x, ctx, loss_target, mods, full, small_w)
    loss = lax.psum(loss_part, ("x", "y", "c"))

    dm_ex = jnp.stack([dmods[l][:, 1, :6].reshape(B, 6 * D) for l in range(L)])
    dm_cc = jnp.stack([jnp.sum(dmods[l][:, 0, :6], axis=0).reshape(6 * D) for l in range(L)])
    small_parts = dict(
        mod_b=jnp.sum(dm_ex, axis=1) + dm_cc,
        **{k: jnp.stack([smalls[l][k] for l in range(L)]) for k in SMALL_NAMES[1:-1]},
        final_norm_w=dfnw, conv_w=jnp.stack([smalls[l]["conv_w"] for l in range(L)]), dm_cc=dm_cc)
    adam_grads = [small_parts[k] for k in SMALL_NAMES]
    extras = [small_parts["conv_w"], dm_cc, dm_ex]
    pack = jnp.concatenate([_pack(adam_grads), _pack(extras)], axis=0)
    pack_all = _all_gather("gather_small_grads", [pack])[0]
    wpack = _pack([args[k] for k in SMALL_NAMES])
    mpack = _pack([args["m_" + k] for k in SMALL_NAMES])
    vpack = _pack([args["v_" + k] for k in SMALL_NAMES])
    n_adam = wpack.shape[0]
    res_small = _adamw("adamw_small", pack_all[:, :n_adam], wpack, mpack, vpack, n_adam)
    small_out = [_unpack(r, [args[k] for k in SMALL_NAMES]) for r in res_small]
    ext_all = pack_all[:, n_adam:]
    ext_sum = ext_all[0]
    for k in range(1, NDEV):
        ext_sum = ext_sum + ext_all[k]
    g_conv_full, dm_cc_tot, _ = _unpack(ext_sum, extras)
    dm_ex_all = jnp.stack([_unpack(ext_all[k], extras)[2] for k in range(NDEV)], axis=1)
    dm_rows = jnp.concatenate([dm_ex_all.reshape(L, NDEV * B, 6 * D), dm_cc_tot[:, None, :],
                               jnp.zeros((L, 24 - NDEV * B - 1, 6 * D), F32)], axis=1)
    dm_loc = lax.dynamic_slice(dm_rows, (0, 0, me * CS), (L, 24, CS))
    g_modw, gc_part = _adaln_bwd(cs, dm_loc, mod_w)
    gc_all = _all_gather("gather_cctx_grad", [gc_part[NDEV * B:NDEV * B + 8]])[0]
    cpad = lambda a: jnp.pad(a[None], ((0, 7), (0, 0)))
    res_cc = _adamw("adamw_cctx", gc_all, cpad(c_ctx), cpad(m_c_ctx), cpad(v_c_ctx), 8)
    cc_out = [r[0] for r in res_cc]

    def stack_dev(name, fn):
        return jnp.stack([fn(bigs[l][name]) for l in range(L)], axis=1).astype(BF)

    sends = [
        stack_dev("w_in", lambda g: g.reshape(NDEV, D // NDEV, IN_COLS)),
        stack_dev("w_out", lambda g: g.reshape(NDEV, D // NDEV, D)),
        stack_dev("w_q_b", lambda g: jnp.moveaxis(g.reshape(QL, NDEV, -1), 1, 0)),
        stack_dev("w_kv_b", lambda g: jnp.moveaxis(g.reshape(QL, NDEV, -1), 1, 0)),
        stack_dev("w_mlp1", lambda g: g), stack_dev("w_mlp2", lambda g: g)]
    recvs = _all_to_all("exchange_grads", sends)
    big_names = ["w_in", "w_out", "w_q_b", "w_kv_b", "w_mlp1", "w_mlp2"]
    big_out = {}
    for name, rv in zip(big_names, recvs):
        w = args[name]
        Rr, C = math.prod(w.shape[:-1]), w.shape[-1]
        rt = Rr if Rr * C <= (1 << 18) else Rr // 8
        res = _adamw("adamw_" + name, rv.reshape(NDEV, Rr, C), w.reshape(Rr, C), args["m_" + name].reshape(Rr, C),
                     args["v_" + name].reshape(Rr, C), rt)
        big_out[name] = [r.reshape(w.shape) for r in res]
    res = _adamw("adamw_mod_w", g_modw.reshape(1, L * D, CS), mod_w.reshape(L * D, CS), m_mod_w.reshape(L * D, CS),
                 v_mod_w.reshape(L * D, CS), L * D // 8)
    big_out["mod_w"] = [r.reshape(mod_w.shape) for r in res]
    CW = conv_w.shape[2]
    g_conv = lax.dynamic_slice(g_conv_full, (0, 0, me * CW), (L, 4, CW))
    res = _adamw("adamw_conv_w", g_conv.reshape(1, L * 4, CW), conv_w.reshape(L * 4, CW), m_conv_w.reshape(L * 4, CW),
                 v_conv_w.reshape(L * 4, CW), L * 4)
    big_out["conv_w"] = [r.reshape(conv_w.shape) for r in res]

    weights = ["c_ctx", "mod_w", "mod_b", "norm1_w", "norm2_w", "w_in", "conv_w", "conv_b", "dt_bias", "a_log", "ssd_d",
               "ssd_norm_w", "q_a_norm_w", "w_q_b", "kv_a_norm_w", "w_kv_b", "pool_w", "pool_scale", "w_out", "w_mlp1",
               "w_mlp2", "final_norm_w"]
    outs = [loss, grad_x]
    for kind in range(4):
        for name in weights:
            if name == "c_ctx":
                outs.append(cc_out[kind])
            elif name in big_out:
                outs.append(big_out[name][kind])
            else:
                outs.append(small_out[kind][SMALL_NAMES.index(name)])
    return tuple(outs)
```

```python
import functools
import math

import jax
import jax.numpy as jnp
from jax import lax
from jax.experimental import pallas as pl
from jax.experimental.pallas import tpu as pltpu

F32 = jnp.float32
BF = jnp.bfloat16
MXU = BF

D = 1024
CTX = 256
GRID_W = 64
EPS = 1e-6
H = 6
P = 64
SSD_IN = 384
NST = 128
XBC = 896
CH = 128
QL = 256
NOPE = 64
ROPE = 32
VH = 64
QK = 96
HP = 128
QW = H * HP
PD = 256
FF = 4096
IN_COLS = 2092
ROPE_THETA = 10000.0
PC = 2304
A_Z, A_XBC, A_QA, A_KVA, A_POOL, A_KR, A_DT = 0, 384, 1280, 1536, 1792, 2048, 2176
MIXW = SSD_IN + QW + PD
NDEV = 8
FSH = FF // NDEV
TM = 256
TQ = 256
TK = 256
VMEM_CAP = 64 * 1024 * 1024
ADAM_LR, ADAM_B1, ADAM_B2, ADAM_EPS, ADAM_WD, ADAM_STEP = 0.001, 0.9, 0.999, 1e-08, 0.01, 10


def _nbytes(shape, dtype):
    n = 1
    for s in shape:
        if s is not None:
            n *= s
    return n * jnp.dtype(dtype).itemsize


def _params(sem, block_bytes, extra=0):
    lim = min(2 * block_bytes + extra + (8 << 20), VMEM_CAP - (6 << 20))
    return pltpu.CompilerParams(dimension_semantics=sem, vmem_limit_bytes=int(lim))


def _dot(a, b):
    return jnp.dot(a.astype(MXU), b.astype(MXU), preferred_element_type=F32)


def _dot_nt(a, b):
    return lax.dot_general(a.astype(MXU), b.astype(MXU), (((1,), (1,)), ((), ())), preferred_element_type=F32)


def _dot_tn(a, b):
    return lax.dot_general(a.astype(MXU), b.astype(MXU), (((0,), (0,)), ((), ())), preferred_element_type=F32)


def _dot01(m01, x):
    b16 = jnp.bfloat16
    m = m01.astype(b16)
    hi = x.astype(b16)
    r1 = x - hi.astype(F32)
    mid = r1.astype(b16)
    lo = (r1 - mid.astype(F32)).astype(b16)
    f = lambda v: jnp.dot(m, v, preferred_element_type=F32)
    return f(hi) + f(mid) + f(lo)


def _sigmoid(x):
    return 1.0 / (1.0 + jnp.exp(-x))


def _silu(x):
    return x * _sigmoid(x)


def _dsilu(x):
    s = _sigmoid(x)
    return s * (1.0 + x * (1.0 - s))


def _iota(shape, dim):
    return lax.broadcasted_iota(jnp.int32, shape, dim)


def _row(ref, k):
    blk = ref[...]
    return jnp.sum(jnp.where(_iota(blk.shape, 0) == k, blk, 0.0), axis=0, keepdims=True)


def _shift_rows(x, k):
    n = x.shape[0]
    return pltpu.roll(x, (-k) % n, axis=0)


class _Rows:
    def __init__(self, B, T):
        self.B, self.T = B, T
        self.nt = T // TM
        self.nct = CTX // TM

    def row(self, F):
        return pl.BlockSpec((None, TM, F), lambda b, i: (b, i, 0))

    def row2(self, F):
        return pl.BlockSpec((None, 2, TM, F), lambda b, i: (b, 0, i, 0))

    def prev8(self, F):
        return pl.BlockSpec((None, 8, F), lambda b, i: (b, jnp.maximum(i * (TM // 8) - 1, 0), 0))

    def next8(self, F):
        last = self.T // 8 - 1
        return pl.BlockSpec((None, 8, F), lambda b, i: (b, jnp.minimum((i + 1) * (TM // 8), last), 0))

    def mod(self):
        nct = self.nct
        return pl.BlockSpec((None, None, 8, D), lambda b, i: (b, jnp.where(i < nct, 0, 1), 0, 0))

    def const(self, shape):
        z = (0,) * len(shape)
        return pl.BlockSpec(tuple(shape), lambda b, i: z)

    def tgt(self, F):
        nct = self.nct
        return pl.BlockSpec((None, TM, F), lambda b, i: (b, jnp.maximum(i - nct, 0), 0))

    def call(self, name, body, ins, outs, scratch=(), extra=0):
        arrays = [a for a, _ in ins]
        in_specs = [s for _, s in ins]
        out_shape = [o for o, _ in outs]
        out_specs = [s for _, s in outs]
        bb = 0
        for a, s in list(ins) + list(outs):
            if s.block_shape is not None:
                bb += _nbytes(s.block_shape, a.dtype)
        return pl.pallas_call(
            functools.partial(body), name=name, grid=(self.B, self.nt),
            in_specs=in_specs, out_specs=out_specs, out_shape=out_shape, scratch_shapes=list(scratch),
            compiler_params=_params(("arbitrary", "arbitrary"), bb, extra),
        )(*arrays)

    def first_of_stream(self, i):
        return jnp.logical_or(i == 0, i == self.nct)

    def last_of_stream(self, i):
        return jnp.logical_or(i == self.nct - 1, i == self.nt - 1)

    def ext(self, i, prev_ref, cur, next_ref):
        pv = prev_ref[...].astype(F32) * jnp.where(self.first_of_stream(i), 0.0, 1.0)
        nx = next_ref[...].astype(F32) * jnp.where(self.last_of_stream(i), 0.0, 1.0)
        return jnp.concatenate([pv, cur, nx], axis=0)

    def stream_pos(self, i, rows):
        start = jnp.where(i < self.nct, 0, CTX)
        n = jnp.where(i < self.nct, CTX, self.T - CTX)
        t = i * TM - 8 - start + _iota((rows, 1), 0)
        return t, n


def _sds(shape, dtype):
    return jax.ShapeDtypeStruct(tuple(shape), dtype)


def _norm_mod(x, nw, sh, sc):
    r = lax.rsqrt(jnp.mean(x * x, axis=-1, keepdims=True) + EPS)
    xn = x * r
    return xn * nw * (1.0 + sc) + sh, xn, r


def _norm_mod_bwd(dh, xn, r, nw, sc):
    dsh = jnp.sum(dh, axis=0, keepdims=True)
    dsc = jnp.sum(dh * (xn * nw), axis=0, keepdims=True)
    dnw = jnp.sum(dh * (1.0 + sc) * xn, axis=0, keepdims=True)
    dxn = dh * nw * (1.0 + sc)
    dx = r * (dxn - xn * jnp.mean(dxn * xn, axis=-1, keepdims=True))
    return dx, dsh, dsc, dnw


def _acc_rows(ref, first, rows):
    rid = _iota(ref.shape, 0)
    upd = jnp.zeros(ref.shape, F32)
    for k, v in rows.items():
        upd = upd + jnp.where(rid == k, v, 0.0)

    @pl.when(first)
    def _():
        ref[...] = upd

    @pl.when(jnp.logical_not(first))
    def _():
        ref[...] += upd


def _in_proj(R, x, mod, nw1, w_arr):
    B, T = R.B, R.T

    def body(x_ref, mod_ref, nw_ref, w_ref, z_ref, xbc_ref, qa_ref, kva_ref, pool_ref, kr_ref, dt_ref, h_ref):
        h, _, _ = _norm_mod(x_ref[...], nw_ref[...], _row(mod_ref, 0), _row(mod_ref, 1))
        hb = h.astype(BF)
        h_ref[...] = hb
        p = jnp.dot(hb, w_ref[...], preferred_element_type=F32)
        z_ref[...] = p[:, A_Z:A_XBC]
        xbc_ref[...] = p[:, A_XBC:A_QA]
        qa_ref[...] = p[:, A_QA:A_KVA]
        kva_ref[...] = p[:, A_KVA:A_POOL]
        pool_ref[...] = p[:, A_POOL:A_KR]
        kr_ref[...] = p[:, A_KR:A_DT]
        dt_ref[...] = p[:, A_DT:PC]

    widths = [SSD_IN, XBC, QL, QL, PD, HP, HP]
    outs = [(_sds((B, T, w), F32), R.row(w)) for w in widths] + [(_sds((B, T, D), BF), R.row(D))]
    return R.call("in_proj", body,
                  [(x, R.row(D)), (mod, R.mod()), (nw1, R.const((1, D))), (w_arr, R.const((D, PC)))],
                  outs, extra=8 << 20)


def _conv_pre(ext, w_ref, b_ref):
    return (_row(w_ref, 0) * _shift_rows(ext, -1)[8:8 + TM] + _row(w_ref, 1) * ext[8:8 + TM]
            + _row(w_ref, 2) * _shift_rows(ext, 1)[8:8 + TM] + _row(w_ref, 3) * _shift_rows(ext, 2)[8:8 + TM]
            + b_ref[...])


def _softplus(x):
    return jnp.maximum(x, 0.0) + jnp.log(1.0 + jnp.exp(-jnp.abs(x)))


def _ssd_prep(R, xbc_raw, dt_raw, conv_w8, conv_b, dtb):
    B, T = R.B, R.T

    def body(raw_ref, pv_ref, nx_ref, dtr_ref, w_ref, b_ref, dtb_ref, xbc_ref, dt_ref):
        i = pl.program_id(1)
        ext = R.ext(i, pv_ref, raw_ref[...], nx_ref)
        xbc_ref[...] = _silu(_conv_pre(ext, w_ref, b_ref))
        lane = _iota((TM, HP), 1)
        dtv = _softplus(dtr_ref[...] + dtb_ref[...])
        keep = lane < H
        dt_ref[0] = jnp.where(keep, dtv, 0.0)
        dt_ref[1] = jnp.where(keep, pltpu.roll(dtv, HP - H, axis=1), 0.0)

    return R.call("ssd_prep", body,
                  [(xbc_raw, R.row(XBC)), (xbc_raw, R.prev8(XBC)), (xbc_raw, R.next8(XBC)), (dt_raw, R.row(HP)),
                   (conv_w8, R.const((8, XBC))), (conv_b, R.const((1, XBC))), (dtb, R.const((1, HP)))],
                  [(_sds((B, T, XBC), F32), R.row(XBC)), (_sds((B, 2, T, HP), F32), R.row2(HP))], extra=12 << 20)


def _chunk_index(d, s, nc, nctc):
    rev = jnp.where(s < nctc, nctc - 1 - s, nc - 1 - (s - nctc))
    return jnp.where(d == 0, s, rev)


def _scan_common(d, xbc, dtv, alog_ref):
    a = -jnp.exp(_row(alog_ref, 0))
    adt = dtv * a
    row = _iota((CH, CH), 0)
    col = _iota((CH, CH), 1)
    sgn = jnp.where(d == 0, 1, -1)
    inc = (row - col) * sgn >= 0
    inc_t = (col - row) * sgn >= 0
    q = _dot01(inc, adt)
    return a, adt, inc, inc_t, q


def _head_cols(h, adt, q, dtv, inc_t):
    onehot = (_iota((1, HP), 1) == h).astype(F32)
    adt_h = jnp.sum(adt * onehot, axis=1, keepdims=True)
    qc = jnp.sum(q * onehot, axis=1, keepdims=True)
    dt_h = jnp.sum(dtv * onehot, axis=1, keepdims=True)
    qr = jnp.sum(adt_h * inc_t.astype(F32), axis=0, keepdims=True)
    qtot = jnp.sum(adt_h, axis=0, keepdims=True)
    return onehot, qc, qr, qtot, dt_h


def _ssd_scan(B, T, xbc, dt2, alog2):
    nc, nctc = T // CH, CTX // CH

    def body(xbc_ref, dt_ref, alog_ref, y_ref, hin_ref, hs):
        d, s = pl.program_id(1), pl.program_id(2)

        @pl.when(s == 0)
        def _():
            hs[...] = jnp.zeros(hs.shape, F32)

        xbc_v = xbc_ref[...]
        dtv = dt_ref[...]
        _, adt, inc, inc_t, q = _scan_common(d, xbc_v, dtv, alog_ref)
        hin_ref[...] = hs[...]
        for g in range(2):
            bg = xbc_v[:, SSD_IN + NST * g:SSD_IN + NST * (g + 1)]
            cg = xbc_v[:, SSD_IN + 2 * NST + NST * g:SSD_IN + 2 * NST + NST * (g + 1)]
            gm = _dot_nt(cg, bg)
            for r in range(3):
                h = 3 * g + r
                _, qc, qr, qtot, dt_h = _head_cols(h, adt, q, dtv, inc_t)
                lm = jnp.where(inc, jnp.exp(qc - qr), 0.0)
                xh = xbc_v[:, P * h:P * (h + 1)] * dt_h
                hprev = hs[P * h:P * (h + 1), :]
                y = _dot(gm * lm, xh) + jnp.exp(qc) * _dot_nt(cg, hprev)
                y_ref[:, P * h:P * (h + 1)] = y
                hs[P * h:P * (h + 1), :] = jnp.exp(qtot) * hprev + _dot_tn(xh * jnp.exp(qtot - qc), bg)

    cidx = lambda d, s: _chunk_index(d, s, nc, nctc)
    bb = _nbytes((CH, XBC), F32) + _nbytes((CH, HP), F32) + _nbytes((CH, SSD_IN), F32) + _nbytes((SSD_IN, NST), F32)
    return pl.pallas_call(
        functools.partial(body), name="ssd_scan", grid=(B, 2, nc),
        in_specs=[pl.BlockSpec((None, CH, XBC), lambda b, d, s: (b, cidx(d, s), 0)),
                  pl.BlockSpec((None, None, CH, HP), lambda b, d, s: (b, d, cidx(d, s), 0)),
                  pl.BlockSpec((None, 8, HP), lambda b, d, s: (d, 0, 0))],
        out_specs=[pl.BlockSpec((None, None, CH, SSD_IN), lambda b, d, s: (b, d, cidx(d, s), 0)),
                   pl.BlockSpec((None, None, None, SSD_IN, NST), lambda b, d, s: (b, d, cidx(d, s), 0, 0))],
        out_shape=[_sds((B, 2, T, SSD_IN), F32), _sds((B, 2, nc, SSD_IN, NST), F32)],
        scratch_shapes=[pltpu.VMEM((SSD_IN, NST), F32)],
        compiler_params=_params(("arbitrary",) * 3, bb, 8 << 20),
    )(xbc, dt2, alog2)


def _swap8(u):
    lane = _iota(u.shape, 1)
    n = u.shape[1]
    return jnp.where((lane & 15) < 8, pltpu.roll(u, n - 8, axis=1), pltpu.roll(u, 8, axis=1))


def _rope(u, cos, sin_signed):
    return u * cos + _swap8(u) * sin_signed


def _rms(x, w):
    r = lax.rsqrt(jnp.mean(x * x, axis=-1, keepdims=True) + EPS)
    xh = x * r
    return xh * w, xh, r


def _rms_bwd(dy, xh, r, w):
    dw = jnp.sum(dy * xh, axis=0, keepdims=True)
    dxh = dy * w
    return r * (dxh - xh * jnp.mean(dxh * xh, axis=-1, keepdims=True)), dw


def _tile6(t):
    return jnp.concatenate([t] * H, axis=1)


def _per_head(fn, u):
    return jnp.concatenate([fn(u[:, HP * h:HP * (h + 1)]) for h in range(H)], axis=1)


def _mla_prep(R, qa, kva, kr, qnw, kvnw, wq, wk, wv, cos, sin):
    B, T = R.B, R.T
    scale = QK ** -0.5

    def body(qa_ref, kva_ref, kr_ref, qnw_ref, kvnw_ref, wq_ref, wk_ref, wv_ref, cos_ref, sin_ref,
             qt_ref, kt_ref, vt_ref, cq_ref, ckv_ref):
        cq, _, _ = _rms(qa_ref[...], qnw_ref[...])
        ckv, _, _ = _rms(kva_ref[...], kvnw_ref[...])
        cqb, ckvb = cq.astype(BF), ckv.astype(BF)
        cq_ref[...] = cqb
        ckv_ref[...] = ckvb
        cos1, sin1 = cos_ref[...], sin_ref[...]
        q = jnp.dot(cqb, wq_ref[...], preferred_element_type=F32)
        qt_ref[...] = (_per_head(lambda u: _rope(u, cos1, sin1), q) * scale).astype(BF)
        kk = _rope(kr_ref[...], cos1, sin1)
        k = jnp.dot(ckvb, wk_ref[...], preferred_element_type=F32)
        kt_ref[...] = (k + _tile6(kk)).astype(BF)
        v = jnp.dot(ckvb, wv_ref[...], preferred_element_type=F32)
        vt_ref[...] = jnp.where((_iota((TM, QW), 1) & (HP - 1)) == VH, 1.0, v).astype(BF)

    return R.call("mla_prep", body,
                  [(qa, R.row(QL)), (kva, R.row(QL)), (kr, R.row(HP)), (qnw, R.const((1, QL))), (kvnw, R.const((1, QL))),
                   (wq, R.const((QL, QW))), (wk, R.const((QL, QW))), (wv, R.const((QL, QW))),
                   (cos, pl.BlockSpec((TM, HP), lambda b, i: (i, 0))), (sin, pl.BlockSpec((TM, HP), lambda b, i: (i, 0)))],
                  [(_sds((B, T, QW), BF), R.row(QW))] * 3 + [(_sds((B, T, QL), BF), R.row(QL))] * 2, extra=8 << 20)


def _flash_fwd(B, T, qt, kt, vt):
    nq, nk = T // TQ, T // TK

    def body(q_ref, k_ref, v_ref, o_ref, lse_ref, s_scr):
        i = pl.program_id(2)
        q = q_ref[...]

        def attend(nch):
            mrun = None
            for j in range(nch):
                s = _dot_nt(q, k_ref[TK * j:TK * (j + 1), :])
                s_scr[j] = s
                mj = jnp.maximum(s[:, :HP], s[:, HP:])
                mrun = mj if mrun is None else jnp.maximum(mrun, mj)
            m = jnp.max(mrun, axis=1, keepdims=True)
            mb = jnp.broadcast_to(m, (TQ, TK))
            acc = jnp.zeros((TQ, HP), F32)
            for j in range(nch):
                acc = acc + _dot(jnp.exp(s_scr[j] - mb), v_ref[TK * j:TK * (j + 1), :])
            lane = _iota((TQ, HP), 1)
            l = jnp.sum(jnp.where(lane == VH, acc, 0.0), axis=1, keepdims=True)
            o_ref[...] = jnp.where(lane < VH, acc / l, 0.0)
            lse_ref[...] = m + jnp.log(l)

        @pl.when(i < CTX // TQ)
        def _():
            attend(CTX // TK)

        @pl.when(i >= CTX // TQ)
        def _():
            attend(nk)

    bb = _nbytes((TQ, HP), BF) + 2 * _nbytes((T, HP), BF) + _nbytes((TQ, HP), F32) + _nbytes((TQ, 1), F32)
    return pl.pallas_call(
        functools.partial(body), name="flash_fwd", grid=(B, H, nq),
        in_specs=[pl.BlockSpec((None, TQ, HP), lambda b, h, i: (b, i, h)),
                  pl.BlockSpec((None, T, HP), lambda b, h, i: (b, 0, h)),
                  pl.BlockSpec((None, T, HP), lambda b, h, i: (b, 0, h))],
        out_specs=[pl.BlockSpec((None, TQ, HP), lambda b, h, i: (b, i, h)),
                   pl.BlockSpec((None, None, TQ, 1), lambda b, h, i: (b, h, i, 0))],
        out_shape=[_sds((B, T, QW), F32), _sds((B, H, T, 1), F32)],
        scratch_shapes=[pltpu.VMEM((nk, TQ, TK), F32)],
        compiler_params=_params(("arbitrary",) * 3, bb, _nbytes((nk, TQ, TK), F32) + (8 << 20)),
    )(qt, kt, vt)


def _pool_terms(R, i, rows):
    t, n = R.stream_pos(i, rows)
    lane = _iota((1, PD), 1)
    half = jnp.where(lane < 64, 1, jnp.where(lane < 128, 2, jnp.where(lane < 192, 4, 8)))
    cnt = (jnp.minimum(t + half, n) - jnp.maximum(t - half, 0)).astype(F32)
    valid = jnp.logical_and(t >= 0, t < n)
    return jnp.where(valid, cnt, 1.0), valid.astype(F32), lane


def _lane_select(lane, a2, a4, a8, a16):
    return jnp.where(lane < 64, a2, jnp.where(lane < 128, a4, jnp.where(lane < 192, a8, a16)))


def _pool_centred(R, i, ext):
    cnt, valid, lane = _pool_terms(R, i, ext.shape[0])
    s2 = ext + _shift_rows(ext, -1)
    s4 = _shift_rows(s2, -1) + _shift_rows(s2, 1)
    s8 = _shift_rows(s4, -2) + _shift_rows(s4, 2)
    s16 = _shift_rows(s8, -4) + _shift_rows(s8, 4)
    return _lane_select(lane, s2, s4, s8, s16) / cnt - ext, cnt, valid, lane


def _pool_fwd(R, u, wbd, scale):
    B, T = R.B, R.T

    def body(u_ref, pv_ref, nx_ref, w_ref, sc_ref, o_ref):
        i = pl.program_id(1)
        ext = R.ext(i, pv_ref, u_ref[...], nx_ref)
        dm, _, _, _ = _pool_centred(R, i, ext)
        o_ref[...] = _dot(dm[8:8 + TM], w_ref[...]) * sc_ref[...]

    return R.call("pool_fwd", body,
                  [(u, R.row(PD)), (u, R.prev8(PD)), (u, R.next8(PD)), (wbd, R.const((PD, PD))), (scale, R.const((1, PD)))],
                  [(_sds((B, T, PD), F32), R.row(PD))], extra=8 << 20)


def _group_mask():
    return _iota((1, SSD_IN), 1) < SSD_IN // 2


def _ssd_gate(y2_ref, xbc_ref, z_ref, dsk_ref):
    ytot = y2_ref[0] + y2_ref[1] + xbc_ref[:, 0:SSD_IN] * dsk_ref[...]
    z = z_ref[...]
    gz = ytot * _silu(z)
    g0 = _group_mask()
    sq = gz * gz
    s0 = jnp.sum(jnp.where(g0, sq, 0.0), axis=1, keepdims=True)
    s1 = jnp.sum(jnp.where(g0, 0.0, sq), axis=1, keepdims=True)
    half = SSD_IN // 2
    r = jnp.where(g0, lax.rsqrt(s0 / half + EPS), lax.rsqrt(s1 / half + EPS))
    return ytot, z, gz, r


def _out_proj(R, x, mod, y2, xbc, z, o, pool, dsk, snw, wout):
    B, T = R.B, R.T

    def body(x_ref, mod_ref, y2_ref, xbc_ref, z_ref, o_ref, pool_ref, dsk_ref, snw_ref, w_ref, xmid_ref, cat_ref, mix_ref):
        _, _, gz, r = _ssd_gate(y2_ref, xbc_ref, z_ref, dsk_ref)
        cat_ref[:, 0:SSD_IN] = (gz * r * snw_ref[...]).astype(BF)
        cat_ref[:, SSD_IN:SSD_IN + QW] = o_ref[...].astype(BF)
        cat_ref[:, SSD_IN + QW:MIXW] = pool_ref[...].astype(BF)
        mix = jnp.dot(cat_ref[...], w_ref[...], preferred_element_type=F32)
        mix_ref[...] = mix.astype(BF)
        xmid_ref[...] = x_ref[...] + _row(mod_ref, 2) * mix

    return R.call("out_proj", body,
                  [(x, R.row(D)), (mod, R.mod()), (y2, R.row2(SSD_IN)), (xbc, R.row(XBC)), (z, R.row(SSD_IN)), (o, R.row(QW)),
                   (pool, R.row(PD)), (dsk, R.const((1, SSD_IN))), (snw, R.const((1, SSD_IN))), (wout, R.const((MIXW, D)))],
                  [(_sds((B, T, D), F32), R.row(D)), (_sds((B, T, MIXW), BF), R.row(MIXW)), (_sds((B, T, D), BF), R.row(D))],
                  extra=8 << 20)


def _load_once(first, pairs, sem):
    @pl.when(first)
    def _():
        cps = [pltpu.make_async_copy(src, dst, sem.at[k]) for k, (src, dst) in enumerate(pairs)]
        for cp in cps:
            cp.start()
        for cp in cps:
            cp.wait()


ANY = pl.BlockSpec(memory_space=pl.ANY)


def _mlp_fwd(R, xmid, mod, nw2, w1, w2):
    B, T = R.B, R.T

    def body(x_ref, mod_ref, nw_ref, w1_hbm, w2_hbm, xo_ref, h_ref, u_ref, y_ref, w1_v, w2_v, sem):
        first = jnp.logical_and(pl.program_id(0) == 0, pl.program_id(1) == 0)
        _load_once(first, [(w1_hbm, w1_v), (w2_hbm, w2_v)], sem)
        x = x_ref[...]
        h, _, _ = _norm_mod(x, nw_ref[...], _row(mod_ref, 3), _row(mod_ref, 4))
        hb = h.astype(BF)
        h_ref[...] = hb
        y = jnp.zeros((TM, D), F32)
        for j in range(NDEV):
            u = jnp.dot(hb, w1_v[j], preferred_element_type=F32)
            u_ref[:, FSH * j:FSH * (j + 1)] = u.astype(BF)
            a = jnp.square(jnp.maximum(u, 0.0))
            y = y + jnp.dot(a.astype(BF), w2_v[j], preferred_element_type=F32)
        y_ref[...] = y.astype(BF)
        xo_ref[...] = x + _row(mod_ref, 5) * y

    return R.call("mlp_fwd", body,
                  [(xmid, R.row(D)), (mod, R.mod()), (nw2, R.const((1, D))), (w1, ANY), (w2, ANY)],
                  [(_sds((B, T, D), F32), R.row(D)), (_sds((B, T, D), BF), R.row(D)), (_sds((B, T, FF), BF), R.row(FF)),
                   (_sds((B, T, D), BF), R.row(D))],
                  scratch=[pltpu.VMEM((NDEV, D, FSH), w1.dtype), pltpu.VMEM((NDEV, FSH, D), w2.dtype), pltpu.SemaphoreType.DMA((2,))],
                  extra=(2 * _nbytes((NDEV, D, FSH), BF)) + (8 << 20))


def _loss_head(R, x, tgt, fnw):
    B, T = R.B, R.T

    def body(x_ref, t_ref, w_ref, dx_ref, loss_ref, dw_ref):
        b, i = pl.program_id(0), pl.program_id(1)
        live = jnp.where(i >= R.nct, 1.0, 0.0)
        y, xh, r = _rms(x_ref[...], w_ref[...])
        err = (y - t_ref[...]) * live
        dy = err / D
        dxn, dw = _rms_bwd(dy, xh, r, w_ref[...])
        dx_ref[...] = dxn
        first = jnp.logical_and(b == 0, i == 0)
        part = 0.5 * jnp.sum(jnp.sum(err * err, axis=1, keepdims=True), axis=0, keepdims=True) / D
        _acc_rows(loss_ref, first, {0: jnp.broadcast_to(part, (1, HP))})
        _acc_rows(dw_ref, first, {0: dw})

    return R.call("loss_head", body,
                  [(x, R.row(D)), (tgt, R.tgt(D)), (fnw, R.const((1, D)))],
                  [(_sds((B, T, D), F32), R.row(D)), (_sds((8, HP), F32), R.const((8, HP))), (_sds((8, D), F32), R.const((8, D)))],
                  extra=8 << 20)


def _tn_matmul(name, a, b, tn):
    B, T, K = a.shape
    N = b.shape[2]
    nk = 1
    while T % nk or (T // nk) > 1088 or (T // nk) % 16:
        nk += 1
    tk = T // nk
    kt = K if K <= 1536 else 1024
    assert K % kt == 0 and N % tn == 0

    def body(a_ref, b_ref, o_ref):
        first = jnp.logical_and(pl.program_id(2) == 0, pl.program_id(3) == 0)

        @pl.when(first)
        def _():
            o_ref[...] = jnp.zeros(o_ref.shape, F32)

        o_ref[...] += _dot_tn(a_ref[...], b_ref[...])

    bb = _nbytes((tk, kt), a.dtype) + _nbytes((tk, tn), b.dtype) + _nbytes((kt, tn), F32)
    return pl.pallas_call(
        functools.partial(body), name=name, grid=(N // tn, K // kt, B, nk),
        in_specs=[pl.BlockSpec((None, tk, kt), lambda j, kk, bi, t: (bi, t, kk)),
                  pl.BlockSpec((None, tk, tn), lambda j, kk, bi, t: (bi, t, j))],
        out_specs=pl.BlockSpec((None, kt, tn), lambda j, kk, bi, t: (j, kk, 0)),
        out_shape=_sds((N // tn, K, tn), F32),
        compiler_params=_params(("arbitrary",) * 4, bb, 8 << 20),
    )(a, b)


def _mlp_bwd(R, dxo, xmid, ubf, ybf, mod, nw2, w1, w2):
    B, T = R.B, R.T

    def body(dxo_ref, x_ref, u_ref, y_ref, mod_ref, nw_ref, w1_hbm, w2_hbm,
             dxm_ref, du_ref, a_ref, dy_ref, dmod_ref, dnw_ref, w1_v, w2_v, sem):
        b, i = pl.program_id(0), pl.program_id(1)
        _load_once(jnp.logical_and(b == 0, i == 0), [(w1_hbm, w1_v), (w2_hbm, w2_v)], sem)
        dxo = dxo_ref[...]
        _, xn, r = _norm_mod(x_ref[...], nw_ref[...], _row(mod_ref, 3), _row(mod_ref, 4))
        dyb = (dxo * _row(mod_ref, 5)).astype(BF)
        dy_ref[...] = dyb
        dg2 = jnp.sum(dxo * y_ref[...].astype(F32), axis=0, keepdims=True)
        dh = jnp.zeros((TM, D), F32)
        for j in range(NDEV):
            rl = jnp.maximum(u_ref[:, FSH * j:FSH * (j + 1)].astype(F32), 0.0)
            a_ref[:, FSH * j:FSH * (j + 1)] = (rl * rl).astype(BF)
            du = (_dot_nt(dyb, w2_v[j]) * (2.0 * rl)).astype(BF)
            du_ref[:, FSH * j:FSH * (j + 1)] = du
            dh = dh + _dot_nt(du, w1_v[j])
        dx, dsh, dsc, dnw = _norm_mod_bwd(dh, xn, r, nw_ref[...], _row(mod_ref, 4))
        dxm_ref[...] = dxo + dx
        _acc_rows(dmod_ref, R.first_of_stream(i), {3: dsh, 4: dsc, 5: dg2})
        _acc_rows(dnw_ref, jnp.logical_and(b == 0, i == 0), {0: dnw})

    return R.call("mlp_bwd", body,
                  [(dxo, R.row(D)), (xmid, R.row(D)), (ubf, R.row(FF)), (ybf, R.row(D)), (mod, R.mod()), (nw2, R.const((1, D))),
                   (w1, ANY), (w2, ANY)],
                  [(_sds((B, T, D), F32), R.row(D)), (_sds((B, T, FF), BF), R.row(FF)), (_sds((B, T, FF), BF), R.row(FF)),
                   (_sds((B, T, D), BF), R.row(D)), (_sds((B, 2, 8, D), F32), R.mod()), (_sds((8, D), F32), R.const((8, D)))],
                  scratch=[pltpu.VMEM((NDEV, D, FSH), w1.dtype), pltpu.VMEM((NDEV, FSH, D), w2.dtype), pltpu.SemaphoreType.DMA((2,))],
                  extra=(2 * _nbytes((NDEV, D, FSH), BF)) + (8 << 20))


def _out_bwd(R, dxm, mod, mixbf, y2, xbc, z, o, dsk, snw, wout):
    B, T = R.B, R.T

    def body(dxm_ref, mod_ref, mix_ref, y2_ref, xbc_ref, z_ref, o_ref, dsk_ref, snw_ref, w_ref,
             dmix_ref, dz_ref, dyt_ref, dxsk_ref, do_ref, dd_ref, dpo_ref, dmod_ref, dvec_ref):
        b, i = pl.program_id(0), pl.program_id(1)
        dxm = dxm_ref[...]
        dmixb = (dxm * _row(mod_ref, 2)).astype(BF)
        dmix_ref[...] = dmixb
        dg1 = jnp.sum(dxm * mix_ref[...].astype(F32), axis=0, keepdims=True)
        dcat = _dot_nt(dmixb, w_ref[...])
        do_v = dcat[:, SSD_IN:SSD_IN + QW]
        do_ref[...] = do_v.astype(BF)
        doo = do_v * o_ref[...]
        for h in range(H):
            dd_ref[h] = jnp.sum(doo[:, HP * h:HP * (h + 1)], axis=1, keepdims=True)
        dpo_ref[...] = dcat[:, SSD_IN + QW:MIXW]
        dsn = dcat[:, 0:SSD_IN]
        ytot, zv, gz, r = _ssd_gate(y2_ref, xbc_ref, z_ref, dsk_ref)
        gh = gz * r
        dsnw = jnp.sum(dsn * gh, axis=0, keepdims=True)
        dgh = dsn * snw_ref[...]
        g0 = _group_mask()
        pr = dgh * gh
        half = SSD_IN // 2
        m0 = jnp.sum(jnp.where(g0, pr, 0.0), axis=1, keepdims=True) / half
        m1 = jnp.sum(jnp.where(g0, 0.0, pr), axis=1, keepdims=True) / half
        dgz = r * (dgh - gh * jnp.where(g0, m0, m1))
        dyt = dgz * _silu(zv)
        dz_ref[...] = (dgz * ytot * _dsilu(zv)).astype(BF)
        dyt_ref[...] = dyt
        dxsk_ref[...] = dyt * dsk_ref[...]
        ddsk = jnp.sum(dyt * xbc_ref[:, 0:SSD_IN], axis=0, keepdims=True)
        _acc_rows(dmod_ref, R.first_of_stream(i), {2: dg1})
        _acc_rows(dvec_ref, jnp.logical_and(b == 0, i == 0), {0: dsnw, 1: ddsk})

    return R.call("out_bwd", body,
                  [(dxm, R.row(D)), (mod, R.mod()), (mixbf, R.row(D)), (y2, R.row2(SSD_IN)), (xbc, R.row(XBC)), (z, R.row(SSD_IN)),
                   (o, R.row(QW)), (dsk, R.const((1, SSD_IN))), (snw, R.const((1, SSD_IN))), (wout, R.const((MIXW, D)))],
                  [(_sds((B, T, D), BF), R.row(D)), (_sds((B, T, SSD_IN), BF), R.row(SSD_IN)), (_sds((B, T, SSD_IN), F32), R.row(SSD_IN)),
                   (_sds((B, T, SSD_IN), F32), R.row(SSD_IN)), (_sds((B, T, QW), BF), R.row(QW)),
                   (_sds((B, H, T, 1), F32), pl.BlockSpec((None, H, TM, 1), lambda b, i: (b, 0, i, 0))),
                   (_sds((B, T, PD), F32), R.row(PD)),
                   (_sds((B, 2, 8, D), F32), R.mod()), (_sds((8, SSD_IN), F32), R.const((8, SSD_IN)))],
                  extra=8 << 20)


def _pool_bwd(R, dpo, u, wbd, scale):
    B, T = R.B, R.T

    def body(d_ref, dpv_ref, dnx_ref, u_ref, upv_ref, unx_ref, w_ref, sc_ref, du_ref, dw_ref, dsc_ref):
        b, i = pl.program_id(0), pl.program_id(1)
        ext_u = R.ext(i, upv_ref, u_ref[...], unx_ref)
        ext_d = R.ext(i, dpv_ref, d_ref[...], dnx_ref)
        dm, cnt, valid, lane = _pool_centred(R, i, ext_u)
        ddm = _dot_nt(ext_d * sc_ref[...], w_ref[...]) * valid
        e = ddm / cnt
        a2 = e + _shift_rows(e, 1)
        a4 = _shift_rows(a2, -1) + _shift_rows(a2, 1)
        a8 = _shift_rows(a4, -2) + _shift_rows(a4, 2)
        a16 = _shift_rows(a8, -4) + _shift_rows(a8, 4)
        du_ref[...] = (_lane_select(lane, a2, a4, a8, a16) - ddm)[8:8 + TM].astype(BF)
        dmc = dm[8:8 + TM]
        dpo_c = d_ref[...]
        first = jnp.logical_and(b == 0, i == 0)

        @pl.when(first)
        def _():
            dw_ref[...] = jnp.zeros(dw_ref.shape, F32)

        dw_ref[...] += _dot_tn(dmc, dpo_c * sc_ref[...])
        _acc_rows(dsc_ref, first, {0: jnp.sum(dpo_c * _dot(dmc, w_ref[...]), axis=0, keepdims=True)})

    return R.call("pool_bwd", body,
                  [(dpo, R.row(PD)), (dpo, R.prev8(PD)), (dpo, R.next8(PD)), (u, R.row(PD)), (u, R.prev8(PD)), (u, R.next8(PD)),
                   (wbd, R.const((PD, PD))), (scale, R.const((1, PD)))],
                  [(_sds((B, T, PD), BF), R.row(PD)), (_sds((PD, PD), F32), R.const((PD, PD))), (_sds((8, PD), F32), R.const((8, PD)))],
                  extra=8 << 20)


def _flash_bwd(B, T, qt, kt, vt, dob, lse, dd):
    nk, nq = T // TK, T // TQ

    def body(k_ref, v_ref, q_ref, do_ref, lse_ref, dd_ref, dq_ref, dk_ref, dv_ref):
        j = pl.program_id(2)

        @pl.when(j == 0)
        def _():
            dq_ref[...] = jnp.zeros(dq_ref.shape, F32)

        k = k_ref[...]
        v = v_ref[...]
        ctx_keys = jnp.where(j < CTX // TK, 1.0, 0.0)
        dk = jnp.zeros((TK, HP), F32)
        dv = jnp.zeros((TK, HP), F32)
        for i in range(nq):
            rows = slice(TQ * i, TQ * (i + 1))
            q = q_ref[rows, :]
            do = do_ref[rows, :]
            p = jnp.exp(_dot_nt(q, k) - lse_ref[rows, :])
            if i < CTX // TQ:
                p = p * ctx_keys
            ds = (p * (_dot_nt(do, v) - dd_ref[rows, :])).astype(BF)
            dv = dv + _dot_tn(p, do)
            dk = dk + _dot_tn(ds, q)
            dq_ref[rows, :] += _dot(ds, k)
        dk_ref[...] = dk
        dv_ref[...] = dv

    tspec = pl.BlockSpec((None, TK, HP), lambda b, h, j: (b, j, h))
    fspec = pl.BlockSpec((None, T, HP), lambda b, h, j: (b, 0, h))
    cspec = pl.BlockSpec((None, None, T, 1), lambda b, h, j: (b, h, 0, 0))
    bb = 2 * _nbytes((T, HP), BF) + _nbytes((T, HP), F32) + 4 * _nbytes((TK, HP), F32) + 2 * _nbytes((T, HP), F32)
    return pl.pallas_call(
        functools.partial(body), name="flash_bwd", grid=(B, H, nk),
        in_specs=[tspec, tspec, fspec, fspec, cspec, cspec],
        out_specs=[fspec, tspec, tspec],
        out_shape=[_sds((B, T, QW), F32)] * 3,
        compiler_params=_params(("arbitrary",) * 3, bb, 12 << 20),
    )(kt, vt, qt, dob, lse, dd)


def _mla_bwd(R, dqt, dkt, dvt, qa, kva, qnw, kvnw, wq, wk, wv, cos, sin):
    B, T = R.B, R.T
    scale = QK ** -0.5

    def body(dq_ref, dk_ref, dv_ref, qa_ref, kva_ref, qnw_ref, kvnw_ref, wq_ref, wk_ref, wv_ref, cos_ref, sin_ref,
             dqp_ref, dkb_ref, dvb_ref, dqa_ref, dkva_ref, dkr_ref, dnw_ref):
        b, i = pl.program_id(0), pl.program_id(1)
        cos1, sin1 = cos_ref[...], sin_ref[...]
        dq = dq_ref[...] * scale
        dqp = _per_head(lambda g: g * cos1 + _swap8(g * sin1), dq).astype(BF)
        dqp_ref[...] = dqp
        dkv = dk_ref[...]
        dkb = dkv.astype(BF)
        dkb_ref[...] = dkb
        dvb = dv_ref[...].astype(BF)
        dvb_ref[...] = dvb
        dkk = dkv[:, 0:HP]
        for h in range(1, H):
            dkk = dkk + dkv[:, HP * h:HP * (h + 1)]
        lane = _iota((TM, HP), 1)
        rope_lane = jnp.logical_and(lane >= NOPE, lane < NOPE + ROPE)
        dkr_ref[...] = jnp.where(rope_lane, dkk * cos1 + _swap8(dkk * sin1), 0.0).astype(BF)
        _, qh, qr = _rms(qa_ref[...], qnw_ref[...])
        _, kh, kr_ = _rms(kva_ref[...], kvnw_ref[...])
        dcq = _dot_nt(dqp, wq_ref[...])
        dckv = _dot_nt(dkb, wk_ref[...]) + _dot_nt(dvb, wv_ref[...])
        dqa, dqw = _rms_bwd(dcq, qh, qr, qnw_ref[...])
        dkva, dkw = _rms_bwd(dckv, kh, kr_, kvnw_ref[...])
        dqa_ref[...] = dqa.astype(BF)
        dkva_ref[...] = dkva.astype(BF)
        _acc_rows(dnw_ref, jnp.logical_and(b == 0, i == 0), {0: dqw, 1: dkw})

    tab = pl.BlockSpec((TM, HP), lambda b, i: (i, 0))
    return R.call("mla_bwd", body,
                  [(dqt, R.row(QW)), (dkt, R.row(QW)), (dvt, R.row(QW)), (qa, R.row(QL)), (kva, R.row(QL)),
                   (qnw, R.const((1, QL))), (kvnw, R.const((1, QL))), (wq, R.const((QL, QW))), (wk, R.const((QL, QW))),
                   (wv, R.const((QL, QW))), (cos, tab), (sin, tab)],
                  [(_sds((B, T, QW), BF), R.row(QW))] * 3 + [(_sds((B, T, QL), BF), R.row(QL))] * 2
                  + [(_sds((B, T, HP), BF), R.row(HP)), (_sds((8, QL), F32), R.const((8, QL)))], extra=8 << 20)


def _ssd_scan_bwd(B, T, dyt, xbc, dt2, alog2, hin):
    nc, nctc = T // CH, CTX // CH

    def body(dy_ref, xbc_ref, dt_ref, alog_ref, hin_ref, dxbc_ref, ddt_ref, da_ref, dhs):
        d, s = pl.program_id(1), pl.program_id(2)

        @pl.when(s == 0)
        def _():
            dhs[...] = jnp.zeros(dhs.shape, F32)
            da_ref[...] = jnp.zeros(da_ref.shape, F32)

        xbc_v = xbc_ref[...]
        dtv = dt_ref[...]
        dyv = dy_ref[...]
        a, adt, inc, inc_t, q = _scan_common(d, xbc_v, dtv, alog_ref)
        dq_all = jnp.zeros((CH, HP), F32)
        dqtot_all = jnp.zeros((1, HP), F32)
        ddt_x = jnp.zeros((CH, HP), F32)
        for g in range(2):
            bg = xbc_v[:, SSD_IN + NST * g:SSD_IN + NST * (g + 1)]
            cg = xbc_v[:, SSD_IN + 2 * NST + NST * g:SSD_IN + 2 * NST + NST * (g + 1)]
            gm = _dot_nt(cg, bg)
            gm_t = _dot_nt(bg, cg)
            dgm = jnp.zeros((CH, CH), F32)
            dbg = jnp.zeros((CH, NST), F32)
            dcg = jnp.zeros((CH, NST), F32)
            for r in range(3):
                h = 3 * g + r
                onehot, qc, qr, qtot, dt_h = _head_cols(h, adt, q, dtv, inc_t)
                lm = jnp.where(inc, jnp.exp(qc - qr), 0.0)
                xs_h = xbc_v[:, P * h:P * (h + 1)]
                xh = xs_h * dt_h
                sm = gm * lm
                dy_h = dyv[:, P * h:P * (h + 1)]
                hprev = hin_ref[P * h:P * (h + 1), :]
                dho = dhs[P * h:P * (h + 1), :]
                eq = jnp.exp(qc)
                etot = jnp.exp(qtot)
                dte = jnp.exp(qtot - qc)
                ds = _dot_nt(dy_h, xh)
                dx = _dot_tn(sm, dy_h)
                edy = eq * dy_h
                dq_col = jnp.sum(dy_h * (eq * _dot_nt(cg, hprev)), axis=1, keepdims=True)
                dcg = dcg + _dot(edy, hprev)
                dhin = _dot_tn(edy, cg) + etot * dho
                zs = _dot_nt(bg, dho)
                dx = dx + dte * zs
                w = dte * jnp.sum(xh * zs, axis=1, keepdims=True)
                dbg = dbg + _dot(xh * dte, dho)
                dqtot = jnp.sum(jnp.sum(hprev * dho, axis=1, keepdims=True), axis=0, keepdims=True) * etot \
                    + jnp.sum(w, axis=0, keepdims=True)
                dq_col = dq_col - w
                dgm = dgm + ds * lm
                lm_t = jnp.where(inc_t, jnp.exp(qr - qc), 0.0)
                e_t = _dot_nt(xh, dy_h) * gm_t * lm_t
                dq_col = dq_col + jnp.sum(ds * sm, axis=1, keepdims=True) - jnp.sum(e_t, axis=1, keepdims=True)
                dq_all = dq_all + dq_col * onehot
                dqtot_all = dqtot_all + dqtot * onehot
                dxbc_ref[:, P * h:P * (h + 1)] = dx * dt_h
                ddt_x = ddt_x + jnp.sum(dx * xs_h, axis=1, keepdims=True) * onehot
                dhs[P * h:P * (h + 1), :] = dhin
            dcg = dcg + _dot(dgm, bg)
            dbg = dbg + _dot_tn(dgm, cg)
            dxbc_ref[:, SSD_IN + NST * g:SSD_IN + NST * (g + 1)] = dbg
            dxbc_ref[:, SSD_IN + 2 * NST + NST * g:SSD_IN + 2 * NST + NST * (g + 1)] = dcg
        dadt = _dot01(inc_t, dq_all) + dqtot_all
        ddt_ref[...] = ddt_x + dadt * a
        da_ref[...] += jnp.where(_iota((8, HP), 0) == 0, jnp.sum(dadt * dtv, axis=0, keepdims=True), 0.0)

    cidx = lambda d, s: _chunk_index(d, nc - 1 - s, nc, nctc)
    bb = 2 * _nbytes((CH, XBC), F32) + 2 * _nbytes((CH, HP), F32) + _nbytes((CH, SSD_IN), F32) + _nbytes((SSD_IN, NST), F32)
    return pl.pallas_call(
        functools.partial(body), name="ssd_scan_bwd", grid=(B, 2, nc),
        in_specs=[pl.BlockSpec((None, CH, SSD_IN), lambda b, d, s: (b, cidx(d, s), 0)),
                  pl.BlockSpec((None, CH, XBC), lambda b, d, s: (b, cidx(d, s), 0)),
                  pl.BlockSpec((None, None, CH, HP), lambda b, d, s: (b, d, cidx(d, s), 0)),
                  pl.BlockSpec((None, 8, HP), lambda b, d, s: (d, 0, 0)),
                  pl.BlockSpec((None, None, None, SSD_IN, NST), lambda b, d, s: (b, d, cidx(d, s), 0, 0))],
        out_specs=[pl.BlockSpec((None, None, CH, XBC), lambda b, d, s: (b, d, cidx(d, s), 0)),
                   pl.BlockSpec((None, None, CH, HP), lambda b, d, s: (b, d, cidx(d, s), 0)),
                   pl.BlockSpec((None, None, 8, HP), lambda b, d, s: (b, d, 0, 0))],
        out_shape=[_sds((B, 2, T, XBC), F32), _sds((B, 2, T, HP), F32), _sds((B, 2, 8, HP), F32)],
        scratch_shapes=[pltpu.VMEM((SSD_IN, NST), F32)],
        compiler_params=_params(("arbitrary",) * 3, bb, 8 << 20),
    )(dyt, xbc, dt2, alog2, hin)


def _ssd_prep_bwd(R, dxbc2, dxsk, ddt2, xbc_raw, dt_raw, conv_w8, conv_b, dtb):
    B, T = R.B, R.T

    def body(dx2_ref, dsk_ref, ddt_ref, raw_ref, pv_ref, nx_ref, dtr_ref, w_ref, b_ref, dtb_ref,
             dpre_ref, ddtr_ref, dcw_ref, dvec_ref, ddtb_ref):
        b, i = pl.program_id(0), pl.program_id(1)
        ext = R.ext(i, pv_ref, raw_ref[...], nx_ref)
        pre = _conv_pre(ext, w_ref, b_ref)
        dxbc = dx2_ref[0] + dx2_ref[1]
        lane = _iota((TM, XBC), 1)
        skip = jnp.concatenate([dsk_ref[...], jnp.zeros((TM, XBC - SSD_IN), F32)], axis=1)
        dpre = (dxbc + skip) * _dsilu(pre)
        dpre_ref[...] = dpre
        first = jnp.logical_and(b == 0, i == 0)
        taps = {k: jnp.sum(dpre * _shift_rows(ext, k - 1)[8:8 + TM], axis=0, keepdims=True) for k in range(4)}
        _acc_rows(dcw_ref, first, taps)
        _acc_rows(dvec_ref, first, {0: jnp.sum(dpre, axis=0, keepdims=True)})
        ddt = ddt_ref[0] + pltpu.roll(ddt_ref[1], H, axis=1)
        ddtr = ddt * _sigmoid(dtr_ref[...] + dtb_ref[...])
        ddtr = jnp.where(_iota((TM, HP), 1) < 2 * H, ddtr, 0.0)
        ddtr_ref[...] = ddtr.astype(BF)
        _acc_rows(ddtb_ref, first, {0: jnp.sum(ddtr, axis=0, keepdims=True)})
        del lane

    return R.call("ssd_prep_bwd", body,
                  [(dxbc2, R.row2(XBC)), (dxsk, R.row(SSD_IN)), (ddt2, R.row2(HP)), (xbc_raw, R.row(XBC)), (xbc_raw, R.prev8(XBC)),
                   (xbc_raw, R.next8(XBC)), (dt_raw, R.row(HP)), (conv_w8, R.const((8, XBC))), (conv_b, R.const((1, XBC))),
                   (dtb, R.const((1, HP)))],
                  [(_sds((B, T, XBC), F32), R.row(XBC)), (_sds((B, T, HP), BF), R.row(HP)), (_sds((8, XBC), F32), R.const((8, XBC))),
                   (_sds((8, XBC), F32), R.const((8, XBC))), (_sds((8, HP), F32), R.const((8, HP)))], extra=16 << 20)


def _in_bwd(R, dxm, x, mod, nw1, dz, dpre, dqa, dkva, dpool, dkr, ddtr, conv_w8, w_arr):
    B, T = R.B, R.T

    def body(dxm_ref, x_ref, mod_ref, nw_ref, dz_ref, dp_ref, dpp_ref, dpn_ref, dqa_ref, dkva_ref, dpool_ref, dkr_ref, ddt_ref,
             cw_ref, w_ref, dx_ref, dproj_ref, dmod_ref, dnw_ref):
        b, i = pl.program_id(0), pl.program_id(1)
        ext = R.ext(i, dpp_ref, dp_ref[...], dpn_ref)
        draw = (_row(cw_ref, 0) * _shift_rows(ext, 1)[8:8 + TM] + _row(cw_ref, 1) * ext[8:8 + TM]
                + _row(cw_ref, 2) * _shift_rows(ext, -1)[8:8 + TM] + _row(cw_ref, 3) * _shift_rows(ext, -2)[8:8 + TM])
        dproj_ref[:, A_Z:A_XBC] = dz_ref[...]
        dproj_ref[:, A_XBC:A_QA] = draw.astype(BF)
        dproj_ref[:, A_QA:A_KVA] = dqa_ref[...]
        dproj_ref[:, A_KVA:A_POOL] = dkva_ref[...]
        dproj_ref[:, A_POOL:A_KR] = dpool_ref[...]
        dproj_ref[:, A_KR:A_DT] = dkr_ref[...]
        dproj_ref[:, A_DT:PC] = ddt_ref[...]
        dh = _dot_nt(dproj_ref[...], w_ref[...])
        _, xn, r = _norm_mod(x_ref[...], nw_ref[...], _row(mod_ref, 0), _row(mod_ref, 1))
        dx, dsh, dsc, dnw = _norm_mod_bwd(dh, xn, r, nw_ref[...], _row(mod_ref, 1))
        dx_ref[...] = dxm_ref[...] + dx
        _acc_rows(dmod_ref, R.first_of_stream(i), {0: dsh, 1: dsc})
        _acc_rows(dnw_ref, jnp.logical_and(b == 0, i == 0), {0: dnw})

    return R.call("in_bwd", body,
                  [(dxm, R.row(D)), (x, R.row(D)), (mod, R.mod()), (nw1, R.const((1, D))), (dz, R.row(SSD_IN)), (dpre, R.row(XBC)),
                   (dpre, R.prev8(XBC)), (dpre, R.next8(XBC)), (dqa, R.row(QL)), (dkva, R.row(QL)), (dpool, R.row(PD)), (dkr, R.row(HP)),
                   (ddtr, R.row(HP)), (conv_w8, R.const((8, XBC))), (w_arr, R.const((D, PC)))],
                  [(_sds((B, T, D), F32), R.row(D)), (_sds((B, T, PC), BF), R.row(PC)), (_sds((B, 2, 8, D), F32), R.mod()),
                   (_sds((8, D), F32), R.const((8, D)))], extra=12 << 20)


def _adaln_fwd(cs, mod_w):
    L, _, C = mod_w.shape

    def body(c_ref, w_ref, o_ref):
        s = _silu(c_ref[...]).astype(BF)
        for l in range(L):
            o_ref[l] = jnp.dot(s, w_ref[l].astype(BF), preferred_element_type=F32)

    return pl.pallas_call(functools.partial(body), name="adaln_fwd", out_shape=_sds((L, 24, C), F32),
                          compiler_params=_params(None, _nbytes(mod_w.shape, F32) + _nbytes((L, 24, C), F32), 8 << 20))(cs, mod_w)


def _adaln_bwd(cs, dm, mod_w):
    L, _, C = mod_w.shape

    def body(c_ref, dm_ref, w_ref, gw_ref, gc_ref):
        c = c_ref[...]
        s = _silu(c).astype(BF)
        acc = jnp.zeros((24, D), F32)
        for l in range(L):
            dmb = dm_ref[l].astype(BF)
            gw_ref[l] = _dot_tn(s, dmb)
            acc = acc + _dot_nt(dmb, w_ref[l])
        gc_ref[...] = acc * _dsilu(c)

    return pl.pallas_call(functools.partial(body), name="adaln_bwd", out_shape=[_sds((L, D, C), F32), _sds((24, D), F32)],
                          compiler_params=_params(None, 2 * _nbytes(mod_w.shape, F32), 8 << 20))(cs, dm, mod_w)


def _adamw(name, parts, w, m, v, rt):
    Pn, Rr, C = parts.shape
    c1 = 1.0 - ADAM_B1 ** ADAM_STEP
    c2 = 1.0 - ADAM_B2 ** ADAM_STEP

    def body(p_ref, w_ref, m_ref, v_ref, g_ref, d_ref, nm_ref, nv_ref):
        g = p_ref[0].astype(F32)
        for k in range(1, Pn):
            g = g + p_ref[k].astype(F32)
        mn = ADAM_B1 * m_ref[...] + (1.0 - ADAM_B1) * g
        vn = ADAM_B2 * v_ref[...] + (1.0 - ADAM_B2) * jnp.square(g)
        g_ref[...] = g
        nm_ref[...] = mn
        nv_ref[...] = vn
        d_ref[...] = -ADAM_LR * ((mn / c1) / (jnp.sqrt(vn / c2) + ADAM_EPS) + ADAM_WD * w_ref[...])

    spec = pl.BlockSpec((rt, C), lambda i: (i, 0))
    bb = Pn * _nbytes((rt, C), parts.dtype) + 7 * _nbytes((rt, C), F32)
    return pl.pallas_call(
        functools.partial(body), name=name, grid=(Rr // rt,),
        in_specs=[pl.BlockSpec((Pn, rt, C), lambda i: (0, i, 0)), spec, spec, spec],
        out_specs=[spec] * 4, out_shape=[_sds((Rr, C), F32)] * 4,
        compiler_params=_params(("arbitrary",), bb, 4 << 20),
    )(parts, w, m, v)


MESH = pl.DeviceIdType.MESH


def _my_pos():
    return lax.axis_index("x"), lax.axis_index("y"), lax.axis_index("c")


def _dev_index(x, y, c):
    return 4 * x + 2 * y + c


def _all_gather(name, shards):
    n = len(shards)

    def body(*refs):
        ins, outs = refs[:n], refs[n:2 * n]
        send_sems, recv_sems, local_sem = refs[2 * n:]
        x, y, c = _my_pos()
        me, sibling = (x, y, c), (x, y, 1 - c)
        chips = [(1 - x, y), (x, 1 - y), (1 - x, 1 - y)]

        def copy(t, k, block, to, src=None):
            slot = outs[t].at[_dev_index(*block)]
            return pltpu.make_async_remote_copy(
                src_ref=slot if src is None else src, dst_ref=slot,
                send_sem=send_sems.at[t, k], recv_sem=recv_sems.at[t, k], device_id=to, device_id_type=MESH)

        mine = [pltpu.make_async_copy(ins[t], outs[t].at[_dev_index(*me)], local_sem.at[t]) for t in range(n)]
        for cp in mine:
            cp.start()
        first = []
        for t in range(n):
            first.append(copy(t, 0, me, sibling, src=ins[t]))
            first += [copy(t, 1 + j, me, (*chip, c), src=ins[t]) for j, chip in enumerate(chips)]
        for cp in first:
            cp.start()
        passed = []
        for j, chip in enumerate(chips):
            for t in range(n):
                copy(t, 1 + j, (*chip, c), me).wait_recv()
                cp = copy(t, 4 + j, (*chip, c), sibling)
                cp.start()
                passed.append(cp)
        for t in range(n):
            copy(t, 0, sibling, me).wait_recv()
            for j, chip in enumerate(chips):
                copy(t, 4 + j, (*chip, 1 - c), me).wait_recv()
        for cp in first + passed:
            cp.wait_send()
        for cp in mine:
            cp.wait()

    return pl.pallas_call(
        functools.partial(body), name=name,
        in_specs=[ANY] * n, out_specs=[ANY] * n,
        out_shape=[_sds((NDEV,) + s.shape, s.dtype) for s in shards],
        scratch_shapes=[pltpu.SemaphoreType.DMA((n, 7)), pltpu.SemaphoreType.DMA((n, 7)), pltpu.SemaphoreType.DMA((n,))],
    )(*shards)


def _all_to_all(name, parts):
    n = len(parts)

    def body(*refs):
        ins, outs = refs[:n], refs[n:2 * n]
        send_sems, recv_sems, local_sem = refs[2 * n:]
        x, y, c = _my_pos()
        me = _dev_index(x, y, c)
        peers = [(x ^ ((k >> 2) & 1), y ^ ((k >> 1) & 1), c ^ (k & 1)) for k in range(1, NDEV)]
        mine = [pltpu.make_async_copy(ins[t].at[me], outs[t].at[me], local_sem.at[t]) for t in range(n)]
        for cp in mine:
            cp.start()
        sends = []
        for t in range(n):
            for k, peer in enumerate(peers):
                cp = pltpu.make_async_remote_copy(
                    src_ref=ins[t].at[_dev_index(*peer)], dst_ref=outs[t].at[me],
                    send_sem=send_sems.at[t, k], recv_sem=recv_sems.at[t, k], device_id=peer, device_id_type=MESH)
                cp.start()
                sends.append(cp)
        for t in range(n):
            for k, peer in enumerate(peers):
                slot = outs[t].at[_dev_index(*peer)]
                pltpu.make_async_remote_copy(
                    src_ref=slot, dst_ref=slot, send_sem=send_sems.at[t, k], recv_sem=recv_sems.at[t, k],
                    device_id=peer, device_id_type=MESH).wait_recv()
        for cp in sends:
            cp.wait_send()
        for cp in mine:
            cp.wait()

    return pl.pallas_call(
        functools.partial(body), name=name,
        in_specs=[ANY] * n, out_specs=[ANY] * n,
        out_shape=[_sds(p.shape, p.dtype) for p in parts],
        scratch_shapes=[pltpu.SemaphoreType.DMA((n, 7)), pltpu.SemaphoreType.DMA((n, 7)), pltpu.SemaphoreType.DMA((n,))],
    )(*parts)


def _arrange_w_in(w):
    z = lambda n: jnp.zeros((w.shape[0], n), w.dtype)
    return jnp.concatenate([w[:, 0:1280], w[:, 1292:1548], w[:, 1548:1804], w[:, 1836:2092],
                            z(64), w[:, 1804:1836], z(32), w[:, 1280:1292], z(HP - 2 * H)], axis=1)


def _unarrange_w_in(g):
    return jnp.concatenate([g[:, 0:1280], g[:, A_DT:A_DT + 2 * H], g[:, A_QA:A_KVA], g[:, A_KVA:A_POOL],
                            g[:, A_KR + NOPE:A_KR + NOPE + ROPE], g[:, A_POOL:A_KR]], axis=1)


def _pad_heads(w, width):
    k = w.shape[0]
    return jnp.pad(w.reshape(k, H, width), ((0, 0), (0, 0), (0, HP - width))).reshape(k, H * HP)


def _unpad_heads(g, width):
    k = g.shape[0]
    return g.reshape(k, H, HP)[:, :, :width].reshape(k, H * width)


def _arrange_w_out(w):
    att = jnp.pad(w[SSD_IN:2 * SSD_IN].reshape(H, VH, D), ((0, 0), (0, HP - VH), (0, 0))).reshape(QW, D)
    return jnp.concatenate([w[0:SSD_IN], att, w[2 * SSD_IN:]], axis=0)


def _unarrange_w_out(g):
    att = g[SSD_IN:SSD_IN + QW].reshape(H, HP, D)[:, :VH].reshape(SSD_IN, D)
    return jnp.concatenate([g[0:SSD_IN], att, g[SSD_IN + QW:]], axis=0)


def _rope_tables(T):
    n = T - CTX
    t = jnp.arange(n, dtype=F32)
    row, col = jnp.floor(t / GRID_W), t % GRID_W
    pairs = ROPE // 4
    inv = ROPE_THETA ** (-jnp.arange(pairs, dtype=F32) / pairs)
    ar, ac = row[:, None] * inv, col[:, None] * inv
    cos = jnp.concatenate([jnp.cos(ar)] * 2 + [jnp.cos(ac)] * 2, axis=1)
    sin = jnp.concatenate([-jnp.sin(ar), jnp.sin(ar), -jnp.sin(ac), jnp.sin(ac)], axis=1)
    ones, zeros = jnp.ones((n, NOPE), F32), jnp.zeros((n, NOPE), F32)
    cos = jnp.concatenate([ones, cos, ones[:, :HP - QK]], axis=1)
    sin = jnp.concatenate([zeros, sin, zeros[:, :HP - QK]], axis=1)
    return (jnp.concatenate([jnp.ones((CTX, HP), F32), cos], axis=0),
            jnp.concatenate([jnp.zeros((CTX, HP), F32), sin], axis=0))


def _lane_pad(v, n):
    return jnp.pad(v, (0, n - v.shape[0]))[None, :]


def _layer_weights(w_in, w_out, w_q_b, w_kv_b, conv_w, pool_w):
    kv = w_kv_b.reshape(QL, H, NOPE + VH)
    wbd = jnp.zeros((PD, PD), F32)
    for g in range(4):
        wbd = lax.dynamic_update_slice(wbd, pool_w[g], (64 * g, 64 * g))
    return dict(
        w_in=_arrange_w_in(w_in).astype(BF), w_out=_arrange_w_out(w_out).astype(BF),
        wq=_pad_heads(w_q_b, QK).astype(BF),
        wk=_pad_heads(kv[:, :, :NOPE].reshape(QL, H * NOPE), NOPE).astype(BF),
        wv=_pad_heads(kv[:, :, NOPE:].reshape(QL, H * VH), VH).astype(BF),
        conv_w8=jnp.pad(conv_w, ((0, 4), (0, 0))), wbd=wbd)


def _layer_fwd(R, x, mod, lw, sp, w1, w2, cos, sin):
    B, T = R.B, R.T
    z, xbc_raw, qa, kva, pool_in, kr, dt_raw, h1 = _in_proj(R, x, mod, sp["nw1"], lw["w_in"])
    xbc, dt2 = _ssd_prep(R, xbc_raw, dt_raw, lw["conv_w8"], sp["conv_b"], sp["dtb"])
    y2, hin = _ssd_scan(B, T, xbc, dt2, sp["alog2"])
    qt, kt, vt, cq, ckv = _mla_prep(R, qa, kva, kr, sp["qnw"], sp["kvnw"], lw["wq"], lw["wk"], lw["wv"], cos, sin)
    o, lse = _flash_fwd(B, T, qt, kt, vt)
    pool = _pool_fwd(R, pool_in, lw["wbd"], sp["pscale"])[0]
    xmid, cat, mixbf = _out_proj(R, x, mod, y2, xbc, z, o, pool, sp["dsk"], sp["snw"], lw["w_out"])
    xo, h2, ubf, ybf = _mlp_fwd(R, xmid, mod, sp["nw2"], w1, w2)
    saved = dict(x=x, z=z, xbc_raw=xbc_raw, qa=qa, kva=kva, pool_in=pool_in, dt_raw=dt_raw, h1=h1, xbc=xbc, dt2=dt2,
                 y2=y2, hin=hin, qt=qt, kt=kt, vt=vt, cq=cq, ckv=ckv, o=o, lse=lse, cat=cat, mixbf=mixbf, xmid=xmid,
                 h2=h2, ubf=ubf, ybf=ybf)
    return xo, saved


def _layer_bwd(R, dxo, sv, mod, lw, sp, w1, w2, cos, sin):
    B, T = R.B, R.T
    nq = T // TQ
    dxm, du, abf, dyb, dmod_a, dnw2 = _mlp_bwd(R, dxo, sv["xmid"], sv["ubf"], sv["ybf"], mod, sp["nw2"], w1, w2)
    g_w1 = _tn_matmul("dw_mlp1", sv["h2"], du, FSH)
    g_w2 = _tn_matmul("dw_mlp2", abf, dyb, D)[0].reshape(NDEV, FSH, D)
    dmix, dz, dyt, dxsk, dob, dd, dpo, dmod_b, dvec_o = _out_bwd(R, dxm, mod, sv["mixbf"], sv["y2"], sv["xbc"], sv["z"], sv["o"],
                                                                 sp["dsk"], sp["snw"], lw["w_out"])
    g_wout = _unarrange_w_out(_tn_matmul("dw_out", sv["cat"], dmix, D)[0])
    dpool_in, g_wbd, dpsc = _pool_bwd(R, dpo, sv["pool_in"], lw["wbd"], sp["pscale"])
    dqt, dkt, dvt = _flash_bwd(B, T, sv["qt"], sv["kt"], sv["vt"], dob, sv["lse"], dd)
    dqp, dkb, dvb, dqa, dkva, dkr, dnw_qk = _mla_bwd(R, dqt, dkt, dvt, sv["qa"], sv["kva"], sp["qnw"], sp["kvnw"],
                                                     lw["wq"], lw["wk"], lw["wv"], cos, sin)
    g_wq = _unpad_heads(_tn_matmul("dw_q", sv["cq"], dqp, QW)[0], QK)
    g_wk = _unpad_heads(_tn_matmul("dw_k", sv["ckv"], dkb, QW)[0], NOPE).reshape(QL, H, NOPE)
    g_wv = _unpad_heads(_tn_matmul("dw_v", sv["ckv"], dvb, QW)[0], VH).reshape(QL, H, VH)
    g_wkv = jnp.concatenate([g_wk, g_wv], axis=2).reshape(QL, H * (NOPE + VH))
    dxbc2, ddt2, da = _ssd_scan_bwd(B, T, dyt, sv["xbc"], sv["dt2"], sp["alog2"], sv["hin"])
    dpre, ddtr, dcw, dcb, ddtb = _ssd_prep_bwd(R, dxbc2, dxsk, ddt2, sv["xbc_raw"], sv["dt_raw"], lw["conv_w8"],
                                                sp["conv_b"], sp["dtb"])
    dx, dproj, dmod_c, dnw1 = _in_bwd(R, dxm, sv["x"], mod, sp["nw1"], dz, dpre, dqa, dkva, dpool_in, dkr, ddtr,
                                      lw["conv_w8"], lw["w_in"])
    g_win = _unarrange_w_in(jnp.concatenate(list(_tn_matmul("dw_in", sv["h1"], dproj, PC // 3)), axis=1))
    a2 = -jnp.exp(sp["alog2"][:, 0, :H])
    small = dict(
        norm1_w=dnw1[0], norm2_w=dnw2[0], conv_w=dcw[0:4], conv_b=dcb[0], dt_bias=ddtb[0, :2 * H].reshape(2, H),
        a_log=jnp.sum(da[:, :, 0, :H], axis=0) * a2, ssd_d=jnp.sum(dvec_o[1].reshape(H, P), axis=1), ssd_norm_w=dvec_o[0],
        q_a_norm_w=dnw_qk[0], kv_a_norm_w=dnw_qk[1],
        pool_w=jnp.stack([g_wbd[64 * g:64 * (g + 1), 64 * g:64 * (g + 1)] for g in range(4)]), pool_scale=dpsc[0])
    big = dict(w_in=g_win, w_out=g_wout, w_q_b=g_wq, w_kv_b=g_wkv, w_mlp1=g_w1, w_mlp2=g_w2)
    return dx, big, small, dmod_a + dmod_b + dmod_c


def _small_params(l, norm1_w, norm2_w, conv_b, dt_bias, a_log, ssd_d, ssd_norm_w, q_a_norm_w, kv_a_norm_w, pool_scale):
    alog2 = jnp.broadcast_to(jnp.pad(a_log[l], ((0, 0), (0, HP - H)))[:, None, :], (2, 8, HP))
    return dict(nw1=norm1_w[l][None], nw2=norm2_w[l][None], conv_b=conv_b[l][None],
                dtb=_lane_pad(dt_bias[l].reshape(2 * H), HP), alog2=alog2,
                dsk=jnp.repeat(ssd_d[l], P)[None], snw=ssd_norm_w[l][None], qnw=q_a_norm_w[l][None],
                kvnw=kv_a_norm_w[l][None], pscale=pool_scale[l][None])


SMALL_NAMES = ["mod_b", "norm1_w", "norm2_w", "conv_b", "dt_bias", "a_log", "ssd_d", "ssd_norm_w", "q_a_norm_w",
               "kv_a_norm_w", "pool_w", "pool_scale", "final_norm_w"]


def _pack(arrs):
    rows = []
    for a in arrs:
        f = a.reshape(-1).astype(F32)
        n = -(-f.shape[0] // HP) * HP
        rows.append(jnp.pad(f, (0, n - f.shape[0])).reshape(-1, HP))
    out = jnp.concatenate(rows, axis=0)
    pad = (-out.shape[0]) % 8
    return jnp.pad(out, ((0, pad), (0, 0)))


def _unpack(pack, like):
    outs, r = [], 0
    for a in like:
        n = math.prod(a.shape)
        nr = -(-n // HP)
        outs.append(pack[r:r + nr].reshape(-1)[:n].reshape(a.shape))
        r += nr
    return outs


def _local_step(x, ctx, target, mods, full, small_w):
    B, N = x.shape[0], x.shape[1]
    T = CTX + N
    R = _Rows(B, T)
    cos, sin = _rope_tables(T)
    xu = jnp.concatenate([ctx, x], axis=1)
    L = len(mods)
    lws, sps, saves = [], [], []
    for l in range(L):
        f = full[l]
        lws.append(_layer_weights(f["w_in"], f["w_out"], f["w_q_b"], f["w_kv_b"], f["conv_w"], small_w["pool_w"][l]))
        sps.append(_small_params(l, *[small_w[k] for k in ["norm1_w", "norm2_w", "conv_b", "dt_bias", "a_log", "ssd_d",
                                                          "ssd_norm_w", "q_a_norm_w", "kv_a_norm_w", "pool_scale"]]))
        xu, sv = _layer_fwd(R, xu, mods[l], lws[l], sps[l], f["w_mlp1"], f["w_mlp2"], cos, sin)
        saves.append(sv)
    dx, loss8, dfnw = _loss_head(R, xu, target, small_w["final_norm_w"][None])
    bigs, smalls, dmods = [None] * L, [None] * L, [None] * L
    for l in reversed(range(L)):
        f = full[l]
        dx, bigs[l], smalls[l], dmods[l] = _layer_bwd(R, dx, saves[l], mods[l], lws[l], sps[l], f["w_mlp1"], f["w_mlp2"], cos, sin)
    return loss8[0, 0], dx[:, CTX:], bigs, smalls, dfnw[0], dmods


def kernel(x, c, ctx, c_ctx, mod_w, mod_b, norm1_w, norm2_w, w_in, conv_w, conv_b, dt_bias, a_log, ssd_d, ssd_norm_w, q_a_norm_w, w_q_b, kv_a_norm_w, w_kv_b, pool_w, pool_scale, w_out, w_mlp1, w_mlp2, final_norm_w, loss_target, m_c_ctx, m_mod_w, m_mod_b, m_norm1_w, m_norm2_w, m_w_in, m_conv_w, m_conv_b, m_dt_bias, m_a_log, m_ssd_d, m_ssd_norm_w, m_q_a_norm_w, m_w_q_b, m_kv_a_norm_w, m_w_kv_b, m_pool_w, m_pool_scale, m_w_out, m_w_mlp1, m_w_mlp2, m_final_norm_w, v_c_ctx, v_mod_w, v_mod_b, v_norm1_w, v_norm2_w, v_w_in, v_conv_w, v_conv_b, v_dt_bias, v_a_log, v_ssd_d, v_ssd_norm_w, v_q_a_norm_w, v_w_q_b, v_kv_a_norm_w, v_w_kv_b, v_pool_w, v_pool_scale, v_w_out, v_w_mlp1, v_w_mlp2, v_final_norm_w):
    args = dict(locals())
    B = x.shape[0]
    L = mod_w.shape[0]
    me = _dev_index(*_my_pos())
    CS = mod_w.shape[2]

    c_all, convw_all, win_g, wout_g, wq_g, wkv_g, w1_g, w2_g = _all_gather(
        "gather_weights",
        [c, conv_w, w_in.astype(BF), w_out.astype(BF), w_q_b.astype(BF), w_kv_b.astype(BF), w_mlp1.astype(BF), w_mlp2.astype(BF)])
    cs = jnp.concatenate([c_all.reshape(NDEV * B, D), c_ctx[None], jnp.zeros((24 - NDEV * B - 1, D), F32)], axis=0)
    m_loc = _adaln_fwd(cs, mod_w)
    m_all = _all_gather("gather_mod", [m_loc])[0]
    m_full = jnp.moveaxis(m_all, 0, 2).reshape(L, 24, NDEV * CS) + mod_b[:, None, :]
    mods = []
    for l in range(L):
        ex = lax.dynamic_slice(m_full[l], (me * B, 0), (B, 6 * D)).reshape(B, 6, D)
        cc = jnp.broadcast_to(m_full[l, NDEV * B].reshape(1, 6, D), (B, 6, D))
        mods.append(jnp.pad(jnp.stack([cc, ex], axis=1), ((0, 0), (0, 0), (0, 2), (0, 0))))
    full = []
    for l in range(L):
        full.append(dict(
            w_in=win_g[:, l].reshape(D, IN_COLS), w_out=wout_g[:, l].reshape(D, D),
            w_q_b=jnp.moveaxis(wq_g[:, l], 0, 1).reshape(QL, H * QK),
            w_kv_b=jnp.moveaxis(wkv_g[:, l], 0, 1).reshape(QL, H * (NOPE + VH)),
            conv_w=jnp.moveaxis(convw_all[:, l], 0, 1).reshape(4, XBC),
            w_mlp1=w1_g[:, l], w_mlp2=w2_g[:, l]))
    small_w = {k: args[k] for k in SMALL_NAMES if k != "mod_b"}

    loss_part, grad_x, bigs, smalls, dfnw, dmods = _local_step(x, ctx, loss_target, mods, full, small_w)
    loss = lax.psum(loss_part, ("x", "y", "c"))

    dm_ex = jnp.stack([dmods[l][:, 1, :6].reshape(B, 6 * D) for l in range(L)])
    dm_cc = jnp.stack([jnp.sum(dmods[l][:, 0, :6], axis=0).reshape(6 * D) for l in range(L)])
    small_parts = dict(
        mod_b=jnp.sum(dm_ex, axis=1) + dm_cc,
        **{k: jnp.stack([smalls[l][k] for l in range(L)]) for k in SMALL_NAMES[1:-1]},
        final_norm_w=dfnw, conv_w=jnp.stack([smalls[l]["conv_w"] for l in range(L)]), dm_cc=dm_cc)
    adam_grads = [small_parts[k] for k in SMALL_NAMES]
    extras = [small_parts["conv_w"], dm_cc, dm_ex]
    pack = jnp.concatenate([_pack(adam_grads), _pack(extras)], axis=0)
    pack_all = _all_gather("gather_small_grads", [pack])[0]
    wpack = _pack([args[k] for k in SMALL_NAMES])
    mpack = _pack([args["m_" + k] for k in SMALL_NAMES])
    vpack = _pack([args["v_" + k] for k in SMALL_NAMES])
    n_adam = wpack.shape[0]
    res_small = _adamw("adamw_small", pack_all[:, :n_adam], wpack, mpack, vpack, n_adam)
    small_out = [_unpack(r, [args[k] for k in SMALL_NAMES]) for r in res_small]
    ext_all = pack_all[:, n_adam:]
    ext_sum = ext_all[0]
    for k in range(1, NDEV):
        ext_sum = ext_sum + ext_all[k]
    g_conv_full, dm_cc_tot, _ = _unpack(ext_sum, extras)
    dm_ex_all = jnp.stack([_unpack(ext_all[k], extras)[2] for k in range(NDEV)], axis=1)
    dm_rows = jnp.concatenate([dm_ex_all.reshape(L, NDEV * B, 6 * D), dm_cc_tot[:, None, :],
                               jnp.zeros((L, 24 - NDEV * B - 1, 6 * D), F32)], axis=1)
    dm_loc = lax.dynamic_slice(dm_rows, (0, 0, me * CS), (L, 24, CS))
    g_modw, gc_part = _adaln_bwd(cs, dm_loc, mod_w)
    gc_all = _all_gather("gather_cctx_grad", [gc_part[NDEV * B:NDEV * B + 8]])[0]
    cpad = lambda a: jnp.pad(a[None], ((0, 7), (0, 0)))
    res_cc = _adamw("adamw_cctx", gc_all, cpad(c_ctx), cpad(m_c_ctx), cpad(v_c_ctx), 8)
    cc_out = [r[0] for r in res_cc]

    def stack_dev(name, fn):
        return jnp.stack([fn(bigs[l][name]) for l in range(L)], axis=1).astype(BF)

    sends = [
        stack_dev("w_in", lambda g: g.reshape(NDEV, D // NDEV, IN_COLS)),
        stack_dev("w_out", lambda g: g.reshape(NDEV, D // NDEV, D)),
        stack_dev("w_q_b", lambda g: jnp.moveaxis(g.reshape(QL, NDEV, -1), 1, 0)),
        stack_dev("w_kv_b", lambda g: jnp.moveaxis(g.reshape(QL, NDEV, -1), 1, 0)),
        stack_dev("w_mlp1", lambda g: g), stack_dev("w_mlp2", lambda g: g)]
    recvs = _all_to_all("exchange_grads", sends)
    big_names = ["w_in", "w_out", "w_q_b", "w_kv_b", "w_mlp1", "w_mlp2"]
    big_out = {}
    for name, rv in zip(big_names, recvs):
        w = args[name]
        Rr, C = math.prod(w.shape[:-1]), w.shape[-1]
        rt = Rr if Rr * C <= (1 << 18) else Rr // 8
        res = _adamw("adamw_" + name, rv.reshape(NDEV, Rr, C), w.reshape(Rr, C), args["m_" + name].reshape(Rr, C),
                     args["v_" + name].reshape(Rr, C), rt)
        big_out[name] = [r.reshape(w.shape) for r in res]
    res = _adamw("adamw_mod_w", g_modw.reshape(1, L * D, CS), mod_w.reshape(L * D, CS), m_mod_w.reshape(L * D, CS),
                 v_mod_w.reshape(L * D, CS), L * D // 8)
    big_out["mod_w"] = [r.reshape(mod_w.shape) for r in res]
    CW = conv_w.shape[2]
    g_conv = lax.dynamic_slice(g_conv_full, (0, 0, me * CW), (L, 4, CW))
    res = _adamw("adamw_conv_w", g_conv.reshape(1, L * 4, CW), conv_w.reshape(L * 4, CW), m_conv_w.reshape(L * 4, CW),
                 v_conv_w.reshape(L * 4, CW), L * 4)
    big_out["conv_w"] = [r.reshape(conv_w.shape) for r in res]

    weights = ["c_ctx", "mod_w", "mod_b", "norm1_w", "norm2_w", "w_in", "conv_w", "conv_b", "dt_bias", "a_log", "ssd_d",
               "ssd_norm_w", "q_a_norm_w", "w_q_b", "kv_a_norm_w", "w_kv_b", "pool_w", "pool_scale", "w_out", "w_mlp1",
               "w_mlp2", "final_norm_w"]
    outs = [loss, grad_x]
    for kind in range(4):
        for name in weights:
            if name == "c_ctx":
                outs.append(cc_out[kind])
            elif name in big_out:
                outs.append(big_out[name][kind])
            else:
                outs.append(small_out[kind][SMALL_NAMES.index(name)])
    return tuple(outs)
```

```python
import functools
import math

import jax
import jax.numpy as jnp
from jax import lax
from jax.experimental import pallas as pl
from jax.experimental.pallas import tpu as pltpu

F32 = jnp.float32
BF = jnp.bfloat16
MXU = BF

D = 1024
CTX = 256
GRID_W = 64
EPS = 1e-6
H = 6
P = 64
SSD_IN = 384
NST = 128
XBC = 896
CH = 128
QL = 256
NOPE = 64
ROPE = 32
VH = 64
QK = 96
HP = 128
QW = H * HP
PD = 256
FF = 4096
IN_COLS = 2092
ROPE_THETA = 10000.0
PC = 2304
A_Z, A_XBC, A_QA, A_KVA, A_POOL, A_KR, A_DT = 0, 384, 1280, 1536, 1792, 2048, 2176
MIXW = SSD_IN + QW + PD
NDEV = 8
FSH = FF // NDEV
TM = 256
TQ = 256
TK = 256
VMEM_CAP = 64 * 1024 * 1024
ADAM_LR, ADAM_B1, ADAM_B2, ADAM_EPS, ADAM_WD, ADAM_STEP = 0.001, 0.9, 0.999, 1e-08, 0.01, 10


def _nbytes(shape, dtype):
    n = 1
    for s in shape:
        if s is not None:
            n *= s
    return n * jnp.dtype(dtype).itemsize


def _params(sem, block_bytes, extra=0):
    lim = min(2 * block_bytes + extra + (8 << 20), VMEM_CAP - (6 << 20))
    return pltpu.CompilerParams(dimension_semantics=sem, vmem_limit_bytes=int(lim))


def _dot(a, b):
    return jnp.dot(a.astype(MXU), b.astype(MXU), preferred_element_type=F32)


def _dot_nt(a, b):
    return lax.dot_general(a.astype(MXU), b.astype(MXU), (((1,), (1,)), ((), ())), preferred_element_type=F32)


def _dot_tn(a, b):
    return lax.dot_general(a.astype(MXU), b.astype(MXU), (((0,), (0,)), ((), ())), preferred_element_type=F32)


def _dot01(m01, x):
    b16 = jnp.bfloat16
    m = m01.astype(b16)
    hi = x.astype(b16)
    r1 = x - hi.astype(F32)
    mid = r1.astype(b16)
    lo = (r1 - mid.astype(F32)).astype(b16)
    f = lambda v: jnp.dot(m, v, preferred_element_type=F32)
    return f(hi) + f(mid) + f(lo)


def _sigmoid(x):
    return 1.0 / (1.0 + jnp.exp(-x))


def _silu(x):
    return x * _sigmoid(x)


def _dsilu(x):
    s = _sigmoid(x)
    return s * (1.0 + x * (1.0 - s))


def _iota(shape, dim):
    return lax.broadcasted_iota(jnp.int32, shape, dim)


def _row(ref, k):
    blk = ref[...]
    return jnp.sum(jnp.where(_iota(blk.shape, 0) == k, blk, 0.0), axis=0, keepdims=True)


def _shift_rows(x, k):
    n = x.shape[0]
    return pltpu.roll(x, (-k) % n, axis=0)


class _Rows:
    def __init__(self, B, T):
        self.B, self.T = B, T
        self.nt = T // TM
        self.nct = CTX // TM

    def row(self, F):
        return pl.BlockSpec((None, TM, F), lambda b, i: (b, i, 0))

    def row2(self, F):
        return pl.BlockSpec((None, 2, TM, F), lambda b, i: (b, 0, i, 0))

    def prev8(self, F):
        return pl.BlockSpec((None, 8, F), lambda b, i: (b, jnp.maximum(i * (TM // 8) - 1, 0), 0))

    def next8(self, F):
        last = self.T // 8 - 1
        return pl.BlockSpec((None, 8, F), lambda b, i: (b, jnp.minimum((i + 1) * (TM // 8), last), 0))

    def mod(self):
        nct = self.nct
        return pl.BlockSpec((None, None, 8, D), lambda b, i: (b, jnp.where(i < nct, 0, 1), 0, 0))

    def const(self, shape):
        z = (0,) * len(shape)
        return pl.BlockSpec(tuple(shape), lambda b, i: z)

    def tgt(self, F):
        nct = self.nct
        return pl.BlockSpec((None, TM, F), lambda b, i: (b, jnp.maximum(i - nct, 0), 0))

    def call(self, name, body, ins, outs, scratch=(), extra=0):
        arrays = [a for a, _ in ins]
        in_specs = [s for _, s in ins]
        out_shape = [o for o, _ in outs]
        out_specs = [s for _, s in outs]
        bb = 0
        for a, s in list(ins) + list(outs):
            if s.block_shape is not None:
                bb += _nbytes(s.block_shape, a.dtype)
        return pl.pallas_call(
            functools.partial(body), name=name, grid=(self.B, self.nt),
            in_specs=in_specs, out_specs=out_specs, out_shape=out_shape, scratch_shapes=list(scratch),
            compiler_params=_params(("arbitrary", "arbitrary"), bb, extra),
        )(*arrays)

    def first_of_stream(self, i):
        return jnp.logical_or(i == 0, i == self.nct)

    def last_of_stream(self, i):
        return jnp.logical_or(i == self.nct - 1, i == self.nt - 1)

    def ext(self, i, prev_ref, cur, next_ref):
        pv = prev_ref[...].astype(F32) * jnp.where(self.first_of_stream(i), 0.0, 1.0)
        nx = next_ref[...].astype(F32) * jnp.where(self.last_of_stream(i), 0.0, 1.0)
        return jnp.concatenate([pv, cur, nx], axis=0)

    def stream_pos(self, i, rows):
        start = jnp.where(i < self.nct, 0, CTX)
        n = jnp.where(i < self.nct, CTX, self.T - CTX)
        t = i * TM - 8 - start + _iota((rows, 1), 0)
        return t, n


def _sds(shape, dtype):
    return jax.ShapeDtypeStruct(tuple(shape), dtype)


def _norm_mod(x, nw, sh, sc):
    r = lax.rsqrt(jnp.mean(x * x, axis=-1, keepdims=True) + EPS)
    xn = x * r
    return xn * nw * (1.0 + sc) + sh, xn, r


def _norm_mod_bwd(dh, xn, r, nw, sc):
    dsh = jnp.sum(dh, axis=0, keepdims=True)
    dsc = jnp.sum(dh * (xn * nw), axis=0, keepdims=True)
    dnw = jnp.sum(dh * (1.0 + sc) * xn, axis=0, keepdims=True)
    dxn = dh * nw * (1.0 + sc)
    dx = r * (dxn - xn * jnp.mean(dxn * xn, axis=-1, keepdims=True))
    return dx, dsh, dsc, dnw


def _acc_rows(ref, first, rows):
    rid = _iota(ref.shape, 0)
    upd = jnp.zeros(ref.shape, F32)
    for k, v in rows.items():
        upd = upd + jnp.where(rid == k, v, 0.0)

    @pl.when(first)
    def _():
        ref[...] = upd

    @pl.when(jnp.logical_not(first))
    def _():
        ref[...] += upd


def _in_proj(R, x, mod, nw1, w_arr):
    B, T = R.B, R.T

    def body(x_ref, mod_ref, nw_ref, w_ref, z_ref, xbc_ref, qa_ref, kva_ref, pool_ref, kr_ref, dt_ref, h_ref):
        h, _, _ = _norm_mod(x_ref[...], nw_ref[...], _row(mod_ref, 0), _row(mod_ref, 1))
        hb = h.astype(BF)
        h_ref[...] = hb
        p = jnp.dot(hb, w_ref[...], preferred_element_type=F32)
        z_ref[...] = p[:, A_Z:A_XBC]
        xbc_ref[...] = p[:, A_XBC:A_QA]
        qa_ref[...] = p[:, A_QA:A_KVA]
        kva_ref[...] = p[:, A_KVA:A_POOL]
        pool_ref[...] = p[:, A_POOL:A_KR]
        kr_ref[...] = p[:, A_KR:A_DT]
        dt_ref[...] = p[:, A_DT:PC]

    widths = [SSD_IN, XBC, QL, QL, PD, HP, HP]
    outs = [(_sds((B, T, w), F32), R.row(w)) for w in widths] + [(_sds((B, T, D), BF), R.row(D))]
    return R.call("in_proj", body,
                  [(x, R.row(D)), (mod, R.mod()), (nw1, R.const((1, D))), (w_arr, R.const((D, PC)))],
                  outs, extra=8 << 20)


def _conv_pre(ext, w_ref, b_ref):
    return (_row(w_ref, 0) * _shift_rows(ext, -1)[8:8 + TM] + _row(w_ref, 1) * ext[8:8 + TM]
            + _row(w_ref, 2) * _shift_rows(ext, 1)[8:8 + TM] + _row(w_ref, 3) * _shift_rows(ext, 2)[8:8 + TM]
            + b_ref[...])


def _softplus(x):
    return jnp.maximum(x, 0.0) + jnp.log(1.0 + jnp.exp(-jnp.abs(x)))


def _ssd_prep(R, xbc_raw, dt_raw, conv_w8, conv_b, dtb):
    B, T = R.B, R.T

    def body(raw_ref, pv_ref, nx_ref, dtr_ref, w_ref, b_ref, dtb_ref, xbc_ref, dt_ref):
        i = pl.program_id(1)
        ext = R.ext(i, pv_ref, raw_ref[...], nx_ref)
        xbc_ref[...] = _silu(_conv_pre(ext, w_ref, b_ref))
        lane = _iota((TM, HP), 1)
        dtv = _softplus(dtr_ref[...] + dtb_ref[...])
        keep = lane < H
        dt_ref[0] = jnp.where(keep, dtv, 0.0)
        dt_ref[1] = jnp.where(keep, pltpu.roll(dtv, HP - H, axis=1), 0.0)

    return R.call("ssd_prep", body,
                  [(xbc_raw, R.row(XBC)), (xbc_raw, R.prev8(XBC)), (xbc_raw, R.next8(XBC)), (dt_raw, R.row(HP)),
                   (conv_w8, R.const((8, XBC))), (conv_b, R.const((1, XBC))), (dtb, R.const((1, HP)))],
                  [(_sds((B, T, XBC), F32), R.row(XBC)), (_sds((B, 2, T, HP), F32), R.row2(HP))], extra=12 << 20)


def _chunk_index(d, s, nc, nctc):
    rev = jnp.where(s < nctc, nctc - 1 - s, nc - 1 - (s - nctc))
    return jnp.where(d == 0, s, rev)


def _scan_common(d, xbc, dtv, alog_ref):
    a = -jnp.exp(_row(alog_ref, 0))
    adt = dtv * a
    row = _iota((CH, CH), 0)
    col = _iota((CH, CH), 1)
    sgn = jnp.where(d == 0, 1, -1)
    inc = (row - col) * sgn >= 0
    inc_t = (col - row) * sgn >= 0
    q = _dot01(inc, adt)
    return a, adt, inc, inc_t, q


def _head_cols(h, adt, q, dtv, inc_t):
    onehot = (_iota((1, HP), 1) == h).astype(F32)
    adt_h = jnp.sum(adt * onehot, axis=1, keepdims=True)
    qc = jnp.sum(q * onehot, axis=1, keepdims=True)
    dt_h = jnp.sum(dtv * onehot, axis=1, keepdims=True)
    qr = jnp.sum(adt_h * inc_t.astype(F32), axis=0, keepdims=True)
    qtot = jnp.sum(adt_h, axis=0, keepdims=True)
    return onehot, qc, qr, qtot, dt_h


def _ssd_scan(B, T, xbc, dt2, alog2):
    nc, nctc = T // CH, CTX // CH

    def body(xbc_ref, dt_ref, alog_ref, y_ref, hin_ref, hs):
        d, s = pl.program_id(1), pl.program_id(2)

        @pl.when(s == 0)
        def _():
            hs[...] = jnp.zeros(hs.shape, F32)

        xbc_v = xbc_ref[...]
        dtv = dt_ref[...]
        _, adt, inc, inc_t, q = _scan_common(d, xbc_v, dtv, alog_ref)
        hin_ref[...] = hs[...]
        for g in range(2):
            bg = xbc_v[:, SSD_IN + NST * g:SSD_IN + NST * (g + 1)]
            cg = xbc_v[:, SSD_IN + 2 * NST + NST * g:SSD_IN + 2 * NST + NST * (g + 1)]
            gm = _dot_nt(cg, bg)
            for r in range(3):
                h = 3 * g + r
                _, qc, qr, qtot, dt_h = _head_cols(h, adt, q, dtv, inc_t)
                lm = jnp.where(inc, jnp.exp(qc - qr), 0.0)
                xh = xbc_v[:, P * h:P * (h + 1)] * dt_h
                hprev = hs[P * h:P * (h + 1), :]
                y = _dot(gm * lm, xh) + jnp.exp(qc) * _dot_nt(cg, hprev)
                y_ref[:, P * h:P * (h + 1)] = y
                hs[P * h:P * (h + 1), :] = jnp.exp(qtot) * hprev + _dot_tn(xh * jnp.exp(qtot - qc), bg)

    cidx = lambda d, s: _chunk_index(d, s, nc, nctc)
    bb = _nbytes((CH, XBC), F32) + _nbytes((CH, HP), F32) + _nbytes((CH, SSD_IN), F32) + _nbytes((SSD_IN, NST), F32)
    return pl.pallas_call(
        functools.partial(body), name="ssd_scan", grid=(B, 2, nc),
        in_specs=[pl.BlockSpec((None, CH, XBC), lambda b, d, s: (b, cidx(d, s), 0)),
                  pl.BlockSpec((None, None, CH, HP), lambda b, d, s: (b, d, cidx(d, s), 0)),
                  pl.BlockSpec((None, 8, HP), lambda b, d, s: (d, 0, 0))],
        out_specs=[pl.BlockSpec((None, None, CH, SSD_IN), lambda b, d, s: (b, d, cidx(d, s), 0)),
                   pl.BlockSpec((None, None, None, SSD_IN, NST), lambda b, d, s: (b, d, cidx(d, s), 0, 0))],
        out_shape=[_sds((B, 2, T, SSD_IN), F32), _sds((B, 2, nc, SSD_IN, NST), F32)],
        scratch_shapes=[pltpu.VMEM((SSD_IN, NST), F32)],
        compiler_params=_params(("arbitrary",) * 3, bb, 8 << 20),
    )(xbc, dt2, alog2)


def _swap8(u):
    lane = _iota(u.shape, 1)
    n = u.shape[1]
    return jnp.where((lane & 15) < 8, pltpu.roll(u, n - 8, axis=1), pltpu.roll(u, 8, axis=1))


def _rope(u, cos, sin_signed):
    return u * cos + _swap8(u) * sin_signed


def _rms(x, w):
    r = lax.rsqrt(jnp.mean(x * x, axis=-1, keepdims=True) + EPS)
    xh = x * r
    return xh * w, xh, r


def _rms_bwd(dy, xh, r, w):
    dw = jnp.sum(dy * xh, axis=0, keepdims=True)
    dxh = dy * w
    return r * (dxh - xh * jnp.mean(dxh * xh, axis=-1, keepdims=True)), dw


def _tile6(t):
    return jnp.concatenate([t] * H, axis=1)


def _per_head(fn, u):
    return jnp.concatenate([fn(u[:, HP * h:HP * (h + 1)]) for h in range(H)], axis=1)


def _mla_prep(R, qa, kva, kr, qnw, kvnw, wq, wk, wv, cos, sin):
    B, T = R.B, R.T
    scale = QK ** -0.5

    def body(qa_ref, kva_ref, kr_ref, qnw_ref, kvnw_ref, wq_ref, wk_ref, wv_ref, cos_ref, sin_ref,
             qt_ref, kt_ref, vt_ref, cq_ref, ckv_ref):
        cq, _, _ = _rms(qa_ref[...], qnw_ref[...])
        ckv, _, _ = _rms(kva_ref[...], kvnw_ref[...])
        cqb, ckvb = cq.astype(BF), ckv.astype(BF)
        cq_ref[...] = cqb
        ckv_ref[...] = ckvb
        cos1, sin1 = cos_ref[...], sin_ref[...]
        q = jnp.dot(cqb, wq_ref[...], preferred_element_type=F32)
        qt_ref[...] = (_per_head(lambda u: _rope(u, cos1, sin1), q) * scale).astype(BF)
        kk = _rope(kr_ref[...], cos1, sin1)
        k = jnp.dot(ckvb, wk_ref[...], preferred_element_type=F32)
        kt_ref[...] = (k + _tile6(kk)).astype(BF)
        v = jnp.dot(ckvb, wv_ref[...], preferred_element_type=F32)
        vt_ref[...] = jnp.where((_iota((TM, QW), 1) & (HP - 1)) == VH, 1.0, v).astype(BF)

    return R.call("mla_prep", body,
                  [(qa, R.row(QL)), (kva, R.row(QL)), (kr, R.row(HP)), (qnw, R.const((1, QL))), (kvnw, R.const((1, QL))),
                   (wq, R.const((QL, QW))), (wk, R.const((QL, QW))), (wv, R.const((QL, QW))),
                   (cos, pl.BlockSpec((TM, HP), lambda b, i: (i, 0))), (sin, pl.BlockSpec((TM, HP), lambda b, i: (i, 0)))],
                  [(_sds((B, T, QW), BF), R.row(QW))] * 3 + [(_sds((B, T, QL), BF), R.row(QL))] * 2, extra=8 << 20)


def _flash_fwd(B, T, q_t, kt, v_t):
    nq, nk = T // TQ, T // TK

    def body(q_ref, k_ref, v_ref, o_ref, lse_ref, s_scr):
        i = pl.program_id(2)
        q_tr = q_ref[...]

        def attend(nch):
            mrun = None
            for j in range(nch):
                s = _dot(k_ref[TK * j:TK * (j + 1), :], q_tr)
                s_scr[j] = s
                mrun = s if mrun is None else jnp.maximum(mrun, s)
            m = jnp.max(mrun, axis=0, keepdims=True)
            acc_t = jnp.zeros((HP, TQ), F32)
            for j in range(nch):
                acc_t = acc_t + _dot(v_ref[:, TK * j:TK * (j + 1)], jnp.exp(s_scr[j] - m))
            row = _iota((HP, TQ), 0)
            l = jnp.sum(jnp.where(row == VH, acc_t, 0.0), axis=0, keepdims=True)
            o_ref[...] = jnp.where(row < VH, acc_t / l, 0.0).T
            lse_ref[...] = jnp.broadcast_to(m + jnp.log(l), (HP, TQ)).T

        @pl.when(i < CTX // TQ)
        def _():
            attend(CTX // TK)

        @pl.when(i >= CTX // TQ)
        def _():
            attend(nk)

    bb = _nbytes((TQ, HP), BF) + 2 * _nbytes((T, HP), BF) + _nbytes((TQ, HP), F32) + _nbytes((8, TQ), F32)
    return pl.pallas_call(
        functools.partial(body), name="flash_fwd", grid=(B, H, nq),
        in_specs=[pl.BlockSpec((None, None, HP, TQ), lambda b, h, i: (b, h, 0, i)),
                  pl.BlockSpec((None, T, HP), lambda b, h, i: (b, 0, h)),
                  pl.BlockSpec((None, None, HP, T), lambda b, h, i: (b, h, 0, 0))],
        out_specs=[pl.BlockSpec((None, TQ, HP), lambda b, h, i: (b, i, h)),
                   pl.BlockSpec((None, None, TQ, HP), lambda b, h, i: (b, h, i, 0))],
        out_shape=[_sds((B, T, QW), F32), _sds((B, H, T, HP), F32)],
        scratch_shapes=[pltpu.VMEM((nk, TK, TQ), F32)],
        compiler_params=_params(("arbitrary",) * 3, bb, _nbytes((nk, TK, TQ), F32) + (8 << 20)),
    )(q_t, kt, v_t)


def _pool_terms(R, i, rows):
    t, n = R.stream_pos(i, rows)
    lane = _iota((1, PD), 1)
    half = jnp.where(lane < 64, 1, jnp.where(lane < 128, 2, jnp.where(lane < 192, 4, 8)))
    cnt = (jnp.minimum(t + half, n) - jnp.maximum(t - half, 0)).astype(F32)
    valid = jnp.logical_and(t >= 0, t < n)
    return jnp.where(valid, cnt, 1.0), valid.astype(F32), lane


def _lane_select(lane, a2, a4, a8, a16):
    return jnp.where(lane < 64, a2, jnp.where(lane < 128, a4, jnp.where(lane < 192, a8, a16)))


def _pool_centred(R, i, ext):
    cnt, valid, lane = _pool_terms(R, i, ext.shape[0])
    s2 = ext + _shift_rows(ext, -1)
    s4 = _shift_rows(s2, -1) + _shift_rows(s2, 1)
    s8 = _shift_rows(s4, -2) + _shift_rows(s4, 2)
    s16 = _shift_rows(s8, -4) + _shift_rows(s8, 4)
    return _lane_select(lane, s2, s4, s8, s16) / cnt - ext, cnt, valid, lane


def _pool_fwd(R, u, wbd, scale):
    B, T = R.B, R.T

    def body(u_ref, pv_ref, nx_ref, w_ref, sc_ref, o_ref):
        i = pl.program_id(1)
        ext = R.ext(i, pv_ref, u_ref[...], nx_ref)
        dm, _, _, _ = _pool_centred(R, i, ext)
        o_ref[...] = _dot(dm[8:8 + TM], w_ref[...]) * sc_ref[...]

    return R.call("pool_fwd", body,
                  [(u, R.row(PD)), (u, R.prev8(PD)), (u, R.next8(PD)), (wbd, R.const((PD, PD))), (scale, R.const((1, PD)))],
                  [(_sds((B, T, PD), F32), R.row(PD))], extra=8 << 20)


def _group_mask():
    return _iota((1, SSD_IN), 1) < SSD_IN // 2


def _ssd_gate(y2_ref, xbc_ref, z_ref, dsk_ref):
    ytot = y2_ref[0] + y2_ref[1] + xbc_ref[:, 0:SSD_IN] * dsk_ref[...]
    z = z_ref[...]
    gz = ytot * _silu(z)
    g0 = _group_mask()
    sq = gz * gz
    s0 = jnp.sum(jnp.where(g0, sq, 0.0), axis=1, keepdims=True)
    s1 = jnp.sum(jnp.where(g0, 0.0, sq), axis=1, keepdims=True)
    half = SSD_IN // 2
    r = jnp.where(g0, lax.rsqrt(s0 / half + EPS), lax.rsqrt(s1 / half + EPS))
    return ytot, z, gz, r


def _out_proj(R, x, mod, y2, xbc, z, o, pool, dsk, snw, wout):
    B, T = R.B, R.T

    def body(x_ref, mod_ref, y2_ref, xbc_ref, z_ref, o_ref, pool_ref, dsk_ref, snw_ref, w_ref, xmid_ref, cat_ref, mix_ref):
        _, _, gz, r = _ssd_gate(y2_ref, xbc_ref, z_ref, dsk_ref)
        cat_ref[:, 0:SSD_IN] = (gz * r * snw_ref[...]).astype(BF)
        cat_ref[:, SSD_IN:SSD_IN + QW] = o_ref[...].astype(BF)
        cat_ref[:, SSD_IN + QW:MIXW] = pool_ref[...].astype(BF)
        mix = jnp.dot(cat_ref[...], w_ref[...], preferred_element_type=F32)
        mix_ref[...] = mix.astype(BF)
        xmid_ref[...] = x_ref[...] + _row(mod_ref, 2) * mix

    return R.call("out_proj", body,
                  [(x, R.row(D)), (mod, R.mod()), (y2, R.row2(SSD_IN)), (xbc, R.row(XBC)), (z, R.row(SSD_IN)), (o, R.row(QW)),
                   (pool, R.row(PD)), (dsk, R.const((1, SSD_IN))), (snw, R.const((1, SSD_IN))), (wout, R.const((MIXW, D)))],
                  [(_sds((B, T, D), F32), R.row(D)), (_sds((B, T, MIXW), BF), R.row(MIXW)), (_sds((B, T, D), BF), R.row(D))],
                  extra=8 << 20)


def _load_once(first, pairs, sem):
    @pl.when(first)
    def _():
        cps = [pltpu.make_async_copy(src, dst, sem.at[k]) for k, (src, dst) in enumerate(pairs)]
        for cp in cps:
            cp.start()
        for cp in cps:
            cp.wait()


ANY = pl.BlockSpec(memory_space=pl.ANY)


def _mlp_fwd(R, xmid, mod, nw2, w1, w2):
    B, T = R.B, R.T

    def body(x_ref, mod_ref, nw_ref, w1_hbm, w2_hbm, xo_ref, h_ref, u_ref, y_ref, w1_v, w2_v, sem):
        first = jnp.logical_and(pl.program_id(0) == 0, pl.program_id(1) == 0)
        _load_once(first, [(w1_hbm, w1_v), (w2_hbm, w2_v)], sem)
        x = x_ref[...]
        h, _, _ = _norm_mod(x, nw_ref[...], _row(mod_ref, 3), _row(mod_ref, 4))
        hb = h.astype(BF)
        h_ref[...] = hb
        y = jnp.zeros((TM, D), F32)
        for j in range(NDEV):
            u = jnp.dot(hb, w1_v[j], preferred_element_type=F32)
            u_ref[:, FSH * j:FSH * (j + 1)] = u.astype(BF)
            a = jnp.square(jnp.maximum(u, 0.0))
            y = y + jnp.dot(a.astype(BF), w2_v[j], preferred_element_type=F32)
        y_ref[...] = y.astype(BF)
        xo_ref[...] = x + _row(mod_ref, 5) * y

    return R.call("mlp_fwd", body,
                  [(xmid, R.row(D)), (mod, R.mod()), (nw2, R.const((1, D))), (w1, ANY), (w2, ANY)],
                  [(_sds((B, T, D), F32), R.row(D)), (_sds((B, T, D), BF), R.row(D)), (_sds((B, T, FF), BF), R.row(FF)),
                   (_sds((B, T, D), BF), R.row(D))],
                  scratch=[pltpu.VMEM((NDEV, D, FSH), w1.dtype), pltpu.VMEM((NDEV, FSH, D), w2.dtype), pltpu.SemaphoreType.DMA((2,))],
                  extra=(2 * _nbytes((NDEV, D, FSH), BF)) + (8 << 20))


def _loss_head(R, x, tgt, fnw):
    B, T = R.B, R.T

    def body(x_ref, t_ref, w_ref, dx_ref, loss_ref, dw_ref):
        b, i = pl.program_id(0), pl.program_id(1)
        live = jnp.where(i >= R.nct, 1.0, 0.0)
        y, xh, r = _rms(x_ref[...], w_ref[...])
        err = (y - t_ref[...]) * live
        dy = err / D
        dxn, dw = _rms_bwd(dy, xh, r, w_ref[...])
        dx_ref[...] = dxn
        first = jnp.logical_and(b == 0, i == 0)
        part = 0.5 * jnp.sum(jnp.sum(err * err, axis=1, keepdims=True), axis=0, keepdims=True) / D
        _acc_rows(loss_ref, first, {0: jnp.broadcast_to(part, (1, HP))})
        _acc_rows(dw_ref, first, {0: dw})

    return R.call("loss_head", body,
                  [(x, R.row(D)), (tgt, R.tgt(D)), (fnw, R.const((1, D)))],
                  [(_sds((B, T, D), F32), R.row(D)), (_sds((8, HP), F32), R.const((8, HP))), (_sds((8, D), F32), R.const((8, D)))],
                  extra=8 << 20)


def _tn_matmul(name, a, b, tn):
    B, T, K = a.shape
    N = b.shape[2]
    nk = 1
    while T % nk or (T // nk) > 1088 or (T // nk) % 16:
        nk += 1
    tk = T // nk
    kt = K if K <= 1536 else 1024
    assert K % kt == 0 and N % tn == 0

    def body(a_ref, b_ref, o_ref):
        first = jnp.logical_and(pl.program_id(2) == 0, pl.program_id(3) == 0)

        @pl.when(first)
        def _():
            o_ref[...] = jnp.zeros(o_ref.shape, F32)

        o_ref[...] += _dot_tn(a_ref[...], b_ref[...])

    bb = _nbytes((tk, kt), a.dtype) + _nbytes((tk, tn), b.dtype) + _nbytes((kt, tn), F32)
    return pl.pallas_call(
        functools.partial(body), name=name, grid=(N // tn, K // kt, B, nk),
        in_specs=[pl.BlockSpec((None, tk, kt), lambda j, kk, bi, t: (bi, t, kk)),
                  pl.BlockSpec((None, tk, tn), lambda j, kk, bi, t: (bi, t, j))],
        out_specs=pl.BlockSpec((None, kt, tn), lambda j, kk, bi, t: (j, kk, 0)),
        out_shape=_sds((N // tn, K, tn), F32),
        compiler_params=_params(("arbitrary",) * 4, bb, 8 << 20),
    )(a, b)


def _mlp_bwd(R, dxo, xmid, ubf, ybf, mod, nw2, w1, w2):
    B, T = R.B, R.T

    def body(dxo_ref, x_ref, u_ref, y_ref, mod_ref, nw_ref, w1_hbm, w2_hbm,
             dxm_ref, du_ref, a_ref, dy_ref, dmod_ref, dnw_ref, w1_v, w2_v, sem):
        b, i = pl.program_id(0), pl.program_id(1)
        _load_once(jnp.logical_and(b == 0, i == 0), [(w1_hbm, w1_v), (w2_hbm, w2_v)], sem)
        dxo = dxo_ref[...]
        _, xn, r = _norm_mod(x_ref[...], nw_ref[...], _row(mod_ref, 3), _row(mod_ref, 4))
        dyb = (dxo * _row(mod_ref, 5)).astype(BF)
        dy_ref[...] = dyb
        dg2 = jnp.sum(dxo * y_ref[...].astype(F32), axis=0, keepdims=True)
        dh = jnp.zeros((TM, D), F32)
        for j in range(NDEV):
            rl = jnp.maximum(u_ref[:, FSH * j:FSH * (j + 1)].astype(F32), 0.0)
            a_ref[:, FSH * j:FSH * (j + 1)] = (rl * rl).astype(BF)
            du = (_dot_nt(dyb, w2_v[j]) * (2.0 * rl)).astype(BF)
            du_ref[:, FSH * j:FSH * (j + 1)] = du
            dh = dh + _dot_nt(du, w1_v[j])
        dx, dsh, dsc, dnw = _norm_mod_bwd(dh, xn, r, nw_ref[...], _row(mod_ref, 4))
        dxm_ref[...] = dxo + dx
        _acc_rows(dmod_ref, R.first_of_stream(i), {3: dsh, 4: dsc, 5: dg2})
        _acc_rows(dnw_ref, jnp.logical_and(b == 0, i == 0), {0: dnw})

    return R.call("mlp_bwd", body,
                  [(dxo, R.row(D)), (xmid, R.row(D)), (ubf, R.row(FF)), (ybf, R.row(D)), (mod, R.mod()), (nw2, R.const((1, D))),
                   (w1, ANY), (w2, ANY)],
                  [(_sds((B, T, D), F32), R.row(D)), (_sds((B, T, FF), BF), R.row(FF)), (_sds((B, T, FF), BF), R.row(FF)),
                   (_sds((B, T, D), BF), R.row(D)), (_sds((B, 2, 8, D), F32), R.mod()), (_sds((8, D), F32), R.const((8, D)))],
                  scratch=[pltpu.VMEM((NDEV, D, FSH), w1.dtype), pltpu.VMEM((NDEV, FSH, D), w2.dtype), pltpu.SemaphoreType.DMA((2,))],
                  extra=(2 * _nbytes((NDEV, D, FSH), BF)) + (8 << 20))


def _out_bwd(R, dxm, mod, mixbf, y2, xbc, z, o, dsk, snw, wout):
    B, T = R.B, R.T

    def body(dxm_ref, mod_ref, mix_ref, y2_ref, xbc_ref, z_ref, o_ref, dsk_ref, snw_ref, w_ref,
             dmix_ref, dz_ref, dyt_ref, dxsk_ref, do_ref, dd_ref, dpo_ref, dmod_ref, dvec_ref):
        b, i = pl.program_id(0), pl.program_id(1)
        dxm = dxm_ref[...]
        dmixb = (dxm * _row(mod_ref, 2)).astype(BF)
        dmix_ref[...] = dmixb
        dg1 = jnp.sum(dxm * mix_ref[...].astype(F32), axis=0, keepdims=True)
        dcat = _dot_nt(dmixb, w_ref[...])
        do_v = dcat[:, SSD_IN:SSD_IN + QW]
        do_ref[...] = do_v.astype(BF)
        doo = do_v * o_ref[...]
        for h in range(H):
            dd_ref[h] = jnp.broadcast_to(jnp.sum(doo[:, HP * h:HP * (h + 1)], axis=1, keepdims=True), (TM, HP))
        dpo_ref[...] = dcat[:, SSD_IN + QW:MIXW]
        dsn = dcat[:, 0:SSD_IN]
        ytot, zv, gz, r = _ssd_gate(y2_ref, xbc_ref, z_ref, dsk_ref)
        gh = gz * r
        dsnw = jnp.sum(dsn * gh, axis=0, keepdims=True)
        dgh = dsn * snw_ref[...]
        g0 = _group_mask()
        pr = dgh * gh
        half = SSD_IN // 2
        m0 = jnp.sum(jnp.where(g0, pr, 0.0), axis=1, keepdims=True) / half
        m1 = jnp.sum(jnp.where(g0, 0.0, pr), axis=1, keepdims=True) / half
        dgz = r * (dgh - gh * jnp.where(g0, m0, m1))
        dyt = dgz * _silu(zv)
        dz_ref[...] = (dgz * ytot * _dsilu(zv)).astype(BF)
        dyt_ref[...] = dyt
        dxsk_ref[...] = dyt * dsk_ref[...]
        ddsk = jnp.sum(dyt * xbc_ref[:, 0:SSD_IN], axis=0, keepdims=True)
        _acc_rows(dmod_ref, R.first_of_stream(i), {2: dg1})
        _acc_rows(dvec_ref, jnp.logical_and(b == 0, i == 0), {0: dsnw, 1: ddsk})

    return R.call("out_bwd", body,
                  [(dxm, R.row(D)), (mod, R.mod()), (mixbf, R.row(D)), (y2, R.row2(SSD_IN)), (xbc, R.row(XBC)), (z, R.row(SSD_IN)),
                   (o, R.row(QW)), (dsk, R.const((1, SSD_IN))), (snw, R.const((1, SSD_IN))), (wout, R.const((MIXW, D)))],
                  [(_sds((B, T, D), BF), R.row(D)), (_sds((B, T, SSD_IN), BF), R.row(SSD_IN)), (_sds((B, T, SSD_IN), F32), R.row(SSD_IN)),
                   (_sds((B, T, SSD_IN), F32), R.row(SSD_IN)), (_sds((B, T, QW), BF), R.row(QW)),
                   (_sds((B, H, T, HP), F32), pl.BlockSpec((None, H, TM, HP), lambda b, i: (b, 0, i, 0))),
                   (_sds((B, T, PD), F32), R.row(PD)),
                   (_sds((B, 2, 8, D), F32), R.mod()), (_sds((8, SSD_IN), F32), R.const((8, SSD_IN)))],
                  extra=8 << 20)


def _pool_bwd(R, dpo, u, wbd, scale):
    B, T = R.B, R.T

    def body(d_ref, dpv_ref, dnx_ref, u_ref, upv_ref, unx_ref, w_ref, sc_ref, du_ref, dw_ref, dsc_ref):
        b, i = pl.program_id(0), pl.program_id(1)
        ext_u = R.ext(i, upv_ref, u_ref[...], unx_ref)
        ext_d = R.ext(i, dpv_ref, d_ref[...], dnx_ref)
        dm, cnt, valid, lane = _pool_centred(R, i, ext_u)
        ddm = _dot_nt(ext_d * sc_ref[...], w_ref[...]) * valid
        e = ddm / cnt
        a2 = e + _shift_rows(e, 1)
        a4 = _shift_rows(a2, -1) + _shift_rows(a2, 1)
        a8 = _shift_rows(a4, -2) + _shift_rows(a4, 2)
        a16 = _shift_rows(a8, -4) + _shift_rows(a8, 4)
        du_ref[...] = (_lane_select(lane, a2, a4, a8, a16) - ddm)[8:8 + TM].astype(BF)
        dmc = dm[8:8 + TM]
        dpo_c = d_ref[...]
        first = jnp.logical_and(b == 0, i == 0)

        @pl.when(first)
        def _():
            dw_ref[...] = jnp.zeros(dw_ref.shape, F32)

        dw_ref[...] += _dot_tn(dmc, dpo_c * sc_ref[...])
        _acc_rows(dsc_ref, first, {0: jnp.sum(dpo_c * _dot(dmc, w_ref[...]), axis=0, keepdims=True)})

    return R.call("pool_bwd", body,
                  [(dpo, R.row(PD)), (dpo, R.prev8(PD)), (dpo, R.next8(PD)), (u, R.row(PD)), (u, R.prev8(PD)), (u, R.next8(PD)),
                   (wbd, R.const((PD, PD))), (scale, R.const((1, PD)))],
                  [(_sds((B, T, PD), BF), R.row(PD)), (_sds((PD, PD), F32), R.const((PD, PD))), (_sds((8, PD), F32), R.const((8, PD)))],
                  extra=8 << 20)


def _heads_t(a):
    B, T, _ = a.shape
    return jnp.transpose(a.reshape(B, T, H, HP), (0, 2, 3, 1))


def _flash_bwd(B, T, qt, kt, dob, q_t, k_t, v_t, do_t, lse_b, dd_b):
    nk, nq = T // TK, T // TQ

    def body(k_ref, kt_ref, vt_ref, q_ref, do_ref, qt_ref, dot_ref, lse_ref, dd_ref, dq_ref, dk_ref, dv_ref,
             s_scr, dp_scr, ds_scr):
        j = pl.program_id(2)
        s_scr[...] = _dot(q_ref[...], kt_ref[...])
        dp_scr[...] = _dot(do_ref[...], vt_ref[...])
        ctx_keys = jnp.where(j < CTX // TK, 1.0, 0.0)
        dk_tr = jnp.zeros((HP, TK), F32)
        dv_tr = jnp.zeros((HP, TK), F32)
        for i in range(nq):
            rows = slice(TQ * i, TQ * (i + 1))
            lse2 = jnp.concatenate([lse_ref[rows, :]] * (TK // HP), axis=1)
            dd2 = jnp.concatenate([dd_ref[rows, :]] * (TK // HP), axis=1)
            p = jnp.exp(s_scr[rows, :] - lse2)
            if i < CTX // TQ:
                p = p * ctx_keys
            ds = (p * (dp_scr[rows, :] - dd2)).astype(BF)
            ds_scr[rows, :] = ds
            dv_tr = dv_tr + _dot(dot_ref[:, rows], p)
            dk_tr = dk_tr + _dot(qt_ref[:, rows], ds)
        dq = _dot(ds_scr[...], k_ref[...])

        @pl.when(j == 0)
        def _():
            dq_ref[...] = dq

        @pl.when(j > 0)
        def _():
            dq_ref[...] += dq

        dk_ref[...] = dk_tr.T
        dv_ref[...] = dv_tr.T

    tspec = pl.BlockSpec((None, TK, HP), lambda b, h, j: (b, j, h))
    fspec = pl.BlockSpec((None, T, HP), lambda b, h, j: (b, 0, h))
    ttspec = pl.BlockSpec((None, None, HP, TK), lambda b, h, j: (b, h, 0, j))
    ftspec = pl.BlockSpec((None, None, HP, T), lambda b, h, j: (b, h, 0, 0))
    bspec = pl.BlockSpec((None, None, T, HP), lambda b, h, j: (b, h, 0, 0))
    bb = 4 * _nbytes((T, HP), BF) + 3 * _nbytes((T, HP), F32) + 8 * _nbytes((TK, HP), F32)
    scr = 2 * _nbytes((T, TK), F32) + _nbytes((T, TK), BF)
    return pl.pallas_call(
        functools.partial(body), name="flash_bwd", grid=(B, H, nk),
        in_specs=[tspec, ttspec, ttspec, fspec, fspec, ftspec, ftspec, bspec, bspec],
        out_specs=[fspec, tspec, tspec],
        out_shape=[_sds((B, T, QW), F32)] * 3,
        scratch_shapes=[pltpu.VMEM((T, TK), F32), pltpu.VMEM((T, TK), F32), pltpu.VMEM((T, TK), BF)],
        compiler_params=_params(("arbitrary",) * 3, bb, scr + (12 << 20)),
    )(kt, k_t, v_t, qt, dob, q_t, do_t, lse_b, dd_b)


def _mla_bwd(R, dqt, dkt, dvt, qa, kva, qnw, kvnw, wq, wk, wv, cos, sin):
    B, T = R.B, R.T
    scale = QK ** -0.5

    def body(dq_ref, dk_ref, dv_ref, qa_ref, kva_ref, qnw_ref, kvnw_ref, wq_ref, wk_ref, wv_ref, cos_ref, sin_ref,
             dqp_ref, dkb_ref, dvb_ref, dqa_ref, dkva_ref, dkr_ref, dnw_ref):
        b, i = pl.program_id(0), pl.program_id(1)
        cos1, sin1 = cos_ref[...], sin_ref[...]
        dq = dq_ref[...] * scale
        dqp = _per_head(lambda g: g * cos1 + _swap8(g * sin1), dq).astype(BF)
        dqp_ref[...] = dqp
        dkv = dk_ref[...]
        dkb = dkv.astype(BF)
        dkb_ref[...] = dkb
        dvb = dv_ref[...].astype(BF)
        dvb_ref[...] = dvb
        dkk = dkv[:, 0:HP]
        for h in range(1, H):
            dkk = dkk + dkv[:, HP * h:HP * (h + 1)]
        lane = _iota((TM, HP), 1)
        rope_lane = jnp.logical_and(lane >= NOPE, lane < NOPE + ROPE)
        dkr_ref[...] = jnp.where(rope_lane, dkk * cos1 + _swap8(dkk * sin1), 0.0).astype(BF)
        _, qh, qr = _rms(qa_ref[...], qnw_ref[...])
        _, kh, kr_ = _rms(kva_ref[...], kvnw_ref[...])
        dcq = _dot_nt(dqp, wq_ref[...])
        dckv = _dot_nt(dkb, wk_ref[...]) + _dot_nt(dvb, wv_ref[...])
        dqa, dqw = _rms_bwd(dcq, qh, qr, qnw_ref[...])
        dkva, dkw = _rms_bwd(dckv, kh, kr_, kvnw_ref[...])
        dqa_ref[...] = dqa.astype(BF)
        dkva_ref[...] = dkva.astype(BF)
        _acc_rows(dnw_ref, jnp.logical_and(b == 0, i == 0), {0: dqw, 1: dkw})

    tab = pl.BlockSpec((TM, HP), lambda b, i: (i, 0))
    return R.call("mla_bwd", body,
                  [(dqt, R.row(QW)), (dkt, R.row(QW)), (dvt, R.row(QW)), (qa, R.row(QL)), (kva, R.row(QL)),
                   (qnw, R.const((1, QL))), (kvnw, R.const((1, QL))), (wq, R.const((QL, QW))), (wk, R.const((QL, QW))),
                   (wv, R.const((QL, QW))), (cos, tab), (sin, tab)],
                  [(_sds((B, T, QW), BF), R.row(QW))] * 3 + [(_sds((B, T, QL), BF), R.row(QL))] * 2
                  + [(_sds((B, T, HP), BF), R.row(HP)), (_sds((8, QL), F32), R.const((8, QL)))], extra=8 << 20)


def _ssd_scan_bwd(B, T, dyt, xbc, dt2, alog2, hin):
    nc, nctc = T // CH, CTX // CH

    def body(dy_ref, xbc_ref, dt_ref, alog_ref, hin_ref, dxbc_ref, ddt_ref, da_ref, dhs):
        d, s = pl.program_id(1), pl.program_id(2)

        @pl.when(s == 0)
        def _():
            dhs[...] = jnp.zeros(dhs.shape, F32)
            da_ref[...] = jnp.zeros(da_ref.shape, F32)

        xbc_v = xbc_ref[...]
        dtv = dt_ref[...]
        dyv = dy_ref[...]
        a, adt, inc, inc_t, q = _scan_common(d, xbc_v, dtv, alog_ref)
        dq_all = jnp.zeros((CH, HP), F32)
        dqtot_all = jnp.zeros((1, HP), F32)
        ddt_x = jnp.zeros((CH, HP), F32)
        for g in range(2):
            bg = xbc_v[:, SSD_IN + NST * g:SSD_IN + NST * (g + 1)]
            cg = xbc_v[:, SSD_IN + 2 * NST + NST * g:SSD_IN + 2 * NST + NST * (g + 1)]
            gm = _dot_nt(cg, bg)
            gm_t = _dot_nt(bg, cg)
            dgm = jnp.zeros((CH, CH), F32)
            dbg = jnp.zeros((CH, NST), F32)
            dcg = jnp.zeros((CH, NST), F32)
            for r in range(3):
                h = 3 * g + r
                onehot, qc, qr, qtot, dt_h = _head_cols(h, adt, q, dtv, inc_t)
                lm = jnp.where(inc, jnp.exp(qc - qr), 0.0)
                xs_h = xbc_v[:, P * h:P * (h + 1)]
                xh = xs_h * dt_h
                sm = gm * lm
                dy_h = dyv[:, P * h:P * (h + 1)]
                hprev = hin_ref[P * h:P * (h + 1), :]
                dho = dhs[P * h:P * (h + 1), :]
                eq = jnp.exp(qc)
                etot = jnp.exp(qtot)
                dte = jnp.exp(qtot - qc)
                ds = _dot_nt(dy_h, xh)
                dx = _dot_tn(sm, dy_h)
                edy = eq * dy_h
                dq_col = jnp.sum(dy_h * (eq * _dot_nt(cg, hprev)), axis=1, keepdims=True)
                dcg = dcg + _dot(edy, hprev)
                dhin = _dot_tn(edy, cg) + etot * dho
                zs = _dot_nt(bg, dho)
                dx = dx + dte * zs
                w = dte * jnp.sum(xh * zs, axis=1, keepdims=True)
                dbg = dbg + _dot(xh * dte, dho)
                dqtot = jnp.sum(jnp.sum(hprev * dho, axis=1, keepdims=True), axis=0, keepdims=True) * etot \
                    + jnp.sum(w, axis=0, keepdims=True)
                dq_col = dq_col - w
                dgm = dgm + ds * lm
                lm_t = jnp.where(inc_t, jnp.exp(qr - qc), 0.0)
                e_t = _dot_nt(xh, dy_h) * gm_t * lm_t
                dq_col = dq_col + jnp.sum(ds * sm, axis=1, keepdims=True) - jnp.sum(e_t, axis=1, keepdims=True)
                dq_all = dq_all + dq_col * onehot
                dqtot_all = dqtot_all + dqtot * onehot
                dxbc_ref[:, P * h:P * (h + 1)] = dx * dt_h
                ddt_x = ddt_x + jnp.sum(dx * xs_h, axis=1, keepdims=True) * onehot
                dhs[P * h:P * (h + 1), :] = dhin
            dcg = dcg + _dot(dgm, bg)
            dbg = dbg + _dot_tn(dgm, cg)
            dxbc_ref[:, SSD_IN + NST * g:SSD_IN + NST * (g + 1)] = dbg
            dxbc_ref[:, SSD_IN + 2 * NST + NST * g:SSD_IN + 2 * NST + NST * (g + 1)] = dcg
        dadt = _dot01(inc_t, dq_all) + dqtot_all
        ddt_ref[...] = ddt_x + dadt * a
        da_ref[...] += jnp.where(_iota((8, HP), 0) == 0, jnp.sum(dadt * dtv, axis=0, keepdims=True), 0.0)

    cidx = lambda d, s: _chunk_index(d, nc - 1 - s, nc, nctc)
    bb = 2 * _nbytes((CH, XBC), F32) + 2 * _nbytes((CH, HP), F32) + _nbytes((CH, SSD_IN), F32) + _nbytes((SSD_IN, NST), F32)
    return pl.pallas_call(
        functools.partial(body), name="ssd_scan_bwd", grid=(B, 2, nc),
        in_specs=[pl.BlockSpec((None, CH, SSD_IN), lambda b, d, s: (b, cidx(d, s), 0)),
                  pl.BlockSpec((None, CH, XBC), lambda b, d, s: (b, cidx(d, s), 0)),
                  pl.BlockSpec((None, None, CH, HP), lambda b, d, s: (b, d, cidx(d, s), 0)),
                  pl.BlockSpec((None, 8, HP), lambda b, d, s: (d, 0, 0)),
                  pl.BlockSpec((None, None, None, SSD_IN, NST), lambda b, d, s: (b, d, cidx(d, s), 0, 0))],
        out_specs=[pl.BlockSpec((None, None, CH, XBC), lambda b, d, s: (b, d, cidx(d, s), 0)),
                   pl.BlockSpec((None, None, CH, HP), lambda b, d, s: (b, d, cidx(d, s), 0)),
                   pl.BlockSpec((None, None, 8, HP), lambda b, d, s: (b, d, 0, 0))],
        out_shape=[_sds((B, 2, T, XBC), F32), _sds((B, 2, T, HP), F32), _sds((B, 2, 8, HP), F32)],
        scratch_shapes=[pltpu.VMEM((SSD_IN, NST), F32)],
        compiler_params=_params(("arbitrary",) * 3, bb, 8 << 20),
    )(dyt, xbc, dt2, alog2, hin)


def _ssd_prep_bwd(R, dxbc2, dxsk, ddt2, xbc_raw, dt_raw, conv_w8, conv_b, dtb):
    B, T = R.B, R.T

    def body(dx2_ref, dsk_ref, ddt_ref, raw_ref, pv_ref, nx_ref, dtr_ref, w_ref, b_ref, dtb_ref,
             dpre_ref, ddtr_ref, dcw_ref, dvec_ref, ddtb_ref):
        b, i = pl.program_id(0), pl.program_id(1)
        ext = R.ext(i, pv_ref, raw_ref[...], nx_ref)
        pre = _conv_pre(ext, w_ref, b_ref)
        dxbc = dx2_ref[0] + dx2_ref[1]
        lane = _iota((TM, XBC), 1)
        skip = jnp.concatenate([dsk_ref[...], jnp.zeros((TM, XBC - SSD_IN), F32)], axis=1)
        dpre = (dxbc + skip) * _dsilu(pre)
        dpre_ref[...] = dpre
        first = jnp.logical_and(b == 0, i == 0)
        taps = {k: jnp.sum(dpre * _shift_rows(ext, k - 1)[8:8 + TM], axis=0, keepdims=True) for k in range(4)}
        _acc_rows(dcw_ref, first, taps)
        _acc_rows(dvec_ref, first, {0: jnp.sum(dpre, axis=0, keepdims=True)})
        ddt = ddt_ref[0] + pltpu.roll(ddt_ref[1], H, axis=1)
        ddtr = ddt * _sigmoid(dtr_ref[...] + dtb_ref[...])
        ddtr = jnp.where(_iota((TM, HP), 1) < 2 * H, ddtr, 0.0)
        ddtr_ref[...] = ddtr.astype(BF)
        _acc_rows(ddtb_ref, first, {0: jnp.sum(ddtr, axis=0, keepdims=True)})
        del lane

    return R.call("ssd_prep_bwd", body,
                  [(dxbc2, R.row2(XBC)), (dxsk, R.row(SSD_IN)), (ddt2, R.row2(HP)), (xbc_raw, R.row(XBC)), (xbc_raw, R.prev8(XBC)),
                   (xbc_raw, R.next8(XBC)), (dt_raw, R.row(HP)), (conv_w8, R.const((8, XBC))), (conv_b, R.const((1, XBC))),
                   (dtb, R.const((1, HP)))],
                  [(_sds((B, T, XBC), F32), R.row(XBC)), (_sds((B, T, HP), BF), R.row(HP)), (_sds((8, XBC), F32), R.const((8, XBC))),
                   (_sds((8, XBC), F32), R.const((8, XBC))), (_sds((8, HP), F32), R.const((8, HP)))], extra=16 << 20)


def _in_bwd(R, dxm, x, mod, nw1, dz, dpre, dqa, dkva, dpool, dkr, ddtr, conv_w8, w_arr):
    B, T = R.B, R.T

    def body(dxm_ref, x_ref, mod_ref, nw_ref, dz_ref, dp_ref, dpp_ref, dpn_ref, dqa_ref, dkva_ref, dpool_ref, dkr_ref, ddt_ref,
             cw_ref, w_ref, dx_ref, dproj_ref, dmod_ref, dnw_ref):
        b, i = pl.program_id(0), pl.program_id(1)
        ext = R.ext(i, dpp_ref, dp_ref[...], dpn_ref)
        draw = (_row(cw_ref, 0) * _shift_rows(ext, 1)[8:8 + TM] + _row(cw_ref, 1) * ext[8:8 + TM]
                + _row(cw_ref, 2) * _shift_rows(ext, -1)[8:8 + TM] + _row(cw_ref, 3) * _shift_rows(ext, -2)[8:8 + TM])
        dproj_ref[:, A_Z:A_XBC] = dz_ref[...]
        dproj_ref[:, A_XBC:A_QA] = draw.astype(BF)
        dproj_ref[:, A_QA:A_KVA] = dqa_ref[...]
        dproj_ref[:, A_KVA:A_POOL] = dkva_ref[...]
        dproj_ref[:, A_POOL:A_KR] = dpool_ref[...]
        dproj_ref[:, A_KR:A_DT] = dkr_ref[...]
        dproj_ref[:, A_DT:PC] = ddt_ref[...]
        dh = _dot_nt(dproj_ref[...], w_ref[...])
        _, xn, r = _norm_mod(x_ref[...], nw_ref[...], _row(mod_ref, 0), _row(mod_ref, 1))
        dx, dsh, dsc, dnw = _norm_mod_bwd(dh, xn, r, nw_ref[...], _row(mod_ref, 1))
        dx_ref[...] = dxm_ref[...] + dx
        _acc_rows(dmod_ref, R.first_of_stream(i), {0: dsh, 1: dsc})
        _acc_rows(dnw_ref, jnp.logical_and(b == 0, i == 0), {0: dnw})

    return R.call("in_bwd", body,
                  [(dxm, R.row(D)), (x, R.row(D)), (mod, R.mod()), (nw1, R.const((1, D))), (dz, R.row(SSD_IN)), (dpre, R.row(XBC)),
                   (dpre, R.prev8(XBC)), (dpre, R.next8(XBC)), (dqa, R.row(QL)), (dkva, R.row(QL)), (dpool, R.row(PD)), (dkr, R.row(HP)),
                   (ddtr, R.row(HP)), (conv_w8, R.const((8, XBC))), (w_arr, R.const((D, PC)))],
                  [(_sds((B, T, D), F32), R.row(D)), (_sds((B, T, PC), BF), R.row(PC)), (_sds((B, 2, 8, D), F32), R.mod()),
                   (_sds((8, D), F32), R.const((8, D)))], extra=12 << 20)


def _adaln_fwd(cs, mod_w):
    L, _, C = mod_w.shape

    def body(c_ref, w_ref, o_ref):
        s = _silu(c_ref[...]).astype(BF)
        for l in range(L):
            o_ref[l] = jnp.dot(s, w_ref[l].astype(BF), preferred_element_type=F32)

    return pl.pallas_call(functools.partial(body), name="adaln_fwd", out_shape=_sds((L, 24, C), F32),
                          compiler_params=_params(None, _nbytes(mod_w.shape, F32) + _nbytes((L, 24, C), F32), 8 << 20))(cs, mod_w)


def _adaln_bwd(cs, dm, mod_w):
    L, _, C = mod_w.shape

    def body(c_ref, dm_ref, w_ref, gw_ref, gc_ref):
        c = c_ref[...]
        s = _silu(c).astype(BF)
        acc = jnp.zeros((24, D), F32)
        for l in range(L):
            dmb = dm_ref[l].astype(BF)
            gw_ref[l] = _dot_tn(s, dmb)
            acc = acc + _dot_nt(dmb, w_ref[l])
        gc_ref[...] = acc * _dsilu(c)

    return pl.pallas_call(functools.partial(body), name="adaln_bwd", out_shape=[_sds((L, D, C), F32), _sds((24, D), F32)],
                          compiler_params=_params(None, 2 * _nbytes(mod_w.shape, F32), 8 << 20))(cs, dm, mod_w)


def _adamw(name, parts, w, m, v, rt):
    Pn, Rr, C = parts.shape
    c1 = 1.0 - ADAM_B1 ** ADAM_STEP
    c2 = 1.0 - ADAM_B2 ** ADAM_STEP

    def body(p_ref, w_ref, m_ref, v_ref, g_ref, d_ref, nm_ref, nv_ref):
        g = p_ref[0].astype(F32)
        for k in range(1, Pn):
            g = g + p_ref[k].astype(F32)
        mn = ADAM_B1 * m_ref[...] + (1.0 - ADAM_B1) * g
        vn = ADAM_B2 * v_ref[...] + (1.0 - ADAM_B2) * jnp.square(g)
        g_ref[...] = g
        nm_ref[...] = mn
        nv_ref[...] = vn
        d_ref[...] = -ADAM_LR * ((mn / c1) / (jnp.sqrt(vn / c2) + ADAM_EPS) + ADAM_WD * w_ref[...])

    spec = pl.BlockSpec((rt, C), lambda i: (i, 0))
    bb = Pn * _nbytes((rt, C), parts.dtype) + 7 * _nbytes((rt, C), F32)
    return pl.pallas_call(
        functools.partial(body), name=name, grid=(Rr // rt,),
        in_specs=[pl.BlockSpec((Pn, rt, C), lambda i: (0, i, 0)), spec, spec, spec],
        out_specs=[spec] * 4, out_shape=[_sds((Rr, C), F32)] * 4,
        compiler_params=_params(("arbitrary",), bb, 4 << 20),
    )(parts, w, m, v)


MESH = pl.DeviceIdType.MESH


def _my_pos():
    return lax.axis_index("x"), lax.axis_index("y"), lax.axis_index("c")


def _dev_index(x, y, c):
    return 4 * x + 2 * y + c


def _all_gather(name, shards):
    n = len(shards)

    def body(*refs):
        ins, outs = refs[:n], refs[n:2 * n]
        send_sems, recv_sems, local_sem = refs[2 * n:]
        x, y, c = _my_pos()
        me, sibling = (x, y, c), (x, y, 1 - c)
        chips = [(1 - x, y), (x, 1 - y), (1 - x, 1 - y)]

        def copy(t, k, block, to, src=None):
            slot = outs[t].at[_dev_index(*block)]
            return pltpu.make_async_remote_copy(
                src_ref=slot if src is None else src, dst_ref=slot,
                send_sem=send_sems.at[t, k], recv_sem=recv_sems.at[t, k], device_id=to, device_id_type=MESH)

        mine = [pltpu.make_async_copy(ins[t], outs[t].at[_dev_index(*me)], local_sem.at[t]) for t in range(n)]
        for cp in mine:
            cp.start()
        first = []
        for t in range(n):
            first.append(copy(t, 0, me, sibling, src=ins[t]))
            first += [copy(t, 1 + j, me, (*chip, c), src=ins[t]) for j, chip in enumerate(chips)]
        for cp in first:
            cp.start()
        passed = []
        for j, chip in enumerate(chips):
            for t in range(n):
                copy(t, 1 + j, (*chip, c), me).wait_recv()
                cp = copy(t, 4 + j, (*chip, c), sibling)
                cp.start()
                passed.append(cp)
        for t in range(n):
            copy(t, 0, sibling, me).wait_recv()
            for j, chip in enumerate(chips):
                copy(t, 4 + j, (*chip, 1 - c), me).wait_recv()
        for cp in first + passed:
            cp.wait_send()
        for cp in mine:
            cp.wait()

    return pl.pallas_call(
        functools.partial(body), name=name,
        in_specs=[ANY] * n, out_specs=[ANY] * n,
        out_shape=[_sds((NDEV,) + s.shape, s.dtype) for s in shards],
        scratch_shapes=[pltpu.SemaphoreType.DMA((n, 7)), pltpu.SemaphoreType.DMA((n, 7)), pltpu.SemaphoreType.DMA((n,))],
    )(*shards)


def _all_to_all(name, parts):
    n = len(parts)

    def body(*refs):
        ins, outs = refs[:n], refs[n:2 * n]
        send_sems, recv_sems, local_sem = refs[2 * n:]
        x, y, c = _my_pos()
        me = _dev_index(x, y, c)
        peers = [(x ^ ((k >> 2) & 1), y ^ ((k >> 1) & 1), c ^ (k & 1)) for k in range(1, NDEV)]
        mine = [pltpu.make_async_copy(ins[t].at[me], outs[t].at[me], local_sem.at[t]) for t in range(n)]
        for cp in mine:
            cp.start()
        sends = []
        for t in range(n):
            for k, peer in enumerate(peers):
                cp = pltpu.make_async_remote_copy(
                    src_ref=ins[t].at[_dev_index(*peer)], dst_ref=outs[t].at[me],
                    send_sem=send_sems.at[t, k], recv_sem=recv_sems.at[t, k], device_id=peer, device_id_type=MESH)
                cp.start()
                sends.append(cp)
        for t in range(n):
            for k, peer in enumerate(peers):
                slot = outs[t].at[_dev_index(*peer)]
                pltpu.make_async_remote_copy(
                    src_ref=slot, dst_ref=slot, send_sem=send_sems.at[t, k], recv_sem=recv_sems.at[t, k],
                    device_id=peer, device_id_type=MESH).wait_recv()
        for cp in sends:
            cp.wait_send()
        for cp in mine:
            cp.wait()

    return pl.pallas_call(
        functools.partial(body), name=name,
        in_specs=[ANY] * n, out_specs=[ANY] * n,
        out_shape=[_sds(p.shape, p.dtype) for p in parts],
        scratch_shapes=[pltpu.SemaphoreType.DMA((n, 7)), pltpu.SemaphoreType.DMA((n, 7)), pltpu.SemaphoreType.DMA((n,))],
    )(*parts)


def _arrange_w_in(w):
    z = lambda n: jnp.zeros((w.shape[0], n), w.dtype)
    return jnp.concatenate([w[:, 0:1280], w[:, 1292:1548], w[:, 1548:1804], w[:, 1836:2092],
                            z(64), w[:, 1804:1836], z(32), w[:, 1280:1292], z(HP - 2 * H)], axis=1)


def _unarrange_w_in(g):
    return jnp.concatenate([g[:, 0:1280], g[:, A_DT:A_DT + 2 * H], g[:, A_QA:A_KVA], g[:, A_KVA:A_POOL],
                            g[:, A_KR + NOPE:A_KR + NOPE + ROPE], g[:, A_POOL:A_KR]], axis=1)


def _pad_heads(w, width):
    k = w.shape[0]
    return jnp.pad(w.reshape(k, H, width), ((0, 0), (0, 0), (0, HP - width))).reshape(k, H * HP)


def _unpad_heads(g, width):
    k = g.shape[0]
    return g.reshape(k, H, HP)[:, :, :width].reshape(k, H * width)


def _arrange_w_out(w):
    att = jnp.pad(w[SSD_IN:2 * SSD_IN].reshape(H, VH, D), ((0, 0), (0, HP - VH), (0, 0))).reshape(QW, D)
    return jnp.concatenate([w[0:SSD_IN], att, w[2 * SSD_IN:]], axis=0)


def _unarrange_w_out(g):
    att = g[SSD_IN:SSD_IN + QW].reshape(H, HP, D)[:, :VH].reshape(SSD_IN, D)
    return jnp.concatenate([g[0:SSD_IN], att, g[SSD_IN + QW:]], axis=0)


def _rope_tables(T):
    n = T - CTX
    t = jnp.arange(n, dtype=F32)
    row, col = jnp.floor(t / GRID_W), t % GRID_W
    pairs = ROPE // 4
    inv = ROPE_THETA ** (-jnp.arange(pairs, dtype=F32) / pairs)
    ar, ac = row[:, None] * inv, col[:, None] * inv
    cos = jnp.concatenate([jnp.cos(ar)] * 2 + [jnp.cos(ac)] * 2, axis=1)
    sin = jnp.concatenate([-jnp.sin(ar), jnp.sin(ar), -jnp.sin(ac), jnp.sin(ac)], axis=1)
    ones, zeros = jnp.ones((n, NOPE), F32), jnp.zeros((n, NOPE), F32)
    cos = jnp.concatenate([ones, cos, ones[:, :HP - QK]], axis=1)
    sin = jnp.concatenate([zeros, sin, zeros[:, :HP - QK]], axis=1)
    return (jnp.concatenate([jnp.ones((CTX, HP), F32), cos], axis=0),
            jnp.concatenate([jnp.zeros((CTX, HP), F32), sin], axis=0))


def _lane_pad(v, n):
    return jnp.pad(v, (0, n - v.shape[0]))[None, :]


def _layer_weights(w_in, w_out, w_q_b, w_kv_b, conv_w, pool_w):
    kv = w_kv_b.reshape(QL, H, NOPE + VH)
    wbd = jnp.zeros((PD, PD), F32)
    for g in range(4):
        wbd = lax.dynamic_update_slice(wbd, pool_w[g], (64 * g, 64 * g))
    return dict(
        w_in=_arrange_w_in(w_in).astype(BF), w_out=_arrange_w_out(w_out).astype(BF),
        wq=_pad_heads(w_q_b, QK).astype(BF),
        wk=_pad_heads(kv[:, :, :NOPE].reshape(QL, H * NOPE), NOPE).astype(BF),
        wv=_pad_heads(kv[:, :, NOPE:].reshape(QL, H * VH), VH).astype(BF),
        conv_w8=jnp.pad(conv_w, ((0, 4), (0, 0))), wbd=wbd)


def _layer_fwd(R, x, mod, lw, sp, w1, w2, cos, sin):
    B, T = R.B, R.T
    z, xbc_raw, qa, kva, pool_in, kr, dt_raw, h1 = _in_proj(R, x, mod, sp["nw1"], lw["w_in"])
    xbc, dt2 = _ssd_prep(R, xbc_raw, dt_raw, lw["conv_w8"], sp["conv_b"], sp["dtb"])
    y2, hin = _ssd_scan(B, T, xbc, dt2, sp["alog2"])
    qt, kt, vt, cq, ckv = _mla_prep(R, qa, kva, kr, sp["qnw"], sp["kvnw"], lw["wq"], lw["wk"], lw["wv"], cos, sin)
    q_t, k_t, v_t = _heads_t(qt), _heads_t(kt), _heads_t(vt)
    o, lse = _flash_fwd(B, T, q_t, kt, v_t)
    pool = _pool_fwd(R, pool_in, lw["wbd"], sp["pscale"])[0]
    xmid, cat, mixbf = _out_proj(R, x, mod, y2, xbc, z, o, pool, sp["dsk"], sp["snw"], lw["w_out"])
    xo, h2, ubf, ybf = _mlp_fwd(R, xmid, mod, sp["nw2"], w1, w2)
    saved = dict(x=x, z=z, xbc_raw=xbc_raw, qa=qa, kva=kva, pool_in=pool_in, dt_raw=dt_raw, h1=h1, xbc=xbc, dt2=dt2,
                 y2=y2, hin=hin, qt=qt, kt=kt, q_t=q_t, k_t=k_t, v_t=v_t, cq=cq, ckv=ckv, o=o, lse=lse, cat=cat, mixbf=mixbf,
                 xmid=xmid,
                 h2=h2, ubf=ubf, ybf=ybf)
    return xo, saved


def _layer_bwd(R, dxo, sv, mod, lw, sp, w1, w2, cos, sin):
    B, T = R.B, R.T
    nq = T // TQ
    dxm, du, abf, dyb, dmod_a, dnw2 = _mlp_bwd(R, dxo, sv["xmid"], sv["ubf"], sv["ybf"], mod, sp["nw2"], w1, w2)
    g_w1 = _tn_matmul("dw_mlp1", sv["h2"], du, FSH)
    g_w2 = _tn_matmul("dw_mlp2", abf, dyb, D)[0].reshape(NDEV, FSH, D)
    dmix, dz, dyt, dxsk, dob, dd, dpo, dmod_b, dvec_o = _out_bwd(R, dxm, mod, sv["mixbf"], sv["y2"], sv["xbc"], sv["z"], sv["o"],
                                                                 sp["dsk"], sp["snw"], lw["w_out"])
    g_wout = _unarrange_w_out(_tn_matmul("dw_out", sv["cat"], dmix, D)[0])
    dpool_in, g_wbd, dpsc = _pool_bwd(R, dpo, sv["pool_in"], lw["wbd"], sp["pscale"])
    dqt, dkt, dvt = _flash_bwd(B, T, sv["qt"], sv["kt"], dob, sv["q_t"], sv["k_t"], sv["v_t"], _heads_t(dob), sv["lse"], dd)
    dqp, dkb, dvb, dqa, dkva, dkr, dnw_qk = _mla_bwd(R, dqt, dkt, dvt, sv["qa"], sv["kva"], sp["qnw"], sp["kvnw"],
                                                     lw["wq"], lw["wk"], lw["wv"], cos, sin)
    g_wq = _unpad_heads(_tn_matmul("dw_q", sv["cq"], dqp, QW)[0], QK)
    g_wk = _unpad_heads(_tn_matmul("dw_k", sv["ckv"], dkb, QW)[0], NOPE).reshape(QL, H, NOPE)
    g_wv = _unpad_heads(_tn_matmul("dw_v", sv["ckv"], dvb, QW)[0], VH).reshape(QL, H, VH)
    g_wkv = jnp.concatenate([g_wk, g_wv], axis=2).reshape(QL, H * (NOPE + VH))
    dxbc2, ddt2, da = _ssd_scan_bwd(B, T, dyt, sv["xbc"], sv["dt2"], sp["alog2"], sv["hin"])
    dpre, ddtr, dcw, dcb, ddtb = _ssd_prep_bwd(R, dxbc2, dxsk, ddt2, sv["xbc_raw"], sv["dt_raw"], lw["conv_w8"],
                                                sp["conv_b"], sp["dtb"])
    dx, dproj, dmod_c, dnw1 = _in_bwd(R, dxm, sv["x"], mod, sp["nw1"], dz, dpre, dqa, dkva, dpool_in, dkr, ddtr,
                                      lw["conv_w8"], lw["w_in"])
    g_win = _unarrange_w_in(jnp.concatenate(list(_tn_matmul("dw_in", sv["h1"], dproj, PC // 3)), axis=1))
    a2 = -jnp.exp(sp["alog2"][:, 0, :H])
    small = dict(
        norm1_w=dnw1[0], norm2_w=dnw2[0], conv_w=dcw[0:4], conv_b=dcb[0], dt_bias=ddtb[0, :2 * H].reshape(2, H),
        a_log=jnp.sum(da[:, :, 0, :H], axis=0) * a2, ssd_d=jnp.sum(dvec_o[1].reshape(H, P), axis=1), ssd_norm_w=dvec_o[0],
        q_a_norm_w=dnw_qk[0], kv_a_norm_w=dnw_qk[1],
        pool_w=jnp.stack([g_wbd[64 * g:64 * (g + 1), 64 * g:64 * (g + 1)] for g in range(4)]), pool_scale=dpsc[0])
    big = dict(w_in=g_win, w_out=g_wout, w_q_b=g_wq, w_kv_b=g_wkv, w_mlp1=g_w1, w_mlp2=g_w2)
    return dx, big, small, dmod_a + dmod_b + dmod_c


def _small_params(l, norm1_w, norm2_w, conv_b, dt_bias, a_log, ssd_d, ssd_norm_w, q_a_norm_w, kv_a_norm_w, pool_scale):
    alog2 = jnp.broadcast_to(jnp.pad(a_log[l], ((0, 0), (0, HP - H)))[:, None, :], (2, 8, HP))
    return dict(nw1=norm1_w[l][None], nw2=norm2_w[l][None], conv_b=conv_b[l][None],
                dtb=_lane_pad(dt_bias[l].reshape(2 * H), HP), alog2=alog2,
                dsk=jnp.repeat(ssd_d[l], P)[None], snw=ssd_norm_w[l][None], qnw=q_a_norm_w[l][None],
                kvnw=kv_a_norm_w[l][None], pscale=pool_scale[l][None])


SMALL_NAMES = ["mod_b", "norm1_w", "norm2_w", "conv_b", "dt_bias", "a_log", "ssd_d", "ssd_norm_w", "q_a_norm_w",
               "kv_a_norm_w", "pool_w", "pool_scale", "final_norm_w"]


def _pack(arrs):
    rows = []
    for a in arrs:
        f = a.reshape(-1).astype(F32)
        n = -(-f.shape[0] // HP) * HP
        rows.append(jnp.pad(f, (0, n - f.shape[0])).reshape(-1, HP))
    out = jnp.concatenate(rows, axis=0)
    pad = (-out.shape[0]) % 8
    return jnp.pad(out, ((0, pad), (0, 0)))


def _unpack(pack, like):
    outs, r = [], 0
    for a in like:
        n = math.prod(a.shape)
        nr = -(-n // HP)
        outs.append(pack[r:r + nr].reshape(-1)[:n].reshape(a.shape))
        r += nr
    return outs


def _local_step(x, ctx, target, mods, full, small_w):
    B, N = x.shape[0], x.shape[1]
    T = CTX + N
    R = _Rows(B, T)
    cos, sin = _rope_tables(T)
    xu = jnp.concatenate([ctx, x], axis=1)
    L = len(mods)
    lws, sps, saves = [], [], []
    for l in range(L):
        f = full[l]
        lws.append(_layer_weights(f["w_in"], f["w_out"], f["w_q_b"], f["w_kv_b"], f["conv_w"], small_w["pool_w"][l]))
        sps.append(_small_params(l, *[small_w[k] for k in ["norm1_w", "norm2_w", "conv_b", "dt_bias", "a_log", "ssd_d",
                                                          "ssd_norm_w", "q_a_norm_w", "kv_a_norm_w", "pool_scale"]]))
        xu, sv = _layer_fwd(R, xu, mods[l], lws[l], sps[l], f["w_mlp1"], f["w_mlp2"], cos, sin)
        saves.append(sv)
    dx, loss8, dfnw = _loss_head(R, xu, target, small_w["final_norm_w"][None])
    bigs, smalls, dmods = [None] * L, [None] * L, [None] * L
    for l in reversed(range(L)):
        f = full[l]
        dx, bigs[l], smalls[l], dmods[l] = _layer_bwd(R, dx, saves[l], mods[l], lws[l], sps[l], f["w_mlp1"], f["w_mlp2"], cos, sin)
    return loss8[0, 0], dx[:, CTX:], bigs, smalls, dfnw[0], dmods


def kernel(x, c, ctx, c_ctx, mod_w, mod_b, norm1_w, norm2_w, w_in, conv_w, conv_b, dt_bias, a_log, ssd_d, ssd_norm_w, q_a_norm_w, w_q_b, kv_a_norm_w, w_kv_b, pool_w, pool_scale, w_out, w_mlp1, w_mlp2, final_norm_w, loss_target, m_c_ctx, m_mod_w, m_mod_b, m_norm1_w, m_norm2_w, m_w_in, m_conv_w, m_conv_b, m_dt_bias, m_a_log, m_ssd_d, m_ssd_norm_w, m_q_a_norm_w, m_w_q_b, m_kv_a_norm_w, m_w_kv_b, m_pool_w, m_pool_scale, m_w_out, m_w_mlp1, m_w_mlp2, m_final_norm_w, v_c_ctx, v_mod_w, v_mod_b, v_norm1_w, v_norm2_w, v_w_in, v_conv_w, v_conv_b, v_dt_bias, v_a_log, v_ssd_d, v_ssd_norm_w, v_q_a_norm_w, v_w_q_b, v_kv_a_norm_w, v_w_kv_b, v_pool_w, v_pool_scale, v_w_out, v_w_mlp1, v_w_mlp2, v_final_norm_w):
    args = dict(locals())
    B = x.shape[0]
    L = mod_w.shape[0]
    me = _dev_index(*_my_pos())
    CS = mod_w.shape[2]

    c_all, convw_all, win_g, wout_g, wq_g, wkv_g, w1_g, w2_g = _all_gather(
        "gather_weights",
        [c, conv_w, w_in.astype(BF), w_out.astype(BF), w_q_b.astype(BF), w_kv_b.astype(BF), w_mlp1.astype(BF), w_mlp2.astype(BF)])
    cs = jnp.concatenate([c_all.reshape(NDEV * B, D), c_ctx[None], jnp.zeros((24 - NDEV * B - 1, D), F32)], axis=0)
    m_loc = _adaln_fwd(cs, mod_w)
    m_all = _all_gather("gather_mod", [m_loc])[0]
    m_full = jnp.moveaxis(m_all, 0, 2).reshape(L, 24, NDEV * CS) + mod_b[:, None, :]
    mods = []
    for l in range(L):
        ex = lax.dynamic_slice(m_full[l], (me * B, 0), (B, 6 * D)).reshape(B, 6, D)
        cc = jnp.broadcast_to(m_full[l, NDEV * B].reshape(1, 6, D), (B, 6, D))
        mods.append(jnp.pad(jnp.stack([cc, ex], axis=1), ((0, 0), (0, 0), (0, 2), (0, 0))))
    full = []
    for l in range(L):
        full.append(dict(
            w_in=win_g[:, l].reshape(D, IN_COLS), w_out=wout_g[:, l].reshape(D, D),
            w_q_b=jnp.moveaxis(wq_g[:, l], 0, 1).reshape(QL, H * QK),
            w_kv_b=jnp.moveaxis(wkv_g[:, l], 0, 1).reshape(QL, H * (NOPE + VH)),
            conv_w=jnp.moveaxis(convw_all[:, l], 0, 1).reshape(4, XBC),
            w_mlp1=w1_g[:, l], w_mlp2=w2_g[:, l]))
    small_w = {k: args[k] for k in SMALL_NAMES if k != "mod_b"}

    loss_part, grad_x, bigs, smalls, dfnw, dmods = _local_step(x, ctx, loss_target, mods, full, small_w)
    loss = lax.psum(loss_part, ("x", "y", "c"))

    dm_ex = jnp.stack([dmods[l][:, 1, :6].reshape(B, 6 * D) for l in range(L)])
    dm_cc = jnp.stack([jnp.sum(dmods[l][:, 0, :6], axis=0).reshape(6 * D) for l in range(L)])
    small_parts = dict(
        mod_b=jnp.sum(dm_ex, axis=1) + dm_cc,
        **{k: jnp.stack([smalls[l][k] for l in range(L)]) for k in SMALL_NAMES[1:-1]},
        final_norm_w=dfnw, conv_w=jnp.stack([smalls[l]["conv_w"] for l in range(L)]), dm_cc=dm_cc)
    adam_grads = [small_parts[k] for k in SMALL_NAMES]
    extras = [small_parts["conv_w"], dm_cc, dm_ex]
    pack = jnp.concatenate([_pack(adam_grads), _pack(extras)], axis=0)
    pack_all = _all_gather("gather_small_grads", [pack])[0]
    wpack = _pack([args[k] for k in SMALL_NAMES])
    mpack = _pack([args["m_" + k] for k in SMALL_NAMES])
    vpack = _pack([args["v_" + k] for k in SMALL_NAMES])
    n_adam = wpack.shape[0]
    res_small = _adamw("adamw_small", pack_all[:, :n_adam], wpack, mpack, vpack, n_adam)
    small_out = [_unpack(r, [args[k] for k in SMALL_NAMES]) for r in res_small]
    ext_all = pack_all[:, n_adam:]
    ext_sum = ext_all[0]
    for k in range(1, NDEV):
        ext_sum = ext_sum + ext_all[k]
    g_conv_full, dm_cc_tot, _ = _unpack(ext_sum, extras)
    dm_ex_all = jnp.stack([_unpack(ext_all[k], extras)[2] for k in range(NDEV)], axis=1)
    dm_rows = jnp.concatenate([dm_ex_all.reshape(L, NDEV * B, 6 * D), dm_cc_tot[:, None, :],
                               jnp.zeros((L, 24 - NDEV * B - 1, 6 * D), F32)], axis=1)
    dm_loc = lax.dynamic_slice(dm_rows, (0, 0, me * CS), (L, 24, CS))
    g_modw, gc_part = _adaln_bwd(cs, dm_loc, mod_w)
    gc_all = _all_gather("gather_cctx_grad", [gc_part[NDEV * B:NDEV * B + 8]])[0]
    cpad = lambda a: jnp.pad(a[None], ((0, 7), (0, 0)))
    res_cc = _adamw("adamw_cctx", gc_all, cpad(c_ctx), cpad(m_c_ctx), cpad(v_c_ctx), 8)
    cc_out = [r[0] for r in res_cc]

    def stack_dev(name, fn):
        return jnp.stack([fn(bigs[l][name]) for l in range(L)], axis=1).astype(BF)

    sends = [
        stack_dev("w_in", lambda g: g.reshape(NDEV, D // NDEV, IN_COLS)),
        stack_dev("w_out", lambda g: g.reshape(NDEV, D // NDEV, D)),
        stack_dev("w_q_b", lambda g: jnp.moveaxis(g.reshape(QL, NDEV, -1), 1, 0)),
        stack_dev("w_kv_b", lambda g: jnp.moveaxis(g.reshape(QL, NDEV, -1), 1, 0)),
        stack_dev("w_mlp1", lambda g: g), stack_dev("w_mlp2", lambda g: g)]
    recvs = _all_to_all("exchange_grads", sends)
    big_names = ["w_in", "w_out", "w_q_b", "w_kv_b", "w_mlp1", "w_mlp2"]
    big_out = {}
    for name, rv in zip(big_names, recvs):
        w = args[name]
        Rr, C = math.prod(w.shape[:-1]), w.shape[-1]
        rt = Rr if Rr * C <= (1 << 18) else Rr // 8
        res = _adamw("adamw_" + name, rv.reshape(NDEV, Rr, C), w.reshape(Rr, C), args["m_" + name].reshape(Rr, C),
                     args["v_" + name].reshape(Rr, C), rt)
        big_out[name] = [r.reshape(w.shape) for r in res]
    res = _adamw("adamw_mod_w", g_modw.reshape(1, L * D, CS), mod_w.reshape(L * D, CS), m_mod_w.reshape(L * D, CS),
                 v_mod_w.reshape(L * D, CS), L * D // 8)
    big_out["mod_w"] = [r.reshape(mod_w.shape) for r in res]
    CW = conv_w.shape[2]
    g_conv = lax.dynamic_slice(g_conv_full, (0, 0, me * CW), (L, 4, CW))
    res = _adamw("adamw_conv_w", g_conv.reshape(1, L * 4, CW), conv_w.reshape(L * 4, CW), m_conv_w.reshape(L * 4, CW),
                 v_conv_w.reshape(L * 4, CW), L * 4)
    big_out["conv_w"] = [r.reshape(conv_w.shape) for r in res]

    weights = ["c_ctx", "mod_w", "mod_b", "norm1_w", "norm2_w", "w_in", "conv_w", "conv_b", "dt_bias", "a_log", "ssd_d",
               "ssd_norm_w", "q_a_norm_w", "w_q_b", "kv_a_norm_w", "w_kv_b", "pool_w", "pool_scale", "w_out", "w_mlp1",
               "w_mlp2", "final_norm_w"]
    outs = [loss, grad_x]
    for kind in range(4):
        for name in weights:
            if name == "c_ctx":
                outs.append(cc_out[kind])
            elif name in big_out:
                outs.append(big_out[name][kind])
            else:
                outs.append(small_out[kind][SMALL_NAMES.index(name)])
    return tuple(outs)
```

```python
import functools
import math

import jax
import jax.numpy as jnp
from jax import lax
from jax.experimental import pallas as pl
from jax.experimental.pallas import tpu as pltpu

F32 = jnp.float32
BF = jnp.bfloat16
MXU = BF

D = 1024
CTX = 256
GRID_W = 64
EPS = 1e-6
H = 6
P = 64
SSD_IN = 384
NST = 128
XBC = 896
CH = 128
QL = 256
NOPE = 64
ROPE = 32
VH = 64
QK = 96
HP = 128
QW = H * HP
PD = 256
FF = 4096
IN_COLS = 2092
ROPE_THETA = 10000.0
PC = 2304
A_Z, A_XBC, A_QA, A_KVA, A_POOL, A_KR, A_DT = 0, 384, 1280, 1536, 1792, 2048, 2176
MIXW = SSD_IN + QW + PD
NDEV = 8
FSH = FF // NDEV
TM = 256
TQ = 256
TK = 256
VMEM_CAP = 64 * 1024 * 1024
ADAM_LR, ADAM_B1, ADAM_B2, ADAM_EPS, ADAM_WD, ADAM_STEP = 0.001, 0.9, 0.999, 1e-08, 0.01, 10


def _nbytes(shape, dtype):
    n = 1
    for s in shape:
        if s is not None:
            n *= s
    return n * jnp.dtype(dtype).itemsize


def _params(sem, block_bytes, extra=0):
    lim = min(2 * block_bytes + extra + (8 << 20), VMEM_CAP - (6 << 20))
    return pltpu.CompilerParams(dimension_semantics=sem, vmem_limit_bytes=int(lim))


def _hbm(arrays):
    return [pltpu.with_memory_space_constraint(a, pltpu.HBM) for a in arrays]


def _dot(a, b):
    return jnp.dot(a.astype(MXU), b.astype(MXU), preferred_element_type=F32)


def _dot_nt(a, b):
    return lax.dot_general(a.astype(MXU), b.astype(MXU), (((1,), (1,)), ((), ())), preferred_element_type=F32)


def _dot_tn(a, b):
    return lax.dot_general(a.astype(MXU), b.astype(MXU), (((0,), (0,)), ((), ())), preferred_element_type=F32)


def _dot01(m01, x):
    b16 = jnp.bfloat16
    m = m01.astype(b16)
    hi = x.astype(b16)
    r1 = x - hi.astype(F32)
    mid = r1.astype(b16)
    lo = (r1 - mid.astype(F32)).astype(b16)
    f = lambda v: jnp.dot(m, v, preferred_element_type=F32)
    return f(hi) + f(mid) + f(lo)


def _sigmoid(x):
    return 1.0 / (1.0 + jnp.exp(-x))


def _silu(x):
    return x * _sigmoid(x)


def _dsilu(x):
    s = _sigmoid(x)
    return s * (1.0 + x * (1.0 - s))


def _iota(shape, dim):
    return lax.broadcasted_iota(jnp.int32, shape, dim)


def _row(ref, k):
    blk = ref[...]
    return jnp.sum(jnp.where(_iota(blk.shape, 0) == k, blk, 0.0), axis=0, keepdims=True)


def _shift_rows(x, k):
    n = x.shape[0]
    return pltpu.roll(x, (-k) % n, axis=0)


class _Rows:
    def __init__(self, B, T):
        self.B, self.T = B, T
        self.nt = T // TM
        self.nct = CTX // TM

    def row(self, F):
        return pl.BlockSpec((None, TM, F), lambda b, i: (b, i, 0))

    def row2(self, F):
        return pl.BlockSpec((None, 2, TM, F), lambda b, i: (b, 0, i, 0))

    def prev8(self, F):
        return pl.BlockSpec((None, 8, F), lambda b, i: (b, jnp.maximum(i * (TM // 8) - 1, 0), 0))

    def next8(self, F):
        last = self.T // 8 - 1
        return pl.BlockSpec((None, 8, F), lambda b, i: (b, jnp.minimum((i + 1) * (TM // 8), last), 0))

    def mod(self):
        nct = self.nct
        return pl.BlockSpec((None, None, 8, D), lambda b, i: (b, jnp.where(i < nct, 0, 1), 0, 0))

    def const(self, shape):
        z = (0,) * len(shape)
        return pl.BlockSpec(tuple(shape), lambda b, i: z)

    def tgt(self, F):
        nct = self.nct
        return pl.BlockSpec((None, TM, F), lambda b, i: (b, jnp.maximum(i - nct, 0), 0))

    def call(self, name, body, ins, outs, scratch=(), extra=0):
        arrays = [a for a, _ in ins]
        in_specs = [s for _, s in ins]
        out_shape = [pltpu.HBM(o.shape, o.dtype) for o, _ in outs]
        out_specs = [s for _, s in outs]
        bb = 0
        for a, s in list(ins) + list(outs):
            if s.block_shape is not None:
                bb += _nbytes(s.block_shape, a.dtype)
        return pl.pallas_call(
            functools.partial(body), name=name, grid=(self.B, self.nt),
            in_specs=in_specs, out_specs=out_specs, out_shape=out_shape, scratch_shapes=list(scratch),
            compiler_params=_params(("arbitrary", "arbitrary"), bb, extra),
        )(*_hbm(arrays))

    def first_of_stream(self, i):
        return jnp.logical_or(i == 0, i == self.nct)

    def last_of_stream(self, i):
        return jnp.logical_or(i == self.nct - 1, i == self.nt - 1)

    def ext(self, i, prev_ref, cur, next_ref):
        pv = prev_ref[...].astype(F32) * jnp.where(self.first_of_stream(i), 0.0, 1.0)
        nx = next_ref[...].astype(F32) * jnp.where(self.last_of_stream(i), 0.0, 1.0)
        return jnp.concatenate([pv, cur, nx], axis=0)

    def stream_pos(self, i, rows):
        start = jnp.where(i < self.nct, 0, CTX)
        n = jnp.where(i < self.nct, CTX, self.T - CTX)
        t = i * TM - 8 - start + _iota((rows, 1), 0)
        return t, n


def _sds(shape, dtype):
    return jax.ShapeDtypeStruct(tuple(shape), dtype)


def _out(shape, dtype):
    return pltpu.HBM(tuple(shape), dtype)


def _norm_mod(x, nw, sh, sc):
    r = lax.rsqrt(jnp.mean(x * x, axis=-1, keepdims=True) + EPS)
    xn = x * r
    return xn * nw * (1.0 + sc) + sh, xn, r


def _norm_mod_bwd(dh, xn, r, nw, sc):
    dsh = jnp.sum(dh, axis=0, keepdims=True)
    dsc = jnp.sum(dh * (xn * nw), axis=0, keepdims=True)
    dnw = jnp.sum(dh * (1.0 + sc) * xn, axis=0, keepdims=True)
    dxn = dh * nw * (1.0 + sc)
    dx = r * (dxn - xn * jnp.mean(dxn * xn, axis=-1, keepdims=True))
    return dx, dsh, dsc, dnw


def _acc_rows(ref, first, rows):
    rid = _iota(ref.shape, 0)
    upd = jnp.zeros(ref.shape, F32)
    for k, v in rows.items():
        upd = upd + jnp.where(rid == k, v, 0.0)

    @pl.when(first)
    def _():
        ref[...] = upd

    @pl.when(jnp.logical_not(first))
    def _():
        ref[...] += upd


def _in_proj(R, x, mod, nw1, w_arr):
    B, T = R.B, R.T

    def body(x_ref, mod_ref, nw_ref, w_ref, z_ref, xbc_ref, qa_ref, kva_ref, pool_ref, kr_ref, dt_ref, h_ref):
        h, _, _ = _norm_mod(x_ref[...], nw_ref[...], _row(mod_ref, 0), _row(mod_ref, 1))
        hb = h.astype(BF)
        h_ref[...] = hb
        p = jnp.dot(hb, w_ref[...], preferred_element_type=F32)
        z_ref[...] = p[:, A_Z:A_XBC]
        xbc_ref[...] = p[:, A_XBC:A_QA]
        qa_ref[...] = p[:, A_QA:A_KVA]
        kva_ref[...] = p[:, A_KVA:A_POOL]
        pool_ref[...] = p[:, A_POOL:A_KR]
        kr_ref[...] = p[:, A_KR:A_DT]
        dt_ref[...] = p[:, A_DT:PC]

    widths = [SSD_IN, XBC, QL, QL, PD, HP, HP]
    outs = [(_sds((B, T, w), F32), R.row(w)) for w in widths] + [(_sds((B, T, D), BF), R.row(D))]
    return R.call("in_proj", body,
                  [(x, R.row(D)), (mod, R.mod()), (nw1, R.const((1, D))), (w_arr, R.const((D, PC)))],
                  outs, extra=8 << 20)


def _conv_pre(ext, w_ref, b_ref):
    return (_row(w_ref, 0) * _shift_rows(ext, -1)[8:8 + TM] + _row(w_ref, 1) * ext[8:8 + TM]
            + _row(w_ref, 2) * _shift_rows(ext, 1)[8:8 + TM] + _row(w_ref, 3) * _shift_rows(ext, 2)[8:8 + TM]
            + b_ref[...])


def _softplus(x):
    return jnp.maximum(x, 0.0) + jnp.log(1.0 + jnp.exp(-jnp.abs(x)))


def _ssd_prep(R, xbc_raw, dt_raw, conv_w8, conv_b, dtb):
    B, T = R.B, R.T

    def body(raw_ref, pv_ref, nx_ref, dtr_ref, w_ref, b_ref, dtb_ref, xbc_ref, dt_ref):
        i = pl.program_id(1)
        ext = R.ext(i, pv_ref, raw_ref[...], nx_ref)
        xbc_ref[...] = _silu(_conv_pre(ext, w_ref, b_ref))
        lane = _iota((TM, HP), 1)
        dtv = _softplus(dtr_ref[...] + dtb_ref[...])
        keep = lane < H
        dt_ref[0] = jnp.where(keep, dtv, 0.0)
        dt_ref[1] = jnp.where(keep, pltpu.roll(dtv, HP - H, axis=1), 0.0)

    return R.call("ssd_prep", body,
                  [(xbc_raw, R.row(XBC)), (xbc_raw, R.prev8(XBC)), (xbc_raw, R.next8(XBC)), (dt_raw, R.row(HP)),
                   (conv_w8, R.const((8, XBC))), (conv_b, R.const((1, XBC))), (dtb, R.const((1, HP)))],
                  [(_sds((B, T, XBC), F32), R.row(XBC)), (_sds((B, 2, T, HP), F32), R.row2(HP))], extra=12 << 20)


def _chunk_index(d, s, nc, nctc):
    if d == 0:
        return s
    return jnp.where(s < nctc, nctc - 1 - s, nc - 1 - (s - nctc))


def _scan_common(d, dtv, alog_ref):
    blk = alog_ref[d]
    a = -jnp.exp(jnp.sum(jnp.where(_iota(blk.shape, 0) == 0, blk, 0.0), axis=0, keepdims=True))
    adt = dtv * a
    row = _iota((CH, CH), 0)
    col = _iota((CH, CH), 1)
    inc = col <= row if d == 0 else col >= row
    inc_t = row <= col if d == 0 else row >= col
    q = _dot01(inc, adt)
    return a, adt, inc, inc_t, q


def _head_cols(h, adt, q, dtv, inc_t):
    onehot = (_iota((1, HP), 1) == h).astype(F32)
    adt_h = jnp.sum(adt * onehot, axis=1, keepdims=True)
    qc = jnp.sum(q * onehot, axis=1, keepdims=True)
    dt_h = jnp.sum(dtv * onehot, axis=1, keepdims=True)
    qr = jnp.sum(adt_h * inc_t.astype(F32), axis=0, keepdims=True)
    qtot = jnp.sum(adt_h, axis=0, keepdims=True)
    return onehot, qc, qr, qtot, dt_h


def _ssd_scan(B, T, xbc, dt2, alog2):
    nc, nctc = T // CH, CTX // CH

    def chain(d, xbc_ref, dt_ref, alog_ref, y_ref, hin_ref, hs):
        xbc_v = xbc_ref[...]
        dtv = dt_ref[...]
        _, adt, inc, inc_t, q = _scan_common(d, dtv, alog_ref)
        hin_ref[...] = hs[d]
        for g in range(2):
            bg = xbc_v[:, SSD_IN + NST * g:SSD_IN + NST * (g + 1)]
            cg = xbc_v[:, SSD_IN + 2 * NST + NST * g:SSD_IN + 2 * NST + NST * (g + 1)]
            gm = _dot_nt(cg, bg)
            for r in range(3):
                h = 3 * g + r
                _, qc, qr, qtot, dt_h = _head_cols(h, adt, q, dtv, inc_t)
                lm = jnp.where(inc, jnp.exp(qc - qr), 0.0)
                xh = xbc_v[:, P * h:P * (h + 1)] * dt_h
                hprev = hs[d, P * h:P * (h + 1), :]
                y = _dot(gm * lm, xh) + jnp.exp(qc) * _dot_nt(cg, hprev)
                y_ref[:, P * h:P * (h + 1)] = y
                hs[d, P * h:P * (h + 1), :] = jnp.exp(qtot) * hprev + _dot_tn(xh * jnp.exp(qtot - qc), bg)

    def body(xf_ref, xb_ref, dtf_ref, dtb_ref, alog_ref, yf_ref, yb_ref, hf_ref, hb_ref, hs):
        @pl.when(pl.program_id(1) == 0)
        def _():
            hs[...] = jnp.zeros(hs.shape, F32)

        chain(0, xf_ref, dtf_ref, alog_ref, yf_ref, hf_ref, hs)
        chain(1, xb_ref, dtb_ref, alog_ref, yb_ref, hb_ref, hs)

    cidx = lambda d: (lambda s: _chunk_index(d, s, nc, nctc))
    specs = lambda d: dict(
        xbc=pl.BlockSpec((None, CH, XBC), lambda b, s: (b, cidx(d)(s), 0)),
        dt=pl.BlockSpec((None, None, CH, HP), lambda b, s: (b, d, cidx(d)(s), 0)),
        y=pl.BlockSpec((None, CH, SSD_IN), lambda b, s: (b, cidx(d)(s), 0)),
        h=pl.BlockSpec((None, None, SSD_IN, NST), lambda b, s: (b, cidx(d)(s), 0, 0)))
    f, r = specs(0), specs(1)
    bb = 2 * (_nbytes((CH, XBC), F32) + _nbytes((CH, HP), F32) + _nbytes((CH, SSD_IN), F32) + _nbytes((SSD_IN, NST), F32))
    return pl.pallas_call(
        functools.partial(body), name="ssd_scan", grid=(B, nc),
        in_specs=[f["xbc"], r["xbc"], f["dt"], r["dt"], pl.BlockSpec((2, 8, HP), lambda b, s: (0, 0, 0))],
        out_specs=[f["y"], r["y"], f["h"], r["h"]],
        out_shape=[_out((B, T, SSD_IN), F32)] * 2 + [_out((B, nc, SSD_IN, NST), F32)] * 2,
        scratch_shapes=[pltpu.VMEM((2, SSD_IN, NST), F32)],
        compiler_params=_params(("arbitrary",) * 2, bb, 8 << 20),
    )(*_hbm([xbc, xbc, dt2, dt2, alog2]))


def _swap8(u):
    lane = _iota(u.shape, 1)
    n = u.shape[1]
    return jnp.where((lane & 15) < 8, pltpu.roll(u, n - 8, axis=1), pltpu.roll(u, 8, axis=1))


def _rope(u, cos, sin_signed):
    return u * cos + _swap8(u) * sin_signed


def _rms(x, w):
    r = lax.rsqrt(jnp.mean(x * x, axis=-1, keepdims=True) + EPS)
    xh = x * r
    return xh * w, xh, r


def _rms_bwd(dy, xh, r, w):
    dw = jnp.sum(dy * xh, axis=0, keepdims=True)
    dxh = dy * w
    return r * (dxh - xh * jnp.mean(dxh * xh, axis=-1, keepdims=True)), dw


def _tile6(t):
    return jnp.concatenate([t] * H, axis=1)


def _per_head(fn, u):
    return jnp.concatenate([fn(u[:, HP * h:HP * (h + 1)]) for h in range(H)], axis=1)


def _mla_prep(R, qa, kva, kr, qnw, kvnw, wq, wk, wv, cos, sin):
    B, T = R.B, R.T
    scale = QK ** -0.5

    def body(qa_ref, kva_ref, kr_ref, qnw_ref, kvnw_ref, wq_ref, wk_ref, wv_ref, cos_ref, sin_ref,
             qt_ref, kt_ref, qtr_ref, ktr_ref, vtr_ref, cq_ref, ckv_ref):
        cq, _, _ = _rms(qa_ref[...], qnw_ref[...])
        ckv, _, _ = _rms(kva_ref[...], kvnw_ref[...])
        cqb, ckvb = cq.astype(BF), ckv.astype(BF)
        cq_ref[...] = cqb
        ckv_ref[...] = ckvb
        cos1, sin1 = cos_ref[...], sin_ref[...]
        q = _per_head(lambda u: _rope(u, cos1, sin1), jnp.dot(cqb, wq_ref[...], preferred_element_type=F32)) * scale
        qt_ref[...] = q.astype(BF)
        kk = _rope(kr_ref[...], cos1, sin1)
        k = jnp.dot(ckvb, wk_ref[...], preferred_element_type=F32) + _tile6(kk)
        kt_ref[...] = k.astype(BF)
        v = jnp.dot(ckvb, wv_ref[...], preferred_element_type=F32)
        v = jnp.where((_iota((TM, QW), 1) & (HP - 1)) == VH, 1.0, v)
        for h in range(H):
            cols = slice(HP * h, HP * (h + 1))
            qtr_ref[h] = q[:, cols].T.astype(BF)
            ktr_ref[h] = k[:, cols].T.astype(BF)
            vtr_ref[h] = v[:, cols].T.astype(BF)

    tr = (_sds((B, H, HP, T), BF), pl.BlockSpec((None, H, HP, TM), lambda b, i: (b, 0, 0, i)))
    return R.call("mla_prep", body,
                  [(qa, R.row(QL)), (kva, R.row(QL)), (kr, R.row(HP)), (qnw, R.const((1, QL))), (kvnw, R.const((1, QL))),
                   (wq, R.const((QL, QW))), (wk, R.const((QL, QW))), (wv, R.const((QL, QW))),
                   (cos, pl.BlockSpec((TM, HP), lambda b, i: (i, 0))), (sin, pl.BlockSpec((TM, HP), lambda b, i: (i, 0)))],
                  [(_sds((B, T, QW), BF), R.row(QW))] * 2 + [tr] * 3 + [(_sds((B, T, QL), BF), R.row(QL))] * 2, extra=12 << 20)


def _flash_fwd(B, T, q_t, kt, v_t):
    nq, nk = T // TQ, T // TK

    def body(q_ref, k_ref, v_ref, o_ref, lse_ref, s_scr):
        i = pl.program_id(2)
        q_tr = q_ref[...]

        def attend(nch):
            mrun = None
            for j in range(nch):
                s = _dot(k_ref[TK * j:TK * (j + 1), :], q_tr)
                s_scr[j] = s
                mrun = s if mrun is None else jnp.maximum(mrun, s)
            m = jnp.max(mrun, axis=0, keepdims=True)
            acc_t = jnp.zeros((HP, TQ), F32)
            for j in range(nch):
                acc_t = acc_t + _dot(v_ref[:, TK * j:TK * (j + 1)], jnp.exp(s_scr[j] - m))
            row = _iota((HP, TQ), 0)
            l = jnp.sum(jnp.where(row == VH, acc_t, 0.0), axis=0, keepdims=True)
            o_ref[...] = jnp.where(row < VH, acc_t / l, 0.0).T
            lse_ref[...] = jnp.broadcast_to(m + jnp.log(l), (HP, TQ)).T

        @pl.when(i < CTX // TQ)
        def _():
            attend(CTX // TK)

        @pl.when(i >= CTX // TQ)
        def _():
            attend(nk)

    bb = _nbytes((TQ, HP), BF) + 2 * _nbytes((T, HP), BF) + _nbytes((TQ, HP), F32) + _nbytes((8, TQ), F32)
    return pl.pallas_call(
        functools.partial(body), name="flash_fwd", grid=(B, H, nq),
        in_specs=[pl.BlockSpec((None, None, HP, TQ), lambda b, h, i: (b, h, 0, i)),
                  pl.BlockSpec((None, T, HP), lambda b, h, i: (b, 0, h)),
                  pl.BlockSpec((None, None, HP, T), lambda b, h, i: (b, h, 0, 0))],
        out_specs=[pl.BlockSpec((None, TQ, HP), lambda b, h, i: (b, i, h)),
                   pl.BlockSpec((None, None, TQ, HP), lambda b, h, i: (b, h, i, 0))],
        out_shape=[_out((B, T, QW), F32), _out((B, H, T, HP), F32)],
        scratch_shapes=[pltpu.VMEM((nk, TK, TQ), F32)],
        compiler_params=_params(("arbitrary",) * 3, bb, _nbytes((nk, TK, TQ), F32) + (8 << 20)),
    )(*_hbm([q_t, kt, v_t]))


def _pool_terms(R, i, rows):
    t, n = R.stream_pos(i, rows)
    lane = _iota((1, PD), 1)
    half = jnp.where(lane < 64, 1, jnp.where(lane < 128, 2, jnp.where(lane < 192, 4, 8)))
    cnt = (jnp.minimum(t + half, n) - jnp.maximum(t - half, 0)).astype(F32)
    valid = jnp.logical_and(t >= 0, t < n)
    return jnp.where(valid, cnt, 1.0), valid.astype(F32), lane


def _lane_select(lane, a2, a4, a8, a16):
    return jnp.where(lane < 64, a2, jnp.where(lane < 128, a4, jnp.where(lane < 192, a8, a16)))


def _pool_centred(R, i, ext):
    cnt, valid, lane = _pool_terms(R, i, ext.shape[0])
    s2 = ext + _shift_rows(ext, -1)
    s4 = _shift_rows(s2, -1) + _shift_rows(s2, 1)
    s8 = _shift_rows(s4, -2) + _shift_rows(s4, 2)
    s16 = _shift_rows(s8, -4) + _shift_rows(s8, 4)
    return _lane_select(lane, s2, s4, s8, s16) / cnt - ext, cnt, valid, lane


def _pool_fwd(R, u, wbd, scale):
    B, T = R.B, R.T

    def body(u_ref, pv_ref, nx_ref, w_ref, sc_ref, o_ref):
        i = pl.program_id(1)
        ext = R.ext(i, pv_ref, u_ref[...], nx_ref)
        dm, _, _, _ = _pool_centred(R, i, ext)
        o_ref[...] = _dot(dm[8:8 + TM], w_ref[...]) * sc_ref[...]

    return R.call("pool_fwd", body,
                  [(u, R.row(PD)), (u, R.prev8(PD)), (u, R.next8(PD)), (wbd, R.const((PD, PD))), (scale, R.const((1, PD)))],
                  [(_sds((B, T, PD), F32), R.row(PD))], extra=8 << 20)


def _group_mask():
    return _iota((1, SSD_IN), 1) < SSD_IN // 2


def _ssd_gate(yf_ref, yb_ref, xbc_ref, z_ref, dsk_ref):
    ytot = yf_ref[...] + yb_ref[...] + xbc_ref[:, 0:SSD_IN] * dsk_ref[...]
    z = z_ref[...]
    gz = ytot * _silu(z)
    g0 = _group_mask()
    sq = gz * gz
    s0 = jnp.sum(jnp.where(g0, sq, 0.0), axis=1, keepdims=True)
    s1 = jnp.sum(jnp.where(g0, 0.0, sq), axis=1, keepdims=True)
    half = SSD_IN // 2
    r = jnp.where(g0, lax.rsqrt(s0 / half + EPS), lax.rsqrt(s1 / half + EPS))
    return ytot, z, gz, r


def _out_proj(R, x, mod, yf, yb, xbc, z, o, pool, dsk, snw, wout):
    B, T = R.B, R.T

    def body(x_ref, mod_ref, yf_ref, yb_ref, xbc_ref, z_ref, o_ref, pool_ref, dsk_ref, snw_ref, w_ref, xmid_ref, cat_ref,
             mix_ref):
        _, _, gz, r = _ssd_gate(yf_ref, yb_ref, xbc_ref, z_ref, dsk_ref)
        cat_ref[:, 0:SSD_IN] = (gz * r * snw_ref[...]).astype(BF)
        cat_ref[:, SSD_IN:SSD_IN + QW] = o_ref[...].astype(BF)
        cat_ref[:, SSD_IN + QW:MIXW] = pool_ref[...].astype(BF)
        mix = jnp.dot(cat_ref[...], w_ref[...], preferred_element_type=F32)
        mix_ref[...] = mix.astype(BF)
        xmid_ref[...] = x_ref[...] + _row(mod_ref, 2) * mix

    return R.call("out_proj", body,
                  [(x, R.row(D)), (mod, R.mod()), (yf, R.row(SSD_IN)), (yb, R.row(SSD_IN)), (xbc, R.row(XBC)), (z, R.row(SSD_IN)),
                   (o, R.row(QW)),
                   (pool, R.row(PD)), (dsk, R.const((1, SSD_IN))), (snw, R.const((1, SSD_IN))), (wout, R.const((MIXW, D)))],
                  [(_sds((B, T, D), F32), R.row(D)), (_sds((B, T, MIXW), BF), R.row(MIXW)), (_sds((B, T, D), BF), R.row(D))],
                  extra=8 << 20)


def _load_once(first, pairs, sem):
    @pl.when(first)
    def _():
        cps = [pltpu.make_async_copy(src, dst, sem.at[k]) for k, (src, dst) in enumerate(pairs)]
        for cp in cps:
            cp.start()
        for cp in cps:
            cp.wait()


ANY = pl.BlockSpec(memory_space=pl.ANY)


def _mlp_fwd(R, xmid, mod, nw2, w1, w2):
    B, T = R.B, R.T

    def body(x_ref, mod_ref, nw_ref, w1_hbm, w2_hbm, xo_ref, h_ref, u_ref, y_ref, w1_v, w2_v, sem):
        first = jnp.logical_and(pl.program_id(0) == 0, pl.program_id(1) == 0)
        _load_once(first, [(w1_hbm, w1_v), (w2_hbm, w2_v)], sem)
        x = x_ref[...]
        h, _, _ = _norm_mod(x, nw_ref[...], _row(mod_ref, 3), _row(mod_ref, 4))
        hb = h.astype(BF)
        h_ref[...] = hb
        y = jnp.zeros((TM, D), F32)
        for j in range(NDEV):
            u = jnp.dot(hb, w1_v[j], preferred_element_type=F32)
            u_ref[:, FSH * j:FSH * (j + 1)] = u.astype(BF)
            a = jnp.square(jnp.maximum(u, 0.0))
            y = y + jnp.dot(a.astype(BF), w2_v[j], preferred_element_type=F32)
        y_ref[...] = y.astype(BF)
        xo_ref[...] = x + _row(mod_ref, 5) * y

    return R.call("mlp_fwd", body,
                  [(xmid, R.row(D)), (mod, R.mod()), (nw2, R.const((1, D))), (w1, ANY), (w2, ANY)],
                  [(_sds((B, T, D), F32), R.row(D)), (_sds((B, T, D), BF), R.row(D)), (_sds((B, T, FF), BF), R.row(FF)),
                   (_sds((B, T, D), BF), R.row(D))],
                  scratch=[pltpu.VMEM((NDEV, D, FSH), w1.dtype), pltpu.VMEM((NDEV, FSH, D), w2.dtype), pltpu.SemaphoreType.DMA((2,))],
                  extra=(2 * _nbytes((NDEV, D, FSH), BF)) + (8 << 20))


def _loss_head(R, x, tgt, fnw):
    B, T = R.B, R.T

    def body(x_ref, t_ref, w_ref, dx_ref, loss_ref, dw_ref):
        b, i = pl.program_id(0), pl.program_id(1)
        live = jnp.where(i >= R.nct, 1.0, 0.0)
        y, xh, r = _rms(x_ref[...], w_ref[...])
        err = (y - t_ref[...]) * live
        dy = err / D
        dxn, dw = _rms_bwd(dy, xh, r, w_ref[...])
        dx_ref[...] = dxn
        first = jnp.logical_and(b == 0, i == 0)
        part = 0.5 * jnp.sum(jnp.sum(err * err, axis=1, keepdims=True), axis=0, keepdims=True) / D
        _acc_rows(loss_ref, first, {0: jnp.broadcast_to(part, (1, HP))})
        _acc_rows(dw_ref, first, {0: dw})

    return R.call("loss_head", body,
                  [(x, R.row(D)), (tgt, R.tgt(D)), (fnw, R.const((1, D)))],
                  [(_sds((B, T, D), F32), R.row(D)), (_sds((8, HP), F32), R.const((8, HP))), (_sds((8, D), F32), R.const((8, D)))],
                  extra=8 << 20)


def _tn_matmul(name, a, b, tn):
    B, T, K = a.shape
    N = b.shape[2]
    nk = 1
    while T % nk or (T // nk) > 1088 or (T // nk) % 16:
        nk += 1
    tk = T // nk
    kt = K if K <= 1536 else 1024
    assert K % kt == 0 and N % tn == 0

    def body(a_ref, b_ref, o_ref):
        first = jnp.logical_and(pl.program_id(2) == 0, pl.program_id(3) == 0)

        @pl.when(first)
        def _():
            o_ref[...] = jnp.zeros(o_ref.shape, F32)

        o_ref[...] += _dot_tn(a_ref[...], b_ref[...])

    bb = _nbytes((tk, kt), a.dtype) + _nbytes((tk, tn), b.dtype) + _nbytes((kt, tn), F32)
    return pl.pallas_call(
        functools.partial(body), name=name, grid=(N // tn, K // kt, B, nk),
        in_specs=[pl.BlockSpec((None, tk, kt), lambda j, kk, bi, t: (bi, t, kk)),
                  pl.BlockSpec((None, tk, tn), lambda j, kk, bi, t: (bi, t, j))],
        out_specs=pl.BlockSpec((None, kt, tn), lambda j, kk, bi, t: (j, kk, 0)),
        out_shape=_out((N // tn, K, tn), F32),
        compiler_params=_params(("arbitrary",) * 4, bb, 8 << 20),
    )(*_hbm([a, b]))


def _mlp_bwd(R, dxo, xmid, ubf, ybf, mod, nw2, w1, w2):
    B, T = R.B, R.T

    def body(dxo_ref, x_ref, u_ref, y_ref, mod_ref, nw_ref, w1_hbm, w2_hbm,
             dxm_ref, du_ref, a_ref, dy_ref, dmod_ref, dnw_ref, w1_v, w2_v, sem):
        b, i = pl.program_id(0), pl.program_id(1)
        _load_once(jnp.logical_and(b == 0, i == 0), [(w1_hbm, w1_v), (w2_hbm, w2_v)], sem)
        dxo = dxo_ref[...]
        _, xn, r = _norm_mod(x_ref[...], nw_ref[...], _row(mod_ref, 3), _row(mod_ref, 4))
        dyb = (dxo * _row(mod_ref, 5)).astype(BF)
        dy_ref[...] = dyb
        dg2 = jnp.sum(dxo * y_ref[...].astype(F32), axis=0, keepdims=True)
        dh = jnp.zeros((TM, D), F32)
        for j in range(NDEV):
            rl = jnp.maximum(u_ref[:, FSH * j:FSH * (j + 1)].astype(F32), 0.0)
            a_ref[:, FSH * j:FSH * (j + 1)] = (rl * rl).astype(BF)
            du = (_dot_nt(dyb, w2_v[j]) * (2.0 * rl)).astype(BF)
            du_ref[:, FSH * j:FSH * (j + 1)] = du
            dh = dh + _dot_nt(du, w1_v[j])
        dx, dsh, dsc, dnw = _norm_mod_bwd(dh, xn, r, nw_ref[...], _row(mod_ref, 4))
        dxm_ref[...] = dxo + dx
        _acc_rows(dmod_ref, R.first_of_stream(i), {3: dsh, 4: dsc, 5: dg2})
        _acc_rows(dnw_ref, jnp.logical_and(b == 0, i == 0), {0: dnw})

    return R.call("mlp_bwd", body,
                  [(dxo, R.row(D)), (xmid, R.row(D)), (ubf, R.row(FF)), (ybf, R.row(D)), (mod, R.mod()), (nw2, R.const((1, D))),
                   (w1, ANY), (w2, ANY)],
                  [(_sds((B, T, D), F32), R.row(D)), (_sds((B, T, FF), BF), R.row(FF)), (_sds((B, T, FF), BF), R.row(FF)),
                   (_sds((B, T, D), BF), R.row(D)), (_sds((B, 2, 8, D), F32), R.mod()), (_sds((8, D), F32), R.const((8, D)))],
                  scratch=[pltpu.VMEM((NDEV, D, FSH), w1.dtype), pltpu.VMEM((NDEV, FSH, D), w2.dtype), pltpu.SemaphoreType.DMA((2,))],
                  extra=(2 * _nbytes((NDEV, D, FSH), BF)) + (8 << 20))


def _out_bwd(R, dxm, mod, mixbf, yf, yb, xbc, z, o, dsk, snw, wout):
    B, T = R.B, R.T

    def body(dxm_ref, mod_ref, mix_ref, yf_ref, yb_ref, xbc_ref, z_ref, o_ref, dsk_ref, snw_ref, w_ref,
             dmix_ref, dz_ref, dyt_ref, dxsk_ref, do_ref, dotr_ref, dd_ref, dpo_ref, dmod_ref, dvec_ref):
        b, i = pl.program_id(0), pl.program_id(1)
        dxm = dxm_ref[...]
        dmixb = (dxm * _row(mod_ref, 2)).astype(BF)
        dmix_ref[...] = dmixb
        dg1 = jnp.sum(dxm * mix_ref[...].astype(F32), axis=0, keepdims=True)
        dcat = _dot_nt(dmixb, w_ref[...])
        do_v = dcat[:, SSD_IN:SSD_IN + QW]
        do_ref[...] = do_v.astype(BF)
        doo = do_v * o_ref[...]
        for h in range(H):
            dd_ref[h] = jnp.broadcast_to(jnp.sum(doo[:, HP * h:HP * (h + 1)], axis=1, keepdims=True), (TM, HP))
            dotr_ref[h] = do_v[:, HP * h:HP * (h + 1)].T.astype(BF)
        dpo_ref[...] = dcat[:, SSD_IN + QW:MIXW]
        dsn = dcat[:, 0:SSD_IN]
        ytot, zv, gz, r = _ssd_gate(yf_ref, yb_ref, xbc_ref, z_ref, dsk_ref)
        gh = gz * r
        dsnw = jnp.sum(dsn * gh, axis=0, keepdims=True)
        dgh = dsn * snw_ref[...]
        g0 = _group_mask()
        pr = dgh * gh
        half = SSD_IN // 2
        m0 = jnp.sum(jnp.where(g0, pr, 0.0), axis=1, keepdims=True) / half
        m1 = jnp.sum(jnp.where(g0, 0.0, pr), axis=1, keepdims=True) / half
        dgz = r * (dgh - gh * jnp.where(g0, m0, m1))
        dyt = dgz * _silu(zv)
        dz_ref[...] = (dgz * ytot * _dsilu(zv)).astype(BF)
        dyt_ref[...] = dyt
        dxsk_ref[...] = dyt * dsk_ref[...]
        ddsk = jnp.sum(dyt * xbc_ref[:, 0:SSD_IN], axis=0, keepdims=True)
        _acc_rows(dmod_ref, R.first_of_stream(i), {2: dg1})
        _acc_rows(dvec_ref, jnp.logical_and(b == 0, i == 0), {0: dsnw, 1: ddsk})

    return R.call("out_bwd", body,
                  [(dxm, R.row(D)), (mod, R.mod()), (mixbf, R.row(D)), (yf, R.row(SSD_IN)), (yb, R.row(SSD_IN)), (xbc, R.row(XBC)),
                   (z, R.row(SSD_IN)),
                   (o, R.row(QW)), (dsk, R.const((1, SSD_IN))), (snw, R.const((1, SSD_IN))), (wout, R.const((MIXW, D)))],
                  [(_sds((B, T, D), BF), R.row(D)), (_sds((B, T, SSD_IN), BF), R.row(SSD_IN)), (_sds((B, T, SSD_IN), F32), R.row(SSD_IN)),
                   (_sds((B, T, SSD_IN), F32), R.row(SSD_IN)), (_sds((B, T, QW), BF), R.row(QW)),
                   (_sds((B, H, HP, T), BF), pl.BlockSpec((None, H, HP, TM), lambda b, i: (b, 0, 0, i))),
                   (_sds((B, H, T, HP), F32), pl.BlockSpec((None, H, TM, HP), lambda b, i: (b, 0, i, 0))),
                   (_sds((B, T, PD), F32), R.row(PD)),
                   (_sds((B, 2, 8, D), F32), R.mod()), (_sds((8, SSD_IN), F32), R.const((8, SSD_IN)))],
                  extra=8 << 20)


def _pool_bwd(R, dpo, u, wbd, scale):
    B, T = R.B, R.T

    def body(d_ref, dpv_ref, dnx_ref, u_ref, upv_ref, unx_ref, w_ref, sc_ref, du_ref, dw_ref, dsc_ref):
        b, i = pl.program_id(0), pl.program_id(1)
        ext_u = R.ext(i, upv_ref, u_ref[...], unx_ref)
        ext_d = R.ext(i, dpv_ref, d_ref[...], dnx_ref)
        dm, cnt, valid, lane = _pool_centred(R, i, ext_u)
        ddm = _dot_nt(ext_d * sc_ref[...], w_ref[...]) * valid
        e = ddm / cnt
        a2 = e + _shift_rows(e, 1)
        a4 = _shift_rows(a2, -1) + _shift_rows(a2, 1)
        a8 = _shift_rows(a4, -2) + _shift_rows(a4, 2)
        a16 = _shift_rows(a8, -4) + _shift_rows(a8, 4)
        du_ref[...] = (_lane_select(lane, a2, a4, a8, a16) - ddm)[8:8 + TM].astype(BF)
        dmc = dm[8:8 + TM]
        dpo_c = d_ref[...]
        first = jnp.logical_and(b == 0, i == 0)

        @pl.when(first)
        def _():
            dw_ref[...] = jnp.zeros(dw_ref.shape, F32)

        dw_ref[...] += _dot_tn(dmc, dpo_c * sc_ref[...])
        _acc_rows(dsc_ref, first, {0: jnp.sum(dpo_c * _dot(dmc, w_ref[...]), axis=0, keepdims=True)})

    return R.call("pool_bwd", body,
                  [(dpo, R.row(PD)), (dpo, R.prev8(PD)), (dpo, R.next8(PD)), (u, R.row(PD)), (u, R.prev8(PD)), (u, R.next8(PD)),
                   (wbd, R.const((PD, PD))), (scale, R.const((1, PD)))],
                  [(_sds((B, T, PD), BF), R.row(PD)), (_sds((PD, PD), F32), R.const((PD, PD))), (_sds((8, PD), F32), R.const((8, PD)))],
                  extra=8 << 20)


def _flash_bwd(B, T, qt, kt, dob, q_t, k_t, v_t, do_t, lse_b, dd_b):
    nk, nq = T // TK, T // TQ

    def body(k_ref, kt_ref, vt_ref, q_ref, do_ref, qt_ref, dot_ref, lse_ref, dd_ref, dq_ref, dk_ref, dv_ref,
             s_scr, dp_scr, ds_scr):
        j = pl.program_id(2)
        s_scr[...] = _dot(q_ref[...], kt_ref[...])
        dp_scr[...] = _dot(do_ref[...], vt_ref[...])
        ctx_keys = jnp.where(j < CTX // TK, 1.0, 0.0)
        dk_tr = jnp.zeros((HP, TK), F32)
        dv_tr = jnp.zeros((HP, TK), F32)
        for i in range(nq):
            rows = slice(TQ * i, TQ * (i + 1))
            lse2 = jnp.concatenate([lse_ref[rows, :]] * (TK // HP), axis=1)
            dd2 = jnp.concatenate([dd_ref[rows, :]] * (TK // HP), axis=1)
            p = jnp.exp(s_scr[rows, :] - lse2)
            if i < CTX // TQ:
                p = p * ctx_keys
            ds = (p * (dp_scr[rows, :] - dd2)).astype(BF)
            ds_scr[rows, :] = ds
            dv_tr = dv_tr + _dot(dot_ref[:, rows], p)
            dk_tr = dk_tr + _dot(qt_ref[:, rows], ds)
        dq = _dot(ds_scr[...], k_ref[...])

        @pl.when(j == 0)
        def _():
            dq_ref[...] = dq

        @pl.when(j > 0)
        def _():
            dq_ref[...] += dq

        dk_ref[...] = dk_tr.T
        dv_ref[...] = dv_tr.T

    tspec = pl.BlockSpec((None, TK, HP), lambda b, h, j: (b, j, h))
    fspec = pl.BlockSpec((None, T, HP), lambda b, h, j: (b, 0, h))
    ttspec = pl.BlockSpec((None, None, HP, TK), lambda b, h, j: (b, h, 0, j))
    ftspec = pl.BlockSpec((None, None, HP, T), lambda b, h, j: (b, h, 0, 0))
    bspec = pl.BlockSpec((None, None, T, HP), lambda b, h, j: (b, h, 0, 0))
    bb = 4 * _nbytes((T, HP), BF) + 3 * _nbytes((T, HP), F32) + 8 * _nbytes((TK, HP), F32)
    scr = 2 * _nbytes((T, TK), F32) + _nbytes((T, TK), BF)
    return pl.pallas_call(
        functools.partial(body), name="flash_bwd", grid=(B, H, nk),
        in_specs=[tspec, ttspec, ttspec, fspec, fspec, ftspec, ftspec, bspec, bspec],
        out_specs=[fspec, tspec, tspec],
        out_shape=[_out((B, T, QW), F32)] * 3,
        scratch_shapes=[pltpu.VMEM((T, TK), F32), pltpu.VMEM((T, TK), F32), pltpu.VMEM((T, TK), BF)],
        compiler_params=_params(("arbitrary",) * 3, bb, scr + (12 << 20)),
    )(*_hbm([kt, k_t, v_t, qt, dob, q_t, do_t, lse_b, dd_b]))


def _mla_bwd(R, dqt, dkt, dvt, qa, kva, qnw, kvnw, wq, wk, wv, cos, sin):
    B, T = R.B, R.T
    scale = QK ** -0.5

    def body(dq_ref, dk_ref, dv_ref, qa_ref, kva_ref, qnw_ref, kvnw_ref, wq_ref, wk_ref, wv_ref, cos_ref, sin_ref,
             dqp_ref, dkb_ref, dvb_ref, dqa_ref, dkva_ref, dkr_ref, dnw_ref):
        b, i = pl.program_id(0), pl.program_id(1)
        cos1, sin1 = cos_ref[...], sin_ref[...]
        dq = dq_ref[...] * scale
        dqp = _per_head(lambda g: g * cos1 + _swap8(g * sin1), dq).astype(BF)
        dqp_ref[...] = dqp
        dkv = dk_ref[...]
        dkb = dkv.astype(BF)
        dkb_ref[...] = dkb
        dvb = dv_ref[...].astype(BF)
        dvb_ref[...] = dvb
        dkk = dkv[:, 0:HP]
        for h in range(1, H):
            dkk = dkk + dkv[:, HP * h:HP * (h + 1)]
        lane = _iota((TM, HP), 1)
        rope_lane = jnp.logical_and(lane >= NOPE, lane < NOPE + ROPE)
        dkr_ref[...] = jnp.where(rope_lane, dkk * cos1 + _swap8(dkk * sin1), 0.0).astype(BF)
        _, qh, qr = _rms(qa_ref[...], qnw_ref[...])
        _, kh, kr_ = _rms(kva_ref[...], kvnw_ref[...])
        dcq = _dot_nt(dqp, wq_ref[...])
        dckv = _dot_nt(dkb, wk_ref[...]) + _dot_nt(dvb, wv_ref[...])
        dqa, dqw = _rms_bwd(dcq, qh, qr, qnw_ref[...])
        dkva, dkw = _rms_bwd(dckv, kh, kr_, kvnw_ref[...])
        dqa_ref[...] = dqa.astype(BF)
        dkva_ref[...] = dkva.astype(BF)
        _acc_rows(dnw_ref, jnp.logical_and(b == 0, i == 0), {0: dqw, 1: dkw})

    tab = pl.BlockSpec((TM, HP), lambda b, i: (i, 0))
    return R.call("mla_bwd", body,
                  [(dqt, R.row(QW)), (dkt, R.row(QW)), (dvt, R.row(QW)), (qa, R.row(QL)), (kva, R.row(QL)),
                   (qnw, R.const((1, QL))), (kvnw, R.const((1, QL))), (wq, R.const((QL, QW))), (wk, R.const((QL, QW))),
                   (wv, R.const((QL, QW))), (cos, tab), (sin, tab)],
                  [(_sds((B, T, QW), BF), R.row(QW))] * 3 + [(_sds((B, T, QL), BF), R.row(QL))] * 2
                  + [(_sds((B, T, HP), BF), R.row(HP)), (_sds((8, QL), F32), R.const((8, QL)))], extra=8 << 20)


def _ssd_scan_bwd(B, T, dyt, xbc, dt2, alog2, hin_f, hin_b):
    nc, nctc = T // CH, CTX // CH

    def chain(d, dy_ref, xbc_ref, dt_ref, alog_ref, hin_ref, dxbc_ref, ddt_ref, da_ref, dhs_all):
        dhs = dhs_all.at[d]
        xbc_v = xbc_ref[...]
        dtv = dt_ref[...]
        dyv = dy_ref[...]
        a, adt, inc, inc_t, q = _scan_common(d, dtv, alog_ref)
        dq_all = jnp.zeros((CH, HP), F32)
        dqtot_all = jnp.zeros((1, HP), F32)
        ddt_x = jnp.zeros((CH, HP), F32)
        for g in range(2):
            bg = xbc_v[:, SSD_IN + NST * g:SSD_IN + NST * (g + 1)]
            cg = xbc_v[:, SSD_IN + 2 * NST + NST * g:SSD_IN + 2 * NST + NST * (g + 1)]
            gm = _dot_nt(cg, bg)
            gm_t = _dot_nt(bg, cg)
            dgm = jnp.zeros((CH, CH), F32)
            dbg = jnp.zeros((CH, NST), F32)
            dcg = jnp.zeros((CH, NST), F32)
            for r in range(3):
                h = 3 * g + r
                onehot, qc, qr, qtot, dt_h = _head_cols(h, adt, q, dtv, inc_t)
                lm = jnp.where(inc, jnp.exp(qc - qr), 0.0)
                xs_h = xbc_v[:, P * h:P * (h + 1)]
                xh = xs_h * dt_h
                sm = gm * lm
                dy_h = dyv[:, P * h:P * (h + 1)]
                hprev = hin_ref[P * h:P * (h + 1), :]
                dho = dhs[P * h:P * (h + 1), :]
                eq = jnp.exp(qc)
                etot = jnp.exp(qtot)
                dte = jnp.exp(qtot - qc)
                ds = _dot_nt(dy_h, xh)
                dx = _dot_tn(sm, dy_h)
                edy = eq * dy_h
                dq_col = jnp.sum(dy_h * (eq * _dot_nt(cg, hprev)), axis=1, keepdims=True)
                dcg = dcg + _dot(edy, hprev)
                dhin = _dot_tn(edy, cg) + etot * dho
                zs = _dot_nt(bg, dho)
                dx = dx + dte * zs
                w = dte * jnp.sum(xh * zs, axis=1, keepdims=True)
                dbg = dbg + _dot(xh * dte, dho)
                dqtot = jnp.sum(jnp.sum(hprev * dho, axis=1, keepdims=True), axis=0, keepdims=True) * etot \
                    + jnp.sum(w, axis=0, keepdims=True)
                dq_col = dq_col - w
                dgm = dgm + ds * lm
                lm_t = jnp.where(inc_t, jnp.exp(qr - qc), 0.0)
                e_t = _dot_nt(xh, dy_h) * gm_t * lm_t
                dq_col = dq_col + jnp.sum(ds * sm, axis=1, keepdims=True) - jnp.sum(e_t, axis=1, keepdims=True)
                dq_all = dq_all + dq_col * onehot
                dqtot_all = dqtot_all + dqtot * onehot
                dxbc_ref[:, P * h:P * (h + 1)] = dx * dt_h
                ddt_x = ddt_x + jnp.sum(dx * xs_h, axis=1, keepdims=True) * onehot
                dhs[P * h:P * (h + 1), :] = dhin
            dcg = dcg + _dot(dgm, bg)
            dbg = dbg + _dot_tn(dgm, cg)
            dxbc_ref[:, SSD_IN + NST * g:SSD_IN + NST * (g + 1)] = dbg
            dxbc_ref[:, SSD_IN + 2 * NST + NST * g:SSD_IN + 2 * NST + NST * (g + 1)] = dcg
        dadt = _dot01(inc_t, dq_all) + dqtot_all
        ddt_ref[...] = ddt_x + dadt * a
        da_ref[d] += jnp.where(_iota((8, HP), 0) == 0, jnp.sum(dadt * dtv, axis=0, keepdims=True), 0.0)

    def body(dyf_ref, dyb_ref, xf_ref, xb_ref, dtf_ref, dtb_ref, alog_ref, hf_ref, hb_ref,
             dxf_ref, dxb_ref, ddtf_ref, ddtb_ref, da_ref, dhs):
        @pl.when(pl.program_id(1) == 0)
        def _():
            dhs[...] = jnp.zeros(dhs.shape, F32)
            da_ref[...] = jnp.zeros(da_ref.shape, F32)

        chain(0, dyf_ref, xf_ref, dtf_ref, alog_ref, hf_ref, dxf_ref, ddtf_ref, da_ref, dhs)
        chain(1, dyb_ref, xb_ref, dtb_ref, alog_ref, hb_ref, dxb_ref, ddtb_ref, da_ref, dhs)

    cidx = lambda d: (lambda s: _chunk_index(d, nc - 1 - s, nc, nctc))
    specs = lambda d: dict(
        dy=pl.BlockSpec((None, CH, SSD_IN), lambda b, s: (b, cidx(d)(s), 0)),
        xbc=pl.BlockSpec((None, CH, XBC), lambda b, s: (b, cidx(d)(s), 0)),
        dt=pl.BlockSpec((None, None, CH, HP), lambda b, s: (b, d, cidx(d)(s), 0)),
        h=pl.BlockSpec((None, None, SSD_IN, NST), lambda b, s: (b, cidx(d)(s), 0, 0)),
        ddt=pl.BlockSpec((None, CH, HP), lambda b, s: (b, cidx(d)(s), 0)))
    f, r = specs(0), specs(1)
    bb = 2 * (2 * _nbytes((CH, XBC), F32) + 2 * _nbytes((CH, HP), F32) + _nbytes((CH, SSD_IN), F32) + _nbytes((SSD_IN, NST), F32))
    return pl.pallas_call(
        functools.partial(body), name="ssd_scan_bwd", grid=(B, nc),
        in_specs=[f["dy"], r["dy"], f["xbc"], r["xbc"], f["dt"], r["dt"], pl.BlockSpec((2, 8, HP), lambda b, s: (0, 0, 0)),
                  f["h"], r["h"]],
        out_specs=[f["xbc"], r["xbc"], f["ddt"], r["ddt"], pl.BlockSpec((None, 2, 8, HP), lambda b, s: (b, 0, 0, 0))],
        out_shape=[_out((B, T, XBC), F32)] * 2 + [_out((B, T, HP), F32)] * 2 + [_out((B, 2, 8, HP), F32)],
        scratch_shapes=[pltpu.VMEM((2, SSD_IN, NST), F32)],
        compiler_params=_params(("arbitrary",) * 2, bb, 8 << 20),
    )(*_hbm([dyt, dyt, xbc, xbc, dt2, dt2, alog2, hin_f, hin_b]))


def _ssd_prep_bwd(R, dxbc_f, dxbc_b, dxsk, ddt_f, ddt_b, xbc_raw, dt_raw, conv_w8, conv_b, dtb):
    B, T = R.B, R.T

    def body(dxf_ref, dxb_ref, dsk_ref, ddtf_ref, ddtb2_ref, raw_ref, pv_ref, nx_ref, dtr_ref, w_ref, b_ref, dtb_ref,
             dpre_ref, ddtr_ref, dcw_ref, dvec_ref, ddtb_ref):
        b, i = pl.program_id(0), pl.program_id(1)
        ext = R.ext(i, pv_ref, raw_ref[...], nx_ref)
        pre = _conv_pre(ext, w_ref, b_ref)
        dxbc = dxf_ref[...] + dxb_ref[...]
        skip = jnp.concatenate([dsk_ref[...], jnp.zeros((TM, XBC - SSD_IN), F32)], axis=1)
        dpre = (dxbc + skip) * _dsilu(pre)
        dpre_ref[...] = dpre
        first = jnp.logical_and(b == 0, i == 0)
        taps = {k: jnp.sum(dpre * _shift_rows(ext, k - 1)[8:8 + TM], axis=0, keepdims=True) for k in range(4)}
        _acc_rows(dcw_ref, first, taps)
        _acc_rows(dvec_ref, first, {0: jnp.sum(dpre, axis=0, keepdims=True)})
        ddt = ddtf_ref[...] + pltpu.roll(ddtb2_ref[...], H, axis=1)
        ddtr = ddt * _sigmoid(dtr_ref[...] + dtb_ref[...])
        ddtr = jnp.where(_iota((TM, HP), 1) < 2 * H, ddtr, 0.0)
        ddtr_ref[...] = ddtr.astype(BF)
        _acc_rows(ddtb_ref, first, {0: jnp.sum(ddtr, axis=0, keepdims=True)})

    return R.call("ssd_prep_bwd", body,
                  [(dxbc_f, R.row(XBC)), (dxbc_b, R.row(XBC)), (dxsk, R.row(SSD_IN)), (ddt_f, R.row(HP)), (ddt_b, R.row(HP)),
                   (xbc_raw, R.row(XBC)), (xbc_raw, R.prev8(XBC)),
                   (xbc_raw, R.next8(XBC)), (dt_raw, R.row(HP)), (conv_w8, R.const((8, XBC))), (conv_b, R.const((1, XBC))),
                   (dtb, R.const((1, HP)))],
                  [(_sds((B, T, XBC), F32), R.row(XBC)), (_sds((B, T, HP), BF), R.row(HP)), (_sds((8, XBC), F32), R.const((8, XBC))),
                   (_sds((8, XBC), F32), R.const((8, XBC))), (_sds((8, HP), F32), R.const((8, HP)))], extra=16 << 20)


def _in_bwd(R, dxm, x, mod, nw1, dz, dpre, dqa, dkva, dpool, dkr, ddtr, conv_w8, w_arr):
    B, T = R.B, R.T

    def body(dxm_ref, x_ref, mod_ref, nw_ref, dz_ref, dp_ref, dpp_ref, dpn_ref, dqa_ref, dkva_ref, dpool_ref, dkr_ref, ddt_ref,
             cw_ref, w_ref, dx_ref, dproj_ref, dmod_ref, dnw_ref):
        b, i = pl.program_id(0), pl.program_id(1)
        ext = R.ext(i, dpp_ref, dp_ref[...], dpn_ref)
        draw = (_row(cw_ref, 0) * _shift_rows(ext, 1)[8:8 + TM] + _row(cw_ref, 1) * ext[8:8 + TM]
                + _row(cw_ref, 2) * _shift_rows(ext, -1)[8:8 + TM] + _row(cw_ref, 3) * _shift_rows(ext, -2)[8:8 + TM])
        dproj_ref[:, A_Z:A_XBC] = dz_ref[...]
        dproj_ref[:, A_XBC:A_QA] = draw.astype(BF)
        dproj_ref[:, A_QA:A_KVA] = dqa_ref[...]
        dproj_ref[:, A_KVA:A_POOL] = dkva_ref[...]
        dproj_ref[:, A_POOL:A_KR] = dpool_ref[...]
        dproj_ref[:, A_KR:A_DT] = dkr_ref[...]
        dproj_ref[:, A_DT:PC] = ddt_ref[...]
        dh = _dot_nt(dproj_ref[...], w_ref[...])
        _, xn, r = _norm_mod(x_ref[...], nw_ref[...], _row(mod_ref, 0), _row(mod_ref, 1))
        dx, dsh, dsc, dnw = _norm_mod_bwd(dh, xn, r, nw_ref[...], _row(mod_ref, 1))
        dx_ref[...] = dxm_ref[...] + dx
        _acc_rows(dmod_ref, R.first_of_stream(i), {0: dsh, 1: dsc})
        _acc_rows(dnw_ref, jnp.logical_and(b == 0, i == 0), {0: dnw})

    return R.call("in_bwd", body,
                  [(dxm, R.row(D)), (x, R.row(D)), (mod, R.mod()), (nw1, R.const((1, D))), (dz, R.row(SSD_IN)), (dpre, R.row(XBC)),
                   (dpre, R.prev8(XBC)), (dpre, R.next8(XBC)), (dqa, R.row(QL)), (dkva, R.row(QL)), (dpool, R.row(PD)), (dkr, R.row(HP)),
                   (ddtr, R.row(HP)), (conv_w8, R.const((8, XBC))), (w_arr, R.const((D, PC)))],
                  [(_sds((B, T, D), F32), R.row(D)), (_sds((B, T, PC), BF), R.row(PC)), (_sds((B, 2, 8, D), F32), R.mod()),
                   (_sds((8, D), F32), R.const((8, D)))], extra=12 << 20)


def _adaln_fwd(cs, mod_w):
    L, _, C = mod_w.shape

    def body(c_ref, w_ref, o_ref):
        s = _silu(c_ref[...]).astype(BF)
        for l in range(L):
            o_ref[l] = jnp.dot(s, w_ref[l].astype(BF), preferred_element_type=F32)

    return pl.pallas_call(functools.partial(body), name="adaln_fwd", out_shape=_sds((L, 24, C), F32),
                          compiler_params=_params(None, _nbytes(mod_w.shape, F32) + _nbytes((L, 24, C), F32), 8 << 20))(cs, mod_w)


def _adaln_bwd(cs, dm, mod_w):
    L, _, C = mod_w.shape

    def body(c_ref, dm_ref, w_ref, gw_ref, gc_ref):
        c = c_ref[...]
        s = _silu(c).astype(BF)
        acc = jnp.zeros((24, D), F32)
        for l in range(L):
            dmb = dm_ref[l].astype(BF)
            gw_ref[l] = _dot_tn(s, dmb)
            acc = acc + _dot_nt(dmb, w_ref[l])
        gc_ref[...] = acc * _dsilu(c)

    return pl.pallas_call(functools.partial(body), name="adaln_bwd", out_shape=[_sds((L, D, C), F32), _sds((24, D), F32)],
                          compiler_params=_params(None, 2 * _nbytes(mod_w.shape, F32), 8 << 20))(cs, dm, mod_w)


def _adamw(name, parts, w, m, v, rt):
    Pn, Rr, C = parts.shape
    c1 = 1.0 - ADAM_B1 ** ADAM_STEP
    c2 = 1.0 - ADAM_B2 ** ADAM_STEP

    def body(p_ref, w_ref, m_ref, v_ref, g_ref, d_ref, nm_ref, nv_ref):
        g = p_ref[0].astype(F32)
        for k in range(1, Pn):
            g = g + p_ref[k].astype(F32)
        mn = ADAM_B1 * m_ref[...] + (1.0 - ADAM_B1) * g
        vn = ADAM_B2 * v_ref[...] + (1.0 - ADAM_B2) * jnp.square(g)
        g_ref[...] = g
        nm_ref[...] = mn
        nv_ref[...] = vn
        d_ref[...] = -ADAM_LR * ((mn / c1) / (jnp.sqrt(vn / c2) + ADAM_EPS) + ADAM_WD * w_ref[...])

    spec = pl.BlockSpec((rt, C), lambda i: (i, 0))
    bb = Pn * _nbytes((rt, C), parts.dtype) + 7 * _nbytes((rt, C), F32)
    return pl.pallas_call(
        functools.partial(body), name=name, grid=(Rr // rt,),
        in_specs=[pl.BlockSpec((Pn, rt, C), lambda i: (0, i, 0)), spec, spec, spec],
        out_specs=[spec] * 4, out_shape=[_out((Rr, C), F32)] * 4,
        compiler_params=_params(("arbitrary",), bb, 4 << 20),
    )(*_hbm([parts, w, m, v]))


MESH = pl.DeviceIdType.MESH


def _my_pos():
    return lax.axis_index("x"), lax.axis_index("y"), lax.axis_index("c")


def _dev_index(x, y, c):
    return 4 * x + 2 * y + c


def _all_gather(name, shards):
    n = len(shards)

    def body(*refs):
        ins, outs = refs[:n], refs[n:2 * n]
        send_sems, recv_sems, local_sem = refs[2 * n:]
        x, y, c = _my_pos()
        me, sibling = (x, y, c), (x, y, 1 - c)
        chips = [(1 - x, y), (x, 1 - y), (1 - x, 1 - y)]

        def copy(t, k, block, to, src=None):
            slot = outs[t].at[_dev_index(*block)]
            return pltpu.make_async_remote_copy(
                src_ref=slot if src is None else src, dst_ref=slot,
                send_sem=send_sems.at[t, k], recv_sem=recv_sems.at[t, k], device_id=to, device_id_type=MESH)

        mine = [pltpu.make_async_copy(ins[t], outs[t].at[_dev_index(*me)], local_sem.at[t]) for t in range(n)]
        for cp in mine:
            cp.start()
        first = []
        for t in range(n):
            first.append(copy(t, 0, me, sibling, src=ins[t]))
            first += [copy(t, 1 + j, me, (*chip, c), src=ins[t]) for j, chip in enumerate(chips)]
        for cp in first:
            cp.start()
        passed = []
        for j, chip in enumerate(chips):
            for t in range(n):
                copy(t, 1 + j, (*chip, c), me).wait_recv()
                cp = copy(t, 4 + j, (*chip, c), sibling)
                cp.start()
                passed.append(cp)
        for t in range(n):
            copy(t, 0, sibling, me).wait_recv()
            for j, chip in enumerate(chips):
                copy(t, 4 + j, (*chip, 1 - c), me).wait_recv()
        for cp in first + passed:
            cp.wait_send()
        for cp in mine:
            cp.wait()

    return pl.pallas_call(
        functools.partial(body), name=name,
        in_specs=[ANY] * n, out_specs=[ANY] * n,
        out_shape=[_sds((NDEV,) + s.shape, s.dtype) for s in shards],
        scratch_shapes=[pltpu.SemaphoreType.DMA((n, 7)), pltpu.SemaphoreType.DMA((n, 7)), pltpu.SemaphoreType.DMA((n,))],
    )(*shards)


def _all_to_all(name, parts):
    n = len(parts)

    def body(*refs):
        ins, outs = refs[:n], refs[n:2 * n]
        send_sems, recv_sems, local_sem = refs[2 * n:]
        x, y, c = _my_pos()
        me = _dev_index(x, y, c)
        peers = [(x ^ ((k >> 2) & 1), y ^ ((k >> 1) & 1), c ^ (k & 1)) for k in range(1, NDEV)]
        mine = [pltpu.make_async_copy(ins[t].at[me], outs[t].at[me], local_sem.at[t]) for t in range(n)]
        for cp in mine:
            cp.start()
        sends = []
        for t in range(n):
            for k, peer in enumerate(peers):
                cp = pltpu.make_async_remote_copy(
                    src_ref=ins[t].at[_dev_index(*peer)], dst_ref=outs[t].at[me],
                    send_sem=send_sems.at[t, k], recv_sem=recv_sems.at[t, k], device_id=peer, device_id_type=MESH)
                cp.start()
                sends.append(cp)
        for t in range(n):
            for k, peer in enumerate(peers):
                slot = outs[t].at[_dev_index(*peer)]
                pltpu.make_async_remote_copy(
                    src_ref=slot, dst_ref=slot, send_sem=send_sems.at[t, k], recv_sem=recv_sems.at[t, k],
                    device_id=peer, device_id_type=MESH).wait_recv()
        for cp in sends:
            cp.wait_send()
        for cp in mine:
            cp.wait()

    return pl.pallas_call(
        functools.partial(body), name=name,
        in_specs=[ANY] * n, out_specs=[ANY] * n,
        out_shape=[_sds(p.shape, p.dtype) for p in parts],
        scratch_shapes=[pltpu.SemaphoreType.DMA((n, 7)), pltpu.SemaphoreType.DMA((n, 7)), pltpu.SemaphoreType.DMA((n,))],
    )(*parts)


def _arrange_w_in(w):
    z = lambda n: jnp.zeros((w.shape[0], n), w.dtype)
    return jnp.concatenate([w[:, 0:1280], w[:, 1292:1548], w[:, 1548:1804], w[:, 1836:2092],
                            z(64), w[:, 1804:1836], z(32), w[:, 1280:1292], z(HP - 2 * H)], axis=1)


def _unarrange_w_in(g):
    return jnp.concatenate([g[:, 0:1280], g[:, A_DT:A_DT + 2 * H], g[:, A_QA:A_KVA], g[:, A_KVA:A_POOL],
                            g[:, A_KR + NOPE:A_KR + NOPE + ROPE], g[:, A_POOL:A_KR]], axis=1)


def _pad_heads(w, width):
    k = w.shape[0]
    return jnp.pad(w.reshape(k, H, width), ((0, 0), (0, 0), (0, HP - width))).reshape(k, H * HP)


def _unpad_heads(g, width):
    k = g.shape[0]
    return g.reshape(k, H, HP)[:, :, :width].reshape(k, H * width)


def _arrange_w_out(w):
    att = jnp.pad(w[SSD_IN:2 * SSD_IN].reshape(H, VH, D), ((0, 0), (0, HP - VH), (0, 0))).reshape(QW, D)
    return jnp.concatenate([w[0:SSD_IN], att, w[2 * SSD_IN:]], axis=0)


def _unarrange_w_out(g):
    att = g[SSD_IN:SSD_IN + QW].reshape(H, HP, D)[:, :VH].reshape(SSD_IN, D)
    return jnp.concatenate([g[0:SSD_IN], att, g[SSD_IN + QW:]], axis=0)


def _rope_tables(T):
    n = T - CTX
    t = jnp.arange(n, dtype=F32)
    row, col = jnp.floor(t / GRID_W), t % GRID_W
    pairs = ROPE // 4
    inv = ROPE_THETA ** (-jnp.arange(pairs, dtype=F32) / pairs)
    ar, ac = row[:, None] * inv, col[:, None] * inv
    cos = jnp.concatenate([jnp.cos(ar)] * 2 + [jnp.cos(ac)] * 2, axis=1)
    sin = jnp.concatenate([-jnp.sin(ar), jnp.sin(ar), -jnp.sin(ac), jnp.sin(ac)], axis=1)
    ones, zeros = jnp.ones((n, NOPE), F32), jnp.zeros((n, NOPE), F32)
    cos = jnp.concatenate([ones, cos, ones[:, :HP - QK]], axis=1)
    sin = jnp.concatenate([zeros, sin, zeros[:, :HP - QK]], axis=1)
    return (jnp.concatenate([jnp.ones((CTX, HP), F32), cos], axis=0),
            jnp.concatenate([jnp.zeros((CTX, HP), F32), sin], axis=0))


def _lane_pad(v, n):
    return jnp.pad(v, (0, n - v.shape[0]))[None, :]


def _layer_weights(w_in, w_out, w_q_b, w_kv_b, conv_w, pool_w):
    kv = w_kv_b.reshape(QL, H, NOPE + VH)
    wbd = jnp.zeros((PD, PD), F32)
    for g in range(4):
        wbd = lax.dynamic_update_slice(wbd, pool_w[g], (64 * g, 64 * g))
    return dict(
        w_in=_arrange_w_in(w_in).astype(BF), w_out=_arrange_w_out(w_out).astype(BF),
        wq=_pad_heads(w_q_b, QK).astype(BF),
        wk=_pad_heads(kv[:, :, :NOPE].reshape(QL, H * NOPE), NOPE).astype(BF),
        wv=_pad_heads(kv[:, :, NOPE:].reshape(QL, H * VH), VH).astype(BF),
        conv_w8=jnp.pad(conv_w, ((0, 4), (0, 0))), wbd=wbd)


def _layer_fwd(R, x, mod, lw, sp, w1, w2, cos, sin):
    B, T = R.B, R.T
    z, xbc_raw, qa, kva, pool_in, kr, dt_raw, h1 = _in_proj(R, x, mod, sp["nw1"], lw["w_in"])
    xbc, dt2 = _ssd_prep(R, xbc_raw, dt_raw, lw["conv_w8"], sp["conv_b"], sp["dtb"])
    yf, yb, hin_f, hin_b = _ssd_scan(B, T, xbc, dt2, sp["alog2"])
    qt, kt, q_t, k_t, v_t, cq, ckv = _mla_prep(R, qa, kva, kr, sp["qnw"], sp["kvnw"], lw["wq"], lw["wk"], lw["wv"], cos, sin)
    o, lse = _flash_fwd(B, T, q_t, kt, v_t)
    pool = _pool_fwd(R, pool_in, lw["wbd"], sp["pscale"])[0]
    xmid, cat, mixbf = _out_proj(R, x, mod, yf, yb, xbc, z, o, pool, sp["dsk"], sp["snw"], lw["w_out"])
    xo, h2, ubf, ybf = _mlp_fwd(R, xmid, mod, sp["nw2"], w1, w2)
    saved = dict(x=x, z=z, xbc_raw=xbc_raw, qa=qa, kva=kva, pool_in=pool_in, dt_raw=dt_raw, h1=h1, xbc=xbc, dt2=dt2,
                 yf=yf, yb=yb, hin_f=hin_f, hin_b=hin_b, qt=qt, kt=kt, q_t=q_t, k_t=k_t, v_t=v_t, cq=cq, ckv=ckv, o=o, lse=lse, cat=cat, mixbf=mixbf,
                 xmid=xmid,
                 h2=h2, ubf=ubf, ybf=ybf)
    return xo, saved


def _layer_bwd(R, dxo, sv, mod, lw, sp, w1, w2, cos, sin):
    B, T = R.B, R.T
    nq = T // TQ
    dxm, du, abf, dyb, dmod_a, dnw2 = _mlp_bwd(R, dxo, sv["xmid"], sv["ubf"], sv["ybf"], mod, sp["nw2"], w1, w2)
    g_w1 = _tn_matmul("dw_mlp1", sv["h2"], du, FSH)
    g_w2 = _tn_matmul("dw_mlp2", abf, dyb, D)[0].reshape(NDEV, FSH, D)
    dmix, dz, dyt, dxsk, dob, do_t, dd, dpo, dmod_b, dvec_o = _out_bwd(R, dxm, mod, sv["mixbf"], sv["yf"], sv["yb"], sv["xbc"], sv["z"],
                                                                 sv["o"], sp["dsk"], sp["snw"], lw["w_out"])
    g_wout = _unarrange_w_out(_tn_matmul("dw_out", sv["cat"], dmix, D)[0])
    dpool_in, g_wbd, dpsc = _pool_bwd(R, dpo, sv["pool_in"], lw["wbd"], sp["pscale"])
    dqt, dkt, dvt = _flash_bwd(B, T, sv["qt"], sv["kt"], dob, sv["q_t"], sv["k_t"], sv["v_t"], do_t, sv["lse"], dd)
    dqp, dkb, dvb, dqa, dkva, dkr, dnw_qk = _mla_bwd(R, dqt, dkt, dvt, sv["qa"], sv["kva"], sp["qnw"], sp["kvnw"],
                                                     lw["wq"], lw["wk"], lw["wv"], cos, sin)
    g_wq = _unpad_heads(_tn_matmul("dw_q", sv["cq"], dqp, QW)[0], QK)
    g_wk = _unpad_heads(_tn_matmul("dw_k", sv["ckv"], dkb, QW)[0], NOPE).reshape(QL, H, NOPE)
    g_wv = _unpad_heads(_tn_matmul("dw_v", sv["ckv"], dvb, QW)[0], VH).reshape(QL, H, VH)
    g_wkv = jnp.concatenate([g_wk, g_wv], axis=2).reshape(QL, H * (NOPE + VH))
    dxbc_f, dxbc_b, ddt_f, ddt_b, da = _ssd_scan_bwd(B, T, dyt, sv["xbc"], sv["dt2"], sp["alog2"], sv["hin_f"], sv["hin_b"])
    dpre, ddtr, dcw, dcb, ddtb = _ssd_prep_bwd(R, dxbc_f, dxbc_b, dxsk, ddt_f, ddt_b, sv["xbc_raw"], sv["dt_raw"], lw["conv_w8"],
                                                sp["conv_b"], sp["dtb"])
    dx, dproj, dmod_c, dnw1 = _in_bwd(R, dxm, sv["x"], mod, sp["nw1"], dz, dpre, dqa, dkva, dpool_in, dkr, ddtr,
                                      lw["conv_w8"], lw["w_in"])
    g_win = _unarrange_w_in(jnp.concatenate(list(_tn_matmul("dw_in", sv["h1"], dproj, PC // 3)), axis=1))
    a2 = -jnp.exp(sp["alog2"][:, 0, :H])
    small = dict(
        norm1_w=dnw1[0], norm2_w=dnw2[0], conv_w=dcw[0:4], conv_b=dcb[0], dt_bias=ddtb[0, :2 * H].reshape(2, H),
        a_log=jnp.sum(da[:, :, 0, :H], axis=0) * a2, ssd_d=jnp.sum(dvec_o[1].reshape(H, P), axis=1), ssd_norm_w=dvec_o[0],
        q_a_norm_w=dnw_qk[0], kv_a_norm_w=dnw_qk[1],
        pool_w=jnp.stack([g_wbd[64 * g:64 * (g + 1), 64 * g:64 * (g + 1)] for g in range(4)]), pool_scale=dpsc[0])
    big = dict(w_in=g_win, w_out=g_wout, w_q_b=g_wq, w_kv_b=g_wkv, w_mlp1=g_w1, w_mlp2=g_w2)
    return dx, big, small, dmod_a + dmod_b + dmod_c


def _small_params(l, norm1_w, norm2_w, conv_b, dt_bias, a_log, ssd_d, ssd_norm_w, q_a_norm_w, kv_a_norm_w, pool_scale):
    alog2 = jnp.broadcast_to(jnp.pad(a_log[l], ((0, 0), (0, HP - H)))[:, None, :], (2, 8, HP))
    return dict(nw1=norm1_w[l][None], nw2=norm2_w[l][None], conv_b=conv_b[l][None],
                dtb=_lane_pad(dt_bias[l].reshape(2 * H), HP), alog2=alog2,
                dsk=jnp.repeat(ssd_d[l], P)[None], snw=ssd_norm_w[l][None], qnw=q_a_norm_w[l][None],
                kvnw=kv_a_norm_w[l][None], pscale=pool_scale[l][None])


SMALL_NAMES = ["mod_b", "norm1_w", "norm2_w", "conv_b", "dt_bias", "a_log", "ssd_d", "ssd_norm_w", "q_a_norm_w",
               "kv_a_norm_w", "pool_w", "pool_scale", "final_norm_w"]


def _pack(arrs):
    rows = []
    for a in arrs:
        f = a.reshape(-1).astype(F32)
        n = -(-f.shape[0] // HP) * HP
        rows.append(jnp.pad(f, (0, n - f.shape[0])).reshape(-1, HP))
    out = jnp.concatenate(rows, axis=0)
    pad = (-out.shape[0]) % 8
    return jnp.pad(out, ((0, pad), (0, 0)))


def _unpack(pack, like):
    outs, r = [], 0
    for a in like:
        n = math.prod(a.shape)
        nr = -(-n // HP)
        outs.append(pack[r:r + nr].reshape(-1)[:n].reshape(a.shape))
        r += nr
    return outs


def _local_step(x, ctx, target, mods, full, small_w):
    B, N = x.shape[0], x.shape[1]
    T = CTX + N
    R = _Rows(B, T)
    cos, sin = _rope_tables(T)
    xu = jnp.concatenate([ctx, x], axis=1)
    L = len(mods)
    lws, sps, saves = [], [], []
    for l in range(L):
        f = full[l]
        lws.append(_layer_weights(f["w_in"], f["w_out"], f["w_q_b"], f["w_kv_b"], f["conv_w"], small_w["pool_w"][l]))
        sps.append(_small_params(l, *[small_w[k] for k in ["norm1_w", "norm2_w", "conv_b", "dt_bias", "a_log", "ssd_d",
                                                          "ssd_norm_w", "q_a_norm_w", "kv_a_norm_w", "pool_scale"]]))
        xu, sv = _layer_fwd(R, xu, mods[l], lws[l], sps[l], f["w_mlp1"], f["w_mlp2"], cos, sin)
        saves.append(sv)
    dx, loss8, dfnw = _loss_head(R, xu, target, small_w["final_norm_w"][None])
    bigs, smalls, dmods = [None] * L, [None] * L, [None] * L
    for l in reversed(range(L)):
        f = full[l]
        dx, bigs[l], smalls[l], dmods[l] = _layer_bwd(R, dx, saves[l], mods[l], lws[l], sps[l], f["w_mlp1"], f["w_mlp2"], cos, sin)
    return loss8[0, 0], dx[:, CTX:], bigs, smalls, dfnw[0], dmods


def kernel(x, c, ctx, c_ctx, mod_w, mod_b, norm1_w, norm2_w, w_in, conv_w, conv_b, dt_bias, a_log, ssd_d, ssd_norm_w, q_a_norm_w, w_q_b, kv_a_norm_w, w_kv_b, pool_w, pool_scale, w_out, w_mlp1, w_mlp2, final_norm_w, loss_target, m_c_ctx, m_mod_w, m_mod_b, m_norm1_w, m_norm2_w, m_w_in, m_conv_w, m_conv_b, m_dt_bias, m_a_log, m_ssd_d, m_ssd_norm_w, m_q_a_norm_w, m_w_q_b, m_kv_a_norm_w, m_w_kv_b, m_pool_w, m_pool_scale, m_w_out, m_w_mlp1, m_w_mlp2, m_final_norm_w, v_c_ctx, v_mod_w, v_mod_b, v_norm1_w, v_norm2_w, v_w_in, v_conv_w, v_conv_b, v_dt_bias, v_a_log, v_ssd_d, v_ssd_norm_w, v_q_a_norm_w, v_w_q_b, v_kv_a_norm_w, v_w_kv_b, v_pool_w, v_pool_scale, v_w_out, v_w_mlp1, v_w_mlp2, v_final_norm_w):
    args = dict(locals())
    B = x.shape[0]
    L = mod_w.shape[0]
    me = _dev_index(*_my_pos())
    CS = mod_w.shape[2]

    c_all, convw_all, win_g, wout_g, wq_g, wkv_g, w1_g, w2_g = _all_gather(
        "gather_weights",
        [c, conv_w, w_in.astype(BF), w_out.astype(BF), w_q_b.astype(BF), w_kv_b.astype(BF), w_mlp1.astype(BF), w_mlp2.astype(BF)])
    cs = jnp.concatenate([c_all.reshape(NDEV * B, D), c_ctx[None], jnp.zeros((24 - NDEV * B - 1, D), F32)], axis=0)
    m_loc = _adaln_fwd(cs, mod_w)
    m_all = _all_gather("gather_mod", [m_loc])[0]
    m_full = jnp.moveaxis(m_all, 0, 2).reshape(L, 24, NDEV * CS) + mod_b[:, None, :]
    mods = []
    for l in range(L):
        ex = lax.dynamic_slice(m_full[l], (me * B, 0), (B, 6 * D)).reshape(B, 6, D)
        cc = jnp.broadcast_to(m_full[l, NDEV * B].reshape(1, 6, D), (B, 6, D))
        mods.append(jnp.pad(jnp.stack([cc, ex], axis=1), ((0, 0), (0, 0), (0, 2), (0, 0))))
    full = []
    for l in range(L):
        full.append(dict(
            w_in=win_g[:, l].reshape(D, IN_COLS), w_out=wout_g[:, l].reshape(D, D),
            w_q_b=jnp.moveaxis(wq_g[:, l], 0, 1).reshape(QL, H * QK),
            w_kv_b=jnp.moveaxis(wkv_g[:, l], 0, 1).reshape(QL, H * (NOPE + VH)),
            conv_w=jnp.moveaxis(convw_all[:, l], 0, 1).reshape(4, XBC),
            w_mlp1=w1_g[:, l], w_mlp2=w2_g[:, l]))
    small_w = {k: args[k] for k in SMALL_NAMES if k != "mod_b"}

    loss_part, grad_x, bigs, smalls, dfnw, dmods = _local_step(x, ctx, loss_target, mods, full, small_w)
    loss = lax.psum(loss_part, ("x", "y", "c"))

    dm_ex = jnp.stack([dmods[l][:, 1, :6].reshape(B, 6 * D) for l in range(L)])
    dm_cc = jnp.stack([jnp.sum(dmods[l][:, 0, :6], axis=0).reshape(6 * D) for l in range(L)])
    small_parts = dict(
        mod_b=jnp.sum(dm_ex, axis=1) + dm_cc,
        **{k: jnp.stack([smalls[l][k] for l in range(L)]) for k in SMALL_NAMES[1:-1]},
        final_norm_w=dfnw, conv_w=jnp.stack([smalls[l]["conv_w"] for l in range(L)]), dm_cc=dm_cc)
    adam_grads = [small_parts[k] for k in SMALL_NAMES]
    extras = [small_parts["conv_w"], dm_cc, dm_ex]
    pack = jnp.concatenate([_pack(adam_grads), _pack(extras)], axis=0)
    pack_all = _all_gather("gather_small_grads", [pack])[0]
    wpack = _pack([args[k] for k in SMALL_NAMES])
    mpack = _pack([args["m_" + k] for k in SMALL_NAMES])
    vpack = _pack([args["v_" + k] for k in SMALL_NAMES])
    n_adam = wpack.shape[0]
    res_small = _adamw("adamw_small", pack_all[:, :n_adam], wpack, mpack, vpack, n_adam)
    small_out = [_unpack(r, [args[k] for k in SMALL_NAMES]) for r in res_small]
    ext_all = pack_all[:, n_adam:]
    ext_sum = ext_all[0]
    for k in range(1, NDEV):
        ext_sum = ext_sum + ext_all[k]
    g_conv_full, dm_cc_tot, _ = _unpack(ext_sum, extras)
    dm_ex_all = jnp.stack([_unpack(ext_all[k], extras)[2] for k in range(NDEV)], axis=1)
    dm_rows = jnp.concatenate([dm_ex_all.reshape(L, NDEV * B, 6 * D), dm_cc_tot[:, None, :],
                               jnp.zeros((L, 24 - NDEV * B - 1, 6 * D), F32)], axis=1)
    dm_loc = lax.dynamic_slice(dm_rows, (0, 0, me * CS), (L, 24, CS))
    g_modw, gc_part = _adaln_bwd(cs, dm_loc, mod_w)
    gc_all = _all_gather("gather_cctx_grad", [gc_part[NDEV * B:NDEV * B + 8]])[0]
    cpad = lambda a: jnp.pad(a[None], ((0, 7), (0, 0)))
    res_cc = _adamw("adamw_cctx", gc_all, cpad(c_ctx), cpad(m_c_ctx), cpad(v_c_ctx), 8)
    cc_out = [r[0] for r in res_cc]

    def stack_dev(name, fn):
        return jnp.stack([fn(bigs[l][name]) for l in range(L)], axis=1).astype(BF)

    sends = [
        stack_dev("w_in", lambda g: g.reshape(NDEV, D // NDEV, IN_COLS)),
        stack_dev("w_out", lambda g: g.reshape(NDEV, D // NDEV, D)),
        stack_dev("w_q_b", lambda g: jnp.moveaxis(g.reshape(QL, NDEV, -1), 1, 0)),
        stack_dev("w_kv_b", lambda g: jnp.moveaxis(g.reshape(QL, NDEV, -1), 1, 0)),
        stack_dev("w_mlp1", lambda g: g), stack_dev("w_mlp2", lambda g: g)]
    recvs = _all_to_all("exchange_grads", sends)
    big_names = ["w_in", "w_out", "w_q_b", "w_kv_b", "w_mlp1", "w_mlp2"]
    big_out = {}
    for name, rv in zip(big_names, recvs):
        w = args[name]
        Rr, C = math.prod(w.shape[:-1]), w.shape[-1]
        rt = Rr if Rr * C <= (1 << 18) else Rr // 8
        res = _adamw("adamw_" + name, rv.reshape(NDEV, Rr, C), w.reshape(Rr, C), args["m_" + name].reshape(Rr, C),
                     args["v_" + name].reshape(Rr, C), rt)
        big_out[name] = [r.reshape(w.shape) for r in res]
    res = _adamw("adamw_mod_w", g_modw.reshape(1, L * D, CS), mod_w.reshape(L * D, CS), m_mod_w.reshape(L * D, CS),
                 v_mod_w.reshape(L * D, CS), L * D // 8)
    big_out["mod_w"] = [r.reshape(mod_w.shape) for r in res]
    CW = conv_w.shape[2]
    g_conv = lax.dynamic_slice(g_conv_full, (0, 0, me * CW), (L, 4, CW))
    res = _adamw("adamw_conv_w", g_conv.reshape(1, L * 4, CW), conv_w.reshape(L * 4, CW), m_conv_w.reshape(L * 4, CW),
                 v_conv_w.reshape(L * 4, CW), L * 4)
    big_out["conv_w"] = [r.reshape(conv_w.shape) for r in res]

    weights = ["c_ctx", "mod_w", "mod_b", "norm1_w", "norm2_w", "w_in", "conv_w", "conv_b", "dt_bias", "a_log", "ssd_d",
               "ssd_norm_w", "q_a_norm_w", "w_q_b", "kv_a_norm_w", "w_kv_b", "pool_w", "pool_scale", "w_out", "w_mlp1",
               "w_mlp2", "final_norm_w"]
    outs = [loss, grad_x]
    for kind in range(4):
        for name in weights:
            if name == "c_ctx":
                outs.append(cc_out[kind])
            elif name in big_out:
                outs.append(big_out[name][kind])
            else:
                outs.append(small_out[kind][SMALL_NAMES.index(name)])
    return tuple(outs)
```

```python
import functools
import math

import jax
import jax.numpy as jnp
from jax import lax
from jax.experimental import pallas as pl
from jax.experimental.pallas import tpu as pltpu

F32 = jnp.float32
BF = jnp.bfloat16
MXU = BF

D = 1024
CTX = 256
GRID_W = 64
EPS = 1e-6
H = 6
P = 64
SSD_IN = 384
NST = 128
XBC = 896
CH = 128
QL = 256
NOPE = 64
ROPE = 32
VH = 64
QK = 96
HP = 128
QW = H * HP
PD = 256
FF = 4096
IN_COLS = 2092
ROPE_THETA = 10000.0
PC = 2304
A_Z, A_XBC, A_QA, A_KVA, A_POOL, A_KR, A_DT = 0, 384, 1280, 1536, 1792, 2048, 2176
MIXW = SSD_IN + QW + PD
NDEV = 8
FSH = FF // NDEV
TM = 256
TQ = 256
TK = 256
VMEM_CAP = 64 * 1024 * 1024
ADAM_LR, ADAM_B1, ADAM_B2, ADAM_EPS, ADAM_WD, ADAM_STEP = 0.001, 0.9, 0.999, 1e-08, 0.01, 10


def _nbytes(shape, dtype):
    n = 1
    for s in shape:
        if s is not None:
            n *= s
    return n * jnp.dtype(dtype).itemsize


def _params(sem, block_bytes, extra=0):
    lim = min(2 * block_bytes + extra + (8 << 20), VMEM_CAP - (6 << 20))
    return pltpu.CompilerParams(dimension_semantics=sem, vmem_limit_bytes=int(lim))


def _hbm(arrays):
    return [pltpu.with_memory_space_constraint(a, pltpu.HBM) for a in arrays]


def _dot(a, b):
    return jnp.dot(a.astype(MXU), b.astype(MXU), preferred_element_type=F32)


def _dot_nt(a, b):
    return lax.dot_general(a.astype(MXU), b.astype(MXU), (((1,), (1,)), ((), ())), preferred_element_type=F32)


def _dot_tn(a, b):
    return lax.dot_general(a.astype(MXU), b.astype(MXU), (((0,), (0,)), ((), ())), preferred_element_type=F32)


def _dot01(m01, x):
    b16 = jnp.bfloat16
    m = m01.astype(b16)
    hi = x.astype(b16)
    r1 = x - hi.astype(F32)
    mid = r1.astype(b16)
    lo = (r1 - mid.astype(F32)).astype(b16)
    f = lambda v: jnp.dot(m, v, preferred_element_type=F32)
    return f(hi) + f(mid) + f(lo)


def _sigmoid(x):
    return 1.0 / (1.0 + jnp.exp(-x))


def _silu(x):
    return x * _sigmoid(x)


def _dsilu(x):
    s = _sigmoid(x)
    return s * (1.0 + x * (1.0 - s))


def _iota(shape, dim):
    return lax.broadcasted_iota(jnp.int32, shape, dim)


def _row(ref, k):
    blk = ref[...]
    return jnp.sum(jnp.where(_iota(blk.shape, 0) == k, blk, 0.0), axis=0, keepdims=True)


def _shift_rows(x, k):
    n = x.shape[0]
    return pltpu.roll(x, (-k) % n, axis=0)


class _Rows:
    def __init__(self, B, T):
        self.B, self.T = B, T
        self.nt = T // TM
        self.nct = CTX // TM

    def row(self, F):
        return pl.BlockSpec((None, TM, F), lambda b, i: (b, i, 0))

    def row2(self, F):
        return pl.BlockSpec((None, 2, TM, F), lambda b, i: (b, 0, i, 0))

    def prev8(self, F):
        return pl.BlockSpec((None, 8, F), lambda b, i: (b, jnp.maximum(i * (TM // 8) - 1, 0), 0))

    def next8(self, F):
        last = self.T // 8 - 1
        return pl.BlockSpec((None, 8, F), lambda b, i: (b, jnp.minimum((i + 1) * (TM // 8), last), 0))

    def mod(self):
        nct = self.nct
        return pl.BlockSpec((None, None, 8, D), lambda b, i: (b, jnp.where(i < nct, 0, 1), 0, 0))

    def const(self, shape):
        z = (0,) * len(shape)
        return pl.BlockSpec(tuple(shape), lambda b, i: z)

    def tgt(self, F):
        nct = self.nct
        return pl.BlockSpec((None, TM, F), lambda b, i: (b, jnp.maximum(i - nct, 0), 0))

    def call(self, name, body, ins, outs, scratch=(), extra=0):
        arrays = [a for a, _ in ins]
        in_specs = [s for _, s in ins]
        out_shape = [pltpu.HBM(o.shape, o.dtype) for o, _ in outs]
        out_specs = [s for _, s in outs]
        bb = 0
        for a, s in list(ins) + list(outs):
            if s.block_shape is not None:
                bb += _nbytes(s.block_shape, a.dtype)
        return pl.pallas_call(
            functools.partial(body), name=name, grid=(self.B, self.nt),
            in_specs=in_specs, out_specs=out_specs, out_shape=out_shape, scratch_shapes=list(scratch),
            compiler_params=_params(("arbitrary", "arbitrary"), bb, extra),
        )(*_hbm(arrays))

    def first_of_stream(self, i):
        return jnp.logical_or(i == 0, i == self.nct)

    def last_of_stream(self, i):
        return jnp.logical_or(i == self.nct - 1, i == self.nt - 1)

    def ext(self, i, prev_ref, cur, next_ref):
        pv = prev_ref[...].astype(F32) * jnp.where(self.first_of_stream(i), 0.0, 1.0)
        nx = next_ref[...].astype(F32) * jnp.where(self.last_of_stream(i), 0.0, 1.0)
        return jnp.concatenate([pv, cur, nx], axis=0)

    def stream_pos(self, i, rows):
        start = jnp.where(i < self.nct, 0, CTX)
        n = jnp.where(i < self.nct, CTX, self.T - CTX)
        t = i * TM - 8 - start + _iota((rows, 1), 0)
        return t, n


def _sds(shape, dtype):
    return jax.ShapeDtypeStruct(tuple(shape), dtype)


def _out(shape, dtype):
    return pltpu.HBM(tuple(shape), dtype)


def _norm_mod(x, nw, sh, sc):
    r = lax.rsqrt(jnp.mean(x * x, axis=-1, keepdims=True) + EPS)
    xn = x * r
    return xn * nw * (1.0 + sc) + sh, xn, r


def _norm_mod_bwd(dh, xn, r, nw, sc):
    dsh = jnp.sum(dh, axis=0, keepdims=True)
    dsc = jnp.sum(dh * (xn * nw), axis=0, keepdims=True)
    dnw = jnp.sum(dh * (1.0 + sc) * xn, axis=0, keepdims=True)
    dxn = dh * nw * (1.0 + sc)
    dx = r * (dxn - xn * jnp.mean(dxn * xn, axis=-1, keepdims=True))
    return dx, dsh, dsc, dnw


def _acc_rows(ref, first, rows):
    rid = _iota(ref.shape, 0)
    upd = jnp.zeros(ref.shape, F32)
    for k, v in rows.items():
        upd = upd + jnp.where(rid == k, v, 0.0)

    @pl.when(first)
    def _():
        ref[...] = upd

    @pl.when(jnp.logical_not(first))
    def _():
        ref[...] += upd


def _in_proj(R, x, mod, nw1, w_arr):
    B, T = R.B, R.T

    def body(x_ref, mod_ref, nw_ref, w_ref, z_ref, xbc_ref, qa_ref, kva_ref, pool_ref, kr_ref, dt_ref, h_ref):
        h, _, _ = _norm_mod(x_ref[...], nw_ref[...], _row(mod_ref, 0), _row(mod_ref, 1))
        hb = h.astype(BF)
        h_ref[...] = hb
        p = jnp.dot(hb, w_ref[...], preferred_element_type=F32)
        z_ref[...] = p[:, A_Z:A_XBC]
        xbc_ref[...] = p[:, A_XBC:A_QA]
        qa_ref[...] = p[:, A_QA:A_KVA]
        kva_ref[...] = p[:, A_KVA:A_POOL]
        pool_ref[...] = p[:, A_POOL:A_KR]
        kr_ref[...] = p[:, A_KR:A_DT]
        dt_ref[...] = p[:, A_DT:PC]

    widths = [SSD_IN, XBC, QL, QL, PD, HP, HP]
    outs = [(_sds((B, T, w), F32), R.row(w)) for w in widths] + [(_sds((B, T, D), BF), R.row(D))]
    return R.call("in_proj", body,
                  [(x, R.row(D)), (mod, R.mod()), (nw1, R.const((1, D))), (w_arr, R.const((D, PC)))],
                  outs, extra=8 << 20)


def _conv_pre(ext, w_ref, b_ref):
    return (_row(w_ref, 0) * _shift_rows(ext, -1)[8:8 + TM] + _row(w_ref, 1) * ext[8:8 + TM]
            + _row(w_ref, 2) * _shift_rows(ext, 1)[8:8 + TM] + _row(w_ref, 3) * _shift_rows(ext, 2)[8:8 + TM]
            + b_ref[...])


def _softplus(x):
    return jnp.maximum(x, 0.0) + jnp.log(1.0 + jnp.exp(-jnp.abs(x)))


def _ssd_prep(R, xbc_raw, dt_raw, conv_w8, conv_b, dtb):
    B, T = R.B, R.T

    def body(raw_ref, pv_ref, nx_ref, dtr_ref, w_ref, b_ref, dtb_ref, xbc_ref, dt_ref):
        i = pl.program_id(1)
        ext = R.ext(i, pv_ref, raw_ref[...], nx_ref)
        xbc_ref[...] = _silu(_conv_pre(ext, w_ref, b_ref))
        lane = _iota((TM, HP), 1)
        dtv = _softplus(dtr_ref[...] + dtb_ref[...])
        keep = lane < H
        dt_ref[0] = jnp.where(keep, dtv, 0.0)
        dt_ref[1] = jnp.where(keep, pltpu.roll(dtv, HP - H, axis=1), 0.0)

    return R.call("ssd_prep", body,
                  [(xbc_raw, R.row(XBC)), (xbc_raw, R.prev8(XBC)), (xbc_raw, R.next8(XBC)), (dt_raw, R.row(HP)),
                   (conv_w8, R.const((8, XBC))), (conv_b, R.const((1, XBC))), (dtb, R.const((1, HP)))],
                  [(_sds((B, T, XBC), F32), R.row(XBC)), (_sds((B, 2, T, HP), F32), R.row2(HP))], extra=12 << 20)


def _chunk_index(d, s, nc, nctc):
    if d == 0:
        return s
    return jnp.where(s < nctc, nctc - 1 - s, nc - 1 - (s - nctc))


def _scan_common(d, dtv, alog_ref):
    blk = alog_ref[d]
    a = -jnp.exp(jnp.sum(jnp.where(_iota(blk.shape, 0) == 0, blk, 0.0), axis=0, keepdims=True))
    adt = dtv * a
    row = _iota((CH, CH), 0)
    col = _iota((CH, CH), 1)
    inc = col <= row if d == 0 else col >= row
    inc_t = row <= col if d == 0 else row >= col
    q = _dot01(inc, adt)
    return a, adt, inc, inc_t, q


def _head_cols(h, adt, q, dtv, inc_t):
    onehot = (_iota((1, HP), 1) == h).astype(F32)
    adt_h = jnp.sum(adt * onehot, axis=1, keepdims=True)
    qc = jnp.sum(q * onehot, axis=1, keepdims=True)
    dt_h = jnp.sum(dtv * onehot, axis=1, keepdims=True)
    qr = jnp.sum(adt_h * inc_t.astype(F32), axis=0, keepdims=True)
    qtot = jnp.sum(adt_h, axis=0, keepdims=True)
    return onehot, qc, qr, qtot, dt_h


def _ssd_scan(B, T, xbc, dt2, alog2):
    nc, nctc = T // CH, CTX // CH

    def chain(d, xbc_ref, dt_ref, alog_ref, y_ref, hin_ref, hs):
        xbc_v = xbc_ref[...]
        dtv = dt_ref[...]
        _, adt, inc, inc_t, q = _scan_common(d, dtv, alog_ref)
        hin_ref[...] = hs[d]
        for g in range(2):
            bg = xbc_v[:, SSD_IN + NST * g:SSD_IN + NST * (g + 1)]
            cg = xbc_v[:, SSD_IN + 2 * NST + NST * g:SSD_IN + 2 * NST + NST * (g + 1)]
            gm = _dot_nt(cg, bg)
            for r in range(3):
                h = 3 * g + r
                _, qc, qr, qtot, dt_h = _head_cols(h, adt, q, dtv, inc_t)
                lm = jnp.where(inc, jnp.exp(qc - qr), 0.0)
                xh = xbc_v[:, P * h:P * (h + 1)] * dt_h
                hprev = hs[d, P * h:P * (h + 1), :]
                y = _dot(gm * lm, xh) + jnp.exp(qc) * _dot_nt(cg, hprev)
                y_ref[:, P * h:P * (h + 1)] = y
                hs[d, P * h:P * (h + 1), :] = jnp.exp(qtot) * hprev + _dot_tn(xh * jnp.exp(qtot - qc), bg)

    def body(xf_ref, xb_ref, dtf_ref, dtb_ref, alog_ref, yf_ref, yb_ref, hf_ref, hb_ref, hs):
        @pl.when(pl.program_id(1) == 0)
        def _():
            hs[...] = jnp.zeros(hs.shape, F32)

        chain(0, xf_ref, dtf_ref, alog_ref, yf_ref, hf_ref, hs)
        chain(1, xb_ref, dtb_ref, alog_ref, yb_ref, hb_ref, hs)

    cidx = lambda d: (lambda s: _chunk_index(d, s, nc, nctc))
    specs = lambda d: dict(
        xbc=pl.BlockSpec((None, CH, XBC), lambda b, s: (b, cidx(d)(s), 0)),
        dt=pl.BlockSpec((None, None, CH, HP), lambda b, s: (b, d, cidx(d)(s), 0)),
        y=pl.BlockSpec((None, CH, SSD_IN), lambda b, s: (b, cidx(d)(s), 0)),
        h=pl.BlockSpec((None, None, SSD_IN, NST), lambda b, s: (b, cidx(d)(s), 0, 0)))
    f, r = specs(0), specs(1)
    bb = 2 * (_nbytes((CH, XBC), F32) + _nbytes((CH, HP), F32) + _nbytes((CH, SSD_IN), F32) + _nbytes((SSD_IN, NST), F32))
    return pl.pallas_call(
        functools.partial(body), name="ssd_scan", grid=(B, nc),
        in_specs=[f["xbc"], r["xbc"], f["dt"], r["dt"], pl.BlockSpec((2, 8, HP), lambda b, s: (0, 0, 0))],
        out_specs=[f["y"], r["y"], f["h"], r["h"]],
        out_shape=[_out((B, T, SSD_IN), F32)] * 2 + [_out((B, nc, SSD_IN, NST), F32)] * 2,
        scratch_shapes=[pltpu.VMEM((2, SSD_IN, NST), F32)],
        compiler_params=_params(("arbitrary",) * 2, bb, 8 << 20),
    )(*_hbm([xbc, xbc, dt2, dt2, alog2]))


def _swap8(u):
    lane = _iota(u.shape, 1)
    n = u.shape[1]
    return jnp.where((lane & 15) < 8, pltpu.roll(u, n - 8, axis=1), pltpu.roll(u, 8, axis=1))


def _rope(u, cos, sin_signed):
    return u * cos + _swap8(u) * sin_signed


def _rms(x, w):
    r = lax.rsqrt(jnp.mean(x * x, axis=-1, keepdims=True) + EPS)
    xh = x * r
    return xh * w, xh, r


def _rms_bwd(dy, xh, r, w):
    dw = jnp.sum(dy * xh, axis=0, keepdims=True)
    dxh = dy * w
    return r * (dxh - xh * jnp.mean(dxh * xh, axis=-1, keepdims=True)), dw


def _tile6(t):
    return jnp.concatenate([t] * H, axis=1)


def _per_head(fn, u):
    return jnp.concatenate([fn(u[:, HP * h:HP * (h + 1)]) for h in range(H)], axis=1)


def _mla_prep(R, qa, kva, kr, qnw, kvnw, wq, wk, wv, cos, sin):
    B, T = R.B, R.T
    scale = QK ** -0.5

    def body(qa_ref, kva_ref, kr_ref, qnw_ref, kvnw_ref, wq_ref, wk_ref, wv_ref, cos_ref, sin_ref,
             qt_ref, kt_ref, qtr_ref, ktr_ref, vtr_ref, cq_ref, ckv_ref):
        cq, _, _ = _rms(qa_ref[...], qnw_ref[...])
        ckv, _, _ = _rms(kva_ref[...], kvnw_ref[...])
        cqb, ckvb = cq.astype(BF), ckv.astype(BF)
        cq_ref[...] = cqb
        ckv_ref[...] = ckvb
        cos1, sin1 = cos_ref[...], sin_ref[...]
        q = _per_head(lambda u: _rope(u, cos1, sin1), jnp.dot(cqb, wq_ref[...], preferred_element_type=F32)) * scale
        qt_ref[...] = q.astype(BF)
        kk = _rope(kr_ref[...], cos1, sin1)
        k = jnp.dot(ckvb, wk_ref[...], preferred_element_type=F32) + _tile6(kk)
        kt_ref[...] = k.astype(BF)
        v = jnp.dot(ckvb, wv_ref[...], preferred_element_type=F32)
        v = jnp.where((_iota((TM, QW), 1) & (HP - 1)) == VH, 1.0, v)
        for h in range(H):
            cols = slice(HP * h, HP * (h + 1))
            qtr_ref[h] = q[:, cols].T.astype(BF)
            ktr_ref[h] = k[:, cols].T.astype(BF)
            vtr_ref[h] = v[:, cols].T.astype(BF)

    tr = (_sds((B, H, HP, T), BF), pl.BlockSpec((None, H, HP, TM), lambda b, i: (b, 0, 0, i)))
    return R.call("mla_prep", body,
                  [(qa, R.row(QL)), (kva, R.row(QL)), (kr, R.row(HP)), (qnw, R.const((1, QL))), (kvnw, R.const((1, QL))),
                   (wq, R.const((QL, QW))), (wk, R.const((QL, QW))), (wv, R.const((QL, QW))),
                   (cos, pl.BlockSpec((TM, HP), lambda b, i: (i, 0))), (sin, pl.BlockSpec((TM, HP), lambda b, i: (i, 0)))],
                  [(_sds((B, T, QW), BF), R.row(QW))] * 2 + [tr] * 3 + [(_sds((B, T, QL), BF), R.row(QL))] * 2, extra=12 << 20)


def _flash_fwd(B, T, q_t, kt, v_t):
    nq, nk = T // TQ, T // TK

    def body(q_ref, k_ref, v_ref, o_ref, lse_ref, s_scr):
        i = pl.program_id(2)
        q_tr = q_ref[...]

        def attend(nch):
            mrun = None
            for j in range(nch):
                s = _dot(k_ref[TK * j:TK * (j + 1), :], q_tr)
                s_scr[j] = s
                mrun = s if mrun is None else jnp.maximum(mrun, s)
            m = jnp.max(mrun, axis=0, keepdims=True)
            acc_t = jnp.zeros((HP, TQ), F32)
            for j in range(nch):
                acc_t = acc_t + _dot(v_ref[:, TK * j:TK * (j + 1)], jnp.exp(s_scr[j] - m))
            row = _iota((HP, TQ), 0)
            l = jnp.sum(jnp.where(row == VH, acc_t, 0.0), axis=0, keepdims=True)
            o_ref[...] = jnp.where(row < VH, acc_t / l, 0.0).T
            lse_ref[...] = jnp.broadcast_to(m + jnp.log(l), (HP, TQ)).T

        @pl.when(i < CTX // TQ)
        def _():
            attend(CTX // TK)

        @pl.when(i >= CTX // TQ)
        def _():
            attend(nk)

    bb = _nbytes((TQ, HP), BF) + 2 * _nbytes((T, HP), BF) + _nbytes((TQ, HP), F32) + _nbytes((8, TQ), F32)
    return pl.pallas_call(
        functools.partial(body), name="flash_fwd", grid=(B, H, nq),
        in_specs=[pl.BlockSpec((None, None, HP, TQ), lambda b, h, i: (b, h, 0, i)),
                  pl.BlockSpec((None, T, HP), lambda b, h, i: (b, 0, h)),
                  pl.BlockSpec((None, None, HP, T), lambda b, h, i: (b, h, 0, 0))],
        out_specs=[pl.BlockSpec((None, TQ, HP), lambda b, h, i: (b, i, h)),
                   pl.BlockSpec((None, None, TQ, HP), lambda b, h, i: (b, h, i, 0))],
        out_shape=[_out((B, T, QW), F32), _out((B, H, T, HP), F32)],
        scratch_shapes=[pltpu.VMEM((nk, TK, TQ), F32)],
        compiler_params=_params(("arbitrary",) * 3, bb, _nbytes((nk, TK, TQ), F32) + (8 << 20)),
    )(*_hbm([q_t, kt, v_t]))


def _pool_terms(R, i, rows):
    t, n = R.stream_pos(i, rows)
    lane = _iota((1, PD), 1)
    half = jnp.where(lane < 64, 1, jnp.where(lane < 128, 2, jnp.where(lane < 192, 4, 8)))
    cnt = (jnp.minimum(t + half, n) - jnp.maximum(t - half, 0)).astype(F32)
    valid = jnp.logical_and(t >= 0, t < n)
    return jnp.where(valid, cnt, 1.0), valid.astype(F32), lane


def _lane_select(lane, a2, a4, a8, a16):
    return jnp.where(lane < 64, a2, jnp.where(lane < 128, a4, jnp.where(lane < 192, a8, a16)))


def _pool_centred(R, i, ext):
    cnt, valid, lane = _pool_terms(R, i, ext.shape[0])
    s2 = ext + _shift_rows(ext, -1)
    s4 = _shift_rows(s2, -1) + _shift_rows(s2, 1)
    s8 = _shift_rows(s4, -2) + _shift_rows(s4, 2)
    s16 = _shift_rows(s8, -4) + _shift_rows(s8, 4)
    return _lane_select(lane, s2, s4, s8, s16) / cnt - ext, cnt, valid, lane


def _pool_fwd(R, u, wbd, scale):
    B, T = R.B, R.T

    def body(u_ref, pv_ref, nx_ref, w_ref, sc_ref, o_ref):
        i = pl.program_id(1)
        ext = R.ext(i, pv_ref, u_ref[...], nx_ref)
        dm, _, _, _ = _pool_centred(R, i, ext)
        o_ref[...] = _dot(dm[8:8 + TM], w_ref[...]) * sc_ref[...]

    return R.call("pool_fwd", body,
                  [(u, R.row(PD)), (u, R.prev8(PD)), (u, R.next8(PD)), (wbd, R.const((PD, PD))), (scale, R.const((1, PD)))],
                  [(_sds((B, T, PD), F32), R.row(PD))], extra=8 << 20)


def _group_mask():
    return _iota((1, SSD_IN), 1) < SSD_IN // 2


def _ssd_gate(yf_ref, yb_ref, xbc_ref, z_ref, dsk_ref):
    ytot = yf_ref[...] + yb_ref[...] + xbc_ref[:, 0:SSD_IN] * dsk_ref[...]
    z = z_ref[...]
    gz = ytot * _silu(z)
    g0 = _group_mask()
    sq = gz * gz
    s0 = jnp.sum(jnp.where(g0, sq, 0.0), axis=1, keepdims=True)
    s1 = jnp.sum(jnp.where(g0, 0.0, sq), axis=1, keepdims=True)
    half = SSD_IN // 2
    r = jnp.where(g0, lax.rsqrt(s0 / half + EPS), lax.rsqrt(s1 / half + EPS))
    return ytot, z, gz, r


def _out_proj(R, x, mod, yf, yb, xbc, z, o, pool, dsk, snw, wout):
    B, T = R.B, R.T

    def body(x_ref, mod_ref, yf_ref, yb_ref, xbc_ref, z_ref, o_ref, pool_ref, dsk_ref, snw_ref, w_ref, xmid_ref, cat_ref,
             mix_ref):
        _, _, gz, r = _ssd_gate(yf_ref, yb_ref, xbc_ref, z_ref, dsk_ref)
        cat_ref[:, 0:SSD_IN] = (gz * r * snw_ref[...]).astype(BF)
        cat_ref[:, SSD_IN:SSD_IN + QW] = o_ref[...].astype(BF)
        cat_ref[:, SSD_IN + QW:MIXW] = pool_ref[...].astype(BF)
        mix = jnp.dot(cat_ref[...], w_ref[...], preferred_element_type=F32)
        mix_ref[...] = mix.astype(BF)
        xmid_ref[...] = x_ref[...] + _row(mod_ref, 2) * mix

    return R.call("out_proj", body,
                  [(x, R.row(D)), (mod, R.mod()), (yf, R.row(SSD_IN)), (yb, R.row(SSD_IN)), (xbc, R.row(XBC)), (z, R.row(SSD_IN)),
                   (o, R.row(QW)),
                   (pool, R.row(PD)), (dsk, R.const((1, SSD_IN))), (snw, R.const((1, SSD_IN))), (wout, R.const((MIXW, D)))],
                  [(_sds((B, T, D), F32), R.row(D)), (_sds((B, T, MIXW), BF), R.row(MIXW)), (_sds((B, T, D), BF), R.row(D))],
                  extra=8 << 20)


def _load_once(first, pairs, sem):
    @pl.when(first)
    def _():
        cps = [pltpu.make_async_copy(src, dst, sem.at[k]) for k, (src, dst) in enumerate(pairs)]
        for cp in cps:
            cp.start()
        for cp in cps:
            cp.wait()


ANY = pl.BlockSpec(memory_space=pl.ANY)


def _mlp_fwd(R, xmid, mod, nw2, w1, w2):
    B, T = R.B, R.T

    def body(x_ref, mod_ref, nw_ref, w1_hbm, w2_hbm, xo_ref, h_ref, u_ref, y_ref, w1_v, w2_v, sem):
        first = jnp.logical_and(pl.program_id(0) == 0, pl.program_id(1) == 0)
        _load_once(first, [(w1_hbm, w1_v), (w2_hbm, w2_v)], sem)
        x = x_ref[...]
        h, _, _ = _norm_mod(x, nw_ref[...], _row(mod_ref, 3), _row(mod_ref, 4))
        hb = h.astype(BF)
        h_ref[...] = hb
        y = jnp.zeros((TM, D), F32)
        for j in range(NDEV):
            u = jnp.dot(hb, w1_v[j], preferred_element_type=F32)
            u_ref[:, FSH * j:FSH * (j + 1)] = u.astype(BF)
            a = jnp.square(jnp.maximum(u, 0.0))
            y = y + jnp.dot(a.astype(BF), w2_v[j], preferred_element_type=F32)
        y_ref[...] = y.astype(BF)
        xo_ref[...] = x + _row(mod_ref, 5) * y

    return R.call("mlp_fwd", body,
                  [(xmid, R.row(D)), (mod, R.mod()), (nw2, R.const((1, D))), (w1, ANY), (w2, ANY)],
                  [(_sds((B, T, D), F32), R.row(D)), (_sds((B, T, D), BF), R.row(D)), (_sds((B, T, FF), BF), R.row(FF)),
                   (_sds((B, T, D), BF), R.row(D))],
                  scratch=[pltpu.VMEM((NDEV, D, FSH), w1.dtype), pltpu.VMEM((NDEV, FSH, D), w2.dtype), pltpu.SemaphoreType.DMA((2,))],
                  extra=(2 * _nbytes((NDEV, D, FSH), BF)) + (8 << 20))


def _loss_head(R, x, tgt, fnw):
    B, T = R.B, R.T

    def body(x_ref, t_ref, w_ref, dx_ref, loss_ref, dw_ref):
        b, i = pl.program_id(0), pl.program_id(1)
        live = jnp.where(i >= R.nct, 1.0, 0.0)
        y, xh, r = _rms(x_ref[...], w_ref[...])
        err = (y - t_ref[...]) * live
        dy = err / D
        dxn, dw = _rms_bwd(dy, xh, r, w_ref[...])
        dx_ref[...] = dxn
        first = jnp.logical_and(b == 0, i == 0)
        part = 0.5 * jnp.sum(jnp.sum(err * err, axis=1, keepdims=True), axis=0, keepdims=True) / D
        _acc_rows(loss_ref, first, {0: jnp.broadcast_to(part, (1, HP))})
        _acc_rows(dw_ref, first, {0: dw})

    return R.call("loss_head", body,
                  [(x, R.row(D)), (tgt, R.tgt(D)), (fnw, R.const((1, D)))],
                  [(_sds((B, T, D), F32), R.row(D)), (_sds((8, HP), F32), R.const((8, HP))), (_sds((8, D), F32), R.const((8, D)))],
                  extra=8 << 20)


def _tn_matmul(name, a, b, tn):
    B, T, K = a.shape
    N = b.shape[2]
    nk = 1
    while T % nk or (T // nk) > 1088 or (T // nk) % 16:
        nk += 1
    tk = T // nk
    kt = K if K <= 1536 else 1024
    assert K % kt == 0 and N % tn == 0

    def body(a_ref, b_ref, o_ref):
        first = jnp.logical_and(pl.program_id(2) == 0, pl.program_id(3) == 0)

        @pl.when(first)
        def _():
            o_ref[...] = jnp.zeros(o_ref.shape, F32)

        o_ref[...] += _dot_tn(a_ref[...], b_ref[...])

    bb = _nbytes((tk, kt), a.dtype) + _nbytes((tk, tn), b.dtype) + _nbytes((kt, tn), F32)
    return pl.pallas_call(
        functools.partial(body), name=name, grid=(N // tn, K // kt, B, nk),
        in_specs=[pl.BlockSpec((None, tk, kt), lambda j, kk, bi, t: (bi, t, kk)),
                  pl.BlockSpec((None, tk, tn), lambda j, kk, bi, t: (bi, t, j))],
        out_specs=pl.BlockSpec((None, kt, tn), lambda j, kk, bi, t: (j, kk, 0)),
        out_shape=_out((N // tn, K, tn), F32),
        compiler_params=_params(("arbitrary",) * 4, bb, 8 << 20),
    )(*_hbm([a, b]))


def _mlp_bwd(R, dxo, xmid, ubf, ybf, mod, nw2, w1, w2):
    B, T = R.B, R.T

    def body(dxo_ref, x_ref, u_ref, y_ref, mod_ref, nw_ref, w1_hbm, w2_hbm,
             dxm_ref, du_ref, a_ref, dy_ref, dmod_ref, dnw_ref, w1_v, w2_v, sem):
        b, i = pl.program_id(0), pl.program_id(1)
        _load_once(jnp.logical_and(b == 0, i == 0), [(w1_hbm, w1_v), (w2_hbm, w2_v)], sem)
        dxo = dxo_ref[...]
        _, xn, r = _norm_mod(x_ref[...], nw_ref[...], _row(mod_ref, 3), _row(mod_ref, 4))
        dyb = (dxo * _row(mod_ref, 5)).astype(BF)
        dy_ref[...] = dyb
        dg2 = jnp.sum(dxo * y_ref[...].astype(F32), axis=0, keepdims=True)
        dh = jnp.zeros((TM, D), F32)
        for j in range(NDEV):
            rl = jnp.maximum(u_ref[:, FSH * j:FSH * (j + 1)].astype(F32), 0.0)
            a_ref[:, FSH * j:FSH * (j + 1)] = (rl * rl).astype(BF)
            du = (_dot_nt(dyb, w2_v[j]) * (2.0 * rl)).astype(BF)
            du_ref[:, FSH * j:FSH * (j + 1)] = du
            dh = dh + _dot_nt(du, w1_v[j])
        dx, dsh, dsc, dnw = _norm_mod_bwd(dh, xn, r, nw_ref[...], _row(mod_ref, 4))
        dxm_ref[...] = dxo + dx
        _acc_rows(dmod_ref, R.first_of_stream(i), {3: dsh, 4: dsc, 5: dg2})
        _acc_rows(dnw_ref, jnp.logical_and(b == 0, i == 0), {0: dnw})

    return R.call("mlp_bwd", body,
                  [(dxo, R.row(D)), (xmid, R.row(D)), (ubf, R.row(FF)), (ybf, R.row(D)), (mod, R.mod()), (nw2, R.const((1, D))),
                   (w1, ANY), (w2, ANY)],
                  [(_sds((B, T, D), F32), R.row(D)), (_sds((B, T, FF), BF), R.row(FF)), (_sds((B, T, FF), BF), R.row(FF)),
                   (_sds((B, T, D), BF), R.row(D)), (_sds((B, 2, 8, D), F32), R.mod()), (_sds((8, D), F32), R.const((8, D)))],
                  scratch=[pltpu.VMEM((NDEV, D, FSH), w1.dtype), pltpu.VMEM((NDEV, FSH, D), w2.dtype), pltpu.SemaphoreType.DMA((2,))],
                  extra=(2 * _nbytes((NDEV, D, FSH), BF)) + (8 << 20))


def _out_bwd(R, dxm, mod, mixbf, yf, yb, xbc, z, o, dsk, snw, wout):
    B, T = R.B, R.T

    def body(dxm_ref, mod_ref, mix_ref, yf_ref, yb_ref, xbc_ref, z_ref, o_ref, dsk_ref, snw_ref, w_ref,
             dmix_ref, dz_ref, dyt_ref, dxsk_ref, do_ref, dotr_ref, dd_ref, dpo_ref, dmod_ref, dvec_ref):
        b, i = pl.program_id(0), pl.program_id(1)
        dxm = dxm_ref[...]
        dmixb = (dxm * _row(mod_ref, 2)).astype(BF)
        dmix_ref[...] = dmixb
        dg1 = jnp.sum(dxm * mix_ref[...].astype(F32), axis=0, keepdims=True)
        dcat = _dot_nt(dmixb, w_ref[...])
        do_v = dcat[:, SSD_IN:SSD_IN + QW]
        do_ref[...] = do_v.astype(BF)
        doo = do_v * o_ref[...]
        for h in range(H):
            dd_ref[h] = jnp.broadcast_to(jnp.sum(doo[:, HP * h:HP * (h + 1)], axis=1, keepdims=True), (TM, HP))
            dotr_ref[h] = do_v[:, HP * h:HP * (h + 1)].T.astype(BF)
        dpo_ref[...] = dcat[:, SSD_IN + QW:MIXW]
        dsn = dcat[:, 0:SSD_IN]
        ytot, zv, gz, r = _ssd_gate(yf_ref, yb_ref, xbc_ref, z_ref, dsk_ref)
        gh = gz * r
        dsnw = jnp.sum(dsn * gh, axis=0, keepdims=True)
        dgh = dsn * snw_ref[...]
        g0 = _group_mask()
        pr = dgh * gh
        half = SSD_IN // 2
        m0 = jnp.sum(jnp.where(g0, pr, 0.0), axis=1, keepdims=True) / half
        m1 = jnp.sum(jnp.where(g0, 0.0, pr), axis=1, keepdims=True) / half
        dgz = r * (dgh - gh * jnp.where(g0, m0, m1))
        dyt = dgz * _silu(zv)
        dz_ref[...] = (dgz * ytot * _dsilu(zv)).astype(BF)
        dyt_ref[...] = dyt
        dxsk_ref[...] = dyt * dsk_ref[...]
        ddsk = jnp.sum(dyt * xbc_ref[:, 0:SSD_IN], axis=0, keepdims=True)
        _acc_rows(dmod_ref, R.first_of_stream(i), {2: dg1})
        _acc_rows(dvec_ref, jnp.logical_and(b == 0, i == 0), {0: dsnw, 1: ddsk})

    return R.call("out_bwd", body,
                  [(dxm, R.row(D)), (mod, R.mod()), (mixbf, R.row(D)), (yf, R.row(SSD_IN)), (yb, R.row(SSD_IN)), (xbc, R.row(XBC)),
                   (z, R.row(SSD_IN)),
                   (o, R.row(QW)), (dsk, R.const((1, SSD_IN))), (snw, R.const((1, SSD_IN))), (wout, R.const((MIXW, D)))],
                  [(_sds((B, T, D), BF), R.row(D)), (_sds((B, T, SSD_IN), BF), R.row(SSD_IN)), (_sds((B, T, SSD_IN), F32), R.row(SSD_IN)),
                   (_sds((B, T, SSD_IN), F32), R.row(SSD_IN)), (_sds((B, T, QW), BF), R.row(QW)),
                   (_sds((B, H, HP, T), BF), pl.BlockSpec((None, H, HP, TM), lambda b, i: (b, 0, 0, i))),
                   (_sds((B, H, T, HP), F32), pl.BlockSpec((None, H, TM, HP), lambda b, i: (b, 0, i, 0))),
                   (_sds((B, T, PD), F32), R.row(PD)),
                   (_sds((B, 2, 8, D), F32), R.mod()), (_sds((8, SSD_IN), F32), R.const((8, SSD_IN)))],
                  extra=8 << 20)


def _pool_bwd(R, dpo, u, wbd, scale):
    B, T = R.B, R.T

    def body(d_ref, dpv_ref, dnx_ref, u_ref, upv_ref, unx_ref, w_ref, sc_ref, du_ref, dw_ref, dsc_ref):
        b, i = pl.program_id(0), pl.program_id(1)
        ext_u = R.ext(i, upv_ref, u_ref[...], unx_ref)
        ext_d = R.ext(i, dpv_ref, d_ref[...], dnx_ref)
        dm, cnt, valid, lane = _pool_centred(R, i, ext_u)
        ddm = _dot_nt(ext_d * sc_ref[...], w_ref[...]) * valid
        e = ddm / cnt
        a2 = e + _shift_rows(e, 1)
        a4 = _shift_rows(a2, -1) + _shift_rows(a2, 1)
        a8 = _shift_rows(a4, -2) + _shift_rows(a4, 2)
        a16 = _shift_rows(a8, -4) + _shift_rows(a8, 4)
        du_ref[...] = (_lane_select(lane, a2, a4, a8, a16) - ddm)[8:8 + TM].astype(BF)
        dmc = dm[8:8 + TM]
        dpo_c = d_ref[...]
        first = jnp.logical_and(b == 0, i == 0)

        @pl.when(first)
        def _():
            dw_ref[...] = jnp.zeros(dw_ref.shape, F32)

        dw_ref[...] += _dot_tn(dmc, dpo_c * sc_ref[...])
        _acc_rows(dsc_ref, first, {0: jnp.sum(dpo_c * _dot(dmc, w_ref[...]), axis=0, keepdims=True)})

    return R.call("pool_bwd", body,
                  [(dpo, R.row(PD)), (dpo, R.prev8(PD)), (dpo, R.next8(PD)), (u, R.row(PD)), (u, R.prev8(PD)), (u, R.next8(PD)),
                   (wbd, R.const((PD, PD))), (scale, R.const((1, PD)))],
                  [(_sds((B, T, PD), BF), R.row(PD)), (_sds((PD, PD), F32), R.const((PD, PD))), (_sds((8, PD), F32), R.const((8, PD)))],
                  extra=8 << 20)


def _flash_bwd(B, T, qt, kt, dob, q_t, k_t, v_t, do_t, lse_b, dd_b):
    nk, nq = T // TK, T // TQ

    def body(k_ref, kt_ref, vt_ref, q_ref, do_ref, qt_ref, dot_ref, lse_ref, dd_ref, dq_ref, dk_ref, dv_ref,
             s_scr, dp_scr, ds_scr):
        j = pl.program_id(2)
        s_scr[...] = _dot(q_ref[...], kt_ref[...])
        dp_scr[...] = _dot(do_ref[...], vt_ref[...])
        ctx_keys = jnp.where(j < CTX // TK, 1.0, 0.0)
        dk_tr = jnp.zeros((HP, TK), F32)
        dv_tr = jnp.zeros((HP, TK), F32)
        for i in range(nq):
            rows = slice(TQ * i, TQ * (i + 1))
            lse2 = jnp.concatenate([lse_ref[rows, :]] * (TK // HP), axis=1)
            dd2 = jnp.concatenate([dd_ref[rows, :]] * (TK // HP), axis=1)
            p = jnp.exp(s_scr[rows, :] - lse2)
            if i < CTX // TQ:
                p = p * ctx_keys
            ds = (p * (dp_scr[rows, :] - dd2)).astype(BF)
            ds_scr[rows, :] = ds
            dv_tr = dv_tr + _dot(dot_ref[:, rows], p)
            dk_tr = dk_tr + _dot(qt_ref[:, rows], ds)
        dq = _dot(ds_scr[...], k_ref[...])

        @pl.when(j == 0)
        def _():
            dq_ref[...] = dq

        @pl.when(j > 0)
        def _():
            dq_ref[...] += dq

        dk_ref[...] = dk_tr.T
        dv_ref[...] = dv_tr.T

    tspec = pl.BlockSpec((None, TK, HP), lambda b, h, j: (b, j, h))
    fspec = pl.BlockSpec((None, T, HP), lambda b, h, j: (b, 0, h))
    ttspec = pl.BlockSpec((None, None, HP, TK), lambda b, h, j: (b, h, 0, j))
    ftspec = pl.BlockSpec((None, None, HP, T), lambda b, h, j: (b, h, 0, 0))
    bspec = pl.BlockSpec((None, None, T, HP), lambda b, h, j: (b, h, 0, 0))
    bb = 4 * _nbytes((T, HP), BF) + 3 * _nbytes((T, HP), F32) + 8 * _nbytes((TK, HP), F32)
    scr = 2 * _nbytes((T, TK), F32) + _nbytes((T, TK), BF)
    return pl.pallas_call(
        functools.partial(body), name="flash_bwd", grid=(B, H, nk),
        in_specs=[tspec, ttspec, ttspec, fspec, fspec, ftspec, ftspec, bspec, bspec],
        out_specs=[fspec, tspec, tspec],
        out_shape=[_out((B, T, QW), F32)] * 3,
        scratch_shapes=[pltpu.VMEM((T, TK), F32), pltpu.VMEM((T, TK), F32), pltpu.VMEM((T, TK), BF)],
        compiler_params=_params(("arbitrary",) * 3, bb, scr + (12 << 20)),
    )(*_hbm([kt, k_t, v_t, qt, dob, q_t, do_t, lse_b, dd_b]))


def _mla_bwd(R, dqt, dkt, dvt, qa, kva, qnw, kvnw, wq, wk, wv, cos, sin):
    B, T = R.B, R.T
    scale = QK ** -0.5

    def body(dq_ref, dk_ref, dv_ref, qa_ref, kva_ref, qnw_ref, kvnw_ref, wq_ref, wk_ref, wv_ref, cos_ref, sin_ref,
             dqp_ref, dkb_ref, dvb_ref, dqa_ref, dkva_ref, dkr_ref, dnw_ref):
        b, i = pl.program_id(0), pl.program_id(1)
        cos1, sin1 = cos_ref[...], sin_ref[...]
        dq = dq_ref[...] * scale
        dqp = _per_head(lambda g: g * cos1 + _swap8(g * sin1), dq).astype(BF)
        dqp_ref[...] = dqp
        dkv = dk_ref[...]
        dkb = dkv.astype(BF)
        dkb_ref[...] = dkb
        dvb = dv_ref[...].astype(BF)
        dvb_ref[...] = dvb
        dkk = dkv[:, 0:HP]
        for h in range(1, H):
            dkk = dkk + dkv[:, HP * h:HP * (h + 1)]
        lane = _iota((TM, HP), 1)
        rope_lane = jnp.logical_and(lane >= NOPE, lane < NOPE + ROPE)
        dkr_ref[...] = jnp.where(rope_lane, dkk * cos1 + _swap8(dkk * sin1), 0.0).astype(BF)
        _, qh, qr = _rms(qa_ref[...], qnw_ref[...])
        _, kh, kr_ = _rms(kva_ref[...], kvnw_ref[...])
        dcq = _dot_nt(dqp, wq_ref[...])
        dckv = _dot_nt(dkb, wk_ref[...]) + _dot_nt(dvb, wv_ref[...])
        dqa, dqw = _rms_bwd(dcq, qh, qr, qnw_ref[...])
        dkva, dkw = _rms_bwd(dckv, kh, kr_, kvnw_ref[...])
        dqa_ref[...] = dqa.astype(BF)
        dkva_ref[...] = dkva.astype(BF)
        _acc_rows(dnw_ref, jnp.logical_and(b == 0, i == 0), {0: dqw, 1: dkw})

    tab = pl.BlockSpec((TM, HP), lambda b, i: (i, 0))
    return R.call("mla_bwd", body,
                  [(dqt, R.row(QW)), (dkt, R.row(QW)), (dvt, R.row(QW)), (qa, R.row(QL)), (kva, R.row(QL)),
                   (qnw, R.const((1, QL))), (kvnw, R.const((1, QL))), (wq, R.const((QL, QW))), (wk, R.const((QL, QW))),
                   (wv, R.const((QL, QW))), (cos, tab), (sin, tab)],
                  [(_sds((B, T, QW), BF), R.row(QW))] * 3 + [(_sds((B, T, QL), BF), R.row(QL))] * 2
                  + [(_sds((B, T, HP), BF), R.row(HP)), (_sds((8, QL), F32), R.const((8, QL)))], extra=8 << 20)


def _ssd_scan_bwd(B, T, dyt, xbc, dt2, alog2, hin_f, hin_b):
    nc, nctc = T // CH, CTX // CH

    def chain(d, dy_ref, xbc_ref, dt_ref, alog_ref, hin_ref, dxbc_ref, ddt_ref, da_ref, dhs_all):
        dhs = dhs_all.at[d]
        xbc_v = xbc_ref[...]
        dtv = dt_ref[...]
        dyv = dy_ref[...]
        a, adt, inc, inc_t, q = _scan_common(d, dtv, alog_ref)
        dq_all = jnp.zeros((CH, HP), F32)
        dqtot_all = jnp.zeros((1, HP), F32)
        ddt_x = jnp.zeros((CH, HP), F32)
        for g in range(2):
            bg = xbc_v[:, SSD_IN + NST * g:SSD_IN + NST * (g + 1)]
            cg = xbc_v[:, SSD_IN + 2 * NST + NST * g:SSD_IN + 2 * NST + NST * (g + 1)]
            gm = _dot_nt(cg, bg)
            gm_t = _dot_nt(bg, cg)
            dgm = jnp.zeros((CH, CH), F32)
            dbg = jnp.zeros((CH, NST), F32)
            dcg = jnp.zeros((CH, NST), F32)
            for r in range(3):
                h = 3 * g + r
                onehot, qc, qr, qtot, dt_h = _head_cols(h, adt, q, dtv, inc_t)
                lm = jnp.where(inc, jnp.exp(qc - qr), 0.0)
                xs_h = xbc_v[:, P * h:P * (h + 1)]
                xh = xs_h * dt_h
                sm = gm * lm
                dy_h = dyv[:, P * h:P * (h + 1)]
                hprev = hin_ref[P * h:P * (h + 1), :]
                dho = dhs[P * h:P * (h + 1), :]
                eq = jnp.exp(qc)
                etot = jnp.exp(qtot)
                dte = jnp.exp(qtot - qc)
                ds = _dot_nt(dy_h, xh)
                dx = _dot_tn(sm, dy_h)
                edy = eq * dy_h
                dq_col = jnp.sum(dy_h * (eq * _dot_nt(cg, hprev)), axis=1, keepdims=True)
                dcg = dcg + _dot(edy, hprev)
                dhin = _dot_tn(edy, cg) + etot * dho
                zs = _dot_nt(bg, dho)
                dx = dx + dte * zs
                w = dte * jnp.sum(xh * zs, axis=1, keepdims=True)
                dbg = dbg + _dot(xh * dte, dho)
                dqtot = jnp.sum(jnp.sum(hprev * dho, axis=1, keepdims=True), axis=0, keepdims=True) * etot \
                    + jnp.sum(w, axis=0, keepdims=True)
                dq_col = dq_col - w
                dgm = dgm + ds * lm
                lm_t = jnp.where(inc_t, jnp.exp(qr - qc), 0.0)
                e_t = _dot_nt(xh, dy_h) * gm_t * lm_t
                dq_col = dq_col + jnp.sum(ds * sm, axis=1, keepdims=True) - jnp.sum(e_t, axis=1, keepdims=True)
                dq_all = dq_all + dq_col * onehot
                dqtot_all = dqtot_all + dqtot * onehot
                dxbc_ref[:, P * h:P * (h + 1)] = dx * dt_h
                ddt_x = ddt_x + jnp.sum(dx * xs_h, axis=1, keepdims=True) * onehot
                dhs[P * h:P * (h + 1), :] = dhin
            dcg = dcg + _dot(dgm, bg)
            dbg = dbg + _dot_tn(dgm, cg)
            dxbc_ref[:, SSD_IN + NST * g:SSD_IN + NST * (g + 1)] = dbg
            dxbc_ref[:, SSD_IN + 2 * NST + NST * g:SSD_IN + 2 * NST + NST * (g + 1)] = dcg
        dadt = _dot01(inc_t, dq_all) + dqtot_all
        ddt_ref[...] = ddt_x + dadt * a
        da_ref[d] += jnp.where(_iota((8, HP), 0) == 0, jnp.sum(dadt * dtv, axis=0, keepdims=True), 0.0)

    def body(dyf_ref, dyb_ref, xf_ref, xb_ref, dtf_ref, dtb_ref, alog_ref, hf_ref, hb_ref,
             dxf_ref, dxb_ref, ddtf_ref, ddtb_ref, da_ref, dhs):
        @pl.when(pl.program_id(1) == 0)
        def _():
            dhs[...] = jnp.zeros(dhs.shape, F32)
            da_ref[...] = jnp.zeros(da_ref.shape, F32)

        chain(0, dyf_ref, xf_ref, dtf_ref, alog_ref, hf_ref, dxf_ref, ddtf_ref, da_ref, dhs)
        chain(1, dyb_ref, xb_ref, dtb_ref, alog_ref, hb_ref, dxb_ref, ddtb_ref, da_ref, dhs)

    cidx = lambda d: (lambda s: _chunk_index(d, nc - 1 - s, nc, nctc))
    specs = lambda d: dict(
        dy=pl.BlockSpec((None, CH, SSD_IN), lambda b, s: (b, cidx(d)(s), 0)),
        xbc=pl.BlockSpec((None, CH, XBC), lambda b, s: (b, cidx(d)(s), 0)),
        dt=pl.BlockSpec((None, None, CH, HP), lambda b, s: (b, d, cidx(d)(s), 0)),
        h=pl.BlockSpec((None, None, SSD_IN, NST), lambda b, s: (b, cidx(d)(s), 0, 0)),
        ddt=pl.BlockSpec((None, CH, HP), lambda b, s: (b, cidx(d)(s), 0)))
    f, r = specs(0), specs(1)
    bb = 2 * (2 * _nbytes((CH, XBC), F32) + 2 * _nbytes((CH, HP), F32) + _nbytes((CH, SSD_IN), F32) + _nbytes((SSD_IN, NST), F32))
    return pl.pallas_call(
        functools.partial(body), name="ssd_scan_bwd", grid=(B, nc),
        in_specs=[f["dy"], r["dy"], f["xbc"], r["xbc"], f["dt"], r["dt"], pl.BlockSpec((2, 8, HP), lambda b, s: (0, 0, 0)),
                  f["h"], r["h"]],
        out_specs=[f["xbc"], r["xbc"], f["ddt"], r["ddt"], pl.BlockSpec((None, 2, 8, HP), lambda b, s: (b, 0, 0, 0))],
        out_shape=[_out((B, T, XBC), F32)] * 2 + [_out((B, T, HP), F32)] * 2 + [_out((B, 2, 8, HP), F32)],
        scratch_shapes=[pltpu.VMEM((2, SSD_IN, NST), F32)],
        compiler_params=_params(("arbitrary",) * 2, bb, 8 << 20),
    )(*_hbm([dyt, dyt, xbc, xbc, dt2, dt2, alog2, hin_f, hin_b]))


def _ssd_prep_bwd(R, dxbc_f, dxbc_b, dxsk, ddt_f, ddt_b, xbc_raw, dt_raw, conv_w8, conv_b, dtb):
    B, T = R.B, R.T

    def body(dxf_ref, dxb_ref, dsk_ref, ddtf_ref, ddtb2_ref, raw_ref, pv_ref, nx_ref, dtr_ref, w_ref, b_ref, dtb_ref,
             dpre_ref, ddtr_ref, dcw_ref, dvec_ref, ddtb_ref):
        b, i = pl.program_id(0), pl.program_id(1)
        ext = R.ext(i, pv_ref, raw_ref[...], nx_ref)
        pre = _conv_pre(ext, w_ref, b_ref)
        dxbc = dxf_ref[...] + dxb_ref[...]
        skip = jnp.concatenate([dsk_ref[...], jnp.zeros((TM, XBC - SSD_IN), F32)], axis=1)
        dpre = (dxbc + skip) * _dsilu(pre)
        dpre_ref[...] = dpre
        first = jnp.logical_and(b == 0, i == 0)
        taps = {k: jnp.sum(dpre * _shift_rows(ext, k - 1)[8:8 + TM], axis=0, keepdims=True) for k in range(4)}
        _acc_rows(dcw_ref, first, taps)
        _acc_rows(dvec_ref, first, {0: jnp.sum(dpre, axis=0, keepdims=True)})
        ddt = ddtf_ref[...] + pltpu.roll(ddtb2_ref[...], H, axis=1)
        ddtr = ddt * _sigmoid(dtr_ref[...] + dtb_ref[...])
        ddtr = jnp.where(_iota((TM, HP), 1) < 2 * H, ddtr, 0.0)
        ddtr_ref[...] = ddtr.astype(BF)
        _acc_rows(ddtb_ref, first, {0: jnp.sum(ddtr, axis=0, keepdims=True)})

    return R.call("ssd_prep_bwd", body,
                  [(dxbc_f, R.row(XBC)), (dxbc_b, R.row(XBC)), (dxsk, R.row(SSD_IN)), (ddt_f, R.row(HP)), (ddt_b, R.row(HP)),
                   (xbc_raw, R.row(XBC)), (xbc_raw, R.prev8(XBC)),
                   (xbc_raw, R.next8(XBC)), (dt_raw, R.row(HP)), (conv_w8, R.const((8, XBC))), (conv_b, R.const((1, XBC))),
                   (dtb, R.const((1, HP)))],
                  [(_sds((B, T, XBC), F32), R.row(XBC)), (_sds((B, T, HP), BF), R.row(HP)), (_sds((8, XBC), F32), R.const((8, XBC))),
                   (_sds((8, XBC), F32), R.const((8, XBC))), (_sds((8, HP), F32), R.const((8, HP)))], extra=16 << 20)


def _in_bwd(R, dxm, x, mod, nw1, dz, dpre, dqa, dkva, dpool, dkr, ddtr, conv_w8, w_arr):
    B, T = R.B, R.T

    def body(dxm_ref, x_ref, mod_ref, nw_ref, dz_ref, dp_ref, dpp_ref, dpn_ref, dqa_ref, dkva_ref, dpool_ref, dkr_ref, ddt_ref,
             cw_ref, w_ref, dx_ref, dproj_ref, dmod_ref, dnw_ref):
        b, i = pl.program_id(0), pl.program_id(1)
        ext = R.ext(i, dpp_ref, dp_ref[...], dpn_ref)
        draw = (_row(cw_ref, 0) * _shift_rows(ext, 1)[8:8 + TM] + _row(cw_ref, 1) * ext[8:8 + TM]
                + _row(cw_ref, 2) * _shift_rows(ext, -1)[8:8 + TM] + _row(cw_ref, 3) * _shift_rows(ext, -2)[8:8 + TM])
        dproj_ref[:, A_Z:A_XBC] = dz_ref[...]
        dproj_ref[:, A_XBC:A_QA] = draw.astype(BF)
        dproj_ref[:, A_QA:A_KVA] = dqa_ref[...]
        dproj_ref[:, A_KVA:A_POOL] = dkva_ref[...]
        dproj_ref[:, A_POOL:A_KR] = dpool_ref[...]
        dproj_ref[:, A_KR:A_DT] = dkr_ref[...]
        dproj_ref[:, A_DT:PC] = ddt_ref[...]
        dh = _dot_nt(dproj_ref[...], w_ref[...])
        _, xn, r = _norm_mod(x_ref[...], nw_ref[...], _row(mod_ref, 0), _row(mod_ref, 1))
        dx, dsh, dsc, dnw = _norm_mod_bwd(dh, xn, r, nw_ref[...], _row(mod_ref, 1))
        dx_ref[...] = dxm_ref[...] + dx
        _acc_rows(dmod_ref, R.first_of_stream(i), {0: dsh, 1: dsc})
        _acc_rows(dnw_ref, jnp.logical_and(b == 0, i == 0), {0: dnw})

    return R.call("in_bwd", body,
                  [(dxm, R.row(D)), (x, R.row(D)), (mod, R.mod()), (nw1, R.const((1, D))), (dz, R.row(SSD_IN)), (dpre, R.row(XBC)),
                   (dpre, R.prev8(XBC)), (dpre, R.next8(XBC)), (dqa, R.row(QL)), (dkva, R.row(QL)), (dpool, R.row(PD)), (dkr, R.row(HP)),
                   (ddtr, R.row(HP)), (conv_w8, R.const((8, XBC))), (w_arr, R.const((D, PC)))],
                  [(_sds((B, T, D), F32), R.row(D)), (_sds((B, T, PC), BF), R.row(PC)), (_sds((B, 2, 8, D), F32), R.mod()),
                   (_sds((8, D), F32), R.const((8, D)))], extra=12 << 20)


def _adaln_fwd(cs, mod_w):
    L, _, C = mod_w.shape

    def body(c_ref, w_ref, o_ref):
        s = _silu(c_ref[...]).astype(BF)
        for l in range(L):
            o_ref[l] = jnp.dot(s, w_ref[l].astype(BF), preferred_element_type=F32)

    return pl.pallas_call(functools.partial(body), name="adaln_fwd", out_shape=_sds((L, 24, C), F32),
                          compiler_params=_params(None, _nbytes(mod_w.shape, F32) + _nbytes((L, 24, C), F32), 8 << 20))(cs, mod_w)


def _adaln_bwd(cs, dm, mod_w):
    L, _, C = mod_w.shape

    def body(c_ref, dm_ref, w_ref, gw_ref, gc_ref):
        c = c_ref[...]
        s = _silu(c).astype(BF)
        acc = jnp.zeros((24, D), F32)
        for l in range(L):
            dmb = dm_ref[l].astype(BF)
            gw_ref[l] = _dot_tn(s, dmb)
            acc = acc + _dot_nt(dmb, w_ref[l])
        gc_ref[...] = acc * _dsilu(c)

    return pl.pallas_call(functools.partial(body), name="adaln_bwd", out_shape=[_sds((L, D, C), F32), _sds((24, D), F32)],
                          compiler_params=_params(None, 2 * _nbytes(mod_w.shape, F32), 8 << 20))(cs, dm, mod_w)


def _adamw(name, parts, w, m, v, rt):
    Pn, Rr, C = parts.shape
    c1 = 1.0 - ADAM_B1 ** ADAM_STEP
    c2 = 1.0 - ADAM_B2 ** ADAM_STEP

    def body(p_ref, w_ref, m_ref, v_ref, g_ref, d_ref, nm_ref, nv_ref):
        g = p_ref[0].astype(F32)
        for k in range(1, Pn):
            g = g + p_ref[k].astype(F32)
        mn = ADAM_B1 * m_ref[...] + (1.0 - ADAM_B1) * g
        vn = ADAM_B2 * v_ref[...] + (1.0 - ADAM_B2) * jnp.square(g)
        g_ref[...] = g
        nm_ref[...] = mn
        nv_ref[...] = vn
        d_ref[...] = -ADAM_LR * ((mn / c1) / (jnp.sqrt(vn / c2) + ADAM_EPS) + ADAM_WD * w_ref[...])

    spec = pl.BlockSpec((rt, C), lambda i: (i, 0))
    bb = Pn * _nbytes((rt, C), parts.dtype) + 7 * _nbytes((rt, C), F32)
    return pl.pallas_call(
        functools.partial(body), name=name, grid=(Rr // rt,),
        in_specs=[pl.BlockSpec((Pn, rt, C), lambda i: (0, i, 0)), spec, spec, spec],
        out_specs=[spec] * 4, out_shape=[_out((Rr, C), F32)] * 4,
        compiler_params=_params(("arbitrary",), bb, 4 << 20),
    )(*_hbm([parts, w, m, v]))


MESH = pl.DeviceIdType.MESH


def _my_pos():
    return lax.axis_index("x"), lax.axis_index("y"), lax.axis_index("c")


def _dev_index(x, y, c):
    return 4 * x + 2 * y + c


def _all_gather(name, shards):
    n = len(shards)

    def body(*refs):
        ins, outs = refs[:n], refs[n:2 * n]
        send_sems, recv_sems, local_sem = refs[2 * n:]
        x, y, c = _my_pos()
        me, sibling = (x, y, c), (x, y, 1 - c)
        chips = [(1 - x, y), (x, 1 - y), (1 - x, 1 - y)]

        def copy(t, k, block, to, src=None):
            slot = outs[t].at[_dev_index(*block)]
            return pltpu.make_async_remote_copy(
                src_ref=slot if src is None else src, dst_ref=slot,
                send_sem=send_sems.at[t, k], recv_sem=recv_sems.at[t, k], device_id=to, device_id_type=MESH)

        mine = [pltpu.make_async_copy(ins[t], outs[t].at[_dev_index(*me)], local_sem.at[t]) for t in range(n)]
        for cp in mine:
            cp.start()
        first = []
        for t in range(n):
            first.append(copy(t, 0, me, sibling, src=ins[t]))
            first += [copy(t, 1 + j, me, (*chip, c), src=ins[t]) for j, chip in enumerate(chips)]
        for cp in first:
            cp.start()
        passed = []
        for j, chip in enumerate(chips):
            for t in range(n):
                copy(t, 1 + j, (*chip, c), me).wait_recv()
                cp = copy(t, 4 + j, (*chip, c), sibling)
                cp.start()
                passed.append(cp)
        for t in range(n):
            copy(t, 0, sibling, me).wait_recv()
            for j, chip in enumerate(chips):
                copy(t, 4 + j, (*chip, 1 - c), me).wait_recv()
        for cp in first + passed:
            cp.wait_send()
        for cp in mine:
            cp.wait()

    return pl.pallas_call(
        functools.partial(body), name=name,
        in_specs=[ANY] * n, out_specs=[ANY] * n,
        out_shape=[_sds((NDEV,) + s.shape, s.dtype) for s in shards],
        scratch_shapes=[pltpu.SemaphoreType.DMA((n, 7)), pltpu.SemaphoreType.DMA((n, 7)), pltpu.SemaphoreType.DMA((n,))],
    )(*shards)


def _all_to_all(name, parts):
    n = len(parts)

    def body(*refs):
        ins, outs = refs[:n], refs[n:2 * n]
        send_sems, recv_sems, local_sem = refs[2 * n:]
        x, y, c = _my_pos()
        me = _dev_index(x, y, c)
        peers = [(x ^ ((k >> 2) & 1), y ^ ((k >> 1) & 1), c ^ (k & 1)) for k in range(1, NDEV)]
        mine = [pltpu.make_async_copy(ins[t].at[me], outs[t].at[me], local_sem.at[t]) for t in range(n)]
        for cp in mine:
            cp.start()
        sends = []
        for t in range(n):
            for k, peer in enumerate(peers):
                cp = pltpu.make_async_remote_copy(
                    src_ref=ins[t].at[_dev_index(*peer)], dst_ref=outs[t].at[me],
                    send_sem=send_sems.at[t, k], recv_sem=recv_sems.at[t, k], device_id=peer, device_id_type=MESH)
                cp.start()
                sends.append(cp)
        for t in range(n):
            for k, peer in enumerate(peers):
                slot = outs[t].at[_dev_index(*peer)]
                pltpu.make_async_remote_copy(
                    src_ref=slot, dst_ref=slot, send_sem=send_sems.at[t, k], recv_sem=recv_sems.at[t, k],
                    device_id=peer, device_id_type=MESH).wait_recv()
        for cp in sends:
            cp.wait_send()
        for cp in mine:
            cp.wait()

    return pl.pallas_call(
        functools.partial(body), name=name,
        in_specs=[ANY] * n, out_specs=[ANY] * n,
        out_shape=[_sds(p.shape, p.dtype) for p in parts],
        scratch_shapes=[pltpu.SemaphoreType.DMA((n, 7)), pltpu.SemaphoreType.DMA((n, 7)), pltpu.SemaphoreType.DMA((n,))],
    )(*parts)


SEM = pl.BlockSpec(memory_space=pltpu.SEMAPHORE)
IN_HBM = pl.BlockSpec(memory_space=pltpu.HBM)
DATAFLOW = pltpu.SideEffectType.DATAFLOW_SIDE_EFFECTING


def _flip_peers(x, y, c):
    return [(x ^ ((k >> 2) & 1), y ^ ((k >> 1) & 1), c ^ (k & 1)) for k in range(1, NDEV)]


def _split_copies(srcs, lands, send_sems, recv_sems, gather):
    x, y, c = _my_pos()
    me = _dev_index(x, y, c)
    out = []
    for t in range(len(srcs)):
        for k, peer in enumerate(_flip_peers(x, y, c)):
            p = _dev_index(*peer)
            src = srcs[t] if gather else srcs[t].at[p]
            sems = dict(send_sem=send_sems.at[7 * t + k], recv_sem=recv_sems.at[7 * t + k], device_id=peer, device_id_type=MESH)
            out.append((pltpu.make_async_remote_copy(src_ref=src, dst_ref=lands[t].at[me], **sems),
                        pltpu.make_async_remote_copy(src_ref=src, dst_ref=lands[t].at[p], **sems)))
    return out


def _exchange_start(name, collective_id, srcs, gather):
    n = len(srcs)
    lands = [lax.empty(((NDEV,) + s.shape) if gather else s.shape, s.dtype) for s in srcs]

    def body(*refs):
        src_refs, land_refs = refs[:n], refs[n:2 * n]
        send_sems, recv_sems = refs[2 * n], refs[2 * n + 1]
        token = refs[-1]
        barrier = pltpu.get_barrier_semaphore()
        for peer in _flip_peers(*_my_pos()):
            pl.semaphore_signal(barrier, inc=1, device_id=peer, device_id_type=MESH)
        pl.semaphore_wait(barrier, NDEV - 1)
        for send, _ in _split_copies(src_refs, land_refs, send_sems, recv_sems, gather):
            send.start()
        token[...] = jnp.zeros(token.shape, token.dtype)

    hbm = lambda a: pltpu.HBM(a.shape, a.dtype)
    res = pl.pallas_call(
        functools.partial(body), name=name,
        out_shape=[pltpu.SemaphoreType.DMA((7 * n,)), pltpu.SemaphoreType.DMA((7 * n,))] + [hbm(s) for s in srcs]
        + [hbm(a) for a in lands] + [_sds((8, HP), F32)],
        in_specs=[IN_HBM] * (2 * n), out_specs=[SEM, SEM] + [IN_HBM] * (2 * n) + [pl.BlockSpec(memory_space=pltpu.VMEM)],
        input_output_aliases={i: 2 + i for i in range(2 * n)},
        compiler_params=pltpu.CompilerParams(has_side_effects=DATAFLOW, collective_id=collective_id),
    )(*_hbm(list(srcs) + lands))
    return res[0], res[1], list(res[2:2 + n]), list(res[2 + n:2 + 2 * n]), res[-1]


def _exchange_wait(name, send_sems, recv_sems, srcs, lands, after, gather):
    n = len(srcs)

    def body(*refs):
        src_refs, land_refs = refs[:n], refs[n:2 * n]
        for _, recv in _split_copies(src_refs, land_refs, refs[2 * n], refs[2 * n + 1], gather):
            recv.wait_send()
            recv.wait_recv()

    hbm = lambda a: pltpu.HBM(a.shape, a.dtype)
    res = pl.pallas_call(
        functools.partial(body), name=name,
        out_shape=[hbm(s) for s in srcs] + [hbm(a) for a in lands],
        in_specs=[IN_HBM] * (2 * n) + [SEM, SEM, ANY], out_specs=[IN_HBM] * (2 * n),
        input_output_aliases={i: i for i in range(2 * n)},
        compiler_params=pltpu.CompilerParams(has_side_effects=DATAFLOW),
    )(*srcs, *lands, send_sems, recv_sems, after)
    return list(res[:n]), list(res[n:])


def _with_own(lands, own_blocks, me):
    out = []
    for land, own in zip(lands, own_blocks):
        out.append(lax.dynamic_update_slice(land, own[None], (me,) + (0,) * own.ndim))
    return out


def _arrange_w_in(w):
    z = lambda n: jnp.zeros((w.shape[0], n), w.dtype)
    return jnp.concatenate([w[:, 0:1280], w[:, 1292:1548], w[:, 1548:1804], w[:, 1836:2092],
                            z(64), w[:, 1804:1836], z(32), w[:, 1280:1292], z(HP - 2 * H)], axis=1)


def _unarrange_w_in(g):
    return jnp.concatenate([g[:, 0:1280], g[:, A_DT:A_DT + 2 * H], g[:, A_QA:A_KVA], g[:, A_KVA:A_POOL],
                            g[:, A_KR + NOPE:A_KR + NOPE + ROPE], g[:, A_POOL:A_KR]], axis=1)


def _pad_heads(w, width):
    k = w.shape[0]
    return jnp.pad(w.reshape(k, H, width), ((0, 0), (0, 0), (0, HP - width))).reshape(k, H * HP)


def _unpad_heads(g, width):
    k = g.shape[0]
    return g.reshape(k, H, HP)[:, :, :width].reshape(k, H * width)


def _arrange_w_out(w):
    att = jnp.pad(w[SSD_IN:2 * SSD_IN].reshape(H, VH, D), ((0, 0), (0, HP - VH), (0, 0))).reshape(QW, D)
    return jnp.concatenate([w[0:SSD_IN], att, w[2 * SSD_IN:]], axis=0)


def _unarrange_w_out(g):
    att = g[SSD_IN:SSD_IN + QW].reshape(H, HP, D)[:, :VH].reshape(SSD_IN, D)
    return jnp.concatenate([g[0:SSD_IN], att, g[SSD_IN + QW:]], axis=0)


def _rope_tables(T):
    n = T - CTX
    t = jnp.arange(n, dtype=F32)
    row, col = jnp.floor(t / GRID_W), t % GRID_W
    pairs = ROPE // 4
    inv = ROPE_THETA ** (-jnp.arange(pairs, dtype=F32) / pairs)
    ar, ac = row[:, None] * inv, col[:, None] * inv
    cos = jnp.concatenate([jnp.cos(ar)] * 2 + [jnp.cos(ac)] * 2, axis=1)
    sin = jnp.concatenate([-jnp.sin(ar), jnp.sin(ar), -jnp.sin(ac), jnp.sin(ac)], axis=1)
    ones, zeros = jnp.ones((n, NOPE), F32), jnp.zeros((n, NOPE), F32)
    cos = jnp.concatenate([ones, cos, ones[:, :HP - QK]], axis=1)
    sin = jnp.concatenate([zeros, sin, zeros[:, :HP - QK]], axis=1)
    return (jnp.concatenate([jnp.ones((CTX, HP), F32), cos], axis=0),
            jnp.concatenate([jnp.zeros((CTX, HP), F32), sin], axis=0))


def _lane_pad(v, n):
    return jnp.pad(v, (0, n - v.shape[0]))[None, :]


def _layer_weights(w_in, w_out, w_q_b, w_kv_b, conv_w, pool_w):
    kv = w_kv_b.reshape(QL, H, NOPE + VH)
    wbd = jnp.zeros((PD, PD), F32)
    for g in range(4):
        wbd = lax.dynamic_update_slice(wbd, pool_w[g], (64 * g, 64 * g))
    return dict(
        w_in=_arrange_w_in(w_in).astype(BF), w_out=_arrange_w_out(w_out).astype(BF),
        wq=_pad_heads(w_q_b, QK).astype(BF),
        wk=_pad_heads(kv[:, :, :NOPE].reshape(QL, H * NOPE), NOPE).astype(BF),
        wv=_pad_heads(kv[:, :, NOPE:].reshape(QL, H * VH), VH).astype(BF),
        conv_w8=jnp.pad(conv_w, ((0, 4), (0, 0))), wbd=wbd)


def _layer_fwd(R, x, mod, lw, sp, w1, w2, cos, sin):
    B, T = R.B, R.T
    z, xbc_raw, qa, kva, pool_in, kr, dt_raw, h1 = _in_proj(R, x, mod, sp["nw1"], lw["w_in"])
    xbc, dt2 = _ssd_prep(R, xbc_raw, dt_raw, lw["conv_w8"], sp["conv_b"], sp["dtb"])
    yf, yb, hin_f, hin_b = _ssd_scan(B, T, xbc, dt2, sp["alog2"])
    qt, kt, q_t, k_t, v_t, cq, ckv = _mla_prep(R, qa, kva, kr, sp["qnw"], sp["kvnw"], lw["wq"], lw["wk"], lw["wv"], cos, sin)
    o, lse = _flash_fwd(B, T, q_t, kt, v_t)
    pool = _pool_fwd(R, pool_in, lw["wbd"], sp["pscale"])[0]
    xmid, cat, mixbf = _out_proj(R, x, mod, yf, yb, xbc, z, o, pool, sp["dsk"], sp["snw"], lw["w_out"])
    xo, h2, ubf, ybf = _mlp_fwd(R, xmid, mod, sp["nw2"], w1, w2)
    saved = dict(x=x, z=z, xbc_raw=xbc_raw, qa=qa, kva=kva, pool_in=pool_in, dt_raw=dt_raw, h1=h1, xbc=xbc, dt2=dt2,
                 yf=yf, yb=yb, hin_f=hin_f, hin_b=hin_b, qt=qt, kt=kt, q_t=q_t, k_t=k_t, v_t=v_t, cq=cq, ckv=ckv, o=o, lse=lse, cat=cat, mixbf=mixbf,
                 xmid=xmid,
                 h2=h2, ubf=ubf, ybf=ybf)
    return xo, saved


def _layer_bwd(R, dxo, sv, mod, lw, sp, w1, w2, cos, sin):
    B, T = R.B, R.T
    nq = T // TQ
    dxm, du, abf, dyb, dmod_a, dnw2 = _mlp_bwd(R, dxo, sv["xmid"], sv["ubf"], sv["ybf"], mod, sp["nw2"], w1, w2)
    g_w1 = _tn_matmul("dw_mlp1", sv["h2"], du, FSH)
    g_w2 = _tn_matmul("dw_mlp2", abf, dyb, D)[0].reshape(NDEV, FSH, D)
    dmix, dz, dyt, dxsk, dob, do_t, dd, dpo, dmod_b, dvec_o = _out_bwd(R, dxm, mod, sv["mixbf"], sv["yf"], sv["yb"], sv["xbc"], sv["z"],
                                                                 sv["o"], sp["dsk"], sp["snw"], lw["w_out"])
    g_wout = _unarrange_w_out(_tn_matmul("dw_out", sv["cat"], dmix, D)[0])
    dpool_in, g_wbd, dpsc = _pool_bwd(R, dpo, sv["pool_in"], lw["wbd"], sp["pscale"])
    dqt, dkt, dvt = _flash_bwd(B, T, sv["qt"], sv["kt"], dob, sv["q_t"], sv["k_t"], sv["v_t"], do_t, sv["lse"], dd)
    dqp, dkb, dvb, dqa, dkva, dkr, dnw_qk = _mla_bwd(R, dqt, dkt, dvt, sv["qa"], sv["kva"], sp["qnw"], sp["kvnw"],
                                                     lw["wq"], lw["wk"], lw["wv"], cos, sin)
    g_wq = _unpad_heads(_tn_matmul("dw_q", sv["cq"], dqp, QW)[0], QK)
    g_wk = _unpad_heads(_tn_matmul("dw_k", sv["ckv"], dkb, QW)[0], NOPE).reshape(QL, H, NOPE)
    g_wv = _unpad_heads(_tn_matmul("dw_v", sv["ckv"], dvb, QW)[0], VH).reshape(QL, H, VH)
    g_wkv = jnp.concatenate([g_wk, g_wv], axis=2).reshape(QL, H * (NOPE + VH))
    dxbc_f, dxbc_b, ddt_f, ddt_b, da = _ssd_scan_bwd(B, T, dyt, sv["xbc"], sv["dt2"], sp["alog2"], sv["hin_f"], sv["hin_b"])
    dpre, ddtr, dcw, dcb, ddtb = _ssd_prep_bwd(R, dxbc_f, dxbc_b, dxsk, ddt_f, ddt_b, sv["xbc_raw"], sv["dt_raw"], lw["conv_w8"],
                                                sp["conv_b"], sp["dtb"])
    dx, dproj, dmod_c, dnw1 = _in_bwd(R, dxm, sv["x"], mod, sp["nw1"], dz, dpre, dqa, dkva, dpool_in, dkr, ddtr,
                                      lw["conv_w8"], lw["w_in"])
    g_win = _unarrange_w_in(jnp.concatenate(list(_tn_matmul("dw_in", sv["h1"], dproj, PC // 3)), axis=1))
    a2 = -jnp.exp(sp["alog2"][:, 0, :H])
    small = dict(
        norm1_w=dnw1[0], norm2_w=dnw2[0], conv_w=dcw[0:4], conv_b=dcb[0], dt_bias=ddtb[0, :2 * H].reshape(2, H),
        a_log=jnp.sum(da[:, :, 0, :H], axis=0) * a2, ssd_d=jnp.sum(dvec_o[1].reshape(H, P), axis=1), ssd_norm_w=dvec_o[0],
        q_a_norm_w=dnw_qk[0], kv_a_norm_w=dnw_qk[1],
        pool_w=jnp.stack([g_wbd[64 * g:64 * (g + 1), 64 * g:64 * (g + 1)] for g in range(4)]), pool_scale=dpsc[0])
    big = dict(w_in=g_win, w_out=g_wout, w_q_b=g_wq, w_kv_b=g_wkv, w_mlp1=g_w1, w_mlp2=g_w2)
    return dx, big, small, dmod_a + dmod_b + dmod_c


def _small_params(l, norm1_w, norm2_w, conv_b, dt_bias, a_log, ssd_d, ssd_norm_w, q_a_norm_w, kv_a_norm_w, pool_scale):
    alog2 = jnp.broadcast_to(jnp.pad(a_log[l], ((0, 0), (0, HP - H)))[:, None, :], (2, 8, HP))
    return dict(nw1=norm1_w[l][None], nw2=norm2_w[l][None], conv_b=conv_b[l][None],
                dtb=_lane_pad(dt_bias[l].reshape(2 * H), HP), alog2=alog2,
                dsk=jnp.repeat(ssd_d[l], P)[None], snw=ssd_norm_w[l][None], qnw=q_a_norm_w[l][None],
                kvnw=kv_a_norm_w[l][None], pscale=pool_scale[l][None])


SMALL_NAMES = ["mod_b", "norm1_w", "norm2_w", "conv_b", "dt_bias", "a_log", "ssd_d", "ssd_norm_w", "q_a_norm_w",
               "kv_a_norm_w", "pool_w", "pool_scale", "final_norm_w"]


def _pack(arrs):
    rows = []
    for a in arrs:
        f = a.reshape(-1).astype(F32)
        n = -(-f.shape[0] // HP) * HP
        rows.append(jnp.pad(f, (0, n - f.shape[0])).reshape(-1, HP))
    out = jnp.concatenate(rows, axis=0)
    pad = (-out.shape[0]) % 8
    return jnp.pad(out, ((0, pad), (0, 0)))


def _unpack(pack, like):
    outs, r = [], 0
    for a in like:
        n = math.prod(a.shape)
        nr = -(-n // HP)
        outs.append(pack[r:r + nr].reshape(-1)[:n].reshape(a.shape))
        r += nr
    return outs


def _local_step(x, ctx, target, mods, full_of, small_w, on_grads=None):
    B, N = x.shape[0], x.shape[1]
    T = CTX + N
    R = _Rows(B, T)
    cos, sin = _rope_tables(T)
    xu = jnp.concatenate([ctx, x], axis=1)
    L = len(mods)
    fulls, lws, sps, saves = [], [], [], []
    for l in range(L):
        f = full_of(l, xu)
        fulls.append(f)
        lws.append(_layer_weights(f["w_in"], f["w_out"], f["w_q_b"], f["w_kv_b"], f["conv_w"], small_w["pool_w"][l]))
        sps.append(_small_params(l, *[small_w[k] for k in ["norm1_w", "norm2_w", "conv_b", "dt_bias", "a_log", "ssd_d",
                                                          "ssd_norm_w", "q_a_norm_w", "kv_a_norm_w", "pool_scale"]]))
        xu, sv = _layer_fwd(R, xu, mods[l], lws[l], sps[l], f["w_mlp1"], f["w_mlp2"], cos, sin)
        saves.append(sv)
    dx, loss8, dfnw = _loss_head(R, xu, target, small_w["final_norm_w"][None])
    bigs, smalls, dmods = [None] * L, [None] * L, [None] * L
    for l in reversed(range(L)):
        f = fulls[l]
        dx, bigs[l], smalls[l], dmods[l] = _layer_bwd(R, dx, saves[l], mods[l], lws[l], sps[l], f["w_mlp1"], f["w_mlp2"], cos, sin)
        if on_grads is not None:
            bigs[l], tok = on_grads(l, bigs[l], dx)
            if tok is not None:
                sps[l - 1] = dict(sps[l - 1], nw2=sps[l - 1]["nw2"] + tok)
    return loss8[0, 0], dx[:, CTX:], bigs, smalls, dfnw[0], dmods


def kernel(x, c, ctx, c_ctx, mod_w, mod_b, norm1_w, norm2_w, w_in, conv_w, conv_b, dt_bias, a_log, ssd_d, ssd_norm_w, q_a_norm_w, w_q_b, kv_a_norm_w, w_kv_b, pool_w, pool_scale, w_out, w_mlp1, w_mlp2, final_norm_w, loss_target, m_c_ctx, m_mod_w, m_mod_b, m_norm1_w, m_norm2_w, m_w_in, m_conv_w, m_conv_b, m_dt_bias, m_a_log, m_ssd_d, m_ssd_norm_w, m_q_a_norm_w, m_w_q_b, m_kv_a_norm_w, m_w_kv_b, m_pool_w, m_pool_scale, m_w_out, m_w_mlp1, m_w_mlp2, m_final_norm_w, v_c_ctx, v_mod_w, v_mod_b, v_norm1_w, v_norm2_w, v_w_in, v_conv_w, v_conv_b, v_dt_bias, v_a_log, v_ssd_d, v_ssd_norm_w, v_q_a_norm_w, v_w_q_b, v_kv_a_norm_w, v_w_kv_b, v_pool_w, v_pool_scale, v_w_out, v_w_mlp1, v_w_mlp2, v_final_norm_w):
    args = dict(locals())
    B = x.shape[0]
    L = mod_w.shape[0]
    me = _dev_index(*_my_pos())
    CS = mod_w.shape[2]

    big_names = ["w_in", "w_out", "w_q_b", "w_kv_b", "w_mlp1", "w_mlp2"]
    shards = {n: args[n].astype(BF) for n in big_names}
    g0 = _all_gather("gather_weights", [c, conv_w] + [shards[n][0] for n in big_names])
    c_all, convw_all = g0[0], g0[1]
    gathered = {0: dict(zip(big_names, g0[2:]))}
    pending = {}
    tok = jnp.zeros((), F32)
    for l in range(1, L):
        own = [shards[n][l] for n in big_names]
        ss, rs, srcs, lands, token = _exchange_start("gather_start_%d" % l, l, own, gather=True)
        pending[l] = (ss, rs, srcs, lands)
        tok = tok + token[0, 0]
    cs = jnp.concatenate([c_all.reshape(NDEV * B, D), c_ctx[None], jnp.zeros((24 - NDEV * B - 1, D), F32)], axis=0) + tok
    m_loc = _adaln_fwd(cs, mod_w)
    m_all = _all_gather("gather_mod", [m_loc])[0]
    m_full = jnp.moveaxis(m_all, 0, 2).reshape(L, 24, NDEV * CS) + mod_b[:, None, :]
    mods = []
    for l in range(L):
        ex = lax.dynamic_slice(m_full[l], (me * B, 0), (B, 6 * D)).reshape(B, 6, D)
        cc = jnp.broadcast_to(m_full[l, NDEV * B].reshape(1, 6, D), (B, 6, D))
        mods.append(jnp.pad(jnp.stack([cc, ex], axis=1), ((0, 0), (0, 0), (0, 2), (0, 0))))
    def full_of(l, xu):
        if l in pending:
            own, lands = _exchange_wait("gather_wait_%d" % l, *pending.pop(l), xu, gather=True)
            gathered[l] = dict(zip(big_names, _with_own(lands, own, me)))
        g = gathered[l]
        return dict(
            w_in=g["w_in"].reshape(D, IN_COLS), w_out=g["w_out"].reshape(D, D),
            w_q_b=jnp.moveaxis(g["w_q_b"], 0, 1).reshape(QL, H * QK),
            w_kv_b=jnp.moveaxis(g["w_kv_b"], 0, 1).reshape(QL, H * (NOPE + VH)),
            conv_w=jnp.moveaxis(convw_all[:, l], 0, 1).reshape(4, XBC),
            w_mlp1=g["w_mlp1"], w_mlp2=g["w_mlp2"])

    def grad_blocks(big):
        return [big["w_in"].reshape(NDEV, D // NDEV, IN_COLS).astype(BF), big["w_out"].reshape(NDEV, D // NDEV, D).astype(BF),
                jnp.moveaxis(big["w_q_b"].reshape(QL, NDEV, -1), 1, 0).astype(BF),
                jnp.moveaxis(big["w_kv_b"].reshape(QL, NDEV, -1), 1, 0).astype(BF),
                big["w_mlp1"].astype(BF), big["w_mlp2"].astype(BF)]

    sent = {}

    def on_grads(l, big, dx):
        blocks = grad_blocks(big)
        if l == 0:
            return blocks, None
        ss, rs, srcs, lands, token = _exchange_start("grads_start_%d" % l, L + l, blocks, gather=False)
        sent[l] = (ss, rs, srcs, lands)
        return blocks, token[0, 0]

    small_w = {k: args[k] for k in SMALL_NAMES if k != "mod_b"}
    loss_part, grad_x, blocks, smalls, dfnw, dmods = _local_step(x, ctx, loss_target, mods, full_of, small_w, on_grads)
    loss = lax.psum(loss_part, ("x", "y", "c"))

    dm_ex = jnp.stack([dmods[l][:, 1, :6].reshape(B, 6 * D) for l in range(L)])
    dm_cc = jnp.stack([jnp.sum(dmods[l][:, 0, :6], axis=0).reshape(6 * D) for l in range(L)])
    small_parts = dict(
        mod_b=jnp.sum(dm_ex, axis=1) + dm_cc,
        **{k: jnp.stack([smalls[l][k] for l in range(L)]) for k in SMALL_NAMES[1:-1]},
        final_norm_w=dfnw, conv_w=jnp.stack([smalls[l]["conv_w"] for l in range(L)]), dm_cc=dm_cc)
    adam_grads = [small_parts[k] for k in SMALL_NAMES]
    extras = [small_parts["conv_w"], dm_cc, dm_ex]
    pack = jnp.concatenate([_pack(adam_grads), _pack(extras)], axis=0)
    pack_all = _all_gather("gather_small_grads", [pack])[0]
    wpack = _pack([args[k] for k in SMALL_NAMES])
    mpack = _pack([args["m_" + k] for k in SMALL_NAMES])
    vpack = _pack([args["v_" + k] for k in SMALL_NAMES])
    n_adam = wpack.shape[0]
    res_small = _adamw("adamw_small", pack_all[:, :n_adam], wpack, mpack, vpack, n_adam)
    small_out = [_unpack(r, [args[k] for k in SMALL_NAMES]) for r in res_small]
    ext_all = pack_all[:, n_adam:]
    ext_sum = ext_all[0]
    for k in range(1, NDEV):
        ext_sum = ext_sum + ext_all[k]
    g_conv_full, dm_cc_tot, _ = _unpack(ext_sum, extras)
    dm_ex_all = jnp.stack([_unpack(ext_all[k], extras)[2] for k in range(NDEV)], axis=1)
    dm_rows = jnp.concatenate([dm_ex_all.reshape(L, NDEV * B, 6 * D), dm_cc_tot[:, None, :],
                               jnp.zeros((L, 24 - NDEV * B - 1, 6 * D), F32)], axis=1)
    dm_loc = lax.dynamic_slice(dm_rows, (0, 0, me * CS), (L, 24, CS))
    g_modw, gc_part = _adaln_bwd(cs, dm_loc, mod_w)
    gc_all = _all_gather("gather_cctx_grad", [gc_part[NDEV * B:NDEV * B + 8]])[0]
    cpad = lambda a: jnp.pad(a[None], ((0, 7), (0, 0)))
    res_cc = _adamw("adamw_cctx", gc_all, cpad(c_ctx), cpad(m_c_ctx), cpad(v_c_ctx), 8)
    cc_out = [r[0] for r in res_cc]

    recv = {0: _all_to_all("exchange_grads", blocks[0])}
    for l in range(1, L):
        srcs, lands = _exchange_wait("grads_wait_%d" % l, *sent.pop(l), grad_x, gather=False)
        own = [lax.dynamic_index_in_dim(s, me, 0, keepdims=False) for s in srcs]
        recv[l] = _with_own(lands, own, me)
    recvs = [jnp.stack([recv[l][t] for l in range(L)], axis=1) for t in range(len(big_names))]
    big_out = {}
    for name, rv in zip(big_names, recvs):
        w = args[name]
        Rr, C = math.prod(w.shape[:-1]), w.shape[-1]
        rt = Rr if Rr * C <= (1 << 18) else Rr // 8
        res = _adamw("adamw_" + name, rv.reshape(NDEV, Rr, C), w.reshape(Rr, C), args["m_" + name].reshape(Rr, C),
                     args["v_" + name].reshape(Rr, C), rt)
        big_out[name] = [r.reshape(w.shape) for r in res]
    res = _adamw("adamw_mod_w", g_modw.reshape(1, L * D, CS), mod_w.reshape(L * D, CS), m_mod_w.reshape(L * D, CS),
                 v_mod_w.reshape(L * D, CS), L * D // 8)
    big_out["mod_w"] = [r.reshape(mod_w.shape) for r in res]
    CW = conv_w.shape[2]
    g_conv = lax.dynamic_slice(g_conv_full, (0, 0, me * CW), (L, 4, CW))
    res = _adamw("adamw_conv_w", g_conv.reshape(1, L * 4, CW), conv_w.reshape(L * 4, CW), m_conv_w.reshape(L * 4, CW),
                 v_conv_w.reshape(L * 4, CW), L * 4)
    big_out["conv_w"] = [r.reshape(conv_w.shape) for r in res]

    weights = ["c_ctx", "mod_w", "mod_b", "norm1_w", "norm2_w", "w_in", "conv_w", "conv_b", "dt_bias", "a_log", "ssd_d",
               "ssd_norm_w", "q_a_norm_w", "w_q_b", "kv_a_norm_w", "w_kv_b", "pool_w", "pool_scale", "w_out", "w_mlp1",
               "w_mlp2", "final_norm_w"]
    outs = [loss, grad_x]
    for kind in range(4):
        for name in weights:
            if name == "c_ctx":
                outs.append(cc_out[kind])
            elif name in big_out:
                outs.append(big_out[name][kind])
            else:
                outs.append(small_out[kind][SMALL_NAMES.index(name)])
    return tuple(outs)
```

```python
import functools
import math

import jax
import jax.numpy as jnp
from jax import lax
from jax.experimental import pallas as pl
from jax.experimental.pallas import tpu as pltpu

F32 = jnp.float32
BF = jnp.bfloat16
MXU = BF

D = 1024
CTX = 256
GRID_W = 64
EPS = 1e-6
H = 6
P = 64
SSD_IN = 384
NST = 128
XBC = 896
CH = 128
QL = 256
NOPE = 64
ROPE = 32
VH = 64
QK = 96
HP = 128
QW = H * HP
PD = 256
FF = 4096
IN_COLS = 2092
ROPE_THETA = 10000.0
PC = 2304
A_Z, A_XBC, A_QA, A_KVA, A_POOL, A_KR, A_DT = 0, 384, 1280, 1536, 1792, 2048, 2176
MIXW = SSD_IN + QW + PD
NDEV = 8
FSH = FF // NDEV
TM = 256
TQ = 256
TK = 256
VMEM_CAP = 64 * 1024 * 1024
ADAM_LR, ADAM_B1, ADAM_B2, ADAM_EPS, ADAM_WD, ADAM_STEP = 0.001, 0.9, 0.999, 1e-08, 0.01, 10


def _nbytes(shape, dtype):
    n = 1
    for s in shape:
        if s is not None:
            n *= s
    return n * jnp.dtype(dtype).itemsize


def _params(sem, block_bytes, extra=0):
    lim = min(2 * block_bytes + extra + (8 << 20), VMEM_CAP - (6 << 20))
    return pltpu.CompilerParams(dimension_semantics=sem, vmem_limit_bytes=int(lim))


def _hbm(arrays):
    return [pltpu.with_memory_space_constraint(a, pltpu.HBM) for a in arrays]


def _dot(a, b):
    return jnp.dot(a.astype(MXU), b.astype(MXU), preferred_element_type=F32)


def _dot_nt(a, b):
    return lax.dot_general(a.astype(MXU), b.astype(MXU), (((1,), (1,)), ((), ())), preferred_element_type=F32)


def _dot_tn(a, b):
    return lax.dot_general(a.astype(MXU), b.astype(MXU), (((0,), (0,)), ((), ())), preferred_element_type=F32)


def _dot01(m01, x):
    b16 = jnp.bfloat16
    m = m01.astype(b16)
    hi = x.astype(b16)
    r1 = x - hi.astype(F32)
    mid = r1.astype(b16)
    lo = (r1 - mid.astype(F32)).astype(b16)
    f = lambda v: jnp.dot(m, v, preferred_element_type=F32)
    return f(hi) + f(mid) + f(lo)


def _sigmoid(x):
    return 1.0 / (1.0 + jnp.exp(-x))


def _silu(x):
    return x * _sigmoid(x)


def _dsilu(x):
    s = _sigmoid(x)
    return s * (1.0 + x * (1.0 - s))


def _iota(shape, dim):
    return lax.broadcasted_iota(jnp.int32, shape, dim)


def _row(ref, k):
    blk = ref[...]
    return jnp.sum(jnp.where(_iota(blk.shape, 0) == k, blk, 0.0), axis=0, keepdims=True)


def _shift_rows(x, k):
    n = x.shape[0]
    return pltpu.roll(x, (-k) % n, axis=0)


class _Rows:
    def __init__(self, B, T):
        self.B, self.T = B, T
        self.nt = T // TM
        self.nct = CTX // TM

    def row(self, F):
        return pl.BlockSpec((None, TM, F), lambda b, i: (b, i, 0))

    def row2(self, F):
        return pl.BlockSpec((None, 2, TM, F), lambda b, i: (b, 0, i, 0))

    def prev8(self, F):
        return pl.BlockSpec((None, 8, F), lambda b, i: (b, jnp.maximum(i * (TM // 8) - 1, 0), 0))

    def next8(self, F):
        last = self.T // 8 - 1
        return pl.BlockSpec((None, 8, F), lambda b, i: (b, jnp.minimum((i + 1) * (TM // 8), last), 0))

    def mod(self):
        nct = self.nct
        return pl.BlockSpec((None, None, 8, D), lambda b, i: (b, jnp.where(i < nct, 0, 1), 0, 0))

    def const(self, shape):
        z = (0,) * len(shape)
        return pl.BlockSpec(tuple(shape), lambda b, i: z)

    def tgt(self, F):
        nct = self.nct
        return pl.BlockSpec((None, TM, F), lambda b, i: (b, jnp.maximum(i - nct, 0), 0))

    def call(self, name, body, ins, outs, scratch=(), extra=0):
        arrays = [a for a, _ in ins]
        in_specs = [s for _, s in ins]
        out_shape = [pltpu.HBM(o.shape, o.dtype) for o, _ in outs]
        out_specs = [s for _, s in outs]
        bb = 0
        for a, s in list(ins) + list(outs):
            if s.block_shape is not None:
                bb += _nbytes(s.block_shape, a.dtype)
        return pl.pallas_call(
            functools.partial(body), name=name, grid=(self.B, self.nt),
            in_specs=in_specs, out_specs=out_specs, out_shape=out_shape, scratch_shapes=list(scratch),
            compiler_params=_params(("arbitrary", "arbitrary"), bb, extra),
        )(*_hbm(arrays))

    def first_of_stream(self, i):
        return jnp.logical_or(i == 0, i == self.nct)

    def last_of_stream(self, i):
        return jnp.logical_or(i == self.nct - 1, i == self.nt - 1)

    def ext(self, i, prev_ref, cur, next_ref):
        pv = prev_ref[...].astype(F32) * jnp.where(self.first_of_stream(i), 0.0, 1.0)
        nx = next_ref[...].astype(F32) * jnp.where(self.last_of_stream(i), 0.0, 1.0)
        return jnp.concatenate([pv, cur, nx], axis=0)

    def stream_pos(self, i, rows):
        start = jnp.where(i < self.nct, 0, CTX)
        n = jnp.where(i < self.nct, CTX, self.T - CTX)
        t = i * TM - 8 - start + _iota((rows, 1), 0)
        return t, n


def _sds(shape, dtype):
    return jax.ShapeDtypeStruct(tuple(shape), dtype)


def _out(shape, dtype):
    return pltpu.HBM(tuple(shape), dtype)


def _norm_mod(x, nw, sh, sc):
    r = lax.rsqrt(jnp.mean(x * x, axis=-1, keepdims=True) + EPS)
    xn = x * r
    return xn * nw * (1.0 + sc) + sh, xn, r


def _norm_mod_bwd(dh, xn, r, nw, sc):
    dsh = jnp.sum(dh, axis=0, keepdims=True)
    dsc = jnp.sum(dh * (xn * nw), axis=0, keepdims=True)
    dnw = jnp.sum(dh * (1.0 + sc) * xn, axis=0, keepdims=True)
    dxn = dh * nw * (1.0 + sc)
    dx = r * (dxn - xn * jnp.mean(dxn * xn, axis=-1, keepdims=True))
    return dx, dsh, dsc, dnw


def _acc_rows(ref, first, rows):
    rid = _iota(ref.shape, 0)
    upd = jnp.zeros(ref.shape, F32)
    for k, v in rows.items():
        upd = upd + jnp.where(rid == k, v, 0.0)

    @pl.when(first)
    def _():
        ref[...] = upd

    @pl.when(jnp.logical_not(first))
    def _():
        ref[...] += upd


def _in_proj(R, x, mod, nw1, w_arr):
    B, T = R.B, R.T

    def body(x_ref, mod_ref, nw_ref, w_ref, z_ref, xbc_ref, qa_ref, kva_ref, pool_ref, kr_ref, dt_ref, h_ref):
        h, _, _ = _norm_mod(x_ref[...], nw_ref[...], _row(mod_ref, 0), _row(mod_ref, 1))
        hb = h.astype(BF)
        h_ref[...] = hb
        p = jnp.dot(hb, w_ref[...], preferred_element_type=F32)
        z_ref[...] = p[:, A_Z:A_XBC]
        xbc_ref[...] = p[:, A_XBC:A_QA]
        qa_ref[...] = p[:, A_QA:A_KVA]
        kva_ref[...] = p[:, A_KVA:A_POOL]
        pool_ref[...] = p[:, A_POOL:A_KR]
        kr_ref[...] = p[:, A_KR:A_DT]
        dt_ref[...] = p[:, A_DT:PC]

    widths = [SSD_IN, XBC, QL, QL, PD, HP, HP]
    outs = [(_sds((B, T, w), F32), R.row(w)) for w in widths] + [(_sds((B, T, D), BF), R.row(D))]
    return R.call("in_proj", body,
                  [(x, R.row(D)), (mod, R.mod()), (nw1, R.const((1, D))), (w_arr, R.const((D, PC)))],
                  outs, extra=8 << 20)


def _conv_pre(ext, w_ref, b_ref):
    return (_row(w_ref, 0) * _shift_rows(ext, -1)[8:8 + TM] + _row(w_ref, 1) * ext[8:8 + TM]
            + _row(w_ref, 2) * _shift_rows(ext, 1)[8:8 + TM] + _row(w_ref, 3) * _shift_rows(ext, 2)[8:8 + TM]
            + b_ref[...])


def _softplus(x):
    return jnp.maximum(x, 0.0) + jnp.log(1.0 + jnp.exp(-jnp.abs(x)))


def _ssd_prep(R, xbc_raw, dt_raw, conv_w8, conv_b, dtb):
    B, T = R.B, R.T

    def body(raw_ref, pv_ref, nx_ref, dtr_ref, w_ref, b_ref, dtb_ref, xbc_ref, dt_ref):
        i = pl.program_id(1)
        ext = R.ext(i, pv_ref, raw_ref[...], nx_ref)
        xbc_ref[...] = _silu(_conv_pre(ext, w_ref, b_ref))
        lane = _iota((TM, HP), 1)
        dtv = _softplus(dtr_ref[...] + dtb_ref[...])
        keep = lane < H
        dt_ref[0] = jnp.where(keep, dtv, 0.0)
        dt_ref[1] = jnp.where(keep, pltpu.roll(dtv, HP - H, axis=1), 0.0)

    return R.call("ssd_prep", body,
                  [(xbc_raw, R.row(XBC)), (xbc_raw, R.prev8(XBC)), (xbc_raw, R.next8(XBC)), (dt_raw, R.row(HP)),
                   (conv_w8, R.const((8, XBC))), (conv_b, R.const((1, XBC))), (dtb, R.const((1, HP)))],
                  [(_sds((B, T, XBC), F32), R.row(XBC)), (_sds((B, 2, T, HP), F32), R.row2(HP))], extra=12 << 20)


def _chunk_index(d, s, nc, nctc):
    if d == 0:
        return s
    return jnp.where(s < nctc, nctc - 1 - s, nc - 1 - (s - nctc))


def _scan_common(d, dtv, alog_ref):
    blk = alog_ref[d]
    a = -jnp.exp(jnp.sum(jnp.where(_iota(blk.shape, 0) == 0, blk, 0.0), axis=0, keepdims=True))
    adt = dtv * a
    row = _iota((CH, CH), 0)
    col = _iota((CH, CH), 1)
    inc = col <= row if d == 0 else col >= row
    inc_t = row <= col if d == 0 else row >= col
    q = _dot01(inc, adt)
    return a, adt, inc, inc_t, q


def _head_cols(h, adt, q, dtv, inc_t):
    onehot = (_iota((1, HP), 1) == h).astype(F32)
    adt_h = jnp.sum(adt * onehot, axis=1, keepdims=True)
    qc = jnp.sum(q * onehot, axis=1, keepdims=True)
    dt_h = jnp.sum(dtv * onehot, axis=1, keepdims=True)
    qr = jnp.sum(adt_h * inc_t.astype(F32), axis=0, keepdims=True)
    qtot = jnp.sum(adt_h, axis=0, keepdims=True)
    return onehot, qc, qr, qtot, dt_h


def _ssd_scan(B, T, xbc, dt2, alog2):
    nc, nctc = T // CH, CTX // CH

    def chain(d, xbc_ref, dt_ref, alog_ref, y_ref, hin_ref, hs):
        xbc_v = xbc_ref[...]
        dtv = dt_ref[...]
        _, adt, inc, inc_t, q = _scan_common(d, dtv, alog_ref)
        hin_ref[...] = hs[d]
        for g in range(2):
            bg = xbc_v[:, SSD_IN + NST * g:SSD_IN + NST * (g + 1)]
            cg = xbc_v[:, SSD_IN + 2 * NST + NST * g:SSD_IN + 2 * NST + NST * (g + 1)]
            gm = _dot_nt(cg, bg)
            for r in range(3):
                h = 3 * g + r
                _, qc, qr, qtot, dt_h = _head_cols(h, adt, q, dtv, inc_t)
                lm = jnp.where(inc, jnp.exp(qc - qr), 0.0)
                xh = xbc_v[:, P * h:P * (h + 1)] * dt_h
                hprev = hs[d, P * h:P * (h + 1), :]
                y = _dot(gm * lm, xh) + jnp.exp(qc) * _dot_nt(cg, hprev)
                y_ref[:, P * h:P * (h + 1)] = y
                hs[d, P * h:P * (h + 1), :] = jnp.exp(qtot) * hprev + _dot_tn(xh * jnp.exp(qtot - qc), bg)

    def body(xf_ref, xb_ref, dtf_ref, dtb_ref, alog_ref, yf_ref, yb_ref, hf_ref, hb_ref, hs):
        @pl.when(pl.program_id(1) == 0)
        def _():
            hs[...] = jnp.zeros(hs.shape, F32)

        chain(0, xf_ref, dtf_ref, alog_ref, yf_ref, hf_ref, hs)
        chain(1, xb_ref, dtb_ref, alog_ref, yb_ref, hb_ref, hs)

    cidx = lambda d: (lambda s: _chunk_index(d, s, nc, nctc))
    specs = lambda d: dict(
        xbc=pl.BlockSpec((None, CH, XBC), lambda b, s: (b, cidx(d)(s), 0)),
        dt=pl.BlockSpec((None, None, CH, HP), lambda b, s: (b, d, cidx(d)(s), 0)),
        y=pl.BlockSpec((None, CH, SSD_IN), lambda b, s: (b, cidx(d)(s), 0)),
        h=pl.BlockSpec((None, None, SSD_IN, NST), lambda b, s: (b, cidx(d)(s), 0, 0)))
    f, r = specs(0), specs(1)
    bb = 2 * (_nbytes((CH, XBC), F32) + _nbytes((CH, HP), F32) + _nbytes((CH, SSD_IN), F32) + _nbytes((SSD_IN, NST), F32))
    return pl.pallas_call(
        functools.partial(body), name="ssd_scan", grid=(B, nc),
        in_specs=[f["xbc"], r["xbc"], f["dt"], r["dt"], pl.BlockSpec((2, 8, HP), lambda b, s: (0, 0, 0))],
        out_specs=[f["y"], r["y"], f["h"], r["h"]],
        out_shape=[_out((B, T, SSD_IN), F32)] * 2 + [_out((B, nc, SSD_IN, NST), F32)] * 2,
        scratch_shapes=[pltpu.VMEM((2, SSD_IN, NST), F32)],
        compiler_params=_params(("arbitrary",) * 2, bb, 8 << 20),
    )(*_hbm([xbc, xbc, dt2, dt2, alog2]))


def _swap8(u):
    lane = _iota(u.shape, 1)
    n = u.shape[1]
    return jnp.where((lane & 15) < 8, pltpu.roll(u, n - 8, axis=1), pltpu.roll(u, 8, axis=1))


def _rope(u, cos, sin_signed):
    return u * cos + _swap8(u) * sin_signed


def _rms(x, w):
    r = lax.rsqrt(jnp.mean(x * x, axis=-1, keepdims=True) + EPS)
    xh = x * r
    return xh * w, xh, r


def _rms_bwd(dy, xh, r, w):
    dw = jnp.sum(dy * xh, axis=0, keepdims=True)
    dxh = dy * w
    return r * (dxh - xh * jnp.mean(dxh * xh, axis=-1, keepdims=True)), dw


def _tile6(t):
    return jnp.concatenate([t] * H, axis=1)


def _per_head(fn, u):
    return jnp.concatenate([fn(u[:, HP * h:HP * (h + 1)]) for h in range(H)], axis=1)


def _mla_prep(R, qa, kva, kr, qnw, kvnw, wq, wk, wv, cos, sin):
    B, T = R.B, R.T
    scale = QK ** -0.5

    def body(qa_ref, kva_ref, kr_ref, qnw_ref, kvnw_ref, wq_ref, wk_ref, wv_ref, cos_ref, sin_ref,
             qt_ref, kt_ref, qtr_ref, ktr_ref, vtr_ref, cq_ref, ckv_ref):
        cq, _, _ = _rms(qa_ref[...], qnw_ref[...])
        ckv, _, _ = _rms(kva_ref[...], kvnw_ref[...])
        cqb, ckvb = cq.astype(BF), ckv.astype(BF)
        cq_ref[...] = cqb
        ckv_ref[...] = ckvb
        cos1, sin1 = cos_ref[...], sin_ref[...]
        q = _per_head(lambda u: _rope(u, cos1, sin1), jnp.dot(cqb, wq_ref[...], preferred_element_type=F32)) * scale
        qt_ref[...] = q.astype(BF)
        kk = _rope(kr_ref[...], cos1, sin1)
        k = jnp.dot(ckvb, wk_ref[...], preferred_element_type=F32) + _tile6(kk)
        kt_ref[...] = k.astype(BF)
        v = jnp.dot(ckvb, wv_ref[...], preferred_element_type=F32)
        v = jnp.where((_iota((TM, QW), 1) & (HP - 1)) == VH, 1.0, v)
        for h in range(H):
            cols = slice(HP * h, HP * (h + 1))
            qtr_ref[h] = q[:, cols].T.astype(BF)
            ktr_ref[h] = k[:, cols].T.astype(BF)
            vtr_ref[h] = v[:, cols].T.astype(BF)

    tr = (_sds((B, H, HP, T), BF), pl.BlockSpec((None, H, HP, TM), lambda b, i: (b, 0, 0, i)))
    return R.call("mla_prep", body,
                  [(qa, R.row(QL)), (kva, R.row(QL)), (kr, R.row(HP)), (qnw, R.const((1, QL))), (kvnw, R.const((1, QL))),
                   (wq, R.const((QL, QW))), (wk, R.const((QL, QW))), (wv, R.const((QL, QW))),
                   (cos, pl.BlockSpec((TM, HP), lambda b, i: (i, 0))), (sin, pl.BlockSpec((TM, HP), lambda b, i: (i, 0)))],
                  [(_sds((B, T, QW), BF), R.row(QW))] * 2 + [tr] * 3 + [(_sds((B, T, QL), BF), R.row(QL))] * 2, extra=12 << 20)


def _flash_fwd(B, T, q_t, kt, v_t):
    nq, nk = T // TQ, T // TK

    def body(q_ref, k_ref, v_ref, o_ref, lse_ref, s_scr):
        i = pl.program_id(2)
        q_tr = q_ref[...]

        def attend(nch):
            mrun = None
            for j in range(nch):
                s = _dot(k_ref[TK * j:TK * (j + 1), :], q_tr)
                s_scr[j] = s
                mrun = s if mrun is None else jnp.maximum(mrun, s)
            m = jnp.max(mrun, axis=0, keepdims=True)
            acc_t = jnp.zeros((HP, TQ), F32)
            for j in range(nch):
                acc_t = acc_t + _dot(v_ref[:, TK * j:TK * (j + 1)], jnp.exp(s_scr[j] - m))
            row = _iota((HP, TQ), 0)
            l = jnp.sum(jnp.where(row == VH, acc_t, 0.0), axis=0, keepdims=True)
            o_ref[...] = jnp.where(row < VH, acc_t / l, 0.0).T
            lse_ref[...] = jnp.broadcast_to(m + jnp.log(l), (HP, TQ)).T

        @pl.when(i < CTX // TQ)
        def _():
            attend(CTX // TK)

        @pl.when(i >= CTX // TQ)
        def _():
            attend(nk)

    bb = _nbytes((TQ, HP), BF) + 2 * _nbytes((T, HP), BF) + _nbytes((TQ, HP), F32) + _nbytes((8, TQ), F32)
    return pl.pallas_call(
        functools.partial(body), name="flash_fwd", grid=(B, H, nq),
        in_specs=[pl.BlockSpec((None, None, HP, TQ), lambda b, h, i: (b, h, 0, i)),
                  pl.BlockSpec((None, T, HP), lambda b, h, i: (b, 0, h)),
                  pl.BlockSpec((None, None, HP, T), lambda b, h, i: (b, h, 0, 0))],
        out_specs=[pl.BlockSpec((None, TQ, HP), lambda b, h, i: (b, i, h)),
                   pl.BlockSpec((None, None, TQ, HP), lambda b, h, i: (b, h, i, 0))],
        out_shape=[_out((B, T, QW), F32), _out((B, H, T, HP), F32)],
        scratch_shapes=[pltpu.VMEM((nk, TK, TQ), F32)],
        compiler_params=_params(("arbitrary",) * 3, bb, _nbytes((nk, TK, TQ), F32) + (8 << 20)),
    )(*_hbm([q_t, kt, v_t]))


def _pool_terms(R, i, rows):
    t, n = R.stream_pos(i, rows)
    lane = _iota((1, PD), 1)
    half = jnp.where(lane < 64, 1, jnp.where(lane < 128, 2, jnp.where(lane < 192, 4, 8)))
    cnt = (jnp.minimum(t + half, n) - jnp.maximum(t - half, 0)).astype(F32)
    valid = jnp.logical_and(t >= 0, t < n)
    return jnp.where(valid, cnt, 1.0), valid.astype(F32), lane


def _lane_select(lane, a2, a4, a8, a16):
    return jnp.where(lane < 64, a2, jnp.where(lane < 128, a4, jnp.where(lane < 192, a8, a16)))


def _pool_centred(R, i, ext):
    cnt, valid, lane = _pool_terms(R, i, ext.shape[0])
    s2 = ext + _shift_rows(ext, -1)
    s4 = _shift_rows(s2, -1) + _shift_rows(s2, 1)
    s8 = _shift_rows(s4, -2) + _shift_rows(s4, 2)
    s16 = _shift_rows(s8, -4) + _shift_rows(s8, 4)
    return _lane_select(lane, s2, s4, s8, s16) / cnt - ext, cnt, valid, lane


def _pool_fwd(R, u, wbd, scale):
    B, T = R.B, R.T

    def body(u_ref, pv_ref, nx_ref, w_ref, sc_ref, o_ref):
        i = pl.program_id(1)
        ext = R.ext(i, pv_ref, u_ref[...], nx_ref)
        dm, _, _, _ = _pool_centred(R, i, ext)
        o_ref[...] = _dot(dm[8:8 + TM], w_ref[...]) * sc_ref[...]

    return R.call("pool_fwd", body,
                  [(u, R.row(PD)), (u, R.prev8(PD)), (u, R.next8(PD)), (wbd, R.const((PD, PD))), (scale, R.const((1, PD)))],
                  [(_sds((B, T, PD), F32), R.row(PD))], extra=8 << 20)


def _group_mask():
    return _iota((1, SSD_IN), 1) < SSD_IN // 2


def _ssd_gate(yf_ref, yb_ref, xbc_ref, z_ref, dsk_ref):
    ytot = yf_ref[...] + yb_ref[...] + xbc_ref[:, 0:SSD_IN] * dsk_ref[...]
    z = z_ref[...]
    gz = ytot * _silu(z)
    g0 = _group_mask()
    sq = gz * gz
    s0 = jnp.sum(jnp.where(g0, sq, 0.0), axis=1, keepdims=True)
    s1 = jnp.sum(jnp.where(g0, 0.0, sq), axis=1, keepdims=True)
    half = SSD_IN // 2
    r = jnp.where(g0, lax.rsqrt(s0 / half + EPS), lax.rsqrt(s1 / half + EPS))
    return ytot, z, gz, r


def _out_proj(R, x, mod, yf, yb, xbc, z, o, pool, dsk, snw, wout):
    B, T = R.B, R.T

    def body(x_ref, mod_ref, yf_ref, yb_ref, xbc_ref, z_ref, o_ref, pool_ref, dsk_ref, snw_ref, w_ref, xmid_ref, cat_ref,
             mix_ref):
        _, _, gz, r = _ssd_gate(yf_ref, yb_ref, xbc_ref, z_ref, dsk_ref)
        cat_ref[:, 0:SSD_IN] = (gz * r * snw_ref[...]).astype(BF)
        cat_ref[:, SSD_IN:SSD_IN + QW] = o_ref[...].astype(BF)
        cat_ref[:, SSD_IN + QW:MIXW] = pool_ref[...].astype(BF)
        mix = jnp.dot(cat_ref[...], w_ref[...], preferred_element_type=F32)
        mix_ref[...] = mix.astype(BF)
        xmid_ref[...] = x_ref[...] + _row(mod_ref, 2) * mix

    return R.call("out_proj", body,
                  [(x, R.row(D)), (mod, R.mod()), (yf, R.row(SSD_IN)), (yb, R.row(SSD_IN)), (xbc, R.row(XBC)), (z, R.row(SSD_IN)),
                   (o, R.row(QW)),
                   (pool, R.row(PD)), (dsk, R.const((1, SSD_IN))), (snw, R.const((1, SSD_IN))), (wout, R.const((MIXW, D)))],
                  [(_sds((B, T, D), F32), R.row(D)), (_sds((B, T, MIXW), BF), R.row(MIXW)), (_sds((B, T, D), BF), R.row(D))],
                  extra=8 << 20)


def _load_once(first, pairs, sem):
    @pl.when(first)
    def _():
        cps = [pltpu.make_async_copy(src, dst, sem.at[k]) for k, (src, dst) in enumerate(pairs)]
        for cp in cps:
            cp.start()
        for cp in cps:
            cp.wait()


ANY = pl.BlockSpec(memory_space=pl.ANY)


def _mlp_fwd(R, xmid, mod, nw2, w1, w2):
    B, T = R.B, R.T

    def body(x_ref, mod_ref, nw_ref, w1_hbm, w2_hbm, xo_ref, h_ref, u_ref, y_ref, w1_v, w2_v, sem):
        first = jnp.logical_and(pl.program_id(0) == 0, pl.program_id(1) == 0)
        _load_once(first, [(w1_hbm, w1_v), (w2_hbm, w2_v)], sem)
        x = x_ref[...]
        h, _, _ = _norm_mod(x, nw_ref[...], _row(mod_ref, 3), _row(mod_ref, 4))
        hb = h.astype(BF)
        h_ref[...] = hb
        y = jnp.zeros((TM, D), F32)
        for j in range(NDEV):
            u = jnp.dot(hb, w1_v[j], preferred_element_type=F32)
            u_ref[:, FSH * j:FSH * (j + 1)] = u.astype(BF)
            a = jnp.square(jnp.maximum(u, 0.0))
            y = y + jnp.dot(a.astype(BF), w2_v[j], preferred_element_type=F32)
        y_ref[...] = y.astype(BF)
        xo_ref[...] = x + _row(mod_ref, 5) * y

    return R.call("mlp_fwd", body,
                  [(xmid, R.row(D)), (mod, R.mod()), (nw2, R.const((1, D))), (w1, ANY), (w2, ANY)],
                  [(_sds((B, T, D), F32), R.row(D)), (_sds((B, T, D), BF), R.row(D)), (_sds((B, T, FF), BF), R.row(FF)),
                   (_sds((B, T, D), BF), R.row(D))],
                  scratch=[pltpu.VMEM((NDEV, D, FSH), w1.dtype), pltpu.VMEM((NDEV, FSH, D), w2.dtype), pltpu.SemaphoreType.DMA((2,))],
                  extra=(2 * _nbytes((NDEV, D, FSH), BF)) + (8 << 20))


def _loss_head(R, x, tgt, fnw):
    B, T = R.B, R.T

    def body(x_ref, t_ref, w_ref, dx_ref, loss_ref, dw_ref):
        b, i = pl.program_id(0), pl.program_id(1)
        live = jnp.where(i >= R.nct, 1.0, 0.0)
        y, xh, r = _rms(x_ref[...], w_ref[...])
        err = (y - t_ref[...]) * live
        dy = err / D
        dxn, dw = _rms_bwd(dy, xh, r, w_ref[...])
        dx_ref[...] = dxn
        first = jnp.logical_and(b == 0, i == 0)
        part = 0.5 * jnp.sum(jnp.sum(err * err, axis=1, keepdims=True), axis=0, keepdims=True) / D
        _acc_rows(loss_ref, first, {0: jnp.broadcast_to(part, (1, HP))})
        _acc_rows(dw_ref, first, {0: dw})

    return R.call("loss_head", body,
                  [(x, R.row(D)), (tgt, R.tgt(D)), (fnw, R.const((1, D)))],
                  [(_sds((B, T, D), F32), R.row(D)), (_sds((8, HP), F32), R.const((8, HP))), (_sds((8, D), F32), R.const((8, D)))],
                  extra=8 << 20)


def _tn_matmul(name, a, b, tn):
    B, T, K = a.shape
    N = b.shape[2]
    nk = 1
    while T % nk or (T // nk) > 1088 or (T // nk) % 16:
        nk += 1
    tk = T // nk
    kt = K if K <= 1536 else 1024
    assert K % kt == 0 and N % tn == 0

    def body(a_ref, b_ref, o_ref):
        first = jnp.logical_and(pl.program_id(2) == 0, pl.program_id(3) == 0)

        @pl.when(first)
        def _():
            o_ref[...] = jnp.zeros(o_ref.shape, F32)

        o_ref[...] += _dot_tn(a_ref[...], b_ref[...])

    bb = _nbytes((tk, kt), a.dtype) + _nbytes((tk, tn), b.dtype) + _nbytes((kt, tn), F32)
    return pl.pallas_call(
        functools.partial(body), name=name, grid=(N // tn, K // kt, B, nk),
        in_specs=[pl.BlockSpec((None, tk, kt), lambda j, kk, bi, t: (bi, t, kk)),
                  pl.BlockSpec((None, tk, tn), lambda j, kk, bi, t: (bi, t, j))],
        out_specs=pl.BlockSpec((None, kt, tn), lambda j, kk, bi, t: (j, kk, 0)),
        out_shape=_out((N // tn, K, tn), F32),
        compiler_params=_params(("arbitrary",) * 4, bb, 8 << 20),
    )(*_hbm([a, b]))


def _mlp_bwd(R, dxo, xmid, ubf, ybf, mod, nw2, w1, w2):
    B, T = R.B, R.T

    def body(dxo_ref, x_ref, u_ref, y_ref, mod_ref, nw_ref, w1_hbm, w2_hbm,
             dxm_ref, du_ref, a_ref, dy_ref, dmod_ref, dnw_ref, w1_v, w2_v, sem):
        b, i = pl.program_id(0), pl.program_id(1)
        _load_once(jnp.logical_and(b == 0, i == 0), [(w1_hbm, w1_v), (w2_hbm, w2_v)], sem)
        dxo = dxo_ref[...]
        _, xn, r = _norm_mod(x_ref[...], nw_ref[...], _row(mod_ref, 3), _row(mod_ref, 4))
        dyb = (dxo * _row(mod_ref, 5)).astype(BF)
        dy_ref[...] = dyb
        dg2 = jnp.sum(dxo * y_ref[...].astype(F32), axis=0, keepdims=True)
        dh = jnp.zeros((TM, D), F32)
        for j in range(NDEV):
            rl = jnp.maximum(u_ref[:, FSH * j:FSH * (j + 1)].astype(F32), 0.0)
            a_ref[:, FSH * j:FSH * (j + 1)] = (rl * rl).astype(BF)
            du = (_dot_nt(dyb, w2_v[j]) * (2.0 * rl)).astype(BF)
            du_ref[:, FSH * j:FSH * (j + 1)] = du
            dh = dh + _dot_nt(du, w1_v[j])
        dx, dsh, dsc, dnw = _norm_mod_bwd(dh, xn, r, nw_ref[...], _row(mod_ref, 4))
        dxm_ref[...] = dxo + dx
        _acc_rows(dmod_ref, R.first_of_stream(i), {3: dsh, 4: dsc, 5: dg2})
        _acc_rows(dnw_ref, jnp.logical_and(b == 0, i == 0), {0: dnw})

    return R.call("mlp_bwd", body,
                  [(dxo, R.row(D)), (xmid, R.row(D)), (ubf, R.row(FF)), (ybf, R.row(D)), (mod, R.mod()), (nw2, R.const((1, D))),
                   (w1, ANY), (w2, ANY)],
                  [(_sds((B, T, D), F32), R.row(D)), (_sds((B, T, FF), BF), R.row(FF)), (_sds((B, T, FF), BF), R.row(FF)),
                   (_sds((B, T, D), BF), R.row(D)), (_sds((B, 2, 8, D), F32), R.mod()), (_sds((8, D), F32), R.const((8, D)))],
                  scratch=[pltpu.VMEM((NDEV, D, FSH), w1.dtype), pltpu.VMEM((NDEV, FSH, D), w2.dtype), pltpu.SemaphoreType.DMA((2,))],
                  extra=(2 * _nbytes((NDEV, D, FSH), BF)) + (8 << 20))


def _out_bwd(R, dxm, mod, mixbf, yf, yb, xbc, z, o, dsk, snw, wout):
    B, T = R.B, R.T

    def body(dxm_ref, mod_ref, mix_ref, yf_ref, yb_ref, xbc_ref, z_ref, o_ref, dsk_ref, snw_ref, w_ref,
             dmix_ref, dz_ref, dyt_ref, dxsk_ref, do_ref, dotr_ref, dd_ref, dpo_ref, dmod_ref, dvec_ref):
        b, i = pl.program_id(0), pl.program_id(1)
        dxm = dxm_ref[...]
        dmixb = (dxm * _row(mod_ref, 2)).astype(BF)
        dmix_ref[...] = dmixb
        dg1 = jnp.sum(dxm * mix_ref[...].astype(F32), axis=0, keepdims=True)
        dcat = _dot_nt(dmixb, w_ref[...])
        do_v = dcat[:, SSD_IN:SSD_IN + QW]
        do_ref[...] = do_v.astype(BF)
        doo = do_v * o_ref[...]
        for h in range(H):
            dd_ref[h] = jnp.broadcast_to(jnp.sum(doo[:, HP * h:HP * (h + 1)], axis=1, keepdims=True), (TM, HP))
            dotr_ref[h] = do_v[:, HP * h:HP * (h + 1)].T.astype(BF)
        dpo_ref[...] = dcat[:, SSD_IN + QW:MIXW]
        dsn = dcat[:, 0:SSD_IN]
        ytot, zv, gz, r = _ssd_gate(yf_ref, yb_ref, xbc_ref, z_ref, dsk_ref)
        gh = gz * r
        dsnw = jnp.sum(dsn * gh, axis=0, keepdims=True)
        dgh = dsn * snw_ref[...]
        g0 = _group_mask()
        pr = dgh * gh
        half = SSD_IN // 2
        m0 = jnp.sum(jnp.where(g0, pr, 0.0), axis=1, keepdims=True) / half
        m1 = jnp.sum(jnp.where(g0, 0.0, pr), axis=1, keepdims=True) / half
        dgz = r * (dgh - gh * jnp.where(g0, m0, m1))
        dyt = dgz * _silu(zv)
        dz_ref[...] = (dgz * ytot * _dsilu(zv)).astype(BF)
        dyt_ref[...] = dyt
        dxsk_ref[...] = dyt * dsk_ref[...]
        ddsk = jnp.sum(dyt * xbc_ref[:, 0:SSD_IN], axis=0, keepdims=True)
        _acc_rows(dmod_ref, R.first_of_stream(i), {2: dg1})
        _acc_rows(dvec_ref, jnp.logical_and(b == 0, i == 0), {0: dsnw, 1: ddsk})

    return R.call("out_bwd", body,
                  [(dxm, R.row(D)), (mod, R.mod()), (mixbf, R.row(D)), (yf, R.row(SSD_IN)), (yb, R.row(SSD_IN)), (xbc, R.row(XBC)),
                   (z, R.row(SSD_IN)),
                   (o, R.row(QW)), (dsk, R.const((1, SSD_IN))), (snw, R.const((1, SSD_IN))), (wout, R.const((MIXW, D)))],
                  [(_sds((B, T, D), BF), R.row(D)), (_sds((B, T, SSD_IN), BF), R.row(SSD_IN)), (_sds((B, T, SSD_IN), F32), R.row(SSD_IN)),
                   (_sds((B, T, SSD_IN), F32), R.row(SSD_IN)), (_sds((B, T, QW), BF), R.row(QW)),
                   (_sds((B, H, HP, T), BF), pl.BlockSpec((None, H, HP, TM), lambda b, i: (b, 0, 0, i))),
                   (_sds((B, H, T, HP), F32), pl.BlockSpec((None, H, TM, HP), lambda b, i: (b, 0, i, 0))),
                   (_sds((B, T, PD), F32), R.row(PD)),
                   (_sds((B, 2, 8, D), F32), R.mod()), (_sds((8, SSD_IN), F32), R.const((8, SSD_IN)))],
                  extra=8 << 20)


def _pool_bwd(R, dpo, u, wbd, scale):
    B, T = R.B, R.T

    def body(d_ref, dpv_ref, dnx_ref, u_ref, upv_ref, unx_ref, w_ref, sc_ref, du_ref, dw_ref, dsc_ref):
        b, i = pl.program_id(0), pl.program_id(1)
        ext_u = R.ext(i, upv_ref, u_ref[...], unx_ref)
        ext_d = R.ext(i, dpv_ref, d_ref[...], dnx_ref)
        dm, cnt, valid, lane = _pool_centred(R, i, ext_u)
        ddm = _dot_nt(ext_d * sc_ref[...], w_ref[...]) * valid
        e = ddm / cnt
        a2 = e + _shift_rows(e, 1)
        a4 = _shift_rows(a2, -1) + _shift_rows(a2, 1)
        a8 = _shift_rows(a4, -2) + _shift_rows(a4, 2)
        a16 = _shift_rows(a8, -4) + _shift_rows(a8, 4)
        du_ref[...] = (_lane_select(lane, a2, a4, a8, a16) - ddm)[8:8 + TM].astype(BF)
        dmc = dm[8:8 + TM]
        dpo_c = d_ref[...]
        first = jnp.logical_and(b == 0, i == 0)

        @pl.when(first)
        def _():
            dw_ref[...] = jnp.zeros(dw_ref.shape, F32)

        dw_ref[...] += _dot_tn(dmc, dpo_c * sc_ref[...])
        _acc_rows(dsc_ref, first, {0: jnp.sum(dpo_c * _dot(dmc, w_ref[...]), axis=0, keepdims=True)})

    return R.call("pool_bwd", body,
                  [(dpo, R.row(PD)), (dpo, R.prev8(PD)), (dpo, R.next8(PD)), (u, R.row(PD)), (u, R.prev8(PD)), (u, R.next8(PD)),
                   (wbd, R.const((PD, PD))), (scale, R.const((1, PD)))],
                  [(_sds((B, T, PD), BF), R.row(PD)), (_sds((PD, PD), F32), R.const((PD, PD))), (_sds((8, PD), F32), R.const((8, PD)))],
                  extra=8 << 20)


def _flash_bwd(B, T, qt, kt, dob, q_t, k_t, v_t, do_t, lse_b, dd_b):
    nk, nq = T // TK, T // TQ

    def body(k_ref, kt_ref, vt_ref, q_ref, do_ref, qt_ref, dot_ref, lse_ref, dd_ref, dq_ref, dk_ref, dv_ref,
             s_scr, dp_scr, ds_scr):
        j = pl.program_id(2)
        s_scr[...] = _dot(q_ref[...], kt_ref[...])
        dp_scr[...] = _dot(do_ref[...], vt_ref[...])
        ctx_keys = jnp.where(j < CTX // TK, 1.0, 0.0)
        dk_tr = jnp.zeros((HP, TK), F32)
        dv_tr = jnp.zeros((HP, TK), F32)
        for i in range(nq):
            rows = slice(TQ * i, TQ * (i + 1))
            lse2 = jnp.concatenate([lse_ref[rows, :]] * (TK // HP), axis=1)
            dd2 = jnp.concatenate([dd_ref[rows, :]] * (TK // HP), axis=1)
            p = jnp.exp(s_scr[rows, :] - lse2)
            if i < CTX // TQ:
                p = p * ctx_keys
            ds = (p * (dp_scr[rows, :] - dd2)).astype(BF)
            ds_scr[rows, :] = ds
            dv_tr = dv_tr + _dot(dot_ref[:, rows], p)
            dk_tr = dk_tr + _dot(qt_ref[:, rows], ds)
        dq = _dot(ds_scr[...], k_ref[...])

        @pl.when(j == 0)
        def _():
            dq_ref[...] = dq

        @pl.when(j > 0)
        def _():
            dq_ref[...] += dq

        dk_ref[...] = dk_tr.T
        dv_ref[...] = dv_tr.T

    tspec = pl.BlockSpec((None, TK, HP), lambda b, h, j: (b, j, h))
    fspec = pl.BlockSpec((None, T, HP), lambda b, h, j: (b, 0, h))
    ttspec = pl.BlockSpec((None, None, HP, TK), lambda b, h, j: (b, h, 0, j))
    ftspec = pl.BlockSpec((None, None, HP, T), lambda b, h, j: (b, h, 0, 0))
    bspec = pl.BlockSpec((None, None, T, HP), lambda b, h, j: (b, h, 0, 0))
    bb = 4 * _nbytes((T, HP), BF) + 3 * _nbytes((T, HP), F32) + 8 * _nbytes((TK, HP), F32)
    scr = 2 * _nbytes((T, TK), F32) + _nbytes((T, TK), BF)
    return pl.pallas_call(
        functools.partial(body), name="flash_bwd", grid=(B, H, nk),
        in_specs=[tspec, ttspec, ttspec, fspec, fspec, ftspec, ftspec, bspec, bspec],
        out_specs=[fspec, tspec, tspec],
        out_shape=[_out((B, T, QW), F32)] * 3,
        scratch_shapes=[pltpu.VMEM((T, TK), F32), pltpu.VMEM((T, TK), F32), pltpu.VMEM((T, TK), BF)],
        compiler_params=_params(("arbitrary",) * 3, bb, scr + (12 << 20)),
    )(*_hbm([kt, k_t, v_t, qt, dob, q_t, do_t, lse_b, dd_b]))


def _mla_bwd(R, dqt, dkt, dvt, qa, kva, qnw, kvnw, wq, wk, wv, cos, sin):
    B, T = R.B, R.T
    scale = QK ** -0.5

    def body(dq_ref, dk_ref, dv_ref, qa_ref, kva_ref, qnw_ref, kvnw_ref, wq_ref, wk_ref, wv_ref, cos_ref, sin_ref,
             dqp_ref, dkb_ref, dvb_ref, dqa_ref, dkva_ref, dkr_ref, dnw_ref):
        b, i = pl.program_id(0), pl.program_id(1)
        cos1, sin1 = cos_ref[...], sin_ref[...]
        dq = dq_ref[...] * scale
        dqp = _per_head(lambda g: g * cos1 + _swap8(g * sin1), dq).astype(BF)
        dqp_ref[...] = dqp
        dkv = dk_ref[...]
        dkb = dkv.astype(BF)
        dkb_ref[...] = dkb
        dvb = dv_ref[...].astype(BF)
        dvb_ref[...] = dvb
        dkk = dkv[:, 0:HP]
        for h in range(1, H):
            dkk = dkk + dkv[:, HP * h:HP * (h + 1)]
        lane = _iota((TM, HP), 1)
        rope_lane = jnp.logical_and(lane >= NOPE, lane < NOPE + ROPE)
        dkr_ref[...] = jnp.where(rope_lane, dkk * cos1 + _swap8(dkk * sin1), 0.0).astype(BF)
        _, qh, qr = _rms(qa_ref[...], qnw_ref[...])
        _, kh, kr_ = _rms(kva_ref[...], kvnw_ref[...])
        dcq = _dot_nt(dqp, wq_ref[...])
        dckv = _dot_nt(dkb, wk_ref[...]) + _dot_nt(dvb, wv_ref[...])
        dqa, dqw = _rms_bwd(dcq, qh, qr, qnw_ref[...])
        dkva, dkw = _rms_bwd(dckv, kh, kr_, kvnw_ref[...])
        dqa_ref[...] = dqa.astype(BF)
        dkva_ref[...] = dkva.astype(BF)
        _acc_rows(dnw_ref, jnp.logical_and(b == 0, i == 0), {0: dqw, 1: dkw})

    tab = pl.BlockSpec((TM, HP), lambda b, i: (i, 0))
    return R.call("mla_bwd", body,
                  [(dqt, R.row(QW)), (dkt, R.row(QW)), (dvt, R.row(QW)), (qa, R.row(QL)), (kva, R.row(QL)),
                   (qnw, R.const((1, QL))), (kvnw, R.const((1, QL))), (wq, R.const((QL, QW))), (wk, R.const((QL, QW))),
                   (wv, R.const((QL, QW))), (cos, tab), (sin, tab)],
                  [(_sds((B, T, QW), BF), R.row(QW))] * 3 + [(_sds((B, T, QL), BF), R.row(QL))] * 2
                  + [(_sds((B, T, HP), BF), R.row(HP)), (_sds((8, QL), F32), R.const((8, QL)))], extra=8 << 20)


def _ssd_scan_bwd(B, T, dyt, xbc, dt2, alog2, hin_f, hin_b):
    nc, nctc = T // CH, CTX // CH

    def chain(d, dy_ref, xbc_ref, dt_ref, alog_ref, hin_ref, dxbc_ref, ddt_ref, da_ref, dhs_all):
        dhs = dhs_all.at[d]
        xbc_v = xbc_ref[...]
        dtv = dt_ref[...]
        dyv = dy_ref[...]
        a, adt, inc, inc_t, q = _scan_common(d, dtv, alog_ref)
        dq_all = jnp.zeros((CH, HP), F32)
        dqtot_all = jnp.zeros((1, HP), F32)
        ddt_x = jnp.zeros((CH, HP), F32)
        for g in range(2):
            bg = xbc_v[:, SSD_IN + NST * g:SSD_IN + NST * (g + 1)]
            cg = xbc_v[:, SSD_IN + 2 * NST + NST * g:SSD_IN + 2 * NST + NST * (g + 1)]
            gm = _dot_nt(cg, bg)
            gm_t = _dot_nt(bg, cg)
            dgm = jnp.zeros((CH, CH), F32)
            dbg = jnp.zeros((CH, NST), F32)
            dcg = jnp.zeros((CH, NST), F32)
            for r in range(3):
                h = 3 * g + r
                onehot, qc, qr, qtot, dt_h = _head_cols(h, adt, q, dtv, inc_t)
                lm = jnp.where(inc, jnp.exp(qc - qr), 0.0)
                xs_h = xbc_v[:, P * h:P * (h + 1)]
                xh = xs_h * dt_h
                sm = gm * lm
                dy_h = dyv[:, P * h:P * (h + 1)]
                hprev = hin_ref[P * h:P * (h + 1), :]
                dho = dhs[P * h:P * (h + 1), :]
                eq = jnp.exp(qc)
                etot = jnp.exp(qtot)
                dte = jnp.exp(qtot - qc)
                ds = _dot_nt(dy_h, xh)
                dx = _dot_tn(sm, dy_h)
                edy = eq * dy_h
                dq_col = jnp.sum(dy_h * (eq * _dot_nt(cg, hprev)), axis=1, keepdims=True)
                dcg = dcg + _dot(edy, hprev)
                dhin = _dot_tn(edy, cg) + etot * dho
                zs = _dot_nt(bg, dho)
                dx = dx + dte * zs
                w = dte * jnp.sum(xh * zs, axis=1, keepdims=True)
                dbg = dbg + _dot(xh * dte, dho)
                dqtot = jnp.sum(jnp.sum(hprev * dho, axis=1, keepdims=True), axis=0, keepdims=True) * etot \
                    + jnp.sum(w, axis=0, keepdims=True)
                dq_col = dq_col - w
                dgm = dgm + ds * lm
                lm_t = jnp.where(inc_t, jnp.exp(qr - qc), 0.0)
                e_t = _dot_nt(xh, dy_h) * gm_t * lm_t
                dq_col = dq_col + jnp.sum(ds * sm, axis=1, keepdims=True) - jnp.sum(e_t, axis=1, keepdims=True)
                dq_all = dq_all + dq_col * onehot
                dqtot_all = dqtot_all + dqtot * onehot
                dxbc_ref[:, P * h:P * (h + 1)] = dx * dt_h
                ddt_x = ddt_x + jnp.sum(dx * xs_h, axis=1, keepdims=True) * onehot
                dhs[P * h:P * (h + 1), :] = dhin
            dcg = dcg + _dot(dgm, bg)
            dbg = dbg + _dot_tn(dgm, cg)
            dxbc_ref[:, SSD_IN + NST * g:SSD_IN + NST * (g + 1)] = dbg
            dxbc_ref[:, SSD_IN + 2 * NST + NST * g:SSD_IN + 2 * NST + NST * (g + 1)] = dcg
        dadt = _dot01(inc_t, dq_all) + dqtot_all
        ddt_ref[...] = ddt_x + dadt * a
        da_ref[d] += jnp.where(_iota((8, HP), 0) == 0, jnp.sum(dadt * dtv, axis=0, keepdims=True), 0.0)

    def body(dyf_ref, dyb_ref, xf_ref, xb_ref, dtf_ref, dtb_ref, alog_ref, hf_ref, hb_ref,
             dxf_ref, dxb_ref, ddtf_ref, ddtb_ref, da_ref, dhs):
        @pl.when(pl.program_id(1) == 0)
        def _():
            dhs[...] = jnp.zeros(dhs.shape, F32)
            da_ref[...] = jnp.zeros(da_ref.shape, F32)

        chain(0, dyf_ref, xf_ref, dtf_ref, alog_ref, hf_ref, dxf_ref, ddtf_ref, da_ref, dhs)
        chain(1, dyb_ref, xb_ref, dtb_ref, alog_ref, hb_ref, dxb_ref, ddtb_ref, da_ref, dhs)

    cidx = lambda d: (lambda s: _chunk_index(d, nc - 1 - s, nc, nctc))
    specs = lambda d: dict(
        dy=pl.BlockSpec((None, CH, SSD_IN), lambda b, s: (b, cidx(d)(s), 0)),
        xbc=pl.BlockSpec((None, CH, XBC), lambda b, s: (b, cidx(d)(s), 0)),
        dt=pl.BlockSpec((None, None, CH, HP), lambda b, s: (b, d, cidx(d)(s), 0)),
        h=pl.BlockSpec((None, None, SSD_IN, NST), lambda b, s: (b, cidx(d)(s), 0, 0)),
        ddt=pl.BlockSpec((None, CH, HP), lambda b, s: (b, cidx(d)(s), 0)))
    f, r = specs(0), specs(1)
    bb = 2 * (2 * _nbytes((CH, XBC), F32) + 2 * _nbytes((CH, HP), F32) + _nbytes((CH, SSD_IN), F32) + _nbytes((SSD_IN, NST), F32))
    return pl.pallas_call(
        functools.partial(body), name="ssd_scan_bwd", grid=(B, nc),
        in_specs=[f["dy"], r["dy"], f["xbc"], r["xbc"], f["dt"], r["dt"], pl.BlockSpec((2, 8, HP), lambda b, s: (0, 0, 0)),
                  f["h"], r["h"]],
        out_specs=[f["xbc"], r["xbc"], f["ddt"], r["ddt"], pl.BlockSpec((None, 2, 8, HP), lambda b, s: (b, 0, 0, 0))],
        out_shape=[_out((B, T, XBC), F32)] * 2 + [_out((B, T, HP), F32)] * 2 + [_out((B, 2, 8, HP), F32)],
        scratch_shapes=[pltpu.VMEM((2, SSD_IN, NST), F32)],
        compiler_params=_params(("arbitrary",) * 2, bb, 8 << 20),
    )(*_hbm([dyt, dyt, xbc, xbc, dt2, dt2, alog2, hin_f, hin_b]))


def _ssd_prep_bwd(R, dxbc_f, dxbc_b, dxsk, ddt_f, ddt_b, xbc_raw, dt_raw, conv_w8, conv_b, dtb):
    B, T = R.B, R.T

    def body(dxf_ref, dxb_ref, dsk_ref, ddtf_ref, ddtb2_ref, raw_ref, pv_ref, nx_ref, dtr_ref, w_ref, b_ref, dtb_ref,
             dpre_ref, ddtr_ref, dcw_ref, dvec_ref, ddtb_ref):
        b, i = pl.program_id(0), pl.program_id(1)
        ext = R.ext(i, pv_ref, raw_ref[...], nx_ref)
        pre = _conv_pre(ext, w_ref, b_ref)
        dxbc = dxf_ref[...] + dxb_ref[...]
        skip = jnp.concatenate([dsk_ref[...], jnp.zeros((TM, XBC - SSD_IN), F32)], axis=1)
        dpre = (dxbc + skip) * _dsilu(pre)
        dpre_ref[...] = dpre
        first = jnp.logical_and(b == 0, i == 0)
        taps = {k: jnp.sum(dpre * _shift_rows(ext, k - 1)[8:8 + TM], axis=0, keepdims=True) for k in range(4)}
        _acc_rows(dcw_ref, first, taps)
        _acc_rows(dvec_ref, first, {0: jnp.sum(dpre, axis=0, keepdims=True)})
        ddt = ddtf_ref[...] + pltpu.roll(ddtb2_ref[...], H, axis=1)
        ddtr = ddt * _sigmoid(dtr_ref[...] + dtb_ref[...])
        ddtr = jnp.where(_iota((TM, HP), 1) < 2 * H, ddtr, 0.0)
        ddtr_ref[...] = ddtr.astype(BF)
        _acc_rows(ddtb_ref, first, {0: jnp.sum(ddtr, axis=0, keepdims=True)})

    return R.call("ssd_prep_bwd", body,
                  [(dxbc_f, R.row(XBC)), (dxbc_b, R.row(XBC)), (dxsk, R.row(SSD_IN)), (ddt_f, R.row(HP)), (ddt_b, R.row(HP)),
                   (xbc_raw, R.row(XBC)), (xbc_raw, R.prev8(XBC)),
                   (xbc_raw, R.next8(XBC)), (dt_raw, R.row(HP)), (conv_w8, R.const((8, XBC))), (conv_b, R.const((1, XBC))),
                   (dtb, R.const((1, HP)))],
                  [(_sds((B, T, XBC), F32), R.row(XBC)), (_sds((B, T, HP), BF), R.row(HP)), (_sds((8, XBC), F32), R.const((8, XBC))),
                   (_sds((8, XBC), F32), R.const((8, XBC))), (_sds((8, HP), F32), R.const((8, HP)))], extra=16 << 20)


def _in_bwd(R, dxm, x, mod, nw1, dz, dpre, dqa, dkva, dpool, dkr, ddtr, conv_w8, w_arr):
    B, T = R.B, R.T

    def body(dxm_ref, x_ref, mod_ref, nw_ref, dz_ref, dp_ref, dpp_ref, dpn_ref, dqa_ref, dkva_ref, dpool_ref, dkr_ref, ddt_ref,
             cw_ref, w_ref, dx_ref, dproj_ref, dmod_ref, dnw_ref):
        b, i = pl.program_id(0), pl.program_id(1)
        ext = R.ext(i, dpp_ref, dp_ref[...], dpn_ref)
        draw = (_row(cw_ref, 0) * _shift_rows(ext, 1)[8:8 + TM] + _row(cw_ref, 1) * ext[8:8 + TM]
                + _row(cw_ref, 2) * _shift_rows(ext, -1)[8:8 + TM] + _row(cw_ref, 3) * _shift_rows(ext, -2)[8:8 + TM])
        dproj_ref[:, A_Z:A_XBC] = dz_ref[...]
        dproj_ref[:, A_XBC:A_QA] = draw.astype(BF)
        dproj_ref[:, A_QA:A_KVA] = dqa_ref[...]
        dproj_ref[:, A_KVA:A_POOL] = dkva_ref[...]
        dproj_ref[:, A_POOL:A_KR] = dpool_ref[...]
        dproj_ref[:, A_KR:A_DT] = dkr_ref[...]
        dproj_ref[:, A_DT:PC] = ddt_ref[...]
        dh = _dot_nt(dproj_ref[...], w_ref[...])
        _, xn, r = _norm_mod(x_ref[...], nw_ref[...], _row(mod_ref, 0), _row(mod_ref, 1))
        dx, dsh, dsc, dnw = _norm_mod_bwd(dh, xn, r, nw_ref[...], _row(mod_ref, 1))
        dx_ref[...] = dxm_ref[...] + dx
        _acc_rows(dmod_ref, R.first_of_stream(i), {0: dsh, 1: dsc})
        _acc_rows(dnw_ref, jnp.logical_and(b == 0, i == 0), {0: dnw})

    return R.call("in_bwd", body,
                  [(dxm, R.row(D)), (x, R.row(D)), (mod, R.mod()), (nw1, R.const((1, D))), (dz, R.row(SSD_IN)), (dpre, R.row(XBC)),
                   (dpre, R.prev8(XBC)), (dpre, R.next8(XBC)), (dqa, R.row(QL)), (dkva, R.row(QL)), (dpool, R.row(PD)), (dkr, R.row(HP)),
                   (ddtr, R.row(HP)), (conv_w8, R.const((8, XBC))), (w_arr, R.const((D, PC)))],
                  [(_sds((B, T, D), F32), R.row(D)), (_sds((B, T, PC), BF), R.row(PC)), (_sds((B, 2, 8, D), F32), R.mod()),
                   (_sds((8, D), F32), R.const((8, D)))], extra=12 << 20)


def _adaln_fwd(cs, mod_w):
    L, _, C = mod_w.shape

    def body(c_ref, w_ref, o_ref):
        s = _silu(c_ref[...]).astype(BF)
        for l in range(L):
            o_ref[l] = jnp.dot(s, w_ref[l].astype(BF), preferred_element_type=F32)

    return pl.pallas_call(functools.partial(body), name="adaln_fwd", out_shape=_sds((L, 24, C), F32),
                          compiler_params=_params(None, _nbytes(mod_w.shape, F32) + _nbytes((L, 24, C), F32), 8 << 20))(cs, mod_w)


def _adaln_bwd(cs, dm, mod_w):
    L, _, C = mod_w.shape

    def body(c_ref, dm_ref, w_ref, gw_ref, gc_ref):
        c = c_ref[...]
        s = _silu(c).astype(BF)
        acc = jnp.zeros((24, D), F32)
        for l in range(L):
            dmb = dm_ref[l].astype(BF)
            gw_ref[l] = _dot_tn(s, dmb)
            acc = acc + _dot_nt(dmb, w_ref[l])
        gc_ref[...] = acc * _dsilu(c)

    return pl.pallas_call(functools.partial(body), name="adaln_bwd", out_shape=[_sds((L, D, C), F32), _sds((24, D), F32)],
                          compiler_params=_params(None, 2 * _nbytes(mod_w.shape, F32), 8 << 20))(cs, dm, mod_w)


def _adamw(name, parts, w, m, v, rt):
    Pn, Rr, C = parts.shape
    c1 = 1.0 - ADAM_B1 ** ADAM_STEP
    c2 = 1.0 - ADAM_B2 ** ADAM_STEP

    def body(p_ref, w_ref, m_ref, v_ref, g_ref, d_ref, nm_ref, nv_ref):
        g = p_ref[0].astype(F32)
        for k in range(1, Pn):
            g = g + p_ref[k].astype(F32)
        mn = ADAM_B1 * m_ref[...] + (1.0 - ADAM_B1) * g
        vn = ADAM_B2 * v_ref[...] + (1.0 - ADAM_B2) * jnp.square(g)
        g_ref[...] = g
        nm_ref[...] = mn
        nv_ref[...] = vn
        d_ref[...] = -ADAM_LR * ((mn / c1) / (jnp.sqrt(vn / c2) + ADAM_EPS) + ADAM_WD * w_ref[...])

    spec = pl.BlockSpec((rt, C), lambda i: (i, 0))
    bb = Pn * _nbytes((rt, C), parts.dtype) + 7 * _nbytes((rt, C), F32)
    return pl.pallas_call(
        functools.partial(body), name=name, grid=(Rr // rt,),
        in_specs=[pl.BlockSpec((Pn, rt, C), lambda i: (0, i, 0)), spec, spec, spec],
        out_specs=[spec] * 4, out_shape=[_out((Rr, C), F32)] * 4,
        compiler_params=_params(("arbitrary",), bb, 4 << 20),
    )(*_hbm([parts, w, m, v]))


MESH = pl.DeviceIdType.MESH


def _my_pos():
    return lax.axis_index("x"), lax.axis_index("y"), lax.axis_index("c")


def _dev_index(x, y, c):
    return 4 * x + 2 * y + c


def _all_gather(name, shards):
    n = len(shards)

    def body(*refs):
        ins, outs = refs[:n], refs[n:2 * n]
        send_sems, recv_sems, local_sem = refs[2 * n:]
        x, y, c = _my_pos()
        me, sibling = (x, y, c), (x, y, 1 - c)
        chips = [(1 - x, y), (x, 1 - y), (1 - x, 1 - y)]

        def copy(t, k, block, to, src=None):
            slot = outs[t].at[_dev_index(*block)]
            return pltpu.make_async_remote_copy(
                src_ref=slot if src is None else src, dst_ref=slot,
                send_sem=send_sems.at[t, k], recv_sem=recv_sems.at[t, k], device_id=to, device_id_type=MESH)

        mine = [pltpu.make_async_copy(ins[t], outs[t].at[_dev_index(*me)], local_sem.at[t]) for t in range(n)]
        for cp in mine:
            cp.start()
        first = []
        for t in range(n):
            first.append(copy(t, 0, me, sibling, src=ins[t]))
            first += [copy(t, 1 + j, me, (*chip, c), src=ins[t]) for j, chip in enumerate(chips)]
        for cp in first:
            cp.start()
        passed = []
        for j, chip in enumerate(chips):
            for t in range(n):
                copy(t, 1 + j, (*chip, c), me).wait_recv()
                cp = copy(t, 4 + j, (*chip, c), sibling)
                cp.start()
                passed.append(cp)
        for t in range(n):
            copy(t, 0, sibling, me).wait_recv()
            for j, chip in enumerate(chips):
                copy(t, 4 + j, (*chip, 1 - c), me).wait_recv()
        for cp in first + passed:
            cp.wait_send()
        for cp in mine:
            cp.wait()

    return pl.pallas_call(
        functools.partial(body), name=name,
        in_specs=[ANY] * n, out_specs=[ANY] * n,
        out_shape=[_sds((NDEV,) + s.shape, s.dtype) for s in shards],
        scratch_shapes=[pltpu.SemaphoreType.DMA((n, 7)), pltpu.SemaphoreType.DMA((n, 7)), pltpu.SemaphoreType.DMA((n,))],
    )(*shards)


def _all_to_all(name, parts):
    n = len(parts)

    def body(*refs):
        ins, outs = refs[:n], refs[n:2 * n]
        send_sems, recv_sems, local_sem = refs[2 * n:]
        x, y, c = _my_pos()
        me = _dev_index(x, y, c)
        peers = [(x ^ ((k >> 2) & 1), y ^ ((k >> 1) & 1), c ^ (k & 1)) for k in range(1, NDEV)]
        mine = [pltpu.make_async_copy(ins[t].at[me], outs[t].at[me], local_sem.at[t]) for t in range(n)]
        for cp in mine:
            cp.start()
        sends = []
        for t in range(n):
            for k, peer in enumerate(peers):
                cp = pltpu.make_async_remote_copy(
                    src_ref=ins[t].at[_dev_index(*peer)], dst_ref=outs[t].at[me],
                    send_sem=send_sems.at[t, k], recv_sem=recv_sems.at[t, k], device_id=peer, device_id_type=MESH)
                cp.start()
                sends.append(cp)
        for t in range(n):
            for k, peer in enumerate(peers):
                slot = outs[t].at[_dev_index(*peer)]
                pltpu.make_async_remote_copy(
                    src_ref=slot, dst_ref=slot, send_sem=send_sems.at[t, k], recv_sem=recv_sems.at[t, k],
                    device_id=peer, device_id_type=MESH).wait_recv()
        for cp in sends:
            cp.wait_send()
        for cp in mine:
            cp.wait()

    return pl.pallas_call(
        functools.partial(body), name=name,
        in_specs=[ANY] * n, out_specs=[ANY] * n,
        out_shape=[_sds(p.shape, p.dtype) for p in parts],
        scratch_shapes=[pltpu.SemaphoreType.DMA((n, 7)), pltpu.SemaphoreType.DMA((n, 7)), pltpu.SemaphoreType.DMA((n,))],
    )(*parts)


SEM = pl.BlockSpec(memory_space=pltpu.SEMAPHORE)
IN_HBM = pl.BlockSpec(memory_space=pltpu.HBM)
DATAFLOW = pltpu.SideEffectType.DATAFLOW_SIDE_EFFECTING


def _flip_peers(x, y, c):
    return [(x ^ ((k >> 2) & 1), y ^ ((k >> 1) & 1), c ^ (k & 1)) for k in range(1, NDEV)]


def _split_copies(srcs, lands, send_sems, recv_sems, gather):
    x, y, c = _my_pos()
    me = _dev_index(x, y, c)
    out = []
    for t in range(len(srcs)):
        for k, peer in enumerate(_flip_peers(x, y, c)):
            p = _dev_index(*peer)
            src = srcs[t] if gather else srcs[t].at[p]
            sems = dict(send_sem=send_sems.at[7 * t + k], recv_sem=recv_sems.at[7 * t + k], device_id=peer, device_id_type=MESH)
            out.append((pltpu.make_async_remote_copy(src_ref=src, dst_ref=lands[t].at[me], **sems),
                        pltpu.make_async_remote_copy(src_ref=src, dst_ref=lands[t].at[p], **sems)))
    return out


def _exchange_start(name, collective_id, srcs, gather):
    n = len(srcs)
    lands = [lax.empty(((NDEV,) + s.shape) if gather else s.shape, s.dtype) for s in srcs]

    def body(*refs):
        src_refs, land_refs = refs[:n], refs[n:2 * n]
        send_sems, recv_sems = refs[2 * n], refs[2 * n + 1]
        token = refs[-1]
        barrier = pltpu.get_barrier_semaphore()
        for peer in _flip_peers(*_my_pos()):
            pl.semaphore_signal(barrier, inc=1, device_id=peer, device_id_type=MESH)
        pl.semaphore_wait(barrier, NDEV - 1)
        for send, _ in _split_copies(src_refs, land_refs, send_sems, recv_sems, gather):
            send.start()
        token[...] = jnp.zeros(token.shape, token.dtype)

    hbm = lambda a: pltpu.HBM(a.shape, a.dtype)
    res = pl.pallas_call(
        functools.partial(body), name=name,
        out_shape=[pltpu.SemaphoreType.DMA((7 * n,)), pltpu.SemaphoreType.DMA((7 * n,))] + [hbm(s) for s in srcs]
        + [hbm(a) for a in lands] + [_sds((8, HP), F32)],
        in_specs=[IN_HBM] * (2 * n), out_specs=[SEM, SEM] + [IN_HBM] * (2 * n) + [pl.BlockSpec(memory_space=pltpu.VMEM)],
        input_output_aliases={i: 2 + i for i in range(2 * n)},
        compiler_params=pltpu.CompilerParams(has_side_effects=DATAFLOW, collective_id=collective_id),
    )(*_hbm(list(srcs) + lands))
    return res[0], res[1], list(res[2:2 + n]), list(res[2 + n:2 + 2 * n]), res[-1]


def _exchange_wait(name, send_sems, recv_sems, srcs, lands, after, gather):
    n = len(srcs)

    def body(*refs):
        src_refs, land_refs = refs[:n], refs[n:2 * n]
        for _, recv in _split_copies(src_refs, land_refs, refs[2 * n], refs[2 * n + 1], gather):
            recv.wait_send()
            recv.wait_recv()

    hbm = lambda a: pltpu.HBM(a.shape, a.dtype)
    res = pl.pallas_call(
        functools.partial(body), name=name,
        out_shape=[hbm(s) for s in srcs] + [hbm(a) for a in lands],
        in_specs=[IN_HBM] * (2 * n) + [SEM, SEM, ANY], out_specs=[IN_HBM] * (2 * n),
        input_output_aliases={i: i for i in range(2 * n)},
        compiler_params=pltpu.CompilerParams(has_side_effects=DATAFLOW),
    )(*srcs, *lands, send_sems, recv_sems, after)
    return list(res[:n]), list(res[n:])


def _with_own(lands, own_blocks, me):
    out = []
    for land, own in zip(lands, own_blocks):
        out.append(lax.dynamic_update_slice(land, own[None], (me,) + (0,) * own.ndim))
    return out


def _arrange_w_in(w):
    z = lambda n: jnp.zeros((w.shape[0], n), w.dtype)
    return jnp.concatenate([w[:, 0:1280], w[:, 1292:1548], w[:, 1548:1804], w[:, 1836:2092],
                            z(64), w[:, 1804:1836], z(32), w[:, 1280:1292], z(HP - 2 * H)], axis=1)


def _unarrange_w_in(g):
    return jnp.concatenate([g[:, 0:1280], g[:, A_DT:A_DT + 2 * H], g[:, A_QA:A_KVA], g[:, A_KVA:A_POOL],
                            g[:, A_KR + NOPE:A_KR + NOPE + ROPE], g[:, A_POOL:A_KR]], axis=1)


def _pad_heads(w, width):
    k = w.shape[0]
    return jnp.pad(w.reshape(k, H, width), ((0, 0), (0, 0), (0, HP - width))).reshape(k, H * HP)


def _unpad_heads(g, width):
    k = g.shape[0]
    return g.reshape(k, H, HP)[:, :, :width].reshape(k, H * width)


def _arrange_w_out(w):
    att = jnp.pad(w[SSD_IN:2 * SSD_IN].reshape(H, VH, D), ((0, 0), (0, HP - VH), (0, 0))).reshape(QW, D)
    return jnp.concatenate([w[0:SSD_IN], att, w[2 * SSD_IN:]], axis=0)


def _unarrange_w_out(g):
    att = g[SSD_IN:SSD_IN + QW].reshape(H, HP, D)[:, :VH].reshape(SSD_IN, D)
    return jnp.concatenate([g[0:SSD_IN], att, g[SSD_IN + QW:]], axis=0)


def _rope_tables(T):
    n = T - CTX
    t = jnp.arange(n, dtype=F32)
    row, col = jnp.floor(t / GRID_W), t % GRID_W
    pairs = ROPE // 4
    inv = ROPE_THETA ** (-jnp.arange(pairs, dtype=F32) / pairs)
    ar, ac = row[:, None] * inv, col[:, None] * inv
    cos = jnp.concatenate([jnp.cos(ar)] * 2 + [jnp.cos(ac)] * 2, axis=1)
    sin = jnp.concatenate([-jnp.sin(ar), jnp.sin(ar), -jnp.sin(ac), jnp.sin(ac)], axis=1)
    ones, zeros = jnp.ones((n, NOPE), F32), jnp.zeros((n, NOPE), F32)
    cos = jnp.concatenate([ones, cos, ones[:, :HP - QK]], axis=1)
    sin = jnp.concatenate([zeros, sin, zeros[:, :HP - QK]], axis=1)
    return (jnp.concatenate([jnp.ones((CTX, HP), F32), cos], axis=0),
            jnp.concatenate([jnp.zeros((CTX, HP), F32), sin], axis=0))


def _lane_pad(v, n):
    return jnp.pad(v, (0, n - v.shape[0]))[None, :]


def _layer_weights(w_in, w_q_b, w_kv_b, conv_w, pool_w):
    kv = w_kv_b.reshape(QL, H, NOPE + VH)
    wbd = jnp.zeros((PD, PD), F32)
    for g in range(4):
        wbd = lax.dynamic_update_slice(wbd, pool_w[g], (64 * g, 64 * g))
    return dict(
        w_in=_arrange_w_in(w_in).astype(BF),
        wq=_pad_heads(w_q_b, QK).astype(BF),
        wk=_pad_heads(kv[:, :, :NOPE].reshape(QL, H * NOPE), NOPE).astype(BF),
        wv=_pad_heads(kv[:, :, NOPE:].reshape(QL, H * VH), VH).astype(BF),
        conv_w8=jnp.pad(conv_w, ((0, 4), (0, 0))), wbd=wbd)


def _layer_fwd(R, x, mod, lw, sp, late, cos, sin):
    B, T = R.B, R.T
    z, xbc_raw, qa, kva, pool_in, kr, dt_raw, h1 = _in_proj(R, x, mod, sp["nw1"], lw["w_in"])
    xbc, dt2 = _ssd_prep(R, xbc_raw, dt_raw, lw["conv_w8"], sp["conv_b"], sp["dtb"])
    yf, yb, hin_f, hin_b = _ssd_scan(B, T, xbc, dt2, sp["alog2"])
    qt, kt, q_t, k_t, v_t, cq, ckv = _mla_prep(R, qa, kva, kr, sp["qnw"], sp["kvnw"], lw["wq"], lw["wk"], lw["wv"], cos, sin)
    o, lse = _flash_fwd(B, T, q_t, kt, v_t)
    pool = _pool_fwd(R, pool_in, lw["wbd"], sp["pscale"])[0]
    w_out, w1, w2 = late(o)
    w_out = _arrange_w_out(w_out).astype(BF)
    xmid, cat, mixbf = _out_proj(R, x, mod, yf, yb, xbc, z, o, pool, sp["dsk"], sp["snw"], w_out)
    xo, h2, ubf, ybf = _mlp_fwd(R, xmid, mod, sp["nw2"], w1, w2)
    saved = dict(x=x, z=z, xbc_raw=xbc_raw, qa=qa, kva=kva, pool_in=pool_in, dt_raw=dt_raw, h1=h1, xbc=xbc, dt2=dt2,
                 yf=yf, yb=yb, hin_f=hin_f, hin_b=hin_b, qt=qt, kt=kt, q_t=q_t, k_t=k_t, v_t=v_t, cq=cq, ckv=ckv, o=o, lse=lse,
                 cat=cat, mixbf=mixbf, xmid=xmid, h2=h2, ubf=ubf, ybf=ybf, w_out=w_out, w1=w1, w2=w2)
    return xo, saved


def _layer_bwd(R, dxo, sv, mod, lw, sp, cos, sin, on_mlp=None):
    B, T = R.B, R.T
    dxm, du, abf, dyb, dmod_a, dnw2 = _mlp_bwd(R, dxo, sv["xmid"], sv["ubf"], sv["ybf"], mod, sp["nw2"], sv["w1"], sv["w2"])
    g_w1 = _tn_matmul("dw_mlp1", sv["h2"], du, FSH)
    g_w2 = _tn_matmul("dw_mlp2", abf, dyb, D)[0].reshape(NDEV, FSH, D)
    snw = sp["snw"]
    tok = on_mlp(g_w1, g_w2) if on_mlp is not None else None
    if tok is not None:
        snw = snw + tok
    dmix, dz, dyt, dxsk, dob, do_t, dd, dpo, dmod_b, dvec_o = _out_bwd(R, dxm, mod, sv["mixbf"], sv["yf"], sv["yb"], sv["xbc"], sv["z"],
                                                                 sv["o"], sp["dsk"], snw, sv["w_out"])
    g_wout = _unarrange_w_out(_tn_matmul("dw_out", sv["cat"], dmix, D)[0])
    dpool_in, g_wbd, dpsc = _pool_bwd(R, dpo, sv["pool_in"], lw["wbd"], sp["pscale"])
    dqt, dkt, dvt = _flash_bwd(B, T, sv["qt"], sv["kt"], dob, sv["q_t"], sv["k_t"], sv["v_t"], do_t, sv["lse"], dd)
    dqp, dkb, dvb, dqa, dkva, dkr, dnw_qk = _mla_bwd(R, dqt, dkt, dvt, sv["qa"], sv["kva"], sp["qnw"], sp["kvnw"],
                                                     lw["wq"], lw["wk"], lw["wv"], cos, sin)
    g_wq = _unpad_heads(_tn_matmul("dw_q", sv["cq"], dqp, QW)[0], QK)
    g_wk = _unpad_heads(_tn_matmul("dw_k", sv["ckv"], dkb, QW)[0], NOPE).reshape(QL, H, NOPE)
    g_wv = _unpad_heads(_tn_matmul("dw_v", sv["ckv"], dvb, QW)[0], VH).reshape(QL, H, VH)
    g_wkv = jnp.concatenate([g_wk, g_wv], axis=2).reshape(QL, H * (NOPE + VH))
    dxbc_f, dxbc_b, ddt_f, ddt_b, da = _ssd_scan_bwd(B, T, dyt, sv["xbc"], sv["dt2"], sp["alog2"], sv["hin_f"], sv["hin_b"])
    dpre, ddtr, dcw, dcb, ddtb = _ssd_prep_bwd(R, dxbc_f, dxbc_b, dxsk, ddt_f, ddt_b, sv["xbc_raw"], sv["dt_raw"], lw["conv_w8"],
                                                sp["conv_b"], sp["dtb"])
    dx, dproj, dmod_c, dnw1 = _in_bwd(R, dxm, sv["x"], mod, sp["nw1"], dz, dpre, dqa, dkva, dpool_in, dkr, ddtr,
                                      lw["conv_w8"], lw["w_in"])
    g_win = _unarrange_w_in(jnp.concatenate(list(_tn_matmul("dw_in", sv["h1"], dproj, PC // 3)), axis=1))
    a2 = -jnp.exp(sp["alog2"][:, 0, :H])
    small = dict(
        norm1_w=dnw1[0], norm2_w=dnw2[0], conv_w=dcw[0:4], conv_b=dcb[0], dt_bias=ddtb[0, :2 * H].reshape(2, H),
        a_log=jnp.sum(da[:, :, 0, :H], axis=0) * a2, ssd_d=jnp.sum(dvec_o[1].reshape(H, P), axis=1), ssd_norm_w=dvec_o[0],
        q_a_norm_w=dnw_qk[0], kv_a_norm_w=dnw_qk[1],
        pool_w=jnp.stack([g_wbd[64 * g:64 * (g + 1), 64 * g:64 * (g + 1)] for g in range(4)]), pool_scale=dpsc[0])
    big = dict(w_in=g_win, w_out=g_wout, w_q_b=g_wq, w_kv_b=g_wkv, w_mlp1=g_w1, w_mlp2=g_w2)
    return dx, big, small, dmod_a + dmod_b + dmod_c


def _small_params(l, norm1_w, norm2_w, conv_b, dt_bias, a_log, ssd_d, ssd_norm_w, q_a_norm_w, kv_a_norm_w, pool_scale):
    alog2 = jnp.broadcast_to(jnp.pad(a_log[l], ((0, 0), (0, HP - H)))[:, None, :], (2, 8, HP))
    return dict(nw1=norm1_w[l][None], nw2=norm2_w[l][None], conv_b=conv_b[l][None],
                dtb=_lane_pad(dt_bias[l].reshape(2 * H), HP), alog2=alog2,
                dsk=jnp.repeat(ssd_d[l], P)[None], snw=ssd_norm_w[l][None], qnw=q_a_norm_w[l][None],
                kvnw=kv_a_norm_w[l][None], pscale=pool_scale[l][None])


SMALL_NAMES = ["mod_b", "norm1_w", "norm2_w", "conv_b", "dt_bias", "a_log", "ssd_d", "ssd_norm_w", "q_a_norm_w",
               "kv_a_norm_w", "pool_w", "pool_scale", "final_norm_w"]


def _pack(arrs):
    rows = []
    for a in arrs:
        f = a.reshape(-1).astype(F32)
        n = -(-f.shape[0] // HP) * HP
        rows.append(jnp.pad(f, (0, n - f.shape[0])).reshape(-1, HP))
    out = jnp.concatenate(rows, axis=0)
    pad = (-out.shape[0]) % 8
    return jnp.pad(out, ((0, pad), (0, 0)))


def _unpack(pack, like):
    outs, r = [], 0
    for a in like:
        n = math.prod(a.shape)
        nr = -(-n // HP)
        outs.append(pack[r:r + nr].reshape(-1)[:n].reshape(a.shape))
        r += nr
    return outs


def _local_step(x, ctx, target, mods, full_of, small_w, on_grads=None, on_mlp=None):
    B, N = x.shape[0], x.shape[1]
    T = CTX + N
    R = _Rows(B, T)
    cos, sin = _rope_tables(T)
    xu = jnp.concatenate([ctx, x], axis=1)
    L = len(mods)
    lws, sps, saves = [], [], []
    for l in range(L):
        f = full_of(l, xu)
        lws.append(_layer_weights(f["w_in"], f["w_q_b"], f["w_kv_b"], f["conv_w"], small_w["pool_w"][l]))
        sps.append(_small_params(l, *[small_w[k] for k in ["norm1_w", "norm2_w", "conv_b", "dt_bias", "a_log", "ssd_d",
                                                          "ssd_norm_w", "q_a_norm_w", "kv_a_norm_w", "pool_scale"]]))
        xu, sv = _layer_fwd(R, xu, mods[l], lws[l], sps[l], f["late"], cos, sin)
        saves.append(sv)
    dx, loss8, dfnw = _loss_head(R, xu, target, small_w["final_norm_w"][None])
    bigs, smalls, dmods = [None] * L, [None] * L, [None] * L
    for l in reversed(range(L)):
        hook = functools.partial(on_mlp, l) if on_mlp is not None else None
        dx, bigs[l], smalls[l], dmods[l] = _layer_bwd(R, dx, saves[l], mods[l], lws[l], sps[l], cos, sin, hook)
        if on_grads is not None:
            bigs[l], tok = on_grads(l, bigs[l], dx)
            if tok is not None:
                sps[l - 1] = dict(sps[l - 1], nw2=sps[l - 1]["nw2"] + tok)
    return loss8[0, 0], dx[:, CTX:], bigs, smalls, dfnw[0], dmods


def kernel(x, c, ctx, c_ctx, mod_w, mod_b, norm1_w, norm2_w, w_in, conv_w, conv_b, dt_bias, a_log, ssd_d, ssd_norm_w, q_a_norm_w, w_q_b, kv_a_norm_w, w_kv_b, pool_w, pool_scale, w_out, w_mlp1, w_mlp2, final_norm_w, loss_target, m_c_ctx, m_mod_w, m_mod_b, m_norm1_w, m_norm2_w, m_w_in, m_conv_w, m_conv_b, m_dt_bias, m_a_log, m_ssd_d, m_ssd_norm_w, m_q_a_norm_w, m_w_q_b, m_kv_a_norm_w, m_w_kv_b, m_pool_w, m_pool_scale, m_w_out, m_w_mlp1, m_w_mlp2, m_final_norm_w, v_c_ctx, v_mod_w, v_mod_b, v_norm1_w, v_norm2_w, v_w_in, v_conv_w, v_conv_b, v_dt_bias, v_a_log, v_ssd_d, v_ssd_norm_w, v_q_a_norm_w, v_w_q_b, v_kv_a_norm_w, v_w_kv_b, v_pool_w, v_pool_scale, v_w_out, v_w_mlp1, v_w_mlp2, v_final_norm_w):
    args = dict(locals())
    B = x.shape[0]
    L = mod_w.shape[0]
    me = _dev_index(*_my_pos())
    CS = mod_w.shape[2]

    big_names = ["w_in", "w_out", "w_q_b", "w_kv_b", "w_mlp1", "w_mlp2"]
    shards = {n: args[n].astype(BF) for n in big_names}
    early, late_names = ["w_in", "w_q_b", "w_kv_b"], ["w_out", "w_mlp1", "w_mlp2"]
    g0 = _all_gather("gather_weights", [c, conv_w] + [shards[n][0] for n in early])
    c_all, convw_all = g0[0], g0[1]
    gathered = {0: dict(zip(early, g0[2:]))}
    cs = jnp.concatenate([c_all.reshape(NDEV * B, D), c_ctx[None], jnp.zeros((24 - NDEV * B - 1, D), F32)], axis=0)
    m_loc = _adaln_fwd(cs, mod_w)
    m_all = _all_gather("gather_mod", [m_loc])[0]
    m_full = jnp.moveaxis(m_all, 0, 2).reshape(L, 24, NDEV * CS) + mod_b[:, None, :]
    pending = {}
    tok = jnp.zeros((), F32)
    res = _exchange_start("gather_start_0", 1, [shards[n][0] for n in late_names], gather=True)
    pending[0] = res[:4]
    tok = tok + res[4][0, 0]
    for l in range(1, L):
        res = _exchange_start("gather_start_%d" % l, 1 + l, [shards[n][l] for n in early + late_names], gather=True)
        pending[l] = res[:4]
        tok = tok + res[4][0, 0]
    mods = []
    for l in range(L):
        ex = lax.dynamic_slice(m_full[l], (me * B, 0), (B, 6 * D)).reshape(B, 6, D)
        cc = jnp.broadcast_to(m_full[l, NDEV * B].reshape(1, 6, D), (B, 6, D))
        mods.append(jnp.pad(jnp.stack([cc, ex], axis=1), ((0, 0), (0, 0), (0, 2), (0, 0))) + tok)

    def full_of(l, xu):
        if l > 0:
            own, lands = _exchange_wait("gather_wait_%d" % l, *pending.pop(l), xu, gather=True)
            gathered[l] = dict(zip(early + late_names, _with_own(lands, own, me)))

        def late(after):
            if l == 0:
                own, lands = _exchange_wait("gather_wait_0", *pending.pop(0), after, gather=True)
                gathered[0].update(zip(late_names, _with_own(lands, own, me)))
            gl = gathered[l]
            return gl["w_out"].reshape(D, D), gl["w_mlp1"], gl["w_mlp2"]

        g = gathered[l]
        return dict(
            w_in=g["w_in"].reshape(D, IN_COLS),
            w_q_b=jnp.moveaxis(g["w_q_b"], 0, 1).reshape(QL, H * QK),
            w_kv_b=jnp.moveaxis(g["w_kv_b"], 0, 1).reshape(QL, H * (NOPE + VH)),
            conv_w=jnp.moveaxis(convw_all[:, l], 0, 1).reshape(4, XBC), late=late)

    def grad_blocks(big, names):
        make = dict(
            w_in=lambda g: g.reshape(NDEV, D // NDEV, IN_COLS), w_out=lambda g: g.reshape(NDEV, D // NDEV, D),
            w_q_b=lambda g: jnp.moveaxis(g.reshape(QL, NDEV, -1), 1, 0),
            w_kv_b=lambda g: jnp.moveaxis(g.reshape(QL, NDEV, -1), 1, 0), w_mlp1=lambda g: g, w_mlp2=lambda g: g)
        return [make[n](big[n]).astype(BF) for n in names]

    sent, sent_mlp = {}, {}
    rest_names = ["w_in", "w_out", "w_q_b", "w_kv_b"]

    def on_mlp(l, g_w1, g_w2):
        if l > 0:
            return None
        res = _exchange_start("grads_mlp_start_0", 1 + 2 * L, [g_w1.astype(BF), g_w2.astype(BF)], gather=False)
        sent_mlp[l] = res[:4]
        return res[4][0, 0]

    def on_grads(l, big, dx):
        if l == 0:
            return grad_blocks(big, rest_names), None
        res = _exchange_start("grads_start_%d" % l, 1 + L + l, grad_blocks(big, big_names), gather=False)
        sent[l] = res[:4]
        return None, res[4][0, 0]

    small_w = {k: args[k] for k in SMALL_NAMES if k != "mod_b"}
    loss_part, grad_x, blocks, smalls, dfnw, dmods = _local_step(x, ctx, loss_target, mods, full_of, small_w, on_grads, on_mlp)
    loss = lax.psum(loss_part, ("x", "y", "c"))

    dm_ex = jnp.stack([dmods[l][:, 1, :6].reshape(B, 6 * D) for l in range(L)])
    dm_cc = jnp.stack([jnp.sum(dmods[l][:, 0, :6], axis=0).reshape(6 * D) for l in range(L)])
    small_parts = dict(
        mod_b=jnp.sum(dm_ex, axis=1) + dm_cc,
        **{k: jnp.stack([smalls[l][k] for l in range(L)]) for k in SMALL_NAMES[1:-1]},
        final_norm_w=dfnw, conv_w=jnp.stack([smalls[l]["conv_w"] for l in range(L)]), dm_cc=dm_cc)
    adam_grads = [small_parts[k] for k in SMALL_NAMES]
    extras = [small_parts["conv_w"], dm_cc, dm_ex]
    pack = jnp.concatenate([_pack(adam_grads), _pack(extras)], axis=0)
    pack_all = _all_gather("gather_small_grads", [pack])[0]
    wpack = _pack([args[k] for k in SMALL_NAMES])
    mpack = _pack([args["m_" + k] for k in SMALL_NAMES])
    vpack = _pack([args["v_" + k] for k in SMALL_NAMES])
    n_adam = wpack.shape[0]
    res_small = _adamw("adamw_small", pack_all[:, :n_adam], wpack, mpack, vpack, n_adam)
    small_out = [_unpack(r, [args[k] for k in SMALL_NAMES]) for r in res_small]
    ext_all = pack_all[:, n_adam:]
    ext_sum = ext_all[0]
    for k in range(1, NDEV):
        ext_sum = ext_sum + ext_all[k]
    g_conv_full, dm_cc_tot, _ = _unpack(ext_sum, extras)
    dm_ex_all = jnp.stack([_unpack(ext_all[k], extras)[2] for k in range(NDEV)], axis=1)
    dm_rows = jnp.concatenate([dm_ex_all.reshape(L, NDEV * B, 6 * D), dm_cc_tot[:, None, :],
                               jnp.zeros((L, 24 - NDEV * B - 1, 6 * D), F32)], axis=1)
    dm_loc = lax.dynamic_slice(dm_rows, (0, 0, me * CS), (L, 24, CS))
    g_modw, gc_part = _adaln_bwd(cs, dm_loc, mod_w)
    gc_all = _all_gather("gather_cctx_grad", [gc_part[NDEV * B:NDEV * B + 8]])[0]
    cpad = lambda a: jnp.pad(a[None], ((0, 7), (0, 0)))
    res_cc = _adamw("adamw_cctx", gc_all, cpad(c_ctx), cpad(m_c_ctx), cpad(v_c_ctx), 8)
    cc_out = [r[0] for r in res_cc]

    def waited(name, handles, names):
        srcs, lands = _exchange_wait(name, *handles, grad_x, gather=False)
        own = [lax.dynamic_index_in_dim(s, me, 0, keepdims=False) for s in srcs]
        return dict(zip(names, _with_own(lands, own, me)))

    recv = {0: dict(zip(rest_names, _all_to_all("exchange_grads", blocks[0])))}
    recv[0].update(waited("grads_mlp_wait_0", sent_mlp.pop(0), ["w_mlp1", "w_mlp2"]))
    for l in range(1, L):
        recv[l] = waited("grads_wait_%d" % l, sent.pop(l), big_names)
    recvs = [jnp.stack([recv[l][n] for l in range(L)], axis=1) for n in big_names]
    big_out = {}
    for name, rv in zip(big_names, recvs):
        w = args[name]
        Rr, C = math.prod(w.shape[:-1]), w.shape[-1]
        rt = Rr if Rr * C <= (1 << 18) else Rr // 8
        res = _adamw("adamw_" + name, rv.reshape(NDEV, Rr, C), w.reshape(Rr, C), args["m_" + name].reshape(Rr, C),
                     args["v_" + name].reshape(Rr, C), rt)
        big_out[name] = [r.reshape(w.shape) for r in res]
    res = _adamw("adamw_mod_w", g_modw.reshape(1, L * D, CS), mod_w.reshape(L * D, CS), m_mod_w.reshape(L * D, CS),
                 v_mod_w.reshape(L * D, CS), L * D // 8)
    big_out["mod_w"] = [r.reshape(mod_w.shape) for r in res]
    CW = conv_w.shape[2]
    g_conv = lax.dynamic_slice(g_conv_full, (0, 0, me * CW), (L, 4, CW))
    res = _adamw("adamw_conv_w", g_conv.reshape(1, L * 4, CW), conv_w.reshape(L * 4, CW), m_conv_w.reshape(L * 4, CW),
                 v_conv_w.reshape(L * 4, CW), L * 4)
    big_out["conv_w"] = [r.reshape(conv_w.shape) for r in res]

    weights = ["c_ctx", "mod_w", "mod_b", "norm1_w", "norm2_w", "w_in", "conv_w", "conv_b", "dt_bias", "a_log", "ssd_d",
               "ssd_norm_w", "q_a_norm_w", "w_q_b", "kv_a_norm_w", "w_kv_b", "pool_w", "pool_scale", "w_out", "w_mlp1",
               "w_mlp2", "final_norm_w"]
    outs = [loss, grad_x]
    for kind in range(4):
        for name in weights:
            if name == "c_ctx":
                outs.append(cc_out[kind])
            elif name in big_out:
                outs.append(big_out[name][kind])
            else:
                outs.append(small_out[kind][SMALL_NAMES.index(name)])
    return tuple(outs)
```

```python
import functools
import math

import jax
import jax.numpy as jnp
from jax import lax
from jax.experimental import pallas as pl
from jax.experimental.pallas import tpu as pltpu

F32 = jnp.float32
BF = jnp.bfloat16
MXU = BF

D = 1024
CTX = 256
GRID_W = 64
EPS = 1e-6
H = 6
P = 64
SSD_IN = 384
NST = 128
XBC = 896
CH = 128
QL = 256
NOPE = 64
ROPE = 32
VH = 64
QK = 96
HP = 128
QW = H * HP
PD = 256
FF = 4096
IN_COLS = 2092
ROPE_THETA = 10000.0
PC = 2304
A_Z, A_XBC, A_QA, A_KVA, A_POOL, A_KR, A_DT = 0, 384, 1280, 1536, 1792, 2048, 2176
MIXW = SSD_IN + QW + PD
NDEV = 8
FSH = FF // NDEV
TM = 256
TQ = 256
TK = 256
VMEM_CAP = 64 * 1024 * 1024
ADAM_LR, ADAM_B1, ADAM_B2, ADAM_EPS, ADAM_WD, ADAM_STEP = 0.001, 0.9, 0.999, 1e-08, 0.01, 10


def _nbytes(shape, dtype):
    n = 1
    for s in shape:
        if s is not None:
            n *= s
    return n * jnp.dtype(dtype).itemsize


def _params(sem, block_bytes, extra=0):
    lim = min(2 * block_bytes + extra + (8 << 20), VMEM_CAP - (6 << 20))
    return pltpu.CompilerParams(dimension_semantics=sem, vmem_limit_bytes=int(lim))


def _hbm(arrays):
    return [pltpu.with_memory_space_constraint(a, pltpu.HBM) for a in arrays]


def _dot(a, b):
    return jnp.dot(a.astype(MXU), b.astype(MXU), preferred_element_type=F32)


def _dot_nt(a, b):
    return lax.dot_general(a.astype(MXU), b.astype(MXU), (((1,), (1,)), ((), ())), preferred_element_type=F32)


def _dot_tn(a, b):
    return lax.dot_general(a.astype(MXU), b.astype(MXU), (((0,), (0,)), ((), ())), preferred_element_type=F32)


def _dot01(m01, x):
    b16 = jnp.bfloat16
    m = m01.astype(b16)
    hi = x.astype(b16)
    r1 = x - hi.astype(F32)
    mid = r1.astype(b16)
    lo = (r1 - mid.astype(F32)).astype(b16)
    f = lambda v: jnp.dot(m, v, preferred_element_type=F32)
    return f(hi) + f(mid) + f(lo)


def _sigmoid(x):
    return 1.0 / (1.0 + jnp.exp(-x))


def _silu(x):
    return x * _sigmoid(x)


def _dsilu(x):
    s = _sigmoid(x)
    return s * (1.0 + x * (1.0 - s))


def _iota(shape, dim):
    return lax.broadcasted_iota(jnp.int32, shape, dim)


def _row(ref, k):
    blk = ref[...]
    return jnp.sum(jnp.where(_iota(blk.shape, 0) == k, blk, 0.0), axis=0, keepdims=True)


def _shift_rows(x, k):
    n = x.shape[0]
    return pltpu.roll(x, (-k) % n, axis=0)


class _Rows:
    def __init__(self, B, T):
        self.B, self.T = B, T
        self.nt = T // TM
        self.nct = CTX // TM

    def row(self, F):
        return pl.BlockSpec((None, TM, F), lambda b, i: (b, i, 0))

    def row2(self, F):
        return pl.BlockSpec((None, 2, TM, F), lambda b, i: (b, 0, i, 0))

    def prev8(self, F):
        return pl.BlockSpec((None, 8, F), lambda b, i: (b, jnp.maximum(i * (TM // 8) - 1, 0), 0))

    def next8(self, F):
        last = self.T // 8 - 1
        return pl.BlockSpec((None, 8, F), lambda b, i: (b, jnp.minimum((i + 1) * (TM // 8), last), 0))

    def mod(self):
        nct = self.nct
        return pl.BlockSpec((None, None, 8, D), lambda b, i: (b, jnp.where(i < nct, 0, 1), 0, 0))

    def const(self, shape):
        z = (0,) * len(shape)
        return pl.BlockSpec(tuple(shape), lambda b, i: z)

    def tgt(self, F):
        nct = self.nct
        return pl.BlockSpec((None, TM, F), lambda b, i: (b, jnp.maximum(i - nct, 0), 0))

    def call(self, name, body, ins, outs, scratch=(), extra=0):
        arrays = [a for a, _ in ins]
        in_specs = [s for _, s in ins]
        out_shape = [pltpu.HBM(o.shape, o.dtype) for o, _ in outs]
        out_specs = [s for _, s in outs]
        bb = 0
        for a, s in list(ins) + list(outs):
            if s.block_shape is not None:
                bb += _nbytes(s.block_shape, a.dtype)
        return pl.pallas_call(
            functools.partial(body), name=name, grid=(self.B, self.nt),
            in_specs=in_specs, out_specs=out_specs, out_shape=out_shape, scratch_shapes=list(scratch),
            compiler_params=_params(("arbitrary", "arbitrary"), bb, extra),
        )(*_hbm(arrays))

    def first_of_stream(self, i):
        return jnp.logical_or(i == 0, i == self.nct)

    def last_of_stream(self, i):
        return jnp.logical_or(i == self.nct - 1, i == self.nt - 1)

    def ext(self, i, prev_ref, cur, next_ref):
        pv = prev_ref[...].astype(F32) * jnp.where(self.first_of_stream(i), 0.0, 1.0)
        nx = next_ref[...].astype(F32) * jnp.where(self.last_of_stream(i), 0.0, 1.0)
        return jnp.concatenate([pv, cur, nx], axis=0)

    def stream_pos(self, i, rows):
        start = jnp.where(i < self.nct, 0, CTX)
        n = jnp.where(i < self.nct, CTX, self.T - CTX)
        t = i * TM - 8 - start + _iota((rows, 1), 0)
        return t, n


def _sds(shape, dtype):
    return jax.ShapeDtypeStruct(tuple(shape), dtype)


def _out(shape, dtype):
    return pltpu.HBM(tuple(shape), dtype)


def _norm_mod(x, nw, sh, sc):
    r = lax.rsqrt(jnp.mean(x * x, axis=-1, keepdims=True) + EPS)
    xn = x * r
    return xn * nw * (1.0 + sc) + sh, xn, r


def _norm_mod_bwd(dh, xn, r, nw, sc):
    dsh = jnp.sum(dh, axis=0, keepdims=True)
    dsc = jnp.sum(dh * (xn * nw), axis=0, keepdims=True)
    dnw = jnp.sum(dh * (1.0 + sc) * xn, axis=0, keepdims=True)
    dxn = dh * nw * (1.0 + sc)
    dx = r * (dxn - xn * jnp.mean(dxn * xn, axis=-1, keepdims=True))
    return dx, dsh, dsc, dnw


def _acc_rows(ref, first, rows):
    rid = _iota(ref.shape, 0)
    upd = jnp.zeros(ref.shape, F32)
    for k, v in rows.items():
        upd = upd + jnp.where(rid == k, v, 0.0)

    @pl.when(first)
    def _():
        ref[...] = upd

    @pl.when(jnp.logical_not(first))
    def _():
        ref[...] += upd


def _in_proj(R, x, mod, nw1, w_arr):
    B, T = R.B, R.T

    def body(x_ref, mod_ref, nw_ref, w_ref, z_ref, xbc_ref, qa_ref, kva_ref, pool_ref, kr_ref, dt_ref, h_ref):
        h, _, _ = _norm_mod(x_ref[...], nw_ref[...], _row(mod_ref, 0), _row(mod_ref, 1))
        hb = h.astype(BF)
        h_ref[...] = hb
        p = jnp.dot(hb, w_ref[...], preferred_element_type=F32)
        z_ref[...] = p[:, A_Z:A_XBC]
        xbc_ref[...] = p[:, A_XBC:A_QA]
        qa_ref[...] = p[:, A_QA:A_KVA]
        kva_ref[...] = p[:, A_KVA:A_POOL]
        pool_ref[...] = p[:, A_POOL:A_KR]
        kr_ref[...] = p[:, A_KR:A_DT]
        dt_ref[...] = p[:, A_DT:PC]

    widths = [SSD_IN, XBC, QL, QL, PD, HP, HP]
    outs = [(_sds((B, T, w), F32), R.row(w)) for w in widths] + [(_sds((B, T, D), BF), R.row(D))]
    return R.call("in_proj", body,
                  [(x, R.row(D)), (mod, R.mod()), (nw1, R.const((1, D))), (w_arr, R.const((D, PC)))],
                  outs, extra=8 << 20)


def _conv_pre(ext, w_ref, b_ref):
    return (_row(w_ref, 0) * _shift_rows(ext, -1)[8:8 + TM] + _row(w_ref, 1) * ext[8:8 + TM]
            + _row(w_ref, 2) * _shift_rows(ext, 1)[8:8 + TM] + _row(w_ref, 3) * _shift_rows(ext, 2)[8:8 + TM]
            + b_ref[...])


def _softplus(x):
    return jnp.maximum(x, 0.0) + jnp.log(1.0 + jnp.exp(-jnp.abs(x)))


def _ssd_prep(R, xbc_raw, dt_raw, conv_w8, conv_b, dtb):
    B, T = R.B, R.T

    def body(raw_ref, pv_ref, nx_ref, dtr_ref, w_ref, b_ref, dtb_ref, xbc_ref, dt_ref):
        i = pl.program_id(1)
        ext = R.ext(i, pv_ref, raw_ref[...], nx_ref)
        xbc_ref[...] = _silu(_conv_pre(ext, w_ref, b_ref))
        lane = _iota((TM, HP), 1)
        dtv = _softplus(dtr_ref[...] + dtb_ref[...])
        keep = lane < H
        dt_ref[0] = jnp.where(keep, dtv, 0.0)
        dt_ref[1] = jnp.where(keep, pltpu.roll(dtv, HP - H, axis=1), 0.0)

    return R.call("ssd_prep", body,
                  [(xbc_raw, R.row(XBC)), (xbc_raw, R.prev8(XBC)), (xbc_raw, R.next8(XBC)), (dt_raw, R.row(HP)),
                   (conv_w8, R.const((8, XBC))), (conv_b, R.const((1, XBC))), (dtb, R.const((1, HP)))],
                  [(_sds((B, T, XBC), F32), R.row(XBC)), (_sds((B, 2, T, HP), F32), R.row2(HP))], extra=12 << 20)


def _chunk_index(d, s, nc, nctc):
    if d == 0:
        return s
    return jnp.where(s < nctc, nctc - 1 - s, nc - 1 - (s - nctc))


def _dot01_r(x, m01, passes=3):
    b16 = jnp.bfloat16
    m = m01.astype(b16)
    out, rest = None, x
    for _ in range(passes):
        part = rest.astype(b16)
        rest = rest - part.astype(F32)
        term = jnp.dot(part, m, preferred_element_type=F32)
        out = term if out is None else out + term
    return out


def _scan_common(d, dtv, alog_ref):
    sel = _iota((HP, H * HP), 0) == (_iota((HP, H * HP), 1) >> 7)
    a_all = _dot01_r(-jnp.exp(alog_ref[d]), sel)
    dt_all = _dot01_r(dtv, sel)
    adt_all = dt_all * jnp.concatenate([a_all] * (CH // 8), axis=0)
    row = _iota((CH, CH), 0)
    col = _iota((CH, CH), 1)
    inc = col <= row if d == 0 else col >= row
    inc_t = row <= col if d == 0 else row >= col
    q_all = _dot01(inc, adt_all)
    return a_all, dt_all, adt_all, q_all, inc, inc_t


def _head_decay(h, adt_all, q_all, inc, inc_t):
    q = q_all[:, HP * h:HP * (h + 1)]
    q_t = q.T
    qtot = jnp.sum(adt_all[:, HP * h:HP * (h + 1)], axis=0, keepdims=True)
    lm = jnp.where(inc, jnp.exp(q - q_t), 0.0)
    lm_t = jnp.where(inc_t, jnp.exp(q_t - q), 0.0)
    return lm, lm_t, jnp.exp(q), jnp.exp(qtot), jnp.exp(qtot - q)


def _ssd_scan(B, T, xbc, dt2, alog2):
    nc, nctc = T // CH, CTX // CH

    def chain(d, xbc_ref, dt_ref, alog_ref, y_ref, hin_ref, hs):
        xbc_v = xbc_ref[...]
        dtv = dt_ref[...]
        blk = alog_ref[d]
        adt = dtv * -jnp.exp(jnp.sum(jnp.where(_iota(blk.shape, 0) == 0, blk, 0.0), axis=0, keepdims=True))
        row, col = _iota((CH, CH), 0), _iota((CH, CH), 1)
        inc = col <= row if d == 0 else col >= row
        inc_tf = (row <= col if d == 0 else row >= col).astype(F32)
        q = _dot01(inc, adt)
        hin_ref[...] = hs[d]
        for g in range(2):
            bg = xbc_v[:, SSD_IN + NST * g:SSD_IN + NST * (g + 1)]
            cg = xbc_v[:, SSD_IN + 2 * NST + NST * g:SSD_IN + 2 * NST + NST * (g + 1)]
            bg_t = bg.T
            gm = _dot(cg, bg_t)
            for r in range(3):
                h = 3 * g + r
                onehot = (_iota((1, HP), 1) == h).astype(F32)
                adt_h = jnp.sum(adt * onehot, axis=1, keepdims=True)
                qc = jnp.sum(q * onehot, axis=1, keepdims=True)
                dt_h = jnp.sum(dtv * onehot, axis=1, keepdims=True)
                qr = jnp.sum(adt_h * inc_tf, axis=0, keepdims=True)
                qtot = jnp.sum(adt_h, axis=0, keepdims=True)
                lm = jnp.where(inc, jnp.exp(qc - qr), 0.0)
                xh = xbc_v[:, P * h:P * (h + 1)] * dt_h
                hprev = hs[d, :, P * h:P * (h + 1)]
                y_ref[:, P * h:P * (h + 1)] = _dot(gm * lm, xh) + jnp.exp(qc) * _dot(cg, hprev)
                hs[d, :, P * h:P * (h + 1)] = jnp.exp(qtot) * hprev + _dot(bg_t, xh * jnp.exp(qtot - qc))

    def body(xf_ref, xb_ref, dtf_ref, dtb_ref, alog_ref, yf_ref, yb_ref, hf_ref, hb_ref, hs):
        @pl.when(pl.program_id(1) == 0)
        def _():
            hs[...] = jnp.zeros(hs.shape, F32)

        chain(0, xf_ref, dtf_ref, alog_ref, yf_ref, hf_ref, hs)
        chain(1, xb_ref, dtb_ref, alog_ref, yb_ref, hb_ref, hs)

    cidx = lambda d: (lambda s: _chunk_index(d, s, nc, nctc))
    specs = lambda d: dict(
        xbc=pl.BlockSpec((None, CH, XBC), lambda b, s: (b, cidx(d)(s), 0)),
        dt=pl.BlockSpec((None, None, CH, HP), lambda b, s: (b, d, cidx(d)(s), 0)),
        y=pl.BlockSpec((None, CH, SSD_IN), lambda b, s: (b, cidx(d)(s), 0)),
        h=pl.BlockSpec((None, None, NST, SSD_IN), lambda b, s: (b, cidx(d)(s), 0, 0)))
    f, r = specs(0), specs(1)
    bb = 2 * (_nbytes((CH, XBC), F32) + _nbytes((CH, HP), F32) + _nbytes((CH, SSD_IN), F32) + _nbytes((SSD_IN, NST), F32))
    return pl.pallas_call(
        functools.partial(body), name="ssd_scan", grid=(B, nc),
        in_specs=[f["xbc"], r["xbc"], f["dt"], r["dt"], pl.BlockSpec((2, 8, HP), lambda b, s: (0, 0, 0))],
        out_specs=[f["y"], r["y"], f["h"], r["h"]],
        out_shape=[_out((B, T, SSD_IN), F32)] * 2 + [_out((B, nc, NST, SSD_IN), F32)] * 2,
        scratch_shapes=[pltpu.VMEM((2, NST, SSD_IN), F32)],
        compiler_params=_params(("arbitrary",) * 2, bb, 12 << 20),
    )(*_hbm([xbc, xbc, dt2, dt2, alog2]))


def _swap8(u):
    lane = _iota(u.shape, 1)
    n = u.shape[1]
    return jnp.where((lane & 15) < 8, pltpu.roll(u, n - 8, axis=1), pltpu.roll(u, 8, axis=1))


def _rope(u, cos, sin_signed):
    return u * cos + _swap8(u) * sin_signed


def _rms(x, w):
    r = lax.rsqrt(jnp.mean(x * x, axis=-1, keepdims=True) + EPS)
    xh = x * r
    return xh * w, xh, r


def _rms_bwd(dy, xh, r, w):
    dw = jnp.sum(dy * xh, axis=0, keepdims=True)
    dxh = dy * w
    return r * (dxh - xh * jnp.mean(dxh * xh, axis=-1, keepdims=True)), dw


def _tile6(t):
    return jnp.concatenate([t] * H, axis=1)


def _per_head(fn, u):
    return jnp.concatenate([fn(u[:, HP * h:HP * (h + 1)]) for h in range(H)], axis=1)


def _mla_prep(R, qa, kva, kr, qnw, kvnw, wq, wk, wv, cos, sin):
    B, T = R.B, R.T
    scale = QK ** -0.5

    def body(qa_ref, kva_ref, kr_ref, qnw_ref, kvnw_ref, wq_ref, wk_ref, wv_ref, cos_ref, sin_ref,
             qt_ref, kt_ref, qtr_ref, ktr_ref, vtr_ref, cq_ref, ckv_ref):
        cq, _, _ = _rms(qa_ref[...], qnw_ref[...])
        ckv, _, _ = _rms(kva_ref[...], kvnw_ref[...])
        cqb, ckvb = cq.astype(BF), ckv.astype(BF)
        cq_ref[...] = cqb
        ckv_ref[...] = ckvb
        cos1, sin1 = cos_ref[...], sin_ref[...]
        q = _per_head(lambda u: _rope(u, cos1, sin1), jnp.dot(cqb, wq_ref[...], preferred_element_type=F32)) * scale
        qt_ref[...] = q.astype(BF)
        kk = _rope(kr_ref[...], cos1, sin1)
        k = jnp.dot(ckvb, wk_ref[...], preferred_element_type=F32) + _tile6(kk)
        kt_ref[...] = k.astype(BF)
        v = jnp.dot(ckvb, wv_ref[...], preferred_element_type=F32)
        v = jnp.where((_iota((TM, QW), 1) & (HP - 1)) == VH, 1.0, v)
        for h in range(H):
            cols = slice(HP * h, HP * (h + 1))
            qtr_ref[h] = q[:, cols].T.astype(BF)
            ktr_ref[h] = k[:, cols].T.astype(BF)
            vtr_ref[h] = v[:, cols].T.astype(BF)

    tr = (_sds((B, H, HP, T), BF), pl.BlockSpec((None, H, HP, TM), lambda b, i: (b, 0, 0, i)))
    return R.call("mla_prep", body,
                  [(qa, R.row(QL)), (kva, R.row(QL)), (kr, R.row(HP)), (qnw, R.const((1, QL))), (kvnw, R.const((1, QL))),
                   (wq, R.const((QL, QW))), (wk, R.const((QL, QW))), (wv, R.const((QL, QW))),
                   (cos, pl.BlockSpec((TM, HP), lambda b, i: (i, 0))), (sin, pl.BlockSpec((TM, HP), lambda b, i: (i, 0)))],
                  [(_sds((B, T, QW), BF), R.row(QW))] * 2 + [tr] * 3 + [(_sds((B, T, QL), BF), R.row(QL))] * 2, extra=12 << 20)


def _flash_fwd(B, T, q_t, kt, v_t):
    nq, nk = T // TQ, T // TK

    def body(q_ref, k_ref, v_ref, o_ref, lse_ref, s_scr):
        i = pl.program_id(2)
        q_tr = q_ref[...]

        def attend(nch):
            mrun = None
            for j in range(nch):
                s = _dot(k_ref[TK * j:TK * (j + 1), :], q_tr)
                s_scr[j] = s
                mrun = s if mrun is None else jnp.maximum(mrun, s)
            m = jnp.max(mrun, axis=0, keepdims=True)
            acc_t = jnp.zeros((HP, TQ), F32)
            for j in range(nch):
                acc_t = acc_t + _dot(v_ref[:, TK * j:TK * (j + 1)], jnp.exp(s_scr[j] - m))
            row = _iota((HP, TQ), 0)
            l = jnp.sum(jnp.where(row == VH, acc_t, 0.0), axis=0, keepdims=True)
            o_ref[...] = jnp.where(row < VH, acc_t / l, 0.0).T
            lse_ref[...] = jnp.broadcast_to(m + jnp.log(l), (HP, TQ)).T

        @pl.when(i < CTX // TQ)
        def _():
            attend(CTX // TK)

        @pl.when(i >= CTX // TQ)
        def _():
            attend(nk)

    bb = _nbytes((TQ, HP), BF) + 2 * _nbytes((T, HP), BF) + _nbytes((TQ, HP), F32) + _nbytes((8, TQ), F32)
    return pl.pallas_call(
        functools.partial(body), name="flash_fwd", grid=(B, H, nq),
        in_specs=[pl.BlockSpec((None, None, HP, TQ), lambda b, h, i: (b, h, 0, i)),
                  pl.BlockSpec((None, T, HP), lambda b, h, i: (b, 0, h)),
                  pl.BlockSpec((None, None, HP, T), lambda b, h, i: (b, h, 0, 0))],
        out_specs=[pl.BlockSpec((None, TQ, HP), lambda b, h, i: (b, i, h)),
                   pl.BlockSpec((None, None, TQ, HP), lambda b, h, i: (b, h, i, 0))],
        out_shape=[_out((B, T, QW), F32), _out((B, H, T, HP), F32)],
        scratch_shapes=[pltpu.VMEM((nk, TK, TQ), F32)],
        compiler_params=_params(("arbitrary",) * 3, bb, _nbytes((nk, TK, TQ), F32) + (8 << 20)),
    )(*_hbm([q_t, kt, v_t]))


def _pool_terms(R, i, rows):
    t, n = R.stream_pos(i, rows)
    lane = _iota((1, PD), 1)
    half = jnp.where(lane < 64, 1, jnp.where(lane < 128, 2, jnp.where(lane < 192, 4, 8)))
    cnt = (jnp.minimum(t + half, n) - jnp.maximum(t - half, 0)).astype(F32)
    valid = jnp.logical_and(t >= 0, t < n)
    return jnp.where(valid, cnt, 1.0), valid.astype(F32), lane


def _lane_select(lane, a2, a4, a8, a16):
    return jnp.where(lane < 64, a2, jnp.where(lane < 128, a4, jnp.where(lane < 192, a8, a16)))


def _pool_centred(R, i, ext):
    cnt, valid, lane = _pool_terms(R, i, ext.shape[0])
    s2 = ext + _shift_rows(ext, -1)
    s4 = _shift_rows(s2, -1) + _shift_rows(s2, 1)
    s8 = _shift_rows(s4, -2) + _shift_rows(s4, 2)
    s16 = _shift_rows(s8, -4) + _shift_rows(s8, 4)
    return _lane_select(lane, s2, s4, s8, s16) / cnt - ext, cnt, valid, lane


def _pool_fwd(R, u, wbd, scale):
    B, T = R.B, R.T

    def body(u_ref, pv_ref, nx_ref, w_ref, sc_ref, o_ref):
        i = pl.program_id(1)
        ext = R.ext(i, pv_ref, u_ref[...], nx_ref)
        dm, _, _, _ = _pool_centred(R, i, ext)
        o_ref[...] = _dot(dm[8:8 + TM], w_ref[...]) * sc_ref[...]

    return R.call("pool_fwd", body,
                  [(u, R.row(PD)), (u, R.prev8(PD)), (u, R.next8(PD)), (wbd, R.const((PD, PD))), (scale, R.const((1, PD)))],
                  [(_sds((B, T, PD), F32), R.row(PD))], extra=8 << 20)


def _group_mask():
    return _iota((1, SSD_IN), 1) < SSD_IN // 2


def _ssd_gate(yf_ref, yb_ref, xbc_ref, z_ref, dsk_ref):
    ytot = yf_ref[...] + yb_ref[...] + xbc_ref[:, 0:SSD_IN] * dsk_ref[...]
    z = z_ref[...]
    gz = ytot * _silu(z)
    g0 = _group_mask()
    sq = gz * gz
    s0 = jnp.sum(jnp.where(g0, sq, 0.0), axis=1, keepdims=True)
    s1 = jnp.sum(jnp.where(g0, 0.0, sq), axis=1, keepdims=True)
    half = SSD_IN // 2
    r = jnp.where(g0, lax.rsqrt(s0 / half + EPS), lax.rsqrt(s1 / half + EPS))
    return ytot, z, gz, r


def _out_proj(R, x, mod, yf, yb, xbc, z, o, pool, dsk, snw, wout):
    B, T = R.B, R.T

    def body(x_ref, mod_ref, yf_ref, yb_ref, xbc_ref, z_ref, o_ref, pool_ref, dsk_ref, snw_ref, w_ref, xmid_ref, cat_ref,
             mix_ref):
        _, _, gz, r = _ssd_gate(yf_ref, yb_ref, xbc_ref, z_ref, dsk_ref)
        cat_ref[:, 0:SSD_IN] = (gz * r * snw_ref[...]).astype(BF)
        cat_ref[:, SSD_IN:SSD_IN + QW] = o_ref[...].astype(BF)
        cat_ref[:, SSD_IN + QW:MIXW] = pool_ref[...].astype(BF)
        mix = jnp.dot(cat_ref[...], w_ref[...], preferred_element_type=F32)
        mix_ref[...] = mix.astype(BF)
        xmid_ref[...] = x_ref[...] + _row(mod_ref, 2) * mix

    return R.call("out_proj", body,
                  [(x, R.row(D)), (mod, R.mod()), (yf, R.row(SSD_IN)), (yb, R.row(SSD_IN)), (xbc, R.row(XBC)), (z, R.row(SSD_IN)),
                   (o, R.row(QW)),
                   (pool, R.row(PD)), (dsk, R.const((1, SSD_IN))), (snw, R.const((1, SSD_IN))), (wout, R.const((MIXW, D)))],
                  [(_sds((B, T, D), F32), R.row(D)), (_sds((B, T, MIXW), BF), R.row(MIXW)), (_sds((B, T, D), BF), R.row(D))],
                  extra=8 << 20)


def _load_once(first, pairs, sem):
    @pl.when(first)
    def _():
        cps = [pltpu.make_async_copy(src, dst, sem.at[k]) for k, (src, dst) in enumerate(pairs)]
        for cp in cps:
            cp.start()
        for cp in cps:
            cp.wait()


ANY = pl.BlockSpec(memory_space=pl.ANY)


def _mlp_fwd(R, xmid, mod, nw2, w1, w2):
    B, T = R.B, R.T

    def body(x_ref, mod_ref, nw_ref, w1_hbm, w2_hbm, xo_ref, h_ref, u_ref, y_ref, w1_v, w2_v, sem):
        first = jnp.logical_and(pl.program_id(0) == 0, pl.program_id(1) == 0)
        _load_once(first, [(w1_hbm, w1_v), (w2_hbm, w2_v)], sem)
        x = x_ref[...]
        h, _, _ = _norm_mod(x, nw_ref[...], _row(mod_ref, 3), _row(mod_ref, 4))
        hb = h.astype(BF)
        h_ref[...] = hb
        y = jnp.zeros((TM, D), F32)
        for j in range(NDEV):
            u = jnp.dot(hb, w1_v[j], preferred_element_type=F32)
            u_ref[:, FSH * j:FSH * (j + 1)] = u.astype(BF)
            a = jnp.square(jnp.maximum(u, 0.0))
            y = y + jnp.dot(a.astype(BF), w2_v[j], preferred_element_type=F32)
        y_ref[...] = y.astype(BF)
        xo_ref[...] = x + _row(mod_ref, 5) * y

    return R.call("mlp_fwd", body,
                  [(xmid, R.row(D)), (mod, R.mod()), (nw2, R.const((1, D))), (w1, ANY), (w2, ANY)],
                  [(_sds((B, T, D), F32), R.row(D)), (_sds((B, T, D), BF), R.row(D)), (_sds((B, T, FF), BF), R.row(FF)),
                   (_sds((B, T, D), BF), R.row(D))],
                  scratch=[pltpu.VMEM((NDEV, D, FSH), w1.dtype), pltpu.VMEM((NDEV, FSH, D), w2.dtype), pltpu.SemaphoreType.DMA((2,))],
                  extra=(2 * _nbytes((NDEV, D, FSH), BF)) + (8 << 20))


def _loss_head(R, x, tgt, fnw):
    B, T = R.B, R.T

    def body(x_ref, t_ref, w_ref, dx_ref, loss_ref, dw_ref):
        b, i = pl.program_id(0), pl.program_id(1)
        live = jnp.where(i >= R.nct, 1.0, 0.0)
        y, xh, r = _rms(x_ref[...], w_ref[...])
        err = (y - t_ref[...]) * live
        dy = err / D
        dxn, dw = _rms_bwd(dy, xh, r, w_ref[...])
        dx_ref[...] = dxn
        first = jnp.logical_and(b == 0, i == 0)
        part = 0.5 * jnp.sum(jnp.sum(err * err, axis=1, keepdims=True), axis=0, keepdims=True) / D
        _acc_rows(loss_ref, first, {0: jnp.broadcast_to(part, (1, HP))})
        _acc_rows(dw_ref, first, {0: dw})

    return R.call("loss_head", body,
                  [(x, R.row(D)), (tgt, R.tgt(D)), (fnw, R.const((1, D)))],
                  [(_sds((B, T, D), F32), R.row(D)), (_sds((8, HP), F32), R.const((8, HP))), (_sds((8, D), F32), R.const((8, D)))],
                  extra=8 << 20)


def _tn_matmul(name, a, b, tn):
    B, T, K = a.shape
    N = b.shape[2]
    nk = 1
    while T % nk or (T // nk) > 1088 or (T // nk) % 16:
        nk += 1
    tk = T // nk
    kt = K if K <= 1536 else 1024
    assert K % kt == 0 and N % tn == 0

    def body(a_ref, b_ref, o_ref):
        first = jnp.logical_and(pl.program_id(2) == 0, pl.program_id(3) == 0)

        @pl.when(first)
        def _():
            o_ref[...] = jnp.zeros(o_ref.shape, F32)

        o_ref[...] += _dot_tn(a_ref[...], b_ref[...])

    bb = _nbytes((tk, kt), a.dtype) + _nbytes((tk, tn), b.dtype) + _nbytes((kt, tn), F32)
    return pl.pallas_call(
        functools.partial(body), name=name, grid=(N // tn, K // kt, B, nk),
        in_specs=[pl.BlockSpec((None, tk, kt), lambda j, kk, bi, t: (bi, t, kk)),
                  pl.BlockSpec((None, tk, tn), lambda j, kk, bi, t: (bi, t, j))],
        out_specs=pl.BlockSpec((None, kt, tn), lambda j, kk, bi, t: (j, kk, 0)),
        out_shape=_out((N // tn, K, tn), F32),
        compiler_params=_params(("arbitrary",) * 4, bb, 8 << 20),
    )(*_hbm([a, b]))


def _mlp_bwd(R, dxo, xmid, ubf, ybf, mod, nw2, w1, w2):
    B, T = R.B, R.T

    def body(dxo_ref, x_ref, u_ref, y_ref, mod_ref, nw_ref, w1_hbm, w2_hbm,
             dxm_ref, du_ref, a_ref, dy_ref, dmod_ref, dnw_ref, w1_v, w2_v, sem):
        b, i = pl.program_id(0), pl.program_id(1)
        _load_once(jnp.logical_and(b == 0, i == 0), [(w1_hbm, w1_v), (w2_hbm, w2_v)], sem)
        dxo = dxo_ref[...]
        _, xn, r = _norm_mod(x_ref[...], nw_ref[...], _row(mod_ref, 3), _row(mod_ref, 4))
        dyb = (dxo * _row(mod_ref, 5)).astype(BF)
        dy_ref[...] = dyb
        dg2 = jnp.sum(dxo * y_ref[...].astype(F32), axis=0, keepdims=True)
        dh = jnp.zeros((TM, D), F32)
        for j in range(NDEV):
            rl = jnp.maximum(u_ref[:, FSH * j:FSH * (j + 1)].astype(F32), 0.0)
            a_ref[:, FSH * j:FSH * (j + 1)] = (rl * rl).astype(BF)
            du = (_dot_nt(dyb, w2_v[j]) * (2.0 * rl)).astype(BF)
            du_ref[:, FSH * j:FSH * (j + 1)] = du
            dh = dh + _dot_nt(du, w1_v[j])
        dx, dsh, dsc, dnw = _norm_mod_bwd(dh, xn, r, nw_ref[...], _row(mod_ref, 4))
        dxm_ref[...] = dxo + dx
        _acc_rows(dmod_ref, R.first_of_stream(i), {3: dsh, 4: dsc, 5: dg2})
        _acc_rows(dnw_ref, jnp.logical_and(b == 0, i == 0), {0: dnw})

    return R.call("mlp_bwd", body,
                  [(dxo, R.row(D)), (xmid, R.row(D)), (ubf, R.row(FF)), (ybf, R.row(D)), (mod, R.mod()), (nw2, R.const((1, D))),
                   (w1, ANY), (w2, ANY)],
                  [(_sds((B, T, D), F32), R.row(D)), (_sds((B, T, FF), BF), R.row(FF)), (_sds((B, T, FF), BF), R.row(FF)),
                   (_sds((B, T, D), BF), R.row(D)), (_sds((B, 2, 8, D), F32), R.mod()), (_sds((8, D), F32), R.const((8, D)))],
                  scratch=[pltpu.VMEM((NDEV, D, FSH), w1.dtype), pltpu.VMEM((NDEV, FSH, D), w2.dtype), pltpu.SemaphoreType.DMA((2,))],
                  extra=(2 * _nbytes((NDEV, D, FSH), BF)) + (8 << 20))


def _out_bwd(R, dxm, mod, mixbf, yf, yb, xbc, z, o, dsk, snw, wout):
    B, T = R.B, R.T

    def body(dxm_ref, mod_ref, mix_ref, yf_ref, yb_ref, xbc_ref, z_ref, o_ref, dsk_ref, snw_ref, w_ref,
             dmix_ref, dz_ref, dyt_ref, dxsk_ref, do_ref, dotr_ref, dd_ref, dpo_ref, dmod_ref, dvec_ref):
        b, i = pl.program_id(0), pl.program_id(1)
        dxm = dxm_ref[...]
        dmixb = (dxm * _row(mod_ref, 2)).astype(BF)
        dmix_ref[...] = dmixb
        dg1 = jnp.sum(dxm * mix_ref[...].astype(F32), axis=0, keepdims=True)
        dcat = _dot_nt(dmixb, w_ref[...])
        do_v = dcat[:, SSD_IN:SSD_IN + QW]
        do_ref[...] = do_v.astype(BF)
        doo = do_v * o_ref[...]
        for h in range(H):
            dd_ref[h] = jnp.broadcast_to(jnp.sum(doo[:, HP * h:HP * (h + 1)], axis=1, keepdims=True), (TM, HP))
            dotr_ref[h] = do_v[:, HP * h:HP * (h + 1)].T.astype(BF)
        dpo_ref[...] = dcat[:, SSD_IN + QW:MIXW]
        dsn = dcat[:, 0:SSD_IN]
        ytot, zv, gz, r = _ssd_gate(yf_ref, yb_ref, xbc_ref, z_ref, dsk_ref)
        gh = gz * r
        dsnw = jnp.sum(dsn * gh, axis=0, keepdims=True)
        dgh = dsn * snw_ref[...]
        g0 = _group_mask()
        pr = dgh * gh
        half = SSD_IN // 2
        m0 = jnp.sum(jnp.where(g0, pr, 0.0), axis=1, keepdims=True) / half
        m1 = jnp.sum(jnp.where(g0, 0.0, pr), axis=1, keepdims=True) / half
        dgz = r * (dgh - gh * jnp.where(g0, m0, m1))
        dyt = dgz * _silu(zv)
        dz_ref[...] = (dgz * ytot * _dsilu(zv)).astype(BF)
        dyt_ref[...] = dyt
        dxsk_ref[...] = dyt * dsk_ref[...]
        ddsk = jnp.sum(dyt * xbc_ref[:, 0:SSD_IN], axis=0, keepdims=True)
        _acc_rows(dmod_ref, R.first_of_stream(i), {2: dg1})
        _acc_rows(dvec_ref, jnp.logical_and(b == 0, i == 0), {0: dsnw, 1: ddsk})

    return R.call("out_bwd", body,
                  [(dxm, R.row(D)), (mod, R.mod()), (mixbf, R.row(D)), (yf, R.row(SSD_IN)), (yb, R.row(SSD_IN)), (xbc, R.row(XBC)),
                   (z, R.row(SSD_IN)),
                   (o, R.row(QW)), (dsk, R.const((1, SSD_IN))), (snw, R.const((1, SSD_IN))), (wout, R.const((MIXW, D)))],
                  [(_sds((B, T, D), BF), R.row(D)), (_sds((B, T, SSD_IN), BF), R.row(SSD_IN)), (_sds((B, T, SSD_IN), F32), R.row(SSD_IN)),
                   (_sds((B, T, SSD_IN), F32), R.row(SSD_IN)), (_sds((B, T, QW), BF), R.row(QW)),
                   (_sds((B, H, HP, T), BF), pl.BlockSpec((None, H, HP, TM), lambda b, i: (b, 0, 0, i))),
                   (_sds((B, H, T, HP), F32), pl.BlockSpec((None, H, TM, HP), lambda b, i: (b, 0, i, 0))),
                   (_sds((B, T, PD), F32), R.row(PD)),
                   (_sds((B, 2, 8, D), F32), R.mod()), (_sds((8, SSD_IN), F32), R.const((8, SSD_IN)))],
                  extra=8 << 20)


def _pool_bwd(R, dpo, u, wbd, scale):
    B, T = R.B, R.T

    def body(d_ref, dpv_ref, dnx_ref, u_ref, upv_ref, unx_ref, w_ref, sc_ref, du_ref, dw_ref, dsc_ref):
        b, i = pl.program_id(0), pl.program_id(1)
        ext_u = R.ext(i, upv_ref, u_ref[...], unx_ref)
        ext_d = R.ext(i, dpv_ref, d_ref[...], dnx_ref)
        dm, cnt, valid, lane = _pool_centred(R, i, ext_u)
        ddm = _dot_nt(ext_d * sc_ref[...], w_ref[...]) * valid
        e = ddm / cnt
        a2 = e + _shift_rows(e, 1)
        a4 = _shift_rows(a2, -1) + _shift_rows(a2, 1)
        a8 = _shift_rows(a4, -2) + _shift_rows(a4, 2)
        a16 = _shift_rows(a8, -4) + _shift_rows(a8, 4)
        du_ref[...] = (_lane_select(lane, a2, a4, a8, a16) - ddm)[8:8 + TM].astype(BF)
        dmc = dm[8:8 + TM]
        dpo_c = d_ref[...]
        first = jnp.logical_and(b == 0, i == 0)

        @pl.when(first)
        def _():
            dw_ref[...] = jnp.zeros(dw_ref.shape, F32)

        dw_ref[...] += _dot_tn(dmc, dpo_c * sc_ref[...])
        _acc_rows(dsc_ref, first, {0: jnp.sum(dpo_c * _dot(dmc, w_ref[...]), axis=0, keepdims=True)})

    return R.call("pool_bwd", body,
                  [(dpo, R.row(PD)), (dpo, R.prev8(PD)), (dpo, R.next8(PD)), (u, R.row(PD)), (u, R.prev8(PD)), (u, R.next8(PD)),
                   (wbd, R.const((PD, PD))), (scale, R.const((1, PD)))],
                  [(_sds((B, T, PD), BF), R.row(PD)), (_sds((PD, PD), F32), R.const((PD, PD))), (_sds((8, PD), F32), R.const((8, PD)))],
                  extra=8 << 20)


def _flash_bwd(B, T, qt, kt, dob, q_t, k_t, v_t, do_t, lse_b, dd_b):
    nk, nq = T // TK, T // TQ

    def body(k_ref, kt_ref, vt_ref, q_ref, do_ref, qt_ref, dot_ref, lse_ref, dd_ref, dq_ref, dk_ref, dv_ref,
             s_scr, dp_scr, ds_scr):
        j = pl.program_id(2)
        s_scr[...] = _dot(q_ref[...], kt_ref[...])
        dp_scr[...] = _dot(do_ref[...], vt_ref[...])
        ctx_keys = jnp.where(j < CTX // TK, 1.0, 0.0)
        dk_tr = jnp.zeros((HP, TK), F32)
        dv_tr = jnp.zeros((HP, TK), F32)
        for i in range(nq):
            rows = slice(TQ * i, TQ * (i + 1))
            lse2 = jnp.concatenate([lse_ref[rows, :]] * (TK // HP), axis=1)
            dd2 = jnp.concatenate([dd_ref[rows, :]] * (TK // HP), axis=1)
            p = jnp.exp(s_scr[rows, :] - lse2)
            if i < CTX // TQ:
                p = p * ctx_keys
            ds = (p * (dp_scr[rows, :] - dd2)).astype(BF)
            ds_scr[rows, :] = ds
            dv_tr = dv_tr + _dot(dot_ref[:, rows], p)
            dk_tr = dk_tr + _dot(qt_ref[:, rows], ds)
        dq = _dot(ds_scr[...], k_ref[...])

        @pl.when(j == 0)
        def _():
            dq_ref[...] = dq

        @pl.when(j > 0)
        def _():
            dq_ref[...] += dq

        dk_ref[...] = dk_tr.T
        dv_ref[...] = dv_tr.T

    tspec = pl.BlockSpec((None, TK, HP), lambda b, h, j: (b, j, h))
    fspec = pl.BlockSpec((None, T, HP), lambda b, h, j: (b, 0, h))
    ttspec = pl.BlockSpec((None, None, HP, TK), lambda b, h, j: (b, h, 0, j))
    ftspec = pl.BlockSpec((None, None, HP, T), lambda b, h, j: (b, h, 0, 0))
    bspec = pl.BlockSpec((None, None, T, HP), lambda b, h, j: (b, h, 0, 0))
    bb = 4 * _nbytes((T, HP), BF) + 3 * _nbytes((T, HP), F32) + 8 * _nbytes((TK, HP), F32)
    scr = 2 * _nbytes((T, TK), F32) + _nbytes((T, TK), BF)
    return pl.pallas_call(
        functools.partial(body), name="flash_bwd", grid=(B, H, nk),
        in_specs=[tspec, ttspec, ttspec, fspec, fspec, ftspec, ftspec, bspec, bspec],
        out_specs=[fspec, tspec, tspec],
        out_shape=[_out((B, T, QW), F32)] * 3,
        scratch_shapes=[pltpu.VMEM((T, TK), F32), pltpu.VMEM((T, TK), F32), pltpu.VMEM((T, TK), BF)],
        compiler_params=_params(("arbitrary",) * 3, bb, scr + (12 << 20)),
    )(*_hbm([kt, k_t, v_t, qt, dob, q_t, do_t, lse_b, dd_b]))


def _mla_bwd(R, dqt, dkt, dvt, qa, kva, qnw, kvnw, wq, wk, wv, cos, sin):
    B, T = R.B, R.T
    scale = QK ** -0.5

    def body(dq_ref, dk_ref, dv_ref, qa_ref, kva_ref, qnw_ref, kvnw_ref, wq_ref, wk_ref, wv_ref, cos_ref, sin_ref,
             dqp_ref, dkb_ref, dvb_ref, dqa_ref, dkva_ref, dkr_ref, dnw_ref):
        b, i = pl.program_id(0), pl.program_id(1)
        cos1, sin1 = cos_ref[...], sin_ref[...]
        dq = dq_ref[...] * scale
        dqp = _per_head(lambda g: g * cos1 + _swap8(g * sin1), dq).astype(BF)
        dqp_ref[...] = dqp
        dkv = dk_ref[...]
        dkb = dkv.astype(BF)
        dkb_ref[...] = dkb
        dvb = dv_ref[...].astype(BF)
        dvb_ref[...] = dvb
        dkk = dkv[:, 0:HP]
        for h in range(1, H):
            dkk = dkk + dkv[:, HP * h:HP * (h + 1)]
        lane = _iota((TM, HP), 1)
        rope_lane = jnp.logical_and(lane >= NOPE, lane < NOPE + ROPE)
        dkr_ref[...] = jnp.where(rope_lane, dkk * cos1 + _swap8(dkk * sin1), 0.0).astype(BF)
        _, qh, qr = _rms(qa_ref[...], qnw_ref[...])
        _, kh, kr_ = _rms(kva_ref[...], kvnw_ref[...])
        dcq = _dot_nt(dqp, wq_ref[...])
        dckv = _dot_nt(dkb, wk_ref[...]) + _dot_nt(dvb, wv_ref[...])
        dqa, dqw = _rms_bwd(dcq, qh, qr, qnw_ref[...])
        dkva, dkw = _rms_bwd(dckv, kh, kr_, kvnw_ref[...])
        dqa_ref[...] = dqa.astype(BF)
        dkva_ref[...] = dkva.astype(BF)
        _acc_rows(dnw_ref, jnp.logical_and(b == 0, i == 0), {0: dqw, 1: dkw})

    tab = pl.BlockSpec((TM, HP), lambda b, i: (i, 0))
    return R.call("mla_bwd", body,
                  [(dqt, R.row(QW)), (dkt, R.row(QW)), (dvt, R.row(QW)), (qa, R.row(QL)), (kva, R.row(QL)),
                   (qnw, R.const((1, QL))), (kvnw, R.const((1, QL))), (wq, R.const((QL, QW))), (wk, R.const((QL, QW))),
                   (wv, R.const((QL, QW))), (cos, tab), (sin, tab)],
                  [(_sds((B, T, QW), BF), R.row(QW))] * 3 + [(_sds((B, T, QL), BF), R.row(QL))] * 2
                  + [(_sds((B, T, HP), BF), R.row(HP)), (_sds((8, QL), F32), R.const((8, QL)))], extra=8 << 20)


def _ssd_scan_bwd(B, T, dyt, xbc, dt2, alog2, hin_f, hin_b):
    nc, nctc = T // CH, CTX // CH

    def chain(d, dy_ref, xbc_ref, dt_ref, alog_ref, hin_ref, dxbc_ref, ddt_ref, da_ref, dhs_all):
        dhs = dhs_all.at[d]
        xbc_v = xbc_ref[...]
        dyv = dy_ref[...]
        a_all, dt_all, adt_all, q_all, inc, inc_t = _scan_common(d, dt_ref[...], alog_ref)
        ones_p = jnp.ones((P, HP), F32)
        total = lambda m: jnp.sum(jnp.sum(m, axis=0, keepdims=True), axis=1, keepdims=True)
        dq_parts, dqtot_parts, ddtx_parts = [], [], []
        for g in range(2):
            bg = xbc_v[:, SSD_IN + NST * g:SSD_IN + NST * (g + 1)]
            cg = xbc_v[:, SSD_IN + 2 * NST + NST * g:SSD_IN + 2 * NST + NST * (g + 1)]
            bg_t, cg_t = bg.T, cg.T
            gm = _dot(cg, bg_t)
            gm_t = _dot(bg, cg_t)
            dgm = jnp.zeros((CH, CH), F32)
            dgm_t = jnp.zeros((CH, CH), F32)
            dbg = jnp.zeros((CH, NST), F32)
            dcg = jnp.zeros((CH, NST), F32)
            for r in range(3):
                h = 3 * g + r
                lm, lm_t, eq, etot, dte = _head_decay(h, adt_all, q_all, inc, inc_t)
                eq_p, etot_p, dte_p = eq[:, :P], etot[:, :P], dte[:, :P]
                dt_p = dt_all[:, HP * h:HP * h + P]
                xs_h = xbc_v[:, P * h:P * (h + 1)]
                xh = xs_h * dt_p
                sm, sm_t = gm * lm, gm_t * lm_t
                dy_h = dyv[:, P * h:P * (h + 1)]
                hprev = hin_ref[:, P * h:P * (h + 1)]
                dho = dhs[:, P * h:P * (h + 1)]
                ds = _dot_nt(dy_h, xh)
                ds_t = _dot_nt(xh, dy_h)
                dx = _dot(sm_t, dy_h)
                edy = eq_p * dy_h
                yo = _dot(cg, hprev)
                dcg = dcg + _dot_nt(edy, hprev)
                dhin = _dot(cg_t, edy) + etot_p * dho
                zs = _dot(bg, dho)
                dx = dx + dte_p * zs
                wm = dte_p * xh * zs
                dbg = dbg + _dot_nt(xh * dte_p, dho)
                dgm = dgm + ds * lm
                dgm_t = dgm_t + ds_t * lm_t
                rs = jnp.sum(ds * sm - ds_t * sm_t, axis=1, keepdims=True) + jnp.sum(edy * yo - wm, axis=1, keepdims=True)
                dq_parts.append(jnp.broadcast_to(rs, (CH, HP)))
                dqtot_parts.append(total(hprev * dho) * etot + total(wm))
                dxbc_ref[:, P * h:P * (h + 1)] = dx * dt_p
                ddtx_parts.append(_dot01_r(dx * xs_h, ones_p, passes=2))
                dhs[:, P * h:P * (h + 1)] = dhin
            dcg = dcg + _dot(dgm, bg)
            dbg = dbg + _dot(dgm_t, cg)
            dxbc_ref[:, SSD_IN + NST * g:SSD_IN + NST * (g + 1)] = dbg
            dxbc_ref[:, SSD_IN + 2 * NST + NST * g:SSD_IN + 2 * NST + NST * (g + 1)] = dcg
        cat = lambda parts: jnp.concatenate(parts, axis=1)
        dadt_all = _dot01(inc_t, cat(dq_parts)) + cat(dqtot_parts)
        ddt_all = cat(ddtx_parts) + dadt_all * jnp.concatenate([a_all] * (CH // 8), axis=0)
        da_all = jnp.sum(dadt_all * dt_all, axis=0, keepdims=True)
        lane, lane1 = _iota((CH, HP), 1), _iota((1, HP), 1)
        ddt = jnp.zeros((CH, HP), F32)
        da = jnp.zeros((1, HP), F32)
        for h in range(H):
            ddt = ddt + jnp.where(lane == h, ddt_all[:, HP * h:HP * (h + 1)], 0.0)
            da = da + jnp.where(lane1 == h, da_all[:, HP * h:HP * (h + 1)], 0.0)
        ddt_ref[...] = ddt
        da_ref[d] += jnp.where(_iota((8, HP), 0) == 0, da, 0.0)

    def body(dyf_ref, dyb_ref, xf_ref, xb_ref, dtf_ref, dtb_ref, alog_ref, hf_ref, hb_ref,
             dxf_ref, dxb_ref, ddtf_ref, ddtb_ref, da_ref, dhs):
        @pl.when(pl.program_id(1) == 0)
        def _():
            dhs[...] = jnp.zeros(dhs.shape, F32)
            da_ref[...] = jnp.zeros(da_ref.shape, F32)

        chain(0, dyf_ref, xf_ref, dtf_ref, alog_ref, hf_ref, dxf_ref, ddtf_ref, da_ref, dhs)
        chain(1, dyb_ref, xb_ref, dtb_ref, alog_ref, hb_ref, dxb_ref, ddtb_ref, da_ref, dhs)

    cidx = lambda d: (lambda s: _chunk_index(d, nc - 1 - s, nc, nctc))
    specs = lambda d: dict(
        dy=pl.BlockSpec((None, CH, SSD_IN), lambda b, s: (b, cidx(d)(s), 0)),
        xbc=pl.BlockSpec((None, CH, XBC), lambda b, s: (b, cidx(d)(s), 0)),
        dt=pl.BlockSpec((None, None, CH, HP), lambda b, s: (b, d, cidx(d)(s), 0)),
        h=pl.BlockSpec((None, None, NST, SSD_IN), lambda b, s: (b, cidx(d)(s), 0, 0)),
        ddt=pl.BlockSpec((None, CH, HP), lambda b, s: (b, cidx(d)(s), 0)))
    f, r = specs(0), specs(1)
    bb = 2 * (2 * _nbytes((CH, XBC), F32) + 2 * _nbytes((CH, HP), F32) + _nbytes((CH, SSD_IN), F32) + _nbytes((SSD_IN, NST), F32))
    return pl.pallas_call(
        functools.partial(body), name="ssd_scan_bwd", grid=(B, nc),
        in_specs=[f["dy"], r["dy"], f["xbc"], r["xbc"], f["dt"], r["dt"], pl.BlockSpec((2, 8, HP), lambda b, s: (0, 0, 0)),
                  f["h"], r["h"]],
        out_specs=[f["xbc"], r["xbc"], f["ddt"], r["ddt"], pl.BlockSpec((None, 2, 8, HP), lambda b, s: (b, 0, 0, 0))],
        out_shape=[_out((B, T, XBC), F32)] * 2 + [_out((B, T, HP), F32)] * 2 + [_out((B, 2, 8, HP), F32)],
        scratch_shapes=[pltpu.VMEM((2, NST, SSD_IN), F32)],
        compiler_params=_params(("arbitrary",) * 2, bb, 16 << 20),
    )(*_hbm([dyt, dyt, xbc, xbc, dt2, dt2, alog2, hin_f, hin_b]))


def _ssd_prep_bwd(R, dxbc_f, dxbc_b, dxsk, ddt_f, ddt_b, xbc_raw, dt_raw, conv_w8, conv_b, dtb):
    B, T = R.B, R.T

    def body(dxf_ref, dxb_ref, dsk_ref, ddtf_ref, ddtb2_ref, raw_ref, pv_ref, nx_ref, dtr_ref, w_ref, b_ref, dtb_ref,
             dpre_ref, ddtr_ref, dcw_ref, dvec_ref, ddtb_ref):
        b, i = pl.program_id(0), pl.program_id(1)
        ext = R.ext(i, pv_ref, raw_ref[...], nx_ref)
        pre = _conv_pre(ext, w_ref, b_ref)
        dxbc = dxf_ref[...] + dxb_ref[...]
        skip = jnp.concatenate([dsk_ref[...], jnp.zeros((TM, XBC - SSD_IN), F32)], axis=1)
        dpre = (dxbc + skip) * _dsilu(pre)
        dpre_ref[...] = dpre
        first = jnp.logical_and(b == 0, i == 0)
        taps = {k: jnp.sum(dpre * _shift_rows(ext, k - 1)[8:8 + TM], axis=0, keepdims=True) for k in range(4)}
        _acc_rows(dcw_ref, first, taps)
        _acc_rows(dvec_ref, first, {0: jnp.sum(dpre, axis=0, keepdims=True)})
        ddt = ddtf_ref[...] + pltpu.roll(ddtb2_ref[...], H, axis=1)
        ddtr = ddt * _sigmoid(dtr_ref[...] + dtb_ref[...])
        ddtr = jnp.where(_iota((TM, HP), 1) < 2 * H, ddtr, 0.0)
        ddtr_ref[...] = ddtr.astype(BF)
        _acc_rows(ddtb_ref, first, {0: jnp.sum(ddtr, axis=0, keepdims=True)})

    return R.call("ssd_prep_bwd", body,
                  [(dxbc_f, R.row(XBC)), (dxbc_b, R.row(XBC)), (dxsk, R.row(SSD_IN)), (ddt_f, R.row(HP)), (ddt_b, R.row(HP)),
                   (xbc_raw, R.row(XBC)), (xbc_raw, R.prev8(XBC)),
                   (xbc_raw, R.next8(XBC)), (dt_raw, R.row(HP)), (conv_w8, R.const((8, XBC))), (conv_b, R.const((1, XBC))),
                   (dtb, R.const((1, HP)))],
                  [(_sds((B, T, XBC), F32), R.row(XBC)), (_sds((B, T, HP), BF), R.row(HP)), (_sds((8, XBC), F32), R.const((8, XBC))),
                   (_sds((8, XBC), F32), R.const((8, XBC))), (_sds((8, HP), F32), R.const((8, HP)))], extra=16 << 20)


def _in_bwd(R, dxm, x, mod, nw1, dz, dpre, dqa, dkva, dpool, dkr, ddtr, conv_w8, w_arr):
    B, T = R.B, R.T

    def body(dxm_ref, x_ref, mod_ref, nw_ref, dz_ref, dp_ref, dpp_ref, dpn_ref, dqa_ref, dkva_ref, dpool_ref, dkr_ref, ddt_ref,
             cw_ref, w_ref, dx_ref, dproj_ref, dmod_ref, dnw_ref):
        b, i = pl.program_id(0), pl.program_id(1)
        ext = R.ext(i, dpp_ref, dp_ref[...], dpn_ref)
        draw = (_row(cw_ref, 0) * _shift_rows(ext, 1)[8:8 + TM] + _row(cw_ref, 1) * ext[8:8 + TM]
                + _row(cw_ref, 2) * _shift_rows(ext, -1)[8:8 + TM] + _row(cw_ref, 3) * _shift_rows(ext, -2)[8:8 + TM])
        dproj_ref[:, A_Z:A_XBC] = dz_ref[...]
        dproj_ref[:, A_XBC:A_QA] = draw.astype(BF)
        dproj_ref[:, A_QA:A_KVA] = dqa_ref[...]
        dproj_ref[:, A_KVA:A_POOL] = dkva_ref[...]
        dproj_ref[:, A_POOL:A_KR] = dpool_ref[...]
        dproj_ref[:, A_KR:A_DT] = dkr_ref[...]
        dproj_ref[:, A_DT:PC] = ddt_ref[...]
        dh = _dot_nt(dproj_ref[...], w_ref[...])
        _, xn, r = _norm_mod(x_ref[...], nw_ref[...], _row(mod_ref, 0), _row(mod_ref, 1))
        dx, dsh, dsc, dnw = _norm_mod_bwd(dh, xn, r, nw_ref[...], _row(mod_ref, 1))
        dx_ref[...] = dxm_ref[...] + dx
        _acc_rows(dmod_ref, R.first_of_stream(i), {0: dsh, 1: dsc})
        _acc_rows(dnw_ref, jnp.logical_and(b == 0, i == 0), {0: dnw})

    return R.call("in_bwd", body,
                  [(dxm, R.row(D)), (x, R.row(D)), (mod, R.mod()), (nw1, R.const((1, D))), (dz, R.row(SSD_IN)), (dpre, R.row(XBC)),
                   (dpre, R.prev8(XBC)), (dpre, R.next8(XBC)), (dqa, R.row(QL)), (dkva, R.row(QL)), (dpool, R.row(PD)), (dkr, R.row(HP)),
                   (ddtr, R.row(HP)), (conv_w8, R.const((8, XBC))), (w_arr, R.const((D, PC)))],
                  [(_sds((B, T, D), F32), R.row(D)), (_sds((B, T, PC), BF), R.row(PC)), (_sds((B, 2, 8, D), F32), R.mod()),
                   (_sds((8, D), F32), R.const((8, D)))], extra=12 << 20)


def _adaln_fwd(cs, mod_w):
    L, _, C = mod_w.shape

    def body(c_ref, w_ref, o_ref):
        s = _silu(c_ref[...]).astype(BF)
        for l in range(L):
            o_ref[l] = jnp.dot(s, w_ref[l].astype(BF), preferred_element_type=F32)

    return pl.pallas_call(functools.partial(body), name="adaln_fwd", out_shape=_sds((L, 24, C), F32),
                          compiler_params=_params(None, _nbytes(mod_w.shape, F32) + _nbytes((L, 24, C), F32), 8 << 20))(cs, mod_w)


def _adaln_bwd(cs, dm, mod_w):
    L, _, C = mod_w.shape

    def body(c_ref, dm_ref, w_ref, gw_ref, gc_ref):
        c = c_ref[...]
        s = _silu(c).astype(BF)
        acc = jnp.zeros((24, D), F32)
        for l in range(L):
            dmb = dm_ref[l].astype(BF)
            gw_ref[l] = _dot_tn(s, dmb)
            acc = acc + _dot_nt(dmb, w_ref[l])
        gc_ref[...] = acc * _dsilu(c)

    return pl.pallas_call(functools.partial(body), name="adaln_bwd", out_shape=[_sds((L, D, C), F32), _sds((24, D), F32)],
                          compiler_params=_params(None, 2 * _nbytes(mod_w.shape, F32), 8 << 20))(cs, dm, mod_w)


def _adamw(name, parts, w, m, v, rt):
    Pn, Rr, C = parts.shape
    c1 = 1.0 - ADAM_B1 ** ADAM_STEP
    c2 = 1.0 - ADAM_B2 ** ADAM_STEP

    def body(p_ref, w_ref, m_ref, v_ref, g_ref, d_ref, nm_ref, nv_ref):
        g = p_ref[0].astype(F32)
        for k in range(1, Pn):
            g = g + p_ref[k].astype(F32)
        mn = ADAM_B1 * m_ref[...] + (1.0 - ADAM_B1) * g
        vn = ADAM_B2 * v_ref[...] + (1.0 - ADAM_B2) * jnp.square(g)
        g_ref[...] = g
        nm_ref[...] = mn
        nv_ref[...] = vn
        d_ref[...] = -ADAM_LR * ((mn / c1) / (jnp.sqrt(vn / c2) + ADAM_EPS) + ADAM_WD * w_ref[...])

    spec = pl.BlockSpec((rt, C), lambda i: (i, 0))
    bb = Pn * _nbytes((rt, C), parts.dtype) + 7 * _nbytes((rt, C), F32)
    return pl.pallas_call(
        functools.partial(body), name=name, grid=(Rr // rt,),
        in_specs=[pl.BlockSpec((Pn, rt, C), lambda i: (0, i, 0)), spec, spec, spec],
        out_specs=[spec] * 4, out_shape=[_out((Rr, C), F32)] * 4,
        compiler_params=_params(("arbitrary",), bb, 4 << 20),
    )(*_hbm([parts, w, m, v]))


MESH = pl.DeviceIdType.MESH


def _my_pos():
    return lax.axis_index("x"), lax.axis_index("y"), lax.axis_index("c")


def _dev_index(x, y, c):
    return 4 * x + 2 * y + c


def _all_gather(name, shards):
    n = len(shards)

    def body(*refs):
        ins, outs = refs[:n], refs[n:2 * n]
        send_sems, recv_sems, local_sem = refs[2 * n:]
        x, y, c = _my_pos()
        me, sibling = (x, y, c), (x, y, 1 - c)
        chips = [(1 - x, y), (x, 1 - y), (1 - x, 1 - y)]

        def copy(t, k, block, to, src=None):
            slot = outs[t].at[_dev_index(*block)]
            return pltpu.make_async_remote_copy(
                src_ref=slot if src is None else src, dst_ref=slot,
                send_sem=send_sems.at[t, k], recv_sem=recv_sems.at[t, k], device_id=to, device_id_type=MESH)

        mine = [pltpu.make_async_copy(ins[t], outs[t].at[_dev_index(*me)], local_sem.at[t]) for t in range(n)]
        for cp in mine:
            cp.start()
        first = []
        for t in range(n):
            first.append(copy(t, 0, me, sibling, src=ins[t]))
            first += [copy(t, 1 + j, me, (*chip, c), src=ins[t]) for j, chip in enumerate(chips)]
        for cp in first:
            cp.start()
        passed = []
        for j, chip in enumerate(chips):
            for t in range(n):
                copy(t, 1 + j, (*chip, c), me).wait_recv()
                cp = copy(t, 4 + j, (*chip, c), sibling)
                cp.start()
                passed.append(cp)
        for t in range(n):
            copy(t, 0, sibling, me).wait_recv()
            for j, chip in enumerate(chips):
                copy(t, 4 + j, (*chip, 1 - c), me).wait_recv()
        for cp in first + passed:
            cp.wait_send()
        for cp in mine:
            cp.wait()

    return pl.pallas_call(
        functools.partial(body), name=name,
        in_specs=[ANY] * n, out_specs=[ANY] * n,
        out_shape=[_sds((NDEV,) + s.shape, s.dtype) for s in shards],
        scratch_shapes=[pltpu.SemaphoreType.DMA((n, 7)), pltpu.SemaphoreType.DMA((n, 7)), pltpu.SemaphoreType.DMA((n,))],
    )(*shards)


def _all_to_all(name, parts):
    n = len(parts)

    def body(*refs):
        ins, outs = refs[:n], refs[n:2 * n]
        send_sems, recv_sems, local_sem = refs[2 * n:]
        x, y, c = _my_pos()
        me = _dev_index(x, y, c)
        peers = [(x ^ ((k >> 2) & 1), y ^ ((k >> 1) & 1), c ^ (k & 1)) for k in range(1, NDEV)]
        mine = [pltpu.make_async_copy(ins[t].at[me], outs[t].at[me], local_sem.at[t]) for t in range(n)]
        for cp in mine:
            cp.start()
        sends = []
        for t in range(n):
            for k, peer in enumerate(peers):
                cp = pltpu.make_async_remote_copy(
                    src_ref=ins[t].at[_dev_index(*peer)], dst_ref=outs[t].at[me],
                    send_sem=send_sems.at[t, k], recv_sem=recv_sems.at[t, k], device_id=peer, device_id_type=MESH)
                cp.start()
                sends.append(cp)
        for t in range(n):
            for k, peer in enumerate(peers):
                slot = outs[t].at[_dev_index(*peer)]
                pltpu.make_async_remote_copy(
                    src_ref=slot, dst_ref=slot, send_sem=send_sems.at[t, k], recv_sem=recv_sems.at[t, k],
                    device_id=peer, device_id_type=MESH).wait_recv()
        for cp in sends:
            cp.wait_send()
        for cp in mine:
            cp.wait()

    return pl.pallas_call(
        functools.partial(body), name=name,
        in_specs=[ANY] * n, out_specs=[ANY] * n,
        out_shape=[_sds(p.shape, p.dtype) for p in parts],
        scratch_shapes=[pltpu.SemaphoreType.DMA((n, 7)), pltpu.SemaphoreType.DMA((n, 7)), pltpu.SemaphoreType.DMA((n,))],
    )(*parts)


SEM = pl.BlockSpec(memory_space=pltpu.SEMAPHORE)
IN_HBM = pl.BlockSpec(memory_space=pltpu.HBM)
DATAFLOW = pltpu.SideEffectType.DATAFLOW_SIDE_EFFECTING


def _flip_peers(x, y, c):
    return [(x ^ ((k >> 2) & 1), y ^ ((k >> 1) & 1), c ^ (k & 1)) for k in range(1, NDEV)]


def _split_copies(srcs, lands, send_sems, recv_sems, gather):
    x, y, c = _my_pos()
    me = _dev_index(x, y, c)
    out = []
    for t in range(len(srcs)):
        for k, peer in enumerate(_flip_peers(x, y, c)):
            p = _dev_index(*peer)
            src = srcs[t] if gather else srcs[t].at[p]
            sems = dict(send_sem=send_sems.at[7 * t + k], recv_sem=recv_sems.at[7 * t + k], device_id=peer, device_id_type=MESH)
            out.append((pltpu.make_async_remote_copy(src_ref=src, dst_ref=lands[t].at[me], **sems),
                        pltpu.make_async_remote_copy(src_ref=src, dst_ref=lands[t].at[p], **sems)))
    return out


def _exchange_start(name, collective_id, srcs, gather):
    n = len(srcs)
    lands = [lax.empty(((NDEV,) + s.shape) if gather else s.shape, s.dtype) for s in srcs]

    def body(*refs):
        src_refs, land_refs = refs[:n], refs[n:2 * n]
        send_sems, recv_sems = refs[2 * n], refs[2 * n + 1]
        token = refs[-1]
        barrier = pltpu.get_barrier_semaphore()
        for peer in _flip_peers(*_my_pos()):
            pl.semaphore_signal(barrier, inc=1, device_id=peer, device_id_type=MESH)
        pl.semaphore_wait(barrier, NDEV - 1)
        for send, _ in _split_copies(src_refs, land_refs, send_sems, recv_sems, gather):
            send.start()
        token[...] = jnp.zeros(token.shape, token.dtype)

    hbm = lambda a: pltpu.HBM(a.shape, a.dtype)
    res = pl.pallas_call(
        functools.partial(body), name=name,
        out_shape=[pltpu.SemaphoreType.DMA((7 * n,)), pltpu.SemaphoreType.DMA((7 * n,))] + [hbm(s) for s in srcs]
        + [hbm(a) for a in lands] + [_sds((8, HP), F32)],
        in_specs=[IN_HBM] * (2 * n), out_specs=[SEM, SEM] + [IN_HBM] * (2 * n) + [pl.BlockSpec(memory_space=pltpu.VMEM)],
        input_output_aliases={i: 2 + i for i in range(2 * n)},
        compiler_params=pltpu.CompilerParams(has_side_effects=DATAFLOW, collective_id=collective_id),
    )(*_hbm(list(srcs) + lands))
    return res[0], res[1], list(res[2:2 + n]), list(res[2 + n:2 + 2 * n]), res[-1]


def _exchange_wait(name, send_sems, recv_sems, srcs, lands, after, gather):
    n = len(srcs)

    def body(*refs):
        src_refs, land_refs = refs[:n], refs[n:2 * n]
        for _, recv in _split_copies(src_refs, land_refs, refs[2 * n], refs[2 * n + 1], gather):
            recv.wait_send()
            recv.wait_recv()

    hbm = lambda a: pltpu.HBM(a.shape, a.dtype)
    res = pl.pallas_call(
        functools.partial(body), name=name,
        out_shape=[hbm(s) for s in srcs] + [hbm(a) for a in lands],
        in_specs=[IN_HBM] * (2 * n) + [SEM, SEM, ANY], out_specs=[IN_HBM] * (2 * n),
        input_output_aliases={i: i for i in range(2 * n)},
        compiler_params=pltpu.CompilerParams(has_side_effects=DATAFLOW),
    )(*srcs, *lands, send_sems, recv_sems, after)
    return list(res[:n]), list(res[n:])


def _with_own(lands, own_blocks, me):
    out = []
    for land, own in zip(lands, own_blocks):
        out.append(lax.dynamic_update_slice(land, own[None], (me,) + (0,) * own.ndim))
    return out


def _arrange_w_in(w):
    z = lambda n: jnp.zeros((w.shape[0], n), w.dtype)
    return jnp.concatenate([w[:, 0:1280], w[:, 1292:1548], w[:, 1548:1804], w[:, 1836:2092],
                            z(64), w[:, 1804:1836], z(32), w[:, 1280:1292], z(HP - 2 * H)], axis=1)


def _unarrange_w_in(g):
    return jnp.concatenate([g[:, 0:1280], g[:, A_DT:A_DT + 2 * H], g[:, A_QA:A_KVA], g[:, A_KVA:A_POOL],
                            g[:, A_KR + NOPE:A_KR + NOPE + ROPE], g[:, A_POOL:A_KR]], axis=1)


def _pad_heads(w, width):
    k = w.shape[0]
    return jnp.pad(w.reshape(k, H, width), ((0, 0), (0, 0), (0, HP - width))).reshape(k, H * HP)


def _unpad_heads(g, width):
    k = g.shape[0]
    return g.reshape(k, H, HP)[:, :, :width].reshape(k, H * width)


def _arrange_w_out(w):
    att = jnp.pad(w[SSD_IN:2 * SSD_IN].reshape(H, VH, D), ((0, 0), (0, HP - VH), (0, 0))).reshape(QW, D)
    return jnp.concatenate([w[0:SSD_IN], att, w[2 * SSD_IN:]], axis=0)


def _unarrange_w_out(g):
    att = g[SSD_IN:SSD_IN + QW].reshape(H, HP, D)[:, :VH].reshape(SSD_IN, D)
    return jnp.concatenate([g[0:SSD_IN], att, g[SSD_IN + QW:]], axis=0)


def _rope_tables(T):
    n = T - CTX
    t = jnp.arange(n, dtype=F32)
    row, col = jnp.floor(t / GRID_W), t % GRID_W
    pairs = ROPE // 4
    inv = ROPE_THETA ** (-jnp.arange(pairs, dtype=F32) / pairs)
    ar, ac = row[:, None] * inv, col[:, None] * inv
    cos = jnp.concatenate([jnp.cos(ar)] * 2 + [jnp.cos(ac)] * 2, axis=1)
    sin = jnp.concatenate([-jnp.sin(ar), jnp.sin(ar), -jnp.sin(ac), jnp.sin(ac)], axis=1)
    ones, zeros = jnp.ones((n, NOPE), F32), jnp.zeros((n, NOPE), F32)
    cos = jnp.concatenate([ones, cos, ones[:, :HP - QK]], axis=1)
    sin = jnp.concatenate([zeros, sin, zeros[:, :HP - QK]], axis=1)
    return (jnp.concatenate([jnp.ones((CTX, HP), F32), cos], axis=0),
            jnp.concatenate([jnp.zeros((CTX, HP), F32), sin], axis=0))


def _lane_pad(v, n):
    return jnp.pad(v, (0, n - v.shape[0]))[None, :]


def _layer_weights(w_in, w_q_b, w_kv_b, conv_w, pool_w):
    kv = w_kv_b.reshape(QL, H, NOPE + VH)
    wbd = jnp.zeros((PD, PD), F32)
    for g in range(4):
        wbd = lax.dynamic_update_slice(wbd, pool_w[g], (64 * g, 64 * g))
    return dict(
        w_in=_arrange_w_in(w_in).astype(BF),
        wq=_pad_heads(w_q_b, QK).astype(BF),
        wk=_pad_heads(kv[:, :, :NOPE].reshape(QL, H * NOPE), NOPE).astype(BF),
        wv=_pad_heads(kv[:, :, NOPE:].reshape(QL, H * VH), VH).astype(BF),
        conv_w8=jnp.pad(conv_w, ((0, 4), (0, 0))), wbd=wbd)


def _layer_fwd(R, x, mod, lw, sp, late, cos, sin):
    B, T = R.B, R.T
    z, xbc_raw, qa, kva, pool_in, kr, dt_raw, h1 = _in_proj(R, x, mod, sp["nw1"], lw["w_in"])
    xbc, dt2 = _ssd_prep(R, xbc_raw, dt_raw, lw["conv_w8"], sp["conv_b"], sp["dtb"])
    yf, yb, hin_f, hin_b = _ssd_scan(B, T, xbc, dt2, sp["alog2"])
    qt, kt, q_t, k_t, v_t, cq, ckv = _mla_prep(R, qa, kva, kr, sp["qnw"], sp["kvnw"], lw["wq"], lw["wk"], lw["wv"], cos, sin)
    o, lse = _flash_fwd(B, T, q_t, kt, v_t)
    pool = _pool_fwd(R, pool_in, lw["wbd"], sp["pscale"])[0]
    w_out, w1, w2 = late(o)
    w_out = _arrange_w_out(w_out).astype(BF)
    xmid, cat, mixbf = _out_proj(R, x, mod, yf, yb, xbc, z, o, pool, sp["dsk"], sp["snw"], w_out)
    xo, h2, ubf, ybf = _mlp_fwd(R, xmid, mod, sp["nw2"], w1, w2)
    saved = dict(x=x, z=z, xbc_raw=xbc_raw, qa=qa, kva=kva, pool_in=pool_in, dt_raw=dt_raw, h1=h1, xbc=xbc, dt2=dt2,
                 yf=yf, yb=yb, hin_f=hin_f, hin_b=hin_b, qt=qt, kt=kt, q_t=q_t, k_t=k_t, v_t=v_t, cq=cq, ckv=ckv, o=o, lse=lse,
                 cat=cat, mixbf=mixbf, xmid=xmid, h2=h2, ubf=ubf, ybf=ybf, w_out=w_out, w1=w1, w2=w2)
    return xo, saved


def _layer_bwd(R, dxo, sv, mod, lw, sp, cos, sin, on_mlp=None):
    B, T = R.B, R.T
    dxm, du, abf, dyb, dmod_a, dnw2 = _mlp_bwd(R, dxo, sv["xmid"], sv["ubf"], sv["ybf"], mod, sp["nw2"], sv["w1"], sv["w2"])
    g_w1 = _tn_matmul("dw_mlp1", sv["h2"], du, FSH)
    g_w2 = _tn_matmul("dw_mlp2", abf, dyb, D)[0].reshape(NDEV, FSH, D)
    snw = sp["snw"]
    tok = on_mlp(g_w1, g_w2) if on_mlp is not None else None
    if tok is not None:
        snw = snw + tok
    dmix, dz, dyt, dxsk, dob, do_t, dd, dpo, dmod_b, dvec_o = _out_bwd(R, dxm, mod, sv["mixbf"], sv["yf"], sv["yb"], sv["xbc"], sv["z"],
                                                                 sv["o"], sp["dsk"], snw, sv["w_out"])
    g_wout = _unarrange_w_out(_tn_matmul("dw_out", sv["cat"], dmix, D)[0])
    dpool_in, g_wbd, dpsc = _pool_bwd(R, dpo, sv["pool_in"], lw["wbd"], sp["pscale"])
    dqt, dkt, dvt = _flash_bwd(B, T, sv["qt"], sv["kt"], dob, sv["q_t"], sv["k_t"], sv["v_t"], do_t, sv["lse"], dd)
    dqp, dkb, dvb, dqa, dkva, dkr, dnw_qk = _mla_bwd(R, dqt, dkt, dvt, sv["qa"], sv["kva"], sp["qnw"], sp["kvnw"],
                                                     lw["wq"], lw["wk"], lw["wv"], cos, sin)
    g_wq = _unpad_heads(_tn_matmul("dw_q", sv["cq"], dqp, QW)[0], QK)
    g_wk = _unpad_heads(_tn_matmul("dw_k", sv["ckv"], dkb, QW)[0], NOPE).reshape(QL, H, NOPE)
    g_wv = _unpad_heads(_tn_matmul("dw_v", sv["ckv"], dvb, QW)[0], VH).reshape(QL, H, VH)
    g_wkv = jnp.concatenate([g_wk, g_wv], axis=2).reshape(QL, H * (NOPE + VH))
    dxbc_f, dxbc_b, ddt_f, ddt_b, da = _ssd_scan_bwd(B, T, dyt, sv["xbc"], sv["dt2"], sp["alog2"], sv["hin_f"], sv["hin_b"])
    dpre, ddtr, dcw, dcb, ddtb = _ssd_prep_bwd(R, dxbc_f, dxbc_b, dxsk, ddt_f, ddt_b, sv["xbc_raw"], sv["dt_raw"], lw["conv_w8"],
                                                sp["conv_b"], sp["dtb"])
    dx, dproj, dmod_c, dnw1 = _in_bwd(R, dxm, sv["x"], mod, sp["nw1"], dz, dpre, dqa, dkva, dpool_in, dkr, ddtr,
                                      lw["conv_w8"], lw["w_in"])
    g_win = _unarrange_w_in(jnp.concatenate(list(_tn_matmul("dw_in", sv["h1"], dproj, PC // 3)), axis=1))
    a2 = -jnp.exp(sp["alog2"][:, 0, :H])
    small = dict(
        norm1_w=dnw1[0], norm2_w=dnw2[0], conv_w=dcw[0:4], conv_b=dcb[0], dt_bias=ddtb[0, :2 * H].reshape(2, H),
        a_log=jnp.sum(da[:, :, 0, :H], axis=0) * a2, ssd_d=jnp.sum(dvec_o[1].reshape(H, P), axis=1), ssd_norm_w=dvec_o[0],
        q_a_norm_w=dnw_qk[0], kv_a_norm_w=dnw_qk[1],
        pool_w=jnp.stack([g_wbd[64 * g:64 * (g + 1), 64 * g:64 * (g + 1)] for g in range(4)]), pool_scale=dpsc[0])
    big = dict(w_in=g_win, w_out=g_wout, w_q_b=g_wq, w_kv_b=g_wkv, w_mlp1=g_w1, w_mlp2=g_w2)
    return dx, big, small, dmod_a + dmod_b + dmod_c


def _small_params(l, norm1_w, norm2_w, conv_b, dt_bias, a_log, ssd_d, ssd_norm_w, q_a_norm_w, kv_a_norm_w, pool_scale):
    alog2 = jnp.broadcast_to(jnp.pad(a_log[l], ((0, 0), (0, HP - H)))[:, None, :], (2, 8, HP))
    return dict(nw1=norm1_w[l][None], nw2=norm2_w[l][None], conv_b=conv_b[l][None],
                dtb=_lane_pad(dt_bias[l].reshape(2 * H), HP), alog2=alog2,
                dsk=jnp.repeat(ssd_d[l], P)[None], snw=ssd_norm_w[l][None], qnw=q_a_norm_w[l][None],
                kvnw=kv_a_norm_w[l][None], pscale=pool_scale[l][None])


SMALL_NAMES = ["mod_b", "norm1_w", "norm2_w", "conv_b", "dt_bias", "a_log", "ssd_d", "ssd_norm_w", "q_a_norm_w",
               "kv_a_norm_w", "pool_w", "pool_scale", "final_norm_w"]


def _pack(arrs):
    rows = []
    for a in arrs:
        f = a.reshape(-1).astype(F32)
        n = -(-f.shape[0] // HP) * HP
        rows.append(jnp.pad(f, (0, n - f.shape[0])).reshape(-1, HP))
    out = jnp.concatenate(rows, axis=0)
    pad = (-out.shape[0]) % 8
    return jnp.pad(out, ((0, pad), (0, 0)))


def _unpack(pack, like):
    outs, r = [], 0
    for a in like:
        n = math.prod(a.shape)
        nr = -(-n // HP)
        outs.append(pack[r:r + nr].reshape(-1)[:n].reshape(a.shape))
        r += nr
    return outs


def _local_step(x, ctx, target, mods, full_of, small_w, on_grads=None, on_mlp=None):
    B, N = x.shape[0], x.shape[1]
    T = CTX + N
    R = _Rows(B, T)
    cos, sin = _rope_tables(T)
    xu = jnp.concatenate([ctx, x], axis=1)
    L = len(mods)
    lws, sps, saves = [], [], []
    for l in range(L):
        f = full_of(l, xu)
        lws.append(_layer_weights(f["w_in"], f["w_q_b"], f["w_kv_b"], f["conv_w"], small_w["pool_w"][l]))
        sps.append(_small_params(l, *[small_w[k] for k in ["norm1_w", "norm2_w", "conv_b", "dt_bias", "a_log", "ssd_d",
                                                          "ssd_norm_w", "q_a_norm_w", "kv_a_norm_w", "pool_scale"]]))
        xu, sv = _layer_fwd(R, xu, mods[l], lws[l], sps[l], f["late"], cos, sin)
        saves.append(sv)
    dx, loss8, dfnw = _loss_head(R, xu, target, small_w["final_norm_w"][None])
    bigs, smalls, dmods = [None] * L, [None] * L, [None] * L
    for l in reversed(range(L)):
        hook = functools.partial(on_mlp, l) if on_mlp is not None else None
        dx, bigs[l], smalls[l], dmods[l] = _layer_bwd(R, dx, saves[l], mods[l], lws[l], sps[l], cos, sin, hook)
        if on_grads is not None:
            bigs[l], tok = on_grads(l, bigs[l], dx)
            if tok is not None:
                sps[l - 1] = dict(sps[l - 1], nw2=sps[l - 1]["nw2"] + tok)
    return loss8[0, 0], dx[:, CTX:], bigs, smalls, dfnw[0], dmods


def kernel(x, c, ctx, c_ctx, mod_w, mod_b, norm1_w, norm2_w, w_in, conv_w, conv_b, dt_bias, a_log, ssd_d, ssd_norm_w, q_a_norm_w, w_q_b, kv_a_norm_w, w_kv_b, pool_w, pool_scale, w_out, w_mlp1, w_mlp2, final_norm_w, loss_target, m_c_ctx, m_mod_w, m_mod_b, m_norm1_w, m_norm2_w, m_w_in, m_conv_w, m_conv_b, m_dt_bias, m_a_log, m_ssd_d, m_ssd_norm_w, m_q_a_norm_w, m_w_q_b, m_kv_a_norm_w, m_w_kv_b, m_pool_w, m_pool_scale, m_w_out, m_w_mlp1, m_w_mlp2, m_final_norm_w, v_c_ctx, v_mod_w, v_mod_b, v_norm1_w, v_norm2_w, v_w_in, v_conv_w, v_conv_b, v_dt_bias, v_a_log, v_ssd_d, v_ssd_norm_w, v_q_a_norm_w, v_w_q_b, v_kv_a_norm_w, v_w_kv_b, v_pool_w, v_pool_scale, v_w_out, v_w_mlp1, v_w_mlp2, v_final_norm_w):
    args = dict(locals())
    B = x.shape[0]
    L = mod_w.shape[0]
    me = _dev_index(*_my_pos())
    CS = mod_w.shape[2]

    big_names = ["w_in", "w_out", "w_q_b", "w_kv_b", "w_mlp1", "w_mlp2"]
    shards = {n: args[n].astype(BF) for n in big_names}
    early, late_names = ["w_in", "w_q_b", "w_kv_b"], ["w_out", "w_mlp1", "w_mlp2"]
    g0 = _all_gather("gather_weights", [c, conv_w] + [shards[n][0] for n in early])
    c_all, convw_all = g0[0], g0[1]
    gathered = {0: dict(zip(early, g0[2:]))}
    cs = jnp.concatenate([c_all.reshape(NDEV * B, D), c_ctx[None], jnp.zeros((24 - NDEV * B - 1, D), F32)], axis=0)
    m_loc = _adaln_fwd(cs, mod_w)
    m_all = _all_gather("gather_mod", [m_loc])[0]
    m_full = jnp.moveaxis(m_all, 0, 2).reshape(L, 24, NDEV * CS) + mod_b[:, None, :]
    pending = {}
    tok = jnp.zeros((), F32)
    res = _exchange_start("gather_start_0", 1, [shards[n][0] for n in late_names], gather=True)
    pending[0] = res[:4]
    tok = tok + res[4][0, 0]
    for l in range(1, L):
        res = _exchange_start("gather_start_%d" % l, 1 + l, [shards[n][l] for n in early + late_names], gather=True)
        pending[l] = res[:4]
        tok = tok + res[4][0, 0]
    mods = []
    for l in range(L):
        ex = lax.dynamic_slice(m_full[l], (me * B, 0), (B, 6 * D)).reshape(B, 6, D)
        cc = jnp.broadcast_to(m_full[l, NDEV * B].reshape(1, 6, D), (B, 6, D))
        mods.append(jnp.pad(jnp.stack([cc, ex], axis=1), ((0, 0), (0, 0), (0, 2), (0, 0))) + tok)

    def full_of(l, xu):
        if l > 0:
            own, lands = _exchange_wait("gather_wait_%d" % l, *pending.pop(l), xu, gather=True)
            gathered[l] = dict(zip(early + late_names, _with_own(lands, own, me)))

        def late(after):
            if l == 0:
                own, lands = _exchange_wait("gather_wait_0", *pending.pop(0), after, gather=True)
                gathered[0].update(zip(late_names, _with_own(lands, own, me)))
            gl = gathered[l]
            return gl["w_out"].reshape(D, D), gl["w_mlp1"], gl["w_mlp2"]

        g = gathered[l]
        return dict(
            w_in=g["w_in"].reshape(D, IN_COLS),
            w_q_b=jnp.moveaxis(g["w_q_b"], 0, 1).reshape(QL, H * QK),
            w_kv_b=jnp.moveaxis(g["w_kv_b"], 0, 1).reshape(QL, H * (NOPE + VH)),
            conv_w=jnp.moveaxis(convw_all[:, l], 0, 1).reshape(4, XBC), late=late)

    def grad_blocks(big, names):
        make = dict(
            w_in=lambda g: g.reshape(NDEV, D // NDEV, IN_COLS), w_out=lambda g: g.reshape(NDEV, D // NDEV, D),
            w_q_b=lambda g: jnp.moveaxis(g.reshape(QL, NDEV, -1), 1, 0),
            w_kv_b=lambda g: jnp.moveaxis(g.reshape(QL, NDEV, -1), 1, 0), w_mlp1=lambda g: g, w_mlp2=lambda g: g)
        return [make[n](big[n]).astype(BF) for n in names]

    sent, sent_mlp = {}, {}
    rest_names = ["w_in", "w_out", "w_q_b", "w_kv_b"]

    def on_mlp(l, g_w1, g_w2):
        if l > 0:
            return None
        res = _exchange_start("grads_mlp_start_0", 1 + 2 * L, [g_w1.astype(BF), g_w2.astype(BF)], gather=False)
        sent_mlp[l] = res[:4]
        return res[4][0, 0]

    def on_grads(l, big, dx):
        if l == 0:
            return grad_blocks(big, rest_names), None
        res = _exchange_start("grads_start_%d" % l, 1 + L + l, grad_blocks(big, big_names), gather=False)
        sent[l] = res[:4]
        return None, res[4][0, 0]

    small_w = {k: args[k] for k in SMALL_NAMES if k != "mod_b"}
    loss_part, grad_x, blocks, smalls, dfnw, dmods = _local_step(x, ctx, loss_target, mods, full_of, small_w, on_grads, on_mlp)
    loss = lax.psum(loss_part, ("x", "y", "c"))

    dm_ex = jnp.stack([dmods[l][:, 1, :6].reshape(B, 6 * D) for l in range(L)])
    dm_cc = jnp.stack([jnp.sum(dmods[l][:, 0, :6], axis=0).reshape(6 * D) for l in range(L)])
    small_parts = dict(
        mod_b=jnp.sum(dm_ex, axis=1) + dm_cc,
        **{k: jnp.stack([smalls[l][k] for l in range(L)]) for k in SMALL_NAMES[1:-1]},
        final_norm_w=dfnw, conv_w=jnp.stack([smalls[l]["conv_w"] for l in range(L)]), dm_cc=dm_cc)
    adam_grads = [small_parts[k] for k in SMALL_NAMES]
    extras = [small_parts["conv_w"], dm_cc, dm_ex]
    pack = jnp.concatenate([_pack(adam_grads), _pack(extras)], axis=0)
    pack_all = _all_gather("gather_small_grads", [pack])[0]
    wpack = _pack([args[k] for k in SMALL_NAMES])
    mpack = _pack([args["m_" + k] for k in SMALL_NAMES])
    vpack = _pack([args["v_" + k] for k in SMALL_NAMES])
    n_adam = wpack.shape[0]
    res_small = _adamw("adamw_small", pack_all[:, :n_adam], wpack, mpack, vpack, n_adam)
    small_out = [_unpack(r, [args[k] for k in SMALL_NAMES]) for r in res_small]
    ext_all = pack_all[:, n_adam:]
    ext_sum = ext_all[0]
    for k in range(1, NDEV):
        ext_sum = ext_sum + ext_all[k]
    g_conv_full, dm_cc_tot, _ = _unpack(ext_sum, extras)
    dm_ex_all = jnp.stack([_unpack(ext_all[k], extras)[2] for k in range(NDEV)], axis=1)
    dm_rows = jnp.concatenate([dm_ex_all.reshape(L, NDEV * B, 6 * D), dm_cc_tot[:, None, :],
                               jnp.zeros((L, 24 - NDEV * B - 1, 6 * D), F32)], axis=1)
    dm_loc = lax.dynamic_slice(dm_rows, (0, 0, me * CS), (L, 24, CS))
    g_modw, gc_part = _adaln_bwd(cs, dm_loc, mod_w)
    gc_all = _all_gather("gather_cctx_grad", [gc_part[NDEV * B:NDEV * B + 8]])[0]
    cpad = lambda a: jnp.pad(a[None], ((0, 7), (0, 0)))
    res_cc = _adamw("adamw_cctx", gc_all, cpad(c_ctx), cpad(m_c_ctx), cpad(v_c_ctx), 8)
    cc_out = [r[0] for r in res_cc]

    def waited(name, handles, names):
        srcs, lands = _exchange_wait(name, *handles, grad_x, gather=False)
        own = [lax.dynamic_index_in_dim(s, me, 0, keepdims=False) for s in srcs]
        return dict(zip(names, _with_own(lands, own, me)))

    recv = {0: dict(zip(rest_names, _all_to_all("exchange_grads", blocks[0])))}
    recv[0].update(waited("grads_mlp_wait_0", sent_mlp.pop(0), ["w_mlp1", "w_mlp2"]))
    for l in range(1, L):
        recv[l] = waited("grads_wait_%d" % l, sent.pop(l), big_names)
    recvs = [jnp.stack([recv[l][n] for l in range(L)], axis=1) for n in big_names]
    big_out = {}
    for name, rv in zip(big_names, recvs):
        w = args[name]
        Rr, C = math.prod(w.shape[:-1]), w.shape[-1]
        rt = Rr if Rr * C <= (1 << 18) else Rr // 8
        res = _adamw("adamw_" + name, rv.reshape(NDEV, Rr, C), w.reshape(Rr, C), args["m_" + name].reshape(Rr, C),
                     args["v_" + name].reshape(Rr, C), rt)
        big_out[name] = [r.reshape(w.shape) for r in res]
    res = _adamw("adamw_mod_w", g_modw.reshape(1, L * D, CS), mod_w.reshape(L * D, CS), m_mod_w.reshape(L * D, CS),
                 v_mod_w.reshape(L * D, CS), L * D // 8)
    big_out["mod_w"] = [r.reshape(mod_w.shape) for r in res]
    CW = conv_w.shape[2]
    g_conv = lax.dynamic_slice(g_conv_full, (0, 0, me * CW), (L, 4, CW))
    res = _adamw("adamw_conv_w", g_conv.reshape(1, L * 4, CW), conv_w.reshape(L * 4, CW), m_conv_w.reshape(L * 4, CW),
                 v_conv_w.reshape(L * 4, CW), L * 4)
    big_out["conv_w"] = [r.reshape(conv_w.shape) for r in res]

    weights = ["c_ctx", "mod_w", "mod_b", "norm1_w", "norm2_w", "w_in", "conv_w", "conv_b", "dt_bias", "a_log", "ssd_d",
               "ssd_norm_w", "q_a_norm_w", "w_q_b", "kv_a_norm_w", "w_kv_b", "pool_w", "pool_scale", "w_out", "w_mlp1",
               "w_mlp2", "final_norm_w"]
    outs = [loss, grad_x]
    for kind in range(4):
        for name in weights:
            if name == "c_ctx":
                outs.append(cc_out[kind])
            elif name in big_out:
                outs.append(big_out[name][kind])
            else:
                outs.append(small_out[kind][SMALL_NAMES.index(name)])
    return tuple(outs)
```

```python
import functools
import math

import jax
import jax.numpy as jnp
from jax import lax
from jax.experimental import pallas as pl
from jax.experimental.pallas import tpu as pltpu

F32 = jnp.float32
BF = jnp.bfloat16
MXU = BF

D = 1024
CTX = 256
GRID_W = 64
EPS = 1e-6
H = 6
P = 64
SSD_IN = 384
NST = 128
XBC = 896
CH = 128
QL = 256
NOPE = 64
ROPE = 32
VH = 64
QK = 96
HP = 128
QW = H * HP
PD = 256
FF = 4096
IN_COLS = 2092
ROPE_THETA = 10000.0
PC = 2304
A_Z, A_XBC, A_QA, A_KVA, A_POOL, A_KR, A_DT = 0, 384, 1280, 1536, 1792, 2048, 2176
MIXW = SSD_IN + QW + PD
NDEV = 8
FSH = FF // NDEV
TM = 256
TQ = 256
TK = 256
VMEM_CAP = 64 * 1024 * 1024
ADAM_LR, ADAM_B1, ADAM_B2, ADAM_EPS, ADAM_WD, ADAM_STEP = 0.001, 0.9, 0.999, 1e-08, 0.01, 10


def _nbytes(shape, dtype):
    n = 1
    for s in shape:
        if s is not None:
            n *= s
    return n * jnp.dtype(dtype).itemsize


def _params(sem, block_bytes, extra=0):
    lim = min(2 * block_bytes + extra + (8 << 20), VMEM_CAP - (6 << 20))
    return pltpu.CompilerParams(dimension_semantics=sem, vmem_limit_bytes=int(lim))


def _hbm(arrays):
    return [pltpu.with_memory_space_constraint(a, pltpu.HBM) for a in arrays]


def _dot(a, b):
    return jnp.dot(a.astype(MXU), b.astype(MXU), preferred_element_type=F32)


def _dot_nt(a, b):
    return lax.dot_general(a.astype(MXU), b.astype(MXU), (((1,), (1,)), ((), ())), preferred_element_type=F32)


def _dot_tn(a, b):
    return lax.dot_general(a.astype(MXU), b.astype(MXU), (((0,), (0,)), ((), ())), preferred_element_type=F32)


def _dot01(m01, x):
    b16 = jnp.bfloat16
    m = m01.astype(b16)
    hi = x.astype(b16)
    r1 = x - hi.astype(F32)
    mid = r1.astype(b16)
    lo = (r1 - mid.astype(F32)).astype(b16)
    f = lambda v: jnp.dot(m, v, preferred_element_type=F32)
    return f(hi) + f(mid) + f(lo)


def _sigmoid(x):
    return 1.0 / (1.0 + jnp.exp(-x))


def _silu(x):
    return x * _sigmoid(x)


def _dsilu(x):
    s = _sigmoid(x)
    return s * (1.0 + x * (1.0 - s))


def _iota(shape, dim):
    return lax.broadcasted_iota(jnp.int32, shape, dim)


def _row(ref, k):
    blk = ref[...]
    return jnp.sum(jnp.where(_iota(blk.shape, 0) == k, blk, 0.0), axis=0, keepdims=True)


def _shift_rows(x, k):
    n = x.shape[0]
    return pltpu.roll(x, (-k) % n, axis=0)


class _Rows:
    def __init__(self, B, T):
        self.B, self.T = B, T
        self.nt = T // TM
        self.nct = CTX // TM

    def row(self, F):
        return pl.BlockSpec((None, TM, F), lambda b, i: (b, i, 0))

    def row2(self, F):
        return pl.BlockSpec((None, 2, TM, F), lambda b, i: (b, 0, i, 0))

    def prev8(self, F):
        return pl.BlockSpec((None, 8, F), lambda b, i: (b, jnp.maximum(i * (TM // 8) - 1, 0), 0))

    def next8(self, F):
        last = self.T // 8 - 1
        return pl.BlockSpec((None, 8, F), lambda b, i: (b, jnp.minimum((i + 1) * (TM // 8), last), 0))

    def mod(self):
        nct = self.nct
        return pl.BlockSpec((None, None, 8, D), lambda b, i: (b, jnp.where(i < nct, 0, 1), 0, 0))

    def const(self, shape):
        z = (0,) * len(shape)
        return pl.BlockSpec(tuple(shape), lambda b, i: z)

    def tgt(self, F):
        nct = self.nct
        return pl.BlockSpec((None, TM, F), lambda b, i: (b, jnp.maximum(i - nct, 0), 0))

    def call(self, name, body, ins, outs, scratch=(), extra=0):
        arrays = [a for a, _ in ins]
        in_specs = [s for _, s in ins]
        out_shape = [pltpu.HBM(o.shape, o.dtype) for o, _ in outs]
        out_specs = [s for _, s in outs]
        bb = 0
        for a, s in list(ins) + list(outs):
            if s.block_shape is not None:
                bb += _nbytes(s.block_shape, a.dtype)
        return pl.pallas_call(
            functools.partial(body), name=name, grid=(self.B, self.nt),
            in_specs=in_specs, out_specs=out_specs, out_shape=out_shape, scratch_shapes=list(scratch),
            compiler_params=_params(("arbitrary", "arbitrary"), bb, extra),
        )(*_hbm(arrays))

    def first_of_stream(self, i):
        return jnp.logical_or(i == 0, i == self.nct)

    def last_of_stream(self, i):
        return jnp.logical_or(i == self.nct - 1, i == self.nt - 1)

    def ext(self, i, prev_ref, cur, next_ref):
        pv = prev_ref[...].astype(F32) * jnp.where(self.first_of_stream(i), 0.0, 1.0)
        nx = next_ref[...].astype(F32) * jnp.where(self.last_of_stream(i), 0.0, 1.0)
        return jnp.concatenate([pv, cur, nx], axis=0)

    def stream_pos(self, i, rows):
        start = jnp.where(i < self.nct, 0, CTX)
        n = jnp.where(i < self.nct, CTX, self.T - CTX)
        t = i * TM - 8 - start + _iota((rows, 1), 0)
        return t, n


def _sds(shape, dtype):
    return jax.ShapeDtypeStruct(tuple(shape), dtype)


def _out(shape, dtype):
    return pltpu.HBM(tuple(shape), dtype)


def _norm_mod(x, nw, sh, sc):
    r = lax.rsqrt(jnp.mean(x * x, axis=-1, keepdims=True) + EPS)
    xn = x * r
    return xn * nw * (1.0 + sc) + sh, xn, r


def _norm_mod_bwd(dh, xn, r, nw, sc):
    dsh = jnp.sum(dh, axis=0, keepdims=True)
    dsc = jnp.sum(dh * (xn * nw), axis=0, keepdims=True)
    dnw = jnp.sum(dh * (1.0 + sc) * xn, axis=0, keepdims=True)
    dxn = dh * nw * (1.0 + sc)
    dx = r * (dxn - xn * jnp.mean(dxn * xn, axis=-1, keepdims=True))
    return dx, dsh, dsc, dnw


def _acc_rows(ref, first, rows):
    rid = _iota(ref.shape, 0)
    upd = jnp.zeros(ref.shape, F32)
    for k, v in rows.items():
        upd = upd + jnp.where(rid == k, v, 0.0)

    @pl.when(first)
    def _():
        ref[...] = upd

    @pl.when(jnp.logical_not(first))
    def _():
        ref[...] += upd


def _in_proj(R, x, mod, nw1, w_arr):
    B, T = R.B, R.T

    def body(x_ref, mod_ref, nw_ref, w_ref, z_ref, xbc_ref, qa_ref, kva_ref, pool_ref, kr_ref, dt_ref, h_ref):
        h, _, _ = _norm_mod(x_ref[...], nw_ref[...], _row(mod_ref, 0), _row(mod_ref, 1))
        hb = h.astype(BF)
        h_ref[...] = hb
        p = jnp.dot(hb, w_ref[...], preferred_element_type=F32)
        z_ref[...] = p[:, A_Z:A_XBC]
        xbc_ref[...] = p[:, A_XBC:A_QA]
        qa_ref[...] = p[:, A_QA:A_KVA]
        kva_ref[...] = p[:, A_KVA:A_POOL]
        pool_ref[...] = p[:, A_POOL:A_KR]
        kr_ref[...] = p[:, A_KR:A_DT]
        dt_ref[...] = p[:, A_DT:PC]

    widths = [SSD_IN, XBC, QL, QL, PD, HP, HP]
    outs = [(_sds((B, T, w), F32), R.row(w)) for w in widths] + [(_sds((B, T, D), BF), R.row(D))]
    return R.call("in_proj", body,
                  [(x, R.row(D)), (mod, R.mod()), (nw1, R.const((1, D))), (w_arr, R.const((D, PC)))],
                  outs, extra=8 << 20)


def _conv_pre(ext, w_ref, b_ref):
    return (_row(w_ref, 0) * _shift_rows(ext, -1)[8:8 + TM] + _row(w_ref, 1) * ext[8:8 + TM]
            + _row(w_ref, 2) * _shift_rows(ext, 1)[8:8 + TM] + _row(w_ref, 3) * _shift_rows(ext, 2)[8:8 + TM]
            + b_ref[...])


def _softplus(x):
    return jnp.maximum(x, 0.0) + jnp.log(1.0 + jnp.exp(-jnp.abs(x)))


def _ssd_prep(R, xbc_raw, dt_raw, conv_w8, conv_b, dtb):
    B, T = R.B, R.T

    def body(raw_ref, pv_ref, nx_ref, dtr_ref, w_ref, b_ref, dtb_ref, xbc_ref, dt_ref):
        i = pl.program_id(1)
        ext = R.ext(i, pv_ref, raw_ref[...], nx_ref)
        xbc_ref[...] = _silu(_conv_pre(ext, w_ref, b_ref))
        lane = _iota((TM, HP), 1)
        dtv = _softplus(dtr_ref[...] + dtb_ref[...])
        keep = lane < H
        dt_ref[0] = jnp.where(keep, dtv, 0.0)
        dt_ref[1] = jnp.where(keep, pltpu.roll(dtv, HP - H, axis=1), 0.0)

    return R.call("ssd_prep", body,
                  [(xbc_raw, R.row(XBC)), (xbc_raw, R.prev8(XBC)), (xbc_raw, R.next8(XBC)), (dt_raw, R.row(HP)),
                   (conv_w8, R.const((8, XBC))), (conv_b, R.const((1, XBC))), (dtb, R.const((1, HP)))],
                  [(_sds((B, T, XBC), F32), R.row(XBC)), (_sds((B, 2, T, HP), F32), R.row2(HP))], extra=12 << 20)


def _chunk_index(d, s, nc, nctc):
    if d == 0:
        return s
    return jnp.where(s < nctc, nctc - 1 - s, nc - 1 - (s - nctc))


def _dot01_r(x, m01, passes=3):
    b16 = jnp.bfloat16
    m = m01.astype(b16)
    out, rest = None, x
    for _ in range(passes):
        part = rest.astype(b16)
        rest = rest - part.astype(F32)
        term = jnp.dot(part, m, preferred_element_type=F32)
        out = term if out is None else out + term
    return out


def _scan_common(d, dtv, alog_ref):
    sel = _iota((HP, H * HP), 0) == (_iota((HP, H * HP), 1) >> 7)
    a_all = _dot01_r(-jnp.exp(alog_ref[d]), sel)
    dt_all = _dot01_r(dtv, sel)
    adt_all = dt_all * jnp.concatenate([a_all] * (CH // 8), axis=0)
    row = _iota((CH, CH), 0)
    col = _iota((CH, CH), 1)
    inc = col <= row if d == 0 else col >= row
    inc_t = row <= col if d == 0 else row >= col
    q_all = _dot01(inc, adt_all)
    return a_all, dt_all, adt_all, q_all, inc, inc_t


def _head_decay(h, adt_all, q_all, inc, inc_t):
    q = q_all[:, HP * h:HP * (h + 1)]
    q_t = q.T
    qtot = jnp.sum(adt_all[:, HP * h:HP * (h + 1)], axis=0, keepdims=True)
    lm = jnp.where(inc, jnp.exp(q - q_t), 0.0)
    lm_t = jnp.where(inc_t, jnp.exp(q_t - q), 0.0)
    return lm, lm_t, jnp.exp(q), jnp.exp(qtot), jnp.exp(qtot - q)


def _ssd_scan(B, T, xbc, dt2, alog2):
    nc, nctc = T // CH, CTX // CH

    def chain(d, xbc_ref, dt_ref, alog_ref, y_ref, hin_ref, hs):
        xbc_v = xbc_ref[...]
        dtv = dt_ref[...]
        blk = alog_ref[d]
        adt = dtv * -jnp.exp(jnp.sum(jnp.where(_iota(blk.shape, 0) == 0, blk, 0.0), axis=0, keepdims=True))
        row, col = _iota((CH, CH), 0), _iota((CH, CH), 1)
        inc = col <= row if d == 0 else col >= row
        inc_tf = (row <= col if d == 0 else row >= col).astype(F32)
        q = _dot01(inc, adt)
        hin_ref[...] = hs[d]
        for g in range(2):
            bg = xbc_v[:, SSD_IN + NST * g:SSD_IN + NST * (g + 1)]
            cg = xbc_v[:, SSD_IN + 2 * NST + NST * g:SSD_IN + 2 * NST + NST * (g + 1)]
            gm = _dot_nt(cg, bg)
            for r in range(3):
                h = 3 * g + r
                onehot = (_iota((1, HP), 1) == h).astype(F32)
                adt_h = jnp.sum(adt * onehot, axis=1, keepdims=True)
                qc = jnp.sum(q * onehot, axis=1, keepdims=True)
                dt_h = jnp.sum(dtv * onehot, axis=1, keepdims=True)
                qr = jnp.sum(adt_h * inc_tf, axis=0, keepdims=True)
                qtot = jnp.sum(adt_h, axis=0, keepdims=True)
                lm = jnp.where(inc, jnp.exp(qc - qr), 0.0)
                xh = xbc_v[:, P * h:P * (h + 1)] * dt_h
                hprev = hs[d, :, P * h:P * (h + 1)]
                y_ref[:, P * h:P * (h + 1)] = _dot(gm * lm, xh) + jnp.exp(qc) * _dot(cg, hprev)
                hs[d, :, P * h:P * (h + 1)] = jnp.exp(qtot) * hprev + _dot_tn(bg, xh * jnp.exp(qtot - qc))

    def body(xf_ref, xb_ref, dtf_ref, dtb_ref, alog_ref, yf_ref, yb_ref, hf_ref, hb_ref, hs):
        @pl.when(pl.program_id(1) == 0)
        def _():
            hs[...] = jnp.zeros(hs.shape, F32)

        chain(0, xf_ref, dtf_ref, alog_ref, yf_ref, hf_ref, hs)
        chain(1, xb_ref, dtb_ref, alog_ref, yb_ref, hb_ref, hs)

    cidx = lambda d: (lambda s: _chunk_index(d, s, nc, nctc))
    specs = lambda d: dict(
        xbc=pl.BlockSpec((None, CH, XBC), lambda b, s: (b, cidx(d)(s), 0)),
        dt=pl.BlockSpec((None, None, CH, HP), lambda b, s: (b, d, cidx(d)(s), 0)),
        y=pl.BlockSpec((None, CH, SSD_IN), lambda b, s: (b, cidx(d)(s), 0)),
        h=pl.BlockSpec((None, None, NST, SSD_IN), lambda b, s: (b, cidx(d)(s), 0, 0)))
    f, r = specs(0), specs(1)
    bb = 2 * (_nbytes((CH, XBC), F32) + _nbytes((CH, HP), F32) + _nbytes((CH, SSD_IN), F32) + _nbytes((SSD_IN, NST), F32))
    return pl.pallas_call(
        functools.partial(body), name="ssd_scan", grid=(B, nc),
        in_specs=[f["xbc"], r["xbc"], f["dt"], r["dt"], pl.BlockSpec((2, 8, HP), lambda b, s: (0, 0, 0))],
        out_specs=[f["y"], r["y"], f["h"], r["h"]],
        out_shape=[_out((B, T, SSD_IN), F32)] * 2 + [_out((B, nc, NST, SSD_IN), F32)] * 2,
        scratch_shapes=[pltpu.VMEM((2, NST, SSD_IN), F32)],
        compiler_params=_params(("arbitrary",) * 2, bb, 12 << 20),
    )(*_hbm([xbc, xbc, dt2, dt2, alog2]))


def _swap8(u):
    lane = _iota(u.shape, 1)
    n = u.shape[1]
    return jnp.where((lane & 15) < 8, pltpu.roll(u, n - 8, axis=1), pltpu.roll(u, 8, axis=1))


def _rope(u, cos, sin_signed):
    return u * cos + _swap8(u) * sin_signed


def _rms(x, w):
    r = lax.rsqrt(jnp.mean(x * x, axis=-1, keepdims=True) + EPS)
    xh = x * r
    return xh * w, xh, r


def _rms_bwd(dy, xh, r, w):
    dw = jnp.sum(dy * xh, axis=0, keepdims=True)
    dxh = dy * w
    return r * (dxh - xh * jnp.mean(dxh * xh, axis=-1, keepdims=True)), dw


def _tile6(t):
    return jnp.concatenate([t] * H, axis=1)


def _per_head(fn, u):
    return jnp.concatenate([fn(u[:, HP * h:HP * (h + 1)]) for h in range(H)], axis=1)


def _mla_prep(R, qa, kva, kr, qnw, kvnw, wq, wk, wv, cos, sin):
    B, T = R.B, R.T
    scale = QK ** -0.5

    def body(qa_ref, kva_ref, kr_ref, qnw_ref, kvnw_ref, wq_ref, wk_ref, wv_ref, cos_ref, sin_ref,
             qt_ref, kt_ref, qtr_ref, ktr_ref, vtr_ref, cq_ref, ckv_ref):
        cq, _, _ = _rms(qa_ref[...], qnw_ref[...])
        ckv, _, _ = _rms(kva_ref[...], kvnw_ref[...])
        cqb, ckvb = cq.astype(BF), ckv.astype(BF)
        cq_ref[...] = cqb
        ckv_ref[...] = ckvb
        cos1, sin1 = cos_ref[...], sin_ref[...]
        q = _per_head(lambda u: _rope(u, cos1, sin1), jnp.dot(cqb, wq_ref[...], preferred_element_type=F32)) * scale
        qt_ref[...] = q.astype(BF)
        kk = _rope(kr_ref[...], cos1, sin1)
        k = jnp.dot(ckvb, wk_ref[...], preferred_element_type=F32) + _tile6(kk)
        kt_ref[...] = k.astype(BF)
        v = jnp.dot(ckvb, wv_ref[...], preferred_element_type=F32)
        v = jnp.where((_iota((TM, QW), 1) & (HP - 1)) == VH, 1.0, v)
        for h in range(H):
            cols = slice(HP * h, HP * (h + 1))
            qtr_ref[h] = q[:, cols].T.astype(BF)
            ktr_ref[h] = k[:, cols].T.astype(BF)
            vtr_ref[h] = v[:, cols].T.astype(BF)

    tr = (_sds((B, H, HP, T), BF), pl.BlockSpec((None, H, HP, TM), lambda b, i: (b, 0, 0, i)))
    return R.call("mla_prep", body,
                  [(qa, R.row(QL)), (kva, R.row(QL)), (kr, R.row(HP)), (qnw, R.const((1, QL))), (kvnw, R.const((1, QL))),
                   (wq, R.const((QL, QW))), (wk, R.const((QL, QW))), (wv, R.const((QL, QW))),
                   (cos, pl.BlockSpec((TM, HP), lambda b, i: (i, 0))), (sin, pl.BlockSpec((TM, HP), lambda b, i: (i, 0)))],
                  [(_sds((B, T, QW), BF), R.row(QW))] * 2 + [tr] * 3 + [(_sds((B, T, QL), BF), R.row(QL))] * 2, extra=12 << 20)


def _flash_fwd(B, T, q_t, kt, v_t):
    nq, nk = T // TQ, T // TK
    HS = 2

    def body(q_ref, k_ref, v_ref, o_ref, lse_ref, s_scr):
        i = pl.program_id(2)

        def attend(nch):
            ms = []
            for hh in range(HS):
                q_tr = q_ref[hh]
                mrun = None
                for j in range(nch):
                    s = _dot(k_ref[TK * j:TK * (j + 1), HP * hh:HP * (hh + 1)], q_tr)
                    s_scr[hh, j] = s
                    mrun = s if mrun is None else jnp.maximum(mrun, s)
                ms.append(jnp.max(mrun, axis=0, keepdims=True))
            row = _iota((HP, TQ), 0)
            for hh in range(HS):
                acc_t = jnp.zeros((HP, TQ), F32)
                for j in range(nch):
                    acc_t = acc_t + _dot(v_ref[hh, :, TK * j:TK * (j + 1)], jnp.exp(s_scr[hh, j] - ms[hh]))
                l = jnp.sum(jnp.where(row == VH, acc_t, 0.0), axis=0, keepdims=True)
                o_ref[:, HP * hh:HP * (hh + 1)] = jnp.where(row < VH, acc_t / l, 0.0).T
                lse_ref[hh] = jnp.broadcast_to(ms[hh] + jnp.log(l), (HP, TQ)).T

        @pl.when(i < CTX // TQ)
        def _():
            attend(CTX // TK)

        @pl.when(i >= CTX // TQ)
        def _():
            attend(nk)

    bb = HS * (_nbytes((TQ, HP), BF) + 2 * _nbytes((T, HP), BF) + 2 * _nbytes((TQ, HP), F32))
    return pl.pallas_call(
        functools.partial(body), name="flash_fwd", grid=(B, H // HS, nq),
        in_specs=[pl.BlockSpec((None, HS, HP, TQ), lambda b, h, i: (b, h, 0, i)),
                  pl.BlockSpec((None, T, HS * HP), lambda b, h, i: (b, 0, h)),
                  pl.BlockSpec((None, HS, HP, T), lambda b, h, i: (b, h, 0, 0))],
        out_specs=[pl.BlockSpec((None, TQ, HS * HP), lambda b, h, i: (b, i, h)),
                   pl.BlockSpec((None, HS, TQ, HP), lambda b, h, i: (b, h, i, 0))],
        out_shape=[_out((B, T, QW), F32), _out((B, H, T, HP), F32)],
        scratch_shapes=[pltpu.VMEM((HS, nk, TK, TQ), F32)],
        compiler_params=_params(("arbitrary",) * 3, bb, _nbytes((HS, nk, TK, TQ), F32) + (8 << 20)),
    )(*_hbm([q_t, kt, v_t]))


def _pool_terms(R, i, rows):
    t, n = R.stream_pos(i, rows)
    lane = _iota((1, PD), 1)
    half = jnp.where(lane < 64, 1, jnp.where(lane < 128, 2, jnp.where(lane < 192, 4, 8)))
    cnt = (jnp.minimum(t + half, n) - jnp.maximum(t - half, 0)).astype(F32)
    valid = jnp.logical_and(t >= 0, t < n)
    return jnp.where(valid, cnt, 1.0), valid.astype(F32), lane


def _lane_select(lane, a2, a4, a8, a16):
    return jnp.where(lane < 64, a2, jnp.where(lane < 128, a4, jnp.where(lane < 192, a8, a16)))


def _pool_centred(R, i, ext):
    cnt, valid, lane = _pool_terms(R, i, ext.shape[0])
    s2 = ext + _shift_rows(ext, -1)
    s4 = _shift_rows(s2, -1) + _shift_rows(s2, 1)
    s8 = _shift_rows(s4, -2) + _shift_rows(s4, 2)
    s16 = _shift_rows(s8, -4) + _shift_rows(s8, 4)
    return _lane_select(lane, s2, s4, s8, s16) / cnt - ext, cnt, valid, lane


def _pool_fwd(R, u, wbd, scale):
    B, T = R.B, R.T

    def body(u_ref, pv_ref, nx_ref, w_ref, sc_ref, o_ref):
        i = pl.program_id(1)
        ext = R.ext(i, pv_ref, u_ref[...], nx_ref)
        dm, _, _, _ = _pool_centred(R, i, ext)
        o_ref[...] = _dot(dm[8:8 + TM], w_ref[...]) * sc_ref[...]

    return R.call("pool_fwd", body,
                  [(u, R.row(PD)), (u, R.prev8(PD)), (u, R.next8(PD)), (wbd, R.const((PD, PD))), (scale, R.const((1, PD)))],
                  [(_sds((B, T, PD), F32), R.row(PD))], extra=8 << 20)


def _group_mask():
    return _iota((1, SSD_IN), 1) < SSD_IN // 2


def _ssd_gate(yf_ref, yb_ref, xbc_ref, z_ref, dsk_ref):
    ytot = yf_ref[...] + yb_ref[...] + xbc_ref[:, 0:SSD_IN] * dsk_ref[...]
    z = z_ref[...]
    gz = ytot * _silu(z)
    g0 = _group_mask()
    sq = gz * gz
    s0 = jnp.sum(jnp.where(g0, sq, 0.0), axis=1, keepdims=True)
    s1 = jnp.sum(jnp.where(g0, 0.0, sq), axis=1, keepdims=True)
    half = SSD_IN // 2
    r = jnp.where(g0, lax.rsqrt(s0 / half + EPS), lax.rsqrt(s1 / half + EPS))
    return ytot, z, gz, r


def _out_proj(R, x, mod, yf, yb, xbc, z, o, pool, dsk, snw, wout):
    B, T = R.B, R.T

    def body(x_ref, mod_ref, yf_ref, yb_ref, xbc_ref, z_ref, o_ref, pool_ref, dsk_ref, snw_ref, w_ref, xmid_ref, cat_ref,
             mix_ref):
        _, _, gz, r = _ssd_gate(yf_ref, yb_ref, xbc_ref, z_ref, dsk_ref)
        cat_ref[:, 0:SSD_IN] = (gz * r * snw_ref[...]).astype(BF)
        cat_ref[:, SSD_IN:SSD_IN + QW] = o_ref[...].astype(BF)
        cat_ref[:, SSD_IN + QW:MIXW] = pool_ref[...].astype(BF)
        mix = jnp.dot(cat_ref[...], w_ref[...], preferred_element_type=F32)
        mix_ref[...] = mix.astype(BF)
        xmid_ref[...] = x_ref[...] + _row(mod_ref, 2) * mix

    return R.call("out_proj", body,
                  [(x, R.row(D)), (mod, R.mod()), (yf, R.row(SSD_IN)), (yb, R.row(SSD_IN)), (xbc, R.row(XBC)), (z, R.row(SSD_IN)),
                   (o, R.row(QW)),
                   (pool, R.row(PD)), (dsk, R.const((1, SSD_IN))), (snw, R.const((1, SSD_IN))), (wout, R.const((MIXW, D)))],
                  [(_sds((B, T, D), F32), R.row(D)), (_sds((B, T, MIXW), BF), R.row(MIXW)), (_sds((B, T, D), BF), R.row(D))],
                  extra=8 << 20)


def _load_once(first, pairs, sem):
    @pl.when(first)
    def _():
        cps = [pltpu.make_async_copy(src, dst, sem.at[k]) for k, (src, dst) in enumerate(pairs)]
        for cp in cps:
            cp.start()
        for cp in cps:
            cp.wait()


ANY = pl.BlockSpec(memory_space=pl.ANY)


def _mlp_fwd(R, xmid, mod, nw2, w1, w2):
    B, T = R.B, R.T

    def body(x_ref, mod_ref, nw_ref, w1_hbm, w2_hbm, xo_ref, h_ref, u_ref, y_ref, w1_v, w2_v, sem):
        first = jnp.logical_and(pl.program_id(0) == 0, pl.program_id(1) == 0)
        _load_once(first, [(w1_hbm, w1_v), (w2_hbm, w2_v)], sem)
        x = x_ref[...]
        h, _, _ = _norm_mod(x, nw_ref[...], _row(mod_ref, 3), _row(mod_ref, 4))
        hb = h.astype(BF)
        h_ref[...] = hb
        y = jnp.zeros((TM, D), F32)
        for j in range(NDEV):
            u = jnp.dot(hb, w1_v[j], preferred_element_type=F32)
            u_ref[:, FSH * j:FSH * (j + 1)] = u.astype(BF)
            a = jnp.square(jnp.maximum(u, 0.0))
            y = y + jnp.dot(a.astype(BF), w2_v[j], preferred_element_type=F32)
        y_ref[...] = y.astype(BF)
        xo_ref[...] = x + _row(mod_ref, 5) * y

    return R.call("mlp_fwd", body,
                  [(xmid, R.row(D)), (mod, R.mod()), (nw2, R.const((1, D))), (w1, ANY), (w2, ANY)],
                  [(_sds((B, T, D), F32), R.row(D)), (_sds((B, T, D), BF), R.row(D)), (_sds((B, T, FF), BF), R.row(FF)),
                   (_sds((B, T, D), BF), R.row(D))],
                  scratch=[pltpu.VMEM((NDEV, D, FSH), w1.dtype), pltpu.VMEM((NDEV, FSH, D), w2.dtype), pltpu.SemaphoreType.DMA((2,))],
                  extra=(2 * _nbytes((NDEV, D, FSH), BF)) + (8 << 20))


def _loss_head(R, x, tgt, fnw):
    B, T = R.B, R.T

    def body(x_ref, t_ref, w_ref, dx_ref, loss_ref, dw_ref):
        b, i = pl.program_id(0), pl.program_id(1)
        live = jnp.where(i >= R.nct, 1.0, 0.0)
        y, xh, r = _rms(x_ref[...], w_ref[...])
        err = (y - t_ref[...]) * live
        dy = err / D
        dxn, dw = _rms_bwd(dy, xh, r, w_ref[...])
        dx_ref[...] = dxn
        first = jnp.logical_and(b == 0, i == 0)
        part = 0.5 * jnp.sum(jnp.sum(err * err, axis=1, keepdims=True), axis=0, keepdims=True) / D
        _acc_rows(loss_ref, first, {0: jnp.broadcast_to(part, (1, HP))})
        _acc_rows(dw_ref, first, {0: dw})

    return R.call("loss_head", body,
                  [(x, R.row(D)), (tgt, R.tgt(D)), (fnw, R.const((1, D)))],
                  [(_sds((B, T, D), F32), R.row(D)), (_sds((8, HP), F32), R.const((8, HP))), (_sds((8, D), F32), R.const((8, D)))],
                  extra=8 << 20)


def _tn_matmul(name, a, b, tn):
    B, T, K = a.shape
    N = b.shape[2]
    nk = 1
    while T % nk or (T // nk) > 1088 or (T // nk) % 16:
        nk += 1
    tk = T // nk
    kt = K if K <= 1536 else 1024
    assert K % kt == 0 and N % tn == 0

    def body(a_ref, b_ref, o_ref):
        first = jnp.logical_and(pl.program_id(2) == 0, pl.program_id(3) == 0)

        @pl.when(first)
        def _():
            o_ref[...] = jnp.zeros(o_ref.shape, F32)

        o_ref[...] += _dot_tn(a_ref[...], b_ref[...])

    bb = _nbytes((tk, kt), a.dtype) + _nbytes((tk, tn), b.dtype) + _nbytes((kt, tn), F32)
    return pl.pallas_call(
        functools.partial(body), name=name, grid=(N // tn, K // kt, B, nk),
        in_specs=[pl.BlockSpec((None, tk, kt), lambda j, kk, bi, t: (bi, t, kk)),
                  pl.BlockSpec((None, tk, tn), lambda j, kk, bi, t: (bi, t, j))],
        out_specs=pl.BlockSpec((None, kt, tn), lambda j, kk, bi, t: (j, kk, 0)),
        out_shape=_out((N // tn, K, tn), F32),
        compiler_params=_params(("arbitrary",) * 4, bb, 8 << 20),
    )(*_hbm([a, b]))


def _mlp_bwd(R, dxo, xmid, ubf, ybf, mod, nw2, w1, w2):
    B, T = R.B, R.T

    def body(dxo_ref, x_ref, u_ref, y_ref, mod_ref, nw_ref, w1_hbm, w2_hbm,
             dxm_ref, du_ref, a_ref, dy_ref, dmod_ref, dnw_ref, w1_v, w2_v, sem):
        b, i = pl.program_id(0), pl.program_id(1)
        _load_once(jnp.logical_and(b == 0, i == 0), [(w1_hbm, w1_v), (w2_hbm, w2_v)], sem)
        dxo = dxo_ref[...]
        _, xn, r = _norm_mod(x_ref[...], nw_ref[...], _row(mod_ref, 3), _row(mod_ref, 4))
        dyb = (dxo * _row(mod_ref, 5)).astype(BF)
        dy_ref[...] = dyb
        dg2 = jnp.sum(dxo * y_ref[...].astype(F32), axis=0, keepdims=True)
        dh = jnp.zeros((TM, D), F32)
        for j in range(NDEV):
            rl = jnp.maximum(u_ref[:, FSH * j:FSH * (j + 1)].astype(F32), 0.0)
            a_ref[:, FSH * j:FSH * (j + 1)] = (rl * rl).astype(BF)
            du = (_dot_nt(dyb, w2_v[j]) * (2.0 * rl)).astype(BF)
            du_ref[:, FSH * j:FSH * (j + 1)] = du
            dh = dh + _dot_nt(du, w1_v[j])
        dx, dsh, dsc, dnw = _norm_mod_bwd(dh, xn, r, nw_ref[...], _row(mod_ref, 4))
        dxm_ref[...] = dxo + dx
        _acc_rows(dmod_ref, R.first_of_stream(i), {3: dsh, 4: dsc, 5: dg2})
        _acc_rows(dnw_ref, jnp.logical_and(b == 0, i == 0), {0: dnw})

    return R.call("mlp_bwd", body,
                  [(dxo, R.row(D)), (xmid, R.row(D)), (ubf, R.row(FF)), (ybf, R.row(D)), (mod, R.mod()), (nw2, R.const((1, D))),
                   (w1, ANY), (w2, ANY)],
                  [(_sds((B, T, D), F32), R.row(D)), (_sds((B, T, FF), BF), R.row(FF)), (_sds((B, T, FF), BF), R.row(FF)),
                   (_sds((B, T, D), BF), R.row(D)), (_sds((B, 2, 8, D), F32), R.mod()), (_sds((8, D), F32), R.const((8, D)))],
                  scratch=[pltpu.VMEM((NDEV, D, FSH), w1.dtype), pltpu.VMEM((NDEV, FSH, D), w2.dtype), pltpu.SemaphoreType.DMA((2,))],
                  extra=(2 * _nbytes((NDEV, D, FSH), BF)) + (8 << 20))


def _out_bwd(R, dxm, mod, mixbf, yf, yb, xbc, z, o, dsk, snw, wout):
    B, T = R.B, R.T

    def body(dxm_ref, mod_ref, mix_ref, yf_ref, yb_ref, xbc_ref, z_ref, o_ref, dsk_ref, snw_ref, w_ref,
             dmix_ref, dz_ref, dyt_ref, dxsk_ref, do_ref, dotr_ref, dd_ref, dpo_ref, dmod_ref, dvec_ref):
        b, i = pl.program_id(0), pl.program_id(1)
        dxm = dxm_ref[...]
        dmixb = (dxm * _row(mod_ref, 2)).astype(BF)
        dmix_ref[...] = dmixb
        dg1 = jnp.sum(dxm * mix_ref[...].astype(F32), axis=0, keepdims=True)
        dcat = _dot_nt(dmixb, w_ref[...])
        do_v = dcat[:, SSD_IN:SSD_IN + QW]
        do_ref[...] = do_v.astype(BF)
        doo = do_v * o_ref[...]
        for h in range(H):
            dd_ref[h] = jnp.broadcast_to(jnp.sum(doo[:, HP * h:HP * (h + 1)], axis=1, keepdims=True), (TM, HP))
            dotr_ref[h] = do_v[:, HP * h:HP * (h + 1)].T.astype(BF)
        dpo_ref[...] = dcat[:, SSD_IN + QW:MIXW]
        dsn = dcat[:, 0:SSD_IN]
        ytot, zv, gz, r = _ssd_gate(yf_ref, yb_ref, xbc_ref, z_ref, dsk_ref)
        gh = gz * r
        dsnw = jnp.sum(dsn * gh, axis=0, keepdims=True)
        dgh = dsn * snw_ref[...]
        g0 = _group_mask()
        pr = dgh * gh
        half = SSD_IN // 2
        m0 = jnp.sum(jnp.where(g0, pr, 0.0), axis=1, keepdims=True) / half
        m1 = jnp.sum(jnp.where(g0, 0.0, pr), axis=1, keepdims=True) / half
        dgz = r * (dgh - gh * jnp.where(g0, m0, m1))
        dyt = dgz * _silu(zv)
        dz_ref[...] = (dgz * ytot * _dsilu(zv)).astype(BF)
        dyt_ref[...] = dyt
        dxsk_ref[...] = dyt * dsk_ref[...]
        ddsk = jnp.sum(dyt * xbc_ref[:, 0:SSD_IN], axis=0, keepdims=True)
        _acc_rows(dmod_ref, R.first_of_stream(i), {2: dg1})
        _acc_rows(dvec_ref, jnp.logical_and(b == 0, i == 0), {0: dsnw, 1: ddsk})

    return R.call("out_bwd", body,
                  [(dxm, R.row(D)), (mod, R.mod()), (mixbf, R.row(D)), (yf, R.row(SSD_IN)), (yb, R.row(SSD_IN)), (xbc, R.row(XBC)),
                   (z, R.row(SSD_IN)),
                   (o, R.row(QW)), (dsk, R.const((1, SSD_IN))), (snw, R.const((1, SSD_IN))), (wout, R.const((MIXW, D)))],
                  [(_sds((B, T, D), BF), R.row(D)), (_sds((B, T, SSD_IN), BF), R.row(SSD_IN)), (_sds((B, T, SSD_IN), F32), R.row(SSD_IN)),
                   (_sds((B, T, SSD_IN), F32), R.row(SSD_IN)), (_sds((B, T, QW), BF), R.row(QW)),
                   (_sds((B, H, HP, T), BF), pl.BlockSpec((None, H, HP, TM), lambda b, i: (b, 0, 0, i))),
                   (_sds((B, H, T, HP), F32), pl.BlockSpec((None, H, TM, HP), lambda b, i: (b, 0, i, 0))),
                   (_sds((B, T, PD), F32), R.row(PD)),
                   (_sds((B, 2, 8, D), F32), R.mod()), (_sds((8, SSD_IN), F32), R.const((8, SSD_IN)))],
                  extra=8 << 20)


def _pool_bwd(R, dpo, u, wbd, scale):
    B, T = R.B, R.T

    def body(d_ref, dpv_ref, dnx_ref, u_ref, upv_ref, unx_ref, w_ref, sc_ref, du_ref, dw_ref, dsc_ref):
        b, i = pl.program_id(0), pl.program_id(1)
        ext_u = R.ext(i, upv_ref, u_ref[...], unx_ref)
        ext_d = R.ext(i, dpv_ref, d_ref[...], dnx_ref)
        dm, cnt, valid, lane = _pool_centred(R, i, ext_u)
        ddm = _dot_nt(ext_d * sc_ref[...], w_ref[...]) * valid
        e = ddm / cnt
        a2 = e + _shift_rows(e, 1)
        a4 = _shift_rows(a2, -1) + _shift_rows(a2, 1)
        a8 = _shift_rows(a4, -2) + _shift_rows(a4, 2)
        a16 = _shift_rows(a8, -4) + _shift_rows(a8, 4)
        du_ref[...] = (_lane_select(lane, a2, a4, a8, a16) - ddm)[8:8 + TM].astype(BF)
        dmc = dm[8:8 + TM]
        dpo_c = d_ref[...]
        first = jnp.logical_and(b == 0, i == 0)

        @pl.when(first)
        def _():
            dw_ref[...] = jnp.zeros(dw_ref.shape, F32)

        dw_ref[...] += _dot_tn(dmc, dpo_c * sc_ref[...])
        _acc_rows(dsc_ref, first, {0: jnp.sum(dpo_c * _dot(dmc, w_ref[...]), axis=0, keepdims=True)})

    return R.call("pool_bwd", body,
                  [(dpo, R.row(PD)), (dpo, R.prev8(PD)), (dpo, R.next8(PD)), (u, R.row(PD)), (u, R.prev8(PD)), (u, R.next8(PD)),
                   (wbd, R.const((PD, PD))), (scale, R.const((1, PD)))],
                  [(_sds((B, T, PD), BF), R.row(PD)), (_sds((PD, PD), F32), R.const((PD, PD))), (_sds((8, PD), F32), R.const((8, PD)))],
                  extra=8 << 20)


def _flash_bwd(B, T, qt, kt, dob, q_t, k_t, v_t, do_t, lse_b, dd_b):
    nk, nq = T // TK, T // TQ

    def body(k_ref, kt_ref, vt_ref, q_ref, do_ref, qt_ref, dot_ref, lse_ref, dd_ref, dq_ref, dk_ref, dv_ref,
             s_scr, dp_scr, ds_scr):
        j = pl.program_id(2)
        s_scr[...] = _dot(q_ref[...], kt_ref[...])
        dp_scr[...] = _dot(do_ref[...], vt_ref[...])
        ctx_keys = jnp.where(j < CTX // TK, 1.0, 0.0)
        dk_tr = jnp.zeros((HP, TK), F32)
        dv_tr = jnp.zeros((HP, TK), F32)
        for i in range(nq):
            rows = slice(TQ * i, TQ * (i + 1))
            lse2 = jnp.concatenate([lse_ref[rows, :]] * (TK // HP), axis=1)
            dd2 = jnp.concatenate([dd_ref[rows, :]] * (TK // HP), axis=1)
            p = jnp.exp(s_scr[rows, :] - lse2)
            if i < CTX // TQ:
                p = p * ctx_keys
            ds = (p * (dp_scr[rows, :] - dd2)).astype(BF)
            ds_scr[rows, :] = ds
            dv_tr = dv_tr + _dot(dot_ref[:, rows], p)
            dk_tr = dk_tr + _dot(qt_ref[:, rows], ds)
        dq = _dot(ds_scr[...], k_ref[...])

        @pl.when(j == 0)
        def _():
            dq_ref[...] = dq

        @pl.when(j > 0)
        def _():
            dq_ref[...] += dq

        dk_ref[...] = dk_tr.T
        dv_ref[...] = dv_tr.T

    tspec = pl.BlockSpec((None, TK, HP), lambda b, h, j: (b, j, h))
    fspec = pl.BlockSpec((None, T, HP), lambda b, h, j: (b, 0, h))
    ttspec = pl.BlockSpec((None, None, HP, TK), lambda b, h, j: (b, h, 0, j))
    ftspec = pl.BlockSpec((None, None, HP, T), lambda b, h, j: (b, h, 0, 0))
    bspec = pl.BlockSpec((None, None, T, HP), lambda b, h, j: (b, h, 0, 0))
    bb = 4 * _nbytes((T, HP), BF) + 3 * _nbytes((T, HP), F32) + 8 * _nbytes((TK, HP), F32)
    scr = 2 * _nbytes((T, TK), F32) + _nbytes((T, TK), BF)
    return pl.pallas_call(
        functools.partial(body), name="flash_bwd", grid=(B, H, nk),
        in_specs=[tspec, ttspec, ttspec, fspec, fspec, ftspec, ftspec, bspec, bspec],
        out_specs=[fspec, tspec, tspec],
        out_shape=[_out((B, T, QW), F32)] * 3,
        scratch_shapes=[pltpu.VMEM((T, TK), F32), pltpu.VMEM((T, TK), F32), pltpu.VMEM((T, TK), BF)],
        compiler_params=_params(("arbitrary",) * 3, bb, scr + (12 << 20)),
    )(*_hbm([kt, k_t, v_t, qt, dob, q_t, do_t, lse_b, dd_b]))


def _mla_bwd(R, dqt, dkt, dvt, qa, kva, qnw, kvnw, wq, wk, wv, cos, sin):
    B, T = R.B, R.T
    scale = QK ** -0.5

    def body(dq_ref, dk_ref, dv_ref, qa_ref, kva_ref, qnw_ref, kvnw_ref, wq_ref, wk_ref, wv_ref, cos_ref, sin_ref,
             dqp_ref, dkb_ref, dvb_ref, dqa_ref, dkva_ref, dkr_ref, dnw_ref):
        b, i = pl.program_id(0), pl.program_id(1)
        cos1, sin1 = cos_ref[...], sin_ref[...]
        dq = dq_ref[...] * scale
        dqp = _per_head(lambda g: g * cos1 + _swap8(g * sin1), dq).astype(BF)
        dqp_ref[...] = dqp
        dkv = dk_ref[...]
        dkb = dkv.astype(BF)
        dkb_ref[...] = dkb
        dvb = dv_ref[...].astype(BF)
        dvb_ref[...] = dvb
        dkk = dkv[:, 0:HP]
        for h in range(1, H):
            dkk = dkk + dkv[:, HP * h:HP * (h + 1)]
        lane = _iota((TM, HP), 1)
        rope_lane = jnp.logical_and(lane >= NOPE, lane < NOPE + ROPE)
        dkr_ref[...] = jnp.where(rope_lane, dkk * cos1 + _swap8(dkk * sin1), 0.0).astype(BF)
        _, qh, qr = _rms(qa_ref[...], qnw_ref[...])
        _, kh, kr_ = _rms(kva_ref[...], kvnw_ref[...])
        dcq = _dot_nt(dqp, wq_ref[...])
        dckv = _dot_nt(dkb, wk_ref[...]) + _dot_nt(dvb, wv_ref[...])
        dqa, dqw = _rms_bwd(dcq, qh, qr, qnw_ref[...])
        dkva, dkw = _rms_bwd(dckv, kh, kr_, kvnw_ref[...])
        dqa_ref[...] = dqa.astype(BF)
        dkva_ref[...] = dkva.astype(BF)
        _acc_rows(dnw_ref, jnp.logical_and(b == 0, i == 0), {0: dqw, 1: dkw})

    tab = pl.BlockSpec((TM, HP), lambda b, i: (i, 0))
    return R.call("mla_bwd", body,
                  [(dqt, R.row(QW)), (dkt, R.row(QW)), (dvt, R.row(QW)), (qa, R.row(QL)), (kva, R.row(QL)),
                   (qnw, R.const((1, QL))), (kvnw, R.const((1, QL))), (wq, R.const((QL, QW))), (wk, R.const((QL, QW))),
                   (wv, R.const((QL, QW))), (cos, tab), (sin, tab)],
                  [(_sds((B, T, QW), BF), R.row(QW))] * 3 + [(_sds((B, T, QL), BF), R.row(QL))] * 2
                  + [(_sds((B, T, HP), BF), R.row(HP)), (_sds((8, QL), F32), R.const((8, QL)))], extra=8 << 20)


def _ssd_scan_bwd(B, T, dyt, xbc, dt2, alog2, hin_f, hin_b):
    nc, nctc = T // CH, CTX // CH

    def chain(d, dy_ref, xbc_ref, dt_ref, alog_ref, hin_ref, dxbc_ref, ddt_ref, da_ref, dhs_all):
        dhs = dhs_all.at[d]
        xbc_v = xbc_ref[...]
        dyv = dy_ref[...]
        a_all, dt_all, adt_all, q_all, inc, inc_t = _scan_common(d, dt_ref[...], alog_ref)
        ones_p = jnp.ones((P, HP), F32)
        total = lambda m: jnp.sum(jnp.sum(m, axis=0, keepdims=True), axis=1, keepdims=True)
        dq_parts, dqtot_parts, ddtx_parts = [], [], []
        for g in range(2):
            bg = xbc_v[:, SSD_IN + NST * g:SSD_IN + NST * (g + 1)]
            cg = xbc_v[:, SSD_IN + 2 * NST + NST * g:SSD_IN + 2 * NST + NST * (g + 1)]
            bg_t, cg_t = bg.T, cg.T
            gm = _dot(cg, bg_t)
            gm_t = _dot(bg, cg_t)
            dgm = jnp.zeros((CH, CH), F32)
            dgm_t = jnp.zeros((CH, CH), F32)
            dbg = jnp.zeros((CH, NST), F32)
            dcg = jnp.zeros((CH, NST), F32)
            for r in range(3):
                h = 3 * g + r
                lm, lm_t, eq, etot, dte = _head_decay(h, adt_all, q_all, inc, inc_t)
                eq_p, etot_p, dte_p = eq[:, :P], etot[:, :P], dte[:, :P]
                dt_p = dt_all[:, HP * h:HP * h + P]
                xs_h = xbc_v[:, P * h:P * (h + 1)]
                xh = xs_h * dt_p
                sm, sm_t = gm * lm, gm_t * lm_t
                dy_h = dyv[:, P * h:P * (h + 1)]
                hprev = hin_ref[:, P * h:P * (h + 1)]
                dho = dhs[:, P * h:P * (h + 1)]
                ds = _dot_nt(dy_h, xh)
                ds_t = _dot_nt(xh, dy_h)
                dx = _dot(sm_t, dy_h)
                edy = eq_p * dy_h
                yo = _dot(cg, hprev)
                dcg = dcg + _dot_nt(edy, hprev)
                dhin = _dot(cg_t, edy) + etot_p * dho
                zs = _dot(bg, dho)
                dx = dx + dte_p * zs
                wm = dte_p * xh * zs
                dbg = dbg + _dot_nt(xh * dte_p, dho)
                dgm = dgm + ds * lm
                dgm_t = dgm_t + ds_t * lm_t
                rs = jnp.sum(ds * sm - ds_t * sm_t, axis=1, keepdims=True) + jnp.sum(edy * yo - wm, axis=1, keepdims=True)
                dq_parts.append(jnp.broadcast_to(rs, (CH, HP)))
                dqtot_parts.append(total(hprev * dho) * etot + total(wm))
                dxbc_ref[:, P * h:P * (h + 1)] = dx * dt_p
                ddtx_parts.append(_dot01_r(dx * xs_h, ones_p, passes=2))
                dhs[:, P * h:P * (h + 1)] = dhin
            dcg = dcg + _dot(dgm, bg)
            dbg = dbg + _dot(dgm_t, cg)
            dxbc_ref[:, SSD_IN + NST * g:SSD_IN + NST * (g + 1)] = dbg
            dxbc_ref[:, SSD_IN + 2 * NST + NST * g:SSD_IN + 2 * NST + NST * (g + 1)] = dcg
        cat = lambda parts: jnp.concatenate(parts, axis=1)
        dadt_all = _dot01(inc_t, cat(dq_parts)) + cat(dqtot_parts)
        ddt_all = cat(ddtx_parts) + dadt_all * jnp.concatenate([a_all] * (CH // 8), axis=0)
        da_all = jnp.sum(dadt_all * dt_all, axis=0, keepdims=True)
        lane, lane1 = _iota((CH, HP), 1), _iota((1, HP), 1)
        ddt = jnp.zeros((CH, HP), F32)
        da = jnp.zeros((1, HP), F32)
        for h in range(H):
            ddt = ddt + jnp.where(lane == h, ddt_all[:, HP * h:HP * (h + 1)], 0.0)
            da = da + jnp.where(lane1 == h, da_all[:, HP * h:HP * (h + 1)], 0.0)
        ddt_ref[...] = ddt
        da_ref[d] += jnp.where(_iota((8, HP), 0) == 0, da, 0.0)

    def body(dyf_ref, dyb_ref, xf_ref, xb_ref, dtf_ref, dtb_ref, alog_ref, hf_ref, hb_ref,
             dxf_ref, dxb_ref, ddtf_ref, ddtb_ref, da_ref, dhs):
        @pl.when(pl.program_id(1) == 0)
        def _():
            dhs[...] = jnp.zeros(dhs.shape, F32)
            da_ref[...] = jnp.zeros(da_ref.shape, F32)

        chain(0, dyf_ref, xf_ref, dtf_ref, alog_ref, hf_ref, dxf_ref, ddtf_ref, da_ref, dhs)
        chain(1, dyb_ref, xb_ref, dtb_ref, alog_ref, hb_ref, dxb_ref, ddtb_ref, da_ref, dhs)

    cidx = lambda d: (lambda s: _chunk_index(d, nc - 1 - s, nc, nctc))
    specs = lambda d: dict(
        dy=pl.BlockSpec((None, CH, SSD_IN), lambda b, s: (b, cidx(d)(s), 0)),
        xbc=pl.BlockSpec((None, CH, XBC), lambda b, s: (b, cidx(d)(s), 0)),
        dt=pl.BlockSpec((None, None, CH, HP), lambda b, s: (b, d, cidx(d)(s), 0)),
        h=pl.BlockSpec((None, None, NST, SSD_IN), lambda b, s: (b, cidx(d)(s), 0, 0)),
        ddt=pl.BlockSpec((None, CH, HP), lambda b, s: (b, cidx(d)(s), 0)))
    f, r = specs(0), specs(1)
    bb = 2 * (2 * _nbytes((CH, XBC), F32) + 2 * _nbytes((CH, HP), F32) + _nbytes((CH, SSD_IN), F32) + _nbytes((SSD_IN, NST), F32))
    return pl.pallas_call(
        functools.partial(body), name="ssd_scan_bwd", grid=(B, nc),
        in_specs=[f["dy"], r["dy"], f["xbc"], r["xbc"], f["dt"], r["dt"], pl.BlockSpec((2, 8, HP), lambda b, s: (0, 0, 0)),
                  f["h"], r["h"]],
        out_specs=[f["xbc"], r["xbc"], f["ddt"], r["ddt"], pl.BlockSpec((None, 2, 8, HP), lambda b, s: (b, 0, 0, 0))],
        out_shape=[_out((B, T, XBC), F32)] * 2 + [_out((B, T, HP), F32)] * 2 + [_out((B, 2, 8, HP), F32)],
        scratch_shapes=[pltpu.VMEM((2, NST, SSD_IN), F32)],
        compiler_params=_params(("arbitrary",) * 2, bb, 16 << 20),
    )(*_hbm([dyt, dyt, xbc, xbc, dt2, dt2, alog2, hin_f, hin_b]))


def _ssd_prep_bwd(R, dxbc_f, dxbc_b, dxsk, ddt_f, ddt_b, xbc_raw, dt_raw, conv_w8, conv_b, dtb):
    B, T = R.B, R.T

    def body(dxf_ref, dxb_ref, dsk_ref, ddtf_ref, ddtb2_ref, raw_ref, pv_ref, nx_ref, dtr_ref, w_ref, b_ref, dtb_ref,
             dpre_ref, ddtr_ref, dcw_ref, dvec_ref, ddtb_ref):
        b, i = pl.program_id(0), pl.program_id(1)
        ext = R.ext(i, pv_ref, raw_ref[...], nx_ref)
        pre = _conv_pre(ext, w_ref, b_ref)
        dxbc = dxf_ref[...] + dxb_ref[...]
        skip = jnp.concatenate([dsk_ref[...], jnp.zeros((TM, XBC - SSD_IN), F32)], axis=1)
        dpre = (dxbc + skip) * _dsilu(pre)
        dpre_ref[...] = dpre
        first = jnp.logical_and(b == 0, i == 0)
        taps = {k: jnp.sum(dpre * _shift_rows(ext, k - 1)[8:8 + TM], axis=0, keepdims=True) for k in range(4)}
        _acc_rows(dcw_ref, first, taps)
        _acc_rows(dvec_ref, first, {0: jnp.sum(dpre, axis=0, keepdims=True)})
        ddt = ddtf_ref[...] + pltpu.roll(ddtb2_ref[...], H, axis=1)
        ddtr = ddt * _sigmoid(dtr_ref[...] + dtb_ref[...])
        ddtr = jnp.where(_iota((TM, HP), 1) < 2 * H, ddtr, 0.0)
        ddtr_ref[...] = ddtr.astype(BF)
        _acc_rows(ddtb_ref, first, {0: jnp.sum(ddtr, axis=0, keepdims=True)})

    return R.call("ssd_prep_bwd", body,
                  [(dxbc_f, R.row(XBC)), (dxbc_b, R.row(XBC)), (dxsk, R.row(SSD_IN)), (ddt_f, R.row(HP)), (ddt_b, R.row(HP)),
                   (xbc_raw, R.row(XBC)), (xbc_raw, R.prev8(XBC)),
                   (xbc_raw, R.next8(XBC)), (dt_raw, R.row(HP)), (conv_w8, R.const((8, XBC))), (conv_b, R.const((1, XBC))),
                   (dtb, R.const((1, HP)))],
                  [(_sds((B, T, XBC), F32), R.row(XBC)), (_sds((B, T, HP), BF), R.row(HP)), (_sds((8, XBC), F32), R.const((8, XBC))),
                   (_sds((8, XBC), F32), R.const((8, XBC))), (_sds((8, HP), F32), R.const((8, HP)))], extra=16 << 20)


def _in_bwd(R, dxm, x, mod, nw1, dz, dpre, dqa, dkva, dpool, dkr, ddtr, conv_w8, w_arr):
    B, T = R.B, R.T

    def body(dxm_ref, x_ref, mod_ref, nw_ref, dz_ref, dp_ref, dpp_ref, dpn_ref, dqa_ref, dkva_ref, dpool_ref, dkr_ref, ddt_ref,
             cw_ref, w_ref, dx_ref, dproj_ref, dmod_ref, dnw_ref):
        b, i = pl.program_id(0), pl.program_id(1)
        ext = R.ext(i, dpp_ref, dp_ref[...], dpn_ref)
        draw = (_row(cw_ref, 0) * _shift_rows(ext, 1)[8:8 + TM] + _row(cw_ref, 1) * ext[8:8 + TM]
                + _row(cw_ref, 2) * _shift_rows(ext, -1)[8:8 + TM] + _row(cw_ref, 3) * _shift_rows(ext, -2)[8:8 + TM])
        dproj_ref[:, A_Z:A_XBC] = dz_ref[...]
        dproj_ref[:, A_XBC:A_QA] = draw.astype(BF)
        dproj_ref[:, A_QA:A_KVA] = dqa_ref[...]
        dproj_ref[:, A_KVA:A_POOL] = dkva_ref[...]
        dproj_ref[:, A_POOL:A_KR] = dpool_ref[...]
        dproj_ref[:, A_KR:A_DT] = dkr_ref[...]
        dproj_ref[:, A_DT:PC] = ddt_ref[...]
        dh = _dot_nt(dproj_ref[...], w_ref[...])
        _, xn, r = _norm_mod(x_ref[...], nw_ref[...], _row(mod_ref, 0), _row(mod_ref, 1))
        dx, dsh, dsc, dnw = _norm_mod_bwd(dh, xn, r, nw_ref[...], _row(mod_ref, 1))
        dx_ref[...] = dxm_ref[...] + dx
        _acc_rows(dmod_ref, R.first_of_stream(i), {0: dsh, 1: dsc})
        _acc_rows(dnw_ref, jnp.logical_and(b == 0, i == 0), {0: dnw})

    return R.call("in_bwd", body,
                  [(dxm, R.row(D)), (x, R.row(D)), (mod, R.mod()), (nw1, R.const((1, D))), (dz, R.row(SSD_IN)), (dpre, R.row(XBC)),
                   (dpre, R.prev8(XBC)), (dpre, R.next8(XBC)), (dqa, R.row(QL)), (dkva, R.row(QL)), (dpool, R.row(PD)), (dkr, R.row(HP)),
                   (ddtr, R.row(HP)), (conv_w8, R.const((8, XBC))), (w_arr, R.const((D, PC)))],
                  [(_sds((B, T, D), F32), R.row(D)), (_sds((B, T, PC), BF), R.row(PC)), (_sds((B, 2, 8, D), F32), R.mod()),
                   (_sds((8, D), F32), R.const((8, D)))], extra=12 << 20)


def _adaln_fwd(cs, mod_w):
    L, _, C = mod_w.shape

    def body(c_ref, w_ref, o_ref):
        s = _silu(c_ref[...]).astype(BF)
        for l in range(L):
            o_ref[l] = jnp.dot(s, w_ref[l].astype(BF), preferred_element_type=F32)

    return pl.pallas_call(functools.partial(body), name="adaln_fwd", out_shape=_sds((L, 24, C), F32),
                          compiler_params=_params(None, _nbytes(mod_w.shape, F32) + _nbytes((L, 24, C), F32), 8 << 20))(cs, mod_w)


def _adaln_bwd(cs, dm, mod_w):
    L, _, C = mod_w.shape

    def body(c_ref, dm_ref, w_ref, gw_ref, gc_ref):
        c = c_ref[...]
        s = _silu(c).astype(BF)
        acc = jnp.zeros((24, D), F32)
        for l in range(L):
            dmb = dm_ref[l].astype(BF)
            gw_ref[l] = _dot_tn(s, dmb)
            acc = acc + _dot_nt(dmb, w_ref[l])
        gc_ref[...] = acc * _dsilu(c)

    return pl.pallas_call(functools.partial(body), name="adaln_bwd", out_shape=[_sds((L, D, C), F32), _sds((24, D), F32)],
                          compiler_params=_params(None, 2 * _nbytes(mod_w.shape, F32), 8 << 20))(cs, dm, mod_w)


def _adamw(name, parts, w, m, v, rt):
    Pn, Rr, C = parts.shape
    c1 = 1.0 - ADAM_B1 ** ADAM_STEP
    c2 = 1.0 - ADAM_B2 ** ADAM_STEP

    def body(p_ref, w_ref, m_ref, v_ref, g_ref, d_ref, nm_ref, nv_ref):
        g = p_ref[0].astype(F32)
        for k in range(1, Pn):
            g = g + p_ref[k].astype(F32)
        mn = ADAM_B1 * m_ref[...] + (1.0 - ADAM_B1) * g
        vn = ADAM_B2 * v_ref[...] + (1.0 - ADAM_B2) * jnp.square(g)
        g_ref[...] = g
        nm_ref[...] = mn
        nv_ref[...] = vn
        d_ref[...] = -ADAM_LR * ((mn / c1) / (jnp.sqrt(vn / c2) + ADAM_EPS) + ADAM_WD * w_ref[...])

    spec = pl.BlockSpec((rt, C), lambda i: (i, 0))
    bb = Pn * _nbytes((rt, C), parts.dtype) + 7 * _nbytes((rt, C), F32)
    return pl.pallas_call(
        functools.partial(body), name=name, grid=(Rr // rt,),
        in_specs=[pl.BlockSpec((Pn, rt, C), lambda i: (0, i, 0)), spec, spec, spec],
        out_specs=[spec] * 4, out_shape=[_out((Rr, C), F32)] * 4,
        compiler_params=_params(("arbitrary",), bb, 4 << 20),
    )(*_hbm([parts, w, m, v]))


MESH = pl.DeviceIdType.MESH


def _my_pos():
    return lax.axis_index("x"), lax.axis_index("y"), lax.axis_index("c")


def _dev_index(x, y, c):
    return 4 * x + 2 * y + c


def _all_gather(name, shards):
    n = len(shards)

    def body(*refs):
        ins, outs = refs[:n], refs[n:2 * n]
        send_sems, recv_sems, local_sem = refs[2 * n:]
        x, y, c = _my_pos()
        me, sibling = (x, y, c), (x, y, 1 - c)
        chips = [(1 - x, y), (x, 1 - y), (1 - x, 1 - y)]

        def copy(t, k, block, to, src=None):
            slot = outs[t].at[_dev_index(*block)]
            return pltpu.make_async_remote_copy(
                src_ref=slot if src is None else src, dst_ref=slot,
                send_sem=send_sems.at[t, k], recv_sem=recv_sems.at[t, k], device_id=to, device_id_type=MESH)

        mine = [pltpu.make_async_copy(ins[t], outs[t].at[_dev_index(*me)], local_sem.at[t]) for t in range(n)]
        for cp in mine:
            cp.start()
        first = []
        for t in range(n):
            first.append(copy(t, 0, me, sibling, src=ins[t]))
            first += [copy(t, 1 + j, me, (*chip, c), src=ins[t]) for j, chip in enumerate(chips)]
        for cp in first:
            cp.start()
        passed = []
        for j, chip in enumerate(chips):
            for t in range(n):
                copy(t, 1 + j, (*chip, c), me).wait_recv()
                cp = copy(t, 4 + j, (*chip, c), sibling)
                cp.start()
                passed.append(cp)
        for t in range(n):
            copy(t, 0, sibling, me).wait_recv()
            for j, chip in enumerate(chips):
                copy(t, 4 + j, (*chip, 1 - c), me).wait_recv()
        for cp in first + passed:
            cp.wait_send()
        for cp in mine:
            cp.wait()

    return pl.pallas_call(
        functools.partial(body), name=name,
        in_specs=[ANY] * n, out_specs=[ANY] * n,
        out_shape=[_sds((NDEV,) + s.shape, s.dtype) for s in shards],
        scratch_shapes=[pltpu.SemaphoreType.DMA((n, 7)), pltpu.SemaphoreType.DMA((n, 7)), pltpu.SemaphoreType.DMA((n,))],
    )(*shards)


def _all_to_all(name, parts):
    n = len(parts)

    def body(*refs):
        ins, outs = refs[:n], refs[n:2 * n]
        send_sems, recv_sems, local_sem = refs[2 * n:]
        x, y, c = _my_pos()
        me = _dev_index(x, y, c)
        peers = [(x ^ ((k >> 2) & 1), y ^ ((k >> 1) & 1), c ^ (k & 1)) for k in range(1, NDEV)]
        mine = [pltpu.make_async_copy(ins[t].at[me], outs[t].at[me], local_sem.at[t]) for t in range(n)]
        for cp in mine:
            cp.start()
        sends = []
        for t in range(n):
            for k, peer in enumerate(peers):
                cp = pltpu.make_async_remote_copy(
                    src_ref=ins[t].at[_dev_index(*peer)], dst_ref=outs[t].at[me],
                    send_sem=send_sems.at[t, k], recv_sem=recv_sems.at[t, k], device_id=peer, device_id_type=MESH)
                cp.start()
                sends.append(cp)
        for t in range(n):
            for k, peer in enumerate(peers):
                slot = outs[t].at[_dev_index(*peer)]
                pltpu.make_async_remote_copy(
                    src_ref=slot, dst_ref=slot, send_sem=send_sems.at[t, k], recv_sem=recv_sems.at[t, k],
                    device_id=peer, device_id_type=MESH).wait_recv()
        for cp in sends:
            cp.wait_send()
        for cp in mine:
            cp.wait()

    return pl.pallas_call(
        functools.partial(body), name=name,
        in_specs=[ANY] * n, out_specs=[ANY] * n,
        out_shape=[_sds(p.shape, p.dtype) for p in parts],
        scratch_shapes=[pltpu.SemaphoreType.DMA((n, 7)), pltpu.SemaphoreType.DMA((n, 7)), pltpu.SemaphoreType.DMA((n,))],
    )(*parts)


SEM = pl.BlockSpec(memory_space=pltpu.SEMAPHORE)
IN_HBM = pl.BlockSpec(memory_space=pltpu.HBM)
DATAFLOW = pltpu.SideEffectType.DATAFLOW_SIDE_EFFECTING


def _flip_peers(x, y, c):
    return [(x ^ ((k >> 2) & 1), y ^ ((k >> 1) & 1), c ^ (k & 1)) for k in range(1, NDEV)]


def _split_copies(srcs, lands, send_sems, recv_sems, gather):
    x, y, c = _my_pos()
    me = _dev_index(x, y, c)
    out = []
    for t in range(len(srcs)):
        for k, peer in enumerate(_flip_peers(x, y, c)):
            p = _dev_index(*peer)
            src = srcs[t] if gather else srcs[t].at[p]
            sems = dict(send_sem=send_sems.at[7 * t + k], recv_sem=recv_sems.at[7 * t + k], device_id=peer, device_id_type=MESH)
            out.append((pltpu.make_async_remote_copy(src_ref=src, dst_ref=lands[t].at[me], **sems),
                        pltpu.make_async_remote_copy(src_ref=src, dst_ref=lands[t].at[p], **sems)))
    return out


def _exchange_start(name, collective_id, srcs, gather):
    n = len(srcs)
    lands = [lax.empty(((NDEV,) + s.shape) if gather else s.shape, s.dtype) for s in srcs]

    def body(*refs):
        src_refs, land_refs = refs[:n], refs[n:2 * n]
        send_sems, recv_sems = refs[2 * n], refs[2 * n + 1]
        token = refs[-1]
        barrier = pltpu.get_barrier_semaphore()
        for peer in _flip_peers(*_my_pos()):
            pl.semaphore_signal(barrier, inc=1, device_id=peer, device_id_type=MESH)
        pl.semaphore_wait(barrier, NDEV - 1)
        for send, _ in _split_copies(src_refs, land_refs, send_sems, recv_sems, gather):
            send.start()
        token[...] = jnp.zeros(token.shape, token.dtype)

    hbm = lambda a: pltpu.HBM(a.shape, a.dtype)
    res = pl.pallas_call(
        functools.partial(body), name=name,
        out_shape=[pltpu.SemaphoreType.DMA((7 * n,)), pltpu.SemaphoreType.DMA((7 * n,))] + [hbm(s) for s in srcs]
        + [hbm(a) for a in lands] + [_sds((8, HP), F32)],
        in_specs=[IN_HBM] * (2 * n), out_specs=[SEM, SEM] + [IN_HBM] * (2 * n) + [pl.BlockSpec(memory_space=pltpu.VMEM)],
        input_output_aliases={i: 2 + i for i in range(2 * n)},
        compiler_params=pltpu.CompilerParams(has_side_effects=DATAFLOW, collective_id=collective_id),
    )(*_hbm(list(srcs) + lands))
    return res[0], res[1], list(res[2:2 + n]), list(res[2 + n:2 + 2 * n]), res[-1]


def _exchange_wait(name, send_sems, recv_sems, srcs, lands, after, gather):
    n = len(srcs)

    def body(*refs):
        src_refs, land_refs = refs[:n], refs[n:2 * n]
        for _, recv in _split_copies(src_refs, land_refs, refs[2 * n], refs[2 * n + 1], gather):
            recv.wait_send()
            recv.wait_recv()

    hbm = lambda a: pltpu.HBM(a.shape, a.dtype)
    res = pl.pallas_call(
        functools.partial(body), name=name,
        out_shape=[hbm(s) for s in srcs] + [hbm(a) for a in lands],
        in_specs=[IN_HBM] * (2 * n) + [SEM, SEM, ANY], out_specs=[IN_HBM] * (2 * n),
        input_output_aliases={i: i for i in range(2 * n)},
        compiler_params=pltpu.CompilerParams(has_side_effects=DATAFLOW),
    )(*srcs, *lands, send_sems, recv_sems, after)
    return list(res[:n]), list(res[n:])


def _with_own(lands, own_blocks, me):
    out = []
    for land, own in zip(lands, own_blocks):
        out.append(lax.dynamic_update_slice(land, own[None], (me,) + (0,) * own.ndim))
    return out


def _arrange_w_in(w):
    z = lambda n: jnp.zeros((w.shape[0], n), w.dtype)
    return jnp.concatenate([w[:, 0:1280], w[:, 1292:1548], w[:, 1548:1804], w[:, 1836:2092],
                            z(64), w[:, 1804:1836], z(32), w[:, 1280:1292], z(HP - 2 * H)], axis=1)


def _unarrange_w_in(g):
    return jnp.concatenate([g[:, 0:1280], g[:, A_DT:A_DT + 2 * H], g[:, A_QA:A_KVA], g[:, A_KVA:A_POOL],
                            g[:, A_KR + NOPE:A_KR + NOPE + ROPE], g[:, A_POOL:A_KR]], axis=1)


def _pad_heads(w, width):
    k = w.shape[0]
    return jnp.pad(w.reshape(k, H, width), ((0, 0), (0, 0), (0, HP - width))).reshape(k, H * HP)


def _unpad_heads(g, width):
    k = g.shape[0]
    return g.reshape(k, H, HP)[:, :, :width].reshape(k, H * width)


def _arrange_w_out(w):
    att = jnp.pad(w[SSD_IN:2 * SSD_IN].reshape(H, VH, D), ((0, 0), (0, HP - VH), (0, 0))).reshape(QW, D)
    return jnp.concatenate([w[0:SSD_IN], att, w[2 * SSD_IN:]], axis=0)


def _unarrange_w_out(g):
    att = g[SSD_IN:SSD_IN + QW].reshape(H, HP, D)[:, :VH].reshape(SSD_IN, D)
    return jnp.concatenate([g[0:SSD_IN], att, g[SSD_IN + QW:]], axis=0)


def _rope_tables(T):
    n = T - CTX
    t = jnp.arange(n, dtype=F32)
    row, col = jnp.floor(t / GRID_W), t % GRID_W
    pairs = ROPE // 4
    inv = ROPE_THETA ** (-jnp.arange(pairs, dtype=F32) / pairs)
    ar, ac = row[:, None] * inv, col[:, None] * inv
    cos = jnp.concatenate([jnp.cos(ar)] * 2 + [jnp.cos(ac)] * 2, axis=1)
    sin = jnp.concatenate([-jnp.sin(ar), jnp.sin(ar), -jnp.sin(ac), jnp.sin(ac)], axis=1)
    ones, zeros = jnp.ones((n, NOPE), F32), jnp.zeros((n, NOPE), F32)
    cos = jnp.concatenate([ones, cos, ones[:, :HP - QK]], axis=1)
    sin = jnp.concatenate([zeros, sin, zeros[:, :HP - QK]], axis=1)
    return (jnp.concatenate([jnp.ones((CTX, HP), F32), cos], axis=0),
            jnp.concatenate([jnp.zeros((CTX, HP), F32), sin], axis=0))


def _lane_pad(v, n):
    return jnp.pad(v, (0, n - v.shape[0]))[None, :]


def _layer_weights(w_in, w_q_b, w_kv_b, conv_w, pool_w):
    kv = w_kv_b.reshape(QL, H, NOPE + VH)
    wbd = jnp.zeros((PD, PD), F32)
    for g in range(4):
        wbd = lax.dynamic_update_slice(wbd, pool_w[g], (64 * g, 64 * g))
    return dict(
        w_in=_arrange_w_in(w_in).astype(BF),
        wq=_pad_heads(w_q_b, QK).astype(BF),
        wk=_pad_heads(kv[:, :, :NOPE].reshape(QL, H * NOPE), NOPE).astype(BF),
        wv=_pad_heads(kv[:, :, NOPE:].reshape(QL, H * VH), VH).astype(BF),
        conv_w8=jnp.pad(conv_w, ((0, 4), (0, 0))), wbd=wbd)


def _layer_fwd(R, x, mod, lw, sp, late, cos, sin):
    B, T = R.B, R.T
    z, xbc_raw, qa, kva, pool_in, kr, dt_raw, h1 = _in_proj(R, x, mod, sp["nw1"], lw["w_in"])
    xbc, dt2 = _ssd_prep(R, xbc_raw, dt_raw, lw["conv_w8"], sp["conv_b"], sp["dtb"])
    yf, yb, hin_f, hin_b = _ssd_scan(B, T, xbc, dt2, sp["alog2"])
    qt, kt, q_t, k_t, v_t, cq, ckv = _mla_prep(R, qa, kva, kr, sp["qnw"], sp["kvnw"], lw["wq"], lw["wk"], lw["wv"], cos, sin)
    o, lse = _flash_fwd(B, T, q_t, kt, v_t)
    pool = _pool_fwd(R, pool_in, lw["wbd"], sp["pscale"])[0]
    w_out, w1, w2 = late(o)
    w_out = _arrange_w_out(w_out).astype(BF)
    xmid, cat, mixbf = _out_proj(R, x, mod, yf, yb, xbc, z, o, pool, sp["dsk"], sp["snw"], w_out)
    xo, h2, ubf, ybf = _mlp_fwd(R, xmid, mod, sp["nw2"], w1, w2)
    saved = dict(x=x, z=z, xbc_raw=xbc_raw, qa=qa, kva=kva, pool_in=pool_in, dt_raw=dt_raw, h1=h1, xbc=xbc, dt2=dt2,
                 yf=yf, yb=yb, hin_f=hin_f, hin_b=hin_b, qt=qt, kt=kt, q_t=q_t, k_t=k_t, v_t=v_t, cq=cq, ckv=ckv, o=o, lse=lse,
                 cat=cat, mixbf=mixbf, xmid=xmid, h2=h2, ubf=ubf, ybf=ybf, w_out=w_out, w1=w1, w2=w2)
    return xo, saved


def _layer_bwd(R, dxo, sv, mod, lw, sp, cos, sin, on_mlp=None):
    B, T = R.B, R.T
    dxm, du, abf, dyb, dmod_a, dnw2 = _mlp_bwd(R, dxo, sv["xmid"], sv["ubf"], sv["ybf"], mod, sp["nw2"], sv["w1"], sv["w2"])
    g_w1 = _tn_matmul("dw_mlp1", sv["h2"], du, FSH)
    g_w2 = _tn_matmul("dw_mlp2", abf, dyb, D)[0].reshape(NDEV, FSH, D)
    snw = sp["snw"]
    tok = on_mlp(g_w1, g_w2) if on_mlp is not None else None
    if tok is not None:
        snw = snw + tok
    dmix, dz, dyt, dxsk, dob, do_t, dd, dpo, dmod_b, dvec_o = _out_bwd(R, dxm, mod, sv["mixbf"], sv["yf"], sv["yb"], sv["xbc"], sv["z"],
                                                                 sv["o"], sp["dsk"], snw, sv["w_out"])
    g_wout = _unarrange_w_out(_tn_matmul("dw_out", sv["cat"], dmix, D)[0])
    dpool_in, g_wbd, dpsc = _pool_bwd(R, dpo, sv["pool_in"], lw["wbd"], sp["pscale"])
    dqt, dkt, dvt = _flash_bwd(B, T, sv["qt"], sv["kt"], dob, sv["q_t"], sv["k_t"], sv["v_t"], do_t, sv["lse"], dd)
    dqp, dkb, dvb, dqa, dkva, dkr, dnw_qk = _mla_bwd(R, dqt, dkt, dvt, sv["qa"], sv["kva"], sp["qnw"], sp["kvnw"],
                                                     lw["wq"], lw["wk"], lw["wv"], cos, sin)
    g_wq = _unpad_heads(_tn_matmul("dw_q", sv["cq"], dqp, QW)[0], QK)
    g_wk = _unpad_heads(_tn_matmul("dw_k", sv["ckv"], dkb, QW)[0], NOPE).reshape(QL, H, NOPE)
    g_wv = _unpad_heads(_tn_matmul("dw_v", sv["ckv"], dvb, QW)[0], VH).reshape(QL, H, VH)
    g_wkv = jnp.concatenate([g_wk, g_wv], axis=2).reshape(QL, H * (NOPE + VH))
    dxbc_f, dxbc_b, ddt_f, ddt_b, da = _ssd_scan_bwd(B, T, dyt, sv["xbc"], sv["dt2"], sp["alog2"], sv["hin_f"], sv["hin_b"])
    dpre, ddtr, dcw, dcb, ddtb = _ssd_prep_bwd(R, dxbc_f, dxbc_b, dxsk, ddt_f, ddt_b, sv["xbc_raw"], sv["dt_raw"], lw["conv_w8"],
                                                sp["conv_b"], sp["dtb"])
    dx, dproj, dmod_c, dnw1 = _in_bwd(R, dxm, sv["x"], mod, sp["nw1"], dz, dpre, dqa, dkva, dpool_in, dkr, ddtr,
                                      lw["conv_w8"], lw["w_in"])
    g_win = _unarrange_w_in(jnp.concatenate(list(_tn_matmul("dw_in", sv["h1"], dproj, PC // 3)), axis=1))
    a2 = -jnp.exp(sp["alog2"][:, 0, :H])
    small = dict(
        norm1_w=dnw1[0], norm2_w=dnw2[0], conv_w=dcw[0:4], conv_b=dcb[0], dt_bias=ddtb[0, :2 * H].reshape(2, H),
        a_log=jnp.sum(da[:, :, 0, :H], axis=0) * a2, ssd_d=jnp.sum(dvec_o[1].reshape(H, P), axis=1), ssd_norm_w=dvec_o[0],
        q_a_norm_w=dnw_qk[0], kv_a_norm_w=dnw_qk[1],
        pool_w=jnp.stack([g_wbd[64 * g:64 * (g + 1), 64 * g:64 * (g + 1)] for g in range(4)]), pool_scale=dpsc[0])
    big = dict(w_in=g_win, w_out=g_wout, w_q_b=g_wq, w_kv_b=g_wkv, w_mlp1=g_w1, w_mlp2=g_w2)
    return dx, big, small, dmod_a + dmod_b + dmod_c


def _small_params(l, norm1_w, norm2_w, conv_b, dt_bias, a_log, ssd_d, ssd_norm_w, q_a_norm_w, kv_a_norm_w, pool_scale):
    alog2 = jnp.broadcast_to(jnp.pad(a_log[l], ((0, 0), (0, HP - H)))[:, None, :], (2, 8, HP))
    return dict(nw1=norm1_w[l][None], nw2=norm2_w[l][None], conv_b=conv_b[l][None],
                dtb=_lane_pad(dt_bias[l].reshape(2 * H), HP), alog2=alog2,
                dsk=jnp.repeat(ssd_d[l], P)[None], snw=ssd_norm_w[l][None], qnw=q_a_norm_w[l][None],
                kvnw=kv_a_norm_w[l][None], pscale=pool_scale[l][None])


SMALL_NAMES = ["mod_b", "norm1_w", "norm2_w", "conv_b", "dt_bias", "a_log", "ssd_d", "ssd_norm_w", "q_a_norm_w",
               "kv_a_norm_w", "pool_w", "pool_scale", "final_norm_w"]


def _pack(arrs):
    rows = []
    for a in arrs:
        f = a.reshape(-1).astype(F32)
        n = -(-f.shape[0] // HP) * HP
        rows.append(jnp.pad(f, (0, n - f.shape[0])).reshape(-1, HP))
    out = jnp.concatenate(rows, axis=0)
    pad = (-out.shape[0]) % 8
    return jnp.pad(out, ((0, pad), (0, 0)))


def _unpack(pack, like):
    outs, r = [], 0
    for a in like:
        n = math.prod(a.shape)
        nr = -(-n // HP)
        outs.append(pack[r:r + nr].reshape(-1)[:n].reshape(a.shape))
        r += nr
    return outs


def _local_step(x, ctx, target, mods, full_of, small_w, on_grads=None, on_mlp=None):
    B, N = x.shape[0], x.shape[1]
    T = CTX + N
    R = _Rows(B, T)
    cos, sin = _rope_tables(T)
    xu = jnp.concatenate([ctx, x], axis=1)
    L = len(mods)
    lws, sps, saves = [], [], []
    for l in range(L):
        f = full_of(l, xu)
        lws.append(_layer_weights(f["w_in"], f["w_q_b"], f["w_kv_b"], f["conv_w"], small_w["pool_w"][l]))
        sps.append(_small_params(l, *[small_w[k] for k in ["norm1_w", "norm2_w", "conv_b", "dt_bias", "a_log", "ssd_d",
                                                          "ssd_norm_w", "q_a_norm_w", "kv_a_norm_w", "pool_scale"]]))
        xu, sv = _layer_fwd(R, xu, mods[l], lws[l], sps[l], f["late"], cos, sin)
        saves.append(sv)
    dx, loss8, dfnw = _loss_head(R, xu, target, small_w["final_norm_w"][None])
    bigs, smalls, dmods = [None] * L, [None] * L, [None] * L
    for l in reversed(range(L)):
        hook = functools.partial(on_mlp, l) if on_mlp is not None else None
        dx, bigs[l], smalls[l], dmods[l] = _layer_bwd(R, dx, saves[l], mods[l], lws[l], sps[l], cos, sin, hook)
        if on_grads is not None:
            bigs[l], tok = on_grads(l, bigs[l], dx)
            if tok is not None:
                sps[l - 1] = dict(sps[l - 1], nw2=sps[l - 1]["nw2"] + tok)
    return loss8[0, 0], dx[:, CTX:], bigs, smalls, dfnw[0], dmods


def kernel(x, c, ctx, c_ctx, mod_w, mod_b, norm1_w, norm2_w, w_in, conv_w, conv_b, dt_bias, a_log, ssd_d, ssd_norm_w, q_a_norm_w, w_q_b, kv_a_norm_w, w_kv_b, pool_w, pool_scale, w_out, w_mlp1, w_mlp2, final_norm_w, loss_target, m_c_ctx, m_mod_w, m_mod_b, m_norm1_w, m_norm2_w, m_w_in, m_conv_w, m_conv_b, m_dt_bias, m_a_log, m_ssd_d, m_ssd_norm_w, m_q_a_norm_w, m_w_q_b, m_kv_a_norm_w, m_w_kv_b, m_pool_w, m_pool_scale, m_w_out, m_w_mlp1, m_w_mlp2, m_final_norm_w, v_c_ctx, v_mod_w, v_mod_b, v_norm1_w, v_norm2_w, v_w_in, v_conv_w, v_conv_b, v_dt_bias, v_a_log, v_ssd_d, v_ssd_norm_w, v_q_a_norm_w, v_w_q_b, v_kv_a_norm_w, v_w_kv_b, v_pool_w, v_pool_scale, v_w_out, v_w_mlp1, v_w_mlp2, v_final_norm_w):
    args = dict(locals())
    B = x.shape[0]
    L = mod_w.shape[0]
    me = _dev_index(*_my_pos())
    CS = mod_w.shape[2]

    big_names = ["w_in", "w_out", "w_q_b", "w_kv_b", "w_mlp1", "w_mlp2"]
    shards = {n: args[n].astype(BF) for n in big_names}
    early, late_names = ["w_in", "w_q_b", "w_kv_b"], ["w_out", "w_mlp1", "w_mlp2"]
    g0 = _all_gather("gather_weights", [c, conv_w] + [shards[n][0] for n in early])
    c_all, convw_all = g0[0], g0[1]
    gathered = {0: dict(zip(early, g0[2:]))}
    cs = jnp.concatenate([c_all.reshape(NDEV * B, D), c_ctx[None], jnp.zeros((24 - NDEV * B - 1, D), F32)], axis=0)
    m_loc = _adaln_fwd(cs, mod_w)
    m_all = _all_gather("gather_mod", [m_loc])[0]
    m_full = jnp.moveaxis(m_all, 0, 2).reshape(L, 24, NDEV * CS) + mod_b[:, None, :]
    pending = {}
    tok = jnp.zeros((), F32)
    res = _exchange_start("gather_start_0", 1, [shards[n][0] for n in late_names], gather=True)
    pending[0] = res[:4]
    tok = tok + res[4][0, 0]
    for l in range(1, L):
        res = _exchange_start("gather_start_%d" % l, 1 + l, [shards[n][l] for n in early + late_names], gather=True)
        pending[l] = res[:4]
        tok = tok + res[4][0, 0]
    mods = []
    for l in range(L):
        ex = lax.dynamic_slice(m_full[l], (me * B, 0), (B, 6 * D)).reshape(B, 6, D)
        cc = jnp.broadcast_to(m_full[l, NDEV * B].reshape(1, 6, D), (B, 6, D))
        mods.append(jnp.pad(jnp.stack([cc, ex], axis=1), ((0, 0), (0, 0), (0, 2), (0, 0))) + tok)

    def full_of(l, xu):
        if l > 0:
            own, lands = _exchange_wait("gather_wait_%d" % l, *pending.pop(l), xu, gather=True)
            gathered[l] = dict(zip(early + late_names, _with_own(lands, own, me)))

        def late(after):
            if l == 0:
                own, lands = _exchange_wait("gather_wait_0", *pending.pop(0), after, gather=True)
                gathered[0].update(zip(late_names, _with_own(lands, own, me)))
            gl = gathered[l]
            return gl["w_out"].reshape(D, D), gl["w_mlp1"], gl["w_mlp2"]

        g = gathered[l]
        return dict(
            w_in=g["w_in"].reshape(D, IN_COLS),
            w_q_b=jnp.moveaxis(g["w_q_b"], 0, 1).reshape(QL, H * QK),
            w_kv_b=jnp.moveaxis(g["w_kv_b"], 0, 1).reshape(QL, H * (NOPE + VH)),
            conv_w=jnp.moveaxis(convw_all[:, l], 0, 1).reshape(4, XBC), late=late)

    def grad_blocks(big, names):
        make = dict(
            w_in=lambda g: g.reshape(NDEV, D // NDEV, IN_COLS), w_out=lambda g: g.reshape(NDEV, D // NDEV, D),
            w_q_b=lambda g: jnp.moveaxis(g.reshape(QL, NDEV, -1), 1, 0),
            w_kv_b=lambda g: jnp.moveaxis(g.reshape(QL, NDEV, -1), 1, 0), w_mlp1=lambda g: g, w_mlp2=lambda g: g)
        return [make[n](big[n]).astype(BF) for n in names]

    sent, sent_mlp = {}, {}
    rest_names = ["w_in", "w_out", "w_q_b", "w_kv_b"]

    def on_mlp(l, g_w1, g_w2):
        if l > 0:
            return None
        res = _exchange_start("grads_mlp_start_0", 1 + 2 * L, [g_w1.astype(BF), g_w2.astype(BF)], gather=False)
        sent_mlp[l] = res[:4]
        return res[4][0, 0]

    def on_grads(l, big, dx):
        if l == 0:
            return grad_blocks(big, rest_names), None
        res = _exchange_start("grads_start_%d" % l, 1 + L + l, grad_blocks(big, big_names), gather=False)
        sent[l] = res[:4]
        return None, res[4][0, 0]

    small_w = {k: args[k] for k in SMALL_NAMES if k != "mod_b"}
    loss_part, grad_x, blocks, smalls, dfnw, dmods = _local_step(x, ctx, loss_target, mods, full_of, small_w, on_grads, on_mlp)
    loss = lax.psum(loss_part, ("x", "y", "c"))

    dm_ex = jnp.stack([dmods[l][:, 1, :6].reshape(B, 6 * D) for l in range(L)])
    dm_cc = jnp.stack([jnp.sum(dmods[l][:, 0, :6], axis=0).reshape(6 * D) for l in range(L)])
    small_parts = dict(
        mod_b=jnp.sum(dm_ex, axis=1) + dm_cc,
        **{k: jnp.stack([smalls[l][k] for l in range(L)]) for k in SMALL_NAMES[1:-1]},
        final_norm_w=dfnw, conv_w=jnp.stack([smalls[l]["conv_w"] for l in range(L)]), dm_cc=dm_cc)
    adam_grads = [small_parts[k] for k in SMALL_NAMES]
    extras = [small_parts["conv_w"], dm_cc, dm_ex]
    pack = jnp.concatenate([_pack(adam_grads), _pack(extras)], axis=0)
    pack_all = _all_gather("gather_small_grads", [pack])[0]
    wpack = _pack([args[k] for k in SMALL_NAMES])
    mpack = _pack([args["m_" + k] for k in SMALL_NAMES])
    vpack = _pack([args["v_" + k] for k in SMALL_NAMES])
    n_adam = wpack.shape[0]
    res_small = _adamw("adamw_small", pack_all[:, :n_adam], wpack, mpack, vpack, n_adam)
    small_out = [_unpack(r, [args[k] for k in SMALL_NAMES]) for r in res_small]
    ext_all = pack_all[:, n_adam:]
    ext_sum = ext_all[0]
    for k in range(1, NDEV):
        ext_sum = ext_sum + ext_all[k]
    g_conv_full, dm_cc_tot, _ = _unpack(ext_sum, extras)
    dm_ex_all = jnp.stack([_unpack(ext_all[k], extras)[2] for k in range(NDEV)], axis=1)
    dm_rows = jnp.concatenate([dm_ex_all.reshape(L, NDEV * B, 6 * D), dm_cc_tot[:, None, :],
                               jnp.zeros((L, 24 - NDEV * B - 1, 6 * D), F32)], axis=1)
    dm_loc = lax.dynamic_slice(dm_rows, (0, 0, me * CS), (L, 24, CS))
    g_modw, gc_part = _adaln_bwd(cs, dm_loc, mod_w)
    gc_all = _all_gather("gather_cctx_grad", [gc_part[NDEV * B:NDEV * B + 8]])[0]
    cpad = lambda a: jnp.pad(a[None], ((0, 7), (0, 0)))
    res_cc = _adamw("adamw_cctx", gc_all, cpad(c_ctx), cpad(m_c_ctx), cpad(v_c_ctx), 8)
    cc_out = [r[0] for r in res_cc]

    def waited(name, handles, names):
        srcs, lands = _exchange_wait(name, *handles, grad_x, gather=False)
        own = [lax.dynamic_index_in_dim(s, me, 0, keepdims=False) for s in srcs]
        return dict(zip(names, _with_own(lands, own, me)))

    recv = {0: dict(zip(rest_names, _all_to_all("exchange_grads", blocks[0])))}
    recv[0].update(waited("grads_mlp_wait_0", sent_mlp.pop(0), ["w_mlp1", "w_mlp2"]))
    for l in range(1, L):
        recv[l] = waited("grads_wait_%d" % l, sent.pop(l), big_names)
    recvs = [jnp.stack([recv[l][n] for l in range(L)], axis=1) for n in big_names]
    big_out = {}
    for name, rv in zip(big_names, recvs):
        w = args[name]
        Rr, C = math.prod(w.shape[:-1]), w.shape[-1]
        rt = Rr if Rr * C <= (1 << 18) else Rr // 8
        res = _adamw("adamw_" + name, rv.reshape(NDEV, Rr, C), w.reshape(Rr, C), args["m_" + name].reshape(Rr, C),
                     args["v_" + name].reshape(Rr, C), rt)
        big_out[name] = [r.reshape(w.shape) for r in res]
    res = _adamw("adamw_mod_w", g_modw.reshape(1, L * D, CS), mod_w.reshape(L * D, CS), m_mod_w.reshape(L * D, CS),
                 v_mod_w.reshape(L * D, CS), L * D // 8)
    big_out["mod_w"] = [r.reshape(mod_w.shape) for r in res]
    CW = conv_w.shape[2]
    g_conv = lax.dynamic_slice(g_conv_full, (0, 0, me * CW), (L, 4, CW))
    res = _adamw("adamw_conv_w", g_conv.reshape(1, L * 4, CW), conv_w.reshape(L * 4, CW), m_conv_w.reshape(L * 4, CW),
                 v_conv_w.reshape(L * 4, CW), L * 4)
    big_out["conv_w"] = [r.reshape(conv_w.shape) for r in res]

    weights = ["c_ctx", "mod_w", "mod_b", "norm1_w", "norm2_w", "w_in", "conv_w", "conv_b", "dt_bias", "a_log", "ssd_d",
               "ssd_norm_w", "q_a_norm_w", "w_q_b", "kv_a_norm_w", "w_kv_b", "pool_w", "pool_scale", "w_out", "w_mlp1",
               "w_mlp2", "final_norm_w"]
    outs = [loss, grad_x]
    for kind in range(4):
        for name in weights:
            if name == "c_ctx":
                outs.append(cc_out[kind])
            elif name in big_out:
                outs.append(big_out[name][kind])
            else:
                outs.append(small_out[kind][SMALL_NAMES.index(name)])
    return tuple(outs)
```

```python
import functools
import math

import jax
import jax.numpy as jnp
from jax import lax
from jax.experimental import pallas as pl
from jax.experimental.pallas import tpu as pltpu

F32 = jnp.float32
BF = jnp.bfloat16
MXU = BF

D = 1024
CTX = 256
GRID_W = 64
EPS = 1e-6
H = 6
P = 64
SSD_IN = 384
NST = 128
XBC = 896
CH = 128
QL = 256
NOPE = 64
ROPE = 32
VH = 64
QK = 96
HP = 128
QW = H * HP
PD = 256
FF = 4096
IN_COLS = 2092
ROPE_THETA = 10000.0
PC = 2304
A_Z, A_XBC, A_QA, A_KVA, A_POOL, A_KR, A_DT = 0, 384, 1280, 1536, 1792, 2048, 2176
MIXW = SSD_IN + QW + PD
NDEV = 8
FSH = FF // NDEV
TM = 256
TQ = 256
TK = 256
VMEM_CAP = 64 * 1024 * 1024
ADAM_LR, ADAM_B1, ADAM_B2, ADAM_EPS, ADAM_WD, ADAM_STEP = 0.001, 0.9, 0.999, 1e-08, 0.01, 10


def _nbytes(shape, dtype):
    n = 1
    for s in shape:
        if s is not None:
            n *= s
    return n * jnp.dtype(dtype).itemsize


def _params(sem, block_bytes, extra=0):
    lim = min(2 * block_bytes + extra + (8 << 20), VMEM_CAP - (6 << 20))
    return pltpu.CompilerParams(dimension_semantics=sem, vmem_limit_bytes=int(lim))


def _hbm(arrays):
    return [pltpu.with_memory_space_constraint(a, pltpu.HBM) for a in arrays]


def _dot(a, b):
    return jnp.dot(a.astype(MXU), b.astype(MXU), preferred_element_type=F32)


def _dot_nt(a, b):
    return lax.dot_general(a.astype(MXU), b.astype(MXU), (((1,), (1,)), ((), ())), preferred_element_type=F32)


def _dot_tn(a, b):
    return lax.dot_general(a.astype(MXU), b.astype(MXU), (((0,), (0,)), ((), ())), preferred_element_type=F32)


def _dot01(m01, x):
    b16 = jnp.bfloat16
    m = m01.astype(b16)
    hi = x.astype(b16)
    r1 = x - hi.astype(F32)
    mid = r1.astype(b16)
    lo = (r1 - mid.astype(F32)).astype(b16)
    f = lambda v: jnp.dot(m, v, preferred_element_type=F32)
    return f(hi) + f(mid) + f(lo)


def _sigmoid(x):
    return 1.0 / (1.0 + jnp.exp(-x))


def _silu(x):
    return x * _sigmoid(x)


def _dsilu(x):
    s = _sigmoid(x)
    return s * (1.0 + x * (1.0 - s))


def _iota(shape, dim):
    return lax.broadcasted_iota(jnp.int32, shape, dim)


def _row(ref, k):
    blk = ref[...]
    return jnp.sum(jnp.where(_iota(blk.shape, 0) == k, blk, 0.0), axis=0, keepdims=True)


def _shift_rows(x, k):
    n = x.shape[0]
    return pltpu.roll(x, (-k) % n, axis=0)


class _Rows:
    def __init__(self, B, T):
        self.B, self.T = B, T
        self.nt = T // TM
        self.nct = CTX // TM

    def row(self, F):
        return pl.BlockSpec((None, TM, F), lambda b, i: (b, i, 0))

    def row2(self, F):
        return pl.BlockSpec((None, 2, TM, F), lambda b, i: (b, 0, i, 0))

    def prev8(self, F):
        return pl.BlockSpec((None, 8, F), lambda b, i: (b, jnp.maximum(i * (TM // 8) - 1, 0), 0))

    def next8(self, F):
        last = self.T // 8 - 1
        return pl.BlockSpec((None, 8, F), lambda b, i: (b, jnp.minimum((i + 1) * (TM // 8), last), 0))

    def mod(self):
        nct = self.nct
        return pl.BlockSpec((None, None, 8, D), lambda b, i: (b, jnp.where(i < nct, 0, 1), 0, 0))

    def const(self, shape):
        z = (0,) * len(shape)
        return pl.BlockSpec(tuple(shape), lambda b, i: z)

    def tgt(self, F):
        nct = self.nct
        return pl.BlockSpec((None, TM, F), lambda b, i: (b, jnp.maximum(i - nct, 0), 0))

    def call(self, name, body, ins, outs, scratch=(), extra=0):
        arrays = [a for a, _ in ins]
        in_specs = [s for _, s in ins]
        out_shape = [pltpu.HBM(o.shape, o.dtype) for o, _ in outs]
        out_specs = [s for _, s in outs]
        bb = 0
        for a, s in list(ins) + list(outs):
            if s.block_shape is not None:
                bb += _nbytes(s.block_shape, a.dtype)
        return pl.pallas_call(
            functools.partial(body), name=name, grid=(self.B, self.nt),
            in_specs=in_specs, out_specs=out_specs, out_shape=out_shape, scratch_shapes=list(scratch),
            compiler_params=_params(("arbitrary", "arbitrary"), bb, extra),
        )(*_hbm(arrays))

    def first_of_stream(self, i):
        return jnp.logical_or(i == 0, i == self.nct)

    def last_of_stream(self, i):
        return jnp.logical_or(i == self.nct - 1, i == self.nt - 1)

    def ext(self, i, prev_ref, cur, next_ref):
        pv = prev_ref[...].astype(F32) * jnp.where(self.first_of_stream(i), 0.0, 1.0)
        nx = next_ref[...].astype(F32) * jnp.where(self.last_of_stream(i), 0.0, 1.0)
        return jnp.concatenate([pv, cur, nx], axis=0)

    def stream_pos(self, i, rows):
        start = jnp.where(i < self.nct, 0, CTX)
        n = jnp.where(i < self.nct, CTX, self.T - CTX)
        t = i * TM - 8 - start + _iota((rows, 1), 0)
        return t, n


def _sds(shape, dtype):
    return jax.ShapeDtypeStruct(tuple(shape), dtype)


def _out(shape, dtype):
    return pltpu.HBM(tuple(shape), dtype)


def _norm_mod(x, nw, sh, sc):
    r = lax.rsqrt(jnp.mean(x * x, axis=-1, keepdims=True) + EPS)
    xn = x * r
    return xn * nw * (1.0 + sc) + sh, xn, r


def _norm_mod_bwd(dh, xn, r, nw, sc):
    dsh = jnp.sum(dh, axis=0, keepdims=True)
    dsc = jnp.sum(dh * (xn * nw), axis=0, keepdims=True)
    dnw = jnp.sum(dh * (1.0 + sc) * xn, axis=0, keepdims=True)
    dxn = dh * nw * (1.0 + sc)
    dx = r * (dxn - xn * jnp.mean(dxn * xn, axis=-1, keepdims=True))
    return dx, dsh, dsc, dnw


def _acc_rows(ref, first, rows):
    rid = _iota(ref.shape, 0)
    upd = jnp.zeros(ref.shape, F32)
    for k, v in rows.items():
        upd = upd + jnp.where(rid == k, v, 0.0)

    @pl.when(first)
    def _():
        ref[...] = upd

    @pl.when(jnp.logical_not(first))
    def _():
        ref[...] += upd


def _in_proj(R, x, mod, nw1, w_arr):
    B, T = R.B, R.T

    def body(x_ref, mod_ref, nw_ref, w_ref, z_ref, xbc_ref, qa_ref, kva_ref, pool_ref, kr_ref, dt_ref, h_ref):
        h, _, _ = _norm_mod(x_ref[...], nw_ref[...], _row(mod_ref, 0), _row(mod_ref, 1))
        hb = h.astype(BF)
        h_ref[...] = hb
        p = jnp.dot(hb, w_ref[...], preferred_element_type=F32)
        z_ref[...] = p[:, A_Z:A_XBC]
        xbc_ref[...] = p[:, A_XBC:A_QA]
        qa_ref[...] = p[:, A_QA:A_KVA]
        kva_ref[...] = p[:, A_KVA:A_POOL]
        pool_ref[...] = p[:, A_POOL:A_KR]
        kr_ref[...] = p[:, A_KR:A_DT]
        dt_ref[...] = p[:, A_DT:PC]

    widths = [SSD_IN, XBC, QL, QL, PD, HP, HP]
    outs = [(_sds((B, T, w), F32), R.row(w)) for w in widths] + [(_sds((B, T, D), BF), R.row(D))]
    return R.call("in_proj", body,
                  [(x, R.row(D)), (mod, R.mod()), (nw1, R.const((1, D))), (w_arr, R.const((D, PC)))],
                  outs, extra=8 << 20)


def _conv_pre(ext, w_ref, b_ref):
    return (_row(w_ref, 0) * _shift_rows(ext, -1)[8:8 + TM] + _row(w_ref, 1) * ext[8:8 + TM]
            + _row(w_ref, 2) * _shift_rows(ext, 1)[8:8 + TM] + _row(w_ref, 3) * _shift_rows(ext, 2)[8:8 + TM]
            + b_ref[...])


def _softplus(x):
    return jnp.maximum(x, 0.0) + jnp.log(1.0 + jnp.exp(-jnp.abs(x)))


def _ssd_prep(R, xbc_raw, dt_raw, conv_w8, conv_b, dtb):
    B, T = R.B, R.T

    def body(raw_ref, pv_ref, nx_ref, dtr_ref, w_ref, b_ref, dtb_ref, xbc_ref, dt_ref):
        i = pl.program_id(1)
        ext = R.ext(i, pv_ref, raw_ref[...], nx_ref)
        xbc_ref[...] = _silu(_conv_pre(ext, w_ref, b_ref))
        lane = _iota((TM, HP), 1)
        dtv = _softplus(dtr_ref[...] + dtb_ref[...])
        keep = lane < H
        dt_ref[0] = jnp.where(keep, dtv, 0.0)
        dt_ref[1] = jnp.where(keep, pltpu.roll(dtv, HP - H, axis=1), 0.0)

    return R.call("ssd_prep", body,
                  [(xbc_raw, R.row(XBC)), (xbc_raw, R.prev8(XBC)), (xbc_raw, R.next8(XBC)), (dt_raw, R.row(HP)),
                   (conv_w8, R.const((8, XBC))), (conv_b, R.const((1, XBC))), (dtb, R.const((1, HP)))],
                  [(_sds((B, T, XBC), F32), R.row(XBC)), (_sds((B, 2, T, HP), F32), R.row2(HP))], extra=12 << 20)


def _chunk_index(d, s, nc, nctc):
    if d == 0:
        return s
    return jnp.where(s < nctc, nctc - 1 - s, nc - 1 - (s - nctc))


def _dot01_r(x, m01, passes=3):
    b16 = jnp.bfloat16
    m = m01.astype(b16)
    out, rest = None, x
    for _ in range(passes):
        part = rest.astype(b16)
        rest = rest - part.astype(F32)
        term = jnp.dot(part, m, preferred_element_type=F32)
        out = term if out is None else out + term
    return out


def _scan_common(d, dtv, alog_ref):
    sel = _iota((HP, H * HP), 0) == (_iota((HP, H * HP), 1) >> 7)
    a_all = _dot01_r(-jnp.exp(alog_ref[d]), sel)
    dt_all = _dot01_r(dtv, sel)
    adt_all = dt_all * jnp.concatenate([a_all] * (CH // 8), axis=0)
    row = _iota((CH, CH), 0)
    col = _iota((CH, CH), 1)
    inc = col <= row if d == 0 else col >= row
    inc_t = row <= col if d == 0 else row >= col
    q_all = _dot01(inc, adt_all)
    return a_all, dt_all, adt_all, q_all, inc, inc_t


def _head_decay(h, adt_all, q_all, inc, inc_t):
    q = q_all[:, HP * h:HP * (h + 1)]
    q_t = q.T
    qtot = jnp.sum(adt_all[:, HP * h:HP * (h + 1)], axis=0, keepdims=True)
    lm = jnp.where(inc, jnp.exp(q - q_t), 0.0)
    lm_t = jnp.where(inc_t, jnp.exp(q_t - q), 0.0)
    return lm, lm_t, jnp.exp(q), jnp.exp(qtot), jnp.exp(qtot - q)


def _ssd_scan(B, T, xbc, dt2, alog2):
    nc, nctc = T // CH, CTX // CH

    def chain(d, xbc_ref, dt_ref, alog_ref, y_ref, hin_ref, hs):
        xbc_v = xbc_ref[...]
        dtv = dt_ref[...]
        blk = alog_ref[d]
        adt = dtv * -jnp.exp(jnp.sum(jnp.where(_iota(blk.shape, 0) == 0, blk, 0.0), axis=0, keepdims=True))
        row, col = _iota((CH, CH), 0), _iota((CH, CH), 1)
        inc = col <= row if d == 0 else col >= row
        inc_tf = (row <= col if d == 0 else row >= col).astype(F32)
        q = _dot01(inc, adt)
        hin_ref[...] = hs[d]
        for g in range(2):
            bg = xbc_v[:, SSD_IN + NST * g:SSD_IN + NST * (g + 1)]
            cg = xbc_v[:, SSD_IN + 2 * NST + NST * g:SSD_IN + 2 * NST + NST * (g + 1)]
            gm = _dot_nt(cg, bg)
            for r in range(3):
                h = 3 * g + r
                onehot = (_iota((1, HP), 1) == h).astype(F32)
                adt_h = jnp.sum(adt * onehot, axis=1, keepdims=True)
                qc = jnp.sum(q * onehot, axis=1, keepdims=True)
                dt_h = jnp.sum(dtv * onehot, axis=1, keepdims=True)
                qr = jnp.sum(adt_h * inc_tf, axis=0, keepdims=True)
                qtot = jnp.sum(adt_h, axis=0, keepdims=True)
                lm = jnp.where(inc, jnp.exp(qc - qr), 0.0)
                xh = xbc_v[:, P * h:P * (h + 1)] * dt_h
                hprev = hs[d, P * h:P * (h + 1), :]
                y_ref[:, P * h:P * (h + 1)] = _dot(gm * lm, xh) + jnp.exp(qc) * _dot_nt(cg, hprev)
                hs[d, P * h:P * (h + 1), :] = jnp.exp(qtot) * hprev + _dot_tn(xh * jnp.exp(qtot - qc), bg)

    def body(xf_ref, xb_ref, dtf_ref, dtb_ref, alog_ref, yf_ref, yb_ref, hf_ref, hb_ref, hs):
        @pl.when(pl.program_id(1) == 0)
        def _():
            hs[...] = jnp.zeros(hs.shape, F32)

        chain(0, xf_ref, dtf_ref, alog_ref, yf_ref, hf_ref, hs)
        chain(1, xb_ref, dtb_ref, alog_ref, yb_ref, hb_ref, hs)

    cidx = lambda d: (lambda s: _chunk_index(d, s, nc, nctc))
    specs = lambda d: dict(
        xbc=pl.BlockSpec((None, CH, XBC), lambda b, s: (b, cidx(d)(s), 0)),
        dt=pl.BlockSpec((None, None, CH, HP), lambda b, s: (b, d, cidx(d)(s), 0)),
        y=pl.BlockSpec((None, CH, SSD_IN), lambda b, s: (b, cidx(d)(s), 0)),
        h=pl.BlockSpec((None, None, SSD_IN, NST), lambda b, s: (b, cidx(d)(s), 0, 0)))
    f, r = specs(0), specs(1)
    bb = 2 * (_nbytes((CH, XBC), F32) + _nbytes((CH, HP), F32) + _nbytes((CH, SSD_IN), F32) + _nbytes((SSD_IN, NST), F32))
    return pl.pallas_call(
        functools.partial(body), name="ssd_scan", grid=(B, nc),
        in_specs=[f["xbc"], r["xbc"], f["dt"], r["dt"], pl.BlockSpec((2, 8, HP), lambda b, s: (0, 0, 0))],
        out_specs=[f["y"], r["y"], f["h"], r["h"]],
        out_shape=[_out((B, T, SSD_IN), F32)] * 2 + [_out((B, nc, SSD_IN, NST), F32)] * 2,
        scratch_shapes=[pltpu.VMEM((2, SSD_IN, NST), F32)],
        compiler_params=_params(("arbitrary",) * 2, bb, 12 << 20),
    )(*_hbm([xbc, xbc, dt2, dt2, alog2]))


def _swap8(u):
    lane = _iota(u.shape, 1)
    n = u.shape[1]
    return jnp.where((lane & 15) < 8, pltpu.roll(u, n - 8, axis=1), pltpu.roll(u, 8, axis=1))


def _rope(u, cos, sin_signed):
    return u * cos + _swap8(u) * sin_signed


def _rms(x, w):
    r = lax.rsqrt(jnp.mean(x * x, axis=-1, keepdims=True) + EPS)
    xh = x * r
    return xh * w, xh, r


def _rms_bwd(dy, xh, r, w):
    dw = jnp.sum(dy * xh, axis=0, keepdims=True)
    dxh = dy * w
    return r * (dxh - xh * jnp.mean(dxh * xh, axis=-1, keepdims=True)), dw


def _tile6(t):
    return jnp.concatenate([t] * H, axis=1)


def _per_head(fn, u):
    return jnp.concatenate([fn(u[:, HP * h:HP * (h + 1)]) for h in range(H)], axis=1)


def _mla_prep(R, qa, kva, kr, qnw, kvnw, wq, wk, wv, cos, sin):
    B, T = R.B, R.T
    scale = QK ** -0.5

    def body(qa_ref, kva_ref, kr_ref, qnw_ref, kvnw_ref, wq_ref, wk_ref, wv_ref, cos_ref, sin_ref,
             qt_ref, kt_ref, qtr_ref, ktr_ref, vtr_ref, cq_ref, ckv_ref):
        cq, _, _ = _rms(qa_ref[...], qnw_ref[...])
        ckv, _, _ = _rms(kva_ref[...], kvnw_ref[...])
        cqb, ckvb = cq.astype(BF), ckv.astype(BF)
        cq_ref[...] = cqb
        ckv_ref[...] = ckvb
        cos1, sin1 = cos_ref[...], sin_ref[...]
        q = _per_head(lambda u: _rope(u, cos1, sin1), jnp.dot(cqb, wq_ref[...], preferred_element_type=F32)) * scale
        qt_ref[...] = q.astype(BF)
        kk = _rope(kr_ref[...], cos1, sin1)
        k = jnp.dot(ckvb, wk_ref[...], preferred_element_type=F32) + _tile6(kk)
        kt_ref[...] = k.astype(BF)
        v = jnp.dot(ckvb, wv_ref[...], preferred_element_type=F32)
        v = jnp.where((_iota((TM, QW), 1) & (HP - 1)) == VH, 1.0, v)
        for h in range(H):
            cols = slice(HP * h, HP * (h + 1))
            qtr_ref[h] = q[:, cols].T.astype(BF)
            ktr_ref[h] = k[:, cols].T.astype(BF)
            vtr_ref[h] = v[:, cols].T.astype(BF)

    tr = (_sds((B, H, HP, T), BF), pl.BlockSpec((None, H, HP, TM), lambda b, i: (b, 0, 0, i)))
    return R.call("mla_prep", body,
                  [(qa, R.row(QL)), (kva, R.row(QL)), (kr, R.row(HP)), (qnw, R.const((1, QL))), (kvnw, R.const((1, QL))),
                   (wq, R.const((QL, QW))), (wk, R.const((QL, QW))), (wv, R.const((QL, QW))),
                   (cos, pl.BlockSpec((TM, HP), lambda b, i: (i, 0))), (sin, pl.BlockSpec((TM, HP), lambda b, i: (i, 0)))],
                  [(_sds((B, T, QW), BF), R.row(QW))] * 2 + [tr] * 3 + [(_sds((B, T, QL), BF), R.row(QL))] * 2, extra=12 << 20)


def _flash_fwd(B, T, q_t, kt, v_t):
    nq, nk = T // TQ, T // TK
    HS = 2

    def body(q_ref, k_ref, v_ref, o_ref, lse_ref, s_scr):
        i = pl.program_id(2)

        def attend(nch):
            ms = []
            for hh in range(HS):
                q_tr = q_ref[hh]
                mrun = None
                for j in range(nch):
                    s = _dot(k_ref[TK * j:TK * (j + 1), HP * hh:HP * (hh + 1)], q_tr)
                    s_scr[hh, j] = s
                    mrun = s if mrun is None else jnp.maximum(mrun, s)
                ms.append(jnp.max(mrun, axis=0, keepdims=True))
            row = _iota((HP, TQ), 0)
            for hh in range(HS):
                acc_t = jnp.zeros((HP, TQ), F32)
                for j in range(nch):
                    acc_t = acc_t + _dot(v_ref[hh, :, TK * j:TK * (j + 1)], jnp.exp(s_scr[hh, j] - ms[hh]))
                l = jnp.sum(jnp.where(row == VH, acc_t, 0.0), axis=0, keepdims=True)
                o_ref[:, HP * hh:HP * (hh + 1)] = jnp.where(row < VH, acc_t / l, 0.0).T
                lse_ref[hh] = jnp.broadcast_to(ms[hh] + jnp.log(l), (HP, TQ)).T

        @pl.when(i < CTX // TQ)
        def _():
            attend(CTX // TK)

        @pl.when(i >= CTX // TQ)
        def _():
            attend(nk)

    bb = HS * (_nbytes((TQ, HP), BF) + 2 * _nbytes((T, HP), BF) + 2 * _nbytes((TQ, HP), F32))
    return pl.pallas_call(
        functools.partial(body), name="flash_fwd", grid=(B, H // HS, nq),
        in_specs=[pl.BlockSpec((None, HS, HP, TQ), lambda b, h, i: (b, h, 0, i)),
                  pl.BlockSpec((None, T, HS * HP), lambda b, h, i: (b, 0, h)),
                  pl.BlockSpec((None, HS, HP, T), lambda b, h, i: (b, h, 0, 0))],
        out_specs=[pl.BlockSpec((None, TQ, HS * HP), lambda b, h, i: (b, i, h)),
                   pl.BlockSpec((None, HS, TQ, HP), lambda b, h, i: (b, h, i, 0))],
        out_shape=[_out((B, T, QW), F32), _out((B, H, T, HP), F32)],
        scratch_shapes=[pltpu.VMEM((HS, nk, TK, TQ), F32)],
        compiler_params=_params(("arbitrary",) * 3, bb, _nbytes((HS, nk, TK, TQ), F32) + (8 << 20)),
    )(*_hbm([q_t, kt, v_t]))


def _pool_terms(R, i, rows):
    t, n = R.stream_pos(i, rows)
    lane = _iota((1, PD), 1)
    half = jnp.where(lane < 64, 1, jnp.where(lane < 128, 2, jnp.where(lane < 192, 4, 8)))
    cnt = (jnp.minimum(t + half, n) - jnp.maximum(t - half, 0)).astype(F32)
    valid = jnp.logical_and(t >= 0, t < n)
    return jnp.where(valid, cnt, 1.0), valid.astype(F32), lane


def _lane_select(lane, a2, a4, a8, a16):
    return jnp.where(lane < 64, a2, jnp.where(lane < 128, a4, jnp.where(lane < 192, a8, a16)))


def _pool_centred(R, i, ext):
    cnt, valid, lane = _pool_terms(R, i, ext.shape[0])
    s2 = ext + _shift_rows(ext, -1)
    s4 = _shift_rows(s2, -1) + _shift_rows(s2, 1)
    s8 = _shift_rows(s4, -2) + _shift_rows(s4, 2)
    s16 = _shift_rows(s8, -4) + _shift_rows(s8, 4)
    return _lane_select(lane, s2, s4, s8, s16) / cnt - ext, cnt, valid, lane


def _pool_fwd(R, u, wbd, scale):
    B, T = R.B, R.T

    def body(u_ref, pv_ref, nx_ref, w_ref, sc_ref, o_ref):
        i = pl.program_id(1)
        ext = R.ext(i, pv_ref, u_ref[...], nx_ref)
        dm, _, _, _ = _pool_centred(R, i, ext)
        o_ref[...] = _dot(dm[8:8 + TM], w_ref[...]) * sc_ref[...]

    return R.call("pool_fwd", body,
                  [(u, R.row(PD)), (u, R.prev8(PD)), (u, R.next8(PD)), (wbd, R.const((PD, PD))), (scale, R.const((1, PD)))],
                  [(_sds((B, T, PD), F32), R.row(PD))], extra=8 << 20)


def _group_mask():
    return _iota((1, SSD_IN), 1) < SSD_IN // 2


def _ssd_gate(yf_ref, yb_ref, xbc_ref, z_ref, dsk_ref):
    ytot = yf_ref[...] + yb_ref[...] + xbc_ref[:, 0:SSD_IN] * dsk_ref[...]
    z = z_ref[...]
    gz = ytot * _silu(z)
    g0 = _group_mask()
    sq = gz * gz
    s0 = jnp.sum(jnp.where(g0, sq, 0.0), axis=1, keepdims=True)
    s1 = jnp.sum(jnp.where(g0, 0.0, sq), axis=1, keepdims=True)
    half = SSD_IN // 2
    r = jnp.where(g0, lax.rsqrt(s0 / half + EPS), lax.rsqrt(s1 / half + EPS))
    return ytot, z, gz, r


def _out_proj(R, x, mod, yf, yb, xbc, z, o, pool, dsk, snw, wout):
    B, T = R.B, R.T

    def body(x_ref, mod_ref, yf_ref, yb_ref, xbc_ref, z_ref, o_ref, pool_ref, dsk_ref, snw_ref, w_ref, xmid_ref, cat_ref,
             mix_ref):
        _, _, gz, r = _ssd_gate(yf_ref, yb_ref, xbc_ref, z_ref, dsk_ref)
        cat_ref[:, 0:SSD_IN] = (gz * r * snw_ref[...]).astype(BF)
        cat_ref[:, SSD_IN:SSD_IN + QW] = o_ref[...].astype(BF)
        cat_ref[:, SSD_IN + QW:MIXW] = pool_ref[...].astype(BF)
        mix = jnp.dot(cat_ref[...], w_ref[...], preferred_element_type=F32)
        mix_ref[...] = mix.astype(BF)
        xmid_ref[...] = x_ref[...] + _row(mod_ref, 2) * mix

    return R.call("out_proj", body,
                  [(x, R.row(D)), (mod, R.mod()), (yf, R.row(SSD_IN)), (yb, R.row(SSD_IN)), (xbc, R.row(XBC)), (z, R.row(SSD_IN)),
                   (o, R.row(QW)),
                   (pool, R.row(PD)), (dsk, R.const((1, SSD_IN))), (snw, R.const((1, SSD_IN))), (wout, R.const((MIXW, D)))],
                  [(_sds((B, T, D), F32), R.row(D)), (_sds((B, T, MIXW), BF), R.row(MIXW)), (_sds((B, T, D), BF), R.row(D))],
                  extra=8 << 20)


def _load_once(first, pairs, sem):
    @pl.when(first)
    def _():
        cps = [pltpu.make_async_copy(src, dst, sem.at[k]) for k, (src, dst) in enumerate(pairs)]
        for cp in cps:
            cp.start()
        for cp in cps:
            cp.wait()


ANY = pl.BlockSpec(memory_space=pl.ANY)


def _mlp_fwd(R, xmid, mod, nw2, w1, w2):
    B, T = R.B, R.T

    def body(x_ref, mod_ref, nw_ref, w1_hbm, w2_hbm, xo_ref, h_ref, u_ref, y_ref, w1_v, w2_v, sem):
        first = jnp.logical_and(pl.program_id(0) == 0, pl.program_id(1) == 0)
        _load_once(first, [(w1_hbm, w1_v), (w2_hbm, w2_v)], sem)
        x = x_ref[...]
        h, _, _ = _norm_mod(x, nw_ref[...], _row(mod_ref, 3), _row(mod_ref, 4))
        hb = h.astype(BF)
        h_ref[...] = hb
        y = jnp.zeros((TM, D), F32)
        for j in range(NDEV):
            u = jnp.dot(hb, w1_v[j], preferred_element_type=F32)
            u_ref[:, FSH * j:FSH * (j + 1)] = u.astype(BF)
            a = jnp.square(jnp.maximum(u, 0.0))
            y = y + jnp.dot(a.astype(BF), w2_v[j], preferred_element_type=F32)
        y_ref[...] = y.astype(BF)
        xo_ref[...] = x + _row(mod_ref, 5) * y

    return R.call("mlp_fwd", body,
                  [(xmid, R.row(D)), (mod, R.mod()), (nw2, R.const((1, D))), (w1, ANY), (w2, ANY)],
                  [(_sds((B, T, D), F32), R.row(D)), (_sds((B, T, D), BF), R.row(D)), (_sds((B, T, FF), BF), R.row(FF)),
                   (_sds((B, T, D), BF), R.row(D))],
                  scratch=[pltpu.VMEM((NDEV, D, FSH), w1.dtype), pltpu.VMEM((NDEV, FSH, D), w2.dtype), pltpu.SemaphoreType.DMA((2,))],
                  extra=(2 * _nbytes((NDEV, D, FSH), BF)) + (8 << 20))


def _loss_head(R, x, tgt, fnw):
    B, T = R.B, R.T

    def body(x_ref, t_ref, w_ref, dx_ref, loss_ref, dw_ref):
        b, i = pl.program_id(0), pl.program_id(1)
        live = jnp.where(i >= R.nct, 1.0, 0.0)
        y, xh, r = _rms(x_ref[...], w_ref[...])
        err = (y - t_ref[...]) * live
        dy = err / D
        dxn, dw = _rms_bwd(dy, xh, r, w_ref[...])
        dx_ref[...] = dxn
        first = jnp.logical_and(b == 0, i == 0)
        part = 0.5 * jnp.sum(jnp.sum(err * err, axis=1, keepdims=True), axis=0, keepdims=True) / D
        _acc_rows(loss_ref, first, {0: jnp.broadcast_to(part, (1, HP))})
        _acc_rows(dw_ref, first, {0: dw})

    return R.call("loss_head", body,
                  [(x, R.row(D)), (tgt, R.tgt(D)), (fnw, R.const((1, D)))],
                  [(_sds((B, T, D), F32), R.row(D)), (_sds((8, HP), F32), R.const((8, HP))), (_sds((8, D), F32), R.const((8, D)))],
                  extra=8 << 20)


def _tn_matmul(name, a, b, tn, sub=1):
    B, T, K = a.shape
    N = b.shape[2]
    nk = 1
    while T % nk or (T // nk) > 1088 or (T // nk) % 16:
        nk += 1
    tk = T // nk
    kt = K if K <= 1536 else 1024
    w = tn // sub
    assert K % kt == 0 and N % tn == 0 and tn % sub == 0

    def body(a_ref, b_ref, o_ref):
        first = jnp.logical_and(pl.program_id(2) == 0, pl.program_id(3) == 0)

        @pl.when(first)
        def _():
            o_ref[...] = jnp.zeros(o_ref.shape, F32)

        acc = _dot_tn(a_ref[...], b_ref[...])
        for s in range(sub):
            o_ref[s] += acc[:, w * s:w * (s + 1)]

    bb = _nbytes((tk, kt), a.dtype) + _nbytes((tk, tn), b.dtype) + _nbytes((kt, tn), F32)
    return pl.pallas_call(
        functools.partial(body), name=name, grid=(N // tn, K // kt, B, nk),
        in_specs=[pl.BlockSpec((None, tk, kt), lambda j, kk, bi, t: (bi, t, kk)),
                  pl.BlockSpec((None, tk, tn), lambda j, kk, bi, t: (bi, t, j))],
        out_specs=pl.BlockSpec((sub, kt, w), lambda j, kk, bi, t: (j, kk, 0)),
        out_shape=_out((sub * N // tn, K, w), F32),
        compiler_params=_params(("arbitrary",) * 4, bb, _nbytes((kt, tn), F32) + (8 << 20)),
    )(*_hbm([a, b]))


def _mlp_bwd(R, dxo, xmid, ubf, ybf, mod, nw2, w1, w2):
    B, T = R.B, R.T

    def body(dxo_ref, x_ref, u_ref, y_ref, mod_ref, nw_ref, w1_hbm, w2_hbm,
             dxm_ref, du_ref, a_ref, dy_ref, dmod_ref, dnw_ref, w1_v, w2_v, sem):
        b, i = pl.program_id(0), pl.program_id(1)
        _load_once(jnp.logical_and(b == 0, i == 0), [(w1_hbm, w1_v), (w2_hbm, w2_v)], sem)
        dxo = dxo_ref[...]
        _, xn, r = _norm_mod(x_ref[...], nw_ref[...], _row(mod_ref, 3), _row(mod_ref, 4))
        dyb = (dxo * _row(mod_ref, 5)).astype(BF)
        dy_ref[...] = dyb
        dg2 = jnp.sum(dxo * y_ref[...].astype(F32), axis=0, keepdims=True)
        dh = jnp.zeros((TM, D), F32)
        for j in range(NDEV):
            rl = jnp.maximum(u_ref[:, FSH * j:FSH * (j + 1)].astype(F32), 0.0)
            a_ref[:, FSH * j:FSH * (j + 1)] = (rl * rl).astype(BF)
            du = (_dot_nt(dyb, w2_v[j]) * (2.0 * rl)).astype(BF)
            du_ref[:, FSH * j:FSH * (j + 1)] = du
            dh = dh + _dot_nt(du, w1_v[j])
        dx, dsh, dsc, dnw = _norm_mod_bwd(dh, xn, r, nw_ref[...], _row(mod_ref, 4))
        dxm_ref[...] = dxo + dx
        _acc_rows(dmod_ref, R.first_of_stream(i), {3: dsh, 4: dsc, 5: dg2})
        _acc_rows(dnw_ref, jnp.logical_and(b == 0, i == 0), {0: dnw})

    return R.call("mlp_bwd", body,
                  [(dxo, R.row(D)), (xmid, R.row(D)), (ubf, R.row(FF)), (ybf, R.row(D)), (mod, R.mod()), (nw2, R.const((1, D))),
                   (w1, ANY), (w2, ANY)],
                  [(_sds((B, T, D), F32), R.row(D)), (_sds((B, T, FF), BF), R.row(FF)), (_sds((B, T, FF), BF), R.row(FF)),
                   (_sds((B, T, D), BF), R.row(D)), (_sds((B, 2, 8, D), F32), R.mod()), (_sds((8, D), F32), R.const((8, D)))],
                  scratch=[pltpu.VMEM((NDEV, D, FSH), w1.dtype), pltpu.VMEM((NDEV, FSH, D), w2.dtype), pltpu.SemaphoreType.DMA((2,))],
                  extra=(2 * _nbytes((NDEV, D, FSH), BF)) + (8 << 20))


def _out_bwd(R, dxm, mod, mixbf, yf, yb, xbc, z, o, dsk, snw, wout):
    B, T = R.B, R.T

    def body(dxm_ref, mod_ref, mix_ref, yf_ref, yb_ref, xbc_ref, z_ref, o_ref, dsk_ref, snw_ref, w_ref,
             dmix_ref, dz_ref, dyt_ref, dxsk_ref, do_ref, dotr_ref, dd_ref, dpo_ref, dmod_ref, dvec_ref):
        b, i = pl.program_id(0), pl.program_id(1)
        dxm = dxm_ref[...]
        dmixb = (dxm * _row(mod_ref, 2)).astype(BF)
        dmix_ref[...] = dmixb
        dg1 = jnp.sum(dxm * mix_ref[...].astype(F32), axis=0, keepdims=True)
        dcat = _dot_nt(dmixb, w_ref[...])
        do_v = dcat[:, SSD_IN:SSD_IN + QW]
        do_ref[...] = do_v.astype(BF)
        doo = do_v * o_ref[...]
        for h in range(H):
            dd_ref[h] = jnp.broadcast_to(jnp.sum(doo[:, HP * h:HP * (h + 1)], axis=1, keepdims=True), (TM, HP))
            dotr_ref[h] = do_v[:, HP * h:HP * (h + 1)].T.astype(BF)
        dpo_ref[...] = dcat[:, SSD_IN + QW:MIXW]
        dsn = dcat[:, 0:SSD_IN]
        ytot, zv, gz, r = _ssd_gate(yf_ref, yb_ref, xbc_ref, z_ref, dsk_ref)
        gh = gz * r
        dsnw = jnp.sum(dsn * gh, axis=0, keepdims=True)
        dgh = dsn * snw_ref[...]
        g0 = _group_mask()
        pr = dgh * gh
        half = SSD_IN // 2
        m0 = jnp.sum(jnp.where(g0, pr, 0.0), axis=1, keepdims=True) / half
        m1 = jnp.sum(jnp.where(g0, 0.0, pr), axis=1, keepdims=True) / half
        dgz = r * (dgh - gh * jnp.where(g0, m0, m1))
        dyt = dgz * _silu(zv)
        dz_ref[...] = (dgz * ytot * _dsilu(zv)).astype(BF)
        dyt_ref[...] = dyt
        dxsk_ref[...] = dyt * dsk_ref[...]
        ddsk = jnp.sum(dyt * xbc_ref[:, 0:SSD_IN], axis=0, keepdims=True)
        _acc_rows(dmod_ref, R.first_of_stream(i), {2: dg1})
        _acc_rows(dvec_ref, jnp.logical_and(b == 0, i == 0), {0: dsnw, 1: ddsk})

    return R.call("out_bwd", body,
                  [(dxm, R.row(D)), (mod, R.mod()), (mixbf, R.row(D)), (yf, R.row(SSD_IN)), (yb, R.row(SSD_IN)), (xbc, R.row(XBC)),
                   (z, R.row(SSD_IN)),
                   (o, R.row(QW)), (dsk, R.const((1, SSD_IN))), (snw, R.const((1, SSD_IN))), (wout, R.const((MIXW, D)))],
                  [(_sds((B, T, D), BF), R.row(D)), (_sds((B, T, SSD_IN), BF), R.row(SSD_IN)), (_sds((B, T, SSD_IN), F32), R.row(SSD_IN)),
                   (_sds((B, T, SSD_IN), F32), R.row(SSD_IN)), (_sds((B, T, QW), BF), R.row(QW)),
                   (_sds((B, H, HP, T), BF), pl.BlockSpec((None, H, HP, TM), lambda b, i: (b, 0, 0, i))),
                   (_sds((B, H, T, HP), F32), pl.BlockSpec((None, H, TM, HP), lambda b, i: (b, 0, i, 0))),
                   (_sds((B, T, PD), F32), R.row(PD)),
                   (_sds((B, 2, 8, D), F32), R.mod()), (_sds((8, SSD_IN), F32), R.const((8, SSD_IN)))],
                  extra=8 << 20)


def _pool_bwd(R, dpo, u, wbd, scale):
    B, T = R.B, R.T

    def body(d_ref, dpv_ref, dnx_ref, u_ref, upv_ref, unx_ref, w_ref, sc_ref, du_ref, dw_ref, dsc_ref):
        b, i = pl.program_id(0), pl.program_id(1)
        ext_u = R.ext(i, upv_ref, u_ref[...], unx_ref)
        ext_d = R.ext(i, dpv_ref, d_ref[...], dnx_ref)
        dm, cnt, valid, lane = _pool_centred(R, i, ext_u)
        ddm = _dot_nt(ext_d * sc_ref[...], w_ref[...]) * valid
        e = ddm / cnt
        a2 = e + _shift_rows(e, 1)
        a4 = _shift_rows(a2, -1) + _shift_rows(a2, 1)
        a8 = _shift_rows(a4, -2) + _shift_rows(a4, 2)
        a16 = _shift_rows(a8, -4) + _shift_rows(a8, 4)
        du_ref[...] = (_lane_select(lane, a2, a4, a8, a16) - ddm)[8:8 + TM].astype(BF)
        dmc = dm[8:8 + TM]
        dpo_c = d_ref[...]
        first = jnp.logical_and(b == 0, i == 0)

        @pl.when(first)
        def _():
            dw_ref[...] = jnp.zeros(dw_ref.shape, F32)

        dw_ref[...] += _dot_tn(dmc, dpo_c * sc_ref[...])
        _acc_rows(dsc_ref, first, {0: jnp.sum(dpo_c * _dot(dmc, w_ref[...]), axis=0, keepdims=True)})

    return R.call("pool_bwd", body,
                  [(dpo, R.row(PD)), (dpo, R.prev8(PD)), (dpo, R.next8(PD)), (u, R.row(PD)), (u, R.prev8(PD)), (u, R.next8(PD)),
                   (wbd, R.const((PD, PD))), (scale, R.const((1, PD)))],
                  [(_sds((B, T, PD), BF), R.row(PD)), (_sds((PD, PD), F32), R.const((PD, PD))), (_sds((8, PD), F32), R.const((8, PD)))],
                  extra=8 << 20)


def _flash_bwd(B, T, qt, kt, dob, q_t, k_t, v_t, do_t, lse_b, dd_b):
    nk, nq = T // TK, T // TQ

    def body(k_ref, kt_ref, vt_ref, q_ref, do_ref, qt_ref, dot_ref, lse_ref, dd_ref, dq_ref, dk_ref, dv_ref,
             s_scr, dp_scr, ds_scr):
        j = pl.program_id(2)
        s_scr[...] = _dot(q_ref[...], kt_ref[...])
        dp_scr[...] = _dot(do_ref[...], vt_ref[...])
        ctx_keys = jnp.where(j < CTX // TK, 1.0, 0.0)
        dk_tr = jnp.zeros((HP, TK), F32)
        dv_tr = jnp.zeros((HP, TK), F32)
        for i in range(nq):
            rows = slice(TQ * i, TQ * (i + 1))
            lse2 = jnp.concatenate([lse_ref[rows, :]] * (TK // HP), axis=1)
            dd2 = jnp.concatenate([dd_ref[rows, :]] * (TK // HP), axis=1)
            p = jnp.exp(s_scr[rows, :] - lse2)
            if i < CTX // TQ:
                p = p * ctx_keys
            ds = (p * (dp_scr[rows, :] - dd2)).astype(BF)
            ds_scr[rows, :] = ds
            dv_tr = dv_tr + _dot(dot_ref[:, rows], p)
            dk_tr = dk_tr + _dot(qt_ref[:, rows], ds)
        dq = _dot(ds_scr[...], k_ref[...])

        @pl.when(j == 0)
        def _():
            dq_ref[...] = dq

        @pl.when(j > 0)
        def _():
            dq_ref[...] += dq

        dk_ref[...] = dk_tr.T
        dv_ref[...] = dv_tr.T

    tspec = pl.BlockSpec((None, TK, HP), lambda b, h, j: (b, j, h))
    fspec = pl.BlockSpec((None, T, HP), lambda b, h, j: (b, 0, h))
    ttspec = pl.BlockSpec((None, None, HP, TK), lambda b, h, j: (b, h, 0, j))
    ftspec = pl.BlockSpec((None, None, HP, T), lambda b, h, j: (b, h, 0, 0))
    bspec = pl.BlockSpec((None, None, T, HP), lambda b, h, j: (b, h, 0, 0))
    bb = 4 * _nbytes((T, HP), BF) + 3 * _nbytes((T, HP), F32) + 8 * _nbytes((TK, HP), F32)
    scr = 2 * _nbytes((T, TK), F32) + _nbytes((T, TK), BF)
    return pl.pallas_call(
        functools.partial(body), name="flash_bwd", grid=(B, H, nk),
        in_specs=[tspec, ttspec, ttspec, fspec, fspec, ftspec, ftspec, bspec, bspec],
        out_specs=[fspec, tspec, tspec],
        out_shape=[_out((B, T, QW), F32)] * 3,
        scratch_shapes=[pltpu.VMEM((T, TK), F32), pltpu.VMEM((T, TK), F32), pltpu.VMEM((T, TK), BF)],
        compiler_params=_params(("arbitrary",) * 3, bb, scr + (12 << 20)),
    )(*_hbm([kt, k_t, v_t, qt, dob, q_t, do_t, lse_b, dd_b]))


def _mla_bwd(R, dqt, dkt, dvt, qa, kva, qnw, kvnw, wq, wk, wv, cos, sin):
    B, T = R.B, R.T
    scale = QK ** -0.5

    def body(dq_ref, dk_ref, dv_ref, qa_ref, kva_ref, qnw_ref, kvnw_ref, wq_ref, wk_ref, wv_ref, cos_ref, sin_ref,
             dqp_ref, dkb_ref, dvb_ref, dqa_ref, dkva_ref, dkr_ref, dnw_ref):
        b, i = pl.program_id(0), pl.program_id(1)
        cos1, sin1 = cos_ref[...], sin_ref[...]
        dq = dq_ref[...] * scale
        dqp = _per_head(lambda g: g * cos1 + _swap8(g * sin1), dq).astype(BF)
        dqp_ref[...] = dqp
        dkv = dk_ref[...]
        dkb = dkv.astype(BF)
        dkb_ref[...] = dkb
        dvb = dv_ref[...].astype(BF)
        dvb_ref[...] = dvb
        dkk = dkv[:, 0:HP]
        for h in range(1, H):
            dkk = dkk + dkv[:, HP * h:HP * (h + 1)]
        lane = _iota((TM, HP), 1)
        rope_lane = jnp.logical_and(lane >= NOPE, lane < NOPE + ROPE)
        dkr_ref[...] = jnp.where(rope_lane, dkk * cos1 + _swap8(dkk * sin1), 0.0).astype(BF)
        _, qh, qr = _rms(qa_ref[...], qnw_ref[...])
        _, kh, kr_ = _rms(kva_ref[...], kvnw_ref[...])
        dcq = _dot_nt(dqp, wq_ref[...])
        dckv = _dot_nt(dkb, wk_ref[...]) + _dot_nt(dvb, wv_ref[...])
        dqa, dqw = _rms_bwd(dcq, qh, qr, qnw_ref[...])
        dkva, dkw = _rms_bwd(dckv, kh, kr_, kvnw_ref[...])
        dqa_ref[...] = dqa.astype(BF)
        dkva_ref[...] = dkva.astype(BF)
        _acc_rows(dnw_ref, jnp.logical_and(b == 0, i == 0), {0: dqw, 1: dkw})

    tab = pl.BlockSpec((TM, HP), lambda b, i: (i, 0))
    return R.call("mla_bwd", body,
                  [(dqt, R.row(QW)), (dkt, R.row(QW)), (dvt, R.row(QW)), (qa, R.row(QL)), (kva, R.row(QL)),
                   (qnw, R.const((1, QL))), (kvnw, R.const((1, QL))), (wq, R.const((QL, QW))), (wk, R.const((QL, QW))),
                   (wv, R.const((QL, QW))), (cos, tab), (sin, tab)],
                  [(_sds((B, T, QW), BF), R.row(QW))] * 3 + [(_sds((B, T, QL), BF), R.row(QL))] * 2
                  + [(_sds((B, T, HP), BF), R.row(HP)), (_sds((8, QL), F32), R.const((8, QL)))], extra=8 << 20)


def _ssd_scan_bwd(B, T, dyt, xbc, dt2, alog2, hin_f, hin_b):
    nc, nctc = T // CH, CTX // CH

    def chain(d, dy_ref, xbc_ref, dt_ref, alog_ref, hin_ref, dxbc_ref, ddt_ref, da_ref, dhs_all):
        dhs = dhs_all.at[d]
        xbc_v = xbc_ref[...]
        dyv = dy_ref[...]
        a_all, dt_all, adt_all, q_all, inc, inc_t = _scan_common(d, dt_ref[...], alog_ref)
        ones_p = jnp.ones((P, HP), F32)
        total = lambda m: jnp.sum(jnp.sum(m, axis=0, keepdims=True), axis=1, keepdims=True)
        dq_parts, dqtot_parts, ddtx_parts = [], [], []
        for g in range(2):
            bg = xbc_v[:, SSD_IN + NST * g:SSD_IN + NST * (g + 1)]
            cg = xbc_v[:, SSD_IN + 2 * NST + NST * g:SSD_IN + 2 * NST + NST * (g + 1)]
            bg_t, cg_t = bg.T, cg.T
            gm = _dot(cg, bg_t)
            gm_t = _dot(bg, cg_t)
            dgm = jnp.zeros((CH, CH), F32)
            dgm_t = jnp.zeros((CH, CH), F32)
            dbg = jnp.zeros((CH, NST), F32)
            dcg = jnp.zeros((CH, NST), F32)
            for r in range(3):
                h = 3 * g + r
                lm, lm_t, eq, etot, dte = _head_decay(h, adt_all, q_all, inc, inc_t)
                eq_p, etot_p, dte_p = eq[:, :P], etot[:, :P], dte[:, :P]
                dt_p = dt_all[:, HP * h:HP * h + P]
                xs_h = xbc_v[:, P * h:P * (h + 1)]
                xh = xs_h * dt_p
                sm, sm_t = gm * lm, gm_t * lm_t
                dy_h = dyv[:, P * h:P * (h + 1)]
                hprev = hin_ref[P * h:P * (h + 1), :].T
                dho = dhs[:, P * h:P * (h + 1)]
                ds = _dot_nt(dy_h, xh)
                ds_t = _dot_nt(xh, dy_h)
                dx = _dot(sm_t, dy_h)
                edy = eq_p * dy_h
                yo = _dot(cg, hprev)
                dcg = dcg + _dot_nt(edy, hprev)
                dhin = _dot(cg_t, edy) + etot_p * dho
                zs = _dot(bg, dho)
                dx = dx + dte_p * zs
                wm = dte_p * xh * zs
                dbg = dbg + _dot_nt(xh * dte_p, dho)
                dgm = dgm + ds * lm
                dgm_t = dgm_t + ds_t * lm_t
                rs = jnp.sum(ds * sm - ds_t * sm_t, axis=1, keepdims=True) + jnp.sum(edy * yo - wm, axis=1, keepdims=True)
                dq_parts.append(jnp.broadcast_to(rs, (CH, HP)))
                dqtot_parts.append(total(hprev * dho) * etot + total(wm))
                dxbc_ref[:, P * h:P * (h + 1)] = dx * dt_p
                ddtx_parts.append(_dot01_r(dx * xs_h, ones_p, passes=2))
                dhs[:, P * h:P * (h + 1)] = dhin
            dcg = dcg + _dot(dgm, bg)
            dbg = dbg + _dot(dgm_t, cg)
            dxbc_ref[:, SSD_IN + NST * g:SSD_IN + NST * (g + 1)] = dbg
            dxbc_ref[:, SSD_IN + 2 * NST + NST * g:SSD_IN + 2 * NST + NST * (g + 1)] = dcg
        cat = lambda parts: jnp.concatenate(parts, axis=1)
        dadt_all = _dot01(inc_t, cat(dq_parts)) + cat(dqtot_parts)
        ddt_all = cat(ddtx_parts) + dadt_all * jnp.concatenate([a_all] * (CH // 8), axis=0)
        da_all = jnp.sum(dadt_all * dt_all, axis=0, keepdims=True)
        lane, lane1 = _iota((CH, HP), 1), _iota((1, HP), 1)
        ddt = jnp.zeros((CH, HP), F32)
        da = jnp.zeros((1, HP), F32)
        for h in range(H):
            ddt = ddt + jnp.where(lane == h, ddt_all[:, HP * h:HP * (h + 1)], 0.0)
            da = da + jnp.where(lane1 == h, da_all[:, HP * h:HP * (h + 1)], 0.0)
        ddt_ref[...] = ddt
        da_ref[d] += jnp.where(_iota((8, HP), 0) == 0, da, 0.0)

    def body(dyf_ref, dyb_ref, xf_ref, xb_ref, dtf_ref, dtb_ref, alog_ref, hf_ref, hb_ref,
             dxf_ref, dxb_ref, ddtf_ref, ddtb_ref, da_ref, dhs):
        @pl.when(pl.program_id(1) == 0)
        def _():
            dhs[...] = jnp.zeros(dhs.shape, F32)
            da_ref[...] = jnp.zeros(da_ref.shape, F32)

        chain(0, dyf_ref, xf_ref, dtf_ref, alog_ref, hf_ref, dxf_ref, ddtf_ref, da_ref, dhs)
        chain(1, dyb_ref, xb_ref, dtb_ref, alog_ref, hb_ref, dxb_ref, ddtb_ref, da_ref, dhs)

    cidx = lambda d: (lambda s: _chunk_index(d, nc - 1 - s, nc, nctc))
    specs = lambda d: dict(
        dy=pl.BlockSpec((None, CH, SSD_IN), lambda b, s: (b, cidx(d)(s), 0)),
        xbc=pl.BlockSpec((None, CH, XBC), lambda b, s: (b, cidx(d)(s), 0)),
        dt=pl.BlockSpec((None, None, CH, HP), lambda b, s: (b, d, cidx(d)(s), 0)),
        h=pl.BlockSpec((None, None, SSD_IN, NST), lambda b, s: (b, cidx(d)(s), 0, 0)),
        ddt=pl.BlockSpec((None, CH, HP), lambda b, s: (b, cidx(d)(s), 0)))
    f, r = specs(0), specs(1)
    bb = 2 * (2 * _nbytes((CH, XBC), F32) + 2 * _nbytes((CH, HP), F32) + _nbytes((CH, SSD_IN), F32) + _nbytes((SSD_IN, NST), F32))
    return pl.pallas_call(
        functools.partial(body), name="ssd_scan_bwd", grid=(B, nc),
        in_specs=[f["dy"], r["dy"], f["xbc"], r["xbc"], f["dt"], r["dt"], pl.BlockSpec((2, 8, HP), lambda b, s: (0, 0, 0)),
                  f["h"], r["h"]],
        out_specs=[f["xbc"], r["xbc"], f["ddt"], r["ddt"], pl.BlockSpec((None, 2, 8, HP), lambda b, s: (b, 0, 0, 0))],
        out_shape=[_out((B, T, XBC), F32)] * 2 + [_out((B, T, HP), F32)] * 2 + [_out((B, 2, 8, HP), F32)],
        scratch_shapes=[pltpu.VMEM((2, NST, SSD_IN), F32)],
        compiler_params=_params(("arbitrary",) * 2, bb, 16 << 20),
    )(*_hbm([dyt, dyt, xbc, xbc, dt2, dt2, alog2, hin_f, hin_b]))


def _ssd_prep_bwd(R, dxbc_f, dxbc_b, dxsk, ddt_f, ddt_b, xbc_raw, dt_raw, conv_w8, conv_b, dtb):
    B, T = R.B, R.T

    def body(dxf_ref, dxb_ref, dsk_ref, ddtf_ref, ddtb2_ref, raw_ref, pv_ref, nx_ref, dtr_ref, w_ref, b_ref, dtb_ref,
             dpre_ref, ddtr_ref, dcw_ref, dvec_ref, ddtb_ref):
        b, i = pl.program_id(0), pl.program_id(1)
        ext = R.ext(i, pv_ref, raw_ref[...], nx_ref)
        pre = _conv_pre(ext, w_ref, b_ref)
        dxbc = dxf_ref[...] + dxb_ref[...]
        skip = jnp.concatenate([dsk_ref[...], jnp.zeros((TM, XBC - SSD_IN), F32)], axis=1)
        dpre = (dxbc + skip) * _dsilu(pre)
        dpre_ref[...] = dpre
        first = jnp.logical_and(b == 0, i == 0)
        taps = {k: jnp.sum(dpre * _shift_rows(ext, k - 1)[8:8 + TM], axis=0, keepdims=True) for k in range(4)}
        _acc_rows(dcw_ref, first, taps)
        _acc_rows(dvec_ref, first, {0: jnp.sum(dpre, axis=0, keepdims=True)})
        ddt = ddtf_ref[...] + pltpu.roll(ddtb2_ref[...], H, axis=1)
        ddtr = ddt * _sigmoid(dtr_ref[...] + dtb_ref[...])
        ddtr = jnp.where(_iota((TM, HP), 1) < 2 * H, ddtr, 0.0)
        ddtr_ref[...] = ddtr.astype(BF)
        _acc_rows(ddtb_ref, first, {0: jnp.sum(ddtr, axis=0, keepdims=True)})

    return R.call("ssd_prep_bwd", body,
                  [(dxbc_f, R.row(XBC)), (dxbc_b, R.row(XBC)), (dxsk, R.row(SSD_IN)), (ddt_f, R.row(HP)), (ddt_b, R.row(HP)),
                   (xbc_raw, R.row(XBC)), (xbc_raw, R.prev8(XBC)),
                   (xbc_raw, R.next8(XBC)), (dt_raw, R.row(HP)), (conv_w8, R.const((8, XBC))), (conv_b, R.const((1, XBC))),
                   (dtb, R.const((1, HP)))],
                  [(_sds((B, T, XBC), F32), R.row(XBC)), (_sds((B, T, HP), BF), R.row(HP)), (_sds((8, XBC), F32), R.const((8, XBC))),
                   (_sds((8, XBC), F32), R.const((8, XBC))), (_sds((8, HP), F32), R.const((8, HP)))], extra=16 << 20)


def _in_bwd(R, dxm, x, mod, nw1, dz, dpre, dqa, dkva, dpool, dkr, ddtr, conv_w8, w_arr):
    B, T = R.B, R.T

    def body(dxm_ref, x_ref, mod_ref, nw_ref, dz_ref, dp_ref, dpp_ref, dpn_ref, dqa_ref, dkva_ref, dpool_ref, dkr_ref, ddt_ref,
             cw_ref, w_ref, dx_ref, dproj_ref, dmod_ref, dnw_ref):
        b, i = pl.program_id(0), pl.program_id(1)
        ext = R.ext(i, dpp_ref, dp_ref[...], dpn_ref)
        draw = (_row(cw_ref, 0) * _shift_rows(ext, 1)[8:8 + TM] + _row(cw_ref, 1) * ext[8:8 + TM]
                + _row(cw_ref, 2) * _shift_rows(ext, -1)[8:8 + TM] + _row(cw_ref, 3) * _shift_rows(ext, -2)[8:8 + TM])
        dproj_ref[:, A_Z:A_XBC] = dz_ref[...]
        dproj_ref[:, A_XBC:A_QA] = draw.astype(BF)
        dproj_ref[:, A_QA:A_KVA] = dqa_ref[...]
        dproj_ref[:, A_KVA:A_POOL] = dkva_ref[...]
        dproj_ref[:, A_POOL:A_KR] = dpool_ref[...]
        dproj_ref[:, A_KR:A_DT] = dkr_ref[...]
        dproj_ref[:, A_DT:PC] = ddt_ref[...]
        dh = _dot_nt(dproj_ref[...], w_ref[...])
        _, xn, r = _norm_mod(x_ref[...], nw_ref[...], _row(mod_ref, 0), _row(mod_ref, 1))
        dx, dsh, dsc, dnw = _norm_mod_bwd(dh, xn, r, nw_ref[...], _row(mod_ref, 1))
        dx_ref[...] = dxm_ref[...] + dx
        _acc_rows(dmod_ref, R.first_of_stream(i), {0: dsh, 1: dsc})
        _acc_rows(dnw_ref, jnp.logical_and(b == 0, i == 0), {0: dnw})

    return R.call("in_bwd", body,
                  [(dxm, R.row(D)), (x, R.row(D)), (mod, R.mod()), (nw1, R.const((1, D))), (dz, R.row(SSD_IN)), (dpre, R.row(XBC)),
                   (dpre, R.prev8(XBC)), (dpre, R.next8(XBC)), (dqa, R.row(QL)), (dkva, R.row(QL)), (dpool, R.row(PD)), (dkr, R.row(HP)),
                   (ddtr, R.row(HP)), (conv_w8, R.const((8, XBC))), (w_arr, R.const((D, PC)))],
                  [(_sds((B, T, D), F32), R.row(D)), (_sds((B, T, PC), BF), R.row(PC)), (_sds((B, 2, 8, D), F32), R.mod()),
                   (_sds((8, D), F32), R.const((8, D)))], extra=12 << 20)


def _adaln_fwd(cs, mod_w):
    L, _, C = mod_w.shape

    def body(c_ref, w_ref, o_ref):
        s = _silu(c_ref[...]).astype(BF)
        for l in range(L):
            o_ref[l] = jnp.dot(s, w_ref[l].astype(BF), preferred_element_type=F32)

    return pl.pallas_call(functools.partial(body), name="adaln_fwd", out_shape=_sds((L, 24, C), F32),
                          compiler_params=_params(None, _nbytes(mod_w.shape, F32) + _nbytes((L, 24, C), F32), 8 << 20))(cs, mod_w)


def _adaln_bwd(cs, dm, mod_w):
    L, _, C = mod_w.shape

    def body(c_ref, dm_ref, w_ref, gw_ref, gc_ref):
        c = c_ref[...]
        s = _silu(c).astype(BF)
        acc = jnp.zeros((24, D), F32)
        for l in range(L):
            dmb = dm_ref[l].astype(BF)
            gw_ref[l] = _dot_tn(s, dmb)
            acc = acc + _dot_nt(dmb, w_ref[l])
        gc_ref[...] = acc * _dsilu(c)

    return pl.pallas_call(functools.partial(body), name="adaln_bwd", out_shape=[_sds((L, D, C), F32), _sds((24, D), F32)],
                          compiler_params=_params(None, 2 * _nbytes(mod_w.shape, F32), 8 << 20))(cs, dm, mod_w)


def _adamw(name, parts, w, m, v, rt):
    Pn, Rr, C = parts.shape
    c1 = 1.0 - ADAM_B1 ** ADAM_STEP
    c2 = 1.0 - ADAM_B2 ** ADAM_STEP

    def body(p_ref, w_ref, m_ref, v_ref, g_ref, d_ref, nm_ref, nv_ref):
        g = p_ref[0].astype(F32)
        for k in range(1, Pn):
            g = g + p_ref[k].astype(F32)
        mn = ADAM_B1 * m_ref[...] + (1.0 - ADAM_B1) * g
        vn = ADAM_B2 * v_ref[...] + (1.0 - ADAM_B2) * jnp.square(g)
        g_ref[...] = g
        nm_ref[...] = mn
        nv_ref[...] = vn
        d_ref[...] = -ADAM_LR * ((mn / c1) / (jnp.sqrt(vn / c2) + ADAM_EPS) + ADAM_WD * w_ref[...])

    spec = pl.BlockSpec((rt, C), lambda i: (i, 0))
    bb = Pn * _nbytes((rt, C), parts.dtype) + 7 * _nbytes((rt, C), F32)
    return pl.pallas_call(
        functools.partial(body), name=name, grid=(Rr // rt,),
        in_specs=[pl.BlockSpec((Pn, rt, C), lambda i: (0, i, 0)), spec, spec, spec],
        out_specs=[spec] * 4, out_shape=[_out((Rr, C), F32)] * 4,
        compiler_params=_params(("arbitrary",), bb, 4 << 20),
    )(*_hbm([parts, w, m, v]))


MESH = pl.DeviceIdType.MESH


def _my_pos():
    return lax.axis_index("x"), lax.axis_index("y"), lax.axis_index("c")


def _dev_index(x, y, c):
    return 4 * x + 2 * y + c


def _all_gather(name, shards):
    n = len(shards)

    def body(*refs):
        ins, outs = refs[:n], refs[n:2 * n]
        send_sems, recv_sems, local_sem = refs[2 * n:]
        x, y, c = _my_pos()
        me, sibling = (x, y, c), (x, y, 1 - c)
        chips = [(1 - x, y), (x, 1 - y), (1 - x, 1 - y)]

        def copy(t, k, block, to, src=None):
            slot = outs[t].at[_dev_index(*block)]
            return pltpu.make_async_remote_copy(
                src_ref=slot if src is None else src, dst_ref=slot,
                send_sem=send_sems.at[t, k], recv_sem=recv_sems.at[t, k], device_id=to, device_id_type=MESH)

        mine = [pltpu.make_async_copy(ins[t], outs[t].at[_dev_index(*me)], local_sem.at[t]) for t in range(n)]
        for cp in mine:
            cp.start()
        first = []
        for t in range(n):
            first.append(copy(t, 0, me, sibling, src=ins[t]))
            first += [copy(t, 1 + j, me, (*chip, c), src=ins[t]) for j, chip in enumerate(chips)]
        for cp in first:
            cp.start()
        passed = []
        for j, chip in enumerate(chips):
            for t in range(n):
                copy(t, 1 + j, (*chip, c), me).wait_recv()
                cp = copy(t, 4 + j, (*chip, c), sibling)
                cp.start()
                passed.append(cp)
        for t in range(n):
            copy(t, 0, sibling, me).wait_recv()
            for j, chip in enumerate(chips):
                copy(t, 4 + j, (*chip, 1 - c), me).wait_recv()
        for cp in first + passed:
            cp.wait_send()
        for cp in mine:
            cp.wait()

    return pl.pallas_call(
        functools.partial(body), name=name,
        in_specs=[ANY] * n, out_specs=[ANY] * n,
        out_shape=[_sds((NDEV,) + s.shape, s.dtype) for s in shards],
        scratch_shapes=[pltpu.SemaphoreType.DMA((n, 7)), pltpu.SemaphoreType.DMA((n, 7)), pltpu.SemaphoreType.DMA((n,))],
    )(*shards)


def _all_to_all(name, parts):
    n = len(parts)

    def body(*refs):
        ins, outs = refs[:n], refs[n:2 * n]
        send_sems, recv_sems, local_sem = refs[2 * n:]
        x, y, c = _my_pos()
        me = _dev_index(x, y, c)
        peers = [(x ^ ((k >> 2) & 1), y ^ ((k >> 1) & 1), c ^ (k & 1)) for k in range(1, NDEV)]
        mine = [pltpu.make_async_copy(ins[t].at[me], outs[t].at[me], local_sem.at[t]) for t in range(n)]
        for cp in mine:
            cp.start()
        sends = []
        for t in range(n):
            for k, peer in enumerate(peers):
                cp = pltpu.make_async_remote_copy(
                    src_ref=ins[t].at[_dev_index(*peer)], dst_ref=outs[t].at[me],
                    send_sem=send_sems.at[t, k], recv_sem=recv_sems.at[t, k], device_id=peer, device_id_type=MESH)
                cp.start()
                sends.append(cp)
        for t in range(n):
            for k, peer in enumerate(peers):
                slot = outs[t].at[_dev_index(*peer)]
                pltpu.make_async_remote_copy(
                    src_ref=slot, dst_ref=slot, send_sem=send_sems.at[t, k], recv_sem=recv_sems.at[t, k],
                    device_id=peer, device_id_type=MESH).wait_recv()
        for cp in sends:
            cp.wait_send()
        for cp in mine:
            cp.wait()

    return pl.pallas_call(
        functools.partial(body), name=name,
        in_specs=[ANY] * n, out_specs=[ANY] * n,
        out_shape=[_sds(p.shape, p.dtype) for p in parts],
        scratch_shapes=[pltpu.SemaphoreType.DMA((n, 7)), pltpu.SemaphoreType.DMA((n, 7)), pltpu.SemaphoreType.DMA((n,))],
    )(*parts)


SEM = pl.BlockSpec(memory_space=pltpu.SEMAPHORE)
IN_HBM = pl.BlockSpec(memory_space=pltpu.HBM)
DATAFLOW = pltpu.SideEffectType.DATAFLOW_SIDE_EFFECTING


def _flip_peers(x, y, c):
    return [(x ^ ((k >> 2) & 1), y ^ ((k >> 1) & 1), c ^ (k & 1)) for k in range(1, NDEV)]


def _split_copies(srcs, lands, send_sems, recv_sems, gather):
    x, y, c = _my_pos()
    me = _dev_index(x, y, c)
    out = []
    for t in range(len(srcs)):
        for k, peer in enumerate(_flip_peers(x, y, c)):
            p = _dev_index(*peer)
            src = srcs[t] if gather else srcs[t].at[p]
            sems = dict(send_sem=send_sems.at[7 * t + k], recv_sem=recv_sems.at[7 * t + k], device_id=peer, device_id_type=MESH)
            out.append((pltpu.make_async_remote_copy(src_ref=src, dst_ref=lands[t].at[me], **sems),
                        pltpu.make_async_remote_copy(src_ref=src, dst_ref=lands[t].at[p], **sems)))
    return out


def _exchange_start(name, collective_id, srcs, gather):
    n = len(srcs)
    lands = [lax.empty(((NDEV,) + s.shape) if gather else s.shape, s.dtype) for s in srcs]

    def body(*refs):
        src_refs, land_refs = refs[:n], refs[n:2 * n]
        send_sems, recv_sems = refs[2 * n], refs[2 * n + 1]
        token = refs[-1]
        barrier = pltpu.get_barrier_semaphore()
        for peer in _flip_peers(*_my_pos()):
            pl.semaphore_signal(barrier, inc=1, device_id=peer, device_id_type=MESH)
        pl.semaphore_wait(barrier, NDEV - 1)
        for send, _ in _split_copies(src_refs, land_refs, send_sems, recv_sems, gather):
            send.start()
        token[...] = jnp.zeros(token.shape, token.dtype)

    hbm = lambda a: pltpu.HBM(a.shape, a.dtype)
    res = pl.pallas_call(
        functools.partial(body), name=name,
        out_shape=[pltpu.SemaphoreType.DMA((7 * n,)), pltpu.SemaphoreType.DMA((7 * n,))] + [hbm(s) for s in srcs]
        + [hbm(a) for a in lands] + [_sds((8, HP), F32)],
        in_specs=[IN_HBM] * (2 * n), out_specs=[SEM, SEM] + [IN_HBM] * (2 * n) + [pl.BlockSpec(memory_space=pltpu.VMEM)],
        input_output_aliases={i: 2 + i for i in range(2 * n)},
        compiler_params=pltpu.CompilerParams(has_side_effects=DATAFLOW, collective_id=collective_id),
    )(*_hbm(list(srcs) + lands))
    return res[0], res[1], list(res[2:2 + n]), list(res[2 + n:2 + 2 * n]), res[-1]


def _exchange_wait(name, send_sems, recv_sems, srcs, lands, after, gather):
    n = len(srcs)

    def body(*refs):
        src_refs, land_refs = refs[:n], refs[n:2 * n]
        for _, recv in _split_copies(src_refs, land_refs, refs[2 * n], refs[2 * n + 1], gather):
            recv.wait_send()
            recv.wait_recv()

    hbm = lambda a: pltpu.HBM(a.shape, a.dtype)
    res = pl.pallas_call(
        functools.partial(body), name=name,
        out_shape=[hbm(s) for s in srcs] + [hbm(a) for a in lands],
        in_specs=[IN_HBM] * (2 * n) + [SEM, SEM, ANY], out_specs=[IN_HBM] * (2 * n),
        input_output_aliases={i: i for i in range(2 * n)},
        compiler_params=pltpu.CompilerParams(has_side_effects=DATAFLOW),
    )(*srcs, *lands, send_sems, recv_sems, after)
    return list(res[:n]), list(res[n:])


def _with_own(lands, own_blocks, me):
    out = []
    for land, own in zip(lands, own_blocks):
        out.append(lax.dynamic_update_slice(land, own[None], (me,) + (0,) * own.ndim))
    return out


def _arrange_w_in(w):
    z = lambda n: jnp.zeros((w.shape[0], n), w.dtype)
    return jnp.concatenate([w[:, 0:1280], w[:, 1292:1548], w[:, 1548:1804], w[:, 1836:2092],
                            z(64), w[:, 1804:1836], z(32), w[:, 1280:1292], z(HP - 2 * H)], axis=1)


def _unarrange_w_in(g):
    return jnp.concatenate([g[:, 0:1280], g[:, A_DT:A_DT + 2 * H], g[:, A_QA:A_KVA], g[:, A_KVA:A_POOL],
                            g[:, A_KR + NOPE:A_KR + NOPE + ROPE], g[:, A_POOL:A_KR]], axis=1)


def _pad_heads(w, width):
    k = w.shape[0]
    return jnp.pad(w.reshape(k, H, width), ((0, 0), (0, 0), (0, HP - width))).reshape(k, H * HP)


def _unpad_heads(g, width):
    k = g.shape[0]
    return g.reshape(k, H, HP)[:, :, :width].reshape(k, H * width)


def _arrange_w_out(w):
    att = jnp.pad(w[SSD_IN:2 * SSD_IN].reshape(H, VH, D), ((0, 0), (0, HP - VH), (0, 0))).reshape(QW, D)
    return jnp.concatenate([w[0:SSD_IN], att, w[2 * SSD_IN:]], axis=0)


def _unarrange_w_out(g):
    att = g[SSD_IN:SSD_IN + QW].reshape(H, HP, D)[:, :VH].reshape(SSD_IN, D)
    return jnp.concatenate([g[0:SSD_IN], att, g[SSD_IN + QW:]], axis=0)


def _rope_tables(T):
    n = T - CTX
    t = jnp.arange(n, dtype=F32)
    row, col = jnp.floor(t / GRID_W), t % GRID_W
    pairs = ROPE // 4
    inv = ROPE_THETA ** (-jnp.arange(pairs, dtype=F32) / pairs)
    ar, ac = row[:, None] * inv, col[:, None] * inv
    cos = jnp.concatenate([jnp.cos(ar)] * 2 + [jnp.cos(ac)] * 2, axis=1)
    sin = jnp.concatenate([-jnp.sin(ar), jnp.sin(ar), -jnp.sin(ac), jnp.sin(ac)], axis=1)
    ones, zeros = jnp.ones((n, NOPE), F32), jnp.zeros((n, NOPE), F32)
    cos = jnp.concatenate([ones, cos, ones[:, :HP - QK]], axis=1)
    sin = jnp.concatenate([zeros, sin, zeros[:, :HP - QK]], axis=1)
    return (jnp.concatenate([jnp.ones((CTX, HP), F32), cos], axis=0),
            jnp.concatenate([jnp.zeros((CTX, HP), F32), sin], axis=0))


def _lane_pad(v, n):
    return jnp.pad(v, (0, n - v.shape[0]))[None, :]


def _layer_weights(w_in, w_q_b, w_kv_b, conv_w, pool_w):
    kv = w_kv_b.reshape(QL, H, NOPE + VH)
    wbd = jnp.zeros((PD, PD), F32)
    for g in range(4):
        wbd = lax.dynamic_update_slice(wbd, pool_w[g], (64 * g, 64 * g))
    return dict(
        w_in=_arrange_w_in(w_in).astype(BF),
        wq=_pad_heads(w_q_b, QK).astype(BF),
        wk=_pad_heads(kv[:, :, :NOPE].reshape(QL, H * NOPE), NOPE).astype(BF),
        wv=_pad_heads(kv[:, :, NOPE:].reshape(QL, H * VH), VH).astype(BF),
        conv_w8=jnp.pad(conv_w, ((0, 4), (0, 0))), wbd=wbd)


def _layer_fwd(R, x, mod, lw, sp, late, cos, sin):
    B, T = R.B, R.T
    z, xbc_raw, qa, kva, pool_in, kr, dt_raw, h1 = _in_proj(R, x, mod, sp["nw1"], lw["w_in"])
    xbc, dt2 = _ssd_prep(R, xbc_raw, dt_raw, lw["conv_w8"], sp["conv_b"], sp["dtb"])
    yf, yb, hin_f, hin_b = _ssd_scan(B, T, xbc, dt2, sp["alog2"])
    qt, kt, q_t, k_t, v_t, cq, ckv = _mla_prep(R, qa, kva, kr, sp["qnw"], sp["kvnw"], lw["wq"], lw["wk"], lw["wv"], cos, sin)
    o, lse = _flash_fwd(B, T, q_t, kt, v_t)
    pool = _pool_fwd(R, pool_in, lw["wbd"], sp["pscale"])[0]
    w_out, w1, w2 = late(o)
    w_out = _arrange_w_out(w_out).astype(BF)
    xmid, cat, mixbf = _out_proj(R, x, mod, yf, yb, xbc, z, o, pool, sp["dsk"], sp["snw"], w_out)
    xo, h2, ubf, ybf = _mlp_fwd(R, xmid, mod, sp["nw2"], w1, w2)
    saved = dict(x=x, z=z, xbc_raw=xbc_raw, qa=qa, kva=kva, pool_in=pool_in, dt_raw=dt_raw, h1=h1, xbc=xbc, dt2=dt2,
                 yf=yf, yb=yb, hin_f=hin_f, hin_b=hin_b, qt=qt, kt=kt, q_t=q_t, k_t=k_t, v_t=v_t, cq=cq, ckv=ckv, o=o, lse=lse,
                 cat=cat, mixbf=mixbf, xmid=xmid, h2=h2, ubf=ubf, ybf=ybf, w_out=w_out, w1=w1, w2=w2)
    return xo, saved


def _layer_bwd(R, dxo, sv, mod, lw, sp, cos, sin, on_mlp=None):
    B, T = R.B, R.T
    dxm, du, abf, dyb, dmod_a, dnw2 = _mlp_bwd(R, dxo, sv["xmid"], sv["ubf"], sv["ybf"], mod, sp["nw2"], sv["w1"], sv["w2"])
    g_w1 = _tn_matmul("dw_mlp1", sv["h2"], du, 4 * FSH, sub=4)
    g_w2 = _tn_matmul("dw_mlp2", abf, dyb, D)[0].reshape(NDEV, FSH, D)
    snw = sp["snw"]
    tok = on_mlp(g_w1, g_w2) if on_mlp is not None else None
    if tok is not None:
        snw = snw + tok
    dmix, dz, dyt, dxsk, dob, do_t, dd, dpo, dmod_b, dvec_o = _out_bwd(R, dxm, mod, sv["mixbf"], sv["yf"], sv["yb"], sv["xbc"], sv["z"],
                                                                 sv["o"], sp["dsk"], snw, sv["w_out"])
    g_wout = _unarrange_w_out(_tn_matmul("dw_out", sv["cat"], dmix, D)[0])
    dpool_in, g_wbd, dpsc = _pool_bwd(R, dpo, sv["pool_in"], lw["wbd"], sp["pscale"])
    dqt, dkt, dvt = _flash_bwd(B, T, sv["qt"], sv["kt"], dob, sv["q_t"], sv["k_t"], sv["v_t"], do_t, sv["lse"], dd)
    dqp, dkb, dvb, dqa, dkva, dkr, dnw_qk = _mla_bwd(R, dqt, dkt, dvt, sv["qa"], sv["kva"], sp["qnw"], sp["kvnw"],
                                                     lw["wq"], lw["wk"], lw["wv"], cos, sin)
    g_wq = _unpad_heads(_tn_matmul("dw_q", sv["cq"], dqp, QW)[0], QK)
    g_wk = _unpad_heads(_tn_matmul("dw_k", sv["ckv"], dkb, QW)[0], NOPE).reshape(QL, H, NOPE)
    g_wv = _unpad_heads(_tn_matmul("dw_v", sv["ckv"], dvb, QW)[0], VH).reshape(QL, H, VH)
    g_wkv = jnp.concatenate([g_wk, g_wv], axis=2).reshape(QL, H * (NOPE + VH))
    dxbc_f, dxbc_b, ddt_f, ddt_b, da = _ssd_scan_bwd(B, T, dyt, sv["xbc"], sv["dt2"], sp["alog2"], sv["hin_f"], sv["hin_b"])
    dpre, ddtr, dcw, dcb, ddtb = _ssd_prep_bwd(R, dxbc_f, dxbc_b, dxsk, ddt_f, ddt_b, sv["xbc_raw"], sv["dt_raw"], lw["conv_w8"],
                                                sp["conv_b"], sp["dtb"])
    dx, dproj, dmod_c, dnw1 = _in_bwd(R, dxm, sv["x"], mod, sp["nw1"], dz, dpre, dqa, dkva, dpool_in, dkr, ddtr,
                                      lw["conv_w8"], lw["w_in"])
    g_win = _unarrange_w_in(jnp.concatenate(list(_tn_matmul("dw_in", sv["h1"], dproj, PC)), axis=1))
    a2 = -jnp.exp(sp["alog2"][:, 0, :H])
    small = dict(
        norm1_w=dnw1[0], norm2_w=dnw2[0], conv_w=dcw[0:4], conv_b=dcb[0], dt_bias=ddtb[0, :2 * H].reshape(2, H),
        a_log=jnp.sum(da[:, :, 0, :H], axis=0) * a2, ssd_d=jnp.sum(dvec_o[1].reshape(H, P), axis=1), ssd_norm_w=dvec_o[0],
        q_a_norm_w=dnw_qk[0], kv_a_norm_w=dnw_qk[1],
        pool_w=jnp.stack([g_wbd[64 * g:64 * (g + 1), 64 * g:64 * (g + 1)] for g in range(4)]), pool_scale=dpsc[0])
    big = dict(w_in=g_win, w_out=g_wout, w_q_b=g_wq, w_kv_b=g_wkv, w_mlp1=g_w1, w_mlp2=g_w2)
    return dx, big, small, dmod_a + dmod_b + dmod_c


def _small_params(l, norm1_w, norm2_w, conv_b, dt_bias, a_log, ssd_d, ssd_norm_w, q_a_norm_w, kv_a_norm_w, pool_scale):
    alog2 = jnp.broadcast_to(jnp.pad(a_log[l], ((0, 0), (0, HP - H)))[:, None, :], (2, 8, HP))
    return dict(nw1=norm1_w[l][None], nw2=norm2_w[l][None], conv_b=conv_b[l][None],
                dtb=_lane_pad(dt_bias[l].reshape(2 * H), HP), alog2=alog2,
                dsk=jnp.repeat(ssd_d[l], P)[None], snw=ssd_norm_w[l][None], qnw=q_a_norm_w[l][None],
                kvnw=kv_a_norm_w[l][None], pscale=pool_scale[l][None])


SMALL_NAMES = ["mod_b", "norm1_w", "norm2_w", "conv_b", "dt_bias", "a_log", "ssd_d", "ssd_norm_w", "q_a_norm_w",
               "kv_a_norm_w", "pool_w", "pool_scale", "final_norm_w"]


def _pack(arrs):
    rows = []
    for a in arrs:
        f = a.reshape(-1).astype(F32)
        n = -(-f.shape[0] // HP) * HP
        rows.append(jnp.pad(f, (0, n - f.shape[0])).reshape(-1, HP))
    out = jnp.concatenate(rows, axis=0)
    pad = (-out.shape[0]) % 8
    return jnp.pad(out, ((0, pad), (0, 0)))


def _unpack(pack, like):
    outs, r = [], 0
    for a in like:
        n = math.prod(a.shape)
        nr = -(-n // HP)
        outs.append(pack[r:r + nr].reshape(-1)[:n].reshape(a.shape))
        r += nr
    return outs


def _local_step(x, ctx, target, mods, full_of, small_w, on_grads=None, on_mlp=None):
    B, N = x.shape[0], x.shape[1]
    T = CTX + N
    R = _Rows(B, T)
    cos, sin = _rope_tables(T)
    xu = jnp.concatenate([ctx, x], axis=1)
    L = len(mods)
    lws, sps, saves = [], [], []
    for l in range(L):
        f = full_of(l, xu)
        lws.append(_layer_weights(f["w_in"], f["w_q_b"], f["w_kv_b"], f["conv_w"], small_w["pool_w"][l]))
        sps.append(_small_params(l, *[small_w[k] for k in ["norm1_w", "norm2_w", "conv_b", "dt_bias", "a_log", "ssd_d",
                                                          "ssd_norm_w", "q_a_norm_w", "kv_a_norm_w", "pool_scale"]]))
        xu, sv = _layer_fwd(R, xu, mods[l], lws[l], sps[l], f["late"], cos, sin)
        saves.append(sv)
    dx, loss8, dfnw = _loss_head(R, xu, target, small_w["final_norm_w"][None])
    bigs, smalls, dmods = [None] * L, [None] * L, [None] * L
    for l in reversed(range(L)):
        hook = functools.partial(on_mlp, l) if on_mlp is not None else None
        dx, bigs[l], smalls[l], dmods[l] = _layer_bwd(R, dx, saves[l], mods[l], lws[l], sps[l], cos, sin, hook)
        if on_grads is not None:
            bigs[l], tok = on_grads(l, bigs[l], dx)
            if tok is not None:
                sps[l - 1] = dict(sps[l - 1], nw2=sps[l - 1]["nw2"] + tok)
    return loss8[0, 0], dx[:, CTX:], bigs, smalls, dfnw[0], dmods


def kernel(x, c, ctx, c_ctx, mod_w, mod_b, norm1_w, norm2_w, w_in, conv_w, conv_b, dt_bias, a_log, ssd_d, ssd_norm_w, q_a_norm_w, w_q_b, kv_a_norm_w, w_kv_b, pool_w, pool_scale, w_out, w_mlp1, w_mlp2, final_norm_w, loss_target, m_c_ctx, m_mod_w, m_mod_b, m_norm1_w, m_norm2_w, m_w_in, m_conv_w, m_conv_b, m_dt_bias, m_a_log, m_ssd_d, m_ssd_norm_w, m_q_a_norm_w, m_w_q_b, m_kv_a_norm_w, m_w_kv_b, m_pool_w, m_pool_scale, m_w_out, m_w_mlp1, m_w_mlp2, m_final_norm_w, v_c_ctx, v_mod_w, v_mod_b, v_norm1_w, v_norm2_w, v_w_in, v_conv_w, v_conv_b, v_dt_bias, v_a_log, v_ssd_d, v_ssd_norm_w, v_q_a_norm_w, v_w_q_b, v_kv_a_norm_w, v_w_kv_b, v_pool_w, v_pool_scale, v_w_out, v_w_mlp1, v_w_mlp2, v_final_norm_w):
    args = dict(locals())
    B = x.shape[0]
    L = mod_w.shape[0]
    me = _dev_index(*_my_pos())
    CS = mod_w.shape[2]

    big_names = ["w_in", "w_out", "w_q_b", "w_kv_b", "w_mlp1", "w_mlp2"]
    shards = {n: args[n].astype(BF) for n in big_names}
    early, late_names = ["w_in", "w_q_b", "w_kv_b"], ["w_out", "w_mlp1", "w_mlp2"]
    g0 = _all_gather("gather_weights", [c, conv_w] + [shards[n][0] for n in early])
    c_all, convw_all = g0[0], g0[1]
    gathered = {0: dict(zip(early, g0[2:]))}
    cs = jnp.concatenate([c_all.reshape(NDEV * B, D), c_ctx[None], jnp.zeros((24 - NDEV * B - 1, D), F32)], axis=0)
    m_loc = _adaln_fwd(cs, mod_w)
    m_all = _all_gather("gather_mod", [m_loc])[0]
    m_full = jnp.moveaxis(m_all, 0, 2).reshape(L, 24, NDEV * CS) + mod_b[:, None, :]
    pending = {}
    tok = jnp.zeros((), F32)
    res = _exchange_start("gather_start_0", 1, [shards[n][0] for n in late_names], gather=True)
    pending[0] = res[:4]
    tok = tok + res[4][0, 0]
    for l in range(1, L):
        res = _exchange_start("gather_start_%d" % l, 1 + l, [shards[n][l] for n in early + late_names], gather=True)
        pending[l] = res[:4]
        tok = tok + res[4][0, 0]
    mods = []
    for l in range(L):
        ex = lax.dynamic_slice(m_full[l], (me * B, 0), (B, 6 * D)).reshape(B, 6, D)
        cc = jnp.broadcast_to(m_full[l, NDEV * B].reshape(1, 6, D), (B, 6, D))
        mods.append(jnp.pad(jnp.stack([cc, ex], axis=1), ((0, 0), (0, 0), (0, 2), (0, 0))) + tok)

    def full_of(l, xu):
        if l > 0:
            own, lands = _exchange_wait("gather_wait_%d" % l, *pending.pop(l), xu, gather=True)
            gathered[l] = dict(zip(early + late_names, _with_own(lands, own, me)))

        def late(after):
            if l == 0:
                own, lands = _exchange_wait("gather_wait_0", *pending.pop(0), after, gather=True)
                gathered[0].update(zip(late_names, _with_own(lands, own, me)))
            gl = gathered[l]
            return gl["w_out"].reshape(D, D), gl["w_mlp1"], gl["w_mlp2"]

        g = gathered[l]
        return dict(
            w_in=g["w_in"].reshape(D, IN_COLS),
            w_q_b=jnp.moveaxis(g["w_q_b"], 0, 1).reshape(QL, H * QK),
            w_kv_b=jnp.moveaxis(g["w_kv_b"], 0, 1).reshape(QL, H * (NOPE + VH)),
            conv_w=jnp.moveaxis(convw_all[:, l], 0, 1).reshape(4, XBC), late=late)

    def grad_blocks(big, names):
        make = dict(
            w_in=lambda g: g.reshape(NDEV, D // NDEV, IN_COLS), w_out=lambda g: g.reshape(NDEV, D // NDEV, D),
            w_q_b=lambda g: jnp.moveaxis(g.reshape(QL, NDEV, -1), 1, 0),
            w_kv_b=lambda g: jnp.moveaxis(g.reshape(QL, NDEV, -1), 1, 0), w_mlp1=lambda g: g, w_mlp2=lambda g: g)
        return [make[n](big[n]).astype(BF) for n in names]

    sent, sent_mlp = {}, {}
    rest_names = ["w_in", "w_out", "w_q_b", "w_kv_b"]

    def on_mlp(l, g_w1, g_w2):
        if l > 0:
            return None
        res = _exchange_start("grads_mlp_start_0", 1 + 2 * L, [g_w1.astype(BF), g_w2.astype(BF)], gather=False)
        sent_mlp[l] = res[:4]
        return res[4][0, 0]

    def on_grads(l, big, dx):
        if l == 0:
            return grad_blocks(big, rest_names), None
        res = _exchange_start("grads_start_%d" % l, 1 + L + l, grad_blocks(big, big_names), gather=False)
        sent[l] = res[:4]
        return None, res[4][0, 0]

    small_w = {k: args[k] for k in SMALL_NAMES if k != "mod_b"}
    loss_part, grad_x, blocks, smalls, dfnw, dmods = _local_step(x, ctx, loss_target, mods, full_of, small_w, on_grads, on_mlp)
    loss = lax.psum(loss_part, ("x", "y", "c"))

    dm_ex = jnp.stack([dmods[l][:, 1, :6].reshape(B, 6 * D) for l in range(L)])
    dm_cc = jnp.stack([jnp.sum(dmods[l][:, 0, :6], axis=0).reshape(6 * D) for l in range(L)])
    small_parts = dict(
        mod_b=jnp.sum(dm_ex, axis=1) + dm_cc,
        **{k: jnp.stack([smalls[l][k] for l in range(L)]) for k in SMALL_NAMES[1:-1]},
        final_norm_w=dfnw, conv_w=jnp.stack([smalls[l]["conv_w"] for l in range(L)]), dm_cc=dm_cc)
    adam_grads = [small_parts[k] for k in SMALL_NAMES]
    extras = [small_parts["conv_w"], dm_cc, dm_ex]
    pack = jnp.concatenate([_pack(adam_grads), _pack(extras)], axis=0)
    pack_all = _all_gather("gather_small_grads", [pack])[0]
    wpack = _pack([args[k] for k in SMALL_NAMES])
    mpack = _pack([args["m_" + k] for k in SMALL_NAMES])
    vpack = _pack([args["v_" + k] for k in SMALL_NAMES])
    n_adam = wpack.shape[0]
    res_small = _adamw("adamw_small", pack_all[:, :n_adam], wpack, mpack, vpack, n_adam)
    small_out = [_unpack(r, [args[k] for k in SMALL_NAMES]) for r in res_small]
    ext_all = pack_all[:, n_adam:]
    ext_sum = ext_all[0]
    for k in range(1, NDEV):
        ext_sum = ext_sum + ext_all[k]
    g_conv_full, dm_cc_tot, _ = _unpack(ext_sum, extras)
    dm_ex_all = jnp.stack([_unpack(ext_all[k], extras)[2] for k in range(NDEV)], axis=1)
    dm_rows = jnp.concatenate([dm_ex_all.reshape(L, NDEV * B, 6 * D), dm_cc_tot[:, None, :],
                               jnp.zeros((L, 24 - NDEV * B - 1, 6 * D), F32)], axis=1)
    dm_loc = lax.dynamic_slice(dm_rows, (0, 0, me * CS), (L, 24, CS))
    g_modw, gc_part = _adaln_bwd(cs, dm_loc, mod_w)
    gc_all = _all_gather("gather_cctx_grad", [gc_part[NDEV * B:NDEV * B + 8]])[0]
    cpad = lambda a: jnp.pad(a[None], ((0, 7), (0, 0)))
    res_cc = _adamw("adamw_cctx", gc_all, cpad(c_ctx), cpad(m_c_ctx), cpad(v_c_ctx), 8)
    cc_out = [r[0] for r in res_cc]

    def waited(name, handles, names):
        srcs, lands = _exchange_wait(name, *handles, grad_x, gather=False)
        own = [lax.dynamic_index_in_dim(s, me, 0, keepdims=False) for s in srcs]
        return dict(zip(names, _with_own(lands, own, me)))

    recv = {0: dict(zip(rest_names, _all_to_all("exchange_grads", blocks[0])))}
    recv[0].update(waited("grads_mlp_wait_0", sent_mlp.pop(0), ["w_mlp1", "w_mlp2"]))
    for l in range(1, L):
        recv[l] = waited("grads_wait_%d" % l, sent.pop(l), big_names)
    recvs = [jnp.stack([recv[l][n] for l in range(L)], axis=1) for n in big_names]
    big_out = {}
    for name, rv in zip(big_names, recvs):
        w = args[name]
        Rr, C = math.prod(w.shape[:-1]), w.shape[-1]
        rt = Rr if Rr * C <= (1 << 18) else Rr // 8
        res = _adamw("adamw_" + name, rv.reshape(NDEV, Rr, C), w.reshape(Rr, C), args["m_" + name].reshape(Rr, C),
                     args["v_" + name].reshape(Rr, C), rt)
        big_out[name] = [r.reshape(w.shape) for r in res]
    res = _adamw("adamw_mod_w", g_modw.reshape(1, L * D, CS), mod_w.reshape(L * D, CS), m_mod_w.reshape(L * D, CS),
                 v_mod_w.reshape(L * D, CS), L * D // 8)
    big_out["mod_w"] = [r.reshape(mod_w.shape) for r in res]
    CW = conv_w.shape[2]
    g_conv = lax.dynamic_slice(g_conv_full, (0, 0, me * CW), (L, 4, CW))
    res = _adamw("adamw_conv_w", g_conv.reshape(1, L * 4, CW), conv_w.reshape(L * 4, CW), m_conv_w.reshape(L * 4, CW),
                 v_conv_w.reshape(L * 4, CW), L * 4)
    big_out["conv_w"] = [r.reshape(conv_w.shape) for r in res]

    weights = ["c_ctx", "mod_w", "mod_b", "norm1_w", "norm2_w", "w_in", "conv_w", "conv_b", "dt_bias", "a_log", "ssd_d",
               "ssd_norm_w", "q_a_norm_w", "w_q_b", "kv_a_norm_w", "w_kv_b", "pool_w", "pool_scale", "w_out", "w_mlp1",
               "w_mlp2", "final_norm_w"]
    outs = [loss, grad_x]
    for kind in range(4):
        for name in weights:
            if name == "c_ctx":
                outs.append(cc_out[kind])
            elif name in big_out:
                outs.append(big_out[name][kind])
            else:
                outs.append(small_out[kind][SMALL_NAMES.index(name)])
    return tuple(outs)
```

```python
import functools
import math

import jax
import jax.numpy as jnp
from jax import lax
from jax.experimental import pallas as pl
from jax.experimental.pallas import tpu as pltpu

F32 = jnp.float32
BF = jnp.bfloat16
MXU = BF

D = 1024
CTX = 256
GRID_W = 64
EPS = 1e-6
H = 6
P = 64
SSD_IN = 384
NST = 128
XBC = 896
CH = 128
QL = 256
NOPE = 64
ROPE = 32
VH = 64
QK = 96
HP = 128
QW = H * HP
PD = 256
FF = 4096
IN_COLS = 2092
ROPE_THETA = 10000.0
PC = 2304
A_Z, A_XBC, A_QA, A_KVA, A_POOL, A_KR, A_DT = 0, 384, 1280, 1536, 1792, 2048, 2176
MIXW = SSD_IN + QW + PD
NDEV = 8
FSH = FF // NDEV
TM = 256
TQ = 256
TK = 256
VMEM_CAP = 64 * 1024 * 1024
ADAM_LR, ADAM_B1, ADAM_B2, ADAM_EPS, ADAM_WD, ADAM_STEP = 0.001, 0.9, 0.999, 1e-08, 0.01, 10


def _nbytes(shape, dtype):
    n = 1
    for s in shape:
        if s is not None:
            n *= s
    return n * jnp.dtype(dtype).itemsize


def _params(sem, block_bytes, extra=0):
    lim = min(2 * block_bytes + extra + (8 << 20), VMEM_CAP - (6 << 20))
    return pltpu.CompilerParams(dimension_semantics=sem, vmem_limit_bytes=int(lim))


def _hbm(arrays):
    return [pltpu.with_memory_space_constraint(a, pltpu.HBM) for a in arrays]


def _dot(a, b):
    return jnp.dot(a.astype(MXU), b.astype(MXU), preferred_element_type=F32)


def _dot_nt(a, b):
    return lax.dot_general(a.astype(MXU), b.astype(MXU), (((1,), (1,)), ((), ())), preferred_element_type=F32)


def _dot_tn(a, b):
    return lax.dot_general(a.astype(MXU), b.astype(MXU), (((0,), (0,)), ((), ())), preferred_element_type=F32)


def _dot01(m01, x):
    b16 = jnp.bfloat16
    m = m01.astype(b16)
    hi = x.astype(b16)
    r1 = x - hi.astype(F32)
    mid = r1.astype(b16)
    lo = (r1 - mid.astype(F32)).astype(b16)
    f = lambda v: jnp.dot(m, v, preferred_element_type=F32)
    return f(hi) + f(mid) + f(lo)


def _sigmoid(x):
    return 1.0 / (1.0 + jnp.exp(-x))


def _silu(x):
    return x * _sigmoid(x)


def _dsilu(x):
    s = _sigmoid(x)
    return s * (1.0 + x * (1.0 - s))


def _iota(shape, dim):
    return lax.broadcasted_iota(jnp.int32, shape, dim)


def _row(ref, k):
    blk = ref[...]
    return jnp.sum(jnp.where(_iota(blk.shape, 0) == k, blk, 0.0), axis=0, keepdims=True)


def _shift_rows(x, k):
    n = x.shape[0]
    return pltpu.roll(x, (-k) % n, axis=0)


class _Rows:
    def __init__(self, B, T):
        self.B, self.T = B, T
        self.nt = T // TM
        self.nct = CTX // TM

    def row(self, F):
        return pl.BlockSpec((None, TM, F), lambda b, i: (b, i, 0))

    def row2(self, F):
        return pl.BlockSpec((None, 2, TM, F), lambda b, i: (b, 0, i, 0))

    def prev8(self, F):
        return pl.BlockSpec((None, 8, F), lambda b, i: (b, jnp.maximum(i * (TM // 8) - 1, 0), 0))

    def next8(self, F):
        last = self.T // 8 - 1
        return pl.BlockSpec((None, 8, F), lambda b, i: (b, jnp.minimum((i + 1) * (TM // 8), last), 0))

    def mod(self):
        nct = self.nct
        return pl.BlockSpec((None, None, 8, D), lambda b, i: (b, jnp.where(i < nct, 0, 1), 0, 0))

    def const(self, shape):
        z = (0,) * len(shape)
        return pl.BlockSpec(tuple(shape), lambda b, i: z)

    def tgt(self, F):
        nct = self.nct
        return pl.BlockSpec((None, TM, F), lambda b, i: (b, jnp.maximum(i - nct, 0), 0))

    def call(self, name, body, ins, outs, scratch=(), extra=0):
        arrays = [a for a, _ in ins]
        in_specs = [s for _, s in ins]
        out_shape = [pltpu.HBM(o.shape, o.dtype) for o, _ in outs]
        out_specs = [s for _, s in outs]
        bb = 0
        for a, s in list(ins) + list(outs):
            if s.block_shape is not None:
                bb += _nbytes(s.block_shape, a.dtype)
        return pl.pallas_call(
            functools.partial(body), name=name, grid=(self.B, self.nt),
            in_specs=in_specs, out_specs=out_specs, out_shape=out_shape, scratch_shapes=list(scratch),
            compiler_params=_params(("arbitrary", "arbitrary"), bb, extra),
        )(*_hbm(arrays))

    def first_of_stream(self, i):
        return jnp.logical_or(i == 0, i == self.nct)

    def last_of_stream(self, i):
        return jnp.logical_or(i == self.nct - 1, i == self.nt - 1)

    def ext(self, i, prev_ref, cur, next_ref):
        pv = prev_ref[...].astype(F32) * jnp.where(self.first_of_stream(i), 0.0, 1.0)
        nx = next_ref[...].astype(F32) * jnp.where(self.last_of_stream(i), 0.0, 1.0)
        return jnp.concatenate([pv, cur, nx], axis=0)

    def stream_pos(self, i, rows):
        start = jnp.where(i < self.nct, 0, CTX)
        n = jnp.where(i < self.nct, CTX, self.T - CTX)
        t = i * TM - 8 - start + _iota((rows, 1), 0)
        return t, n


def _sds(shape, dtype):
    return jax.ShapeDtypeStruct(tuple(shape), dtype)


def _out(shape, dtype):
    return pltpu.HBM(tuple(shape), dtype)


def _norm_mod(x, nw, sh, sc):
    r = lax.rsqrt(jnp.mean(x * x, axis=-1, keepdims=True) + EPS)
    xn = x * r
    return xn * nw * (1.0 + sc) + sh, xn, r


def _norm_mod_bwd(dh, xn, r, nw, sc):
    dsh = jnp.sum(dh, axis=0, keepdims=True)
    dsc = jnp.sum(dh * (xn * nw), axis=0, keepdims=True)
    dnw = jnp.sum(dh * (1.0 + sc) * xn, axis=0, keepdims=True)
    dxn = dh * nw * (1.0 + sc)
    dx = r * (dxn - xn * jnp.mean(dxn * xn, axis=-1, keepdims=True))
    return dx, dsh, dsc, dnw


def _acc_rows(ref, first, rows):
    rid = _iota(ref.shape, 0)
    upd = jnp.zeros(ref.shape, F32)
    for k, v in rows.items():
        upd = upd + jnp.where(rid == k, v, 0.0)

    @pl.when(first)
    def _():
        ref[...] = upd

    @pl.when(jnp.logical_not(first))
    def _():
        ref[...] += upd


def _in_proj(R, x, mod, nw1, w_arr):
    B, T = R.B, R.T

    def body(x_ref, mod_ref, nw_ref, w_ref, z_ref, xbc_ref, qa_ref, kva_ref, pool_ref, kr_ref, dt_ref, h_ref):
        h, _, _ = _norm_mod(x_ref[...], nw_ref[...], _row(mod_ref, 0), _row(mod_ref, 1))
        hb = h.astype(BF)
        h_ref[...] = hb
        p = jnp.dot(hb, w_ref[...], preferred_element_type=F32)
        z_ref[...] = p[:, A_Z:A_XBC]
        xbc_ref[...] = p[:, A_XBC:A_QA]
        qa_ref[...] = p[:, A_QA:A_KVA]
        kva_ref[...] = p[:, A_KVA:A_POOL]
        pool_ref[...] = p[:, A_POOL:A_KR]
        kr_ref[...] = p[:, A_KR:A_DT]
        dt_ref[...] = p[:, A_DT:PC]

    widths = [SSD_IN, XBC, QL, QL, PD, HP, HP]
    outs = [(_sds((B, T, w), F32), R.row(w)) for w in widths] + [(_sds((B, T, D), BF), R.row(D))]
    return R.call("in_proj", body,
                  [(x, R.row(D)), (mod, R.mod()), (nw1, R.const((1, D))), (w_arr, R.const((D, PC)))],
                  outs, extra=8 << 20)


def _conv_pre(ext, w_ref, b_ref):
    return (_row(w_ref, 0) * _shift_rows(ext, -1)[8:8 + TM] + _row(w_ref, 1) * ext[8:8 + TM]
            + _row(w_ref, 2) * _shift_rows(ext, 1)[8:8 + TM] + _row(w_ref, 3) * _shift_rows(ext, 2)[8:8 + TM]
            + b_ref[...])


def _softplus(x):
    return jnp.maximum(x, 0.0) + jnp.log(1.0 + jnp.exp(-jnp.abs(x)))


def _ssd_prep(R, xbc_raw, dt_raw, conv_w8, conv_b, dtb):
    B, T = R.B, R.T

    def body(raw_ref, pv_ref, nx_ref, dtr_ref, w_ref, b_ref, dtb_ref, xbc_ref, dt_ref):
        i = pl.program_id(1)
        ext = R.ext(i, pv_ref, raw_ref[...], nx_ref)
        xbc_ref[...] = _silu(_conv_pre(ext, w_ref, b_ref))
        lane = _iota((TM, HP), 1)
        dtv = _softplus(dtr_ref[...] + dtb_ref[...])
        keep = lane < H
        dt_ref[0] = jnp.where(keep, dtv, 0.0)
        dt_ref[1] = jnp.where(keep, pltpu.roll(dtv, HP - H, axis=1), 0.0)

    return R.call("ssd_prep", body,
                  [(xbc_raw, R.row(XBC)), (xbc_raw, R.prev8(XBC)), (xbc_raw, R.next8(XBC)), (dt_raw, R.row(HP)),
                   (conv_w8, R.const((8, XBC))), (conv_b, R.const((1, XBC))), (dtb, R.const((1, HP)))],
                  [(_sds((B, T, XBC), F32), R.row(XBC)), (_sds((B, 2, T, HP), F32), R.row2(HP))], extra=12 << 20)


def _chunk_index(d, s, nc, nctc):
    if d == 0:
        return s
    return jnp.where(s < nctc, nctc - 1 - s, nc - 1 - (s - nctc))


def _dot01_r(x, m01, passes=3):
    b16 = jnp.bfloat16
    m = m01.astype(b16)
    out, rest = None, x
    for _ in range(passes):
        part = rest.astype(b16)
        rest = rest - part.astype(F32)
        term = jnp.dot(part, m, preferred_element_type=F32)
        out = term if out is None else out + term
    return out


def _scan_common(d, dtv, alog_ref):
    sel = _iota((HP, H * HP), 0) == (_iota((HP, H * HP), 1) >> 7)
    a_all = _dot01_r(-jnp.exp(alog_ref[d]), sel)
    dt_all = _dot01_r(dtv, sel)
    adt_all = dt_all * jnp.concatenate([a_all] * (CH // 8), axis=0)
    row = _iota((CH, CH), 0)
    col = _iota((CH, CH), 1)
    inc = col <= row if d == 0 else col >= row
    inc_t = row <= col if d == 0 else row >= col
    q_all = _dot01(inc, adt_all)
    return a_all, dt_all, adt_all, q_all, inc, inc_t


def _head_decay(h, adt_all, q_all, inc, inc_t):
    q = q_all[:, HP * h:HP * (h + 1)]
    q_t = q.T
    qtot = jnp.sum(adt_all[:, HP * h:HP * (h + 1)], axis=0, keepdims=True)
    lm = jnp.where(inc, jnp.exp(q - q_t), 0.0)
    lm_t = jnp.where(inc_t, jnp.exp(q_t - q), 0.0)
    return lm, lm_t, jnp.exp(q), jnp.exp(qtot), jnp.exp(qtot - q)


def _ssd_scan(B, T, xbc, dt2, alog2):
    nc, nctc = T // CH, CTX // CH

    def chain(d, xbc_ref, dt_ref, alog_ref, y_ref, hin_ref, hs):
        xbc_v = xbc_ref[...]
        dtv = dt_ref[...]
        blk = alog_ref[d]
        adt = dtv * -jnp.exp(jnp.sum(jnp.where(_iota(blk.shape, 0) == 0, blk, 0.0), axis=0, keepdims=True))
        row, col = _iota((CH, CH), 0), _iota((CH, CH), 1)
        inc = col <= row if d == 0 else col >= row
        inc_tf = (row <= col if d == 0 else row >= col).astype(F32)
        q = _dot01(inc, adt)
        hin_ref[...] = hs[d]
        for g in range(2):
            bg = xbc_v[:, SSD_IN + NST * g:SSD_IN + NST * (g + 1)]
            cg = xbc_v[:, SSD_IN + 2 * NST + NST * g:SSD_IN + 2 * NST + NST * (g + 1)]
            gm = _dot_nt(cg, bg)
            for r in range(3):
                h = 3 * g + r
                onehot = (_iota((1, HP), 1) == h).astype(F32)
                adt_h = jnp.sum(adt * onehot, axis=1, keepdims=True)
                qc = jnp.sum(q * onehot, axis=1, keepdims=True)
                dt_h = jnp.sum(dtv * onehot, axis=1, keepdims=True)
                qr = jnp.sum(adt_h * inc_tf, axis=0, keepdims=True)
                qtot = jnp.sum(adt_h, axis=0, keepdims=True)
                lm = jnp.where(inc, jnp.exp(qc - qr), 0.0)
                xh = xbc_v[:, P * h:P * (h + 1)] * dt_h
                hprev = hs[d, P * h:P * (h + 1), :]
                y_ref[:, P * h:P * (h + 1)] = _dot(gm * lm, xh) + jnp.exp(qc) * _dot_nt(cg, hprev)
                hs[d, P * h:P * (h + 1), :] = jnp.exp(qtot) * hprev + _dot_tn(xh * jnp.exp(qtot - qc), bg)

    def body(xf_ref, xb_ref, dtf_ref, dtb_ref, alog_ref, yf_ref, yb_ref, hf_ref, hb_ref, hs):
        @pl.when(pl.program_id(1) == 0)
        def _():
            hs[...] = jnp.zeros(hs.shape, F32)

        chain(0, xf_ref, dtf_ref, alog_ref, yf_ref, hf_ref, hs)
        chain(1, xb_ref, dtb_ref, alog_ref, yb_ref, hb_ref, hs)

    cidx = lambda d: (lambda s: _chunk_index(d, s, nc, nctc))
    specs = lambda d: dict(
        xbc=pl.BlockSpec((None, CH, XBC), lambda b, s: (b, cidx(d)(s), 0)),
        dt=pl.BlockSpec((None, None, CH, HP), lambda b, s: (b, d, cidx(d)(s), 0)),
        y=pl.BlockSpec((None, CH, SSD_IN), lambda b, s: (b, cidx(d)(s), 0)),
        h=pl.BlockSpec((None, None, SSD_IN, NST), lambda b, s: (b, cidx(d)(s), 0, 0)))
    f, r = specs(0), specs(1)
    bb = 2 * (_nbytes((CH, XBC), F32) + _nbytes((CH, HP), F32) + _nbytes((CH, SSD_IN), F32) + _nbytes((SSD_IN, NST), F32))
    return pl.pallas_call(
        functools.partial(body), name="ssd_scan", grid=(B, nc),
        in_specs=[f["xbc"], r["xbc"], f["dt"], r["dt"], pl.BlockSpec((2, 8, HP), lambda b, s: (0, 0, 0))],
        out_specs=[f["y"], r["y"], f["h"], r["h"]],
        out_shape=[_out((B, T, SSD_IN), F32)] * 2 + [_out((B, nc, SSD_IN, NST), F32)] * 2,
        scratch_shapes=[pltpu.VMEM((2, SSD_IN, NST), F32)],
        compiler_params=_params(("arbitrary",) * 2, bb, 12 << 20),
    )(*_hbm([xbc, xbc, dt2, dt2, alog2]))


def _swap8(u):
    lane = _iota(u.shape, 1)
    n = u.shape[1]
    return jnp.where((lane & 15) < 8, pltpu.roll(u, n - 8, axis=1), pltpu.roll(u, 8, axis=1))


def _rope(u, cos, sin_signed):
    return u * cos + _swap8(u) * sin_signed


def _rms(x, w):
    r = lax.rsqrt(jnp.mean(x * x, axis=-1, keepdims=True) + EPS)
    xh = x * r
    return xh * w, xh, r


def _rms_bwd(dy, xh, r, w):
    dw = jnp.sum(dy * xh, axis=0, keepdims=True)
    dxh = dy * w
    return r * (dxh - xh * jnp.mean(dxh * xh, axis=-1, keepdims=True)), dw


def _tile6(t):
    return jnp.concatenate([t] * H, axis=1)


def _per_head(fn, u):
    return jnp.concatenate([fn(u[:, HP * h:HP * (h + 1)]) for h in range(H)], axis=1)


def _mla_prep(R, qa, kva, kr, qnw, kvnw, wq, wk, wv, cos, sin):
    B, T = R.B, R.T
    scale = QK ** -0.5

    def body(qa_ref, kva_ref, kr_ref, qnw_ref, kvnw_ref, wq_ref, wk_ref, wv_ref, cos_ref, sin_ref,
             qt_ref, kt_ref, qtr_ref, ktr_ref, vtr_ref, cq_ref, ckv_ref):
        cq, _, _ = _rms(qa_ref[...], qnw_ref[...])
        ckv, _, _ = _rms(kva_ref[...], kvnw_ref[...])
        cqb, ckvb = cq.astype(BF), ckv.astype(BF)
        cq_ref[...] = cqb
        ckv_ref[...] = ckvb
        cos1, sin1 = cos_ref[...], sin_ref[...]
        q = _per_head(lambda u: _rope(u, cos1, sin1), jnp.dot(cqb, wq_ref[...], preferred_element_type=F32)) * scale
        qt_ref[...] = q.astype(BF)
        kk = _rope(kr_ref[...], cos1, sin1)
        hl = _iota((TM, QW), 1) & (HP - 1)
        k = jnp.dot(ckvb, wk_ref[...], preferred_element_type=F32) + _tile6(kk)
        k = jnp.where(jnp.logical_and(hl >= QK, hl < QK + 3), 1.0, k)
        kt_ref[...] = k.astype(BF)
        v = jnp.dot(ckvb, wv_ref[...], preferred_element_type=F32)
        v = jnp.where(jnp.logical_and(hl >= VH, hl < VH + 4), 1.0, v)
        for h in range(H):
            cols = slice(HP * h, HP * (h + 1))
            qtr_ref[h] = q[:, cols].T.astype(BF)
            ktr_ref[h] = k[:, cols].T.astype(BF)
            vtr_ref[h] = v[:, cols].T.astype(BF)

    tr = (_sds((B, H, HP, T), BF), pl.BlockSpec((None, H, HP, TM), lambda b, i: (b, 0, 0, i)))
    return R.call("mla_prep", body,
                  [(qa, R.row(QL)), (kva, R.row(QL)), (kr, R.row(HP)), (qnw, R.const((1, QL))), (kvnw, R.const((1, QL))),
                   (wq, R.const((QL, QW))), (wk, R.const((QL, QW))), (wv, R.const((QL, QW))),
                   (cos, pl.BlockSpec((TM, HP), lambda b, i: (i, 0))), (sin, pl.BlockSpec((TM, HP), lambda b, i: (i, 0)))],
                  [(_sds((B, T, QW), BF), R.row(QW))] * 2 + [tr] * 3 + [(_sds((B, T, QL), BF), R.row(QL))] * 2, extra=12 << 20)


def _flash_fwd(B, T, q_t, kt, v_t):
    nq, nk = T // TQ, T // TK
    HS = 2

    def body(q_ref, k_ref, v_ref, o_ref, lse_ref, s_scr):
        i = pl.program_id(2)

        def attend(nch):
            ms = []
            for hh in range(HS):
                q_tr = q_ref[hh]
                mrun = None
                for j in range(nch):
                    s = _dot(k_ref[TK * j:TK * (j + 1), HP * hh:HP * (hh + 1)], q_tr)
                    s_scr[hh, j] = s
                    mrun = s if mrun is None else jnp.maximum(mrun, s)
                ms.append(jnp.max(mrun, axis=0, keepdims=True))
            row = _iota((HP, TQ), 0)
            for hh in range(HS):
                acc_t = jnp.zeros((HP, TQ), F32)
                for j in range(nch):
                    acc_t = acc_t + _dot(v_ref[hh, :, TK * j:TK * (j + 1)], jnp.exp(s_scr[hh, j] - ms[hh]))
                l = jnp.sum(jnp.where(row == VH, acc_t, 0.0), axis=0, keepdims=True)
                o_ref[:, HP * hh:HP * (hh + 1)] = jnp.where(row < VH, acc_t / l, 0.0).T
                lse_ref[hh] = ms[hh] + jnp.log(l)

        @pl.when(i < CTX // TQ)
        def _():
            attend(CTX // TK)

        @pl.when(i >= CTX // TQ)
        def _():
            attend(nk)

    bb = HS * (_nbytes((TQ, HP), BF) + 2 * _nbytes((T, HP), BF) + 2 * _nbytes((TQ, HP), F32))
    return pl.pallas_call(
        functools.partial(body), name="flash_fwd", grid=(B, H // HS, nq),
        in_specs=[pl.BlockSpec((None, HS, HP, TQ), lambda b, h, i: (b, h, 0, i)),
                  pl.BlockSpec((None, T, HS * HP), lambda b, h, i: (b, 0, h)),
                  pl.BlockSpec((None, HS, HP, T), lambda b, h, i: (b, h, 0, 0))],
        out_specs=[pl.BlockSpec((None, TQ, HS * HP), lambda b, h, i: (b, i, h)),
                   pl.BlockSpec((None, HS, 1, TQ), lambda b, h, i: (b, h, 0, i))],
        out_shape=[_out((B, T, QW), F32), _out((B, H, 1, T), F32)],
        scratch_shapes=[pltpu.VMEM((HS, nk, TK, TQ), F32)],
        compiler_params=_params(("arbitrary",) * 3, bb, _nbytes((HS, nk, TK, TQ), F32) + (8 << 20)),
    )(*_hbm([q_t, kt, v_t]))


def _pool_terms(R, i, rows):
    t, n = R.stream_pos(i, rows)
    lane = _iota((1, PD), 1)
    half = jnp.where(lane < 64, 1, jnp.where(lane < 128, 2, jnp.where(lane < 192, 4, 8)))
    cnt = (jnp.minimum(t + half, n) - jnp.maximum(t - half, 0)).astype(F32)
    valid = jnp.logical_and(t >= 0, t < n)
    return jnp.where(valid, cnt, 1.0), valid.astype(F32), lane


def _lane_select(lane, a2, a4, a8, a16):
    return jnp.where(lane < 64, a2, jnp.where(lane < 128, a4, jnp.where(lane < 192, a8, a16)))


def _pool_centred(R, i, ext):
    cnt, valid, lane = _pool_terms(R, i, ext.shape[0])
    s2 = ext + _shift_rows(ext, -1)
    s4 = _shift_rows(s2, -1) + _shift_rows(s2, 1)
    s8 = _shift_rows(s4, -2) + _shift_rows(s4, 2)
    s16 = _shift_rows(s8, -4) + _shift_rows(s8, 4)
    return _lane_select(lane, s2, s4, s8, s16) / cnt - ext, cnt, valid, lane


def _pool_fwd(R, u, wbd, scale):
    B, T = R.B, R.T

    def body(u_ref, pv_ref, nx_ref, w_ref, sc_ref, o_ref):
        i = pl.program_id(1)
        ext = R.ext(i, pv_ref, u_ref[...], nx_ref)
        dm, _, _, _ = _pool_centred(R, i, ext)
        o_ref[...] = _dot(dm[8:8 + TM], w_ref[...]) * sc_ref[...]

    return R.call("pool_fwd", body,
                  [(u, R.row(PD)), (u, R.prev8(PD)), (u, R.next8(PD)), (wbd, R.const((PD, PD))), (scale, R.const((1, PD)))],
                  [(_sds((B, T, PD), F32), R.row(PD))], extra=8 << 20)


def _group_mask():
    return _iota((1, SSD_IN), 1) < SSD_IN // 2


def _ssd_gate(yf_ref, yb_ref, xbc_ref, z_ref, dsk_ref):
    ytot = yf_ref[...] + yb_ref[...] + xbc_ref[:, 0:SSD_IN] * dsk_ref[...]
    z = z_ref[...]
    gz = ytot * _silu(z)
    g0 = _group_mask()
    sq = gz * gz
    s0 = jnp.sum(jnp.where(g0, sq, 0.0), axis=1, keepdims=True)
    s1 = jnp.sum(jnp.where(g0, 0.0, sq), axis=1, keepdims=True)
    half = SSD_IN // 2
    r = jnp.where(g0, lax.rsqrt(s0 / half + EPS), lax.rsqrt(s1 / half + EPS))
    return ytot, z, gz, r


def _out_proj(R, x, mod, yf, yb, xbc, z, o, pool, dsk, snw, wout):
    B, T = R.B, R.T

    def body(x_ref, mod_ref, yf_ref, yb_ref, xbc_ref, z_ref, o_ref, pool_ref, dsk_ref, snw_ref, w_ref, xmid_ref, cat_ref,
             mix_ref):
        _, _, gz, r = _ssd_gate(yf_ref, yb_ref, xbc_ref, z_ref, dsk_ref)
        cat_ref[:, 0:SSD_IN] = (gz * r * snw_ref[...]).astype(BF)
        cat_ref[:, SSD_IN:SSD_IN + QW] = o_ref[...].astype(BF)
        cat_ref[:, SSD_IN + QW:MIXW] = pool_ref[...].astype(BF)
        mix = jnp.dot(cat_ref[...], w_ref[...], preferred_element_type=F32)
        mix_ref[...] = mix.astype(BF)
        xmid_ref[...] = x_ref[...] + _row(mod_ref, 2) * mix

    return R.call("out_proj", body,
                  [(x, R.row(D)), (mod, R.mod()), (yf, R.row(SSD_IN)), (yb, R.row(SSD_IN)), (xbc, R.row(XBC)), (z, R.row(SSD_IN)),
                   (o, R.row(QW)),
                   (pool, R.row(PD)), (dsk, R.const((1, SSD_IN))), (snw, R.const((1, SSD_IN))), (wout, R.const((MIXW, D)))],
                  [(_sds((B, T, D), F32), R.row(D)), (_sds((B, T, MIXW), BF), R.row(MIXW)), (_sds((B, T, D), BF), R.row(D))],
                  extra=8 << 20)


def _load_once(first, pairs, sem):
    @pl.when(first)
    def _():
        cps = [pltpu.make_async_copy(src, dst, sem.at[k]) for k, (src, dst) in enumerate(pairs)]
        for cp in cps:
            cp.start()
        for cp in cps:
            cp.wait()


ANY = pl.BlockSpec(memory_space=pl.ANY)


def _mlp_fwd(R, xmid, mod, nw2, w1, w2):
    B, T = R.B, R.T

    def body(x_ref, mod_ref, nw_ref, w1_hbm, w2_hbm, xo_ref, h_ref, u_ref, y_ref, w1_v, w2_v, sem):
        first = jnp.logical_and(pl.program_id(0) == 0, pl.program_id(1) == 0)
        _load_once(first, [(w1_hbm, w1_v), (w2_hbm, w2_v)], sem)
        x = x_ref[...]
        h, _, _ = _norm_mod(x, nw_ref[...], _row(mod_ref, 3), _row(mod_ref, 4))
        hb = h.astype(BF)
        h_ref[...] = hb
        y = jnp.zeros((TM, D), F32)
        for j in range(NDEV):
            u = jnp.dot(hb, w1_v[j], preferred_element_type=F32)
            u_ref[:, FSH * j:FSH * (j + 1)] = u.astype(BF)
            a = jnp.square(jnp.maximum(u, 0.0))
            y = y + jnp.dot(a.astype(BF), w2_v[j], preferred_element_type=F32)
        y_ref[...] = y.astype(BF)
        xo_ref[...] = x + _row(mod_ref, 5) * y

    return R.call("mlp_fwd", body,
                  [(xmid, R.row(D)), (mod, R.mod()), (nw2, R.const((1, D))), (w1, ANY), (w2, ANY)],
                  [(_sds((B, T, D), F32), R.row(D)), (_sds((B, T, D), BF), R.row(D)), (_sds((B, T, FF), BF), R.row(FF)),
                   (_sds((B, T, D), BF), R.row(D))],
                  scratch=[pltpu.VMEM((NDEV, D, FSH), w1.dtype), pltpu.VMEM((NDEV, FSH, D), w2.dtype), pltpu.SemaphoreType.DMA((2,))],
                  extra=(2 * _nbytes((NDEV, D, FSH), BF)) + (8 << 20))


def _loss_head(R, x, tgt, fnw):
    B, T = R.B, R.T

    def body(x_ref, t_ref, w_ref, dx_ref, loss_ref, dw_ref):
        b, i = pl.program_id(0), pl.program_id(1)
        live = jnp.where(i >= R.nct, 1.0, 0.0)
        y, xh, r = _rms(x_ref[...], w_ref[...])
        err = (y - t_ref[...]) * live
        dy = err / D
        dxn, dw = _rms_bwd(dy, xh, r, w_ref[...])
        dx_ref[...] = dxn
        first = jnp.logical_and(b == 0, i == 0)
        part = 0.5 * jnp.sum(jnp.sum(err * err, axis=1, keepdims=True), axis=0, keepdims=True) / D
        _acc_rows(loss_ref, first, {0: jnp.broadcast_to(part, (1, HP))})
        _acc_rows(dw_ref, first, {0: dw})

    return R.call("loss_head", body,
                  [(x, R.row(D)), (tgt, R.tgt(D)), (fnw, R.const((1, D)))],
                  [(_sds((B, T, D), F32), R.row(D)), (_sds((8, HP), F32), R.const((8, HP))), (_sds((8, D), F32), R.const((8, D)))],
                  extra=8 << 20)


def _tn_matmul(name, a, b, tn, sub=1):
    B, T, K = a.shape
    N = b.shape[2]
    nk = 1
    while T % nk or (T // nk) > 1088 or (T // nk) % 16:
        nk += 1
    tk = T // nk
    kt = K if K <= 1536 else 1024
    w = tn // sub
    assert K % kt == 0 and N % tn == 0 and tn % sub == 0

    def body(a_ref, b_ref, o_ref):
        first = jnp.logical_and(pl.program_id(2) == 0, pl.program_id(3) == 0)

        @pl.when(first)
        def _():
            o_ref[...] = jnp.zeros(o_ref.shape, F32)

        acc = _dot_tn(a_ref[...], b_ref[...])
        for s in range(sub):
            o_ref[s] += acc[:, w * s:w * (s + 1)]

    bb = _nbytes((tk, kt), a.dtype) + _nbytes((tk, tn), b.dtype) + _nbytes((kt, tn), F32)
    return pl.pallas_call(
        functools.partial(body), name=name, grid=(N // tn, K // kt, B, nk),
        in_specs=[pl.BlockSpec((None, tk, kt), lambda j, kk, bi, t: (bi, t, kk)),
                  pl.BlockSpec((None, tk, tn), lambda j, kk, bi, t: (bi, t, j))],
        out_specs=pl.BlockSpec((sub, kt, w), lambda j, kk, bi, t: (j, kk, 0)),
        out_shape=_out((sub * N // tn, K, w), F32),
        compiler_params=_params(("arbitrary",) * 4, bb, _nbytes((kt, tn), F32) + (8 << 20)),
    )(*_hbm([a, b]))


def _mlp_bwd(R, dxo, xmid, ubf, ybf, mod, nw2, w1, w2):
    B, T = R.B, R.T

    def body(dxo_ref, x_ref, u_ref, y_ref, mod_ref, nw_ref, w1_hbm, w2_hbm,
             dxm_ref, du_ref, a_ref, dy_ref, dmod_ref, dnw_ref, w1_v, w2_v, sem):
        b, i = pl.program_id(0), pl.program_id(1)
        _load_once(jnp.logical_and(b == 0, i == 0), [(w1_hbm, w1_v), (w2_hbm, w2_v)], sem)
        dxo = dxo_ref[...]
        _, xn, r = _norm_mod(x_ref[...], nw_ref[...], _row(mod_ref, 3), _row(mod_ref, 4))
        dyb = (dxo * _row(mod_ref, 5)).astype(BF)
        dy_ref[...] = dyb
        dg2 = jnp.sum(dxo * y_ref[...].astype(F32), axis=0, keepdims=True)
        dh = jnp.zeros((TM, D), F32)
        for j in range(NDEV):
            rl = jnp.maximum(u_ref[:, FSH * j:FSH * (j + 1)].astype(F32), 0.0)
            a_ref[:, FSH * j:FSH * (j + 1)] = (rl * rl).astype(BF)
            du = (_dot_nt(dyb, w2_v[j]) * (2.0 * rl)).astype(BF)
            du_ref[:, FSH * j:FSH * (j + 1)] = du
            dh = dh + _dot_nt(du, w1_v[j])
        dx, dsh, dsc, dnw = _norm_mod_bwd(dh, xn, r, nw_ref[...], _row(mod_ref, 4))
        dxm_ref[...] = dxo + dx
        _acc_rows(dmod_ref, R.first_of_stream(i), {3: dsh, 4: dsc, 5: dg2})
        _acc_rows(dnw_ref, jnp.logical_and(b == 0, i == 0), {0: dnw})

    return R.call("mlp_bwd", body,
                  [(dxo, R.row(D)), (xmid, R.row(D)), (ubf, R.row(FF)), (ybf, R.row(D)), (mod, R.mod()), (nw2, R.const((1, D))),
                   (w1, ANY), (w2, ANY)],
                  [(_sds((B, T, D), F32), R.row(D)), (_sds((B, T, FF), BF), R.row(FF)), (_sds((B, T, FF), BF), R.row(FF)),
                   (_sds((B, T, D), BF), R.row(D)), (_sds((B, 2, 8, D), F32), R.mod()), (_sds((8, D), F32), R.const((8, D)))],
                  scratch=[pltpu.VMEM((NDEV, D, FSH), w1.dtype), pltpu.VMEM((NDEV, FSH, D), w2.dtype), pltpu.SemaphoreType.DMA((2,))],
                  extra=(2 * _nbytes((NDEV, D, FSH), BF)) + (8 << 20))


def _minus_in_lanes(x, col, first):
    b16 = jnp.bfloat16
    hi = col.astype(b16).astype(F32)
    r1 = col - hi
    mid = r1.astype(b16).astype(F32)
    lane = _iota(x.shape, 1)
    return jnp.where(lane == first, -hi, jnp.where(lane == first + 1, -mid, jnp.where(lane == first + 2, mid - r1, x)))


def _out_bwd(R, dxm, mod, mixbf, yf, yb, xbc, z, o, qt, lse, dsk, snw, wout):
    B, T = R.B, R.T

    def body(dxm_ref, mod_ref, mix_ref, yf_ref, yb_ref, xbc_ref, z_ref, o_ref, q_ref, lse_ref, dsk_ref, snw_ref, w_ref,
             dmix_ref, dz_ref, dyt_ref, dxsk_ref, qs_ref, dos_ref, dotr_ref, dpo_ref, dmod_ref, dvec_ref):
        b, i = pl.program_id(0), pl.program_id(1)
        dxm = dxm_ref[...]
        dmixb = (dxm * _row(mod_ref, 2)).astype(BF)
        dmix_ref[...] = dmixb
        dg1 = jnp.sum(dxm * mix_ref[...].astype(F32), axis=0, keepdims=True)
        dcat = _dot_nt(dmixb, w_ref[...])
        do_v = dcat[:, SSD_IN:SSD_IN + QW]
        doo = do_v * o_ref[...]
        for h in range(H):
            cols = slice(HP * h, HP * (h + 1))
            dd = jnp.sum(doo[:, cols], axis=1, keepdims=True)
            dos_ref[:, cols] = _minus_in_lanes(do_v[:, cols], dd, VH + 1).astype(BF)
            qs_ref[:, cols] = _minus_in_lanes(q_ref[:, cols].astype(F32), lse_ref[h], QK).astype(BF)
            dotr_ref[h] = do_v[:, cols].T.astype(BF)
        dpo_ref[...] = dcat[:, SSD_IN + QW:MIXW]
        dsn = dcat[:, 0:SSD_IN]
        ytot, zv, gz, r = _ssd_gate(yf_ref, yb_ref, xbc_ref, z_ref, dsk_ref)
        gh = gz * r
        dsnw = jnp.sum(dsn * gh, axis=0, keepdims=True)
        dgh = dsn * snw_ref[...]
        g0 = _group_mask()
        pr = dgh * gh
        half = SSD_IN // 2
        m0 = jnp.sum(jnp.where(g0, pr, 0.0), axis=1, keepdims=True) / half
        m1 = jnp.sum(jnp.where(g0, 0.0, pr), axis=1, keepdims=True) / half
        dgz = r * (dgh - gh * jnp.where(g0, m0, m1))
        dyt = dgz * _silu(zv)
        dz_ref[...] = (dgz * ytot * _dsilu(zv)).astype(BF)
        dyt_ref[...] = dyt
        dxsk_ref[...] = dyt * dsk_ref[...]
        ddsk = jnp.sum(dyt * xbc_ref[:, 0:SSD_IN], axis=0, keepdims=True)
        _acc_rows(dmod_ref, R.first_of_stream(i), {2: dg1})
        _acc_rows(dvec_ref, jnp.logical_and(b == 0, i == 0), {0: dsnw, 1: ddsk})

    return R.call("out_bwd", body,
                  [(dxm, R.row(D)), (mod, R.mod()), (mixbf, R.row(D)), (yf, R.row(SSD_IN)), (yb, R.row(SSD_IN)), (xbc, R.row(XBC)),
                   (z, R.row(SSD_IN)),
                   (o, R.row(QW)), (qt, R.row(QW)), (lse, pl.BlockSpec((None, H, TM, 1), lambda b, i: (b, 0, i, 0))),
                   (dsk, R.const((1, SSD_IN))), (snw, R.const((1, SSD_IN))), (wout, R.const((MIXW, D)))],
                  [(_sds((B, T, D), BF), R.row(D)), (_sds((B, T, SSD_IN), BF), R.row(SSD_IN)), (_sds((B, T, SSD_IN), F32), R.row(SSD_IN)),
                   (_sds((B, T, SSD_IN), F32), R.row(SSD_IN)), (_sds((B, T, QW), BF), R.row(QW)), (_sds((B, T, QW), BF), R.row(QW)),
                   (_sds((B, H, HP, T), BF), pl.BlockSpec((None, H, HP, TM), lambda b, i: (b, 0, 0, i))),
                   (_sds((B, T, PD), F32), R.row(PD)),
                   (_sds((B, 2, 8, D), F32), R.mod()), (_sds((8, SSD_IN), F32), R.const((8, SSD_IN)))],
                  extra=8 << 20)


def _pool_bwd(R, dpo, u, wbd, scale):
    B, T = R.B, R.T

    def body(d_ref, dpv_ref, dnx_ref, u_ref, upv_ref, unx_ref, w_ref, sc_ref, du_ref, dw_ref, dsc_ref):
        b, i = pl.program_id(0), pl.program_id(1)
        ext_u = R.ext(i, upv_ref, u_ref[...], unx_ref)
        ext_d = R.ext(i, dpv_ref, d_ref[...], dnx_ref)
        dm, cnt, valid, lane = _pool_centred(R, i, ext_u)
        ddm = _dot_nt(ext_d * sc_ref[...], w_ref[...]) * valid
        e = ddm / cnt
        a2 = e + _shift_rows(e, 1)
        a4 = _shift_rows(a2, -1) + _shift_rows(a2, 1)
        a8 = _shift_rows(a4, -2) + _shift_rows(a4, 2)
        a16 = _shift_rows(a8, -4) + _shift_rows(a8, 4)
        du_ref[...] = (_lane_select(lane, a2, a4, a8, a16) - ddm)[8:8 + TM].astype(BF)
        dmc = dm[8:8 + TM]
        dpo_c = d_ref[...]
        first = jnp.logical_and(b == 0, i == 0)

        @pl.when(first)
        def _():
            dw_ref[...] = jnp.zeros(dw_ref.shape, F32)

        dw_ref[...] += _dot_tn(dmc, dpo_c * sc_ref[...])
        _acc_rows(dsc_ref, first, {0: jnp.sum(dpo_c * _dot(dmc, w_ref[...]), axis=0, keepdims=True)})

    return R.call("pool_bwd", body,
                  [(dpo, R.row(PD)), (dpo, R.prev8(PD)), (dpo, R.next8(PD)), (u, R.row(PD)), (u, R.prev8(PD)), (u, R.next8(PD)),
                   (wbd, R.const((PD, PD))), (scale, R.const((1, PD)))],
                  [(_sds((B, T, PD), BF), R.row(PD)), (_sds((PD, PD), F32), R.const((PD, PD))), (_sds((8, PD), F32), R.const((8, PD)))],
                  extra=8 << 20)


def _flash_bwd(B, T, qs, kt, dos, q_t, k_t, v_t, do_t):
    nk, nq = T // TK, T // TQ
    HS = 2

    def body(k_ref, kt_ref, vt_ref, q_ref, do_ref, qt_ref, dot_ref, dq_ref, dk_ref, dv_ref, s_scr, dp_scr, ds_scr):
        j = pl.program_id(2)
        ctx_keys = jnp.where(j < CTX // TK, 1.0, 0.0)
        for hh in range(HS):
            cols = slice(HP * hh, HP * (hh + 1))
            s_scr[hh] = _dot(q_ref[:, cols], kt_ref[hh])
            dp_scr[hh] = _dot(do_ref[:, cols], vt_ref[hh])
        for hh in range(HS):
            cols = slice(HP * hh, HP * (hh + 1))
            dk_tr = jnp.zeros((HP, TK), F32)
            dv_tr = jnp.zeros((HP, TK), F32)
            for i in range(nq):
                rows = slice(TQ * i, TQ * (i + 1))
                p = jnp.exp(s_scr[hh, rows, :])
                if i < CTX // TQ:
                    p = p * ctx_keys
                ds = (p * dp_scr[hh, rows, :]).astype(BF)
                ds_scr[hh, rows, :] = ds
                dv_tr = dv_tr + _dot(dot_ref[hh, :, rows], p)
                dk_tr = dk_tr + _dot(qt_ref[hh, :, rows], ds)
            dq = _dot(ds_scr[hh], k_ref[:, cols])

            @pl.when(j == 0)
            def _():
                dq_ref[:, cols] = dq

            @pl.when(j > 0)
            def _():
                dq_ref[:, cols] += dq

            dk_ref[:, cols] = dk_tr.T
            dv_ref[:, cols] = dv_tr.T

    tspec = pl.BlockSpec((None, TK, HS * HP), lambda b, h, j: (b, j, h))
    fspec = pl.BlockSpec((None, T, HS * HP), lambda b, h, j: (b, 0, h))
    ttspec = pl.BlockSpec((None, HS, HP, TK), lambda b, h, j: (b, h, 0, j))
    ftspec = pl.BlockSpec((None, HS, HP, T), lambda b, h, j: (b, h, 0, 0))
    bb = HS * (4 * _nbytes((T, HP), BF) + _nbytes((T, HP), F32) + 8 * _nbytes((TK, HP), F32))
    scr = HS * (2 * _nbytes((T, TK), F32) + _nbytes((T, TK), BF))
    return pl.pallas_call(
        functools.partial(body), name="flash_bwd", grid=(B, H // HS, nk),
        in_specs=[tspec, ttspec, ttspec, fspec, fspec, ftspec, ftspec],
        out_specs=[fspec, tspec, tspec],
        out_shape=[_out((B, T, QW), F32)] * 3,
        scratch_shapes=[pltpu.VMEM((HS, T, TK), F32), pltpu.VMEM((HS, T, TK), F32), pltpu.VMEM((HS, T, TK), BF)],
        compiler_params=_params(("arbitrary",) * 3, bb, scr + (8 << 20)),
    )(*_hbm([kt, k_t, v_t, qs, dos, q_t, do_t]))


def _mla_bwd(R, dqt, dkt, dvt, qa, kva, qnw, kvnw, wq, wk, wv, cos, sin):
    B, T = R.B, R.T
    scale = QK ** -0.5

    def body(dq_ref, dk_ref, dv_ref, qa_ref, kva_ref, qnw_ref, kvnw_ref, wq_ref, wk_ref, wv_ref, cos_ref, sin_ref,
             dqp_ref, dkb_ref, dvb_ref, dqa_ref, dkva_ref, dkr_ref, dnw_ref):
        b, i = pl.program_id(0), pl.program_id(1)
        cos1, sin1 = cos_ref[...], sin_ref[...]
        dq = dq_ref[...] * scale
        dqp = _per_head(lambda g: g * cos1 + _swap8(g * sin1), dq).astype(BF)
        dqp_ref[...] = dqp
        dkv = dk_ref[...]
        dkb = dkv.astype(BF)
        dkb_ref[...] = dkb
        dvb = dv_ref[...].astype(BF)
        dvb_ref[...] = dvb
        dkk = dkv[:, 0:HP]
        for h in range(1, H):
            dkk = dkk + dkv[:, HP * h:HP * (h + 1)]
        lane = _iota((TM, HP), 1)
        rope_lane = jnp.logical_and(lane >= NOPE, lane < NOPE + ROPE)
        dkr_ref[...] = jnp.where(rope_lane, dkk * cos1 + _swap8(dkk * sin1), 0.0).astype(BF)
        _, qh, qr = _rms(qa_ref[...], qnw_ref[...])
        _, kh, kr_ = _rms(kva_ref[...], kvnw_ref[...])
        dcq = _dot_nt(dqp, wq_ref[...])
        dckv = _dot_nt(dkb, wk_ref[...]) + _dot_nt(dvb, wv_ref[...])
        dqa, dqw = _rms_bwd(dcq, qh, qr, qnw_ref[...])
        dkva, dkw = _rms_bwd(dckv, kh, kr_, kvnw_ref[...])
        dqa_ref[...] = dqa.astype(BF)
        dkva_ref[...] = dkva.astype(BF)
        _acc_rows(dnw_ref, jnp.logical_and(b == 0, i == 0), {0: dqw, 1: dkw})

    tab = pl.BlockSpec((TM, HP), lambda b, i: (i, 0))
    return R.call("mla_bwd", body,
                  [(dqt, R.row(QW)), (dkt, R.row(QW)), (dvt, R.row(QW)), (qa, R.row(QL)), (kva, R.row(QL)),
                   (qnw, R.const((1, QL))), (kvnw, R.const((1, QL))), (wq, R.const((QL, QW))), (wk, R.const((QL, QW))),
                   (wv, R.const((QL, QW))), (cos, tab), (sin, tab)],
                  [(_sds((B, T, QW), BF), R.row(QW))] * 3 + [(_sds((B, T, QL), BF), R.row(QL))] * 2
                  + [(_sds((B, T, HP), BF), R.row(HP)), (_sds((8, QL), F32), R.const((8, QL)))], extra=8 << 20)


def _ssd_scan_bwd(B, T, dyt, xbc, dt2, alog2, hin_f, hin_b):
    nc, nctc = T // CH, CTX // CH

    def chain(d, dy_ref, xbc_ref, dt_ref, alog_ref, hin_ref, dxbc_ref, ddt_ref, da_ref, dhs_all):
        dhs = dhs_all.at[d]
        xbc_v = xbc_ref[...]
        dyv = dy_ref[...]
        a_all, dt_all, adt_all, q_all, inc, inc_t = _scan_common(d, dt_ref[...], alog_ref)
        ones_p = jnp.ones((P, HP), F32)
        total = lambda m: jnp.sum(jnp.sum(m, axis=0, keepdims=True), axis=1, keepdims=True)
        dq_parts, dqtot_parts, ddtx_parts = [], [], []
        for g in range(2):
            bg = xbc_v[:, SSD_IN + NST * g:SSD_IN + NST * (g + 1)]
            cg = xbc_v[:, SSD_IN + 2 * NST + NST * g:SSD_IN + 2 * NST + NST * (g + 1)]
            bg_t, cg_t = bg.T, cg.T
            gm = _dot(cg, bg_t)
            gm_t = _dot(bg, cg_t)
            dgm = jnp.zeros((CH, CH), F32)
            dgm_t = jnp.zeros((CH, CH), F32)
            dbg = jnp.zeros((CH, NST), F32)
            dcg = jnp.zeros((CH, NST), F32)
            for r in range(3):
                h = 3 * g + r
                lm, lm_t, eq, etot, dte = _head_decay(h, adt_all, q_all, inc, inc_t)
                eq_p, etot_p, dte_p = eq[:, :P], etot[:, :P], dte[:, :P]
                dt_p = dt_all[:, HP * h:HP * h + P]
                xs_h = xbc_v[:, P * h:P * (h + 1)]
                xh = xs_h * dt_p
                sm, sm_t = gm * lm, gm_t * lm_t
                dy_h = dyv[:, P * h:P * (h + 1)]
                hprev = hin_ref[P * h:P * (h + 1), :].T
                dho = dhs[:, P * h:P * (h + 1)]
                ds = _dot_nt(dy_h, xh)
                ds_t = _dot_nt(xh, dy_h)
                dx = _dot(sm_t, dy_h)
                edy = eq_p * dy_h
                yo = _dot(cg, hprev)
                dcg = dcg + _dot_nt(edy, hprev)
                dhin = _dot(cg_t, edy) + etot_p * dho
                zs = _dot(bg, dho)
                dx = dx + dte_p * zs
                wm = dte_p * xh * zs
                dbg = dbg + _dot_nt(xh * dte_p, dho)
                dgm = dgm + ds * lm
                dgm_t = dgm_t + ds_t * lm_t
                rs = jnp.sum(ds * sm - ds_t * sm_t, axis=1, keepdims=True) + jnp.sum(edy * yo - wm, axis=1, keepdims=True)
                dq_parts.append(jnp.broadcast_to(rs, (CH, HP)))
                dqtot_parts.append(total(hprev * dho) * etot + total(wm))
                dxbc_ref[:, P * h:P * (h + 1)] = dx * dt_p
                ddtx_parts.append(_dot01_r(dx * xs_h, ones_p, passes=2))
                dhs[:, P * h:P * (h + 1)] = dhin
            dcg = dcg + _dot(dgm, bg)
            dbg = dbg + _dot(dgm_t, cg)
            dxbc_ref[:, SSD_IN + NST * g:SSD_IN + NST * (g + 1)] = dbg
            dxbc_ref[:, SSD_IN + 2 * NST + NST * g:SSD_IN + 2 * NST + NST * (g + 1)] = dcg
        cat = lambda parts: jnp.concatenate(parts, axis=1)
        dadt_all = _dot01(inc_t, cat(dq_parts)) + cat(dqtot_parts)
        ddt_all = cat(ddtx_parts) + dadt_all * jnp.concatenate([a_all] * (CH // 8), axis=0)
        da_all = jnp.sum(dadt_all * dt_all, axis=0, keepdims=True)
        lane, lane1 = _iota((CH, HP), 1), _iota((1, HP), 1)
        ddt = jnp.zeros((CH, HP), F32)
        da = jnp.zeros((1, HP), F32)
        for h in range(H):
            ddt = ddt + jnp.where(lane == h, ddt_all[:, HP * h:HP * (h + 1)], 0.0)
            da = da + jnp.where(lane1 == h, da_all[:, HP * h:HP * (h + 1)], 0.0)
        ddt_ref[...] = ddt
        da_ref[d] += jnp.where(_iota((8, HP), 0) == 0, da, 0.0)

    def body(dyf_ref, dyb_ref, xf_ref, xb_ref, dtf_ref, dtb_ref, alog_ref, hf_ref, hb_ref,
             dxf_ref, dxb_ref, ddtf_ref, ddtb_ref, da_ref, dhs):
        @pl.when(pl.program_id(1) == 0)
        def _():
            dhs[...] = jnp.zeros(dhs.shape, F32)
            da_ref[...] = jnp.zeros(da_ref.shape, F32)

        chain(0, dyf_ref, xf_ref, dtf_ref, alog_ref, hf_ref, dxf_ref, ddtf_ref, da_ref, dhs)
        chain(1, dyb_ref, xb_ref, dtb_ref, alog_ref, hb_ref, dxb_ref, ddtb_ref, da_ref, dhs)

    cidx = lambda d: (lambda s: _chunk_index(d, nc - 1 - s, nc, nctc))
    specs = lambda d: dict(
        dy=pl.BlockSpec((None, CH, SSD_IN), lambda b, s: (b, cidx(d)(s), 0)),
        xbc=pl.BlockSpec((None, CH, XBC), lambda b, s: (b, cidx(d)(s), 0)),
        dt=pl.BlockSpec((None, None, CH, HP), lambda b, s: (b, d, cidx(d)(s), 0)),
        h=pl.BlockSpec((None, None, SSD_IN, NST), lambda b, s: (b, cidx(d)(s), 0, 0)),
        ddt=pl.BlockSpec((None, CH, HP), lambda b, s: (b, cidx(d)(s), 0)))
    f, r = specs(0), specs(1)
    bb = 2 * (2 * _nbytes((CH, XBC), F32) + 2 * _nbytes((CH, HP), F32) + _nbytes((CH, SSD_IN), F32) + _nbytes((SSD_IN, NST), F32))
    return pl.pallas_call(
        functools.partial(body), name="ssd_scan_bwd", grid=(B, nc),
        in_specs=[f["dy"], r["dy"], f["xbc"], r["xbc"], f["dt"], r["dt"], pl.BlockSpec((2, 8, HP), lambda b, s: (0, 0, 0)),
                  f["h"], r["h"]],
        out_specs=[f["xbc"], r["xbc"], f["ddt"], r["ddt"], pl.BlockSpec((None, 2, 8, HP), lambda b, s: (b, 0, 0, 0))],
        out_shape=[_out((B, T, XBC), F32)] * 2 + [_out((B, T, HP), F32)] * 2 + [_out((B, 2, 8, HP), F32)],
        scratch_shapes=[pltpu.VMEM((2, NST, SSD_IN), F32)],
        compiler_params=_params(("arbitrary",) * 2, bb, 16 << 20),
    )(*_hbm([dyt, dyt, xbc, xbc, dt2, dt2, alog2, hin_f, hin_b]))


def _ssd_prep_bwd(R, dxbc_f, dxbc_b, dxsk, ddt_f, ddt_b, xbc_raw, dt_raw, conv_w8, conv_b, dtb):
    B, T = R.B, R.T

    def body(dxf_ref, dxb_ref, dsk_ref, ddtf_ref, ddtb2_ref, raw_ref, pv_ref, nx_ref, dtr_ref, w_ref, b_ref, dtb_ref,
             dpre_ref, ddtr_ref, dcw_ref, dvec_ref, ddtb_ref):
        b, i = pl.program_id(0), pl.program_id(1)
        ext = R.ext(i, pv_ref, raw_ref[...], nx_ref)
        pre = _conv_pre(ext, w_ref, b_ref)
        dxbc = dxf_ref[...] + dxb_ref[...]
        skip = jnp.concatenate([dsk_ref[...], jnp.zeros((TM, XBC - SSD_IN), F32)], axis=1)
        dpre = (dxbc + skip) * _dsilu(pre)
        dpre_ref[...] = dpre
        first = jnp.logical_and(b == 0, i == 0)
        taps = {k: jnp.sum(dpre * _shift_rows(ext, k - 1)[8:8 + TM], axis=0, keepdims=True) for k in range(4)}
        _acc_rows(dcw_ref, first, taps)
        _acc_rows(dvec_ref, first, {0: jnp.sum(dpre, axis=0, keepdims=True)})
        ddt = ddtf_ref[...] + pltpu.roll(ddtb2_ref[...], H, axis=1)
        ddtr = ddt * _sigmoid(dtr_ref[...] + dtb_ref[...])
        ddtr = jnp.where(_iota((TM, HP), 1) < 2 * H, ddtr, 0.0)
        ddtr_ref[...] = ddtr.astype(BF)
        _acc_rows(ddtb_ref, first, {0: jnp.sum(ddtr, axis=0, keepdims=True)})

    return R.call("ssd_prep_bwd", body,
                  [(dxbc_f, R.row(XBC)), (dxbc_b, R.row(XBC)), (dxsk, R.row(SSD_IN)), (ddt_f, R.row(HP)), (ddt_b, R.row(HP)),
                   (xbc_raw, R.row(XBC)), (xbc_raw, R.prev8(XBC)),
                   (xbc_raw, R.next8(XBC)), (dt_raw, R.row(HP)), (conv_w8, R.const((8, XBC))), (conv_b, R.const((1, XBC))),
                   (dtb, R.const((1, HP)))],
                  [(_sds((B, T, XBC), F32), R.row(XBC)), (_sds((B, T, HP), BF), R.row(HP)), (_sds((8, XBC), F32), R.const((8, XBC))),
                   (_sds((8, XBC), F32), R.const((8, XBC))), (_sds((8, HP), F32), R.const((8, HP)))], extra=16 << 20)


def _in_bwd(R, dxm, x, mod, nw1, dz, dpre, dqa, dkva, dpool, dkr, ddtr, conv_w8, w_arr):
    B, T = R.B, R.T

    def body(dxm_ref, x_ref, mod_ref, nw_ref, dz_ref, dp_ref, dpp_ref, dpn_ref, dqa_ref, dkva_ref, dpool_ref, dkr_ref, ddt_ref,
             cw_ref, w_ref, dx_ref, dproj_ref, dmod_ref, dnw_ref):
        b, i = pl.program_id(0), pl.program_id(1)
        ext = R.ext(i, dpp_ref, dp_ref[...], dpn_ref)
        draw = (_row(cw_ref, 0) * _shift_rows(ext, 1)[8:8 + TM] + _row(cw_ref, 1) * ext[8:8 + TM]
                + _row(cw_ref, 2) * _shift_rows(ext, -1)[8:8 + TM] + _row(cw_ref, 3) * _shift_rows(ext, -2)[8:8 + TM])
        dproj_ref[:, A_Z:A_XBC] = dz_ref[...]
        dproj_ref[:, A_XBC:A_QA] = draw.astype(BF)
        dproj_ref[:, A_QA:A_KVA] = dqa_ref[...]
        dproj_ref[:, A_KVA:A_POOL] = dkva_ref[...]
        dproj_ref[:, A_POOL:A_KR] = dpool_ref[...]
        dproj_ref[:, A_KR:A_DT] = dkr_ref[...]
        dproj_ref[:, A_DT:PC] = ddt_ref[...]
        dh = _dot_nt(dproj_ref[...], w_ref[...])
        _, xn, r = _norm_mod(x_ref[...], nw_ref[...], _row(mod_ref, 0), _row(mod_ref, 1))
        dx, dsh, dsc, dnw = _norm_mod_bwd(dh, xn, r, nw_ref[...], _row(mod_ref, 1))
        dx_ref[...] = dxm_ref[...] + dx
        _acc_rows(dmod_ref, R.first_of_stream(i), {0: dsh, 1: dsc})
        _acc_rows(dnw_ref, jnp.logical_and(b == 0, i == 0), {0: dnw})

    return R.call("in_bwd", body,
                  [(dxm, R.row(D)), (x, R.row(D)), (mod, R.mod()), (nw1, R.const((1, D))), (dz, R.row(SSD_IN)), (dpre, R.row(XBC)),
                   (dpre, R.prev8(XBC)), (dpre, R.next8(XBC)), (dqa, R.row(QL)), (dkva, R.row(QL)), (dpool, R.row(PD)), (dkr, R.row(HP)),
                   (ddtr, R.row(HP)), (conv_w8, R.const((8, XBC))), (w_arr, R.const((D, PC)))],
                  [(_sds((B, T, D), F32), R.row(D)), (_sds((B, T, PC), BF), R.row(PC)), (_sds((B, 2, 8, D), F32), R.mod()),
                   (_sds((8, D), F32), R.const((8, D)))], extra=12 << 20)


def _adaln_fwd(cs, mod_w):
    L, _, C = mod_w.shape

    def body(c_ref, w_ref, o_ref):
        s = _silu(c_ref[...]).astype(BF)
        for l in range(L):
            o_ref[l] = jnp.dot(s, w_ref[l].astype(BF), preferred_element_type=F32)

    return pl.pallas_call(functools.partial(body), name="adaln_fwd", out_shape=_sds((L, 24, C), F32),
                          compiler_params=_params(None, _nbytes(mod_w.shape, F32) + _nbytes((L, 24, C), F32), 8 << 20))(cs, mod_w)


def _adaln_bwd(cs, dm, mod_w):
    L, _, C = mod_w.shape

    def body(c_ref, dm_ref, w_ref, gw_ref, gc_ref):
        c = c_ref[...]
        s = _silu(c).astype(BF)
        acc = jnp.zeros((24, D), F32)
        for l in range(L):
            dmb = dm_ref[l].astype(BF)
            gw_ref[l] = _dot_tn(s, dmb)
            acc = acc + _dot_nt(dmb, w_ref[l])
        gc_ref[...] = acc * _dsilu(c)

    return pl.pallas_call(functools.partial(body), name="adaln_bwd", out_shape=[_sds((L, D, C), F32), _sds((24, D), F32)],
                          compiler_params=_params(None, 2 * _nbytes(mod_w.shape, F32), 8 << 20))(cs, dm, mod_w)


def _adamw(name, parts, w, m, v, rt):
    Pn, Rr, C = parts.shape
    c1 = 1.0 - ADAM_B1 ** ADAM_STEP
    c2 = 1.0 - ADAM_B2 ** ADAM_STEP

    def body(p_ref, w_ref, m_ref, v_ref, g_ref, d_ref, nm_ref, nv_ref):
        g = p_ref[0].astype(F32)
        for k in range(1, Pn):
            g = g + p_ref[k].astype(F32)
        mn = ADAM_B1 * m_ref[...] + (1.0 - ADAM_B1) * g
        vn = ADAM_B2 * v_ref[...] + (1.0 - ADAM_B2) * jnp.square(g)
        g_ref[...] = g
        nm_ref[...] = mn
        nv_ref[...] = vn
        d_ref[...] = -ADAM_LR * ((mn / c1) / (jnp.sqrt(vn / c2) + ADAM_EPS) + ADAM_WD * w_ref[...])

    spec = pl.BlockSpec((rt, C), lambda i: (i, 0))
    bb = Pn * _nbytes((rt, C), parts.dtype) + 7 * _nbytes((rt, C), F32)
    return pl.pallas_call(
        functools.partial(body), name=name, grid=(Rr // rt,),
        in_specs=[pl.BlockSpec((Pn, rt, C), lambda i: (0, i, 0)), spec, spec, spec],
        out_specs=[spec] * 4, out_shape=[_out((Rr, C), F32)] * 4,
        compiler_params=_params(("arbitrary",), bb, 4 << 20),
    )(*_hbm([parts, w, m, v]))


MESH = pl.DeviceIdType.MESH


def _my_pos():
    return lax.axis_index("x"), lax.axis_index("y"), lax.axis_index("c")


def _dev_index(x, y, c):
    return 4 * x + 2 * y + c


def _all_gather(name, shards):
    n = len(shards)

    def body(*refs):
        ins, outs = refs[:n], refs[n:2 * n]
        send_sems, recv_sems, local_sem = refs[2 * n:]
        x, y, c = _my_pos()
        me, sibling = (x, y, c), (x, y, 1 - c)
        chips = [(1 - x, y), (x, 1 - y), (1 - x, 1 - y)]

        def copy(t, k, block, to, src=None):
            slot = outs[t].at[_dev_index(*block)]
            return pltpu.make_async_remote_copy(
                src_ref=slot if src is None else src, dst_ref=slot,
                send_sem=send_sems.at[t, k], recv_sem=recv_sems.at[t, k], device_id=to, device_id_type=MESH)

        mine = [pltpu.make_async_copy(ins[t], outs[t].at[_dev_index(*me)], local_sem.at[t]) for t in range(n)]
        for cp in mine:
            cp.start()
        first = []
        for t in range(n):
            first.append(copy(t, 0, me, sibling, src=ins[t]))
            first += [copy(t, 1 + j, me, (*chip, c), src=ins[t]) for j, chip in enumerate(chips)]
        for cp in first:
            cp.start()
        passed = []
        for j, chip in enumerate(chips):
            for t in range(n):
                copy(t, 1 + j, (*chip, c), me).wait_recv()
                cp = copy(t, 4 + j, (*chip, c), sibling)
                cp.start()
                passed.append(cp)
        for t in range(n):
            copy(t, 0, sibling, me).wait_recv()
            for j, chip in enumerate(chips):
                copy(t, 4 + j, (*chip, 1 - c), me).wait_recv()
        for cp in first + passed:
            cp.wait_send()
        for cp in mine:
            cp.wait()

    return pl.pallas_call(
        functools.partial(body), name=name,
        in_specs=[ANY] * n, out_specs=[ANY] * n,
        out_shape=[_sds((NDEV,) + s.shape, s.dtype) for s in shards],
        scratch_shapes=[pltpu.SemaphoreType.DMA((n, 7)), pltpu.SemaphoreType.DMA((n, 7)), pltpu.SemaphoreType.DMA((n,))],
    )(*shards)


def _all_to_all(name, parts):
    n = len(parts)

    def body(*refs):
        ins, outs = refs[:n], refs[n:2 * n]
        send_sems, recv_sems, local_sem = refs[2 * n:]
        x, y, c = _my_pos()
        me = _dev_index(x, y, c)
        peers = [(x ^ ((k >> 2) & 1), y ^ ((k >> 1) & 1), c ^ (k & 1)) for k in range(1, NDEV)]
        mine = [pltpu.make_async_copy(ins[t].at[me], outs[t].at[me], local_sem.at[t]) for t in range(n)]
        for cp in mine:
            cp.start()
        sends = []
        for t in range(n):
            for k, peer in enumerate(peers):
                cp = pltpu.make_async_remote_copy(
                    src_ref=ins[t].at[_dev_index(*peer)], dst_ref=outs[t].at[me],
                    send_sem=send_sems.at[t, k], recv_sem=recv_sems.at[t, k], device_id=peer, device_id_type=MESH)
                cp.start()
                sends.append(cp)
        for t in range(n):
            for k, peer in enumerate(peers):
                slot = outs[t].at[_dev_index(*peer)]
                pltpu.make_async_remote_copy(
                    src_ref=slot, dst_ref=slot, send_sem=send_sems.at[t, k], recv_sem=recv_sems.at[t, k],
                    device_id=peer, device_id_type=MESH).wait_recv()
        for cp in sends:
            cp.wait_send()
        for cp in mine:
            cp.wait()

    return pl.pallas_call(
        functools.partial(body), name=name,
        in_specs=[ANY] * n, out_specs=[ANY] * n,
        out_shape=[_sds(p.shape, p.dtype) for p in parts],
        scratch_shapes=[pltpu.SemaphoreType.DMA((n, 7)), pltpu.SemaphoreType.DMA((n, 7)), pltpu.SemaphoreType.DMA((n,))],
    )(*parts)


SEM = pl.BlockSpec(memory_space=pltpu.SEMAPHORE)
IN_HBM = pl.BlockSpec(memory_space=pltpu.HBM)
DATAFLOW = pltpu.SideEffectType.DATAFLOW_SIDE_EFFECTING


def _flip_peers(x, y, c):
    return [(x ^ ((k >> 2) & 1), y ^ ((k >> 1) & 1), c ^ (k & 1)) for k in range(1, NDEV)]


def _split_copies(srcs, lands, send_sems, recv_sems, gather):
    x, y, c = _my_pos()
    me = _dev_index(x, y, c)
    out = []
    for t in range(len(srcs)):
        for k, peer in enumerate(_flip_peers(x, y, c)):
            p = _dev_index(*peer)
            src = srcs[t] if gather else srcs[t].at[p]
            sems = dict(send_sem=send_sems.at[7 * t + k], recv_sem=recv_sems.at[7 * t + k], device_id=peer, device_id_type=MESH)
            out.append((pltpu.make_async_remote_copy(src_ref=src, dst_ref=lands[t].at[me], **sems),
                        pltpu.make_async_remote_copy(src_ref=src, dst_ref=lands[t].at[p], **sems)))
    return out


def _exchange_start(name, collective_id, srcs, gather):
    n = len(srcs)
    lands = [lax.empty(((NDEV,) + s.shape) if gather else s.shape, s.dtype) for s in srcs]

    def body(*refs):
        src_refs, land_refs = refs[:n], refs[n:2 * n]
        send_sems, recv_sems = refs[2 * n], refs[2 * n + 1]
        token = refs[-1]
        barrier = pltpu.get_barrier_semaphore()
        for peer in _flip_peers(*_my_pos()):
            pl.semaphore_signal(barrier, inc=1, device_id=peer, device_id_type=MESH)
        pl.semaphore_wait(barrier, NDEV - 1)
        for send, _ in _split_copies(src_refs, land_refs, send_sems, recv_sems, gather):
            send.start()
        token[...] = jnp.zeros(token.shape, token.dtype)

    hbm = lambda a: pltpu.HBM(a.shape, a.dtype)
    res = pl.pallas_call(
        functools.partial(body), name=name,
        out_shape=[pltpu.SemaphoreType.DMA((7 * n,)), pltpu.SemaphoreType.DMA((7 * n,))] + [hbm(s) for s in srcs]
        + [hbm(a) for a in lands] + [_sds((8, HP), F32)],
        in_specs=[IN_HBM] * (2 * n), out_specs=[SEM, SEM] + [IN_HBM] * (2 * n) + [pl.BlockSpec(memory_space=pltpu.VMEM)],
        input_output_aliases={i: 2 + i for i in range(2 * n)},
        compiler_params=pltpu.CompilerParams(has_side_effects=DATAFLOW, collective_id=collective_id),
    )(*_hbm(list(srcs) + lands))
    return res[0], res[1], list(res[2:2 + n]), list(res[2 + n:2 + 2 * n]), res[-1]


def _exchange_wait(name, send_sems, recv_sems, srcs, lands, after, gather):
    n = len(srcs)

    def body(*refs):
        src_refs, land_refs = refs[:n], refs[n:2 * n]
        for _, recv in _split_copies(src_refs, land_refs, refs[2 * n], refs[2 * n + 1], gather):
            recv.wait_send()
            recv.wait_recv()

    hbm = lambda a: pltpu.HBM(a.shape, a.dtype)
    res = pl.pallas_call(
        functools.partial(body), name=name,
        out_shape=[hbm(s) for s in srcs] + [hbm(a) for a in lands],
        in_specs=[IN_HBM] * (2 * n) + [SEM, SEM, ANY], out_specs=[IN_HBM] * (2 * n),
        input_output_aliases={i: i for i in range(2 * n)},
        compiler_params=pltpu.CompilerParams(has_side_effects=DATAFLOW),
    )(*srcs, *lands, send_sems, recv_sems, after)
    return list(res[:n]), list(res[n:])


def _with_own(lands, own_blocks, me):
    out = []
    for land, own in zip(lands, own_blocks):
        out.append(lax.dynamic_update_slice(land, own[None], (me,) + (0,) * own.ndim))
    return out


def _arrange_w_in(w):
    z = lambda n: jnp.zeros((w.shape[0], n), w.dtype)
    return jnp.concatenate([w[:, 0:1280], w[:, 1292:1548], w[:, 1548:1804], w[:, 1836:2092],
                            z(64), w[:, 1804:1836], z(32), w[:, 1280:1292], z(HP - 2 * H)], axis=1)


def _unarrange_w_in(g):
    return jnp.concatenate([g[:, 0:1280], g[:, A_DT:A_DT + 2 * H], g[:, A_QA:A_KVA], g[:, A_KVA:A_POOL],
                            g[:, A_KR + NOPE:A_KR + NOPE + ROPE], g[:, A_POOL:A_KR]], axis=1)


def _pad_heads(w, width):
    k = w.shape[0]
    return jnp.pad(w.reshape(k, H, width), ((0, 0), (0, 0), (0, HP - width))).reshape(k, H * HP)


def _unpad_heads(g, width):
    k = g.shape[0]
    return g.reshape(k, H, HP)[:, :, :width].reshape(k, H * width)


def _arrange_w_out(w):
    att = jnp.pad(w[SSD_IN:2 * SSD_IN].reshape(H, VH, D), ((0, 0), (0, HP - VH), (0, 0))).reshape(QW, D)
    return jnp.concatenate([w[0:SSD_IN], att, w[2 * SSD_IN:]], axis=0)


def _unarrange_w_out(g):
    att = g[SSD_IN:SSD_IN + QW].reshape(H, HP, D)[:, :VH].reshape(SSD_IN, D)
    return jnp.concatenate([g[0:SSD_IN], att, g[SSD_IN + QW:]], axis=0)


def _rope_tables(T):
    n = T - CTX
    t = jnp.arange(n, dtype=F32)
    row, col = jnp.floor(t / GRID_W), t % GRID_W
    pairs = ROPE // 4
    inv = ROPE_THETA ** (-jnp.arange(pairs, dtype=F32) / pairs)
    ar, ac = row[:, None] * inv, col[:, None] * inv
    cos = jnp.concatenate([jnp.cos(ar)] * 2 + [jnp.cos(ac)] * 2, axis=1)
    sin = jnp.concatenate([-jnp.sin(ar), jnp.sin(ar), -jnp.sin(ac), jnp.sin(ac)], axis=1)
    ones, zeros = jnp.ones((n, NOPE), F32), jnp.zeros((n, NOPE), F32)
    cos = jnp.concatenate([ones, cos, ones[:, :HP - QK]], axis=1)
    sin = jnp.concatenate([zeros, sin, zeros[:, :HP - QK]], axis=1)
    return (jnp.concatenate([jnp.ones((CTX, HP), F32), cos], axis=0),
            jnp.concatenate([jnp.zeros((CTX, HP), F32), sin], axis=0))


def _lane_pad(v, n):
    return jnp.pad(v, (0, n - v.shape[0]))[None, :]


def _layer_weights(w_in, w_q_b, w_kv_b, conv_w, pool_w):
    kv = w_kv_b.reshape(QL, H, NOPE + VH)
    wbd = jnp.zeros((PD, PD), F32)
    for g in range(4):
        wbd = lax.dynamic_update_slice(wbd, pool_w[g], (64 * g, 64 * g))
    return dict(
        w_in=_arrange_w_in(w_in).astype(BF),
        wq=_pad_heads(w_q_b, QK).astype(BF),
        wk=_pad_heads(kv[:, :, :NOPE].reshape(QL, H * NOPE), NOPE).astype(BF),
        wv=_pad_heads(kv[:, :, NOPE:].reshape(QL, H * VH), VH).astype(BF),
        conv_w8=jnp.pad(conv_w, ((0, 4), (0, 0))), wbd=wbd)


def _layer_fwd(R, x, mod, lw, sp, late, cos, sin):
    B, T = R.B, R.T
    z, xbc_raw, qa, kva, pool_in, kr, dt_raw, h1 = _in_proj(R, x, mod, sp["nw1"], lw["w_in"])
    xbc, dt2 = _ssd_prep(R, xbc_raw, dt_raw, lw["conv_w8"], sp["conv_b"], sp["dtb"])
    yf, yb, hin_f, hin_b = _ssd_scan(B, T, xbc, dt2, sp["alog2"])
    qt, kt, q_t, k_t, v_t, cq, ckv = _mla_prep(R, qa, kva, kr, sp["qnw"], sp["kvnw"], lw["wq"], lw["wk"], lw["wv"], cos, sin)
    o, lse = _flash_fwd(B, T, q_t, kt, v_t)
    pool = _pool_fwd(R, pool_in, lw["wbd"], sp["pscale"])[0]
    w_out, w1, w2 = late(o)
    w_out = _arrange_w_out(w_out).astype(BF)
    xmid, cat, mixbf = _out_proj(R, x, mod, yf, yb, xbc, z, o, pool, sp["dsk"], sp["snw"], w_out)
    xo, h2, ubf, ybf = _mlp_fwd(R, xmid, mod, sp["nw2"], w1, w2)
    saved = dict(x=x, z=z, xbc_raw=xbc_raw, qa=qa, kva=kva, pool_in=pool_in, dt_raw=dt_raw, h1=h1, xbc=xbc, dt2=dt2,
                 yf=yf, yb=yb, hin_f=hin_f, hin_b=hin_b, qt=qt, kt=kt, q_t=q_t, k_t=k_t, v_t=v_t, cq=cq, ckv=ckv, o=o, lse=lse,
                 cat=cat, mixbf=mixbf, xmid=xmid, h2=h2, ubf=ubf, ybf=ybf, w_out=w_out, w1=w1, w2=w2)
    return xo, saved


def _layer_bwd(R, dxo, sv, mod, lw, sp, cos, sin, on_mlp=None):
    B, T = R.B, R.T
    dxm, du, abf, dyb, dmod_a, dnw2 = _mlp_bwd(R, dxo, sv["xmid"], sv["ubf"], sv["ybf"], mod, sp["nw2"], sv["w1"], sv["w2"])
    g_w1 = _tn_matmul("dw_mlp1", sv["h2"], du, 4 * FSH, sub=4)
    g_w2 = _tn_matmul("dw_mlp2", abf, dyb, D)[0].reshape(NDEV, FSH, D)
    snw = sp["snw"]
    tok = on_mlp(g_w1, g_w2) if on_mlp is not None else None
    if tok is not None:
        snw = snw + tok
    lse_col = sv["lse"].reshape(B, H, T, 1)
    dmix, dz, dyt, dxsk, qs, dos, do_t, dpo, dmod_b, dvec_o = _out_bwd(R, dxm, mod, sv["mixbf"], sv["yf"], sv["yb"], sv["xbc"], sv["z"],
                                                                 sv["o"], sv["qt"], lse_col, sp["dsk"], snw, sv["w_out"])
    g_wout = _unarrange_w_out(_tn_matmul("dw_out", sv["cat"], dmix, D)[0])
    dpool_in, g_wbd, dpsc = _pool_bwd(R, dpo, sv["pool_in"], lw["wbd"], sp["pscale"])
    dqt, dkt, dvt = _flash_bwd(B, T, qs, sv["kt"], dos, sv["q_t"], sv["k_t"], sv["v_t"], do_t)
    dqp, dkb, dvb, dqa, dkva, dkr, dnw_qk = _mla_bwd(R, dqt, dkt, dvt, sv["qa"], sv["kva"], sp["qnw"], sp["kvnw"],
                                                     lw["wq"], lw["wk"], lw["wv"], cos, sin)
    g_wq = _unpad_heads(_tn_matmul("dw_q", sv["cq"], dqp, QW)[0], QK)
    g_wk = _unpad_heads(_tn_matmul("dw_k", sv["ckv"], dkb, QW)[0], NOPE).reshape(QL, H, NOPE)
    g_wv = _unpad_heads(_tn_matmul("dw_v", sv["ckv"], dvb, QW)[0], VH).reshape(QL, H, VH)
    g_wkv = jnp.concatenate([g_wk, g_wv], axis=2).reshape(QL, H * (NOPE + VH))
    dxbc_f, dxbc_b, ddt_f, ddt_b, da = _ssd_scan_bwd(B, T, dyt, sv["xbc"], sv["dt2"], sp["alog2"], sv["hin_f"], sv["hin_b"])
    dpre, ddtr, dcw, dcb, ddtb = _ssd_prep_bwd(R, dxbc_f, dxbc_b, dxsk, ddt_f, ddt_b, sv["xbc_raw"], sv["dt_raw"], lw["conv_w8"],
                                                sp["conv_b"], sp["dtb"])
    dx, dproj, dmod_c, dnw1 = _in_bwd(R, dxm, sv["x"], mod, sp["nw1"], dz, dpre, dqa, dkva, dpool_in, dkr, ddtr,
                                      lw["conv_w8"], lw["w_in"])
    g_win = _unarrange_w_in(jnp.concatenate(list(_tn_matmul("dw_in", sv["h1"], dproj, PC)), axis=1))
    a2 = -jnp.exp(sp["alog2"][:, 0, :H])
    small = dict(
        norm1_w=dnw1[0], norm2_w=dnw2[0], conv_w=dcw[0:4], conv_b=dcb[0], dt_bias=ddtb[0, :2 * H].reshape(2, H),
        a_log=jnp.sum(da[:, :, 0, :H], axis=0) * a2, ssd_d=jnp.sum(dvec_o[1].reshape(H, P), axis=1), ssd_norm_w=dvec_o[0],
        q_a_norm_w=dnw_qk[0], kv_a_norm_w=dnw_qk[1],
        pool_w=jnp.stack([g_wbd[64 * g:64 * (g + 1), 64 * g:64 * (g + 1)] for g in range(4)]), pool_scale=dpsc[0])
    big = dict(w_in=g_win, w_out=g_wout, w_q_b=g_wq, w_kv_b=g_wkv, w_mlp1=g_w1, w_mlp2=g_w2)
    return dx, big, small, dmod_a + dmod_b + dmod_c


def _small_params(l, norm1_w, norm2_w, conv_b, dt_bias, a_log, ssd_d, ssd_norm_w, q_a_norm_w, kv_a_norm_w, pool_scale):
    alog2 = jnp.broadcast_to(jnp.pad(a_log[l], ((0, 0), (0, HP - H)))[:, None, :], (2, 8, HP))
    return dict(nw1=norm1_w[l][None], nw2=norm2_w[l][None], conv_b=conv_b[l][None],
                dtb=_lane_pad(dt_bias[l].reshape(2 * H), HP), alog2=alog2,
                dsk=jnp.repeat(ssd_d[l], P)[None], snw=ssd_norm_w[l][None], qnw=q_a_norm_w[l][None],
                kvnw=kv_a_norm_w[l][None], pscale=pool_scale[l][None])


SMALL_NAMES = ["mod_b", "norm1_w", "norm2_w", "conv_b", "dt_bias", "a_log", "ssd_d", "ssd_norm_w", "q_a_norm_w",
               "kv_a_norm_w", "pool_w", "pool_scale", "final_norm_w"]


def _pack(arrs):
    rows = []
    for a in arrs:
        f = a.reshape(-1).astype(F32)
        n = -(-f.shape[0] // HP) * HP
        rows.append(jnp.pad(f, (0, n - f.shape[0])).reshape(-1, HP))
    out = jnp.concatenate(rows, axis=0)
    pad = (-out.shape[0]) % 8
    return jnp.pad(out, ((0, pad), (0, 0)))


def _unpack(pack, like):
    outs, r = [], 0
    for a in like:
        n = math.prod(a.shape)
        nr = -(-n // HP)
        outs.append(pack[r:r + nr].reshape(-1)[:n].reshape(a.shape))
        r += nr
    return outs


def _local_step(x, ctx, target, mods, full_of, small_w, on_grads=None, on_mlp=None):
    B, N = x.shape[0], x.shape[1]
    T = CTX + N
    R = _Rows(B, T)
    cos, sin = _rope_tables(T)
    xu = jnp.concatenate([ctx, x], axis=1)
    L = len(mods)
    lws, sps, saves = [], [], []
    for l in range(L):
        f = full_of(l, xu)
        lws.append(_layer_weights(f["w_in"], f["w_q_b"], f["w_kv_b"], f["conv_w"], small_w["pool_w"][l]))
        sps.append(_small_params(l, *[small_w[k] for k in ["norm1_w", "norm2_w", "conv_b", "dt_bias", "a_log", "ssd_d",
                                                          "ssd_norm_w", "q_a_norm_w", "kv_a_norm_w", "pool_scale"]]))
        xu, sv = _layer_fwd(R, xu, mods[l], lws[l], sps[l], f["late"], cos, sin)
        saves.append(sv)
    dx, loss8, dfnw = _loss_head(R, xu, target, small_w["final_norm_w"][None])
    bigs, smalls, dmods = [None] * L, [None] * L, [None] * L
    for l in reversed(range(L)):
        hook = functools.partial(on_mlp, l) if on_mlp is not None else None
        dx, bigs[l], smalls[l], dmods[l] = _layer_bwd(R, dx, saves[l], mods[l], lws[l], sps[l], cos, sin, hook)
        if on_grads is not None:
            bigs[l], tok = on_grads(l, bigs[l], dx)
            if tok is not None:
                sps[l - 1] = dict(sps[l - 1], nw2=sps[l - 1]["nw2"] + tok)
    return loss8[0, 0], dx[:, CTX:], bigs, smalls, dfnw[0], dmods


def kernel(x, c, ctx, c_ctx, mod_w, mod_b, norm1_w, norm2_w, w_in, conv_w, conv_b, dt_bias, a_log, ssd_d, ssd_norm_w, q_a_norm_w, w_q_b, kv_a_norm_w, w_kv_b, pool_w, pool_scale, w_out, w_mlp1, w_mlp2, final_norm_w, loss_target, m_c_ctx, m_mod_w, m_mod_b, m_norm1_w, m_norm2_w, m_w_in, m_conv_w, m_conv_b, m_dt_bias, m_a_log, m_ssd_d, m_ssd_norm_w, m_q_a_norm_w, m_w_q_b, m_kv_a_norm_w, m_w_kv_b, m_pool_w, m_pool_scale, m_w_out, m_w_mlp1, m_w_mlp2, m_final_norm_w, v_c_ctx, v_mod_w, v_mod_b, v_norm1_w, v_norm2_w, v_w_in, v_conv_w, v_conv_b, v_dt_bias, v_a_log, v_ssd_d, v_ssd_norm_w, v_q_a_norm_w, v_w_q_b, v_kv_a_norm_w, v_w_kv_b, v_pool_w, v_pool_scale, v_w_out, v_w_mlp1, v_w_mlp2, v_final_norm_w):
    args = dict(locals())
    B = x.shape[0]
    L = mod_w.shape[0]
    me = _dev_index(*_my_pos())
    CS = mod_w.shape[2]

    big_names = ["w_in", "w_out", "w_q_b", "w_kv_b", "w_mlp1", "w_mlp2"]
    shards = {n: args[n].astype(BF) for n in big_names}
    early, late_names = ["w_in", "w_q_b", "w_kv_b"], ["w_out", "w_mlp1", "w_mlp2"]
    g0 = _all_gather("gather_weights", [c, conv_w] + [shards[n][0] for n in early])
    c_all, convw_all = g0[0], g0[1]
    gathered = {0: dict(zip(early, g0[2:]))}
    cs = jnp.concatenate([c_all.reshape(NDEV * B, D), c_ctx[None], jnp.zeros((24 - NDEV * B - 1, D), F32)], axis=0)
    m_loc = _adaln_fwd(cs, mod_w)
    m_all = _all_gather("gather_mod", [m_loc])[0]
    m_full = jnp.moveaxis(m_all, 0, 2).reshape(L, 24, NDEV * CS) + mod_b[:, None, :]
    pending = {}
    tok = jnp.zeros((), F32)
    res = _exchange_start("gather_start_0", 1, [shards[n][0] for n in late_names], gather=True)
    pending[0] = res[:4]
    tok = tok + res[4][0, 0]
    for l in range(1, L):
        res = _exchange_start("gather_start_%d" % l, 1 + l, [shards[n][l] for n in early + late_names], gather=True)
        pending[l] = res[:4]
        tok = tok + res[4][0, 0]
    mods = []
    for l in range(L):
        ex = lax.dynamic_slice(m_full[l], (me * B, 0), (B, 6 * D)).reshape(B, 6, D)
        cc = jnp.broadcast_to(m_full[l, NDEV * B].reshape(1, 6, D), (B, 6, D))
        mods.append(jnp.pad(jnp.stack([cc, ex], axis=1), ((0, 0), (0, 0), (0, 2), (0, 0))) + tok)

    def full_of(l, xu):
        if l > 0:
            own, lands = _exchange_wait("gather_wait_%d" % l, *pending.pop(l), xu, gather=True)
            gathered[l] = dict(zip(early + late_names, _with_own(lands, own, me)))

        def late(after):
            if l == 0:
                own, lands = _exchange_wait("gather_wait_0", *pending.pop(0), after, gather=True)
                gathered[0].update(zip(late_names, _with_own(lands, own, me)))
            gl = gathered[l]
            return gl["w_out"].reshape(D, D), gl["w_mlp1"], gl["w_mlp2"]

        g = gathered[l]
        return dict(
            w_in=g["w_in"].reshape(D, IN_COLS),
            w_q_b=jnp.moveaxis(g["w_q_b"], 0, 1).reshape(QL, H * QK),
            w_kv_b=jnp.moveaxis(g["w_kv_b"], 0, 1).reshape(QL, H * (NOPE + VH)),
            conv_w=jnp.moveaxis(convw_all[:, l], 0, 1).reshape(4, XBC), late=late)

    def grad_blocks(big, names):
        make = dict(
            w_in=lambda g: g.reshape(NDEV, D // NDEV, IN_COLS), w_out=lambda g: g.reshape(NDEV, D // NDEV, D),
            w_q_b=lambda g: jnp.moveaxis(g.reshape(QL, NDEV, -1), 1, 0),
            w_kv_b=lambda g: jnp.moveaxis(g.reshape(QL, NDEV, -1), 1, 0), w_mlp1=lambda g: g, w_mlp2=lambda g: g)
        return [make[n](big[n]).astype(BF) for n in names]

    sent, sent_mlp = {}, {}
    rest_names = ["w_in", "w_out", "w_q_b", "w_kv_b"]

    def on_mlp(l, g_w1, g_w2):
        if l > 0:
            return None
        res = _exchange_start("grads_mlp_start_0", 1 + 2 * L, [g_w1.astype(BF), g_w2.astype(BF)], gather=False)
        sent_mlp[l] = res[:4]
        return res[4][0, 0]

    def on_grads(l, big, dx):
        if l == 0:
            return grad_blocks(big, rest_names), None
        res = _exchange_start("grads_start_%d" % l, 1 + L + l, grad_blocks(big, big_names), gather=False)
        sent[l] = res[:4]
        return None, res[4][0, 0]

    small_w = {k: args[k] for k in SMALL_NAMES if k != "mod_b"}
    loss_part, grad_x, blocks, smalls, dfnw, dmods = _local_step(x, ctx, loss_target, mods, full_of, small_w, on_grads, on_mlp)
    loss = lax.psum(loss_part, ("x", "y", "c"))

    dm_ex = jnp.stack([dmods[l][:, 1, :6].reshape(B, 6 * D) for l in range(L)])
    dm_cc = jnp.stack([jnp.sum(dmods[l][:, 0, :6], axis=0).reshape(6 * D) for l in range(L)])
    small_parts = dict(
        mod_b=jnp.sum(dm_ex, axis=1) + dm_cc,
        **{k: jnp.stack([smalls[l][k] for l in range(L)]) for k in SMALL_NAMES[1:-1]},
        final_norm_w=dfnw, conv_w=jnp.stack([smalls[l]["conv_w"] for l in range(L)]), dm_cc=dm_cc)
    adam_grads = [small_parts[k] for k in SMALL_NAMES]
    extras = [small_parts["conv_w"], dm_cc, dm_ex]
    pack = jnp.concatenate([_pack(adam_grads), _pack(extras)], axis=0)
    pack_all = _all_gather("gather_small_grads", [pack])[0]
    wpack = _pack([args[k] for k in SMALL_NAMES])
    mpack = _pack([args["m_" + k] for k in SMALL_NAMES])
    vpack = _pack([args["v_" + k] for k in SMALL_NAMES])
    n_adam = wpack.shape[0]
    res_small = _adamw("adamw_small", pack_all[:, :n_adam], wpack, mpack, vpack, n_adam)
    small_out = [_unpack(r, [args[k] for k in SMALL_NAMES]) for r in res_small]
    ext_all = pack_all[:, n_adam:]
    ext_sum = ext_all[0]
    for k in range(1, NDEV):
        ext_sum = ext_sum + ext_all[k]
    g_conv_full, dm_cc_tot, _ = _unpack(ext_sum, extras)
    dm_ex_all = jnp.stack([_unpack(ext_all[k], extras)[2] for k in range(NDEV)], axis=1)
    dm_rows = jnp.concatenate([dm_ex_all.reshape(L, NDEV * B, 6 * D), dm_cc_tot[:, None, :],
                               jnp.zeros((L, 24 - NDEV * B - 1, 6 * D), F32)], axis=1)
    dm_loc = lax.dynamic_slice(dm_rows, (0, 0, me * CS), (L, 24, CS))
    g_modw, gc_part = _adaln_bwd(cs, dm_loc, mod_w)
    gc_all = _all_gather("gather_cctx_grad", [gc_part[NDEV * B:NDEV * B + 8]])[0]
    cpad = lambda a: jnp.pad(a[None], ((0, 7), (0, 0)))
    res_cc = _adamw("adamw_cctx", gc_all, cpad(c_ctx), cpad(m_c_ctx), cpad(v_c_ctx), 8)
    cc_out = [r[0] for r in res_cc]

    def waited(name, handles, names):
        srcs, lands = _exchange_wait(name, *handles, grad_x, gather=False)
        own = [lax.dynamic_index_in_dim(s, me, 0, keepdims=False) for s in srcs]
        return dict(zip(names, _with_own(lands, own, me)))

    recv = {0: dict(zip(rest_names, _all_to_all("exchange_grads", blocks[0])))}
    recv[0].update(waited("grads_mlp_wait_0", sent_mlp.pop(0), ["w_mlp1", "w_mlp2"]))
    for l in range(1, L):
        recv[l] = waited("grads_wait_%d" % l, sent.pop(l), big_names)
    recvs = [jnp.stack([recv[l][n] for l in range(L)], axis=1) for n in big_names]
    big_out = {}
    for name, rv in zip(big_names, recvs):
        w = args[name]
        Rr, C = math.prod(w.shape[:-1]), w.shape[-1]
        rt = Rr if Rr * C <= (1 << 18) else Rr // 8
        res = _adamw("adamw_" + name, rv.reshape(NDEV, Rr, C), w.reshape(Rr, C), args["m_" + name].reshape(Rr, C),
                     args["v_" + name].reshape(Rr, C), rt)
        big_out[name] = [r.reshape(w.shape) for r in res]
    res = _adamw("adamw_mod_w", g_modw.reshape(1, L * D, CS), mod_w.reshape(L * D, CS), m_mod_w.reshape(L * D, CS),
                 v_mod_w.reshape(L * D, CS), L * D // 8)
    big_out["mod_w"] = [r.reshape(mod_w.shape) for r in res]
    CW = conv_w.shape[2]
    g_conv = lax.dynamic_slice(g_conv_full, (0, 0, me * CW), (L, 4, CW))
    res = _adamw("adamw_conv_w", g_conv.reshape(1, L * 4, CW), conv_w.reshape(L * 4, CW), m_conv_w.reshape(L * 4, CW),
                 v_conv_w.reshape(L * 4, CW), L * 4)
    big_out["conv_w"] = [r.reshape(conv_w.shape) for r in res]

    weights = ["c_ctx", "mod_w", "mod_b", "norm1_w", "norm2_w", "w_in", "conv_w", "conv_b", "dt_bias", "a_log", "ssd_d",
               "ssd_norm_w", "q_a_norm_w", "w_q_b", "kv_a_norm_w", "w_kv_b", "pool_w", "pool_scale", "w_out", "w_mlp1",
               "w_mlp2", "final_norm_w"]
    outs = [loss, grad_x]
    for kind in range(4):
        for name in weights:
            if name == "c_ctx":
                outs.append(cc_out[kind])
            elif name in big_out:
                outs.append(big_out[name][kind])
            else:
                outs.append(small_out[kind][SMALL_NAMES.index(name)])
    return tuple(outs)
```

```python
import functools
import math

import jax
import jax.numpy as jnp
from jax import lax
from jax.experimental import pallas as pl
from jax.experimental.pallas import tpu as pltpu

F32 = jnp.float32
BF = jnp.bfloat16
MXU = BF

D = 1024
CTX = 256
GRID_W = 64
EPS = 1e-6
H = 6
P = 64
SSD_IN = 384
NST = 128
XBC = 896
CH = 128
QL = 256
NOPE = 64
ROPE = 32
VH = 64
QK = 96
HP = 128
QW = H * HP
PD = 256
FF = 4096
IN_COLS = 2092
ROPE_THETA = 10000.0
PC = 2304
A_Z, A_XBC, A_QA, A_KVA, A_POOL, A_KR, A_DT = 0, 384, 1280, 1536, 1792, 2048, 2176
MIXW = SSD_IN + QW + PD
NDEV = 8
FSH = FF // NDEV
TM = 256
TQ = 256
TK = 256
VMEM_CAP = 64 * 1024 * 1024
ADAM_LR, ADAM_B1, ADAM_B2, ADAM_EPS, ADAM_WD, ADAM_STEP = 0.001, 0.9, 0.999, 1e-08, 0.01, 10


def _nbytes(shape, dtype):
    n = 1
    for s in shape:
        if s is not None:
            n *= s
    return n * jnp.dtype(dtype).itemsize


def _params(sem, block_bytes, extra=0):
    lim = min(2 * block_bytes + extra + (8 << 20), VMEM_CAP - (6 << 20))
    return pltpu.CompilerParams(dimension_semantics=sem, vmem_limit_bytes=int(lim))


def _hbm(arrays):
    return [pltpu.with_memory_space_constraint(a, pltpu.HBM) for a in arrays]


def _dot(a, b):
    return jnp.dot(a.astype(MXU), b.astype(MXU), preferred_element_type=F32)


def _dot_nt(a, b):
    return lax.dot_general(a.astype(MXU), b.astype(MXU), (((1,), (1,)), ((), ())), preferred_element_type=F32)


def _dot_tn(a, b):
    return lax.dot_general(a.astype(MXU), b.astype(MXU), (((0,), (0,)), ((), ())), preferred_element_type=F32)


def _dot01(m01, x):
    b16 = jnp.bfloat16
    m = m01.astype(b16)
    hi = x.astype(b16)
    r1 = x - hi.astype(F32)
    mid = r1.astype(b16)
    lo = (r1 - mid.astype(F32)).astype(b16)
    f = lambda v: jnp.dot(m, v, preferred_element_type=F32)
    return f(hi) + f(mid) + f(lo)


def _sigmoid(x):
    return 1.0 / (1.0 + jnp.exp(-x))


def _silu(x):
    return x * _sigmoid(x)


def _dsilu(x):
    s = _sigmoid(x)
    return s * (1.0 + x * (1.0 - s))


def _iota(shape, dim):
    return lax.broadcasted_iota(jnp.int32, shape, dim)


def _row(ref, k):
    blk = ref[...]
    return jnp.sum(jnp.where(_iota(blk.shape, 0) == k, blk, 0.0), axis=0, keepdims=True)


def _shift_rows(x, k):
    n = x.shape[0]
    return pltpu.roll(x, (-k) % n, axis=0)


class _Rows:
    def __init__(self, B, T):
        self.B, self.T = B, T
        self.nt = T // TM
        self.nct = CTX // TM

    def row(self, F):
        return pl.BlockSpec((None, TM, F), lambda b, i: (b, i, 0))

    def row2(self, F):
        return pl.BlockSpec((None, 2, TM, F), lambda b, i: (b, 0, i, 0))

    def prev8(self, F):
        return pl.BlockSpec((None, 8, F), lambda b, i: (b, jnp.maximum(i * (TM // 8) - 1, 0), 0))

    def next8(self, F):
        last = self.T // 8 - 1
        return pl.BlockSpec((None, 8, F), lambda b, i: (b, jnp.minimum((i + 1) * (TM // 8), last), 0))

    def mod(self):
        nct = self.nct
        return pl.BlockSpec((None, None, 8, D), lambda b, i: (b, jnp.where(i < nct, 0, 1), 0, 0))

    def const(self, shape):
        z = (0,) * len(shape)
        return pl.BlockSpec(tuple(shape), lambda b, i: z)

    def tgt(self, F):
        nct = self.nct
        return pl.BlockSpec((None, TM, F), lambda b, i: (b, jnp.maximum(i - nct, 0), 0))

    def call(self, name, body, ins, outs, scratch=(), extra=0):
        arrays = [a for a, _ in ins]
        in_specs = [s for _, s in ins]
        out_shape = [pltpu.HBM(o.shape, o.dtype) for o, _ in outs]
        out_specs = [s for _, s in outs]
        bb = 0
        for a, s in list(ins) + list(outs):
            if s.block_shape is not None:
                bb += _nbytes(s.block_shape, a.dtype)
        return pl.pallas_call(
            functools.partial(body), name=name, grid=(self.B, self.nt),
            in_specs=in_specs, out_specs=out_specs, out_shape=out_shape, scratch_shapes=list(scratch),
            compiler_params=_params(("arbitrary", "arbitrary"), bb, extra),
        )(*_hbm(arrays))

    def first_of_stream(self, i):
        return jnp.logical_or(i == 0, i == self.nct)

    def last_of_stream(self, i):
        return jnp.logical_or(i == self.nct - 1, i == self.nt - 1)

    def ext(self, i, prev_ref, cur, next_ref):
        pv = prev_ref[...].astype(F32) * jnp.where(self.first_of_stream(i), 0.0, 1.0)
        nx = next_ref[...].astype(F32) * jnp.where(self.last_of_stream(i), 0.0, 1.0)
        return jnp.concatenate([pv, cur, nx], axis=0)

    def stream_pos(self, i, rows):
        start = jnp.where(i < self.nct, 0, CTX)
        n = jnp.where(i < self.nct, CTX, self.T - CTX)
        t = i * TM - 8 - start + _iota((rows, 1), 0)
        return t, n


def _sds(shape, dtype):
    return jax.ShapeDtypeStruct(tuple(shape), dtype)


def _out(shape, dtype):
    return pltpu.HBM(tuple(shape), dtype)


def _norm_mod(x, nw, sh, sc):
    r = lax.rsqrt(jnp.mean(x * x, axis=-1, keepdims=True) + EPS)
    xn = x * r
    return xn * nw * (1.0 + sc) + sh, xn, r


def _norm_mod_bwd(dh, xn, r, nw, sc):
    dsh = jnp.sum(dh, axis=0, keepdims=True)
    dsc = jnp.sum(dh * (xn * nw), axis=0, keepdims=True)
    dnw = jnp.sum(dh * (1.0 + sc) * xn, axis=0, keepdims=True)
    dxn = dh * nw * (1.0 + sc)
    dx = r * (dxn - xn * jnp.mean(dxn * xn, axis=-1, keepdims=True))
    return dx, dsh, dsc, dnw


def _acc_rows(ref, first, rows):
    rid = _iota(ref.shape, 0)
    upd = jnp.zeros(ref.shape, F32)
    for k, v in rows.items():
        upd = upd + jnp.where(rid == k, v, 0.0)

    @pl.when(first)
    def _():
        ref[...] = upd

    @pl.when(jnp.logical_not(first))
    def _():
        ref[...] += upd


def _in_proj(R, x, mod, nw1, w_arr):
    B, T = R.B, R.T

    def body(x_ref, mod_ref, nw_ref, w_ref, z_ref, xbc_ref, qa_ref, kva_ref, pool_ref, kr_ref, dt_ref, h_ref):
        h, _, _ = _norm_mod(x_ref[...], nw_ref[...], _row(mod_ref, 0), _row(mod_ref, 1))
        hb = h.astype(BF)
        h_ref[...] = hb
        p = jnp.dot(hb, w_ref[...], preferred_element_type=F32)
        z_ref[...] = p[:, A_Z:A_XBC]
        xbc_ref[...] = p[:, A_XBC:A_QA]
        qa_ref[...] = p[:, A_QA:A_KVA]
        kva_ref[...] = p[:, A_KVA:A_POOL]
        pool_ref[...] = p[:, A_POOL:A_KR]
        kr_ref[...] = p[:, A_KR:A_DT]
        dt_ref[...] = p[:, A_DT:PC]

    widths = [SSD_IN, XBC, QL, QL, PD, HP, HP]
    outs = [(_sds((B, T, w), F32), R.row(w)) for w in widths] + [(_sds((B, T, D), BF), R.row(D))]
    return R.call("in_proj", body,
                  [(x, R.row(D)), (mod, R.mod()), (nw1, R.const((1, D))), (w_arr, R.const((D, PC)))],
                  outs, extra=8 << 20)


def _conv_pre(ext, w_ref, b_ref):
    return (_row(w_ref, 0) * _shift_rows(ext, -1)[8:8 + TM] + _row(w_ref, 1) * ext[8:8 + TM]
            + _row(w_ref, 2) * _shift_rows(ext, 1)[8:8 + TM] + _row(w_ref, 3) * _shift_rows(ext, 2)[8:8 + TM]
            + b_ref[...])


def _softplus(x):
    return jnp.maximum(x, 0.0) + jnp.log(1.0 + jnp.exp(-jnp.abs(x)))


def _ssd_prep(R, xbc_raw, dt_raw, conv_w8, conv_b, dtb):
    B, T = R.B, R.T

    def body(raw_ref, pv_ref, nx_ref, dtr_ref, w_ref, b_ref, dtb_ref, xbc_ref, dt_ref):
        i = pl.program_id(1)
        ext = R.ext(i, pv_ref, raw_ref[...], nx_ref)
        xbc_ref[...] = _silu(_conv_pre(ext, w_ref, b_ref))
        lane = _iota((TM, HP), 1)
        dtv = _softplus(dtr_ref[...] + dtb_ref[...])
        keep = lane < H
        dt_ref[0] = jnp.where(keep, dtv, 0.0)
        dt_ref[1] = jnp.where(keep, pltpu.roll(dtv, HP - H, axis=1), 0.0)

    return R.call("ssd_prep", body,
                  [(xbc_raw, R.row(XBC)), (xbc_raw, R.prev8(XBC)), (xbc_raw, R.next8(XBC)), (dt_raw, R.row(HP)),
                   (conv_w8, R.const((8, XBC))), (conv_b, R.const((1, XBC))), (dtb, R.const((1, HP)))],
                  [(_sds((B, T, XBC), F32), R.row(XBC)), (_sds((B, 2, T, HP), F32), R.row2(HP))], extra=12 << 20)


def _chunk_index(d, s, nc, nctc):
    if d == 0:
        return s
    return jnp.where(s < nctc, nctc - 1 - s, nc - 1 - (s - nctc))


def _dot01_r(x, m01, passes=3):
    b16 = jnp.bfloat16
    m = m01.astype(b16)
    out, rest = None, x
    for _ in range(passes):
        part = rest.astype(b16)
        rest = rest - part.astype(F32)
        term = jnp.dot(part, m, preferred_element_type=F32)
        out = term if out is None else out + term
    return out


def _scan_common(d, dtv, alog_ref):
    sel = _iota((HP, H * HP), 0) == (_iota((HP, H * HP), 1) >> 7)
    a_all = _dot01_r(-jnp.exp(alog_ref[d]), sel)
    dt_all = _dot01_r(dtv, sel)
    adt_all = dt_all * jnp.concatenate([a_all] * (CH // 8), axis=0)
    row = _iota((CH, CH), 0)
    col = _iota((CH, CH), 1)
    inc = col <= row if d == 0 else col >= row
    inc_t = row <= col if d == 0 else row >= col
    q_all = _dot01(inc, adt_all)
    return a_all, dt_all, adt_all, q_all, inc, inc_t


def _head_decay(h, adt_all, q_all, inc, inc_t):
    q = q_all[:, HP * h:HP * (h + 1)]
    q_t = q.T
    qtot = jnp.sum(adt_all[:, HP * h:HP * (h + 1)], axis=0, keepdims=True)
    lm = jnp.where(inc, jnp.exp(q - q_t), 0.0)
    lm_t = jnp.where(inc_t, jnp.exp(q_t - q), 0.0)
    return lm, lm_t, jnp.exp(q), jnp.exp(qtot), jnp.exp(qtot - q)


def _ssd_scan(B, T, xbc, dt2, alog2):
    nc, nctc = T // CH, CTX // CH

    def chain(d, xbc_ref, dt_ref, alog_ref, y_ref, hin_ref, hs):
        xbc_v = xbc_ref[...]
        dtv = dt_ref[...]
        blk = alog_ref[d]
        adt = dtv * -jnp.exp(jnp.sum(jnp.where(_iota(blk.shape, 0) == 0, blk, 0.0), axis=0, keepdims=True))
        row, col = _iota((CH, CH), 0), _iota((CH, CH), 1)
        inc = col <= row if d == 0 else col >= row
        inc_tf = (row <= col if d == 0 else row >= col).astype(F32)
        q = _dot01(inc, adt)
        hin_ref[...] = hs[d]
        for g in range(2):
            bg = xbc_v[:, SSD_IN + NST * g:SSD_IN + NST * (g + 1)]
            cg = xbc_v[:, SSD_IN + 2 * NST + NST * g:SSD_IN + 2 * NST + NST * (g + 1)]
            gm = _dot_nt(cg, bg)
            for r in range(3):
                h = 3 * g + r
                onehot = (_iota((1, HP), 1) == h).astype(F32)
                adt_h = jnp.sum(adt * onehot, axis=1, keepdims=True)
                qc = jnp.sum(q * onehot, axis=1, keepdims=True)
                dt_h = jnp.sum(dtv * onehot, axis=1, keepdims=True)
                qr = jnp.sum(adt_h * inc_tf, axis=0, keepdims=True)
                qtot = jnp.sum(adt_h, axis=0, keepdims=True)
                lm = jnp.where(inc, jnp.exp(qc - qr), 0.0)
                xh = xbc_v[:, P * h:P * (h + 1)] * dt_h
                hprev = hs[d, P * h:P * (h + 1), :]
                y_ref[:, P * h:P * (h + 1)] = _dot(gm * lm, xh) + jnp.exp(qc) * _dot_nt(cg, hprev)
                hs[d, P * h:P * (h + 1), :] = jnp.exp(qtot) * hprev + _dot_tn(xh * jnp.exp(qtot - qc), bg)

    def body(xf_ref, xb_ref, dtf_ref, dtb_ref, alog_ref, yf_ref, yb_ref, hf_ref, hb_ref, hs):
        @pl.when(pl.program_id(1) == 0)
        def _():
            hs[...] = jnp.zeros(hs.shape, F32)

        chain(0, xf_ref, dtf_ref, alog_ref, yf_ref, hf_ref, hs)
        chain(1, xb_ref, dtb_ref, alog_ref, yb_ref, hb_ref, hs)

    cidx = lambda d: (lambda s: _chunk_index(d, s, nc, nctc))
    specs = lambda d: dict(
        xbc=pl.BlockSpec((None, CH, XBC), lambda b, s: (b, cidx(d)(s), 0)),
        dt=pl.BlockSpec((None, None, CH, HP), lambda b, s: (b, d, cidx(d)(s), 0)),
        y=pl.BlockSpec((None, CH, SSD_IN), lambda b, s: (b, cidx(d)(s), 0)),
        h=pl.BlockSpec((None, None, SSD_IN, NST), lambda b, s: (b, cidx(d)(s), 0, 0)))
    f, r = specs(0), specs(1)
    bb = 2 * (_nbytes((CH, XBC), F32) + _nbytes((CH, HP), F32) + _nbytes((CH, SSD_IN), F32) + _nbytes((SSD_IN, NST), F32))
    return pl.pallas_call(
        functools.partial(body), name="ssd_scan", grid=(B, nc),
        in_specs=[f["xbc"], r["xbc"], f["dt"], r["dt"], pl.BlockSpec((2, 8, HP), lambda b, s: (0, 0, 0))],
        out_specs=[f["y"], r["y"], f["h"], r["h"]],
        out_shape=[_out((B, T, SSD_IN), F32)] * 2 + [_out((B, nc, SSD_IN, NST), F32)] * 2,
        scratch_shapes=[pltpu.VMEM((2, SSD_IN, NST), F32)],
        compiler_params=_params(("arbitrary",) * 2, bb, 12 << 20),
    )(*_hbm([xbc, xbc, dt2, dt2, alog2]))


def _swap8(u):
    lane = _iota(u.shape, 1)
    n = u.shape[1]
    return jnp.where((lane & 15) < 8, pltpu.roll(u, n - 8, axis=1), pltpu.roll(u, 8, axis=1))


def _rope(u, cos, sin_signed):
    return u * cos + _swap8(u) * sin_signed


def _rms(x, w):
    r = lax.rsqrt(jnp.mean(x * x, axis=-1, keepdims=True) + EPS)
    xh = x * r
    return xh * w, xh, r


def _rms_bwd(dy, xh, r, w):
    dw = jnp.sum(dy * xh, axis=0, keepdims=True)
    dxh = dy * w
    return r * (dxh - xh * jnp.mean(dxh * xh, axis=-1, keepdims=True)), dw


def _tile6(t):
    return jnp.concatenate([t] * H, axis=1)


def _per_head(fn, u):
    return jnp.concatenate([fn(u[:, HP * h:HP * (h + 1)]) for h in range(H)], axis=1)


def _mla_prep(R, qa, kva, kr, qnw, kvnw, wq, wk, wv, cos, sin):
    B, T = R.B, R.T
    scale = QK ** -0.5

    def body(qa_ref, kva_ref, kr_ref, qnw_ref, kvnw_ref, wq_ref, wk_ref, wv_ref, cos_ref, sin_ref,
             qt_ref, kt_ref, qtr_ref, ktr_ref, vtr_ref, cq_ref, ckv_ref):
        cq, _, _ = _rms(qa_ref[...], qnw_ref[...])
        ckv, _, _ = _rms(kva_ref[...], kvnw_ref[...])
        cqb, ckvb = cq.astype(BF), ckv.astype(BF)
        cq_ref[...] = cqb
        ckv_ref[...] = ckvb
        cos1, sin1 = cos_ref[...], sin_ref[...]
        q = _per_head(lambda u: _rope(u, cos1, sin1), jnp.dot(cqb, wq_ref[...], preferred_element_type=F32)) * scale
        qt_ref[...] = q.astype(BF)
        kk = _rope(kr_ref[...], cos1, sin1)
        hl = _iota((TM, QW), 1) & (HP - 1)
        k = jnp.dot(ckvb, wk_ref[...], preferred_element_type=F32) + _tile6(kk)
        k = jnp.where(jnp.logical_and(hl >= QK, hl < QK + 3), 1.0, k)
        kt_ref[...] = k.astype(BF)
        v = jnp.dot(ckvb, wv_ref[...], preferred_element_type=F32)
        v = jnp.where(jnp.logical_and(hl >= VH, hl < VH + 4), 1.0, v)
        for h in range(H):
            cols = slice(HP * h, HP * (h + 1))
            qtr_ref[h] = q[:, cols].T.astype(BF)
            ktr_ref[h] = k[:, cols].T.astype(BF)
            vtr_ref[h] = v[:, cols].T.astype(BF)

    tr = (_sds((B, H, HP, T), BF), pl.BlockSpec((None, H, HP, TM), lambda b, i: (b, 0, 0, i)))
    return R.call("mla_prep", body,
                  [(qa, R.row(QL)), (kva, R.row(QL)), (kr, R.row(HP)), (qnw, R.const((1, QL))), (kvnw, R.const((1, QL))),
                   (wq, R.const((QL, QW))), (wk, R.const((QL, QW))), (wv, R.const((QL, QW))),
                   (cos, pl.BlockSpec((TM, HP), lambda b, i: (i, 0))), (sin, pl.BlockSpec((TM, HP), lambda b, i: (i, 0)))],
                  [(_sds((B, T, QW), BF), R.row(QW))] * 2 + [tr] * 3 + [(_sds((B, T, QL), BF), R.row(QL))] * 2, extra=12 << 20)


def _flash_fwd(B, T, q_t, kt, v_t):
    nq, nk = T // TQ, T // TK
    HS = 2

    def body(q_ref, k_ref, v_ref, o_ref, lse_ref, s_scr):
        i = pl.program_id(2)

        def attend(nch):
            ms = []
            for hh in range(HS):
                q_tr = q_ref[hh]
                mrun = None
                for j in range(nch):
                    s = _dot(k_ref[TK * j:TK * (j + 1), HP * hh:HP * (hh + 1)], q_tr)
                    s_scr[hh, j] = s
                    mrun = s if mrun is None else jnp.maximum(mrun, s)
                ms.append(jnp.max(mrun, axis=0, keepdims=True))
            row = _iota((HP, TQ), 0)
            for hh in range(HS):
                acc_t = jnp.zeros((HP, TQ), F32)
                for j in range(nch):
                    acc_t = acc_t + _dot(v_ref[hh, :, TK * j:TK * (j + 1)], jnp.exp(s_scr[hh, j] - ms[hh]))
                l = jnp.sum(jnp.where(row == VH, acc_t, 0.0), axis=0, keepdims=True)
                o_ref[:, HP * hh:HP * (hh + 1)] = jnp.where(row < VH, acc_t / l, 0.0).T
                lse_ref[hh] = ms[hh] + jnp.log(l)

        @pl.when(i < CTX // TQ)
        def _():
            attend(CTX // TK)

        @pl.when(i >= CTX // TQ)
        def _():
            attend(nk)

    bb = HS * (_nbytes((TQ, HP), BF) + 2 * _nbytes((T, HP), BF) + 2 * _nbytes((TQ, HP), F32))
    return pl.pallas_call(
        functools.partial(body), name="flash_fwd", grid=(B, H // HS, nq),
        in_specs=[pl.BlockSpec((None, HS, HP, TQ), lambda b, h, i: (b, h, 0, i)),
                  pl.BlockSpec((None, T, HS * HP), lambda b, h, i: (b, 0, h)),
                  pl.BlockSpec((None, HS, HP, T), lambda b, h, i: (b, h, 0, 0))],
        out_specs=[pl.BlockSpec((None, TQ, HS * HP), lambda b, h, i: (b, i, h)),
                   pl.BlockSpec((None, HS, 1, TQ), lambda b, h, i: (b, h, 0, i))],
        out_shape=[_out((B, T, QW), F32), _out((B, H, 1, T), F32)],
        scratch_shapes=[pltpu.VMEM((HS, nk, TK, TQ), F32)],
        compiler_params=_params(("arbitrary",) * 3, bb, _nbytes((HS, nk, TK, TQ), F32) + (8 << 20)),
    )(*_hbm([q_t, kt, v_t]))


def _pool_terms(R, i, rows):
    t, n = R.stream_pos(i, rows)
    lane = _iota((1, PD), 1)
    half = jnp.where(lane < 64, 1, jnp.where(lane < 128, 2, jnp.where(lane < 192, 4, 8)))
    cnt = (jnp.minimum(t + half, n) - jnp.maximum(t - half, 0)).astype(F32)
    valid = jnp.logical_and(t >= 0, t < n)
    return jnp.where(valid, cnt, 1.0), valid.astype(F32), lane


def _lane_select(lane, a2, a4, a8, a16):
    return jnp.where(lane < 64, a2, jnp.where(lane < 128, a4, jnp.where(lane < 192, a8, a16)))


def _pool_centred(R, i, ext):
    cnt, valid, lane = _pool_terms(R, i, ext.shape[0])
    s2 = ext + _shift_rows(ext, -1)
    s4 = _shift_rows(s2, -1) + _shift_rows(s2, 1)
    s8 = _shift_rows(s4, -2) + _shift_rows(s4, 2)
    s16 = _shift_rows(s8, -4) + _shift_rows(s8, 4)
    return _lane_select(lane, s2, s4, s8, s16) / cnt - ext, cnt, valid, lane


def _pool_fwd(R, u, wbd, scale):
    B, T = R.B, R.T

    def body(u_ref, pv_ref, nx_ref, w_ref, sc_ref, o_ref):
        i = pl.program_id(1)
        ext = R.ext(i, pv_ref, u_ref[...], nx_ref)
        dm, _, _, _ = _pool_centred(R, i, ext)
        o_ref[...] = _dot(dm[8:8 + TM], w_ref[...]) * sc_ref[...]

    return R.call("pool_fwd", body,
                  [(u, R.row(PD)), (u, R.prev8(PD)), (u, R.next8(PD)), (wbd, R.const((PD, PD))), (scale, R.const((1, PD)))],
                  [(_sds((B, T, PD), F32), R.row(PD))], extra=8 << 20)


def _group_mask():
    return _iota((1, SSD_IN), 1) < SSD_IN // 2


def _ssd_gate(yf_ref, yb_ref, xbc_ref, z_ref, dsk_ref):
    ytot = yf_ref[...] + yb_ref[...] + xbc_ref[:, 0:SSD_IN] * dsk_ref[...]
    z = z_ref[...]
    gz = ytot * _silu(z)
    g0 = _group_mask()
    sq = gz * gz
    s0 = jnp.sum(jnp.where(g0, sq, 0.0), axis=1, keepdims=True)
    s1 = jnp.sum(jnp.where(g0, 0.0, sq), axis=1, keepdims=True)
    half = SSD_IN // 2
    r = jnp.where(g0, lax.rsqrt(s0 / half + EPS), lax.rsqrt(s1 / half + EPS))
    return ytot, z, gz, r


def _out_proj(R, x, mod, yf, yb, xbc, z, o, pool, dsk, snw, wout):
    B, T = R.B, R.T

    def body(x_ref, mod_ref, yf_ref, yb_ref, xbc_ref, z_ref, o_ref, pool_ref, dsk_ref, snw_ref, w_ref, xmid_ref, cat_ref,
             mix_ref):
        _, _, gz, r = _ssd_gate(yf_ref, yb_ref, xbc_ref, z_ref, dsk_ref)
        cat_ref[:, 0:SSD_IN] = (gz * r * snw_ref[...]).astype(BF)
        cat_ref[:, SSD_IN:SSD_IN + QW] = o_ref[...].astype(BF)
        cat_ref[:, SSD_IN + QW:MIXW] = pool_ref[...].astype(BF)
        mix = jnp.dot(cat_ref[...], w_ref[...], preferred_element_type=F32)
        mix_ref[...] = mix.astype(BF)
        xmid_ref[...] = x_ref[...] + _row(mod_ref, 2) * mix

    return R.call("out_proj", body,
                  [(x, R.row(D)), (mod, R.mod()), (yf, R.row(SSD_IN)), (yb, R.row(SSD_IN)), (xbc, R.row(XBC)), (z, R.row(SSD_IN)),
                   (o, R.row(QW)),
                   (pool, R.row(PD)), (dsk, R.const((1, SSD_IN))), (snw, R.const((1, SSD_IN))), (wout, R.const((MIXW, D)))],
                  [(_sds((B, T, D), F32), R.row(D)), (_sds((B, T, MIXW), BF), R.row(MIXW)), (_sds((B, T, D), BF), R.row(D))],
                  extra=8 << 20)


def _load_once(first, pairs, sem):
    @pl.when(first)
    def _():
        cps = [pltpu.make_async_copy(src, dst, sem.at[k]) for k, (src, dst) in enumerate(pairs)]
        for cp in cps:
            cp.start()
        for cp in cps:
            cp.wait()


ANY = pl.BlockSpec(memory_space=pl.ANY)


def _mlp_fwd(R, xmid, mod, nw2, w1, w2):
    B, T = R.B, R.T

    def body(x_ref, mod_ref, nw_ref, w1_hbm, w2_hbm, xo_ref, h_ref, u_ref, y_ref, w1_v, w2_v, sem):
        first = jnp.logical_and(pl.program_id(0) == 0, pl.program_id(1) == 0)
        _load_once(first, [(w1_hbm, w1_v), (w2_hbm, w2_v)], sem)
        x = x_ref[...]
        h, _, _ = _norm_mod(x, nw_ref[...], _row(mod_ref, 3), _row(mod_ref, 4))
        hb = h.astype(BF)
        h_ref[...] = hb
        y = jnp.zeros((TM, D), F32)
        for j in range(NDEV):
            u = jnp.dot(hb, w1_v[j], preferred_element_type=F32)
            u_ref[:, FSH * j:FSH * (j + 1)] = u.astype(BF)
            a = jnp.square(jnp.maximum(u, 0.0))
            y = y + jnp.dot(a.astype(BF), w2_v[j], preferred_element_type=F32)
        y_ref[...] = y.astype(BF)
        xo_ref[...] = x + _row(mod_ref, 5) * y

    return R.call("mlp_fwd", body,
                  [(xmid, R.row(D)), (mod, R.mod()), (nw2, R.const((1, D))), (w1, ANY), (w2, ANY)],
                  [(_sds((B, T, D), F32), R.row(D)), (_sds((B, T, D), BF), R.row(D)), (_sds((B, T, FF), BF), R.row(FF)),
                   (_sds((B, T, D), BF), R.row(D))],
                  scratch=[pltpu.VMEM((NDEV, D, FSH), w1.dtype), pltpu.VMEM((NDEV, FSH, D), w2.dtype), pltpu.SemaphoreType.DMA((2,))],
                  extra=(2 * _nbytes((NDEV, D, FSH), BF)) + (8 << 20))


def _loss_head(R, x, tgt, fnw):
    B, T = R.B, R.T

    def body(x_ref, t_ref, w_ref, dx_ref, loss_ref, dw_ref):
        b, i = pl.program_id(0), pl.program_id(1)
        live = jnp.where(i >= R.nct, 1.0, 0.0)
        y, xh, r = _rms(x_ref[...], w_ref[...])
        err = (y - t_ref[...]) * live
        dy = err / D
        dxn, dw = _rms_bwd(dy, xh, r, w_ref[...])
        dx_ref[...] = dxn
        first = jnp.logical_and(b == 0, i == 0)
        part = 0.5 * jnp.sum(jnp.sum(err * err, axis=1, keepdims=True), axis=0, keepdims=True) / D
        _acc_rows(loss_ref, first, {0: jnp.broadcast_to(part, (1, HP))})
        _acc_rows(dw_ref, first, {0: dw})

    return R.call("loss_head", body,
                  [(x, R.row(D)), (tgt, R.tgt(D)), (fnw, R.const((1, D)))],
                  [(_sds((B, T, D), F32), R.row(D)), (_sds((8, HP), F32), R.const((8, HP))), (_sds((8, D), F32), R.const((8, D)))],
                  extra=8 << 20)


def _tn_matmul(name, a, b, tn, sub=1):
    B, T, K = a.shape
    N = b.shape[2]
    nk = 1
    while T % nk or (T // nk) > 1088 or (T // nk) % 16:
        nk += 1
    tk = T // nk
    kt = K if K <= 1536 else 1024
    w = tn // sub
    assert K % kt == 0 and N % tn == 0 and tn % sub == 0

    def body(a_ref, b_ref, o_ref):
        first = jnp.logical_and(pl.program_id(2) == 0, pl.program_id(3) == 0)

        @pl.when(first)
        def _():
            o_ref[...] = jnp.zeros(o_ref.shape, F32)

        acc = _dot_tn(a_ref[...], b_ref[...])
        for s in range(sub):
            o_ref[s] += acc[:, w * s:w * (s + 1)]

    bb = _nbytes((tk, kt), a.dtype) + _nbytes((tk, tn), b.dtype) + _nbytes((kt, tn), F32)
    return pl.pallas_call(
        functools.partial(body), name=name, grid=(N // tn, K // kt, B, nk),
        in_specs=[pl.BlockSpec((None, tk, kt), lambda j, kk, bi, t: (bi, t, kk)),
                  pl.BlockSpec((None, tk, tn), lambda j, kk, bi, t: (bi, t, j))],
        out_specs=pl.BlockSpec((sub, kt, w), lambda j, kk, bi, t: (j, kk, 0)),
        out_shape=_out((sub * N // tn, K, w), F32),
        compiler_params=_params(("arbitrary",) * 4, bb, _nbytes((kt, tn), F32) + (8 << 20)),
    )(*_hbm([a, b]))


def _mlp_bwd(R, dxo, xmid, ubf, ybf, mod, nw2, w1, w2):
    B, T = R.B, R.T

    def body(dxo_ref, x_ref, u_ref, y_ref, mod_ref, nw_ref, w1_hbm, w2_hbm,
             dxm_ref, du_ref, a_ref, dy_ref, dmod_ref, dnw_ref, w1_v, w2_v, sem):
        b, i = pl.program_id(0), pl.program_id(1)
        _load_once(jnp.logical_and(b == 0, i == 0), [(w1_hbm, w1_v), (w2_hbm, w2_v)], sem)
        dxo = dxo_ref[...]
        _, xn, r = _norm_mod(x_ref[...], nw_ref[...], _row(mod_ref, 3), _row(mod_ref, 4))
        dyb = (dxo * _row(mod_ref, 5)).astype(BF)
        dy_ref[...] = dyb
        dg2 = jnp.sum(dxo * y_ref[...].astype(F32), axis=0, keepdims=True)
        dh = jnp.zeros((TM, D), F32)
        for j in range(NDEV):
            rl = jnp.maximum(u_ref[:, FSH * j:FSH * (j + 1)].astype(F32), 0.0)
            a_ref[:, FSH * j:FSH * (j + 1)] = (rl * rl).astype(BF)
            du = (_dot_nt(dyb, w2_v[j]) * (2.0 * rl)).astype(BF)
            du_ref[:, FSH * j:FSH * (j + 1)] = du
            dh = dh + _dot_nt(du, w1_v[j])
        dx, dsh, dsc, dnw = _norm_mod_bwd(dh, xn, r, nw_ref[...], _row(mod_ref, 4))
        dxm_ref[...] = dxo + dx
        _acc_rows(dmod_ref, R.first_of_stream(i), {3: dsh, 4: dsc, 5: dg2})
        _acc_rows(dnw_ref, jnp.logical_and(b == 0, i == 0), {0: dnw})

    return R.call("mlp_bwd", body,
                  [(dxo, R.row(D)), (xmid, R.row(D)), (ubf, R.row(FF)), (ybf, R.row(D)), (mod, R.mod()), (nw2, R.const((1, D))),
                   (w1, ANY), (w2, ANY)],
                  [(_sds((B, T, D), F32), R.row(D)), (_sds((B, T, FF), BF), R.row(FF)), (_sds((B, T, FF), BF), R.row(FF)),
                   (_sds((B, T, D), BF), R.row(D)), (_sds((B, 2, 8, D), F32), R.mod()), (_sds((8, D), F32), R.const((8, D)))],
                  scratch=[pltpu.VMEM((NDEV, D, FSH), w1.dtype), pltpu.VMEM((NDEV, FSH, D), w2.dtype), pltpu.SemaphoreType.DMA((2,))],
                  extra=(2 * _nbytes((NDEV, D, FSH), BF)) + (8 << 20))


def _minus_in_lanes(x, col, first):
    b16 = jnp.bfloat16
    hi = col.astype(b16).astype(F32)
    r1 = col - hi
    mid = r1.astype(b16).astype(F32)
    lane = _iota(x.shape, 1)
    return jnp.where(lane == first, -hi, jnp.where(lane == first + 1, -mid, jnp.where(lane == first + 2, mid - r1, x)))


def _out_bwd(R, dxm, mod, mixbf, yf, yb, xbc, z, o, qt, lse, dsk, snw, wout):
    B, T = R.B, R.T

    def body(dxm_ref, mod_ref, mix_ref, yf_ref, yb_ref, xbc_ref, z_ref, o_ref, q_ref, lse_ref, dsk_ref, snw_ref, w_ref,
             dmix_ref, dz_ref, dyt_ref, dxsk_ref, qs_ref, dos_ref, dotr_ref, dpo_ref, dmod_ref, dvec_ref):
        b, i = pl.program_id(0), pl.program_id(1)
        dxm = dxm_ref[...]
        dmixb = (dxm * _row(mod_ref, 2)).astype(BF)
        dmix_ref[...] = dmixb
        dg1 = jnp.sum(dxm * mix_ref[...].astype(F32), axis=0, keepdims=True)
        dcat = _dot_nt(dmixb, w_ref[...])
        do_v = dcat[:, SSD_IN:SSD_IN + QW]
        doo = do_v * o_ref[...]
        for h in range(H):
            cols = slice(HP * h, HP * (h + 1))
            dd = jnp.sum(doo[:, cols], axis=1, keepdims=True)
            dos_ref[:, cols] = _minus_in_lanes(do_v[:, cols], dd, VH + 1).astype(BF)
            qs_ref[:, cols] = _minus_in_lanes(q_ref[:, cols].astype(F32), lse_ref[h], QK).astype(BF)
            dotr_ref[h] = do_v[:, cols].T.astype(BF)
        dpo_ref[...] = dcat[:, SSD_IN + QW:MIXW]
        dsn = dcat[:, 0:SSD_IN]
        ytot, zv, gz, r = _ssd_gate(yf_ref, yb_ref, xbc_ref, z_ref, dsk_ref)
        gh = gz * r
        dsnw = jnp.sum(dsn * gh, axis=0, keepdims=True)
        dgh = dsn * snw_ref[...]
        g0 = _group_mask()
        pr = dgh * gh
        half = SSD_IN // 2
        m0 = jnp.sum(jnp.where(g0, pr, 0.0), axis=1, keepdims=True) / half
        m1 = jnp.sum(jnp.where(g0, 0.0, pr), axis=1, keepdims=True) / half
        dgz = r * (dgh - gh * jnp.where(g0, m0, m1))
        dyt = dgz * _silu(zv)
        dz_ref[...] = (dgz * ytot * _dsilu(zv)).astype(BF)
        dyt_ref[...] = dyt
        dxsk_ref[...] = dyt * dsk_ref[...]
        ddsk = jnp.sum(dyt * xbc_ref[:, 0:SSD_IN], axis=0, keepdims=True)
        _acc_rows(dmod_ref, R.first_of_stream(i), {2: dg1})
        _acc_rows(dvec_ref, jnp.logical_and(b == 0, i == 0), {0: dsnw, 1: ddsk})

    return R.call("out_bwd", body,
                  [(dxm, R.row(D)), (mod, R.mod()), (mixbf, R.row(D)), (yf, R.row(SSD_IN)), (yb, R.row(SSD_IN)), (xbc, R.row(XBC)),
                   (z, R.row(SSD_IN)),
                   (o, R.row(QW)), (qt, R.row(QW)), (lse, pl.BlockSpec((None, H, TM, 1), lambda b, i: (b, 0, i, 0))),
                   (dsk, R.const((1, SSD_IN))), (snw, R.const((1, SSD_IN))), (wout, R.const((MIXW, D)))],
                  [(_sds((B, T, D), BF), R.row(D)), (_sds((B, T, SSD_IN), BF), R.row(SSD_IN)), (_sds((B, T, SSD_IN), F32), R.row(SSD_IN)),
                   (_sds((B, T, SSD_IN), F32), R.row(SSD_IN)), (_sds((B, T, QW), BF), R.row(QW)), (_sds((B, T, QW), BF), R.row(QW)),
                   (_sds((B, H, HP, T), BF), pl.BlockSpec((None, H, HP, TM), lambda b, i: (b, 0, 0, i))),
                   (_sds((B, T, PD), F32), R.row(PD)),
                   (_sds((B, 2, 8, D), F32), R.mod()), (_sds((8, SSD_IN), F32), R.const((8, SSD_IN)))],
                  extra=8 << 20)


def _pool_bwd(R, dpo, u, wbd, scale):
    B, T = R.B, R.T

    def body(d_ref, dpv_ref, dnx_ref, u_ref, upv_ref, unx_ref, w_ref, sc_ref, du_ref, dw_ref, dsc_ref):
        b, i = pl.program_id(0), pl.program_id(1)
        ext_u = R.ext(i, upv_ref, u_ref[...], unx_ref)
        ext_d = R.ext(i, dpv_ref, d_ref[...], dnx_ref)
        dm, cnt, valid, lane = _pool_centred(R, i, ext_u)
        ddm = _dot_nt(ext_d * sc_ref[...], w_ref[...]) * valid
        e = ddm / cnt
        a2 = e + _shift_rows(e, 1)
        a4 = _shift_rows(a2, -1) + _shift_rows(a2, 1)
        a8 = _shift_rows(a4, -2) + _shift_rows(a4, 2)
        a16 = _shift_rows(a8, -4) + _shift_rows(a8, 4)
        du_ref[...] = (_lane_select(lane, a2, a4, a8, a16) - ddm)[8:8 + TM].astype(BF)
        dmc = dm[8:8 + TM]
        dpo_c = d_ref[...]
        first = jnp.logical_and(b == 0, i == 0)

        @pl.when(first)
        def _():
            dw_ref[...] = jnp.zeros(dw_ref.shape, F32)

        dw_ref[...] += _dot_tn(dmc, dpo_c * sc_ref[...])
        _acc_rows(dsc_ref, first, {0: jnp.sum(dpo_c * _dot(dmc, w_ref[...]), axis=0, keepdims=True)})

    return R.call("pool_bwd", body,
                  [(dpo, R.row(PD)), (dpo, R.prev8(PD)), (dpo, R.next8(PD)), (u, R.row(PD)), (u, R.prev8(PD)), (u, R.next8(PD)),
                   (wbd, R.const((PD, PD))), (scale, R.const((1, PD)))],
                  [(_sds((B, T, PD), BF), R.row(PD)), (_sds((PD, PD), F32), R.const((PD, PD))), (_sds((8, PD), F32), R.const((8, PD)))],
                  extra=8 << 20)


def _flash_bwd(B, T, qs, kt, dos, q_t, k_t, v_t, do_t):
    nk, nq = T // TK, T // TQ
    HS = 2

    def body(k_ref, kt_ref, vt_ref, q_ref, do_ref, qt_ref, dot_ref, dq_ref, dk_ref, dv_ref, s_scr, dp_scr, ds_scr):
        j = pl.program_id(2)
        ctx_keys = jnp.where(j < CTX // TK, 1.0, 0.0)
        for hh in range(HS):
            cols = slice(HP * hh, HP * (hh + 1))
            s_scr[hh] = _dot(q_ref[:, cols], kt_ref[hh])
            dp_scr[hh] = _dot(do_ref[:, cols], vt_ref[hh])
        for hh in range(HS):
            cols = slice(HP * hh, HP * (hh + 1))
            dk_tr = jnp.zeros((HP, TK), F32)
            dv_tr = jnp.zeros((HP, TK), F32)
            for i in range(nq):
                rows = slice(TQ * i, TQ * (i + 1))
                p = jnp.exp(s_scr[hh, rows, :])
                if i < CTX // TQ:
                    p = p * ctx_keys
                ds = (p * dp_scr[hh, rows, :]).astype(BF)
                ds_scr[hh, rows, :] = ds
                dv_tr = dv_tr + _dot(dot_ref[hh, :, rows], p)
                dk_tr = dk_tr + _dot(qt_ref[hh, :, rows], ds)
            dq = _dot(ds_scr[hh], k_ref[:, cols])

            @pl.when(j == 0)
            def _():
                dq_ref[:, cols] = dq

            @pl.when(j > 0)
            def _():
                dq_ref[:, cols] += dq

            dk_ref[:, cols] = dk_tr.T
            dv_ref[:, cols] = dv_tr.T

    tspec = pl.BlockSpec((None, TK, HS * HP), lambda b, h, j: (b, j, h))
    fspec = pl.BlockSpec((None, T, HS * HP), lambda b, h, j: (b, 0, h))
    ttspec = pl.BlockSpec((None, HS, HP, TK), lambda b, h, j: (b, h, 0, j))
    ftspec = pl.BlockSpec((None, HS, HP, T), lambda b, h, j: (b, h, 0, 0))
    bb = HS * (4 * _nbytes((T, HP), BF) + _nbytes((T, HP), F32) + 8 * _nbytes((TK, HP), F32))
    scr = HS * (2 * _nbytes((T, TK), F32) + _nbytes((T, TK), BF))
    return pl.pallas_call(
        functools.partial(body), name="flash_bwd", grid=(B, H // HS, nk),
        in_specs=[tspec, ttspec, ttspec, fspec, fspec, ftspec, ftspec],
        out_specs=[fspec, tspec, tspec],
        out_shape=[_out((B, T, QW), F32)] * 3,
        scratch_shapes=[pltpu.VMEM((HS, T, TK), F32), pltpu.VMEM((HS, T, TK), F32), pltpu.VMEM((HS, T, TK), BF)],
        compiler_params=_params(("arbitrary",) * 3, bb, scr + (8 << 20)),
    )(*_hbm([kt, k_t, v_t, qs, dos, q_t, do_t]))


def _mla_bwd(R, dqt, dkt, dvt, qa, kva, qnw, kvnw, wq, wk, wv, cos, sin):
    B, T = R.B, R.T
    scale = QK ** -0.5

    def body(dq_ref, dk_ref, dv_ref, qa_ref, kva_ref, qnw_ref, kvnw_ref, wq_ref, wk_ref, wv_ref, cos_ref, sin_ref,
             dqp_ref, dkvb_ref, dqa_ref, dkva_ref, dkr_ref, dnw_ref):
        b, i = pl.program_id(0), pl.program_id(1)
        cos1, sin1 = cos_ref[...], sin_ref[...]
        dq = dq_ref[...] * scale
        dqp = _per_head(lambda g: g * cos1 + _swap8(g * sin1), dq).astype(BF)
        dqp_ref[...] = dqp
        dkv = dk_ref[...]
        dkb = dkv.astype(BF)
        dvb = dv_ref[...].astype(BF)
        dkvb_ref[:, :QW] = dkb
        dkvb_ref[:, QW:] = dvb
        dkk = dkv[:, 0:HP]
        for h in range(1, H):
            dkk = dkk + dkv[:, HP * h:HP * (h + 1)]
        lane = _iota((TM, HP), 1)
        rope_lane = jnp.logical_and(lane >= NOPE, lane < NOPE + ROPE)
        dkr_ref[...] = jnp.where(rope_lane, dkk * cos1 + _swap8(dkk * sin1), 0.0).astype(BF)
        _, qh, qr = _rms(qa_ref[...], qnw_ref[...])
        _, kh, kr_ = _rms(kva_ref[...], kvnw_ref[...])
        dcq = _dot_nt(dqp, wq_ref[...])
        dckv = _dot_nt(dkb, wk_ref[...]) + _dot_nt(dvb, wv_ref[...])
        dqa, dqw = _rms_bwd(dcq, qh, qr, qnw_ref[...])
        dkva, dkw = _rms_bwd(dckv, kh, kr_, kvnw_ref[...])
        dqa_ref[...] = dqa.astype(BF)
        dkva_ref[...] = dkva.astype(BF)
        _acc_rows(dnw_ref, jnp.logical_and(b == 0, i == 0), {0: dqw, 1: dkw})

    tab = pl.BlockSpec((TM, HP), lambda b, i: (i, 0))
    return R.call("mla_bwd", body,
                  [(dqt, R.row(QW)), (dkt, R.row(QW)), (dvt, R.row(QW)), (qa, R.row(QL)), (kva, R.row(QL)),
                   (qnw, R.const((1, QL))), (kvnw, R.const((1, QL))), (wq, R.const((QL, QW))), (wk, R.const((QL, QW))),
                   (wv, R.const((QL, QW))), (cos, tab), (sin, tab)],
                  [(_sds((B, T, QW), BF), R.row(QW)), (_sds((B, T, 2 * QW), BF), R.row(2 * QW))]
                  + [(_sds((B, T, QL), BF), R.row(QL))] * 2
                  + [(_sds((B, T, HP), BF), R.row(HP)), (_sds((8, QL), F32), R.const((8, QL)))], extra=8 << 20)


def _ssd_scan_bwd(B, T, dyt, xbc, dt2, alog2, hin_f, hin_b):
    nc, nctc = T // CH, CTX // CH

    def chain(d, dy_ref, xbc_ref, dt_ref, alog_ref, hin_ref, dxbc_ref, ddt_ref, da_ref, dhs_all):
        dhs = dhs_all.at[d]
        xbc_v = xbc_ref[...]
        dyv = dy_ref[...]
        a_all, dt_all, adt_all, q_all, inc, inc_t = _scan_common(d, dt_ref[...], alog_ref)
        ones_p = jnp.ones((P, HP), F32)
        total = lambda m: jnp.sum(jnp.sum(m, axis=0, keepdims=True), axis=1, keepdims=True)
        dq_parts, dqtot_parts, ddtx_parts = [], [], []
        for g in range(2):
            bg = xbc_v[:, SSD_IN + NST * g:SSD_IN + NST * (g + 1)]
            cg = xbc_v[:, SSD_IN + 2 * NST + NST * g:SSD_IN + 2 * NST + NST * (g + 1)]
            bg_t, cg_t = bg.T, cg.T
            gm = _dot(cg, bg_t)
            gm_t = _dot(bg, cg_t)
            dgm = jnp.zeros((CH, CH), F32)
            dgm_t = jnp.zeros((CH, CH), F32)
            dbg = jnp.zeros((CH, NST), F32)
            dcg = jnp.zeros((CH, NST), F32)
            for r in range(3):
                h = 3 * g + r
                lm, lm_t, eq, etot, dte = _head_decay(h, adt_all, q_all, inc, inc_t)
                eq_p, etot_p, dte_p = eq[:, :P], etot[:, :P], dte[:, :P]
                dt_p = dt_all[:, HP * h:HP * h + P]
                xs_h = xbc_v[:, P * h:P * (h + 1)]
                xh = xs_h * dt_p
                sm, sm_t = gm * lm, gm_t * lm_t
                dy_h = dyv[:, P * h:P * (h + 1)]
                hprev = hin_ref[P * h:P * (h + 1), :].T
                dho = dhs[:, P * h:P * (h + 1)]
                ds = _dot_nt(dy_h, xh)
                ds_t = _dot_nt(xh, dy_h)
                dx = _dot(sm_t, dy_h)
                edy = eq_p * dy_h
                yo = _dot(cg, hprev)
                dcg = dcg + _dot_nt(edy, hprev)
                dhin = _dot(cg_t, edy) + etot_p * dho
                zs = _dot(bg, dho)
                dx = dx + dte_p * zs
                wm = dte_p * xh * zs
                dbg = dbg + _dot_nt(xh * dte_p, dho)
                dgm = dgm + ds * lm
                dgm_t = dgm_t + ds_t * lm_t
                rs = jnp.sum(ds * sm - ds_t * sm_t, axis=1, keepdims=True) + jnp.sum(edy * yo - wm, axis=1, keepdims=True)
                dq_parts.append(jnp.broadcast_to(rs, (CH, HP)))
                dqtot_parts.append(total(hprev * dho) * etot + total(wm))
                dxbc_ref[:, P * h:P * (h + 1)] = dx * dt_p
                ddtx_parts.append(_dot01_r(dx * xs_h, ones_p, passes=2))
                dhs[:, P * h:P * (h + 1)] = dhin
            dcg = dcg + _dot(dgm, bg)
            dbg = dbg + _dot(dgm_t, cg)
            dxbc_ref[:, SSD_IN + NST * g:SSD_IN + NST * (g + 1)] = dbg
            dxbc_ref[:, SSD_IN + 2 * NST + NST * g:SSD_IN + 2 * NST + NST * (g + 1)] = dcg
        cat = lambda parts: jnp.concatenate(parts, axis=1)
        dadt_all = _dot01(inc_t, cat(dq_parts)) + cat(dqtot_parts)
        ddt_all = cat(ddtx_parts) + dadt_all * jnp.concatenate([a_all] * (CH // 8), axis=0)
        da_all = jnp.sum(dadt_all * dt_all, axis=0, keepdims=True)
        lane, lane1 = _iota((CH, HP), 1), _iota((1, HP), 1)
        ddt = jnp.zeros((CH, HP), F32)
        da = jnp.zeros((1, HP), F32)
        for h in range(H):
            ddt = ddt + jnp.where(lane == h, ddt_all[:, HP * h:HP * (h + 1)], 0.0)
            da = da + jnp.where(lane1 == h, da_all[:, HP * h:HP * (h + 1)], 0.0)
        ddt_ref[...] = ddt
        da_ref[d] += jnp.where(_iota((8, HP), 0) == 0, da, 0.0)

    def body(dyf_ref, dyb_ref, xf_ref, xb_ref, dtf_ref, dtb_ref, alog_ref, hf_ref, hb_ref,
             dxf_ref, dxb_ref, ddtf_ref, ddtb_ref, da_ref, dhs):
        @pl.when(pl.program_id(1) == 0)
        def _():
            dhs[...] = jnp.zeros(dhs.shape, F32)
            da_ref[...] = jnp.zeros(da_ref.shape, F32)

        chain(0, dyf_ref, xf_ref, dtf_ref, alog_ref, hf_ref, dxf_ref, ddtf_ref, da_ref, dhs)
        chain(1, dyb_ref, xb_ref, dtb_ref, alog_ref, hb_ref, dxb_ref, ddtb_ref, da_ref, dhs)

    cidx = lambda d: (lambda s: _chunk_index(d, nc - 1 - s, nc, nctc))
    specs = lambda d: dict(
        dy=pl.BlockSpec((None, CH, SSD_IN), lambda b, s: (b, cidx(d)(s), 0)),
        xbc=pl.BlockSpec((None, CH, XBC), lambda b, s: (b, cidx(d)(s), 0)),
        dt=pl.BlockSpec((None, None, CH, HP), lambda b, s: (b, d, cidx(d)(s), 0)),
        h=pl.BlockSpec((None, None, SSD_IN, NST), lambda b, s: (b, cidx(d)(s), 0, 0)),
        ddt=pl.BlockSpec((None, CH, HP), lambda b, s: (b, cidx(d)(s), 0)))
    f, r = specs(0), specs(1)
    bb = 2 * (2 * _nbytes((CH, XBC), F32) + 2 * _nbytes((CH, HP), F32) + _nbytes((CH, SSD_IN), F32) + _nbytes((SSD_IN, NST), F32))
    return pl.pallas_call(
        functools.partial(body), name="ssd_scan_bwd", grid=(B, nc),
        in_specs=[f["dy"], r["dy"], f["xbc"], r["xbc"], f["dt"], r["dt"], pl.BlockSpec((2, 8, HP), lambda b, s: (0, 0, 0)),
                  f["h"], r["h"]],
        out_specs=[f["xbc"], r["xbc"], f["ddt"], r["ddt"], pl.BlockSpec((None, 2, 8, HP), lambda b, s: (b, 0, 0, 0))],
        out_shape=[_out((B, T, XBC), F32)] * 2 + [_out((B, T, HP), F32)] * 2 + [_out((B, 2, 8, HP), F32)],
        scratch_shapes=[pltpu.VMEM((2, NST, SSD_IN), F32)],
        compiler_params=_params(("arbitrary",) * 2, bb, 16 << 20),
    )(*_hbm([dyt, dyt, xbc, xbc, dt2, dt2, alog2, hin_f, hin_b]))


def _ssd_prep_bwd(R, dxbc_f, dxbc_b, dxsk, ddt_f, ddt_b, xbc_raw, dt_raw, conv_w8, conv_b, dtb):
    B, T = R.B, R.T

    def body(dxf_ref, dxb_ref, dsk_ref, ddtf_ref, ddtb2_ref, raw_ref, pv_ref, nx_ref, dtr_ref, w_ref, b_ref, dtb_ref,
             dpre_ref, ddtr_ref, dcw_ref, dvec_ref, ddtb_ref):
        b, i = pl.program_id(0), pl.program_id(1)
        ext = R.ext(i, pv_ref, raw_ref[...], nx_ref)
        pre = _conv_pre(ext, w_ref, b_ref)
        dxbc = dxf_ref[...] + dxb_ref[...]
        skip = jnp.concatenate([dsk_ref[...], jnp.zeros((TM, XBC - SSD_IN), F32)], axis=1)
        dpre = (dxbc + skip) * _dsilu(pre)
        dpre_ref[...] = dpre
        first = jnp.logical_and(b == 0, i == 0)
        taps = {k: jnp.sum(dpre * _shift_rows(ext, k - 1)[8:8 + TM], axis=0, keepdims=True) for k in range(4)}
        _acc_rows(dcw_ref, first, taps)
        _acc_rows(dvec_ref, first, {0: jnp.sum(dpre, axis=0, keepdims=True)})
        ddt = ddtf_ref[...] + pltpu.roll(ddtb2_ref[...], H, axis=1)
        ddtr = ddt * _sigmoid(dtr_ref[...] + dtb_ref[...])
        ddtr = jnp.where(_iota((TM, HP), 1) < 2 * H, ddtr, 0.0)
        ddtr_ref[...] = ddtr.astype(BF)
        _acc_rows(ddtb_ref, first, {0: jnp.sum(ddtr, axis=0, keepdims=True)})

    return R.call("ssd_prep_bwd", body,
                  [(dxbc_f, R.row(XBC)), (dxbc_b, R.row(XBC)), (dxsk, R.row(SSD_IN)), (ddt_f, R.row(HP)), (ddt_b, R.row(HP)),
                   (xbc_raw, R.row(XBC)), (xbc_raw, R.prev8(XBC)),
                   (xbc_raw, R.next8(XBC)), (dt_raw, R.row(HP)), (conv_w8, R.const((8, XBC))), (conv_b, R.const((1, XBC))),
                   (dtb, R.const((1, HP)))],
                  [(_sds((B, T, XBC), F32), R.row(XBC)), (_sds((B, T, HP), BF), R.row(HP)), (_sds((8, XBC), F32), R.const((8, XBC))),
                   (_sds((8, XBC), F32), R.const((8, XBC))), (_sds((8, HP), F32), R.const((8, HP)))], extra=16 << 20)


def _in_bwd(R, dxm, x, mod, nw1, dz, dpre, dqa, dkva, dpool, dkr, ddtr, conv_w8, w_arr, latent_only=False):
    B, T = R.B, R.T
    dx_out = (_sds((B, T - CTX, D), F32), R.tgt(D)) if latent_only else (_sds((B, T, D), F32), R.row(D))

    def body(dxm_ref, x_ref, mod_ref, nw_ref, dz_ref, dp_ref, dpp_ref, dpn_ref, dqa_ref, dkva_ref, dpool_ref, dkr_ref, ddt_ref,
             cw_ref, w_ref, dx_ref, dproj_ref, dmod_ref, dnw_ref):
        b, i = pl.program_id(0), pl.program_id(1)
        ext = R.ext(i, dpp_ref, dp_ref[...], dpn_ref)
        draw = (_row(cw_ref, 0) * _shift_rows(ext, 1)[8:8 + TM] + _row(cw_ref, 1) * ext[8:8 + TM]
                + _row(cw_ref, 2) * _shift_rows(ext, -1)[8:8 + TM] + _row(cw_ref, 3) * _shift_rows(ext, -2)[8:8 + TM])
        dproj_ref[:, A_Z:A_XBC] = dz_ref[...]
        dproj_ref[:, A_XBC:A_QA] = draw.astype(BF)
        dproj_ref[:, A_QA:A_KVA] = dqa_ref[...]
        dproj_ref[:, A_KVA:A_POOL] = dkva_ref[...]
        dproj_ref[:, A_POOL:A_KR] = dpool_ref[...]
        dproj_ref[:, A_KR:A_DT] = dkr_ref[...]
        dproj_ref[:, A_DT:PC] = ddt_ref[...]
        dh = _dot_nt(dproj_ref[...], w_ref[...])
        _, xn, r = _norm_mod(x_ref[...], nw_ref[...], _row(mod_ref, 0), _row(mod_ref, 1))
        dx, dsh, dsc, dnw = _norm_mod_bwd(dh, xn, r, nw_ref[...], _row(mod_ref, 1))
        dx_ref[...] = dxm_ref[...] + dx
        _acc_rows(dmod_ref, R.first_of_stream(i), {0: dsh, 1: dsc})
        _acc_rows(dnw_ref, jnp.logical_and(b == 0, i == 0), {0: dnw})

    return R.call("in_bwd", body,
                  [(dxm, R.row(D)), (x, R.row(D)), (mod, R.mod()), (nw1, R.const((1, D))), (dz, R.row(SSD_IN)), (dpre, R.row(XBC)),
                   (dpre, R.prev8(XBC)), (dpre, R.next8(XBC)), (dqa, R.row(QL)), (dkva, R.row(QL)), (dpool, R.row(PD)), (dkr, R.row(HP)),
                   (ddtr, R.row(HP)), (conv_w8, R.const((8, XBC))), (w_arr, R.const((D, PC)))],
                  [dx_out, (_sds((B, T, PC), BF), R.row(PC)), (_sds((B, 2, 8, D), F32), R.mod()),
                   (_sds((8, D), F32), R.const((8, D)))], extra=12 << 20)


def _adaln_fwd(cs, mod_w):
    L, _, C = mod_w.shape

    def body(c_ref, w_ref, o_ref):
        s = _silu(c_ref[...]).astype(BF)
        for l in range(L):
            o_ref[l] = jnp.dot(s, w_ref[l].astype(BF), preferred_element_type=F32)

    return pl.pallas_call(functools.partial(body), name="adaln_fwd", out_shape=_sds((L, 24, C), F32),
                          compiler_params=_params(None, _nbytes(mod_w.shape, F32) + _nbytes((L, 24, C), F32), 8 << 20))(cs, mod_w)


def _adaln_bwd(cs, dm, mod_w):
    L, _, C = mod_w.shape

    def body(c_ref, dm_ref, w_ref, gw_ref, gc_ref):
        c = c_ref[...]
        s = _silu(c).astype(BF)
        acc = jnp.zeros((24, D), F32)
        for l in range(L):
            dmb = dm_ref[l].astype(BF)
            gw_ref[l] = _dot_tn(s, dmb)
            acc = acc + _dot_nt(dmb, w_ref[l])
        gc_ref[...] = acc * _dsilu(c)

    return pl.pallas_call(functools.partial(body), name="adaln_bwd", out_shape=[_sds((L, D, C), F32), _sds((24, D), F32)],
                          compiler_params=_params(None, 2 * _nbytes(mod_w.shape, F32), 8 << 20))(cs, dm, mod_w)


def _sum_blocks(name, parts):
    Pn, Rr, C = parts.shape

    def body(p_ref, o_ref):
        acc = p_ref[0]
        for k in range(1, Pn):
            acc = acc + p_ref[k]
        o_ref[...] = acc

    return pl.pallas_call(functools.partial(body), name=name, out_shape=_sds((Rr, C), F32),
                          compiler_params=_params(None, _nbytes(parts.shape, F32), 4 << 20))(parts)


def _adamw(name, parts, w, m, v, rt):
    Pn, Rr, C = parts.shape
    c1 = 1.0 - ADAM_B1 ** ADAM_STEP
    c2 = 1.0 - ADAM_B2 ** ADAM_STEP

    def body(p_ref, w_ref, m_ref, v_ref, g_ref, d_ref, nm_ref, nv_ref):
        g = p_ref[0].astype(F32)
        for k in range(1, Pn):
            g = g + p_ref[k].astype(F32)
        mn = ADAM_B1 * m_ref[...] + (1.0 - ADAM_B1) * g
        vn = ADAM_B2 * v_ref[...] + (1.0 - ADAM_B2) * jnp.square(g)
        g_ref[...] = g
        nm_ref[...] = mn
        nv_ref[...] = vn
        d_ref[...] = -ADAM_LR * ((mn / c1) / (jnp.sqrt(vn / c2) + ADAM_EPS) + ADAM_WD * w_ref[...])

    spec = pl.BlockSpec((rt, C), lambda i: (i, 0))
    bb = Pn * _nbytes((rt, C), parts.dtype) + 7 * _nbytes((rt, C), F32)
    return pl.pallas_call(
        functools.partial(body), name=name, grid=(Rr // rt,),
        in_specs=[pl.BlockSpec((Pn, rt, C), lambda i: (0, i, 0)), spec, spec, spec],
        out_specs=[spec] * 4, out_shape=[_out((Rr, C), F32)] * 4,
        compiler_params=_params(("arbitrary",), bb, 4 << 20),
    )(*_hbm([parts, w, m, v]))


MESH = pl.DeviceIdType.MESH


def _my_pos():
    return lax.axis_index("x"), lax.axis_index("y"), lax.axis_index("c")


def _dev_index(x, y, c):
    return 4 * x + 2 * y + c


def _all_gather(name, shards):
    n = len(shards)

    def body(*refs):
        ins, outs = refs[:n], refs[n:2 * n]
        send_sems, recv_sems, local_sem = refs[2 * n:]
        x, y, c = _my_pos()
        me, sibling = (x, y, c), (x, y, 1 - c)
        chips = [(1 - x, y), (x, 1 - y), (1 - x, 1 - y)]

        def copy(t, k, block, to, src=None):
            slot = outs[t].at[_dev_index(*block)]
            return pltpu.make_async_remote_copy(
                src_ref=slot if src is None else src, dst_ref=slot,
                send_sem=send_sems.at[t, k], recv_sem=recv_sems.at[t, k], device_id=to, device_id_type=MESH)

        mine = [pltpu.make_async_copy(ins[t], outs[t].at[_dev_index(*me)], local_sem.at[t]) for t in range(n)]
        for cp in mine:
            cp.start()
        first = []
        for t in range(n):
            first.append(copy(t, 0, me, sibling, src=ins[t]))
            first += [copy(t, 1 + j, me, (*chip, c), src=ins[t]) for j, chip in enumerate(chips)]
        for cp in first:
            cp.start()
        passed = []
        for j, chip in enumerate(chips):
            for t in range(n):
                copy(t, 1 + j, (*chip, c), me).wait_recv()
                cp = copy(t, 4 + j, (*chip, c), sibling)
                cp.start()
                passed.append(cp)
        for t in range(n):
            copy(t, 0, sibling, me).wait_recv()
            for j, chip in enumerate(chips):
                copy(t, 4 + j, (*chip, 1 - c), me).wait_recv()
        for cp in first + passed:
            cp.wait_send()
        for cp in mine:
            cp.wait()

    return pl.pallas_call(
        functools.partial(body), name=name,
        in_specs=[ANY] * n, out_specs=[ANY] * n,
        out_shape=[_sds((NDEV,) + s.shape, s.dtype) for s in shards],
        scratch_shapes=[pltpu.SemaphoreType.DMA((n, 7)), pltpu.SemaphoreType.DMA((n, 7)), pltpu.SemaphoreType.DMA((n,))],
    )(*shards)


def _all_to_all(name, parts):
    n = len(parts)

    def body(*refs):
        ins, outs = refs[:n], refs[n:2 * n]
        send_sems, recv_sems, local_sem = refs[2 * n:]
        x, y, c = _my_pos()
        me = _dev_index(x, y, c)
        peers = [(x ^ ((k >> 2) & 1), y ^ ((k >> 1) & 1), c ^ (k & 1)) for k in range(1, NDEV)]
        mine = [pltpu.make_async_copy(ins[t].at[me], outs[t].at[me], local_sem.at[t]) for t in range(n)]
        for cp in mine:
            cp.start()
        sends = []
        for t in range(n):
            for k, peer in enumerate(peers):
                cp = pltpu.make_async_remote_copy(
                    src_ref=ins[t].at[_dev_index(*peer)], dst_ref=outs[t].at[me],
                    send_sem=send_sems.at[t, k], recv_sem=recv_sems.at[t, k], device_id=peer, device_id_type=MESH)
                cp.start()
                sends.append(cp)
        for t in range(n):
            for k, peer in enumerate(peers):
                slot = outs[t].at[_dev_index(*peer)]
                pltpu.make_async_remote_copy(
                    src_ref=slot, dst_ref=slot, send_sem=send_sems.at[t, k], recv_sem=recv_sems.at[t, k],
                    device_id=peer, device_id_type=MESH).wait_recv()
        for cp in sends:
            cp.wait_send()
        for cp in mine:
            cp.wait()

    return pl.pallas_call(
        functools.partial(body), name=name,
        in_specs=[ANY] * n, out_specs=[ANY] * n,
        out_shape=[_sds(p.shape, p.dtype) for p in parts],
        scratch_shapes=[pltpu.SemaphoreType.DMA((n, 7)), pltpu.SemaphoreType.DMA((n, 7)), pltpu.SemaphoreType.DMA((n,))],
    )(*parts)


SEM = pl.BlockSpec(memory_space=pltpu.SEMAPHORE)
IN_HBM = pl.BlockSpec(memory_space=pltpu.HBM)
DATAFLOW = pltpu.SideEffectType.DATAFLOW_SIDE_EFFECTING


def _flip_peers(x, y, c):
    return [(x ^ ((k >> 2) & 1), y ^ ((k >> 1) & 1), c ^ (k & 1)) for k in range(1, NDEV)]


def _split_copies(srcs, lands, send_sems, recv_sems, gather):
    x, y, c = _my_pos()
    me = _dev_index(x, y, c)
    out = []
    for t in range(len(srcs)):
        for k, peer in enumerate(_flip_peers(x, y, c)):
            p = _dev_index(*peer)
            src = srcs[t] if gather else srcs[t].at[p]
            sems = dict(send_sem=send_sems.at[7 * t + k], recv_sem=recv_sems.at[7 * t + k], device_id=peer, device_id_type=MESH)
            out.append((pltpu.make_async_remote_copy(src_ref=src, dst_ref=lands[t].at[me], **sems),
                        pltpu.make_async_remote_copy(src_ref=src, dst_ref=lands[t].at[p], **sems)))
    return out


def _exchange_start(name, collective_id, srcs, gather):
    n = len(srcs)
    lands = [lax.empty(((NDEV,) + s.shape) if gather else s.shape, s.dtype) for s in srcs]

    def body(*refs):
        src_refs, land_refs = refs[:n], refs[n:2 * n]
        send_sems, recv_sems = refs[2 * n], refs[2 * n + 1]
        token = refs[-1]
        barrier = pltpu.get_barrier_semaphore()
        for peer in _flip_peers(*_my_pos()):
            pl.semaphore_signal(barrier, inc=1, device_id=peer, device_id_type=MESH)
        pl.semaphore_wait(barrier, NDEV - 1)
        for send, _ in _split_copies(src_refs, land_refs, send_sems, recv_sems, gather):
            send.start()
        token[...] = jnp.zeros(token.shape, token.dtype)

    hbm = lambda a: pltpu.HBM(a.shape, a.dtype)
    res = pl.pallas_call(
        functools.partial(body), name=name,
        out_shape=[pltpu.SemaphoreType.DMA((7 * n,)), pltpu.SemaphoreType.DMA((7 * n,))] + [hbm(s) for s in srcs]
        + [hbm(a) for a in lands] + [_sds((8, HP), F32)],
        in_specs=[IN_HBM] * (2 * n), out_specs=[SEM, SEM] + [IN_HBM] * (2 * n) + [pl.BlockSpec(memory_space=pltpu.VMEM)],
        input_output_aliases={i: 2 + i for i in range(2 * n)},
        compiler_params=pltpu.CompilerParams(has_side_effects=DATAFLOW, collective_id=collective_id),
    )(*_hbm(list(srcs) + lands))
    return res[0], res[1], list(res[2:2 + n]), list(res[2 + n:2 + 2 * n]), res[-1]


def _exchange_wait(name, send_sems, recv_sems, srcs, lands, after, gather):
    n = len(srcs)

    def body(*refs):
        src_refs, land_refs = refs[:n], refs[n:2 * n]
        for _, recv in _split_copies(src_refs, land_refs, refs[2 * n], refs[2 * n + 1], gather):
            recv.wait_send()
            recv.wait_recv()

    hbm = lambda a: pltpu.HBM(a.shape, a.dtype)
    res = pl.pallas_call(
        functools.partial(body), name=name,
        out_shape=[hbm(s) for s in srcs] + [hbm(a) for a in lands],
        in_specs=[IN_HBM] * (2 * n) + [SEM, SEM, ANY], out_specs=[IN_HBM] * (2 * n),
        input_output_aliases={i: i for i in range(2 * n)},
        compiler_params=pltpu.CompilerParams(has_side_effects=DATAFLOW),
    )(*srcs, *lands, send_sems, recv_sems, after)
    return list(res[:n]), list(res[n:])


def _with_own(lands, own_blocks, me):
    out = []
    for land, own in zip(lands, own_blocks):
        out.append(lax.dynamic_update_slice(land, own[None], (me,) + (0,) * own.ndim))
    return out


def _arrange_w_in(w):
    z = lambda n: jnp.zeros((w.shape[0], n), w.dtype)
    return jnp.concatenate([w[:, 0:1280], w[:, 1292:1548], w[:, 1548:1804], w[:, 1836:2092],
                            z(64), w[:, 1804:1836], z(32), w[:, 1280:1292], z(HP - 2 * H)], axis=1)


def _unarrange_w_in(g):
    return jnp.concatenate([g[:, 0:1280], g[:, A_DT:A_DT + 2 * H], g[:, A_QA:A_KVA], g[:, A_KVA:A_POOL],
                            g[:, A_KR + NOPE:A_KR + NOPE + ROPE], g[:, A_POOL:A_KR]], axis=1)


def _pad_heads(w, width):
    k = w.shape[0]
    return jnp.pad(w.reshape(k, H, width), ((0, 0), (0, 0), (0, HP - width))).reshape(k, H * HP)


def _unpad_heads(g, width):
    k = g.shape[0]
    return g.reshape(k, H, HP)[:, :, :width].reshape(k, H * width)


def _arrange_w_out(w):
    att = jnp.pad(w[SSD_IN:2 * SSD_IN].reshape(H, VH, D), ((0, 0), (0, HP - VH), (0, 0))).reshape(QW, D)
    return jnp.concatenate([w[0:SSD_IN], att, w[2 * SSD_IN:]], axis=0)


def _unarrange_w_out(g):
    att = g[SSD_IN:SSD_IN + QW].reshape(H, HP, D)[:, :VH].reshape(SSD_IN, D)
    return jnp.concatenate([g[0:SSD_IN], att, g[SSD_IN + QW:]], axis=0)


def _rope_tables(T):
    n = T - CTX
    rows = n // GRID_W
    pairs = ROPE // 4
    inv = ROPE_THETA ** (-jnp.arange(pairs, dtype=F32) / pairs)
    ar = jnp.arange(rows, dtype=F32)[:, None] * inv
    ac = jnp.arange(GRID_W, dtype=F32)[:, None] * inv
    by_row = lambda a: jnp.repeat(a, GRID_W, axis=0)
    by_col = lambda a: jnp.tile(a, (rows, 1))
    cos = jnp.concatenate([by_row(jnp.cos(ar))] * 2 + [by_col(jnp.cos(ac))] * 2, axis=1)
    sin = jnp.concatenate([-by_row(jnp.sin(ar)), by_row(jnp.sin(ar)), -by_col(jnp.sin(ac)), by_col(jnp.sin(ac))], axis=1)
    ones, zeros = jnp.ones((n, NOPE), F32), jnp.zeros((n, NOPE), F32)
    cos = jnp.concatenate([ones, cos, ones[:, :HP - QK]], axis=1)
    sin = jnp.concatenate([zeros, sin, zeros[:, :HP - QK]], axis=1)
    return (jnp.concatenate([jnp.ones((CTX, HP), F32), cos], axis=0),
            jnp.concatenate([jnp.zeros((CTX, HP), F32), sin], axis=0))


def _lane_pad(v, n):
    return jnp.pad(v, (0, n - v.shape[0]))[None, :]


def _layer_weights(w_in, w_q_b, w_kv_b, conv_w, pool_w):
    kv = w_kv_b.reshape(QL, H, NOPE + VH)
    wbd = jnp.concatenate([jnp.pad(pool_w[g], ((0, 0), (64 * g, PD - 64 * (g + 1)))) for g in range(4)], axis=0)
    return dict(
        w_in=_arrange_w_in(w_in).astype(BF),
        wq=_pad_heads(w_q_b, QK).astype(BF),
        wk=_pad_heads(kv[:, :, :NOPE].reshape(QL, H * NOPE), NOPE).astype(BF),
        wv=_pad_heads(kv[:, :, NOPE:].reshape(QL, H * VH), VH).astype(BF),
        conv_w8=jnp.pad(conv_w, ((0, 4), (0, 0))), wbd=wbd)


def _layer_fwd(R, x, mod, lw, sp, late, cos, sin):
    B, T = R.B, R.T
    z, xbc_raw, qa, kva, pool_in, kr, dt_raw, h1 = _in_proj(R, x, mod, sp["nw1"], lw["w_in"])
    xbc, dt2 = _ssd_prep(R, xbc_raw, dt_raw, lw["conv_w8"], sp["conv_b"], sp["dtb"])
    yf, yb, hin_f, hin_b = _ssd_scan(B, T, xbc, dt2, sp["alog2"])
    qt, kt, q_t, k_t, v_t, cq, ckv = _mla_prep(R, qa, kva, kr, sp["qnw"], sp["kvnw"], lw["wq"], lw["wk"], lw["wv"], cos, sin)
    o, lse = _flash_fwd(B, T, q_t, kt, v_t)
    pool = _pool_fwd(R, pool_in, lw["wbd"], sp["pscale"])[0]
    w_out, w1, w2 = late(o)
    w_out = _arrange_w_out(w_out).astype(BF)
    xmid, cat, mixbf = _out_proj(R, x, mod, yf, yb, xbc, z, o, pool, sp["dsk"], sp["snw"], w_out)
    xo, h2, ubf, ybf = _mlp_fwd(R, xmid, mod, sp["nw2"], w1, w2)
    saved = dict(x=x, z=z, xbc_raw=xbc_raw, qa=qa, kva=kva, pool_in=pool_in, dt_raw=dt_raw, h1=h1, xbc=xbc, dt2=dt2,
                 yf=yf, yb=yb, hin_f=hin_f, hin_b=hin_b, qt=qt, kt=kt, q_t=q_t, k_t=k_t, v_t=v_t, cq=cq, ckv=ckv, o=o, lse=lse,
                 cat=cat, mixbf=mixbf, xmid=xmid, h2=h2, ubf=ubf, ybf=ybf, w_out=w_out, w1=w1, w2=w2)
    return xo, saved


def _layer_bwd(R, dxo, sv, mod, lw, sp, cos, sin, on_mlp=None, latent_only=False):
    B, T = R.B, R.T
    dxm, du, abf, dyb, dmod_a, dnw2 = _mlp_bwd(R, dxo, sv["xmid"], sv["ubf"], sv["ybf"], mod, sp["nw2"], sv["w1"], sv["w2"])
    g_w1 = _tn_matmul("dw_mlp1", sv["h2"], du, 4 * FSH, sub=4)
    g_w2 = _tn_matmul("dw_mlp2", abf, dyb, D)[0].reshape(NDEV, FSH, D)
    snw = sp["snw"]
    tok = on_mlp(g_w1, g_w2) if on_mlp is not None else None
    if tok is not None:
        snw = snw + tok
    lse_col = sv["lse"].reshape(B, H, T, 1)
    dmix, dz, dyt, dxsk, qs, dos, do_t, dpo, dmod_b, dvec_o = _out_bwd(R, dxm, mod, sv["mixbf"], sv["yf"], sv["yb"], sv["xbc"], sv["z"],
                                                                 sv["o"], sv["qt"], lse_col, sp["dsk"], snw, sv["w_out"])
    g_wout = _unarrange_w_out(_tn_matmul("dw_out", sv["cat"], dmix, D)[0])
    dpool_in, g_wbd, dpsc = _pool_bwd(R, dpo, sv["pool_in"], lw["wbd"], sp["pscale"])
    dqt, dkt, dvt = _flash_bwd(B, T, qs, sv["kt"], dos, sv["q_t"], sv["k_t"], sv["v_t"], do_t)
    dqp, dkvb, dqa, dkva, dkr, dnw_qk = _mla_bwd(R, dqt, dkt, dvt, sv["qa"], sv["kva"], sp["qnw"], sp["kvnw"],
                                                 lw["wq"], lw["wk"], lw["wv"], cos, sin)
    g_wq = _unpad_heads(_tn_matmul("dw_q", sv["cq"], dqp, QW)[0], QK)
    g_kv = _tn_matmul("dw_kv", sv["ckv"], dkvb, 2 * QW)[0]
    g_wk = _unpad_heads(g_kv[:, :QW], NOPE).reshape(QL, H, NOPE)
    g_wv = _unpad_heads(g_kv[:, QW:], VH).reshape(QL, H, VH)
    g_wkv = jnp.concatenate([g_wk, g_wv], axis=2).reshape(QL, H * (NOPE + VH))
    dxbc_f, dxbc_b, ddt_f, ddt_b, da = _ssd_scan_bwd(B, T, dyt, sv["xbc"], sv["dt2"], sp["alog2"], sv["hin_f"], sv["hin_b"])
    dpre, ddtr, dcw, dcb, ddtb = _ssd_prep_bwd(R, dxbc_f, dxbc_b, dxsk, ddt_f, ddt_b, sv["xbc_raw"], sv["dt_raw"], lw["conv_w8"],
                                                sp["conv_b"], sp["dtb"])
    dx, dproj, dmod_c, dnw1 = _in_bwd(R, dxm, sv["x"], mod, sp["nw1"], dz, dpre, dqa, dkva, dpool_in, dkr, ddtr,
                                      lw["conv_w8"], lw["w_in"], latent_only)
    g_win = _unarrange_w_in(jnp.concatenate(list(_tn_matmul("dw_in", sv["h1"], dproj, PC)), axis=1))
    a2 = -jnp.exp(sp["alog2"][:, 0, :H])
    small = dict(
        norm1_w=dnw1[0], norm2_w=dnw2[0], conv_w=dcw[0:4], conv_b=dcb[0], dt_bias=ddtb[0, :2 * H].reshape(2, H),
        a_log=jnp.sum(da[:, :, 0, :H], axis=0) * a2, ssd_d=jnp.sum(dvec_o[1].reshape(H, P), axis=1), ssd_norm_w=dvec_o[0],
        q_a_norm_w=dnw_qk[0], kv_a_norm_w=dnw_qk[1],
        pool_w=jnp.stack([g_wbd[64 * g:64 * (g + 1), 64 * g:64 * (g + 1)] for g in range(4)]), pool_scale=dpsc[0])
    big = dict(w_in=g_win, w_out=g_wout, w_q_b=g_wq, w_kv_b=g_wkv, w_mlp1=g_w1, w_mlp2=g_w2)
    return dx, big, small, dmod_a + dmod_b + dmod_c


def _small_params(l, norm1_w, norm2_w, conv_b, dt_bias, a_log, ssd_d, ssd_norm_w, q_a_norm_w, kv_a_norm_w, pool_scale):
    alog2 = jnp.broadcast_to(jnp.pad(a_log[l], ((0, 0), (0, HP - H)))[:, None, :], (2, 8, HP))
    return dict(nw1=norm1_w[l][None], nw2=norm2_w[l][None], conv_b=conv_b[l][None],
                dtb=_lane_pad(dt_bias[l].reshape(2 * H), HP), alog2=alog2,
                dsk=jnp.repeat(ssd_d[l], P)[None], snw=ssd_norm_w[l][None], qnw=q_a_norm_w[l][None],
                kvnw=kv_a_norm_w[l][None], pscale=pool_scale[l][None])


SMALL_NAMES = ["mod_b", "norm1_w", "norm2_w", "conv_b", "dt_bias", "a_log", "ssd_d", "ssd_norm_w", "q_a_norm_w",
               "kv_a_norm_w", "pool_w", "pool_scale", "final_norm_w"]


def _pack(arrs):
    rows = []
    for a in arrs:
        f = a.reshape(-1).astype(F32)
        n = -(-f.shape[0] // HP) * HP
        rows.append(jnp.pad(f, (0, n - f.shape[0])).reshape(-1, HP))
    out = jnp.concatenate(rows, axis=0)
    pad = (-out.shape[0]) % 8
    return jnp.pad(out, ((0, pad), (0, 0)))


def _unpack(pack, like):
    outs, r = [], 0
    for a in like:
        n = math.prod(a.shape)
        nr = -(-n // HP)
        outs.append(pack[r:r + nr].reshape(-1)[:n].reshape(a.shape))
        r += nr
    return outs


def _local_step(x, ctx, target, mods, full_of, small_w, on_grads=None, on_mlp=None):
    B, N = x.shape[0], x.shape[1]
    T = CTX + N
    R = _Rows(B, T)
    cos, sin = _rope_tables(T)
    xu = jnp.concatenate([ctx, x], axis=1)
    L = len(mods)
    lws, sps, saves = [], [], []
    for l in range(L):
        f = full_of(l, xu)
        lws.append(_layer_weights(f["w_in"], f["w_q_b"], f["w_kv_b"], f["conv_w"], small_w["pool_w"][l]))
        sps.append(_small_params(l, *[small_w[k] for k in ["norm1_w", "norm2_w", "conv_b", "dt_bias", "a_log", "ssd_d",
                                                          "ssd_norm_w", "q_a_norm_w", "kv_a_norm_w", "pool_scale"]]))
        xu, sv = _layer_fwd(R, xu, mods[l], lws[l], sps[l], f["late"], cos, sin)
        saves.append(sv)
    dx, loss8, dfnw = _loss_head(R, xu, target, small_w["final_norm_w"][None])
    bigs, smalls, dmods = [None] * L, [None] * L, [None] * L
    for l in reversed(range(L)):
        hook = functools.partial(on_mlp, l) if on_mlp is not None else None
        dx, bigs[l], smalls[l], dmods[l] = _layer_bwd(R, dx, saves[l], mods[l], lws[l], sps[l], cos, sin, hook, l == 0)
        if on_grads is not None:
            bigs[l], tok = on_grads(l, bigs[l], dx)
            if tok is not None:
                sps[l - 1] = dict(sps[l - 1], nw2=sps[l - 1]["nw2"] + tok)
    return loss8[0, 0], dx, bigs, smalls, dfnw[0], dmods


def kernel(x, c, ctx, c_ctx, mod_w, mod_b, norm1_w, norm2_w, w_in, conv_w, conv_b, dt_bias, a_log, ssd_d, ssd_norm_w, q_a_norm_w, w_q_b, kv_a_norm_w, w_kv_b, pool_w, pool_scale, w_out, w_mlp1, w_mlp2, final_norm_w, loss_target, m_c_ctx, m_mod_w, m_mod_b, m_norm1_w, m_norm2_w, m_w_in, m_conv_w, m_conv_b, m_dt_bias, m_a_log, m_ssd_d, m_ssd_norm_w, m_q_a_norm_w, m_w_q_b, m_kv_a_norm_w, m_w_kv_b, m_pool_w, m_pool_scale, m_w_out, m_w_mlp1, m_w_mlp2, m_final_norm_w, v_c_ctx, v_mod_w, v_mod_b, v_norm1_w, v_norm2_w, v_w_in, v_conv_w, v_conv_b, v_dt_bias, v_a_log, v_ssd_d, v_ssd_norm_w, v_q_a_norm_w, v_w_q_b, v_kv_a_norm_w, v_w_kv_b, v_pool_w, v_pool_scale, v_w_out, v_w_mlp1, v_w_mlp2, v_final_norm_w):
    args = dict(locals())
    B = x.shape[0]
    L = mod_w.shape[0]
    me = _dev_index(*_my_pos())
    CS = mod_w.shape[2]

    big_names = ["w_in", "w_out", "w_q_b", "w_kv_b", "w_mlp1", "w_mlp2"]
    shards = {n: args[n].astype(BF) for n in big_names}
    early, late_names = ["w_in", "w_q_b", "w_kv_b"], ["w_out", "w_mlp1", "w_mlp2"]
    g0 = _all_gather("gather_weights", [c, conv_w] + [shards[n][0] for n in early])
    c_all, convw_all = g0[0], g0[1]
    gathered = {0: dict(zip(early, g0[2:]))}
    cs = jnp.concatenate([c_all.reshape(NDEV * B, D), c_ctx[None], jnp.zeros((24 - NDEV * B - 1, D), F32)], axis=0)
    m_loc = _adaln_fwd(cs, mod_w)
    m_all = _all_gather("gather_mod", [m_loc])[0]
    m_full = jnp.moveaxis(m_all, 0, 2).reshape(L, 24, NDEV * CS) + mod_b[:, None, :]
    pending = {}
    tok = jnp.zeros((), F32)
    res = _exchange_start("gather_start_0", 1, [shards[n][0] for n in late_names], gather=True)
    pending[0] = res[:4]
    tok = tok + res[4][0, 0]
    for l in range(1, L):
        res = _exchange_start("gather_start_%d" % l, 1 + l, [shards[n][l] for n in early + late_names], gather=True)
        pending[l] = res[:4]
        tok = tok + res[4][0, 0]
    mods = []
    for l in range(L):
        ex = lax.dynamic_slice(m_full[l], (me * B, 0), (B, 6 * D)).reshape(B, 6, D)
        cc = jnp.broadcast_to(m_full[l, NDEV * B].reshape(1, 6, D), (B, 6, D))
        mods.append(jnp.pad(jnp.stack([cc, ex], axis=1), ((0, 0), (0, 0), (0, 2), (0, 0))) + tok)

    def full_of(l, xu):
        if l > 0:
            own, lands = _exchange_wait("gather_wait_%d" % l, *pending.pop(l), xu, gather=True)
            gathered[l] = dict(zip(early + late_names, _with_own(lands, own, me)))

        def late(after):
            if l == 0:
                own, lands = _exchange_wait("gather_wait_0", *pending.pop(0), after, gather=True)
                gathered[0].update(zip(late_names, _with_own(lands, own, me)))
            gl = gathered[l]
            return gl["w_out"].reshape(D, D), gl["w_mlp1"], gl["w_mlp2"]

        g = gathered[l]
        return dict(
            w_in=g["w_in"].reshape(D, IN_COLS),
            w_q_b=jnp.moveaxis(g["w_q_b"], 0, 1).reshape(QL, H * QK),
            w_kv_b=jnp.moveaxis(g["w_kv_b"], 0, 1).reshape(QL, H * (NOPE + VH)),
            conv_w=jnp.moveaxis(convw_all[:, l], 0, 1).reshape(4, XBC), late=late)

    def grad_blocks(big, names):
        make = dict(
            w_in=lambda g: g.reshape(NDEV, D // NDEV, IN_COLS), w_out=lambda g: g.reshape(NDEV, D // NDEV, D),
            w_q_b=lambda g: jnp.moveaxis(g.reshape(QL, NDEV, -1), 1, 0),
            w_kv_b=lambda g: jnp.moveaxis(g.reshape(QL, NDEV, -1), 1, 0), w_mlp1=lambda g: g, w_mlp2=lambda g: g)
        return [make[n](big[n]).astype(BF) for n in names]

    sent, sent_mlp = {}, {}
    rest_names = ["w_in", "w_out", "w_q_b", "w_kv_b"]

    def on_mlp(l, g_w1, g_w2):
        if l > 0:
            return None
        res = _exchange_start("grads_mlp_start_0", 1 + 2 * L, [g_w1.astype(BF), g_w2.astype(BF)], gather=False)
        sent_mlp[l] = res[:4]
        return res[4][0, 0]

    def on_grads(l, big, dx):
        if l == 0:
            return grad_blocks(big, rest_names), None
        res = _exchange_start("grads_start_%d" % l, 1 + L + l, grad_blocks(big, big_names), gather=False)
        sent[l] = res[:4]
        return None, res[4][0, 0]

    small_w = {k: args[k] for k in SMALL_NAMES if k != "mod_b"}
    loss_part, grad_x, blocks, smalls, dfnw, dmods = _local_step(x, ctx, loss_target, mods, full_of, small_w, on_grads, on_mlp)
    loss = lax.psum(loss_part, ("x", "y", "c"))

    dm_ex = jnp.stack([dmods[l][:, 1, :6].reshape(B, 6 * D) for l in range(L)])
    dm_cc = jnp.stack([jnp.sum(dmods[l][:, 0, :6], axis=0).reshape(6 * D) for l in range(L)])
    small_parts = dict(
        mod_b=jnp.sum(dm_ex, axis=1) + dm_cc,
        **{k: jnp.stack([smalls[l][k] for l in range(L)]) for k in SMALL_NAMES[1:-1]},
        final_norm_w=dfnw, conv_w=jnp.stack([smalls[l]["conv_w"] for l in range(L)]), dm_cc=dm_cc)
    adam_grads = [small_parts[k] for k in SMALL_NAMES]
    extras = [small_parts["conv_w"], dm_cc, dm_ex]
    pack = jnp.concatenate([_pack(adam_grads), _pack(extras)], axis=0)
    pack_all = _all_gather("gather_small_grads", [pack])[0]
    wpack = _pack([args[k] for k in SMALL_NAMES])
    mpack = _pack([args["m_" + k] for k in SMALL_NAMES])
    vpack = _pack([args["v_" + k] for k in SMALL_NAMES])
    n_adam = wpack.shape[0]
    res_small = _adamw("adamw_small", pack_all[:, :n_adam], wpack, mpack, vpack, n_adam)
    small_out = [_unpack(r, [args[k] for k in SMALL_NAMES]) for r in res_small]
    ext_all = pack_all[:, n_adam:]
    g_conv_full, dm_cc_tot, _ = _unpack(_sum_blocks("sum_small_grads", ext_all), extras)
    dm_ex_all = jnp.stack([_unpack(ext_all[k], extras)[2] for k in range(NDEV)], axis=1)
    dm_rows = jnp.concatenate([dm_ex_all.reshape(L, NDEV * B, 6 * D), dm_cc_tot[:, None, :],
                               jnp.zeros((L, 24 - NDEV * B - 1, 6 * D), F32)], axis=1)
    dm_loc = lax.dynamic_slice(dm_rows, (0, 0, me * CS), (L, 24, CS))
    g_modw, gc_part = _adaln_bwd(cs, dm_loc, mod_w)
    gc_all = _all_gather("gather_cctx_grad", [gc_part[NDEV * B:NDEV * B + 8]])[0]
    cpad = lambda a: jnp.pad(a[None], ((0, 7), (0, 0)))
    res_cc = _adamw("adamw_cctx", gc_all, cpad(c_ctx), cpad(m_c_ctx), cpad(v_c_ctx), 8)
    cc_out = [r[0] for r in res_cc]

    def waited(name, handles, names):
        srcs, lands = _exchange_wait(name, *handles, grad_x, gather=False)
        own = [lax.dynamic_index_in_dim(s, me, 0, keepdims=False) for s in srcs]
        return dict(zip(names, _with_own(lands, own, me)))

    recv = {0: dict(zip(rest_names, _all_to_all("exchange_grads", blocks[0])))}
    recv[0].update(waited("grads_mlp_wait_0", sent_mlp.pop(0), ["w_mlp1", "w_mlp2"]))
    for l in range(1, L):
        recv[l] = waited("grads_wait_%d" % l, sent.pop(l), big_names)
    recvs = [jnp.stack([recv[l][n] for l in range(L)], axis=1) for n in big_names]
    big_out = {}
    for name, rv in zip(big_names, recvs):
        w = args[name]
        Rr, C = math.prod(w.shape[:-1]), w.shape[-1]
        rt = Rr if Rr * C <= (1 << 18) else Rr // 8
        res = _adamw("adamw_" + name, rv.reshape(NDEV, Rr, C), w.reshape(Rr, C), args["m_" + name].reshape(Rr, C),
                     args["v_" + name].reshape(Rr, C), rt)
        big_out[name] = [r.reshape(w.shape) for r in res]
    res = _adamw("adamw_mod_w", g_modw.reshape(1, L * D, CS), mod_w.reshape(L * D, CS), m_mod_w.reshape(L * D, CS),
                 v_mod_w.reshape(L * D, CS), L * D // 8)
    big_out["mod_w"] = [r.reshape(mod_w.shape) for r in res]
    CW = conv_w.shape[2]
    g_conv = lax.dynamic_slice(g_conv_full, (0, 0, me * CW), (L, 4, CW))
    res = _adamw("adamw_conv_w", g_conv.reshape(1, L * 4, CW), conv_w.reshape(L * 4, CW), m_conv_w.reshape(L * 4, CW),
                 v_conv_w.reshape(L * 4, CW), L * 4)
    big_out["conv_w"] = [r.reshape(conv_w.shape) for r in res]

    weights = ["c_ctx", "mod_w", "mod_b", "norm1_w", "norm2_w", "w_in", "conv_w", "conv_b", "dt_bias", "a_log", "ssd_d",
               "ssd_norm_w", "q_a_norm_w", "w_q_b", "kv_a_norm_w", "w_kv_b", "pool_w", "pool_scale", "w_out", "w_mlp1",
               "w_mlp2", "final_norm_w"]
    outs = [loss, grad_x]
    for kind in range(4):
        for name in weights:
            if name == "c_ctx":
                outs.append(cc_out[kind])
            elif name in big_out:
                outs.append(big_out[name][kind])
            else:
                outs.append(small_out[kind][SMALL_NAMES.index(name)])
    return tuple(outs)
```

```python
import functools
import math

import jax
import jax.numpy as jnp
from jax import lax
from jax.experimental import pallas as pl
from jax.experimental.pallas import tpu as pltpu

F32 = jnp.float32
BF = jnp.bfloat16
MXU = BF

D = 1024
CTX = 256
GRID_W = 64
EPS = 1e-6
H = 6
P = 64
SSD_IN = 384
NST = 128
XBC = 896
CH = 128
QL = 256
NOPE = 64
ROPE = 32
VH = 64
QK = 96
HP = 128
QW = H * HP
PD = 256
FF = 4096
IN_COLS = 2092
ROPE_THETA = 10000.0
PC = 2304
A_Z, A_XBC, A_QA, A_KVA, A_POOL, A_KR, A_DT = 0, 384, 1280, 1536, 1792, 2048, 2176
MIXW = SSD_IN + QW + PD
NDEV = 8
FSH = FF // NDEV
TM = 256
TQ = 256
TK = 256
VMEM_CAP = 64 * 1024 * 1024
ADAM_LR, ADAM_B1, ADAM_B2, ADAM_EPS, ADAM_WD, ADAM_STEP = 0.001, 0.9, 0.999, 1e-08, 0.01, 10


def _nbytes(shape, dtype):
    n = 1
    for s in shape:
        if s is not None:
            n *= s
    return n * jnp.dtype(dtype).itemsize


def _params(sem, block_bytes, extra=0):
    lim = min(2 * block_bytes + extra + (8 << 20), VMEM_CAP - (6 << 20))
    return pltpu.CompilerParams(dimension_semantics=sem, vmem_limit_bytes=int(lim))


def _hbm(arrays):
    return [pltpu.with_memory_space_constraint(a, pltpu.HBM) for a in arrays]


def _dot(a, b):
    return jnp.dot(a.astype(MXU), b.astype(MXU), preferred_element_type=F32)


def _dot_nt(a, b):
    return lax.dot_general(a.astype(MXU), b.astype(MXU), (((1,), (1,)), ((), ())), preferred_element_type=F32)


def _dot_tn(a, b):
    return lax.dot_general(a.astype(MXU), b.astype(MXU), (((0,), (0,)), ((), ())), preferred_element_type=F32)


def _dot01(m01, x):
    b16 = jnp.bfloat16
    m = m01.astype(b16)
    hi = x.astype(b16)
    r1 = x - hi.astype(F32)
    mid = r1.astype(b16)
    lo = (r1 - mid.astype(F32)).astype(b16)
    f = lambda v: jnp.dot(m, v, preferred_element_type=F32)
    return f(hi) + f(mid) + f(lo)


def _sigmoid(x):
    return 1.0 / (1.0 + jnp.exp(-x))


def _silu(x):
    return x * _sigmoid(x)


def _dsilu(x):
    s = _sigmoid(x)
    return s * (1.0 + x * (1.0 - s))


def _iota(shape, dim):
    return lax.broadcasted_iota(jnp.int32, shape, dim)


def _row(ref, k):
    blk = ref[...]
    return jnp.sum(jnp.where(_iota(blk.shape, 0) == k, blk, 0.0), axis=0, keepdims=True)


def _shift_rows(x, k):
    n = x.shape[0]
    return pltpu.roll(x, (-k) % n, axis=0)


class _Rows:
    def __init__(self, B, T):
        self.B, self.T = B, T
        self.nt = T // TM
        self.nct = CTX // TM

    def row(self, F):
        return pl.BlockSpec((None, TM, F), lambda b, i: (b, i, 0))

    def row2(self, F):
        return pl.BlockSpec((None, 2, TM, F), lambda b, i: (b, 0, i, 0))

    def prev8(self, F):
        return pl.BlockSpec((None, 8, F), lambda b, i: (b, jnp.maximum(i * (TM // 8) - 1, 0), 0))

    def next8(self, F):
        last = self.T // 8 - 1
        return pl.BlockSpec((None, 8, F), lambda b, i: (b, jnp.minimum((i + 1) * (TM // 8), last), 0))

    def mod(self):
        nct = self.nct
        return pl.BlockSpec((None, None, 8, D), lambda b, i: (b, jnp.where(i < nct, 0, 1), 0, 0))

    def const(self, shape):
        z = (0,) * len(shape)
        return pl.BlockSpec(tuple(shape), lambda b, i: z)

    def tgt(self, F):
        nct = self.nct
        return pl.BlockSpec((None, TM, F), lambda b, i: (b, jnp.maximum(i - nct, 0), 0))

    def call(self, name, body, ins, outs, scratch=(), extra=0):
        arrays = [a for a, _ in ins]
        in_specs = [s for _, s in ins]
        out_shape = [pltpu.HBM(o.shape, o.dtype) for o, _ in outs]
        out_specs = [s for _, s in outs]
        bb = 0
        for a, s in list(ins) + list(outs):
            if s.block_shape is not None:
                bb += _nbytes(s.block_shape, a.dtype)
        return pl.pallas_call(
            functools.partial(body), name=name, grid=(self.B, self.nt),
            in_specs=in_specs, out_specs=out_specs, out_shape=out_shape, scratch_shapes=list(scratch),
            compiler_params=_params(("arbitrary", "arbitrary"), bb, extra),
        )(*_hbm(arrays))

    def first_of_stream(self, i):
        return jnp.logical_or(i == 0, i == self.nct)

    def last_of_stream(self, i):
        return jnp.logical_or(i == self.nct - 1, i == self.nt - 1)

    def ext(self, i, prev_ref, cur, next_ref):
        pv = prev_ref[...].astype(F32) * jnp.where(self.first_of_stream(i), 0.0, 1.0)
        nx = next_ref[...].astype(F32) * jnp.where(self.last_of_stream(i), 0.0, 1.0)
        return jnp.concatenate([pv, cur, nx], axis=0)

    def stream_pos(self, i, rows):
        start = jnp.where(i < self.nct, 0, CTX)
        n = jnp.where(i < self.nct, CTX, self.T - CTX)
        t = i * TM - 8 - start + _iota((rows, 1), 0)
        return t, n


def _sds(shape, dtype):
    return jax.ShapeDtypeStruct(tuple(shape), dtype)


def _out(shape, dtype):
    return pltpu.HBM(tuple(shape), dtype)


def _norm_mod(x, nw, sh, sc):
    r = lax.rsqrt(jnp.mean(x * x, axis=-1, keepdims=True) + EPS)
    xn = x * r
    return xn * nw * (1.0 + sc) + sh, xn, r


def _norm_mod_bwd(dh, xn, r, nw, sc):
    dsh = jnp.sum(dh, axis=0, keepdims=True)
    dsc = jnp.sum(dh * (xn * nw), axis=0, keepdims=True)
    dnw = jnp.sum(dh * (1.0 + sc) * xn, axis=0, keepdims=True)
    dxn = dh * nw * (1.0 + sc)
    dx = r * (dxn - xn * jnp.mean(dxn * xn, axis=-1, keepdims=True))
    return dx, dsh, dsc, dnw


def _acc_rows(ref, first, rows):
    rid = _iota(ref.shape, 0)
    upd = jnp.zeros(ref.shape, F32)
    for k, v in rows.items():
        upd = upd + jnp.where(rid == k, v, 0.0)

    @pl.when(first)
    def _():
        ref[...] = upd

    @pl.when(jnp.logical_not(first))
    def _():
        ref[...] += upd


def _in_proj(R, x, mod, nw1, w_arr):
    B, T = R.B, R.T

    def body(x_ref, mod_ref, nw_ref, w_ref, z_ref, xbc_ref, qa_ref, kva_ref, pool_ref, kr_ref, dt_ref, h_ref):
        h, _, _ = _norm_mod(x_ref[...], nw_ref[...], _row(mod_ref, 0), _row(mod_ref, 1))
        hb = h.astype(BF)
        h_ref[...] = hb
        p = jnp.dot(hb, w_ref[...], preferred_element_type=F32)
        z_ref[...] = p[:, A_Z:A_XBC]
        xbc_ref[...] = p[:, A_XBC:A_QA]
        qa_ref[...] = p[:, A_QA:A_KVA]
        kva_ref[...] = p[:, A_KVA:A_POOL]
        pool_ref[...] = p[:, A_POOL:A_KR]
        kr_ref[...] = p[:, A_KR:A_DT]
        dt_ref[...] = p[:, A_DT:PC]

    widths = [SSD_IN, XBC, QL, QL, PD, HP, HP]
    outs = [(_sds((B, T, w), F32), R.row(w)) for w in widths] + [(_sds((B, T, D), BF), R.row(D))]
    return R.call("in_proj", body,
                  [(x, R.row(D)), (mod, R.mod()), (nw1, R.const((1, D))), (w_arr, R.const((D, PC)))],
                  outs, extra=8 << 20)


def _conv_pre(ext, w_ref, b_ref):
    return (_row(w_ref, 0) * _shift_rows(ext, -1)[8:8 + TM] + _row(w_ref, 1) * ext[8:8 + TM]
            + _row(w_ref, 2) * _shift_rows(ext, 1)[8:8 + TM] + _row(w_ref, 3) * _shift_rows(ext, 2)[8:8 + TM]
            + b_ref[...])


def _softplus(x):
    return jnp.maximum(x, 0.0) + jnp.log(1.0 + jnp.exp(-jnp.abs(x)))


def _ssd_prep(R, xbc_raw, dt_raw, conv_w8, conv_b, dtb):
    B, T = R.B, R.T

    def body(raw_ref, pv_ref, nx_ref, dtr_ref, w_ref, b_ref, dtb_ref, xbc_ref, dt_ref):
        i = pl.program_id(1)
        ext = R.ext(i, pv_ref, raw_ref[...], nx_ref)
        xbc_ref[...] = _silu(_conv_pre(ext, w_ref, b_ref))
        lane = _iota((TM, HP), 1)
        dtv = _softplus(dtr_ref[...] + dtb_ref[...])
        keep = lane < H
        dt_ref[0] = jnp.where(keep, dtv, 0.0)
        dt_ref[1] = jnp.where(keep, pltpu.roll(dtv, HP - H, axis=1), 0.0)

    return R.call("ssd_prep", body,
                  [(xbc_raw, R.row(XBC)), (xbc_raw, R.prev8(XBC)), (xbc_raw, R.next8(XBC)), (dt_raw, R.row(HP)),
                   (conv_w8, R.const((8, XBC))), (conv_b, R.const((1, XBC))), (dtb, R.const((1, HP)))],
                  [(_sds((B, T, XBC), F32), R.row(XBC)), (_sds((B, 2, T, HP), F32), R.row2(HP))], extra=12 << 20)


def _chunk_index(d, s, nc, nctc):
    if d == 0:
        return s
    return jnp.where(s < nctc, nctc - 1 - s, nc - 1 - (s - nctc))


def _dot01_r(x, m01, passes=3):
    b16 = jnp.bfloat16
    m = m01.astype(b16)
    out, rest = None, x
    for _ in range(passes):
        part = rest.astype(b16)
        rest = rest - part.astype(F32)
        term = jnp.dot(part, m, preferred_element_type=F32)
        out = term if out is None else out + term
    return out


def _scan_common(d, dtv, alog_ref):
    sel = _iota((HP, H * HP), 0) == (_iota((HP, H * HP), 1) >> 7)
    a_all = _dot01_r(-jnp.exp(alog_ref[d]), sel)
    dt_all = _dot01_r(dtv, sel)
    adt_all = dt_all * jnp.concatenate([a_all] * (CH // 8), axis=0)
    row = _iota((CH, CH), 0)
    col = _iota((CH, CH), 1)
    inc = col <= row if d == 0 else col >= row
    inc_t = row <= col if d == 0 else row >= col
    q_all = _dot01(inc, adt_all)
    return a_all, dt_all, adt_all, q_all, inc, inc_t


def _head_decay(h, adt_all, q_all, inc, inc_t):
    q = q_all[:, HP * h:HP * (h + 1)]
    q_t = q.T
    qtot = jnp.sum(adt_all[:, HP * h:HP * (h + 1)], axis=0, keepdims=True)
    lm = jnp.where(inc, jnp.exp(q - q_t), 0.0)
    lm_t = jnp.where(inc_t, jnp.exp(q_t - q), 0.0)
    return lm, lm_t, jnp.exp(q), jnp.exp(qtot), jnp.exp(qtot - q)


def _ssd_scan(B, T, xbc, dt2, alog2):
    nc, nctc = T // CH, CTX // CH

    def chain(d, xbc_ref, dt_ref, alog_ref, y_ref, hin_ref, hs):
        xbc_v = xbc_ref[...]
        dtv = dt_ref[...]
        blk = alog_ref[d]
        adt = dtv * -jnp.exp(jnp.sum(jnp.where(_iota(blk.shape, 0) == 0, blk, 0.0), axis=0, keepdims=True))
        row, col = _iota((CH, CH), 0), _iota((CH, CH), 1)
        inc = col <= row if d == 0 else col >= row
        inc_tf = (row <= col if d == 0 else row >= col).astype(F32)
        q = _dot01(inc, adt)
        hin_ref[...] = hs[d]
        for g in range(2):
            bg = xbc_v[:, SSD_IN + NST * g:SSD_IN + NST * (g + 1)]
            cg = xbc_v[:, SSD_IN + 2 * NST + NST * g:SSD_IN + 2 * NST + NST * (g + 1)]
            gm = _dot_nt(cg, bg)
            for r in range(3):
                h = 3 * g + r
                onehot = (_iota((1, HP), 1) == h).astype(F32)
                adt_h = jnp.sum(adt * onehot, axis=1, keepdims=True)
                qc = jnp.sum(q * onehot, axis=1, keepdims=True)
                dt_h = jnp.sum(dtv * onehot, axis=1, keepdims=True)
                qr = jnp.sum(adt_h * inc_tf, axis=0, keepdims=True)
                qtot = jnp.sum(adt_h, axis=0, keepdims=True)
                lm = jnp.where(inc, jnp.exp(qc - qr), 0.0)
                xh = xbc_v[:, P * h:P * (h + 1)] * dt_h
                hprev = hs[d, P * h:P * (h + 1), :]
                y_ref[:, P * h:P * (h + 1)] = _dot(gm * lm, xh) + jnp.exp(qc) * _dot_nt(cg, hprev)
                hs[d, P * h:P * (h + 1), :] = jnp.exp(qtot) * hprev + _dot_tn(xh * jnp.exp(qtot - qc), bg)

    def body(xf_ref, xb_ref, dtf_ref, dtb_ref, alog_ref, yf_ref, yb_ref, hf_ref, hb_ref, hs):
        @pl.when(pl.program_id(1) == 0)
        def _():
            hs[...] = jnp.zeros(hs.shape, F32)

        chain(0, xf_ref, dtf_ref, alog_ref, yf_ref, hf_ref, hs)
        chain(1, xb_ref, dtb_ref, alog_ref, yb_ref, hb_ref, hs)

    cidx = lambda d: (lambda s: _chunk_index(d, s, nc, nctc))
    specs = lambda d: dict(
        xbc=pl.BlockSpec((None, CH, XBC), lambda b, s: (b, cidx(d)(s), 0)),
        dt=pl.BlockSpec((None, None, CH, HP), lambda b, s: (b, d, cidx(d)(s), 0)),
        y=pl.BlockSpec((None, CH, SSD_IN), lambda b, s: (b, cidx(d)(s), 0)),
        h=pl.BlockSpec((None, None, SSD_IN, NST), lambda b, s: (b, cidx(d)(s), 0, 0)))
    f, r = specs(0), specs(1)
    bb = 2 * (_nbytes((CH, XBC), F32) + _nbytes((CH, HP), F32) + _nbytes((CH, SSD_IN), F32) + _nbytes((SSD_IN, NST), F32))
    return pl.pallas_call(
        functools.partial(body), name="ssd_scan", grid=(B, nc),
        in_specs=[f["xbc"], r["xbc"], f["dt"], r["dt"], pl.BlockSpec((2, 8, HP), lambda b, s: (0, 0, 0))],
        out_specs=[f["y"], r["y"], f["h"], r["h"]],
        out_shape=[_out((B, T, SSD_IN), F32)] * 2 + [_out((B, nc, SSD_IN, NST), F32)] * 2,
        scratch_shapes=[pltpu.VMEM((2, SSD_IN, NST), F32)],
        compiler_params=_params(("arbitrary",) * 2, bb, 12 << 20),
    )(*_hbm([xbc, xbc, dt2, dt2, alog2]))


def _swap8(u):
    lane = _iota(u.shape, 1)
    n = u.shape[1]
    return jnp.where((lane & 15) < 8, pltpu.roll(u, n - 8, axis=1), pltpu.roll(u, 8, axis=1))


def _rope(u, cos, sin_signed):
    return u * cos + _swap8(u) * sin_signed


def _rms(x, w):
    r = lax.rsqrt(jnp.mean(x * x, axis=-1, keepdims=True) + EPS)
    xh = x * r
    return xh * w, xh, r


def _rms_bwd(dy, xh, r, w):
    dw = jnp.sum(dy * xh, axis=0, keepdims=True)
    dxh = dy * w
    return r * (dxh - xh * jnp.mean(dxh * xh, axis=-1, keepdims=True)), dw


def _tile6(t):
    return jnp.concatenate([t] * H, axis=1)


def _per_head(fn, u):
    return jnp.concatenate([fn(u[:, HP * h:HP * (h + 1)]) for h in range(H)], axis=1)


def _mla_prep(R, qa, kva, kr, qnw, kvnw, wq, wk, wv, cos, sin):
    B, T = R.B, R.T
    scale = QK ** -0.5

    def body(qa_ref, kva_ref, kr_ref, qnw_ref, kvnw_ref, wq_ref, wk_ref, wv_ref, cos_ref, sin_ref,
             qt_ref, kt_ref, qtr_ref, ktr_ref, vtr_ref, cq_ref, ckv_ref):
        cq, _, _ = _rms(qa_ref[...], qnw_ref[...])
        ckv, _, _ = _rms(kva_ref[...], kvnw_ref[...])
        cqb, ckvb = cq.astype(BF), ckv.astype(BF)
        cq_ref[...] = cqb
        ckv_ref[...] = ckvb
        cos1, sin1 = cos_ref[...], sin_ref[...]
        q = _per_head(lambda u: _rope(u, cos1, sin1), jnp.dot(cqb, wq_ref[...], preferred_element_type=F32)) * scale
        qt_ref[...] = q.astype(BF)
        kk = _rope(kr_ref[...], cos1, sin1)
        hl = _iota((TM, QW), 1) & (HP - 1)
        k = jnp.dot(ckvb, wk_ref[...], preferred_element_type=F32) + _tile6(kk)
        k = jnp.where(jnp.logical_and(hl >= QK, hl < QK + 3), 1.0, k)
        kt_ref[...] = k.astype(BF)
        v = jnp.dot(ckvb, wv_ref[...], preferred_element_type=F32)
        v = jnp.where(jnp.logical_and(hl >= VH, hl < VH + 4), 1.0, v)
        for h in range(H):
            cols = slice(HP * h, HP * (h + 1))
            qtr_ref[h] = q[:, cols].T.astype(BF)
            ktr_ref[h] = k[:, cols].T.astype(BF)
            vtr_ref[h] = v[:, cols].T.astype(BF)

    tr = (_sds((B, H, HP, T), BF), pl.BlockSpec((None, H, HP, TM), lambda b, i: (b, 0, 0, i)))
    return R.call("mla_prep", body,
                  [(qa, R.row(QL)), (kva, R.row(QL)), (kr, R.row(HP)), (qnw, R.const((1, QL))), (kvnw, R.const((1, QL))),
                   (wq, R.const((QL, QW))), (wk, R.const((QL, QW))), (wv, R.const((QL, QW))),
                   (cos, pl.BlockSpec((TM, HP), lambda b, i: (i, 0))), (sin, pl.BlockSpec((TM, HP), lambda b, i: (i, 0)))],
                  [(_sds((B, T, QW), BF), R.row(QW))] * 2 + [tr] * 3 + [(_sds((B, T, QL), BF), R.row(QL))] * 2, extra=12 << 20)


def _flash_fwd(B, T, q_t, kt, v_t):
    nq, nk = T // TQ, T // TK
    HS = 2

    def body(q_ref, k_ref, v_ref, o_ref, lse_ref, s_scr):
        i = pl.program_id(2)

        def attend(nch):
            ms = []
            for hh in range(HS):
                q_tr = q_ref[hh]
                mrun = None
                for j in range(nch):
                    s = _dot(k_ref[TK * j:TK * (j + 1), HP * hh:HP * (hh + 1)], q_tr)
                    s_scr[hh, j] = s
                    mrun = s if mrun is None else jnp.maximum(mrun, s)
                ms.append(jnp.max(mrun, axis=0, keepdims=True))
            row = _iota((HP, TQ), 0)
            for hh in range(HS):
                acc_t = jnp.zeros((HP, TQ), F32)
                for j in range(nch):
                    acc_t = acc_t + _dot(v_ref[hh, :, TK * j:TK * (j + 1)], jnp.exp(s_scr[hh, j] - ms[hh]))
                l = jnp.sum(jnp.where(row == VH, acc_t, 0.0), axis=0, keepdims=True)
                o_ref[:, HP * hh:HP * (hh + 1)] = jnp.where(row < VH, acc_t / l, 0.0).T
                lse_ref[hh] = ms[hh] + jnp.log(l)

        @pl.when(i < CTX // TQ)
        def _():
            attend(CTX // TK)

        @pl.when(i >= CTX // TQ)
        def _():
            attend(nk)

    bb = HS * (_nbytes((TQ, HP), BF) + 2 * _nbytes((T, HP), BF) + 2 * _nbytes((TQ, HP), F32))
    return pl.pallas_call(
        functools.partial(body), name="flash_fwd", grid=(B, H // HS, nq),
        in_specs=[pl.BlockSpec((None, HS, HP, TQ), lambda b, h, i: (b, h, 0, i)),
                  pl.BlockSpec((None, T, HS * HP), lambda b, h, i: (b, 0, h)),
                  pl.BlockSpec((None, HS, HP, T), lambda b, h, i: (b, h, 0, 0))],
        out_specs=[pl.BlockSpec((None, TQ, HS * HP), lambda b, h, i: (b, i, h)),
                   pl.BlockSpec((None, HS, 1, TQ), lambda b, h, i: (b, h, 0, i))],
        out_shape=[_out((B, T, QW), F32), _out((B, H, 1, T), F32)],
        scratch_shapes=[pltpu.VMEM((HS, nk, TK, TQ), F32)],
        compiler_params=_params(("arbitrary",) * 3, bb, _nbytes((HS, nk, TK, TQ), F32) + (8 << 20)),
    )(*_hbm([q_t, kt, v_t]))


def _pool_terms(R, i, rows):
    t, n = R.stream_pos(i, rows)
    lane = _iota((1, PD), 1)
    half = jnp.where(lane < 64, 1, jnp.where(lane < 128, 2, jnp.where(lane < 192, 4, 8)))
    cnt = (jnp.minimum(t + half, n) - jnp.maximum(t - half, 0)).astype(F32)
    valid = jnp.logical_and(t >= 0, t < n)
    return jnp.where(valid, cnt, 1.0), valid.astype(F32), lane


def _lane_select(lane, a2, a4, a8, a16):
    return jnp.where(lane < 64, a2, jnp.where(lane < 128, a4, jnp.where(lane < 192, a8, a16)))


def _pool_centred(R, i, ext):
    cnt, valid, lane = _pool_terms(R, i, ext.shape[0])
    s2 = ext + _shift_rows(ext, -1)
    s4 = _shift_rows(s2, -1) + _shift_rows(s2, 1)
    s8 = _shift_rows(s4, -2) + _shift_rows(s4, 2)
    s16 = _shift_rows(s8, -4) + _shift_rows(s8, 4)
    return _lane_select(lane, s2, s4, s8, s16) / cnt - ext, cnt, valid, lane


def _pool_fwd(R, u, wbd, scale):
    B, T = R.B, R.T

    def body(u_ref, pv_ref, nx_ref, w_ref, sc_ref, o_ref):
        i = pl.program_id(1)
        ext = R.ext(i, pv_ref, u_ref[...], nx_ref)
        dm, _, _, _ = _pool_centred(R, i, ext)
        o_ref[...] = _dot(dm[8:8 + TM], w_ref[...]) * sc_ref[...]

    return R.call("pool_fwd", body,
                  [(u, R.row(PD)), (u, R.prev8(PD)), (u, R.next8(PD)), (wbd, R.const((PD, PD))), (scale, R.const((1, PD)))],
                  [(_sds((B, T, PD), F32), R.row(PD))], extra=8 << 20)


def _group_mask():
    return _iota((1, SSD_IN), 1) < SSD_IN // 2


def _ssd_gate(yf_ref, yb_ref, xbc_ref, z_ref, dsk_ref):
    ytot = yf_ref[...] + yb_ref[...] + xbc_ref[:, 0:SSD_IN] * dsk_ref[...]
    z = z_ref[...]
    gz = ytot * _silu(z)
    g0 = _group_mask()
    sq = gz * gz
    s0 = jnp.sum(jnp.where(g0, sq, 0.0), axis=1, keepdims=True)
    s1 = jnp.sum(jnp.where(g0, 0.0, sq), axis=1, keepdims=True)
    half = SSD_IN // 2
    r = jnp.where(g0, lax.rsqrt(s0 / half + EPS), lax.rsqrt(s1 / half + EPS))
    return ytot, z, gz, r


def _out_proj(R, x, mod, yf, yb, xbc, z, o, pool, dsk, snw, wout):
    B, T = R.B, R.T

    def body(x_ref, mod_ref, yf_ref, yb_ref, xbc_ref, z_ref, o_ref, pool_ref, dsk_ref, snw_ref, w_ref, xmid_ref, cat_ref,
             mix_ref):
        _, _, gz, r = _ssd_gate(yf_ref, yb_ref, xbc_ref, z_ref, dsk_ref)
        cat_ref[:, 0:SSD_IN] = (gz * r * snw_ref[...]).astype(BF)
        cat_ref[:, SSD_IN:SSD_IN + QW] = o_ref[...].astype(BF)
        cat_ref[:, SSD_IN + QW:MIXW] = pool_ref[...].astype(BF)
        mix = jnp.dot(cat_ref[...], w_ref[...], preferred_element_type=F32)
        mix_ref[...] = mix.astype(BF)
        xmid_ref[...] = x_ref[...] + _row(mod_ref, 2) * mix

    return R.call("out_proj", body,
                  [(x, R.row(D)), (mod, R.mod()), (yf, R.row(SSD_IN)), (yb, R.row(SSD_IN)), (xbc, R.row(XBC)), (z, R.row(SSD_IN)),
                   (o, R.row(QW)),
                   (pool, R.row(PD)), (dsk, R.const((1, SSD_IN))), (snw, R.const((1, SSD_IN))), (wout, R.const((MIXW, D)))],
                  [(_sds((B, T, D), F32), R.row(D)), (_sds((B, T, MIXW), BF), R.row(MIXW)), (_sds((B, T, D), BF), R.row(D))],
                  extra=8 << 20)


def _load_once(first, pairs, sem):
    @pl.when(first)
    def _():
        cps = [pltpu.make_async_copy(src, dst, sem.at[k]) for k, (src, dst) in enumerate(pairs)]
        for cp in cps:
            cp.start()
        for cp in cps:
            cp.wait()


ANY = pl.BlockSpec(memory_space=pl.ANY)


def _mlp_fwd(R, xmid, mod, nw2, w1, w2):
    B, T = R.B, R.T

    def body(x_ref, mod_ref, nw_ref, w1_hbm, w2_hbm, xo_ref, h_ref, u_ref, y_ref, w1_v, w2_v, sem):
        first = jnp.logical_and(pl.program_id(0) == 0, pl.program_id(1) == 0)
        _load_once(first, [(w1_hbm, w1_v), (w2_hbm, w2_v)], sem)
        x = x_ref[...]
        h, _, _ = _norm_mod(x, nw_ref[...], _row(mod_ref, 3), _row(mod_ref, 4))
        hb = h.astype(BF)
        h_ref[...] = hb
        y = jnp.zeros((TM, D), F32)
        for j in range(NDEV):
            u = jnp.dot(hb, w1_v[j], preferred_element_type=F32)
            u_ref[:, FSH * j:FSH * (j + 1)] = u.astype(BF)
            a = jnp.square(jnp.maximum(u, 0.0))
            y = y + jnp.dot(a.astype(BF), w2_v[j], preferred_element_type=F32)
        y_ref[...] = y.astype(BF)
        xo_ref[...] = x + _row(mod_ref, 5) * y

    return R.call("mlp_fwd", body,
                  [(xmid, R.row(D)), (mod, R.mod()), (nw2, R.const((1, D))), (w1, ANY), (w2, ANY)],
                  [(_sds((B, T, D), F32), R.row(D)), (_sds((B, T, D), BF), R.row(D)), (_sds((B, T, FF), BF), R.row(FF)),
                   (_sds((B, T, D), BF), R.row(D))],
                  scratch=[pltpu.VMEM((NDEV, D, FSH), w1.dtype), pltpu.VMEM((NDEV, FSH, D), w2.dtype), pltpu.SemaphoreType.DMA((2,))],
                  extra=(2 * _nbytes((NDEV, D, FSH), BF)) + (8 << 20))


def _loss_head(R, x, tgt, fnw):
    B, T = R.B, R.T

    def body(x_ref, t_ref, w_ref, dx_ref, loss_ref, dw_ref):
        b, i = pl.program_id(0), pl.program_id(1)
        live = jnp.where(i >= R.nct, 1.0, 0.0)
        y, xh, r = _rms(x_ref[...], w_ref[...])
        err = (y - t_ref[...]) * live
        dy = err / D
        dxn, dw = _rms_bwd(dy, xh, r, w_ref[...])
        dx_ref[...] = dxn
        first = jnp.logical_and(b == 0, i == 0)
        part = 0.5 * jnp.sum(jnp.sum(err * err, axis=1, keepdims=True), axis=0, keepdims=True) / D
        _acc_rows(loss_ref, first, {0: jnp.broadcast_to(part, (1, HP))})
        _acc_rows(dw_ref, first, {0: dw})

    return R.call("loss_head", body,
                  [(x, R.row(D)), (tgt, R.tgt(D)), (fnw, R.const((1, D)))],
                  [(_sds((B, T, D), F32), R.row(D)), (_sds((8, HP), F32), R.const((8, HP))), (_sds((8, D), F32), R.const((8, D)))],
                  extra=8 << 20)


def _tn_matmul(name, a, b, tn, sub=1):
    B, T, K = a.shape
    N = b.shape[2]
    nk = 1
    while T % nk or (T // nk) > 1088 or (T // nk) % 16:
        nk += 1
    tk = T // nk
    kt = K if K <= 1536 else 1024
    w = tn // sub
    assert K % kt == 0 and N % tn == 0 and tn % sub == 0

    def body(a_ref, b_ref, o_ref):
        first = jnp.logical_and(pl.program_id(2) == 0, pl.program_id(3) == 0)

        @pl.when(first)
        def _():
            o_ref[...] = jnp.zeros(o_ref.shape, F32)

        acc = _dot_tn(a_ref[...], b_ref[...])
        for s in range(sub):
            o_ref[s] += acc[:, w * s:w * (s + 1)]

    bb = _nbytes((tk, kt), a.dtype) + _nbytes((tk, tn), b.dtype) + _nbytes((kt, tn), F32)
    return pl.pallas_call(
        functools.partial(body), name=name, grid=(N // tn, K // kt, B, nk),
        in_specs=[pl.BlockSpec((None, tk, kt), lambda j, kk, bi, t: (bi, t, kk)),
                  pl.BlockSpec((None, tk, tn), lambda j, kk, bi, t: (bi, t, j))],
        out_specs=pl.BlockSpec((sub, kt, w), lambda j, kk, bi, t: (j, kk, 0)),
        out_shape=_out((sub * N // tn, K, w), F32),
        compiler_params=_params(("arbitrary",) * 4, bb, _nbytes((kt, tn), F32) + (8 << 20)),
    )(*_hbm([a, b]))


def _mlp_bwd(R, dxo, xmid, ubf, ybf, mod, nw2, w1, w2):
    B, T = R.B, R.T

    def body(dxo_ref, x_ref, u_ref, y_ref, mod_ref, nw_ref, w1_hbm, w2_hbm,
             dxm_ref, du_ref, a_ref, dy_ref, dmod_ref, dnw_ref, w1_v, w2_v, sem):
        b, i = pl.program_id(0), pl.program_id(1)
        _load_once(jnp.logical_and(b == 0, i == 0), [(w1_hbm, w1_v), (w2_hbm, w2_v)], sem)
        dxo = dxo_ref[...]
        _, xn, r = _norm_mod(x_ref[...], nw_ref[...], _row(mod_ref, 3), _row(mod_ref, 4))
        dyb = (dxo * _row(mod_ref, 5)).astype(BF)
        dy_ref[...] = dyb
        dg2 = jnp.sum(dxo * y_ref[...].astype(F32), axis=0, keepdims=True)
        dh = jnp.zeros((TM, D), F32)
        for j in range(NDEV):
            rl = jnp.maximum(u_ref[:, FSH * j:FSH * (j + 1)].astype(F32), 0.0)
            a_ref[:, FSH * j:FSH * (j + 1)] = (rl * rl).astype(BF)
            du = (_dot_nt(dyb, w2_v[j]) * (2.0 * rl)).astype(BF)
            du_ref[:, FSH * j:FSH * (j + 1)] = du
            dh = dh + _dot_nt(du, w1_v[j])
        dx, dsh, dsc, dnw = _norm_mod_bwd(dh, xn, r, nw_ref[...], _row(mod_ref, 4))
        dxm_ref[...] = dxo + dx
        _acc_rows(dmod_ref, R.first_of_stream(i), {3: dsh, 4: dsc, 5: dg2})
        _acc_rows(dnw_ref, jnp.logical_and(b == 0, i == 0), {0: dnw})

    return R.call("mlp_bwd", body,
                  [(dxo, R.row(D)), (xmid, R.row(D)), (ubf, R.row(FF)), (ybf, R.row(D)), (mod, R.mod()), (nw2, R.const((1, D))),
                   (w1, ANY), (w2, ANY)],
                  [(_sds((B, T, D), F32), R.row(D)), (_sds((B, T, FF), BF), R.row(FF)), (_sds((B, T, FF), BF), R.row(FF)),
                   (_sds((B, T, D), BF), R.row(D)), (_sds((B, 2, 8, D), F32), R.mod()), (_sds((8, D), F32), R.const((8, D)))],
                  scratch=[pltpu.VMEM((NDEV, D, FSH), w1.dtype), pltpu.VMEM((NDEV, FSH, D), w2.dtype), pltpu.SemaphoreType.DMA((2,))],
                  extra=(2 * _nbytes((NDEV, D, FSH), BF)) + (8 << 20))


def _minus_in_lanes(x, col, first):
    b16 = jnp.bfloat16
    hi = col.astype(b16).astype(F32)
    r1 = col - hi
    mid = r1.astype(b16).astype(F32)
    lane = _iota(x.shape, 1)
    return jnp.where(lane == first, -hi, jnp.where(lane == first + 1, -mid, jnp.where(lane == first + 2, mid - r1, x)))


def _out_bwd(R, dxm, mod, mixbf, yf, yb, xbc, z, o, qt, lse, dsk, snw, wout):
    B, T = R.B, R.T

    def body(dxm_ref, mod_ref, mix_ref, yf_ref, yb_ref, xbc_ref, z_ref, o_ref, q_ref, lse_ref, dsk_ref, snw_ref, w_ref,
             dmix_ref, dz_ref, dyt_ref, dxsk_ref, qs_ref, dos_ref, dotr_ref, dpo_ref, dmod_ref, dvec_ref):
        b, i = pl.program_id(0), pl.program_id(1)
        dxm = dxm_ref[...]
        dmixb = (dxm * _row(mod_ref, 2)).astype(BF)
        dmix_ref[...] = dmixb
        dg1 = jnp.sum(dxm * mix_ref[...].astype(F32), axis=0, keepdims=True)
        dcat = _dot_nt(dmixb, w_ref[...])
        do_v = dcat[:, SSD_IN:SSD_IN + QW]
        doo = do_v * o_ref[...]
        for h in range(H):
            cols = slice(HP * h, HP * (h + 1))
            dd = jnp.sum(doo[:, cols], axis=1, keepdims=True)
            dos_ref[:, cols] = _minus_in_lanes(do_v[:, cols], dd, VH + 1).astype(BF)
            lse_b = jnp.broadcast_to(lse_ref[h], (HP, TM)).T
            qs_ref[:, cols] = _minus_in_lanes(q_ref[:, cols].astype(F32), lse_b, QK).astype(BF)
            dotr_ref[h] = do_v[:, cols].T.astype(BF)
        dpo_ref[...] = dcat[:, SSD_IN + QW:MIXW]
        dsn = dcat[:, 0:SSD_IN]
        ytot, zv, gz, r = _ssd_gate(yf_ref, yb_ref, xbc_ref, z_ref, dsk_ref)
        gh = gz * r
        dsnw = jnp.sum(dsn * gh, axis=0, keepdims=True)
        dgh = dsn * snw_ref[...]
        g0 = _group_mask()
        pr = dgh * gh
        half = SSD_IN // 2
        m0 = jnp.sum(jnp.where(g0, pr, 0.0), axis=1, keepdims=True) / half
        m1 = jnp.sum(jnp.where(g0, 0.0, pr), axis=1, keepdims=True) / half
        dgz = r * (dgh - gh * jnp.where(g0, m0, m1))
        dyt = dgz * _silu(zv)
        dz_ref[...] = (dgz * ytot * _dsilu(zv)).astype(BF)
        dyt_ref[...] = dyt
        dxsk_ref[...] = dyt * dsk_ref[...]
        ddsk = jnp.sum(dyt * xbc_ref[:, 0:SSD_IN], axis=0, keepdims=True)
        _acc_rows(dmod_ref, R.first_of_stream(i), {2: dg1})
        _acc_rows(dvec_ref, jnp.logical_and(b == 0, i == 0), {0: dsnw, 1: ddsk})

    return R.call("out_bwd", body,
                  [(dxm, R.row(D)), (mod, R.mod()), (mixbf, R.row(D)), (yf, R.row(SSD_IN)), (yb, R.row(SSD_IN)), (xbc, R.row(XBC)),
                   (z, R.row(SSD_IN)),
                   (o, R.row(QW)), (qt, R.row(QW)), (lse, pl.BlockSpec((None, H, 1, TM), lambda b, i: (b, 0, 0, i))),
                   (dsk, R.const((1, SSD_IN))), (snw, R.const((1, SSD_IN))), (wout, R.const((MIXW, D)))],
                  [(_sds((B, T, D), BF), R.row(D)), (_sds((B, T, SSD_IN), BF), R.row(SSD_IN)), (_sds((B, T, SSD_IN), F32), R.row(SSD_IN)),
                   (_sds((B, T, SSD_IN), F32), R.row(SSD_IN)), (_sds((B, T, QW), BF), R.row(QW)), (_sds((B, T, QW), BF), R.row(QW)),
                   (_sds((B, H, HP, T), BF), pl.BlockSpec((None, H, HP, TM), lambda b, i: (b, 0, 0, i))),
                   (_sds((B, T, PD), F32), R.row(PD)),
                   (_sds((B, 2, 8, D), F32), R.mod()), (_sds((8, SSD_IN), F32), R.const((8, SSD_IN)))],
                  extra=8 << 20)


def _pool_bwd(R, dpo, u, wbd, scale):
    B, T = R.B, R.T

    def body(d_ref, dpv_ref, dnx_ref, u_ref, upv_ref, unx_ref, w_ref, sc_ref, du_ref, dw_ref, dsc_ref):
        b, i = pl.program_id(0), pl.program_id(1)
        ext_u = R.ext(i, upv_ref, u_ref[...], unx_ref)
        ext_d = R.ext(i, dpv_ref, d_ref[...], dnx_ref)
        dm, cnt, valid, lane = _pool_centred(R, i, ext_u)
        ddm = _dot_nt(ext_d * sc_ref[...], w_ref[...]) * valid
        e = ddm / cnt
        a2 = e + _shift_rows(e, 1)
        a4 = _shift_rows(a2, -1) + _shift_rows(a2, 1)
        a8 = _shift_rows(a4, -2) + _shift_rows(a4, 2)
        a16 = _shift_rows(a8, -4) + _shift_rows(a8, 4)
        du_ref[...] = (_lane_select(lane, a2, a4, a8, a16) - ddm)[8:8 + TM].astype(BF)
        dmc = dm[8:8 + TM]
        dpo_c = d_ref[...]
        first = jnp.logical_and(b == 0, i == 0)

        @pl.when(first)
        def _():
            dw_ref[...] = jnp.zeros(dw_ref.shape, F32)

        dw_ref[...] += _dot_tn(dmc, dpo_c * sc_ref[...])
        _acc_rows(dsc_ref, first, {0: jnp.sum(dpo_c * _dot(dmc, w_ref[...]), axis=0, keepdims=True)})

    return R.call("pool_bwd", body,
                  [(dpo, R.row(PD)), (dpo, R.prev8(PD)), (dpo, R.next8(PD)), (u, R.row(PD)), (u, R.prev8(PD)), (u, R.next8(PD)),
                   (wbd, R.const((PD, PD))), (scale, R.const((1, PD)))],
                  [(_sds((B, T, PD), BF), R.row(PD)), (_sds((PD, PD), F32), R.const((PD, PD))), (_sds((8, PD), F32), R.const((8, PD)))],
                  extra=8 << 20)


def _flash_bwd(B, T, qs, kt, dos, q_t, k_t, v_t, do_t):
    nk, nq = T // TK, T // TQ
    HS = 2

    def body(k_ref, kt_ref, vt_ref, q_ref, do_ref, qt_ref, dot_ref, dq_ref, dk_ref, dv_ref, s_scr, dp_scr, ds_scr):
        j = pl.program_id(2)
        ctx_keys = jnp.where(j < CTX // TK, 1.0, 0.0)
        for hh in range(HS):
            cols = slice(HP * hh, HP * (hh + 1))
            s_scr[hh] = _dot(q_ref[:, cols], kt_ref[hh])
            dp_scr[hh] = _dot(do_ref[:, cols], vt_ref[hh])
        for hh in range(HS):
            cols = slice(HP * hh, HP * (hh + 1))
            dk_tr = jnp.zeros((HP, TK), F32)
            dv_tr = jnp.zeros((HP, TK), F32)
            for i in range(nq):
                rows = slice(TQ * i, TQ * (i + 1))
                p = jnp.exp(s_scr[hh, rows, :])
                if i < CTX // TQ:
                    p = p * ctx_keys
                ds = (p * dp_scr[hh, rows, :]).astype(BF)
                ds_scr[hh, rows, :] = ds
                dv_tr = dv_tr + _dot(dot_ref[hh, :, rows], p)
                dk_tr = dk_tr + _dot(qt_ref[hh, :, rows], ds)
            dq = _dot(ds_scr[hh], k_ref[:, cols])

            @pl.when(j == 0)
            def _():
                dq_ref[:, cols] = dq

            @pl.when(j > 0)
            def _():
                dq_ref[:, cols] += dq

            dk_ref[:, cols] = dk_tr.T
            dv_ref[:, cols] = dv_tr.T

    tspec = pl.BlockSpec((None, TK, HS * HP), lambda b, h, j: (b, j, h))
    fspec = pl.BlockSpec((None, T, HS * HP), lambda b, h, j: (b, 0, h))
    ttspec = pl.BlockSpec((None, HS, HP, TK), lambda b, h, j: (b, h, 0, j))
    ftspec = pl.BlockSpec((None, HS, HP, T), lambda b, h, j: (b, h, 0, 0))
    bb = HS * (4 * _nbytes((T, HP), BF) + _nbytes((T, HP), F32) + 8 * _nbytes((TK, HP), F32))
    scr = HS * (2 * _nbytes((T, TK), F32) + _nbytes((T, TK), BF))
    return pl.pallas_call(
        functools.partial(body), name="flash_bwd", grid=(B, H // HS, nk),
        in_specs=[tspec, ttspec, ttspec, fspec, fspec, ftspec, ftspec],
        out_specs=[fspec, tspec, tspec],
        out_shape=[_out((B, T, QW), F32)] * 3,
        scratch_shapes=[pltpu.VMEM((HS, T, TK), F32), pltpu.VMEM((HS, T, TK), F32), pltpu.VMEM((HS, T, TK), BF)],
        compiler_params=_params(("arbitrary",) * 3, bb, scr + (8 << 20)),
    )(*_hbm([kt, k_t, v_t, qs, dos, q_t, do_t]))


def _mla_bwd(R, dqt, dkt, dvt, qa, kva, qnw, kvnw, wq, wk, wv, cos, sin):
    B, T = R.B, R.T
    scale = QK ** -0.5

    def body(dq_ref, dk_ref, dv_ref, qa_ref, kva_ref, qnw_ref, kvnw_ref, wq_ref, wk_ref, wv_ref, cos_ref, sin_ref,
             dqp_ref, dkvb_ref, dqa_ref, dkva_ref, dkr_ref, dnw_ref):
        b, i = pl.program_id(0), pl.program_id(1)
        cos1, sin1 = cos_ref[...], sin_ref[...]
        dq = dq_ref[...] * scale
        dqp = _per_head(lambda g: g * cos1 + _swap8(g * sin1), dq).astype(BF)
        dqp_ref[...] = dqp
        dkv = dk_ref[...]
        dkb = dkv.astype(BF)
        dvb = dv_ref[...].astype(BF)
        dkvb_ref[:, :QW] = dkb
        dkvb_ref[:, QW:] = dvb
        dkk = dkv[:, 0:HP]
        for h in range(1, H):
            dkk = dkk + dkv[:, HP * h:HP * (h + 1)]
        lane = _iota((TM, HP), 1)
        rope_lane = jnp.logical_and(lane >= NOPE, lane < NOPE + ROPE)
        dkr_ref[...] = jnp.where(rope_lane, dkk * cos1 + _swap8(dkk * sin1), 0.0).astype(BF)
        _, qh, qr = _rms(qa_ref[...], qnw_ref[...])
        _, kh, kr_ = _rms(kva_ref[...], kvnw_ref[...])
        dcq = _dot_nt(dqp, wq_ref[...])
        dckv = _dot_nt(dkb, wk_ref[...]) + _dot_nt(dvb, wv_ref[...])
        dqa, dqw = _rms_bwd(dcq, qh, qr, qnw_ref[...])
        dkva, dkw = _rms_bwd(dckv, kh, kr_, kvnw_ref[...])
        dqa_ref[...] = dqa.astype(BF)
        dkva_ref[...] = dkva.astype(BF)
        _acc_rows(dnw_ref, jnp.logical_and(b == 0, i == 0), {0: dqw, 1: dkw})

    tab = pl.BlockSpec((TM, HP), lambda b, i: (i, 0))
    return R.call("mla_bwd", body,
                  [(dqt, R.row(QW)), (dkt, R.row(QW)), (dvt, R.row(QW)), (qa, R.row(QL)), (kva, R.row(QL)),
                   (qnw, R.const((1, QL))), (kvnw, R.const((1, QL))), (wq, R.const((QL, QW))), (wk, R.const((QL, QW))),
                   (wv, R.const((QL, QW))), (cos, tab), (sin, tab)],
                  [(_sds((B, T, QW), BF), R.row(QW)), (_sds((B, T, 2 * QW), BF), R.row(2 * QW))]
                  + [(_sds((B, T, QL), BF), R.row(QL))] * 2
                  + [(_sds((B, T, HP), BF), R.row(HP)), (_sds((8, QL), F32), R.const((8, QL)))], extra=8 << 20)


def _ssd_scan_bwd(B, T, dyt, xbc, dt2, alog2, hin_f, hin_b):
    nc, nctc = T // CH, CTX // CH

    def chain(d, dy_ref, xbc_ref, dt_ref, alog_ref, hin_ref, dxbc_ref, ddt_ref, da_ref, dhs_all):
        dhs = dhs_all.at[d]
        xbc_v = xbc_ref[...]
        dyv = dy_ref[...]
        a_all, dt_all, adt_all, q_all, inc, inc_t = _scan_common(d, dt_ref[...], alog_ref)
        ones_p = jnp.ones((P, HP), F32)
        total = lambda m: jnp.sum(jnp.sum(m, axis=0, keepdims=True), axis=1, keepdims=True)
        dq_parts, dqtot_parts, ddtx_parts = [], [], []
        for g in range(2):
            bg = xbc_v[:, SSD_IN + NST * g:SSD_IN + NST * (g + 1)]
            cg = xbc_v[:, SSD_IN + 2 * NST + NST * g:SSD_IN + 2 * NST + NST * (g + 1)]
            bg_t, cg_t = bg.T, cg.T
            gm = _dot(cg, bg_t)
            gm_t = _dot(bg, cg_t)
            dgm = jnp.zeros((CH, CH), F32)
            dgm_t = jnp.zeros((CH, CH), F32)
            dbg = jnp.zeros((CH, NST), F32)
            dcg = jnp.zeros((CH, NST), F32)
            for r in range(3):
                h = 3 * g + r
                lm, lm_t, eq, etot, dte = _head_decay(h, adt_all, q_all, inc, inc_t)
                eq_p, etot_p, dte_p = eq[:, :P], etot[:, :P], dte[:, :P]
                dt_p = dt_all[:, HP * h:HP * h + P]
                xs_h = xbc_v[:, P * h:P * (h + 1)]
                xh = xs_h * dt_p
                sm, sm_t = gm * lm, gm_t * lm_t
                dy_h = dyv[:, P * h:P * (h + 1)]
                hprev = hin_ref[P * h:P * (h + 1), :].T
                dho = dhs[:, P * h:P * (h + 1)]
                ds = _dot_nt(dy_h, xh)
                ds_t = _dot_nt(xh, dy_h)
                dx = _dot(sm_t, dy_h)
                edy = eq_p * dy_h
                yo = _dot(cg, hprev)
                dcg = dcg + _dot_nt(edy, hprev)
                dhin = _dot(cg_t, edy) + etot_p * dho
                zs = _dot(bg, dho)
                dx = dx + dte_p * zs
                wm = dte_p * xh * zs
                dbg = dbg + _dot_nt(xh * dte_p, dho)
                dgm = dgm + ds * lm
                dgm_t = dgm_t + ds_t * lm_t
                rs = jnp.sum(ds * sm - ds_t * sm_t, axis=1, keepdims=True) + jnp.sum(edy * yo - wm, axis=1, keepdims=True)
                dq_parts.append(jnp.broadcast_to(rs, (CH, HP)))
                dqtot_parts.append(total(hprev * dho) * etot + total(wm))
                dxbc_ref[:, P * h:P * (h + 1)] = dx * dt_p
                ddtx_parts.append(_dot01_r(dx * xs_h, ones_p, passes=2))
                dhs[:, P * h:P * (h + 1)] = dhin
            dcg = dcg + _dot(dgm, bg)
            dbg = dbg + _dot(dgm_t, cg)
            dxbc_ref[:, SSD_IN + NST * g:SSD_IN + NST * (g + 1)] = dbg
            dxbc_ref[:, SSD_IN + 2 * NST + NST * g:SSD_IN + 2 * NST + NST * (g + 1)] = dcg
        cat = lambda parts: jnp.concatenate(parts, axis=1)
        dadt_all = _dot01(inc_t, cat(dq_parts)) + cat(dqtot_parts)
        ddt_all = cat(ddtx_parts) + dadt_all * jnp.concatenate([a_all] * (CH // 8), axis=0)
        da_all = jnp.sum(dadt_all * dt_all, axis=0, keepdims=True)
        lane, lane1 = _iota((CH, HP), 1), _iota((1, HP), 1)
        ddt = jnp.zeros((CH, HP), F32)
        da = jnp.zeros((1, HP), F32)
        for h in range(H):
            ddt = ddt + jnp.where(lane == h, ddt_all[:, HP * h:HP * (h + 1)], 0.0)
            da = da + jnp.where(lane1 == h, da_all[:, HP * h:HP * (h + 1)], 0.0)
        ddt_ref[...] = ddt
        da_ref[d] += jnp.where(_iota((8, HP), 0) == 0, da, 0.0)

    def body(dyf_ref, dyb_ref, xf_ref, xb_ref, dtf_ref, dtb_ref, alog_ref, hf_ref, hb_ref,
             dxf_ref, dxb_ref, ddtf_ref, ddtb_ref, da_ref, dhs):
        @pl.when(pl.program_id(1) == 0)
        def _():
            dhs[...] = jnp.zeros(dhs.shape, F32)
            da_ref[...] = jnp.zeros(da_ref.shape, F32)

        chain(0, dyf_ref, xf_ref, dtf_ref, alog_ref, hf_ref, dxf_ref, ddtf_ref, da_ref, dhs)
        chain(1, dyb_ref, xb_ref, dtb_ref, alog_ref, hb_ref, dxb_ref, ddtb_ref, da_ref, dhs)

    cidx = lambda d: (lambda s: _chunk_index(d, nc - 1 - s, nc, nctc))
    specs = lambda d: dict(
        dy=pl.BlockSpec((None, CH, SSD_IN), lambda b, s: (b, cidx(d)(s), 0)),
        xbc=pl.BlockSpec((None, CH, XBC), lambda b, s: (b, cidx(d)(s), 0)),
        dt=pl.BlockSpec((None, None, CH, HP), lambda b, s: (b, d, cidx(d)(s), 0)),
        h=pl.BlockSpec((None, None, SSD_IN, NST), lambda b, s: (b, cidx(d)(s), 0, 0)),
        ddt=pl.BlockSpec((None, CH, HP), lambda b, s: (b, cidx(d)(s), 0)))
    f, r = specs(0), specs(1)
    bb = 2 * (2 * _nbytes((CH, XBC), F32) + 2 * _nbytes((CH, HP), F32) + _nbytes((CH, SSD_IN), F32) + _nbytes((SSD_IN, NST), F32))
    return pl.pallas_call(
        functools.partial(body), name="ssd_scan_bwd", grid=(B, nc),
        in_specs=[f["dy"], r["dy"], f["xbc"], r["xbc"], f["dt"], r["dt"], pl.BlockSpec((2, 8, HP), lambda b, s: (0, 0, 0)),
                  f["h"], r["h"]],
        out_specs=[f["xbc"], r["xbc"], f["ddt"], r["ddt"], pl.BlockSpec((None, 2, 8, HP), lambda b, s: (b, 0, 0, 0))],
        out_shape=[_out((B, T, XBC), F32)] * 2 + [_out((B, T, HP), F32)] * 2 + [_out((B, 2, 8, HP), F32)],
        scratch_shapes=[pltpu.VMEM((2, NST, SSD_IN), F32)],
        compiler_params=_params(("arbitrary",) * 2, bb, 16 << 20),
    )(*_hbm([dyt, dyt, xbc, xbc, dt2, dt2, alog2, hin_f, hin_b]))


def _ssd_prep_bwd(R, dxbc_f, dxbc_b, dxsk, ddt_f, ddt_b, xbc_raw, dt_raw, conv_w8, conv_b, dtb):
    B, T = R.B, R.T

    def body(dxf_ref, dxb_ref, dsk_ref, ddtf_ref, ddtb2_ref, raw_ref, pv_ref, nx_ref, dtr_ref, w_ref, b_ref, dtb_ref,
             dpre_ref, ddtr_ref, dcw_ref, dvec_ref, ddtb_ref):
        b, i = pl.program_id(0), pl.program_id(1)
        ext = R.ext(i, pv_ref, raw_ref[...], nx_ref)
        pre = _conv_pre(ext, w_ref, b_ref)
        dxbc = dxf_ref[...] + dxb_ref[...]
        skip = jnp.concatenate([dsk_ref[...], jnp.zeros((TM, XBC - SSD_IN), F32)], axis=1)
        dpre = (dxbc + skip) * _dsilu(pre)
        dpre_ref[...] = dpre
        first = jnp.logical_and(b == 0, i == 0)
        taps = {k: jnp.sum(dpre * _shift_rows(ext, k - 1)[8:8 + TM], axis=0, keepdims=True) for k in range(4)}
        _acc_rows(dcw_ref, first, taps)
        _acc_rows(dvec_ref, first, {0: jnp.sum(dpre, axis=0, keepdims=True)})
        ddt = ddtf_ref[...] + pltpu.roll(ddtb2_ref[...], H, axis=1)
        ddtr = ddt * _sigmoid(dtr_ref[...] + dtb_ref[...])
        ddtr = jnp.where(_iota((TM, HP), 1) < 2 * H, ddtr, 0.0)
        ddtr_ref[...] = ddtr.astype(BF)
        _acc_rows(ddtb_ref, first, {0: jnp.sum(ddtr, axis=0, keepdims=True)})

    return R.call("ssd_prep_bwd", body,
                  [(dxbc_f, R.row(XBC)), (dxbc_b, R.row(XBC)), (dxsk, R.row(SSD_IN)), (ddt_f, R.row(HP)), (ddt_b, R.row(HP)),
                   (xbc_raw, R.row(XBC)), (xbc_raw, R.prev8(XBC)),
                   (xbc_raw, R.next8(XBC)), (dt_raw, R.row(HP)), (conv_w8, R.const((8, XBC))), (conv_b, R.const((1, XBC))),
                   (dtb, R.const((1, HP)))],
                  [(_sds((B, T, XBC), F32), R.row(XBC)), (_sds((B, T, HP), BF), R.row(HP)), (_sds((8, XBC), F32), R.const((8, XBC))),
                   (_sds((8, XBC), F32), R.const((8, XBC))), (_sds((8, HP), F32), R.const((8, HP)))], extra=16 << 20)


def _in_bwd(R, dxm, x, mod, nw1, dz, dpre, dqa, dkva, dpool, dkr, ddtr, conv_w8, w_arr, latent_only=False):
    B, T = R.B, R.T
    dx_out = (_sds((B, T - CTX, D), F32), R.tgt(D)) if latent_only else (_sds((B, T, D), F32), R.row(D))

    def body(dxm_ref, x_ref, mod_ref, nw_ref, dz_ref, dp_ref, dpp_ref, dpn_ref, dqa_ref, dkva_ref, dpool_ref, dkr_ref, ddt_ref,
             cw_ref, w_ref, dx_ref, dproj_ref, dmod_ref, dnw_ref):
        b, i = pl.program_id(0), pl.program_id(1)
        ext = R.ext(i, dpp_ref, dp_ref[...], dpn_ref)
        draw = (_row(cw_ref, 0) * _shift_rows(ext, 1)[8:8 + TM] + _row(cw_ref, 1) * ext[8:8 + TM]
                + _row(cw_ref, 2) * _shift_rows(ext, -1)[8:8 + TM] + _row(cw_ref, 3) * _shift_rows(ext, -2)[8:8 + TM])
        dproj_ref[:, A_Z:A_XBC] = dz_ref[...]
        dproj_ref[:, A_XBC:A_QA] = draw.astype(BF)
        dproj_ref[:, A_QA:A_KVA] = dqa_ref[...]
        dproj_ref[:, A_KVA:A_POOL] = dkva_ref[...]
        dproj_ref[:, A_POOL:A_KR] = dpool_ref[...]
        dproj_ref[:, A_KR:A_DT] = dkr_ref[...]
        dproj_ref[:, A_DT:PC] = ddt_ref[...]
        dh = _dot_nt(dproj_ref[...], w_ref[...])
        _, xn, r = _norm_mod(x_ref[...], nw_ref[...], _row(mod_ref, 0), _row(mod_ref, 1))
        dx, dsh, dsc, dnw = _norm_mod_bwd(dh, xn, r, nw_ref[...], _row(mod_ref, 1))
        dx_ref[...] = dxm_ref[...] + dx
        _acc_rows(dmod_ref, R.first_of_stream(i), {0: dsh, 1: dsc})
        _acc_rows(dnw_ref, jnp.logical_and(b == 0, i == 0), {0: dnw})

    return R.call("in_bwd", body,
                  [(dxm, R.row(D)), (x, R.row(D)), (mod, R.mod()), (nw1, R.const((1, D))), (dz, R.row(SSD_IN)), (dpre, R.row(XBC)),
                   (dpre, R.prev8(XBC)), (dpre, R.next8(XBC)), (dqa, R.row(QL)), (dkva, R.row(QL)), (dpool, R.row(PD)), (dkr, R.row(HP)),
                   (ddtr, R.row(HP)), (conv_w8, R.const((8, XBC))), (w_arr, R.const((D, PC)))],
                  [dx_out, (_sds((B, T, PC), BF), R.row(PC)), (_sds((B, 2, 8, D), F32), R.mod()),
                   (_sds((8, D), F32), R.const((8, D)))], extra=12 << 20)


def _adaln_fwd(cs, mod_w):
    L, _, C = mod_w.shape

    def body(c_ref, w_ref, o_ref):
        s = _silu(c_ref[...]).astype(BF)
        for l in range(L):
            o_ref[l] = jnp.dot(s, w_ref[l].astype(BF), preferred_element_type=F32)

    return pl.pallas_call(functools.partial(body), name="adaln_fwd", out_shape=_sds((L, 24, C), F32),
                          compiler_params=_params(None, _nbytes(mod_w.shape, F32) + _nbytes((L, 24, C), F32), 8 << 20))(cs, mod_w)


def _adaln_bwd(cs, dm, mod_w):
    L, _, C = mod_w.shape

    def body(c_ref, dm_ref, w_ref, gw_ref, gc_ref):
        c = c_ref[...]
        s = _silu(c).astype(BF)
        acc = jnp.zeros((24, D), F32)
        for l in range(L):
            dmb = dm_ref[l].astype(BF)
            gw_ref[l] = _dot_tn(s, dmb)
            acc = acc + _dot_nt(dmb, w_ref[l])
        gc_ref[...] = acc * _dsilu(c)

    return pl.pallas_call(functools.partial(body), name="adaln_bwd", out_shape=[_sds((L, D, C), F32), _sds((24, D), F32)],
                          compiler_params=_params(None, 2 * _nbytes(mod_w.shape, F32), 8 << 20))(cs, dm, mod_w)


def _sum_blocks(name, parts):
    Pn, Rr, C = parts.shape

    def body(p_ref, o_ref):
        acc = p_ref[0]
        for k in range(1, Pn):
            acc = acc + p_ref[k]
        o_ref[...] = acc

    return pl.pallas_call(functools.partial(body), name=name, out_shape=_sds((Rr, C), F32),
                          compiler_params=_params(None, _nbytes(parts.shape, F32), 4 << 20))(parts)


def _adamw(name, parts, w, m, v, rt):
    nseg = len(parts)
    Pn, rseg, C = parts[0].shape
    Rr = nseg * rseg
    tiles = rseg // rt
    assert rseg % rt == 0 and w.shape == (Rr, C)
    c1 = 1.0 - ADAM_B1 ** ADAM_STEP
    c2 = 1.0 - ADAM_B2 ** ADAM_STEP

    def body(*refs):
        p_refs = refs[:nseg]
        w_ref, m_ref, v_ref, g_ref, d_ref, nm_ref, nv_ref = refs[nseg:]
        i = pl.program_id(0)
        g = None
        for s, p_ref in enumerate(p_refs):
            gs = p_ref[0].astype(F32)
            for k in range(1, Pn):
                gs = gs + p_ref[k].astype(F32)
            g = gs if g is None else jnp.where(i >= s * tiles, gs, g)
        mn = ADAM_B1 * m_ref[...] + (1.0 - ADAM_B1) * g
        vn = ADAM_B2 * v_ref[...] + (1.0 - ADAM_B2) * jnp.square(g)
        g_ref[...] = g
        nm_ref[...] = mn
        nv_ref[...] = vn
        d_ref[...] = -ADAM_LR * ((mn / c1) / (jnp.sqrt(vn / c2) + ADAM_EPS) + ADAM_WD * w_ref[...])

    spec = pl.BlockSpec((rt, C), lambda i: (i, 0))
    pspec = lambda s: pl.BlockSpec((Pn, rt, C), lambda i: (0, jnp.clip(i - s * tiles, 0, tiles - 1), 0))
    bb = nseg * Pn * _nbytes((rt, C), parts[0].dtype) + 7 * _nbytes((rt, C), F32)
    return pl.pallas_call(
        functools.partial(body), name=name, grid=(Rr // rt,),
        in_specs=[pspec(s) for s in range(nseg)] + [spec, spec, spec],
        out_specs=[spec] * 4, out_shape=[_out((Rr, C), F32)] * 4,
        compiler_params=_params(("arbitrary",), bb, 4 << 20),
    )(*_hbm(list(parts) + [w, m, v]))


MESH = pl.DeviceIdType.MESH


def _my_pos():
    return lax.axis_index("x"), lax.axis_index("y"), lax.axis_index("c")


def _dev_index(x, y, c):
    return 4 * x + 2 * y + c


def _all_gather(name, shards):
    n = len(shards)

    def body(*refs):
        ins, outs = refs[:n], refs[n:2 * n]
        send_sems, recv_sems, local_sem = refs[2 * n:]
        x, y, c = _my_pos()
        me, sibling = (x, y, c), (x, y, 1 - c)
        chips = [(1 - x, y), (x, 1 - y), (1 - x, 1 - y)]

        def copy(t, k, block, to, src=None):
            slot = outs[t].at[_dev_index(*block)]
            return pltpu.make_async_remote_copy(
                src_ref=slot if src is None else src, dst_ref=slot,
                send_sem=send_sems.at[t, k], recv_sem=recv_sems.at[t, k], device_id=to, device_id_type=MESH)

        mine = [pltpu.make_async_copy(ins[t], outs[t].at[_dev_index(*me)], local_sem.at[t]) for t in range(n)]
        for cp in mine:
            cp.start()
        first = []
        for t in range(n):
            first.append(copy(t, 0, me, sibling, src=ins[t]))
            first += [copy(t, 1 + j, me, (*chip, c), src=ins[t]) for j, chip in enumerate(chips)]
        for cp in first:
            cp.start()
        passed = []
        for j, chip in enumerate(chips):
            for t in range(n):
                copy(t, 1 + j, (*chip, c), me).wait_recv()
                cp = copy(t, 4 + j, (*chip, c), sibling)
                cp.start()
                passed.append(cp)
        for t in range(n):
            copy(t, 0, sibling, me).wait_recv()
            for j, chip in enumerate(chips):
                copy(t, 4 + j, (*chip, 1 - c), me).wait_recv()
        for cp in first + passed:
            cp.wait_send()
        for cp in mine:
            cp.wait()

    return pl.pallas_call(
        functools.partial(body), name=name,
        in_specs=[ANY] * n, out_specs=[ANY] * n,
        out_shape=[_sds((NDEV,) + s.shape, s.dtype) for s in shards],
        scratch_shapes=[pltpu.SemaphoreType.DMA((n, 7)), pltpu.SemaphoreType.DMA((n, 7)), pltpu.SemaphoreType.DMA((n,))],
    )(*shards)


def _all_to_all(name, parts):
    n = len(parts)

    def body(*refs):
        ins, outs = refs[:n], refs[n:2 * n]
        send_sems, recv_sems, local_sem = refs[2 * n:]
        x, y, c = _my_pos()
        me = _dev_index(x, y, c)
        peers = [(x ^ ((k >> 2) & 1), y ^ ((k >> 1) & 1), c ^ (k & 1)) for k in range(1, NDEV)]
        mine = [pltpu.make_async_copy(ins[t].at[me], outs[t].at[me], local_sem.at[t]) for t in range(n)]
        for cp in mine:
            cp.start()
        sends = []
        for t in range(n):
            for k, peer in enumerate(peers):
                cp = pltpu.make_async_remote_copy(
                    src_ref=ins[t].at[_dev_index(*peer)], dst_ref=outs[t].at[me],
                    send_sem=send_sems.at[t, k], recv_sem=recv_sems.at[t, k], device_id=peer, device_id_type=MESH)
                cp.start()
                sends.append(cp)
        for t in range(n):
            for k, peer in enumerate(peers):
                slot = outs[t].at[_dev_index(*peer)]
                pltpu.make_async_remote_copy(
                    src_ref=slot, dst_ref=slot, send_sem=send_sems.at[t, k], recv_sem=recv_sems.at[t, k],
                    device_id=peer, device_id_type=MESH).wait_recv()
        for cp in sends:
            cp.wait_send()
        for cp in mine:
            cp.wait()

    return pl.pallas_call(
        functools.partial(body), name=name,
        in_specs=[ANY] * n, out_specs=[ANY] * n,
        out_shape=[_sds(p.shape, p.dtype) for p in parts],
        scratch_shapes=[pltpu.SemaphoreType.DMA((n, 7)), pltpu.SemaphoreType.DMA((n, 7)), pltpu.SemaphoreType.DMA((n,))],
    )(*parts)


SEM = pl.BlockSpec(memory_space=pltpu.SEMAPHORE)
IN_HBM = pl.BlockSpec(memory_space=pltpu.HBM)
DATAFLOW = pltpu.SideEffectType.DATAFLOW_SIDE_EFFECTING


def _flip_peers(x, y, c):
    return [(x ^ ((k >> 2) & 1), y ^ ((k >> 1) & 1), c ^ (k & 1)) for k in range(1, NDEV)]


def _split_copies(srcs, lands, send_sems, recv_sems, gather):
    x, y, c = _my_pos()
    me = _dev_index(x, y, c)
    out = []
    for t in range(len(srcs)):
        for k, peer in enumerate(_flip_peers(x, y, c)):
            p = _dev_index(*peer)
            src = srcs[t] if gather else srcs[t].at[p]
            sems = dict(send_sem=send_sems.at[7 * t + k], recv_sem=recv_sems.at[7 * t + k], device_id=peer, device_id_type=MESH)
            out.append((pltpu.make_async_remote_copy(src_ref=src, dst_ref=lands[t].at[me], **sems),
                        pltpu.make_async_remote_copy(src_ref=src, dst_ref=lands[t].at[p], **sems)))
    return out


def _exchange_start(name, collective_id, srcs, gather):
    n = len(srcs)
    lands = [lax.empty(((NDEV,) + s.shape) if gather else s.shape, s.dtype) for s in srcs]

    def body(*refs):
        src_refs, land_refs = refs[:n], refs[n:2 * n]
        send_sems, recv_sems = refs[2 * n], refs[2 * n + 1]
        token = refs[-1]
        barrier = pltpu.get_barrier_semaphore()
        for peer in _flip_peers(*_my_pos()):
            pl.semaphore_signal(barrier, inc=1, device_id=peer, device_id_type=MESH)
        pl.semaphore_wait(barrier, NDEV - 1)
        for send, _ in _split_copies(src_refs, land_refs, send_sems, recv_sems, gather):
            send.start()
        token[...] = jnp.zeros(token.shape, token.dtype)

    hbm = lambda a: pltpu.HBM(a.shape, a.dtype)
    res = pl.pallas_call(
        functools.partial(body), name=name,
        out_shape=[pltpu.SemaphoreType.DMA((7 * n,)), pltpu.SemaphoreType.DMA((7 * n,))] + [hbm(s) for s in srcs]
        + [hbm(a) for a in lands] + [_sds((8, HP), F32)],
        in_specs=[IN_HBM] * (2 * n), out_specs=[SEM, SEM] + [IN_HBM] * (2 * n) + [pl.BlockSpec(memory_space=pltpu.VMEM)],
        input_output_aliases={i: 2 + i for i in range(2 * n)},
        compiler_params=pltpu.CompilerParams(has_side_effects=DATAFLOW, collective_id=collective_id),
    )(*_hbm(list(srcs) + lands))
    return res[0], res[1], list(res[2:2 + n]), list(res[2 + n:2 + 2 * n]), res[-1]


def _exchange_wait(name, send_sems, recv_sems, srcs, lands, after, gather):
    n = len(srcs)

    def body(*refs):
        src_refs, land_refs = refs[:n], refs[n:2 * n]
        for _, recv in _split_copies(src_refs, land_refs, refs[2 * n], refs[2 * n + 1], gather):
            recv.wait_send()
            recv.wait_recv()

    hbm = lambda a: pltpu.HBM(a.shape, a.dtype)
    res = pl.pallas_call(
        functools.partial(body), name=name,
        out_shape=[hbm(s) for s in srcs] + [hbm(a) for a in lands],
        in_specs=[IN_HBM] * (2 * n) + [SEM, SEM, ANY], out_specs=[IN_HBM] * (2 * n),
        input_output_aliases={i: i for i in range(2 * n)},
        compiler_params=pltpu.CompilerParams(has_side_effects=DATAFLOW),
    )(*srcs, *lands, send_sems, recv_sems, after)
    return list(res[:n]), list(res[n:])


def _with_own(lands, own_blocks, me):
    out = []
    for land, own in zip(lands, own_blocks):
        out.append(lax.dynamic_update_slice(land, own[None], (me,) + (0,) * own.ndim))
    return out


def _arrange_w_in(w):
    z = lambda n: jnp.zeros((w.shape[0], n), w.dtype)
    return jnp.concatenate([w[:, 0:1280], w[:, 1292:1548], w[:, 1548:1804], w[:, 1836:2092],
                            z(64), w[:, 1804:1836], z(32), w[:, 1280:1292], z(HP - 2 * H)], axis=1)


def _unarrange_w_in(g):
    return jnp.concatenate([g[:, 0:1280], g[:, A_DT:A_DT + 2 * H], g[:, A_QA:A_KVA], g[:, A_KVA:A_POOL],
                            g[:, A_KR + NOPE:A_KR + NOPE + ROPE], g[:, A_POOL:A_KR]], axis=1)


def _pad_heads(w, width):
    k = w.shape[0]
    return jnp.pad(w.reshape(k, H, width), ((0, 0), (0, 0), (0, HP - width))).reshape(k, H * HP)


def _unpad_heads(g, width):
    k = g.shape[0]
    return g.reshape(k, H, HP)[:, :, :width].reshape(k, H * width)


def _arrange_w_out(w):
    att = jnp.pad(w[SSD_IN:2 * SSD_IN].reshape(H, VH, D), ((0, 0), (0, HP - VH), (0, 0))).reshape(QW, D)
    return jnp.concatenate([w[0:SSD_IN], att, w[2 * SSD_IN:]], axis=0)


def _unarrange_w_out(g):
    att = g[SSD_IN:SSD_IN + QW].reshape(H, HP, D)[:, :VH].reshape(SSD_IN, D)
    return jnp.concatenate([g[0:SSD_IN], att, g[SSD_IN + QW:]], axis=0)


def _rope_tables(T):
    n = T - CTX
    rows = n // GRID_W
    pairs = ROPE // 4
    inv = ROPE_THETA ** (-jnp.arange(pairs, dtype=F32) / pairs)
    ar = jnp.arange(rows, dtype=F32)[:, None] * inv
    ac = jnp.arange(GRID_W, dtype=F32)[:, None] * inv
    by_row = lambda a: jnp.repeat(a, GRID_W, axis=0)
    by_col = lambda a: jnp.tile(a, (rows, 1))
    cos = jnp.concatenate([by_row(jnp.cos(ar))] * 2 + [by_col(jnp.cos(ac))] * 2, axis=1)
    sin = jnp.concatenate([-by_row(jnp.sin(ar)), by_row(jnp.sin(ar)), -by_col(jnp.sin(ac)), by_col(jnp.sin(ac))], axis=1)
    ones, zeros = jnp.ones((n, NOPE), F32), jnp.zeros((n, NOPE), F32)
    cos = jnp.concatenate([ones, cos, ones[:, :HP - QK]], axis=1)
    sin = jnp.concatenate([zeros, sin, zeros[:, :HP - QK]], axis=1)
    return (jnp.concatenate([jnp.ones((CTX, HP), F32), cos], axis=0),
            jnp.concatenate([jnp.zeros((CTX, HP), F32), sin], axis=0))


def _lane_pad(v, n):
    return jnp.pad(v, (0, n - v.shape[0]))[None, :]


def _layer_weights(w_in, w_q_b, w_kv_b, conv_w, pool_w):
    kv = w_kv_b.reshape(QL, H, NOPE + VH)
    wbd = jnp.concatenate([jnp.pad(pool_w[g], ((0, 0), (64 * g, PD - 64 * (g + 1)))) for g in range(4)], axis=0)
    return dict(
        w_in=_arrange_w_in(w_in).astype(BF),
        wq=_pad_heads(w_q_b, QK).astype(BF),
        wk=_pad_heads(kv[:, :, :NOPE].reshape(QL, H * NOPE), NOPE).astype(BF),
        wv=_pad_heads(kv[:, :, NOPE:].reshape(QL, H * VH), VH).astype(BF),
        conv_w8=jnp.pad(conv_w, ((0, 4), (0, 0))), wbd=wbd)


def _layer_fwd(R, x, mod, lw, sp, late, cos, sin):
    B, T = R.B, R.T
    z, xbc_raw, qa, kva, pool_in, kr, dt_raw, h1 = _in_proj(R, x, mod, sp["nw1"], lw["w_in"])
    xbc, dt2 = _ssd_prep(R, xbc_raw, dt_raw, lw["conv_w8"], sp["conv_b"], sp["dtb"])
    yf, yb, hin_f, hin_b = _ssd_scan(B, T, xbc, dt2, sp["alog2"])
    qt, kt, q_t, k_t, v_t, cq, ckv = _mla_prep(R, qa, kva, kr, sp["qnw"], sp["kvnw"], lw["wq"], lw["wk"], lw["wv"], cos, sin)
    o, lse = _flash_fwd(B, T, q_t, kt, v_t)
    pool = _pool_fwd(R, pool_in, lw["wbd"], sp["pscale"])[0]
    w_out, w1, w2 = late(o)
    w_out = _arrange_w_out(w_out).astype(BF)
    xmid, cat, mixbf = _out_proj(R, x, mod, yf, yb, xbc, z, o, pool, sp["dsk"], sp["snw"], w_out)
    xo, h2, ubf, ybf = _mlp_fwd(R, xmid, mod, sp["nw2"], w1, w2)
    saved = dict(x=x, z=z, xbc_raw=xbc_raw, qa=qa, kva=kva, pool_in=pool_in, dt_raw=dt_raw, h1=h1, xbc=xbc, dt2=dt2,
                 yf=yf, yb=yb, hin_f=hin_f, hin_b=hin_b, qt=qt, kt=kt, q_t=q_t, k_t=k_t, v_t=v_t, cq=cq, ckv=ckv, o=o, lse=lse,
                 cat=cat, mixbf=mixbf, xmid=xmid, h2=h2, ubf=ubf, ybf=ybf, w_out=w_out, w1=w1, w2=w2)
    return xo, saved


def _layer_bwd(R, dxo, sv, mod, lw, sp, cos, sin, on_mlp=None, latent_only=False):
    B, T = R.B, R.T
    dxm, du, abf, dyb, dmod_a, dnw2 = _mlp_bwd(R, dxo, sv["xmid"], sv["ubf"], sv["ybf"], mod, sp["nw2"], sv["w1"], sv["w2"])
    g_w1 = _tn_matmul("dw_mlp1", sv["h2"], du, 4 * FSH, sub=4)
    g_w2 = _tn_matmul("dw_mlp2", abf, dyb, D)[0].reshape(NDEV, FSH, D)
    snw = sp["snw"]
    tok = on_mlp(g_w1, g_w2) if on_mlp is not None else None
    if tok is not None:
        snw = snw + tok
    dmix, dz, dyt, dxsk, qs, dos, do_t, dpo, dmod_b, dvec_o = _out_bwd(R, dxm, mod, sv["mixbf"], sv["yf"], sv["yb"], sv["xbc"], sv["z"],
                                                                 sv["o"], sv["qt"], sv["lse"], sp["dsk"], snw, sv["w_out"])
    g_wout = _unarrange_w_out(_tn_matmul("dw_out", sv["cat"], dmix, D)[0])
    dpool_in, g_wbd, dpsc = _pool_bwd(R, dpo, sv["pool_in"], lw["wbd"], sp["pscale"])
    dqt, dkt, dvt = _flash_bwd(B, T, qs, sv["kt"], dos, sv["q_t"], sv["k_t"], sv["v_t"], do_t)
    dqp, dkvb, dqa, dkva, dkr, dnw_qk = _mla_bwd(R, dqt, dkt, dvt, sv["qa"], sv["kva"], sp["qnw"], sp["kvnw"],
                                                 lw["wq"], lw["wk"], lw["wv"], cos, sin)
    g_wq = _unpad_heads(_tn_matmul("dw_q", sv["cq"], dqp, QW)[0], QK)
    g_kv = _tn_matmul("dw_kv", sv["ckv"], dkvb, 2 * QW)[0]
    g_wk = _unpad_heads(g_kv[:, :QW], NOPE).reshape(QL, H, NOPE)
    g_wv = _unpad_heads(g_kv[:, QW:], VH).reshape(QL, H, VH)
    g_wkv = jnp.concatenate([g_wk, g_wv], axis=2).reshape(QL, H * (NOPE + VH))
    dxbc_f, dxbc_b, ddt_f, ddt_b, da = _ssd_scan_bwd(B, T, dyt, sv["xbc"], sv["dt2"], sp["alog2"], sv["hin_f"], sv["hin_b"])
    dpre, ddtr, dcw, dcb, ddtb = _ssd_prep_bwd(R, dxbc_f, dxbc_b, dxsk, ddt_f, ddt_b, sv["xbc_raw"], sv["dt_raw"], lw["conv_w8"],
                                                sp["conv_b"], sp["dtb"])
    dx, dproj, dmod_c, dnw1 = _in_bwd(R, dxm, sv["x"], mod, sp["nw1"], dz, dpre, dqa, dkva, dpool_in, dkr, ddtr,
                                      lw["conv_w8"], lw["w_in"], latent_only)
    g_win = _unarrange_w_in(jnp.concatenate(list(_tn_matmul("dw_in", sv["h1"], dproj, PC)), axis=1))
    a2 = -jnp.exp(sp["alog2"][:, 0, :H])
    small = dict(
        norm1_w=dnw1[0], norm2_w=dnw2[0], conv_w=dcw[0:4], conv_b=dcb[0], dt_bias=ddtb[0, :2 * H].reshape(2, H),
        a_log=jnp.sum(da[:, :, 0, :H], axis=0) * a2, ssd_d=jnp.sum(dvec_o[1].reshape(H, P), axis=1), ssd_norm_w=dvec_o[0],
        q_a_norm_w=dnw_qk[0], kv_a_norm_w=dnw_qk[1],
        pool_w=jnp.stack([g_wbd[64 * g:64 * (g + 1), 64 * g:64 * (g + 1)] for g in range(4)]), pool_scale=dpsc[0])
    big = dict(w_in=g_win, w_out=g_wout, w_q_b=g_wq, w_kv_b=g_wkv, w_mlp1=g_w1, w_mlp2=g_w2)
    return dx, big, small, dmod_a + dmod_b + dmod_c


def _small_params(l, norm1_w, norm2_w, conv_b, dt_bias, a_log, ssd_d, ssd_norm_w, q_a_norm_w, kv_a_norm_w, pool_scale):
    alog2 = jnp.broadcast_to(jnp.pad(a_log[l], ((0, 0), (0, HP - H)))[:, None, :], (2, 8, HP))
    return dict(nw1=norm1_w[l][None], nw2=norm2_w[l][None], conv_b=conv_b[l][None],
                dtb=_lane_pad(dt_bias[l].reshape(2 * H), HP), alog2=alog2,
                dsk=jnp.repeat(ssd_d[l], P)[None], snw=ssd_norm_w[l][None], qnw=q_a_norm_w[l][None],
                kvnw=kv_a_norm_w[l][None], pscale=pool_scale[l][None])


SMALL_NAMES = ["mod_b", "norm1_w", "norm2_w", "conv_b", "dt_bias", "a_log", "ssd_d", "ssd_norm_w", "q_a_norm_w",
               "kv_a_norm_w", "pool_w", "pool_scale", "final_norm_w"]


def _pack(arrs):
    rows = []
    for a in arrs:
        f = a.reshape(-1).astype(F32)
        n = -(-f.shape[0] // HP) * HP
        rows.append(jnp.pad(f, (0, n - f.shape[0])).reshape(-1, HP))
    out = jnp.concatenate(rows, axis=0)
    pad = (-out.shape[0]) % 8
    return jnp.pad(out, ((0, pad), (0, 0)))


def _unpack(pack, like):
    outs, r = [], 0
    for a in like:
        n = math.prod(a.shape)
        nr = -(-n // HP)
        outs.append(pack[r:r + nr].reshape(-1)[:n].reshape(a.shape))
        r += nr
    return outs


def _local_step(x, ctx, target, mods, full_of, small_w, on_grads=None, on_mlp=None):
    B, N = x.shape[0], x.shape[1]
    T = CTX + N
    R = _Rows(B, T)
    cos, sin = _rope_tables(T)
    xu = jnp.concatenate([ctx, x], axis=1)
    L = len(mods)
    lws, sps, saves = [], [], []
    for l in range(L):
        f = full_of(l, xu)
        lws.append(_layer_weights(f["w_in"], f["w_q_b"], f["w_kv_b"], f["conv_w"], small_w["pool_w"][l]))
        sps.append(_small_params(l, *[small_w[k] for k in ["norm1_w", "norm2_w", "conv_b", "dt_bias", "a_log", "ssd_d",
                                                          "ssd_norm_w", "q_a_norm_w", "kv_a_norm_w", "pool_scale"]]))
        xu, sv = _layer_fwd(R, xu, mods[l], lws[l], sps[l], f["late"], cos, sin)
        saves.append(sv)
    dx, loss8, dfnw = _loss_head(R, xu, target, small_w["final_norm_w"][None])
    bigs, smalls, dmods = [None] * L, [None] * L, [None] * L
    for l in reversed(range(L)):
        hook = functools.partial(on_mlp, l) if on_mlp is not None else None
        dx, bigs[l], smalls[l], dmods[l] = _layer_bwd(R, dx, saves[l], mods[l], lws[l], sps[l], cos, sin, hook, l == 0)
        if on_grads is not None:
            bigs[l], tok = on_grads(l, bigs[l], dx)
            if tok is not None:
                sps[l - 1] = dict(sps[l - 1], nw2=sps[l - 1]["nw2"] + tok)
    return loss8[0, 0], dx, bigs, smalls, dfnw[0], dmods


def kernel(x, c, ctx, c_ctx, mod_w, mod_b, norm1_w, norm2_w, w_in, conv_w, conv_b, dt_bias, a_log, ssd_d, ssd_norm_w, q_a_norm_w, w_q_b, kv_a_norm_w, w_kv_b, pool_w, pool_scale, w_out, w_mlp1, w_mlp2, final_norm_w, loss_target, m_c_ctx, m_mod_w, m_mod_b, m_norm1_w, m_norm2_w, m_w_in, m_conv_w, m_conv_b, m_dt_bias, m_a_log, m_ssd_d, m_ssd_norm_w, m_q_a_norm_w, m_w_q_b, m_kv_a_norm_w, m_w_kv_b, m_pool_w, m_pool_scale, m_w_out, m_w_mlp1, m_w_mlp2, m_final_norm_w, v_c_ctx, v_mod_w, v_mod_b, v_norm1_w, v_norm2_w, v_w_in, v_conv_w, v_conv_b, v_dt_bias, v_a_log, v_ssd_d, v_ssd_norm_w, v_q_a_norm_w, v_w_q_b, v_kv_a_norm_w, v_w_kv_b, v_pool_w, v_pool_scale, v_w_out, v_w_mlp1, v_w_mlp2, v_final_norm_w):
    args = dict(locals())
    B = x.shape[0]
    L = mod_w.shape[0]
    me = _dev_index(*_my_pos())
    CS = mod_w.shape[2]

    big_names = ["w_in", "w_out", "w_q_b", "w_kv_b", "w_mlp1", "w_mlp2"]
    shards = {n: args[n].astype(BF) for n in big_names}
    early, late_names = ["w_in", "w_q_b", "w_kv_b"], ["w_out", "w_mlp1", "w_mlp2"]
    g0 = _all_gather("gather_weights", [c, conv_w] + [shards[n][0] for n in early])
    c_all, convw_all = g0[0], g0[1]
    gathered = {0: dict(zip(early, g0[2:]))}
    cs = jnp.concatenate([c_all.reshape(NDEV * B, D), c_ctx[None], jnp.zeros((24 - NDEV * B - 1, D), F32)], axis=0)
    m_loc = _adaln_fwd(cs, mod_w)
    m_all = _all_gather("gather_mod", [m_loc])[0]
    m_full = jnp.moveaxis(m_all, 0, 2).reshape(L, 24, NDEV * CS) + mod_b[:, None, :]
    pending = {}
    tok = jnp.zeros((), F32)
    res = _exchange_start("gather_start_0", 1, [shards[n][0] for n in late_names], gather=True)
    pending[0] = res[:4]
    tok = tok + res[4][0, 0]
    for l in range(1, L):
        res = _exchange_start("gather_start_%d" % l, 1 + l, [shards[n][l] for n in early + late_names], gather=True)
        pending[l] = res[:4]
        tok = tok + res[4][0, 0]
    mods = []
    for l in range(L):
        ex = lax.dynamic_slice(m_full[l], (me * B, 0), (B, 6 * D)).reshape(B, 6, D)
        cc = jnp.broadcast_to(m_full[l, NDEV * B].reshape(1, 6, D), (B, 6, D))
        mods.append(jnp.pad(jnp.stack([cc, ex], axis=1), ((0, 0), (0, 0), (0, 2), (0, 0))) + tok)

    def full_of(l, xu):
        if l > 0:
            own, lands = _exchange_wait("gather_wait_%d" % l, *pending.pop(l), xu, gather=True)
            gathered[l] = dict(zip(early + late_names, _with_own(lands, own, me)))

        def late(after):
            if l == 0:
                own, lands = _exchange_wait("gather_wait_0", *pending.pop(0), after, gather=True)
                gathered[0].update(zip(late_names, _with_own(lands, own, me)))
            gl = gathered[l]
            return gl["w_out"].reshape(D, D), gl["w_mlp1"], gl["w_mlp2"]

        g = gathered[l]
        return dict(
            w_in=g["w_in"].reshape(D, IN_COLS),
            w_q_b=jnp.moveaxis(g["w_q_b"], 0, 1).reshape(QL, H * QK),
            w_kv_b=jnp.moveaxis(g["w_kv_b"], 0, 1).reshape(QL, H * (NOPE + VH)),
            conv_w=jnp.moveaxis(convw_all[:, l], 0, 1).reshape(4, XBC), late=late)

    def grad_blocks(big, names):
        make = dict(
            w_in=lambda g: g.reshape(NDEV, D // NDEV, IN_COLS), w_out=lambda g: g.reshape(NDEV, D // NDEV, D),
            w_q_b=lambda g: jnp.moveaxis(g.reshape(QL, NDEV, -1), 1, 0),
            w_kv_b=lambda g: jnp.moveaxis(g.reshape(QL, NDEV, -1), 1, 0), w_mlp1=lambda g: g, w_mlp2=lambda g: g)
        return [make[n](big[n]).astype(BF) for n in names]

    sent, sent_mlp = {}, {}
    rest_names = ["w_in", "w_out", "w_q_b", "w_kv_b"]

    def on_mlp(l, g_w1, g_w2):
        if l > 0:
            return None
        res = _exchange_start("grads_mlp_start_0", 1 + 2 * L, [g_w1.astype(BF), g_w2.astype(BF)], gather=False)
        sent_mlp[l] = res[:4]
        return res[4][0, 0]

    def on_grads(l, big, dx):
        if l == 0:
            return grad_blocks(big, rest_names), None
        res = _exchange_start("grads_start_%d" % l, 1 + L + l, grad_blocks(big, big_names), gather=False)
        sent[l] = res[:4]
        return None, res[4][0, 0]

    small_w = {k: args[k] for k in SMALL_NAMES if k != "mod_b"}
    loss_part, grad_x, blocks, smalls, dfnw, dmods = _local_step(x, ctx, loss_target, mods, full_of, small_w, on_grads, on_mlp)
    loss = lax.psum(loss_part, ("x", "y", "c"))

    dm_ex = jnp.stack([dmods[l][:, 1, :6].reshape(B, 6 * D) for l in range(L)])
    dm_cc = jnp.stack([jnp.sum(dmods[l][:, 0, :6], axis=0).reshape(6 * D) for l in range(L)])
    small_parts = dict(
        mod_b=jnp.sum(dm_ex, axis=1) + dm_cc,
        **{k: jnp.stack([smalls[l][k] for l in range(L)]) for k in SMALL_NAMES[1:-1]},
        final_norm_w=dfnw, conv_w=jnp.stack([smalls[l]["conv_w"] for l in range(L)]), dm_cc=dm_cc)
    adam_grads = [small_parts[k] for k in SMALL_NAMES]
    extras = [small_parts["conv_w"], dm_cc, dm_ex]
    pack = jnp.concatenate([_pack(adam_grads), _pack(extras)], axis=0)
    pack_all = _all_gather("gather_small_grads", [pack])[0]
    wpack = _pack([args[k] for k in SMALL_NAMES])
    mpack = _pack([args["m_" + k] for k in SMALL_NAMES])
    vpack = _pack([args["v_" + k] for k in SMALL_NAMES])
    n_adam = wpack.shape[0]
    res_small = _adamw("adamw_small", [pack_all[:, :n_adam]], wpack, mpack, vpack, n_adam)
    small_out = [_unpack(r, [args[k] for k in SMALL_NAMES]) for r in res_small]
    ext_all = pack_all[:, n_adam:]
    g_conv_full, dm_cc_tot, _ = _unpack(_sum_blocks("sum_small_grads", ext_all), extras)
    dm_ex_all = jnp.stack([_unpack(ext_all[k], extras)[2] for k in range(NDEV)], axis=1)
    dm_rows = jnp.concatenate([dm_ex_all.reshape(L, NDEV * B, 6 * D), dm_cc_tot[:, None, :],
                               jnp.zeros((L, 24 - NDEV * B - 1, 6 * D), F32)], axis=1)
    dm_loc = lax.dynamic_slice(dm_rows, (0, 0, me * CS), (L, 24, CS))
    g_modw, gc_part = _adaln_bwd(cs, dm_loc, mod_w)
    gc_all = _all_gather("gather_cctx_grad", [gc_part[NDEV * B:NDEV * B + 8]])[0]
    cpad = lambda a: jnp.pad(a[None], ((0, 7), (0, 0)))
    res_cc = _adamw("adamw_cctx", [gc_all], cpad(c_ctx), cpad(m_c_ctx), cpad(v_c_ctx), 8)
    cc_out = [r[0] for r in res_cc]

    def waited(name, handles, names):
        srcs, lands = _exchange_wait(name, *handles, grad_x, gather=False)
        own = [lax.dynamic_index_in_dim(s, me, 0, keepdims=False) for s in srcs]
        return dict(zip(names, _with_own(lands, own, me)))

    recv = {0: dict(zip(rest_names, _all_to_all("exchange_grads", blocks[0])))}
    recv[0].update(waited("grads_mlp_wait_0", sent_mlp.pop(0), ["w_mlp1", "w_mlp2"]))
    for l in range(1, L):
        recv[l] = waited("grads_wait_%d" % l, sent.pop(l), big_names)
    big_out = {}
    for name in big_names:
        w = args[name]
        Rr, C = math.prod(w.shape[:-1]), w.shape[-1]
        rl = Rr // L
        rt = rl if rl * C <= (1 << 17) else rl // 4
        res = _adamw("adamw_" + name, [recv[l][name].reshape(NDEV, rl, C) for l in range(L)], w.reshape(Rr, C),
                     args["m_" + name].reshape(Rr, C), args["v_" + name].reshape(Rr, C), rt)
        big_out[name] = [r.reshape(w.shape) for r in res]
    res = _adamw("adamw_mod_w", [g_modw.reshape(1, L * D, CS)], mod_w.reshape(L * D, CS), m_mod_w.reshape(L * D, CS),
                 v_mod_w.reshape(L * D, CS), L * D // 8)
    big_out["mod_w"] = [r.reshape(mod_w.shape) for r in res]
    CW = conv_w.shape[2]
    g_conv = lax.dynamic_slice(g_conv_full, (0, 0, me * CW), (L, 4, CW))
    res = _adamw("adamw_conv_w", [g_conv.reshape(1, L * 4, CW)], conv_w.reshape(L * 4, CW), m_conv_w.reshape(L * 4, CW),
                 v_conv_w.reshape(L * 4, CW), L * 4)
    big_out["conv_w"] = [r.reshape(conv_w.shape) for r in res]

    weights = ["c_ctx", "mod_w", "mod_b", "norm1_w", "norm2_w", "w_in", "conv_w", "conv_b", "dt_bias", "a_log", "ssd_d",
               "ssd_norm_w", "q_a_norm_w", "w_q_b", "kv_a_norm_w", "w_kv_b", "pool_w", "pool_scale", "w_out", "w_mlp1",
               "w_mlp2", "final_norm_w"]
    outs = [loss, grad_x]
    for kind in range(4):
        for name in weights:
            if name == "c_ctx":
                outs.append(cc_out[kind])
            elif name in big_out:
                outs.append(big_out[name][kind])
            else:
                outs.append(small_out[kind][SMALL_NAMES.index(name)])
    return tuple(outs)
```

```python
import functools
import math

import jax
import jax.numpy as jnp
from jax import lax
from jax.experimental import pallas as pl
from jax.experimental.pallas import tpu as pltpu

F32 = jnp.float32
BF = jnp.bfloat16
MXU = BF

D = 1024
CTX = 256
GRID_W = 64
EPS = 1e-6
H = 6
P = 64
SSD_IN = 384
NST = 128
XBC = 896
CH = 128
QL = 256
NOPE = 64
ROPE = 32
VH = 64
QK = 96
HP = 128
QW = H * HP
PD = 256
FF = 4096
IN_COLS = 2092
ROPE_THETA = 10000.0
PC = 2304
A_Z, A_XBC, A_QA, A_KVA, A_POOL, A_KR, A_DT = 0, 384, 1280, 1536, 1792, 2048, 2176
MIXW = SSD_IN + QW + PD
NDEV = 8
FSH = FF // NDEV
TM = 256
TQ = 256
TK = 256
VMEM_CAP = 64 * 1024 * 1024
ADAM_LR, ADAM_B1, ADAM_B2, ADAM_EPS, ADAM_WD, ADAM_STEP = 0.001, 0.9, 0.999, 1e-08, 0.01, 10


def _nbytes(shape, dtype):
    n = 1
    for s in shape:
        if s is not None:
            n *= s
    return n * jnp.dtype(dtype).itemsize


def _params(sem, block_bytes, extra=0):
    lim = min(2 * block_bytes + extra + (8 << 20), VMEM_CAP - (6 << 20))
    return pltpu.CompilerParams(dimension_semantics=sem, vmem_limit_bytes=int(lim))


def _hbm(arrays):
    return [pltpu.with_memory_space_constraint(a, pltpu.HBM) for a in arrays]


def _dot(a, b):
    return jnp.dot(a.astype(MXU), b.astype(MXU), preferred_element_type=F32)


def _dot_nt(a, b):
    return lax.dot_general(a.astype(MXU), b.astype(MXU), (((1,), (1,)), ((), ())), preferred_element_type=F32)


def _dot_tn(a, b):
    return lax.dot_general(a.astype(MXU), b.astype(MXU), (((0,), (0,)), ((), ())), preferred_element_type=F32)


def _dot01(m01, x):
    b16 = jnp.bfloat16
    m = m01.astype(b16)
    hi = x.astype(b16)
    r1 = x - hi.astype(F32)
    mid = r1.astype(b16)
    lo = (r1 - mid.astype(F32)).astype(b16)
    f = lambda v: jnp.dot(m, v, preferred_element_type=F32)
    return f(hi) + f(mid) + f(lo)


def _sigmoid(x):
    return 1.0 / (1.0 + jnp.exp(-x))


def _silu(x):
    return x * _sigmoid(x)


def _dsilu(x):
    s = _sigmoid(x)
    return s * (1.0 + x * (1.0 - s))


def _iota(shape, dim):
    return lax.broadcasted_iota(jnp.int32, shape, dim)


def _row(ref, k):
    blk = ref[...]
    return jnp.sum(jnp.where(_iota(blk.shape, 0) == k, blk, 0.0), axis=0, keepdims=True)


def _shift_rows(x, k):
    n = x.shape[0]
    return pltpu.roll(x, (-k) % n, axis=0)


class _Rows:
    def __init__(self, B, T):
        self.B, self.T = B, T
        self.nt = T // TM
        self.nct = CTX // TM

    def row(self, F):
        return pl.BlockSpec((None, TM, F), lambda b, i: (b, i, 0))

    def row2(self, F):
        return pl.BlockSpec((None, 2, TM, F), lambda b, i: (b, 0, i, 0))

    def prev8(self, F):
        return pl.BlockSpec((None, 8, F), lambda b, i: (b, jnp.maximum(i * (TM // 8) - 1, 0), 0))

    def next8(self, F):
        last = self.T // 8 - 1
        return pl.BlockSpec((None, 8, F), lambda b, i: (b, jnp.minimum((i + 1) * (TM // 8), last), 0))

    def mod(self):
        nct = self.nct
        return pl.BlockSpec((None, None, 8, D), lambda b, i: (b, jnp.where(i < nct, 0, 1), 0, 0))

    def const(self, shape):
        z = (0,) * len(shape)
        return pl.BlockSpec(tuple(shape), lambda b, i: z)

    def tgt(self, F):
        nct = self.nct
        return pl.BlockSpec((None, TM, F), lambda b, i: (b, jnp.maximum(i - nct, 0), 0))

    def call(self, name, body, ins, outs, scratch=(), extra=0):
        arrays = [a for a, _ in ins]
        in_specs = [s for _, s in ins]
        out_shape = [pltpu.HBM(o.shape, o.dtype) for o, _ in outs]
        out_specs = [s for _, s in outs]
        bb = 0
        for a, s in list(ins) + list(outs):
            if s.block_shape is not None:
                bb += _nbytes(s.block_shape, a.dtype)
        return pl.pallas_call(
            functools.partial(body), name=name, grid=(self.B, self.nt),
            in_specs=in_specs, out_specs=out_specs, out_shape=out_shape, scratch_shapes=list(scratch),
            compiler_params=_params(("arbitrary", "arbitrary"), bb, extra),
        )(*_hbm(arrays))

    def first_of_stream(self, i):
        return jnp.logical_or(i == 0, i == self.nct)

    def last_of_stream(self, i):
        return jnp.logical_or(i == self.nct - 1, i == self.nt - 1)

    def ext(self, i, prev_ref, cur, next_ref):
        pv = prev_ref[...].astype(F32) * jnp.where(self.first_of_stream(i), 0.0, 1.0)
        nx = next_ref[...].astype(F32) * jnp.where(self.last_of_stream(i), 0.0, 1.0)
        return jnp.concatenate([pv, cur, nx], axis=0)

    def stream_pos(self, i, rows):
        start = jnp.where(i < self.nct, 0, CTX)
        n = jnp.where(i < self.nct, CTX, self.T - CTX)
        t = i * TM - 8 - start + _iota((rows, 1), 0)
        return t, n


def _sds(shape, dtype):
    return jax.ShapeDtypeStruct(tuple(shape), dtype)


def _out(shape, dtype):
    return pltpu.HBM(tuple(shape), dtype)


def _norm_mod(x, nw, sh, sc):
    r = lax.rsqrt(jnp.mean(x * x, axis=-1, keepdims=True) + EPS)
    xn = x * r
    return xn * nw * (1.0 + sc) + sh, xn, r


def _norm_mod_bwd(dh, xn, r, nw, sc):
    dsh = jnp.sum(dh, axis=0, keepdims=True)
    dsc = jnp.sum(dh * (xn * nw), axis=0, keepdims=True)
    dnw = jnp.sum(dh * (1.0 + sc) * xn, axis=0, keepdims=True)
    dxn = dh * nw * (1.0 + sc)
    dx = r * (dxn - xn * jnp.mean(dxn * xn, axis=-1, keepdims=True))
    return dx, dsh, dsc, dnw


def _acc_rows(ref, first, rows):
    rid = _iota(ref.shape, 0)
    upd = jnp.zeros(ref.shape, F32)
    for k, v in rows.items():
        upd = upd + jnp.where(rid == k, v, 0.0)

    @pl.when(first)
    def _():
        ref[...] = upd

    @pl.when(jnp.logical_not(first))
    def _():
        ref[...] += upd


def _in_proj(R, x, mod, nw1, w_arr):
    B, T = R.B, R.T

    def body(x_ref, mod_ref, nw_ref, w_ref, z_ref, xbc_ref, qa_ref, kva_ref, pool_ref, kr_ref, dt_ref, h_ref):
        h, _, _ = _norm_mod(x_ref[...], nw_ref[...], _row(mod_ref, 0), _row(mod_ref, 1))
        hb = h.astype(BF)
        h_ref[...] = hb
        p = jnp.dot(hb, w_ref[...], preferred_element_type=F32)
        z_ref[...] = p[:, A_Z:A_XBC]
        xbc_ref[...] = p[:, A_XBC:A_QA]
        qa_ref[...] = p[:, A_QA:A_KVA]
        kva_ref[...] = p[:, A_KVA:A_POOL]
        pool_ref[...] = p[:, A_POOL:A_KR]
        kr_ref[...] = p[:, A_KR:A_DT]
        dt_ref[...] = p[:, A_DT:PC]

    widths = [SSD_IN, XBC, QL, QL, PD, HP, HP]
    outs = [(_sds((B, T, w), F32), R.row(w)) for w in widths] + [(_sds((B, T, D), BF), R.row(D))]
    return R.call("in_proj", body,
                  [(x, R.row(D)), (mod, R.mod()), (nw1, R.const((1, D))), (w_arr, R.const((D, PC)))],
                  outs, extra=8 << 20)


def _conv_pre(ext, w_ref, b_ref):
    return (_row(w_ref, 0) * _shift_rows(ext, -1)[8:8 + TM] + _row(w_ref, 1) * ext[8:8 + TM]
            + _row(w_ref, 2) * _shift_rows(ext, 1)[8:8 + TM] + _row(w_ref, 3) * _shift_rows(ext, 2)[8:8 + TM]
            + b_ref[...])


def _softplus(x):
    return jnp.maximum(x, 0.0) + jnp.log(1.0 + jnp.exp(-jnp.abs(x)))


def _ssd_prep(R, xbc_raw, dt_raw, conv_w8, conv_b, dtb):
    B, T = R.B, R.T

    def body(raw_ref, pv_ref, nx_ref, dtr_ref, w_ref, b_ref, dtb_ref, xbc_ref, dt_ref):
        i = pl.program_id(1)
        ext = R.ext(i, pv_ref, raw_ref[...], nx_ref)
        xbc_ref[...] = _silu(_conv_pre(ext, w_ref, b_ref))
        lane = _iota((TM, HP), 1)
        dtv = _softplus(dtr_ref[...] + dtb_ref[...])
        keep = lane < H
        dt_ref[0] = jnp.where(keep, dtv, 0.0)
        dt_ref[1] = jnp.where(keep, pltpu.roll(dtv, HP - H, axis=1), 0.0)

    return R.call("ssd_prep", body,
                  [(xbc_raw, R.row(XBC)), (xbc_raw, R.prev8(XBC)), (xbc_raw, R.next8(XBC)), (dt_raw, R.row(HP)),
                   (conv_w8, R.const((8, XBC))), (conv_b, R.const((1, XBC))), (dtb, R.const((1, HP)))],
                  [(_sds((B, T, XBC), F32), R.row(XBC)), (_sds((B, 2, T, HP), F32), R.row2(HP))], extra=12 << 20)


def _chunk_index(d, s, nc, nctc):
    if d == 0:
        return s
    return jnp.where(s < nctc, nctc - 1 - s, nc - 1 - (s - nctc))


def _dot01_r(x, m01, passes=3):
    b16 = jnp.bfloat16
    m = m01.astype(b16)
    out, rest = None, x
    for _ in range(passes):
        part = rest.astype(b16)
        rest = rest - part.astype(F32)
        term = jnp.dot(part, m, preferred_element_type=F32)
        out = term if out is None else out + term
    return out


def _scan_common(d, dtv, alog_ref):
    sel = _iota((HP, H * HP), 0) == (_iota((HP, H * HP), 1) >> 7)
    a_all = _dot01_r(-jnp.exp(alog_ref[d]), sel)
    dt_all = _dot01_r(dtv, sel)
    adt_all = dt_all * jnp.concatenate([a_all] * (CH // 8), axis=0)
    row = _iota((CH, CH), 0)
    col = _iota((CH, CH), 1)
    inc = col <= row if d == 0 else col >= row
    inc_t = row <= col if d == 0 else row >= col
    q_all = _dot01(inc, adt_all)
    return a_all, dt_all, adt_all, q_all, inc, inc_t


def _head_decay(h, adt_all, q_all, inc, inc_t):
    q = q_all[:, HP * h:HP * (h + 1)]
    q_t = q.T
    qtot = jnp.sum(adt_all[:, HP * h:HP * (h + 1)], axis=0, keepdims=True)
    lm = jnp.where(inc, jnp.exp(q - q_t), 0.0)
    lm_t = jnp.where(inc_t, jnp.exp(q_t - q), 0.0)
    return lm, lm_t, jnp.exp(q), jnp.exp(qtot), jnp.exp(qtot - q)


def _ssd_scan(B, T, xbc, dt2, alog2):
    nc, nctc = T // CH, CTX // CH

    def chain(d, xbc_ref, dt_ref, alog_ref, y_ref, hin_ref, hs):
        xbc_v = xbc_ref[...]
        dtv = dt_ref[...]
        blk = alog_ref[d]
        adt = dtv * -jnp.exp(jnp.sum(jnp.where(_iota(blk.shape, 0) == 0, blk, 0.0), axis=0, keepdims=True))
        row, col = _iota((CH, CH), 0), _iota((CH, CH), 1)
        inc = col <= row if d == 0 else col >= row
        inc_tf = (row <= col if d == 0 else row >= col).astype(F32)
        q = _dot01(inc, adt)
        hin_ref[...] = hs[d]
        for g in range(2):
            bg = xbc_v[:, SSD_IN + NST * g:SSD_IN + NST * (g + 1)]
            cg = xbc_v[:, SSD_IN + 2 * NST + NST * g:SSD_IN + 2 * NST + NST * (g + 1)]
            gm = _dot_nt(cg, bg)
            for r in range(3):
                h = 3 * g + r
                onehot = (_iota((1, HP), 1) == h).astype(F32)
                adt_h = jnp.sum(adt * onehot, axis=1, keepdims=True)
                qc = jnp.sum(q * onehot, axis=1, keepdims=True)
                dt_h = jnp.sum(dtv * onehot, axis=1, keepdims=True)
                qr = jnp.sum(adt_h * inc_tf, axis=0, keepdims=True)
                qtot = jnp.sum(adt_h, axis=0, keepdims=True)
                lm = jnp.where(inc, jnp.exp(qc - qr), 0.0)
                xh = xbc_v[:, P * h:P * (h + 1)] * dt_h
                hprev = hs[d, P * h:P * (h + 1), :]
                y_ref[:, P * h:P * (h + 1)] = _dot(gm * lm, xh) + jnp.exp(qc) * _dot_nt(cg, hprev)
                hs[d, P * h:P * (h + 1), :] = jnp.exp(qtot) * hprev + _dot_tn(xh * jnp.exp(qtot - qc), bg)

    def body(xf_ref, xb_ref, dtf_ref, dtb_ref, alog_ref, yf_ref, yb_ref, hf_ref, hb_ref, hs):
        @pl.when(pl.program_id(1) == 0)
        def _():
            hs[...] = jnp.zeros(hs.shape, F32)

        chain(0, xf_ref, dtf_ref, alog_ref, yf_ref, hf_ref, hs)
        chain(1, xb_ref, dtb_ref, alog_ref, yb_ref, hb_ref, hs)

    cidx = lambda d: (lambda s: _chunk_index(d, s, nc, nctc))
    specs = lambda d: dict(
        xbc=pl.BlockSpec((None, CH, XBC), lambda b, s: (b, cidx(d)(s), 0)),
        dt=pl.BlockSpec((None, None, CH, HP), lambda b, s: (b, d, cidx(d)(s), 0)),
        y=pl.BlockSpec((None, CH, SSD_IN), lambda b, s: (b, cidx(d)(s), 0)),
        h=pl.BlockSpec((None, None, SSD_IN, NST), lambda b, s: (b, cidx(d)(s), 0, 0)))
    f, r = specs(0), specs(1)
    bb = 2 * (_nbytes((CH, XBC), F32) + _nbytes((CH, HP), F32) + _nbytes((CH, SSD_IN), F32) + _nbytes((SSD_IN, NST), F32))
    return pl.pallas_call(
        functools.partial(body), name="ssd_scan", grid=(B, nc),
        in_specs=[f["xbc"], r["xbc"], f["dt"], r["dt"], pl.BlockSpec((2, 8, HP), lambda b, s: (0, 0, 0))],
        out_specs=[f["y"], r["y"], f["h"], r["h"]],
        out_shape=[_out((B, T, SSD_IN), F32)] * 2 + [_out((B, nc, SSD_IN, NST), F32)] * 2,
        scratch_shapes=[pltpu.VMEM((2, SSD_IN, NST), F32)],
        compiler_params=_params(("arbitrary",) * 2, bb, 12 << 20),
    )(*_hbm([xbc, xbc, dt2, dt2, alog2]))


def _swap8(u):
    lane = _iota(u.shape, 1)
    n = u.shape[1]
    return jnp.where((lane & 15) < 8, pltpu.roll(u, n - 8, axis=1), pltpu.roll(u, 8, axis=1))


def _rope(u, cos, sin_signed):
    return u * cos + _swap8(u) * sin_signed


def _rms(x, w):
    r = lax.rsqrt(jnp.mean(x * x, axis=-1, keepdims=True) + EPS)
    xh = x * r
    return xh * w, xh, r


def _rms_bwd(dy, xh, r, w):
    dw = jnp.sum(dy * xh, axis=0, keepdims=True)
    dxh = dy * w
    return r * (dxh - xh * jnp.mean(dxh * xh, axis=-1, keepdims=True)), dw


def _tile6(t):
    return jnp.concatenate([t] * H, axis=1)


def _per_head(fn, u):
    return jnp.concatenate([fn(u[:, HP * h:HP * (h + 1)]) for h in range(H)], axis=1)


def _mla_prep(R, qa, kva, kr, qnw, kvnw, wq, wk, wv, cos, sin):
    B, T = R.B, R.T
    scale = QK ** -0.5

    def body(qa_ref, kva_ref, kr_ref, qnw_ref, kvnw_ref, wq_ref, wk_ref, wv_ref, cos_ref, sin_ref,
             qt_ref, kt_ref, qtr_ref, ktr_ref, vtr_ref, cq_ref, ckv_ref):
        cq, _, _ = _rms(qa_ref[...], qnw_ref[...])
        ckv, _, _ = _rms(kva_ref[...], kvnw_ref[...])
        cqb, ckvb = cq.astype(BF), ckv.astype(BF)
        cq_ref[...] = cqb
        ckv_ref[...] = ckvb
        cos1, sin1 = cos_ref[...], sin_ref[...]
        q = _per_head(lambda u: _rope(u, cos1, sin1), jnp.dot(cqb, wq_ref[...], preferred_element_type=F32)) * scale
        qt_ref[...] = q.astype(BF)
        kk = _rope(kr_ref[...], cos1, sin1)
        hl = _iota((TM, QW), 1) & (HP - 1)
        k = jnp.dot(ckvb, wk_ref[...], preferred_element_type=F32) + _tile6(kk)
        k = jnp.where(jnp.logical_and(hl >= QK, hl < QK + 3), 1.0, k)
        kt_ref[...] = k.astype(BF)
        v = jnp.dot(ckvb, wv_ref[...], preferred_element_type=F32)
        v = jnp.where(jnp.logical_and(hl >= VH, hl < VH + 4), 1.0, v)
        for h in range(H):
            cols = slice(HP * h, HP * (h + 1))
            qtr_ref[h] = q[:, cols].T.astype(BF)
            ktr_ref[h] = k[:, cols].T.astype(BF)
            vtr_ref[h] = v[:, cols].T.astype(BF)

    tr = (_sds((B, H, HP, T), BF), pl.BlockSpec((None, H, HP, TM), lambda b, i: (b, 0, 0, i)))
    return R.call("mla_prep", body,
                  [(qa, R.row(QL)), (kva, R.row(QL)), (kr, R.row(HP)), (qnw, R.const((1, QL))), (kvnw, R.const((1, QL))),
                   (wq, R.const((QL, QW))), (wk, R.const((QL, QW))), (wv, R.const((QL, QW))),
                   (cos, pl.BlockSpec((TM, HP), lambda b, i: (i, 0))), (sin, pl.BlockSpec((TM, HP), lambda b, i: (i, 0)))],
                  [(_sds((B, T, QW), BF), R.row(QW))] * 2 + [tr] * 3 + [(_sds((B, T, QL), BF), R.row(QL))] * 2, extra=12 << 20)


def _flash_fwd(B, T, q_t, kt, v_t):
    nq, nk = T // TQ, T // TK
    HS = 2

    def body(q_ref, k_ref, v_ref, o_ref, lse_ref, s_scr):
        i = pl.program_id(2)

        def attend(nch):
            ms = []
            for hh in range(HS):
                q_tr = q_ref[hh]
                mrun = None
                for j in range(nch):
                    s = _dot(k_ref[TK * j:TK * (j + 1), HP * hh:HP * (hh + 1)], q_tr)
                    s_scr[hh, j] = s
                    mrun = s if mrun is None else jnp.maximum(mrun, s)
                ms.append(jnp.max(mrun, axis=0, keepdims=True))
            row = _iota((HP, TQ), 0)
            for hh in range(HS):
                acc_t = jnp.zeros((HP, TQ), F32)
                for j in range(nch):
                    acc_t = acc_t + _dot(v_ref[hh, :, TK * j:TK * (j + 1)], jnp.exp(s_scr[hh, j] - ms[hh]))
                l = jnp.sum(jnp.where(row == VH, acc_t, 0.0), axis=0, keepdims=True)
                o_ref[:, HP * hh:HP * (hh + 1)] = jnp.where(row < VH, acc_t / l, 0.0).T
                lse_ref[hh] = ms[hh] + jnp.log(l)

        @pl.when(i < CTX // TQ)
        def _():
            attend(CTX // TK)

        @pl.when(i >= CTX // TQ)
        def _():
            attend(nk)

    bb = HS * (_nbytes((TQ, HP), BF) + 2 * _nbytes((T, HP), BF) + 2 * _nbytes((TQ, HP), F32))
    return pl.pallas_call(
        functools.partial(body), name="flash_fwd", grid=(B, H // HS, nq),
        in_specs=[pl.BlockSpec((None, HS, HP, TQ), lambda b, h, i: (b, h, 0, i)),
                  pl.BlockSpec((None, T, HS * HP), lambda b, h, i: (b, 0, h)),
                  pl.BlockSpec((None, HS, HP, T), lambda b, h, i: (b, h, 0, 0))],
        out_specs=[pl.BlockSpec((None, TQ, HS * HP), lambda b, h, i: (b, i, h)),
                   pl.BlockSpec((None, HS, 1, TQ), lambda b, h, i: (b, h, 0, i))],
        out_shape=[_out((B, T, QW), F32), _out((B, H, 1, T), F32)],
        scratch_shapes=[pltpu.VMEM((HS, nk, TK, TQ), F32)],
        compiler_params=_params(("arbitrary",) * 3, bb, _nbytes((HS, nk, TK, TQ), F32) + (8 << 20)),
    )(*_hbm([q_t, kt, v_t]))


def _pool_terms(R, i, rows):
    t, n = R.stream_pos(i, rows)
    lane = _iota((1, PD), 1)
    half = jnp.where(lane < 64, 1, jnp.where(lane < 128, 2, jnp.where(lane < 192, 4, 8)))
    cnt = (jnp.minimum(t + half, n) - jnp.maximum(t - half, 0)).astype(F32)
    valid = jnp.logical_and(t >= 0, t < n)
    return jnp.where(valid, cnt, 1.0), valid.astype(F32), lane


def _lane_select(lane, a2, a4, a8, a16):
    return jnp.where(lane < 64, a2, jnp.where(lane < 128, a4, jnp.where(lane < 192, a8, a16)))


def _pool_centred(R, i, ext):
    cnt, valid, lane = _pool_terms(R, i, ext.shape[0])
    s2 = ext + _shift_rows(ext, -1)
    s4 = _shift_rows(s2, -1) + _shift_rows(s2, 1)
    s8 = _shift_rows(s4, -2) + _shift_rows(s4, 2)
    s16 = _shift_rows(s8, -4) + _shift_rows(s8, 4)
    return _lane_select(lane, s2, s4, s8, s16) / cnt - ext, cnt, valid, lane


def _pool_fwd(R, u, wbd, scale):
    B, T = R.B, R.T

    def body(u_ref, pv_ref, nx_ref, w_ref, sc_ref, o_ref):
        i = pl.program_id(1)
        ext = R.ext(i, pv_ref, u_ref[...], nx_ref)
        dm, _, _, _ = _pool_centred(R, i, ext)
        o_ref[...] = _dot(dm[8:8 + TM], w_ref[...]) * sc_ref[...]

    return R.call("pool_fwd", body,
                  [(u, R.row(PD)), (u, R.prev8(PD)), (u, R.next8(PD)), (wbd, R.const((PD, PD))), (scale, R.const((1, PD)))],
                  [(_sds((B, T, PD), F32), R.row(PD))], extra=8 << 20)


def _group_mask():
    return _iota((1, SSD_IN), 1) < SSD_IN // 2


def _ssd_gate(yf_ref, yb_ref, xbc_ref, z_ref, dsk_ref):
    ytot = yf_ref[...] + yb_ref[...] + xbc_ref[:, 0:SSD_IN] * dsk_ref[...]
    z = z_ref[...]
    gz = ytot * _silu(z)
    g0 = _group_mask()
    sq = gz * gz
    s0 = jnp.sum(jnp.where(g0, sq, 0.0), axis=1, keepdims=True)
    s1 = jnp.sum(jnp.where(g0, 0.0, sq), axis=1, keepdims=True)
    half = SSD_IN // 2
    r = jnp.where(g0, lax.rsqrt(s0 / half + EPS), lax.rsqrt(s1 / half + EPS))
    return ytot, z, gz, r


def _out_proj(R, x, mod, yf, yb, xbc, z, o, pool, dsk, snw, wout):
    B, T = R.B, R.T

    def body(x_ref, mod_ref, yf_ref, yb_ref, xbc_ref, z_ref, o_ref, pool_ref, dsk_ref, snw_ref, w_ref, xmid_ref, cat_ref,
             mix_ref):
        _, _, gz, r = _ssd_gate(yf_ref, yb_ref, xbc_ref, z_ref, dsk_ref)
        cat_ref[:, 0:SSD_IN] = (gz * r * snw_ref[...]).astype(BF)
        cat_ref[:, SSD_IN:SSD_IN + QW] = o_ref[...].astype(BF)
        cat_ref[:, SSD_IN + QW:MIXW] = pool_ref[...].astype(BF)
        mix = jnp.dot(cat_ref[...], w_ref[...], preferred_element_type=F32)
        mix_ref[...] = mix.astype(BF)
        xmid_ref[...] = x_ref[...] + _row(mod_ref, 2) * mix

    return R.call("out_proj", body,
                  [(x, R.row(D)), (mod, R.mod()), (yf, R.row(SSD_IN)), (yb, R.row(SSD_IN)), (xbc, R.row(XBC)), (z, R.row(SSD_IN)),
                   (o, R.row(QW)),
                   (pool, R.row(PD)), (dsk, R.const((1, SSD_IN))), (snw, R.const((1, SSD_IN))), (wout, R.const((MIXW, D)))],
                  [(_sds((B, T, D), F32), R.row(D)), (_sds((B, T, MIXW), BF), R.row(MIXW)), (_sds((B, T, D), BF), R.row(D))],
                  extra=8 << 20)


def _load_once(first, pairs, sem):
    @pl.when(first)
    def _():
        cps = [pltpu.make_async_copy(src, dst, sem.at[k]) for k, (src, dst) in enumerate(pairs)]
        for cp in cps:
            cp.start()
        for cp in cps:
            cp.wait()


ANY = pl.BlockSpec(memory_space=pl.ANY)


def _mlp_fwd(R, xmid, mod, nw2, w1, w2):
    B, T = R.B, R.T

    def body(x_ref, mod_ref, nw_ref, w1_hbm, w2_hbm, xo_ref, h_ref, u_ref, y_ref, w1_v, w2_v, sem):
        first = jnp.logical_and(pl.program_id(0) == 0, pl.program_id(1) == 0)
        _load_once(first, [(w1_hbm, w1_v), (w2_hbm, w2_v)], sem)
        x = x_ref[...]
        h, _, _ = _norm_mod(x, nw_ref[...], _row(mod_ref, 3), _row(mod_ref, 4))
        hb = h.astype(BF)
        h_ref[...] = hb
        y = jnp.zeros((TM, D), F32)
        for j in range(NDEV):
            u = jnp.dot(hb, w1_v[j], preferred_element_type=F32)
            u_ref[:, FSH * j:FSH * (j + 1)] = u.astype(BF)
            a = jnp.square(jnp.maximum(u, 0.0))
            y = y + jnp.dot(a.astype(BF), w2_v[j], preferred_element_type=F32)
        y_ref[...] = y.astype(BF)
        xo_ref[...] = x + _row(mod_ref, 5) * y

    return R.call("mlp_fwd", body,
                  [(xmid, R.row(D)), (mod, R.mod()), (nw2, R.const((1, D))), (w1, ANY), (w2, ANY)],
                  [(_sds((B, T, D), F32), R.row(D)), (_sds((B, T, D), BF), R.row(D)), (_sds((B, T, FF), BF), R.row(FF)),
                   (_sds((B, T, D), BF), R.row(D))],
                  scratch=[pltpu.VMEM((NDEV, D, FSH), w1.dtype), pltpu.VMEM((NDEV, FSH, D), w2.dtype), pltpu.SemaphoreType.DMA((2,))],
                  extra=(2 * _nbytes((NDEV, D, FSH), BF)) + (8 << 20))


def _loss_head(R, x, tgt, fnw):
    B, T = R.B, R.T

    def body(x_ref, t_ref, w_ref, dx_ref, loss_ref, dw_ref):
        b, i = pl.program_id(0), pl.program_id(1)
        live = jnp.where(i >= R.nct, 1.0, 0.0)
        y, xh, r = _rms(x_ref[...], w_ref[...])
        err = (y - t_ref[...]) * live
        dy = err / D
        dxn, dw = _rms_bwd(dy, xh, r, w_ref[...])
        dx_ref[...] = dxn
        first = jnp.logical_and(b == 0, i == 0)
        part = 0.5 * jnp.sum(jnp.sum(err * err, axis=1, keepdims=True), axis=0, keepdims=True) / D
        _acc_rows(loss_ref, first, {0: jnp.broadcast_to(part, (1, HP))})
        _acc_rows(dw_ref, first, {0: dw})

    return R.call("loss_head", body,
                  [(x, R.row(D)), (tgt, R.tgt(D)), (fnw, R.const((1, D)))],
                  [(_sds((B, T, D), F32), R.row(D)), (_sds((8, HP), F32), R.const((8, HP))), (_sds((8, D), F32), R.const((8, D)))],
                  extra=8 << 20)


def _tn_matmul(name, a, b, tn, sub=1):
    B, T, K = a.shape
    N = b.shape[2]
    nk = 1
    while T % nk or (T // nk) > 1088 or (T // nk) % 16:
        nk += 1
    tk = T // nk
    kt = K if K <= 1536 else 1024
    w = tn // sub
    assert K % kt == 0 and N % tn == 0 and tn % sub == 0

    def body(a_ref, b_ref, o_ref):
        first = jnp.logical_and(pl.program_id(2) == 0, pl.program_id(3) == 0)

        @pl.when(first)
        def _():
            o_ref[...] = jnp.zeros(o_ref.shape, F32)

        acc = _dot_tn(a_ref[...], b_ref[...])
        for s in range(sub):
            o_ref[s] += acc[:, w * s:w * (s + 1)]

    bb = _nbytes((tk, kt), a.dtype) + _nbytes((tk, tn), b.dtype) + _nbytes((kt, tn), F32)
    return pl.pallas_call(
        functools.partial(body), name=name, grid=(N // tn, K // kt, B, nk),
        in_specs=[pl.BlockSpec((None, tk, kt), lambda j, kk, bi, t: (bi, t, kk)),
                  pl.BlockSpec((None, tk, tn), lambda j, kk, bi, t: (bi, t, j))],
        out_specs=pl.BlockSpec((sub, kt, w), lambda j, kk, bi, t: (j, kk, 0)),
        out_shape=_out((sub * N // tn, K, w), F32),
        compiler_params=_params(("arbitrary",) * 4, bb, _nbytes((kt, tn), F32) + (8 << 20)),
    )(*_hbm([a, b]))


def _mlp_bwd(R, dxo, xmid, ubf, ybf, mod, nw2, w1, w2):
    B, T = R.B, R.T

    def body(dxo_ref, x_ref, u_ref, y_ref, mod_ref, nw_ref, w1_hbm, w2_hbm,
             dxm_ref, du_ref, a_ref, dy_ref, dmod_ref, dnw_ref, w1_v, w2_v, sem):
        b, i = pl.program_id(0), pl.program_id(1)
        _load_once(jnp.logical_and(b == 0, i == 0), [(w1_hbm, w1_v), (w2_hbm, w2_v)], sem)
        dxo = dxo_ref[...]
        _, xn, r = _norm_mod(x_ref[...], nw_ref[...], _row(mod_ref, 3), _row(mod_ref, 4))
        dyb = (dxo * _row(mod_ref, 5)).astype(BF)
        dy_ref[...] = dyb
        dg2 = jnp.sum(dxo * y_ref[...].astype(F32), axis=0, keepdims=True)
        dh = jnp.zeros((TM, D), F32)
        for j in range(NDEV):
            rl = jnp.maximum(u_ref[:, FSH * j:FSH * (j + 1)].astype(F32), 0.0)
            a_ref[:, FSH * j:FSH * (j + 1)] = (rl * rl).astype(BF)
            du = (_dot_nt(dyb, w2_v[j]) * (2.0 * rl)).astype(BF)
            du_ref[:, FSH * j:FSH * (j + 1)] = du
            dh = dh + _dot_nt(du, w1_v[j])
        dx, dsh, dsc, dnw = _norm_mod_bwd(dh, xn, r, nw_ref[...], _row(mod_ref, 4))
        dxm_ref[...] = dxo + dx
        _acc_rows(dmod_ref, R.first_of_stream(i), {3: dsh, 4: dsc, 5: dg2})
        _acc_rows(dnw_ref, jnp.logical_and(b == 0, i == 0), {0: dnw})

    return R.call("mlp_bwd", body,
                  [(dxo, R.row(D)), (xmid, R.row(D)), (ubf, R.row(FF)), (ybf, R.row(D)), (mod, R.mod()), (nw2, R.const((1, D))),
                   (w1, ANY), (w2, ANY)],
                  [(_sds((B, T, D), F32), R.row(D)), (_sds((B, T, FF), BF), R.row(FF)), (_sds((B, T, FF), BF), R.row(FF)),
                   (_sds((B, T, D), BF), R.row(D)), (_sds((B, 2, 8, D), F32), R.mod()), (_sds((8, D), F32), R.const((8, D)))],
                  scratch=[pltpu.VMEM((NDEV, D, FSH), w1.dtype), pltpu.VMEM((NDEV, FSH, D), w2.dtype), pltpu.SemaphoreType.DMA((2,))],
                  extra=(2 * _nbytes((NDEV, D, FSH), BF)) + (8 << 20))


def _minus_in_lanes(x, col, first):
    b16 = jnp.bfloat16
    hi = col.astype(b16).astype(F32)
    r1 = col - hi
    mid = r1.astype(b16).astype(F32)
    lane = _iota(x.shape, 1)
    return jnp.where(lane == first, -hi, jnp.where(lane == first + 1, -mid, jnp.where(lane == first + 2, mid - r1, x)))


def _out_bwd(R, dxm, mod, mixbf, yf, yb, xbc, z, o, qt, lse, dsk, snw, wout):
    B, T = R.B, R.T

    def body(dxm_ref, mod_ref, mix_ref, yf_ref, yb_ref, xbc_ref, z_ref, o_ref, q_ref, lse_ref, dsk_ref, snw_ref, w_ref,
             dmix_ref, dz_ref, dyt_ref, dxsk_ref, qs_ref, dos_ref, dotr_ref, dpo_ref, dmod_ref, dvec_ref):
        b, i = pl.program_id(0), pl.program_id(1)
        dxm = dxm_ref[...]
        dmixb = (dxm * _row(mod_ref, 2)).astype(BF)
        dmix_ref[...] = dmixb
        dg1 = jnp.sum(dxm * mix_ref[...].astype(F32), axis=0, keepdims=True)
        dcat = _dot_nt(dmixb, w_ref[...])
        do_v = dcat[:, SSD_IN:SSD_IN + QW]
        doo = do_v * o_ref[...]
        for h in range(H):
            cols = slice(HP * h, HP * (h + 1))
            dd = jnp.sum(doo[:, cols], axis=1, keepdims=True)
            dos_ref[:, cols] = _minus_in_lanes(do_v[:, cols], dd, VH + 1).astype(BF)
            lse_b = jnp.broadcast_to(lse_ref[h], (HP, TM)).T
            qs_ref[:, cols] = _minus_in_lanes(q_ref[:, cols].astype(F32), lse_b, QK).astype(BF)
            dotr_ref[h] = do_v[:, cols].T.astype(BF)
        dpo_ref[...] = dcat[:, SSD_IN + QW:MIXW]
        dsn = dcat[:, 0:SSD_IN]
        ytot, zv, gz, r = _ssd_gate(yf_ref, yb_ref, xbc_ref, z_ref, dsk_ref)
        gh = gz * r
        dsnw = jnp.sum(dsn * gh, axis=0, keepdims=True)
        dgh = dsn * snw_ref[...]
        g0 = _group_mask()
        pr = dgh * gh
        half = SSD_IN // 2
        m0 = jnp.sum(jnp.where(g0, pr, 0.0), axis=1, keepdims=True) / half
        m1 = jnp.sum(jnp.where(g0, 0.0, pr), axis=1, keepdims=True) / half
        dgz = r * (dgh - gh * jnp.where(g0, m0, m1))
        dyt = dgz * _silu(zv)
        dz_ref[...] = (dgz * ytot * _dsilu(zv)).astype(BF)
        dyt_ref[...] = dyt
        dxsk_ref[...] = dyt * dsk_ref[...]
        ddsk = jnp.sum(dyt * xbc_ref[:, 0:SSD_IN], axis=0, keepdims=True)
        _acc_rows(dmod_ref, R.first_of_stream(i), {2: dg1})
        _acc_rows(dvec_ref, jnp.logical_and(b == 0, i == 0), {0: dsnw, 1: ddsk})

    return R.call("out_bwd", body,
                  [(dxm, R.row(D)), (mod, R.mod()), (mixbf, R.row(D)), (yf, R.row(SSD_IN)), (yb, R.row(SSD_IN)), (xbc, R.row(XBC)),
                   (z, R.row(SSD_IN)),
                   (o, R.row(QW)), (qt, R.row(QW)), (lse, pl.BlockSpec((None, H, 1, TM), lambda b, i: (b, 0, 0, i))),
                   (dsk, R.const((1, SSD_IN))), (snw, R.const((1, SSD_IN))), (wout, R.const((MIXW, D)))],
                  [(_sds((B, T, D), BF), R.row(D)), (_sds((B, T, SSD_IN), BF), R.row(SSD_IN)), (_sds((B, T, SSD_IN), F32), R.row(SSD_IN)),
                   (_sds((B, T, SSD_IN), F32), R.row(SSD_IN)), (_sds((B, T, QW), BF), R.row(QW)), (_sds((B, T, QW), BF), R.row(QW)),
                   (_sds((B, H, HP, T), BF), pl.BlockSpec((None, H, HP, TM), lambda b, i: (b, 0, 0, i))),
                   (_sds((B, T, PD), F32), R.row(PD)),
                   (_sds((B, 2, 8, D), F32), R.mod()), (_sds((8, SSD_IN), F32), R.const((8, SSD_IN)))],
                  extra=8 << 20)


def _pool_bwd(R, dpo, u, wbd, scale):
    B, T = R.B, R.T

    def body(d_ref, dpv_ref, dnx_ref, u_ref, upv_ref, unx_ref, w_ref, sc_ref, du_ref, dw_ref, dsc_ref):
        b, i = pl.program_id(0), pl.program_id(1)
        ext_u = R.ext(i, upv_ref, u_ref[...], unx_ref)
        ext_d = R.ext(i, dpv_ref, d_ref[...], dnx_ref)
        dm, cnt, valid, lane = _pool_centred(R, i, ext_u)
        ddm = _dot_nt(ext_d * sc_ref[...], w_ref[...]) * valid
        e = ddm / cnt
        a2 = e + _shift_rows(e, 1)
        a4 = _shift_rows(a2, -1) + _shift_rows(a2, 1)
        a8 = _shift_rows(a4, -2) + _shift_rows(a4, 2)
        a16 = _shift_rows(a8, -4) + _shift_rows(a8, 4)
        du_ref[...] = (_lane_select(lane, a2, a4, a8, a16) - ddm)[8:8 + TM].astype(BF)
        dmc = dm[8:8 + TM]
        dpo_c = d_ref[...]
        first = jnp.logical_and(b == 0, i == 0)

        @pl.when(first)
        def _():
            dw_ref[...] = jnp.zeros(dw_ref.shape, F32)

        dw_ref[...] += _dot_tn(dmc, dpo_c * sc_ref[...])
        _acc_rows(dsc_ref, first, {0: jnp.sum(dpo_c * _dot(dmc, w_ref[...]), axis=0, keepdims=True)})

    return R.call("pool_bwd", body,
                  [(dpo, R.row(PD)), (dpo, R.prev8(PD)), (dpo, R.next8(PD)), (u, R.row(PD)), (u, R.prev8(PD)), (u, R.next8(PD)),
                   (wbd, R.const((PD, PD))), (scale, R.const((1, PD)))],
                  [(_sds((B, T, PD), BF), R.row(PD)), (_sds((PD, PD), F32), R.const((PD, PD))), (_sds((8, PD), F32), R.const((8, PD)))],
                  extra=8 << 20)


def _flash_bwd(B, T, qs, kt, dos, q_t, k_t, v_t, do_t):
    nk, nq = T // TK, T // TQ
    HS = 2

    def body(k_ref, kt_ref, vt_ref, q_ref, do_ref, qt_ref, dot_ref, dq_ref, dk_ref, dv_ref, s_scr, dp_scr, ds_scr):
        j = pl.program_id(2)
        ctx_keys = jnp.where(j < CTX // TK, 1.0, 0.0)
        for hh in range(HS):
            cols = slice(HP * hh, HP * (hh + 1))
            s_scr[hh] = _dot(q_ref[:, cols], kt_ref[hh])
            dp_scr[hh] = _dot(do_ref[:, cols], vt_ref[hh])
        for hh in range(HS):
            cols = slice(HP * hh, HP * (hh + 1))
            dk_tr = jnp.zeros((HP, TK), F32)
            dv_tr = jnp.zeros((HP, TK), F32)
            for i in range(nq):
                rows = slice(TQ * i, TQ * (i + 1))
                p = jnp.exp(s_scr[hh, rows, :])
                if i < CTX // TQ:
                    p = p * ctx_keys
                ds = (p * dp_scr[hh, rows, :]).astype(BF)
                ds_scr[hh, rows, :] = ds
                dv_tr = dv_tr + _dot(dot_ref[hh, :, rows], p)
                dk_tr = dk_tr + _dot(qt_ref[hh, :, rows], ds)
            dq = _dot(ds_scr[hh], k_ref[:, cols])

            @pl.when(j == 0)
            def _():
                dq_ref[:, cols] = dq

            @pl.when(j > 0)
            def _():
                dq_ref[:, cols] += dq

            dk_ref[:, cols] = dk_tr.T
            dv_ref[:, cols] = dv_tr.T

    tspec = pl.BlockSpec((None, TK, HS * HP), lambda b, h, j: (b, j, h))
    fspec = pl.BlockSpec((None, T, HS * HP), lambda b, h, j: (b, 0, h))
    ttspec = pl.BlockSpec((None, HS, HP, TK), lambda b, h, j: (b, h, 0, j))
    ftspec = pl.BlockSpec((None, HS, HP, T), lambda b, h, j: (b, h, 0, 0))
    bb = HS * (4 * _nbytes((T, HP), BF) + _nbytes((T, HP), F32) + 8 * _nbytes((TK, HP), F32))
    scr = HS * (2 * _nbytes((T, TK), F32) + _nbytes((T, TK), BF))
    return pl.pallas_call(
        functools.partial(body), name="flash_bwd", grid=(B, H // HS, nk),
        in_specs=[tspec, ttspec, ttspec, fspec, fspec, ftspec, ftspec],
        out_specs=[fspec, tspec, tspec],
        out_shape=[_out((B, T, QW), F32)] * 3,
        scratch_shapes=[pltpu.VMEM((HS, T, TK), F32), pltpu.VMEM((HS, T, TK), F32), pltpu.VMEM((HS, T, TK), BF)],
        compiler_params=_params(("arbitrary",) * 3, bb, scr + (8 << 20)),
    )(*_hbm([kt, k_t, v_t, qs, dos, q_t, do_t]))


def _mla_bwd(R, dqt, dkt, dvt, qa, kva, qnw, kvnw, wq, wk, wv, cos, sin):
    B, T = R.B, R.T
    scale = QK ** -0.5

    def body(dq_ref, dk_ref, dv_ref, qa_ref, kva_ref, qnw_ref, kvnw_ref, wq_ref, wk_ref, wv_ref, cos_ref, sin_ref,
             dqp_ref, dkvb_ref, dqa_ref, dkva_ref, dkr_ref, dnw_ref):
        b, i = pl.program_id(0), pl.program_id(1)
        cos1, sin1 = cos_ref[...], sin_ref[...]
        dq = dq_ref[...] * scale
        dqp = _per_head(lambda g: g * cos1 + _swap8(g * sin1), dq).astype(BF)
        dqp_ref[...] = dqp
        dkv = dk_ref[...]
        dkb = dkv.astype(BF)
        dvb = dv_ref[...].astype(BF)
        dkvb_ref[:, :QW] = dkb
        dkvb_ref[:, QW:] = dvb
        dkk = dkv[:, 0:HP]
        for h in range(1, H):
            dkk = dkk + dkv[:, HP * h:HP * (h + 1)]
        lane = _iota((TM, HP), 1)
        rope_lane = jnp.logical_and(lane >= NOPE, lane < NOPE + ROPE)
        dkr_ref[...] = jnp.where(rope_lane, dkk * cos1 + _swap8(dkk * sin1), 0.0).astype(BF)
        _, qh, qr = _rms(qa_ref[...], qnw_ref[...])
        _, kh, kr_ = _rms(kva_ref[...], kvnw_ref[...])
        dcq = _dot_nt(dqp, wq_ref[...])
        dckv = _dot_nt(dkb, wk_ref[...]) + _dot_nt(dvb, wv_ref[...])
        dqa, dqw = _rms_bwd(dcq, qh, qr, qnw_ref[...])
        dkva, dkw = _rms_bwd(dckv, kh, kr_, kvnw_ref[...])
        dqa_ref[...] = dqa.astype(BF)
        dkva_ref[...] = dkva.astype(BF)
        _acc_rows(dnw_ref, jnp.logical_and(b == 0, i == 0), {0: dqw, 1: dkw})

    tab = pl.BlockSpec((TM, HP), lambda b, i: (i, 0))
    return R.call("mla_bwd", body,
                  [(dqt, R.row(QW)), (dkt, R.row(QW)), (dvt, R.row(QW)), (qa, R.row(QL)), (kva, R.row(QL)),
                   (qnw, R.const((1, QL))), (kvnw, R.const((1, QL))), (wq, R.const((QL, QW))), (wk, R.const((QL, QW))),
                   (wv, R.const((QL, QW))), (cos, tab), (sin, tab)],
                  [(_sds((B, T, QW), BF), R.row(QW)), (_sds((B, T, 2 * QW), BF), R.row(2 * QW))]
                  + [(_sds((B, T, QL), BF), R.row(QL))] * 2
                  + [(_sds((B, T, HP), BF), R.row(HP)), (_sds((8, QL), F32), R.const((8, QL)))], extra=8 << 20)


def _ssd_scan_bwd(B, T, dyt, xbc, dt2, alog2, hin_f, hin_b):
    nc, nctc = T // CH, CTX // CH

    def chain(d, dy_ref, xbc_ref, dt_ref, alog_ref, hin_ref, dxbc_ref, ddt_ref, da_ref, dhs_all):
        dhs = dhs_all.at[d]
        xbc_v = xbc_ref[...]
        dyv = dy_ref[...]
        a_all, dt_all, adt_all, q_all, inc, inc_t = _scan_common(d, dt_ref[...], alog_ref)
        ones_p = jnp.ones((P, HP), F32)
        total = lambda m: jnp.sum(jnp.sum(m, axis=0, keepdims=True), axis=1, keepdims=True)
        dq_parts, dqtot_parts, ddtx_parts = [], [], []
        for g in range(2):
            bg = xbc_v[:, SSD_IN + NST * g:SSD_IN + NST * (g + 1)]
            cg = xbc_v[:, SSD_IN + 2 * NST + NST * g:SSD_IN + 2 * NST + NST * (g + 1)]
            bg_t, cg_t = bg.T, cg.T
            gm = _dot(cg, bg_t)
            gm_t = _dot(bg, cg_t)
            dgm = jnp.zeros((CH, CH), F32)
            dgm_t = jnp.zeros((CH, CH), F32)
            dbg = jnp.zeros((CH, NST), F32)
            dcg = jnp.zeros((CH, NST), F32)
            for r in range(3):
                h = 3 * g + r
                lm, lm_t, eq, etot, dte = _head_decay(h, adt_all, q_all, inc, inc_t)
                eq_p, etot_p, dte_p = eq[:, :P], etot[:, :P], dte[:, :P]
                dt_p = dt_all[:, HP * h:HP * h + P]
                xs_h = xbc_v[:, P * h:P * (h + 1)]
                xh = xs_h * dt_p
                sm, sm_t = gm * lm, gm_t * lm_t
                dy_h = dyv[:, P * h:P * (h + 1)]
                hprev = hin_ref[P * h:P * (h + 1), :].T
                dho = dhs[:, P * h:P * (h + 1)]
                ds = _dot_nt(dy_h, xh)
                ds_t = _dot_nt(xh, dy_h)
                dx = _dot(sm_t, dy_h)
                edy = eq_p * dy_h
                yo = _dot(cg, hprev)
                dcg = dcg + _dot_nt(edy, hprev)
                dhin = _dot(cg_t, edy) + etot_p * dho
                zs = _dot(bg, dho)
                dx = dx + dte_p * zs
                wm = dte_p * xh * zs
                dbg = dbg + _dot_nt(xh * dte_p, dho)
                dgm = dgm + ds * lm
                dgm_t = dgm_t + ds_t * lm_t
                rs = jnp.sum(ds * sm - ds_t * sm_t, axis=1, keepdims=True) + jnp.sum(edy * yo - wm, axis=1, keepdims=True)
                dq_parts.append(jnp.broadcast_to(rs, (CH, HP)))
                dqtot_parts.append(total(hprev * dho) * etot + total(wm))
                dxbc_ref[:, P * h:P * (h + 1)] = dx * dt_p
                ddtx_parts.append(_dot01_r(dx * xs_h, ones_p, passes=2))
                dhs[:, P * h:P * (h + 1)] = dhin
            dcg = dcg + _dot(dgm, bg)
            dbg = dbg + _dot(dgm_t, cg)
            dxbc_ref[:, SSD_IN + NST * g:SSD_IN + NST * (g + 1)] = dbg
            dxbc_ref[:, SSD_IN + 2 * NST + NST * g:SSD_IN + 2 * NST + NST * (g + 1)] = dcg
        cat = lambda parts: jnp.concatenate(parts, axis=1)
        dadt_all = _dot01(inc_t, cat(dq_parts)) + cat(dqtot_parts)
        ddt_all = cat(ddtx_parts) + dadt_all * jnp.concatenate([a_all] * (CH // 8), axis=0)
        da_all = jnp.sum(dadt_all * dt_all, axis=0, keepdims=True)
        lane, lane1 = _iota((CH, HP), 1), _iota((1, HP), 1)
        ddt = jnp.zeros((CH, HP), F32)
        da = jnp.zeros((1, HP), F32)
        for h in range(H):
            ddt = ddt + jnp.where(lane == h, ddt_all[:, HP * h:HP * (h + 1)], 0.0)
            da = da + jnp.where(lane1 == h, da_all[:, HP * h:HP * (h + 1)], 0.0)
        ddt_ref[...] = ddt
        da_ref[d] += jnp.where(_iota((8, HP), 0) == 0, da, 0.0)

    def body(dyf_ref, dyb_ref, xf_ref, xb_ref, dtf_ref, dtb_ref, alog_ref, hf_ref, hb_ref,
             dxf_ref, dxb_ref, ddtf_ref, ddtb_ref, da_ref, dhs):
        @pl.when(pl.program_id(1) == 0)
        def _():
            dhs[...] = jnp.zeros(dhs.shape, F32)
            da_ref[...] = jnp.zeros(da_ref.shape, F32)

        chain(0, dyf_ref, xf_ref, dtf_ref, alog_ref, hf_ref, dxf_ref, ddtf_ref, da_ref, dhs)
        chain(1, dyb_ref, xb_ref, dtb_ref, alog_ref, hb_ref, dxb_ref, ddtb_ref, da_ref, dhs)

    cidx = lambda d: (lambda s: _chunk_index(d, nc - 1 - s, nc, nctc))
    specs = lambda d: dict(
        dy=pl.BlockSpec((None, CH, SSD_IN), lambda b, s: (b, cidx(d)(s), 0)),
        xbc=pl.BlockSpec((None, CH, XBC), lambda b, s: (b, cidx(d)(s), 0)),
        dt=pl.BlockSpec((None, None, CH, HP), lambda b, s: (b, d, cidx(d)(s), 0)),
        h=pl.BlockSpec((None, None, SSD_IN, NST), lambda b, s: (b, cidx(d)(s), 0, 0)),
        ddt=pl.BlockSpec((None, CH, HP), lambda b, s: (b, cidx(d)(s), 0)))
    f, r = specs(0), specs(1)
    bb = 2 * (2 * _nbytes((CH, XBC), F32) + 2 * _nbytes((CH, HP), F32) + _nbytes((CH, SSD_IN), F32) + _nbytes((SSD_IN, NST), F32))
    return pl.pallas_call(
        functools.partial(body), name="ssd_scan_bwd", grid=(B, nc),
        in_specs=[f["dy"], r["dy"], f["xbc"], r["xbc"], f["dt"], r["dt"], pl.BlockSpec((2, 8, HP), lambda b, s: (0, 0, 0)),
                  f["h"], r["h"]],
        out_specs=[f["xbc"], r["xbc"], f["ddt"], r["ddt"], pl.BlockSpec((None, 2, 8, HP), lambda b, s: (b, 0, 0, 0))],
        out_shape=[_out((B, T, XBC), F32)] * 2 + [_out((B, T, HP), F32)] * 2 + [_out((B, 2, 8, HP), F32)],
        scratch_shapes=[pltpu.VMEM((2, NST, SSD_IN), F32)],
        compiler_params=_params(("arbitrary",) * 2, bb, 16 << 20),
    )(*_hbm([dyt, dyt, xbc, xbc, dt2, dt2, alog2, hin_f, hin_b]))


def _ssd_prep_bwd(R, dxbc_f, dxbc_b, dxsk, ddt_f, ddt_b, xbc_raw, dt_raw, conv_w8, conv_b, dtb):
    B, T = R.B, R.T

    def body(dxf_ref, dxb_ref, dsk_ref, ddtf_ref, ddtb2_ref, raw_ref, pv_ref, nx_ref, dtr_ref, w_ref, b_ref, dtb_ref,
             dpre_ref, ddtr_ref, dcw_ref, dvec_ref, ddtb_ref):
        b, i = pl.program_id(0), pl.program_id(1)
        ext = R.ext(i, pv_ref, raw_ref[...], nx_ref)
        pre = _conv_pre(ext, w_ref, b_ref)
        dxbc = dxf_ref[...] + dxb_ref[...]
        skip = jnp.concatenate([dsk_ref[...], jnp.zeros((TM, XBC - SSD_IN), F32)], axis=1)
        dpre = (dxbc + skip) * _dsilu(pre)
        dpre_ref[...] = dpre
        first = jnp.logical_and(b == 0, i == 0)
        taps = {k: jnp.sum(dpre * _shift_rows(ext, k - 1)[8:8 + TM], axis=0, keepdims=True) for k in range(4)}
        _acc_rows(dcw_ref, first, taps)
        _acc_rows(dvec_ref, first, {0: jnp.sum(dpre, axis=0, keepdims=True)})
        ddt = ddtf_ref[...] + pltpu.roll(ddtb2_ref[...], H, axis=1)
        ddtr = ddt * _sigmoid(dtr_ref[...] + dtb_ref[...])
        ddtr = jnp.where(_iota((TM, HP), 1) < 2 * H, ddtr, 0.0)
        ddtr_ref[...] = ddtr.astype(BF)
        _acc_rows(ddtb_ref, first, {0: jnp.sum(ddtr, axis=0, keepdims=True)})

    return R.call("ssd_prep_bwd", body,
                  [(dxbc_f, R.row(XBC)), (dxbc_b, R.row(XBC)), (dxsk, R.row(SSD_IN)), (ddt_f, R.row(HP)), (ddt_b, R.row(HP)),
                   (xbc_raw, R.row(XBC)), (xbc_raw, R.prev8(XBC)),
                   (xbc_raw, R.next8(XBC)), (dt_raw, R.row(HP)), (conv_w8, R.const((8, XBC))), (conv_b, R.const((1, XBC))),
                   (dtb, R.const((1, HP)))],
                  [(_sds((B, T, XBC), F32), R.row(XBC)), (_sds((B, T, HP), BF), R.row(HP)), (_sds((8, XBC), F32), R.const((8, XBC))),
                   (_sds((8, XBC), F32), R.const((8, XBC))), (_sds((8, HP), F32), R.const((8, HP)))], extra=16 << 20)


def _in_bwd(R, dxm, x, mod, nw1, dz, dpre, dqa, dkva, dpool, dkr, ddtr, conv_w8, w_arr, latent_only=False):
    B, T = R.B, R.T
    dx_out = (_sds((B, T - CTX, D), F32), R.tgt(D)) if latent_only else (_sds((B, T, D), F32), R.row(D))

    def body(dxm_ref, x_ref, mod_ref, nw_ref, dz_ref, dp_ref, dpp_ref, dpn_ref, dqa_ref, dkva_ref, dpool_ref, dkr_ref, ddt_ref,
             cw_ref, w_ref, dx_ref, dproj_ref, dmod_ref, dnw_ref):
        b, i = pl.program_id(0), pl.program_id(1)
        ext = R.ext(i, dpp_ref, dp_ref[...], dpn_ref)
        draw = (_row(cw_ref, 0) * _shift_rows(ext, 1)[8:8 + TM] + _row(cw_ref, 1) * ext[8:8 + TM]
                + _row(cw_ref, 2) * _shift_rows(ext, -1)[8:8 + TM] + _row(cw_ref, 3) * _shift_rows(ext, -2)[8:8 + TM])
        dproj_ref[:, A_Z:A_XBC] = dz_ref[...]
        dproj_ref[:, A_XBC:A_QA] = draw.astype(BF)
        dproj_ref[:, A_QA:A_KVA] = dqa_ref[...]
        dproj_ref[:, A_KVA:A_POOL] = dkva_ref[...]
        dproj_ref[:, A_POOL:A_KR] = dpool_ref[...]
        dproj_ref[:, A_KR:A_DT] = dkr_ref[...]
        dproj_ref[:, A_DT:PC] = ddt_ref[...]
        dh = _dot_nt(dproj_ref[...], w_ref[...])
        _, xn, r = _norm_mod(x_ref[...], nw_ref[...], _row(mod_ref, 0), _row(mod_ref, 1))
        dx, dsh, dsc, dnw = _norm_mod_bwd(dh, xn, r, nw_ref[...], _row(mod_ref, 1))
        dx_ref[...] = dxm_ref[...] + dx
        _acc_rows(dmod_ref, R.first_of_stream(i), {0: dsh, 1: dsc})
        _acc_rows(dnw_ref, jnp.logical_and(b == 0, i == 0), {0: dnw})

    return R.call("in_bwd", body,
                  [(dxm, R.row(D)), (x, R.row(D)), (mod, R.mod()), (nw1, R.const((1, D))), (dz, R.row(SSD_IN)), (dpre, R.row(XBC)),
                   (dpre, R.prev8(XBC)), (dpre, R.next8(XBC)), (dqa, R.row(QL)), (dkva, R.row(QL)), (dpool, R.row(PD)), (dkr, R.row(HP)),
                   (ddtr, R.row(HP)), (conv_w8, R.const((8, XBC))), (w_arr, R.const((D, PC)))],
                  [dx_out, (_sds((B, T, PC), BF), R.row(PC)), (_sds((B, 2, 8, D), F32), R.mod()),
                   (_sds((8, D), F32), R.const((8, D)))], extra=12 << 20)


def _adaln_fwd(cs, mod_w):
    L, _, C = mod_w.shape

    def body(c_ref, w_ref, o_ref):
        s = _silu(c_ref[...]).astype(BF)
        for l in range(L):
            o_ref[l] = jnp.dot(s, w_ref[l].astype(BF), preferred_element_type=F32)

    return pl.pallas_call(functools.partial(body), name="adaln_fwd", out_shape=_sds((L, 24, C), F32),
                          compiler_params=_params(None, _nbytes(mod_w.shape, F32) + _nbytes((L, 24, C), F32), 8 << 20))(cs, mod_w)


def _adaln_bwd(cs, dm, mod_w):
    L, _, C = mod_w.shape

    def body(c_ref, dm_ref, w_ref, gw_ref, gc_ref):
        c = c_ref[...]
        s = _silu(c).astype(BF)
        acc = jnp.zeros((24, D), F32)
        for l in range(L):
            dmb = dm_ref[l].astype(BF)
            gw_ref[l] = _dot_tn(s, dmb)
            acc = acc + _dot_nt(dmb, w_ref[l])
        gc_ref[...] = acc * _dsilu(c)

    return pl.pallas_call(functools.partial(body), name="adaln_bwd", out_shape=[_sds((L, D, C), F32), _sds((24, D), F32)],
                          compiler_params=_params(None, 2 * _nbytes(mod_w.shape, F32), 8 << 20))(cs, dm, mod_w)


def _sum_blocks(name, parts):
    Pn, Rr, C = parts.shape

    def body(p_ref, o_ref):
        acc = p_ref[0]
        for k in range(1, Pn):
            acc = acc + p_ref[k]
        o_ref[...] = acc

    return pl.pallas_call(functools.partial(body), name=name, out_shape=_sds((Rr, C), F32),
                          compiler_params=_params(None, _nbytes(parts.shape, F32), 4 << 20))(parts)


def _adamw(name, parts, w, m, v, rt):
    nseg = len(parts)
    Pn, rseg, C = parts[0].shape
    Rr = nseg * rseg
    tiles = rseg // rt
    assert rseg % rt == 0 and w.shape == (Rr, C)
    c1 = 1.0 - ADAM_B1 ** ADAM_STEP
    c2 = 1.0 - ADAM_B2 ** ADAM_STEP

    def body(*refs):
        p_refs = refs[:nseg]
        w_ref, m_ref, v_ref, g_ref, d_ref, nm_ref, nv_ref = refs[nseg:]
        i = pl.program_id(0)
        g = None
        for s, p_ref in enumerate(p_refs):
            gs = p_ref[0].astype(F32)
            for k in range(1, Pn):
                gs = gs + p_ref[k].astype(F32)
            g = gs if g is None else jnp.where(i >= s * tiles, gs, g)
        mn = ADAM_B1 * m_ref[...] + (1.0 - ADAM_B1) * g
        vn = ADAM_B2 * v_ref[...] + (1.0 - ADAM_B2) * jnp.square(g)
        g_ref[...] = g
        nm_ref[...] = mn
        nv_ref[...] = vn
        d_ref[...] = -ADAM_LR * ((mn / c1) / (jnp.sqrt(vn / c2) + ADAM_EPS) + ADAM_WD * w_ref[...])

    spec = pl.BlockSpec((rt, C), lambda i: (i, 0))
    pspec = lambda s: pl.BlockSpec((Pn, rt, C), lambda i: (0, jnp.clip(i - s * tiles, 0, tiles - 1), 0))
    bb = nseg * Pn * _nbytes((rt, C), parts[0].dtype) + 7 * _nbytes((rt, C), F32)
    return pl.pallas_call(
        functools.partial(body), name=name, grid=(Rr // rt,),
        in_specs=[pspec(s) for s in range(nseg)] + [spec, spec, spec],
        out_specs=[spec] * 4, out_shape=[_out((Rr, C), F32)] * 4,
        compiler_params=_params(("arbitrary",), bb, 4 << 20),
    )(*_hbm(list(parts) + [w, m, v]))


MESH = pl.DeviceIdType.MESH


def _my_pos():
    return lax.axis_index("x"), lax.axis_index("y"), lax.axis_index("c")


def _dev_index(x, y, c):
    return 4 * x + 2 * y + c


def _all_gather(name, shards):
    n = len(shards)

    def body(*refs):
        ins, outs = refs[:n], refs[n:2 * n]
        send_sems, recv_sems, local_sem = refs[2 * n:]
        x, y, c = _my_pos()
        me, sibling = (x, y, c), (x, y, 1 - c)
        chips = [(1 - x, y), (x, 1 - y), (1 - x, 1 - y)]

        def copy(t, k, block, to, src=None):
            slot = outs[t].at[_dev_index(*block)]
            return pltpu.make_async_remote_copy(
                src_ref=slot if src is None else src, dst_ref=slot,
                send_sem=send_sems.at[t, k], recv_sem=recv_sems.at[t, k], device_id=to, device_id_type=MESH)

        mine = [pltpu.make_async_copy(ins[t], outs[t].at[_dev_index(*me)], local_sem.at[t]) for t in range(n)]
        for cp in mine:
            cp.start()
        first = []
        for t in range(n):
            first.append(copy(t, 0, me, sibling, src=ins[t]))
            first += [copy(t, 1 + j, me, (*chip, c), src=ins[t]) for j, chip in enumerate(chips)]
        for cp in first:
            cp.start()
        passed = []
        for j, chip in enumerate(chips):
            for t in range(n):
                copy(t, 1 + j, (*chip, c), me).wait_recv()
                cp = copy(t, 4 + j, (*chip, c), sibling)
                cp.start()
                passed.append(cp)
        for t in range(n):
            copy(t, 0, sibling, me).wait_recv()
            for j, chip in enumerate(chips):
                copy(t, 4 + j, (*chip, 1 - c), me).wait_recv()
        for cp in first + passed:
            cp.wait_send()
        for cp in mine:
            cp.wait()

    return pl.pallas_call(
        functools.partial(body), name=name,
        in_specs=[ANY] * n, out_specs=[ANY] * n,
        out_shape=[_sds((NDEV,) + s.shape, s.dtype) for s in shards],
        scratch_shapes=[pltpu.SemaphoreType.DMA((n, 7)), pltpu.SemaphoreType.DMA((n, 7)), pltpu.SemaphoreType.DMA((n,))],
    )(*shards)


def _all_to_all(name, parts):
    n = len(parts)

    def body(*refs):
        ins, outs = refs[:n], refs[n:2 * n]
        send_sems, recv_sems, local_sem = refs[2 * n:]
        x, y, c = _my_pos()
        me = _dev_index(x, y, c)
        peers = [(x ^ ((k >> 2) & 1), y ^ ((k >> 1) & 1), c ^ (k & 1)) for k in range(1, NDEV)]
        mine = [pltpu.make_async_copy(ins[t].at[me], outs[t].at[me], local_sem.at[t]) for t in range(n)]
        for cp in mine:
            cp.start()
        sends = []
        for t in range(n):
            for k, peer in enumerate(peers):
                cp = pltpu.make_async_remote_copy(
                    src_ref=ins[t].at[_dev_index(*peer)], dst_ref=outs[t].at[me],
                    send_sem=send_sems.at[t, k], recv_sem=recv_sems.at[t, k], device_id=peer, device_id_type=MESH)
                cp.start()
                sends.append(cp)
        for t in range(n):
            for k, peer in enumerate(peers):
                slot = outs[t].at[_dev_index(*peer)]
                pltpu.make_async_remote_copy(
                    src_ref=slot, dst_ref=slot, send_sem=send_sems.at[t, k], recv_sem=recv_sems.at[t, k],
                    device_id=peer, device_id_type=MESH).wait_recv()
        for cp in sends:
            cp.wait_send()
        for cp in mine:
            cp.wait()

    return pl.pallas_call(
        functools.partial(body), name=name,
        in_specs=[ANY] * n, out_specs=[ANY] * n,
        out_shape=[_sds(p.shape, p.dtype) for p in parts],
        scratch_shapes=[pltpu.SemaphoreType.DMA((n, 7)), pltpu.SemaphoreType.DMA((n, 7)), pltpu.SemaphoreType.DMA((n,))],
    )(*parts)


SEM = pl.BlockSpec(memory_space=pltpu.SEMAPHORE)
IN_HBM = pl.BlockSpec(memory_space=pltpu.HBM)
DATAFLOW = pltpu.SideEffectType.DATAFLOW_SIDE_EFFECTING


def _flip_peers(x, y, c):
    return [(x ^ ((k >> 2) & 1), y ^ ((k >> 1) & 1), c ^ (k & 1)) for k in range(1, NDEV)]


def _split_copies(srcs, lands, send_sems, recv_sems, gather):
    x, y, c = _my_pos()
    me = _dev_index(x, y, c)
    out = []
    for t in range(len(srcs)):
        for k, peer in enumerate(_flip_peers(x, y, c)):
            p = _dev_index(*peer)
            src = srcs[t] if gather else srcs[t].at[p]
            sems = dict(send_sem=send_sems.at[7 * t + k], recv_sem=recv_sems.at[7 * t + k], device_id=peer, device_id_type=MESH)
            out.append((pltpu.make_async_remote_copy(src_ref=src, dst_ref=lands[t].at[me], **sems),
                        pltpu.make_async_remote_copy(src_ref=src, dst_ref=lands[t].at[p], **sems)))
    return out


def _exchange_start(name, collective_id, srcs, gather):
    n = len(srcs)
    lands = [lax.empty(((NDEV,) + s.shape) if gather else s.shape, s.dtype) for s in srcs]

    def body(*refs):
        src_refs, land_refs = refs[:n], refs[n:2 * n]
        send_sems, recv_sems = refs[2 * n], refs[2 * n + 1]
        token = refs[-1]
        barrier = pltpu.get_barrier_semaphore()
        for peer in _flip_peers(*_my_pos()):
            pl.semaphore_signal(barrier, inc=1, device_id=peer, device_id_type=MESH)
        pl.semaphore_wait(barrier, NDEV - 1)
        for send, _ in _split_copies(src_refs, land_refs, send_sems, recv_sems, gather):
            send.start()
        token[...] = jnp.zeros(token.shape, token.dtype)

    hbm = lambda a: pltpu.HBM(a.shape, a.dtype)
    res = pl.pallas_call(
        functools.partial(body), name=name,
        out_shape=[pltpu.SemaphoreType.DMA((7 * n,)), pltpu.SemaphoreType.DMA((7 * n,))] + [hbm(s) for s in srcs]
        + [hbm(a) for a in lands] + [_sds((8, HP), F32)],
        in_specs=[IN_HBM] * (2 * n), out_specs=[SEM, SEM] + [IN_HBM] * (2 * n) + [pl.BlockSpec(memory_space=pltpu.VMEM)],
        input_output_aliases={i: 2 + i for i in range(2 * n)},
        compiler_params=pltpu.CompilerParams(has_side_effects=DATAFLOW, collective_id=collective_id),
    )(*_hbm(list(srcs) + lands))
    return res[0], res[1], list(res[2:2 + n]), list(res[2 + n:2 + 2 * n]), res[-1]


def _exchange_wait(name, send_sems, recv_sems, srcs, lands, after, gather):
    n = len(srcs)

    def body(*refs):
        src_refs, land_refs = refs[:n], refs[n:2 * n]
        for _, recv in _split_copies(src_refs, land_refs, refs[2 * n], refs[2 * n + 1], gather):
            recv.wait_send()
            recv.wait_recv()

    hbm = lambda a: pltpu.HBM(a.shape, a.dtype)
    res = pl.pallas_call(
        functools.partial(body), name=name,
        out_shape=[hbm(s) for s in srcs] + [hbm(a) for a in lands],
        in_specs=[IN_HBM] * (2 * n) + [SEM, SEM, ANY], out_specs=[IN_HBM] * (2 * n),
        input_output_aliases={i: i for i in range(2 * n)},
        compiler_params=pltpu.CompilerParams(has_side_effects=DATAFLOW),
    )(*srcs, *lands, send_sems, recv_sems, after)
    return list(res[:n]), list(res[n:])


def _with_own(lands, own_blocks, me):
    out = []
    for land, own in zip(lands, own_blocks):
        out.append(lax.dynamic_update_slice(land, own[None], (me,) + (0,) * own.ndim))
    return out


def _arrange_w_in(w):
    z = lambda n: jnp.zeros((w.shape[0], n), w.dtype)
    return jnp.concatenate([w[:, 0:1280], w[:, 1292:1548], w[:, 1548:1804], w[:, 1836:2092],
                            z(64), w[:, 1804:1836], z(32), w[:, 1280:1292], z(HP - 2 * H)], axis=1)


def _unarrange_w_in(g):
    return jnp.concatenate([g[:, 0:1280], g[:, A_DT:A_DT + 2 * H], g[:, A_QA:A_KVA], g[:, A_KVA:A_POOL],
                            g[:, A_KR + NOPE:A_KR + NOPE + ROPE], g[:, A_POOL:A_KR]], axis=1)


def _pad_heads(w, width):
    k = w.shape[0]
    return jnp.pad(w.reshape(k, H, width), ((0, 0), (0, 0), (0, HP - width))).reshape(k, H * HP)


def _unpad_heads(g, width):
    k = g.shape[0]
    return g.reshape(k, H, HP)[:, :, :width].reshape(k, H * width)


def _arrange_w_out(w):
    att = jnp.pad(w[SSD_IN:2 * SSD_IN].reshape(H, VH, D), ((0, 0), (0, HP - VH), (0, 0))).reshape(QW, D)
    return jnp.concatenate([w[0:SSD_IN], att, w[2 * SSD_IN:]], axis=0)


def _unarrange_w_out(g):
    att = g[SSD_IN:SSD_IN + QW].reshape(H, HP, D)[:, :VH].reshape(SSD_IN, D)
    return jnp.concatenate([g[0:SSD_IN], att, g[SSD_IN + QW:]], axis=0)


def _rope_tables(T):
    n = T - CTX
    rows = n // GRID_W
    pairs = ROPE // 4
    inv = ROPE_THETA ** (-jnp.arange(pairs, dtype=F32) / pairs)
    ar = jnp.arange(rows, dtype=F32)[:, None] * inv
    ac = jnp.arange(GRID_W, dtype=F32)[:, None] * inv
    by_row = lambda a: jnp.repeat(a, GRID_W, axis=0)
    by_col = lambda a: jnp.tile(a, (rows, 1))
    cos = jnp.concatenate([by_row(jnp.cos(ar))] * 2 + [by_col(jnp.cos(ac))] * 2, axis=1)
    sin = jnp.concatenate([-by_row(jnp.sin(ar)), by_row(jnp.sin(ar)), -by_col(jnp.sin(ac)), by_col(jnp.sin(ac))], axis=1)
    ones, zeros = jnp.ones((n, NOPE), F32), jnp.zeros((n, NOPE), F32)
    cos = jnp.concatenate([ones, cos, ones[:, :HP - QK]], axis=1)
    sin = jnp.concatenate([zeros, sin, zeros[:, :HP - QK]], axis=1)
    return (jnp.concatenate([jnp.ones((CTX, HP), F32), cos], axis=0),
            jnp.concatenate([jnp.zeros((CTX, HP), F32), sin], axis=0))


def _lane_pad(v, n):
    return jnp.pad(v, (0, n - v.shape[0]))[None, :]


def _layer_weights(w_in, w_q_b, w_kv_b, conv_w, pool_w):
    kv = w_kv_b.reshape(QL, H, NOPE + VH)
    wbd = jnp.concatenate([jnp.pad(pool_w[g], ((0, 0), (64 * g, PD - 64 * (g + 1)))) for g in range(4)], axis=0)
    return dict(
        w_in=_arrange_w_in(w_in).astype(BF),
        wq=_pad_heads(w_q_b, QK).astype(BF),
        wk=_pad_heads(kv[:, :, :NOPE].reshape(QL, H * NOPE), NOPE).astype(BF),
        wv=_pad_heads(kv[:, :, NOPE:].reshape(QL, H * VH), VH).astype(BF),
        conv_w8=jnp.pad(conv_w, ((0, 4), (0, 0))), wbd=wbd)


def _layer_fwd(R, x, mod, lw, sp, late, cos, sin):
    B, T = R.B, R.T
    z, xbc_raw, qa, kva, pool_in, kr, dt_raw, h1 = _in_proj(R, x, mod, sp["nw1"], lw["w_in"])
    xbc, dt2 = _ssd_prep(R, xbc_raw, dt_raw, lw["conv_w8"], sp["conv_b"], sp["dtb"])
    yf, yb, hin_f, hin_b = _ssd_scan(B, T, xbc, dt2, sp["alog2"])
    qt, kt, q_t, k_t, v_t, cq, ckv = _mla_prep(R, qa, kva, kr, sp["qnw"], sp["kvnw"], lw["wq"], lw["wk"], lw["wv"], cos, sin)
    o, lse = _flash_fwd(B, T, q_t, kt, v_t)
    pool = _pool_fwd(R, pool_in, lw["wbd"], sp["pscale"])[0]
    w_out, w1, w2 = late(o)
    w_out = _arrange_w_out(w_out).astype(BF)
    xmid, cat, mixbf = _out_proj(R, x, mod, yf, yb, xbc, z, o, pool, sp["dsk"], sp["snw"], w_out)
    xo, h2, ubf, ybf = _mlp_fwd(R, xmid, mod, sp["nw2"], w1, w2)
    saved = dict(x=x, z=z, xbc_raw=xbc_raw, qa=qa, kva=kva, pool_in=pool_in, dt_raw=dt_raw, h1=h1, xbc=xbc, dt2=dt2,
                 yf=yf, yb=yb, hin_f=hin_f, hin_b=hin_b, qt=qt, kt=kt, q_t=q_t, k_t=k_t, v_t=v_t, cq=cq, ckv=ckv, o=o, lse=lse,
                 cat=cat, mixbf=mixbf, xmid=xmid, h2=h2, ubf=ubf, ybf=ybf, w_out=w_out, w1=w1, w2=w2)
    return xo, saved


def _layer_bwd(R, dxo, sv, mod, lw, sp, cos, sin, on_mlp=None, on_mid=None, latent_only=False):
    B, T = R.B, R.T
    dxm, du, abf, dyb, dmod_a, dnw2 = _mlp_bwd(R, dxo, sv["xmid"], sv["ubf"], sv["ybf"], mod, sp["nw2"], sv["w1"], sv["w2"])
    g_w1 = _tn_matmul("dw_mlp1", sv["h2"], du, 4 * FSH, sub=4)
    g_w2 = _tn_matmul("dw_mlp2", abf, dyb, D)[0].reshape(NDEV, FSH, D)
    snw = sp["snw"]
    tok = on_mlp(g_w1, g_w2) if on_mlp is not None else None
    if tok is not None:
        snw = snw + tok
    dmix, dz, dyt, dxsk, qs, dos, do_t, dpo, dmod_b, dvec_o = _out_bwd(R, dxm, mod, sv["mixbf"], sv["yf"], sv["yb"], sv["xbc"], sv["z"],
                                                                 sv["o"], sv["qt"], sv["lse"], sp["dsk"], snw, sv["w_out"])
    g_wout = _unarrange_w_out(_tn_matmul("dw_out", sv["cat"], dmix, D)[0])
    dpool_in, g_wbd, dpsc = _pool_bwd(R, dpo, sv["pool_in"], lw["wbd"], sp["pscale"])
    dqt, dkt, dvt = _flash_bwd(B, T, qs, sv["kt"], dos, sv["q_t"], sv["k_t"], sv["v_t"], do_t)
    dqp, dkvb, dqa, dkva, dkr, dnw_qk = _mla_bwd(R, dqt, dkt, dvt, sv["qa"], sv["kva"], sp["qnw"], sp["kvnw"],
                                                 lw["wq"], lw["wk"], lw["wv"], cos, sin)
    g_wq = _unpad_heads(_tn_matmul("dw_q", sv["cq"], dqp, QW)[0], QK)
    g_kv = _tn_matmul("dw_kv", sv["ckv"], dkvb, 2 * QW)[0]
    g_wk = _unpad_heads(g_kv[:, :QW], NOPE).reshape(QL, H, NOPE)
    g_wv = _unpad_heads(g_kv[:, QW:], VH).reshape(QL, H, VH)
    g_wkv = jnp.concatenate([g_wk, g_wv], axis=2).reshape(QL, H * (NOPE + VH))
    alog2 = sp["alog2"]
    tok = on_mid(g_wout, g_wq, g_wkv) if on_mid is not None else None
    if tok is not None:
        alog2 = alog2 + tok
    dxbc_f, dxbc_b, ddt_f, ddt_b, da = _ssd_scan_bwd(B, T, dyt, sv["xbc"], sv["dt2"], alog2, sv["hin_f"], sv["hin_b"])
    dpre, ddtr, dcw, dcb, ddtb = _ssd_prep_bwd(R, dxbc_f, dxbc_b, dxsk, ddt_f, ddt_b, sv["xbc_raw"], sv["dt_raw"], lw["conv_w8"],
                                                sp["conv_b"], sp["dtb"])
    dx, dproj, dmod_c, dnw1 = _in_bwd(R, dxm, sv["x"], mod, sp["nw1"], dz, dpre, dqa, dkva, dpool_in, dkr, ddtr,
                                      lw["conv_w8"], lw["w_in"], latent_only)
    g_win = _unarrange_w_in(jnp.concatenate(list(_tn_matmul("dw_in", sv["h1"], dproj, PC)), axis=1))
    a2 = -jnp.exp(sp["alog2"][:, 0, :H])
    small = dict(
        norm1_w=dnw1[0], norm2_w=dnw2[0], conv_w=dcw[0:4], conv_b=dcb[0], dt_bias=ddtb[0, :2 * H].reshape(2, H),
        a_log=jnp.sum(da[:, :, 0, :H], axis=0) * a2, ssd_d=jnp.sum(dvec_o[1].reshape(H, P), axis=1), ssd_norm_w=dvec_o[0],
        q_a_norm_w=dnw_qk[0], kv_a_norm_w=dnw_qk[1],
        pool_w=jnp.stack([g_wbd[64 * g:64 * (g + 1), 64 * g:64 * (g + 1)] for g in range(4)]), pool_scale=dpsc[0])
    big = dict(w_in=g_win, w_out=g_wout, w_q_b=g_wq, w_kv_b=g_wkv, w_mlp1=g_w1, w_mlp2=g_w2)
    return dx, big, small, dmod_a + dmod_b + dmod_c


def _small_params(l, norm1_w, norm2_w, conv_b, dt_bias, a_log, ssd_d, ssd_norm_w, q_a_norm_w, kv_a_norm_w, pool_scale):
    alog2 = jnp.broadcast_to(jnp.pad(a_log[l], ((0, 0), (0, HP - H)))[:, None, :], (2, 8, HP))
    return dict(nw1=norm1_w[l][None], nw2=norm2_w[l][None], conv_b=conv_b[l][None],
                dtb=_lane_pad(dt_bias[l].reshape(2 * H), HP), alog2=alog2,
                dsk=jnp.repeat(ssd_d[l], P)[None], snw=ssd_norm_w[l][None], qnw=q_a_norm_w[l][None],
                kvnw=kv_a_norm_w[l][None], pscale=pool_scale[l][None])


SMALL_NAMES = ["mod_b", "norm1_w", "norm2_w", "conv_b", "dt_bias", "a_log", "ssd_d", "ssd_norm_w", "q_a_norm_w",
               "kv_a_norm_w", "pool_w", "pool_scale", "final_norm_w"]


def _pack(arrs):
    rows = []
    for a in arrs:
        f = a.reshape(-1).astype(F32)
        n = -(-f.shape[0] // HP) * HP
        rows.append(jnp.pad(f, (0, n - f.shape[0])).reshape(-1, HP))
    out = jnp.concatenate(rows, axis=0)
    pad = (-out.shape[0]) % 8
    return jnp.pad(out, ((0, pad), (0, 0)))


def _unpack(pack, like):
    outs, r = [], 0
    for a in like:
        n = math.prod(a.shape)
        nr = -(-n // HP)
        outs.append(pack[r:r + nr].reshape(-1)[:n].reshape(a.shape))
        r += nr
    return outs


def _local_step(x, ctx, target, mods, full_of, small_w, on_grads=None, on_mlp=None, on_mid=None):
    B, N = x.shape[0], x.shape[1]
    T = CTX + N
    R = _Rows(B, T)
    cos, sin = _rope_tables(T)
    xu = jnp.concatenate([ctx, x], axis=1)
    L = len(mods)
    lws, sps, saves = [], [], []
    for l in range(L):
        f = full_of(l, xu)
        lws.append(_layer_weights(f["w_in"], f["w_q_b"], f["w_kv_b"], f["conv_w"], small_w["pool_w"][l]))
        sps.append(_small_params(l, *[small_w[k] for k in ["norm1_w", "norm2_w", "conv_b", "dt_bias", "a_log", "ssd_d",
                                                          "ssd_norm_w", "q_a_norm_w", "kv_a_norm_w", "pool_scale"]]))
        xu, sv = _layer_fwd(R, xu, mods[l], lws[l], sps[l], f["late"], cos, sin)
        saves.append(sv)
    dx, loss8, dfnw = _loss_head(R, xu, target, small_w["final_norm_w"][None])
    bigs, smalls, dmods = [None] * L, [None] * L, [None] * L
    for l in reversed(range(L)):
        hook = functools.partial(on_mlp, l) if on_mlp is not None else None
        hook2 = functools.partial(on_mid, l) if on_mid is not None else None
        dx, bigs[l], smalls[l], dmods[l] = _layer_bwd(R, dx, saves[l], mods[l], lws[l], sps[l], cos, sin, hook, hook2, l == 0)
        if on_grads is not None:
            bigs[l], tok = on_grads(l, bigs[l], dx)
            if tok is not None:
                sps[l - 1] = dict(sps[l - 1], nw2=sps[l - 1]["nw2"] + tok)
    return loss8[0, 0], dx, bigs, smalls, dfnw[0], dmods


def kernel(x, c, ctx, c_ctx, mod_w, mod_b, norm1_w, norm2_w, w_in, conv_w, conv_b, dt_bias, a_log, ssd_d, ssd_norm_w, q_a_norm_w, w_q_b, kv_a_norm_w, w_kv_b, pool_w, pool_scale, w_out, w_mlp1, w_mlp2, final_norm_w, loss_target, m_c_ctx, m_mod_w, m_mod_b, m_norm1_w, m_norm2_w, m_w_in, m_conv_w, m_conv_b, m_dt_bias, m_a_log, m_ssd_d, m_ssd_norm_w, m_q_a_norm_w, m_w_q_b, m_kv_a_norm_w, m_w_kv_b, m_pool_w, m_pool_scale, m_w_out, m_w_mlp1, m_w_mlp2, m_final_norm_w, v_c_ctx, v_mod_w, v_mod_b, v_norm1_w, v_norm2_w, v_w_in, v_conv_w, v_conv_b, v_dt_bias, v_a_log, v_ssd_d, v_ssd_norm_w, v_q_a_norm_w, v_w_q_b, v_kv_a_norm_w, v_w_kv_b, v_pool_w, v_pool_scale, v_w_out, v_w_mlp1, v_w_mlp2, v_final_norm_w):
    args = dict(locals())
    B = x.shape[0]
    L = mod_w.shape[0]
    me = _dev_index(*_my_pos())
    CS = mod_w.shape[2]

    big_names = ["w_in", "w_out", "w_q_b", "w_kv_b", "w_mlp1", "w_mlp2"]
    shards = {n: args[n].astype(BF) for n in big_names}
    early, late_names = ["w_in", "w_q_b", "w_kv_b"], ["w_out", "w_mlp1", "w_mlp2"]
    g0 = _all_gather("gather_weights", [c, conv_w] + [shards[n][0] for n in early])
    c_all, convw_all = g0[0], g0[1]
    gathered = {0: dict(zip(early, g0[2:]))}
    cs = jnp.concatenate([c_all.reshape(NDEV * B, D), c_ctx[None], jnp.zeros((24 - NDEV * B - 1, D), F32)], axis=0)
    m_loc = _adaln_fwd(cs, mod_w)
    m_all = _all_gather("gather_mod", [m_loc])[0]
    m_full = jnp.moveaxis(m_all, 0, 2).reshape(L, 24, NDEV * CS) + mod_b[:, None, :]
    pending = {}
    tok = jnp.zeros((), F32)
    res = _exchange_start("gather_start_0", 1, [shards[n][0] for n in late_names], gather=True)
    pending[0] = res[:4]
    tok = tok + res[4][0, 0]
    for l in range(1, L):
        res = _exchange_start("gather_start_%d" % l, 1 + l, [shards[n][l] for n in early + late_names], gather=True)
        pending[l] = res[:4]
        tok = tok + res[4][0, 0]
    mods = []
    for l in range(L):
        ex = lax.dynamic_slice(m_full[l], (me * B, 0), (B, 6 * D)).reshape(B, 6, D)
        cc = jnp.broadcast_to(m_full[l, NDEV * B].reshape(1, 6, D), (B, 6, D))
        mods.append(jnp.pad(jnp.stack([cc, ex], axis=1), ((0, 0), (0, 0), (0, 2), (0, 0))) + tok)

    def full_of(l, xu):
        if l > 0:
            own, lands = _exchange_wait("gather_wait_%d" % l, *pending.pop(l), xu, gather=True)
            gathered[l] = dict(zip(early + late_names, _with_own(lands, own, me)))

        def late(after):
            if l == 0:
                own, lands = _exchange_wait("gather_wait_0", *pending.pop(0), after, gather=True)
                gathered[0].update(zip(late_names, _with_own(lands, own, me)))
            gl = gathered[l]
            return gl["w_out"].reshape(D, D), gl["w_mlp1"], gl["w_mlp2"]

        g = gathered[l]
        return dict(
            w_in=g["w_in"].reshape(D, IN_COLS),
            w_q_b=jnp.moveaxis(g["w_q_b"], 0, 1).reshape(QL, H * QK),
            w_kv_b=jnp.moveaxis(g["w_kv_b"], 0, 1).reshape(QL, H * (NOPE + VH)),
            conv_w=jnp.moveaxis(convw_all[:, l], 0, 1).reshape(4, XBC), late=late)

    def grad_blocks(big, names):
        make = dict(
            w_in=lambda g: g.reshape(NDEV, D // NDEV, IN_COLS), w_out=lambda g: g.reshape(NDEV, D // NDEV, D),
            w_q_b=lambda g: jnp.moveaxis(g.reshape(QL, NDEV, -1), 1, 0),
            w_kv_b=lambda g: jnp.moveaxis(g.reshape(QL, NDEV, -1), 1, 0), w_mlp1=lambda g: g, w_mlp2=lambda g: g)
        return [make[n](big[n]).astype(BF) for n in names]

    sent, sent_mlp, sent_mid = {}, {}, {}
    rest_names, mid_names = ["w_in"], ["w_out", "w_q_b", "w_kv_b"]

    def on_mid(l, g_wout, g_wq, g_wkv):
        if l > 0:
            return None
        blocks_mid = grad_blocks(dict(w_out=g_wout, w_q_b=g_wq, w_kv_b=g_wkv), mid_names)
        res = _exchange_start("grads_mid_start_0", 2 + 2 * L, blocks_mid, gather=False)
        sent_mid[l] = res[:4]
        return res[4][0, 0]

    def on_mlp(l, g_w1, g_w2):
        if l > 0:
            return None
        res = _exchange_start("grads_mlp_start_0", 1 + 2 * L, [g_w1.astype(BF), g_w2.astype(BF)], gather=False)
        sent_mlp[l] = res[:4]
        return res[4][0, 0]

    def on_grads(l, big, dx):
        if l == 0:
            return grad_blocks(big, rest_names), None
        res = _exchange_start("grads_start_%d" % l, 1 + L + l, grad_blocks(big, big_names), gather=False)
        sent[l] = res[:4]
        return None, res[4][0, 0]

    small_w = {k: args[k] for k in SMALL_NAMES if k != "mod_b"}
    loss_part, grad_x, blocks, smalls, dfnw, dmods = _local_step(x, ctx, loss_target, mods, full_of, small_w, on_grads, on_mlp,
                                                                 on_mid)
    loss = lax.psum(loss_part, ("x", "y", "c"))

    dm_ex = jnp.stack([dmods[l][:, 1, :6].reshape(B, 6 * D) for l in range(L)])
    dm_cc = jnp.stack([jnp.sum(dmods[l][:, 0, :6], axis=0).reshape(6 * D) for l in range(L)])
    small_parts = dict(
        mod_b=jnp.sum(dm_ex, axis=1) + dm_cc,
        **{k: jnp.stack([smalls[l][k] for l in range(L)]) for k in SMALL_NAMES[1:-1]},
        final_norm_w=dfnw, conv_w=jnp.stack([smalls[l]["conv_w"] for l in range(L)]), dm_cc=dm_cc)
    adam_grads = [small_parts[k] for k in SMALL_NAMES]
    extras = [small_parts["conv_w"], dm_cc, dm_ex]
    pack = jnp.concatenate([_pack(adam_grads), _pack(extras)], axis=0)
    pack_all = _all_gather("gather_small_grads", [pack])[0]
    wpack = _pack([args[k] for k in SMALL_NAMES])
    mpack = _pack([args["m_" + k] for k in SMALL_NAMES])
    vpack = _pack([args["v_" + k] for k in SMALL_NAMES])
    n_adam = wpack.shape[0]
    res_small = _adamw("adamw_small", [pack_all[:, :n_adam]], wpack, mpack, vpack, n_adam)
    small_out = [_unpack(r, [args[k] for k in SMALL_NAMES]) for r in res_small]
    ext_all = pack_all[:, n_adam:]
    g_conv_full, dm_cc_tot, _ = _unpack(_sum_blocks("sum_small_grads", ext_all), extras)
    dm_ex_all = jnp.stack([_unpack(ext_all[k], extras)[2] for k in range(NDEV)], axis=1)
    dm_rows = jnp.concatenate([dm_ex_all.reshape(L, NDEV * B, 6 * D), dm_cc_tot[:, None, :],
                               jnp.zeros((L, 24 - NDEV * B - 1, 6 * D), F32)], axis=1)
    dm_loc = lax.dynamic_slice(dm_rows, (0, 0, me * CS), (L, 24, CS))
    g_modw, gc_part = _adaln_bwd(cs, dm_loc, mod_w)
    gc_all = _all_gather("gather_cctx_grad", [gc_part[NDEV * B:NDEV * B + 8]])[0]
    cpad = lambda a: jnp.pad(a[None], ((0, 7), (0, 0)))
    res_cc = _adamw("adamw_cctx", [gc_all], cpad(c_ctx), cpad(m_c_ctx), cpad(v_c_ctx), 8)
    cc_out = [r[0] for r in res_cc]

    def waited(name, handles, names):
        srcs, lands = _exchange_wait(name, *handles, grad_x, gather=False)
        own = [lax.dynamic_index_in_dim(s, me, 0, keepdims=False) for s in srcs]
        return dict(zip(names, _with_own(lands, own, me)))

    recv = {0: dict(zip(rest_names, _all_to_all("exchange_grads", blocks[0])))}
    recv[0].update(waited("grads_mlp_wait_0", sent_mlp.pop(0), ["w_mlp1", "w_mlp2"]))
    recv[0].update(waited("grads_mid_wait_0", sent_mid.pop(0), mid_names))
    for l in range(1, L):
        recv[l] = waited("grads_wait_%d" % l, sent.pop(l), big_names)
    big_out = {}
    for name in big_names:
        w = args[name]
        Rr, C = math.prod(w.shape[:-1]), w.shape[-1]
        rl = Rr // L
        rt = rl if rl * C <= (1 << 17) else rl // 4
        res = _adamw("adamw_" + name, [recv[l][name].reshape(NDEV, rl, C) for l in range(L)], w.reshape(Rr, C),
                     args["m_" + name].reshape(Rr, C), args["v_" + name].reshape(Rr, C), rt)
        big_out[name] = [r.reshape(w.shape) for r in res]
    res = _adamw("adamw_mod_w", [g_modw.reshape(1, L * D, CS)], mod_w.reshape(L * D, CS), m_mod_w.reshape(L * D, CS),
                 v_mod_w.reshape(L * D, CS), L * D // 8)
    big_out["mod_w"] = [r.reshape(mod_w.shape) for r in res]
    CW = conv_w.shape[2]
    g_conv = lax.dynamic_slice(g_conv_full, (0, 0, me * CW), (L, 4, CW))
    res = _adamw("adamw_conv_w", [g_conv.reshape(1, L * 4, CW)], conv_w.reshape(L * 4, CW), m_conv_w.reshape(L * 4, CW),
                 v_conv_w.reshape(L * 4, CW), L * 4)
    big_out["conv_w"] = [r.reshape(conv_w.shape) for r in res]

    weights = ["c_ctx", "mod_w", "mod_b", "norm1_w", "norm2_w", "w_in", "conv_w", "conv_b", "dt_bias", "a_log", "ssd_d",
               "ssd_norm_w", "q_a_norm_w", "w_q_b", "kv_a_norm_w", "w_kv_b", "pool_w", "pool_scale", "w_out", "w_mlp1",
               "w_mlp2", "final_norm_w"]
    outs = [loss, grad_x]
    for kind in range(4):
        for name in weights:
            if name == "c_ctx":
                outs.append(cc_out[kind])
            elif name in big_out:
                outs.append(big_out[name][kind])
            else:
                outs.append(small_out[kind][SMALL_NAMES.index(name)])
    return tuple(outs)
```

```python
import functools
import math

import jax
import jax.numpy as jnp
from jax import lax
from jax.experimental import pallas as pl
from jax.experimental.pallas import tpu as pltpu

F32 = jnp.float32
BF = jnp.bfloat16
MXU = BF

D = 1024
CTX = 256
GRID_W = 64
EPS = 1e-6
H = 6
P = 64
SSD_IN = 384
NST = 128
XBC = 896
CH = 128
QL = 256
NOPE = 64
ROPE = 32
VH = 64
QK = 96
HP = 128
QW = H * HP
PD = 256
FF = 4096
IN_COLS = 2092
ROPE_THETA = 10000.0
PC = 2304
A_Z, A_XBC, A_QA, A_KVA, A_POOL, A_KR, A_DT = 0, 384, 1280, 1536, 1792, 2048, 2176
MIXW = SSD_IN + QW + PD
NDEV = 8
FSH = FF // NDEV
TM = 256
TQ = 256
TK = 256
VMEM_CAP = 64 * 1024 * 1024
ADAM_LR, ADAM_B1, ADAM_B2, ADAM_EPS, ADAM_WD, ADAM_STEP = 0.001, 0.9, 0.999, 1e-08, 0.01, 10


def _nbytes(shape, dtype):
    n = 1
    for s in shape:
        if s is not None:
            n *= s
    return n * jnp.dtype(dtype).itemsize


def _params(sem, block_bytes, extra=0):
    lim = min(2 * block_bytes + extra + (8 << 20), VMEM_CAP - (6 << 20))
    return pltpu.CompilerParams(dimension_semantics=sem, vmem_limit_bytes=int(lim))


def _hbm(arrays):
    return [pltpu.with_memory_space_constraint(a, pltpu.HBM) for a in arrays]


def _dot(a, b):
    return jnp.dot(a.astype(MXU), b.astype(MXU), preferred_element_type=F32)


def _dot_nt(a, b):
    return lax.dot_general(a.astype(MXU), b.astype(MXU), (((1,), (1,)), ((), ())), preferred_element_type=F32)


def _dot_tn(a, b):
    return lax.dot_general(a.astype(MXU), b.astype(MXU), (((0,), (0,)), ((), ())), preferred_element_type=F32)


def _dot01(m01, x):
    b16 = jnp.bfloat16
    m = m01.astype(b16)
    hi = x.astype(b16)
    r1 = x - hi.astype(F32)
    mid = r1.astype(b16)
    lo = (r1 - mid.astype(F32)).astype(b16)
    f = lambda v: jnp.dot(m, v, preferred_element_type=F32)
    return f(hi) + f(mid) + f(lo)


def _sigmoid(x):
    return 1.0 / (1.0 + jnp.exp(-x))


def _silu(x):
    return x * _sigmoid(x)


def _dsilu(x):
    s = _sigmoid(x)
    return s * (1.0 + x * (1.0 - s))


def _iota(shape, dim):
    return lax.broadcasted_iota(jnp.int32, shape, dim)


def _row(ref, k):
    blk = ref[...]
    return jnp.sum(jnp.where(_iota(blk.shape, 0) == k, blk, 0.0), axis=0, keepdims=True)


def _shift_rows(x, k):
    n = x.shape[0]
    return pltpu.roll(x, (-k) % n, axis=0)


class _Rows:
    def __init__(self, B, T):
        self.B, self.T = B, T
        self.nt = T // TM
        self.nct = CTX // TM

    def row(self, F):
        return pl.BlockSpec((None, TM, F), lambda b, i: (b, i, 0))

    def row2(self, F):
        return pl.BlockSpec((None, 2, TM, F), lambda b, i: (b, 0, i, 0))

    def prev8(self, F):
        return pl.BlockSpec((None, 8, F), lambda b, i: (b, jnp.maximum(i * (TM // 8) - 1, 0), 0))

    def next8(self, F):
        last = self.T // 8 - 1
        return pl.BlockSpec((None, 8, F), lambda b, i: (b, jnp.minimum((i + 1) * (TM // 8), last), 0))

    def mod(self):
        nct = self.nct
        return pl.BlockSpec((None, None, 8, D), lambda b, i: (b, jnp.where(i < nct, 0, 1), 0, 0))

    def const(self, shape):
        z = (0,) * len(shape)
        return pl.BlockSpec(tuple(shape), lambda b, i: z)

    def tgt(self, F):
        nct = self.nct
        return pl.BlockSpec((None, TM, F), lambda b, i: (b, jnp.maximum(i - nct, 0), 0))

    def xin(self, x):
        if isinstance(x, tuple):
            nct = self.nct
            return [(x[0], pl.BlockSpec((None, TM, D), lambda b, i: (b, jnp.minimum(i, nct - 1), 0))), (x[1], self.tgt(D))]
        return [(x, self.row(D))]

    def xval(self, i, refs):
        if len(refs) == 2:
            return jnp.where(i < self.nct, refs[0][...], refs[1][...])
        return refs[0][...]

    def call(self, name, body, ins, outs, scratch=(), extra=0):
        arrays = [a for a, _ in ins]
        in_specs = [s for _, s in ins]
        out_shape = [pltpu.HBM(o.shape, o.dtype) for o, _ in outs]
        out_specs = [s for _, s in outs]
        bb = 0
        for a, s in list(ins) + list(outs):
            if s.block_shape is not None:
                bb += _nbytes(s.block_shape, a.dtype)
        return pl.pallas_call(
            functools.partial(body), name=name, grid=(self.B, self.nt),
            in_specs=in_specs, out_specs=out_specs, out_shape=out_shape, scratch_shapes=list(scratch),
            compiler_params=_params(("arbitrary", "arbitrary"), bb, extra),
        )(*_hbm(arrays))

    def first_of_stream(self, i):
        return jnp.logical_or(i == 0, i == self.nct)

    def last_of_stream(self, i):
        return jnp.logical_or(i == self.nct - 1, i == self.nt - 1)

    def ext(self, i, prev_ref, cur, next_ref):
        pv = prev_ref[...].astype(F32) * jnp.where(self.first_of_stream(i), 0.0, 1.0)
        nx = next_ref[...].astype(F32) * jnp.where(self.last_of_stream(i), 0.0, 1.0)
        return jnp.concatenate([pv, cur, nx], axis=0)

    def stream_pos(self, i, rows):
        start = jnp.where(i < self.nct, 0, CTX)
        n = jnp.where(i < self.nct, CTX, self.T - CTX)
        t = i * TM - 8 - start + _iota((rows, 1), 0)
        return t, n


def _sds(shape, dtype):
    return jax.ShapeDtypeStruct(tuple(shape), dtype)


def _out(shape, dtype):
    return pltpu.HBM(tuple(shape), dtype)


def _norm_mod(x, nw, sh, sc):
    r = lax.rsqrt(jnp.mean(x * x, axis=-1, keepdims=True) + EPS)
    xn = x * r
    return xn * nw * (1.0 + sc) + sh, xn, r


def _norm_mod_bwd(dh, xn, r, nw, sc):
    dsh = jnp.sum(dh, axis=0, keepdims=True)
    dsc = jnp.sum(dh * (xn * nw), axis=0, keepdims=True)
    dnw = jnp.sum(dh * (1.0 + sc) * xn, axis=0, keepdims=True)
    dxn = dh * nw * (1.0 + sc)
    dx = r * (dxn - xn * jnp.mean(dxn * xn, axis=-1, keepdims=True))
    return dx, dsh, dsc, dnw


def _acc_rows(ref, first, rows):
    rid = _iota(ref.shape, 0)
    upd = jnp.zeros(ref.shape, F32)
    for k, v in rows.items():
        upd = upd + jnp.where(rid == k, v, 0.0)

    @pl.when(first)
    def _():
        ref[...] = upd

    @pl.when(jnp.logical_not(first))
    def _():
        ref[...] += upd


def _in_proj(R, x, mod, nw1, w_arr):
    B, T = R.B, R.T

    xin = R.xin(x)

    def body(*refs):
        x_refs = refs[:len(xin)]
        mod_ref, nw_ref, w_ref, z_ref, xbc_ref, qa_ref, kva_ref, pool_ref, kr_ref, dt_ref, h_ref = refs[len(xin):]
        h, _, _ = _norm_mod(R.xval(pl.program_id(1), x_refs), nw_ref[...], _row(mod_ref, 0), _row(mod_ref, 1))
        hb = h.astype(BF)
        h_ref[...] = hb
        p = jnp.dot(hb, w_ref[...], preferred_element_type=F32)
        z_ref[...] = p[:, A_Z:A_XBC]
        xbc_ref[...] = p[:, A_XBC:A_QA]
        qa_ref[...] = p[:, A_QA:A_KVA]
        kva_ref[...] = p[:, A_KVA:A_POOL]
        pool_ref[...] = p[:, A_POOL:A_KR]
        kr_ref[...] = p[:, A_KR:A_DT]
        dt_ref[...] = p[:, A_DT:PC]

    widths = [SSD_IN, XBC, QL, QL, PD, HP, HP]
    outs = [(_sds((B, T, w), F32), R.row(w)) for w in widths] + [(_sds((B, T, D), BF), R.row(D))]
    return R.call("in_proj", body,
                  xin + [(mod, R.mod()), (nw1, R.const((1, D))), (w_arr, R.const((D, PC)))],
                  outs, extra=8 << 20)


def _conv_pre(ext, w_ref, b_ref):
    return (_row(w_ref, 0) * _shift_rows(ext, -1)[8:8 + TM] + _row(w_ref, 1) * ext[8:8 + TM]
            + _row(w_ref, 2) * _shift_rows(ext, 1)[8:8 + TM] + _row(w_ref, 3) * _shift_rows(ext, 2)[8:8 + TM]
            + b_ref[...])


def _softplus(x):
    return jnp.maximum(x, 0.0) + jnp.log(1.0 + jnp.exp(-jnp.abs(x)))


def _ssd_prep(R, xbc_raw, dt_raw, conv_w8, conv_b, dtb):
    B, T = R.B, R.T

    def body(raw_ref, pv_ref, nx_ref, dtr_ref, w_ref, b_ref, dtb_ref, xbc_ref, dt_ref):
        i = pl.program_id(1)
        ext = R.ext(i, pv_ref, raw_ref[...], nx_ref)
        xbc_ref[...] = _silu(_conv_pre(ext, w_ref, b_ref))
        lane = _iota((TM, HP), 1)
        dtv = _softplus(dtr_ref[...] + dtb_ref[...])
        keep = lane < H
        dt_ref[0] = jnp.where(keep, dtv, 0.0)
        dt_ref[1] = jnp.where(keep, pltpu.roll(dtv, HP - H, axis=1), 0.0)

    return R.call("ssd_prep", body,
                  [(xbc_raw, R.row(XBC)), (xbc_raw, R.prev8(XBC)), (xbc_raw, R.next8(XBC)), (dt_raw, R.row(HP)),
                   (conv_w8, R.const((8, XBC))), (conv_b, R.const((1, XBC))), (dtb, R.const((1, HP)))],
                  [(_sds((B, T, XBC), F32), R.row(XBC)), (_sds((B, 2, T, HP), F32), R.row2(HP))], extra=12 << 20)


def _chunk_index(d, s, nc, nctc):
    if d == 0:
        return s
    return jnp.where(s < nctc, nctc - 1 - s, nc - 1 - (s - nctc))


def _dot01_r(x, m01, passes=3):
    b16 = jnp.bfloat16
    m = m01.astype(b16)
    out, rest = None, x
    for _ in range(passes):
        part = rest.astype(b16)
        rest = rest - part.astype(F32)
        term = jnp.dot(part, m, preferred_element_type=F32)
        out = term if out is None else out + term
    return out


def _scan_common(d, dtv, alog_ref):
    sel = _iota((HP, H * HP), 0) == (_iota((HP, H * HP), 1) >> 7)
    a_all = _dot01_r(-jnp.exp(alog_ref[d]), sel)
    dt_all = _dot01_r(dtv, sel)
    adt_all = dt_all * jnp.concatenate([a_all] * (CH // 8), axis=0)
    row = _iota((CH, CH), 0)
    col = _iota((CH, CH), 1)
    inc = col <= row if d == 0 else col >= row
    inc_t = row <= col if d == 0 else row >= col
    q_all = _dot01(inc, adt_all)
    return a_all, dt_all, adt_all, q_all, inc, inc_t


def _head_decay(h, adt_all, q_all, inc, inc_t):
    q = q_all[:, HP * h:HP * (h + 1)]
    q_t = q.T
    qtot = jnp.sum(adt_all[:, HP * h:HP * (h + 1)], axis=0, keepdims=True)
    lm = jnp.where(inc, jnp.exp(q - q_t), 0.0)
    lm_t = jnp.where(inc_t, jnp.exp(q_t - q), 0.0)
    return lm, lm_t, jnp.exp(q), jnp.exp(qtot), jnp.exp(qtot - q)


def _ssd_scan(B, T, xbc, dt2, alog2):
    nc, nctc = T // CH, CTX // CH

    def chain(d, xbc_ref, dt_ref, alog_ref, y_ref, hin_ref, hs):
        xbc_v = xbc_ref[...]
        dtv = dt_ref[...]
        blk = alog_ref[d]
        adt = dtv * -jnp.exp(jnp.sum(jnp.where(_iota(blk.shape, 0) == 0, blk, 0.0), axis=0, keepdims=True))
        row, col = _iota((CH, CH), 0), _iota((CH, CH), 1)
        inc = col <= row if d == 0 else col >= row
        inc_tf = (row <= col if d == 0 else row >= col).astype(F32)
        q = _dot01(inc, adt)
        hin_ref[...] = hs[d]
        for g in range(2):
            bg = xbc_v[:, SSD_IN + NST * g:SSD_IN + NST * (g + 1)]
            cg = xbc_v[:, SSD_IN + 2 * NST + NST * g:SSD_IN + 2 * NST + NST * (g + 1)]
            gm = _dot_nt(cg, bg)
            for r in range(3):
                h = 3 * g + r
                onehot = (_iota((1, HP), 1) == h).astype(F32)
                adt_h = jnp.sum(adt * onehot, axis=1, keepdims=True)
                qc = jnp.sum(q * onehot, axis=1, keepdims=True)
                dt_h = jnp.sum(dtv * onehot, axis=1, keepdims=True)
                qr = jnp.sum(adt_h * inc_tf, axis=0, keepdims=True)
                qtot = jnp.sum(adt_h, axis=0, keepdims=True)
                lm = jnp.where(inc, jnp.exp(qc - qr), 0.0)
                xh = xbc_v[:, P * h:P * (h + 1)] * dt_h
                hprev = hs[d, P * h:P * (h + 1), :]
                y_ref[:, P * h:P * (h + 1)] = _dot(gm * lm, xh) + jnp.exp(qc) * _dot_nt(cg, hprev)
                hs[d, P * h:P * (h + 1), :] = jnp.exp(qtot) * hprev + _dot_tn(xh * jnp.exp(qtot - qc), bg)

    def body(xf_ref, xb_ref, dtf_ref, dtb_ref, alog_ref, yf_ref, yb_ref, hf_ref, hb_ref, hs):
        @pl.when(pl.program_id(1) == 0)
        def _():
            hs[...] = jnp.zeros(hs.shape, F32)

        chain(0, xf_ref, dtf_ref, alog_ref, yf_ref, hf_ref, hs)
        chain(1, xb_ref, dtb_ref, alog_ref, yb_ref, hb_ref, hs)

    cidx = lambda d: (lambda s: _chunk_index(d, s, nc, nctc))
    specs = lambda d: dict(
        xbc=pl.BlockSpec((None, CH, XBC), lambda b, s: (b, cidx(d)(s), 0)),
        dt=pl.BlockSpec((None, None, CH, HP), lambda b, s: (b, d, cidx(d)(s), 0)),
        y=pl.BlockSpec((None, CH, SSD_IN), lambda b, s: (b, cidx(d)(s), 0)),
        h=pl.BlockSpec((None, None, SSD_IN, NST), lambda b, s: (b, cidx(d)(s), 0, 0)))
    f, r = specs(0), specs(1)
    bb = 2 * (_nbytes((CH, XBC), F32) + _nbytes((CH, HP), F32) + _nbytes((CH, SSD_IN), F32) + _nbytes((SSD_IN, NST), F32))
    return pl.pallas_call(
        functools.partial(body), name="ssd_scan", grid=(B, nc),
        in_specs=[f["xbc"], r["xbc"], f["dt"], r["dt"], pl.BlockSpec((2, 8, HP), lambda b, s: (0, 0, 0))],
        out_specs=[f["y"], r["y"], f["h"], r["h"]],
        out_shape=[_out((B, T, SSD_IN), F32)] * 2 + [_out((B, nc, SSD_IN, NST), F32)] * 2,
        scratch_shapes=[pltpu.VMEM((2, SSD_IN, NST), F32)],
        compiler_params=_params(("arbitrary",) * 2, bb, 12 << 20),
    )(*_hbm([xbc, xbc, dt2, dt2, alog2]))


def _swap8(u):
    lane = _iota(u.shape, 1)
    n = u.shape[1]
    return jnp.where((lane & 15) < 8, pltpu.roll(u, n - 8, axis=1), pltpu.roll(u, 8, axis=1))


def _rope(u, cos, sin_signed):
    return u * cos + _swap8(u) * sin_signed


def _rms(x, w):
    r = lax.rsqrt(jnp.mean(x * x, axis=-1, keepdims=True) + EPS)
    xh = x * r
    return xh * w, xh, r


def _rms_bwd(dy, xh, r, w):
    dw = jnp.sum(dy * xh, axis=0, keepdims=True)
    dxh = dy * w
    return r * (dxh - xh * jnp.mean(dxh * xh, axis=-1, keepdims=True)), dw


def _tile6(t):
    return jnp.concatenate([t] * H, axis=1)


def _per_head(fn, u):
    return jnp.concatenate([fn(u[:, HP * h:HP * (h + 1)]) for h in range(H)], axis=1)


def _mla_prep(R, qa, kva, kr, qnw, kvnw, wq, wk, wv, cos, sin):
    B, T = R.B, R.T
    scale = QK ** -0.5

    def body(qa_ref, kva_ref, kr_ref, qnw_ref, kvnw_ref, wq_ref, wk_ref, wv_ref, cos_ref, sin_ref,
             qt_ref, kt_ref, qtr_ref, ktr_ref, vtr_ref, cq_ref, ckv_ref):
        cq, _, _ = _rms(qa_ref[...], qnw_ref[...])
        ckv, _, _ = _rms(kva_ref[...], kvnw_ref[...])
        cqb, ckvb = cq.astype(BF), ckv.astype(BF)
        cq_ref[...] = cqb
        ckv_ref[...] = ckvb
        cos1, sin1 = cos_ref[...], sin_ref[...]
        q = _per_head(lambda u: _rope(u, cos1, sin1), jnp.dot(cqb, wq_ref[...], preferred_element_type=F32)) * scale
        qt_ref[...] = q.astype(BF)
        kk = _rope(kr_ref[...], cos1, sin1)
        hl = _iota((TM, QW), 1) & (HP - 1)
        k = jnp.dot(ckvb, wk_ref[...], preferred_element_type=F32) + _tile6(kk)
        k = jnp.where(jnp.logical_and(hl >= QK, hl < QK + 3), 1.0, k)
        kt_ref[...] = k.astype(BF)
        v = jnp.dot(ckvb, wv_ref[...], preferred_element_type=F32)
        v = jnp.where(jnp.logical_and(hl >= VH, hl < VH + 4), 1.0, v)
        for h in range(H):
            cols = slice(HP * h, HP * (h + 1))
            qtr_ref[h] = q[:, cols].T.astype(BF)
            ktr_ref[h] = k[:, cols].T.astype(BF)
            vtr_ref[h] = v[:, cols].T.astype(BF)

    tr = (_sds((B, H, HP, T), BF), pl.BlockSpec((None, H, HP, TM), lambda b, i: (b, 0, 0, i)))
    return R.call("mla_prep", body,
                  [(qa, R.row(QL)), (kva, R.row(QL)), (kr, R.row(HP)), (qnw, R.const((1, QL))), (kvnw, R.const((1, QL))),
                   (wq, R.const((QL, QW))), (wk, R.const((QL, QW))), (wv, R.const((QL, QW))),
                   (cos, pl.BlockSpec((TM, HP), lambda b, i: (i, 0))), (sin, pl.BlockSpec((TM, HP), lambda b, i: (i, 0)))],
                  [(_sds((B, T, QW), BF), R.row(QW))] * 2 + [tr] * 3 + [(_sds((B, T, QL), BF), R.row(QL))] * 2, extra=12 << 20)


def _flash_fwd(B, T, q_t, kt, v_t):
    nq, nk = T // TQ, T // TK
    HS = 2

    def body(q_ref, k_ref, v_ref, o_ref, lse_ref, s_scr):
        i = pl.program_id(2)

        def attend(nch):
            ms = []
            for hh in range(HS):
                q_tr = q_ref[hh]
                mrun = None
                for j in range(nch):
                    s = _dot(k_ref[TK * j:TK * (j + 1), HP * hh:HP * (hh + 1)], q_tr)
                    s_scr[hh, j] = s
                    mrun = s if mrun is None else jnp.maximum(mrun, s)
                ms.append(jnp.max(mrun, axis=0, keepdims=True))
            row = _iota((HP, TQ), 0)
            for hh in range(HS):
                acc_t = jnp.zeros((HP, TQ), F32)
                for j in range(nch):
                    acc_t = acc_t + _dot(v_ref[hh, :, TK * j:TK * (j + 1)], jnp.exp(s_scr[hh, j] - ms[hh]))
                l = jnp.sum(jnp.where(row == VH, acc_t, 0.0), axis=0, keepdims=True)
                o_ref[:, HP * hh:HP * (hh + 1)] = jnp.where(row < VH, acc_t / l, 0.0).T
                lse_ref[hh] = ms[hh] + jnp.log(l)

        @pl.when(i < CTX // TQ)
        def _():
            attend(CTX // TK)

        @pl.when(i >= CTX // TQ)
        def _():
            attend(nk)

    bb = HS * (_nbytes((TQ, HP), BF) + 2 * _nbytes((T, HP), BF) + 2 * _nbytes((TQ, HP), F32))
    return pl.pallas_call(
        functools.partial(body), name="flash_fwd", grid=(B, H // HS, nq),
        in_specs=[pl.BlockSpec((None, HS, HP, TQ), lambda b, h, i: (b, h, 0, i)),
                  pl.BlockSpec((None, T, HS * HP), lambda b, h, i: (b, 0, h)),
                  pl.BlockSpec((None, HS, HP, T), lambda b, h, i: (b, h, 0, 0))],
        out_specs=[pl.BlockSpec((None, TQ, HS * HP), lambda b, h, i: (b, i, h)),
                   pl.BlockSpec((None, HS, 1, TQ), lambda b, h, i: (b, h, 0, i))],
        out_shape=[_out((B, T, QW), F32), _out((B, H, 1, T), F32)],
        scratch_shapes=[pltpu.VMEM((HS, nk, TK, TQ), F32)],
        compiler_params=_params(("arbitrary",) * 3, bb, _nbytes((HS, nk, TK, TQ), F32) + (8 << 20)),
    )(*_hbm([q_t, kt, v_t]))


def _pool_terms(R, i, rows):
    t, n = R.stream_pos(i, rows)
    lane = _iota((1, PD), 1)
    half = jnp.where(lane < 64, 1, jnp.where(lane < 128, 2, jnp.where(lane < 192, 4, 8)))
    cnt = (jnp.minimum(t + half, n) - jnp.maximum(t - half, 0)).astype(F32)
    valid = jnp.logical_and(t >= 0, t < n)
    return jnp.where(valid, cnt, 1.0), valid.astype(F32), lane


def _lane_select(lane, a2, a4, a8, a16):
    return jnp.where(lane < 64, a2, jnp.where(lane < 128, a4, jnp.where(lane < 192, a8, a16)))


def _pool_centred(R, i, ext):
    cnt, valid, lane = _pool_terms(R, i, ext.shape[0])
    s2 = ext + _shift_rows(ext, -1)
    s4 = _shift_rows(s2, -1) + _shift_rows(s2, 1)
    s8 = _shift_rows(s4, -2) + _shift_rows(s4, 2)
    s16 = _shift_rows(s8, -4) + _shift_rows(s8, 4)
    return _lane_select(lane, s2, s4, s8, s16) / cnt - ext, cnt, valid, lane


def _group_mask():
    return _iota((1, SSD_IN), 1) < SSD_IN // 2


def _ssd_gate(yf_ref, yb_ref, xbc_ref, z_ref, dsk_ref):
    ytot = yf_ref[...] + yb_ref[...] + xbc_ref[:, 0:SSD_IN] * dsk_ref[...]
    z = z_ref[...]
    gz = ytot * _silu(z)
    g0 = _group_mask()
    sq = gz * gz
    s0 = jnp.sum(jnp.where(g0, sq, 0.0), axis=1, keepdims=True)
    s1 = jnp.sum(jnp.where(g0, 0.0, sq), axis=1, keepdims=True)
    half = SSD_IN // 2
    r = jnp.where(g0, lax.rsqrt(s0 / half + EPS), lax.rsqrt(s1 / half + EPS))
    return ytot, z, gz, r


def _out_proj(R, x, mod, yf, yb, xbc, z, o, pool_in, wbd, pscale, dsk, snw, wout):
    B, T = R.B, R.T
    xin = R.xin(x)

    def body(*refs):
        x_refs = refs[:len(xin)]
        (mod_ref, yf_ref, yb_ref, xbc_ref, z_ref, o_ref, u_ref, upv_ref, unx_ref, wbd_ref, psc_ref, dsk_ref, snw_ref, w_ref,
         xmid_ref, cat_ref, mix_ref) = refs[len(xin):]
        i = pl.program_id(1)
        _, _, gz, r = _ssd_gate(yf_ref, yb_ref, xbc_ref, z_ref, dsk_ref)
        dm, _, _, _ = _pool_centred(R, i, R.ext(i, upv_ref, u_ref[...], unx_ref))
        cat_ref[:, 0:SSD_IN] = (gz * r * snw_ref[...]).astype(BF)
        cat_ref[:, SSD_IN:SSD_IN + QW] = o_ref[...].astype(BF)
        cat_ref[:, SSD_IN + QW:MIXW] = (_dot(dm[8:8 + TM], wbd_ref[...]) * psc_ref[...]).astype(BF)
        mix = jnp.dot(cat_ref[...], w_ref[...], preferred_element_type=F32)
        mix_ref[...] = mix.astype(BF)
        xmid_ref[...] = R.xval(i, x_refs) + _row(mod_ref, 2) * mix

    return R.call("out_proj", body,
                  xin + [(mod, R.mod()), (yf, R.row(SSD_IN)), (yb, R.row(SSD_IN)), (xbc, R.row(XBC)), (z, R.row(SSD_IN)),
                         (o, R.row(QW)), (pool_in, R.row(PD)), (pool_in, R.prev8(PD)), (pool_in, R.next8(PD)),
                         (wbd, R.const((PD, PD))), (pscale, R.const((1, PD))),
                         (dsk, R.const((1, SSD_IN))), (snw, R.const((1, SSD_IN))), (wout, R.const((MIXW, D)))],
                  [(_sds((B, T, D), F32), R.row(D)), (_sds((B, T, MIXW), BF), R.row(MIXW)), (_sds((B, T, D), BF), R.row(D))],
                  extra=12 << 20)


def _load_once(first, pairs, sem):
    @pl.when(first)
    def _():
        cps = [pltpu.make_async_copy(src, dst, sem.at[k]) for k, (src, dst) in enumerate(pairs)]
        for cp in cps:
            cp.start()
        for cp in cps:
            cp.wait()


ANY = pl.BlockSpec(memory_space=pl.ANY)


def _mlp_fwd(R, xmid, mod, nw2, w1, w2):
    B, T = R.B, R.T

    def body(x_ref, mod_ref, nw_ref, w1_hbm, w2_hbm, xo_ref, h_ref, u_ref, y_ref, w1_v, w2_v, sem):
        first = jnp.logical_and(pl.program_id(0) == 0, pl.program_id(1) == 0)
        _load_once(first, [(w1_hbm, w1_v), (w2_hbm, w2_v)], sem)
        x = x_ref[...]
        h, _, _ = _norm_mod(x, nw_ref[...], _row(mod_ref, 3), _row(mod_ref, 4))
        hb = h.astype(BF)
        h_ref[...] = hb
        y = jnp.zeros((TM, D), F32)
        for j in range(NDEV):
            u = jnp.dot(hb, w1_v[j], preferred_element_type=F32)
            u_ref[:, FSH * j:FSH * (j + 1)] = u.astype(BF)
            a = jnp.square(jnp.maximum(u, 0.0))
            y = y + jnp.dot(a.astype(BF), w2_v[j], preferred_element_type=F32)
        y_ref[...] = y.astype(BF)
        xo_ref[...] = x + _row(mod_ref, 5) * y

    return R.call("mlp_fwd", body,
                  [(xmid, R.row(D)), (mod, R.mod()), (nw2, R.const((1, D))), (w1, ANY), (w2, ANY)],
                  [(_sds((B, T, D), F32), R.row(D)), (_sds((B, T, D), BF), R.row(D)), (_sds((B, T, FF), BF), R.row(FF)),
                   (_sds((B, T, D), BF), R.row(D))],
                  scratch=[pltpu.VMEM((NDEV, D, FSH), w1.dtype), pltpu.VMEM((NDEV, FSH, D), w2.dtype), pltpu.SemaphoreType.DMA((2,))],
                  extra=(2 * _nbytes((NDEV, D, FSH), BF)) + (8 << 20))


def _loss_head(R, x, tgt, fnw):
    B, T = R.B, R.T

    def body(x_ref, t_ref, w_ref, dx_ref, loss_ref, dw_ref):
        b, i = pl.program_id(0), pl.program_id(1)
        live = jnp.where(i >= R.nct, 1.0, 0.0)
        y, xh, r = _rms(x_ref[...], w_ref[...])
        err = (y - t_ref[...]) * live
        dy = err / D
        dxn, dw = _rms_bwd(dy, xh, r, w_ref[...])
        dx_ref[...] = dxn
        first = jnp.logical_and(b == 0, i == 0)
        part = 0.5 * jnp.sum(jnp.sum(err * err, axis=1, keepdims=True), axis=0, keepdims=True) / D
        _acc_rows(loss_ref, first, {0: jnp.broadcast_to(part, (1, HP))})
        _acc_rows(dw_ref, first, {0: dw})

    return R.call("loss_head", body,
                  [(x, R.row(D)), (tgt, R.tgt(D)), (fnw, R.const((1, D)))],
                  [(_sds((B, T, D), F32), R.row(D)), (_sds((8, HP), F32), R.const((8, HP))), (_sds((8, D), F32), R.const((8, D)))],
                  extra=8 << 20)


def _tn_matmul(name, a, b, tn, sub=1):
    B, T, K = a.shape
    N = b.shape[2]
    nk = 1
    while T % nk or (T // nk) > 1088 or (T // nk) % 16:
        nk += 1
    tk = T // nk
    kt = K if K <= 1536 else 1024
    w = tn // sub
    assert K % kt == 0 and N % tn == 0 and tn % sub == 0

    def body(a_ref, b_ref, o_ref):
        first = jnp.logical_and(pl.program_id(2) == 0, pl.program_id(3) == 0)

        @pl.when(first)
        def _():
            o_ref[...] = jnp.zeros(o_ref.shape, F32)

        acc = _dot_tn(a_ref[...], b_ref[...])
        for s in range(sub):
            o_ref[s] += acc[:, w * s:w * (s + 1)]

    bb = _nbytes((tk, kt), a.dtype) + _nbytes((tk, tn), b.dtype) + _nbytes((kt, tn), F32)
    return pl.pallas_call(
        functools.partial(body), name=name, grid=(N // tn, K // kt, B, nk),
        in_specs=[pl.BlockSpec((None, tk, kt), lambda j, kk, bi, t: (bi, t, kk)),
                  pl.BlockSpec((None, tk, tn), lambda j, kk, bi, t: (bi, t, j))],
        out_specs=pl.BlockSpec((sub, kt, w), lambda j, kk, bi, t: (j, kk, 0)),
        out_shape=_out((sub * N // tn, K, w), F32),
        compiler_params=_params(("arbitrary",) * 4, bb, _nbytes((kt, tn), F32) + (8 << 20)),
    )(*_hbm([a, b]))


def _mlp_bwd(R, dxo, xmid, ubf, ybf, mod, nw2, w1, w2):
    B, T = R.B, R.T

    def body(dxo_ref, x_ref, u_ref, y_ref, mod_ref, nw_ref, w1_hbm, w2_hbm,
             dxm_ref, du_ref, a_ref, dy_ref, dmod_ref, dnw_ref, w1_v, w2_v, sem):
        b, i = pl.program_id(0), pl.program_id(1)
        _load_once(jnp.logical_and(b == 0, i == 0), [(w1_hbm, w1_v), (w2_hbm, w2_v)], sem)
        dxo = dxo_ref[...]
        _, xn, r = _norm_mod(x_ref[...], nw_ref[...], _row(mod_ref, 3), _row(mod_ref, 4))
        dyb = (dxo * _row(mod_ref, 5)).astype(BF)
        dy_ref[...] = dyb
        dg2 = jnp.sum(dxo * y_ref[...].astype(F32), axis=0, keepdims=True)
        dh = jnp.zeros((TM, D), F32)
        for j in range(NDEV):
            rl = jnp.maximum(u_ref[:, FSH * j:FSH * (j + 1)].astype(F32), 0.0)
            a_ref[:, FSH * j:FSH * (j + 1)] = (rl * rl).astype(BF)
            du = (_dot_nt(dyb, w2_v[j]) * (2.0 * rl)).astype(BF)
            du_ref[:, FSH * j:FSH * (j + 1)] = du
            dh = dh + _dot_nt(du, w1_v[j])
        dx, dsh, dsc, dnw = _norm_mod_bwd(dh, xn, r, nw_ref[...], _row(mod_ref, 4))
        dxm_ref[...] = dxo + dx
        _acc_rows(dmod_ref, R.first_of_stream(i), {3: dsh, 4: dsc, 5: dg2})
        _acc_rows(dnw_ref, jnp.logical_and(b == 0, i == 0), {0: dnw})

    return R.call("mlp_bwd", body,
                  [(dxo, R.row(D)), (xmid, R.row(D)), (ubf, R.row(FF)), (ybf, R.row(D)), (mod, R.mod()), (nw2, R.const((1, D))),
                   (w1, ANY), (w2, ANY)],
                  [(_sds((B, T, D), F32), R.row(D)), (_sds((B, T, FF), BF), R.row(FF)), (_sds((B, T, FF), BF), R.row(FF)),
                   (_sds((B, T, D), BF), R.row(D)), (_sds((B, 2, 8, D), F32), R.mod()), (_sds((8, D), F32), R.const((8, D)))],
                  scratch=[pltpu.VMEM((NDEV, D, FSH), w1.dtype), pltpu.VMEM((NDEV, FSH, D), w2.dtype), pltpu.SemaphoreType.DMA((2,))],
                  extra=(2 * _nbytes((NDEV, D, FSH), BF)) + (8 << 20))


def _minus_in_lanes(x, col, first):
    b16 = jnp.bfloat16
    hi = col.astype(b16).astype(F32)
    r1 = col - hi
    mid = r1.astype(b16).astype(F32)
    lane = _iota(x.shape, 1)
    return jnp.where(lane == first, -hi, jnp.where(lane == first + 1, -mid, jnp.where(lane == first + 2, mid - r1, x)))


def _out_bwd(R, dxm, mod, mixbf, yf, yb, xbc, z, o, qt, lse, dsk, snw, wout):
    B, T = R.B, R.T

    def body(dxm_ref, mod_ref, mix_ref, yf_ref, yb_ref, xbc_ref, z_ref, o_ref, q_ref, lse_ref, dsk_ref, snw_ref, w_ref,
             dmix_ref, dz_ref, dyt_ref, dxsk_ref, qs_ref, dos_ref, dotr_ref, dpo_ref, dmod_ref, dvec_ref):
        b, i = pl.program_id(0), pl.program_id(1)
        dxm = dxm_ref[...]
        dmixb = (dxm * _row(mod_ref, 2)).astype(BF)
        dmix_ref[...] = dmixb
        dg1 = jnp.sum(dxm * mix_ref[...].astype(F32), axis=0, keepdims=True)
        dcat = _dot_nt(dmixb, w_ref[...])
        do_v = dcat[:, SSD_IN:SSD_IN + QW]
        doo = do_v * o_ref[...]
        for h in range(H):
            cols = slice(HP * h, HP * (h + 1))
            dd = jnp.sum(doo[:, cols], axis=1, keepdims=True)
            dos_ref[:, cols] = _minus_in_lanes(do_v[:, cols], dd, VH + 1).astype(BF)
            lse_b = jnp.broadcast_to(lse_ref[h], (HP, TM)).T
            qs_ref[:, cols] = _minus_in_lanes(q_ref[:, cols].astype(F32), lse_b, QK).astype(BF)
            dotr_ref[h] = do_v[:, cols].T.astype(BF)
        dpo_ref[...] = dcat[:, SSD_IN + QW:MIXW]
        dsn = dcat[:, 0:SSD_IN]
        ytot, zv, gz, r = _ssd_gate(yf_ref, yb_ref, xbc_ref, z_ref, dsk_ref)
        gh = gz * r
        dsnw = jnp.sum(dsn * gh, axis=0, keepdims=True)
        dgh = dsn * snw_ref[...]
        g0 = _group_mask()
        pr = dgh * gh
        half = SSD_IN // 2
        m0 = jnp.sum(jnp.where(g0, pr, 0.0), axis=1, keepdims=True) / half
        m1 = jnp.sum(jnp.where(g0, 0.0, pr), axis=1, keepdims=True) / half
        dgz = r * (dgh - gh * jnp.where(g0, m0, m1))
        dyt = dgz * _silu(zv)
        dz_ref[...] = (dgz * ytot * _dsilu(zv)).astype(BF)
        dyt_ref[...] = dyt
        dxsk_ref[...] = dyt * dsk_ref[...]
        ddsk = jnp.sum(dyt * xbc_ref[:, 0:SSD_IN], axis=0, keepdims=True)
        _acc_rows(dmod_ref, R.first_of_stream(i), {2: dg1})
        _acc_rows(dvec_ref, jnp.logical_and(b == 0, i == 0), {0: dsnw, 1: ddsk})

    return R.call("out_bwd", body,
                  [(dxm, R.row(D)), (mod, R.mod()), (mixbf, R.row(D)), (yf, R.row(SSD_IN)), (yb, R.row(SSD_IN)), (xbc, R.row(XBC)),
                   (z, R.row(SSD_IN)),
                   (o, R.row(QW)), (qt, R.row(QW)), (lse, pl.BlockSpec((None, H, 1, TM), lambda b, i: (b, 0, 0, i))),
                   (dsk, R.const((1, SSD_IN))), (snw, R.const((1, SSD_IN))), (wout, R.const((MIXW, D)))],
                  [(_sds((B, T, D), BF), R.row(D)), (_sds((B, T, SSD_IN), BF), R.row(SSD_IN)), (_sds((B, T, SSD_IN), F32), R.row(SSD_IN)),
                   (_sds((B, T, SSD_IN), F32), R.row(SSD_IN)), (_sds((B, T, QW), BF), R.row(QW)), (_sds((B, T, QW), BF), R.row(QW)),
                   (_sds((B, H, HP, T), BF), pl.BlockSpec((None, H, HP, TM), lambda b, i: (b, 0, 0, i))),
                   (_sds((B, T, PD), F32), R.row(PD)),
                   (_sds((B, 2, 8, D), F32), R.mod()), (_sds((8, SSD_IN), F32), R.const((8, SSD_IN)))],
                  extra=8 << 20)


def _pool_bwd(R, dpo, u, wbd, scale):
    B, T = R.B, R.T

    def body(d_ref, dpv_ref, dnx_ref, u_ref, upv_ref, unx_ref, w_ref, sc_ref, du_ref, dw_ref, dsc_ref):
        b, i = pl.program_id(0), pl.program_id(1)
        ext_u = R.ext(i, upv_ref, u_ref[...], unx_ref)
        ext_d = R.ext(i, dpv_ref, d_ref[...], dnx_ref)
        dm, cnt, valid, lane = _pool_centred(R, i, ext_u)
        ddm = _dot_nt(ext_d * sc_ref[...], w_ref[...]) * valid
        e = ddm / cnt
        a2 = e + _shift_rows(e, 1)
        a4 = _shift_rows(a2, -1) + _shift_rows(a2, 1)
        a8 = _shift_rows(a4, -2) + _shift_rows(a4, 2)
        a16 = _shift_rows(a8, -4) + _shift_rows(a8, 4)
        du_ref[...] = (_lane_select(lane, a2, a4, a8, a16) - ddm)[8:8 + TM].astype(BF)
        dmc = dm[8:8 + TM]
        dpo_c = d_ref[...]
        first = jnp.logical_and(b == 0, i == 0)

        @pl.when(first)
        def _():
            dw_ref[...] = jnp.zeros(dw_ref.shape, F32)

        dw_ref[...] += _dot_tn(dmc, dpo_c * sc_ref[...])
        _acc_rows(dsc_ref, first, {0: jnp.sum(dpo_c * _dot(dmc, w_ref[...]), axis=0, keepdims=True)})

    return R.call("pool_bwd", body,
                  [(dpo, R.row(PD)), (dpo, R.prev8(PD)), (dpo, R.next8(PD)), (u, R.row(PD)), (u, R.prev8(PD)), (u, R.next8(PD)),
                   (wbd, R.const((PD, PD))), (scale, R.const((1, PD)))],
                  [(_sds((B, T, PD), BF), R.row(PD)), (_sds((PD, PD), F32), R.const((PD, PD))), (_sds((8, PD), F32), R.const((8, PD)))],
                  extra=8 << 20)


def _flash_bwd(B, T, qs, kt, dos, q_t, k_t, v_t, do_t):
    nk, nq = T // TK, T // TQ
    HS = 2

    def body(k_ref, kt_ref, vt_ref, q_ref, do_ref, qt_ref, dot_ref, dq_ref, dk_ref, dv_ref, s_scr, dp_scr, ds_scr):
        j = pl.program_id(2)
        ctx_keys = jnp.where(j < CTX // TK, 1.0, 0.0)
        for hh in range(HS):
            cols = slice(HP * hh, HP * (hh + 1))
            s_scr[hh] = _dot(q_ref[:, cols], kt_ref[hh])
            dp_scr[hh] = _dot(do_ref[:, cols], vt_ref[hh])
        for hh in range(HS):
            cols = slice(HP * hh, HP * (hh + 1))
            dk_tr = jnp.zeros((HP, TK), F32)
            dv_tr = jnp.zeros((HP, TK), F32)
            for i in range(nq):
                rows = slice(TQ * i, TQ * (i + 1))
                p = jnp.exp(s_scr[hh, rows, :])
                if i < CTX // TQ:
                    p = p * ctx_keys
                ds = (p * dp_scr[hh, rows, :]).astype(BF)
                ds_scr[hh, rows, :] = ds
                dv_tr = dv_tr + _dot(dot_ref[hh, :, rows], p)
                dk_tr = dk_tr + _dot(qt_ref[hh, :, rows], ds)
            dq = _dot(ds_scr[hh], k_ref[:, cols])

            @pl.when(j == 0)
            def _():
                dq_ref[:, cols] = dq

            @pl.when(j > 0)
            def _():
                dq_ref[:, cols] += dq

            dk_ref[:, cols] = dk_tr.T
            dv_ref[:, cols] = dv_tr.T

    tspec = pl.BlockSpec((None, TK, HS * HP), lambda b, h, j: (b, j, h))
    fspec = pl.BlockSpec((None, T, HS * HP), lambda b, h, j: (b, 0, h))
    ttspec = pl.BlockSpec((None, HS, HP, TK), lambda b, h, j: (b, h, 0, j))
    ftspec = pl.BlockSpec((None, HS, HP, T), lambda b, h, j: (b, h, 0, 0))
    bb = HS * (4 * _nbytes((T, HP), BF) + _nbytes((T, HP), F32) + 8 * _nbytes((TK, HP), F32))
    scr = HS * (2 * _nbytes((T, TK), F32) + _nbytes((T, TK), BF))
    return pl.pallas_call(
        functools.partial(body), name="flash_bwd", grid=(B, H // HS, nk),
        in_specs=[tspec, ttspec, ttspec, fspec, fspec, ftspec, ftspec],
        out_specs=[fspec, tspec, tspec],
        out_shape=[_out((B, T, QW), F32)] * 3,
        scratch_shapes=[pltpu.VMEM((HS, T, TK), F32), pltpu.VMEM((HS, T, TK), F32), pltpu.VMEM((HS, T, TK), BF)],
        compiler_params=_params(("arbitrary",) * 3, bb, scr + (8 << 20)),
    )(*_hbm([kt, k_t, v_t, qs, dos, q_t, do_t]))


def _mla_bwd(R, dqt, dkt, dvt, qa, kva, qnw, kvnw, wq, wk, wv, cos, sin):
    B, T = R.B, R.T
    scale = QK ** -0.5

    def body(dq_ref, dk_ref, dv_ref, qa_ref, kva_ref, qnw_ref, kvnw_ref, wq_ref, wk_ref, wv_ref, cos_ref, sin_ref,
             dqp_ref, dkvb_ref, dqa_ref, dkva_ref, dkr_ref, dnw_ref):
        b, i = pl.program_id(0), pl.program_id(1)
        cos1, sin1 = cos_ref[...], sin_ref[...]
        dq = dq_ref[...] * scale
        dqp = _per_head(lambda g: g * cos1 + _swap8(g * sin1), dq).astype(BF)
        dqp_ref[...] = dqp
        dkv = dk_ref[...]
        dkb = dkv.astype(BF)
        dvb = dv_ref[...].astype(BF)
        dkvb_ref[:, :QW] = dkb
        dkvb_ref[:, QW:] = dvb
        dkk = dkv[:, 0:HP]
        for h in range(1, H):
            dkk = dkk + dkv[:, HP * h:HP * (h + 1)]
        lane = _iota((TM, HP), 1)
        rope_lane = jnp.logical_and(lane >= NOPE, lane < NOPE + ROPE)
        dkr_ref[...] = jnp.where(rope_lane, dkk * cos1 + _swap8(dkk * sin1), 0.0).astype(BF)
        _, qh, qr = _rms(qa_ref[...], qnw_ref[...])
        _, kh, kr_ = _rms(kva_ref[...], kvnw_ref[...])
        dcq = _dot_nt(dqp, wq_ref[...])
        dckv = _dot_nt(dkb, wk_ref[...]) + _dot_nt(dvb, wv_ref[...])
        dqa, dqw = _rms_bwd(dcq, qh, qr, qnw_ref[...])
        dkva, dkw = _rms_bwd(dckv, kh, kr_, kvnw_ref[...])
        dqa_ref[...] = dqa.astype(BF)
        dkva_ref[...] = dkva.astype(BF)
        _acc_rows(dnw_ref, jnp.logical_and(b == 0, i == 0), {0: dqw, 1: dkw})

    tab = pl.BlockSpec((TM, HP), lambda b, i: (i, 0))
    return R.call("mla_bwd", body,
                  [(dqt, R.row(QW)), (dkt, R.row(QW)), (dvt, R.row(QW)), (qa, R.row(QL)), (kva, R.row(QL)),
                   (qnw, R.const((1, QL))), (kvnw, R.const((1, QL))), (wq, R.const((QL, QW))), (wk, R.const((QL, QW))),
                   (wv, R.const((QL, QW))), (cos, tab), (sin, tab)],
                  [(_sds((B, T, QW), BF), R.row(QW)), (_sds((B, T, 2 * QW), BF), R.row(2 * QW))]
                  + [(_sds((B, T, QL), BF), R.row(QL))] * 2
                  + [(_sds((B, T, HP), BF), R.row(HP)), (_sds((8, QL), F32), R.const((8, QL)))], extra=8 << 20)


def _ssd_scan_bwd(B, T, dyt, xbc, dt2, alog2, hin_f, hin_b):
    nc, nctc = T // CH, CTX // CH

    def chain(d, dy_ref, xbc_ref, dt_ref, alog_ref, hin_ref, dxbc_ref, ddt_ref, da_ref, dhs_all):
        dhs = dhs_all.at[d]
        xbc_v = xbc_ref[...]
        dyv = dy_ref[...]
        a_all, dt_all, adt_all, q_all, inc, inc_t = _scan_common(d, dt_ref[...], alog_ref)
        ones_p = jnp.ones((P, HP), F32)
        total = lambda m: jnp.sum(jnp.sum(m, axis=0, keepdims=True), axis=1, keepdims=True)
        dq_parts, dqtot_parts, ddtx_parts = [], [], []
        for g in range(2):
            bg = xbc_v[:, SSD_IN + NST * g:SSD_IN + NST * (g + 1)]
            cg = xbc_v[:, SSD_IN + 2 * NST + NST * g:SSD_IN + 2 * NST + NST * (g + 1)]
            bg_t, cg_t = bg.T, cg.T
            gm = _dot(cg, bg_t)
            gm_t = _dot(bg, cg_t)
            dgm = jnp.zeros((CH, CH), F32)
            dgm_t = jnp.zeros((CH, CH), F32)
            dbg = jnp.zeros((CH, NST), F32)
            dcg = jnp.zeros((CH, NST), F32)
            for r in range(3):
                h = 3 * g + r
                lm, lm_t, eq, etot, dte = _head_decay(h, adt_all, q_all, inc, inc_t)
                eq_p, etot_p, dte_p = eq[:, :P], etot[:, :P], dte[:, :P]
                dt_p = dt_all[:, HP * h:HP * h + P]
                xs_h = xbc_v[:, P * h:P * (h + 1)]
                xh = xs_h * dt_p
                sm, sm_t = gm * lm, gm_t * lm_t
                dy_h = dyv[:, P * h:P * (h + 1)]
                hprev = hin_ref[P * h:P * (h + 1), :].T
                dho = dhs[:, P * h:P * (h + 1)]
                ds = _dot_nt(dy_h, xh)
                ds_t = _dot_nt(xh, dy_h)
                dx = _dot(sm_t, dy_h)
                edy = eq_p * dy_h
                yo = _dot(cg, hprev)
                dcg = dcg + _dot_nt(edy, hprev)
                dhin = _dot(cg_t, edy) + etot_p * dho
                zs = _dot(bg, dho)
                dx = dx + dte_p * zs
                wm = dte_p * xh * zs
                dbg = dbg + _dot_nt(xh * dte_p, dho)
                dgm = dgm + ds * lm
                dgm_t = dgm_t + ds_t * lm_t
                rs = jnp.sum(ds * sm - ds_t * sm_t, axis=1, keepdims=True) + jnp.sum(edy * yo - wm, axis=1, keepdims=True)
                dq_parts.append(jnp.broadcast_to(rs, (CH, HP)))
                dqtot_parts.append(total(hprev * dho) * etot + total(wm))
                dxbc_ref[:, P * h:P * (h + 1)] = dx * dt_p
                ddtx_parts.append(_dot01_r(dx * xs_h, ones_p, passes=2))
                dhs[:, P * h:P * (h + 1)] = dhin
            dcg = dcg + _dot(dgm, bg)
            dbg = dbg + _dot(dgm_t, cg)
            dxbc_ref[:, SSD_IN + NST * g:SSD_IN + NST * (g + 1)] = dbg
            dxbc_ref[:, SSD_IN + 2 * NST + NST * g:SSD_IN + 2 * NST + NST * (g + 1)] = dcg
        cat = lambda parts: jnp.concatenate(parts, axis=1)
        dadt_all = _dot01(inc_t, cat(dq_parts)) + cat(dqtot_parts)
        ddt_all = cat(ddtx_parts) + dadt_all * jnp.concatenate([a_all] * (CH // 8), axis=0)
        da_all = jnp.sum(dadt_all * dt_all, axis=0, keepdims=True)
        lane, lane1 = _iota((CH, HP), 1), _iota((1, HP), 1)
        ddt = jnp.zeros((CH, HP), F32)
        da = jnp.zeros((1, HP), F32)
        for h in range(H):
            ddt = ddt + jnp.where(lane == h, ddt_all[:, HP * h:HP * (h + 1)], 0.0)
            da = da + jnp.where(lane1 == h, da_all[:, HP * h:HP * (h + 1)], 0.0)
        ddt_ref[...] = ddt
        da_ref[d] += jnp.where(_iota((8, HP), 0) == 0, da, 0.0)

    def body(dyf_ref, dyb_ref, xf_ref, xb_ref, dtf_ref, dtb_ref, alog_ref, hf_ref, hb_ref,
             dxf_ref, dxb_ref, ddtf_ref, ddtb_ref, da_ref, dhs):
        @pl.when(pl.program_id(1) == 0)
        def _():
            dhs[...] = jnp.zeros(dhs.shape, F32)
            da_ref[...] = jnp.zeros(da_ref.shape, F32)

        chain(0, dyf_ref, xf_ref, dtf_ref, alog_ref, hf_ref, dxf_ref, ddtf_ref, da_ref, dhs)
        chain(1, dyb_ref, xb_ref, dtb_ref, alog_ref, hb_ref, dxb_ref, ddtb_ref, da_ref, dhs)

    cidx = lambda d: (lambda s: _chunk_index(d, nc - 1 - s, nc, nctc))
    specs = lambda d: dict(
        dy=pl.BlockSpec((None, CH, SSD_IN), lambda b, s: (b, cidx(d)(s), 0)),
        xbc=pl.BlockSpec((None, CH, XBC), lambda b, s: (b, cidx(d)(s), 0)),
        dt=pl.BlockSpec((None, None, CH, HP), lambda b, s: (b, d, cidx(d)(s), 0)),
        h=pl.BlockSpec((None, None, SSD_IN, NST), lambda b, s: (b, cidx(d)(s), 0, 0)),
        ddt=pl.BlockSpec((None, CH, HP), lambda b, s: (b, cidx(d)(s), 0)))
    f, r = specs(0), specs(1)
    bb = 2 * (2 * _nbytes((CH, XBC), F32) + 2 * _nbytes((CH, HP), F32) + _nbytes((CH, SSD_IN), F32) + _nbytes((SSD_IN, NST), F32))
    return pl.pallas_call(
        functools.partial(body), name="ssd_scan_bwd", grid=(B, nc),
        in_specs=[f["dy"], r["dy"], f["xbc"], r["xbc"], f["dt"], r["dt"], pl.BlockSpec((2, 8, HP), lambda b, s: (0, 0, 0)),
                  f["h"], r["h"]],
        out_specs=[f["xbc"], r["xbc"], f["ddt"], r["ddt"], pl.BlockSpec((None, 2, 8, HP), lambda b, s: (b, 0, 0, 0))],
        out_shape=[_out((B, T, XBC), F32)] * 2 + [_out((B, T, HP), F32)] * 2 + [_out((B, 2, 8, HP), F32)],
        scratch_shapes=[pltpu.VMEM((2, NST, SSD_IN), F32)],
        compiler_params=_params(("arbitrary",) * 2, bb, 16 << 20),
    )(*_hbm([dyt, dyt, xbc, xbc, dt2, dt2, alog2, hin_f, hin_b]))


def _ssd_prep_bwd(R, dxbc_f, dxbc_b, dxsk, ddt_f, ddt_b, xbc_raw, dt_raw, conv_w8, conv_b, dtb):
    B, T = R.B, R.T

    def body(dxf_ref, dxb_ref, dsk_ref, ddtf_ref, ddtb2_ref, raw_ref, pv_ref, nx_ref, dtr_ref, w_ref, b_ref, dtb_ref,
             dpre_ref, ddtr_ref, dcw_ref, dvec_ref, ddtb_ref):
        b, i = pl.program_id(0), pl.program_id(1)
        ext = R.ext(i, pv_ref, raw_ref[...], nx_ref)
        pre = _conv_pre(ext, w_ref, b_ref)
        dxbc = dxf_ref[...] + dxb_ref[...]
        skip = jnp.concatenate([dsk_ref[...], jnp.zeros((TM, XBC - SSD_IN), F32)], axis=1)
        dpre = (dxbc + skip) * _dsilu(pre)
        dpre_ref[...] = dpre
        first = jnp.logical_and(b == 0, i == 0)
        taps = {k: jnp.sum(dpre * _shift_rows(ext, k - 1)[8:8 + TM], axis=0, keepdims=True) for k in range(4)}
        _acc_rows(dcw_ref, first, taps)
        _acc_rows(dvec_ref, first, {0: jnp.sum(dpre, axis=0, keepdims=True)})
        ddt = ddtf_ref[...] + pltpu.roll(ddtb2_ref[...], H, axis=1)
        ddtr = ddt * _sigmoid(dtr_ref[...] + dtb_ref[...])
        ddtr = jnp.where(_iota((TM, HP), 1) < 2 * H, ddtr, 0.0)
        ddtr_ref[...] = ddtr.astype(BF)
        _acc_rows(ddtb_ref, first, {0: jnp.sum(ddtr, axis=0, keepdims=True)})

    return R.call("ssd_prep_bwd", body,
                  [(dxbc_f, R.row(XBC)), (dxbc_b, R.row(XBC)), (dxsk, R.row(SSD_IN)), (ddt_f, R.row(HP)), (ddt_b, R.row(HP)),
                   (xbc_raw, R.row(XBC)), (xbc_raw, R.prev8(XBC)),
                   (xbc_raw, R.next8(XBC)), (dt_raw, R.row(HP)), (conv_w8, R.const((8, XBC))), (conv_b, R.const((1, XBC))),
                   (dtb, R.const((1, HP)))],
                  [(_sds((B, T, XBC), F32), R.row(XBC)), (_sds((B, T, HP), BF), R.row(HP)), (_sds((8, XBC), F32), R.const((8, XBC))),
                   (_sds((8, XBC), F32), R.const((8, XBC))), (_sds((8, HP), F32), R.const((8, HP)))], extra=16 << 20)


def _in_bwd(R, dxm, x, mod, nw1, dz, dpre, dqa, dkva, dpool, dkr, ddtr, conv_w8, w_arr, latent_only=False):
    B, T = R.B, R.T
    dx_out = (_sds((B, T - CTX, D), F32), R.tgt(D)) if latent_only else (_sds((B, T, D), F32), R.row(D))

    xin = R.xin(x)

    def body(*refs):
        x_refs = refs[:len(xin)]
        (dxm_ref, mod_ref, nw_ref, dz_ref, dp_ref, dpp_ref, dpn_ref, dqa_ref, dkva_ref, dpool_ref, dkr_ref, ddt_ref,
         cw_ref, w_ref, dx_ref, dproj_ref, dmod_ref, dnw_ref) = refs[len(xin):]
        b, i = pl.program_id(0), pl.program_id(1)
        ext = R.ext(i, dpp_ref, dp_ref[...], dpn_ref)
        draw = (_row(cw_ref, 0) * _shift_rows(ext, 1)[8:8 + TM] + _row(cw_ref, 1) * ext[8:8 + TM]
                + _row(cw_ref, 2) * _shift_rows(ext, -1)[8:8 + TM] + _row(cw_ref, 3) * _shift_rows(ext, -2)[8:8 + TM])
        dproj_ref[:, A_Z:A_XBC] = dz_ref[...]
        dproj_ref[:, A_XBC:A_QA] = draw.astype(BF)
        dproj_ref[:, A_QA:A_KVA] = dqa_ref[...]
        dproj_ref[:, A_KVA:A_POOL] = dkva_ref[...]
        dproj_ref[:, A_POOL:A_KR] = dpool_ref[...]
        dproj_ref[:, A_KR:A_DT] = dkr_ref[...]
        dproj_ref[:, A_DT:PC] = ddt_ref[...]
        dh = _dot_nt(dproj_ref[...], w_ref[...])
        _, xn, r = _norm_mod(R.xval(i, x_refs), nw_ref[...], _row(mod_ref, 0), _row(mod_ref, 1))
        dx, dsh, dsc, dnw = _norm_mod_bwd(dh, xn, r, nw_ref[...], _row(mod_ref, 1))
        dx_ref[...] = dxm_ref[...] + dx
        _acc_rows(dmod_ref, R.first_of_stream(i), {0: dsh, 1: dsc})
        _acc_rows(dnw_ref, jnp.logical_and(b == 0, i == 0), {0: dnw})

    return R.call("in_bwd", body,
                  xin + [(dxm, R.row(D)), (mod, R.mod()), (nw1, R.const((1, D))), (dz, R.row(SSD_IN)), (dpre, R.row(XBC)),
                   (dpre, R.prev8(XBC)), (dpre, R.next8(XBC)), (dqa, R.row(QL)), (dkva, R.row(QL)), (dpool, R.row(PD)), (dkr, R.row(HP)),
                   (ddtr, R.row(HP)), (conv_w8, R.const((8, XBC))), (w_arr, R.const((D, PC)))],
                  [dx_out, (_sds((B, T, PC), BF), R.row(PC)), (_sds((B, 2, 8, D), F32), R.mod()),
                   (_sds((8, D), F32), R.const((8, D)))], extra=12 << 20)


def _adaln_fwd(cs, mod_w):
    L, _, C = mod_w.shape

    def body(c_ref, w_ref, o_ref):
        s = _silu(c_ref[...]).astype(BF)
        for l in range(L):
            o_ref[l] = jnp.dot(s, w_ref[l].astype(BF), preferred_element_type=F32)

    return pl.pallas_call(functools.partial(body), name="adaln_fwd", out_shape=_sds((L, 24, C), F32),
                          compiler_params=_params(None, _nbytes(mod_w.shape, F32) + _nbytes((L, 24, C), F32), 8 << 20))(cs, mod_w)


def _adaln_bwd(cs, dm, mod_w):
    L, _, C = mod_w.shape

    def body(c_ref, dm_ref, w_ref, gw_ref, gc_ref):
        c = c_ref[...]
        s = _silu(c).astype(BF)
        acc = jnp.zeros((24, D), F32)
        for l in range(L):
            dmb = dm_ref[l].astype(BF)
            gw_ref[l] = _dot_tn(s, dmb)
            acc = acc + _dot_nt(dmb, w_ref[l])
        gc_ref[...] = acc * _dsilu(c)

    return pl.pallas_call(functools.partial(body), name="adaln_bwd", out_shape=[_sds((L, D, C), F32), _sds((24, D), F32)],
                          compiler_params=_params(None, 2 * _nbytes(mod_w.shape, F32), 8 << 20))(cs, dm, mod_w)


def _sum_blocks(name, parts):
    Pn, Rr, C = parts.shape

    def body(p_ref, o_ref):
        acc = p_ref[0]
        for k in range(1, Pn):
            acc = acc + p_ref[k]
        o_ref[...] = acc

    return pl.pallas_call(functools.partial(body), name=name, out_shape=_sds((Rr, C), F32),
                          compiler_params=_params(None, _nbytes(parts.shape, F32), 4 << 20))(parts)


def _adamw(name, parts, w, m, v, rt):
    nseg = len(parts)
    Pn, rseg, C = parts[0].shape
    Rr = nseg * rseg
    tiles = rseg // rt
    assert rseg % rt == 0 and w.shape == (Rr, C)
    c1 = 1.0 - ADAM_B1 ** ADAM_STEP
    c2 = 1.0 - ADAM_B2 ** ADAM_STEP

    def body(*refs):
        p_refs = refs[:nseg]
        w_ref, m_ref, v_ref, g_ref, d_ref, nm_ref, nv_ref = refs[nseg:]
        i = pl.program_id(0)
        g = None
        for s, p_ref in enumerate(p_refs):
            gs = p_ref[0].astype(F32)
            for k in range(1, Pn):
                gs = gs + p_ref[k].astype(F32)
            g = gs if g is None else jnp.where(i >= s * tiles, gs, g)
        mn = ADAM_B1 * m_ref[...] + (1.0 - ADAM_B1) * g
        vn = ADAM_B2 * v_ref[...] + (1.0 - ADAM_B2) * jnp.square(g)
        g_ref[...] = g
        nm_ref[...] = mn
        nv_ref[...] = vn
        d_ref[...] = -ADAM_LR * ((mn / c1) / (jnp.sqrt(vn / c2) + ADAM_EPS) + ADAM_WD * w_ref[...])

    spec = pl.BlockSpec((rt, C), lambda i: (i, 0))
    pspec = lambda s: pl.BlockSpec((Pn, rt, C), lambda i: (0, jnp.clip(i - s * tiles, 0, tiles - 1), 0))
    bb = nseg * Pn * _nbytes((rt, C), parts[0].dtype) + 7 * _nbytes((rt, C), F32)
    return pl.pallas_call(
        functools.partial(body), name=name, grid=(Rr // rt,),
        in_specs=[pspec(s) for s in range(nseg)] + [spec, spec, spec],
        out_specs=[spec] * 4, out_shape=[_out((Rr, C), F32)] * 4,
        compiler_params=_params(("arbitrary",), bb, 4 << 20),
    )(*_hbm(list(parts) + [w, m, v]))


MESH = pl.DeviceIdType.MESH


def _my_pos():
    return lax.axis_index("x"), lax.axis_index("y"), lax.axis_index("c")


def _dev_index(x, y, c):
    return 4 * x + 2 * y + c


def _all_gather(name, shards):
    n = len(shards)

    def body(*refs):
        ins, outs = refs[:n], refs[n:2 * n]
        send_sems, recv_sems, local_sem = refs[2 * n:]
        x, y, c = _my_pos()
        me, sibling = (x, y, c), (x, y, 1 - c)
        chips = [(1 - x, y), (x, 1 - y), (1 - x, 1 - y)]

        def copy(t, k, block, to, src=None):
            slot = outs[t].at[_dev_index(*block)]
            return pltpu.make_async_remote_copy(
                src_ref=slot if src is None else src, dst_ref=slot,
                send_sem=send_sems.at[t, k], recv_sem=recv_sems.at[t, k], device_id=to, device_id_type=MESH)

        mine = [pltpu.make_async_copy(ins[t], outs[t].at[_dev_index(*me)], local_sem.at[t]) for t in range(n)]
        for cp in mine:
            cp.start()
        first = []
        for t in range(n):
            first.append(copy(t, 0, me, sibling, src=ins[t]))
            first += [copy(t, 1 + j, me, (*chip, c), src=ins[t]) for j, chip in enumerate(chips)]
        for cp in first:
            cp.start()
        passed = []
        for j, chip in enumerate(chips):
            for t in range(n):
                copy(t, 1 + j, (*chip, c), me).wait_recv()
                cp = copy(t, 4 + j, (*chip, c), sibling)
                cp.start()
                passed.append(cp)
        for t in range(n):
            copy(t, 0, sibling, me).wait_recv()
            for j, chip in enumerate(chips):
                copy(t, 4 + j, (*chip, 1 - c), me).wait_recv()
        for cp in first + passed:
            cp.wait_send()
        for cp in mine:
            cp.wait()

    return pl.pallas_call(
        functools.partial(body), name=name,
        in_specs=[ANY] * n, out_specs=[ANY] * n,
        out_shape=[_sds((NDEV,) + s.shape, s.dtype) for s in shards],
        scratch_shapes=[pltpu.SemaphoreType.DMA((n, 7)), pltpu.SemaphoreType.DMA((n, 7)), pltpu.SemaphoreType.DMA((n,))],
    )(*shards)


def _all_to_all(name, parts):
    n = len(parts)

    def body(*refs):
        ins, outs = refs[:n], refs[n:2 * n]
        send_sems, recv_sems, local_sem = refs[2 * n:]
        x, y, c = _my_pos()
        me = _dev_index(x, y, c)
        peers = [(x ^ ((k >> 2) & 1), y ^ ((k >> 1) & 1), c ^ (k & 1)) for k in range(1, NDEV)]
        mine = [pltpu.make_async_copy(ins[t].at[me], outs[t].at[me], local_sem.at[t]) for t in range(n)]
        for cp in mine:
            cp.start()
        sends = []
        for t in range(n):
            for k, peer in enumerate(peers):
                cp = pltpu.make_async_remote_copy(
                    src_ref=ins[t].at[_dev_index(*peer)], dst_ref=outs[t].at[me],
                    send_sem=send_sems.at[t, k], recv_sem=recv_sems.at[t, k], device_id=peer, device_id_type=MESH)
                cp.start()
                sends.append(cp)
        for t in range(n):
            for k, peer in enumerate(peers):
                slot = outs[t].at[_dev_index(*peer)]
                pltpu.make_async_remote_copy(
                    src_ref=slot, dst_ref=slot, send_sem=send_sems.at[t, k], recv_sem=recv_sems.at[t, k],
                    device_id=peer, device_id_type=MESH).wait_recv()
        for cp in sends:
            cp.wait_send()
        for cp in mine:
            cp.wait()

    return pl.pallas_call(
        functools.partial(body), name=name,
        in_specs=[ANY] * n, out_specs=[ANY] * n,
        out_shape=[_sds(p.shape, p.dtype) for p in parts],
        scratch_shapes=[pltpu.SemaphoreType.DMA((n, 7)), pltpu.SemaphoreType.DMA((n, 7)), pltpu.SemaphoreType.DMA((n,))],
    )(*parts)


SEM = pl.BlockSpec(memory_space=pltpu.SEMAPHORE)
IN_HBM = pl.BlockSpec(memory_space=pltpu.HBM)
DATAFLOW = pltpu.SideEffectType.DATAFLOW_SIDE_EFFECTING


def _flip_peers(x, y, c):
    return [(x ^ ((k >> 2) & 1), y ^ ((k >> 1) & 1), c ^ (k & 1)) for k in range(1, NDEV)]


def _split_copies(srcs, lands, send_sems, recv_sems, gather):
    x, y, c = _my_pos()
    me = _dev_index(x, y, c)
    out = []
    for t in range(len(srcs)):
        for k, peer in enumerate(_flip_peers(x, y, c)):
            p = _dev_index(*peer)
            src = srcs[t] if gather else srcs[t].at[p]
            sems = dict(send_sem=send_sems.at[7 * t + k], recv_sem=recv_sems.at[7 * t + k], device_id=peer, device_id_type=MESH)
            out.append((pltpu.make_async_remote_copy(src_ref=src, dst_ref=lands[t].at[me], **sems),
                        pltpu.make_async_remote_copy(src_ref=src, dst_ref=lands[t].at[p], **sems)))
    return out


def _exchange_start(name, collective_id, srcs, gather):
    n = len(srcs)
    lands = [lax.empty(((NDEV,) + s.shape) if gather else s.shape, s.dtype) for s in srcs]

    def body(*refs):
        src_refs, land_refs = refs[:n], refs[n:2 * n]
        send_sems, recv_sems = refs[2 * n], refs[2 * n + 1]
        token = refs[-1]
        barrier = pltpu.get_barrier_semaphore()
        for peer in _flip_peers(*_my_pos()):
            pl.semaphore_signal(barrier, inc=1, device_id=peer, device_id_type=MESH)
        pl.semaphore_wait(barrier, NDEV - 1)
        for send, _ in _split_copies(src_refs, land_refs, send_sems, recv_sems, gather):
            send.start()
        token[...] = jnp.zeros(token.shape, token.dtype)

    hbm = lambda a: pltpu.HBM(a.shape, a.dtype)
    res = pl.pallas_call(
        functools.partial(body), name=name,
        out_shape=[pltpu.SemaphoreType.DMA((7 * n,)), pltpu.SemaphoreType.DMA((7 * n,))] + [hbm(s) for s in srcs]
        + [hbm(a) for a in lands] + [_sds((8, HP), F32)],
        in_specs=[IN_HBM] * (2 * n), out_specs=[SEM, SEM] + [IN_HBM] * (2 * n) + [pl.BlockSpec(memory_space=pltpu.VMEM)],
        input_output_aliases={i: 2 + i for i in range(2 * n)},
        compiler_params=pltpu.CompilerParams(has_side_effects=DATAFLOW, collective_id=collective_id),
    )(*_hbm(list(srcs) + lands))
    return res[0], res[1], list(res[2:2 + n]), list(res[2 + n:2 + 2 * n]), res[-1]


def _exchange_wait(name, send_sems, recv_sems, srcs, lands, after, gather):
    n = len(srcs)

    def body(*refs):
        src_refs, land_refs = refs[:n], refs[n:2 * n]
        for _, recv in _split_copies(src_refs, land_refs, refs[2 * n], refs[2 * n + 1], gather):
            recv.wait_send()
            recv.wait_recv()

    hbm = lambda a: pltpu.HBM(a.shape, a.dtype)
    res = pl.pallas_call(
        functools.partial(body), name=name,
        out_shape=[hbm(s) for s in srcs] + [hbm(a) for a in lands],
        in_specs=[IN_HBM] * (2 * n) + [SEM, SEM, ANY], out_specs=[IN_HBM] * (2 * n),
        input_output_aliases={i: i for i in range(2 * n)},
        compiler_params=pltpu.CompilerParams(has_side_effects=DATAFLOW),
    )(*srcs, *lands, send_sems, recv_sems, after)
    return list(res[:n]), list(res[n:])


def _with_own(lands, own_blocks, me):
    out = []
    for land, own in zip(lands, own_blocks):
        out.append(lax.dynamic_update_slice(land, own[None], (me,) + (0,) * own.ndim))
    return out


def _arrange_w_in(w):
    z = lambda n: jnp.zeros((w.shape[0], n), w.dtype)
    return jnp.concatenate([w[:, 0:1280], w[:, 1292:1548], w[:, 1548:1804], w[:, 1836:2092],
                            z(64), w[:, 1804:1836], z(32), w[:, 1280:1292], z(HP - 2 * H)], axis=1)


def _unarrange_w_in(g):
    return jnp.concatenate([g[:, 0:1280], g[:, A_DT:A_DT + 2 * H], g[:, A_QA:A_KVA], g[:, A_KVA:A_POOL],
                            g[:, A_KR + NOPE:A_KR + NOPE + ROPE], g[:, A_POOL:A_KR]], axis=1)


def _pad_heads(w, width):
    k = w.shape[0]
    return jnp.pad(w.reshape(k, H, width), ((0, 0), (0, 0), (0, HP - width))).reshape(k, H * HP)


def _unpad_heads(g, width):
    k = g.shape[0]
    return g.reshape(k, H, HP)[:, :, :width].reshape(k, H * width)


def _arrange_w_out(w):
    att = jnp.pad(w[SSD_IN:2 * SSD_IN].reshape(H, VH, D), ((0, 0), (0, HP - VH), (0, 0))).reshape(QW, D)
    return jnp.concatenate([w[0:SSD_IN], att, w[2 * SSD_IN:]], axis=0)


def _unarrange_w_out(g):
    att = g[SSD_IN:SSD_IN + QW].reshape(H, HP, D)[:, :VH].reshape(SSD_IN, D)
    return jnp.concatenate([g[0:SSD_IN], att, g[SSD_IN + QW:]], axis=0)


def _rope_tables(T):
    n = T - CTX
    rows = n // GRID_W
    pairs = ROPE // 4
    inv = ROPE_THETA ** (-jnp.arange(pairs, dtype=F32) / pairs)
    ar = jnp.arange(rows, dtype=F32)[:, None] * inv
    ac = jnp.arange(GRID_W, dtype=F32)[:, None] * inv
    by_row = lambda a: jnp.repeat(a, GRID_W, axis=0)
    by_col = lambda a: jnp.tile(a, (rows, 1))
    cos = jnp.concatenate([by_row(jnp.cos(ar))] * 2 + [by_col(jnp.cos(ac))] * 2, axis=1)
    sin = jnp.concatenate([-by_row(jnp.sin(ar)), by_row(jnp.sin(ar)), -by_col(jnp.sin(ac)), by_col(jnp.sin(ac))], axis=1)
    ones, zeros = jnp.ones((n, NOPE), F32), jnp.zeros((n, NOPE), F32)
    cos = jnp.concatenate([ones, cos, ones[:, :HP - QK]], axis=1)
    sin = jnp.concatenate([zeros, sin, zeros[:, :HP - QK]], axis=1)
    return (jnp.concatenate([jnp.ones((CTX, HP), F32), cos], axis=0),
            jnp.concatenate([jnp.zeros((CTX, HP), F32), sin], axis=0))


def _lane_pad(v, n):
    return jnp.pad(v, (0, n - v.shape[0]))[None, :]


def _layer_weights(w_in, w_q_b, w_kv_b, conv_w, pool_w):
    kv = w_kv_b.reshape(QL, H, NOPE + VH)
    wbd = jnp.concatenate([jnp.pad(pool_w[g], ((0, 0), (64 * g, PD - 64 * (g + 1)))) for g in range(4)], axis=0)
    return dict(
        w_in=_arrange_w_in(w_in).astype(BF),
        wq=_pad_heads(w_q_b, QK).astype(BF),
        wk=_pad_heads(kv[:, :, :NOPE].reshape(QL, H * NOPE), NOPE).astype(BF),
        wv=_pad_heads(kv[:, :, NOPE:].reshape(QL, H * VH), VH).astype(BF),
        conv_w8=jnp.pad(conv_w, ((0, 4), (0, 0))), wbd=wbd)


def _layer_fwd(R, x, mod, lw, sp, late, cos, sin):
    B, T = R.B, R.T
    z, xbc_raw, qa, kva, pool_in, kr, dt_raw, h1 = _in_proj(R, x, mod, sp["nw1"], lw["w_in"])
    xbc, dt2 = _ssd_prep(R, xbc_raw, dt_raw, lw["conv_w8"], sp["conv_b"], sp["dtb"])
    yf, yb, hin_f, hin_b = _ssd_scan(B, T, xbc, dt2, sp["alog2"])
    qt, kt, q_t, k_t, v_t, cq, ckv = _mla_prep(R, qa, kva, kr, sp["qnw"], sp["kvnw"], lw["wq"], lw["wk"], lw["wv"], cos, sin)
    o, lse = _flash_fwd(B, T, q_t, kt, v_t)
    w_out, w1, w2 = late(o)
    w_out = _arrange_w_out(w_out).astype(BF)
    xmid, cat, mixbf = _out_proj(R, x, mod, yf, yb, xbc, z, o, pool_in, lw["wbd"], sp["pscale"], sp["dsk"], sp["snw"], w_out)
    xo, h2, ubf, ybf = _mlp_fwd(R, xmid, mod, sp["nw2"], w1, w2)
    saved = dict(x=x, z=z, xbc_raw=xbc_raw, qa=qa, kva=kva, pool_in=pool_in, dt_raw=dt_raw, h1=h1, xbc=xbc, dt2=dt2,
                 yf=yf, yb=yb, hin_f=hin_f, hin_b=hin_b, qt=qt, kt=kt, q_t=q_t, k_t=k_t, v_t=v_t, cq=cq, ckv=ckv, o=o, lse=lse,
                 cat=cat, mixbf=mixbf, xmid=xmid, h2=h2, ubf=ubf, ybf=ybf, w_out=w_out, w1=w1, w2=w2)
    return xo, saved


def _layer_bwd(R, dxo, sv, mod, lw, sp, cos, sin, on_mlp=None, on_mid=None, latent_only=False):
    B, T = R.B, R.T
    dxm, du, abf, dyb, dmod_a, dnw2 = _mlp_bwd(R, dxo, sv["xmid"], sv["ubf"], sv["ybf"], mod, sp["nw2"], sv["w1"], sv["w2"])
    g_w1 = _tn_matmul("dw_mlp1", sv["h2"], du, 4 * FSH, sub=4)
    g_w2 = _tn_matmul("dw_mlp2", abf, dyb, D)[0].reshape(NDEV, FSH, D)
    snw = sp["snw"]
    tok = on_mlp(g_w1, g_w2) if on_mlp is not None else None
    if tok is not None:
        snw = snw + tok
    dmix, dz, dyt, dxsk, qs, dos, do_t, dpo, dmod_b, dvec_o = _out_bwd(R, dxm, mod, sv["mixbf"], sv["yf"], sv["yb"], sv["xbc"], sv["z"],
                                                                 sv["o"], sv["qt"], sv["lse"], sp["dsk"], snw, sv["w_out"])
    g_wout = _unarrange_w_out(_tn_matmul("dw_out", sv["cat"], dmix, D)[0])
    dpool_in, g_wbd, dpsc = _pool_bwd(R, dpo, sv["pool_in"], lw["wbd"], sp["pscale"])
    dqt, dkt, dvt = _flash_bwd(B, T, qs, sv["kt"], dos, sv["q_t"], sv["k_t"], sv["v_t"], do_t)
    dqp, dkvb, dqa, dkva, dkr, dnw_qk = _mla_bwd(R, dqt, dkt, dvt, sv["qa"], sv["kva"], sp["qnw"], sp["kvnw"],
                                                 lw["wq"], lw["wk"], lw["wv"], cos, sin)
    g_wq = _unpad_heads(_tn_matmul("dw_q", sv["cq"], dqp, QW)[0], QK)
    g_kv = _tn_matmul("dw_kv", sv["ckv"], dkvb, 2 * QW)[0]
    g_wk = _unpad_heads(g_kv[:, :QW], NOPE).reshape(QL, H, NOPE)
    g_wv = _unpad_heads(g_kv[:, QW:], VH).reshape(QL, H, VH)
    g_wkv = jnp.concatenate([g_wk, g_wv], axis=2).reshape(QL, H * (NOPE + VH))
    alog2 = sp["alog2"]
    tok = on_mid(g_wout, g_wq, g_wkv) if on_mid is not None else None
    if tok is not None:
        alog2 = alog2 + tok
    dxbc_f, dxbc_b, ddt_f, ddt_b, da = _ssd_scan_bwd(B, T, dyt, sv["xbc"], sv["dt2"], alog2, sv["hin_f"], sv["hin_b"])
    dpre, ddtr, dcw, dcb, ddtb = _ssd_prep_bwd(R, dxbc_f, dxbc_b, dxsk, ddt_f, ddt_b, sv["xbc_raw"], sv["dt_raw"], lw["conv_w8"],
                                                sp["conv_b"], sp["dtb"])
    dx, dproj, dmod_c, dnw1 = _in_bwd(R, dxm, sv["x"], mod, sp["nw1"], dz, dpre, dqa, dkva, dpool_in, dkr, ddtr,
                                      lw["conv_w8"], lw["w_in"], latent_only)
    g_win = _unarrange_w_in(jnp.concatenate(list(_tn_matmul("dw_in", sv["h1"], dproj, PC)), axis=1))
    a2 = -jnp.exp(sp["alog2"][:, 0, :H])
    small = dict(
        norm1_w=dnw1[0], norm2_w=dnw2[0], conv_w=dcw[0:4], conv_b=dcb[0], dt_bias=ddtb[0, :2 * H].reshape(2, H),
        a_log=jnp.sum(da[:, :, 0, :H], axis=0) * a2, ssd_d=jnp.sum(dvec_o[1].reshape(H, P), axis=1), ssd_norm_w=dvec_o[0],
        q_a_norm_w=dnw_qk[0], kv_a_norm_w=dnw_qk[1],
        pool_w=jnp.stack([g_wbd[64 * g:64 * (g + 1), 64 * g:64 * (g + 1)] for g in range(4)]), pool_scale=dpsc[0])
    big = dict(w_in=g_win, w_out=g_wout, w_q_b=g_wq, w_kv_b=g_wkv, w_mlp1=g_w1, w_mlp2=g_w2)
    return dx, big, small, dmod_a + dmod_b + dmod_c


def _small_params(l, norm1_w, norm2_w, conv_b, dt_bias, a_log, ssd_d, ssd_norm_w, q_a_norm_w, kv_a_norm_w, pool_scale):
    alog2 = jnp.broadcast_to(jnp.pad(a_log[l], ((0, 0), (0, HP - H)))[:, None, :], (2, 8, HP))
    return dict(nw1=norm1_w[l][None], nw2=norm2_w[l][None], conv_b=conv_b[l][None],
                dtb=_lane_pad(dt_bias[l].reshape(2 * H), HP), alog2=alog2,
                dsk=jnp.repeat(ssd_d[l], P)[None], snw=ssd_norm_w[l][None], qnw=q_a_norm_w[l][None],
                kvnw=kv_a_norm_w[l][None], pscale=pool_scale[l][None])


SMALL_NAMES = ["mod_b", "norm1_w", "norm2_w", "conv_b", "dt_bias", "a_log", "ssd_d", "ssd_norm_w", "q_a_norm_w",
               "kv_a_norm_w", "pool_w", "pool_scale", "final_norm_w"]


def _pack(arrs):
    rows = []
    for a in arrs:
        f = a.reshape(-1).astype(F32)
        n = -(-f.shape[0] // HP) * HP
        rows.append(jnp.pad(f, (0, n - f.shape[0])).reshape(-1, HP))
    out = jnp.concatenate(rows, axis=0)
    pad = (-out.shape[0]) % 8
    return jnp.pad(out, ((0, pad), (0, 0)))


def _unpack(pack, like):
    outs, r = [], 0
    for a in like:
        n = math.prod(a.shape)
        nr = -(-n // HP)
        outs.append(pack[r:r + nr].reshape(-1)[:n].reshape(a.shape))
        r += nr
    return outs


def _local_step(x, ctx, target, mods, full_of, small_w, on_grads=None, on_mlp=None, on_mid=None):
    B, N = x.shape[0], x.shape[1]
    T = CTX + N
    R = _Rows(B, T)
    cos, sin = _rope_tables(T)
    xu = (ctx, x)
    L = len(mods)
    lws, sps, saves = [], [], []
    for l in range(L):
        f = full_of(l, xu)
        lws.append(_layer_weights(f["w_in"], f["w_q_b"], f["w_kv_b"], f["conv_w"], small_w["pool_w"][l]))
        sps.append(_small_params(l, *[small_w[k] for k in ["norm1_w", "norm2_w", "conv_b", "dt_bias", "a_log", "ssd_d",
                                                          "ssd_norm_w", "q_a_norm_w", "kv_a_norm_w", "pool_scale"]]))
        xu, sv = _layer_fwd(R, xu, mods[l], lws[l], sps[l], f["late"], cos, sin)
        saves.append(sv)
    dx, loss8, dfnw = _loss_head(R, xu, target, small_w["final_norm_w"][None])
    bigs, smalls, dmods = [None] * L, [None] * L, [None] * L
    for l in reversed(range(L)):
        hook = functools.partial(on_mlp, l) if on_mlp is not None else None
        hook2 = functools.partial(on_mid, l) if on_mid is not None else None
        dx, bigs[l], smalls[l], dmods[l] = _layer_bwd(R, dx, saves[l], mods[l], lws[l], sps[l], cos, sin, hook, hook2, l == 0)
        if on_grads is not None:
            bigs[l], tok = on_grads(l, bigs[l], dx)
            if tok is not None:
                sps[l - 1] = dict(sps[l - 1], nw2=sps[l - 1]["nw2"] + tok)
    return loss8[0, 0], dx, bigs, smalls, dfnw[0], dmods


def kernel(x, c, ctx, c_ctx, mod_w, mod_b, norm1_w, norm2_w, w_in, conv_w, conv_b, dt_bias, a_log, ssd_d, ssd_norm_w, q_a_norm_w, w_q_b, kv_a_norm_w, w_kv_b, pool_w, pool_scale, w_out, w_mlp1, w_mlp2, final_norm_w, loss_target, m_c_ctx, m_mod_w, m_mod_b, m_norm1_w, m_norm2_w, m_w_in, m_conv_w, m_conv_b, m_dt_bias, m_a_log, m_ssd_d, m_ssd_norm_w, m_q_a_norm_w, m_w_q_b, m_kv_a_norm_w, m_w_kv_b, m_pool_w, m_pool_scale, m_w_out, m_w_mlp1, m_w_mlp2, m_final_norm_w, v_c_ctx, v_mod_w, v_mod_b, v_norm1_w, v_norm2_w, v_w_in, v_conv_w, v_conv_b, v_dt_bias, v_a_log, v_ssd_d, v_ssd_norm_w, v_q_a_norm_w, v_w_q_b, v_kv_a_norm_w, v_w_kv_b, v_pool_w, v_pool_scale, v_w_out, v_w_mlp1, v_w_mlp2, v_final_norm_w):
    args = dict(locals())
    B = x.shape[0]
    L = mod_w.shape[0]
    me = _dev_index(*_my_pos())
    CS = mod_w.shape[2]

    big_names = ["w_in", "w_out", "w_q_b", "w_kv_b", "w_mlp1", "w_mlp2"]
    shards = {n: args[n].astype(BF) for n in big_names}
    early, late_names = ["w_in", "w_q_b", "w_kv_b"], ["w_out", "w_mlp1", "w_mlp2"]
    g0 = _all_gather("gather_weights", [c, conv_w] + [shards[n][0] for n in early])
    c_all, convw_all = g0[0], g0[1]
    gathered = {0: dict(zip(early, g0[2:]))}
    cs = jnp.concatenate([c_all.reshape(NDEV * B, D), c_ctx[None], jnp.zeros((24 - NDEV * B - 1, D), F32)], axis=0)
    m_loc = _adaln_fwd(cs, mod_w)
    m_all = _all_gather("gather_mod", [m_loc])[0]
    m_full = jnp.moveaxis(m_all, 0, 2).reshape(L, 24, NDEV * CS) + mod_b[:, None, :]
    pending = {}
    tok = jnp.zeros((), F32)
    res = _exchange_start("gather_start_0", 1, [shards[n][0] for n in late_names], gather=True)
    pending[0] = res[:4]
    tok = tok + res[4][0, 0]
    for l in range(1, L):
        res = _exchange_start("gather_start_%d" % l, 1 + l, [shards[n][l] for n in early + late_names], gather=True)
        pending[l] = res[:4]
        tok = tok + res[4][0, 0]
    mods = []
    for l in range(L):
        ex = lax.dynamic_slice(m_full[l], (me * B, 0), (B, 6 * D)).reshape(B, 6, D)
        cc = jnp.broadcast_to(m_full[l, NDEV * B].reshape(1, 6, D), (B, 6, D))
        mods.append(jnp.pad(jnp.stack([cc, ex], axis=1), ((0, 0), (0, 0), (0, 2), (0, 0))) + tok)

    def full_of(l, xu):
        if l > 0:
            own, lands = _exchange_wait("gather_wait_%d" % l, *pending.pop(l), xu, gather=True)
            gathered[l] = dict(zip(early + late_names, _with_own(lands, own, me)))

        def late(after):
            if l == 0:
                own, lands = _exchange_wait("gather_wait_0", *pending.pop(0), after, gather=True)
                gathered[0].update(zip(late_names, _with_own(lands, own, me)))
            gl = gathered[l]
            return gl["w_out"].reshape(D, D), gl["w_mlp1"], gl["w_mlp2"]

        g = gathered[l]
        return dict(
            w_in=g["w_in"].reshape(D, IN_COLS),
            w_q_b=jnp.moveaxis(g["w_q_b"], 0, 1).reshape(QL, H * QK),
            w_kv_b=jnp.moveaxis(g["w_kv_b"], 0, 1).reshape(QL, H * (NOPE + VH)),
            conv_w=jnp.moveaxis(convw_all[:, l], 0, 1).reshape(4, XBC), late=late)

    def grad_blocks(big, names):
        make = dict(
            w_in=lambda g: g.reshape(NDEV, D // NDEV, IN_COLS), w_out=lambda g: g.reshape(NDEV, D // NDEV, D),
            w_q_b=lambda g: jnp.moveaxis(g.reshape(QL, NDEV, -1), 1, 0),
            w_kv_b=lambda g: jnp.moveaxis(g.reshape(QL, NDEV, -1), 1, 0), w_mlp1=lambda g: g, w_mlp2=lambda g: g)
        return [make[n](big[n]).astype(BF) for n in names]

    sent, sent_mlp, sent_mid = {}, {}, {}
    rest_names, mid_names = ["w_in"], ["w_out", "w_q_b", "w_kv_b"]

    def on_mid(l, g_wout, g_wq, g_wkv):
        if l > 0:
            return None
        blocks_mid = grad_blocks(dict(w_out=g_wout, w_q_b=g_wq, w_kv_b=g_wkv), mid_names)
        res = _exchange_start("grads_mid_start_0", 2 + 2 * L, blocks_mid, gather=False)
        sent_mid[l] = res[:4]
        return res[4][0, 0]

    def on_mlp(l, g_w1, g_w2):
        if l > 0:
            return None
        res = _exchange_start("grads_mlp_start_0", 1 + 2 * L, [g_w1.astype(BF), g_w2.astype(BF)], gather=False)
        sent_mlp[l] = res[:4]
        return res[4][0, 0]

    def on_grads(l, big, dx):
        if l == 0:
            return grad_blocks(big, rest_names), None
        res = _exchange_start("grads_start_%d" % l, 1 + L + l, grad_blocks(big, big_names), gather=False)
        sent[l] = res[:4]
        return None, res[4][0, 0]

    small_w = {k: args[k] for k in SMALL_NAMES if k != "mod_b"}
    loss_part, grad_x, blocks, smalls, dfnw, dmods = _local_step(x, ctx, loss_target, mods, full_of, small_w, on_grads, on_mlp,
                                                                 on_mid)
    loss = lax.psum(loss_part, ("x", "y", "c"))

    dm_ex = jnp.stack([dmods[l][:, 1, :6].reshape(B, 6 * D) for l in range(L)])
    dm_cc = jnp.stack([jnp.sum(dmods[l][:, 0, :6], axis=0).reshape(6 * D) for l in range(L)])
    small_parts = dict(
        mod_b=jnp.sum(dm_ex, axis=1) + dm_cc,
        **{k: jnp.stack([smalls[l][k] for l in range(L)]) for k in SMALL_NAMES[1:-1]},
        final_norm_w=dfnw, conv_w=jnp.stack([smalls[l]["conv_w"] for l in range(L)]), dm_cc=dm_cc)
    adam_grads = [small_parts[k] for k in SMALL_NAMES]
    extras = [small_parts["conv_w"], dm_cc, dm_ex]
    pack = jnp.concatenate([_pack(adam_grads), _pack(extras)], axis=0)
    pack_all = _all_gather("gather_small_grads", [pack])[0]
    wpack = _pack([args[k] for k in SMALL_NAMES])
    mpack = _pack([args["m_" + k] for k in SMALL_NAMES])
    vpack = _pack([args["v_" + k] for k in SMALL_NAMES])
    n_adam = wpack.shape[0]
    res_small = _adamw("adamw_small", [pack_all[:, :n_adam]], wpack, mpack, vpack, n_adam)
    small_out = [_unpack(r, [args[k] for k in SMALL_NAMES]) for r in res_small]
    ext_all = pack_all[:, n_adam:]
    g_conv_full, dm_cc_tot, _ = _unpack(_sum_blocks("sum_small_grads", ext_all), extras)
    dm_ex_all = jnp.stack([_unpack(ext_all[k], extras)[2] for k in range(NDEV)], axis=1)
    dm_rows = jnp.concatenate([dm_ex_all.reshape(L, NDEV * B, 6 * D), dm_cc_tot[:, None, :],
                               jnp.zeros((L, 24 - NDEV * B - 1, 6 * D), F32)], axis=1)
    dm_loc = lax.dynamic_slice(dm_rows, (0, 0, me * CS), (L, 24, CS))
    g_modw, gc_part = _adaln_bwd(cs, dm_loc, mod_w)
    gc_all = _all_gather("gather_cctx_grad", [gc_part[NDEV * B:NDEV * B + 8]])[0]
    cpad = lambda a: jnp.pad(a[None], ((0, 7), (0, 0)))
    res_cc = _adamw("adamw_cctx", [gc_all], cpad(c_ctx), cpad(m_c_ctx), cpad(v_c_ctx), 8)
    cc_out = [r[0] for r in res_cc]

    def waited(name, handles, names):
        srcs, lands = _exchange_wait(name, *handles, grad_x, gather=False)
        own = [lax.dynamic_index_in_dim(s, me, 0, keepdims=False) for s in srcs]
        return dict(zip(names, _with_own(lands, own, me)))

    recv = {0: dict(zip(rest_names, _all_to_all("exchange_grads", blocks[0])))}
    recv[0].update(waited("grads_mlp_wait_0", sent_mlp.pop(0), ["w_mlp1", "w_mlp2"]))
    recv[0].update(waited("grads_mid_wait_0", sent_mid.pop(0), mid_names))
    for l in range(1, L):
        recv[l] = waited("grads_wait_%d" % l, sent.pop(l), big_names)
    big_out = {}
    for name in big_names:
        w = args[name]
        Rr, C = math.prod(w.shape[:-1]), w.shape[-1]
        rl = Rr // L
        rt = rl if rl * C <= (1 << 17) else rl // 4
        res = _adamw("adamw_" + name, [recv[l][name].reshape(NDEV, rl, C) for l in range(L)], w.reshape(Rr, C),
                     args["m_" + name].reshape(Rr, C), args["v_" + name].reshape(Rr, C), rt)
        big_out[name] = [r.reshape(w.shape) for r in res]
    res = _adamw("adamw_mod_w", [g_modw.reshape(1, L * D, CS)], mod_w.reshape(L * D, CS), m_mod_w.reshape(L * D, CS),
                 v_mod_w.reshape(L * D, CS), L * D // 8)
    big_out["mod_w"] = [r.reshape(mod_w.shape) for r in res]
    CW = conv_w.shape[2]
    g_conv = lax.dynamic_slice(g_conv_full, (0, 0, me * CW), (L, 4, CW))
    res = _adamw("adamw_conv_w", [g_conv.reshape(1, L * 4, CW)], conv_w.reshape(L * 4, CW), m_conv_w.reshape(L * 4, CW),
                 v_conv_w.reshape(L * 4, CW), L * 4)
    big_out["conv_w"] = [r.reshape(conv_w.shape) for r in res]

    weights = ["c_ctx", "mod_w", "mod_b", "norm1_w", "norm2_w", "w_in", "conv_w", "conv_b", "dt_bias", "a_log", "ssd_d",
               "ssd_norm_w", "q_a_norm_w", "w_q_b", "kv_a_norm_w", "w_kv_b", "pool_w", "pool_scale", "w_out", "w_mlp1",
               "w_mlp2", "final_norm_w"]
    outs = [loss, grad_x]
    for kind in range(4):
        for name in weights:
            if name == "c_ctx":
                outs.append(cc_out[kind])
            elif name in big_out:
                outs.append(big_out[name][kind])
            else:
                outs.append(small_out[kind][SMALL_NAMES.index(name)])
    return tuple(outs)
```

```python
import functools
import math

import jax
import jax.numpy as jnp
from jax import lax
from jax.experimental import pallas as pl
from jax.experimental.pallas import tpu as pltpu

F32 = jnp.float32
BF = jnp.bfloat16
MXU = BF

D = 1024
CTX = 256
GRID_W = 64
EPS = 1e-6
H = 6
P = 64
SSD_IN = 384
NST = 128
XBC = 896
CH = 128
QL = 256
NOPE = 64
ROPE = 32
VH = 64
QK = 96
HP = 128
QW = H * HP
PD = 256
FF = 4096
IN_COLS = 2092
ROPE_THETA = 10000.0
PC = 2304
A_Z, A_XBC, A_QA, A_KVA, A_POOL, A_KR, A_DT = 0, 384, 1280, 1536, 1792, 2048, 2176
MIXW = SSD_IN + QW + PD
NDEV = 8
FSH = FF // NDEV
TM = 256
TQ = 256
TK = 256
VMEM_CAP = 64 * 1024 * 1024
ADAM_LR, ADAM_B1, ADAM_B2, ADAM_EPS, ADAM_WD, ADAM_STEP = 0.001, 0.9, 0.999, 1e-08, 0.01, 10


def _nbytes(shape, dtype):
    n = 1
    for s in shape:
        if s is not None:
            n *= s
    return n * jnp.dtype(dtype).itemsize


def _params(sem, block_bytes, extra=0):
    lim = min(2 * block_bytes + extra + (8 << 20), VMEM_CAP - (6 << 20))
    return pltpu.CompilerParams(dimension_semantics=sem, vmem_limit_bytes=int(lim))


def _hbm(arrays):
    return [pltpu.with_memory_space_constraint(a, pltpu.HBM) for a in arrays]


def _dot(a, b):
    return jnp.dot(a.astype(MXU), b.astype(MXU), preferred_element_type=F32)


def _dot_nt(a, b):
    return lax.dot_general(a.astype(MXU), b.astype(MXU), (((1,), (1,)), ((), ())), preferred_element_type=F32)


def _dot_tn(a, b):
    return lax.dot_general(a.astype(MXU), b.astype(MXU), (((0,), (0,)), ((), ())), preferred_element_type=F32)


def _dot01(m01, x):
    b16 = jnp.bfloat16
    m = m01.astype(b16)
    hi = x.astype(b16)
    r1 = x - hi.astype(F32)
    mid = r1.astype(b16)
    lo = (r1 - mid.astype(F32)).astype(b16)
    f = lambda v: jnp.dot(m, v, preferred_element_type=F32)
    return f(hi) + f(mid) + f(lo)


def _sigmoid(x):
    return 1.0 / (1.0 + jnp.exp(-x))


def _silu(x):
    return x * _sigmoid(x)


def _dsilu(x):
    s = _sigmoid(x)
    return s * (1.0 + x * (1.0 - s))


def _iota(shape, dim):
    return lax.broadcasted_iota(jnp.int32, shape, dim)


def _row(ref, k):
    blk = ref[...]
    return jnp.sum(jnp.where(_iota(blk.shape, 0) == k, blk, 0.0), axis=0, keepdims=True)


def _shift_rows(x, k):
    n = x.shape[0]
    return pltpu.roll(x, (-k) % n, axis=0)


class _Rows:
    def __init__(self, B, T):
        self.B, self.T = B, T
        self.nt = T // TM
        self.nct = CTX // TM

    def row(self, F):
        return pl.BlockSpec((None, TM, F), lambda b, i: (b, i, 0))

    def row2(self, F):
        return pl.BlockSpec((None, 2, TM, F), lambda b, i: (b, 0, i, 0))

    def prev8(self, F):
        return pl.BlockSpec((None, 8, F), lambda b, i: (b, jnp.maximum(i * (TM // 8) - 1, 0), 0))

    def next8(self, F):
        last = self.T // 8 - 1
        return pl.BlockSpec((None, 8, F), lambda b, i: (b, jnp.minimum((i + 1) * (TM // 8), last), 0))

    def mod(self):
        nct = self.nct
        return pl.BlockSpec((None, None, 8, D), lambda b, i: (b, jnp.where(i < nct, 0, 1), 0, 0))

    def const(self, shape):
        z = (0,) * len(shape)
        return pl.BlockSpec(tuple(shape), lambda b, i: z)

    def tgt(self, F):
        nct = self.nct
        return pl.BlockSpec((None, TM, F), lambda b, i: (b, jnp.maximum(i - nct, 0), 0))

    def xin(self, x):
        if isinstance(x, tuple):
            nct = self.nct
            return [(x[0], pl.BlockSpec((None, TM, D), lambda b, i: (b, jnp.minimum(i, nct - 1), 0))), (x[1], self.tgt(D))]
        return [(x, self.row(D))]

    def xval(self, i, refs):
        if len(refs) == 2:
            return jnp.where(i < self.nct, refs[0][...], refs[1][...])
        return refs[0][...]

    def call(self, name, body, ins, outs, scratch=(), extra=0):
        arrays = [a for a, _ in ins]
        in_specs = [s for _, s in ins]
        out_shape = [pltpu.HBM(o.shape, o.dtype) for o, _ in outs]
        out_specs = [s for _, s in outs]
        bb = 0
        for a, s in list(ins) + list(outs):
            if s.block_shape is not None:
                bb += _nbytes(s.block_shape, a.dtype)
        return pl.pallas_call(
            functools.partial(body), name=name, grid=(self.B, self.nt),
            in_specs=in_specs, out_specs=out_specs, out_shape=out_shape, scratch_shapes=list(scratch),
            compiler_params=_params(("arbitrary", "arbitrary"), bb, extra),
        )(*_hbm(arrays))

    def first_of_stream(self, i):
        return jnp.logical_or(i == 0, i == self.nct)

    def last_of_stream(self, i):
        return jnp.logical_or(i == self.nct - 1, i == self.nt - 1)

    def ext(self, i, prev_ref, cur, next_ref):
        pv = prev_ref[...].astype(F32) * jnp.where(self.first_of_stream(i), 0.0, 1.0)
        nx = next_ref[...].astype(F32) * jnp.where(self.last_of_stream(i), 0.0, 1.0)
        return jnp.concatenate([pv, cur, nx], axis=0)

    def stream_pos(self, i, rows):
        start = jnp.where(i < self.nct, 0, CTX)
        n = jnp.where(i < self.nct, CTX, self.T - CTX)
        t = i * TM - 8 - start + _iota((rows, 1), 0)
        return t, n


def _sds(shape, dtype):
    return jax.ShapeDtypeStruct(tuple(shape), dtype)


def _out(shape, dtype):
    return pltpu.HBM(tuple(shape), dtype)


def _norm_mod(x, nw, sh, sc):
    r = lax.rsqrt(jnp.mean(x * x, axis=-1, keepdims=True) + EPS)
    xn = x * r
    return xn * nw * (1.0 + sc) + sh, xn, r


def _norm_mod_bwd(dh, xn, r, nw, sc):
    dsh = jnp.sum(dh, axis=0, keepdims=True)
    dsc = jnp.sum(dh * (xn * nw), axis=0, keepdims=True)
    dnw = jnp.sum(dh * (1.0 + sc) * xn, axis=0, keepdims=True)
    dxn = dh * nw * (1.0 + sc)
    dx = r * (dxn - xn * jnp.mean(dxn * xn, axis=-1, keepdims=True))
    return dx, dsh, dsc, dnw


def _acc_rows(ref, first, rows):
    rid = _iota(ref.shape, 0)
    upd = jnp.zeros(ref.shape, F32)
    for k, v in rows.items():
        upd = upd + jnp.where(rid == k, v, 0.0)

    @pl.when(first)
    def _():
        ref[...] = upd

    @pl.when(jnp.logical_not(first))
    def _():
        ref[...] += upd


def _in_proj(R, x, mod, nw1, w_arr):
    B, T = R.B, R.T

    xin = R.xin(x)

    def body(*refs):
        x_refs = refs[:len(xin)]
        mod_ref, nw_ref, w_ref, z_ref, xbc_ref, qa_ref, kva_ref, pool_ref, kr_ref, dt_ref, h_ref = refs[len(xin):]
        h, _, _ = _norm_mod(R.xval(pl.program_id(1), x_refs), nw_ref[...], _row(mod_ref, 0), _row(mod_ref, 1))
        hb = h.astype(BF)
        h_ref[...] = hb
        p = jnp.dot(hb, w_ref[...], preferred_element_type=F32)
        z_ref[...] = p[:, A_Z:A_XBC]
        xbc_ref[...] = p[:, A_XBC:A_QA]
        qa_ref[...] = p[:, A_QA:A_KVA]
        kva_ref[...] = p[:, A_KVA:A_POOL]
        pool_ref[...] = p[:, A_POOL:A_KR]
        kr_ref[...] = p[:, A_KR:A_DT]
        dt_ref[...] = p[:, A_DT:PC]

    widths = [SSD_IN, XBC, QL, QL, PD, HP, HP]
    outs = [(_sds((B, T, w), F32), R.row(w)) for w in widths] + [(_sds((B, T, D), BF), R.row(D))]
    return R.call("in_proj", body,
                  xin + [(mod, R.mod()), (nw1, R.const((1, D))), (w_arr, R.const((D, PC)))],
                  outs, extra=8 << 20)


def _conv_pre(ext, w_ref, b_ref):
    return (_row(w_ref, 0) * _shift_rows(ext, -1)[8:8 + TM] + _row(w_ref, 1) * ext[8:8 + TM]
            + _row(w_ref, 2) * _shift_rows(ext, 1)[8:8 + TM] + _row(w_ref, 3) * _shift_rows(ext, 2)[8:8 + TM]
            + b_ref[...])


def _softplus(x):
    return jnp.maximum(x, 0.0) + jnp.log(1.0 + jnp.exp(-jnp.abs(x)))


def _ssd_prep(R, xbc_raw, dt_raw, conv_w8, conv_b, dtb):
    B, T = R.B, R.T

    def body(raw_ref, pv_ref, nx_ref, dtr_ref, w_ref, b_ref, dtb_ref, xbc_ref, dt_ref):
        i = pl.program_id(1)
        ext = R.ext(i, pv_ref, raw_ref[...], nx_ref)
        xbc_ref[...] = _silu(_conv_pre(ext, w_ref, b_ref))
        lane = _iota((TM, HP), 1)
        dtv = _softplus(dtr_ref[...] + dtb_ref[...])
        keep = lane < H
        dt_ref[0] = jnp.where(keep, dtv, 0.0)
        dt_ref[1] = jnp.where(keep, pltpu.roll(dtv, HP - H, axis=1), 0.0)

    return R.call("ssd_prep", body,
                  [(xbc_raw, R.row(XBC)), (xbc_raw, R.prev8(XBC)), (xbc_raw, R.next8(XBC)), (dt_raw, R.row(HP)),
                   (conv_w8, R.const((8, XBC))), (conv_b, R.const((1, XBC))), (dtb, R.const((1, HP)))],
                  [(_sds((B, T, XBC), F32), R.row(XBC)), (_sds((B, 2, T, HP), F32), R.row2(HP))], extra=12 << 20)


def _chunk_index(d, s, nc, nctc):
    if d == 0:
        return s
    return jnp.where(s < nctc, nctc - 1 - s, nc - 1 - (s - nctc))


def _dot01_r(x, m01, passes=3):
    b16 = jnp.bfloat16
    m = m01.astype(b16)
    out, rest = None, x
    for _ in range(passes):
        part = rest.astype(b16)
        rest = rest - part.astype(F32)
        term = jnp.dot(part, m, preferred_element_type=F32)
        out = term if out is None else out + term
    return out


def _scan_common(d, dtv, alog_ref):
    sel = _iota((HP, H * HP), 0) == (_iota((HP, H * HP), 1) >> 7)
    a_all = _dot01_r(-jnp.exp(alog_ref[d]), sel)
    dt_all = _dot01_r(dtv, sel)
    adt_all = dt_all * jnp.concatenate([a_all] * (CH // 8), axis=0)
    row = _iota((CH, CH), 0)
    col = _iota((CH, CH), 1)
    inc = col <= row if d == 0 else col >= row
    inc_t = row <= col if d == 0 else row >= col
    q_all = _dot01(inc, adt_all)
    return a_all, dt_all, adt_all, q_all, inc, inc_t


def _head_decay(h, adt_all, q_all, inc, inc_t):
    q = q_all[:, HP * h:HP * (h + 1)]
    q_t = q.T
    qtot = jnp.sum(adt_all[:, HP * h:HP * (h + 1)], axis=0, keepdims=True)
    lm = jnp.where(inc, jnp.exp(q - q_t), 0.0)
    lm_t = jnp.where(inc_t, jnp.exp(q_t - q), 0.0)
    return lm, lm_t, jnp.exp(q), jnp.exp(qtot), jnp.exp(qtot - q)


def _ssd_scan(B, T, xbc, dt2, alog2):
    nc, nctc = T // CH, CTX // CH

    def chain(d, xbc_ref, dt_ref, alog_ref, y_ref, hin_ref, hs):
        xbc_v = xbc_ref[...]
        dtv = dt_ref[...]
        blk = alog_ref[d]
        adt = dtv * -jnp.exp(jnp.sum(jnp.where(_iota(blk.shape, 0) == 0, blk, 0.0), axis=0, keepdims=True))
        row, col = _iota((CH, CH), 0), _iota((CH, CH), 1)
        inc = col <= row if d == 0 else col >= row
        inc_tf = (row <= col if d == 0 else row >= col).astype(F32)
        q = _dot01(inc, adt)
        hin_ref[...] = hs[d]
        for g in range(2):
            bg = xbc_v[:, SSD_IN + NST * g:SSD_IN + NST * (g + 1)]
            cg = xbc_v[:, SSD_IN + 2 * NST + NST * g:SSD_IN + 2 * NST + NST * (g + 1)]
            gm = _dot_nt(cg, bg)
            for r in range(3):
                h = 3 * g + r
                onehot = (_iota((1, HP), 1) == h).astype(F32)
                adt_h = jnp.sum(adt * onehot, axis=1, keepdims=True)
                qc = jnp.sum(q * onehot, axis=1, keepdims=True)
                dt_h = jnp.sum(dtv * onehot, axis=1, keepdims=True)
                qr = jnp.sum(adt_h * inc_tf, axis=0, keepdims=True)
                qtot = jnp.sum(adt_h, axis=0, keepdims=True)
                lm = jnp.where(inc, jnp.exp(qc - qr), 0.0)
                xh = xbc_v[:, P * h:P * (h + 1)] * dt_h
                hprev = hs[d, P * h:P * (h + 1), :]
                y_ref[:, P * h:P * (h + 1)] = _dot(gm * lm, xh) + jnp.exp(qc) * _dot_nt(cg, hprev)
                hs[d, P * h:P * (h + 1), :] = jnp.exp(qtot) * hprev + _dot_tn(xh * jnp.exp(qtot - qc), bg)

    def body(xf_ref, xb_ref, dtf_ref, dtb_ref, alog_ref, yf_ref, yb_ref, hf_ref, hb_ref, hs):
        @pl.when(pl.program_id(1) == 0)
        def _():
            hs[...] = jnp.zeros(hs.shape, F32)

        chain(0, xf_ref, dtf_ref, alog_ref, yf_ref, hf_ref, hs)
        chain(1, xb_ref, dtb_ref, alog_ref, yb_ref, hb_ref, hs)

    cidx = lambda d: (lambda s: _chunk_index(d, s, nc, nctc))
    specs = lambda d: dict(
        xbc=pl.BlockSpec((None, CH, XBC), lambda b, s: (b, cidx(d)(s), 0)),
        dt=pl.BlockSpec((None, None, CH, HP), lambda b, s: (b, d, cidx(d)(s), 0)),
        y=pl.BlockSpec((None, CH, SSD_IN), lambda b, s: (b, cidx(d)(s), 0)),
        h=pl.BlockSpec((None, None, SSD_IN, NST), lambda b, s: (b, cidx(d)(s), 0, 0)))
    f, r = specs(0), specs(1)
    bb = 2 * (_nbytes((CH, XBC), F32) + _nbytes((CH, HP), F32) + _nbytes((CH, SSD_IN), F32) + _nbytes((SSD_IN, NST), F32))
    return pl.pallas_call(
        functools.partial(body), name="ssd_scan", grid=(B, nc),
        in_specs=[f["xbc"], r["xbc"], f["dt"], r["dt"], pl.BlockSpec((2, 8, HP), lambda b, s: (0, 0, 0))],
        out_specs=[f["y"], r["y"], f["h"], r["h"]],
        out_shape=[_out((B, T, SSD_IN), F32)] * 2 + [_out((B, nc, SSD_IN, NST), F32)] * 2,
        scratch_shapes=[pltpu.VMEM((2, SSD_IN, NST), F32)],
        compiler_params=_params(("arbitrary",) * 2, bb, 12 << 20),
    )(*_hbm([xbc, xbc, dt2, dt2, alog2]))


def _swap8(u):
    lane = _iota(u.shape, 1)
    n = u.shape[1]
    return jnp.where((lane & 15) < 8, pltpu.roll(u, n - 8, axis=1), pltpu.roll(u, 8, axis=1))


def _rope(u, cos, sin_signed):
    return u * cos + _swap8(u) * sin_signed


def _rms(x, w):
    r = lax.rsqrt(jnp.mean(x * x, axis=-1, keepdims=True) + EPS)
    xh = x * r
    return xh * w, xh, r


def _rms_bwd(dy, xh, r, w):
    dw = jnp.sum(dy * xh, axis=0, keepdims=True)
    dxh = dy * w
    return r * (dxh - xh * jnp.mean(dxh * xh, axis=-1, keepdims=True)), dw


def _tile6(t):
    return jnp.concatenate([t] * H, axis=1)


def _per_head(fn, u):
    return jnp.concatenate([fn(u[:, HP * h:HP * (h + 1)]) for h in range(H)], axis=1)


def _mla_prep(R, qa, kva, kr, qnw, kvnw, wq, wk, wv, cos, sin):
    B, T = R.B, R.T
    scale = QK ** -0.5

    def body(qa_ref, kva_ref, kr_ref, qnw_ref, kvnw_ref, wq_ref, wk_ref, wv_ref, cos_ref, sin_ref,
             qt_ref, kt_ref, qtr_ref, ktr_ref, vtr_ref, cq_ref, ckv_ref):
        cq, _, _ = _rms(qa_ref[...], qnw_ref[...])
        ckv, _, _ = _rms(kva_ref[...], kvnw_ref[...])
        cqb, ckvb = cq.astype(BF), ckv.astype(BF)
        cq_ref[...] = cqb
        ckv_ref[...] = ckvb
        cos1, sin1 = cos_ref[...], sin_ref[...]
        q = _per_head(lambda u: _rope(u, cos1, sin1), jnp.dot(cqb, wq_ref[...], preferred_element_type=F32)) * scale
        qt_ref[...] = q.astype(BF)
        kk = _rope(kr_ref[...], cos1, sin1)
        hl = _iota((TM, QW), 1) & (HP - 1)
        k = jnp.dot(ckvb, wk_ref[...], preferred_element_type=F32) + _tile6(kk)
        k = jnp.where(jnp.logical_and(hl >= QK, hl < QK + 3), 1.0, k)
        kt_ref[...] = k.astype(BF)
        v = jnp.dot(ckvb, wv_ref[...], preferred_element_type=F32)
        v = jnp.where(jnp.logical_and(hl >= VH, hl < VH + 4), 1.0, v)
        for h in range(H):
            cols = slice(HP * h, HP * (h + 1))
            qtr_ref[h] = q[:, cols].T.astype(BF)
            ktr_ref[h] = k[:, cols].T.astype(BF)
            vtr_ref[h] = v[:, cols].T.astype(BF)

    tr = (_sds((B, H, HP, T), BF), pl.BlockSpec((None, H, HP, TM), lambda b, i: (b, 0, 0, i)))
    return R.call("mla_prep", body,
                  [(qa, R.row(QL)), (kva, R.row(QL)), (kr, R.row(HP)), (qnw, R.const((1, QL))), (kvnw, R.const((1, QL))),
                   (wq, R.const((QL, QW))), (wk, R.const((QL, QW))), (wv, R.const((QL, QW))),
                   (cos, pl.BlockSpec((TM, HP), lambda b, i: (i, 0))), (sin, pl.BlockSpec((TM, HP), lambda b, i: (i, 0)))],
                  [(_sds((B, T, QW), BF), R.row(QW))] * 2 + [tr] * 3 + [(_sds((B, T, QL), BF), R.row(QL))] * 2, extra=12 << 20)


def _flash_fwd(B, T, q_t, kt, v_t):
    nq, nk = T // TQ, T // TK
    HS = 2

    def body(q_ref, k_ref, v_ref, o_ref, lse_ref, s_scr):
        i = pl.program_id(2)

        def attend(nch):
            ms = []
            for hh in range(HS):
                q_tr = q_ref[hh]
                mrun = None
                for j in range(nch):
                    s = _dot(k_ref[TK * j:TK * (j + 1), HP * hh:HP * (hh + 1)], q_tr)
                    s_scr[hh, j] = s
                    mrun = s if mrun is None else jnp.maximum(mrun, s)
                ms.append(jnp.max(mrun, axis=0, keepdims=True))
            row = _iota((HP, TQ), 0)
            for hh in range(HS):
                acc_t = jnp.zeros((HP, TQ), F32)
                for j in range(nch):
                    acc_t = acc_t + _dot(v_ref[hh, :, TK * j:TK * (j + 1)], jnp.exp(s_scr[hh, j] - ms[hh]))
                l = jnp.sum(jnp.where(row == VH, acc_t, 0.0), axis=0, keepdims=True)
                o_ref[:, HP * hh:HP * (hh + 1)] = jnp.where(row < VH, acc_t / l, 0.0).T
                lse_ref[hh] = ms[hh] + jnp.log(l)

        @pl.when(i < CTX // TQ)
        def _():
            attend(CTX // TK)

        @pl.when(i >= CTX // TQ)
        def _():
            attend(nk)

    bb = HS * (_nbytes((TQ, HP), BF) + 2 * _nbytes((T, HP), BF) + 2 * _nbytes((TQ, HP), F32))
    return pl.pallas_call(
        functools.partial(body), name="flash_fwd", grid=(B, H // HS, nq),
        in_specs=[pl.BlockSpec((None, HS, HP, TQ), lambda b, h, i: (b, h, 0, i)),
                  pl.BlockSpec((None, T, HS * HP), lambda b, h, i: (b, 0, h)),
                  pl.BlockSpec((None, HS, HP, T), lambda b, h, i: (b, h, 0, 0))],
        out_specs=[pl.BlockSpec((None, TQ, HS * HP), lambda b, h, i: (b, i, h)),
                   pl.BlockSpec((None, HS, 1, TQ), lambda b, h, i: (b, h, 0, i))],
        out_shape=[_out((B, T, QW), F32), _out((B, H, 1, T), F32)],
        scratch_shapes=[pltpu.VMEM((HS, nk, TK, TQ), F32)],
        compiler_params=_params(("arbitrary",) * 3, bb, _nbytes((HS, nk, TK, TQ), F32) + (8 << 20)),
    )(*_hbm([q_t, kt, v_t]))


def _pool_terms(R, i, rows):
    t, n = R.stream_pos(i, rows)
    lane = _iota((1, PD), 1)
    half = jnp.where(lane < 64, 1, jnp.where(lane < 128, 2, jnp.where(lane < 192, 4, 8)))
    cnt = (jnp.minimum(t + half, n) - jnp.maximum(t - half, 0)).astype(F32)
    valid = jnp.logical_and(t >= 0, t < n)
    return jnp.where(valid, cnt, 1.0), valid.astype(F32), lane


def _lane_select(lane, a2, a4, a8, a16):
    return jnp.where(lane < 64, a2, jnp.where(lane < 128, a4, jnp.where(lane < 192, a8, a16)))


def _pool_centred(R, i, ext):
    cnt, valid, lane = _pool_terms(R, i, ext.shape[0])
    s2 = ext + _shift_rows(ext, -1)
    s4 = _shift_rows(s2, -1) + _shift_rows(s2, 1)
    s8 = _shift_rows(s4, -2) + _shift_rows(s4, 2)
    s16 = _shift_rows(s8, -4) + _shift_rows(s8, 4)
    return _lane_select(lane, s2, s4, s8, s16) / cnt - ext, cnt, valid, lane


def _group_mask():
    return _iota((1, SSD_IN), 1) < SSD_IN // 2


def _ssd_gate(yf_ref, yb_ref, xbc_ref, z_ref, dsk_ref):
    ytot = yf_ref[...] + yb_ref[...] + xbc_ref[:, 0:SSD_IN] * dsk_ref[...]
    z = z_ref[...]
    gz = ytot * _silu(z)
    g0 = _group_mask()
    sq = gz * gz
    s0 = jnp.sum(jnp.where(g0, sq, 0.0), axis=1, keepdims=True)
    s1 = jnp.sum(jnp.where(g0, 0.0, sq), axis=1, keepdims=True)
    half = SSD_IN // 2
    r = jnp.where(g0, lax.rsqrt(s0 / half + EPS), lax.rsqrt(s1 / half + EPS))
    return ytot, z, gz, r


def _out_proj(R, x, mod, yf, yb, xbc, z, o, pool_in, wbd, pscale, dsk, snw, wout):
    B, T = R.B, R.T
    xin = R.xin(x)

    def body(*refs):
        x_refs = refs[:len(xin)]
        (mod_ref, yf_ref, yb_ref, xbc_ref, z_ref, o_ref, u_ref, upv_ref, unx_ref, wbd_ref, psc_ref, dsk_ref, snw_ref, w_ref,
         xmid_ref, cat_ref, mix_ref) = refs[len(xin):]
        i = pl.program_id(1)
        _, _, gz, r = _ssd_gate(yf_ref, yb_ref, xbc_ref, z_ref, dsk_ref)
        dm, _, _, _ = _pool_centred(R, i, R.ext(i, upv_ref, u_ref[...], unx_ref))
        cat_ref[:, 0:SSD_IN] = (gz * r * snw_ref[...]).astype(BF)
        cat_ref[:, SSD_IN:SSD_IN + QW] = o_ref[...].astype(BF)
        cat_ref[:, SSD_IN + QW:MIXW] = (_dot(dm[8:8 + TM], wbd_ref[...]) * psc_ref[...]).astype(BF)
        mix = jnp.dot(cat_ref[...], w_ref[...], preferred_element_type=F32)
        mix_ref[...] = mix.astype(BF)
        xmid_ref[...] = R.xval(i, x_refs) + _row(mod_ref, 2) * mix

    return R.call("out_proj", body,
                  xin + [(mod, R.mod()), (yf, R.row(SSD_IN)), (yb, R.row(SSD_IN)), (xbc, R.row(XBC)), (z, R.row(SSD_IN)),
                         (o, R.row(QW)), (pool_in, R.row(PD)), (pool_in, R.prev8(PD)), (pool_in, R.next8(PD)),
                         (wbd, R.const((PD, PD))), (pscale, R.const((1, PD))),
                         (dsk, R.const((1, SSD_IN))), (snw, R.const((1, SSD_IN))), (wout, R.const((MIXW, D)))],
                  [(_sds((B, T, D), F32), R.row(D)), (_sds((B, T, MIXW), BF), R.row(MIXW)), (_sds((B, T, D), BF), R.row(D))],
                  extra=12 << 20)


def _load_once(first, pairs, sem):
    @pl.when(first)
    def _():
        cps = [pltpu.make_async_copy(src, dst, sem.at[k]) for k, (src, dst) in enumerate(pairs)]
        for cp in cps:
            cp.start()
        for cp in cps:
            cp.wait()


ANY = pl.BlockSpec(memory_space=pl.ANY)


def _mlp_fwd(R, xmid, mod, nw2, w1, w2):
    B, T = R.B, R.T

    def body(x_ref, mod_ref, nw_ref, w1_hbm, w2_hbm, xo_ref, h_ref, u_ref, y_ref, w1_v, w2_v, sem):
        first = jnp.logical_and(pl.program_id(0) == 0, pl.program_id(1) == 0)
        _load_once(first, [(w1_hbm, w1_v), (w2_hbm, w2_v)], sem)
        x = x_ref[...]
        h, _, _ = _norm_mod(x, nw_ref[...], _row(mod_ref, 3), _row(mod_ref, 4))
        hb = h.astype(BF)
        h_ref[...] = hb
        y = jnp.zeros((TM, D), F32)
        for j in range(NDEV):
            u = jnp.dot(hb, w1_v[j], preferred_element_type=F32)
            u_ref[:, FSH * j:FSH * (j + 1)] = u.astype(BF)
            a = jnp.square(jnp.maximum(u, 0.0))
            y = y + jnp.dot(a.astype(BF), w2_v[j], preferred_element_type=F32)
        y_ref[...] = y.astype(BF)
        xo_ref[...] = x + _row(mod_ref, 5) * y

    return R.call("mlp_fwd", body,
                  [(xmid, R.row(D)), (mod, R.mod()), (nw2, R.const((1, D))), (w1, ANY), (w2, ANY)],
                  [(_sds((B, T, D), F32), R.row(D)), (_sds((B, T, D), BF), R.row(D)), (_sds((B, T, FF), BF), R.row(FF)),
                   (_sds((B, T, D), BF), R.row(D))],
                  scratch=[pltpu.VMEM((NDEV, D, FSH), w1.dtype), pltpu.VMEM((NDEV, FSH, D), w2.dtype), pltpu.SemaphoreType.DMA((2,))],
                  extra=(2 * _nbytes((NDEV, D, FSH), BF)) + (8 << 20))


def _loss_head(R, x, tgt, fnw):
    B, T = R.B, R.T

    def body(x_ref, t_ref, w_ref, dx_ref, loss_ref, dw_ref):
        b, i = pl.program_id(0), pl.program_id(1)
        live = jnp.where(i >= R.nct, 1.0, 0.0)
        y, xh, r = _rms(x_ref[...], w_ref[...])
        err = (y - t_ref[...]) * live
        dy = err / D
        dxn, dw = _rms_bwd(dy, xh, r, w_ref[...])
        dx_ref[...] = dxn
        first = jnp.logical_and(b == 0, i == 0)
        part = 0.5 * jnp.sum(jnp.sum(err * err, axis=1, keepdims=True), axis=0, keepdims=True) / D
        _acc_rows(loss_ref, first, {0: jnp.broadcast_to(part, (1, HP))})
        _acc_rows(dw_ref, first, {0: dw})

    return R.call("loss_head", body,
                  [(x, R.row(D)), (tgt, R.tgt(D)), (fnw, R.const((1, D)))],
                  [(_sds((B, T, D), F32), R.row(D)), (_sds((8, HP), F32), R.const((8, HP))), (_sds((8, D), F32), R.const((8, D)))],
                  extra=8 << 20)


def _tn_matmul(name, a, b, tn, sub=1):
    B, T, K = a.shape
    N = b.shape[2]
    nk = 1
    while T % nk or (T // nk) > 1088 or (T // nk) % 16:
        nk += 1
    tk = T // nk
    kt = K if K <= 1536 else 1024
    w = tn // sub
    assert K % kt == 0 and N % tn == 0 and tn % sub == 0

    def body(a_ref, b_ref, o_ref):
        first = jnp.logical_and(pl.program_id(2) == 0, pl.program_id(3) == 0)

        @pl.when(first)
        def _():
            o_ref[...] = jnp.zeros(o_ref.shape, F32)

        acc = _dot_tn(a_ref[...], b_ref[...])
        for s in range(sub):
            o_ref[s] += acc[:, w * s:w * (s + 1)]

    bb = _nbytes((tk, kt), a.dtype) + _nbytes((tk, tn), b.dtype) + _nbytes((kt, tn), F32)
    return pl.pallas_call(
        functools.partial(body), name=name, grid=(N // tn, K // kt, B, nk),
        in_specs=[pl.BlockSpec((None, tk, kt), lambda j, kk, bi, t: (bi, t, kk)),
                  pl.BlockSpec((None, tk, tn), lambda j, kk, bi, t: (bi, t, j))],
        out_specs=pl.BlockSpec((sub, kt, w), lambda j, kk, bi, t: (j, kk, 0)),
        out_shape=_out((sub * N // tn, K, w), F32),
        compiler_params=_params(("arbitrary",) * 4, bb, _nbytes((kt, tn), F32) + (8 << 20)),
    )(*_hbm([a, b]))


def _mlp_bwd(R, dxo, xmid, ubf, ybf, mod, nw2, w1, w2):
    B, T = R.B, R.T

    def body(dxo_ref, x_ref, u_ref, y_ref, mod_ref, nw_ref, w1_hbm, w2_hbm,
             dxm_ref, du_ref, a_ref, dy_ref, dmod_ref, dnw_ref, w1_v, w2_v, sem):
        b, i = pl.program_id(0), pl.program_id(1)
        _load_once(jnp.logical_and(b == 0, i == 0), [(w1_hbm, w1_v), (w2_hbm, w2_v)], sem)
        dxo = dxo_ref[...]
        _, xn, r = _norm_mod(x_ref[...], nw_ref[...], _row(mod_ref, 3), _row(mod_ref, 4))
        dyb = (dxo * _row(mod_ref, 5)).astype(BF)
        dy_ref[...] = dyb
        dg2 = jnp.sum(dxo * y_ref[...].astype(F32), axis=0, keepdims=True)
        dh = jnp.zeros((TM, D), F32)
        for j in range(NDEV):
            rl = jnp.maximum(u_ref[:, FSH * j:FSH * (j + 1)].astype(F32), 0.0)
            a_ref[:, FSH * j:FSH * (j + 1)] = (rl * rl).astype(BF)
            du = (_dot_nt(dyb, w2_v[j]) * (2.0 * rl)).astype(BF)
            du_ref[:, FSH * j:FSH * (j + 1)] = du
            dh = dh + _dot_nt(du, w1_v[j])
        dx, dsh, dsc, dnw = _norm_mod_bwd(dh, xn, r, nw_ref[...], _row(mod_ref, 4))
        dxm_ref[...] = dxo + dx
        _acc_rows(dmod_ref, R.first_of_stream(i), {3: dsh, 4: dsc, 5: dg2})
        _acc_rows(dnw_ref, jnp.logical_and(b == 0, i == 0), {0: dnw})

    return R.call("mlp_bwd", body,
                  [(dxo, R.row(D)), (xmid, R.row(D)), (ubf, R.row(FF)), (ybf, R.row(D)), (mod, R.mod()), (nw2, R.const((1, D))),
                   (w1, ANY), (w2, ANY)],
                  [(_sds((B, T, D), F32), R.row(D)), (_sds((B, T, FF), BF), R.row(FF)), (_sds((B, T, FF), BF), R.row(FF)),
                   (_sds((B, T, D), BF), R.row(D)), (_sds((B, 2, 8, D), F32), R.mod()), (_sds((8, D), F32), R.const((8, D)))],
                  scratch=[pltpu.VMEM((NDEV, D, FSH), w1.dtype), pltpu.VMEM((NDEV, FSH, D), w2.dtype), pltpu.SemaphoreType.DMA((2,))],
                  extra=(2 * _nbytes((NDEV, D, FSH), BF)) + (8 << 20))


def _minus_in_lanes(x, col, first):
    b16 = jnp.bfloat16
    hi = col.astype(b16).astype(F32)
    r1 = col - hi
    mid = r1.astype(b16).astype(F32)
    lane = _iota(x.shape, 1)
    return jnp.where(lane == first, -hi, jnp.where(lane == first + 1, -mid, jnp.where(lane == first + 2, mid - r1, x)))


def _out_bwd(R, dxm, mod, mixbf, yf, yb, xbc, z, o, qt, lse, dsk, snw, wout):
    B, T = R.B, R.T

    def body(dxm_ref, mod_ref, mix_ref, yf_ref, yb_ref, xbc_ref, z_ref, o_ref, q_ref, lse_ref, dsk_ref, snw_ref, w_ref,
             dmix_ref, dz_ref, dyt_ref, dxsk_ref, qs_ref, dos_ref, dotr_ref, dpo_ref, dmod_ref, dvec_ref):
        b, i = pl.program_id(0), pl.program_id(1)
        dxm = dxm_ref[...]
        dmixb = (dxm * _row(mod_ref, 2)).astype(BF)
        dmix_ref[...] = dmixb
        dg1 = jnp.sum(dxm * mix_ref[...].astype(F32), axis=0, keepdims=True)
        dcat = _dot_nt(dmixb, w_ref[...])
        do_v = dcat[:, SSD_IN:SSD_IN + QW]
        doo = do_v * o_ref[...]
        for h in range(H):
            cols = slice(HP * h, HP * (h + 1))
            dd = jnp.sum(doo[:, cols], axis=1, keepdims=True)
            dos_ref[:, cols] = _minus_in_lanes(do_v[:, cols], dd, VH + 1).astype(BF)
            lse_b = jnp.broadcast_to(lse_ref[h], (HP, TM)).T
            qs_ref[:, cols] = _minus_in_lanes(q_ref[:, cols].astype(F32), lse_b, QK).astype(BF)
            dotr_ref[h] = do_v[:, cols].T.astype(BF)
        dpo_ref[...] = dcat[:, SSD_IN + QW:MIXW]
        dsn = dcat[:, 0:SSD_IN]
        ytot, zv, gz, r = _ssd_gate(yf_ref, yb_ref, xbc_ref, z_ref, dsk_ref)
        gh = gz * r
        dsnw = jnp.sum(dsn * gh, axis=0, keepdims=True)
        dgh = dsn * snw_ref[...]
        g0 = _group_mask()
        pr = dgh * gh
        half = SSD_IN // 2
        m0 = jnp.sum(jnp.where(g0, pr, 0.0), axis=1, keepdims=True) / half
        m1 = jnp.sum(jnp.where(g0, 0.0, pr), axis=1, keepdims=True) / half
        dgz = r * (dgh - gh * jnp.where(g0, m0, m1))
        dyt = dgz * _silu(zv)
        dz_ref[...] = (dgz * ytot * _dsilu(zv)).astype(BF)
        dyt_ref[...] = dyt
        dxsk_ref[...] = dyt * dsk_ref[...]
        ddsk = jnp.sum(dyt * xbc_ref[:, 0:SSD_IN], axis=0, keepdims=True)
        _acc_rows(dmod_ref, R.first_of_stream(i), {2: dg1})
        _acc_rows(dvec_ref, jnp.logical_and(b == 0, i == 0), {0: dsnw, 1: ddsk})

    return R.call("out_bwd", body,
                  [(dxm, R.row(D)), (mod, R.mod()), (mixbf, R.row(D)), (yf, R.row(SSD_IN)), (yb, R.row(SSD_IN)), (xbc, R.row(XBC)),
                   (z, R.row(SSD_IN)),
                   (o, R.row(QW)), (qt, R.row(QW)), (lse, pl.BlockSpec((None, H, 1, TM), lambda b, i: (b, 0, 0, i))),
                   (dsk, R.const((1, SSD_IN))), (snw, R.const((1, SSD_IN))), (wout, R.const((MIXW, D)))],
                  [(_sds((B, T, D), BF), R.row(D)), (_sds((B, T, SSD_IN), BF), R.row(SSD_IN)), (_sds((B, T, SSD_IN), F32), R.row(SSD_IN)),
                   (_sds((B, T, SSD_IN), F32), R.row(SSD_IN)), (_sds((B, T, QW), BF), R.row(QW)), (_sds((B, T, QW), BF), R.row(QW)),
                   (_sds((B, H, HP, T), BF), pl.BlockSpec((None, H, HP, TM), lambda b, i: (b, 0, 0, i))),
                   (_sds((B, T, PD), F32), R.row(PD)),
                   (_sds((B, 2, 8, D), F32), R.mod()), (_sds((8, SSD_IN), F32), R.const((8, SSD_IN)))],
                  extra=8 << 20)


def _pool_bwd(R, dpo, u, wbd, scale):
    B, T = R.B, R.T

    def body(d_ref, dpv_ref, dnx_ref, u_ref, upv_ref, unx_ref, w_ref, sc_ref, du_ref, dw_ref, dsc_ref):
        b, i = pl.program_id(0), pl.program_id(1)
        ext_u = R.ext(i, upv_ref, u_ref[...], unx_ref)
        ext_d = R.ext(i, dpv_ref, d_ref[...], dnx_ref)
        dm, cnt, valid, lane = _pool_centred(R, i, ext_u)
        ddm = _dot_nt(ext_d * sc_ref[...], w_ref[...]) * valid
        e = ddm / cnt
        a2 = e + _shift_rows(e, 1)
        a4 = _shift_rows(a2, -1) + _shift_rows(a2, 1)
        a8 = _shift_rows(a4, -2) + _shift_rows(a4, 2)
        a16 = _shift_rows(a8, -4) + _shift_rows(a8, 4)
        du_ref[...] = (_lane_select(lane, a2, a4, a8, a16) - ddm)[8:8 + TM].astype(BF)
        dmc = dm[8:8 + TM]
        dpo_c = d_ref[...]
        first = jnp.logical_and(b == 0, i == 0)

        @pl.when(first)
        def _():
            dw_ref[...] = jnp.zeros(dw_ref.shape, F32)

        dw_ref[...] += _dot_tn(dmc, dpo_c * sc_ref[...])
        _acc_rows(dsc_ref, first, {0: jnp.sum(dpo_c * _dot(dmc, w_ref[...]), axis=0, keepdims=True)})

    return R.call("pool_bwd", body,
                  [(dpo, R.row(PD)), (dpo, R.prev8(PD)), (dpo, R.next8(PD)), (u, R.row(PD)), (u, R.prev8(PD)), (u, R.next8(PD)),
                   (wbd, R.const((PD, PD))), (scale, R.const((1, PD)))],
                  [(_sds((B, T, PD), BF), R.row(PD)), (_sds((PD, PD), F32), R.const((PD, PD))), (_sds((8, PD), F32), R.const((8, PD)))],
                  extra=8 << 20)


def _flash_bwd(B, T, qs, kt, dos, q_t, k_t, v_t, do_t):
    nk, nq = T // TK, T // TQ
    HS = 2

    def body(k_ref, kt_ref, vt_ref, q_ref, do_ref, qt_ref, dot_ref, dq_ref, dk_ref, dv_ref, s_scr, dp_scr, ds_scr):
        j = pl.program_id(2)
        ctx_keys = jnp.where(j < CTX // TK, 1.0, 0.0)
        for hh in range(HS):
            cols = slice(HP * hh, HP * (hh + 1))
            s_scr[hh] = _dot(q_ref[:, cols], kt_ref[hh])
            dp_scr[hh] = _dot(do_ref[:, cols], vt_ref[hh])
        for hh in range(HS):
            cols = slice(HP * hh, HP * (hh + 1))
            dk_tr = jnp.zeros((HP, TK), F32)
            dv_tr = jnp.zeros((HP, TK), F32)
            for i in range(nq):
                rows = slice(TQ * i, TQ * (i + 1))
                p = jnp.exp(s_scr[hh, rows, :])
                if i < CTX // TQ:
                    p = p * ctx_keys
                ds = (p * dp_scr[hh, rows, :]).astype(BF)
                ds_scr[hh, rows, :] = ds
                dv_tr = dv_tr + _dot(dot_ref[hh, :, rows], p)
                dk_tr = dk_tr + _dot(qt_ref[hh, :, rows], ds)
            dq = _dot(ds_scr[hh], k_ref[:, cols])

            @pl.when(j == 0)
            def _():
                dq_ref[:, cols] = dq

            @pl.when(j > 0)
            def _():
                dq_ref[:, cols] += dq

            dk_ref[:, cols] = dk_tr.T
            dv_ref[:, cols] = dv_tr.T

    tspec = pl.BlockSpec((None, TK, HS * HP), lambda b, h, j: (b, j, h))
    fspec = pl.BlockSpec((None, T, HS * HP), lambda b, h, j: (b, 0, h))
    ttspec = pl.BlockSpec((None, HS, HP, TK), lambda b, h, j: (b, h, 0, j))
    ftspec = pl.BlockSpec((None, HS, HP, T), lambda b, h, j: (b, h, 0, 0))
    bb = HS * (4 * _nbytes((T, HP), BF) + _nbytes((T, HP), F32) + 8 * _nbytes((TK, HP), F32))
    scr = HS * (2 * _nbytes((T, TK), F32) + _nbytes((T, TK), BF))
    return pl.pallas_call(
        functools.partial(body), name="flash_bwd", grid=(B, H // HS, nk),
        in_specs=[tspec, ttspec, ttspec, fspec, fspec, ftspec, ftspec],
        out_specs=[fspec, tspec, tspec],
        out_shape=[_out((B, T, QW), F32)] * 3,
        scratch_shapes=[pltpu.VMEM((HS, T, TK), F32), pltpu.VMEM((HS, T, TK), F32), pltpu.VMEM((HS, T, TK), BF)],
        compiler_params=_params(("arbitrary",) * 3, bb, scr + (8 << 20)),
    )(*_hbm([kt, k_t, v_t, qs, dos, q_t, do_t]))


def _mla_bwd(R, dqt, dkt, dvt, qa, kva, qnw, kvnw, wq, wk, wv, cos, sin):
    B, T = R.B, R.T
    scale = QK ** -0.5

    def body(dq_ref, dk_ref, dv_ref, qa_ref, kva_ref, qnw_ref, kvnw_ref, wq_ref, wk_ref, wv_ref, cos_ref, sin_ref,
             dqp_ref, dkvb_ref, dqa_ref, dkva_ref, dkr_ref, dnw_ref):
        b, i = pl.program_id(0), pl.program_id(1)
        cos1, sin1 = cos_ref[...], sin_ref[...]
        dq = dq_ref[...] * scale
        dqp = _per_head(lambda g: g * cos1 + _swap8(g * sin1), dq).astype(BF)
        dqp_ref[...] = dqp
        dkv = dk_ref[...]
        dkb = dkv.astype(BF)
        dvb = dv_ref[...].astype(BF)
        dkvb_ref[:, :QW] = dkb
        dkvb_ref[:, QW:] = dvb
        dkk = dkv[:, 0:HP]
        for h in range(1, H):
            dkk = dkk + dkv[:, HP * h:HP * (h + 1)]
        lane = _iota((TM, HP), 1)
        rope_lane = jnp.logical_and(lane >= NOPE, lane < NOPE + ROPE)
        dkr_ref[...] = jnp.where(rope_lane, dkk * cos1 + _swap8(dkk * sin1), 0.0).astype(BF)
        _, qh, qr = _rms(qa_ref[...], qnw_ref[...])
        _, kh, kr_ = _rms(kva_ref[...], kvnw_ref[...])
        dcq = _dot_nt(dqp, wq_ref[...])
        dckv = _dot_nt(dkb, wk_ref[...]) + _dot_nt(dvb, wv_ref[...])
        dqa, dqw = _rms_bwd(dcq, qh, qr, qnw_ref[...])
        dkva, dkw = _rms_bwd(dckv, kh, kr_, kvnw_ref[...])
        dqa_ref[...] = dqa.astype(BF)
        dkva_ref[...] = dkva.astype(BF)
        _acc_rows(dnw_ref, jnp.logical_and(b == 0, i == 0), {0: dqw, 1: dkw})

    tab = pl.BlockSpec((TM, HP), lambda b, i: (i, 0))
    return R.call("mla_bwd", body,
                  [(dqt, R.row(QW)), (dkt, R.row(QW)), (dvt, R.row(QW)), (qa, R.row(QL)), (kva, R.row(QL)),
                   (qnw, R.const((1, QL))), (kvnw, R.const((1, QL))), (wq, R.const((QL, QW))), (wk, R.const((QL, QW))),
                   (wv, R.const((QL, QW))), (cos, tab), (sin, tab)],
                  [(_sds((B, T, QW), BF), R.row(QW)), (_sds((B, T, 2 * QW), BF), R.row(2 * QW))]
                  + [(_sds((B, T, QL), BF), R.row(QL))] * 2
                  + [(_sds((B, T, HP), BF), R.row(HP)), (_sds((8, QL), F32), R.const((8, QL)))], extra=8 << 20)


def _ssd_scan_bwd(B, T, dyt, xbc, dt2, alog2, hin_f, hin_b):
    nc, nctc = T // CH, CTX // CH

    def chain(d, dy_ref, xbc_ref, dt_ref, alog_ref, hin_ref, dxbc_ref, ddt_ref, da_ref, dhs_all):
        dhs = dhs_all.at[d]
        xbc_v = xbc_ref[...]
        dyv = dy_ref[...]
        a_all, dt_all, adt_all, q_all, inc, inc_t = _scan_common(d, dt_ref[...], alog_ref)
        ones_p = jnp.ones((P, HP), F32)
        total = lambda m: jnp.sum(jnp.sum(m, axis=0, keepdims=True), axis=1, keepdims=True)
        dq_parts, dqtot_parts, ddtx_parts = [], [], []
        for g in range(2):
            bg = xbc_v[:, SSD_IN + NST * g:SSD_IN + NST * (g + 1)]
            cg = xbc_v[:, SSD_IN + 2 * NST + NST * g:SSD_IN + 2 * NST + NST * (g + 1)]
            bg_t, cg_t = bg.T, cg.T
            gm = _dot(cg, bg_t)
            gm_t = _dot(bg, cg_t)
            dgm = jnp.zeros((CH, CH), F32)
            dgm_t = jnp.zeros((CH, CH), F32)
            dbg = jnp.zeros((CH, NST), F32)
            dcg = jnp.zeros((CH, NST), F32)
            for r in range(3):
                h = 3 * g + r
                lm, lm_t, eq, etot, dte = _head_decay(h, adt_all, q_all, inc, inc_t)
                eq_p, etot_p, dte_p = eq[:, :P], etot[:, :P], dte[:, :P]
                dt_p = dt_all[:, HP * h:HP * h + P]
                xs_h = xbc_v[:, P * h:P * (h + 1)]
                xh = xs_h * dt_p
                sm, sm_t = gm * lm, gm_t * lm_t
                dy_h = dyv[:, P * h:P * (h + 1)]
                hprev = hin_ref[P * h:P * (h + 1), :].T
                dho = dhs[:, P * h:P * (h + 1)]
                ds = _dot_nt(dy_h, xh)
                ds_t = _dot_nt(xh, dy_h)
                dx = _dot(sm_t, dy_h)
                edy = eq_p * dy_h
                yo = _dot(cg, hprev)
                dcg = dcg + _dot_nt(edy, hprev)
                dhin = _dot(cg_t, edy) + etot_p * dho
                zs = _dot(bg, dho)
                dx = dx + dte_p * zs
                wm = dte_p * xh * zs
                dbg = dbg + _dot_nt(xh * dte_p, dho)
                dgm = dgm + ds * lm
                dgm_t = dgm_t + ds_t * lm_t
                rs = jnp.sum(ds * sm - ds_t * sm_t, axis=1, keepdims=True) + jnp.sum(edy * yo - wm, axis=1, keepdims=True)
                dq_parts.append(jnp.broadcast_to(rs, (CH, HP)))
                dqtot_parts.append(total(hprev * dho) * etot + total(wm))
                dxbc_ref[:, P * h:P * (h + 1)] = dx * dt_p
                ddtx_parts.append(_dot01_r(dx * xs_h, ones_p, passes=2))
                dhs[:, P * h:P * (h + 1)] = dhin
            dcg = dcg + _dot(dgm, bg)
            dbg = dbg + _dot(dgm_t, cg)
            dxbc_ref[:, SSD_IN + NST * g:SSD_IN + NST * (g + 1)] = dbg
            dxbc_ref[:, SSD_IN + 2 * NST + NST * g:SSD_IN + 2 * NST + NST * (g + 1)] = dcg
        cat = lambda parts: jnp.concatenate(parts, axis=1)
        dadt_all = _dot01(inc_t, cat(dq_parts)) + cat(dqtot_parts)
        ddt_all = cat(ddtx_parts) + dadt_all * jnp.concatenate([a_all] * (CH // 8), axis=0)
        da_all = jnp.sum(dadt_all * dt_all, axis=0, keepdims=True)
        lane, lane1 = _iota((CH, HP), 1), _iota((1, HP), 1)
        ddt = jnp.zeros((CH, HP), F32)
        da = jnp.zeros((1, HP), F32)
        for h in range(H):
            ddt = ddt + jnp.where(lane == h, ddt_all[:, HP * h:HP * (h + 1)], 0.0)
            da = da + jnp.where(lane1 == h, da_all[:, HP * h:HP * (h + 1)], 0.0)
        ddt_ref[...] = ddt
        da_ref[d] += jnp.where(_iota((8, HP), 0) == 0, da, 0.0)

    def body(dyf_ref, dyb_ref, xf_ref, xb_ref, dtf_ref, dtb_ref, alog_ref, hf_ref, hb_ref,
             dxf_ref, dxb_ref, ddtf_ref, ddtb_ref, da_ref, dhs):
        @pl.when(pl.program_id(1) == 0)
        def _():
            dhs[...] = jnp.zeros(dhs.shape, F32)
            da_ref[...] = jnp.zeros(da_ref.shape, F32)

        chain(0, dyf_ref, xf_ref, dtf_ref, alog_ref, hf_ref, dxf_ref, ddtf_ref, da_ref, dhs)
        chain(1, dyb_ref, xb_ref, dtb_ref, alog_ref, hb_ref, dxb_ref, ddtb_ref, da_ref, dhs)

    cidx = lambda d: (lambda s: _chunk_index(d, nc - 1 - s, nc, nctc))
    specs = lambda d: dict(
        dy=pl.BlockSpec((None, CH, SSD_IN), lambda b, s: (b, cidx(d)(s), 0)),
        xbc=pl.BlockSpec((None, CH, XBC), lambda b, s: (b, cidx(d)(s), 0)),
        dt=pl.BlockSpec((None, None, CH, HP), lambda b, s: (b, d, cidx(d)(s), 0)),
        h=pl.BlockSpec((None, None, SSD_IN, NST), lambda b, s: (b, cidx(d)(s), 0, 0)),
        ddt=pl.BlockSpec((None, CH, HP), lambda b, s: (b, cidx(d)(s), 0)))
    f, r = specs(0), specs(1)
    bb = 2 * (2 * _nbytes((CH, XBC), F32) + 2 * _nbytes((CH, HP), F32) + _nbytes((CH, SSD_IN), F32) + _nbytes((SSD_IN, NST), F32))
    return pl.pallas_call(
        functools.partial(body), name="ssd_scan_bwd", grid=(B, nc),
        in_specs=[f["dy"], r["dy"], f["xbc"], r["xbc"], f["dt"], r["dt"], pl.BlockSpec((2, 8, HP), lambda b, s: (0, 0, 0)),
                  f["h"], r["h"]],
        out_specs=[f["xbc"], r["xbc"], f["ddt"], r["ddt"], pl.BlockSpec((None, 2, 8, HP), lambda b, s: (b, 0, 0, 0))],
        out_shape=[_out((B, T, XBC), F32)] * 2 + [_out((B, T, HP), F32)] * 2 + [_out((B, 2, 8, HP), F32)],
        scratch_shapes=[pltpu.VMEM((2, NST, SSD_IN), F32)],
        compiler_params=_params(("arbitrary",) * 2, bb, 16 << 20),
    )(*_hbm([dyt, dyt, xbc, xbc, dt2, dt2, alog2, hin_f, hin_b]))


def _ssd_prep_bwd(R, dxbc_f, dxbc_b, dxsk, ddt_f, ddt_b, xbc_raw, dt_raw, conv_w8, conv_b, dtb):
    B, T = R.B, R.T

    def body(dxf_ref, dxb_ref, dsk_ref, ddtf_ref, ddtb2_ref, raw_ref, pv_ref, nx_ref, dtr_ref, w_ref, b_ref, dtb_ref,
             dpre_ref, ddtr_ref, dcw_ref, dvec_ref, ddtb_ref):
        b, i = pl.program_id(0), pl.program_id(1)
        ext = R.ext(i, pv_ref, raw_ref[...], nx_ref)
        pre = _conv_pre(ext, w_ref, b_ref)
        dxbc = dxf_ref[...] + dxb_ref[...]
        skip = jnp.concatenate([dsk_ref[...], jnp.zeros((TM, XBC - SSD_IN), F32)], axis=1)
        dpre = (dxbc + skip) * _dsilu(pre)
        dpre_ref[...] = dpre
        first = jnp.logical_and(b == 0, i == 0)
        taps = {k: jnp.sum(dpre * _shift_rows(ext, k - 1)[8:8 + TM], axis=0, keepdims=True) for k in range(4)}
        _acc_rows(dcw_ref, first, taps)
        _acc_rows(dvec_ref, first, {0: jnp.sum(dpre, axis=0, keepdims=True)})
        ddt = ddtf_ref[...] + pltpu.roll(ddtb2_ref[...], H, axis=1)
        ddtr = ddt * _sigmoid(dtr_ref[...] + dtb_ref[...])
        ddtr = jnp.where(_iota((TM, HP), 1) < 2 * H, ddtr, 0.0)
        ddtr_ref[...] = ddtr.astype(BF)
        _acc_rows(ddtb_ref, first, {0: jnp.sum(ddtr, axis=0, keepdims=True)})

    return R.call("ssd_prep_bwd", body,
                  [(dxbc_f, R.row(XBC)), (dxbc_b, R.row(XBC)), (dxsk, R.row(SSD_IN)), (ddt_f, R.row(HP)), (ddt_b, R.row(HP)),
                   (xbc_raw, R.row(XBC)), (xbc_raw, R.prev8(XBC)),
                   (xbc_raw, R.next8(XBC)), (dt_raw, R.row(HP)), (conv_w8, R.const((8, XBC))), (conv_b, R.const((1, XBC))),
                   (dtb, R.const((1, HP)))],
                  [(_sds((B, T, XBC), F32), R.row(XBC)), (_sds((B, T, HP), BF), R.row(HP)), (_sds((8, XBC), F32), R.const((8, XBC))),
                   (_sds((8, XBC), F32), R.const((8, XBC))), (_sds((8, HP), F32), R.const((8, HP)))], extra=16 << 20)


def _in_bwd(R, dxm, x, mod, nw1, dz, dpre, dqa, dkva, dpool, dkr, ddtr, conv_w8, w_arr, latent_only=False):
    B, T = R.B, R.T
    dx_out = (_sds((B, T - CTX, D), F32), R.tgt(D)) if latent_only else (_sds((B, T, D), F32), R.row(D))

    xin = R.xin(x)

    def body(*refs):
        x_refs = refs[:len(xin)]
        (dxm_ref, mod_ref, nw_ref, dz_ref, dp_ref, dpp_ref, dpn_ref, dqa_ref, dkva_ref, dpool_ref, dkr_ref, ddt_ref,
         cw_ref, w_ref, dx_ref, dproj_ref, dmod_ref, dnw_ref) = refs[len(xin):]
        b, i = pl.program_id(0), pl.program_id(1)
        ext = R.ext(i, dpp_ref, dp_ref[...], dpn_ref)
        draw = (_row(cw_ref, 0) * _shift_rows(ext, 1)[8:8 + TM] + _row(cw_ref, 1) * ext[8:8 + TM]
                + _row(cw_ref, 2) * _shift_rows(ext, -1)[8:8 + TM] + _row(cw_ref, 3) * _shift_rows(ext, -2)[8:8 + TM])
        dproj_ref[:, A_Z:A_XBC] = dz_ref[...]
        dproj_ref[:, A_XBC:A_QA] = draw.astype(BF)
        dproj_ref[:, A_QA:A_KVA] = dqa_ref[...]
        dproj_ref[:, A_KVA:A_POOL] = dkva_ref[...]
        dproj_ref[:, A_POOL:A_KR] = dpool_ref[...]
        dproj_ref[:, A_KR:A_DT] = dkr_ref[...]
        dproj_ref[:, A_DT:PC] = ddt_ref[...]
        dh = _dot_nt(dproj_ref[...], w_ref[...])
        _, xn, r = _norm_mod(R.xval(i, x_refs), nw_ref[...], _row(mod_ref, 0), _row(mod_ref, 1))
        dx, dsh, dsc, dnw = _norm_mod_bwd(dh, xn, r, nw_ref[...], _row(mod_ref, 1))
        dx_ref[...] = dxm_ref[...] + dx
        _acc_rows(dmod_ref, R.first_of_stream(i), {0: dsh, 1: dsc})
        _acc_rows(dnw_ref, jnp.logical_and(b == 0, i == 0), {0: dnw})

    return R.call("in_bwd", body,
                  xin + [(dxm, R.row(D)), (mod, R.mod()), (nw1, R.const((1, D))), (dz, R.row(SSD_IN)), (dpre, R.row(XBC)),
                   (dpre, R.prev8(XBC)), (dpre, R.next8(XBC)), (dqa, R.row(QL)), (dkva, R.row(QL)), (dpool, R.row(PD)), (dkr, R.row(HP)),
                   (ddtr, R.row(HP)), (conv_w8, R.const((8, XBC))), (w_arr, R.const((D, PC)))],
                  [dx_out, (_sds((B, T, PC), BF), R.row(PC)), (_sds((B, 2, 8, D), F32), R.mod()),
                   (_sds((8, D), F32), R.const((8, D)))], extra=12 << 20)


def _adaln_fwd(cs, mod_w):
    L, _, C = mod_w.shape

    def body(c_ref, w_ref, o_ref):
        s = _silu(c_ref[...]).astype(BF)
        for l in range(L):
            o_ref[l] = jnp.dot(s, w_ref[l].astype(BF), preferred_element_type=F32)

    return pl.pallas_call(functools.partial(body), name="adaln_fwd", out_shape=_sds((L, 24, C), F32),
                          compiler_params=_params(None, _nbytes(mod_w.shape, F32) + _nbytes((L, 24, C), F32), 8 << 20))(cs, mod_w)


def _adaln_bwd(cs, dm, mod_w):
    L, _, C = mod_w.shape

    def body(c_ref, dm_ref, w_ref, gw_ref, gc_ref):
        c = c_ref[...]
        s = _silu(c).astype(BF)
        acc = jnp.zeros((24, D), F32)
        for l in range(L):
            dmb = dm_ref[l].astype(BF)
            gw_ref[l] = _dot_tn(s, dmb)
            acc = acc + _dot_nt(dmb, w_ref[l])
        gc_ref[...] = acc * _dsilu(c)

    return pl.pallas_call(functools.partial(body), name="adaln_bwd", out_shape=[_sds((L, D, C), F32), _sds((24, D), F32)],
                          compiler_params=_params(None, 2 * _nbytes(mod_w.shape, F32), 8 << 20))(cs, dm, mod_w)


def _cast_all(name, arrays, dtype):
    n = len(arrays)

    def body(*refs):
        for src, dst in zip(refs[:n], refs[n:]):
            dst[...] = src[...].astype(dtype)

    nb = sum(_nbytes(a.shape, a.dtype) + _nbytes(a.shape, dtype) for a in arrays)
    return pl.pallas_call(functools.partial(body), name=name, out_shape=[_sds(a.shape, dtype) for a in arrays],
                          compiler_params=_params(None, nb // 2, 4 << 20))(*arrays)


def _sum_blocks(name, parts):
    Pn, Rr, C = parts.shape

    def body(p_ref, o_ref):
        acc = p_ref[0]
        for k in range(1, Pn):
            acc = acc + p_ref[k]
        o_ref[...] = acc

    return pl.pallas_call(functools.partial(body), name=name, out_shape=_sds((Rr, C), F32),
                          compiler_params=_params(None, _nbytes(parts.shape, F32), 4 << 20))(parts)


def _adamw(name, parts, w, m, v, rt):
    nseg = len(parts)
    Pn, rseg, C = parts[0].shape
    Rr = nseg * rseg
    tiles = rseg // rt
    assert rseg % rt == 0 and w.shape == (Rr, C)
    c1 = 1.0 - ADAM_B1 ** ADAM_STEP
    c2 = 1.0 - ADAM_B2 ** ADAM_STEP

    def body(*refs):
        p_refs = refs[:nseg]
        w_ref, m_ref, v_ref, g_ref, d_ref, nm_ref, nv_ref = refs[nseg:]
        i = pl.program_id(0)
        g = None
        for s, p_ref in enumerate(p_refs):
            gs = p_ref[0].astype(F32)
            for k in range(1, Pn):
                gs = gs + p_ref[k].astype(F32)
            g = gs if g is None else jnp.where(i >= s * tiles, gs, g)
        mn = ADAM_B1 * m_ref[...] + (1.0 - ADAM_B1) * g
        vn = ADAM_B2 * v_ref[...] + (1.0 - ADAM_B2) * jnp.square(g)
        g_ref[...] = g
        nm_ref[...] = mn
        nv_ref[...] = vn
        d_ref[...] = -ADAM_LR * ((mn / c1) / (jnp.sqrt(vn / c2) + ADAM_EPS) + ADAM_WD * w_ref[...])

    spec = pl.BlockSpec((rt, C), lambda i: (i, 0))
    pspec = lambda s: pl.BlockSpec((Pn, rt, C), lambda i: (0, jnp.clip(i - s * tiles, 0, tiles - 1), 0))
    bb = nseg * Pn * _nbytes((rt, C), parts[0].dtype) + 7 * _nbytes((rt, C), F32)
    return pl.pallas_call(
        functools.partial(body), name=name, grid=(Rr // rt,),
        in_specs=[pspec(s) for s in range(nseg)] + [spec, spec, spec],
        out_specs=[spec] * 4, out_shape=[_out((Rr, C), F32)] * 4,
        compiler_params=_params(("arbitrary",), bb, 4 << 20),
    )(*_hbm(list(parts) + [w, m, v]))


MESH = pl.DeviceIdType.MESH


def _my_pos():
    return lax.axis_index("x"), lax.axis_index("y"), lax.axis_index("c")


def _dev_index(x, y, c):
    return 4 * x + 2 * y + c


def _all_gather(name, shards):
    n = len(shards)

    def body(*refs):
        ins, outs = refs[:n], refs[n:2 * n]
        send_sems, recv_sems, local_sem = refs[2 * n:]
        x, y, c = _my_pos()
        me, sibling = (x, y, c), (x, y, 1 - c)
        chips = [(1 - x, y), (x, 1 - y), (1 - x, 1 - y)]

        def copy(t, k, block, to, src=None):
            slot = outs[t].at[_dev_index(*block)]
            return pltpu.make_async_remote_copy(
                src_ref=slot if src is None else src, dst_ref=slot,
                send_sem=send_sems.at[t, k], recv_sem=recv_sems.at[t, k], device_id=to, device_id_type=MESH)

        mine = [pltpu.make_async_copy(ins[t], outs[t].at[_dev_index(*me)], local_sem.at[t]) for t in range(n)]
        for cp in mine:
            cp.start()
        first = []
        for t in range(n):
            first.append(copy(t, 0, me, sibling, src=ins[t]))
            first += [copy(t, 1 + j, me, (*chip, c), src=ins[t]) for j, chip in enumerate(chips)]
        for cp in first:
            cp.start()
        passed = []
        for j, chip in enumerate(chips):
            for t in range(n):
                copy(t, 1 + j, (*chip, c), me).wait_recv()
                cp = copy(t, 4 + j, (*chip, c), sibling)
                cp.start()
                passed.append(cp)
        for t in range(n):
            copy(t, 0, sibling, me).wait_recv()
            for j, chip in enumerate(chips):
                copy(t, 4 + j, (*chip, 1 - c), me).wait_recv()
        for cp in first + passed:
            cp.wait_send()
        for cp in mine:
            cp.wait()

    return pl.pallas_call(
        functools.partial(body), name=name,
        in_specs=[ANY] * n, out_specs=[ANY] * n,
        out_shape=[_sds((NDEV,) + s.shape, s.dtype) for s in shards],
        scratch_shapes=[pltpu.SemaphoreType.DMA((n, 7)), pltpu.SemaphoreType.DMA((n, 7)), pltpu.SemaphoreType.DMA((n,))],
    )(*shards)


def _all_to_all(name, parts):
    n = len(parts)

    def body(*refs):
        ins, outs = refs[:n], refs[n:2 * n]
        send_sems, recv_sems, local_sem = refs[2 * n:]
        x, y, c = _my_pos()
        me = _dev_index(x, y, c)
        peers = [(x ^ ((k >> 2) & 1), y ^ ((k >> 1) & 1), c ^ (k & 1)) for k in range(1, NDEV)]
        mine = [pltpu.make_async_copy(ins[t].at[me], outs[t].at[me], local_sem.at[t]) for t in range(n)]
        for cp in mine:
            cp.start()
        sends = []
        for t in range(n):
            for k, peer in enumerate(peers):
                cp = pltpu.make_async_remote_copy(
                    src_ref=ins[t].at[_dev_index(*peer)], dst_ref=outs[t].at[me],
                    send_sem=send_sems.at[t, k], recv_sem=recv_sems.at[t, k], device_id=peer, device_id_type=MESH)
                cp.start()
                sends.append(cp)
        for t in range(n):
            for k, peer in enumerate(peers):
                slot = outs[t].at[_dev_index(*peer)]
                pltpu.make_async_remote_copy(
                    src_ref=slot, dst_ref=slot, send_sem=send_sems.at[t, k], recv_sem=recv_sems.at[t, k],
                    device_id=peer, device_id_type=MESH).wait_recv()
        for cp in sends:
            cp.wait_send()
        for cp in mine:
            cp.wait()

    return pl.pallas_call(
        functools.partial(body), name=name,
        in_specs=[ANY] * n, out_specs=[ANY] * n,
        out_shape=[_sds(p.shape, p.dtype) for p in parts],
        scratch_shapes=[pltpu.SemaphoreType.DMA((n, 7)), pltpu.SemaphoreType.DMA((n, 7)), pltpu.SemaphoreType.DMA((n,))],
    )(*parts)


SEM = pl.BlockSpec(memory_space=pltpu.SEMAPHORE)
IN_HBM = pl.BlockSpec(memory_space=pltpu.HBM)
DATAFLOW = pltpu.SideEffectType.DATAFLOW_SIDE_EFFECTING


def _flip_peers(x, y, c):
    return [(x ^ ((k >> 2) & 1), y ^ ((k >> 1) & 1), c ^ (k & 1)) for k in range(1, NDEV)]


def _split_copies(srcs, lands, send_sems, recv_sems, gather):
    x, y, c = _my_pos()
    me = _dev_index(x, y, c)
    out = []
    for t in range(len(srcs)):
        for k, peer in enumerate(_flip_peers(x, y, c)):
            p = _dev_index(*peer)
            src = srcs[t] if gather else srcs[t].at[p]
            sems = dict(send_sem=send_sems.at[7 * t + k], recv_sem=recv_sems.at[7 * t + k], device_id=peer, device_id_type=MESH)
            out.append((pltpu.make_async_remote_copy(src_ref=src, dst_ref=lands[t].at[me], **sems),
                        pltpu.make_async_remote_copy(src_ref=src, dst_ref=lands[t].at[p], **sems)))
    return out


def _exchange_start(name, collective_id, srcs, gather):
    n = len(srcs)
    lands = [lax.empty(((NDEV,) + s.shape) if gather else s.shape, s.dtype) for s in srcs]

    def body(*refs):
        src_refs, land_refs = refs[:n], refs[n:2 * n]
        send_sems, recv_sems = refs[2 * n], refs[2 * n + 1]
        token = refs[-1]
        barrier = pltpu.get_barrier_semaphore()
        for peer in _flip_peers(*_my_pos()):
            pl.semaphore_signal(barrier, inc=1, device_id=peer, device_id_type=MESH)
        pl.semaphore_wait(barrier, NDEV - 1)
        for send, _ in _split_copies(src_refs, land_refs, send_sems, recv_sems, gather):
            send.start()
        token[...] = jnp.zeros(token.shape, token.dtype)

    hbm = lambda a: pltpu.HBM(a.shape, a.dtype)
    res = pl.pallas_call(
        functools.partial(body), name=name,
        out_shape=[pltpu.SemaphoreType.DMA((7 * n,)), pltpu.SemaphoreType.DMA((7 * n,))] + [hbm(s) for s in srcs]
        + [hbm(a) for a in lands] + [_sds((8, HP), F32)],
        in_specs=[IN_HBM] * (2 * n), out_specs=[SEM, SEM] + [IN_HBM] * (2 * n) + [pl.BlockSpec(memory_space=pltpu.VMEM)],
        input_output_aliases={i: 2 + i for i in range(2 * n)},
        compiler_params=pltpu.CompilerParams(has_side_effects=DATAFLOW, collective_id=collective_id),
    )(*_hbm(list(srcs) + lands))
    return res[0], res[1], list(res[2:2 + n]), list(res[2 + n:2 + 2 * n]), res[-1]


def _exchange_wait(name, send_sems, recv_sems, srcs, lands, after, gather):
    n = len(srcs)

    def body(*refs):
        src_refs, land_refs = refs[:n], refs[n:2 * n]
        for _, recv in _split_copies(src_refs, land_refs, refs[2 * n], refs[2 * n + 1], gather):
            recv.wait_send()
            recv.wait_recv()

    hbm = lambda a: pltpu.HBM(a.shape, a.dtype)
    res = pl.pallas_call(
        functools.partial(body), name=name,
        out_shape=[hbm(s) for s in srcs] + [hbm(a) for a in lands],
        in_specs=[IN_HBM] * (2 * n) + [SEM, SEM, ANY], out_specs=[IN_HBM] * (2 * n),
        input_output_aliases={i: i for i in range(2 * n)},
        compiler_params=pltpu.CompilerParams(has_side_effects=DATAFLOW),
    )(*srcs, *lands, send_sems, recv_sems, after)
    return list(res[:n]), list(res[n:])


def _with_own(lands, own_blocks, me):
    out = []
    for land, own in zip(lands, own_blocks):
        out.append(lax.dynamic_update_slice(land, own[None], (me,) + (0,) * own.ndim))
    return out


def _arrange_w_in(w):
    z = lambda n: jnp.zeros((w.shape[0], n), w.dtype)
    return jnp.concatenate([w[:, 0:1280], w[:, 1292:1548], w[:, 1548:1804], w[:, 1836:2092],
                            z(64), w[:, 1804:1836], z(32), w[:, 1280:1292], z(HP - 2 * H)], axis=1)


def _unarrange_w_in(g):
    return jnp.concatenate([g[:, 0:1280], g[:, A_DT:A_DT + 2 * H], g[:, A_QA:A_KVA], g[:, A_KVA:A_POOL],
                            g[:, A_KR + NOPE:A_KR + NOPE + ROPE], g[:, A_POOL:A_KR]], axis=1)


def _pad_heads(w, width):
    k = w.shape[0]
    return jnp.pad(w.reshape(k, H, width), ((0, 0), (0, 0), (0, HP - width))).reshape(k, H * HP)


def _unpad_heads(g, width):
    k = g.shape[0]
    return g.reshape(k, H, HP)[:, :, :width].reshape(k, H * width)


def _arrange_w_out(w):
    att = jnp.pad(w[SSD_IN:2 * SSD_IN].reshape(H, VH, D), ((0, 0), (0, HP - VH), (0, 0))).reshape(QW, D)
    return jnp.concatenate([w[0:SSD_IN], att, w[2 * SSD_IN:]], axis=0)


def _unarrange_w_out(g):
    att = g[SSD_IN:SSD_IN + QW].reshape(H, HP, D)[:, :VH].reshape(SSD_IN, D)
    return jnp.concatenate([g[0:SSD_IN], att, g[SSD_IN + QW:]], axis=0)


def _rope_tables(T):
    n = T - CTX
    rows = n // GRID_W
    pairs = ROPE // 4
    inv = ROPE_THETA ** (-jnp.arange(pairs, dtype=F32) / pairs)
    ar = jnp.arange(rows, dtype=F32)[:, None] * inv
    ac = jnp.arange(GRID_W, dtype=F32)[:, None] * inv
    by_row = lambda a: jnp.repeat(a, GRID_W, axis=0)
    by_col = lambda a: jnp.tile(a, (rows, 1))
    cos = jnp.concatenate([by_row(jnp.cos(ar))] * 2 + [by_col(jnp.cos(ac))] * 2, axis=1)
    sin = jnp.concatenate([-by_row(jnp.sin(ar)), by_row(jnp.sin(ar)), -by_col(jnp.sin(ac)), by_col(jnp.sin(ac))], axis=1)
    ones, zeros = jnp.ones((n, NOPE), F32), jnp.zeros((n, NOPE), F32)
    cos = jnp.concatenate([ones, cos, ones[:, :HP - QK]], axis=1)
    sin = jnp.concatenate([zeros, sin, zeros[:, :HP - QK]], axis=1)
    return (jnp.concatenate([jnp.ones((CTX, HP), F32), cos], axis=0),
            jnp.concatenate([jnp.zeros((CTX, HP), F32), sin], axis=0))


def _lane_pad(v, n):
    return jnp.pad(v, (0, n - v.shape[0]))[None, :]


def _layer_weights(w_in, w_q_b, w_kv_b, conv_w, pool_w):
    kv = w_kv_b.reshape(QL, H, NOPE + VH)
    wbd = jnp.concatenate([jnp.pad(pool_w[g], ((0, 0), (64 * g, PD - 64 * (g + 1)))) for g in range(4)], axis=0)
    return dict(
        w_in=_arrange_w_in(w_in).astype(BF),
        wq=_pad_heads(w_q_b, QK).astype(BF),
        wk=_pad_heads(kv[:, :, :NOPE].reshape(QL, H * NOPE), NOPE).astype(BF),
        wv=_pad_heads(kv[:, :, NOPE:].reshape(QL, H * VH), VH).astype(BF),
        conv_w8=jnp.pad(conv_w, ((0, 4), (0, 0))), wbd=wbd)


def _layer_fwd(R, x, mod, lw, sp, late, cos, sin):
    B, T = R.B, R.T
    z, xbc_raw, qa, kva, pool_in, kr, dt_raw, h1 = _in_proj(R, x, mod, sp["nw1"], lw["w_in"])
    xbc, dt2 = _ssd_prep(R, xbc_raw, dt_raw, lw["conv_w8"], sp["conv_b"], sp["dtb"])
    yf, yb, hin_f, hin_b = _ssd_scan(B, T, xbc, dt2, sp["alog2"])
    qt, kt, q_t, k_t, v_t, cq, ckv = _mla_prep(R, qa, kva, kr, sp["qnw"], sp["kvnw"], lw["wq"], lw["wk"], lw["wv"], cos, sin)
    o, lse = _flash_fwd(B, T, q_t, kt, v_t)
    w_out, w1, w2 = late(o)
    w_out = _arrange_w_out(w_out).astype(BF)
    xmid, cat, mixbf = _out_proj(R, x, mod, yf, yb, xbc, z, o, pool_in, lw["wbd"], sp["pscale"], sp["dsk"], sp["snw"], w_out)
    xo, h2, ubf, ybf = _mlp_fwd(R, xmid, mod, sp["nw2"], w1, w2)
    saved = dict(x=x, z=z, xbc_raw=xbc_raw, qa=qa, kva=kva, pool_in=pool_in, dt_raw=dt_raw, h1=h1, xbc=xbc, dt2=dt2,
                 yf=yf, yb=yb, hin_f=hin_f, hin_b=hin_b, qt=qt, kt=kt, q_t=q_t, k_t=k_t, v_t=v_t, cq=cq, ckv=ckv, o=o, lse=lse,
                 cat=cat, mixbf=mixbf, xmid=xmid, h2=h2, ubf=ubf, ybf=ybf, w_out=w_out, w1=w1, w2=w2)
    return xo, saved


def _layer_bwd(R, dxo, sv, mod, lw, sp, cos, sin, on_mlp=None, on_mid=None, latent_only=False):
    B, T = R.B, R.T
    dxm, du, abf, dyb, dmod_a, dnw2 = _mlp_bwd(R, dxo, sv["xmid"], sv["ubf"], sv["ybf"], mod, sp["nw2"], sv["w1"], sv["w2"])
    g_w1 = _tn_matmul("dw_mlp1", sv["h2"], du, 4 * FSH, sub=4)
    g_w2 = _tn_matmul("dw_mlp2", abf, dyb, D)[0].reshape(NDEV, FSH, D)
    snw = sp["snw"]
    tok = on_mlp(g_w1, g_w2) if on_mlp is not None else None
    if tok is not None:
        snw = snw + tok
    dmix, dz, dyt, dxsk, qs, dos, do_t, dpo, dmod_b, dvec_o = _out_bwd(R, dxm, mod, sv["mixbf"], sv["yf"], sv["yb"], sv["xbc"], sv["z"],
                                                                 sv["o"], sv["qt"], sv["lse"], sp["dsk"], snw, sv["w_out"])
    g_wout = _unarrange_w_out(_tn_matmul("dw_out", sv["cat"], dmix, D)[0])
    dpool_in, g_wbd, dpsc = _pool_bwd(R, dpo, sv["pool_in"], lw["wbd"], sp["pscale"])
    dqt, dkt, dvt = _flash_bwd(B, T, qs, sv["kt"], dos, sv["q_t"], sv["k_t"], sv["v_t"], do_t)
    dqp, dkvb, dqa, dkva, dkr, dnw_qk = _mla_bwd(R, dqt, dkt, dvt, sv["qa"], sv["kva"], sp["qnw"], sp["kvnw"],
                                                 lw["wq"], lw["wk"], lw["wv"], cos, sin)
    g_wq = _unpad_heads(_tn_matmul("dw_q", sv["cq"], dqp, QW)[0], QK)
    g_kv = _tn_matmul("dw_kv", sv["ckv"], dkvb, 2 * QW)[0]
    g_wk = _unpad_heads(g_kv[:, :QW], NOPE).reshape(QL, H, NOPE)
    g_wv = _unpad_heads(g_kv[:, QW:], VH).reshape(QL, H, VH)
    g_wkv = jnp.concatenate([g_wk, g_wv], axis=2).reshape(QL, H * (NOPE + VH))
    alog2 = sp["alog2"]
    tok = on_mid(g_wout, g_wq, g_wkv) if on_mid is not None else None
    if tok is not None:
        alog2 = alog2 + tok
    dxbc_f, dxbc_b, ddt_f, ddt_b, da = _ssd_scan_bwd(B, T, dyt, sv["xbc"], sv["dt2"], alog2, sv["hin_f"], sv["hin_b"])
    dpre, ddtr, dcw, dcb, ddtb = _ssd_prep_bwd(R, dxbc_f, dxbc_b, dxsk, ddt_f, ddt_b, sv["xbc_raw"], sv["dt_raw"], lw["conv_w8"],
                                                sp["conv_b"], sp["dtb"])
    dx, dproj, dmod_c, dnw1 = _in_bwd(R, dxm, sv["x"], mod, sp["nw1"], dz, dpre, dqa, dkva, dpool_in, dkr, ddtr,
                                      lw["conv_w8"], lw["w_in"], latent_only)
    g_win = _unarrange_w_in(jnp.concatenate(list(_tn_matmul("dw_in", sv["h1"], dproj, PC)), axis=1))
    a2 = -jnp.exp(sp["alog2"][:, 0, :H])
    small = dict(
        norm1_w=dnw1[0], norm2_w=dnw2[0], conv_w=dcw[0:4], conv_b=dcb[0], dt_bias=ddtb[0, :2 * H].reshape(2, H),
        a_log=jnp.sum(da[:, :, 0, :H], axis=0) * a2, ssd_d=jnp.sum(dvec_o[1].reshape(H, P), axis=1), ssd_norm_w=dvec_o[0],
        q_a_norm_w=dnw_qk[0], kv_a_norm_w=dnw_qk[1],
        pool_w=jnp.stack([g_wbd[64 * g:64 * (g + 1), 64 * g:64 * (g + 1)] for g in range(4)]), pool_scale=dpsc[0])
    big = dict(w_in=g_win, w_out=g_wout, w_q_b=g_wq, w_kv_b=g_wkv, w_mlp1=g_w1, w_mlp2=g_w2)
    return dx, big, small, dmod_a + dmod_b + dmod_c


def _small_params(l, norm1_w, norm2_w, conv_b, dt_bias, a_log, ssd_d, ssd_norm_w, q_a_norm_w, kv_a_norm_w, pool_scale):
    alog2 = jnp.broadcast_to(jnp.pad(a_log[l], ((0, 0), (0, HP - H)))[:, None, :], (2, 8, HP))
    return dict(nw1=norm1_w[l][None], nw2=norm2_w[l][None], conv_b=conv_b[l][None],
                dtb=_lane_pad(dt_bias[l].reshape(2 * H), HP), alog2=alog2,
                dsk=jnp.repeat(ssd_d[l], P)[None], snw=ssd_norm_w[l][None], qnw=q_a_norm_w[l][None],
                kvnw=kv_a_norm_w[l][None], pscale=pool_scale[l][None])


SMALL_NAMES = ["mod_b", "norm1_w", "norm2_w", "conv_b", "dt_bias", "a_log", "ssd_d", "ssd_norm_w", "q_a_norm_w",
               "kv_a_norm_w", "pool_w", "pool_scale", "final_norm_w"]


def _pack(arrs):
    rows = []
    for a in arrs:
        f = a.reshape(-1).astype(F32)
        n = -(-f.shape[0] // HP) * HP
        rows.append(jnp.pad(f, (0, n - f.shape[0])).reshape(-1, HP))
    out = jnp.concatenate(rows, axis=0)
    pad = (-out.shape[0]) % 8
    return jnp.pad(out, ((0, pad), (0, 0)))


def _unpack(pack, like):
    outs, r = [], 0
    for a in like:
        n = math.prod(a.shape)
        nr = -(-n // HP)
        outs.append(pack[r:r + nr].reshape(-1)[:n].reshape(a.shape))
        r += nr
    return outs


def _local_step(x, ctx, target, mods, full_of, small_w, on_grads=None, on_mlp=None, on_mid=None):
    B, N = x.shape[0], x.shape[1]
    T = CTX + N
    R = _Rows(B, T)
    cos, sin = _rope_tables(T)
    xu = (ctx, x)
    L = len(mods)
    lws, sps, saves = [], [], []
    for l in range(L):
        f = full_of(l, xu)
        lws.append(_layer_weights(f["w_in"], f["w_q_b"], f["w_kv_b"], f["conv_w"], small_w["pool_w"][l]))
        sps.append(_small_params(l, *[small_w[k] for k in ["norm1_w", "norm2_w", "conv_b", "dt_bias", "a_log", "ssd_d",
                                                          "ssd_norm_w", "q_a_norm_w", "kv_a_norm_w", "pool_scale"]]))
        xu, sv = _layer_fwd(R, xu, mods[l], lws[l], sps[l], f["late"], cos, sin)
        saves.append(sv)
    dx, loss8, dfnw = _loss_head(R, xu, target, small_w["final_norm_w"][None])
    bigs, smalls, dmods = [None] * L, [None] * L, [None] * L
    for l in reversed(range(L)):
        hook = functools.partial(on_mlp, l) if on_mlp is not None else None
        hook2 = functools.partial(on_mid, l) if on_mid is not None else None
        dx, bigs[l], smalls[l], dmods[l] = _layer_bwd(R, dx, saves[l], mods[l], lws[l], sps[l], cos, sin, hook, hook2, l == 0)
        if on_grads is not None:
            bigs[l], tok = on_grads(l, bigs[l], dx)
            if tok is not None:
                sps[l - 1] = dict(sps[l - 1], nw2=sps[l - 1]["nw2"] + tok)
    return loss8[0, 0], dx, bigs, smalls, dfnw[0], dmods


def kernel(x, c, ctx, c_ctx, mod_w, mod_b, norm1_w, norm2_w, w_in, conv_w, conv_b, dt_bias, a_log, ssd_d, ssd_norm_w, q_a_norm_w, w_q_b, kv_a_norm_w, w_kv_b, pool_w, pool_scale, w_out, w_mlp1, w_mlp2, final_norm_w, loss_target, m_c_ctx, m_mod_w, m_mod_b, m_norm1_w, m_norm2_w, m_w_in, m_conv_w, m_conv_b, m_dt_bias, m_a_log, m_ssd_d, m_ssd_norm_w, m_q_a_norm_w, m_w_q_b, m_kv_a_norm_w, m_w_kv_b, m_pool_w, m_pool_scale, m_w_out, m_w_mlp1, m_w_mlp2, m_final_norm_w, v_c_ctx, v_mod_w, v_mod_b, v_norm1_w, v_norm2_w, v_w_in, v_conv_w, v_conv_b, v_dt_bias, v_a_log, v_ssd_d, v_ssd_norm_w, v_q_a_norm_w, v_w_q_b, v_kv_a_norm_w, v_w_kv_b, v_pool_w, v_pool_scale, v_w_out, v_w_mlp1, v_w_mlp2, v_final_norm_w):
    args = dict(locals())
    B = x.shape[0]
    L = mod_w.shape[0]
    me = _dev_index(*_my_pos())
    CS = mod_w.shape[2]

    big_names = ["w_in", "w_out", "w_q_b", "w_kv_b", "w_mlp1", "w_mlp2"]
    shards = dict(zip(big_names, _cast_all("cast_weights", [args[n] for n in big_names], BF)))
    early, late_names = ["w_in", "w_q_b", "w_kv_b"], ["w_out", "w_mlp1", "w_mlp2"]
    g0 = _all_gather("gather_weights", [c, conv_w] + [shards[n][0] for n in early])
    c_all, convw_all = g0[0], g0[1]
    gathered = {0: dict(zip(early, g0[2:]))}
    cs = jnp.concatenate([c_all.reshape(NDEV * B, D), c_ctx[None], jnp.zeros((24 - NDEV * B - 1, D), F32)], axis=0)
    m_loc = _adaln_fwd(cs, mod_w)
    m_all = _all_gather("gather_mod", [m_loc])[0]
    m_full = jnp.moveaxis(m_all, 0, 2).reshape(L, 24, NDEV * CS) + mod_b[:, None, :]
    pending = {}
    tok = jnp.zeros((), F32)
    res = _exchange_start("gather_start_0", 1, [shards[n][0] for n in late_names], gather=True)
    pending[0] = res[:4]
    tok = tok + res[4][0, 0]
    for l in range(1, L):
        res = _exchange_start("gather_start_%d" % l, 1 + l, [shards[n][l] for n in early + late_names], gather=True)
        pending[l] = res[:4]
        tok = tok + res[4][0, 0]
    mods = []
    for l in range(L):
        ex = lax.dynamic_slice(m_full[l], (me * B, 0), (B, 6 * D)).reshape(B, 6, D)
        cc = jnp.broadcast_to(m_full[l, NDEV * B].reshape(1, 6, D), (B, 6, D))
        mods.append(jnp.pad(jnp.stack([cc, ex], axis=1), ((0, 0), (0, 0), (0, 2), (0, 0))) + tok)

    def full_of(l, xu):
        if l > 0:
            own, lands = _exchange_wait("gather_wait_%d" % l, *pending.pop(l), xu, gather=True)
            gathered[l] = dict(zip(early + late_names, _with_own(lands, own, me)))

        def late(after):
            if l == 0:
                own, lands = _exchange_wait("gather_wait_0", *pending.pop(0), after, gather=True)
                gathered[0].update(zip(late_names, _with_own(lands, own, me)))
            gl = gathered[l]
            return gl["w_out"].reshape(D, D), gl["w_mlp1"], gl["w_mlp2"]

        g = gathered[l]
        return dict(
            w_in=g["w_in"].reshape(D, IN_COLS),
            w_q_b=jnp.moveaxis(g["w_q_b"], 0, 1).reshape(QL, H * QK),
            w_kv_b=jnp.moveaxis(g["w_kv_b"], 0, 1).reshape(QL, H * (NOPE + VH)),
            conv_w=jnp.moveaxis(convw_all[:, l], 0, 1).reshape(4, XBC), late=late)

    def grad_blocks(big, names):
        make = dict(
            w_in=lambda g: g.reshape(NDEV, D // NDEV, IN_COLS), w_out=lambda g: g.reshape(NDEV, D // NDEV, D),
            w_q_b=lambda g: jnp.moveaxis(g.reshape(QL, NDEV, -1), 1, 0),
            w_kv_b=lambda g: jnp.moveaxis(g.reshape(QL, NDEV, -1), 1, 0), w_mlp1=lambda g: g, w_mlp2=lambda g: g)
        return [make[n](big[n]).astype(BF) for n in names]

    sent, sent_mlp, sent_mid = {}, {}, {}
    rest_names, mid_names = ["w_in"], ["w_out", "w_q_b", "w_kv_b"]

    def on_mid(l, g_wout, g_wq, g_wkv):
        if l > 0:
            return None
        blocks_mid = grad_blocks(dict(w_out=g_wout, w_q_b=g_wq, w_kv_b=g_wkv), mid_names)
        res = _exchange_start("grads_mid_start_0", 2 + 2 * L, blocks_mid, gather=False)
        sent_mid[l] = res[:4]
        return res[4][0, 0]

    def on_mlp(l, g_w1, g_w2):
        if l > 0:
            return None
        res = _exchange_start("grads_mlp_start_0", 1 + 2 * L, [g_w1.astype(BF), g_w2.astype(BF)], gather=False)
        sent_mlp[l] = res[:4]
        return res[4][0, 0]

    def on_grads(l, big, dx):
        if l == 0:
            return grad_blocks(big, rest_names), None
        res = _exchange_start("grads_start_%d" % l, 1 + L + l, grad_blocks(big, big_names), gather=False)
        sent[l] = res[:4]
        return None, res[4][0, 0]

    small_w = {k: args[k] for k in SMALL_NAMES if k != "mod_b"}
    loss_part, grad_x, blocks, smalls, dfnw, dmods = _local_step(x, ctx, loss_target, mods, full_of, small_w, on_grads, on_mlp,
                                                                 on_mid)

    dm_ex = jnp.stack([dmods[l][:, 1, :6].reshape(B, 6 * D) for l in range(L)])
    dm_cc = jnp.stack([jnp.sum(dmods[l][:, 0, :6], axis=0).reshape(6 * D) for l in range(L)])
    small_parts = dict(
        mod_b=jnp.sum(dm_ex, axis=1) + dm_cc,
        **{k: jnp.stack([smalls[l][k] for l in range(L)]) for k in SMALL_NAMES[1:-1]},
        final_norm_w=dfnw, conv_w=jnp.stack([smalls[l]["conv_w"] for l in range(L)]), dm_cc=dm_cc)
    adam_grads = [small_parts[k] for k in SMALL_NAMES]
    extras = [small_parts["conv_w"], dm_cc, dm_ex, loss_part.reshape(1)]
    pack = jnp.concatenate([_pack(adam_grads), _pack(extras)], axis=0)
    pack_all = _all_gather("gather_small_grads", [pack])[0]
    wpack = _pack([args[k] for k in SMALL_NAMES])
    mpack = _pack([args["m_" + k] for k in SMALL_NAMES])
    vpack = _pack([args["v_" + k] for k in SMALL_NAMES])
    n_adam = wpack.shape[0]
    res_small = _adamw("adamw_small", [pack_all[:, :n_adam]], wpack, mpack, vpack, n_adam)
    small_out = [_unpack(r, [args[k] for k in SMALL_NAMES]) for r in res_small]
    ext_all = pack_all[:, n_adam:]
    g_conv_full, dm_cc_tot, _, loss1 = _unpack(_sum_blocks("sum_small_grads", ext_all), extras)
    loss = loss1[0]
    dm_ex_all = jnp.stack([_unpack(ext_all[k], extras)[2] for k in range(NDEV)], axis=1)
    dm_rows = jnp.concatenate([dm_ex_all.reshape(L, NDEV * B, 6 * D), dm_cc_tot[:, None, :],
                               jnp.zeros((L, 24 - NDEV * B - 1, 6 * D), F32)], axis=1)
    dm_loc = lax.dynamic_slice(dm_rows, (0, 0, me * CS), (L, 24, CS))
    g_modw, gc_part = _adaln_bwd(cs, dm_loc, mod_w)
    gc_all = _all_gather("gather_cctx_grad", [gc_part[NDEV * B:NDEV * B + 8]])[0]
    cpad = lambda a: jnp.pad(a[None], ((0, 7), (0, 0)))
    res_cc = _adamw("adamw_cctx", [gc_all], cpad(c_ctx), cpad(m_c_ctx), cpad(v_c_ctx), 8)
    cc_out = [r[0] for r in res_cc]

    def waited(name, handles, names):
        srcs, lands = _exchange_wait(name, *handles, grad_x, gather=False)
        own = [lax.dynamic_index_in_dim(s, me, 0, keepdims=False) for s in srcs]
        return dict(zip(names, _with_own(lands, own, me)))

    recv = {0: dict(zip(rest_names, _all_to_all("exchange_grads", blocks[0])))}
    recv[0].update(waited("grads_mlp_wait_0", sent_mlp.pop(0), ["w_mlp1", "w_mlp2"]))
    recv[0].update(waited("grads_mid_wait_0", sent_mid.pop(0), mid_names))
    for l in range(1, L):
        recv[l] = waited("grads_wait_%d" % l, sent.pop(l), big_names)
    big_out = {}
    for name in big_names:
        w = args[name]
        Rr, C = math.prod(w.shape[:-1]), w.shape[-1]
        rl = Rr // L
        rt = rl if rl * C <= (1 << 17) else rl // 4
        res = _adamw("adamw_" + name, [recv[l][name].reshape(NDEV, rl, C) for l in range(L)], w.reshape(Rr, C),
                     args["m_" + name].reshape(Rr, C), args["v_" + name].reshape(Rr, C), rt)
        big_out[name] = [r.reshape(w.shape) for r in res]
    res = _adamw("adamw_mod_w", [g_modw.reshape(1, L * D, CS)], mod_w.reshape(L * D, CS), m_mod_w.reshape(L * D, CS),
                 v_mod_w.reshape(L * D, CS), L * D // 8)
    big_out["mod_w"] = [r.reshape(mod_w.shape) for r in res]
    CW = conv_w.shape[2]
    g_conv = lax.dynamic_slice(g_conv_full, (0, 0, me * CW), (L, 4, CW))
    res = _adamw("adamw_conv_w", [g_conv.reshape(1, L * 4, CW)], conv_w.reshape(L * 4, CW), m_conv_w.reshape(L * 4, CW),
                 v_conv_w.reshape(L * 4, CW), L * 4)
    big_out["conv_w"] = [r.reshape(conv_w.shape) for r in res]

    weights = ["c_ctx", "mod_w", "mod_b", "norm1_w", "norm2_w", "w_in", "conv_w", "conv_b", "dt_bias", "a_log", "ssd_d",
               "ssd_norm_w", "q_a_norm_w", "w_q_b", "kv_a_norm_w", "w_kv_b", "pool_w", "pool_scale", "w_out", "w_mlp1",
               "w_mlp2", "final_norm_w"]
    outs = [loss, grad_x]
    for kind in range(4):
        for name in weights:
            if name == "c_ctx":
                outs.append(cc_out[kind])
            elif name in big_out:
                outs.append(big_out[name][kind])
            else:
                outs.append(small_out[kind][SMALL_NAMES.index(name)])
    return tuple(outs)
```

```python
import functools
import math

import jax
import jax.numpy as jnp
from jax import lax
from jax.experimental import pallas as pl
from jax.experimental.pallas import tpu as pltpu

F32 = jnp.float32
BF = jnp.bfloat16
MXU = BF

D = 1024
CTX = 256
GRID_W = 64
EPS = 1e-6
H = 6
P = 64
SSD_IN = 384
NST = 128
XBC = 896
CH = 128
QL = 256
NOPE = 64
ROPE = 32
VH = 64
QK = 96
HP = 128
QW = H * HP
PD = 256
FF = 4096
IN_COLS = 2092
ROPE_THETA = 10000.0
PC = 2304
A_Z, A_XBC, A_QA, A_KVA, A_POOL, A_KR, A_DT = 0, 384, 1280, 1536, 1792, 2048, 2176
MIXW = SSD_IN + QW + PD
NDEV = 8
FSH = FF // NDEV
TM = 256
TQ = 256
TK = 256
VMEM_CAP = 64 * 1024 * 1024
ADAM_LR, ADAM_B1, ADAM_B2, ADAM_EPS, ADAM_WD, ADAM_STEP = 0.001, 0.9, 0.999, 1e-08, 0.01, 10


def _nbytes(shape, dtype):
    n = 1
    for s in shape:
        if s is not None:
            n *= s
    return n * jnp.dtype(dtype).itemsize


def _params(sem, block_bytes, extra=0):
    lim = min(2 * block_bytes + extra + (8 << 20), VMEM_CAP - (6 << 20))
    return pltpu.CompilerParams(dimension_semantics=sem, vmem_limit_bytes=int(lim))


def _hbm(arrays):
    return [pltpu.with_memory_space_constraint(a, pltpu.HBM) for a in arrays]


def _dot(a, b):
    return jnp.dot(a.astype(MXU), b.astype(MXU), preferred_element_type=F32)


def _dot_nt(a, b):
    return lax.dot_general(a.astype(MXU), b.astype(MXU), (((1,), (1,)), ((), ())), preferred_element_type=F32)


def _dot_tn(a, b):
    return lax.dot_general(a.astype(MXU), b.astype(MXU), (((0,), (0,)), ((), ())), preferred_element_type=F32)


def _dot01(m01, x):
    b16 = jnp.bfloat16
    m = m01.astype(b16)
    hi = x.astype(b16)
    r1 = x - hi.astype(F32)
    mid = r1.astype(b16)
    lo = (r1 - mid.astype(F32)).astype(b16)
    f = lambda v: jnp.dot(m, v, preferred_element_type=F32)
    return f(hi) + f(mid) + f(lo)


def _sigmoid(x):
    return 1.0 / (1.0 + jnp.exp(-x))


def _silu(x):
    return x * _sigmoid(x)


def _dsilu(x):
    s = _sigmoid(x)
    return s * (1.0 + x * (1.0 - s))


def _iota(shape, dim):
    return lax.broadcasted_iota(jnp.int32, shape, dim)


def _row(ref, k):
    blk = ref[...]
    return jnp.sum(jnp.where(_iota(blk.shape, 0) == k, blk, 0.0), axis=0, keepdims=True)


def _shift_rows(x, k):
    n = x.shape[0]
    return pltpu.roll(x, (-k) % n, axis=0)


class _Rows:
    def __init__(self, B, T):
        self.B, self.T = B, T
        self.nt = T // TM
        self.nct = CTX // TM

    def row(self, F):
        return pl.BlockSpec((None, TM, F), lambda b, i: (b, i, 0))

    def row2(self, F):
        return pl.BlockSpec((None, 2, TM, F), lambda b, i: (b, 0, i, 0))

    def prev8(self, F):
        return pl.BlockSpec((None, 8, F), lambda b, i: (b, jnp.maximum(i * (TM // 8) - 1, 0), 0))

    def next8(self, F):
        last = self.T // 8 - 1
        return pl.BlockSpec((None, 8, F), lambda b, i: (b, jnp.minimum((i + 1) * (TM // 8), last), 0))

    def mod(self):
        nct = self.nct
        return pl.BlockSpec((None, None, 8, D), lambda b, i: (b, jnp.where(i < nct, 0, 1), 0, 0))

    def const(self, shape):
        z = (0,) * len(shape)
        return pl.BlockSpec(tuple(shape), lambda b, i: z)

    def tgt(self, F):
        nct = self.nct
        return pl.BlockSpec((None, TM, F), lambda b, i: (b, jnp.maximum(i - nct, 0), 0))

    def xin(self, x):
        if isinstance(x, tuple):
            nct = self.nct
            return [(x[0], pl.BlockSpec((None, TM, D), lambda b, i: (b, jnp.minimum(i, nct - 1), 0))), (x[1], self.tgt(D))]
        return [(x, self.row(D))]

    def xval(self, i, refs):
        if len(refs) == 2:
            return jnp.where(i < self.nct, refs[0][...], refs[1][...])
        return refs[0][...]

    def call(self, name, body, ins, outs, scratch=(), extra=0):
        arrays = [a for a, _ in ins]
        in_specs = [s for _, s in ins]
        out_shape = [pltpu.HBM(o.shape, o.dtype) for o, _ in outs]
        out_specs = [s for _, s in outs]
        bb = 0
        for a, s in list(ins) + list(outs):
            if s.block_shape is not None:
                bb += _nbytes(s.block_shape, a.dtype)
        return pl.pallas_call(
            functools.partial(body), name=name, grid=(self.B, self.nt),
            in_specs=in_specs, out_specs=out_specs, out_shape=out_shape, scratch_shapes=list(scratch),
            compiler_params=_params(("arbitrary", "arbitrary"), bb, extra),
        )(*_hbm(arrays))

    def first_of_stream(self, i):
        return jnp.logical_or(i == 0, i == self.nct)

    def last_of_stream(self, i):
        return jnp.logical_or(i == self.nct - 1, i == self.nt - 1)

    def ext(self, i, prev_ref, cur, next_ref):
        pv = prev_ref[...].astype(F32) * jnp.where(self.first_of_stream(i), 0.0, 1.0)
        nx = next_ref[...].astype(F32) * jnp.where(self.last_of_stream(i), 0.0, 1.0)
        return jnp.concatenate([pv, cur, nx], axis=0)

    def stream_pos(self, i, rows):
        start = jnp.where(i < self.nct, 0, CTX)
        n = jnp.where(i < self.nct, CTX, self.T - CTX)
        t = i * TM - 8 - start + _iota((rows, 1), 0)
        return t, n


def _sds(shape, dtype):
    return jax.ShapeDtypeStruct(tuple(shape), dtype)


def _out(shape, dtype):
    return pltpu.HBM(tuple(shape), dtype)


def _norm_mod(x, nw, sh, sc):
    r = lax.rsqrt(jnp.mean(x * x, axis=-1, keepdims=True) + EPS)
    xn = x * r
    return xn * nw * (1.0 + sc) + sh, xn, r


def _norm_mod_bwd(dh, xn, r, nw, sc):
    dsh = jnp.sum(dh, axis=0, keepdims=True)
    dsc = jnp.sum(dh * (xn * nw), axis=0, keepdims=True)
    dnw = jnp.sum(dh * (1.0 + sc) * xn, axis=0, keepdims=True)
    dxn = dh * nw * (1.0 + sc)
    dx = r * (dxn - xn * jnp.mean(dxn * xn, axis=-1, keepdims=True))
    return dx, dsh, dsc, dnw


def _acc_rows(ref, first, rows):
    rid = _iota(ref.shape, 0)
    upd = jnp.zeros(ref.shape, F32)
    for k, v in rows.items():
        upd = upd + jnp.where(rid == k, v, 0.0)

    @pl.when(first)
    def _():
        ref[...] = upd

    @pl.when(jnp.logical_not(first))
    def _():
        ref[...] += upd


def _in_proj(R, x, mod, nw1, w_arr):
    B, T = R.B, R.T

    xin = R.xin(x)

    def body(*refs):
        x_refs = refs[:len(xin)]
        mod_ref, nw_ref, w_ref, z_ref, xbc_ref, qa_ref, kva_ref, pool_ref, kr_ref, dt_ref, h_ref = refs[len(xin):]
        h, _, _ = _norm_mod(R.xval(pl.program_id(1), x_refs), nw_ref[...], _row(mod_ref, 0), _row(mod_ref, 1))
        hb = h.astype(BF)
        h_ref[...] = hb
        p = jnp.dot(hb, w_ref[...], preferred_element_type=F32)
        z_ref[...] = p[:, A_Z:A_XBC]
        xbc_ref[...] = p[:, A_XBC:A_QA]
        qa_ref[...] = p[:, A_QA:A_KVA]
        kva_ref[...] = p[:, A_KVA:A_POOL]
        pool_ref[...] = p[:, A_POOL:A_KR]
        kr_ref[...] = p[:, A_KR:A_DT]
        dt_ref[...] = p[:, A_DT:PC]

    widths = [SSD_IN, XBC, QL, QL, PD, HP, HP]
    outs = [(_sds((B, T, w), F32), R.row(w)) for w in widths] + [(_sds((B, T, D), BF), R.row(D))]
    return R.call("in_proj", body,
                  xin + [(mod, R.mod()), (nw1, R.const((1, D))), (w_arr, R.const((D, PC)))],
                  outs, extra=8 << 20)


def _conv_pre(ext, w_ref, b_ref):
    return (_row(w_ref, 0) * _shift_rows(ext, -1)[8:8 + TM] + _row(w_ref, 1) * ext[8:8 + TM]
            + _row(w_ref, 2) * _shift_rows(ext, 1)[8:8 + TM] + _row(w_ref, 3) * _shift_rows(ext, 2)[8:8 + TM]
            + b_ref[...])


def _softplus(x):
    return jnp.maximum(x, 0.0) + jnp.log(1.0 + jnp.exp(-jnp.abs(x)))


def _ssd_prep(R, xbc_raw, dt_raw, conv_w8, conv_b, dtb):
    B, T = R.B, R.T

    def body(raw_ref, pv_ref, nx_ref, dtr_ref, w_ref, b_ref, dtb_ref, xbc_ref, dt_ref):
        i = pl.program_id(1)
        ext = R.ext(i, pv_ref, raw_ref[...], nx_ref)
        xbc_ref[...] = _silu(_conv_pre(ext, w_ref, b_ref))
        lane = _iota((TM, HP), 1)
        dtv = _softplus(dtr_ref[...] + dtb_ref[...])
        keep = lane < H
        dt_ref[0] = jnp.where(keep, dtv, 0.0)
        dt_ref[1] = jnp.where(keep, pltpu.roll(dtv, HP - H, axis=1), 0.0)

    return R.call("ssd_prep", body,
                  [(xbc_raw, R.row(XBC)), (xbc_raw, R.prev8(XBC)), (xbc_raw, R.next8(XBC)), (dt_raw, R.row(HP)),
                   (conv_w8, R.const((8, XBC))), (conv_b, R.const((1, XBC))), (dtb, R.const((1, HP)))],
                  [(_sds((B, T, XBC), F32), R.row(XBC)), (_sds((B, 2, T, HP), F32), R.row2(HP))], extra=12 << 20)


def _chunk_index(d, s, nc, nctc):
    if d == 0:
        return s
    return jnp.where(s < nctc, nctc - 1 - s, nc - 1 - (s - nctc))


def _dot01_r(x, m01, passes=3):
    b16 = jnp.bfloat16
    m = m01.astype(b16)
    out, rest = None, x
    for _ in range(passes):
        part = rest.astype(b16)
        rest = rest - part.astype(F32)
        term = jnp.dot(part, m, preferred_element_type=F32)
        out = term if out is None else out + term
    return out


def _scan_common(d, dtv, alog_ref):
    sel = _iota((HP, H * HP), 0) == (_iota((HP, H * HP), 1) >> 7)
    a_all = _dot01_r(-jnp.exp(alog_ref[d]), sel)
    dt_all = _dot01_r(dtv, sel)
    adt_all = dt_all * jnp.concatenate([a_all] * (CH // 8), axis=0)
    row = _iota((CH, CH), 0)
    col = _iota((CH, CH), 1)
    inc = col <= row if d == 0 else col >= row
    inc_t = row <= col if d == 0 else row >= col
    q_all = _dot01(inc, adt_all)
    return a_all, dt_all, adt_all, q_all, inc, inc_t


def _head_decay(h, adt_all, q_all, inc, inc_t):
    q = q_all[:, HP * h:HP * (h + 1)]
    q_t = q.T
    qtot = jnp.sum(adt_all[:, HP * h:HP * (h + 1)], axis=0, keepdims=True)
    lm = jnp.where(inc, jnp.exp(q - q_t), 0.0)
    lm_t = jnp.where(inc_t, jnp.exp(q_t - q), 0.0)
    return lm, lm_t, jnp.exp(q), jnp.exp(qtot), jnp.exp(qtot - q)


def _ssd_scan(B, T, xbc, dt2, alog2):
    nc, nctc = T // CH, CTX // CH

    def chain(d, xbc_ref, dt_ref, alog_ref, y_ref, hin_ref, hs):
        xbc_v = xbc_ref[...]
        dtv = dt_ref[...]
        blk = alog_ref[d]
        adt = dtv * -jnp.exp(jnp.sum(jnp.where(_iota(blk.shape, 0) == 0, blk, 0.0), axis=0, keepdims=True))
        row, col = _iota((CH, CH), 0), _iota((CH, CH), 1)
        inc = col <= row if d == 0 else col >= row
        inc_tf = (row <= col if d == 0 else row >= col).astype(F32)
        q = _dot01(inc, adt)
        hin_ref[...] = hs[d]
        for g in range(2):
            bg = xbc_v[:, SSD_IN + NST * g:SSD_IN + NST * (g + 1)]
            cg = xbc_v[:, SSD_IN + 2 * NST + NST * g:SSD_IN + 2 * NST + NST * (g + 1)]
            gm = _dot_nt(cg, bg)
            for r in range(3):
                h = 3 * g + r
                onehot = (_iota((1, HP), 1) == h).astype(F32)
                adt_h = jnp.sum(adt * onehot, axis=1, keepdims=True)
                qc = jnp.sum(q * onehot, axis=1, keepdims=True)
                dt_h = jnp.sum(dtv * onehot, axis=1, keepdims=True)
                qr = jnp.sum(adt_h * inc_tf, axis=0, keepdims=True)
                qtot = jnp.sum(adt_h, axis=0, keepdims=True)
                lm = jnp.where(inc, jnp.exp(qc - qr), 0.0)
                xh = xbc_v[:, P * h:P * (h + 1)] * dt_h
                hprev = hs[d, P * h:P * (h + 1), :]
                y_ref[:, P * h:P * (h + 1)] = _dot(gm * lm, xh) + jnp.exp(qc) * _dot_nt(cg, hprev)
                hs[d, P * h:P * (h + 1), :] = jnp.exp(qtot) * hprev + _dot_tn(xh * jnp.exp(qtot - qc), bg)

    def body(xf_ref, xb_ref, dtf_ref, dtb_ref, alog_ref, yf_ref, yb_ref, hf_ref, hb_ref, hs):
        @pl.when(pl.program_id(1) == 0)
        def _():
            hs[...] = jnp.zeros(hs.shape, F32)

        chain(0, xf_ref, dtf_ref, alog_ref, yf_ref, hf_ref, hs)
        chain(1, xb_ref, dtb_ref, alog_ref, yb_ref, hb_ref, hs)

    cidx = lambda d: (lambda s: _chunk_index(d, s, nc, nctc))
    specs = lambda d: dict(
        xbc=pl.BlockSpec((None, CH, XBC), lambda b, s: (b, cidx(d)(s), 0)),
        dt=pl.BlockSpec((None, None, CH, HP), lambda b, s: (b, d, cidx(d)(s), 0)),
        y=pl.BlockSpec((None, CH, SSD_IN), lambda b, s: (b, cidx(d)(s), 0)),
        h=pl.BlockSpec((None, None, SSD_IN, NST), lambda b, s: (b, cidx(d)(s), 0, 0)))
    f, r = specs(0), specs(1)
    bb = 2 * (_nbytes((CH, XBC), F32) + _nbytes((CH, HP), F32) + _nbytes((CH, SSD_IN), F32) + _nbytes((SSD_IN, NST), F32))
    return pl.pallas_call(
        functools.partial(body), name="ssd_scan", grid=(B, nc),
        in_specs=[f["xbc"], r["xbc"], f["dt"], r["dt"], pl.BlockSpec((2, 8, HP), lambda b, s: (0, 0, 0))],
        out_specs=[f["y"], r["y"], f["h"], r["h"]],
        out_shape=[_out((B, T, SSD_IN), F32)] * 2 + [_out((B, nc, SSD_IN, NST), F32)] * 2,
        scratch_shapes=[pltpu.VMEM((2, SSD_IN, NST), F32)],
        compiler_params=_params(("arbitrary",) * 2, bb, 12 << 20),
    )(*_hbm([xbc, xbc, dt2, dt2, alog2]))


def _swap8(u):
    lane = _iota(u.shape, 1)
    n = u.shape[1]
    return jnp.where((lane & 15) < 8, pltpu.roll(u, n - 8, axis=1), pltpu.roll(u, 8, axis=1))


def _rope(u, cos, sin_signed):
    return u * cos + _swap8(u) * sin_signed


def _rms(x, w):
    r = lax.rsqrt(jnp.mean(x * x, axis=-1, keepdims=True) + EPS)
    xh = x * r
    return xh * w, xh, r


def _rms_bwd(dy, xh, r, w):
    dw = jnp.sum(dy * xh, axis=0, keepdims=True)
    dxh = dy * w
    return r * (dxh - xh * jnp.mean(dxh * xh, axis=-1, keepdims=True)), dw


def _tile6(t):
    return jnp.concatenate([t] * H, axis=1)


def _per_head(fn, u):
    return jnp.concatenate([fn(u[:, HP * h:HP * (h + 1)]) for h in range(H)], axis=1)


def _mla_prep(R, qa, kva, kr, qnw, kvnw, wq, wk, wv, cos, sin):
    B, T = R.B, R.T
    scale = QK ** -0.5

    def body(qa_ref, kva_ref, kr_ref, qnw_ref, kvnw_ref, wq_ref, wk_ref, wv_ref, cos_ref, sin_ref,
             qt_ref, kt_ref, qtr_ref, ktr_ref, vtr_ref, cq_ref, ckv_ref):
        cq, _, _ = _rms(qa_ref[...], qnw_ref[...])
        ckv, _, _ = _rms(kva_ref[...], kvnw_ref[...])
        cqb, ckvb = cq.astype(BF), ckv.astype(BF)
        cq_ref[...] = cqb
        ckv_ref[...] = ckvb
        cos1, sin1 = cos_ref[...], sin_ref[...]
        q = _per_head(lambda u: _rope(u, cos1, sin1), jnp.dot(cqb, wq_ref[...], preferred_element_type=F32)) * scale
        qt_ref[...] = q.astype(BF)
        kk = _rope(kr_ref[...], cos1, sin1)
        hl = _iota((TM, QW), 1) & (HP - 1)
        k = jnp.dot(ckvb, wk_ref[...], preferred_element_type=F32) + _tile6(kk)
        k = jnp.where(jnp.logical_and(hl >= QK, hl < QK + 3), 1.0, k)
        kt_ref[...] = k.astype(BF)
        v = jnp.dot(ckvb, wv_ref[...], preferred_element_type=F32)
        v = jnp.where(jnp.logical_and(hl >= VH, hl < VH + 4), 1.0, v)
        for h in range(H):
            cols = slice(HP * h, HP * (h + 1))
            qtr_ref[h] = q[:, cols].T.astype(BF)
            ktr_ref[h] = k[:, cols].T.astype(BF)
            vtr_ref[h] = v[:, cols].T.astype(BF)

    tr = (_sds((B, H, HP, T), BF), pl.BlockSpec((None, H, HP, TM), lambda b, i: (b, 0, 0, i)))
    return R.call("mla_prep", body,
                  [(qa, R.row(QL)), (kva, R.row(QL)), (kr, R.row(HP)), (qnw, R.const((1, QL))), (kvnw, R.const((1, QL))),
                   (wq, R.const((QL, QW))), (wk, R.const((QL, QW))), (wv, R.const((QL, QW))),
                   (cos, pl.BlockSpec((TM, HP), lambda b, i: (i, 0))), (sin, pl.BlockSpec((TM, HP), lambda b, i: (i, 0)))],
                  [(_sds((B, T, QW), BF), R.row(QW))] * 2 + [tr] * 3 + [(_sds((B, T, QL), BF), R.row(QL))] * 2, extra=12 << 20)


def _flash_fwd(B, T, q_t, kt, v_t):
    nq, nk = T // TQ, T // TK
    HS = 2

    def body(q_ref, k_ref, v_ref, o_ref, lse_ref, s_scr):
        i = pl.program_id(2)

        def attend(nch):
            ms = []
            for hh in range(HS):
                q_tr = q_ref[hh]
                mrun = None
                for j in range(nch):
                    s = _dot(k_ref[TK * j:TK * (j + 1), HP * hh:HP * (hh + 1)], q_tr)
                    s_scr[hh, j] = s
                    mrun = s if mrun is None else jnp.maximum(mrun, s)
                ms.append(jnp.max(mrun, axis=0, keepdims=True))
            row = _iota((HP, TQ), 0)
            for hh in range(HS):
                acc_t = jnp.zeros((HP, TQ), F32)
                for j in range(nch):
                    acc_t = acc_t + _dot(v_ref[hh, :, TK * j:TK * (j + 1)], jnp.exp(s_scr[hh, j] - ms[hh]))
                l = jnp.sum(jnp.where(row == VH, acc_t, 0.0), axis=0, keepdims=True)
                o_ref[:, HP * hh:HP * (hh + 1)] = jnp.where(row < VH, acc_t / l, 0.0).T
                lse_ref[hh] = ms[hh] + jnp.log(l)

        @pl.when(i < CTX // TQ)
        def _():
            attend(CTX // TK)

        @pl.when(i >= CTX // TQ)
        def _():
            attend(nk)

    bb = HS * (_nbytes((TQ, HP), BF) + 2 * _nbytes((T, HP), BF) + 2 * _nbytes((TQ, HP), F32))
    return pl.pallas_call(
        functools.partial(body), name="flash_fwd", grid=(B, H // HS, nq),
        in_specs=[pl.BlockSpec((None, HS, HP, TQ), lambda b, h, i: (b, h, 0, i)),
                  pl.BlockSpec((None, T, HS * HP), lambda b, h, i: (b, 0, h)),
                  pl.BlockSpec((None, HS, HP, T), lambda b, h, i: (b, h, 0, 0))],
        out_specs=[pl.BlockSpec((None, TQ, HS * HP), lambda b, h, i: (b, i, h)),
                   pl.BlockSpec((None, HS, 1, TQ), lambda b, h, i: (b, h, 0, i))],
        out_shape=[_out((B, T, QW), F32), _out((B, H, 1, T), F32)],
        scratch_shapes=[pltpu.VMEM((HS, nk, TK, TQ), F32)],
        compiler_params=_params(("arbitrary",) * 3, bb, _nbytes((HS, nk, TK, TQ), F32) + (8 << 20)),
    )(*_hbm([q_t, kt, v_t]))


def _pool_terms(R, i, rows):
    t, n = R.stream_pos(i, rows)
    lane = _iota((1, PD), 1)
    half = jnp.where(lane < 64, 1, jnp.where(lane < 128, 2, jnp.where(lane < 192, 4, 8)))
    cnt = (jnp.minimum(t + half, n) - jnp.maximum(t - half, 0)).astype(F32)
    valid = jnp.logical_and(t >= 0, t < n)
    return jnp.where(valid, cnt, 1.0), valid.astype(F32), lane


def _lane_select(lane, a2, a4, a8, a16):
    return jnp.where(lane < 64, a2, jnp.where(lane < 128, a4, jnp.where(lane < 192, a8, a16)))


def _pool_centred(R, i, ext):
    cnt, valid, lane = _pool_terms(R, i, ext.shape[0])
    s2 = ext + _shift_rows(ext, -1)
    s4 = _shift_rows(s2, -1) + _shift_rows(s2, 1)
    s8 = _shift_rows(s4, -2) + _shift_rows(s4, 2)
    s16 = _shift_rows(s8, -4) + _shift_rows(s8, 4)
    return _lane_select(lane, s2, s4, s8, s16) / cnt - ext, cnt, valid, lane


def _group_mask():
    return _iota((1, SSD_IN), 1) < SSD_IN // 2


def _ssd_gate(yf_ref, yb_ref, xbc_ref, z_ref, dsk_ref):
    ytot = yf_ref[...] + yb_ref[...] + xbc_ref[:, 0:SSD_IN] * dsk_ref[...]
    z = z_ref[...]
    gz = ytot * _silu(z)
    g0 = _group_mask()
    sq = gz * gz
    s0 = jnp.sum(jnp.where(g0, sq, 0.0), axis=1, keepdims=True)
    s1 = jnp.sum(jnp.where(g0, 0.0, sq), axis=1, keepdims=True)
    half = SSD_IN // 2
    r = jnp.where(g0, lax.rsqrt(s0 / half + EPS), lax.rsqrt(s1 / half + EPS))
    return ytot, z, gz, r


def _out_proj(R, x, mod, yf, yb, xbc, z, o, pool_in, wbd, pscale, dsk, snw, wout):
    B, T = R.B, R.T
    xin = R.xin(x)

    def body(*refs):
        x_refs = refs[:len(xin)]
        (mod_ref, yf_ref, yb_ref, xbc_ref, z_ref, o_ref, u_ref, upv_ref, unx_ref, wbd_ref, psc_ref, dsk_ref, snw_ref, w_ref,
         xmid_ref, cat_ref, mix_ref) = refs[len(xin):]
        i = pl.program_id(1)
        _, _, gz, r = _ssd_gate(yf_ref, yb_ref, xbc_ref, z_ref, dsk_ref)
        dm, _, _, _ = _pool_centred(R, i, R.ext(i, upv_ref, u_ref[...], unx_ref))
        cat_ref[:, 0:SSD_IN] = (gz * r * snw_ref[...]).astype(BF)
        cat_ref[:, SSD_IN:SSD_IN + QW] = o_ref[...].astype(BF)
        cat_ref[:, SSD_IN + QW:MIXW] = (_dot(dm[8:8 + TM], wbd_ref[...]) * psc_ref[...]).astype(BF)
        mix = jnp.dot(cat_ref[...], w_ref[...], preferred_element_type=F32)
        mix_ref[...] = mix.astype(BF)
        xmid_ref[...] = R.xval(i, x_refs) + _row(mod_ref, 2) * mix

    return R.call("out_proj", body,
                  xin + [(mod, R.mod()), (yf, R.row(SSD_IN)), (yb, R.row(SSD_IN)), (xbc, R.row(XBC)), (z, R.row(SSD_IN)),
                         (o, R.row(QW)), (pool_in, R.row(PD)), (pool_in, R.prev8(PD)), (pool_in, R.next8(PD)),
                         (wbd, R.const((PD, PD))), (pscale, R.const((1, PD))),
                         (dsk, R.const((1, SSD_IN))), (snw, R.const((1, SSD_IN))), (wout, R.const((MIXW, D)))],
                  [(_sds((B, T, D), F32), R.row(D)), (_sds((B, T, MIXW), BF), R.row(MIXW)), (_sds((B, T, D), BF), R.row(D))],
                  extra=12 << 20)


def _load_once(first, pairs, sem):
    @pl.when(first)
    def _():
        cps = [pltpu.make_async_copy(src, dst, sem.at[k]) for k, (src, dst) in enumerate(pairs)]
        for cp in cps:
            cp.start()
        for cp in cps:
            cp.wait()


ANY = pl.BlockSpec(memory_space=pl.ANY)


def _mlp_fwd(R, xmid, mod, nw2, w1, w2):
    B, T = R.B, R.T

    def body(x_ref, mod_ref, nw_ref, w1_hbm, w2_hbm, xo_ref, h_ref, u_ref, y_ref, w1_v, w2_v, sem):
        first = jnp.logical_and(pl.program_id(0) == 0, pl.program_id(1) == 0)
        _load_once(first, [(w1_hbm, w1_v), (w2_hbm, w2_v)], sem)
        x = x_ref[...]
        h, _, _ = _norm_mod(x, nw_ref[...], _row(mod_ref, 3), _row(mod_ref, 4))
        hb = h.astype(BF)
        h_ref[...] = hb
        y = jnp.zeros((TM, D), F32)
        for j in range(NDEV):
            u = jnp.dot(hb, w1_v[j], preferred_element_type=F32)
            u_ref[:, FSH * j:FSH * (j + 1)] = u.astype(BF)
            a = jnp.square(jnp.maximum(u, 0.0))
            y = y + jnp.dot(a.astype(BF), w2_v[j], preferred_element_type=F32)
        y_ref[...] = y.astype(BF)
        xo_ref[...] = x + _row(mod_ref, 5) * y

    return R.call("mlp_fwd", body,
                  [(xmid, R.row(D)), (mod, R.mod()), (nw2, R.const((1, D))), (w1, ANY), (w2, ANY)],
                  [(_sds((B, T, D), F32), R.row(D)), (_sds((B, T, D), BF), R.row(D)), (_sds((B, T, FF), BF), R.row(FF)),
                   (_sds((B, T, D), BF), R.row(D))],
                  scratch=[pltpu.VMEM((NDEV, D, FSH), w1.dtype), pltpu.VMEM((NDEV, FSH, D), w2.dtype), pltpu.SemaphoreType.DMA((2,))],
                  extra=(2 * _nbytes((NDEV, D, FSH), BF)) + (8 << 20))


def _loss_head(R, x, tgt, fnw):
    B, T = R.B, R.T

    def body(x_ref, t_ref, w_ref, dx_ref, loss_ref, dw_ref):
        b, i = pl.program_id(0), pl.program_id(1)
        live = jnp.where(i >= R.nct, 1.0, 0.0)
        y, xh, r = _rms(x_ref[...], w_ref[...])
        err = (y - t_ref[...]) * live
        dy = err / D
        dxn, dw = _rms_bwd(dy, xh, r, w_ref[...])
        dx_ref[...] = dxn
        first = jnp.logical_and(b == 0, i == 0)
        part = 0.5 * jnp.sum(jnp.sum(err * err, axis=1, keepdims=True), axis=0, keepdims=True) / D
        _acc_rows(loss_ref, first, {0: jnp.broadcast_to(part, (1, HP))})
        _acc_rows(dw_ref, first, {0: dw})

    return R.call("loss_head", body,
                  [(x, R.row(D)), (tgt, R.tgt(D)), (fnw, R.const((1, D)))],
                  [(_sds((B, T, D), F32), R.row(D)), (_sds((8, HP), F32), R.const((8, HP))), (_sds((8, D), F32), R.const((8, D)))],
                  extra=8 << 20)


def _tn_matmul(name, a, b, tn, sub=1):
    B, T, K = a.shape
    N = b.shape[2]
    nk = 1
    while T % nk or (T // nk) > 1088 or (T // nk) % 16:
        nk += 1
    tk = T // nk
    kt = K if K <= 1536 else 1024
    w = tn // sub
    assert K % kt == 0 and N % tn == 0 and tn % sub == 0

    def body(a_ref, b_ref, o_ref, acc):
        first = jnp.logical_and(pl.program_id(2) == 0, pl.program_id(3) == 0)
        last = jnp.logical_and(pl.program_id(2) == B - 1, pl.program_id(3) == nk - 1)
        part = _dot_tn(a_ref[...], b_ref[...])

        @pl.when(first)
        def _():
            acc[...] = part

        @pl.when(jnp.logical_not(first))
        def _():
            acc[...] += part

        @pl.when(last)
        def _():
            for s in range(sub):
                o_ref[s] = acc[:, w * s:w * (s + 1)].astype(o_ref.dtype)

    bb = _nbytes((tk, kt), a.dtype) + _nbytes((tk, tn), b.dtype) + _nbytes((kt, tn), BF)
    return pl.pallas_call(
        functools.partial(body), name=name, grid=(N // tn, K // kt, B, nk),
        in_specs=[pl.BlockSpec((None, tk, kt), lambda j, kk, bi, t: (bi, t, kk)),
                  pl.BlockSpec((None, tk, tn), lambda j, kk, bi, t: (bi, t, j))],
        out_specs=pl.BlockSpec((sub, kt, w), lambda j, kk, bi, t: (j, kk, 0)),
        out_shape=_out((sub * N // tn, K, w), BF),
        scratch_shapes=[pltpu.VMEM((kt, tn), F32)],
        compiler_params=_params(("arbitrary",) * 4, bb, 2 * _nbytes((kt, tn), F32) + (8 << 20)),
    )(*_hbm([a, b]))


def _mlp_bwd(R, dxo, xmid, ubf, ybf, mod, nw2, w1, w2):
    B, T = R.B, R.T

    def body(dxo_ref, x_ref, u_ref, y_ref, mod_ref, nw_ref, w1_hbm, w2_hbm,
             dxm_ref, du_ref, a_ref, dy_ref, dmod_ref, dnw_ref, w1_v, w2_v, sem):
        b, i = pl.program_id(0), pl.program_id(1)
        _load_once(jnp.logical_and(b == 0, i == 0), [(w1_hbm, w1_v), (w2_hbm, w2_v)], sem)
        dxo = dxo_ref[...]
        _, xn, r = _norm_mod(x_ref[...], nw_ref[...], _row(mod_ref, 3), _row(mod_ref, 4))
        dyb = (dxo * _row(mod_ref, 5)).astype(BF)
        dy_ref[...] = dyb
        dg2 = jnp.sum(dxo * y_ref[...].astype(F32), axis=0, keepdims=True)
        dh = jnp.zeros((TM, D), F32)
        for j in range(NDEV):
            rl = jnp.maximum(u_ref[:, FSH * j:FSH * (j + 1)].astype(F32), 0.0)
            a_ref[:, FSH * j:FSH * (j + 1)] = (rl * rl).astype(BF)
            du = (_dot_nt(dyb, w2_v[j]) * (2.0 * rl)).astype(BF)
            du_ref[:, FSH * j:FSH * (j + 1)] = du
            dh = dh + _dot_nt(du, w1_v[j])
        dx, dsh, dsc, dnw = _norm_mod_bwd(dh, xn, r, nw_ref[...], _row(mod_ref, 4))
        dxm_ref[...] = dxo + dx
        _acc_rows(dmod_ref, R.first_of_stream(i), {3: dsh, 4: dsc, 5: dg2})
        _acc_rows(dnw_ref, jnp.logical_and(b == 0, i == 0), {0: dnw})

    return R.call("mlp_bwd", body,
                  [(dxo, R.row(D)), (xmid, R.row(D)), (ubf, R.row(FF)), (ybf, R.row(D)), (mod, R.mod()), (nw2, R.const((1, D))),
                   (w1, ANY), (w2, ANY)],
                  [(_sds((B, T, D), F32), R.row(D)), (_sds((B, T, FF), BF), R.row(FF)), (_sds((B, T, FF), BF), R.row(FF)),
                   (_sds((B, T, D), BF), R.row(D)), (_sds((B, 2, 8, D), F32), R.mod()), (_sds((8, D), F32), R.const((8, D)))],
                  scratch=[pltpu.VMEM((NDEV, D, FSH), w1.dtype), pltpu.VMEM((NDEV, FSH, D), w2.dtype), pltpu.SemaphoreType.DMA((2,))],
                  extra=(2 * _nbytes((NDEV, D, FSH), BF)) + (8 << 20))


def _minus_in_lanes(x, col, first):
    b16 = jnp.bfloat16
    hi = col.astype(b16).astype(F32)
    r1 = col - hi
    mid = r1.astype(b16).astype(F32)
    lane = _iota(x.shape, 1)
    return jnp.where(lane == first, -hi, jnp.where(lane == first + 1, -mid, jnp.where(lane == first + 2, mid - r1, x)))


def _out_bwd(R, dxm, mod, mixbf, yf, yb, xbc, z, o, qt, lse, dsk, snw, wout):
    B, T = R.B, R.T

    def body(dxm_ref, mod_ref, mix_ref, yf_ref, yb_ref, xbc_ref, z_ref, o_ref, q_ref, lse_ref, dsk_ref, snw_ref, w_ref,
             dmix_ref, dz_ref, dyt_ref, dxsk_ref, qs_ref, dos_ref, dotr_ref, dpo_ref, dmod_ref, dvec_ref):
        b, i = pl.program_id(0), pl.program_id(1)
        dxm = dxm_ref[...]
        dmixb = (dxm * _row(mod_ref, 2)).astype(BF)
        dmix_ref[...] = dmixb
        dg1 = jnp.sum(dxm * mix_ref[...].astype(F32), axis=0, keepdims=True)
        dcat = _dot_nt(dmixb, w_ref[...])
        do_v = dcat[:, SSD_IN:SSD_IN + QW]
        doo = do_v * o_ref[...]
        for h in range(H):
            cols = slice(HP * h, HP * (h + 1))
            dd = jnp.sum(doo[:, cols], axis=1, keepdims=True)
            dos_ref[:, cols] = _minus_in_lanes(do_v[:, cols], dd, VH + 1).astype(BF)
            lse_b = jnp.broadcast_to(lse_ref[h], (HP, TM)).T
            qs_ref[:, cols] = _minus_in_lanes(q_ref[:, cols].astype(F32), lse_b, QK).astype(BF)
            dotr_ref[h] = do_v[:, cols].T.astype(BF)
        dpo_ref[...] = dcat[:, SSD_IN + QW:MIXW]
        dsn = dcat[:, 0:SSD_IN]
        ytot, zv, gz, r = _ssd_gate(yf_ref, yb_ref, xbc_ref, z_ref, dsk_ref)
        gh = gz * r
        dsnw = jnp.sum(dsn * gh, axis=0, keepdims=True)
        dgh = dsn * snw_ref[...]
        g0 = _group_mask()
        pr = dgh * gh
        half = SSD_IN // 2
        m0 = jnp.sum(jnp.where(g0, pr, 0.0), axis=1, keepdims=True) / half
        m1 = jnp.sum(jnp.where(g0, 0.0, pr), axis=1, keepdims=True) / half
        dgz = r * (dgh - gh * jnp.where(g0, m0, m1))
        dyt = dgz * _silu(zv)
        dz_ref[...] = (dgz * ytot * _dsilu(zv)).astype(BF)
        dyt_ref[...] = dyt
        dxsk_ref[...] = dyt * dsk_ref[...]
        ddsk = jnp.sum(dyt * xbc_ref[:, 0:SSD_IN], axis=0, keepdims=True)
        _acc_rows(dmod_ref, R.first_of_stream(i), {2: dg1})
        _acc_rows(dvec_ref, jnp.logical_and(b == 0, i == 0), {0: dsnw, 1: ddsk})

    return R.call("out_bwd", body,
                  [(dxm, R.row(D)), (mod, R.mod()), (mixbf, R.row(D)), (yf, R.row(SSD_IN)), (yb, R.row(SSD_IN)), (xbc, R.row(XBC)),
                   (z, R.row(SSD_IN)),
                   (o, R.row(QW)), (qt, R.row(QW)), (lse, pl.BlockSpec((None, H, 1, TM), lambda b, i: (b, 0, 0, i))),
                   (dsk, R.const((1, SSD_IN))), (snw, R.const((1, SSD_IN))), (wout, R.const((MIXW, D)))],
                  [(_sds((B, T, D), BF), R.row(D)), (_sds((B, T, SSD_IN), BF), R.row(SSD_IN)), (_sds((B, T, SSD_IN), F32), R.row(SSD_IN)),
                   (_sds((B, T, SSD_IN), F32), R.row(SSD_IN)), (_sds((B, T, QW), BF), R.row(QW)), (_sds((B, T, QW), BF), R.row(QW)),
                   (_sds((B, H, HP, T), BF), pl.BlockSpec((None, H, HP, TM), lambda b, i: (b, 0, 0, i))),
                   (_sds((B, T, PD), F32), R.row(PD)),
                   (_sds((B, 2, 8, D), F32), R.mod()), (_sds((8, SSD_IN), F32), R.const((8, SSD_IN)))],
                  extra=8 << 20)


def _pool_bwd(R, dpo, u, wbd, scale):
    B, T = R.B, R.T

    def body(d_ref, dpv_ref, dnx_ref, u_ref, upv_ref, unx_ref, w_ref, sc_ref, du_ref, dw_ref, dsc_ref):
        b, i = pl.program_id(0), pl.program_id(1)
        ext_u = R.ext(i, upv_ref, u_ref[...], unx_ref)
        ext_d = R.ext(i, dpv_ref, d_ref[...], dnx_ref)
        dm, cnt, valid, lane = _pool_centred(R, i, ext_u)
        ddm = _dot_nt(ext_d * sc_ref[...], w_ref[...]) * valid
        e = ddm / cnt
        a2 = e + _shift_rows(e, 1)
        a4 = _shift_rows(a2, -1) + _shift_rows(a2, 1)
        a8 = _shift_rows(a4, -2) + _shift_rows(a4, 2)
        a16 = _shift_rows(a8, -4) + _shift_rows(a8, 4)
        du_ref[...] = (_lane_select(lane, a2, a4, a8, a16) - ddm)[8:8 + TM].astype(BF)
        dmc = dm[8:8 + TM]
        dpo_c = d_ref[...]
        first = jnp.logical_and(b == 0, i == 0)

        @pl.when(first)
        def _():
            dw_ref[...] = jnp.zeros(dw_ref.shape, F32)

        dw_ref[...] += _dot_tn(dmc, dpo_c * sc_ref[...])
        _acc_rows(dsc_ref, first, {0: jnp.sum(dpo_c * _dot(dmc, w_ref[...]), axis=0, keepdims=True)})

    return R.call("pool_bwd", body,
                  [(dpo, R.row(PD)), (dpo, R.prev8(PD)), (dpo, R.next8(PD)), (u, R.row(PD)), (u, R.prev8(PD)), (u, R.next8(PD)),
                   (wbd, R.const((PD, PD))), (scale, R.const((1, PD)))],
                  [(_sds((B, T, PD), BF), R.row(PD)), (_sds((PD, PD), F32), R.const((PD, PD))), (_sds((8, PD), F32), R.const((8, PD)))],
                  extra=8 << 20)


def _flash_bwd(B, T, qs, kt, dos, q_t, k_t, v_t, do_t):
    nk, nq = T // TK, T // TQ
    HS = 2

    def body(k_ref, kt_ref, vt_ref, q_ref, do_ref, qt_ref, dot_ref, dq_ref, dk_ref, dv_ref, s_scr, dp_scr, ds_scr):
        j = pl.program_id(2)
        ctx_keys = jnp.where(j < CTX // TK, 1.0, 0.0)
        for hh in range(HS):
            cols = slice(HP * hh, HP * (hh + 1))
            s_scr[hh] = _dot(q_ref[:, cols], kt_ref[hh])
            dp_scr[hh] = _dot(do_ref[:, cols], vt_ref[hh])
        for hh in range(HS):
            cols = slice(HP * hh, HP * (hh + 1))
            dk_tr = jnp.zeros((HP, TK), F32)
            dv_tr = jnp.zeros((HP, TK), F32)
            for i in range(nq):
                rows = slice(TQ * i, TQ * (i + 1))
                p = jnp.exp(s_scr[hh, rows, :])
                if i < CTX // TQ:
                    p = p * ctx_keys
                ds = (p * dp_scr[hh, rows, :]).astype(BF)
                ds_scr[hh, rows, :] = ds
                dv_tr = dv_tr + _dot(dot_ref[hh, :, rows], p)
                dk_tr = dk_tr + _dot(qt_ref[hh, :, rows], ds)
            dq = _dot(ds_scr[hh], k_ref[:, cols])

            @pl.when(j == 0)
            def _():
                dq_ref[:, cols] = dq

            @pl.when(j > 0)
            def _():
                dq_ref[:, cols] += dq

            dk_ref[:, cols] = dk_tr.T
            dv_ref[:, cols] = dv_tr.T

    tspec = pl.BlockSpec((None, TK, HS * HP), lambda b, h, j: (b, j, h))
    fspec = pl.BlockSpec((None, T, HS * HP), lambda b, h, j: (b, 0, h))
    ttspec = pl.BlockSpec((None, HS, HP, TK), lambda b, h, j: (b, h, 0, j))
    ftspec = pl.BlockSpec((None, HS, HP, T), lambda b, h, j: (b, h, 0, 0))
    bb = HS * (4 * _nbytes((T, HP), BF) + _nbytes((T, HP), F32) + 8 * _nbytes((TK, HP), F32))
    scr = HS * (2 * _nbytes((T, TK), F32) + _nbytes((T, TK), BF))
    return pl.pallas_call(
        functools.partial(body), name="flash_bwd", grid=(B, H // HS, nk),
        in_specs=[tspec, ttspec, ttspec, fspec, fspec, ftspec, ftspec],
        out_specs=[fspec, tspec, tspec],
        out_shape=[_out((B, T, QW), F32)] * 3,
        scratch_shapes=[pltpu.VMEM((HS, T, TK), F32), pltpu.VMEM((HS, T, TK), F32), pltpu.VMEM((HS, T, TK), BF)],
        compiler_params=_params(("arbitrary",) * 3, bb, scr + (8 << 20)),
    )(*_hbm([kt, k_t, v_t, qs, dos, q_t, do_t]))


def _mla_bwd(R, dqt, dkt, dvt, qa, kva, qnw, kvnw, wq, wk, wv, cos, sin):
    B, T = R.B, R.T
    scale = QK ** -0.5

    def body(dq_ref, dk_ref, dv_ref, qa_ref, kva_ref, qnw_ref, kvnw_ref, wq_ref, wk_ref, wv_ref, cos_ref, sin_ref,
             dqp_ref, dkvb_ref, dqa_ref, dkva_ref, dkr_ref, dnw_ref):
        b, i = pl.program_id(0), pl.program_id(1)
        cos1, sin1 = cos_ref[...], sin_ref[...]
        dq = dq_ref[...] * scale
        dqp = _per_head(lambda g: g * cos1 + _swap8(g * sin1), dq).astype(BF)
        dqp_ref[...] = dqp
        dkv = dk_ref[...]
        dkb = dkv.astype(BF)
        dvb = dv_ref[...].astype(BF)
        dkvb_ref[:, :QW] = dkb
        dkvb_ref[:, QW:] = dvb
        dkk = dkv[:, 0:HP]
        for h in range(1, H):
            dkk = dkk + dkv[:, HP * h:HP * (h + 1)]
        lane = _iota((TM, HP), 1)
        rope_lane = jnp.logical_and(lane >= NOPE, lane < NOPE + ROPE)
        dkr_ref[...] = jnp.where(rope_lane, dkk * cos1 + _swap8(dkk * sin1), 0.0).astype(BF)
        _, qh, qr = _rms(qa_ref[...], qnw_ref[...])
        _, kh, kr_ = _rms(kva_ref[...], kvnw_ref[...])
        dcq = _dot_nt(dqp, wq_ref[...])
        dckv = _dot_nt(dkb, wk_ref[...]) + _dot_nt(dvb, wv_ref[...])
        dqa, dqw = _rms_bwd(dcq, qh, qr, qnw_ref[...])
        dkva, dkw = _rms_bwd(dckv, kh, kr_, kvnw_ref[...])
        dqa_ref[...] = dqa.astype(BF)
        dkva_ref[...] = dkva.astype(BF)
        _acc_rows(dnw_ref, jnp.logical_and(b == 0, i == 0), {0: dqw, 1: dkw})

    tab = pl.BlockSpec((TM, HP), lambda b, i: (i, 0))
    return R.call("mla_bwd", body,
                  [(dqt, R.row(QW)), (dkt, R.row(QW)), (dvt, R.row(QW)), (qa, R.row(QL)), (kva, R.row(QL)),
                   (qnw, R.const((1, QL))), (kvnw, R.const((1, QL))), (wq, R.const((QL, QW))), (wk, R.const((QL, QW))),
                   (wv, R.const((QL, QW))), (cos, tab), (sin, tab)],
                  [(_sds((B, T, QW), BF), R.row(QW)), (_sds((B, T, 2 * QW), BF), R.row(2 * QW))]
                  + [(_sds((B, T, QL), BF), R.row(QL))] * 2
                  + [(_sds((B, T, HP), BF), R.row(HP)), (_sds((8, QL), F32), R.const((8, QL)))], extra=8 << 20)


def _ssd_scan_bwd(B, T, dyt, xbc, dt2, alog2, hin_f, hin_b):
    nc, nctc = T // CH, CTX // CH

    def chain(d, dy_ref, xbc_ref, dt_ref, alog_ref, hin_ref, dxbc_ref, ddt_ref, da_ref, dhs_all):
        dhs = dhs_all.at[d]
        xbc_v = xbc_ref[...]
        dyv = dy_ref[...]
        a_all, dt_all, adt_all, q_all, inc, inc_t = _scan_common(d, dt_ref[...], alog_ref)
        ones_p = jnp.ones((P, HP), F32)
        total = lambda m: jnp.sum(jnp.sum(m, axis=0, keepdims=True), axis=1, keepdims=True)
        dq_parts, dqtot_parts, ddtx_parts = [], [], []
        for g in range(2):
            bg = xbc_v[:, SSD_IN + NST * g:SSD_IN + NST * (g + 1)]
            cg = xbc_v[:, SSD_IN + 2 * NST + NST * g:SSD_IN + 2 * NST + NST * (g + 1)]
            bg_t, cg_t = bg.T, cg.T
            gm = _dot(cg, bg_t)
            gm_t = _dot(bg, cg_t)
            dgm = jnp.zeros((CH, CH), F32)
            dgm_t = jnp.zeros((CH, CH), F32)
            dbg = jnp.zeros((CH, NST), F32)
            dcg = jnp.zeros((CH, NST), F32)
            for r in range(3):
                h = 3 * g + r
                lm, lm_t, eq, etot, dte = _head_decay(h, adt_all, q_all, inc, inc_t)
                eq_p, etot_p, dte_p = eq[:, :P], etot[:, :P], dte[:, :P]
                dt_p = dt_all[:, HP * h:HP * h + P]
                xs_h = xbc_v[:, P * h:P * (h + 1)]
                xh = xs_h * dt_p
                sm, sm_t = gm * lm, gm_t * lm_t
                dy_h = dyv[:, P * h:P * (h + 1)]
                hprev = hin_ref[P * h:P * (h + 1), :].T
                dho = dhs[:, P * h:P * (h + 1)]
                ds = _dot_nt(dy_h, xh)
                ds_t = _dot_nt(xh, dy_h)
                dx = _dot(sm_t, dy_h)
                edy = eq_p * dy_h
                yo = _dot(cg, hprev)
                dcg = dcg + _dot_nt(edy, hprev)
                dhin = _dot(cg_t, edy) + etot_p * dho
                zs = _dot(bg, dho)
                dx = dx + dte_p * zs
                wm = dte_p * xh * zs
                dbg = dbg + _dot_nt(xh * dte_p, dho)
                dgm = dgm + ds * lm
                dgm_t = dgm_t + ds_t * lm_t
                rs = jnp.sum(ds * sm - ds_t * sm_t, axis=1, keepdims=True) + jnp.sum(edy * yo - wm, axis=1, keepdims=True)
                dq_parts.append(jnp.broadcast_to(rs, (CH, HP)))
                dqtot_parts.append(total(hprev * dho) * etot + total(wm))
                dxbc_ref[:, P * h:P * (h + 1)] = dx * dt_p
                ddtx_parts.append(_dot01_r(dx * xs_h, ones_p, passes=2))
                dhs[:, P * h:P * (h + 1)] = dhin
            dcg = dcg + _dot(dgm, bg)
            dbg = dbg + _dot(dgm_t, cg)
            dxbc_ref[:, SSD_IN + NST * g:SSD_IN + NST * (g + 1)] = dbg
            dxbc_ref[:, SSD_IN + 2 * NST + NST * g:SSD_IN + 2 * NST + NST * (g + 1)] = dcg
        cat = lambda parts: jnp.concatenate(parts, axis=1)
        dadt_all = _dot01(inc_t, cat(dq_parts)) + cat(dqtot_parts)
        ddt_all = cat(ddtx_parts) + dadt_all * jnp.concatenate([a_all] * (CH // 8), axis=0)
        da_all = jnp.sum(dadt_all * dt_all, axis=0, keepdims=True)
        lane, lane1 = _iota((CH, HP), 1), _iota((1, HP), 1)
        ddt = jnp.zeros((CH, HP), F32)
        da = jnp.zeros((1, HP), F32)
        for h in range(H):
            ddt = ddt + jnp.where(lane == h, ddt_all[:, HP * h:HP * (h + 1)], 0.0)
            da = da + jnp.where(lane1 == h, da_all[:, HP * h:HP * (h + 1)], 0.0)
        ddt_ref[...] = ddt
        da_ref[d] += jnp.where(_iota((8, HP), 0) == 0, da, 0.0)

    def body(dyf_ref, dyb_ref, xf_ref, xb_ref, dtf_ref, dtb_ref, alog_ref, hf_ref, hb_ref,
             dxf_ref, dxb_ref, ddtf_ref, ddtb_ref, da_ref, dhs):
        @pl.when(pl.program_id(1) == 0)
        def _():
            dhs[...] = jnp.zeros(dhs.shape, F32)
            da_ref[...] = jnp.zeros(da_ref.shape, F32)

        chain(0, dyf_ref, xf_ref, dtf_ref, alog_ref, hf_ref, dxf_ref, ddtf_ref, da_ref, dhs)
        chain(1, dyb_ref, xb_ref, dtb_ref, alog_ref, hb_ref, dxb_ref, ddtb_ref, da_ref, dhs)

    cidx = lambda d: (lambda s: _chunk_index(d, nc - 1 - s, nc, nctc))
    specs = lambda d: dict(
        dy=pl.BlockSpec((None, CH, SSD_IN), lambda b, s: (b, cidx(d)(s), 0)),
        xbc=pl.BlockSpec((None, CH, XBC), lambda b, s: (b, cidx(d)(s), 0)),
        dt=pl.BlockSpec((None, None, CH, HP), lambda b, s: (b, d, cidx(d)(s), 0)),
        h=pl.BlockSpec((None, None, SSD_IN, NST), lambda b, s: (b, cidx(d)(s), 0, 0)),
        ddt=pl.BlockSpec((None, CH, HP), lambda b, s: (b, cidx(d)(s), 0)))
    f, r = specs(0), specs(1)
    bb = 2 * (2 * _nbytes((CH, XBC), F32) + 2 * _nbytes((CH, HP), F32) + _nbytes((CH, SSD_IN), F32) + _nbytes((SSD_IN, NST), F32))
    return pl.pallas_call(
        functools.partial(body), name="ssd_scan_bwd", grid=(B, nc),
        in_specs=[f["dy"], r["dy"], f["xbc"], r["xbc"], f["dt"], r["dt"], pl.BlockSpec((2, 8, HP), lambda b, s: (0, 0, 0)),
                  f["h"], r["h"]],
        out_specs=[f["xbc"], r["xbc"], f["ddt"], r["ddt"], pl.BlockSpec((None, 2, 8, HP), lambda b, s: (b, 0, 0, 0))],
        out_shape=[_out((B, T, XBC), F32)] * 2 + [_out((B, T, HP), F32)] * 2 + [_out((B, 2, 8, HP), F32)],
        scratch_shapes=[pltpu.VMEM((2, NST, SSD_IN), F32)],
        compiler_params=_params(("arbitrary",) * 2, bb, 16 << 20),
    )(*_hbm([dyt, dyt, xbc, xbc, dt2, dt2, alog2, hin_f, hin_b]))


def _ssd_prep_bwd(R, dxbc_f, dxbc_b, dxsk, ddt_f, ddt_b, xbc_raw, dt_raw, conv_w8, conv_b, dtb):
    B, T = R.B, R.T

    def body(dxf_ref, dxb_ref, dsk_ref, ddtf_ref, ddtb2_ref, raw_ref, pv_ref, nx_ref, dtr_ref, w_ref, b_ref, dtb_ref,
             dpre_ref, ddtr_ref, dcw_ref, dvec_ref, ddtb_ref):
        b, i = pl.program_id(0), pl.program_id(1)
        ext = R.ext(i, pv_ref, raw_ref[...], nx_ref)
        pre = _conv_pre(ext, w_ref, b_ref)
        dxbc = dxf_ref[...] + dxb_ref[...]
        skip = jnp.concatenate([dsk_ref[...], jnp.zeros((TM, XBC - SSD_IN), F32)], axis=1)
        dpre = (dxbc + skip) * _dsilu(pre)
        dpre_ref[...] = dpre
        first = jnp.logical_and(b == 0, i == 0)
        taps = {k: jnp.sum(dpre * _shift_rows(ext, k - 1)[8:8 + TM], axis=0, keepdims=True) for k in range(4)}
        _acc_rows(dcw_ref, first, taps)
        _acc_rows(dvec_ref, first, {0: jnp.sum(dpre, axis=0, keepdims=True)})
        ddt = ddtf_ref[...] + pltpu.roll(ddtb2_ref[...], H, axis=1)
        ddtr = ddt * _sigmoid(dtr_ref[...] + dtb_ref[...])
        ddtr = jnp.where(_iota((TM, HP), 1) < 2 * H, ddtr, 0.0)
        ddtr_ref[...] = ddtr.astype(BF)
        _acc_rows(ddtb_ref, first, {0: jnp.sum(ddtr, axis=0, keepdims=True)})

    return R.call("ssd_prep_bwd", body,
                  [(dxbc_f, R.row(XBC)), (dxbc_b, R.row(XBC)), (dxsk, R.row(SSD_IN)), (ddt_f, R.row(HP)), (ddt_b, R.row(HP)),
                   (xbc_raw, R.row(XBC)), (xbc_raw, R.prev8(XBC)),
                   (xbc_raw, R.next8(XBC)), (dt_raw, R.row(HP)), (conv_w8, R.const((8, XBC))), (conv_b, R.const((1, XBC))),
                   (dtb, R.const((1, HP)))],
                  [(_sds((B, T, XBC), F32), R.row(XBC)), (_sds((B, T, HP), BF), R.row(HP)), (_sds((8, XBC), F32), R.const((8, XBC))),
                   (_sds((8, XBC), F32), R.const((8, XBC))), (_sds((8, HP), F32), R.const((8, HP)))], extra=16 << 20)


def _in_bwd(R, dxm, x, mod, nw1, dz, dpre, dqa, dkva, dpool, dkr, ddtr, conv_w8, w_arr, latent_only=False):
    B, T = R.B, R.T
    dx_out = (_sds((B, T - CTX, D), F32), R.tgt(D)) if latent_only else (_sds((B, T, D), F32), R.row(D))

    xin = R.xin(x)

    def body(*refs):
        x_refs = refs[:len(xin)]
        (dxm_ref, mod_ref, nw_ref, dz_ref, dp_ref, dpp_ref, dpn_ref, dqa_ref, dkva_ref, dpool_ref, dkr_ref, ddt_ref,
         cw_ref, w_ref, dx_ref, dproj_ref, dmod_ref, dnw_ref) = refs[len(xin):]
        b, i = pl.program_id(0), pl.program_id(1)
        ext = R.ext(i, dpp_ref, dp_ref[...], dpn_ref)
        draw = (_row(cw_ref, 0) * _shift_rows(ext, 1)[8:8 + TM] + _row(cw_ref, 1) * ext[8:8 + TM]
                + _row(cw_ref, 2) * _shift_rows(ext, -1)[8:8 + TM] + _row(cw_ref, 3) * _shift_rows(ext, -2)[8:8 + TM])
        dproj_ref[:, A_Z:A_XBC] = dz_ref[...]
        dproj_ref[:, A_XBC:A_QA] = draw.astype(BF)
        dproj_ref[:, A_QA:A_KVA] = dqa_ref[...]
        dproj_ref[:, A_KVA:A_POOL] = dkva_ref[...]
        dproj_ref[:, A_POOL:A_KR] = dpool_ref[...]
        dproj_ref[:, A_KR:A_DT] = dkr_ref[...]
        dproj_ref[:, A_DT:PC] = ddt_ref[...]
        dh = _dot_nt(dproj_ref[...], w_ref[...])
        _, xn, r = _norm_mod(R.xval(i, x_refs), nw_ref[...], _row(mod_ref, 0), _row(mod_ref, 1))
        dx, dsh, dsc, dnw = _norm_mod_bwd(dh, xn, r, nw_ref[...], _row(mod_ref, 1))
        dx_ref[...] = dxm_ref[...] + dx
        _acc_rows(dmod_ref, R.first_of_stream(i), {0: dsh, 1: dsc})
        _acc_rows(dnw_ref, jnp.logical_and(b == 0, i == 0), {0: dnw})

    return R.call("in_bwd", body,
                  xin + [(dxm, R.row(D)), (mod, R.mod()), (nw1, R.const((1, D))), (dz, R.row(SSD_IN)), (dpre, R.row(XBC)),
                   (dpre, R.prev8(XBC)), (dpre, R.next8(XBC)), (dqa, R.row(QL)), (dkva, R.row(QL)), (dpool, R.row(PD)), (dkr, R.row(HP)),
                   (ddtr, R.row(HP)), (conv_w8, R.const((8, XBC))), (w_arr, R.const((D, PC)))],
                  [dx_out, (_sds((B, T, PC), BF), R.row(PC)), (_sds((B, 2, 8, D), F32), R.mod()),
                   (_sds((8, D), F32), R.const((8, D)))], extra=12 << 20)


def _adaln_fwd(cs, mod_w):
    L, _, C = mod_w.shape

    def body(c_ref, w_ref, o_ref):
        s = _silu(c_ref[...]).astype(BF)
        for l in range(L):
            o_ref[l] = jnp.dot(s, w_ref[l].astype(BF), preferred_element_type=F32)

    return pl.pallas_call(functools.partial(body), name="adaln_fwd", out_shape=_sds((L, 24, C), F32),
                          compiler_params=_params(None, _nbytes(mod_w.shape, F32) + _nbytes((L, 24, C), F32), 8 << 20))(cs, mod_w)


def _adaln_bwd(cs, dm, mod_w):
    L, _, C = mod_w.shape

    def body(c_ref, dm_ref, w_ref, gw_ref, gc_ref):
        c = c_ref[...]
        s = _silu(c).astype(BF)
        acc = jnp.zeros((24, D), F32)
        for l in range(L):
            dmb = dm_ref[l].astype(BF)
            gw_ref[l] = _dot_tn(s, dmb)
            acc = acc + _dot_nt(dmb, w_ref[l])
        gc_ref[...] = acc * _dsilu(c)

    return pl.pallas_call(functools.partial(body), name="adaln_bwd", out_shape=[_sds((L, D, C), F32), _sds((24, D), F32)],
                          compiler_params=_params(None, 2 * _nbytes(mod_w.shape, F32), 8 << 20))(cs, dm, mod_w)


def _sum_blocks(name, parts):
    Pn, Rr, C = parts.shape

    def body(p_ref, o_ref):
        acc = p_ref[0]
        for k in range(1, Pn):
            acc = acc + p_ref[k]
        o_ref[...] = acc

    return pl.pallas_call(functools.partial(body), name=name, out_shape=_sds((Rr, C), F32),
                          compiler_params=_params(None, _nbytes(parts.shape, F32), 4 << 20))(parts)


def _adamw(name, parts, w, m, v, rt):
    nseg = len(parts)
    Pn, rseg, C = parts[0].shape
    Rr = nseg * rseg
    tiles = rseg // rt
    assert rseg % rt == 0 and w.shape == (Rr, C)
    c1 = 1.0 - ADAM_B1 ** ADAM_STEP
    c2 = 1.0 - ADAM_B2 ** ADAM_STEP

    def body(*refs):
        p_refs = refs[:nseg]
        w_ref, m_ref, v_ref, g_ref, d_ref, nm_ref, nv_ref = refs[nseg:]
        i = pl.program_id(0)
        g = None
        for s, p_ref in enumerate(p_refs):
            gs = p_ref[0].astype(F32)
            for k in range(1, Pn):
                gs = gs + p_ref[k].astype(F32)
            g = gs if g is None else jnp.where(i >= s * tiles, gs, g)
        mn = ADAM_B1 * m_ref[...] + (1.0 - ADAM_B1) * g
        vn = ADAM_B2 * v_ref[...] + (1.0 - ADAM_B2) * jnp.square(g)
        g_ref[...] = g
        nm_ref[...] = mn
        nv_ref[...] = vn
        d_ref[...] = -ADAM_LR * ((mn / c1) / (jnp.sqrt(vn / c2) + ADAM_EPS) + ADAM_WD * w_ref[...])

    spec = pl.BlockSpec((rt, C), lambda i: (i, 0))
    pspec = lambda s: pl.BlockSpec((Pn, rt, C), lambda i: (0, jnp.clip(i - s * tiles, 0, tiles - 1), 0))
    bb = nseg * Pn * _nbytes((rt, C), parts[0].dtype) + 7 * _nbytes((rt, C), F32)
    return pl.pallas_call(
        functools.partial(body), name=name, grid=(Rr // rt,),
        in_specs=[pspec(s) for s in range(nseg)] + [spec, spec, spec],
        out_specs=[spec] * 4, out_shape=[_out((Rr, C), F32)] * 4,
        compiler_params=_params(("arbitrary",), bb, 4 << 20),
    )(*_hbm(list(parts) + [w, m, v]))


MESH = pl.DeviceIdType.MESH


def _my_pos():
    return lax.axis_index("x"), lax.axis_index("y"), lax.axis_index("c")


def _dev_index(x, y, c):
    return 4 * x + 2 * y + c


def _all_gather(name, shards):
    n = len(shards)

    def body(*refs):
        ins, outs = refs[:n], refs[n:2 * n]
        send_sems, recv_sems, local_sem = refs[2 * n:]
        x, y, c = _my_pos()
        me, sibling = (x, y, c), (x, y, 1 - c)
        chips = [(1 - x, y), (x, 1 - y), (1 - x, 1 - y)]

        def copy(t, k, block, to, src=None):
            slot = outs[t].at[_dev_index(*block)]
            return pltpu.make_async_remote_copy(
                src_ref=slot if src is None else src, dst_ref=slot,
                send_sem=send_sems.at[t, k], recv_sem=recv_sems.at[t, k], device_id=to, device_id_type=MESH)

        mine = [pltpu.make_async_copy(ins[t], outs[t].at[_dev_index(*me)], local_sem.at[t]) for t in range(n)]
        for cp in mine:
            cp.start()
        first = []
        for t in range(n):
            first.append(copy(t, 0, me, sibling, src=ins[t]))
            first += [copy(t, 1 + j, me, (*chip, c), src=ins[t]) for j, chip in enumerate(chips)]
        for cp in first:
            cp.start()
        passed = []
        for j, chip in enumerate(chips):
            for t in range(n):
                copy(t, 1 + j, (*chip, c), me).wait_recv()
                cp = copy(t, 4 + j, (*chip, c), sibling)
                cp.start()
                passed.append(cp)
        for t in range(n):
            copy(t, 0, sibling, me).wait_recv()
            for j, chip in enumerate(chips):
                copy(t, 4 + j, (*chip, 1 - c), me).wait_recv()
        for cp in first + passed:
            cp.wait_send()
        for cp in mine:
            cp.wait()

    return pl.pallas_call(
        functools.partial(body), name=name,
        in_specs=[ANY] * n, out_specs=[ANY] * n,
        out_shape=[_sds((NDEV,) + s.shape, s.dtype) for s in shards],
        scratch_shapes=[pltpu.SemaphoreType.DMA((n, 7)), pltpu.SemaphoreType.DMA((n, 7)), pltpu.SemaphoreType.DMA((n,))],
    )(*shards)


def _all_to_all(name, parts):
    n = len(parts)

    def body(*refs):
        ins, outs = refs[:n], refs[n:2 * n]
        send_sems, recv_sems, local_sem = refs[2 * n:]
        x, y, c = _my_pos()
        me = _dev_index(x, y, c)
        peers = [(x ^ ((k >> 2) & 1), y ^ ((k >> 1) & 1), c ^ (k & 1)) for k in range(1, NDEV)]
        mine = [pltpu.make_async_copy(ins[t].at[me], outs[t].at[me], local_sem.at[t]) for t in range(n)]
        for cp in mine:
            cp.start()
        sends = []
        for t in range(n):
            for k, peer in enumerate(peers):
                cp = pltpu.make_async_remote_copy(
                    src_ref=ins[t].at[_dev_index(*peer)], dst_ref=outs[t].at[me],
                    send_sem=send_sems.at[t, k], recv_sem=recv_sems.at[t, k], device_id=peer, device_id_type=MESH)
                cp.start()
                sends.append(cp)
        for t in range(n):
            for k, peer in enumerate(peers):
                slot = outs[t].at[_dev_index(*peer)]
                pltpu.make_async_remote_copy(
                    src_ref=slot, dst_ref=slot, send_sem=send_sems.at[t, k], recv_sem=recv_sems.at[t, k],
                    device_id=peer, device_id_type=MESH).wait_recv()
        for cp in sends:
            cp.wait_send()
        for cp in mine:
            cp.wait()

    return pl.pallas_call(
        functools.partial(body), name=name,
        in_specs=[ANY] * n, out_specs=[ANY] * n,
        out_shape=[_sds(p.shape, p.dtype) for p in parts],
        scratch_shapes=[pltpu.SemaphoreType.DMA((n, 7)), pltpu.SemaphoreType.DMA((n, 7)), pltpu.SemaphoreType.DMA((n,))],
    )(*parts)


SEM = pl.BlockSpec(memory_space=pltpu.SEMAPHORE)
IN_HBM = pl.BlockSpec(memory_space=pltpu.HBM)
DATAFLOW = pltpu.SideEffectType.DATAFLOW_SIDE_EFFECTING


def _flip_peers(x, y, c):
    return [(x ^ ((k >> 2) & 1), y ^ ((k >> 1) & 1), c ^ (k & 1)) for k in range(1, NDEV)]


def _split_copies(srcs, lands, send_sems, recv_sems, gather):
    x, y, c = _my_pos()
    me = _dev_index(x, y, c)
    out = []
    for t in range(len(srcs)):
        for k, peer in enumerate(_flip_peers(x, y, c)):
            p = _dev_index(*peer)
            src = srcs[t] if gather else srcs[t].at[p]
            sems = dict(send_sem=send_sems.at[7 * t + k], recv_sem=recv_sems.at[7 * t + k], device_id=peer, device_id_type=MESH)
            out.append((pltpu.make_async_remote_copy(src_ref=src, dst_ref=lands[t].at[me], **sems),
                        pltpu.make_async_remote_copy(src_ref=src, dst_ref=lands[t].at[p], **sems)))
    return out


def _exchange_start(name, collective_id, srcs, gather):
    n = len(srcs)
    lands = [lax.empty(((NDEV,) + s.shape) if gather else s.shape, s.dtype) for s in srcs]

    def body(*refs):
        src_refs, land_refs = refs[:n], refs[n:2 * n]
        send_sems, recv_sems = refs[2 * n], refs[2 * n + 1]
        token = refs[-1]
        barrier = pltpu.get_barrier_semaphore()
        for peer in _flip_peers(*_my_pos()):
            pl.semaphore_signal(barrier, inc=1, device_id=peer, device_id_type=MESH)
        pl.semaphore_wait(barrier, NDEV - 1)
        for send, _ in _split_copies(src_refs, land_refs, send_sems, recv_sems, gather):
            send.start()
        token[...] = jnp.zeros(token.shape, token.dtype)

    hbm = lambda a: pltpu.HBM(a.shape, a.dtype)
    res = pl.pallas_call(
        functools.partial(body), name=name,
        out_shape=[pltpu.SemaphoreType.DMA((7 * n,)), pltpu.SemaphoreType.DMA((7 * n,))] + [hbm(s) for s in srcs]
        + [hbm(a) for a in lands] + [_sds((8, HP), F32)],
        in_specs=[IN_HBM] * (2 * n), out_specs=[SEM, SEM] + [IN_HBM] * (2 * n) + [pl.BlockSpec(memory_space=pltpu.VMEM)],
        input_output_aliases={i: 2 + i for i in range(2 * n)},
        compiler_params=pltpu.CompilerParams(has_side_effects=DATAFLOW, collective_id=collective_id),
    )(*_hbm(list(srcs) + lands))
    return res[0], res[1], list(res[2:2 + n]), list(res[2 + n:2 + 2 * n]), res[-1]


def _exchange_wait(name, send_sems, recv_sems, srcs, lands, after, gather):
    n = len(srcs)

    def body(*refs):
        src_refs, land_refs = refs[:n], refs[n:2 * n]
        for _, recv in _split_copies(src_refs, land_refs, refs[2 * n], refs[2 * n + 1], gather):
            recv.wait_send()
            recv.wait_recv()

    hbm = lambda a: pltpu.HBM(a.shape, a.dtype)
    res = pl.pallas_call(
        functools.partial(body), name=name,
        out_shape=[hbm(s) for s in srcs] + [hbm(a) for a in lands],
        in_specs=[IN_HBM] * (2 * n) + [SEM, SEM, ANY], out_specs=[IN_HBM] * (2 * n),
        input_output_aliases={i: i for i in range(2 * n)},
        compiler_params=pltpu.CompilerParams(has_side_effects=DATAFLOW),
    )(*srcs, *lands, send_sems, recv_sems, after)
    return list(res[:n]), list(res[n:])


def _with_own(lands, own_blocks, me):
    out = []
    for land, own in zip(lands, own_blocks):
        out.append(lax.dynamic_update_slice(land, own[None], (me,) + (0,) * own.ndim))
    return out


def _arrange_w_in(w):
    z = lambda n: jnp.zeros((w.shape[0], n), w.dtype)
    return jnp.concatenate([w[:, 0:1280], w[:, 1292:1548], w[:, 1548:1804], w[:, 1836:2092],
                            z(64), w[:, 1804:1836], z(32), w[:, 1280:1292], z(HP - 2 * H)], axis=1)


def _unarrange_w_in(g):
    return jnp.concatenate([g[:, 0:1280], g[:, A_DT:A_DT + 2 * H], g[:, A_QA:A_KVA], g[:, A_KVA:A_POOL],
                            g[:, A_KR + NOPE:A_KR + NOPE + ROPE], g[:, A_POOL:A_KR]], axis=1)


def _pad_heads(w, width):
    k = w.shape[0]
    return jnp.pad(w.reshape(k, H, width), ((0, 0), (0, 0), (0, HP - width))).reshape(k, H * HP)


def _unpad_heads(g, width):
    k = g.shape[0]
    return g.reshape(k, H, HP)[:, :, :width].reshape(k, H * width)


def _arrange_w_out(w):
    att = jnp.pad(w[SSD_IN:2 * SSD_IN].reshape(H, VH, D), ((0, 0), (0, HP - VH), (0, 0))).reshape(QW, D)
    return jnp.concatenate([w[0:SSD_IN], att, w[2 * SSD_IN:]], axis=0)


def _unarrange_w_out(g):
    att = g[SSD_IN:SSD_IN + QW].reshape(H, HP, D)[:, :VH].reshape(SSD_IN, D)
    return jnp.concatenate([g[0:SSD_IN], att, g[SSD_IN + QW:]], axis=0)


def _rope_tables(T):
    n = T - CTX
    rows = n // GRID_W
    pairs = ROPE // 4
    inv = ROPE_THETA ** (-jnp.arange(pairs, dtype=F32) / pairs)
    ar = jnp.arange(rows, dtype=F32)[:, None] * inv
    ac = jnp.arange(GRID_W, dtype=F32)[:, None] * inv
    by_row = lambda a: jnp.repeat(a, GRID_W, axis=0)
    by_col = lambda a: jnp.tile(a, (rows, 1))
    cos = jnp.concatenate([by_row(jnp.cos(ar))] * 2 + [by_col(jnp.cos(ac))] * 2, axis=1)
    sin = jnp.concatenate([-by_row(jnp.sin(ar)), by_row(jnp.sin(ar)), -by_col(jnp.sin(ac)), by_col(jnp.sin(ac))], axis=1)
    ones, zeros = jnp.ones((n, NOPE), F32), jnp.zeros((n, NOPE), F32)
    cos = jnp.concatenate([ones, cos, ones[:, :HP - QK]], axis=1)
    sin = jnp.concatenate([zeros, sin, zeros[:, :HP - QK]], axis=1)
    return (jnp.concatenate([jnp.ones((CTX, HP), F32), cos], axis=0),
            jnp.concatenate([jnp.zeros((CTX, HP), F32), sin], axis=0))


def _lane_pad(v, n):
    return jnp.pad(v, (0, n - v.shape[0]))[None, :]


def _layer_weights(w_in, w_q_b, w_kv_b, conv_w, pool_w):
    kv = w_kv_b.reshape(QL, H, NOPE + VH)
    wbd = jnp.concatenate([jnp.pad(pool_w[g], ((0, 0), (64 * g, PD - 64 * (g + 1)))) for g in range(4)], axis=0)
    return dict(
        w_in=_arrange_w_in(w_in).astype(BF),
        wq=_pad_heads(w_q_b, QK).astype(BF),
        wk=_pad_heads(kv[:, :, :NOPE].reshape(QL, H * NOPE), NOPE).astype(BF),
        wv=_pad_heads(kv[:, :, NOPE:].reshape(QL, H * VH), VH).astype(BF),
        conv_w8=jnp.pad(conv_w, ((0, 4), (0, 0))), wbd=wbd)


def _layer_fwd(R, x, mod, lw, sp, late, cos, sin):
    B, T = R.B, R.T
    z, xbc_raw, qa, kva, pool_in, kr, dt_raw, h1 = _in_proj(R, x, mod, sp["nw1"], lw["w_in"])
    xbc, dt2 = _ssd_prep(R, xbc_raw, dt_raw, lw["conv_w8"], sp["conv_b"], sp["dtb"])
    yf, yb, hin_f, hin_b = _ssd_scan(B, T, xbc, dt2, sp["alog2"])
    qt, kt, q_t, k_t, v_t, cq, ckv = _mla_prep(R, qa, kva, kr, sp["qnw"], sp["kvnw"], lw["wq"], lw["wk"], lw["wv"], cos, sin)
    o, lse = _flash_fwd(B, T, q_t, kt, v_t)
    w_out, w1, w2 = late(o)
    w_out = _arrange_w_out(w_out).astype(BF)
    xmid, cat, mixbf = _out_proj(R, x, mod, yf, yb, xbc, z, o, pool_in, lw["wbd"], sp["pscale"], sp["dsk"], sp["snw"], w_out)
    xo, h2, ubf, ybf = _mlp_fwd(R, xmid, mod, sp["nw2"], w1, w2)
    saved = dict(x=x, z=z, xbc_raw=xbc_raw, qa=qa, kva=kva, pool_in=pool_in, dt_raw=dt_raw, h1=h1, xbc=xbc, dt2=dt2,
                 yf=yf, yb=yb, hin_f=hin_f, hin_b=hin_b, qt=qt, kt=kt, q_t=q_t, k_t=k_t, v_t=v_t, cq=cq, ckv=ckv, o=o, lse=lse,
                 cat=cat, mixbf=mixbf, xmid=xmid, h2=h2, ubf=ubf, ybf=ybf, w_out=w_out, w1=w1, w2=w2)
    return xo, saved


def _layer_bwd(R, dxo, sv, mod, lw, sp, cos, sin, on_mlp=None, on_mid=None, latent_only=False):
    B, T = R.B, R.T
    dxm, du, abf, dyb, dmod_a, dnw2 = _mlp_bwd(R, dxo, sv["xmid"], sv["ubf"], sv["ybf"], mod, sp["nw2"], sv["w1"], sv["w2"])
    g_w1 = _tn_matmul("dw_mlp1", sv["h2"], du, 4 * FSH, sub=4)
    g_w2 = _tn_matmul("dw_mlp2", abf, dyb, D)[0].reshape(NDEV, FSH, D)
    snw = sp["snw"]
    tok = on_mlp(g_w1, g_w2) if on_mlp is not None else None
    if tok is not None:
        snw = snw + tok
    dmix, dz, dyt, dxsk, qs, dos, do_t, dpo, dmod_b, dvec_o = _out_bwd(R, dxm, mod, sv["mixbf"], sv["yf"], sv["yb"], sv["xbc"], sv["z"],
                                                                 sv["o"], sv["qt"], sv["lse"], sp["dsk"], snw, sv["w_out"])
    g_wout = _unarrange_w_out(_tn_matmul("dw_out", sv["cat"], dmix, D)[0])
    dpool_in, g_wbd, dpsc = _pool_bwd(R, dpo, sv["pool_in"], lw["wbd"], sp["pscale"])
    dqt, dkt, dvt = _flash_bwd(B, T, qs, sv["kt"], dos, sv["q_t"], sv["k_t"], sv["v_t"], do_t)
    dqp, dkvb, dqa, dkva, dkr, dnw_qk = _mla_bwd(R, dqt, dkt, dvt, sv["qa"], sv["kva"], sp["qnw"], sp["kvnw"],
                                                 lw["wq"], lw["wk"], lw["wv"], cos, sin)
    g_wq = _unpad_heads(_tn_matmul("dw_q", sv["cq"], dqp, QW)[0], QK)
    g_kv = _tn_matmul("dw_kv", sv["ckv"], dkvb, 2 * QW)[0]
    g_wk = _unpad_heads(g_kv[:, :QW], NOPE).reshape(QL, H, NOPE)
    g_wv = _unpad_heads(g_kv[:, QW:], VH).reshape(QL, H, VH)
    g_wkv = jnp.concatenate([g_wk, g_wv], axis=2).reshape(QL, H * (NOPE + VH))
    alog2 = sp["alog2"]
    tok = on_mid(g_wout, g_wq, g_wkv) if on_mid is not None else None
    if tok is not None:
        alog2 = alog2 + tok
    dxbc_f, dxbc_b, ddt_f, ddt_b, da = _ssd_scan_bwd(B, T, dyt, sv["xbc"], sv["dt2"], alog2, sv["hin_f"], sv["hin_b"])
    dpre, ddtr, dcw, dcb, ddtb = _ssd_prep_bwd(R, dxbc_f, dxbc_b, dxsk, ddt_f, ddt_b, sv["xbc_raw"], sv["dt_raw"], lw["conv_w8"],
                                                sp["conv_b"], sp["dtb"])
    dx, dproj, dmod_c, dnw1 = _in_bwd(R, dxm, sv["x"], mod, sp["nw1"], dz, dpre, dqa, dkva, dpool_in, dkr, ddtr,
                                      lw["conv_w8"], lw["w_in"], latent_only)
    g_win = _unarrange_w_in(jnp.concatenate(list(_tn_matmul("dw_in", sv["h1"], dproj, PC)), axis=1))
    a2 = -jnp.exp(sp["alog2"][:, 0, :H])
    small = dict(
        norm1_w=dnw1[0], norm2_w=dnw2[0], conv_w=dcw[0:4], conv_b=dcb[0], dt_bias=ddtb[0, :2 * H].reshape(2, H),
        a_log=jnp.sum(da[:, :, 0, :H], axis=0) * a2, ssd_d=jnp.sum(dvec_o[1].reshape(H, P), axis=1), ssd_norm_w=dvec_o[0],
        q_a_norm_w=dnw_qk[0], kv_a_norm_w=dnw_qk[1],
        pool_w=jnp.stack([g_wbd[64 * g:64 * (g + 1), 64 * g:64 * (g + 1)] for g in range(4)]), pool_scale=dpsc[0])
    big = dict(w_in=g_win, w_out=g_wout, w_q_b=g_wq, w_kv_b=g_wkv, w_mlp1=g_w1, w_mlp2=g_w2)
    return dx, big, small, dmod_a + dmod_b + dmod_c


def _small_params(l, norm1_w, norm2_w, conv_b, dt_bias, a_log, ssd_d, ssd_norm_w, q_a_norm_w, kv_a_norm_w, pool_scale):
    alog2 = jnp.broadcast_to(jnp.pad(a_log[l], ((0, 0), (0, HP - H)))[:, None, :], (2, 8, HP))
    return dict(nw1=norm1_w[l][None], nw2=norm2_w[l][None], conv_b=conv_b[l][None],
                dtb=_lane_pad(dt_bias[l].reshape(2 * H), HP), alog2=alog2,
                dsk=jnp.repeat(ssd_d[l], P)[None], snw=ssd_norm_w[l][None], qnw=q_a_norm_w[l][None],
                kvnw=kv_a_norm_w[l][None], pscale=pool_scale[l][None])


SMALL_NAMES = ["mod_b", "norm1_w", "norm2_w", "conv_b", "dt_bias", "a_log", "ssd_d", "ssd_norm_w", "q_a_norm_w",
               "kv_a_norm_w", "pool_w", "pool_scale", "final_norm_w"]


def _pack(arrs):
    rows = []
    for a in arrs:
        f = a.reshape(-1).astype(F32)
        n = -(-f.shape[0] // HP) * HP
        rows.append(jnp.pad(f, (0, n - f.shape[0])).reshape(-1, HP))
    out = jnp.concatenate(rows, axis=0)
    pad = (-out.shape[0]) % 8
    return jnp.pad(out, ((0, pad), (0, 0)))


def _unpack(pack, like):
    outs, r = [], 0
    for a in like:
        n = math.prod(a.shape)
        nr = -(-n // HP)
        outs.append(pack[r:r + nr].reshape(-1)[:n].reshape(a.shape))
        r += nr
    return outs


def _local_step(x, ctx, target, mods, full_of, small_w, on_grads=None, on_mlp=None, on_mid=None):
    B, N = x.shape[0], x.shape[1]
    T = CTX + N
    R = _Rows(B, T)
    cos, sin = _rope_tables(T)
    xu = (ctx, x)
    L = len(mods)
    lws, sps, saves = [], [], []
    for l in range(L):
        f = full_of(l, xu)
        lws.append(_layer_weights(f["w_in"], f["w_q_b"], f["w_kv_b"], f["conv_w"], small_w["pool_w"][l]))
        sps.append(_small_params(l, *[small_w[k] for k in ["norm1_w", "norm2_w", "conv_b", "dt_bias", "a_log", "ssd_d",
                                                          "ssd_norm_w", "q_a_norm_w", "kv_a_norm_w", "pool_scale"]]))
        xu, sv = _layer_fwd(R, xu, mods[l], lws[l], sps[l], f["late"], cos, sin)
        saves.append(sv)
    dx, loss8, dfnw = _loss_head(R, xu, target, small_w["final_norm_w"][None])
    bigs, smalls, dmods = [None] * L, [None] * L, [None] * L
    for l in reversed(range(L)):
        hook = functools.partial(on_mlp, l) if on_mlp is not None else None
        hook2 = functools.partial(on_mid, l) if on_mid is not None else None
        dx, bigs[l], smalls[l], dmods[l] = _layer_bwd(R, dx, saves[l], mods[l], lws[l], sps[l], cos, sin, hook, hook2, l == 0)
        if on_grads is not None:
            bigs[l], tok = on_grads(l, bigs[l], dx)
            if tok is not None:
                sps[l - 1] = dict(sps[l - 1], nw2=sps[l - 1]["nw2"] + tok)
    return loss8[0, 0], dx, bigs, smalls, dfnw[0], dmods


def kernel(x, c, ctx, c_ctx, mod_w, mod_b, norm1_w, norm2_w, w_in, conv_w, conv_b, dt_bias, a_log, ssd_d, ssd_norm_w, q_a_norm_w, w_q_b, kv_a_norm_w, w_kv_b, pool_w, pool_scale, w_out, w_mlp1, w_mlp2, final_norm_w, loss_target, m_c_ctx, m_mod_w, m_mod_b, m_norm1_w, m_norm2_w, m_w_in, m_conv_w, m_conv_b, m_dt_bias, m_a_log, m_ssd_d, m_ssd_norm_w, m_q_a_norm_w, m_w_q_b, m_kv_a_norm_w, m_w_kv_b, m_pool_w, m_pool_scale, m_w_out, m_w_mlp1, m_w_mlp2, m_final_norm_w, v_c_ctx, v_mod_w, v_mod_b, v_norm1_w, v_norm2_w, v_w_in, v_conv_w, v_conv_b, v_dt_bias, v_a_log, v_ssd_d, v_ssd_norm_w, v_q_a_norm_w, v_w_q_b, v_kv_a_norm_w, v_w_kv_b, v_pool_w, v_pool_scale, v_w_out, v_w_mlp1, v_w_mlp2, v_final_norm_w):
    args = dict(locals())
    B = x.shape[0]
    L = mod_w.shape[0]
    me = _dev_index(*_my_pos())
    CS = mod_w.shape[2]

    big_names = ["w_in", "w_out", "w_q_b", "w_kv_b", "w_mlp1", "w_mlp2"]
    shards = {n: args[n].astype(BF) for n in big_names}
    early, late_names = ["w_in", "w_q_b", "w_kv_b"], ["w_out", "w_mlp1", "w_mlp2"]
    g0 = _all_gather("gather_weights", [c, conv_w] + [shards[n][0] for n in early])
    c_all, convw_all = g0[0], g0[1]
    gathered = {0: dict(zip(early, g0[2:]))}
    cs = jnp.concatenate([c_all.reshape(NDEV * B, D), c_ctx[None], jnp.zeros((24 - NDEV * B - 1, D), F32)], axis=0)
    m_loc = _adaln_fwd(cs, mod_w)
    m_all = _all_gather("gather_mod", [m_loc])[0]
    m_full = jnp.moveaxis(m_all, 0, 2).reshape(L, 24, NDEV * CS) + mod_b[:, None, :]
    pending = {}
    tok = jnp.zeros((), F32)
    res = _exchange_start("gather_start_0", 1, [shards[n][0] for n in late_names], gather=True)
    pending[0] = res[:4]
    tok = tok + res[4][0, 0]
    for l in range(1, L):
        res = _exchange_start("gather_start_%d" % l, 1 + l, [shards[n][l] for n in early + late_names], gather=True)
        pending[l] = res[:4]
        tok = tok + res[4][0, 0]
    mods = []
    for l in range(L):
        ex = lax.dynamic_slice(m_full[l], (me * B, 0), (B, 6 * D)).reshape(B, 6, D)
        cc = jnp.broadcast_to(m_full[l, NDEV * B].reshape(1, 6, D), (B, 6, D))
        mods.append(jnp.pad(jnp.stack([cc, ex], axis=1), ((0, 0), (0, 0), (0, 2), (0, 0))) + tok)

    def full_of(l, xu):
        if l > 0:
            own, lands = _exchange_wait("gather_wait_%d" % l, *pending.pop(l), xu, gather=True)
            gathered[l] = dict(zip(early + late_names, _with_own(lands, own, me)))

        def late(after):
            if l == 0:
                own, lands = _exchange_wait("gather_wait_0", *pending.pop(0), after, gather=True)
                gathered[0].update(zip(late_names, _with_own(lands, own, me)))
            gl = gathered[l]
            return gl["w_out"].reshape(D, D), gl["w_mlp1"], gl["w_mlp2"]

        g = gathered[l]
        return dict(
            w_in=g["w_in"].reshape(D, IN_COLS),
            w_q_b=jnp.moveaxis(g["w_q_b"], 0, 1).reshape(QL, H * QK),
            w_kv_b=jnp.moveaxis(g["w_kv_b"], 0, 1).reshape(QL, H * (NOPE + VH)),
            conv_w=jnp.moveaxis(convw_all[:, l], 0, 1).reshape(4, XBC), late=late)

    def grad_blocks(big, names):
        make = dict(
            w_in=lambda g: g.reshape(NDEV, D // NDEV, IN_COLS), w_out=lambda g: g.reshape(NDEV, D // NDEV, D),
            w_q_b=lambda g: jnp.moveaxis(g.reshape(QL, NDEV, -1), 1, 0),
            w_kv_b=lambda g: jnp.moveaxis(g.reshape(QL, NDEV, -1), 1, 0), w_mlp1=lambda g: g, w_mlp2=lambda g: g)
        return [make[n](big[n]).astype(BF) for n in names]

    sent, sent_mlp, sent_mid = {}, {}, {}
    rest_names, mid_names = ["w_in"], ["w_out", "w_q_b", "w_kv_b"]

    def on_mid(l, g_wout, g_wq, g_wkv):
        if l > 0:
            return None
        blocks_mid = grad_blocks(dict(w_out=g_wout, w_q_b=g_wq, w_kv_b=g_wkv), mid_names)
        res = _exchange_start("grads_mid_start_0", 2 + 2 * L, blocks_mid, gather=False)
        sent_mid[l] = res[:4]
        return res[4][0, 0]

    def on_mlp(l, g_w1, g_w2):
        if l > 0:
            return None
        res = _exchange_start("grads_mlp_start_0", 1 + 2 * L, [g_w1.astype(BF), g_w2.astype(BF)], gather=False)
        sent_mlp[l] = res[:4]
        return res[4][0, 0]

    def on_grads(l, big, dx):
        if l == 0:
            return grad_blocks(big, rest_names), None
        res = _exchange_start("grads_start_%d" % l, 1 + L + l, grad_blocks(big, big_names), gather=False)
        sent[l] = res[:4]
        return None, res[4][0, 0]

    small_w = {k: args[k] for k in SMALL_NAMES if k != "mod_b"}
    loss_part, grad_x, blocks, smalls, dfnw, dmods = _local_step(x, ctx, loss_target, mods, full_of, small_w, on_grads, on_mlp,
                                                                 on_mid)

    dm_ex = jnp.stack([dmods[l][:, 1, :6].reshape(B, 6 * D) for l in range(L)])
    dm_cc = jnp.stack([jnp.sum(dmods[l][:, 0, :6], axis=0).reshape(6 * D) for l in range(L)])
    small_parts = dict(
        mod_b=jnp.sum(dm_ex, axis=1) + dm_cc,
        **{k: jnp.stack([smalls[l][k] for l in range(L)]) for k in SMALL_NAMES[1:-1]},
        final_norm_w=dfnw, conv_w=jnp.stack([smalls[l]["conv_w"] for l in range(L)]), dm_cc=dm_cc)
    adam_grads = [small_parts[k] for k in SMALL_NAMES]
    extras = [small_parts["conv_w"], dm_cc, dm_ex, loss_part.reshape(1)]
    pack = jnp.concatenate([_pack(adam_grads), _pack(extras)], axis=0)
    pack_all = _all_gather("gather_small_grads", [pack])[0]
    wpack = _pack([args[k] for k in SMALL_NAMES])
    mpack = _pack([args["m_" + k] for k in SMALL_NAMES])
    vpack = _pack([args["v_" + k] for k in SMALL_NAMES])
    n_adam = wpack.shape[0]
    res_small = _adamw("adamw_small", [pack_all[:, :n_adam]], wpack, mpack, vpack, n_adam)
    small_out = [_unpack(r, [args[k] for k in SMALL_NAMES]) for r in res_small]
    ext_all = pack_all[:, n_adam:]
    g_conv_full, dm_cc_tot, _, loss1 = _unpack(_sum_blocks("sum_small_grads", ext_all), extras)
    loss = loss1[0]
    dm_ex_all = jnp.stack([_unpack(ext_all[k], extras)[2] for k in range(NDEV)], axis=1)
    dm_rows = jnp.concatenate([dm_ex_all.reshape(L, NDEV * B, 6 * D), dm_cc_tot[:, None, :],
                               jnp.zeros((L, 24 - NDEV * B - 1, 6 * D), F32)], axis=1)
    dm_loc = lax.dynamic_slice(dm_rows, (0, 0, me * CS), (L, 24, CS))
    g_modw, gc_part = _adaln_bwd(cs, dm_loc, mod_w)
    gc_all = _all_gather("gather_cctx_grad", [gc_part[NDEV * B:NDEV * B + 8]])[0]
    cpad = lambda a: jnp.pad(a[None], ((0, 7), (0, 0)))
    res_cc = _adamw("adamw_cctx", [gc_all], cpad(c_ctx), cpad(m_c_ctx), cpad(v_c_ctx), 8)
    cc_out = [r[0] for r in res_cc]

    def waited(name, handles, names):
        srcs, lands = _exchange_wait(name, *handles, grad_x, gather=False)
        own = [lax.dynamic_index_in_dim(s, me, 0, keepdims=False) for s in srcs]
        return dict(zip(names, _with_own(lands, own, me)))

    recv = {0: dict(zip(rest_names, _all_to_all("exchange_grads", blocks[0])))}
    recv[0].update(waited("grads_mlp_wait_0", sent_mlp.pop(0), ["w_mlp1", "w_mlp2"]))
    recv[0].update(waited("grads_mid_wait_0", sent_mid.pop(0), mid_names))
    for l in range(1, L):
        recv[l] = waited("grads_wait_%d" % l, sent.pop(l), big_names)
    big_out = {}
    for name in big_names:
        w = args[name]
        Rr, C = math.prod(w.shape[:-1]), w.shape[-1]
        rl = Rr // L
        rt = rl if rl * C <= (1 << 17) else rl // 4
        res = _adamw("adamw_" + name, [recv[l][name].reshape(NDEV, rl, C) for l in range(L)], w.reshape(Rr, C),
                     args["m_" + name].reshape(Rr, C), args["v_" + name].reshape(Rr, C), rt)
        big_out[name] = [r.reshape(w.shape) for r in res]
    res = _adamw("adamw_mod_w", [g_modw.reshape(1, L * D, CS)], mod_w.reshape(L * D, CS), m_mod_w.reshape(L * D, CS),
                 v_mod_w.reshape(L * D, CS), L * D // 8)
    big_out["mod_w"] = [r.reshape(mod_w.shape) for r in res]
    CW = conv_w.shape[2]
    g_conv = lax.dynamic_slice(g_conv_full, (0, 0, me * CW), (L, 4, CW))
    res = _adamw("adamw_conv_w", [g_conv.reshape(1, L * 4, CW)], conv_w.reshape(L * 4, CW), m_conv_w.reshape(L * 4, CW),
                 v_conv_w.reshape(L * 4, CW), L * 4)
    big_out["conv_w"] = [r.reshape(conv_w.shape) for r in res]

    weights = ["c_ctx", "mod_w", "mod_b", "norm1_w", "norm2_w", "w_in", "conv_w", "conv_b", "dt_bias", "a_log", "ssd_d",
               "ssd_norm_w", "q_a_norm_w", "w_q_b", "kv_a_norm_w", "w_kv_b", "pool_w", "pool_scale", "w_out", "w_mlp1",
               "w_mlp2", "final_norm_w"]
    outs = [loss, grad_x]
    for kind in range(4):
        for name in weights:
            if name == "c_ctx":
                outs.append(cc_out[kind])
            elif name in big_out:
                outs.append(big_out[name][kind])
            else:
                outs.append(small_out[kind][SMALL_NAMES.index(name)])
    return tuple(outs)
```

```python
import functools
import math

import jax
import jax.numpy as jnp
from jax import lax
from jax.experimental import pallas as pl
from jax.experimental.pallas import tpu as pltpu

F32 = jnp.float32
BF = jnp.bfloat16
MXU = BF

D = 1024
CTX = 256
GRID_W = 64
EPS = 1e-6
H = 6
P = 64
SSD_IN = 384
NST = 128
XBC = 896
CH = 128
QL = 256
NOPE = 64
ROPE = 32
VH = 64
QK = 96
HP = 128
QW = H * HP
PD = 256
FF = 4096
IN_COLS = 2092
ROPE_THETA = 10000.0
PC = 2304
A_Z, A_XBC, A_QA, A_KVA, A_POOL, A_KR, A_DT = 0, 384, 1280, 1536, 1792, 2048, 2176
MIXW = SSD_IN + QW + PD
NDEV = 8
FSH = FF // NDEV
TM = 256
TQ = 256
TK = 256
VMEM_CAP = 64 * 1024 * 1024
ADAM_LR, ADAM_B1, ADAM_B2, ADAM_EPS, ADAM_WD, ADAM_STEP = 0.001, 0.9, 0.999, 1e-08, 0.01, 10


def _nbytes(shape, dtype):
    n = 1
    for s in shape:
        if s is not None:
            n *= s
    return n * jnp.dtype(dtype).itemsize


def _params(sem, block_bytes, extra=0):
    lim = min(2 * block_bytes + extra + (8 << 20), VMEM_CAP - (6 << 20))
    return pltpu.CompilerParams(dimension_semantics=sem, vmem_limit_bytes=int(lim))


def _hbm(arrays):
    return [pltpu.with_memory_space_constraint(a, pltpu.HBM) for a in arrays]


def _dot(a, b):
    return jnp.dot(a.astype(MXU), b.astype(MXU), preferred_element_type=F32)


def _dot_nt(a, b):
    return lax.dot_general(a.astype(MXU), b.astype(MXU), (((1,), (1,)), ((), ())), preferred_element_type=F32)


def _dot_tn(a, b):
    return lax.dot_general(a.astype(MXU), b.astype(MXU), (((0,), (0,)), ((), ())), preferred_element_type=F32)


def _dot01(m01, x):
    b16 = jnp.bfloat16
    m = m01.astype(b16)
    hi = x.astype(b16)
    r1 = x - hi.astype(F32)
    mid = r1.astype(b16)
    lo = (r1 - mid.astype(F32)).astype(b16)
    f = lambda v: jnp.dot(m, v, preferred_element_type=F32)
    return f(hi) + f(mid) + f(lo)


def _sigmoid(x):
    return 1.0 / (1.0 + jnp.exp(-x))


def _silu(x):
    return x * _sigmoid(x)


def _dsilu(x):
    s = _sigmoid(x)
    return s * (1.0 + x * (1.0 - s))


def _iota(shape, dim):
    return lax.broadcasted_iota(jnp.int32, shape, dim)


def _row(ref, k):
    blk = ref[...]
    return jnp.sum(jnp.where(_iota(blk.shape, 0) == k, blk, 0.0), axis=0, keepdims=True)


def _shift_rows(x, k):
    n = x.shape[0]
    return pltpu.roll(x, (-k) % n, axis=0)


class _Rows:
    def __init__(self, B, T):
        self.B, self.T = B, T
        self.nt = T // TM
        self.nct = CTX // TM

    def row(self, F):
        return pl.BlockSpec((None, TM, F), lambda b, i: (b, i, 0))

    def row2(self, F):
        return pl.BlockSpec((None, 2, TM, F), lambda b, i: (b, 0, i, 0))

    def prev8(self, F):
        return pl.BlockSpec((None, 8, F), lambda b, i: (b, jnp.maximum(i * (TM // 8) - 1, 0), 0))

    def next8(self, F):
        last = self.T // 8 - 1
        return pl.BlockSpec((None, 8, F), lambda b, i: (b, jnp.minimum((i + 1) * (TM // 8), last), 0))

    def mod(self):
        nct = self.nct
        return pl.BlockSpec((None, None, 8, D), lambda b, i: (b, jnp.where(i < nct, 0, 1), 0, 0))

    def const(self, shape):
        z = (0,) * len(shape)
        return pl.BlockSpec(tuple(shape), lambda b, i: z)

    def tgt(self, F):
        nct = self.nct
        return pl.BlockSpec((None, TM, F), lambda b, i: (b, jnp.maximum(i - nct, 0), 0))

    def xin(self, x):
        if isinstance(x, tuple):
            nct = self.nct
            return [(x[0], pl.BlockSpec((None, TM, D), lambda b, i: (b, jnp.minimum(i, nct - 1), 0))), (x[1], self.tgt(D))]
        return [(x, self.row(D))]

    def xval(self, i, refs):
        if len(refs) == 2:
            return jnp.where(i < self.nct, refs[0][...], refs[1][...])
        return refs[0][...]

    def call(self, name, body, ins, outs, scratch=(), extra=0):
        arrays = [a for a, _ in ins]
        in_specs = [s for _, s in ins]
        out_shape = [pltpu.HBM(o.shape, o.dtype) for o, _ in outs]
        out_specs = [s for _, s in outs]
        bb = 0
        for a, s in list(ins) + list(outs):
            if s.block_shape is not None:
                bb += _nbytes(s.block_shape, a.dtype)
        return pl.pallas_call(
            functools.partial(body), name=name, grid=(self.B, self.nt),
            in_specs=in_specs, out_specs=out_specs, out_shape=out_shape, scratch_shapes=list(scratch),
            compiler_params=_params(("arbitrary", "arbitrary"), bb, extra),
        )(*_hbm(arrays))

    def first_of_stream(self, i):
        return jnp.logical_or(i == 0, i == self.nct)

    def last_of_stream(self, i):
        return jnp.logical_or(i == self.nct - 1, i == self.nt - 1)

    def ext(self, i, prev_ref, cur, next_ref):
        pv = prev_ref[...].astype(F32) * jnp.where(self.first_of_stream(i), 0.0, 1.0)
        nx = next_ref[...].astype(F32) * jnp.where(self.last_of_stream(i), 0.0, 1.0)
        return jnp.concatenate([pv, cur, nx], axis=0)

    def stream_pos(self, i, rows):
        start = jnp.where(i < self.nct, 0, CTX)
        n = jnp.where(i < self.nct, CTX, self.T - CTX)
        t = i * TM - 8 - start + _iota((rows, 1), 0)
        return t, n


def _sds(shape, dtype):
    return jax.ShapeDtypeStruct(tuple(shape), dtype)


def _out(shape, dtype):
    return pltpu.HBM(tuple(shape), dtype)


def _norm_mod(x, nw, sh, sc):
    r = lax.rsqrt(jnp.mean(x * x, axis=-1, keepdims=True) + EPS)
    xn = x * r
    return xn * nw * (1.0 + sc) + sh, xn, r


def _norm_mod_bwd(dh, xn, r, nw, sc):
    dsh = jnp.sum(dh, axis=0, keepdims=True)
    dsc = jnp.sum(dh * (xn * nw), axis=0, keepdims=True)
    dnw = jnp.sum(dh * (1.0 + sc) * xn, axis=0, keepdims=True)
    dxn = dh * nw * (1.0 + sc)
    dx = r * (dxn - xn * jnp.mean(dxn * xn, axis=-1, keepdims=True))
    return dx, dsh, dsc, dnw


def _acc_rows(ref, first, rows):
    rid = _iota(ref.shape, 0)
    upd = jnp.zeros(ref.shape, F32)
    for k, v in rows.items():
        upd = upd + jnp.where(rid == k, v, 0.0)

    @pl.when(first)
    def _():
        ref[...] = upd

    @pl.when(jnp.logical_not(first))
    def _():
        ref[...] += upd


def _in_proj(R, x, mod, nw1, w_arr):
    B, T = R.B, R.T

    xin = R.xin(x)

    def body(*refs):
        x_refs = refs[:len(xin)]
        mod_ref, nw_ref, w_ref, z_ref, xbc_ref, qa_ref, kva_ref, pool_ref, kr_ref, dt_ref, h_ref = refs[len(xin):]
        h, _, _ = _norm_mod(R.xval(pl.program_id(1), x_refs), nw_ref[...], _row(mod_ref, 0), _row(mod_ref, 1))
        hb = h.astype(BF)
        h_ref[...] = hb
        p = jnp.dot(hb, w_ref[...], preferred_element_type=F32)
        z_ref[...] = p[:, A_Z:A_XBC]
        xbc_ref[...] = p[:, A_XBC:A_QA]
        qa_ref[...] = p[:, A_QA:A_KVA]
        kva_ref[...] = p[:, A_KVA:A_POOL]
        pool_ref[...] = p[:, A_POOL:A_KR]
        kr_ref[...] = p[:, A_KR:A_DT]
        dt_ref[...] = p[:, A_DT:PC]

    widths = [SSD_IN, XBC, QL, QL, PD, HP, HP]
    outs = [(_sds((B, T, w), F32), R.row(w)) for w in widths] + [(_sds((B, T, D), BF), R.row(D))]
    return R.call("in_proj", body,
                  xin + [(mod, R.mod()), (nw1, R.const((1, D))), (w_arr, R.const((D, PC)))],
                  outs, extra=8 << 20)


def _conv_pre(ext, w_ref, b_ref):
    return (_row(w_ref, 0) * _shift_rows(ext, -1)[8:8 + TM] + _row(w_ref, 1) * ext[8:8 + TM]
            + _row(w_ref, 2) * _shift_rows(ext, 1)[8:8 + TM] + _row(w_ref, 3) * _shift_rows(ext, 2)[8:8 + TM]
            + b_ref[...])


def _softplus(x):
    return jnp.maximum(x, 0.0) + jnp.log(1.0 + jnp.exp(-jnp.abs(x)))


def _ssd_prep(R, xbc_raw, dt_raw, conv_w8, conv_b, dtb):
    B, T = R.B, R.T

    def body(raw_ref, pv_ref, nx_ref, dtr_ref, w_ref, b_ref, dtb_ref, xbc_ref, dt_ref):
        i = pl.program_id(1)
        ext = R.ext(i, pv_ref, raw_ref[...], nx_ref)
        xbc_ref[...] = _silu(_conv_pre(ext, w_ref, b_ref))
        lane = _iota((TM, HP), 1)
        dtv = _softplus(dtr_ref[...] + dtb_ref[...])
        keep = lane < H
        dt_ref[0] = jnp.where(keep, dtv, 0.0)
        dt_ref[1] = jnp.where(keep, pltpu.roll(dtv, HP - H, axis=1), 0.0)

    return R.call("ssd_prep", body,
                  [(xbc_raw, R.row(XBC)), (xbc_raw, R.prev8(XBC)), (xbc_raw, R.next8(XBC)), (dt_raw, R.row(HP)),
                   (conv_w8, R.const((8, XBC))), (conv_b, R.const((1, XBC))), (dtb, R.const((1, HP)))],
                  [(_sds((B, T, XBC), F32), R.row(XBC)), (_sds((B, 2, T, HP), F32), R.row2(HP))], extra=12 << 20)


def _chunk_index(d, s, nc, nctc):
    if d == 0:
        return s
    return jnp.where(s < nctc, nctc - 1 - s, nc - 1 - (s - nctc))


def _dot01_r(x, m01, passes=3):
    b16 = jnp.bfloat16
    m = m01.astype(b16)
    out, rest = None, x
    for _ in range(passes):
        part = rest.astype(b16)
        rest = rest - part.astype(F32)
        term = jnp.dot(part, m, preferred_element_type=F32)
        out = term if out is None else out + term
    return out


def _scan_common(d, dtv, alog_ref):
    sel = _iota((HP, H * HP), 0) == (_iota((HP, H * HP), 1) >> 7)
    a_all = _dot01_r(-jnp.exp(alog_ref[d]), sel)
    dt_all = _dot01_r(dtv, sel)
    adt_all = dt_all * jnp.concatenate([a_all] * (CH // 8), axis=0)
    row = _iota((CH, CH), 0)
    col = _iota((CH, CH), 1)
    inc = col <= row if d == 0 else col >= row
    inc_t = row <= col if d == 0 else row >= col
    q_all = _dot01(inc, adt_all)
    return a_all, dt_all, adt_all, q_all, inc, inc_t


def _head_decay(h, adt_all, q_all, inc, inc_t):
    q = q_all[:, HP * h:HP * (h + 1)]
    q_t = q.T
    qtot = jnp.sum(adt_all[:, HP * h:HP * (h + 1)], axis=0, keepdims=True)
    lm = jnp.where(inc, jnp.exp(q - q_t), 0.0)
    lm_t = jnp.where(inc_t, jnp.exp(q_t - q), 0.0)
    return lm, lm_t, jnp.exp(q), jnp.exp(qtot), jnp.exp(qtot - q)


def _ssd_scan(B, T, xbc, dt2, alog2):
    nc, nctc = T // CH, CTX // CH

    def chain(d, xbc_ref, dt_ref, alog_ref, y_ref, hin_ref, hs):
        xbc_v = xbc_ref[...]
        dtv = dt_ref[...]
        blk = alog_ref[d]
        adt = dtv * -jnp.exp(jnp.sum(jnp.where(_iota(blk.shape, 0) == 0, blk, 0.0), axis=0, keepdims=True))
        row, col = _iota((CH, CH), 0), _iota((CH, CH), 1)
        inc = col <= row if d == 0 else col >= row
        inc_tf = (row <= col if d == 0 else row >= col).astype(F32)
        q = _dot01(inc, adt)
        hin_ref[...] = hs[d]
        for g in range(2):
            bg = xbc_v[:, SSD_IN + NST * g:SSD_IN + NST * (g + 1)]
            cg = xbc_v[:, SSD_IN + 2 * NST + NST * g:SSD_IN + 2 * NST + NST * (g + 1)]
            gm = _dot_nt(cg, bg)
            for r in range(3):
                h = 3 * g + r
                onehot = (_iota((1, HP), 1) == h).astype(F32)
                adt_h = jnp.sum(adt * onehot, axis=1, keepdims=True)
                qc = jnp.sum(q * onehot, axis=1, keepdims=True)
                dt_h = jnp.sum(dtv * onehot, axis=1, keepdims=True)
                qr = jnp.sum(adt_h * inc_tf, axis=0, keepdims=True)
                qtot = jnp.sum(adt_h, axis=0, keepdims=True)
                lm = jnp.where(inc, jnp.exp(qc - qr), 0.0)
                xh = xbc_v[:, P * h:P * (h + 1)] * dt_h
                hprev = hs[d, P * h:P * (h + 1), :]
                y_ref[:, P * h:P * (h + 1)] = _dot(gm * lm, xh) + jnp.exp(qc) * _dot_nt(cg, hprev)
                hs[d, P * h:P * (h + 1), :] = jnp.exp(qtot) * hprev + _dot_tn(xh * jnp.exp(qtot - qc), bg)

    def body(xf_ref, xb_ref, dtf_ref, dtb_ref, alog_ref, yf_ref, yb_ref, hf_ref, hb_ref, hs):
        @pl.when(pl.program_id(1) == 0)
        def _():
            hs[...] = jnp.zeros(hs.shape, F32)

        chain(0, xf_ref, dtf_ref, alog_ref, yf_ref, hf_ref, hs)
        chain(1, xb_ref, dtb_ref, alog_ref, yb_ref, hb_ref, hs)

    cidx = lambda d: (lambda s: _chunk_index(d, s, nc, nctc))
    specs = lambda d: dict(
        xbc=pl.BlockSpec((None, CH, XBC), lambda b, s: (b, cidx(d)(s), 0)),
        dt=pl.BlockSpec((None, None, CH, HP), lambda b, s: (b, d, cidx(d)(s), 0)),
        y=pl.BlockSpec((None, CH, SSD_IN), lambda b, s: (b, cidx(d)(s), 0)),
        h=pl.BlockSpec((None, None, SSD_IN, NST), lambda b, s: (b, cidx(d)(s), 0, 0)))
    f, r = specs(0), specs(1)
    bb = 2 * (_nbytes((CH, XBC), F32) + _nbytes((CH, HP), F32) + _nbytes((CH, SSD_IN), F32) + _nbytes((SSD_IN, NST), F32))
    return pl.pallas_call(
        functools.partial(body), name="ssd_scan", grid=(B, nc),
        in_specs=[f["xbc"], r["xbc"], f["dt"], r["dt"], pl.BlockSpec((2, 8, HP), lambda b, s: (0, 0, 0))],
        out_specs=[f["y"], r["y"], f["h"], r["h"]],
        out_shape=[_out((B, T, SSD_IN), F32)] * 2 + [_out((B, nc, SSD_IN, NST), F32)] * 2,
        scratch_shapes=[pltpu.VMEM((2, SSD_IN, NST), F32)],
        compiler_params=_params(("arbitrary",) * 2, bb, 12 << 20),
    )(*_hbm([xbc, xbc, dt2, dt2, alog2]))


def _swap8(u):
    lane = _iota(u.shape, 1)
    n = u.shape[1]
    return jnp.where((lane & 15) < 8, pltpu.roll(u, n - 8, axis=1), pltpu.roll(u, 8, axis=1))


def _rope(u, cos, sin_signed):
    return u * cos + _swap8(u) * sin_signed


def _rms(x, w):
    r = lax.rsqrt(jnp.mean(x * x, axis=-1, keepdims=True) + EPS)
    xh = x * r
    return xh * w, xh, r


def _rms_bwd(dy, xh, r, w):
    dw = jnp.sum(dy * xh, axis=0, keepdims=True)
    dxh = dy * w
    return r * (dxh - xh * jnp.mean(dxh * xh, axis=-1, keepdims=True)), dw


def _tile6(t):
    return jnp.concatenate([t] * H, axis=1)


def _per_head(fn, u):
    return jnp.concatenate([fn(u[:, HP * h:HP * (h + 1)]) for h in range(H)], axis=1)


def _mla_prep(R, qa, kva, kr, qnw, kvnw, wq, wk, wv, cos, sin):
    B, T = R.B, R.T
    scale = QK ** -0.5

    def body(qa_ref, kva_ref, kr_ref, qnw_ref, kvnw_ref, wq_ref, wk_ref, wv_ref, cos_ref, sin_ref,
             qt_ref, kt_ref, qtr_ref, ktr_ref, vtr_ref, cq_ref, ckv_ref):
        cq, _, _ = _rms(qa_ref[...], qnw_ref[...])
        ckv, _, _ = _rms(kva_ref[...], kvnw_ref[...])
        cqb, ckvb = cq.astype(BF), ckv.astype(BF)
        cq_ref[...] = cqb
        ckv_ref[...] = ckvb
        cos1, sin1 = cos_ref[...], sin_ref[...]
        q = _per_head(lambda u: _rope(u, cos1, sin1), jnp.dot(cqb, wq_ref[...], preferred_element_type=F32)) * scale
        qt_ref[...] = q.astype(BF)
        kk = _rope(kr_ref[...], cos1, sin1)
        hl = _iota((TM, QW), 1) & (HP - 1)
        k = jnp.dot(ckvb, wk_ref[...], preferred_element_type=F32) + _tile6(kk)
        k = jnp.where(jnp.logical_and(hl >= QK, hl < QK + 3), 1.0, k)
        kt_ref[...] = k.astype(BF)
        v = jnp.dot(ckvb, wv_ref[...], preferred_element_type=F32)
        v = jnp.where(jnp.logical_and(hl >= VH, hl < VH + 4), 1.0, v)
        for h in range(H):
            cols = slice(HP * h, HP * (h + 1))
            qtr_ref[h] = q[:, cols].T.astype(BF)
            ktr_ref[h] = k[:, cols].T.astype(BF)
            vtr_ref[h] = v[:, cols].T.astype(BF)

    tr = (_sds((B, H, HP, T), BF), pl.BlockSpec((None, H, HP, TM), lambda b, i: (b, 0, 0, i)))
    return R.call("mla_prep", body,
                  [(qa, R.row(QL)), (kva, R.row(QL)), (kr, R.row(HP)), (qnw, R.const((1, QL))), (kvnw, R.const((1, QL))),
                   (wq, R.const((QL, QW))), (wk, R.const((QL, QW))), (wv, R.const((QL, QW))),
                   (cos, pl.BlockSpec((TM, HP), lambda b, i: (i, 0))), (sin, pl.BlockSpec((TM, HP), lambda b, i: (i, 0)))],
                  [(_sds((B, T, QW), BF), R.row(QW))] * 2 + [tr] * 3 + [(_sds((B, T, QL), BF), R.row(QL))] * 2, extra=12 << 20)


def _flash_fwd(B, T, q_t, kt, v_t):
    nq, nk = T // TQ, T // TK
    HS = 2

    def body(q_ref, k_ref, v_ref, o_ref, lse_ref, s_scr):
        i = pl.program_id(2)

        def attend(nch):
            ms = []
            for hh in range(HS):
                q_tr = q_ref[hh]
                mrun = None
                for j in range(nch):
                    s = _dot(k_ref[TK * j:TK * (j + 1), HP * hh:HP * (hh + 1)], q_tr)
                    s_scr[hh, j] = s
                    mrun = s if mrun is None else jnp.maximum(mrun, s)
                ms.append(jnp.max(mrun, axis=0, keepdims=True))
            row = _iota((HP, TQ), 0)
            for hh in range(HS):
                acc_t = jnp.zeros((HP, TQ), F32)
                for j in range(nch):
                    acc_t = acc_t + _dot(v_ref[hh, :, TK * j:TK * (j + 1)], jnp.exp(s_scr[hh, j] - ms[hh]))
                l = jnp.sum(jnp.where(row == VH, acc_t, 0.0), axis=0, keepdims=True)
                o_ref[:, HP * hh:HP * (hh + 1)] = jnp.where(row < VH, acc_t / l, 0.0).T
                lse_ref[hh] = ms[hh] + jnp.log(l)

        @pl.when(i < CTX // TQ)
        def _():
            attend(CTX // TK)

        @pl.when(i >= CTX // TQ)
        def _():
            attend(nk)

    bb = HS * (_nbytes((TQ, HP), BF) + 2 * _nbytes((T, HP), BF) + 2 * _nbytes((TQ, HP), F32))
    return pl.pallas_call(
        functools.partial(body), name="flash_fwd", grid=(B, H // HS, nq),
        in_specs=[pl.BlockSpec((None, HS, HP, TQ), lambda b, h, i: (b, h, 0, i)),
                  pl.BlockSpec((None, T, HS * HP), lambda b, h, i: (b, 0, h)),
                  pl.BlockSpec((None, HS, HP, T), lambda b, h, i: (b, h, 0, 0))],
        out_specs=[pl.BlockSpec((None, TQ, HS * HP), lambda b, h, i: (b, i, h)),
                   pl.BlockSpec((None, HS, 1, TQ), lambda b, h, i: (b, h, 0, i))],
        out_shape=[_out((B, T, QW), F32), _out((B, H, 1, T), F32)],
        scratch_shapes=[pltpu.VMEM((HS, nk, TK, TQ), F32)],
        compiler_params=_params(("arbitrary",) * 3, bb, _nbytes((HS, nk, TK, TQ), F32) + (8 << 20)),
    )(*_hbm([q_t, kt, v_t]))


def _pool_terms(R, i, rows):
    t, n = R.stream_pos(i, rows)
    lane = _iota((1, PD), 1)
    half = jnp.where(lane < 64, 1, jnp.where(lane < 128, 2, jnp.where(lane < 192, 4, 8)))
    cnt = (jnp.minimum(t + half, n) - jnp.maximum(t - half, 0)).astype(F32)
    valid = jnp.logical_and(t >= 0, t < n)
    return jnp.where(valid, cnt, 1.0), valid.astype(F32), lane


def _lane_select(lane, a2, a4, a8, a16):
    return jnp.where(lane < 64, a2, jnp.where(lane < 128, a4, jnp.where(lane < 192, a8, a16)))


def _pool_centred(R, i, ext):
    cnt, valid, lane = _pool_terms(R, i, ext.shape[0])
    s2 = ext + _shift_rows(ext, -1)
    s4 = _shift_rows(s2, -1) + _shift_rows(s2, 1)
    s8 = _shift_rows(s4, -2) + _shift_rows(s4, 2)
    s16 = _shift_rows(s8, -4) + _shift_rows(s8, 4)
    return _lane_select(lane, s2, s4, s8, s16) / cnt - ext, cnt, valid, lane


def _group_mask():
    return _iota((1, SSD_IN), 1) < SSD_IN // 2


def _ssd_gate(yf_ref, yb_ref, xbc_ref, z_ref, dsk_ref):
    ytot = yf_ref[...] + yb_ref[...] + xbc_ref[:, 0:SSD_IN] * dsk_ref[...]
    z = z_ref[...]
    gz = ytot * _silu(z)
    g0 = _group_mask()
    sq = gz * gz
    s0 = jnp.sum(jnp.where(g0, sq, 0.0), axis=1, keepdims=True)
    s1 = jnp.sum(jnp.where(g0, 0.0, sq), axis=1, keepdims=True)
    half = SSD_IN // 2
    r = jnp.where(g0, lax.rsqrt(s0 / half + EPS), lax.rsqrt(s1 / half + EPS))
    return ytot, z, gz, r


def _out_proj(R, x, mod, yf, yb, xbc, z, o, pool_in, wbd, pscale, dsk, snw, wout):
    B, T = R.B, R.T
    xin = R.xin(x)

    def body(*refs):
        x_refs = refs[:len(xin)]
        (mod_ref, yf_ref, yb_ref, xbc_ref, z_ref, o_ref, u_ref, upv_ref, unx_ref, wbd_ref, psc_ref, dsk_ref, snw_ref, w_ref,
         xmid_ref, cat_ref, mix_ref) = refs[len(xin):]
        i = pl.program_id(1)
        _, _, gz, r = _ssd_gate(yf_ref, yb_ref, xbc_ref, z_ref, dsk_ref)
        dm, _, _, _ = _pool_centred(R, i, R.ext(i, upv_ref, u_ref[...], unx_ref))
        cat_ref[:, 0:SSD_IN] = (gz * r * snw_ref[...]).astype(BF)
        cat_ref[:, SSD_IN:SSD_IN + QW] = o_ref[...].astype(BF)
        cat_ref[:, SSD_IN + QW:MIXW] = (_dot(dm[8:8 + TM], wbd_ref[...]) * psc_ref[...]).astype(BF)
        mix = jnp.dot(cat_ref[...], w_ref[...], preferred_element_type=F32)
        mix_ref[...] = mix.astype(BF)
        xmid_ref[...] = R.xval(i, x_refs) + _row(mod_ref, 2) * mix

    return R.call("out_proj", body,
                  xin + [(mod, R.mod()), (yf, R.row(SSD_IN)), (yb, R.row(SSD_IN)), (xbc, R.row(XBC)), (z, R.row(SSD_IN)),
                         (o, R.row(QW)), (pool_in, R.row(PD)), (pool_in, R.prev8(PD)), (pool_in, R.next8(PD)),
                         (wbd, R.const((PD, PD))), (pscale, R.const((1, PD))),
                         (dsk, R.const((1, SSD_IN))), (snw, R.const((1, SSD_IN))), (wout, R.const((MIXW, D)))],
                  [(_sds((B, T, D), F32), R.row(D)), (_sds((B, T, MIXW), BF), R.row(MIXW)), (_sds((B, T, D), BF), R.row(D))],
                  extra=12 << 20)


def _load_once(first, pairs, sem):
    @pl.when(first)
    def _():
        cps = [pltpu.make_async_copy(src, dst, sem.at[k]) for k, (src, dst) in enumerate(pairs)]
        for cp in cps:
            cp.start()
        for cp in cps:
            cp.wait()


ANY = pl.BlockSpec(memory_space=pl.ANY)


def _mlp_fwd(R, xmid, mod, nw2, w1, w2):
    B, T = R.B, R.T

    def body(x_ref, mod_ref, nw_ref, w1_hbm, w2_hbm, xo_ref, h_ref, u_ref, y_ref, w1_v, w2_v, sem):
        first = jnp.logical_and(pl.program_id(0) == 0, pl.program_id(1) == 0)
        _load_once(first, [(w1_hbm, w1_v), (w2_hbm, w2_v)], sem)
        x = x_ref[...]
        h, _, _ = _norm_mod(x, nw_ref[...], _row(mod_ref, 3), _row(mod_ref, 4))
        hb = h.astype(BF)
        h_ref[...] = hb
        y = jnp.zeros((TM, D), F32)
        for j in range(NDEV):
            u = jnp.dot(hb, w1_v[j], preferred_element_type=F32)
            u_ref[:, FSH * j:FSH * (j + 1)] = u.astype(BF)
            a = jnp.square(jnp.maximum(u, 0.0))
            y = y + jnp.dot(a.astype(BF), w2_v[j], preferred_element_type=F32)
        y_ref[...] = y.astype(BF)
        xo_ref[...] = x + _row(mod_ref, 5) * y

    return R.call("mlp_fwd", body,
                  [(xmid, R.row(D)), (mod, R.mod()), (nw2, R.const((1, D))), (w1, ANY), (w2, ANY)],
                  [(_sds((B, T, D), F32), R.row(D)), (_sds((B, T, D), BF), R.row(D)), (_sds((B, T, FF), BF), R.row(FF)),
                   (_sds((B, T, D), BF), R.row(D))],
                  scratch=[pltpu.VMEM((NDEV, D, FSH), w1.dtype), pltpu.VMEM((NDEV, FSH, D), w2.dtype), pltpu.SemaphoreType.DMA((2,))],
                  extra=(2 * _nbytes((NDEV, D, FSH), BF)) + (8 << 20))


def _loss_head(R, x, tgt, fnw):
    B, T = R.B, R.T

    def body(x_ref, t_ref, w_ref, dx_ref, loss_ref, dw_ref):
        b, i = pl.program_id(0), pl.program_id(1)
        live = jnp.where(i >= R.nct, 1.0, 0.0)
        y, xh, r = _rms(x_ref[...], w_ref[...])
        err = (y - t_ref[...]) * live
        dy = err / D
        dxn, dw = _rms_bwd(dy, xh, r, w_ref[...])
        dx_ref[...] = dxn
        first = jnp.logical_and(b == 0, i == 0)
        part = 0.5 * jnp.sum(jnp.sum(err * err, axis=1, keepdims=True), axis=0, keepdims=True) / D
        _acc_rows(loss_ref, first, {0: jnp.broadcast_to(part, (1, HP))})
        _acc_rows(dw_ref, first, {0: dw})

    return R.call("loss_head", body,
                  [(x, R.row(D)), (tgt, R.tgt(D)), (fnw, R.const((1, D)))],
                  [(_sds((B, T, D), F32), R.row(D)), (_sds((8, HP), F32), R.const((8, HP))), (_sds((8, D), F32), R.const((8, D)))],
                  extra=8 << 20)


def _tn_matmul(name, a, b, tn, sub=1):
    B, T, K = a.shape
    N = b.shape[2]
    nk = 1
    while T % nk or (T // nk) > 1088 or (T // nk) % 16:
        nk += 1
    tk = T // nk
    kt = K if K <= 1536 else 2048
    w = tn // sub
    assert K % kt == 0 and N % tn == 0 and tn % sub == 0

    def body(a_ref, b_ref, o_ref):
        first = jnp.logical_and(pl.program_id(2) == 0, pl.program_id(3) == 0)

        @pl.when(first)
        def _():
            o_ref[...] = jnp.zeros(o_ref.shape, F32)

        acc = _dot_tn(a_ref[...], b_ref[...])
        for s in range(sub):
            o_ref[s] += acc[:, w * s:w * (s + 1)]

    bb = _nbytes((tk, kt), a.dtype) + _nbytes((tk, tn), b.dtype) + _nbytes((kt, tn), F32)
    return pl.pallas_call(
        functools.partial(body), name=name, grid=(N // tn, K // kt, B, nk),
        in_specs=[pl.BlockSpec((None, tk, kt), lambda j, kk, bi, t: (bi, t, kk)),
                  pl.BlockSpec((None, tk, tn), lambda j, kk, bi, t: (bi, t, j))],
        out_specs=pl.BlockSpec((sub, kt, w), lambda j, kk, bi, t: (j, kk, 0)),
        out_shape=_out((sub * N // tn, K, w), F32),
        compiler_params=_params(("arbitrary",) * 4, bb, _nbytes((kt, tn), F32) + (8 << 20)),
    )(*_hbm([a, b]))


def _mlp_bwd(R, dxo, xmid, ubf, ybf, mod, nw2, w1, w2):
    B, T = R.B, R.T

    def body(dxo_ref, x_ref, u_ref, y_ref, mod_ref, nw_ref, w1_hbm, w2_hbm,
             dxm_ref, du_ref, a_ref, dy_ref, dmod_ref, dnw_ref, w1_v, w2_v, sem):
        b, i = pl.program_id(0), pl.program_id(1)
        _load_once(jnp.logical_and(b == 0, i == 0), [(w1_hbm, w1_v), (w2_hbm, w2_v)], sem)
        dxo = dxo_ref[...]
        _, xn, r = _norm_mod(x_ref[...], nw_ref[...], _row(mod_ref, 3), _row(mod_ref, 4))
        dyb = (dxo * _row(mod_ref, 5)).astype(BF)
        dy_ref[...] = dyb
        dg2 = jnp.sum(dxo * y_ref[...].astype(F32), axis=0, keepdims=True)
        dh = jnp.zeros((TM, D), F32)
        for j in range(NDEV):
            rl = jnp.maximum(u_ref[:, FSH * j:FSH * (j + 1)].astype(F32), 0.0)
            a_ref[:, FSH * j:FSH * (j + 1)] = (rl * rl).astype(BF)
            du = (_dot_nt(dyb, w2_v[j]) * (2.0 * rl)).astype(BF)
            du_ref[:, FSH * j:FSH * (j + 1)] = du
            dh = dh + _dot_nt(du, w1_v[j])
        dx, dsh, dsc, dnw = _norm_mod_bwd(dh, xn, r, nw_ref[...], _row(mod_ref, 4))
        dxm_ref[...] = dxo + dx
        _acc_rows(dmod_ref, R.first_of_stream(i), {3: dsh, 4: dsc, 5: dg2})
        _acc_rows(dnw_ref, jnp.logical_and(b == 0, i == 0), {0: dnw})

    return R.call("mlp_bwd", body,
                  [(dxo, R.row(D)), (xmid, R.row(D)), (ubf, R.row(FF)), (ybf, R.row(D)), (mod, R.mod()), (nw2, R.const((1, D))),
                   (w1, ANY), (w2, ANY)],
                  [(_sds((B, T, D), F32), R.row(D)), (_sds((B, T, FF), BF), R.row(FF)), (_sds((B, T, FF), BF), R.row(FF)),
                   (_sds((B, T, D), BF), R.row(D)), (_sds((B, 2, 8, D), F32), R.mod()), (_sds((8, D), F32), R.const((8, D)))],
                  scratch=[pltpu.VMEM((NDEV, D, FSH), w1.dtype), pltpu.VMEM((NDEV, FSH, D), w2.dtype), pltpu.SemaphoreType.DMA((2,))],
                  extra=(2 * _nbytes((NDEV, D, FSH), BF)) + (8 << 20))


def _minus_in_lanes(x, col, first):
    b16 = jnp.bfloat16
    hi = col.astype(b16).astype(F32)
    r1 = col - hi
    mid = r1.astype(b16).astype(F32)
    lane = _iota(x.shape, 1)
    return jnp.where(lane == first, -hi, jnp.where(lane == first + 1, -mid, jnp.where(lane == first + 2, mid - r1, x)))


def _out_bwd(R, dxm, mod, mixbf, yf, yb, xbc, z, o, qt, lse, dsk, snw, wout):
    B, T = R.B, R.T

    def body(dxm_ref, mod_ref, mix_ref, yf_ref, yb_ref, xbc_ref, z_ref, o_ref, q_ref, lse_ref, dsk_ref, snw_ref, w_ref,
             dmix_ref, dz_ref, dyt_ref, dxsk_ref, qs_ref, dos_ref, dotr_ref, dpo_ref, dmod_ref, dvec_ref):
        b, i = pl.program_id(0), pl.program_id(1)
        dxm = dxm_ref[...]
        dmixb = (dxm * _row(mod_ref, 2)).astype(BF)
        dmix_ref[...] = dmixb
        dg1 = jnp.sum(dxm * mix_ref[...].astype(F32), axis=0, keepdims=True)
        dcat = _dot_nt(dmixb, w_ref[...])
        do_v = dcat[:, SSD_IN:SSD_IN + QW]
        doo = do_v * o_ref[...]
        for h in range(H):
            cols = slice(HP * h, HP * (h + 1))
            dd = jnp.sum(doo[:, cols], axis=1, keepdims=True)
            dos_ref[:, cols] = _minus_in_lanes(do_v[:, cols], dd, VH + 1).astype(BF)
            lse_b = jnp.broadcast_to(lse_ref[h], (HP, TM)).T
            qs_ref[:, cols] = _minus_in_lanes(q_ref[:, cols].astype(F32), lse_b, QK).astype(BF)
            dotr_ref[h] = do_v[:, cols].T.astype(BF)
        dpo_ref[...] = dcat[:, SSD_IN + QW:MIXW]
        dsn = dcat[:, 0:SSD_IN]
        ytot, zv, gz, r = _ssd_gate(yf_ref, yb_ref, xbc_ref, z_ref, dsk_ref)
        gh = gz * r
        dsnw = jnp.sum(dsn * gh, axis=0, keepdims=True)
        dgh = dsn * snw_ref[...]
        g0 = _group_mask()
        pr = dgh * gh
        half = SSD_IN // 2
        m0 = jnp.sum(jnp.where(g0, pr, 0.0), axis=1, keepdims=True) / half
        m1 = jnp.sum(jnp.where(g0, 0.0, pr), axis=1, keepdims=True) / half
        dgz = r * (dgh - gh * jnp.where(g0, m0, m1))
        dyt = dgz * _silu(zv)
        dz_ref[...] = (dgz * ytot * _dsilu(zv)).astype(BF)
        dyt_ref[...] = dyt
        dxsk_ref[...] = dyt * dsk_ref[...]
        ddsk = jnp.sum(dyt * xbc_ref[:, 0:SSD_IN], axis=0, keepdims=True)
        _acc_rows(dmod_ref, R.first_of_stream(i), {2: dg1})
        _acc_rows(dvec_ref, jnp.logical_and(b == 0, i == 0), {0: dsnw, 1: ddsk})

    return R.call("out_bwd", body,
                  [(dxm, R.row(D)), (mod, R.mod()), (mixbf, R.row(D)), (yf, R.row(SSD_IN)), (yb, R.row(SSD_IN)), (xbc, R.row(XBC)),
                   (z, R.row(SSD_IN)),
                   (o, R.row(QW)), (qt, R.row(QW)), (lse, pl.BlockSpec((None, H, 1, TM), lambda b, i: (b, 0, 0, i))),
                   (dsk, R.const((1, SSD_IN))), (snw, R.const((1, SSD_IN))), (wout, R.const((MIXW, D)))],
                  [(_sds((B, T, D), BF), R.row(D)), (_sds((B, T, SSD_IN), BF), R.row(SSD_IN)), (_sds((B, T, SSD_IN), F32), R.row(SSD_IN)),
                   (_sds((B, T, SSD_IN), F32), R.row(SSD_IN)), (_sds((B, T, QW), BF), R.row(QW)), (_sds((B, T, QW), BF), R.row(QW)),
                   (_sds((B, H, HP, T), BF), pl.BlockSpec((None, H, HP, TM), lambda b, i: (b, 0, 0, i))),
                   (_sds((B, T, PD), F32), R.row(PD)),
                   (_sds((B, 2, 8, D), F32), R.mod()), (_sds((8, SSD_IN), F32), R.const((8, SSD_IN)))],
                  extra=8 << 20)


def _pool_bwd(R, dpo, u, wbd, scale):
    B, T = R.B, R.T

    def body(d_ref, dpv_ref, dnx_ref, u_ref, upv_ref, unx_ref, w_ref, sc_ref, du_ref, dw_ref, dsc_ref):
        b, i = pl.program_id(0), pl.program_id(1)
        ext_u = R.ext(i, upv_ref, u_ref[...], unx_ref)
        ext_d = R.ext(i, dpv_ref, d_ref[...], dnx_ref)
        dm, cnt, valid, lane = _pool_centred(R, i, ext_u)
        ddm = _dot_nt(ext_d * sc_ref[...], w_ref[...]) * valid
        e = ddm / cnt
        a2 = e + _shift_rows(e, 1)
        a4 = _shift_rows(a2, -1) + _shift_rows(a2, 1)
        a8 = _shift_rows(a4, -2) + _shift_rows(a4, 2)
        a16 = _shift_rows(a8, -4) + _shift_rows(a8, 4)
        du_ref[...] = (_lane_select(lane, a2, a4, a8, a16) - ddm)[8:8 + TM].astype(BF)
        dmc = dm[8:8 + TM]
        dpo_c = d_ref[...]
        first = jnp.logical_and(b == 0, i == 0)

        @pl.when(first)
        def _():
            dw_ref[...] = jnp.zeros(dw_ref.shape, F32)

        dw_ref[...] += _dot_tn(dmc, dpo_c * sc_ref[...])
        _acc_rows(dsc_ref, first, {0: jnp.sum(dpo_c * _dot(dmc, w_ref[...]), axis=0, keepdims=True)})

    return R.call("pool_bwd", body,
                  [(dpo, R.row(PD)), (dpo, R.prev8(PD)), (dpo, R.next8(PD)), (u, R.row(PD)), (u, R.prev8(PD)), (u, R.next8(PD)),
                   (wbd, R.const((PD, PD))), (scale, R.const((1, PD)))],
                  [(_sds((B, T, PD), BF), R.row(PD)), (_sds((PD, PD), F32), R.const((PD, PD))), (_sds((8, PD), F32), R.const((8, PD)))],
                  extra=8 << 20)


def _flash_bwd(B, T, qs, kt, dos, q_t, k_t, v_t, do_t):
    nk, nq = T // TK, T // TQ
    HS = 2

    def body(k_ref, kt_ref, vt_ref, q_ref, do_ref, qt_ref, dot_ref, dq_ref, dk_ref, dv_ref, s_scr, dp_scr, ds_scr):
        j = pl.program_id(2)
        ctx_keys = jnp.where(j < CTX // TK, 1.0, 0.0)
        for hh in range(HS):
            cols = slice(HP * hh, HP * (hh + 1))
            s_scr[hh] = _dot(q_ref[:, cols], kt_ref[hh])
            dp_scr[hh] = _dot(do_ref[:, cols], vt_ref[hh])
        for hh in range(HS):
            cols = slice(HP * hh, HP * (hh + 1))
            dk_tr = jnp.zeros((HP, TK), F32)
            dv_tr = jnp.zeros((HP, TK), F32)
            for i in range(nq):
                rows = slice(TQ * i, TQ * (i + 1))
                p = jnp.exp(s_scr[hh, rows, :])
                if i < CTX // TQ:
                    p = p * ctx_keys
                ds = (p * dp_scr[hh, rows, :]).astype(BF)
                ds_scr[hh, rows, :] = ds
                dv_tr = dv_tr + _dot(dot_ref[hh, :, rows], p)
                dk_tr = dk_tr + _dot(qt_ref[hh, :, rows], ds)
            dq = _dot(ds_scr[hh], k_ref[:, cols])

            @pl.when(j == 0)
            def _():
                dq_ref[:, cols] = dq

            @pl.when(j > 0)
            def _():
                dq_ref[:, cols] += dq

            dk_ref[:, cols] = dk_tr.T
            dv_ref[:, cols] = dv_tr.T

    tspec = pl.BlockSpec((None, TK, HS * HP), lambda b, h, j: (b, j, h))
    fspec = pl.BlockSpec((None, T, HS * HP), lambda b, h, j: (b, 0, h))
    ttspec = pl.BlockSpec((None, HS, HP, TK), lambda b, h, j: (b, h, 0, j))
    ftspec = pl.BlockSpec((None, HS, HP, T), lambda b, h, j: (b, h, 0, 0))
    bb = HS * (4 * _nbytes((T, HP), BF) + _nbytes((T, HP), F32) + 8 * _nbytes((TK, HP), F32))
    scr = HS * (2 * _nbytes((T, TK), F32) + _nbytes((T, TK), BF))
    return pl.pallas_call(
        functools.partial(body), name="flash_bwd", grid=(B, H // HS, nk),
        in_specs=[tspec, ttspec, ttspec, fspec, fspec, ftspec, ftspec],
        out_specs=[fspec, tspec, tspec],
        out_shape=[_out((B, T, QW), F32)] * 3,
        scratch_shapes=[pltpu.VMEM((HS, T, TK), F32), pltpu.VMEM((HS, T, TK), F32), pltpu.VMEM((HS, T, TK), BF)],
        compiler_params=_params(("arbitrary",) * 3, bb, scr + (8 << 20)),
    )(*_hbm([kt, k_t, v_t, qs, dos, q_t, do_t]))


def _mla_bwd(R, dqt, dkt, dvt, qa, kva, qnw, kvnw, wq, wk, wv, cos, sin):
    B, T = R.B, R.T
    scale = QK ** -0.5

    def body(dq_ref, dk_ref, dv_ref, qa_ref, kva_ref, qnw_ref, kvnw_ref, wq_ref, wk_ref, wv_ref, cos_ref, sin_ref,
             dqp_ref, dkvb_ref, dqa_ref, dkva_ref, dkr_ref, dnw_ref):
        b, i = pl.program_id(0), pl.program_id(1)
        cos1, sin1 = cos_ref[...], sin_ref[...]
        dq = dq_ref[...] * scale
        dqp = _per_head(lambda g: g * cos1 + _swap8(g * sin1), dq).astype(BF)
        dqp_ref[...] = dqp
        dkv = dk_ref[...]
        dkb = dkv.astype(BF)
        dvb = dv_ref[...].astype(BF)
        dkvb_ref[:, :QW] = dkb
        dkvb_ref[:, QW:] = dvb
        dkk = dkv[:, 0:HP]
        for h in range(1, H):
            dkk = dkk + dkv[:, HP * h:HP * (h + 1)]
        lane = _iota((TM, HP), 1)
        rope_lane = jnp.logical_and(lane >= NOPE, lane < NOPE + ROPE)
        dkr_ref[...] = jnp.where(rope_lane, dkk * cos1 + _swap8(dkk * sin1), 0.0).astype(BF)
        _, qh, qr = _rms(qa_ref[...], qnw_ref[...])
        _, kh, kr_ = _rms(kva_ref[...], kvnw_ref[...])
        dcq = _dot_nt(dqp, wq_ref[...])
        dckv = _dot_nt(dkb, wk_ref[...]) + _dot_nt(dvb, wv_ref[...])
        dqa, dqw = _rms_bwd(dcq, qh, qr, qnw_ref[...])
        dkva, dkw = _rms_bwd(dckv, kh, kr_, kvnw_ref[...])
        dqa_ref[...] = dqa.astype(BF)
        dkva_ref[...] = dkva.astype(BF)
        _acc_rows(dnw_ref, jnp.logical_and(b == 0, i == 0), {0: dqw, 1: dkw})

    tab = pl.BlockSpec((TM, HP), lambda b, i: (i, 0))
    return R.call("mla_bwd", body,
                  [(dqt, R.row(QW)), (dkt, R.row(QW)), (dvt, R.row(QW)), (qa, R.row(QL)), (kva, R.row(QL)),
                   (qnw, R.const((1, QL))), (kvnw, R.const((1, QL))), (wq, R.const((QL, QW))), (wk, R.const((QL, QW))),
                   (wv, R.const((QL, QW))), (cos, tab), (sin, tab)],
                  [(_sds((B, T, QW), BF), R.row(QW)), (_sds((B, T, 2 * QW), BF), R.row(2 * QW))]
                  + [(_sds((B, T, QL), BF), R.row(QL))] * 2
                  + [(_sds((B, T, HP), BF), R.row(HP)), (_sds((8, QL), F32), R.const((8, QL)))], extra=8 << 20)


def _ssd_scan_bwd(B, T, dyt, xbc, dt2, alog2, hin_f, hin_b):
    nc, nctc = T // CH, CTX // CH

    def chain(d, dy_ref, xbc_ref, dt_ref, alog_ref, hin_ref, dxbc_ref, ddt_ref, da_ref, dhs_all):
        dhs = dhs_all.at[d]
        xbc_v = xbc_ref[...]
        dyv = dy_ref[...]
        a_all, dt_all, adt_all, q_all, inc, inc_t = _scan_common(d, dt_ref[...], alog_ref)
        ones_p = jnp.ones((P, HP), F32)
        total = lambda m: jnp.sum(jnp.sum(m, axis=0, keepdims=True), axis=1, keepdims=True)
        dq_parts, dqtot_parts, ddtx_parts = [], [], []
        for g in range(2):
            bg = xbc_v[:, SSD_IN + NST * g:SSD_IN + NST * (g + 1)]
            cg = xbc_v[:, SSD_IN + 2 * NST + NST * g:SSD_IN + 2 * NST + NST * (g + 1)]
            bg_t, cg_t = bg.T, cg.T
            gm = _dot(cg, bg_t)
            gm_t = _dot(bg, cg_t)
            dgm = jnp.zeros((CH, CH), F32)
            dgm_t = jnp.zeros((CH, CH), F32)
            dbg = jnp.zeros((CH, NST), F32)
            dcg = jnp.zeros((CH, NST), F32)
            for r in range(3):
                h = 3 * g + r
                lm, lm_t, eq, etot, dte = _head_decay(h, adt_all, q_all, inc, inc_t)
                eq_p, etot_p, dte_p = eq[:, :P], etot[:, :P], dte[:, :P]
                dt_p = dt_all[:, HP * h:HP * h + P]
                xs_h = xbc_v[:, P * h:P * (h + 1)]
                xh = xs_h * dt_p
                sm, sm_t = gm * lm, gm_t * lm_t
                dy_h = dyv[:, P * h:P * (h + 1)]
                hprev = hin_ref[P * h:P * (h + 1), :].T
                dho = dhs[:, P * h:P * (h + 1)]
                ds = _dot_nt(dy_h, xh)
                ds_t = _dot_nt(xh, dy_h)
                dx = _dot(sm_t, dy_h)
                edy = eq_p * dy_h
                yo = _dot(cg, hprev)
                dcg = dcg + _dot_nt(edy, hprev)
                dhin = _dot(cg_t, edy) + etot_p * dho
                zs = _dot(bg, dho)
                dx = dx + dte_p * zs
                wm = dte_p * xh * zs
                dbg = dbg + _dot_nt(xh * dte_p, dho)
                dgm = dgm + ds * lm
                dgm_t = dgm_t + ds_t * lm_t
                rs = jnp.sum(ds * sm - ds_t * sm_t, axis=1, keepdims=True) + jnp.sum(edy * yo - wm, axis=1, keepdims=True)
                dq_parts.append(jnp.broadcast_to(rs, (CH, HP)))
                dqtot_parts.append(total(hprev * dho) * etot + total(wm))
                dxbc_ref[:, P * h:P * (h + 1)] = dx * dt_p
                ddtx_parts.append(_dot01_r(dx * xs_h, ones_p, passes=2))
                dhs[:, P * h:P * (h + 1)] = dhin
            dcg = dcg + _dot(dgm, bg)
            dbg = dbg + _dot(dgm_t, cg)
            dxbc_ref[:, SSD_IN + NST * g:SSD_IN + NST * (g + 1)] = dbg
            dxbc_ref[:, SSD_IN + 2 * NST + NST * g:SSD_IN + 2 * NST + NST * (g + 1)] = dcg
        cat = lambda parts: jnp.concatenate(parts, axis=1)
        dadt_all = _dot01(inc_t, cat(dq_parts)) + cat(dqtot_parts)
        ddt_all = cat(ddtx_parts) + dadt_all * jnp.concatenate([a_all] * (CH // 8), axis=0)
        da_all = jnp.sum(dadt_all * dt_all, axis=0, keepdims=True)
        lane, lane1 = _iota((CH, HP), 1), _iota((1, HP), 1)
        ddt = jnp.zeros((CH, HP), F32)
        da = jnp.zeros((1, HP), F32)
        for h in range(H):
            ddt = ddt + jnp.where(lane == h, ddt_all[:, HP * h:HP * (h + 1)], 0.0)
            da = da + jnp.where(lane1 == h, da_all[:, HP * h:HP * (h + 1)], 0.0)
        ddt_ref[...] = ddt
        da_ref[d] += jnp.where(_iota((8, HP), 0) == 0, da, 0.0)

    def body(dyf_ref, dyb_ref, xf_ref, xb_ref, dtf_ref, dtb_ref, alog_ref, hf_ref, hb_ref,
             dxf_ref, dxb_ref, ddtf_ref, ddtb_ref, da_ref, dhs):
        @pl.when(pl.program_id(1) == 0)
        def _():
            dhs[...] = jnp.zeros(dhs.shape, F32)
            da_ref[...] = jnp.zeros(da_ref.shape, F32)

        chain(0, dyf_ref, xf_ref, dtf_ref, alog_ref, hf_ref, dxf_ref, ddtf_ref, da_ref, dhs)
        chain(1, dyb_ref, xb_ref, dtb_ref, alog_ref, hb_ref, dxb_ref, ddtb_ref, da_ref, dhs)

    cidx = lambda d: (lambda s: _chunk_index(d, nc - 1 - s, nc, nctc))
    specs = lambda d: dict(
        dy=pl.BlockSpec((None, CH, SSD_IN), lambda b, s: (b, cidx(d)(s), 0)),
        xbc=pl.BlockSpec((None, CH, XBC), lambda b, s: (b, cidx(d)(s), 0)),
        dt=pl.BlockSpec((None, None, CH, HP), lambda b, s: (b, d, cidx(d)(s), 0)),
        h=pl.BlockSpec((None, None, SSD_IN, NST), lambda b, s: (b, cidx(d)(s), 0, 0)),
        ddt=pl.BlockSpec((None, CH, HP), lambda b, s: (b, cidx(d)(s), 0)))
    f, r = specs(0), specs(1)
    bb = 2 * (2 * _nbytes((CH, XBC), F32) + 2 * _nbytes((CH, HP), F32) + _nbytes((CH, SSD_IN), F32) + _nbytes((SSD_IN, NST), F32))
    return pl.pallas_call(
        functools.partial(body), name="ssd_scan_bwd", grid=(B, nc),
        in_specs=[f["dy"], r["dy"], f["xbc"], r["xbc"], f["dt"], r["dt"], pl.BlockSpec((2, 8, HP), lambda b, s: (0, 0, 0)),
                  f["h"], r["h"]],
        out_specs=[f["xbc"], r["xbc"], f["ddt"], r["ddt"], pl.BlockSpec((None, 2, 8, HP), lambda b, s: (b, 0, 0, 0))],
        out_shape=[_out((B, T, XBC), F32)] * 2 + [_out((B, T, HP), F32)] * 2 + [_out((B, 2, 8, HP), F32)],
        scratch_shapes=[pltpu.VMEM((2, NST, SSD_IN), F32)],
        compiler_params=_params(("arbitrary",) * 2, bb, 16 << 20),
    )(*_hbm([dyt, dyt, xbc, xbc, dt2, dt2, alog2, hin_f, hin_b]))


def _ssd_prep_bwd(R, dxbc_f, dxbc_b, dxsk, ddt_f, ddt_b, xbc_raw, dt_raw, conv_w8, conv_b, dtb):
    B, T = R.B, R.T

    def body(dxf_ref, dxb_ref, dsk_ref, ddtf_ref, ddtb2_ref, raw_ref, pv_ref, nx_ref, dtr_ref, w_ref, b_ref, dtb_ref,
             dpre_ref, ddtr_ref, dcw_ref, dvec_ref, ddtb_ref):
        b, i = pl.program_id(0), pl.program_id(1)
        ext = R.ext(i, pv_ref, raw_ref[...], nx_ref)
        pre = _conv_pre(ext, w_ref, b_ref)
        dxbc = dxf_ref[...] + dxb_ref[...]
        skip = jnp.concatenate([dsk_ref[...], jnp.zeros((TM, XBC - SSD_IN), F32)], axis=1)
        dpre = (dxbc + skip) * _dsilu(pre)
        dpre_ref[...] = dpre
        first = jnp.logical_and(b == 0, i == 0)
        taps = {k: jnp.sum(dpre * _shift_rows(ext, k - 1)[8:8 + TM], axis=0, keepdims=True) for k in range(4)}
        _acc_rows(dcw_ref, first, taps)
        _acc_rows(dvec_ref, first, {0: jnp.sum(dpre, axis=0, keepdims=True)})
        ddt = ddtf_ref[...] + pltpu.roll(ddtb2_ref[...], H, axis=1)
        ddtr = ddt * _sigmoid(dtr_ref[...] + dtb_ref[...])
        ddtr = jnp.where(_iota((TM, HP), 1) < 2 * H, ddtr, 0.0)
        ddtr_ref[...] = ddtr.astype(BF)
        _acc_rows(ddtb_ref, first, {0: jnp.sum(ddtr, axis=0, keepdims=True)})

    return R.call("ssd_prep_bwd", body,
                  [(dxbc_f, R.row(XBC)), (dxbc_b, R.row(XBC)), (dxsk, R.row(SSD_IN)), (ddt_f, R.row(HP)), (ddt_b, R.row(HP)),
                   (xbc_raw, R.row(XBC)), (xbc_raw, R.prev8(XBC)),
                   (xbc_raw, R.next8(XBC)), (dt_raw, R.row(HP)), (conv_w8, R.const((8, XBC))), (conv_b, R.const((1, XBC))),
                   (dtb, R.const((1, HP)))],
                  [(_sds((B, T, XBC), F32), R.row(XBC)), (_sds((B, T, HP), BF), R.row(HP)), (_sds((8, XBC), F32), R.const((8, XBC))),
                   (_sds((8, XBC), F32), R.const((8, XBC))), (_sds((8, HP), F32), R.const((8, HP)))], extra=16 << 20)


def _in_bwd(R, dxm, x, mod, nw1, dz, dpre, dqa, dkva, dpool, dkr, ddtr, conv_w8, w_arr, latent_only=False):
    B, T = R.B, R.T
    dx_out = (_sds((B, T - CTX, D), F32), R.tgt(D)) if latent_only else (_sds((B, T, D), F32), R.row(D))

    xin = R.xin(x)

    def body(*refs):
        x_refs = refs[:len(xin)]
        (dxm_ref, mod_ref, nw_ref, dz_ref, dp_ref, dpp_ref, dpn_ref, dqa_ref, dkva_ref, dpool_ref, dkr_ref, ddt_ref,
         cw_ref, w_ref, dx_ref, dproj_ref, dmod_ref, dnw_ref) = refs[len(xin):]
        b, i = pl.program_id(0), pl.program_id(1)
        ext = R.ext(i, dpp_ref, dp_ref[...], dpn_ref)
        draw = (_row(cw_ref, 0) * _shift_rows(ext, 1)[8:8 + TM] + _row(cw_ref, 1) * ext[8:8 + TM]
                + _row(cw_ref, 2) * _shift_rows(ext, -1)[8:8 + TM] + _row(cw_ref, 3) * _shift_rows(ext, -2)[8:8 + TM])
        dproj_ref[:, A_Z:A_XBC] = dz_ref[...]
        dproj_ref[:, A_XBC:A_QA] = draw.astype(BF)
        dproj_ref[:, A_QA:A_KVA] = dqa_ref[...]
        dproj_ref[:, A_KVA:A_POOL] = dkva_ref[...]
        dproj_ref[:, A_POOL:A_KR] = dpool_ref[...]
        dproj_ref[:, A_KR:A_DT] = dkr_ref[...]
        dproj_ref[:, A_DT:PC] = ddt_ref[...]
        dh = _dot_nt(dproj_ref[...], w_ref[...])
        _, xn, r = _norm_mod(R.xval(i, x_refs), nw_ref[...], _row(mod_ref, 0), _row(mod_ref, 1))
        dx, dsh, dsc, dnw = _norm_mod_bwd(dh, xn, r, nw_ref[...], _row(mod_ref, 1))
        dx_ref[...] = dxm_ref[...] + dx
        _acc_rows(dmod_ref, R.first_of_stream(i), {0: dsh, 1: dsc})
        _acc_rows(dnw_ref, jnp.logical_and(b == 0, i == 0), {0: dnw})

    return R.call("in_bwd", body,
                  xin + [(dxm, R.row(D)), (mod, R.mod()), (nw1, R.const((1, D))), (dz, R.row(SSD_IN)), (dpre, R.row(XBC)),
                   (dpre, R.prev8(XBC)), (dpre, R.next8(XBC)), (dqa, R.row(QL)), (dkva, R.row(QL)), (dpool, R.row(PD)), (dkr, R.row(HP)),
                   (ddtr, R.row(HP)), (conv_w8, R.const((8, XBC))), (w_arr, R.const((D, PC)))],
                  [dx_out, (_sds((B, T, PC), BF), R.row(PC)), (_sds((B, 2, 8, D), F32), R.mod()),
                   (_sds((8, D), F32), R.const((8, D)))], extra=12 << 20)


def _adaln_fwd(cs, mod_w):
    L, _, C = mod_w.shape

    def body(c_ref, w_ref, o_ref):
        s = _silu(c_ref[...]).astype(BF)
        for l in range(L):
            o_ref[l] = jnp.dot(s, w_ref[l].astype(BF), preferred_element_type=F32)

    return pl.pallas_call(functools.partial(body), name="adaln_fwd", out_shape=_sds((L, 24, C), F32),
                          compiler_params=_params(None, _nbytes(mod_w.shape, F32) + _nbytes((L, 24, C), F32), 8 << 20))(cs, mod_w)


def _adaln_bwd(cs, dm, mod_w):
    L, _, C = mod_w.shape

    def body(c_ref, dm_ref, w_ref, gw_ref, gc_ref):
        c = c_ref[...]
        s = _silu(c).astype(BF)
        acc = jnp.zeros((24, D), F32)
        for l in range(L):
            dmb = dm_ref[l].astype(BF)
            gw_ref[l] = _dot_tn(s, dmb)
            acc = acc + _dot_nt(dmb, w_ref[l])
        gc_ref[...] = acc * _dsilu(c)

    return pl.pallas_call(functools.partial(body), name="adaln_bwd", out_shape=[_sds((L, D, C), F32), _sds((24, D), F32)],
                          compiler_params=_params(None, 2 * _nbytes(mod_w.shape, F32), 8 << 20))(cs, dm, mod_w)


def _sum_blocks(name, parts):
    Pn, Rr, C = parts.shape

    def body(p_ref, o_ref):
        acc = p_ref[0]
        for k in range(1, Pn):
            acc = acc + p_ref[k]
        o_ref[...] = acc

    return pl.pallas_call(functools.partial(body), name=name, out_shape=_sds((Rr, C), F32),
                          compiler_params=_params(None, _nbytes(parts.shape, F32), 4 << 20))(parts)


def _adamw(name, parts, w, m, v, rt):
    nseg = len(parts)
    Pn, rseg, C = parts[0].shape
    Rr = nseg * rseg
    tiles = rseg // rt
    assert rseg % rt == 0 and w.shape == (Rr, C)
    c1 = 1.0 - ADAM_B1 ** ADAM_STEP
    c2 = 1.0 - ADAM_B2 ** ADAM_STEP

    def body(*refs):
        p_refs = refs[:nseg]
        w_ref, m_ref, v_ref, g_ref, d_ref, nm_ref, nv_ref = refs[nseg:]
        i = pl.program_id(0)
        g = None
        for s, p_ref in enumerate(p_refs):
            gs = p_ref[0].astype(F32)
            for k in range(1, Pn):
                gs = gs + p_ref[k].astype(F32)
            g = gs if g is None else jnp.where(i >= s * tiles, gs, g)
        mn = ADAM_B1 * m_ref[...] + (1.0 - ADAM_B1) * g
        vn = ADAM_B2 * v_ref[...] + (1.0 - ADAM_B2) * jnp.square(g)
        g_ref[...] = g
        nm_ref[...] = mn
        nv_ref[...] = vn
        d_ref[...] = -ADAM_LR * ((mn / c1) / (jnp.sqrt(vn / c2) + ADAM_EPS) + ADAM_WD * w_ref[...])

    spec = pl.BlockSpec((rt, C), lambda i: (i, 0))
    pspec = lambda s: pl.BlockSpec((Pn, rt, C), lambda i: (0, jnp.clip(i - s * tiles, 0, tiles - 1), 0))
    bb = nseg * Pn * _nbytes((rt, C), parts[0].dtype) + 7 * _nbytes((rt, C), F32)
    return pl.pallas_call(
        functools.partial(body), name=name, grid=(Rr // rt,),
        in_specs=[pspec(s) for s in range(nseg)] + [spec, spec, spec],
        out_specs=[spec] * 4, out_shape=[_out((Rr, C), F32)] * 4,
        compiler_params=_params(("arbitrary",), bb, 4 << 20),
    )(*_hbm(list(parts) + [w, m, v]))


MESH = pl.DeviceIdType.MESH


def _my_pos():
    return lax.axis_index("x"), lax.axis_index("y"), lax.axis_index("c")


def _dev_index(x, y, c):
    return 4 * x + 2 * y + c


def _all_gather(name, shards):
    n = len(shards)

    def body(*refs):
        ins, outs = refs[:n], refs[n:2 * n]
        send_sems, recv_sems, local_sem = refs[2 * n:]
        x, y, c = _my_pos()
        me, sibling = (x, y, c), (x, y, 1 - c)
        chips = [(1 - x, y), (x, 1 - y), (1 - x, 1 - y)]

        def copy(t, k, block, to, src=None):
            slot = outs[t].at[_dev_index(*block)]
            return pltpu.make_async_remote_copy(
                src_ref=slot if src is None else src, dst_ref=slot,
                send_sem=send_sems.at[t, k], recv_sem=recv_sems.at[t, k], device_id=to, device_id_type=MESH)

        mine = [pltpu.make_async_copy(ins[t], outs[t].at[_dev_index(*me)], local_sem.at[t]) for t in range(n)]
        for cp in mine:
            cp.start()
        first = []
        for t in range(n):
            first.append(copy(t, 0, me, sibling, src=ins[t]))
            first += [copy(t, 1 + j, me, (*chip, c), src=ins[t]) for j, chip in enumerate(chips)]
        for cp in first:
            cp.start()
        passed = []
        for j, chip in enumerate(chips):
            for t in range(n):
                copy(t, 1 + j, (*chip, c), me).wait_recv()
                cp = copy(t, 4 + j, (*chip, c), sibling)
                cp.start()
                passed.append(cp)
        for t in range(n):
            copy(t, 0, sibling, me).wait_recv()
            for j, chip in enumerate(chips):
                copy(t, 4 + j, (*chip, 1 - c), me).wait_recv()
        for cp in first + passed:
            cp.wait_send()
        for cp in mine:
            cp.wait()

    return pl.pallas_call(
        functools.partial(body), name=name,
        in_specs=[ANY] * n, out_specs=[ANY] * n,
        out_shape=[_sds((NDEV,) + s.shape, s.dtype) for s in shards],
        scratch_shapes=[pltpu.SemaphoreType.DMA((n, 7)), pltpu.SemaphoreType.DMA((n, 7)), pltpu.SemaphoreType.DMA((n,))],
    )(*shards)


def _all_to_all(name, parts):
    n = len(parts)

    def body(*refs):
        ins, outs = refs[:n], refs[n:2 * n]
        send_sems, recv_sems, local_sem = refs[2 * n:]
        x, y, c = _my_pos()
        me = _dev_index(x, y, c)
        peers = [(x ^ ((k >> 2) & 1), y ^ ((k >> 1) & 1), c ^ (k & 1)) for k in range(1, NDEV)]
        mine = [pltpu.make_async_copy(ins[t].at[me], outs[t].at[me], local_sem.at[t]) for t in range(n)]
        for cp in mine:
            cp.start()
        sends = []
        for t in range(n):
            for k, peer in enumerate(peers):
                cp = pltpu.make_async_remote_copy(
                    src_ref=ins[t].at[_dev_index(*peer)], dst_ref=outs[t].at[me],
                    send_sem=send_sems.at[t, k], recv_sem=recv_sems.at[t, k], device_id=peer, device_id_type=MESH)
                cp.start()
                sends.append(cp)
        for t in range(n):
            for k, peer in enumerate(peers):
                slot = outs[t].at[_dev_index(*peer)]
                pltpu.make_async_remote_copy(
                    src_ref=slot, dst_ref=slot, send_sem=send_sems.at[t, k], recv_sem=recv_sems.at[t, k],
                    device_id=peer, device_id_type=MESH).wait_recv()
        for cp in sends:
            cp.wait_send()
        for cp in mine:
            cp.wait()

    return pl.pallas_call(
        functools.partial(body), name=name,
        in_specs=[ANY] * n, out_specs=[ANY] * n,
        out_shape=[_sds(p.shape, p.dtype) for p in parts],
        scratch_shapes=[pltpu.SemaphoreType.DMA((n, 7)), pltpu.SemaphoreType.DMA((n, 7)), pltpu.SemaphoreType.DMA((n,))],
    )(*parts)


SEM = pl.BlockSpec(memory_space=pltpu.SEMAPHORE)
IN_HBM = pl.BlockSpec(memory_space=pltpu.HBM)
DATAFLOW = pltpu.SideEffectType.DATAFLOW_SIDE_EFFECTING


def _flip_peers(x, y, c):
    return [(x ^ ((k >> 2) & 1), y ^ ((k >> 1) & 1), c ^ (k & 1)) for k in range(1, NDEV)]


def _split_copies(srcs, lands, send_sems, recv_sems, gather):
    x, y, c = _my_pos()
    me = _dev_index(x, y, c)
    out = []
    for t in range(len(srcs)):
        for k, peer in enumerate(_flip_peers(x, y, c)):
            p = _dev_index(*peer)
            src = srcs[t] if gather else srcs[t].at[p]
            sems = dict(send_sem=send_sems.at[7 * t + k], recv_sem=recv_sems.at[7 * t + k], device_id=peer, device_id_type=MESH)
            out.append((pltpu.make_async_remote_copy(src_ref=src, dst_ref=lands[t].at[me], **sems),
                        pltpu.make_async_remote_copy(src_ref=src, dst_ref=lands[t].at[p], **sems)))
    return out


def _exchange_start(name, collective_id, srcs, gather):
    n = len(srcs)
    lands = [lax.empty(((NDEV,) + s.shape) if gather else s.shape, s.dtype) for s in srcs]

    def body(*refs):
        src_refs, land_refs = refs[:n], refs[n:2 * n]
        send_sems, recv_sems = refs[2 * n], refs[2 * n + 1]
        token = refs[-1]
        barrier = pltpu.get_barrier_semaphore()
        for peer in _flip_peers(*_my_pos()):
            pl.semaphore_signal(barrier, inc=1, device_id=peer, device_id_type=MESH)
        pl.semaphore_wait(barrier, NDEV - 1)
        for send, _ in _split_copies(src_refs, land_refs, send_sems, recv_sems, gather):
            send.start()
        token[...] = jnp.zeros(token.shape, token.dtype)

    hbm = lambda a: pltpu.HBM(a.shape, a.dtype)
    res = pl.pallas_call(
        functools.partial(body), name=name,
        out_shape=[pltpu.SemaphoreType.DMA((7 * n,)), pltpu.SemaphoreType.DMA((7 * n,))] + [hbm(s) for s in srcs]
        + [hbm(a) for a in lands] + [_sds((8, HP), F32)],
        in_specs=[IN_HBM] * (2 * n), out_specs=[SEM, SEM] + [IN_HBM] * (2 * n) + [pl.BlockSpec(memory_space=pltpu.VMEM)],
        input_output_aliases={i: 2 + i for i in range(2 * n)},
        compiler_params=pltpu.CompilerParams(has_side_effects=DATAFLOW, collective_id=collective_id),
    )(*_hbm(list(srcs) + lands))
    return res[0], res[1], list(res[2:2 + n]), list(res[2 + n:2 + 2 * n]), res[-1]


def _exchange_wait(name, send_sems, recv_sems, srcs, lands, after, gather):
    n = len(srcs)

    def body(*refs):
        src_refs, land_refs = refs[:n], refs[n:2 * n]
        for _, recv in _split_copies(src_refs, land_refs, refs[2 * n], refs[2 * n + 1], gather):
            recv.wait_send()
            recv.wait_recv()

    hbm = lambda a: pltpu.HBM(a.shape, a.dtype)
    res = pl.pallas_call(
        functools.partial(body), name=name,
        out_shape=[hbm(s) for s in srcs] + [hbm(a) for a in lands],
        in_specs=[IN_HBM] * (2 * n) + [SEM, SEM, ANY], out_specs=[IN_HBM] * (2 * n),
        input_output_aliases={i: i for i in range(2 * n)},
        compiler_params=pltpu.CompilerParams(has_side_effects=DATAFLOW),
    )(*srcs, *lands, send_sems, recv_sems, after)
    return list(res[:n]), list(res[n:])


def _with_own(lands, own_blocks, me):
    out = []
    for land, own in zip(lands, own_blocks):
        out.append(lax.dynamic_update_slice(land, own[None], (me,) + (0,) * own.ndim))
    return out


def _arrange_w_in(w):
    z = lambda n: jnp.zeros((w.shape[0], n), w.dtype)
    return jnp.concatenate([w[:, 0:1280], w[:, 1292:1548], w[:, 1548:1804], w[:, 1836:2092],
                            z(64), w[:, 1804:1836], z(32), w[:, 1280:1292], z(HP - 2 * H)], axis=1)


def _unarrange_w_in(g):
    return jnp.concatenate([g[:, 0:1280], g[:, A_DT:A_DT + 2 * H], g[:, A_QA:A_KVA], g[:, A_KVA:A_POOL],
                            g[:, A_KR + NOPE:A_KR + NOPE + ROPE], g[:, A_POOL:A_KR]], axis=1)


def _pad_heads(w, width):
    k = w.shape[0]
    return jnp.pad(w.reshape(k, H, width), ((0, 0), (0, 0), (0, HP - width))).reshape(k, H * HP)


def _unpad_heads(g, width):
    k = g.shape[0]
    return g.reshape(k, H, HP)[:, :, :width].reshape(k, H * width)


def _arrange_w_out(w):
    att = jnp.pad(w[SSD_IN:2 * SSD_IN].reshape(H, VH, D), ((0, 0), (0, HP - VH), (0, 0))).reshape(QW, D)
    return jnp.concatenate([w[0:SSD_IN], att, w[2 * SSD_IN:]], axis=0)


def _unarrange_w_out(g):
    att = g[SSD_IN:SSD_IN + QW].reshape(H, HP, D)[:, :VH].reshape(SSD_IN, D)
    return jnp.concatenate([g[0:SSD_IN], att, g[SSD_IN + QW:]], axis=0)


def _rope_tables(T):
    n = T - CTX
    rows = n // GRID_W
    pairs = ROPE // 4
    inv = ROPE_THETA ** (-jnp.arange(pairs, dtype=F32) / pairs)
    ar = jnp.arange(rows, dtype=F32)[:, None] * inv
    ac = jnp.arange(GRID_W, dtype=F32)[:, None] * inv
    by_row = lambda a: jnp.repeat(a, GRID_W, axis=0)
    by_col = lambda a: jnp.tile(a, (rows, 1))
    cos = jnp.concatenate([by_row(jnp.cos(ar))] * 2 + [by_col(jnp.cos(ac))] * 2, axis=1)
    sin = jnp.concatenate([-by_row(jnp.sin(ar)), by_row(jnp.sin(ar)), -by_col(jnp.sin(ac)), by_col(jnp.sin(ac))], axis=1)
    ones, zeros = jnp.ones((n, NOPE), F32), jnp.zeros((n, NOPE), F32)
    cos = jnp.concatenate([ones, cos, ones[:, :HP - QK]], axis=1)
    sin = jnp.concatenate([zeros, sin, zeros[:, :HP - QK]], axis=1)
    return (jnp.concatenate([jnp.ones((CTX, HP), F32), cos], axis=0),
            jnp.concatenate([jnp.zeros((CTX, HP), F32), sin], axis=0))


def _lane_pad(v, n):
    return jnp.pad(v, (0, n - v.shape[0]))[None, :]


def _layer_weights(w_in, w_q_b, w_kv_b, conv_w, pool_w):
    kv = w_kv_b.reshape(QL, H, NOPE + VH)
    wbd = jnp.concatenate([jnp.pad(pool_w[g], ((0, 0), (64 * g, PD - 64 * (g + 1)))) for g in range(4)], axis=0)
    return dict(
        w_in=_arrange_w_in(w_in).astype(BF),
        wq=_pad_heads(w_q_b, QK).astype(BF),
        wk=_pad_heads(kv[:, :, :NOPE].reshape(QL, H * NOPE), NOPE).astype(BF),
        wv=_pad_heads(kv[:, :, NOPE:].reshape(QL, H * VH), VH).astype(BF),
        conv_w8=jnp.pad(conv_w, ((0, 4), (0, 0))), wbd=wbd)


def _layer_fwd(R, x, mod, lw, sp, late, cos, sin):
    B, T = R.B, R.T
    z, xbc_raw, qa, kva, pool_in, kr, dt_raw, h1 = _in_proj(R, x, mod, sp["nw1"], lw["w_in"])
    xbc, dt2 = _ssd_prep(R, xbc_raw, dt_raw, lw["conv_w8"], sp["conv_b"], sp["dtb"])
    yf, yb, hin_f, hin_b = _ssd_scan(B, T, xbc, dt2, sp["alog2"])
    qt, kt, q_t, k_t, v_t, cq, ckv = _mla_prep(R, qa, kva, kr, sp["qnw"], sp["kvnw"], lw["wq"], lw["wk"], lw["wv"], cos, sin)
    o, lse = _flash_fwd(B, T, q_t, kt, v_t)
    w_out, w1, w2 = late(o)
    w_out = _arrange_w_out(w_out).astype(BF)
    xmid, cat, mixbf = _out_proj(R, x, mod, yf, yb, xbc, z, o, pool_in, lw["wbd"], sp["pscale"], sp["dsk"], sp["snw"], w_out)
    xo, h2, ubf, ybf = _mlp_fwd(R, xmid, mod, sp["nw2"], w1, w2)
    saved = dict(x=x, z=z, xbc_raw=xbc_raw, qa=qa, kva=kva, pool_in=pool_in, dt_raw=dt_raw, h1=h1, xbc=xbc, dt2=dt2,
                 yf=yf, yb=yb, hin_f=hin_f, hin_b=hin_b, qt=qt, kt=kt, q_t=q_t, k_t=k_t, v_t=v_t, cq=cq, ckv=ckv, o=o, lse=lse,
                 cat=cat, mixbf=mixbf, xmid=xmid, h2=h2, ubf=ubf, ybf=ybf, w_out=w_out, w1=w1, w2=w2)
    return xo, saved


def _layer_bwd(R, dxo, sv, mod, lw, sp, cos, sin, on_mlp=None, on_mid=None, latent_only=False):
    B, T = R.B, R.T
    dxm, du, abf, dyb, dmod_a, dnw2 = _mlp_bwd(R, dxo, sv["xmid"], sv["ubf"], sv["ybf"], mod, sp["nw2"], sv["w1"], sv["w2"])
    g_w1 = _tn_matmul("dw_mlp1", sv["h2"], du, 4 * FSH, sub=4)
    g_w2 = _tn_matmul("dw_mlp2", abf, dyb, D)[0].reshape(NDEV, FSH, D)
    snw = sp["snw"]
    tok = on_mlp(g_w1, g_w2) if on_mlp is not None else None
    if tok is not None:
        snw = snw + tok
    dmix, dz, dyt, dxsk, qs, dos, do_t, dpo, dmod_b, dvec_o = _out_bwd(R, dxm, mod, sv["mixbf"], sv["yf"], sv["yb"], sv["xbc"], sv["z"],
                                                                 sv["o"], sv["qt"], sv["lse"], sp["dsk"], snw, sv["w_out"])
    g_wout = _unarrange_w_out(_tn_matmul("dw_out", sv["cat"], dmix, D)[0])
    dpool_in, g_wbd, dpsc = _pool_bwd(R, dpo, sv["pool_in"], lw["wbd"], sp["pscale"])
    dqt, dkt, dvt = _flash_bwd(B, T, qs, sv["kt"], dos, sv["q_t"], sv["k_t"], sv["v_t"], do_t)
    dqp, dkvb, dqa, dkva, dkr, dnw_qk = _mla_bwd(R, dqt, dkt, dvt, sv["qa"], sv["kva"], sp["qnw"], sp["kvnw"],
                                                 lw["wq"], lw["wk"], lw["wv"], cos, sin)
    g_wq = _unpad_heads(_tn_matmul("dw_q", sv["cq"], dqp, QW)[0], QK)
    g_kv = _tn_matmul("dw_kv", sv["ckv"], dkvb, 2 * QW)[0]
    g_wk = _unpad_heads(g_kv[:, :QW], NOPE).reshape(QL, H, NOPE)
    g_wv = _unpad_heads(g_kv[:, QW:], VH).reshape(QL, H, VH)
    g_wkv = jnp.concatenate([g_wk, g_wv], axis=2).reshape(QL, H * (NOPE + VH))
    alog2 = sp["alog2"]
    tok = on_mid(g_wout, g_wq, g_wkv) if on_mid is not None else None
    if tok is not None:
        alog2 = alog2 + tok
    dxbc_f, dxbc_b, ddt_f, ddt_b, da = _ssd_scan_bwd(B, T, dyt, sv["xbc"], sv["dt2"], alog2, sv["hin_f"], sv["hin_b"])
    dpre, ddtr, dcw, dcb, ddtb = _ssd_prep_bwd(R, dxbc_f, dxbc_b, dxsk, ddt_f, ddt_b, sv["xbc_raw"], sv["dt_raw"], lw["conv_w8"],
                                                sp["conv_b"], sp["dtb"])
    dx, dproj, dmod_c, dnw1 = _in_bwd(R, dxm, sv["x"], mod, sp["nw1"], dz, dpre, dqa, dkva, dpool_in, dkr, ddtr,
                                      lw["conv_w8"], lw["w_in"], latent_only)
    g_win = _unarrange_w_in(jnp.concatenate(list(_tn_matmul("dw_in", sv["h1"], dproj, PC)), axis=1))
    a2 = -jnp.exp(sp["alog2"][:, 0, :H])
    small = dict(
        norm1_w=dnw1[0], norm2_w=dnw2[0], conv_w=dcw[0:4], conv_b=dcb[0], dt_bias=ddtb[0, :2 * H].reshape(2, H),
        a_log=jnp.sum(da[:, :, 0, :H], axis=0) * a2, ssd_d=jnp.sum(dvec_o[1].reshape(H, P), axis=1), ssd_norm_w=dvec_o[0],
        q_a_norm_w=dnw_qk[0], kv_a_norm_w=dnw_qk[1],
        pool_w=jnp.stack([g_wbd[64 * g:64 * (g + 1), 64 * g:64 * (g + 1)] for g in range(4)]), pool_scale=dpsc[0])
    big = dict(w_in=g_win, w_out=g_wout, w_q_b=g_wq, w_kv_b=g_wkv, w_mlp1=g_w1, w_mlp2=g_w2)
    return dx, big, small, dmod_a + dmod_b + dmod_c


def _small_params(l, norm1_w, norm2_w, conv_b, dt_bias, a_log, ssd_d, ssd_norm_w, q_a_norm_w, kv_a_norm_w, pool_scale):
    alog2 = jnp.broadcast_to(jnp.pad(a_log[l], ((0, 0), (0, HP - H)))[:, None, :], (2, 8, HP))
    return dict(nw1=norm1_w[l][None], nw2=norm2_w[l][None], conv_b=conv_b[l][None],
                dtb=_lane_pad(dt_bias[l].reshape(2 * H), HP), alog2=alog2,
                dsk=jnp.repeat(ssd_d[l], P)[None], snw=ssd_norm_w[l][None], qnw=q_a_norm_w[l][None],
                kvnw=kv_a_norm_w[l][None], pscale=pool_scale[l][None])


SMALL_NAMES = ["mod_b", "norm1_w", "norm2_w", "conv_b", "dt_bias", "a_log", "ssd_d", "ssd_norm_w", "q_a_norm_w",
               "kv_a_norm_w", "pool_w", "pool_scale", "final_norm_w"]


def _pack(arrs):
    rows = []
    for a in arrs:
        f = a.reshape(-1).astype(F32)
        n = -(-f.shape[0] // HP) * HP
        rows.append(jnp.pad(f, (0, n - f.shape[0])).reshape(-1, HP))
    out = jnp.concatenate(rows, axis=0)
    pad = (-out.shape[0]) % 8
    return jnp.pad(out, ((0, pad), (0, 0)))


def _unpack(pack, like):
    outs, r = [], 0
    for a in like:
        n = math.prod(a.shape)
        nr = -(-n // HP)
        outs.append(pack[r:r + nr].reshape(-1)[:n].reshape(a.shape))
        r += nr
    return outs


def _local_step(x, ctx, target, mods, full_of, small_w, on_grads=None, on_mlp=None, on_mid=None):
    B, N = x.shape[0], x.shape[1]
    T = CTX + N
    R = _Rows(B, T)
    cos, sin = _rope_tables(T)
    xu = (ctx, x)
    L = len(mods)
    lws, sps, saves = [], [], []
    for l in range(L):
        f = full_of(l, xu)
        lws.append(_layer_weights(f["w_in"], f["w_q_b"], f["w_kv_b"], f["conv_w"], small_w["pool_w"][l]))
        sps.append(_small_params(l, *[small_w[k] for k in ["norm1_w", "norm2_w", "conv_b", "dt_bias", "a_log", "ssd_d",
                                                          "ssd_norm_w", "q_a_norm_w", "kv_a_norm_w", "pool_scale"]]))
        xu, sv = _layer_fwd(R, xu, mods[l], lws[l], sps[l], f["late"], cos, sin)
        saves.append(sv)
    dx, loss8, dfnw = _loss_head(R, xu, target, small_w["final_norm_w"][None])
    bigs, smalls, dmods = [None] * L, [None] * L, [None] * L
    for l in reversed(range(L)):
        hook = functools.partial(on_mlp, l) if on_mlp is not None else None
        hook2 = functools.partial(on_mid, l) if on_mid is not None else None
        dx, bigs[l], smalls[l], dmods[l] = _layer_bwd(R, dx, saves[l], mods[l], lws[l], sps[l], cos, sin, hook, hook2, l == 0)
        if on_grads is not None:
            bigs[l], tok = on_grads(l, bigs[l], dx)
            if tok is not None:
                sps[l - 1] = dict(sps[l - 1], nw2=sps[l - 1]["nw2"] + tok)
    return loss8[0, 0], dx, bigs, smalls, dfnw[0], dmods


def kernel(x, c, ctx, c_ctx, mod_w, mod_b, norm1_w, norm2_w, w_in, conv_w, conv_b, dt_bias, a_log, ssd_d, ssd_norm_w, q_a_norm_w, w_q_b, kv_a_norm_w, w_kv_b, pool_w, pool_scale, w_out, w_mlp1, w_mlp2, final_norm_w, loss_target, m_c_ctx, m_mod_w, m_mod_b, m_norm1_w, m_norm2_w, m_w_in, m_conv_w, m_conv_b, m_dt_bias, m_a_log, m_ssd_d, m_ssd_norm_w, m_q_a_norm_w, m_w_q_b, m_kv_a_norm_w, m_w_kv_b, m_pool_w, m_pool_scale, m_w_out, m_w_mlp1, m_w_mlp2, m_final_norm_w, v_c_ctx, v_mod_w, v_mod_b, v_norm1_w, v_norm2_w, v_w_in, v_conv_w, v_conv_b, v_dt_bias, v_a_log, v_ssd_d, v_ssd_norm_w, v_q_a_norm_w, v_w_q_b, v_kv_a_norm_w, v_w_kv_b, v_pool_w, v_pool_scale, v_w_out, v_w_mlp1, v_w_mlp2, v_final_norm_w):
    args = dict(locals())
    B = x.shape[0]
    L = mod_w.shape[0]
    me = _dev_index(*_my_pos())
    CS = mod_w.shape[2]

    big_names = ["w_in", "w_out", "w_q_b", "w_kv_b", "w_mlp1", "w_mlp2"]
    shards = {n: args[n].astype(BF) for n in big_names}
    early, late_names = ["w_in", "w_q_b", "w_kv_b"], ["w_out", "w_mlp1", "w_mlp2"]
    g0 = _all_gather("gather_weights", [c, conv_w] + [shards[n][0] for n in early])
    c_all, convw_all = g0[0], g0[1]
    gathered = {0: dict(zip(early, g0[2:]))}
    cs = jnp.concatenate([c_all.reshape(NDEV * B, D), c_ctx[None], jnp.zeros((24 - NDEV * B - 1, D), F32)], axis=0)
    m_loc = _adaln_fwd(cs, mod_w)
    m_all = _all_gather("gather_mod", [m_loc])[0]
    m_full = jnp.moveaxis(m_all, 0, 2).reshape(L, 24, NDEV * CS) + mod_b[:, None, :]
    pending = {}
    tok = jnp.zeros((), F32)
    res = _exchange_start("gather_start_0", 1, [shards[n][0] for n in late_names], gather=True)
    pending[0] = res[:4]
    tok = tok + res[4][0, 0]
    for l in range(1, L):
        res = _exchange_start("gather_start_%d" % l, 1 + l, [shards[n][l] for n in early + late_names], gather=True)
        pending[l] = res[:4]
        tok = tok + res[4][0, 0]
    mods = []
    for l in range(L):
        ex = lax.dynamic_slice(m_full[l], (me * B, 0), (B, 6 * D)).reshape(B, 6, D)
        cc = jnp.broadcast_to(m_full[l, NDEV * B].reshape(1, 6, D), (B, 6, D))
        mods.append(jnp.pad(jnp.stack([cc, ex], axis=1), ((0, 0), (0, 0), (0, 2), (0, 0))) + tok)

    def full_of(l, xu):
        if l > 0:
            own, lands = _exchange_wait("gather_wait_%d" % l, *pending.pop(l), xu, gather=True)
            gathered[l] = dict(zip(early + late_names, _with_own(lands, own, me)))

        def late(after):
            if l == 0:
                own, lands = _exchange_wait("gather_wait_0", *pending.pop(0), after, gather=True)
                gathered[0].update(zip(late_names, _with_own(lands, own, me)))
            gl = gathered[l]
            return gl["w_out"].reshape(D, D), gl["w_mlp1"], gl["w_mlp2"]

        g = gathered[l]
        return dict(
            w_in=g["w_in"].reshape(D, IN_COLS),
            w_q_b=jnp.moveaxis(g["w_q_b"], 0, 1).reshape(QL, H * QK),
            w_kv_b=jnp.moveaxis(g["w_kv_b"], 0, 1).reshape(QL, H * (NOPE + VH)),
            conv_w=jnp.moveaxis(convw_all[:, l], 0, 1).reshape(4, XBC), late=late)

    def grad_blocks(big, names):
        make = dict(
            w_in=lambda g: g.reshape(NDEV, D // NDEV, IN_COLS), w_out=lambda g: g.reshape(NDEV, D // NDEV, D),
            w_q_b=lambda g: jnp.moveaxis(g.reshape(QL, NDEV, -1), 1, 0),
            w_kv_b=lambda g: jnp.moveaxis(g.reshape(QL, NDEV, -1), 1, 0), w_mlp1=lambda g: g, w_mlp2=lambda g: g)
        return [make[n](big[n]).astype(BF) for n in names]

    sent, sent_mlp, sent_mid = {}, {}, {}
    rest_names, mid_names = ["w_in"], ["w_out", "w_q_b", "w_kv_b"]

    def on_mid(l, g_wout, g_wq, g_wkv):
        if l > 0:
            return None
        blocks_mid = grad_blocks(dict(w_out=g_wout, w_q_b=g_wq, w_kv_b=g_wkv), mid_names)
        res = _exchange_start("grads_mid_start_0", 2 + 2 * L, blocks_mid, gather=False)
        sent_mid[l] = res[:4]
        return res[4][0, 0]

    def on_mlp(l, g_w1, g_w2):
        if l > 0:
            return None
        res = _exchange_start("grads_mlp_start_0", 1 + 2 * L, [g_w1.astype(BF), g_w2.astype(BF)], gather=False)
        sent_mlp[l] = res[:4]
        return res[4][0, 0]

    def on_grads(l, big, dx):
        if l == 0:
            return grad_blocks(big, rest_names), None
        res = _exchange_start("grads_start_%d" % l, 1 + L + l, grad_blocks(big, big_names), gather=False)
        sent[l] = res[:4]
        return None, res[4][0, 0]

    small_w = {k: args[k] for k in SMALL_NAMES if k != "mod_b"}
    loss_part, grad_x, blocks, smalls, dfnw, dmods = _local_step(x, ctx, loss_target, mods, full_of, small_w, on_grads, on_mlp,
                                                                 on_mid)
    loss = lax.psum(loss_part, ("x", "y", "c"))

    dm_ex = jnp.stack([dmods[l][:, 1, :6].reshape(B, 6 * D) for l in range(L)])
    dm_cc = jnp.stack([jnp.sum(dmods[l][:, 0, :6], axis=0).reshape(6 * D) for l in range(L)])
    small_parts = dict(
        mod_b=jnp.sum(dm_ex, axis=1) + dm_cc,
        **{k: jnp.stack([smalls[l][k] for l in range(L)]) for k in SMALL_NAMES[1:-1]},
        final_norm_w=dfnw, conv_w=jnp.stack([smalls[l]["conv_w"] for l in range(L)]), dm_cc=dm_cc)
    adam_grads = [small_parts[k] for k in SMALL_NAMES]
    extras = [small_parts["conv_w"], dm_cc, dm_ex]
    pack = jnp.concatenate([_pack(adam_grads), _pack(extras)], axis=0)
    pack_all = _all_gather("gather_small_grads", [pack])[0]
    wpack = _pack([args[k] for k in SMALL_NAMES])
    mpack = _pack([args["m_" + k] for k in SMALL_NAMES])
    vpack = _pack([args["v_" + k] for k in SMALL_NAMES])
    n_adam = wpack.shape[0]
    res_small = _adamw("adamw_small", [pack_all[:, :n_adam]], wpack, mpack, vpack, n_adam)
    small_out = [_unpack(r, [args[k] for k in SMALL_NAMES]) for r in res_small]
    ext_all = pack_all[:, n_adam:]
    g_conv_full, dm_cc_tot, _ = _unpack(_sum_blocks("sum_small_grads", ext_all), extras)
    dm_ex_all = jnp.stack([_unpack(ext_all[k], extras)[2] for k in range(NDEV)], axis=1)
    dm_rows = jnp.concatenate([dm_ex_all.reshape(L, NDEV * B, 6 * D), dm_cc_tot[:, None, :],
                               jnp.zeros((L, 24 - NDEV * B - 1, 6 * D), F32)], axis=1)
    dm_loc = lax.dynamic_slice(dm_rows, (0, 0, me * CS), (L, 24, CS))
    g_modw, gc_part = _adaln_bwd(cs, dm_loc, mod_w)
    gc_all = _all_gather("gather_cctx_grad", [gc_part[NDEV * B:NDEV * B + 8]])[0]
    cpad = lambda a: jnp.pad(a[None], ((0, 7), (0, 0)))
    res_cc = _adamw("adamw_cctx", [gc_all], cpad(c_ctx), cpad(m_c_ctx), cpad(v_c_ctx), 8)
    cc_out = [r[0] for r in res_cc]

    def waited(name, handles, names):
        srcs, lands = _exchange_wait(name, *handles, grad_x, gather=False)
        own = [lax.dynamic_index_in_dim(s, me, 0, keepdims=False) for s in srcs]
        return dict(zip(names, _with_own(lands, own, me)))

    recv = {0: dict(zip(rest_names, _all_to_all("exchange_grads", blocks[0])))}
    recv[0].update(waited("grads_mlp_wait_0", sent_mlp.pop(0), ["w_mlp1", "w_mlp2"]))
    recv[0].update(waited("grads_mid_wait_0", sent_mid.pop(0), mid_names))
    for l in range(1, L):
        recv[l] = waited("grads_wait_%d" % l, sent.pop(l), big_names)
    big_out = {}
    for name in big_names:
        w = args[name]
        Rr, C = math.prod(w.shape[:-1]), w.shape[-1]
        rl = Rr // L
        rt = rl if rl * C <= (1 << 17) else rl // 4
        res = _adamw("adamw_" + name, [recv[l][name].reshape(NDEV, rl, C) for l in range(L)], w.reshape(Rr, C),
                     args["m_" + name].reshape(Rr, C), args["v_" + name].reshape(Rr, C), rt)
        big_out[name] = [r.reshape(w.shape) for r in res]
    res = _adamw("adamw_mod_w", [g_modw.reshape(1, L * D, CS)], mod_w.reshape(L * D, CS), m_mod_w.reshape(L * D, CS),
                 v_mod_w.reshape(L * D, CS), L * D // 8)
    big_out["mod_w"] = [r.reshape(mod_w.shape) for r in res]
    CW = conv_w.shape[2]
    g_conv = lax.dynamic_slice(g_conv_full, (0, 0, me * CW), (L, 4, CW))
    res = _adamw("adamw_conv_w", [g_conv.reshape(1, L * 4, CW)], conv_w.reshape(L * 4, CW), m_conv_w.reshape(L * 4, CW),
                 v_conv_w.reshape(L * 4, CW), L * 4)
    big_out["conv_w"] = [r.reshape(conv_w.shape) for r in res]

    weights = ["c_ctx", "mod_w", "mod_b", "norm1_w", "norm2_w", "w_in", "conv_w", "conv_b", "dt_bias", "a_log", "ssd_d",
               "ssd_norm_w", "q_a_norm_w", "w_q_b", "kv_a_norm_w", "w_kv_b", "pool_w", "pool_scale", "w_out", "w_mlp1",
               "w_mlp2", "final_norm_w"]
    outs = [loss, grad_x]
    for kind in range(4):
        for name in weights:
            if name == "c_ctx":
                outs.append(cc_out[kind])
            elif name in big_out:
                outs.append(big_out[name][kind])
            else:
                outs.append(small_out[kind][SMALL_NAMES.index(name)])
    return tuple(outs)
```

```python
import functools
import math

import jax
import jax.numpy as jnp
from jax import lax
from jax.experimental import pallas as pl
from jax.experimental.pallas import tpu as pltpu

F32 = jnp.float32
BF = jnp.bfloat16
MXU = BF

D = 1024
CTX = 256
GRID_W = 64
EPS = 1e-6
H = 6
P = 64
SSD_IN = 384
NST = 128
XBC = 896
CH = 128
QL = 256
NOPE = 64
ROPE = 32
VH = 64
QK = 96
HP = 128
QW = H * HP
PD = 256
FF = 4096
IN_COLS = 2092
ROPE_THETA = 10000.0
PC = 2304
A_Z, A_XBC, A_QA, A_KVA, A_POOL, A_KR, A_DT = 0, 384, 1280, 1536, 1792, 2048, 2176
MIXW = SSD_IN + QW + PD
NDEV = 8
FSH = FF // NDEV
TM = 256
TQ = 256
TK = 256
VMEM_CAP = 64 * 1024 * 1024
ADAM_LR, ADAM_B1, ADAM_B2, ADAM_EPS, ADAM_WD, ADAM_STEP = 0.001, 0.9, 0.999, 1e-08, 0.01, 10


def _nbytes(shape, dtype):
    n = 1
    for s in shape:
        if s is not None:
            n *= s
    return n * jnp.dtype(dtype).itemsize


def _params(sem, block_bytes, extra=0):
    lim = min(2 * block_bytes + extra + (8 << 20), VMEM_CAP - (6 << 20))
    return pltpu.CompilerParams(dimension_semantics=sem, vmem_limit_bytes=int(lim))


def _hbm(arrays):
    return [pltpu.with_memory_space_constraint(a, pltpu.HBM) for a in arrays]


def _dot(a, b):
    return jnp.dot(a.astype(MXU), b.astype(MXU), preferred_element_type=F32)


def _dot_nt(a, b):
    return lax.dot_general(a.astype(MXU), b.astype(MXU), (((1,), (1,)), ((), ())), preferred_element_type=F32)


def _dot_tn(a, b):
    return lax.dot_general(a.astype(MXU), b.astype(MXU), (((0,), (0,)), ((), ())), preferred_element_type=F32)


def _dot01(m01, x):
    b16 = jnp.bfloat16
    m = m01.astype(b16)
    hi = x.astype(b16)
    r1 = x - hi.astype(F32)
    mid = r1.astype(b16)
    lo = (r1 - mid.astype(F32)).astype(b16)
    f = lambda v: jnp.dot(m, v, preferred_element_type=F32)
    return f(hi) + f(mid) + f(lo)


def _sigmoid(x):
    return 1.0 / (1.0 + jnp.exp(-x))


def _silu(x):
    return x * _sigmoid(x)


def _dsilu(x):
    s = _sigmoid(x)
    return s * (1.0 + x * (1.0 - s))


def _iota(shape, dim):
    return lax.broadcasted_iota(jnp.int32, shape, dim)


def _row(ref, k):
    blk = ref[...]
    return jnp.sum(jnp.where(_iota(blk.shape, 0) == k, blk, 0.0), axis=0, keepdims=True)


def _shift_rows(x, k):
    n = x.shape[0]
    return pltpu.roll(x, (-k) % n, axis=0)


class _Rows:
    def __init__(self, B, T):
        self.B, self.T = B, T
        self.nt = T // TM
        self.nct = CTX // TM

    def row(self, F):
        return pl.BlockSpec((None, TM, F), lambda b, i: (b, i, 0))

    def row2(self, F):
        return pl.BlockSpec((None, 2, TM, F), lambda b, i: (b, 0, i, 0))

    def prev8(self, F):
        return pl.BlockSpec((None, 8, F), lambda b, i: (b, jnp.maximum(i * (TM // 8) - 1, 0), 0))

    def next8(self, F):
        last = self.T // 8 - 1
        return pl.BlockSpec((None, 8, F), lambda b, i: (b, jnp.minimum((i + 1) * (TM // 8), last), 0))

    def mod(self):
        nct = self.nct
        return pl.BlockSpec((None, None, 8, D), lambda b, i: (b, jnp.where(i < nct, 0, 1), 0, 0))

    def const(self, shape):
        z = (0,) * len(shape)
        return pl.BlockSpec(tuple(shape), lambda b, i: z)

    def tgt(self, F):
        nct = self.nct
        return pl.BlockSpec((None, TM, F), lambda b, i: (b, jnp.maximum(i - nct, 0), 0))

    def xin(self, x):
        if isinstance(x, tuple):
            nct = self.nct
            return [(x[0], pl.BlockSpec((None, TM, D), lambda b, i: (b, jnp.minimum(i, nct - 1), 0))), (x[1], self.tgt(D))]
        return [(x, self.row(D))]

    def xval(self, i, refs):
        if len(refs) == 2:
            return jnp.where(i < self.nct, refs[0][...], refs[1][...])
        return refs[0][...]

    def call(self, name, body, ins, outs, scratch=(), extra=0):
        arrays = [a for a, _ in ins]
        in_specs = [s for _, s in ins]
        out_shape = [pltpu.HBM(o.shape, o.dtype) for o, _ in outs]
        out_specs = [s for _, s in outs]
        bb = 0
        for a, s in list(ins) + list(outs):
            if s.block_shape is not None:
                bb += _nbytes(s.block_shape, a.dtype)
        return pl.pallas_call(
            functools.partial(body), name=name, grid=(self.B, self.nt),
            in_specs=in_specs, out_specs=out_specs, out_shape=out_shape, scratch_shapes=list(scratch),
            compiler_params=_params(("arbitrary", "arbitrary"), bb, extra),
        )(*_hbm(arrays))

    def first_of_stream(self, i):
        return jnp.logical_or(i == 0, i == self.nct)

    def last_of_stream(self, i):
        return jnp.logical_or(i == self.nct - 1, i == self.nt - 1)

    def ext(self, i, prev_ref, cur, next_ref):
        pv = prev_ref[...].astype(F32) * jnp.where(self.first_of_stream(i), 0.0, 1.0)
        nx = next_ref[...].astype(F32) * jnp.where(self.last_of_stream(i), 0.0, 1.0)
        return jnp.concatenate([pv, cur, nx], axis=0)

    def stream_pos(self, i, rows):
        start = jnp.where(i < self.nct, 0, CTX)
        n = jnp.where(i < self.nct, CTX, self.T - CTX)
        t = i * TM - 8 - start + _iota((rows, 1), 0)
        return t, n


def _sds(shape, dtype):
    return jax.ShapeDtypeStruct(tuple(shape), dtype)


def _out(shape, dtype):
    return pltpu.HBM(tuple(shape), dtype)


def _norm_mod(x, nw, sh, sc):
    r = lax.rsqrt(jnp.mean(x * x, axis=-1, keepdims=True) + EPS)
    xn = x * r
    return xn * nw * (1.0 + sc) + sh, xn, r


def _norm_mod_bwd(dh, xn, r, nw, sc):
    dsh = jnp.sum(dh, axis=0, keepdims=True)
    dsc = jnp.sum(dh * (xn * nw), axis=0, keepdims=True)
    dnw = jnp.sum(dh * (1.0 + sc) * xn, axis=0, keepdims=True)
    dxn = dh * nw * (1.0 + sc)
    dx = r * (dxn - xn * jnp.mean(dxn * xn, axis=-1, keepdims=True))
    return dx, dsh, dsc, dnw


def _acc_rows(ref, first, rows):
    rid = _iota(ref.shape, 0)
    upd = jnp.zeros(ref.shape, F32)
    for k, v in rows.items():
        upd = upd + jnp.where(rid == k, v, 0.0)

    @pl.when(first)
    def _():
        ref[...] = upd

    @pl.when(jnp.logical_not(first))
    def _():
        ref[...] += upd


def _in_proj(R, x, mod, nw1, w_arr):
    B, T = R.B, R.T

    xin = R.xin(x)

    def body(*refs):
        x_refs = refs[:len(xin)]
        mod_ref, nw_ref, w_ref, z_ref, xbc_ref, qa_ref, kva_ref, pool_ref, kr_ref, dt_ref, h_ref = refs[len(xin):]
        h, _, _ = _norm_mod(R.xval(pl.program_id(1), x_refs), nw_ref[...], _row(mod_ref, 0), _row(mod_ref, 1))
        hb = h.astype(BF)
        h_ref[...] = hb
        p = jnp.dot(hb, w_ref[...], preferred_element_type=F32)
        z_ref[...] = p[:, A_Z:A_XBC]
        xbc_ref[...] = p[:, A_XBC:A_QA]
        qa_ref[...] = p[:, A_QA:A_KVA]
        kva_ref[...] = p[:, A_KVA:A_POOL]
        pool_ref[...] = p[:, A_POOL:A_KR]
        kr_ref[...] = p[:, A_KR:A_DT]
        dt_ref[...] = p[:, A_DT:PC]

    widths = [SSD_IN, XBC, QL, QL, PD, HP, HP]
    outs = [(_sds((B, T, w), F32), R.row(w)) for w in widths] + [(_sds((B, T, D), BF), R.row(D))]
    return R.call("in_proj", body,
                  xin + [(mod, R.mod()), (nw1, R.const((1, D))), (w_arr, R.const((D, PC)))],
                  outs, extra=8 << 20)


def _conv_pre(ext, w_ref, b_ref):
    return (_row(w_ref, 0) * _shift_rows(ext, -1)[8:8 + TM] + _row(w_ref, 1) * ext[8:8 + TM]
            + _row(w_ref, 2) * _shift_rows(ext, 1)[8:8 + TM] + _row(w_ref, 3) * _shift_rows(ext, 2)[8:8 + TM]
            + b_ref[...])


def _softplus(x):
    return jnp.maximum(x, 0.0) + jnp.log(1.0 + jnp.exp(-jnp.abs(x)))


def _ssd_prep(R, xbc_raw, dt_raw, conv_w8, conv_b, dtb):
    B, T = R.B, R.T

    def body(raw_ref, pv_ref, nx_ref, dtr_ref, w_ref, b_ref, dtb_ref, xbc_ref, dt_ref):
        i = pl.program_id(1)
        ext = R.ext(i, pv_ref, raw_ref[...], nx_ref)
        xbc_ref[...] = _silu(_conv_pre(ext, w_ref, b_ref))
        lane = _iota((TM, HP), 1)
        dtv = _softplus(dtr_ref[...] + dtb_ref[...])
        keep = lane < H
        dt_ref[0] = jnp.where(keep, dtv, 0.0)
        dt_ref[1] = jnp.where(keep, pltpu.roll(dtv, HP - H, axis=1), 0.0)

    return R.call("ssd_prep", body,
                  [(xbc_raw, R.row(XBC)), (xbc_raw, R.prev8(XBC)), (xbc_raw, R.next8(XBC)), (dt_raw, R.row(HP)),
                   (conv_w8, R.const((8, XBC))), (conv_b, R.const((1, XBC))), (dtb, R.const((1, HP)))],
                  [(_sds((B, T, XBC), F32), R.row(XBC)), (_sds((B, 2, T, HP), F32), R.row2(HP))], extra=12 << 20)


def _chunk_index(d, s, nc, nctc):
    if d == 0:
        return s
    return jnp.where(s < nctc, nctc - 1 - s, nc - 1 - (s - nctc))


def _dot01_r(x, m01, passes=3):
    b16 = jnp.bfloat16
    m = m01.astype(b16)
    out, rest = None, x
    for _ in range(passes):
        part = rest.astype(b16)
        rest = rest - part.astype(F32)
        term = jnp.dot(part, m, preferred_element_type=F32)
        out = term if out is None else out + term
    return out


def _scan_common(d, dtv, alog_ref):
    sel = _iota((HP, H * HP), 0) == (_iota((HP, H * HP), 1) >> 7)
    a_all = _dot01_r(-jnp.exp(alog_ref[d]), sel)
    dt_all = _dot01_r(dtv, sel)
    adt_all = dt_all * jnp.concatenate([a_all] * (CH // 8), axis=0)
    row = _iota((CH, CH), 0)
    col = _iota((CH, CH), 1)
    inc = col <= row if d == 0 else col >= row
    inc_t = row <= col if d == 0 else row >= col
    q_all = _dot01(inc, adt_all)
    return a_all, dt_all, adt_all, q_all, inc, inc_t


def _head_decay(h, adt_all, q_all, inc, inc_t):
    q = q_all[:, HP * h:HP * (h + 1)]
    q_t = q.T
    qtot = jnp.sum(adt_all[:, HP * h:HP * (h + 1)], axis=0, keepdims=True)
    lm = jnp.where(inc, jnp.exp(q - q_t), 0.0)
    lm_t = jnp.where(inc_t, jnp.exp(q_t - q), 0.0)
    return lm, lm_t, jnp.exp(q), jnp.exp(qtot), jnp.exp(qtot - q)


def _ssd_scan(B, T, xbc, dt2, alog2):
    nc, nctc = T // CH, CTX // CH

    def chain(d, xbc_ref, dt_ref, alog_ref, y_ref, hin_ref, hs):
        xbc_v = xbc_ref[...]
        dtv = dt_ref[...]
        blk = alog_ref[d]
        adt = dtv * -jnp.exp(jnp.sum(jnp.where(_iota(blk.shape, 0) == 0, blk, 0.0), axis=0, keepdims=True))
        row, col = _iota((CH, CH), 0), _iota((CH, CH), 1)
        inc = col <= row if d == 0 else col >= row
        inc_tf = (row <= col if d == 0 else row >= col).astype(F32)
        q = _dot01(inc, adt)
        hin_ref[...] = hs[d]
        for g in range(2):
            bg = xbc_v[:, SSD_IN + NST * g:SSD_IN + NST * (g + 1)]
            cg = xbc_v[:, SSD_IN + 2 * NST + NST * g:SSD_IN + 2 * NST + NST * (g + 1)]
            gm = _dot_nt(cg, bg)
            for r in range(3):
                h = 3 * g + r
                onehot = (_iota((1, HP), 1) == h).astype(F32)
                adt_h = jnp.sum(adt * onehot, axis=1, keepdims=True)
                qc = jnp.sum(q * onehot, axis=1, keepdims=True)
                dt_h = jnp.sum(dtv * onehot, axis=1, keepdims=True)
                qr = jnp.sum(adt_h * inc_tf, axis=0, keepdims=True)
                qtot = jnp.sum(adt_h, axis=0, keepdims=True)
                lm = jnp.where(inc, jnp.exp(qc - qr), 0.0)
                xh = xbc_v[:, P * h:P * (h + 1)] * dt_h
                hprev = hs[d, P * h:P * (h + 1), :]
                y_ref[:, P * h:P * (h + 1)] = _dot(gm * lm, xh) + jnp.exp(qc) * _dot_nt(cg, hprev)
                hs[d, P * h:P * (h + 1), :] = jnp.exp(qtot) * hprev + _dot_tn(xh * jnp.exp(qtot - qc), bg)

    def body(xf_ref, xb_ref, dtf_ref, dtb_ref, alog_ref, yf_ref, yb_ref, hf_ref, hb_ref, hs):
        @pl.when(pl.program_id(1) == 0)
        def _():
            hs[...] = jnp.zeros(hs.shape, F32)

        chain(0, xf_ref, dtf_ref, alog_ref, yf_ref, hf_ref, hs)
        chain(1, xb_ref, dtb_ref, alog_ref, yb_ref, hb_ref, hs)

    cidx = lambda d: (lambda s: _chunk_index(d, s, nc, nctc))
    specs = lambda d: dict(
        xbc=pl.BlockSpec((None, CH, XBC), lambda b, s: (b, cidx(d)(s), 0)),
        dt=pl.BlockSpec((None, None, CH, HP), lambda b, s: (b, d, cidx(d)(s), 0)),
        y=pl.BlockSpec((None, CH, SSD_IN), lambda b, s: (b, cidx(d)(s), 0)),
        h=pl.BlockSpec((None, None, SSD_IN, NST), lambda b, s: (b, cidx(d)(s), 0, 0)))
    f, r = specs(0), specs(1)
    bb = 2 * (_nbytes((CH, XBC), F32) + _nbytes((CH, HP), F32) + _nbytes((CH, SSD_IN), F32) + _nbytes((SSD_IN, NST), F32))
    return pl.pallas_call(
        functools.partial(body), name="ssd_scan", grid=(B, nc),
        in_specs=[f["xbc"], r["xbc"], f["dt"], r["dt"], pl.BlockSpec((2, 8, HP), lambda b, s: (0, 0, 0))],
        out_specs=[f["y"], r["y"], f["h"], r["h"]],
        out_shape=[_sds((B, T, SSD_IN), F32)] * 2 + [_sds((B, nc, SSD_IN, NST), F32)] * 2,
        scratch_shapes=[pltpu.VMEM((2, SSD_IN, NST), F32)],
        compiler_params=_params(("arbitrary",) * 2, bb, 12 << 20),
    )(xbc, xbc, dt2, dt2, alog2)


def _swap8(u):
    lane = _iota(u.shape, 1)
    n = u.shape[1]
    return jnp.where((lane & 15) < 8, pltpu.roll(u, n - 8, axis=1), pltpu.roll(u, 8, axis=1))


def _rope(u, cos, sin_signed):
    return u * cos + _swap8(u) * sin_signed


def _rms(x, w):
    r = lax.rsqrt(jnp.mean(x * x, axis=-1, keepdims=True) + EPS)
    xh = x * r
    return xh * w, xh, r


def _rms_bwd(dy, xh, r, w):
    dw = jnp.sum(dy * xh, axis=0, keepdims=True)
    dxh = dy * w
    return r * (dxh - xh * jnp.mean(dxh * xh, axis=-1, keepdims=True)), dw


def _tile6(t):
    return jnp.concatenate([t] * H, axis=1)


def _per_head(fn, u):
    return jnp.concatenate([fn(u[:, HP * h:HP * (h + 1)]) for h in range(H)], axis=1)


def _mla_prep(R, qa, kva, kr, qnw, kvnw, wq, wk, wv, cos, sin):
    B, T = R.B, R.T
    scale = QK ** -0.5

    def body(qa_ref, kva_ref, kr_ref, qnw_ref, kvnw_ref, wq_ref, wk_ref, wv_ref, cos_ref, sin_ref,
             qt_ref, kt_ref, qtr_ref, ktr_ref, vtr_ref, cq_ref, ckv_ref):
        cq, _, _ = _rms(qa_ref[...], qnw_ref[...])
        ckv, _, _ = _rms(kva_ref[...], kvnw_ref[...])
        cqb, ckvb = cq.astype(BF), ckv.astype(BF)
        cq_ref[...] = cqb
        ckv_ref[...] = ckvb
        cos1, sin1 = cos_ref[...], sin_ref[...]
        q = _per_head(lambda u: _rope(u, cos1, sin1), jnp.dot(cqb, wq_ref[...], preferred_element_type=F32)) * scale
        qt_ref[...] = q.astype(BF)
        kk = _rope(kr_ref[...], cos1, sin1)
        hl = _iota((TM, QW), 1) & (HP - 1)
        k = jnp.dot(ckvb, wk_ref[...], preferred_element_type=F32) + _tile6(kk)
        k = jnp.where(jnp.logical_and(hl >= QK, hl < QK + 3), 1.0, k)
        kt_ref[...] = k.astype(BF)
        v = jnp.dot(ckvb, wv_ref[...], preferred_element_type=F32)
        v = jnp.where(jnp.logical_and(hl >= VH, hl < VH + 4), 1.0, v)
        for h in range(H):
            cols = slice(HP * h, HP * (h + 1))
            qtr_ref[h] = q[:, cols].T.astype(BF)
            ktr_ref[h] = k[:, cols].T.astype(BF)
            vtr_ref[h] = v[:, cols].T.astype(BF)

    tr = (_sds((B, H, HP, T), BF), pl.BlockSpec((None, H, HP, TM), lambda b, i: (b, 0, 0, i)))
    return R.call("mla_prep", body,
                  [(qa, R.row(QL)), (kva, R.row(QL)), (kr, R.row(HP)), (qnw, R.const((1, QL))), (kvnw, R.const((1, QL))),
                   (wq, R.const((QL, QW))), (wk, R.const((QL, QW))), (wv, R.const((QL, QW))),
                   (cos, pl.BlockSpec((TM, HP), lambda b, i: (i, 0))), (sin, pl.BlockSpec((TM, HP), lambda b, i: (i, 0)))],
                  [(_sds((B, T, QW), BF), R.row(QW))] * 2 + [tr] * 3 + [(_sds((B, T, QL), BF), R.row(QL))] * 2, extra=12 << 20)


def _flash_fwd(B, T, q_t, kt, v_t):
    nq, nk = T // TQ, T // TK
    HS = 2

    def body(q_ref, k_ref, v_ref, o_ref, lse_ref, s_scr):
        i = pl.program_id(2)

        def attend(nch):
            ms = []
            for hh in range(HS):
                q_tr = q_ref[hh]
                mrun = None
                for j in range(nch):
                    s = _dot(k_ref[TK * j:TK * (j + 1), HP * hh:HP * (hh + 1)], q_tr)
                    s_scr[hh, j] = s
                    mrun = s if mrun is None else jnp.maximum(mrun, s)
                ms.append(jnp.max(mrun, axis=0, keepdims=True))
            row = _iota((HP, TQ), 0)
            for hh in range(HS):
                acc_t = jnp.zeros((HP, TQ), F32)
                for j in range(nch):
                    acc_t = acc_t + _dot(v_ref[hh, :, TK * j:TK * (j + 1)], jnp.exp(s_scr[hh, j] - ms[hh]))
                l = jnp.sum(jnp.where(row == VH, acc_t, 0.0), axis=0, keepdims=True)
                o_ref[:, HP * hh:HP * (hh + 1)] = jnp.where(row < VH, acc_t / l, 0.0).T
                lse_ref[hh] = ms[hh] + jnp.log(l)

        @pl.when(i < CTX // TQ)
        def _():
            attend(CTX // TK)

        @pl.when(i >= CTX // TQ)
        def _():
            attend(nk)

    bb = HS * (_nbytes((TQ, HP), BF) + 2 * _nbytes((T, HP), BF) + 2 * _nbytes((TQ, HP), F32))
    return pl.pallas_call(
        functools.partial(body), name="flash_fwd", grid=(B, H // HS, nq),
        in_specs=[pl.BlockSpec((None, HS, HP, TQ), lambda b, h, i: (b, h, 0, i)),
                  pl.BlockSpec((None, T, HS * HP), lambda b, h, i: (b, 0, h)),
                  pl.BlockSpec((None, HS, HP, T), lambda b, h, i: (b, h, 0, 0))],
        out_specs=[pl.BlockSpec((None, TQ, HS * HP), lambda b, h, i: (b, i, h)),
                   pl.BlockSpec((None, HS, 1, TQ), lambda b, h, i: (b, h, 0, i))],
        out_shape=[_out((B, T, QW), F32), _out((B, H, 1, T), F32)],
        scratch_shapes=[pltpu.VMEM((HS, nk, TK, TQ), F32)],
        compiler_params=_params(("arbitrary",) * 3, bb, _nbytes((HS, nk, TK, TQ), F32) + (8 << 20)),
    )(*_hbm([q_t, kt, v_t]))


def _pool_terms(R, i, rows):
    t, n = R.stream_pos(i, rows)
    lane = _iota((1, PD), 1)
    half = jnp.where(lane < 64, 1, jnp.where(lane < 128, 2, jnp.where(lane < 192, 4, 8)))
    cnt = (jnp.minimum(t + half, n) - jnp.maximum(t - half, 0)).astype(F32)
    valid = jnp.logical_and(t >= 0, t < n)
    return jnp.where(valid, cnt, 1.0), valid.astype(F32), lane


def _lane_select(lane, a2, a4, a8, a16):
    return jnp.where(lane < 64, a2, jnp.where(lane < 128, a4, jnp.where(lane < 192, a8, a16)))


def _pool_centred(R, i, ext):
    cnt, valid, lane = _pool_terms(R, i, ext.shape[0])
    s2 = ext + _shift_rows(ext, -1)
    s4 = _shift_rows(s2, -1) + _shift_rows(s2, 1)
    s8 = _shift_rows(s4, -2) + _shift_rows(s4, 2)
    s16 = _shift_rows(s8, -4) + _shift_rows(s8, 4)
    return _lane_select(lane, s2, s4, s8, s16) / cnt - ext, cnt, valid, lane


def _group_mask():
    return _iota((1, SSD_IN), 1) < SSD_IN // 2


def _ssd_gate(yf_ref, yb_ref, xbc_ref, z_ref, dsk_ref):
    ytot = yf_ref[...] + yb_ref[...] + xbc_ref[:, 0:SSD_IN] * dsk_ref[...]
    z = z_ref[...]
    gz = ytot * _silu(z)
    g0 = _group_mask()
    sq = gz * gz
    s0 = jnp.sum(jnp.where(g0, sq, 0.0), axis=1, keepdims=True)
    s1 = jnp.sum(jnp.where(g0, 0.0, sq), axis=1, keepdims=True)
    half = SSD_IN // 2
    r = jnp.where(g0, lax.rsqrt(s0 / half + EPS), lax.rsqrt(s1 / half + EPS))
    return ytot, z, gz, r


def _out_proj(R, x, mod, yf, yb, xbc, z, o, pool_in, wbd, pscale, dsk, snw, wout):
    B, T = R.B, R.T
    xin = R.xin(x)

    def body(*refs):
        x_refs = refs[:len(xin)]
        (mod_ref, yf_ref, yb_ref, xbc_ref, z_ref, o_ref, u_ref, upv_ref, unx_ref, wbd_ref, psc_ref, dsk_ref, snw_ref, w_ref,
         xmid_ref, cat_ref, mix_ref) = refs[len(xin):]
        i = pl.program_id(1)
        _, _, gz, r = _ssd_gate(yf_ref, yb_ref, xbc_ref, z_ref, dsk_ref)
        dm, _, _, _ = _pool_centred(R, i, R.ext(i, upv_ref, u_ref[...], unx_ref))
        cat_ref[:, 0:SSD_IN] = (gz * r * snw_ref[...]).astype(BF)
        cat_ref[:, SSD_IN:SSD_IN + QW] = o_ref[...].astype(BF)
        cat_ref[:, SSD_IN + QW:MIXW] = (_dot(dm[8:8 + TM], wbd_ref[...]) * psc_ref[...]).astype(BF)
        mix = jnp.dot(cat_ref[...], w_ref[...], preferred_element_type=F32)
        mix_ref[...] = mix.astype(BF)
        xmid_ref[...] = R.xval(i, x_refs) + _row(mod_ref, 2) * mix

    return R.call("out_proj", body,
                  xin + [(mod, R.mod()), (yf, R.row(SSD_IN)), (yb, R.row(SSD_IN)), (xbc, R.row(XBC)), (z, R.row(SSD_IN)),
                         (o, R.row(QW)), (pool_in, R.row(PD)), (pool_in, R.prev8(PD)), (pool_in, R.next8(PD)),
                         (wbd, R.const((PD, PD))), (pscale, R.const((1, PD))),
                         (dsk, R.const((1, SSD_IN))), (snw, R.const((1, SSD_IN))), (wout, R.const((MIXW, D)))],
                  [(_sds((B, T, D), F32), R.row(D)), (_sds((B, T, MIXW), BF), R.row(MIXW)), (_sds((B, T, D), BF), R.row(D))],
                  extra=12 << 20)


def _load_once(first, pairs, sem):
    @pl.when(first)
    def _():
        cps = [pltpu.make_async_copy(src, dst, sem.at[k]) for k, (src, dst) in enumerate(pairs)]
        for cp in cps:
            cp.start()
        for cp in cps:
            cp.wait()


ANY = pl.BlockSpec(memory_space=pl.ANY)


def _mlp_fwd(R, xmid, mod, nw2, w1, w2):
    B, T = R.B, R.T

    def body(x_ref, mod_ref, nw_ref, w1_hbm, w2_hbm, xo_ref, h_ref, u_ref, y_ref, w1_v, w2_v, sem):
        first = jnp.logical_and(pl.program_id(0) == 0, pl.program_id(1) == 0)
        _load_once(first, [(w1_hbm, w1_v), (w2_hbm, w2_v)], sem)
        x = x_ref[...]
        h, _, _ = _norm_mod(x, nw_ref[...], _row(mod_ref, 3), _row(mod_ref, 4))
        hb = h.astype(BF)
        h_ref[...] = hb
        y = jnp.zeros((TM, D), F32)
        for j in range(NDEV):
            u = jnp.dot(hb, w1_v[j], preferred_element_type=F32)
            u_ref[:, FSH * j:FSH * (j + 1)] = u.astype(BF)
            a = jnp.square(jnp.maximum(u, 0.0))
            y = y + jnp.dot(a.astype(BF), w2_v[j], preferred_element_type=F32)
        y_ref[...] = y.astype(BF)
        xo_ref[...] = x + _row(mod_ref, 5) * y

    return R.call("mlp_fwd", body,
                  [(xmid, R.row(D)), (mod, R.mod()), (nw2, R.const((1, D))), (w1, ANY), (w2, ANY)],
                  [(_sds((B, T, D), F32), R.row(D)), (_sds((B, T, D), BF), R.row(D)), (_sds((B, T, FF), BF), R.row(FF)),
                   (_sds((B, T, D), BF), R.row(D))],
                  scratch=[pltpu.VMEM((NDEV, D, FSH), w1.dtype), pltpu.VMEM((NDEV, FSH, D), w2.dtype), pltpu.SemaphoreType.DMA((2,))],
                  extra=(2 * _nbytes((NDEV, D, FSH), BF)) + (8 << 20))


def _loss_head(R, x, tgt, fnw):
    B, T = R.B, R.T

    def body(x_ref, t_ref, w_ref, dx_ref, loss_ref, dw_ref):
        b, i = pl.program_id(0), pl.program_id(1)
        live = jnp.where(i >= R.nct, 1.0, 0.0)
        y, xh, r = _rms(x_ref[...], w_ref[...])
        err = (y - t_ref[...]) * live
        dy = err / D
        dxn, dw = _rms_bwd(dy, xh, r, w_ref[...])
        dx_ref[...] = dxn
        first = jnp.logical_and(b == 0, i == 0)
        part = 0.5 * jnp.sum(jnp.sum(err * err, axis=1, keepdims=True), axis=0, keepdims=True) / D
        _acc_rows(loss_ref, first, {0: jnp.broadcast_to(part, (1, HP))})
        _acc_rows(dw_ref, first, {0: dw})

    return R.call("loss_head", body,
                  [(x, R.row(D)), (tgt, R.tgt(D)), (fnw, R.const((1, D)))],
                  [(_sds((B, T, D), F32), R.row(D)), (_sds((8, HP), F32), R.const((8, HP))), (_sds((8, D), F32), R.const((8, D)))],
                  extra=8 << 20)


def _tn_matmul(name, a, b, tn, sub=1):
    B, T, K = a.shape
    N = b.shape[2]
    nk = 1
    while T % nk or (T // nk) > 1088 or (T // nk) % 16:
        nk += 1
    tk = T // nk
    kt = K if K <= 1536 else 2048
    w = tn // sub
    assert K % kt == 0 and N % tn == 0 and tn % sub == 0

    def body(a_ref, b_ref, o_ref):
        first = jnp.logical_and(pl.program_id(2) == 0, pl.program_id(3) == 0)

        @pl.when(first)
        def _():
            o_ref[...] = jnp.zeros(o_ref.shape, F32)

        acc = _dot_tn(a_ref[...], b_ref[...])
        for s in range(sub):
            o_ref[s] += acc[:, w * s:w * (s + 1)]

    bb = _nbytes((tk, kt), a.dtype) + _nbytes((tk, tn), b.dtype) + _nbytes((kt, tn), F32)
    return pl.pallas_call(
        functools.partial(body), name=name, grid=(N // tn, K // kt, B, nk),
        in_specs=[pl.BlockSpec((None, tk, kt), lambda j, kk, bi, t: (bi, t, kk)),
                  pl.BlockSpec((None, tk, tn), lambda j, kk, bi, t: (bi, t, j))],
        out_specs=pl.BlockSpec((sub, kt, w), lambda j, kk, bi, t: (j, kk, 0)),
        out_shape=_out((sub * N // tn, K, w), F32),
        compiler_params=_params(("arbitrary",) * 4, bb, _nbytes((kt, tn), F32) + (8 << 20)),
    )(*_hbm([a, b]))


def _mlp_bwd(R, dxo, xmid, ubf, ybf, mod, nw2, w1, w2):
    B, T = R.B, R.T

    def body(dxo_ref, x_ref, u_ref, y_ref, mod_ref, nw_ref, w1_hbm, w2_hbm,
             dxm_ref, du_ref, a_ref, dy_ref, dmod_ref, dnw_ref, w1_v, w2_v, sem):
        b, i = pl.program_id(0), pl.program_id(1)
        _load_once(jnp.logical_and(b == 0, i == 0), [(w1_hbm, w1_v), (w2_hbm, w2_v)], sem)
        dxo = dxo_ref[...]
        _, xn, r = _norm_mod(x_ref[...], nw_ref[...], _row(mod_ref, 3), _row(mod_ref, 4))
        dyb = (dxo * _row(mod_ref, 5)).astype(BF)
        dy_ref[...] = dyb
        dg2 = jnp.sum(dxo * y_ref[...].astype(F32), axis=0, keepdims=True)
        dh = jnp.zeros((TM, D), F32)
        for j in range(NDEV):
            rl = jnp.maximum(u_ref[:, FSH * j:FSH * (j + 1)].astype(F32), 0.0)
            a_ref[:, FSH * j:FSH * (j + 1)] = (rl * rl).astype(BF)
            du = (_dot_nt(dyb, w2_v[j]) * (2.0 * rl)).astype(BF)
            du_ref[:, FSH * j:FSH * (j + 1)] = du
            dh = dh + _dot_nt(du, w1_v[j])
        dx, dsh, dsc, dnw = _norm_mod_bwd(dh, xn, r, nw_ref[...], _row(mod_ref, 4))
        dxm_ref[...] = dxo + dx
        _acc_rows(dmod_ref, R.first_of_stream(i), {3: dsh, 4: dsc, 5: dg2})
        _acc_rows(dnw_ref, jnp.logical_and(b == 0, i == 0), {0: dnw})

    return R.call("mlp_bwd", body,
                  [(dxo, R.row(D)), (xmid, R.row(D)), (ubf, R.row(FF)), (ybf, R.row(D)), (mod, R.mod()), (nw2, R.const((1, D))),
                   (w1, ANY), (w2, ANY)],
                  [(_sds((B, T, D), F32), R.row(D)), (_sds((B, T, FF), BF), R.row(FF)), (_sds((B, T, FF), BF), R.row(FF)),
                   (_sds((B, T, D), BF), R.row(D)), (_sds((B, 2, 8, D), F32), R.mod()), (_sds((8, D), F32), R.const((8, D)))],
                  scratch=[pltpu.VMEM((NDEV, D, FSH), w1.dtype), pltpu.VMEM((NDEV, FSH, D), w2.dtype), pltpu.SemaphoreType.DMA((2,))],
                  extra=(2 * _nbytes((NDEV, D, FSH), BF)) + (8 << 20))


def _minus_in_lanes(x, col, first):
    b16 = jnp.bfloat16
    hi = col.astype(b16).astype(F32)
    r1 = col - hi
    mid = r1.astype(b16).astype(F32)
    lane = _iota(x.shape, 1)
    return jnp.where(lane == first, -hi, jnp.where(lane == first + 1, -mid, jnp.where(lane == first + 2, mid - r1, x)))


def _out_bwd(R, dxm, mod, mixbf, yf, yb, xbc, z, o, qt, lse, dsk, snw, wout):
    B, T = R.B, R.T

    def body(dxm_ref, mod_ref, mix_ref, yf_ref, yb_ref, xbc_ref, z_ref, o_ref, q_ref, lse_ref, dsk_ref, snw_ref, w_ref,
             dmix_ref, dz_ref, dyt_ref, dxsk_ref, qs_ref, dos_ref, dotr_ref, dpo_ref, dmod_ref, dvec_ref):
        b, i = pl.program_id(0), pl.program_id(1)
        dxm = dxm_ref[...]
        dmixb = (dxm * _row(mod_ref, 2)).astype(BF)
        dmix_ref[...] = dmixb
        dg1 = jnp.sum(dxm * mix_ref[...].astype(F32), axis=0, keepdims=True)
        dcat = _dot_nt(dmixb, w_ref[...])
        do_v = dcat[:, SSD_IN:SSD_IN + QW]
        doo = do_v * o_ref[...]
        for h in range(H):
            cols = slice(HP * h, HP * (h + 1))
            dd = jnp.sum(doo[:, cols], axis=1, keepdims=True)
            dos_ref[:, cols] = _minus_in_lanes(do_v[:, cols], dd, VH + 1).astype(BF)
            lse_b = jnp.broadcast_to(lse_ref[h], (HP, TM)).T
            qs_ref[:, cols] = _minus_in_lanes(q_ref[:, cols].astype(F32), lse_b, QK).astype(BF)
            dotr_ref[h] = do_v[:, cols].T.astype(BF)
        dpo_ref[...] = dcat[:, SSD_IN + QW:MIXW]
        dsn = dcat[:, 0:SSD_IN]
        ytot, zv, gz, r = _ssd_gate(yf_ref, yb_ref, xbc_ref, z_ref, dsk_ref)
        gh = gz * r
        dsnw = jnp.sum(dsn * gh, axis=0, keepdims=True)
        dgh = dsn * snw_ref[...]
        g0 = _group_mask()
        pr = dgh * gh
        half = SSD_IN // 2
        m0 = jnp.sum(jnp.where(g0, pr, 0.0), axis=1, keepdims=True) / half
        m1 = jnp.sum(jnp.where(g0, 0.0, pr), axis=1, keepdims=True) / half
        dgz = r * (dgh - gh * jnp.where(g0, m0, m1))
        dyt = dgz * _silu(zv)
        dz_ref[...] = (dgz * ytot * _dsilu(zv)).astype(BF)
        dyt_ref[...] = dyt
        dxsk_ref[...] = dyt * dsk_ref[...]
        ddsk = jnp.sum(dyt * xbc_ref[:, 0:SSD_IN], axis=0, keepdims=True)
        _acc_rows(dmod_ref, R.first_of_stream(i), {2: dg1})
        _acc_rows(dvec_ref, jnp.logical_and(b == 0, i == 0), {0: dsnw, 1: ddsk})

    return R.call("out_bwd", body,
                  [(dxm, R.row(D)), (mod, R.mod()), (mixbf, R.row(D)), (yf, R.row(SSD_IN)), (yb, R.row(SSD_IN)), (xbc, R.row(XBC)),
                   (z, R.row(SSD_IN)),
                   (o, R.row(QW)), (qt, R.row(QW)), (lse, pl.BlockSpec((None, H, 1, TM), lambda b, i: (b, 0, 0, i))),
                   (dsk, R.const((1, SSD_IN))), (snw, R.const((1, SSD_IN))), (wout, R.const((MIXW, D)))],
                  [(_sds((B, T, D), BF), R.row(D)), (_sds((B, T, SSD_IN), BF), R.row(SSD_IN)), (_sds((B, T, SSD_IN), F32), R.row(SSD_IN)),
                   (_sds((B, T, SSD_IN), F32), R.row(SSD_IN)), (_sds((B, T, QW), BF), R.row(QW)), (_sds((B, T, QW), BF), R.row(QW)),
                   (_sds((B, H, HP, T), BF), pl.BlockSpec((None, H, HP, TM), lambda b, i: (b, 0, 0, i))),
                   (_sds((B, T, PD), F32), R.row(PD)),
                   (_sds((B, 2, 8, D), F32), R.mod()), (_sds((8, SSD_IN), F32), R.const((8, SSD_IN)))],
                  extra=8 << 20)


def _pool_bwd(R, dpo, u, wbd, scale):
    B, T = R.B, R.T

    def body(d_ref, dpv_ref, dnx_ref, u_ref, upv_ref, unx_ref, w_ref, sc_ref, du_ref, dw_ref, dsc_ref):
        b, i = pl.program_id(0), pl.program_id(1)
        ext_u = R.ext(i, upv_ref, u_ref[...], unx_ref)
        ext_d = R.ext(i, dpv_ref, d_ref[...], dnx_ref)
        dm, cnt, valid, lane = _pool_centred(R, i, ext_u)
        ddm = _dot_nt(ext_d * sc_ref[...], w_ref[...]) * valid
        e = ddm / cnt
        a2 = e + _shift_rows(e, 1)
        a4 = _shift_rows(a2, -1) + _shift_rows(a2, 1)
        a8 = _shift_rows(a4, -2) + _shift_rows(a4, 2)
        a16 = _shift_rows(a8, -4) + _shift_rows(a8, 4)
        du_ref[...] = (_lane_select(lane, a2, a4, a8, a16) - ddm)[8:8 + TM].astype(BF)
        dmc = dm[8:8 + TM]
        dpo_c = d_ref[...]
        first = jnp.logical_and(b == 0, i == 0)

        @pl.when(first)
        def _():
            dw_ref[...] = jnp.zeros(dw_ref.shape, F32)

        dw_ref[...] += _dot_tn(dmc, dpo_c * sc_ref[...])
        _acc_rows(dsc_ref, first, {0: jnp.sum(dpo_c * _dot(dmc, w_ref[...]), axis=0, keepdims=True)})

    return R.call("pool_bwd", body,
                  [(dpo, R.row(PD)), (dpo, R.prev8(PD)), (dpo, R.next8(PD)), (u, R.row(PD)), (u, R.prev8(PD)), (u, R.next8(PD)),
                   (wbd, R.const((PD, PD))), (scale, R.const((1, PD)))],
                  [(_sds((B, T, PD), BF), R.row(PD)), (_sds((PD, PD), F32), R.const((PD, PD))), (_sds((8, PD), F32), R.const((8, PD)))],
                  extra=8 << 20)


def _flash_bwd(B, T, qs, kt, dos, q_t, k_t, v_t, do_t):
    nk, nq = T // TK, T // TQ
    HS = 2

    def body(k_ref, kt_ref, vt_ref, q_ref, do_ref, qt_ref, dot_ref, dq_ref, dk_ref, dv_ref, s_scr, dp_scr, ds_scr):
        j = pl.program_id(2)
        ctx_keys = jnp.where(j < CTX // TK, 1.0, 0.0)
        for hh in range(HS):
            cols = slice(HP * hh, HP * (hh + 1))
            s_scr[hh] = _dot(q_ref[:, cols], kt_ref[hh])
            dp_scr[hh] = _dot(do_ref[:, cols], vt_ref[hh])
        for hh in range(HS):
            cols = slice(HP * hh, HP * (hh + 1))
            dk_tr = jnp.zeros((HP, TK), F32)
            dv_tr = jnp.zeros((HP, TK), F32)
            for i in range(nq):
                rows = slice(TQ * i, TQ * (i + 1))
                p = jnp.exp(s_scr[hh, rows, :])
                if i < CTX // TQ:
                    p = p * ctx_keys
                ds = (p * dp_scr[hh, rows, :]).astype(BF)
                ds_scr[hh, rows, :] = ds
                dv_tr = dv_tr + _dot(dot_ref[hh, :, rows], p)
                dk_tr = dk_tr + _dot(qt_ref[hh, :, rows], ds)
            dq = _dot(ds_scr[hh], k_ref[:, cols])

            @pl.when(j == 0)
            def _():
                dq_ref[:, cols] = dq

            @pl.when(j > 0)
            def _():
                dq_ref[:, cols] += dq

            dk_ref[:, cols] = dk_tr.T
            dv_ref[:, cols] = dv_tr.T

    tspec = pl.BlockSpec((None, TK, HS * HP), lambda b, h, j: (b, j, h))
    fspec = pl.BlockSpec((None, T, HS * HP), lambda b, h, j: (b, 0, h))
    ttspec = pl.BlockSpec((None, HS, HP, TK), lambda b, h, j: (b, h, 0, j))
    ftspec = pl.BlockSpec((None, HS, HP, T), lambda b, h, j: (b, h, 0, 0))
    bb = HS * (4 * _nbytes((T, HP), BF) + _nbytes((T, HP), F32) + 8 * _nbytes((TK, HP), F32))
    scr = HS * (2 * _nbytes((T, TK), F32) + _nbytes((T, TK), BF))
    return pl.pallas_call(
        functools.partial(body), name="flash_bwd", grid=(B, H // HS, nk),
        in_specs=[tspec, ttspec, ttspec, fspec, fspec, ftspec, ftspec],
        out_specs=[fspec, tspec, tspec],
        out_shape=[_out((B, T, QW), F32)] * 3,
        scratch_shapes=[pltpu.VMEM((HS, T, TK), F32), pltpu.VMEM((HS, T, TK), F32), pltpu.VMEM((HS, T, TK), BF)],
        compiler_params=_params(("arbitrary",) * 3, bb, scr + (8 << 20)),
    )(*_hbm([kt, k_t, v_t, qs, dos, q_t, do_t]))


def _mla_bwd(R, dqt, dkt, dvt, qa, kva, qnw, kvnw, wq, wk, wv, cos, sin):
    B, T = R.B, R.T
    scale = QK ** -0.5

    def body(dq_ref, dk_ref, dv_ref, qa_ref, kva_ref, qnw_ref, kvnw_ref, wq_ref, wk_ref, wv_ref, cos_ref, sin_ref,
             dqp_ref, dkvb_ref, dqa_ref, dkva_ref, dkr_ref, dnw_ref):
        b, i = pl.program_id(0), pl.program_id(1)
        cos1, sin1 = cos_ref[...], sin_ref[...]
        dq = dq_ref[...] * scale
        dqp = _per_head(lambda g: g * cos1 + _swap8(g * sin1), dq).astype(BF)
        dqp_ref[...] = dqp
        dkv = dk_ref[...]
        dkb = dkv.astype(BF)
        dvb = dv_ref[...].astype(BF)
        dkvb_ref[:, :QW] = dkb
        dkvb_ref[:, QW:] = dvb
        dkk = dkv[:, 0:HP]
        for h in range(1, H):
            dkk = dkk + dkv[:, HP * h:HP * (h + 1)]
        lane = _iota((TM, HP), 1)
        rope_lane = jnp.logical_and(lane >= NOPE, lane < NOPE + ROPE)
        dkr_ref[...] = jnp.where(rope_lane, dkk * cos1 + _swap8(dkk * sin1), 0.0).astype(BF)
        _, qh, qr = _rms(qa_ref[...], qnw_ref[...])
        _, kh, kr_ = _rms(kva_ref[...], kvnw_ref[...])
        dcq = _dot_nt(dqp, wq_ref[...])
        dckv = _dot_nt(dkb, wk_ref[...]) + _dot_nt(dvb, wv_ref[...])
        dqa, dqw = _rms_bwd(dcq, qh, qr, qnw_ref[...])
        dkva, dkw = _rms_bwd(dckv, kh, kr_, kvnw_ref[...])
        dqa_ref[...] = dqa.astype(BF)
        dkva_ref[...] = dkva.astype(BF)
        _acc_rows(dnw_ref, jnp.logical_and(b == 0, i == 0), {0: dqw, 1: dkw})

    tab = pl.BlockSpec((TM, HP), lambda b, i: (i, 0))
    return R.call("mla_bwd", body,
                  [(dqt, R.row(QW)), (dkt, R.row(QW)), (dvt, R.row(QW)), (qa, R.row(QL)), (kva, R.row(QL)),
                   (qnw, R.const((1, QL))), (kvnw, R.const((1, QL))), (wq, R.const((QL, QW))), (wk, R.const((QL, QW))),
                   (wv, R.const((QL, QW))), (cos, tab), (sin, tab)],
                  [(_sds((B, T, QW), BF), R.row(QW)), (_sds((B, T, 2 * QW), BF), R.row(2 * QW))]
                  + [(_sds((B, T, QL), BF), R.row(QL))] * 2
                  + [(_sds((B, T, HP), BF), R.row(HP)), (_sds((8, QL), F32), R.const((8, QL)))], extra=8 << 20)


def _ssd_scan_bwd(B, T, dyt, xbc, dt2, alog2, hin_f, hin_b):
    nc, nctc = T // CH, CTX // CH

    def chain(d, dy_ref, xbc_ref, dt_ref, alog_ref, hin_ref, dxbc_ref, ddt_ref, da_ref, dhs_all):
        dhs = dhs_all.at[d]
        xbc_v = xbc_ref[...]
        dyv = dy_ref[...]
        a_all, dt_all, adt_all, q_all, inc, inc_t = _scan_common(d, dt_ref[...], alog_ref)
        ones_p = jnp.ones((P, HP), F32)
        total = lambda m: jnp.sum(jnp.sum(m, axis=0, keepdims=True), axis=1, keepdims=True)
        dq_parts, dqtot_parts, ddtx_parts = [], [], []
        for g in range(2):
            bg = xbc_v[:, SSD_IN + NST * g:SSD_IN + NST * (g + 1)]
            cg = xbc_v[:, SSD_IN + 2 * NST + NST * g:SSD_IN + 2 * NST + NST * (g + 1)]
            bg_t, cg_t = bg.T, cg.T
            gm = _dot(cg, bg_t)
            gm_t = _dot(bg, cg_t)
            dgm = jnp.zeros((CH, CH), F32)
            dgm_t = jnp.zeros((CH, CH), F32)
            dbg = jnp.zeros((CH, NST), F32)
            dcg = jnp.zeros((CH, NST), F32)
            for r in range(3):
                h = 3 * g + r
                lm, lm_t, eq, etot, dte = _head_decay(h, adt_all, q_all, inc, inc_t)
                eq_p, etot_p, dte_p = eq[:, :P], etot[:, :P], dte[:, :P]
                dt_p = dt_all[:, HP * h:HP * h + P]
                xs_h = xbc_v[:, P * h:P * (h + 1)]
                xh = xs_h * dt_p
                sm, sm_t = gm * lm, gm_t * lm_t
                dy_h = dyv[:, P * h:P * (h + 1)]
                hprev = hin_ref[P * h:P * (h + 1), :].T
                dho = dhs[:, P * h:P * (h + 1)]
                ds = _dot_nt(dy_h, xh)
                ds_t = _dot_nt(xh, dy_h)
                dx = _dot(sm_t, dy_h)
                edy = eq_p * dy_h
                yo = _dot(cg, hprev)
                dcg = dcg + _dot_nt(edy, hprev)
                dhin = _dot(cg_t, edy) + etot_p * dho
                zs = _dot(bg, dho)
                dx = dx + dte_p * zs
                wm = dte_p * xh * zs
                dbg = dbg + _dot_nt(xh * dte_p, dho)
                dgm = dgm + ds * lm
                dgm_t = dgm_t + ds_t * lm_t
                rs = jnp.sum(ds * sm - ds_t * sm_t, axis=1, keepdims=True) + jnp.sum(edy * yo - wm, axis=1, keepdims=True)
                dq_parts.append(jnp.broadcast_to(rs, (CH, HP)))
                dqtot_parts.append(total(hprev * dho) * etot + total(wm))
                dxbc_ref[:, P * h:P * (h + 1)] = dx * dt_p
                ddtx_parts.append(_dot01_r(dx * xs_h, ones_p, passes=2))
                dhs[:, P * h:P * (h + 1)] = dhin
            dcg = dcg + _dot(dgm, bg)
            dbg = dbg + _dot(dgm_t, cg)
            dxbc_ref[:, SSD_IN + NST * g:SSD_IN + NST * (g + 1)] = dbg
            dxbc_ref[:, SSD_IN + 2 * NST + NST * g:SSD_IN + 2 * NST + NST * (g + 1)] = dcg
        cat = lambda parts: jnp.concatenate(parts, axis=1)
        dadt_all = _dot01(inc_t, cat(dq_parts)) + cat(dqtot_parts)
        ddt_all = cat(ddtx_parts) + dadt_all * jnp.concatenate([a_all] * (CH // 8), axis=0)
        da_all = jnp.sum(dadt_all * dt_all, axis=0, keepdims=True)
        lane, lane1 = _iota((CH, HP), 1), _iota((1, HP), 1)
        ddt = jnp.zeros((CH, HP), F32)
        da = jnp.zeros((1, HP), F32)
        for h in range(H):
            ddt = ddt + jnp.where(lane == h, ddt_all[:, HP * h:HP * (h + 1)], 0.0)
            da = da + jnp.where(lane1 == h, da_all[:, HP * h:HP * (h + 1)], 0.0)
        ddt_ref[...] = ddt
        da_ref[d] += jnp.where(_iota((8, HP), 0) == 0, da, 0.0)

    def body(dyf_ref, dyb_ref, xf_ref, xb_ref, dtf_ref, dtb_ref, alog_ref, hf_ref, hb_ref,
             dxf_ref, dxb_ref, ddtf_ref, ddtb_ref, da_ref, dhs):
        @pl.when(pl.program_id(1) == 0)
        def _():
            dhs[...] = jnp.zeros(dhs.shape, F32)
            da_ref[...] = jnp.zeros(da_ref.shape, F32)

        chain(0, dyf_ref, xf_ref, dtf_ref, alog_ref, hf_ref, dxf_ref, ddtf_ref, da_ref, dhs)
        chain(1, dyb_ref, xb_ref, dtb_ref, alog_ref, hb_ref, dxb_ref, ddtb_ref, da_ref, dhs)

    cidx = lambda d: (lambda s: _chunk_index(d, nc - 1 - s, nc, nctc))
    specs = lambda d: dict(
        dy=pl.BlockSpec((None, CH, SSD_IN), lambda b, s: (b, cidx(d)(s), 0)),
        xbc=pl.BlockSpec((None, CH, XBC), lambda b, s: (b, cidx(d)(s), 0)),
        dt=pl.BlockSpec((None, None, CH, HP), lambda b, s: (b, d, cidx(d)(s), 0)),
        h=pl.BlockSpec((None, None, SSD_IN, NST), lambda b, s: (b, cidx(d)(s), 0, 0)),
        ddt=pl.BlockSpec((None, CH, HP), lambda b, s: (b, cidx(d)(s), 0)))
    f, r = specs(0), specs(1)
    bb = 2 * (2 * _nbytes((CH, XBC), F32) + 2 * _nbytes((CH, HP), F32) + _nbytes((CH, SSD_IN), F32) + _nbytes((SSD_IN, NST), F32))
    return pl.pallas_call(
        functools.partial(body), name="ssd_scan_bwd", grid=(B, nc),
        in_specs=[f["dy"], r["dy"], f["xbc"], r["xbc"], f["dt"], r["dt"], pl.BlockSpec((2, 8, HP), lambda b, s: (0, 0, 0)),
                  f["h"], r["h"]],
        out_specs=[f["xbc"], r["xbc"], f["ddt"], r["ddt"], pl.BlockSpec((None, 2, 8, HP), lambda b, s: (b, 0, 0, 0))],
        out_shape=[_sds((B, T, XBC), F32)] * 2 + [_sds((B, T, HP), F32)] * 2 + [_sds((B, 2, 8, HP), F32)],
        scratch_shapes=[pltpu.VMEM((2, NST, SSD_IN), F32)],
        compiler_params=_params(("arbitrary",) * 2, bb, 16 << 20),
    )(dyt, dyt, xbc, xbc, dt2, dt2, alog2, hin_f, hin_b)


def _ssd_prep_bwd(R, dxbc_f, dxbc_b, dxsk, ddt_f, ddt_b, xbc_raw, dt_raw, conv_w8, conv_b, dtb):
    B, T = R.B, R.T

    def body(dxf_ref, dxb_ref, dsk_ref, ddtf_ref, ddtb2_ref, raw_ref, pv_ref, nx_ref, dtr_ref, w_ref, b_ref, dtb_ref,
             dpre_ref, ddtr_ref, dcw_ref, dvec_ref, ddtb_ref):
        b, i = pl.program_id(0), pl.program_id(1)
        ext = R.ext(i, pv_ref, raw_ref[...], nx_ref)
        pre = _conv_pre(ext, w_ref, b_ref)
        dxbc = dxf_ref[...] + dxb_ref[...]
        skip = jnp.concatenate([dsk_ref[...], jnp.zeros((TM, XBC - SSD_IN), F32)], axis=1)
        dpre = (dxbc + skip) * _dsilu(pre)
        dpre_ref[...] = dpre
        first = jnp.logical_and(b == 0, i == 0)
        taps = {k: jnp.sum(dpre * _shift_rows(ext, k - 1)[8:8 + TM], axis=0, keepdims=True) for k in range(4)}
        _acc_rows(dcw_ref, first, taps)
        _acc_rows(dvec_ref, first, {0: jnp.sum(dpre, axis=0, keepdims=True)})
        ddt = ddtf_ref[...] + pltpu.roll(ddtb2_ref[...], H, axis=1)
        ddtr = ddt * _sigmoid(dtr_ref[...] + dtb_ref[...])
        ddtr = jnp.where(_iota((TM, HP), 1) < 2 * H, ddtr, 0.0)
        ddtr_ref[...] = ddtr.astype(BF)
        _acc_rows(ddtb_ref, first, {0: jnp.sum(ddtr, axis=0, keepdims=True)})

    return R.call("ssd_prep_bwd", body,
                  [(dxbc_f, R.row(XBC)), (dxbc_b, R.row(XBC)), (dxsk, R.row(SSD_IN)), (ddt_f, R.row(HP)), (ddt_b, R.row(HP)),
                   (xbc_raw, R.row(XBC)), (xbc_raw, R.prev8(XBC)),
                   (xbc_raw, R.next8(XBC)), (dt_raw, R.row(HP)), (conv_w8, R.const((8, XBC))), (conv_b, R.const((1, XBC))),
                   (dtb, R.const((1, HP)))],
                  [(_sds((B, T, XBC), F32), R.row(XBC)), (_sds((B, T, HP), BF), R.row(HP)), (_sds((8, XBC), F32), R.const((8, XBC))),
                   (_sds((8, XBC), F32), R.const((8, XBC))), (_sds((8, HP), F32), R.const((8, HP)))], extra=16 << 20)


def _in_bwd(R, dxm, x, mod, nw1, dz, dpre, dqa, dkva, dpool, dkr, ddtr, conv_w8, w_arr, latent_only=False):
    B, T = R.B, R.T
    dx_out = (_sds((B, T - CTX, D), F32), R.tgt(D)) if latent_only else (_sds((B, T, D), F32), R.row(D))

    xin = R.xin(x)

    def body(*refs):
        x_refs = refs[:len(xin)]
        (dxm_ref, mod_ref, nw_ref, dz_ref, dp_ref, dpp_ref, dpn_ref, dqa_ref, dkva_ref, dpool_ref, dkr_ref, ddt_ref,
         cw_ref, w_ref, dx_ref, dproj_ref, dmod_ref, dnw_ref) = refs[len(xin):]
        b, i = pl.program_id(0), pl.program_id(1)
        ext = R.ext(i, dpp_ref, dp_ref[...], dpn_ref)
        draw = (_row(cw_ref, 0) * _shift_rows(ext, 1)[8:8 + TM] + _row(cw_ref, 1) * ext[8:8 + TM]
                + _row(cw_ref, 2) * _shift_rows(ext, -1)[8:8 + TM] + _row(cw_ref, 3) * _shift_rows(ext, -2)[8:8 + TM])
        dproj_ref[:, A_Z:A_XBC] = dz_ref[...]
        dproj_ref[:, A_XBC:A_QA] = draw.astype(BF)
        dproj_ref[:, A_QA:A_KVA] = dqa_ref[...]
        dproj_ref[:, A_KVA:A_POOL] = dkva_ref[...]
        dproj_ref[:, A_POOL:A_KR] = dpool_ref[...]
        dproj_ref[:, A_KR:A_DT] = dkr_ref[...]
        dproj_ref[:, A_DT:PC] = ddt_ref[...]
        dh = _dot_nt(dproj_ref[...], w_ref[...])
        _, xn, r = _norm_mod(R.xval(i, x_refs), nw_ref[...], _row(mod_ref, 0), _row(mod_ref, 1))
        dx, dsh, dsc, dnw = _norm_mod_bwd(dh, xn, r, nw_ref[...], _row(mod_ref, 1))
        dx_ref[...] = dxm_ref[...] + dx
        _acc_rows(dmod_ref, R.first_of_stream(i), {0: dsh, 1: dsc})
        _acc_rows(dnw_ref, jnp.logical_and(b == 0, i == 0), {0: dnw})

    return R.call("in_bwd", body,
                  xin + [(dxm, R.row(D)), (mod, R.mod()), (nw1, R.const((1, D))), (dz, R.row(SSD_IN)), (dpre, R.row(XBC)),
                   (dpre, R.prev8(XBC)), (dpre, R.next8(XBC)), (dqa, R.row(QL)), (dkva, R.row(QL)), (dpool, R.row(PD)), (dkr, R.row(HP)),
                   (ddtr, R.row(HP)), (conv_w8, R.const((8, XBC))), (w_arr, R.const((D, PC)))],
                  [dx_out, (_sds((B, T, PC), BF), R.row(PC)), (_sds((B, 2, 8, D), F32), R.mod()),
                   (_sds((8, D), F32), R.const((8, D)))], extra=12 << 20)


def _adaln_fwd(cs, mod_w):
    L, _, C = mod_w.shape

    def body(c_ref, w_ref, o_ref):
        s = _silu(c_ref[...]).astype(BF)
        for l in range(L):
            o_ref[l] = jnp.dot(s, w_ref[l].astype(BF), preferred_element_type=F32)

    return pl.pallas_call(functools.partial(body), name="adaln_fwd", out_shape=_sds((L, 24, C), F32),
                          compiler_params=_params(None, _nbytes(mod_w.shape, F32) + _nbytes((L, 24, C), F32), 8 << 20))(cs, mod_w)


def _adaln_bwd(cs, dm, mod_w):
    L, _, C = mod_w.shape

    def body(c_ref, dm_ref, w_ref, gw_ref, gc_ref):
        c = c_ref[...]
        s = _silu(c).astype(BF)
        acc = jnp.zeros((24, D), F32)
        for l in range(L):
            dmb = dm_ref[l].astype(BF)
            gw_ref[l] = _dot_tn(s, dmb)
            acc = acc + _dot_nt(dmb, w_ref[l])
        gc_ref[...] = acc * _dsilu(c)

    return pl.pallas_call(functools.partial(body), name="adaln_bwd", out_shape=[_sds((L, D, C), F32), _sds((24, D), F32)],
                          compiler_params=_params(None, 2 * _nbytes(mod_w.shape, F32), 8 << 20))(cs, dm, mod_w)


def _sum_blocks(name, parts):
    Pn, Rr, C = parts.shape

    def body(p_ref, o_ref):
        acc = p_ref[0]
        for k in range(1, Pn):
            acc = acc + p_ref[k]
        o_ref[...] = acc

    return pl.pallas_call(functools.partial(body), name=name, out_shape=_sds((Rr, C), F32),
                          compiler_params=_params(None, _nbytes(parts.shape, F32), 4 << 20))(parts)


def _adamw(name, parts, w, m, v, rt):
    nseg = len(parts)
    Pn, rseg, C = parts[0].shape
    Rr = nseg * rseg
    tiles = rseg // rt
    assert rseg % rt == 0 and w.shape == (Rr, C)
    c1 = 1.0 - ADAM_B1 ** ADAM_STEP
    c2 = 1.0 - ADAM_B2 ** ADAM_STEP

    def body(*refs):
        p_refs = refs[:nseg]
        w_ref, m_ref, v_ref, g_ref, d_ref, nm_ref, nv_ref = refs[nseg:]
        i = pl.program_id(0)
        g = None
        for s, p_ref in enumerate(p_refs):
            gs = p_ref[0].astype(F32)
            for k in range(1, Pn):
                gs = gs + p_ref[k].astype(F32)
            g = gs if g is None else jnp.where(i >= s * tiles, gs, g)
        mn = ADAM_B1 * m_ref[...] + (1.0 - ADAM_B1) * g
        vn = ADAM_B2 * v_ref[...] + (1.0 - ADAM_B2) * jnp.square(g)
        g_ref[...] = g
        nm_ref[...] = mn
        nv_ref[...] = vn
        d_ref[...] = -ADAM_LR * ((mn / c1) / (jnp.sqrt(vn / c2) + ADAM_EPS) + ADAM_WD * w_ref[...])

    spec = pl.BlockSpec((rt, C), lambda i: (i, 0))
    pspec = lambda s: pl.BlockSpec((Pn, rt, C), lambda i: (0, jnp.clip(i - s * tiles, 0, tiles - 1), 0))
    bb = nseg * Pn * _nbytes((rt, C), parts[0].dtype) + 7 * _nbytes((rt, C), F32)
    return pl.pallas_call(
        functools.partial(body), name=name, grid=(Rr // rt,),
        in_specs=[pspec(s) for s in range(nseg)] + [spec, spec, spec],
        out_specs=[spec] * 4, out_shape=[_out((Rr, C), F32)] * 4,
        compiler_params=_params(("arbitrary",), bb, 4 << 20),
    )(*_hbm(list(parts) + [w, m, v]))


MESH = pl.DeviceIdType.MESH


def _my_pos():
    return lax.axis_index("x"), lax.axis_index("y"), lax.axis_index("c")


def _dev_index(x, y, c):
    return 4 * x + 2 * y + c


def _all_gather(name, shards):
    n = len(shards)

    def body(*refs):
        ins, outs = refs[:n], refs[n:2 * n]
        send_sems, recv_sems, local_sem = refs[2 * n:]
        x, y, c = _my_pos()
        me, sibling = (x, y, c), (x, y, 1 - c)
        chips = [(1 - x, y), (x, 1 - y), (1 - x, 1 - y)]

        def copy(t, k, block, to, src=None):
            slot = outs[t].at[_dev_index(*block)]
            return pltpu.make_async_remote_copy(
                src_ref=slot if src is None else src, dst_ref=slot,
                send_sem=send_sems.at[t, k], recv_sem=recv_sems.at[t, k], device_id=to, device_id_type=MESH)

        mine = [pltpu.make_async_copy(ins[t], outs[t].at[_dev_index(*me)], local_sem.at[t]) for t in range(n)]
        for cp in mine:
            cp.start()
        first = []
        for t in range(n):
            first.append(copy(t, 0, me, sibling, src=ins[t]))
            first += [copy(t, 1 + j, me, (*chip, c), src=ins[t]) for j, chip in enumerate(chips)]
        for cp in first:
            cp.start()
        passed = []
        for j, chip in enumerate(chips):
            for t in range(n):
                copy(t, 1 + j, (*chip, c), me).wait_recv()
                cp = copy(t, 4 + j, (*chip, c), sibling)
                cp.start()
                passed.append(cp)
        for t in range(n):
            copy(t, 0, sibling, me).wait_recv()
            for j, chip in enumerate(chips):
                copy(t, 4 + j, (*chip, 1 - c), me).wait_recv()
        for cp in first + passed:
            cp.wait_send()
        for cp in mine:
            cp.wait()

    return pl.pallas_call(
        functools.partial(body), name=name,
        in_specs=[ANY] * n, out_specs=[ANY] * n,
        out_shape=[_sds((NDEV,) + s.shape, s.dtype) for s in shards],
        scratch_shapes=[pltpu.SemaphoreType.DMA((n, 7)), pltpu.SemaphoreType.DMA((n, 7)), pltpu.SemaphoreType.DMA((n,))],
    )(*shards)


def _all_to_all(name, parts):
    n = len(parts)

    def body(*refs):
        ins, outs = refs[:n], refs[n:2 * n]
        send_sems, recv_sems, local_sem = refs[2 * n:]
        x, y, c = _my_pos()
        me = _dev_index(x, y, c)
        peers = [(x ^ ((k >> 2) & 1), y ^ ((k >> 1) & 1), c ^ (k & 1)) for k in range(1, NDEV)]
        mine = [pltpu.make_async_copy(ins[t].at[me], outs[t].at[me], local_sem.at[t]) for t in range(n)]
        for cp in mine:
            cp.start()
        sends = []
        for t in range(n):
            for k, peer in enumerate(peers):
                cp = pltpu.make_async_remote_copy(
                    src_ref=ins[t].at[_dev_index(*peer)], dst_ref=outs[t].at[me],
                    send_sem=send_sems.at[t, k], recv_sem=recv_sems.at[t, k], device_id=peer, device_id_type=MESH)
                cp.start()
                sends.append(cp)
        for t in range(n):
            for k, peer in enumerate(peers):
                slot = outs[t].at[_dev_index(*peer)]
                pltpu.make_async_remote_copy(
                    src_ref=slot, dst_ref=slot, send_sem=send_sems.at[t, k], recv_sem=recv_sems.at[t, k],
                    device_id=peer, device_id_type=MESH).wait_recv()
        for cp in sends:
            cp.wait_send()
        for cp in mine:
            cp.wait()

    return pl.pallas_call(
        functools.partial(body), name=name,
        in_specs=[ANY] * n, out_specs=[ANY] * n,
        out_shape=[_sds(p.shape, p.dtype) for p in parts],
        scratch_shapes=[pltpu.SemaphoreType.DMA((n, 7)), pltpu.SemaphoreType.DMA((n, 7)), pltpu.SemaphoreType.DMA((n,))],
    )(*parts)


SEM = pl.BlockSpec(memory_space=pltpu.SEMAPHORE)
IN_HBM = pl.BlockSpec(memory_space=pltpu.HBM)
DATAFLOW = pltpu.SideEffectType.DATAFLOW_SIDE_EFFECTING


def _flip_peers(x, y, c):
    return [(x ^ ((k >> 2) & 1), y ^ ((k >> 1) & 1), c ^ (k & 1)) for k in range(1, NDEV)]


def _split_copies(srcs, lands, send_sems, recv_sems, gather):
    x, y, c = _my_pos()
    me = _dev_index(x, y, c)
    out = []
    for t in range(len(srcs)):
        for k, peer in enumerate(_flip_peers(x, y, c)):
            p = _dev_index(*peer)
            src = srcs[t] if gather else srcs[t].at[p]
            sems = dict(send_sem=send_sems.at[7 * t + k], recv_sem=recv_sems.at[7 * t + k], device_id=peer, device_id_type=MESH)
            out.append((pltpu.make_async_remote_copy(src_ref=src, dst_ref=lands[t].at[me], **sems),
                        pltpu.make_async_remote_copy(src_ref=src, dst_ref=lands[t].at[p], **sems)))
    return out


def _exchange_start(name, collective_id, srcs, gather):
    n = len(srcs)
    lands = [lax.empty(((NDEV,) + s.shape) if gather else s.shape, s.dtype) for s in srcs]

    def body(*refs):
        src_refs, land_refs = refs[:n], refs[n:2 * n]
        send_sems, recv_sems = refs[2 * n], refs[2 * n + 1]
        token = refs[-1]
        barrier = pltpu.get_barrier_semaphore()
        for peer in _flip_peers(*_my_pos()):
            pl.semaphore_signal(barrier, inc=1, device_id=peer, device_id_type=MESH)
        pl.semaphore_wait(barrier, NDEV - 1)
        for send, _ in _split_copies(src_refs, land_refs, send_sems, recv_sems, gather):
            send.start()
        token[...] = jnp.zeros(token.shape, token.dtype)

    hbm = lambda a: pltpu.HBM(a.shape, a.dtype)
    res = pl.pallas_call(
        functools.partial(body), name=name,
        out_shape=[pltpu.SemaphoreType.DMA((7 * n,)), pltpu.SemaphoreType.DMA((7 * n,))] + [hbm(s) for s in srcs]
        + [hbm(a) for a in lands] + [_sds((8, HP), F32)],
        in_specs=[IN_HBM] * (2 * n), out_specs=[SEM, SEM] + [IN_HBM] * (2 * n) + [pl.BlockSpec(memory_space=pltpu.VMEM)],
        input_output_aliases={i: 2 + i for i in range(2 * n)},
        compiler_params=pltpu.CompilerParams(has_side_effects=DATAFLOW, collective_id=collective_id),
    )(*_hbm(list(srcs) + lands))
    return res[0], res[1], list(res[2:2 + n]), list(res[2 + n:2 + 2 * n]), res[-1]


def _exchange_wait(name, send_sems, recv_sems, srcs, lands, after, gather):
    n = len(srcs)

    def body(*refs):
        src_refs, land_refs = refs[:n], refs[n:2 * n]
        for _, recv in _split_copies(src_refs, land_refs, refs[2 * n], refs[2 * n + 1], gather):
            recv.wait_send()
            recv.wait_recv()

    hbm = lambda a: pltpu.HBM(a.shape, a.dtype)
    res = pl.pallas_call(
        functools.partial(body), name=name,
        out_shape=[hbm(s) for s in srcs] + [hbm(a) for a in lands],
        in_specs=[IN_HBM] * (2 * n) + [SEM, SEM, ANY], out_specs=[IN_HBM] * (2 * n),
        input_output_aliases={i: i for i in range(2 * n)},
        compiler_params=pltpu.CompilerParams(has_side_effects=DATAFLOW),
    )(*srcs, *lands, send_sems, recv_sems, after)
    return list(res[:n]), list(res[n:])


def _with_own(lands, own_blocks, me):
    out = []
    for land, own in zip(lands, own_blocks):
        out.append(lax.dynamic_update_slice(land, own[None], (me,) + (0,) * own.ndim))
    return out


def _arrange_w_in(w):
    z = lambda n: jnp.zeros((w.shape[0], n), w.dtype)
    return jnp.concatenate([w[:, 0:1280], w[:, 1292:1548], w[:, 1548:1804], w[:, 1836:2092],
                            z(64), w[:, 1804:1836], z(32), w[:, 1280:1292], z(HP - 2 * H)], axis=1)


def _unarrange_w_in(g):
    return jnp.concatenate([g[:, 0:1280], g[:, A_DT:A_DT + 2 * H], g[:, A_QA:A_KVA], g[:, A_KVA:A_POOL],
                            g[:, A_KR + NOPE:A_KR + NOPE + ROPE], g[:, A_POOL:A_KR]], axis=1)


def _pad_heads(w, width):
    k = w.shape[0]
    return jnp.pad(w.reshape(k, H, width), ((0, 0), (0, 0), (0, HP - width))).reshape(k, H * HP)


def _unpad_heads(g, width):
    k = g.shape[0]
    return g.reshape(k, H, HP)[:, :, :width].reshape(k, H * width)


def _arrange_w_out(w):
    att = jnp.pad(w[SSD_IN:2 * SSD_IN].reshape(H, VH, D), ((0, 0), (0, HP - VH), (0, 0))).reshape(QW, D)
    return jnp.concatenate([w[0:SSD_IN], att, w[2 * SSD_IN:]], axis=0)


def _unarrange_w_out(g):
    att = g[SSD_IN:SSD_IN + QW].reshape(H, HP, D)[:, :VH].reshape(SSD_IN, D)
    return jnp.concatenate([g[0:SSD_IN], att, g[SSD_IN + QW:]], axis=0)


def _rope_tables(T):
    n = T - CTX
    rows = n // GRID_W
    pairs = ROPE // 4
    inv = ROPE_THETA ** (-jnp.arange(pairs, dtype=F32) / pairs)
    ar = jnp.arange(rows, dtype=F32)[:, None] * inv
    ac = jnp.arange(GRID_W, dtype=F32)[:, None] * inv
    by_row = lambda a: jnp.repeat(a, GRID_W, axis=0)
    by_col = lambda a: jnp.tile(a, (rows, 1))
    cos = jnp.concatenate([by_row(jnp.cos(ar))] * 2 + [by_col(jnp.cos(ac))] * 2, axis=1)
    sin = jnp.concatenate([-by_row(jnp.sin(ar)), by_row(jnp.sin(ar)), -by_col(jnp.sin(ac)), by_col(jnp.sin(ac))], axis=1)
    ones, zeros = jnp.ones((n, NOPE), F32), jnp.zeros((n, NOPE), F32)
    cos = jnp.concatenate([ones, cos, ones[:, :HP - QK]], axis=1)
    sin = jnp.concatenate([zeros, sin, zeros[:, :HP - QK]], axis=1)
    return (jnp.concatenate([jnp.ones((CTX, HP), F32), cos], axis=0),
            jnp.concatenate([jnp.zeros((CTX, HP), F32), sin], axis=0))


def _lane_pad(v, n):
    return jnp.pad(v, (0, n - v.shape[0]))[None, :]


def _layer_weights(w_in, w_q_b, w_kv_b, conv_w, pool_w):
    kv = w_kv_b.reshape(QL, H, NOPE + VH)
    wbd = jnp.concatenate([jnp.pad(pool_w[g], ((0, 0), (64 * g, PD - 64 * (g + 1)))) for g in range(4)], axis=0)
    return dict(
        w_in=_arrange_w_in(w_in).astype(BF),
        wq=_pad_heads(w_q_b, QK).astype(BF),
        wk=_pad_heads(kv[:, :, :NOPE].reshape(QL, H * NOPE), NOPE).astype(BF),
        wv=_pad_heads(kv[:, :, NOPE:].reshape(QL, H * VH), VH).astype(BF),
        conv_w8=jnp.pad(conv_w, ((0, 4), (0, 0))), wbd=wbd)


def _layer_fwd(R, x, mod, lw, sp, late, cos, sin):
    B, T = R.B, R.T
    z, xbc_raw, qa, kva, pool_in, kr, dt_raw, h1 = _in_proj(R, x, mod, sp["nw1"], lw["w_in"])
    xbc, dt2 = _ssd_prep(R, xbc_raw, dt_raw, lw["conv_w8"], sp["conv_b"], sp["dtb"])
    yf, yb, hin_f, hin_b = _ssd_scan(B, T, xbc, dt2, sp["alog2"])
    qt, kt, q_t, k_t, v_t, cq, ckv = _mla_prep(R, qa, kva, kr, sp["qnw"], sp["kvnw"], lw["wq"], lw["wk"], lw["wv"], cos, sin)
    o, lse = _flash_fwd(B, T, q_t, kt, v_t)
    w_out, w1, w2 = late(o)
    w_out = _arrange_w_out(w_out).astype(BF)
    xmid, cat, mixbf = _out_proj(R, x, mod, yf, yb, xbc, z, o, pool_in, lw["wbd"], sp["pscale"], sp["dsk"], sp["snw"], w_out)
    xo, h2, ubf, ybf = _mlp_fwd(R, xmid, mod, sp["nw2"], w1, w2)
    saved = dict(x=x, z=z, xbc_raw=xbc_raw, qa=qa, kva=kva, pool_in=pool_in, dt_raw=dt_raw, h1=h1, xbc=xbc, dt2=dt2,
                 yf=yf, yb=yb, hin_f=hin_f, hin_b=hin_b, qt=qt, kt=kt, q_t=q_t, k_t=k_t, v_t=v_t, cq=cq, ckv=ckv, o=o, lse=lse,
                 cat=cat, mixbf=mixbf, xmid=xmid, h2=h2, ubf=ubf, ybf=ybf, w_out=w_out, w1=w1, w2=w2)
    return xo, saved


def _layer_bwd(R, dxo, sv, mod, lw, sp, cos, sin, on_mlp=None, on_mid=None, latent_only=False):
    B, T = R.B, R.T
    dxm, du, abf, dyb, dmod_a, dnw2 = _mlp_bwd(R, dxo, sv["xmid"], sv["ubf"], sv["ybf"], mod, sp["nw2"], sv["w1"], sv["w2"])
    g_w1 = _tn_matmul("dw_mlp1", sv["h2"], du, 4 * FSH, sub=4)
    g_w2 = _tn_matmul("dw_mlp2", abf, dyb, D)[0].reshape(NDEV, FSH, D)
    snw = sp["snw"]
    tok = on_mlp(g_w1, g_w2) if on_mlp is not None else None
    if tok is not None:
        snw = snw + tok
    dmix, dz, dyt, dxsk, qs, dos, do_t, dpo, dmod_b, dvec_o = _out_bwd(R, dxm, mod, sv["mixbf"], sv["yf"], sv["yb"], sv["xbc"], sv["z"],
                                                                 sv["o"], sv["qt"], sv["lse"], sp["dsk"], snw, sv["w_out"])
    g_wout = _unarrange_w_out(_tn_matmul("dw_out", sv["cat"], dmix, D)[0])
    dpool_in, g_wbd, dpsc = _pool_bwd(R, dpo, sv["pool_in"], lw["wbd"], sp["pscale"])
    dqt, dkt, dvt = _flash_bwd(B, T, qs, sv["kt"], dos, sv["q_t"], sv["k_t"], sv["v_t"], do_t)
    dqp, dkvb, dqa, dkva, dkr, dnw_qk = _mla_bwd(R, dqt, dkt, dvt, sv["qa"], sv["kva"], sp["qnw"], sp["kvnw"],
                                                 lw["wq"], lw["wk"], lw["wv"], cos, sin)
    g_wq = _unpad_heads(_tn_matmul("dw_q", sv["cq"], dqp, QW)[0], QK)
    g_kv = _tn_matmul("dw_kv", sv["ckv"], dkvb, 2 * QW)[0]
    g_wk = _unpad_heads(g_kv[:, :QW], NOPE).reshape(QL, H, NOPE)
    g_wv = _unpad_heads(g_kv[:, QW:], VH).reshape(QL, H, VH)
    g_wkv = jnp.concatenate([g_wk, g_wv], axis=2).reshape(QL, H * (NOPE + VH))
    alog2 = sp["alog2"]
    tok = on_mid(g_wout, g_wq, g_wkv) if on_mid is not None else None
    if tok is not None:
        alog2 = alog2 + tok
    dxbc_f, dxbc_b, ddt_f, ddt_b, da = _ssd_scan_bwd(B, T, dyt, sv["xbc"], sv["dt2"], alog2, sv["hin_f"], sv["hin_b"])
    dpre, ddtr, dcw, dcb, ddtb = _ssd_prep_bwd(R, dxbc_f, dxbc_b, dxsk, ddt_f, ddt_b, sv["xbc_raw"], sv["dt_raw"], lw["conv_w8"],
                                                sp["conv_b"], sp["dtb"])
    dx, dproj, dmod_c, dnw1 = _in_bwd(R, dxm, sv["x"], mod, sp["nw1"], dz, dpre, dqa, dkva, dpool_in, dkr, ddtr,
                                      lw["conv_w8"], lw["w_in"], latent_only)
    g_win = _unarrange_w_in(jnp.concatenate(list(_tn_matmul("dw_in", sv["h1"], dproj, PC)), axis=1))
    a2 = -jnp.exp(sp["alog2"][:, 0, :H])
    small = dict(
        norm1_w=dnw1[0], norm2_w=dnw2[0], conv_w=dcw[0:4], conv_b=dcb[0], dt_bias=ddtb[0, :2 * H].reshape(2, H),
        a_log=jnp.sum(da[:, :, 0, :H], axis=0) * a2, ssd_d=jnp.sum(dvec_o[1].reshape(H, P), axis=1), ssd_norm_w=dvec_o[0],
        q_a_norm_w=dnw_qk[0], kv_a_norm_w=dnw_qk[1],
        pool_w=jnp.stack([g_wbd[64 * g:64 * (g + 1), 64 * g:64 * (g + 1)] for g in range(4)]), pool_scale=dpsc[0])
    big = dict(w_in=g_win, w_out=g_wout, w_q_b=g_wq, w_kv_b=g_wkv, w_mlp1=g_w1, w_mlp2=g_w2)
    return dx, big, small, dmod_a + dmod_b + dmod_c


def _small_params(l, norm1_w, norm2_w, conv_b, dt_bias, a_log, ssd_d, ssd_norm_w, q_a_norm_w, kv_a_norm_w, pool_scale):
    alog2 = jnp.broadcast_to(jnp.pad(a_log[l], ((0, 0), (0, HP - H)))[:, None, :], (2, 8, HP))
    return dict(nw1=norm1_w[l][None], nw2=norm2_w[l][None], conv_b=conv_b[l][None],
                dtb=_lane_pad(dt_bias[l].reshape(2 * H), HP), alog2=alog2,
                dsk=jnp.repeat(ssd_d[l], P)[None], snw=ssd_norm_w[l][None], qnw=q_a_norm_w[l][None],
                kvnw=kv_a_norm_w[l][None], pscale=pool_scale[l][None])


SMALL_NAMES = ["mod_b", "norm1_w", "norm2_w", "conv_b", "dt_bias", "a_log", "ssd_d", "ssd_norm_w", "q_a_norm_w",
               "kv_a_norm_w", "pool_w", "pool_scale", "final_norm_w"]


def _pack(arrs):
    rows = []
    for a in arrs:
        f = a.reshape(-1).astype(F32)
        n = -(-f.shape[0] // HP) * HP
        rows.append(jnp.pad(f, (0, n - f.shape[0])).reshape(-1, HP))
    out = jnp.concatenate(rows, axis=0)
    pad = (-out.shape[0]) % 8
    return jnp.pad(out, ((0, pad), (0, 0)))


def _unpack(pack, like):
    outs, r = [], 0
    for a in like:
        n = math.prod(a.shape)
        nr = -(-n // HP)
        outs.append(pack[r:r + nr].reshape(-1)[:n].reshape(a.shape))
        r += nr
    return outs


def _local_step(x, ctx, target, mods, full_of, small_w, on_grads=None, on_mlp=None, on_mid=None):
    B, N = x.shape[0], x.shape[1]
    T = CTX + N
    R = _Rows(B, T)
    cos, sin = _rope_tables(T)
    xu = (ctx, x)
    L = len(mods)
    lws, sps, saves = [], [], []
    for l in range(L):
        f = full_of(l, xu)
        lws.append(_layer_weights(f["w_in"], f["w_q_b"], f["w_kv_b"], f["conv_w"], small_w["pool_w"][l]))
        sps.append(_small_params(l, *[small_w[k] for k in ["norm1_w", "norm2_w", "conv_b", "dt_bias", "a_log", "ssd_d",
                                                          "ssd_norm_w", "q_a_norm_w", "kv_a_norm_w", "pool_scale"]]))
        xu, sv = _layer_fwd(R, xu, mods[l], lws[l], sps[l], f["late"], cos, sin)
        saves.append(sv)
    dx, loss8, dfnw = _loss_head(R, xu, target, small_w["final_norm_w"][None])
    bigs, smalls, dmods = [None] * L, [None] * L, [None] * L
    for l in reversed(range(L)):
        hook = functools.partial(on_mlp, l) if on_mlp is not None else None
        hook2 = functools.partial(on_mid, l) if on_mid is not None else None
        dx, bigs[l], smalls[l], dmods[l] = _layer_bwd(R, dx, saves[l], mods[l], lws[l], sps[l], cos, sin, hook, hook2, l == 0)
        if on_grads is not None:
            bigs[l], tok = on_grads(l, bigs[l], dx)
            if tok is not None:
                sps[l - 1] = dict(sps[l - 1], nw2=sps[l - 1]["nw2"] + tok)
    return loss8[0, 0], dx, bigs, smalls, dfnw[0], dmods


def kernel(x, c, ctx, c_ctx, mod_w, mod_b, norm1_w, norm2_w, w_in, conv_w, conv_b, dt_bias, a_log, ssd_d, ssd_norm_w, q_a_norm_w, w_q_b, kv_a_norm_w, w_kv_b, pool_w, pool_scale, w_out, w_mlp1, w_mlp2, final_norm_w, loss_target, m_c_ctx, m_mod_w, m_mod_b, m_norm1_w, m_norm2_w, m_w_in, m_conv_w, m_conv_b, m_dt_bias, m_a_log, m_ssd_d, m_ssd_norm_w, m_q_a_norm_w, m_w_q_b, m_kv_a_norm_w, m_w_kv_b, m_pool_w, m_pool_scale, m_w_out, m_w_mlp1, m_w_mlp2, m_final_norm_w, v_c_ctx, v_mod_w, v_mod_b, v_norm1_w, v_norm2_w, v_w_in, v_conv_w, v_conv_b, v_dt_bias, v_a_log, v_ssd_d, v_ssd_norm_w, v_q_a_norm_w, v_w_q_b, v_kv_a_norm_w, v_w_kv_b, v_pool_w, v_pool_scale, v_w_out, v_w_mlp1, v_w_mlp2, v_final_norm_w):
    args = dict(locals())
    B = x.shape[0]
    L = mod_w.shape[0]
    me = _dev_index(*_my_pos())
    CS = mod_w.shape[2]

    big_names = ["w_in", "w_out", "w_q_b", "w_kv_b", "w_mlp1", "w_mlp2"]
    shards = {n: args[n].astype(BF) for n in big_names}
    early, late_names = ["w_in", "w_q_b", "w_kv_b"], ["w_out", "w_mlp1", "w_mlp2"]
    g0 = _all_gather("gather_weights", [c, conv_w] + [shards[n][0] for n in early])
    c_all, convw_all = g0[0], g0[1]
    gathered = {0: dict(zip(early, g0[2:]))}
    cs = jnp.concatenate([c_all.reshape(NDEV * B, D), c_ctx[None], jnp.zeros((24 - NDEV * B - 1, D), F32)], axis=0)
    m_loc = _adaln_fwd(cs, mod_w)
    m_all = _all_gather("gather_mod", [m_loc])[0]
    m_full = jnp.moveaxis(m_all, 0, 2).reshape(L, 24, NDEV * CS) + mod_b[:, None, :]
    pending = {}
    tok = jnp.zeros((), F32)
    res = _exchange_start("gather_start_0", 1, [shards[n][0] for n in late_names], gather=True)
    pending[0] = res[:4]
    tok = tok + res[4][0, 0]
    for l in range(1, L):
        res = _exchange_start("gather_start_%d" % l, 1 + l, [shards[n][l] for n in early + late_names], gather=True)
        pending[l] = res[:4]
        tok = tok + res[4][0, 0]
    mods = []
    for l in range(L):
        ex = lax.dynamic_slice(m_full[l], (me * B, 0), (B, 6 * D)).reshape(B, 6, D)
        cc = jnp.broadcast_to(m_full[l, NDEV * B].reshape(1, 6, D), (B, 6, D))
        mods.append(jnp.pad(jnp.stack([cc, ex], axis=1), ((0, 0), (0, 0), (0, 2), (0, 0))) + tok)

    def full_of(l, xu):
        if l > 0:
            own, lands = _exchange_wait("gather_wait_%d" % l, *pending.pop(l), xu, gather=True)
            gathered[l] = dict(zip(early + late_names, _with_own(lands, own, me)))

        def late(after):
            if l == 0:
                own, lands = _exchange_wait("gather_wait_0", *pending.pop(0), after, gather=True)
                gathered[0].update(zip(late_names, _with_own(lands, own, me)))
            gl = gathered[l]
            return gl["w_out"].reshape(D, D), gl["w_mlp1"], gl["w_mlp2"]

        g = gathered[l]
        return dict(
            w_in=g["w_in"].reshape(D, IN_COLS),
            w_q_b=jnp.moveaxis(g["w_q_b"], 0, 1).reshape(QL, H * QK),
            w_kv_b=jnp.moveaxis(g["w_kv_b"], 0, 1).reshape(QL, H * (NOPE + VH)),
            conv_w=jnp.moveaxis(convw_all[:, l], 0, 1).reshape(4, XBC), late=late)

    def grad_blocks(big, names):
        make = dict(
            w_in=lambda g: g.reshape(NDEV, D // NDEV, IN_COLS), w_out=lambda g: g.reshape(NDEV, D // NDEV, D),
            w_q_b=lambda g: jnp.moveaxis(g.reshape(QL, NDEV, -1), 1, 0),
            w_kv_b=lambda g: jnp.moveaxis(g.reshape(QL, NDEV, -1), 1, 0), w_mlp1=lambda g: g, w_mlp2=lambda g: g)
        return [make[n](big[n]).astype(BF) for n in names]

    sent, sent_mlp, sent_mid = {}, {}, {}
    rest_names, mid_names = ["w_in"], ["w_out", "w_q_b", "w_kv_b"]

    def on_mid(l, g_wout, g_wq, g_wkv):
        if l > 0:
            return None
        blocks_mid = grad_blocks(dict(w_out=g_wout, w_q_b=g_wq, w_kv_b=g_wkv), mid_names)
        res = _exchange_start("grads_mid_start_0", 2 + 2 * L, blocks_mid, gather=False)
        sent_mid[l] = res[:4]
        return res[4][0, 0]

    def on_mlp(l, g_w1, g_w2):
        if l > 0:
            return None
        res = _exchange_start("grads_mlp_start_0", 1 + 2 * L, [g_w1.astype(BF), g_w2.astype(BF)], gather=False)
        sent_mlp[l] = res[:4]
        return res[4][0, 0]

    def on_grads(l, big, dx):
        if l == 0:
            return grad_blocks(big, rest_names), None
        res = _exchange_start("grads_start_%d" % l, 1 + L + l, grad_blocks(big, big_names), gather=False)
        sent[l] = res[:4]
        return None, res[4][0, 0]

    small_w = {k: args[k] for k in SMALL_NAMES if k != "mod_b"}
    loss_part, grad_x, blocks, smalls, dfnw, dmods = _local_step(x, ctx, loss_target, mods, full_of, small_w, on_grads, on_mlp,
                                                                 on_mid)
    loss = lax.psum(loss_part, ("x", "y", "c"))

    dm_ex = jnp.stack([dmods[l][:, 1, :6].reshape(B, 6 * D) for l in range(L)])
    dm_cc = jnp.stack([jnp.sum(dmods[l][:, 0, :6], axis=0).reshape(6 * D) for l in range(L)])
    small_parts = dict(
        mod_b=jnp.sum(dm_ex, axis=1) + dm_cc,
        **{k: jnp.stack([smalls[l][k] for l in range(L)]) for k in SMALL_NAMES[1:-1]},
        final_norm_w=dfnw, conv_w=jnp.stack([smalls[l]["conv_w"] for l in range(L)]), dm_cc=dm_cc)
    adam_grads = [small_parts[k] for k in SMALL_NAMES]
    extras = [small_parts["conv_w"], dm_cc, dm_ex]
    pack = jnp.concatenate([_pack(adam_grads), _pack(extras)], axis=0)
    pack_all = _all_gather("gather_small_grads", [pack])[0]
    wpack = _pack([args[k] for k in SMALL_NAMES])
    mpack = _pack([args["m_" + k] for k in SMALL_NAMES])
    vpack = _pack([args["v_" + k] for k in SMALL_NAMES])
    n_adam = wpack.shape[0]
    res_small = _adamw("adamw_small", [pack_all[:, :n_adam]], wpack, mpack, vpack, n_adam)
    small_out = [_unpack(r, [args[k] for k in SMALL_NAMES]) for r in res_small]
    ext_all = pack_all[:, n_adam:]
    g_conv_full, dm_cc_tot, _ = _unpack(_sum_blocks("sum_small_grads", ext_all), extras)
    dm_ex_all = jnp.stack([_unpack(ext_all[k], extras)[2] for k in range(NDEV)], axis=1)
    dm_rows = jnp.concatenate([dm_ex_all.reshape(L, NDEV * B, 6 * D), dm_cc_tot[:, None, :],
                               jnp.zeros((L, 24 - NDEV * B - 1, 6 * D), F32)], axis=1)
    dm_loc = lax.dynamic_slice(dm_rows, (0, 0, me * CS), (L, 24, CS))
    g_modw, gc_part = _adaln_bwd(cs, dm_loc, mod_w)
    gc_all = _all_gather("gather_cctx_grad", [gc_part[NDEV * B:NDEV * B + 8]])[0]
    cpad = lambda a: jnp.pad(a[None], ((0, 7), (0, 0)))
    res_cc = _adamw("adamw_cctx", [gc_all], cpad(c_ctx), cpad(m_c_ctx), cpad(v_c_ctx), 8)
    cc_out = [r[0] for r in res_cc]

    def waited(name, handles, names):
        srcs, lands = _exchange_wait(name, *handles, grad_x, gather=False)
        own = [lax.dynamic_index_in_dim(s, me, 0, keepdims=False) for s in srcs]
        return dict(zip(names, _with_own(lands, own, me)))

    recv = {0: dict(zip(rest_names, _all_to_all("exchange_grads", blocks[0])))}
    recv[0].update(waited("grads_mlp_wait_0", sent_mlp.pop(0), ["w_mlp1", "w_mlp2"]))
    recv[0].update(waited("grads_mid_wait_0", sent_mid.pop(0), mid_names))
    for l in range(1, L):
        recv[l] = waited("grads_wait_%d" % l, sent.pop(l), big_names)
    big_out = {}
    for name in big_names:
        w = args[name]
        Rr, C = math.prod(w.shape[:-1]), w.shape[-1]
        rl = Rr // L
        rt = rl if rl * C <= (1 << 17) else rl // 4
        res = _adamw("adamw_" + name, [recv[l][name].reshape(NDEV, rl, C) for l in range(L)], w.reshape(Rr, C),
                     args["m_" + name].reshape(Rr, C), args["v_" + name].reshape(Rr, C), rt)
        big_out[name] = [r.reshape(w.shape) for r in res]
    res = _adamw("adamw_mod_w", [g_modw.reshape(1, L * D, CS)], mod_w.reshape(L * D, CS), m_mod_w.reshape(L * D, CS),
                 v_mod_w.reshape(L * D, CS), L * D // 8)
    big_out["mod_w"] = [r.reshape(mod_w.shape) for r in res]
    CW = conv_w.shape[2]
    g_conv = lax.dynamic_slice(g_conv_full, (0, 0, me * CW), (L, 4, CW))
    res = _adamw("adamw_conv_w", [g_conv.reshape(1, L * 4, CW)], conv_w.reshape(L * 4, CW), m_conv_w.reshape(L * 4, CW),
                 v_conv_w.reshape(L * 4, CW), L * 4)
    big_out["conv_w"] = [r.reshape(conv_w.shape) for r in res]

    weights = ["c_ctx", "mod_w", "mod_b", "norm1_w", "norm2_w", "w_in", "conv_w", "conv_b", "dt_bias", "a_log", "ssd_d",
               "ssd_norm_w", "q_a_norm_w", "w_q_b", "kv_a_norm_w", "w_kv_b", "pool_w", "pool_scale", "w_out", "w_mlp1",
               "w_mlp2", "final_norm_w"]
    outs = [loss, grad_x]
    for kind in range(4):
        for name in weights:
            if name == "c_ctx":
                outs.append(cc_out[kind])
            elif name in big_out:
                outs.append(big_out[name][kind])
            else:
                outs.append(small_out[kind][SMALL_NAMES.index(name)])
    return tuple(outs)
```

```python
import functools
import math

import jax
import jax.numpy as jnp
from jax import lax
from jax.experimental import pallas as pl
from jax.experimental.pallas import tpu as pltpu

F32 = jnp.float32
BF = jnp.bfloat16
MXU = BF

D = 1024
CTX = 256
GRID_W = 64
EPS = 1e-6
H = 6
P = 64
SSD_IN = 384
NST = 128
XBC = 896
CH = 128
QL = 256
NOPE = 64
ROPE = 32
VH = 64
QK = 96
HP = 128
QW = H * HP
PD = 256
FF = 4096
IN_COLS = 2092
ROPE_THETA = 10000.0
PC = 2304
A_Z, A_XBC, A_QA, A_KVA, A_POOL, A_KR, A_DT = 0, 384, 1280, 1536, 1792, 2048, 2176
MIXW = SSD_IN + QW + PD
NDEV = 8
FSH = FF // NDEV
TM = 256
TQ = 256
TK = 256
VMEM_CAP = 64 * 1024 * 1024
ADAM_LR, ADAM_B1, ADAM_B2, ADAM_EPS, ADAM_WD, ADAM_STEP = 0.001, 0.9, 0.999, 1e-08, 0.01, 10


def _nbytes(shape, dtype):
    n = 1
    for s in shape:
        if s is not None:
            n *= s
    return n * jnp.dtype(dtype).itemsize


def _params(sem, block_bytes, extra=0):
    lim = min(2 * block_bytes + extra + (8 << 20), VMEM_CAP - (6 << 20))
    return pltpu.CompilerParams(dimension_semantics=sem, vmem_limit_bytes=int(lim))


def _hbm(arrays):
    return [pltpu.with_memory_space_constraint(a, pltpu.HBM) for a in arrays]


def _dot(a, b):
    return jnp.dot(a.astype(MXU), b.astype(MXU), preferred_element_type=F32)


def _dot_nt(a, b):
    return lax.dot_general(a.astype(MXU), b.astype(MXU), (((1,), (1,)), ((), ())), preferred_element_type=F32)


def _dot_tn(a, b):
    return lax.dot_general(a.astype(MXU), b.astype(MXU), (((0,), (0,)), ((), ())), preferred_element_type=F32)


def _dot01(m01, x):
    b16 = jnp.bfloat16
    m = m01.astype(b16)
    hi = x.astype(b16)
    r1 = x - hi.astype(F32)
    mid = r1.astype(b16)
    lo = (r1 - mid.astype(F32)).astype(b16)
    f = lambda v: jnp.dot(m, v, preferred_element_type=F32)
    return f(hi) + f(mid) + f(lo)


def _sigmoid(x):
    return 1.0 / (1.0 + jnp.exp(-x))


def _silu(x):
    return x * _sigmoid(x)


def _dsilu(x):
    s = _sigmoid(x)
    return s * (1.0 + x * (1.0 - s))


def _iota(shape, dim):
    return lax.broadcasted_iota(jnp.int32, shape, dim)


def _row(ref, k):
    blk = ref[...]
    return jnp.sum(jnp.where(_iota(blk.shape, 0) == k, blk, 0.0), axis=0, keepdims=True)


def _shift_rows(x, k):
    n = x.shape[0]
    return pltpu.roll(x, (-k) % n, axis=0)


class _Rows:
    def __init__(self, B, T):
        self.B, self.T = B, T
        self.nt = T // TM
        self.nct = CTX // TM

    def row(self, F):
        return pl.BlockSpec((None, TM, F), lambda b, i: (b, i, 0))

    def row2(self, F):
        return pl.BlockSpec((None, 2, TM, F), lambda b, i: (b, 0, i, 0))

    def prev8(self, F):
        return pl.BlockSpec((None, 8, F), lambda b, i: (b, jnp.maximum(i * (TM // 8) - 1, 0), 0))

    def next8(self, F):
        last = self.T // 8 - 1
        return pl.BlockSpec((None, 8, F), lambda b, i: (b, jnp.minimum((i + 1) * (TM // 8), last), 0))

    def mod(self):
        nct = self.nct
        return pl.BlockSpec((None, None, 8, D), lambda b, i: (b, jnp.where(i < nct, 0, 1), 0, 0))

    def const(self, shape):
        z = (0,) * len(shape)
        return pl.BlockSpec(tuple(shape), lambda b, i: z)

    def tgt(self, F):
        nct = self.nct
        return pl.BlockSpec((None, TM, F), lambda b, i: (b, jnp.maximum(i - nct, 0), 0))

    def xin(self, x):
        if isinstance(x, tuple):
            nct = self.nct
            return [(x[0], pl.BlockSpec((None, TM, D), lambda b, i: (b, jnp.minimum(i, nct - 1), 0))), (x[1], self.tgt(D))]
        return [(x, self.row(D))]

    def xval(self, i, refs):
        if len(refs) == 2:
            return jnp.where(i < self.nct, refs[0][...], refs[1][...])
        return refs[0][...]

    def call(self, name, body, ins, outs, scratch=(), extra=0):
        arrays = [a for a, _ in ins]
        in_specs = [s for _, s in ins]
        out_shape = [pltpu.HBM(o.shape, o.dtype) for o, _ in outs]
        out_specs = [s for _, s in outs]
        bb = 0
        for a, s in list(ins) + list(outs):
            if s.block_shape is not None:
                bb += _nbytes(s.block_shape, a.dtype)
        return pl.pallas_call(
            functools.partial(body), name=name, grid=(self.B, self.nt),
            in_specs=in_specs, out_specs=out_specs, out_shape=out_shape, scratch_shapes=list(scratch),
            compiler_params=_params(("arbitrary", "arbitrary"), bb, extra),
        )(*_hbm(arrays))

    def first_of_stream(self, i):
        return jnp.logical_or(i == 0, i == self.nct)

    def last_of_stream(self, i):
        return jnp.logical_or(i == self.nct - 1, i == self.nt - 1)

    def ext(self, i, prev_ref, cur, next_ref):
        pv = prev_ref[...].astype(F32) * jnp.where(self.first_of_stream(i), 0.0, 1.0)
        nx = next_ref[...].astype(F32) * jnp.where(self.last_of_stream(i), 0.0, 1.0)
        return jnp.concatenate([pv, cur, nx], axis=0)

    def stream_pos(self, i, rows):
        start = jnp.where(i < self.nct, 0, CTX)
        n = jnp.where(i < self.nct, CTX, self.T - CTX)
        t = i * TM - 8 - start + _iota((rows, 1), 0)
        return t, n


def _sds(shape, dtype):
    return jax.ShapeDtypeStruct(tuple(shape), dtype)


def _out(shape, dtype):
    return pltpu.HBM(tuple(shape), dtype)


def _norm_mod(x, nw, sh, sc):
    r = lax.rsqrt(jnp.mean(x * x, axis=-1, keepdims=True) + EPS)
    xn = x * r
    return xn * nw * (1.0 + sc) + sh, xn, r


def _norm_mod_bwd(dh, xn, r, nw, sc):
    dsh = jnp.sum(dh, axis=0, keepdims=True)
    dsc = jnp.sum(dh * (xn * nw), axis=0, keepdims=True)
    dnw = jnp.sum(dh * (1.0 + sc) * xn, axis=0, keepdims=True)
    dxn = dh * nw * (1.0 + sc)
    dx = r * (dxn - xn * jnp.mean(dxn * xn, axis=-1, keepdims=True))
    return dx, dsh, dsc, dnw


def _acc_rows(ref, first, rows):
    rid = _iota(ref.shape, 0)
    upd = jnp.zeros(ref.shape, F32)
    for k, v in rows.items():
        upd = upd + jnp.where(rid == k, v, 0.0)

    @pl.when(first)
    def _():
        ref[...] = upd

    @pl.when(jnp.logical_not(first))
    def _():
        ref[...] += upd


def _in_proj(R, x, mod, nw1, w_arr):
    B, T = R.B, R.T

    xin = R.xin(x)

    def body(*refs):
        x_refs = refs[:len(xin)]
        mod_ref, nw_ref, w_ref, z_ref, xbc_ref, qa_ref, kva_ref, pool_ref, kr_ref, dt_ref, h_ref = refs[len(xin):]
        h, _, _ = _norm_mod(R.xval(pl.program_id(1), x_refs), nw_ref[...], _row(mod_ref, 0), _row(mod_ref, 1))
        hb = h.astype(BF)
        h_ref[...] = hb
        p = jnp.dot(hb, w_ref[...], preferred_element_type=F32)
        z_ref[...] = p[:, A_Z:A_XBC]
        xbc_ref[...] = p[:, A_XBC:A_QA]
        qa_ref[...] = p[:, A_QA:A_KVA]
        kva_ref[...] = p[:, A_KVA:A_POOL]
        pool_ref[...] = p[:, A_POOL:A_KR]
        kr_ref[...] = p[:, A_KR:A_DT]
        dt_ref[...] = p[:, A_DT:PC]

    widths = [SSD_IN, XBC, QL, QL, PD, HP, HP]
    outs = [(_sds((B, T, w), F32), R.row(w)) for w in widths] + [(_sds((B, T, D), BF), R.row(D))]
    return R.call("in_proj", body,
                  xin + [(mod, R.mod()), (nw1, R.const((1, D))), (w_arr, R.const((D, PC)))],
                  outs, extra=8 << 20)


def _conv_pre(ext, w_ref, b_ref):
    return (_row(w_ref, 0) * _shift_rows(ext, -1)[8:8 + TM] + _row(w_ref, 1) * ext[8:8 + TM]
            + _row(w_ref, 2) * _shift_rows(ext, 1)[8:8 + TM] + _row(w_ref, 3) * _shift_rows(ext, 2)[8:8 + TM]
            + b_ref[...])


def _softplus(x):
    return jnp.maximum(x, 0.0) + jnp.log(1.0 + jnp.exp(-jnp.abs(x)))


def _ssd_prep(R, xbc_raw, dt_raw, conv_w8, conv_b, dtb):
    B, T = R.B, R.T

    def body(raw_ref, pv_ref, nx_ref, dtr_ref, w_ref, b_ref, dtb_ref, xbc_ref, dt_ref):
        i = pl.program_id(1)
        ext = R.ext(i, pv_ref, raw_ref[...], nx_ref)
        xbc_ref[...] = _silu(_conv_pre(ext, w_ref, b_ref))
        lane = _iota((TM, HP), 1)
        dtv = _softplus(dtr_ref[...] + dtb_ref[...])
        keep = lane < H
        dt_ref[0] = jnp.where(keep, dtv, 0.0)
        dt_ref[1] = jnp.where(keep, pltpu.roll(dtv, HP - H, axis=1), 0.0)

    return R.call("ssd_prep", body,
                  [(xbc_raw, R.row(XBC)), (xbc_raw, R.prev8(XBC)), (xbc_raw, R.next8(XBC)), (dt_raw, R.row(HP)),
                   (conv_w8, R.const((8, XBC))), (conv_b, R.const((1, XBC))), (dtb, R.const((1, HP)))],
                  [(_sds((B, T, XBC), F32), R.row(XBC)), (_sds((B, 2, T, HP), F32), R.row2(HP))], extra=12 << 20)


def _chunk_index(d, s, nc, nctc):
    if d == 0:
        return s
    return jnp.where(s < nctc, nctc - 1 - s, nc - 1 - (s - nctc))


def _dot01_r(x, m01, passes=3):
    b16 = jnp.bfloat16
    m = m01.astype(b16)
    out, rest = None, x
    for _ in range(passes):
        part = rest.astype(b16)
        rest = rest - part.astype(F32)
        term = jnp.dot(part, m, preferred_element_type=F32)
        out = term if out is None else out + term
    return out


def _scan_common(d, dtv, alog_ref):
    sel = _iota((HP, H * HP), 0) == (_iota((HP, H * HP), 1) >> 7)
    a_all = _dot01_r(-jnp.exp(alog_ref[d]), sel)
    dt_all = _dot01_r(dtv, sel)
    adt_all = dt_all * jnp.concatenate([a_all] * (CH // 8), axis=0)
    row = _iota((CH, CH), 0)
    col = _iota((CH, CH), 1)
    inc = col <= row if d == 0 else col >= row
    inc_t = row <= col if d == 0 else row >= col
    q_all = _dot01(inc, adt_all)
    return a_all, dt_all, adt_all, q_all, inc, inc_t


def _head_decay(h, adt_all, q_all, inc, inc_t):
    q = q_all[:, HP * h:HP * (h + 1)]
    q_t = q.T
    qtot = jnp.sum(adt_all[:, HP * h:HP * (h + 1)], axis=0, keepdims=True)
    lm = jnp.where(inc, jnp.exp(q - q_t), 0.0)
    lm_t = jnp.where(inc_t, jnp.exp(q_t - q), 0.0)
    return lm, lm_t, jnp.exp(q), jnp.exp(qtot), jnp.exp(qtot - q)


def _ssd_scan(B, T, xbc, dt2, alog2):
    nc, nctc = T // CH, CTX // CH

    def chain(d, xbc_ref, dt_ref, alog_ref, y_ref, hin_ref, hs):
        xbc_v = xbc_ref[...]
        dtv = dt_ref[...]
        blk = alog_ref[d]
        adt = dtv * -jnp.exp(jnp.sum(jnp.where(_iota(blk.shape, 0) == 0, blk, 0.0), axis=0, keepdims=True))
        row, col = _iota((CH, CH), 0), _iota((CH, CH), 1)
        inc = col <= row if d == 0 else col >= row
        inc_tf = (row <= col if d == 0 else row >= col).astype(F32)
        q = _dot01(inc, adt)
        hin_ref[...] = hs[d]
        for g in range(2):
            bg = xbc_v[:, SSD_IN + NST * g:SSD_IN + NST * (g + 1)]
            cg = xbc_v[:, SSD_IN + 2 * NST + NST * g:SSD_IN + 2 * NST + NST * (g + 1)]
            gm = _dot_nt(cg, bg)
            for r in range(3):
                h = 3 * g + r
                onehot = (_iota((1, HP), 1) == h).astype(F32)
                adt_h = jnp.sum(adt * onehot, axis=1, keepdims=True)
                qc = jnp.sum(q * onehot, axis=1, keepdims=True)
                dt_h = jnp.sum(dtv * onehot, axis=1, keepdims=True)
                qr = jnp.sum(adt_h * inc_tf, axis=0, keepdims=True)
                qtot = jnp.sum(adt_h, axis=0, keepdims=True)
                lm = jnp.where(inc, jnp.exp(qc - qr), 0.0)
                xh = xbc_v[:, P * h:P * (h + 1)] * dt_h
                hprev = hs[d, P * h:P * (h + 1), :]
                y_ref[:, P * h:P * (h + 1)] = _dot(gm * lm, xh) + jnp.exp(qc) * _dot_nt(cg, hprev)
                hs[d, P * h:P * (h + 1), :] = jnp.exp(qtot) * hprev + _dot_tn(xh * jnp.exp(qtot - qc), bg)

    def body(xf_ref, xb_ref, dtf_ref, dtb_ref, alog_ref, yf_ref, yb_ref, hf_ref, hb_ref, hs):
        @pl.when(pl.program_id(1) == 0)
        def _():
            hs[...] = jnp.zeros(hs.shape, F32)

        chain(0, xf_ref, dtf_ref, alog_ref, yf_ref, hf_ref, hs)
        chain(1, xb_ref, dtb_ref, alog_ref, yb_ref, hb_ref, hs)

    cidx = lambda d: (lambda s: _chunk_index(d, s, nc, nctc))
    specs = lambda d: dict(
        xbc=pl.BlockSpec((None, CH, XBC), lambda b, s: (b, cidx(d)(s), 0)),
        dt=pl.BlockSpec((None, None, CH, HP), lambda b, s: (b, d, cidx(d)(s), 0)),
        y=pl.BlockSpec((None, CH, SSD_IN), lambda b, s: (b, cidx(d)(s), 0)),
        h=pl.BlockSpec((None, None, SSD_IN, NST), lambda b, s: (b, cidx(d)(s), 0, 0)))
    f, r = specs(0), specs(1)
    bb = 2 * (_nbytes((CH, XBC), F32) + _nbytes((CH, HP), F32) + _nbytes((CH, SSD_IN), F32) + _nbytes((SSD_IN, NST), F32))
    return pl.pallas_call(
        functools.partial(body), name="ssd_scan", grid=(B, nc),
        in_specs=[f["xbc"], r["xbc"], f["dt"], r["dt"], pl.BlockSpec((2, 8, HP), lambda b, s: (0, 0, 0))],
        out_specs=[f["y"], r["y"], f["h"], r["h"]],
        out_shape=[_out((B, T, SSD_IN), F32)] * 2 + [_out((B, nc, SSD_IN, NST), F32)] * 2,
        scratch_shapes=[pltpu.VMEM((2, SSD_IN, NST), F32)],
        compiler_params=_params(("arbitrary",) * 2, bb, 12 << 20),
    )(*_hbm([xbc, xbc, dt2, dt2, alog2]))


def _swap8(u):
    lane = _iota(u.shape, 1)
    n = u.shape[1]
    return jnp.where((lane & 15) < 8, pltpu.roll(u, n - 8, axis=1), pltpu.roll(u, 8, axis=1))


def _rope(u, cos, sin_signed):
    return u * cos + _swap8(u) * sin_signed


def _rms(x, w):
    r = lax.rsqrt(jnp.mean(x * x, axis=-1, keepdims=True) + EPS)
    xh = x * r
    return xh * w, xh, r


def _rms_bwd(dy, xh, r, w):
    dw = jnp.sum(dy * xh, axis=0, keepdims=True)
    dxh = dy * w
    return r * (dxh - xh * jnp.mean(dxh * xh, axis=-1, keepdims=True)), dw


def _tile6(t):
    return jnp.concatenate([t] * H, axis=1)


def _per_head(fn, u):
    return jnp.concatenate([fn(u[:, HP * h:HP * (h + 1)]) for h in range(H)], axis=1)


def _mla_prep(R, qa, kva, kr, qnw, kvnw, wq, wk, wv, cos, sin):
    B, T = R.B, R.T
    scale = QK ** -0.5

    def body(qa_ref, kva_ref, kr_ref, qnw_ref, kvnw_ref, wq_ref, wk_ref, wv_ref, cos_ref, sin_ref,
             qt_ref, kt_ref, qtr_ref, ktr_ref, vtr_ref, cq_ref, ckv_ref):
        cq, _, _ = _rms(qa_ref[...], qnw_ref[...])
        ckv, _, _ = _rms(kva_ref[...], kvnw_ref[...])
        cqb, ckvb = cq.astype(BF), ckv.astype(BF)
        cq_ref[...] = cqb
        ckv_ref[...] = ckvb
        cos1, sin1 = cos_ref[...], sin_ref[...]
        q = _per_head(lambda u: _rope(u, cos1, sin1), jnp.dot(cqb, wq_ref[...], preferred_element_type=F32)) * scale
        qt_ref[...] = q.astype(BF)
        kk = _rope(kr_ref[...], cos1, sin1)
        hl = _iota((TM, QW), 1) & (HP - 1)
        k = jnp.dot(ckvb, wk_ref[...], preferred_element_type=F32) + _tile6(kk)
        k = jnp.where(jnp.logical_and(hl >= QK, hl < QK + 3), 1.0, k)
        kt_ref[...] = k.astype(BF)
        v = jnp.dot(ckvb, wv_ref[...], preferred_element_type=F32)
        v = jnp.where(jnp.logical_and(hl >= VH, hl < VH + 4), 1.0, v)
        for h in range(H):
            cols = slice(HP * h, HP * (h + 1))
            qtr_ref[h] = q[:, cols].T.astype(BF)
            ktr_ref[h] = k[:, cols].T.astype(BF)
            vtr_ref[h] = v[:, cols].T.astype(BF)

    tr = (_sds((B, H, HP, T), BF), pl.BlockSpec((None, H, HP, TM), lambda b, i: (b, 0, 0, i)))
    return R.call("mla_prep", body,
                  [(qa, R.row(QL)), (kva, R.row(QL)), (kr, R.row(HP)), (qnw, R.const((1, QL))), (kvnw, R.const((1, QL))),
                   (wq, R.const((QL, QW))), (wk, R.const((QL, QW))), (wv, R.const((QL, QW))),
                   (cos, pl.BlockSpec((TM, HP), lambda b, i: (i, 0))), (sin, pl.BlockSpec((TM, HP), lambda b, i: (i, 0)))],
                  [(_sds((B, T, QW), BF), R.row(QW))] * 2 + [tr] * 3 + [(_sds((B, T, QL), BF), R.row(QL))] * 2, extra=12 << 20)


def _flash_fwd(B, T, q_t, kt, v_t):
    nq, nk = T // TQ, T // TK
    HS = 2

    def body(q_ref, k_ref, v_ref, o_ref, lse_ref, s_scr):
        i = pl.program_id(2)

        def attend(nch):
            ms = []
            for hh in range(HS):
                q_tr = q_ref[hh]
                mrun = None
                for j in range(nch):
                    s = _dot(k_ref[TK * j:TK * (j + 1), HP * hh:HP * (hh + 1)], q_tr)
                    s_scr[hh, j] = s
                    mrun = s if mrun is None else jnp.maximum(mrun, s)
                ms.append(jnp.max(mrun, axis=0, keepdims=True))
            row = _iota((HP, TQ), 0)
            for hh in range(HS):
                acc_t = jnp.zeros((HP, TQ), F32)
                for j in range(nch):
                    acc_t = acc_t + _dot(v_ref[hh, :, TK * j:TK * (j + 1)], jnp.exp(s_scr[hh, j] - ms[hh]))
                l = jnp.sum(jnp.where(row == VH, acc_t, 0.0), axis=0, keepdims=True)
                o_ref[:, HP * hh:HP * (hh + 1)] = jnp.where(row < VH, acc_t / l, 0.0).T
                lse_ref[hh] = ms[hh] + jnp.log(l)

        @pl.when(i < CTX // TQ)
        def _():
            attend(CTX // TK)

        @pl.when(i >= CTX // TQ)
        def _():
            attend(nk)

    bb = HS * (_nbytes((TQ, HP), BF) + 2 * _nbytes((T, HP), BF) + 2 * _nbytes((TQ, HP), F32))
    return pl.pallas_call(
        functools.partial(body), name="flash_fwd", grid=(B, H // HS, nq),
        in_specs=[pl.BlockSpec((None, HS, HP, TQ), lambda b, h, i: (b, h, 0, i)),
                  pl.BlockSpec((None, T, HS * HP), lambda b, h, i: (b, 0, h)),
                  pl.BlockSpec((None, HS, HP, T), lambda b, h, i: (b, h, 0, 0))],
        out_specs=[pl.BlockSpec((None, TQ, HS * HP), lambda b, h, i: (b, i, h)),
                   pl.BlockSpec((None, HS, 1, TQ), lambda b, h, i: (b, h, 0, i))],
        out_shape=[_out((B, T, QW), F32), _out((B, H, 1, T), F32)],
        scratch_shapes=[pltpu.VMEM((HS, nk, TK, TQ), F32)],
        compiler_params=_params(("arbitrary",) * 3, bb, _nbytes((HS, nk, TK, TQ), F32) + (8 << 20)),
    )(*_hbm([q_t, kt, v_t]))


def _pool_terms(R, i, rows):
    t, n = R.stream_pos(i, rows)
    lane = _iota((1, PD), 1)
    half = jnp.where(lane < 64, 1, jnp.where(lane < 128, 2, jnp.where(lane < 192, 4, 8)))
    cnt = (jnp.minimum(t + half, n) - jnp.maximum(t - half, 0)).astype(F32)
    valid = jnp.logical_and(t >= 0, t < n)
    return jnp.where(valid, cnt, 1.0), valid.astype(F32), lane


def _lane_select(lane, a2, a4, a8, a16):
    return jnp.where(lane < 64, a2, jnp.where(lane < 128, a4, jnp.where(lane < 192, a8, a16)))


def _pool_centred(R, i, ext):
    cnt, valid, lane = _pool_terms(R, i, ext.shape[0])
    s2 = ext + _shift_rows(ext, -1)
    s4 = _shift_rows(s2, -1) + _shift_rows(s2, 1)
    s8 = _shift_rows(s4, -2) + _shift_rows(s4, 2)
    s16 = _shift_rows(s8, -4) + _shift_rows(s8, 4)
    return _lane_select(lane, s2, s4, s8, s16) / cnt - ext, cnt, valid, lane


def _group_mask():
    return _iota((1, SSD_IN), 1) < SSD_IN // 2


def _ssd_gate(yf_ref, yb_ref, xbc_ref, z_ref, dsk_ref):
    ytot = yf_ref[...] + yb_ref[...] + xbc_ref[:, 0:SSD_IN] * dsk_ref[...]
    z = z_ref[...]
    gz = ytot * _silu(z)
    g0 = _group_mask()
    sq = gz * gz
    s0 = jnp.sum(jnp.where(g0, sq, 0.0), axis=1, keepdims=True)
    s1 = jnp.sum(jnp.where(g0, 0.0, sq), axis=1, keepdims=True)
    half = SSD_IN // 2
    r = jnp.where(g0, lax.rsqrt(s0 / half + EPS), lax.rsqrt(s1 / half + EPS))
    return ytot, z, gz, r


def _out_proj(R, x, mod, yf, yb, xbc, z, o, pool_in, wbd, pscale, dsk, snw, wout):
    B, T = R.B, R.T
    xin = R.xin(x)

    def body(*refs):
        x_refs = refs[:len(xin)]
        (mod_ref, yf_ref, yb_ref, xbc_ref, z_ref, o_ref, u_ref, upv_ref, unx_ref, wbd_ref, psc_ref, dsk_ref, snw_ref, w_ref,
         xmid_ref, cat_ref, mix_ref) = refs[len(xin):]
        i = pl.program_id(1)
        _, _, gz, r = _ssd_gate(yf_ref, yb_ref, xbc_ref, z_ref, dsk_ref)
        dm, _, _, _ = _pool_centred(R, i, R.ext(i, upv_ref, u_ref[...], unx_ref))
        cat_ref[:, 0:SSD_IN] = (gz * r * snw_ref[...]).astype(BF)
        cat_ref[:, SSD_IN:SSD_IN + QW] = o_ref[...].astype(BF)
        cat_ref[:, SSD_IN + QW:MIXW] = (_dot(dm[8:8 + TM], wbd_ref[...]) * psc_ref[...]).astype(BF)
        mix = jnp.dot(cat_ref[...], w_ref[...], preferred_element_type=F32)
        mix_ref[...] = mix.astype(BF)
        xmid_ref[...] = R.xval(i, x_refs) + _row(mod_ref, 2) * mix

    return R.call("out_proj", body,
                  xin + [(mod, R.mod()), (yf, R.row(SSD_IN)), (yb, R.row(SSD_IN)), (xbc, R.row(XBC)), (z, R.row(SSD_IN)),
                         (o, R.row(QW)), (pool_in, R.row(PD)), (pool_in, R.prev8(PD)), (pool_in, R.next8(PD)),
                         (wbd, R.const((PD, PD))), (pscale, R.const((1, PD))),
                         (dsk, R.const((1, SSD_IN))), (snw, R.const((1, SSD_IN))), (wout, R.const((MIXW, D)))],
                  [(_sds((B, T, D), F32), R.row(D)), (_sds((B, T, MIXW), BF), R.row(MIXW)), (_sds((B, T, D), BF), R.row(D))],
                  extra=12 << 20)


def _load_once(first, pairs, sem):
    @pl.when(first)
    def _():
        cps = [pltpu.make_async_copy(src, dst, sem.at[k]) for k, (src, dst) in enumerate(pairs)]
        for cp in cps:
            cp.start()
        for cp in cps:
            cp.wait()


ANY = pl.BlockSpec(memory_space=pl.ANY)


def _mlp_fwd(R, xmid, mod, nw2, w1, w2, head=None):
    B, T = R.B, R.T
    nhead = 0 if head is None else 2

    def body(*refs):
        x_ref, mod_ref, nw_ref, w1_hbm, w2_hbm = refs[:5]
        xo_ref, h_ref, u_ref, y_ref = refs[5 + nhead:9 + nhead]
        w1_v, w2_v, sem = refs[-3:]
        first = jnp.logical_and(pl.program_id(0) == 0, pl.program_id(1) == 0)
        _load_once(first, [(w1_hbm, w1_v), (w2_hbm, w2_v)], sem)
        x = x_ref[...]
        h, _, _ = _norm_mod(x, nw_ref[...], _row(mod_ref, 3), _row(mod_ref, 4))
        hb = h.astype(BF)
        h_ref[...] = hb
        y = jnp.zeros((TM, D), F32)
        for j in range(NDEV):
            u = jnp.dot(hb, w1_v[j], preferred_element_type=F32)
            u_ref[:, FSH * j:FSH * (j + 1)] = u.astype(BF)
            a = jnp.square(jnp.maximum(u, 0.0))
            y = y + jnp.dot(a.astype(BF), w2_v[j], preferred_element_type=F32)
        y_ref[...] = y.astype(BF)
        xo = x + _row(mod_ref, 5) * y
        if head is None:
            xo_ref[...] = xo
            return
        t_ref, fw_ref = refs[5:7]
        loss_ref, dw_ref = refs[9 + nhead:11 + nhead]
        live = jnp.where(pl.program_id(1) >= R.nct, 1.0, 0.0)
        yv, xh, r = _rms(xo, fw_ref[...])
        err = (yv - t_ref[...]) * live
        dxn, dw = _rms_bwd(err / D, xh, r, fw_ref[...])
        xo_ref[...] = dxn
        part = 0.5 * jnp.sum(jnp.sum(err * err, axis=1, keepdims=True), axis=0, keepdims=True) / D
        _acc_rows(loss_ref, first, {0: jnp.broadcast_to(part, (1, HP))})
        _acc_rows(dw_ref, first, {0: dw})

    extra_in = [] if head is None else [(head[0], R.tgt(D)), (head[1], R.const((1, D)))]
    extra_out = [] if head is None else [(_sds((8, HP), F32), R.const((8, HP))), (_sds((8, D), F32), R.const((8, D)))]
    return R.call("mlp_fwd" if head is None else "mlp_fwd_head", body,
                  [(xmid, R.row(D)), (mod, R.mod()), (nw2, R.const((1, D))), (w1, ANY), (w2, ANY)] + extra_in,
                  [(_sds((B, T, D), F32), R.row(D)), (_sds((B, T, D), BF), R.row(D)), (_sds((B, T, FF), BF), R.row(FF)),
                   (_sds((B, T, D), BF), R.row(D))] + extra_out,
                  scratch=[pltpu.VMEM((NDEV, D, FSH), w1.dtype), pltpu.VMEM((NDEV, FSH, D), w2.dtype), pltpu.SemaphoreType.DMA((2,))],
                  extra=(2 * _nbytes((NDEV, D, FSH), BF)) + (8 << 20))


def _tn_matmul(name, a, b, tn, sub=1):
    B, T, K = a.shape
    N = b.shape[2]
    nk = 1
    while T % nk or (T // nk) > 1088 or (T // nk) % 16:
        nk += 1
    tk = T // nk
    kt = K if K <= 1536 else 2048
    w = tn // sub
    assert K % kt == 0 and N % tn == 0 and tn % sub == 0

    def body(a_ref, b_ref, o_ref):
        first = jnp.logical_and(pl.program_id(2) == 0, pl.program_id(3) == 0)

        @pl.when(first)
        def _():
            o_ref[...] = jnp.zeros(o_ref.shape, F32)

        acc = _dot_tn(a_ref[...], b_ref[...])
        for s in range(sub):
            o_ref[s] += acc[:, w * s:w * (s + 1)]

    bb = _nbytes((tk, kt), a.dtype) + _nbytes((tk, tn), b.dtype) + _nbytes((kt, tn), F32)
    return pl.pallas_call(
        functools.partial(body), name=name, grid=(N // tn, K // kt, B, nk),
        in_specs=[pl.BlockSpec((None, tk, kt), lambda j, kk, bi, t: (bi, t, kk)),
                  pl.BlockSpec((None, tk, tn), lambda j, kk, bi, t: (bi, t, j))],
        out_specs=pl.BlockSpec((sub, kt, w), lambda j, kk, bi, t: (j, kk, 0)),
        out_shape=_out((sub * N // tn, K, w), F32),
        compiler_params=_params(("arbitrary",) * 4, bb, _nbytes((kt, tn), F32) + (8 << 20)),
    )(*_hbm([a, b]))


def _mlp_bwd(R, dxo, xmid, ubf, ybf, mod, nw2, w1, w2):
    B, T = R.B, R.T

    def body(dxo_ref, x_ref, u_ref, y_ref, mod_ref, nw_ref, w1_hbm, w2_hbm,
             dxm_ref, du_ref, a_ref, dy_ref, dmod_ref, dnw_ref, w1_v, w2_v, sem):
        b, i = pl.program_id(0), pl.program_id(1)
        _load_once(jnp.logical_and(b == 0, i == 0), [(w1_hbm, w1_v), (w2_hbm, w2_v)], sem)
        dxo = dxo_ref[...]
        _, xn, r = _norm_mod(x_ref[...], nw_ref[...], _row(mod_ref, 3), _row(mod_ref, 4))
        dyb = (dxo * _row(mod_ref, 5)).astype(BF)
        dy_ref[...] = dyb
        dg2 = jnp.sum(dxo * y_ref[...].astype(F32), axis=0, keepdims=True)
        dh = jnp.zeros((TM, D), F32)
        for j in range(NDEV):
            rl = jnp.maximum(u_ref[:, FSH * j:FSH * (j + 1)].astype(F32), 0.0)
            a_ref[:, FSH * j:FSH * (j + 1)] = (rl * rl).astype(BF)
            du = (_dot_nt(dyb, w2_v[j]) * (2.0 * rl)).astype(BF)
            du_ref[:, FSH * j:FSH * (j + 1)] = du
            dh = dh + _dot_nt(du, w1_v[j])
        dx, dsh, dsc, dnw = _norm_mod_bwd(dh, xn, r, nw_ref[...], _row(mod_ref, 4))
        dxm_ref[...] = dxo + dx
        _acc_rows(dmod_ref, R.first_of_stream(i), {3: dsh, 4: dsc, 5: dg2})
        _acc_rows(dnw_ref, jnp.logical_and(b == 0, i == 0), {0: dnw})

    return R.call("mlp_bwd", body,
                  [(dxo, R.row(D)), (xmid, R.row(D)), (ubf, R.row(FF)), (ybf, R.row(D)), (mod, R.mod()), (nw2, R.const((1, D))),
                   (w1, ANY), (w2, ANY)],
                  [(_sds((B, T, D), F32), R.row(D)), (_sds((B, T, FF), BF), R.row(FF)), (_sds((B, T, FF), BF), R.row(FF)),
                   (_sds((B, T, D), BF), R.row(D)), (_sds((B, 2, 8, D), F32), R.mod()), (_sds((8, D), F32), R.const((8, D)))],
                  scratch=[pltpu.VMEM((NDEV, D, FSH), w1.dtype), pltpu.VMEM((NDEV, FSH, D), w2.dtype), pltpu.SemaphoreType.DMA((2,))],
                  extra=(2 * _nbytes((NDEV, D, FSH), BF)) + (8 << 20))


def _minus_in_lanes(x, col, first):
    b16 = jnp.bfloat16
    hi = col.astype(b16).astype(F32)
    r1 = col - hi
    mid = r1.astype(b16).astype(F32)
    lane = _iota(x.shape, 1)
    return jnp.where(lane == first, -hi, jnp.where(lane == first + 1, -mid, jnp.where(lane == first + 2, mid - r1, x)))


def _out_bwd(R, dxm, mod, mixbf, yf, yb, xbc, z, o, qt, lse, dsk, snw, wout):
    B, T = R.B, R.T

    def body(dxm_ref, mod_ref, mix_ref, yf_ref, yb_ref, xbc_ref, z_ref, o_ref, q_ref, lse_ref, dsk_ref, snw_ref, w_ref,
             dmix_ref, dz_ref, dyt_ref, dxsk_ref, qs_ref, dos_ref, dotr_ref, dpo_ref, dmod_ref, dvec_ref):
        b, i = pl.program_id(0), pl.program_id(1)
        dxm = dxm_ref[...]
        dmixb = (dxm * _row(mod_ref, 2)).astype(BF)
        dmix_ref[...] = dmixb
        dg1 = jnp.sum(dxm * mix_ref[...].astype(F32), axis=0, keepdims=True)
        dcat = _dot_nt(dmixb, w_ref[...])
        do_v = dcat[:, SSD_IN:SSD_IN + QW]
        doo = do_v * o_ref[...]
        for h in range(H):
            cols = slice(HP * h, HP * (h + 1))
            dd = jnp.sum(doo[:, cols], axis=1, keepdims=True)
            dos_ref[:, cols] = _minus_in_lanes(do_v[:, cols], dd, VH + 1).astype(BF)
            lse_b = jnp.broadcast_to(lse_ref[h], (HP, TM)).T
            qs_ref[:, cols] = _minus_in_lanes(q_ref[:, cols].astype(F32), lse_b, QK).astype(BF)
            dotr_ref[h] = do_v[:, cols].T.astype(BF)
        dpo_ref[...] = dcat[:, SSD_IN + QW:MIXW]
        dsn = dcat[:, 0:SSD_IN]
        ytot, zv, gz, r = _ssd_gate(yf_ref, yb_ref, xbc_ref, z_ref, dsk_ref)
        gh = gz * r
        dsnw = jnp.sum(dsn * gh, axis=0, keepdims=True)
        dgh = dsn * snw_ref[...]
        g0 = _group_mask()
        pr = dgh * gh
        half = SSD_IN // 2
        m0 = jnp.sum(jnp.where(g0, pr, 0.0), axis=1, keepdims=True) / half
        m1 = jnp.sum(jnp.where(g0, 0.0, pr), axis=1, keepdims=True) / half
        dgz = r * (dgh - gh * jnp.where(g0, m0, m1))
        dyt = dgz * _silu(zv)
        dz_ref[...] = (dgz * ytot * _dsilu(zv)).astype(BF)
        dyt_ref[...] = dyt
        dxsk_ref[...] = dyt * dsk_ref[...]
        ddsk = jnp.sum(dyt * xbc_ref[:, 0:SSD_IN], axis=0, keepdims=True)
        _acc_rows(dmod_ref, R.first_of_stream(i), {2: dg1})
        _acc_rows(dvec_ref, jnp.logical_and(b == 0, i == 0), {0: dsnw, 1: ddsk})

    return R.call("out_bwd", body,
                  [(dxm, R.row(D)), (mod, R.mod()), (mixbf, R.row(D)), (yf, R.row(SSD_IN)), (yb, R.row(SSD_IN)), (xbc, R.row(XBC)),
                   (z, R.row(SSD_IN)),
                   (o, R.row(QW)), (qt, R.row(QW)), (lse, pl.BlockSpec((None, H, 1, TM), lambda b, i: (b, 0, 0, i))),
                   (dsk, R.const((1, SSD_IN))), (snw, R.const((1, SSD_IN))), (wout, R.const((MIXW, D)))],
                  [(_sds((B, T, D), BF), R.row(D)), (_sds((B, T, SSD_IN), BF), R.row(SSD_IN)), (_sds((B, T, SSD_IN), F32), R.row(SSD_IN)),
                   (_sds((B, T, SSD_IN), F32), R.row(SSD_IN)), (_sds((B, T, QW), BF), R.row(QW)), (_sds((B, T, QW), BF), R.row(QW)),
                   (_sds((B, H, HP, T), BF), pl.BlockSpec((None, H, HP, TM), lambda b, i: (b, 0, 0, i))),
                   (_sds((B, T, PD), F32), R.row(PD)),
                   (_sds((B, 2, 8, D), F32), R.mod()), (_sds((8, SSD_IN), F32), R.const((8, SSD_IN)))],
                  extra=8 << 20)


def _pool_bwd(R, dpo, u, wbd, scale):
    B, T = R.B, R.T

    def body(d_ref, dpv_ref, dnx_ref, u_ref, upv_ref, unx_ref, w_ref, sc_ref, du_ref, dw_ref, dsc_ref):
        b, i = pl.program_id(0), pl.program_id(1)
        ext_u = R.ext(i, upv_ref, u_ref[...], unx_ref)
        ext_d = R.ext(i, dpv_ref, d_ref[...], dnx_ref)
        dm, cnt, valid, lane = _pool_centred(R, i, ext_u)
        ddm = _dot_nt(ext_d * sc_ref[...], w_ref[...]) * valid
        e = ddm / cnt
        a2 = e + _shift_rows(e, 1)
        a4 = _shift_rows(a2, -1) + _shift_rows(a2, 1)
        a8 = _shift_rows(a4, -2) + _shift_rows(a4, 2)
        a16 = _shift_rows(a8, -4) + _shift_rows(a8, 4)
        du_ref[...] = (_lane_select(lane, a2, a4, a8, a16) - ddm)[8:8 + TM].astype(BF)
        dmc = dm[8:8 + TM]
        dpo_c = d_ref[...]
        first = jnp.logical_and(b == 0, i == 0)

        @pl.when(first)
        def _():
            dw_ref[...] = jnp.zeros(dw_ref.shape, F32)

        dw_ref[...] += _dot_tn(dmc, dpo_c * sc_ref[...])
        _acc_rows(dsc_ref, first, {0: jnp.sum(dpo_c * _dot(dmc, w_ref[...]), axis=0, keepdims=True)})

    return R.call("pool_bwd", body,
                  [(dpo, R.row(PD)), (dpo, R.prev8(PD)), (dpo, R.next8(PD)), (u, R.row(PD)), (u, R.prev8(PD)), (u, R.next8(PD)),
                   (wbd, R.const((PD, PD))), (scale, R.const((1, PD)))],
                  [(_sds((B, T, PD), BF), R.row(PD)), (_sds((PD, PD), F32), R.const((PD, PD))), (_sds((8, PD), F32), R.const((8, PD)))],
                  extra=8 << 20)


def _flash_bwd(B, T, qs, kt, dos, q_t, k_t, v_t, do_t):
    nk, nq = T // TK, T // TQ
    HS = 2

    def body(k_ref, kt_ref, vt_ref, q_ref, do_ref, qt_ref, dot_ref, dq_ref, dk_ref, dv_ref, s_scr, dp_scr, ds_scr):
        j = pl.program_id(2)
        ctx_keys = jnp.where(j < CTX // TK, 1.0, 0.0)
        for hh in range(HS):
            cols = slice(HP * hh, HP * (hh + 1))
            s_scr[hh] = _dot(q_ref[:, cols], kt_ref[hh])
            dp_scr[hh] = _dot(do_ref[:, cols], vt_ref[hh])
        for hh in range(HS):
            cols = slice(HP * hh, HP * (hh + 1))
            dk_tr = jnp.zeros((HP, TK), F32)
            dv_tr = jnp.zeros((HP, TK), F32)
            for i in range(nq):
                rows = slice(TQ * i, TQ * (i + 1))
                p = jnp.exp(s_scr[hh, rows, :])
                if i < CTX // TQ:
                    p = p * ctx_keys
                ds = (p * dp_scr[hh, rows, :]).astype(BF)
                ds_scr[hh, rows, :] = ds
                dv_tr = dv_tr + _dot(dot_ref[hh, :, rows], p)
                dk_tr = dk_tr + _dot(qt_ref[hh, :, rows], ds)
            dq = _dot(ds_scr[hh], k_ref[:, cols])

            @pl.when(j == 0)
            def _():
                dq_ref[:, cols] = dq

            @pl.when(j > 0)
            def _():
                dq_ref[:, cols] += dq

            dk_ref[:, cols] = dk_tr.T
            dv_ref[:, cols] = dv_tr.T

    tspec = pl.BlockSpec((None, TK, HS * HP), lambda b, h, j: (b, j, h))
    fspec = pl.BlockSpec((None, T, HS * HP), lambda b, h, j: (b, 0, h))
    ttspec = pl.BlockSpec((None, HS, HP, TK), lambda b, h, j: (b, h, 0, j))
    ftspec = pl.BlockSpec((None, HS, HP, T), lambda b, h, j: (b, h, 0, 0))
    bb = HS * (4 * _nbytes((T, HP), BF) + _nbytes((T, HP), F32) + 8 * _nbytes((TK, HP), F32))
    scr = HS * (2 * _nbytes((T, TK), F32) + _nbytes((T, TK), BF))
    return pl.pallas_call(
        functools.partial(body), name="flash_bwd", grid=(B, H // HS, nk),
        in_specs=[tspec, ttspec, ttspec, fspec, fspec, ftspec, ftspec],
        out_specs=[fspec, tspec, tspec],
        out_shape=[_out((B, T, QW), F32)] * 3,
        scratch_shapes=[pltpu.VMEM((HS, T, TK), F32), pltpu.VMEM((HS, T, TK), F32), pltpu.VMEM((HS, T, TK), BF)],
        compiler_params=_params(("arbitrary",) * 3, bb, scr + (8 << 20)),
    )(*_hbm([kt, k_t, v_t, qs, dos, q_t, do_t]))


def _mla_bwd(R, dqt, dkt, dvt, qa, kva, qnw, kvnw, wq, wk, wv, cos, sin):
    B, T = R.B, R.T
    scale = QK ** -0.5

    def body(dq_ref, dk_ref, dv_ref, qa_ref, kva_ref, qnw_ref, kvnw_ref, wq_ref, wk_ref, wv_ref, cos_ref, sin_ref,
             dqp_ref, dkvb_ref, dqa_ref, dkva_ref, dkr_ref, dnw_ref):
        b, i = pl.program_id(0), pl.program_id(1)
        cos1, sin1 = cos_ref[...], sin_ref[...]
        dq = dq_ref[...] * scale
        dqp = _per_head(lambda g: g * cos1 + _swap8(g * sin1), dq).astype(BF)
        dqp_ref[...] = dqp
        dkv = dk_ref[...]
        dkb = dkv.astype(BF)
        dvb = dv_ref[...].astype(BF)
        dkvb_ref[:, :QW] = dkb
        dkvb_ref[:, QW:] = dvb
        dkk = dkv[:, 0:HP]
        for h in range(1, H):
            dkk = dkk + dkv[:, HP * h:HP * (h + 1)]
        lane = _iota((TM, HP), 1)
        rope_lane = jnp.logical_and(lane >= NOPE, lane < NOPE + ROPE)
        dkr_ref[...] = jnp.where(rope_lane, dkk * cos1 + _swap8(dkk * sin1), 0.0).astype(BF)
        _, qh, qr = _rms(qa_ref[...], qnw_ref[...])
        _, kh, kr_ = _rms(kva_ref[...], kvnw_ref[...])
        dcq = _dot_nt(dqp, wq_ref[...])
        dckv = _dot_nt(dkb, wk_ref[...]) + _dot_nt(dvb, wv_ref[...])
        dqa, dqw = _rms_bwd(dcq, qh, qr, qnw_ref[...])
        dkva, dkw = _rms_bwd(dckv, kh, kr_, kvnw_ref[...])
        dqa_ref[...] = dqa.astype(BF)
        dkva_ref[...] = dkva.astype(BF)
        _acc_rows(dnw_ref, jnp.logical_and(b == 0, i == 0), {0: dqw, 1: dkw})

    tab = pl.BlockSpec((TM, HP), lambda b, i: (i, 0))
    return R.call("mla_bwd", body,
                  [(dqt, R.row(QW)), (dkt, R.row(QW)), (dvt, R.row(QW)), (qa, R.row(QL)), (kva, R.row(QL)),
                   (qnw, R.const((1, QL))), (kvnw, R.const((1, QL))), (wq, R.const((QL, QW))), (wk, R.const((QL, QW))),
                   (wv, R.const((QL, QW))), (cos, tab), (sin, tab)],
                  [(_sds((B, T, QW), BF), R.row(QW)), (_sds((B, T, 2 * QW), BF), R.row(2 * QW))]
                  + [(_sds((B, T, QL), BF), R.row(QL))] * 2
                  + [(_sds((B, T, HP), BF), R.row(HP)), (_sds((8, QL), F32), R.const((8, QL)))], extra=8 << 20)


def _ssd_scan_bwd(B, T, dyt, xbc, dt2, alog2, hin_f, hin_b):
    nc, nctc = T // CH, CTX // CH

    def chain(d, dy_ref, xbc_ref, dt_ref, alog_ref, hin_ref, dxbc_ref, ddt_ref, da_ref, dhs_all):
        dhs = dhs_all.at[d]
        xbc_v = xbc_ref[...]
        dyv = dy_ref[...]
        a_all, dt_all, adt_all, q_all, inc, inc_t = _scan_common(d, dt_ref[...], alog_ref)
        ones_p = jnp.ones((P, HP), F32)
        total = lambda m: jnp.sum(jnp.sum(m, axis=0, keepdims=True), axis=1, keepdims=True)
        dq_parts, dqtot_parts, ddtx_parts = [], [], []
        for g in range(2):
            bg = xbc_v[:, SSD_IN + NST * g:SSD_IN + NST * (g + 1)]
            cg = xbc_v[:, SSD_IN + 2 * NST + NST * g:SSD_IN + 2 * NST + NST * (g + 1)]
            bg_t, cg_t = bg.T, cg.T
            gm = _dot(cg, bg_t)
            gm_t = _dot(bg, cg_t)
            dgm = jnp.zeros((CH, CH), F32)
            dgm_t = jnp.zeros((CH, CH), F32)
            dbg = jnp.zeros((CH, NST), F32)
            dcg = jnp.zeros((CH, NST), F32)
            for r in range(3):
                h = 3 * g + r
                lm, lm_t, eq, etot, dte = _head_decay(h, adt_all, q_all, inc, inc_t)
                eq_p, etot_p, dte_p = eq[:, :P], etot[:, :P], dte[:, :P]
                dt_p = dt_all[:, HP * h:HP * h + P]
                xs_h = xbc_v[:, P * h:P * (h + 1)]
                xh = xs_h * dt_p
                sm, sm_t = gm * lm, gm_t * lm_t
                dy_h = dyv[:, P * h:P * (h + 1)]
                hprev = hin_ref[P * h:P * (h + 1), :].T
                dho = dhs[:, P * h:P * (h + 1)]
                ds = _dot_nt(dy_h, xh)
                ds_t = _dot_nt(xh, dy_h)
                dx = _dot(sm_t, dy_h)
                edy = eq_p * dy_h
                yo = _dot(cg, hprev)
                dcg = dcg + _dot_nt(edy, hprev)
                dhin = _dot(cg_t, edy) + etot_p * dho
                zs = _dot(bg, dho)
                dx = dx + dte_p * zs
                wm = dte_p * xh * zs
                dbg = dbg + _dot_nt(xh * dte_p, dho)
                dgm = dgm + ds * lm
                dgm_t = dgm_t + ds_t * lm_t
                rs = jnp.sum(ds * sm - ds_t * sm_t, axis=1, keepdims=True) + jnp.sum(edy * yo - wm, axis=1, keepdims=True)
                dq_parts.append(jnp.broadcast_to(rs, (CH, HP)))
                dqtot_parts.append(total(hprev * dho) * etot + total(wm))
                dxbc_ref[:, P * h:P * (h + 1)] = dx * dt_p
                ddtx_parts.append(_dot01_r(dx * xs_h, ones_p, passes=2))
                dhs[:, P * h:P * (h + 1)] = dhin
            dcg = dcg + _dot(dgm, bg)
            dbg = dbg + _dot(dgm_t, cg)
            dxbc_ref[:, SSD_IN + NST * g:SSD_IN + NST * (g + 1)] = dbg
            dxbc_ref[:, SSD_IN + 2 * NST + NST * g:SSD_IN + 2 * NST + NST * (g + 1)] = dcg
        cat = lambda parts: jnp.concatenate(parts, axis=1)
        dadt_all = _dot01(inc_t, cat(dq_parts)) + cat(dqtot_parts)
        ddt_all = cat(ddtx_parts) + dadt_all * jnp.concatenate([a_all] * (CH // 8), axis=0)
        da_all = jnp.sum(dadt_all * dt_all, axis=0, keepdims=True)
        lane, lane1 = _iota((CH, HP), 1), _iota((1, HP), 1)
        ddt = jnp.zeros((CH, HP), F32)
        da = jnp.zeros((1, HP), F32)
        for h in range(H):
            ddt = ddt + jnp.where(lane == h, ddt_all[:, HP * h:HP * (h + 1)], 0.0)
            da = da + jnp.where(lane1 == h, da_all[:, HP * h:HP * (h + 1)], 0.0)
        ddt_ref[...] = ddt
        da_ref[d] += jnp.where(_iota((8, HP), 0) == 0, da, 0.0)

    def body(dyf_ref, dyb_ref, xf_ref, xb_ref, dtf_ref, dtb_ref, alog_ref, hf_ref, hb_ref,
             dxf_ref, dxb_ref, ddtf_ref, ddtb_ref, da_ref, dhs):
        @pl.when(pl.program_id(1) == 0)
        def _():
            dhs[...] = jnp.zeros(dhs.shape, F32)
            da_ref[...] = jnp.zeros(da_ref.shape, F32)

        chain(0, dyf_ref, xf_ref, dtf_ref, alog_ref, hf_ref, dxf_ref, ddtf_ref, da_ref, dhs)
        chain(1, dyb_ref, xb_ref, dtb_ref, alog_ref, hb_ref, dxb_ref, ddtb_ref, da_ref, dhs)

    cidx = lambda d: (lambda s: _chunk_index(d, nc - 1 - s, nc, nctc))
    specs = lambda d: dict(
        dy=pl.BlockSpec((None, CH, SSD_IN), lambda b, s: (b, cidx(d)(s), 0)),
        xbc=pl.BlockSpec((None, CH, XBC), lambda b, s: (b, cidx(d)(s), 0)),
        dt=pl.BlockSpec((None, None, CH, HP), lambda b, s: (b, d, cidx(d)(s), 0)),
        h=pl.BlockSpec((None, None, SSD_IN, NST), lambda b, s: (b, cidx(d)(s), 0, 0)),
        ddt=pl.BlockSpec((None, CH, HP), lambda b, s: (b, cidx(d)(s), 0)))
    f, r = specs(0), specs(1)
    bb = 2 * (2 * _nbytes((CH, XBC), F32) + 2 * _nbytes((CH, HP), F32) + _nbytes((CH, SSD_IN), F32) + _nbytes((SSD_IN, NST), F32))
    return pl.pallas_call(
        functools.partial(body), name="ssd_scan_bwd", grid=(B, nc),
        in_specs=[f["dy"], r["dy"], f["xbc"], r["xbc"], f["dt"], r["dt"], pl.BlockSpec((2, 8, HP), lambda b, s: (0, 0, 0)),
                  f["h"], r["h"]],
        out_specs=[f["xbc"], r["xbc"], f["ddt"], r["ddt"], pl.BlockSpec((None, 2, 8, HP), lambda b, s: (b, 0, 0, 0))],
        out_shape=[_out((B, T, XBC), F32)] * 2 + [_out((B, T, HP), F32)] * 2 + [_out((B, 2, 8, HP), F32)],
        scratch_shapes=[pltpu.VMEM((2, NST, SSD_IN), F32)],
        compiler_params=_params(("arbitrary",) * 2, bb, 16 << 20),
    )(*_hbm([dyt, dyt, xbc, xbc, dt2, dt2, alog2, hin_f, hin_b]))


def _ssd_prep_bwd(R, dxbc_f, dxbc_b, dxsk, ddt_f, ddt_b, xbc_raw, dt_raw, conv_w8, conv_b, dtb):
    B, T = R.B, R.T

    def body(dxf_ref, dxb_ref, dsk_ref, ddtf_ref, ddtb2_ref, raw_ref, pv_ref, nx_ref, dtr_ref, w_ref, b_ref, dtb_ref,
             dpre_ref, ddtr_ref, dcw_ref, dvec_ref, ddtb_ref):
        b, i = pl.program_id(0), pl.program_id(1)
        ext = R.ext(i, pv_ref, raw_ref[...], nx_ref)
        pre = _conv_pre(ext, w_ref, b_ref)
        dxbc = dxf_ref[...] + dxb_ref[...]
        skip = jnp.concatenate([dsk_ref[...], jnp.zeros((TM, XBC - SSD_IN), F32)], axis=1)
        dpre = (dxbc + skip) * _dsilu(pre)
        dpre_ref[...] = dpre
        first = jnp.logical_and(b == 0, i == 0)
        taps = {k: jnp.sum(dpre * _shift_rows(ext, k - 1)[8:8 + TM], axis=0, keepdims=True) for k in range(4)}
        _acc_rows(dcw_ref, first, taps)
        _acc_rows(dvec_ref, first, {0: jnp.sum(dpre, axis=0, keepdims=True)})
        ddt = ddtf_ref[...] + pltpu.roll(ddtb2_ref[...], H, axis=1)
        ddtr = ddt * _sigmoid(dtr_ref[...] + dtb_ref[...])
        ddtr = jnp.where(_iota((TM, HP), 1) < 2 * H, ddtr, 0.0)
        ddtr_ref[...] = ddtr.astype(BF)
        _acc_rows(ddtb_ref, first, {0: jnp.sum(ddtr, axis=0, keepdims=True)})

    return R.call("ssd_prep_bwd", body,
                  [(dxbc_f, R.row(XBC)), (dxbc_b, R.row(XBC)), (dxsk, R.row(SSD_IN)), (ddt_f, R.row(HP)), (ddt_b, R.row(HP)),
                   (xbc_raw, R.row(XBC)), (xbc_raw, R.prev8(XBC)),
                   (xbc_raw, R.next8(XBC)), (dt_raw, R.row(HP)), (conv_w8, R.const((8, XBC))), (conv_b, R.const((1, XBC))),
                   (dtb, R.const((1, HP)))],
                  [(_sds((B, T, XBC), F32), R.row(XBC)), (_sds((B, T, HP), BF), R.row(HP)), (_sds((8, XBC), F32), R.const((8, XBC))),
                   (_sds((8, XBC), F32), R.const((8, XBC))), (_sds((8, HP), F32), R.const((8, HP)))], extra=16 << 20)


def _in_bwd(R, dxm, x, mod, nw1, dz, dpre, dqa, dkva, dpool, dkr, ddtr, conv_w8, w_arr, latent_only=False):
    B, T = R.B, R.T
    dx_out = (_sds((B, T - CTX, D), F32), R.tgt(D)) if latent_only else (_sds((B, T, D), F32), R.row(D))

    xin = R.xin(x)

    def body(*refs):
        x_refs = refs[:len(xin)]
        (dxm_ref, mod_ref, nw_ref, dz_ref, dp_ref, dpp_ref, dpn_ref, dqa_ref, dkva_ref, dpool_ref, dkr_ref, ddt_ref,
         cw_ref, w_ref, dx_ref, dproj_ref, dmod_ref, dnw_ref) = refs[len(xin):]
        b, i = pl.program_id(0), pl.program_id(1)
        ext = R.ext(i, dpp_ref, dp_ref[...], dpn_ref)
        draw = (_row(cw_ref, 0) * _shift_rows(ext, 1)[8:8 + TM] + _row(cw_ref, 1) * ext[8:8 + TM]
                + _row(cw_ref, 2) * _shift_rows(ext, -1)[8:8 + TM] + _row(cw_ref, 3) * _shift_rows(ext, -2)[8:8 + TM])
        dproj_ref[:, A_Z:A_XBC] = dz_ref[...]
        dproj_ref[:, A_XBC:A_QA] = draw.astype(BF)
        dproj_ref[:, A_QA:A_KVA] = dqa_ref[...]
        dproj_ref[:, A_KVA:A_POOL] = dkva_ref[...]
        dproj_ref[:, A_POOL:A_KR] = dpool_ref[...]
        dproj_ref[:, A_KR:A_DT] = dkr_ref[...]
        dproj_ref[:, A_DT:PC] = ddt_ref[...]
        dh = _dot_nt(dproj_ref[...], w_ref[...])
        _, xn, r = _norm_mod(R.xval(i, x_refs), nw_ref[...], _row(mod_ref, 0), _row(mod_ref, 1))
        dx, dsh, dsc, dnw = _norm_mod_bwd(dh, xn, r, nw_ref[...], _row(mod_ref, 1))
        dx_ref[...] = dxm_ref[...] + dx
        _acc_rows(dmod_ref, R.first_of_stream(i), {0: dsh, 1: dsc})
        _acc_rows(dnw_ref, jnp.logical_and(b == 0, i == 0), {0: dnw})

    return R.call("in_bwd", body,
                  xin + [(dxm, R.row(D)), (mod, R.mod()), (nw1, R.const((1, D))), (dz, R.row(SSD_IN)), (dpre, R.row(XBC)),
                   (dpre, R.prev8(XBC)), (dpre, R.next8(XBC)), (dqa, R.row(QL)), (dkva, R.row(QL)), (dpool, R.row(PD)), (dkr, R.row(HP)),
                   (ddtr, R.row(HP)), (conv_w8, R.const((8, XBC))), (w_arr, R.const((D, PC)))],
                  [dx_out, (_sds((B, T, PC), BF), R.row(PC)), (_sds((B, 2, 8, D), F32), R.mod()),
                   (_sds((8, D), F32), R.const((8, D)))], extra=12 << 20)


def _adaln_fwd(cs, mod_w):
    L, _, C = mod_w.shape

    def body(c_ref, w_ref, o_ref):
        s = _silu(c_ref[...]).astype(BF)
        for l in range(L):
            o_ref[l] = jnp.dot(s, w_ref[l].astype(BF), preferred_element_type=F32)

    return pl.pallas_call(functools.partial(body), name="adaln_fwd", out_shape=_sds((L, 24, C), F32),
                          compiler_params=_params(None, _nbytes(mod_w.shape, F32) + _nbytes((L, 24, C), F32), 8 << 20))(cs, mod_w)


def _adaln_bwd(cs, dm, mod_w):
    L, _, C = mod_w.shape

    def body(c_ref, dm_ref, w_ref, gw_ref, gc_ref):
        c = c_ref[...]
        s = _silu(c).astype(BF)
        acc = jnp.zeros((24, D), F32)
        for l in range(L):
            dmb = dm_ref[l].astype(BF)
            gw_ref[l] = _dot_tn(s, dmb)
            acc = acc + _dot_nt(dmb, w_ref[l])
        gc_ref[...] = acc * _dsilu(c)

    return pl.pallas_call(functools.partial(body), name="adaln_bwd", out_shape=[_sds((L, D, C), F32), _sds((24, D), F32)],
                          compiler_params=_params(None, 2 * _nbytes(mod_w.shape, F32), 8 << 20))(cs, dm, mod_w)


def _sum_blocks(name, parts):
    Pn, Rr, C = parts.shape

    def body(p_ref, o_ref):
        acc = p_ref[0]
        for k in range(1, Pn):
            acc = acc + p_ref[k]
        o_ref[...] = acc

    return pl.pallas_call(functools.partial(body), name=name, out_shape=_sds((Rr, C), F32),
                          compiler_params=_params(None, _nbytes(parts.shape, F32), 4 << 20))(parts)


def _adamw(name, parts, w, m, v, rt):
    nseg = len(parts)
    Pn, rseg, C = parts[0].shape
    Rr = nseg * rseg
    tiles = rseg // rt
    assert rseg % rt == 0 and w.shape == (Rr, C)
    c1 = 1.0 - ADAM_B1 ** ADAM_STEP
    c2 = 1.0 - ADAM_B2 ** ADAM_STEP

    def body(*refs):
        p_refs = refs[:nseg]
        w_ref, m_ref, v_ref, g_ref, d_ref, nm_ref, nv_ref = refs[nseg:]
        i = pl.program_id(0)
        g = None
        for s, p_ref in enumerate(p_refs):
            gs = p_ref[0].astype(F32)
            for k in range(1, Pn):
                gs = gs + p_ref[k].astype(F32)
            g = gs if g is None else jnp.where(i >= s * tiles, gs, g)
        mn = ADAM_B1 * m_ref[...] + (1.0 - ADAM_B1) * g
        vn = ADAM_B2 * v_ref[...] + (1.0 - ADAM_B2) * jnp.square(g)
        g_ref[...] = g
        nm_ref[...] = mn
        nv_ref[...] = vn
        d_ref[...] = -ADAM_LR * ((mn / c1) / (jnp.sqrt(vn / c2) + ADAM_EPS) + ADAM_WD * w_ref[...])

    spec = pl.BlockSpec((rt, C), lambda i: (i, 0))
    pspec = lambda s: pl.BlockSpec((Pn, rt, C), lambda i: (0, jnp.clip(i - s * tiles, 0, tiles - 1), 0))
    bb = nseg * Pn * _nbytes((rt, C), parts[0].dtype) + 7 * _nbytes((rt, C), F32)
    return pl.pallas_call(
        functools.partial(body), name=name, grid=(Rr // rt,),
        in_specs=[pspec(s) for s in range(nseg)] + [spec, spec, spec],
        out_specs=[spec] * 4, out_shape=[_out((Rr, C), F32)] * 4,
        compiler_params=_params(("arbitrary",), bb, 4 << 20),
    )(*_hbm(list(parts) + [w, m, v]))


MESH = pl.DeviceIdType.MESH


def _my_pos():
    return lax.axis_index("x"), lax.axis_index("y"), lax.axis_index("c")


def _dev_index(x, y, c):
    return 4 * x + 2 * y + c


def _all_gather(name, shards):
    n = len(shards)

    def body(*refs):
        ins, outs = refs[:n], refs[n:2 * n]
        send_sems, recv_sems, local_sem = refs[2 * n:]
        x, y, c = _my_pos()
        me, sibling = (x, y, c), (x, y, 1 - c)
        chips = [(1 - x, y), (x, 1 - y), (1 - x, 1 - y)]

        def copy(t, k, block, to, src=None):
            slot = outs[t].at[_dev_index(*block)]
            return pltpu.make_async_remote_copy(
                src_ref=slot if src is None else src, dst_ref=slot,
                send_sem=send_sems.at[t, k], recv_sem=recv_sems.at[t, k], device_id=to, device_id_type=MESH)

        mine = [pltpu.make_async_copy(ins[t], outs[t].at[_dev_index(*me)], local_sem.at[t]) for t in range(n)]
        for cp in mine:
            cp.start()
        first = []
        for t in range(n):
            first.append(copy(t, 0, me, sibling, src=ins[t]))
            first += [copy(t, 1 + j, me, (*chip, c), src=ins[t]) for j, chip in enumerate(chips)]
        for cp in first:
            cp.start()
        passed = []
        for j, chip in enumerate(chips):
            for t in range(n):
                copy(t, 1 + j, (*chip, c), me).wait_recv()
                cp = copy(t, 4 + j, (*chip, c), sibling)
                cp.start()
                passed.append(cp)
        for t in range(n):
            copy(t, 0, sibling, me).wait_recv()
            for j, chip in enumerate(chips):
                copy(t, 4 + j, (*chip, 1 - c), me).wait_recv()
        for cp in first + passed:
            cp.wait_send()
        for cp in mine:
            cp.wait()

    return pl.pallas_call(
        functools.partial(body), name=name,
        in_specs=[ANY] * n, out_specs=[ANY] * n,
        out_shape=[_sds((NDEV,) + s.shape, s.dtype) for s in shards],
        scratch_shapes=[pltpu.SemaphoreType.DMA((n, 7)), pltpu.SemaphoreType.DMA((n, 7)), pltpu.SemaphoreType.DMA((n,))],
    )(*shards)


def _all_to_all(name, parts):
    n = len(parts)

    def body(*refs):
        ins, outs = refs[:n], refs[n:2 * n]
        send_sems, recv_sems, local_sem = refs[2 * n:]
        x, y, c = _my_pos()
        me = _dev_index(x, y, c)
        peers = [(x ^ ((k >> 2) & 1), y ^ ((k >> 1) & 1), c ^ (k & 1)) for k in range(1, NDEV)]
        mine = [pltpu.make_async_copy(ins[t].at[me], outs[t].at[me], local_sem.at[t]) for t in range(n)]
        for cp in mine:
            cp.start()
        sends = []
        for t in range(n):
            for k, peer in enumerate(peers):
                cp = pltpu.make_async_remote_copy(
                    src_ref=ins[t].at[_dev_index(*peer)], dst_ref=outs[t].at[me],
                    send_sem=send_sems.at[t, k], recv_sem=recv_sems.at[t, k], device_id=peer, device_id_type=MESH)
                cp.start()
                sends.append(cp)
        for t in range(n):
            for k, peer in enumerate(peers):
                slot = outs[t].at[_dev_index(*peer)]
                pltpu.make_async_remote_copy(
                    src_ref=slot, dst_ref=slot, send_sem=send_sems.at[t, k], recv_sem=recv_sems.at[t, k],
                    device_id=peer, device_id_type=MESH).wait_recv()
        for cp in sends:
            cp.wait_send()
        for cp in mine:
            cp.wait()

    return pl.pallas_call(
        functools.partial(body), name=name,
        in_specs=[ANY] * n, out_specs=[ANY] * n,
        out_shape=[_sds(p.shape, p.dtype) for p in parts],
        scratch_shapes=[pltpu.SemaphoreType.DMA((n, 7)), pltpu.SemaphoreType.DMA((n, 7)), pltpu.SemaphoreType.DMA((n,))],
    )(*parts)


SEM = pl.BlockSpec(memory_space=pltpu.SEMAPHORE)
IN_HBM = pl.BlockSpec(memory_space=pltpu.HBM)
DATAFLOW = pltpu.SideEffectType.DATAFLOW_SIDE_EFFECTING


def _flip_peers(x, y, c):
    return [(x ^ ((k >> 2) & 1), y ^ ((k >> 1) & 1), c ^ (k & 1)) for k in range(1, NDEV)]


def _split_copies(srcs, lands, send_sems, recv_sems, gather):
    x, y, c = _my_pos()
    me = _dev_index(x, y, c)
    out = []
    for t in range(len(srcs)):
        for k, peer in enumerate(_flip_peers(x, y, c)):
            p = _dev_index(*peer)
            src = srcs[t] if gather else srcs[t].at[p]
            sems = dict(send_sem=send_sems.at[7 * t + k], recv_sem=recv_sems.at[7 * t + k], device_id=peer, device_id_type=MESH)
            out.append((pltpu.make_async_remote_copy(src_ref=src, dst_ref=lands[t].at[me], **sems),
                        pltpu.make_async_remote_copy(src_ref=src, dst_ref=lands[t].at[p], **sems)))
    return out


def _exchange_start(name, collective_id, srcs, gather):
    n = len(srcs)
    lands = [lax.empty(((NDEV,) + s.shape) if gather else s.shape, s.dtype) for s in srcs]

    def body(*refs):
        src_refs, land_refs = refs[:n], refs[n:2 * n]
        send_sems, recv_sems = refs[2 * n], refs[2 * n + 1]
        token = refs[-1]
        barrier = pltpu.get_barrier_semaphore()
        for peer in _flip_peers(*_my_pos()):
            pl.semaphore_signal(barrier, inc=1, device_id=peer, device_id_type=MESH)
        pl.semaphore_wait(barrier, NDEV - 1)
        for send, _ in _split_copies(src_refs, land_refs, send_sems, recv_sems, gather):
            send.start()
        token[...] = jnp.zeros(token.shape, token.dtype)

    hbm = lambda a: pltpu.HBM(a.shape, a.dtype)
    res = pl.pallas_call(
        functools.partial(body), name=name,
        out_shape=[pltpu.SemaphoreType.DMA((7 * n,)), pltpu.SemaphoreType.DMA((7 * n,))] + [hbm(s) for s in srcs]
        + [hbm(a) for a in lands] + [_sds((8, HP), F32)],
        in_specs=[IN_HBM] * (2 * n), out_specs=[SEM, SEM] + [IN_HBM] * (2 * n) + [pl.BlockSpec(memory_space=pltpu.VMEM)],
        input_output_aliases={i: 2 + i for i in range(2 * n)},
        compiler_params=pltpu.CompilerParams(has_side_effects=DATAFLOW, collective_id=collective_id),
    )(*_hbm(list(srcs) + lands))
    return res[0], res[1], list(res[2:2 + n]), list(res[2 + n:2 + 2 * n]), res[-1]


def _exchange_wait(name, send_sems, recv_sems, srcs, lands, after, gather):
    n = len(srcs)

    def body(*refs):
        src_refs, land_refs = refs[:n], refs[n:2 * n]
        for _, recv in _split_copies(src_refs, land_refs, refs[2 * n], refs[2 * n + 1], gather):
            recv.wait_send()
            recv.wait_recv()

    hbm = lambda a: pltpu.HBM(a.shape, a.dtype)
    res = pl.pallas_call(
        functools.partial(body), name=name,
        out_shape=[hbm(s) for s in srcs] + [hbm(a) for a in lands],
        in_specs=[IN_HBM] * (2 * n) + [SEM, SEM, ANY], out_specs=[IN_HBM] * (2 * n),
        input_output_aliases={i: i for i in range(2 * n)},
        compiler_params=pltpu.CompilerParams(has_side_effects=DATAFLOW),
    )(*srcs, *lands, send_sems, recv_sems, after)
    return list(res[:n]), list(res[n:])


def _with_own(lands, own_blocks, me):
    out = []
    for land, own in zip(lands, own_blocks):
        out.append(lax.dynamic_update_slice(land, own[None], (me,) + (0,) * own.ndim))
    return out


def _arrange_w_in(w):
    z = lambda n: jnp.zeros((w.shape[0], n), w.dtype)
    return jnp.concatenate([w[:, 0:1280], w[:, 1292:1548], w[:, 1548:1804], w[:, 1836:2092],
                            z(64), w[:, 1804:1836], z(32), w[:, 1280:1292], z(HP - 2 * H)], axis=1)


def _unarrange_w_in(g):
    return jnp.concatenate([g[:, 0:1280], g[:, A_DT:A_DT + 2 * H], g[:, A_QA:A_KVA], g[:, A_KVA:A_POOL],
                            g[:, A_KR + NOPE:A_KR + NOPE + ROPE], g[:, A_POOL:A_KR]], axis=1)


def _pad_heads(w, width):
    k = w.shape[0]
    return jnp.pad(w.reshape(k, H, width), ((0, 0), (0, 0), (0, HP - width))).reshape(k, H * HP)


def _unpad_heads(g, width):
    k = g.shape[0]
    return g.reshape(k, H, HP)[:, :, :width].reshape(k, H * width)


def _arrange_w_out(w):
    att = jnp.pad(w[SSD_IN:2 * SSD_IN].reshape(H, VH, D), ((0, 0), (0, HP - VH), (0, 0))).reshape(QW, D)
    return jnp.concatenate([w[0:SSD_IN], att, w[2 * SSD_IN:]], axis=0)


def _unarrange_w_out(g):
    att = g[SSD_IN:SSD_IN + QW].reshape(H, HP, D)[:, :VH].reshape(SSD_IN, D)
    return jnp.concatenate([g[0:SSD_IN], att, g[SSD_IN + QW:]], axis=0)


def _rope_tables(T):
    n = T - CTX
    rows = n // GRID_W
    pairs = ROPE // 4
    inv = ROPE_THETA ** (-jnp.arange(pairs, dtype=F32) / pairs)
    ar = jnp.arange(rows, dtype=F32)[:, None] * inv
    ac = jnp.arange(GRID_W, dtype=F32)[:, None] * inv
    by_row = lambda a: jnp.repeat(a, GRID_W, axis=0)
    by_col = lambda a: jnp.tile(a, (rows, 1))
    cos = jnp.concatenate([by_row(jnp.cos(ar))] * 2 + [by_col(jnp.cos(ac))] * 2, axis=1)
    sin = jnp.concatenate([-by_row(jnp.sin(ar)), by_row(jnp.sin(ar)), -by_col(jnp.sin(ac)), by_col(jnp.sin(ac))], axis=1)
    ones, zeros = jnp.ones((n, NOPE), F32), jnp.zeros((n, NOPE), F32)
    cos = jnp.concatenate([ones, cos, ones[:, :HP - QK]], axis=1)
    sin = jnp.concatenate([zeros, sin, zeros[:, :HP - QK]], axis=1)
    return (jnp.concatenate([jnp.ones((CTX, HP), F32), cos], axis=0),
            jnp.concatenate([jnp.zeros((CTX, HP), F32), sin], axis=0))


def _lane_pad(v, n):
    return jnp.pad(v, (0, n - v.shape[0]))[None, :]


def _layer_weights(w_in, w_q_b, w_kv_b, conv_w, pool_w):
    kv = w_kv_b.reshape(QL, H, NOPE + VH)
    wbd = jnp.concatenate([jnp.pad(pool_w[g], ((0, 0), (64 * g, PD - 64 * (g + 1)))) for g in range(4)], axis=0)
    return dict(
        w_in=_arrange_w_in(w_in).astype(BF),
        wq=_pad_heads(w_q_b, QK).astype(BF),
        wk=_pad_heads(kv[:, :, :NOPE].reshape(QL, H * NOPE), NOPE).astype(BF),
        wv=_pad_heads(kv[:, :, NOPE:].reshape(QL, H * VH), VH).astype(BF),
        conv_w8=jnp.pad(conv_w, ((0, 4), (0, 0))), wbd=wbd)


def _layer_fwd(R, x, mod, lw, sp, late, cos, sin, head=None):
    B, T = R.B, R.T
    z, xbc_raw, qa, kva, pool_in, kr, dt_raw, h1 = _in_proj(R, x, mod, sp["nw1"], lw["w_in"])
    xbc, dt2 = _ssd_prep(R, xbc_raw, dt_raw, lw["conv_w8"], sp["conv_b"], sp["dtb"])
    yf, yb, hin_f, hin_b = _ssd_scan(B, T, xbc, dt2, sp["alog2"])
    qt, kt, q_t, k_t, v_t, cq, ckv = _mla_prep(R, qa, kva, kr, sp["qnw"], sp["kvnw"], lw["wq"], lw["wk"], lw["wv"], cos, sin)
    o, lse = _flash_fwd(B, T, q_t, kt, v_t)
    w_out, w1, w2 = late(o)
    w_out = _arrange_w_out(w_out).astype(BF)
    xmid, cat, mixbf = _out_proj(R, x, mod, yf, yb, xbc, z, o, pool_in, lw["wbd"], sp["pscale"], sp["dsk"], sp["snw"], w_out)
    xo, h2, ubf, ybf, *tail = _mlp_fwd(R, xmid, mod, sp["nw2"], w1, w2, head)
    if head is not None:
        xo = (xo, tail[0], tail[1])
    saved = dict(x=x, z=z, xbc_raw=xbc_raw, qa=qa, kva=kva, pool_in=pool_in, dt_raw=dt_raw, h1=h1, xbc=xbc, dt2=dt2,
                 yf=yf, yb=yb, hin_f=hin_f, hin_b=hin_b, qt=qt, kt=kt, q_t=q_t, k_t=k_t, v_t=v_t, cq=cq, ckv=ckv, o=o, lse=lse,
                 cat=cat, mixbf=mixbf, xmid=xmid, h2=h2, ubf=ubf, ybf=ybf, w_out=w_out, w1=w1, w2=w2)
    return xo, saved


def _layer_bwd(R, dxo, sv, mod, lw, sp, cos, sin, on_mlp=None, on_mid=None, latent_only=False):
    B, T = R.B, R.T
    dxm, du, abf, dyb, dmod_a, dnw2 = _mlp_bwd(R, dxo, sv["xmid"], sv["ubf"], sv["ybf"], mod, sp["nw2"], sv["w1"], sv["w2"])
    g_w1 = _tn_matmul("dw_mlp1", sv["h2"], du, 4 * FSH, sub=4)
    g_w2 = _tn_matmul("dw_mlp2", abf, dyb, D)[0].reshape(NDEV, FSH, D)
    snw = sp["snw"]
    tok = on_mlp(g_w1, g_w2) if on_mlp is not None else None
    if tok is not None:
        snw = snw + tok
    dmix, dz, dyt, dxsk, qs, dos, do_t, dpo, dmod_b, dvec_o = _out_bwd(R, dxm, mod, sv["mixbf"], sv["yf"], sv["yb"], sv["xbc"], sv["z"],
                                                                 sv["o"], sv["qt"], sv["lse"], sp["dsk"], snw, sv["w_out"])
    g_wout = _unarrange_w_out(_tn_matmul("dw_out", sv["cat"], dmix, D)[0])
    dpool_in, g_wbd, dpsc = _pool_bwd(R, dpo, sv["pool_in"], lw["wbd"], sp["pscale"])
    dqt, dkt, dvt = _flash_bwd(B, T, qs, sv["kt"], dos, sv["q_t"], sv["k_t"], sv["v_t"], do_t)
    dqp, dkvb, dqa, dkva, dkr, dnw_qk = _mla_bwd(R, dqt, dkt, dvt, sv["qa"], sv["kva"], sp["qnw"], sp["kvnw"],
                                                 lw["wq"], lw["wk"], lw["wv"], cos, sin)
    g_wq = _unpad_heads(_tn_matmul("dw_q", sv["cq"], dqp, QW)[0], QK)
    g_kv = _tn_matmul("dw_kv", sv["ckv"], dkvb, 2 * QW)[0]
    g_wk = _unpad_heads(g_kv[:, :QW], NOPE).reshape(QL, H, NOPE)
    g_wv = _unpad_heads(g_kv[:, QW:], VH).reshape(QL, H, VH)
    g_wkv = jnp.concatenate([g_wk, g_wv], axis=2).reshape(QL, H * (NOPE + VH))
    alog2 = sp["alog2"]
    tok = on_mid(g_wout, g_wq, g_wkv) if on_mid is not None else None
    if tok is not None:
        alog2 = alog2 + tok
    dxbc_f, dxbc_b, ddt_f, ddt_b, da = _ssd_scan_bwd(B, T, dyt, sv["xbc"], sv["dt2"], alog2, sv["hin_f"], sv["hin_b"])
    dpre, ddtr, dcw, dcb, ddtb = _ssd_prep_bwd(R, dxbc_f, dxbc_b, dxsk, ddt_f, ddt_b, sv["xbc_raw"], sv["dt_raw"], lw["conv_w8"],
                                                sp["conv_b"], sp["dtb"])
    dx, dproj, dmod_c, dnw1 = _in_bwd(R, dxm, sv["x"], mod, sp["nw1"], dz, dpre, dqa, dkva, dpool_in, dkr, ddtr,
                                      lw["conv_w8"], lw["w_in"], latent_only)
    g_win = _unarrange_w_in(jnp.concatenate(list(_tn_matmul("dw_in", sv["h1"], dproj, PC)), axis=1))
    a2 = -jnp.exp(sp["alog2"][:, 0, :H])
    small = dict(
        norm1_w=dnw1[0], norm2_w=dnw2[0], conv_w=dcw[0:4], conv_b=dcb[0], dt_bias=ddtb[0, :2 * H].reshape(2, H),
        a_log=jnp.sum(da[:, :, 0, :H], axis=0) * a2, ssd_d=jnp.sum(dvec_o[1].reshape(H, P), axis=1), ssd_norm_w=dvec_o[0],
        q_a_norm_w=dnw_qk[0], kv_a_norm_w=dnw_qk[1],
        pool_w=jnp.stack([g_wbd[64 * g:64 * (g + 1), 64 * g:64 * (g + 1)] for g in range(4)]), pool_scale=dpsc[0])
    big = dict(w_in=g_win, w_out=g_wout, w_q_b=g_wq, w_kv_b=g_wkv, w_mlp1=g_w1, w_mlp2=g_w2)
    return dx, big, small, dmod_a + dmod_b + dmod_c


def _small_params(l, norm1_w, norm2_w, conv_b, dt_bias, a_log, ssd_d, ssd_norm_w, q_a_norm_w, kv_a_norm_w, pool_scale):
    alog2 = jnp.broadcast_to(jnp.pad(a_log[l], ((0, 0), (0, HP - H)))[:, None, :], (2, 8, HP))
    return dict(nw1=norm1_w[l][None], nw2=norm2_w[l][None], conv_b=conv_b[l][None],
                dtb=_lane_pad(dt_bias[l].reshape(2 * H), HP), alog2=alog2,
                dsk=jnp.repeat(ssd_d[l], P)[None], snw=ssd_norm_w[l][None], qnw=q_a_norm_w[l][None],
                kvnw=kv_a_norm_w[l][None], pscale=pool_scale[l][None])


SMALL_NAMES = ["mod_b", "norm1_w", "norm2_w", "conv_b", "dt_bias", "a_log", "ssd_d", "ssd_norm_w", "q_a_norm_w",
               "kv_a_norm_w", "pool_w", "pool_scale", "final_norm_w"]


def _pack(arrs):
    rows = []
    for a in arrs:
        f = a.reshape(-1).astype(F32)
        n = -(-f.shape[0] // HP) * HP
        rows.append(jnp.pad(f, (0, n - f.shape[0])).reshape(-1, HP))
    out = jnp.concatenate(rows, axis=0)
    pad = (-out.shape[0]) % 8
    return jnp.pad(out, ((0, pad), (0, 0)))


def _unpack(pack, like):
    outs, r = [], 0
    for a in like:
        n = math.prod(a.shape)
        nr = -(-n // HP)
        outs.append(pack[r:r + nr].reshape(-1)[:n].reshape(a.shape))
        r += nr
    return outs


def _local_step(x, ctx, target, mods, full_of, small_w, on_grads=None, on_mlp=None, on_mid=None):
    B, N = x.shape[0], x.shape[1]
    T = CTX + N
    R = _Rows(B, T)
    cos, sin = _rope_tables(T)
    xu = (ctx, x)
    L = len(mods)
    lws, sps, saves = [], [], []
    for l in range(L):
        f = full_of(l, xu)
        lws.append(_layer_weights(f["w_in"], f["w_q_b"], f["w_kv_b"], f["conv_w"], small_w["pool_w"][l]))
        sps.append(_small_params(l, *[small_w[k] for k in ["norm1_w", "norm2_w", "conv_b", "dt_bias", "a_log", "ssd_d",
                                                          "ssd_norm_w", "q_a_norm_w", "kv_a_norm_w", "pool_scale"]]))
        head = (target, small_w["final_norm_w"][None]) if l == L - 1 else None
        xu, sv = _layer_fwd(R, xu, mods[l], lws[l], sps[l], f["late"], cos, sin, head)
        saves.append(sv)
    dx, loss8, dfnw = xu
    bigs, smalls, dmods = [None] * L, [None] * L, [None] * L
    for l in reversed(range(L)):
        hook = functools.partial(on_mlp, l) if on_mlp is not None else None
        hook2 = functools.partial(on_mid, l) if on_mid is not None else None
        dx, bigs[l], smalls[l], dmods[l] = _layer_bwd(R, dx, saves[l], mods[l], lws[l], sps[l], cos, sin, hook, hook2, l == 0)
        if on_grads is not None:
            bigs[l], tok = on_grads(l, bigs[l], dx)
            if tok is not None:
                sps[l - 1] = dict(sps[l - 1], nw2=sps[l - 1]["nw2"] + tok)
    return loss8[0, 0], dx, bigs, smalls, dfnw[0], dmods


def kernel(x, c, ctx, c_ctx, mod_w, mod_b, norm1_w, norm2_w, w_in, conv_w, conv_b, dt_bias, a_log, ssd_d, ssd_norm_w, q_a_norm_w, w_q_b, kv_a_norm_w, w_kv_b, pool_w, pool_scale, w_out, w_mlp1, w_mlp2, final_norm_w, loss_target, m_c_ctx, m_mod_w, m_mod_b, m_norm1_w, m_norm2_w, m_w_in, m_conv_w, m_conv_b, m_dt_bias, m_a_log, m_ssd_d, m_ssd_norm_w, m_q_a_norm_w, m_w_q_b, m_kv_a_norm_w, m_w_kv_b, m_pool_w, m_pool_scale, m_w_out, m_w_mlp1, m_w_mlp2, m_final_norm_w, v_c_ctx, v_mod_w, v_mod_b, v_norm1_w, v_norm2_w, v_w_in, v_conv_w, v_conv_b, v_dt_bias, v_a_log, v_ssd_d, v_ssd_norm_w, v_q_a_norm_w, v_w_q_b, v_kv_a_norm_w, v_w_kv_b, v_pool_w, v_pool_scale, v_w_out, v_w_mlp1, v_w_mlp2, v_final_norm_w):
    args = dict(locals())
    B = x.shape[0]
    L = mod_w.shape[0]
    me = _dev_index(*_my_pos())
    CS = mod_w.shape[2]

    big_names = ["w_in", "w_out", "w_q_b", "w_kv_b", "w_mlp1", "w_mlp2"]
    shards = {n: args[n].astype(BF) for n in big_names}
    early, late_names = ["w_in", "w_q_b", "w_kv_b"], ["w_out", "w_mlp1", "w_mlp2"]
    g0 = _all_gather("gather_weights", [c, conv_w] + [shards[n][0] for n in early])
    c_all, convw_all = g0[0], g0[1]
    gathered = {0: dict(zip(early, g0[2:]))}
    cs = jnp.concatenate([c_all.reshape(NDEV * B, D), c_ctx[None], jnp.zeros((24 - NDEV * B - 1, D), F32)], axis=0)
    m_loc = _adaln_fwd(cs, mod_w)
    m_all = _all_gather("gather_mod", [m_loc])[0]
    m_full = jnp.moveaxis(m_all, 0, 2).reshape(L, 24, NDEV * CS) + mod_b[:, None, :]
    pending = {}
    tok = jnp.zeros((), F32)
    res = _exchange_start("gather_start_0", 1, [shards[n][0] for n in late_names], gather=True)
    pending[0] = res[:4]
    tok = tok + res[4][0, 0]
    for l in range(1, L):
        res = _exchange_start("gather_start_%d" % l, 1 + l, [shards[n][l] for n in early + late_names], gather=True)
        pending[l] = res[:4]
        tok = tok + res[4][0, 0]
    mods = []
    for l in range(L):
        ex = lax.dynamic_slice(m_full[l], (me * B, 0), (B, 6 * D)).reshape(B, 6, D)
        cc = jnp.broadcast_to(m_full[l, NDEV * B].reshape(1, 6, D), (B, 6, D))
        mods.append(jnp.pad(jnp.stack([cc, ex], axis=1), ((0, 0), (0, 0), (0, 2), (0, 0))) + tok)

    def full_of(l, xu):
        if l > 0:
            own, lands = _exchange_wait("gather_wait_%d" % l, *pending.pop(l), xu, gather=True)
            gathered[l] = dict(zip(early + late_names, _with_own(lands, own, me)))

        def late(after):
            if l == 0:
                own, lands = _exchange_wait("gather_wait_0", *pending.pop(0), after, gather=True)
                gathered[0].update(zip(late_names, _with_own(lands, own, me)))
            gl = gathered[l]
            return gl["w_out"].reshape(D, D), gl["w_mlp1"], gl["w_mlp2"]

        g = gathered[l]
        return dict(
            w_in=g["w_in"].reshape(D, IN_COLS),
            w_q_b=jnp.moveaxis(g["w_q_b"], 0, 1).reshape(QL, H * QK),
            w_kv_b=jnp.moveaxis(g["w_kv_b"], 0, 1).reshape(QL, H * (NOPE + VH)),
            conv_w=jnp.moveaxis(convw_all[:, l], 0, 1).reshape(4, XBC), late=late)

    def grad_blocks(big, names):
        make = dict(
            w_in=lambda g: g.reshape(NDEV, D // NDEV, IN_COLS), w_out=lambda g: g.reshape(NDEV, D // NDEV, D),
            w_q_b=lambda g: jnp.moveaxis(g.reshape(QL, NDEV, -1), 1, 0),
            w_kv_b=lambda g: jnp.moveaxis(g.reshape(QL, NDEV, -1), 1, 0), w_mlp1=lambda g: g, w_mlp2=lambda g: g)
        return [make[n](big[n]).astype(BF) for n in names]

    sent, sent_mlp, sent_mid = {}, {}, {}
    rest_names, mid_names = ["w_in"], ["w_out", "w_q_b", "w_kv_b"]

    def on_mid(l, g_wout, g_wq, g_wkv):
        if l > 0:
            return None
        blocks_mid = grad_blocks(dict(w_out=g_wout, w_q_b=g_wq, w_kv_b=g_wkv), mid_names)
        res = _exchange_start("grads_mid_start_0", 2 + 2 * L, blocks_mid, gather=False)
        sent_mid[l] = res[:4]
        return res[4][0, 0]

    def on_mlp(l, g_w1, g_w2):
        if l > 0:
            return None
        res = _exchange_start("grads_mlp_start_0", 1 + 2 * L, [g_w1.astype(BF), g_w2.astype(BF)], gather=False)
        sent_mlp[l] = res[:4]
        return res[4][0, 0]

    def on_grads(l, big, dx):
        if l == 0:
            return grad_blocks(big, rest_names), None
        res = _exchange_start("grads_start_%d" % l, 1 + L + l, grad_blocks(big, big_names), gather=False)
        sent[l] = res[:4]
        return None, res[4][0, 0]

    small_w = {k: args[k] for k in SMALL_NAMES if k != "mod_b"}
    loss_part, grad_x, blocks, smalls, dfnw, dmods = _local_step(x, ctx, loss_target, mods, full_of, small_w, on_grads, on_mlp,
                                                                 on_mid)
    loss = lax.psum(loss_part, ("x", "y", "c"))

    dm_ex = jnp.stack([dmods[l][:, 1, :6].reshape(B, 6 * D) for l in range(L)])
    dm_cc = jnp.stack([jnp.sum(dmods[l][:, 0, :6], axis=0).reshape(6 * D) for l in range(L)])
    small_parts = dict(
        mod_b=jnp.sum(dm_ex, axis=1) + dm_cc,
        **{k: jnp.stack([smalls[l][k] for l in range(L)]) for k in SMALL_NAMES[1:-1]},
        final_norm_w=dfnw, conv_w=jnp.stack([smalls[l]["conv_w"] for l in range(L)]), dm_cc=dm_cc)
    adam_grads = [small_parts[k] for k in SMALL_NAMES]
    extras = [small_parts["conv_w"], dm_cc, dm_ex]
    pack = jnp.concatenate([_pack(adam_grads), _pack(extras)], axis=0)
    pack_all = _all_gather("gather_small_grads", [pack])[0]
    wpack = _pack([args[k] for k in SMALL_NAMES])
    mpack = _pack([args["m_" + k] for k in SMALL_NAMES])
    vpack = _pack([args["v_" + k] for k in SMALL_NAMES])
    n_adam = wpack.shape[0]
    res_small = _adamw("adamw_small", [pack_all[:, :n_adam]], wpack, mpack, vpack, n_adam)
    small_out = [_unpack(r, [args[k] for k in SMALL_NAMES]) for r in res_small]
    ext_all = pack_all[:, n_adam:]
    g_conv_full, dm_cc_tot, _ = _unpack(_sum_blocks("sum_small_grads", ext_all), extras)
    dm_ex_all = jnp.stack([_unpack(ext_all[k], extras)[2] for k in range(NDEV)], axis=1)
    dm_rows = jnp.concatenate([dm_ex_all.reshape(L, NDEV * B, 6 * D), dm_cc_tot[:, None, :],
                               jnp.zeros((L, 24 - NDEV * B - 1, 6 * D), F32)], axis=1)
    dm_loc = lax.dynamic_slice(dm_rows, (0, 0, me * CS), (L, 24, CS))
    g_modw, gc_part = _adaln_bwd(cs, dm_loc, mod_w)
    gc_all = _all_gather("gather_cctx_grad", [gc_part[NDEV * B:NDEV * B + 8]])[0]
    cpad = lambda a: jnp.pad(a[None], ((0, 7), (0, 0)))
    res_cc = _adamw("adamw_cctx", [gc_all], cpad(c_ctx), cpad(m_c_ctx), cpad(v_c_ctx), 8)
    cc_out = [r[0] for r in res_cc]

    def waited(name, handles, names):
        srcs, lands = _exchange_wait(name, *handles, grad_x, gather=False)
        own = [lax.dynamic_index_in_dim(s, me, 0, keepdims=False) for s in srcs]
        return dict(zip(names, _with_own(lands, own, me)))

    recv = {0: dict(zip(rest_names, _all_to_all("exchange_grads", blocks[0])))}
    recv[0].update(waited("grads_mlp_wait_0", sent_mlp.pop(0), ["w_mlp1", "w_mlp2"]))
    recv[0].update(waited("grads_mid_wait_0", sent_mid.pop(0), mid_names))
    for l in range(1, L):
        recv[l] = waited("grads_wait_%d" % l, sent.pop(l), big_names)
    big_out = {}
    for name in big_names:
        w = args[name]
        Rr, C = math.prod(w.shape[:-1]), w.shape[-1]
        rl = Rr // L
        rt = rl if rl * C <= (1 << 17) else rl // 4
        res = _adamw("adamw_" + name, [recv[l][name].reshape(NDEV, rl, C) for l in range(L)], w.reshape(Rr, C),
                     args["m_" + name].reshape(Rr, C), args["v_" + name].reshape(Rr, C), rt)
        big_out[name] = [r.reshape(w.shape) for r in res]
    res = _adamw("adamw_mod_w", [g_modw.reshape(1, L * D, CS)], mod_w.reshape(L * D, CS), m_mod_w.reshape(L * D, CS),
                 v_mod_w.reshape(L * D, CS), L * D // 8)
    big_out["mod_w"] = [r.reshape(mod_w.shape) for r in res]
    CW = conv_w.shape[2]
    g_conv = lax.dynamic_slice(g_conv_full, (0, 0, me * CW), (L, 4, CW))
    res = _adamw("adamw_conv_w", [g_conv.reshape(1, L * 4, CW)], conv_w.reshape(L * 4, CW), m_conv_w.reshape(L * 4, CW),
                 v_conv_w.reshape(L * 4, CW), L * 4)
    big_out["conv_w"] = [r.reshape(conv_w.shape) for r in res]

    weights = ["c_ctx", "mod_w", "mod_b", "norm1_w", "norm2_w", "w_in", "conv_w", "conv_b", "dt_bias", "a_log", "ssd_d",
               "ssd_norm_w", "q_a_norm_w", "w_q_b", "kv_a_norm_w", "w_kv_b", "pool_w", "pool_scale", "w_out", "w_mlp1",
               "w_mlp2", "final_norm_w"]
    outs = [loss, grad_x]
    for kind in range(4):
        for name in weights:
            if name == "c_ctx":
                outs.append(cc_out[kind])
            elif name in big_out:
                outs.append(big_out[name][kind])
            else:
                outs.append(small_out[kind][SMALL_NAMES.index(name)])
    return tuple(outs)
```

```python
import functools
import math

import jax
import jax.numpy as jnp
from jax import lax
from jax.experimental import pallas as pl
from jax.experimental.pallas import tpu as pltpu

F32 = jnp.float32
BF = jnp.bfloat16
MXU = BF

D = 1024
CTX = 256
GRID_W = 64
EPS = 1e-6
H = 6
P = 64
SSD_IN = 384
NST = 128
XBC = 896
CH = 128
QL = 256
NOPE = 64
ROPE = 32
VH = 64
QK = 96
HP = 128
QW = H * HP
PD = 256
FF = 4096
IN_COLS = 2092
ROPE_THETA = 10000.0
PC = 2304
A_Z, A_XBC, A_QA, A_KVA, A_POOL, A_KR, A_DT = 0, 384, 1280, 1536, 1792, 2048, 2176
MIXW = SSD_IN + QW + PD
NDEV = 8
FSH = FF // NDEV
TM = 256
TQ = 256
TK = 256
VMEM_CAP = 64 * 1024 * 1024
ADAM_LR, ADAM_B1, ADAM_B2, ADAM_EPS, ADAM_WD, ADAM_STEP = 0.001, 0.9, 0.999, 1e-08, 0.01, 10


def _nbytes(shape, dtype):
    n = 1
    for s in shape:
        if s is not None:
            n *= s
    return n * jnp.dtype(dtype).itemsize


def _params(sem, block_bytes, extra=0):
    lim = min(2 * block_bytes + extra + (8 << 20), VMEM_CAP - (6 << 20))
    return pltpu.CompilerParams(dimension_semantics=sem, vmem_limit_bytes=int(lim))


def _hbm(arrays):
    return [pltpu.with_memory_space_constraint(a, pltpu.HBM) for a in arrays]


def _dot(a, b):
    return jnp.dot(a.astype(MXU), b.astype(MXU), preferred_element_type=F32)


def _dot_nt(a, b):
    return lax.dot_general(a.astype(MXU), b.astype(MXU), (((1,), (1,)), ((), ())), preferred_element_type=F32)


def _dot_tn(a, b):
    return lax.dot_general(a.astype(MXU), b.astype(MXU), (((0,), (0,)), ((), ())), preferred_element_type=F32)


def _dot01(m01, x):
    b16 = jnp.bfloat16
    m = m01.astype(b16)
    hi = x.astype(b16)
    r1 = x - hi.astype(F32)
    mid = r1.astype(b16)
    lo = (r1 - mid.astype(F32)).astype(b16)
    f = lambda v: jnp.dot(m, v, preferred_element_type=F32)
    return f(hi) + f(mid) + f(lo)


def _sigmoid(x):
    return 1.0 / (1.0 + jnp.exp(-x))


def _silu(x):
    return x * _sigmoid(x)


def _dsilu(x):
    s = _sigmoid(x)
    return s * (1.0 + x * (1.0 - s))


def _iota(shape, dim):
    return lax.broadcasted_iota(jnp.int32, shape, dim)


def _row(ref, k):
    blk = ref[...]
    return jnp.sum(jnp.where(_iota(blk.shape, 0) == k, blk, 0.0), axis=0, keepdims=True)


def _shift_rows(x, k):
    n = x.shape[0]
    return pltpu.roll(x, (-k) % n, axis=0)


class _Rows:
    def __init__(self, B, T):
        self.B, self.T = B, T
        self.nt = T // TM
        self.nct = CTX // TM

    def row(self, F):
        return pl.BlockSpec((None, TM, F), lambda b, i: (b, i, 0))

    def row2(self, F):
        return pl.BlockSpec((None, 2, TM, F), lambda b, i: (b, 0, i, 0))

    def prev8(self, F):
        return pl.BlockSpec((None, 8, F), lambda b, i: (b, jnp.maximum(i * (TM // 8) - 1, 0), 0))

    def next8(self, F):
        last = self.T // 8 - 1
        return pl.BlockSpec((None, 8, F), lambda b, i: (b, jnp.minimum((i + 1) * (TM // 8), last), 0))

    def mod(self):
        nct = self.nct
        return pl.BlockSpec((None, None, 8, D), lambda b, i: (b, jnp.where(i < nct, 0, 1), 0, 0))

    def const(self, shape):
        z = (0,) * len(shape)
        return pl.BlockSpec(tuple(shape), lambda b, i: z)

    def tgt(self, F):
        nct = self.nct
        return pl.BlockSpec((None, TM, F), lambda b, i: (b, jnp.maximum(i - nct, 0), 0))

    def xin(self, x):
        if isinstance(x, tuple):
            nct = self.nct
            return [(x[0], pl.BlockSpec((None, TM, D), lambda b, i: (b, jnp.minimum(i, nct - 1), 0))), (x[1], self.tgt(D))]
        return [(x, self.row(D))]

    def xval(self, i, refs):
        if len(refs) == 2:
            return jnp.where(i < self.nct, refs[0][...], refs[1][...])
        return refs[0][...]

    def call(self, name, body, ins, outs, scratch=(), extra=0):
        arrays = [a for a, _ in ins]
        in_specs = [s for _, s in ins]
        out_shape = [pltpu.HBM(o.shape, o.dtype) for o, _ in outs]
        out_specs = [s for _, s in outs]
        bb = 0
        for a, s in list(ins) + list(outs):
            if s.block_shape is not None:
                bb += _nbytes(s.block_shape, a.dtype)
        return pl.pallas_call(
            functools.partial(body), name=name, grid=(self.B, self.nt),
            in_specs=in_specs, out_specs=out_specs, out_shape=out_shape, scratch_shapes=list(scratch),
            compiler_params=_params(("arbitrary", "arbitrary"), bb, extra),
        )(*_hbm(arrays))

    def first_of_stream(self, i):
        return jnp.logical_or(i == 0, i == self.nct)

    def last_of_stream(self, i):
        return jnp.logical_or(i == self.nct - 1, i == self.nt - 1)

    def ext(self, i, prev_ref, cur, next_ref):
        pv = prev_ref[...].astype(F32) * jnp.where(self.first_of_stream(i), 0.0, 1.0)
        nx = next_ref[...].astype(F32) * jnp.where(self.last_of_stream(i), 0.0, 1.0)
        return jnp.concatenate([pv, cur, nx], axis=0)

    def stream_pos(self, i, rows):
        start = jnp.where(i < self.nct, 0, CTX)
        n = jnp.where(i < self.nct, CTX, self.T - CTX)
        t = i * TM - 8 - start + _iota((rows, 1), 0)
        return t, n


def _sds(shape, dtype):
    return jax.ShapeDtypeStruct(tuple(shape), dtype)


def _out(shape, dtype):
    return pltpu.HBM(tuple(shape), dtype)


def _norm_mod(x, nw, sh, sc):
    r = lax.rsqrt(jnp.mean(x * x, axis=-1, keepdims=True) + EPS)
    xn = x * r
    return xn * nw * (1.0 + sc) + sh, xn, r


def _norm_mod_bwd(dh, xn, r, nw, sc):
    dsh = jnp.sum(dh, axis=0, keepdims=True)
    dsc = jnp.sum(dh * (xn * nw), axis=0, keepdims=True)
    dnw = jnp.sum(dh * (1.0 + sc) * xn, axis=0, keepdims=True)
    dxn = dh * nw * (1.0 + sc)
    dx = r * (dxn - xn * jnp.mean(dxn * xn, axis=-1, keepdims=True))
    return dx, dsh, dsc, dnw


def _acc_rows(ref, first, rows):
    rid = _iota(ref.shape, 0)
    upd = jnp.zeros(ref.shape, F32)
    for k, v in rows.items():
        upd = upd + jnp.where(rid == k, v, 0.0)

    @pl.when(first)
    def _():
        ref[...] = upd

    @pl.when(jnp.logical_not(first))
    def _():
        ref[...] += upd


def _in_proj(R, x, mod, nw1, w_arr):
    B, T = R.B, R.T

    xin = R.xin(x)

    def body(*refs):
        x_refs = refs[:len(xin)]
        mod_ref, nw_ref, w_ref, z_ref, xbc_ref, qa_ref, kva_ref, pool_ref, kr_ref, dt_ref, h_ref = refs[len(xin):]
        h, _, _ = _norm_mod(R.xval(pl.program_id(1), x_refs), nw_ref[...], _row(mod_ref, 0), _row(mod_ref, 1))
        hb = h.astype(BF)
        h_ref[...] = hb
        p = jnp.dot(hb, w_ref[...], preferred_element_type=F32)
        z_ref[...] = p[:, A_Z:A_XBC]
        xbc_ref[...] = p[:, A_XBC:A_QA]
        qa_ref[...] = p[:, A_QA:A_KVA]
        kva_ref[...] = p[:, A_KVA:A_POOL]
        pool_ref[...] = p[:, A_POOL:A_KR]
        kr_ref[...] = p[:, A_KR:A_DT]
        dt_ref[...] = p[:, A_DT:PC]

    widths = [SSD_IN, XBC, QL, QL, PD, HP, HP]
    outs = [(_sds((B, T, w), F32), R.row(w)) for w in widths] + [(_sds((B, T, D), BF), R.row(D))]
    return R.call("in_proj", body,
                  xin + [(mod, R.mod()), (nw1, R.const((1, D))), (w_arr, R.const((D, PC)))],
                  outs, extra=8 << 20)


def _conv_pre(ext, w_ref, b_ref):
    return (_row(w_ref, 0) * _shift_rows(ext, -1)[8:8 + TM] + _row(w_ref, 1) * ext[8:8 + TM]
            + _row(w_ref, 2) * _shift_rows(ext, 1)[8:8 + TM] + _row(w_ref, 3) * _shift_rows(ext, 2)[8:8 + TM]
            + b_ref[...])


def _softplus(x):
    return jnp.maximum(x, 0.0) + jnp.log(1.0 + jnp.exp(-jnp.abs(x)))


def _ssd_prep(R, xbc_raw, dt_raw, conv_w8, conv_b, dtb):
    B, T = R.B, R.T

    def body(raw_ref, pv_ref, nx_ref, dtr_ref, w_ref, b_ref, dtb_ref, xbc_ref, dt_ref):
        i = pl.program_id(1)
        ext = R.ext(i, pv_ref, raw_ref[...], nx_ref)
        xbc_ref[...] = _silu(_conv_pre(ext, w_ref, b_ref))
        lane = _iota((TM, HP), 1)
        dtv = _softplus(dtr_ref[...] + dtb_ref[...])
        keep = lane < H
        dt_ref[0] = jnp.where(keep, dtv, 0.0)
        dt_ref[1] = jnp.where(keep, pltpu.roll(dtv, HP - H, axis=1), 0.0)

    return R.call("ssd_prep", body,
                  [(xbc_raw, R.row(XBC)), (xbc_raw, R.prev8(XBC)), (xbc_raw, R.next8(XBC)), (dt_raw, R.row(HP)),
                   (conv_w8, R.const((8, XBC))), (conv_b, R.const((1, XBC))), (dtb, R.const((1, HP)))],
                  [(_sds((B, T, XBC), F32), R.row(XBC)), (_sds((B, 2, T, HP), F32), R.row2(HP))], extra=12 << 20)


def _chunk_index(d, s, nc, nctc):
    if d == 0:
        return s
    return jnp.where(s < nctc, nctc - 1 - s, nc - 1 - (s - nctc))


def _dot01_r(x, m01, passes=3):
    b16 = jnp.bfloat16
    m = m01.astype(b16)
    out, rest = None, x
    for _ in range(passes):
        part = rest.astype(b16)
        rest = rest - part.astype(F32)
        term = jnp.dot(part, m, preferred_element_type=F32)
        out = term if out is None else out + term
    return out


def _scan_common(d, dtv, alog_ref):
    sel = _iota((HP, H * HP), 0) == (_iota((HP, H * HP), 1) >> 7)
    a_all = _dot01_r(-jnp.exp(alog_ref[d]), sel)
    dt_all = _dot01_r(dtv, sel)
    adt_all = dt_all * jnp.concatenate([a_all] * (CH // 8), axis=0)
    row = _iota((CH, CH), 0)
    col = _iota((CH, CH), 1)
    inc = col <= row if d == 0 else col >= row
    inc_t = row <= col if d == 0 else row >= col
    q_all = _dot01(inc, adt_all)
    return a_all, dt_all, adt_all, q_all, inc, inc_t


def _head_decay(h, adt_all, q_all, inc, inc_t):
    q = q_all[:, HP * h:HP * (h + 1)]
    q_t = q.T
    qtot = jnp.sum(adt_all[:, HP * h:HP * (h + 1)], axis=0, keepdims=True)
    lm = jnp.where(inc, jnp.exp(q - q_t), 0.0)
    lm_t = jnp.where(inc_t, jnp.exp(q_t - q), 0.0)
    return lm, lm_t, jnp.exp(q), jnp.exp(qtot), jnp.exp(qtot - q)


def _ssd_scan(B, T, xbc, dt2, alog2):
    nc, nctc = T // CH, CTX // CH

    def chain(d, xbc_ref, dt_ref, alog_ref, y_ref, hin_ref, hs):
        xbc_v = xbc_ref[...]
        dtv = dt_ref[...]
        blk = alog_ref[d]
        adt = dtv * -jnp.exp(jnp.sum(jnp.where(_iota(blk.shape, 0) == 0, blk, 0.0), axis=0, keepdims=True))
        row, col = _iota((CH, CH), 0), _iota((CH, CH), 1)
        inc = col <= row if d == 0 else col >= row
        inc_tf = (row <= col if d == 0 else row >= col).astype(F32)
        q = _dot01(inc, adt)
        hin_ref[...] = hs[d]
        for g in range(2):
            bg = xbc_v[:, SSD_IN + NST * g:SSD_IN + NST * (g + 1)]
            cg = xbc_v[:, SSD_IN + 2 * NST + NST * g:SSD_IN + 2 * NST + NST * (g + 1)]
            gm = _dot_nt(cg, bg)
            for r in range(3):
                h = 3 * g + r
                onehot = (_iota((1, HP), 1) == h).astype(F32)
                adt_h = jnp.sum(adt * onehot, axis=1, keepdims=True)
                qc = jnp.sum(q * onehot, axis=1, keepdims=True)
                dt_h = jnp.sum(dtv * onehot, axis=1, keepdims=True)
                qr = jnp.sum(adt_h * inc_tf, axis=0, keepdims=True)
                qtot = jnp.sum(adt_h, axis=0, keepdims=True)
                lm = jnp.where(inc, jnp.exp(qc - qr), 0.0)
                xh = xbc_v[:, P * h:P * (h + 1)] * dt_h
                hprev = hs[d, P * h:P * (h + 1), :]
                y_ref[:, P * h:P * (h + 1)] = _dot(gm * lm, xh) + jnp.exp(qc) * _dot_nt(cg, hprev)
                hs[d, P * h:P * (h + 1), :] = jnp.exp(qtot) * hprev + _dot_tn(xh * jnp.exp(qtot - qc), bg)

    def body(xf_ref, xb_ref, dtf_ref, dtb_ref, alog_ref, yf_ref, yb_ref, hf_ref, hb_ref, hs):
        @pl.when(pl.program_id(1) == 0)
        def _():
            hs[...] = jnp.zeros(hs.shape, F32)

        chain(0, xf_ref, dtf_ref, alog_ref, yf_ref, hf_ref, hs)
        chain(1, xb_ref, dtb_ref, alog_ref, yb_ref, hb_ref, hs)

    cidx = lambda d: (lambda s: _chunk_index(d, s, nc, nctc))
    specs = lambda d: dict(
        xbc=pl.BlockSpec((None, CH, XBC), lambda b, s: (b, cidx(d)(s), 0)),
        dt=pl.BlockSpec((None, None, CH, HP), lambda b, s: (b, d, cidx(d)(s), 0)),
        y=pl.BlockSpec((None, CH, SSD_IN), lambda b, s: (b, cidx(d)(s), 0)),
        h=pl.BlockSpec((None, None, SSD_IN, NST), lambda b, s: (b, cidx(d)(s), 0, 0)))
    f, r = specs(0), specs(1)
    bb = 2 * (_nbytes((CH, XBC), F32) + _nbytes((CH, HP), F32) + _nbytes((CH, SSD_IN), F32) + _nbytes((SSD_IN, NST), F32))
    return pl.pallas_call(
        functools.partial(body), name="ssd_scan", grid=(B, nc),
        in_specs=[f["xbc"], r["xbc"], f["dt"], r["dt"], pl.BlockSpec((2, 8, HP), lambda b, s: (0, 0, 0))],
        out_specs=[f["y"], r["y"], f["h"], r["h"]],
        out_shape=[_out((B, T, SSD_IN), F32)] * 2 + [_out((B, nc, SSD_IN, NST), F32)] * 2,
        scratch_shapes=[pltpu.VMEM((2, SSD_IN, NST), F32)],
        compiler_params=_params(("arbitrary",) * 2, bb, 12 << 20),
    )(*_hbm([xbc, xbc, dt2, dt2, alog2]))


def _swap8(u):
    lane = _iota(u.shape, 1)
    n = u.shape[1]
    return jnp.where((lane & 15) < 8, pltpu.roll(u, n - 8, axis=1), pltpu.roll(u, 8, axis=1))


def _rope(u, cos, sin_signed):
    return u * cos + _swap8(u) * sin_signed


def _rms(x, w):
    r = lax.rsqrt(jnp.mean(x * x, axis=-1, keepdims=True) + EPS)
    xh = x * r
    return xh * w, xh, r


def _rms_bwd(dy, xh, r, w):
    dw = jnp.sum(dy * xh, axis=0, keepdims=True)
    dxh = dy * w
    return r * (dxh - xh * jnp.mean(dxh * xh, axis=-1, keepdims=True)), dw


def _tile6(t):
    return jnp.concatenate([t] * H, axis=1)


def _per_head(fn, u):
    return jnp.concatenate([fn(u[:, HP * h:HP * (h + 1)]) for h in range(H)], axis=1)


def _mla_prep(R, qa, kva, kr, qnw, kvnw, wq, wk, wv, cos, sin):
    B, T = R.B, R.T
    scale = QK ** -0.5

    def body(qa_ref, kva_ref, kr_ref, qnw_ref, kvnw_ref, wq_ref, wk_ref, wv_ref, cos_ref, sin_ref,
             qt_ref, kt_ref, qtr_ref, ktr_ref, vtr_ref, cq_ref, ckv_ref):
        cq, _, _ = _rms(qa_ref[...], qnw_ref[...])
        ckv, _, _ = _rms(kva_ref[...], kvnw_ref[...])
        cqb, ckvb = cq.astype(BF), ckv.astype(BF)
        cq_ref[...] = cqb
        ckv_ref[...] = ckvb
        cos1, sin1 = cos_ref[...], sin_ref[...]
        q = _per_head(lambda u: _rope(u, cos1, sin1), jnp.dot(cqb, wq_ref[...], preferred_element_type=F32)) * scale
        qt_ref[...] = q.astype(BF)
        kk = _rope(kr_ref[...], cos1, sin1)
        hl = _iota((TM, QW), 1) & (HP - 1)
        k = jnp.dot(ckvb, wk_ref[...], preferred_element_type=F32) + _tile6(kk)
        k = jnp.where(jnp.logical_and(hl >= QK, hl < QK + 3), 1.0, k)
        kt_ref[...] = k.astype(BF)
        v = jnp.dot(ckvb, wv_ref[...], preferred_element_type=F32)
        v = jnp.where(jnp.logical_and(hl >= VH, hl < VH + 4), 1.0, v)
        for h in range(H):
            cols = slice(HP * h, HP * (h + 1))
            qtr_ref[h] = q[:, cols].T.astype(BF)
            ktr_ref[h] = k[:, cols].T.astype(BF)
            vtr_ref[h] = v[:, cols].T.astype(BF)

    tr = (_sds((B, H, HP, T), BF), pl.BlockSpec((None, H, HP, TM), lambda b, i: (b, 0, 0, i)))
    return R.call("mla_prep", body,
                  [(qa, R.row(QL)), (kva, R.row(QL)), (kr, R.row(HP)), (qnw, R.const((1, QL))), (kvnw, R.const((1, QL))),
                   (wq, R.const((QL, QW))), (wk, R.const((QL, QW))), (wv, R.const((QL, QW))),
                   (cos, pl.BlockSpec((TM, HP), lambda b, i: (i, 0))), (sin, pl.BlockSpec((TM, HP), lambda b, i: (i, 0)))],
                  [(_sds((B, T, QW), BF), R.row(QW))] * 2 + [tr] * 3 + [(_sds((B, T, QL), BF), R.row(QL))] * 2, extra=12 << 20)


def _flash_fwd(B, T, q_t, kt, v_t):
    nq, nk = T // TQ, T // TK
    HS = 2

    def body(q_ref, k_ref, v_ref, o_ref, lse_ref, s_scr):
        i = pl.program_id(2)

        def attend(nch):
            ms = []
            for hh in range(HS):
                q_tr = q_ref[hh]
                mrun = None
                for j in range(nch):
                    s = _dot(k_ref[TK * j:TK * (j + 1), HP * hh:HP * (hh + 1)], q_tr)
                    s_scr[hh, j] = s
                    mrun = s if mrun is None else jnp.maximum(mrun, s)
                ms.append(jnp.max(mrun, axis=0, keepdims=True))
            row = _iota((HP, TQ), 0)
            for hh in range(HS):
                acc_t = jnp.zeros((HP, TQ), F32)
                for j in range(nch):
                    acc_t = acc_t + _dot(v_ref[hh, :, TK * j:TK * (j + 1)], jnp.exp(s_scr[hh, j] - ms[hh]))
                l = jnp.sum(jnp.where(row == VH, acc_t, 0.0), axis=0, keepdims=True)
                o_ref[:, HP * hh:HP * (hh + 1)] = jnp.where(row < VH, acc_t / l, 0.0).T
                lse_ref[hh] = ms[hh] + jnp.log(l)

        @pl.when(i < CTX // TQ)
        def _():
            attend(CTX // TK)

        @pl.when(i >= CTX // TQ)
        def _():
            attend(nk)

    bb = HS * (_nbytes((TQ, HP), BF) + 2 * _nbytes((T, HP), BF) + 2 * _nbytes((TQ, HP), F32))
    return pl.pallas_call(
        functools.partial(body), name="flash_fwd", grid=(B, H // HS, nq),
        in_specs=[pl.BlockSpec((None, HS, HP, TQ), lambda b, h, i: (b, h, 0, i)),
                  pl.BlockSpec((None, T, HS * HP), lambda b, h, i: (b, 0, h)),
                  pl.BlockSpec((None, HS, HP, T), lambda b, h, i: (b, h, 0, 0))],
        out_specs=[pl.BlockSpec((None, TQ, HS * HP), lambda b, h, i: (b, i, h)),
                   pl.BlockSpec((None, HS, 1, TQ), lambda b, h, i: (b, h, 0, i))],
        out_shape=[_out((B, T, QW), F32), _out((B, H, 1, T), F32)],
        scratch_shapes=[pltpu.VMEM((HS, nk, TK, TQ), F32)],
        compiler_params=_params(("arbitrary",) * 3, bb, _nbytes((HS, nk, TK, TQ), F32) + (8 << 20)),
    )(*_hbm([q_t, kt, v_t]))


def _pool_terms(R, i, rows):
    t, n = R.stream_pos(i, rows)
    lane = _iota((1, PD), 1)
    half = jnp.where(lane < 64, 1, jnp.where(lane < 128, 2, jnp.where(lane < 192, 4, 8)))
    cnt = (jnp.minimum(t + half, n) - jnp.maximum(t - half, 0)).astype(F32)
    valid = jnp.logical_and(t >= 0, t < n)
    return jnp.where(valid, cnt, 1.0), valid.astype(F32), lane


def _lane_select(lane, a2, a4, a8, a16):
    return jnp.where(lane < 64, a2, jnp.where(lane < 128, a4, jnp.where(lane < 192, a8, a16)))


def _pool_centred(R, i, ext):
    cnt, valid, lane = _pool_terms(R, i, ext.shape[0])
    s2 = ext + _shift_rows(ext, -1)
    s4 = _shift_rows(s2, -1) + _shift_rows(s2, 1)
    s8 = _shift_rows(s4, -2) + _shift_rows(s4, 2)
    s16 = _shift_rows(s8, -4) + _shift_rows(s8, 4)
    return _lane_select(lane, s2, s4, s8, s16) / cnt - ext, cnt, valid, lane


def _group_mask():
    return _iota((1, SSD_IN), 1) < SSD_IN // 2


def _ssd_gate(yf_ref, yb_ref, xbc_ref, z_ref, dsk_ref):
    ytot = yf_ref[...] + yb_ref[...] + xbc_ref[:, 0:SSD_IN] * dsk_ref[...]
    z = z_ref[...]
    gz = ytot * _silu(z)
    g0 = _group_mask()
    sq = gz * gz
    s0 = jnp.sum(jnp.where(g0, sq, 0.0), axis=1, keepdims=True)
    s1 = jnp.sum(jnp.where(g0, 0.0, sq), axis=1, keepdims=True)
    half = SSD_IN // 2
    r = jnp.where(g0, lax.rsqrt(s0 / half + EPS), lax.rsqrt(s1 / half + EPS))
    return ytot, z, gz, r


def _out_proj(R, x, mod, yf, yb, xbc, z, o, pool_in, wbd, pscale, dsk, snw, wout):
    B, T = R.B, R.T
    xin = R.xin(x)

    def body(*refs):
        x_refs = refs[:len(xin)]
        (mod_ref, yf_ref, yb_ref, xbc_ref, z_ref, o_ref, u_ref, upv_ref, unx_ref, wbd_ref, psc_ref, dsk_ref, snw_ref, w_ref,
         xmid_ref, cat_ref, mix_ref) = refs[len(xin):]
        i = pl.program_id(1)
        _, _, gz, r = _ssd_gate(yf_ref, yb_ref, xbc_ref, z_ref, dsk_ref)
        dm, _, _, _ = _pool_centred(R, i, R.ext(i, upv_ref, u_ref[...], unx_ref))
        cat_ref[:, 0:SSD_IN] = (gz * r * snw_ref[...]).astype(BF)
        cat_ref[:, SSD_IN:SSD_IN + QW] = o_ref[...].astype(BF)
        cat_ref[:, SSD_IN + QW:MIXW] = (_dot(dm[8:8 + TM], wbd_ref[...]) * psc_ref[...]).astype(BF)
        mix = jnp.dot(cat_ref[...], w_ref[...], preferred_element_type=F32)
        mix_ref[...] = mix.astype(BF)
        xmid_ref[...] = R.xval(i, x_refs) + _row(mod_ref, 2) * mix

    return R.call("out_proj", body,
                  xin + [(mod, R.mod()), (yf, R.row(SSD_IN)), (yb, R.row(SSD_IN)), (xbc, R.row(XBC)), (z, R.row(SSD_IN)),
                         (o, R.row(QW)), (pool_in, R.row(PD)), (pool_in, R.prev8(PD)), (pool_in, R.next8(PD)),
                         (wbd, R.const((PD, PD))), (pscale, R.const((1, PD))),
                         (dsk, R.const((1, SSD_IN))), (snw, R.const((1, SSD_IN))), (wout, R.const((MIXW, D)))],
                  [(_sds((B, T, D), F32), R.row(D)), (_sds((B, T, MIXW), BF), R.row(MIXW)), (_sds((B, T, D), BF), R.row(D))],
                  extra=12 << 20)


def _load_once(first, pairs, sem):
    @pl.when(first)
    def _():
        cps = [pltpu.make_async_copy(src, dst, sem.at[k]) for k, (src, dst) in enumerate(pairs)]
        for cp in cps:
            cp.start()
        for cp in cps:
            cp.wait()


ANY = pl.BlockSpec(memory_space=pl.ANY)


def _mlp_fwd(R, xmid, mod, nw2, w1, w2, head=None):
    B, T = R.B, R.T
    nhead = 0 if head is None else 2

    def body(*refs):
        x_ref, mod_ref, nw_ref, w1_hbm, w2_hbm = refs[:5]
        xo_ref, h_ref, u_ref, y_ref = refs[5 + nhead:9 + nhead]
        w1_v, w2_v, sem = refs[-3:]
        first = jnp.logical_and(pl.program_id(0) == 0, pl.program_id(1) == 0)
        _load_once(first, [(w1_hbm, w1_v), (w2_hbm, w2_v)], sem)

        def mlp():
            x = x_ref[...]
            h, _, _ = _norm_mod(x, nw_ref[...], _row(mod_ref, 3), _row(mod_ref, 4))
            hb = h.astype(BF)
            h_ref[...] = hb
            y = jnp.zeros((TM, D), F32)
            for j in range(NDEV):
                u = jnp.dot(hb, w1_v[j], preferred_element_type=F32)
                u_ref[:, FSH * j:FSH * (j + 1)] = u.astype(BF)
                a = jnp.square(jnp.maximum(u, 0.0))
                y = y + jnp.dot(a.astype(BF), w2_v[j], preferred_element_type=F32)
            y_ref[...] = y.astype(BF)
            return x + _row(mod_ref, 5) * y

        if head is None:
            xo_ref[...] = mlp()
            return
        t_ref, fw_ref = refs[5:7]
        loss_ref, dw_ref = refs[9 + nhead:11 + nhead]
        latent = pl.program_id(1) >= R.nct

        @pl.when(jnp.logical_not(latent))
        def _():
            for ref in (xo_ref, h_ref, u_ref, y_ref):
                ref[...] = jnp.zeros(ref.shape, ref.dtype)
            _acc_rows(loss_ref, first, {})
            _acc_rows(dw_ref, first, {})

        @pl.when(latent)
        def _():
            yv, xh, r = _rms(mlp(), fw_ref[...])
            err = yv - t_ref[...]
            dxn, dw = _rms_bwd(err / D, xh, r, fw_ref[...])
            xo_ref[...] = dxn
            part = 0.5 * jnp.sum(jnp.sum(err * err, axis=1, keepdims=True), axis=0, keepdims=True) / D
            _acc_rows(loss_ref, first, {0: jnp.broadcast_to(part, (1, HP))})
            _acc_rows(dw_ref, first, {0: dw})

    extra_in = [] if head is None else [(head[0], R.tgt(D)), (head[1], R.const((1, D)))]
    extra_out = [] if head is None else [(_sds((8, HP), F32), R.const((8, HP))), (_sds((8, D), F32), R.const((8, D)))]
    return R.call("mlp_fwd" if head is None else "mlp_fwd_head", body,
                  [(xmid, R.row(D)), (mod, R.mod()), (nw2, R.const((1, D))), (w1, ANY), (w2, ANY)] + extra_in,
                  [(_sds((B, T, D), F32), R.row(D)), (_sds((B, T, D), BF), R.row(D)), (_sds((B, T, FF), BF), R.row(FF)),
                   (_sds((B, T, D), BF), R.row(D))] + extra_out,
                  scratch=[pltpu.VMEM((NDEV, D, FSH), w1.dtype), pltpu.VMEM((NDEV, FSH, D), w2.dtype), pltpu.SemaphoreType.DMA((2,))],
                  extra=(2 * _nbytes((NDEV, D, FSH), BF)) + (8 << 20))


def _tn_matmul(name, a, b, tn, sub=1):
    B, T, K = a.shape
    N = b.shape[2]
    nk = 1
    while T % nk or (T // nk) > 1088 or (T // nk) % 16:
        nk += 1
    tk = T // nk
    kt = K if K <= 1536 else 2048
    w = tn // sub
    assert K % kt == 0 and N % tn == 0 and tn % sub == 0

    def body(a_ref, b_ref, o_ref):
        first = jnp.logical_and(pl.program_id(2) == 0, pl.program_id(3) == 0)

        @pl.when(first)
        def _():
            o_ref[...] = jnp.zeros(o_ref.shape, F32)

        acc = _dot_tn(a_ref[...], b_ref[...])
        for s in range(sub):
            o_ref[s] += acc[:, w * s:w * (s + 1)]

    bb = _nbytes((tk, kt), a.dtype) + _nbytes((tk, tn), b.dtype) + _nbytes((kt, tn), F32)
    return pl.pallas_call(
        functools.partial(body), name=name, grid=(N // tn, K // kt, B, nk),
        in_specs=[pl.BlockSpec((None, tk, kt), lambda j, kk, bi, t: (bi, t, kk)),
                  pl.BlockSpec((None, tk, tn), lambda j, kk, bi, t: (bi, t, j))],
        out_specs=pl.BlockSpec((sub, kt, w), lambda j, kk, bi, t: (j, kk, 0)),
        out_shape=_out((sub * N // tn, K, w), F32),
        compiler_params=_params(("arbitrary",) * 4, bb, _nbytes((kt, tn), F32) + (8 << 20)),
    )(*_hbm([a, b]))


def _mlp_bwd(R, dxo, xmid, ubf, ybf, mod, nw2, w1, w2):
    B, T = R.B, R.T

    def body(dxo_ref, x_ref, u_ref, y_ref, mod_ref, nw_ref, w1_hbm, w2_hbm,
             dxm_ref, du_ref, a_ref, dy_ref, dmod_ref, dnw_ref, w1_v, w2_v, sem):
        b, i = pl.program_id(0), pl.program_id(1)
        _load_once(jnp.logical_and(b == 0, i == 0), [(w1_hbm, w1_v), (w2_hbm, w2_v)], sem)
        dxo = dxo_ref[...]
        _, xn, r = _norm_mod(x_ref[...], nw_ref[...], _row(mod_ref, 3), _row(mod_ref, 4))
        dyb = (dxo * _row(mod_ref, 5)).astype(BF)
        dy_ref[...] = dyb
        dg2 = jnp.sum(dxo * y_ref[...].astype(F32), axis=0, keepdims=True)
        dh = jnp.zeros((TM, D), F32)
        for j in range(NDEV):
            rl = jnp.maximum(u_ref[:, FSH * j:FSH * (j + 1)].astype(F32), 0.0)
            a_ref[:, FSH * j:FSH * (j + 1)] = (rl * rl).astype(BF)
            du = (_dot_nt(dyb, w2_v[j]) * (2.0 * rl)).astype(BF)
            du_ref[:, FSH * j:FSH * (j + 1)] = du
            dh = dh + _dot_nt(du, w1_v[j])
        dx, dsh, dsc, dnw = _norm_mod_bwd(dh, xn, r, nw_ref[...], _row(mod_ref, 4))
        dxm_ref[...] = dxo + dx
        _acc_rows(dmod_ref, R.first_of_stream(i), {3: dsh, 4: dsc, 5: dg2})
        _acc_rows(dnw_ref, jnp.logical_and(b == 0, i == 0), {0: dnw})

    return R.call("mlp_bwd", body,
                  [(dxo, R.row(D)), (xmid, R.row(D)), (ubf, R.row(FF)), (ybf, R.row(D)), (mod, R.mod()), (nw2, R.const((1, D))),
                   (w1, ANY), (w2, ANY)],
                  [(_sds((B, T, D), F32), R.row(D)), (_sds((B, T, FF), BF), R.row(FF)), (_sds((B, T, FF), BF), R.row(FF)),
                   (_sds((B, T, D), BF), R.row(D)), (_sds((B, 2, 8, D), F32), R.mod()), (_sds((8, D), F32), R.const((8, D)))],
                  scratch=[pltpu.VMEM((NDEV, D, FSH), w1.dtype), pltpu.VMEM((NDEV, FSH, D), w2.dtype), pltpu.SemaphoreType.DMA((2,))],
                  extra=(2 * _nbytes((NDEV, D, FSH), BF)) + (8 << 20))


def _minus_in_lanes(x, col, first):
    b16 = jnp.bfloat16
    hi = col.astype(b16).astype(F32)
    r1 = col - hi
    mid = r1.astype(b16).astype(F32)
    lane = _iota(x.shape, 1)
    return jnp.where(lane == first, -hi, jnp.where(lane == first + 1, -mid, jnp.where(lane == first + 2, mid - r1, x)))


def _out_bwd(R, dxm, mod, mixbf, yf, yb, xbc, z, o, qt, lse, dsk, snw, wout):
    B, T = R.B, R.T

    def body(dxm_ref, mod_ref, mix_ref, yf_ref, yb_ref, xbc_ref, z_ref, o_ref, q_ref, lse_ref, dsk_ref, snw_ref, w_ref,
             dmix_ref, dz_ref, dyt_ref, dxsk_ref, qs_ref, dos_ref, dotr_ref, dpo_ref, dmod_ref, dvec_ref):
        b, i = pl.program_id(0), pl.program_id(1)
        dxm = dxm_ref[...]
        dmixb = (dxm * _row(mod_ref, 2)).astype(BF)
        dmix_ref[...] = dmixb
        dg1 = jnp.sum(dxm * mix_ref[...].astype(F32), axis=0, keepdims=True)
        dcat = _dot_nt(dmixb, w_ref[...])
        do_v = dcat[:, SSD_IN:SSD_IN + QW]
        doo = do_v * o_ref[...]
        for h in range(H):
            cols = slice(HP * h, HP * (h + 1))
            dd = jnp.sum(doo[:, cols], axis=1, keepdims=True)
            dos_ref[:, cols] = _minus_in_lanes(do_v[:, cols], dd, VH + 1).astype(BF)
            lse_b = jnp.broadcast_to(lse_ref[h], (HP, TM)).T
            qs_ref[:, cols] = _minus_in_lanes(q_ref[:, cols].astype(F32), lse_b, QK).astype(BF)
            dotr_ref[h] = do_v[:, cols].T.astype(BF)
        dpo_ref[...] = dcat[:, SSD_IN + QW:MIXW]
        dsn = dcat[:, 0:SSD_IN]
        ytot, zv, gz, r = _ssd_gate(yf_ref, yb_ref, xbc_ref, z_ref, dsk_ref)
        gh = gz * r
        dsnw = jnp.sum(dsn * gh, axis=0, keepdims=True)
        dgh = dsn * snw_ref[...]
        g0 = _group_mask()
        pr = dgh * gh
        half = SSD_IN // 2
        m0 = jnp.sum(jnp.where(g0, pr, 0.0), axis=1, keepdims=True) / half
        m1 = jnp.sum(jnp.where(g0, 0.0, pr), axis=1, keepdims=True) / half
        dgz = r * (dgh - gh * jnp.where(g0, m0, m1))
        dyt = dgz * _silu(zv)
        dz_ref[...] = (dgz * ytot * _dsilu(zv)).astype(BF)
        dyt_ref[...] = dyt
        dxsk_ref[...] = dyt * dsk_ref[...]
        ddsk = jnp.sum(dyt * xbc_ref[:, 0:SSD_IN], axis=0, keepdims=True)
        _acc_rows(dmod_ref, R.first_of_stream(i), {2: dg1})
        _acc_rows(dvec_ref, jnp.logical_and(b == 0, i == 0), {0: dsnw, 1: ddsk})

    return R.call("out_bwd", body,
                  [(dxm, R.row(D)), (mod, R.mod()), (mixbf, R.row(D)), (yf, R.row(SSD_IN)), (yb, R.row(SSD_IN)), (xbc, R.row(XBC)),
                   (z, R.row(SSD_IN)),
                   (o, R.row(QW)), (qt, R.row(QW)), (lse, pl.BlockSpec((None, H, 1, TM), lambda b, i: (b, 0, 0, i))),
                   (dsk, R.const((1, SSD_IN))), (snw, R.const((1, SSD_IN))), (wout, R.const((MIXW, D)))],
                  [(_sds((B, T, D), BF), R.row(D)), (_sds((B, T, SSD_IN), BF), R.row(SSD_IN)), (_sds((B, T, SSD_IN), F32), R.row(SSD_IN)),
                   (_sds((B, T, SSD_IN), F32), R.row(SSD_IN)), (_sds((B, T, QW), BF), R.row(QW)), (_sds((B, T, QW), BF), R.row(QW)),
                   (_sds((B, H, HP, T), BF), pl.BlockSpec((None, H, HP, TM), lambda b, i: (b, 0, 0, i))),
                   (_sds((B, T, PD), F32), R.row(PD)),
                   (_sds((B, 2, 8, D), F32), R.mod()), (_sds((8, SSD_IN), F32), R.const((8, SSD_IN)))],
                  extra=8 << 20)


def _pool_bwd(R, dpo, u, wbd, scale):
    B, T = R.B, R.T

    def body(d_ref, dpv_ref, dnx_ref, u_ref, upv_ref, unx_ref, w_ref, sc_ref, du_ref, dw_ref, dsc_ref):
        b, i = pl.program_id(0), pl.program_id(1)
        ext_u = R.ext(i, upv_ref, u_ref[...], unx_ref)
        ext_d = R.ext(i, dpv_ref, d_ref[...], dnx_ref)
        dm, cnt, valid, lane = _pool_centred(R, i, ext_u)
        ddm = _dot_nt(ext_d * sc_ref[...], w_ref[...]) * valid
        e = ddm / cnt
        a2 = e + _shift_rows(e, 1)
        a4 = _shift_rows(a2, -1) + _shift_rows(a2, 1)
        a8 = _shift_rows(a4, -2) + _shift_rows(a4, 2)
        a16 = _shift_rows(a8, -4) + _shift_rows(a8, 4)
        du_ref[...] = (_lane_select(lane, a2, a4, a8, a16) - ddm)[8:8 + TM].astype(BF)
        dmc = dm[8:8 + TM]
        dpo_c = d_ref[...]
        first = jnp.logical_and(b == 0, i == 0)

        @pl.when(first)
        def _():
            dw_ref[...] = jnp.zeros(dw_ref.shape, F32)

        dw_ref[...] += _dot_tn(dmc, dpo_c * sc_ref[...])
        _acc_rows(dsc_ref, first, {0: jnp.sum(dpo_c * _dot(dmc, w_ref[...]), axis=0, keepdims=True)})

    return R.call("pool_bwd", body,
                  [(dpo, R.row(PD)), (dpo, R.prev8(PD)), (dpo, R.next8(PD)), (u, R.row(PD)), (u, R.prev8(PD)), (u, R.next8(PD)),
                   (wbd, R.const((PD, PD))), (scale, R.const((1, PD)))],
                  [(_sds((B, T, PD), BF), R.row(PD)), (_sds((PD, PD), F32), R.const((PD, PD))), (_sds((8, PD), F32), R.const((8, PD)))],
                  extra=8 << 20)


def _flash_bwd(B, T, qs, kt, dos, q_t, k_t, v_t, do_t):
    nk, nq = T // TK, T // TQ
    HS = 2

    def body(k_ref, kt_ref, vt_ref, q_ref, do_ref, qt_ref, dot_ref, dq_ref, dk_ref, dv_ref, s_scr, dp_scr, ds_scr):
        j = pl.program_id(2)
        ctx_keys = jnp.where(j < CTX // TK, 1.0, 0.0)
        for hh in range(HS):
            cols = slice(HP * hh, HP * (hh + 1))
            s_scr[hh] = _dot(q_ref[:, cols], kt_ref[hh])
            dp_scr[hh] = _dot(do_ref[:, cols], vt_ref[hh])
        for hh in range(HS):
            cols = slice(HP * hh, HP * (hh + 1))
            dk_tr = jnp.zeros((HP, TK), F32)
            dv_tr = jnp.zeros((HP, TK), F32)
            for i in range(nq):
                rows = slice(TQ * i, TQ * (i + 1))
                p = jnp.exp(s_scr[hh, rows, :])
                if i < CTX // TQ:
                    p = p * ctx_keys
                ds = (p * dp_scr[hh, rows, :]).astype(BF)
                ds_scr[hh, rows, :] = ds
                dv_tr = dv_tr + _dot(dot_ref[hh, :, rows], p)
                dk_tr = dk_tr + _dot(qt_ref[hh, :, rows], ds)
            dq = _dot(ds_scr[hh], k_ref[:, cols])

            @pl.when(j == 0)
            def _():
                dq_ref[:, cols] = dq

            @pl.when(j > 0)
            def _():
                dq_ref[:, cols] += dq

            dk_ref[:, cols] = dk_tr.T
            dv_ref[:, cols] = dv_tr.T

    tspec = pl.BlockSpec((None, TK, HS * HP), lambda b, h, j: (b, j, h))
    fspec = pl.BlockSpec((None, T, HS * HP), lambda b, h, j: (b, 0, h))
    ttspec = pl.BlockSpec((None, HS, HP, TK), lambda b, h, j: (b, h, 0, j))
    ftspec = pl.BlockSpec((None, HS, HP, T), lambda b, h, j: (b, h, 0, 0))
    bb = HS * (4 * _nbytes((T, HP), BF) + _nbytes((T, HP), F32) + 8 * _nbytes((TK, HP), F32))
    scr = HS * (2 * _nbytes((T, TK), F32) + _nbytes((T, TK), BF))
    return pl.pallas_call(
        functools.partial(body), name="flash_bwd", grid=(B, H // HS, nk),
        in_specs=[tspec, ttspec, ttspec, fspec, fspec, ftspec, ftspec],
        out_specs=[fspec, tspec, tspec],
        out_shape=[_out((B, T, QW), F32)] * 3,
        scratch_shapes=[pltpu.VMEM((HS, T, TK), F32), pltpu.VMEM((HS, T, TK), F32), pltpu.VMEM((HS, T, TK), BF)],
        compiler_params=_params(("arbitrary",) * 3, bb, scr + (8 << 20)),
    )(*_hbm([kt, k_t, v_t, qs, dos, q_t, do_t]))


def _mla_bwd(R, dqt, dkt, dvt, qa, kva, qnw, kvnw, wq, wk, wv, cos, sin):
    B, T = R.B, R.T
    scale = QK ** -0.5

    def body(dq_ref, dk_ref, dv_ref, qa_ref, kva_ref, qnw_ref, kvnw_ref, wq_ref, wk_ref, wv_ref, cos_ref, sin_ref,
             dqp_ref, dkvb_ref, dqa_ref, dkva_ref, dkr_ref, dnw_ref):
        b, i = pl.program_id(0), pl.program_id(1)
        cos1, sin1 = cos_ref[...], sin_ref[...]
        dq = dq_ref[...] * scale
        dqp = _per_head(lambda g: g * cos1 + _swap8(g * sin1), dq).astype(BF)
        dqp_ref[...] = dqp
        dkv = dk_ref[...]
        dkb = dkv.astype(BF)
        dvb = dv_ref[...].astype(BF)
        dkvb_ref[:, :QW] = dkb
        dkvb_ref[:, QW:] = dvb
        dkk = dkv[:, 0:HP]
        for h in range(1, H):
            dkk = dkk + dkv[:, HP * h:HP * (h + 1)]
        lane = _iota((TM, HP), 1)
        rope_lane = jnp.logical_and(lane >= NOPE, lane < NOPE + ROPE)
        dkr_ref[...] = jnp.where(rope_lane, dkk * cos1 + _swap8(dkk * sin1), 0.0).astype(BF)
        _, qh, qr = _rms(qa_ref[...], qnw_ref[...])
        _, kh, kr_ = _rms(kva_ref[...], kvnw_ref[...])
        dcq = _dot_nt(dqp, wq_ref[...])
        dckv = _dot_nt(dkb, wk_ref[...]) + _dot_nt(dvb, wv_ref[...])
        dqa, dqw = _rms_bwd(dcq, qh, qr, qnw_ref[...])
        dkva, dkw = _rms_bwd(dckv, kh, kr_, kvnw_ref[...])
        dqa_ref[...] = dqa.astype(BF)
        dkva_ref[...] = dkva.astype(BF)
        _acc_rows(dnw_ref, jnp.logical_and(b == 0, i == 0), {0: dqw, 1: dkw})

    tab = pl.BlockSpec((TM, HP), lambda b, i: (i, 0))
    return R.call("mla_bwd", body,
                  [(dqt, R.row(QW)), (dkt, R.row(QW)), (dvt, R.row(QW)), (qa, R.row(QL)), (kva, R.row(QL)),
                   (qnw, R.const((1, QL))), (kvnw, R.const((1, QL))), (wq, R.const((QL, QW))), (wk, R.const((QL, QW))),
                   (wv, R.const((QL, QW))), (cos, tab), (sin, tab)],
                  [(_sds((B, T, QW), BF), R.row(QW)), (_sds((B, T, 2 * QW), BF), R.row(2 * QW))]
                  + [(_sds((B, T, QL), BF), R.row(QL))] * 2
                  + [(_sds((B, T, HP), BF), R.row(HP)), (_sds((8, QL), F32), R.const((8, QL)))], extra=8 << 20)


def _ssd_scan_bwd(B, T, dyt, xbc, dt2, alog2, hin_f, hin_b):
    nc, nctc = T // CH, CTX // CH

    def chain(d, dy_ref, xbc_ref, dt_ref, alog_ref, hin_ref, dxbc_ref, ddt_ref, da_ref, dhs_all):
        dhs = dhs_all.at[d]
        xbc_v = xbc_ref[...]
        dyv = dy_ref[...]
        a_all, dt_all, adt_all, q_all, inc, inc_t = _scan_common(d, dt_ref[...], alog_ref)
        ones_p = jnp.ones((P, HP), F32)
        total = lambda m: jnp.sum(jnp.sum(m, axis=0, keepdims=True), axis=1, keepdims=True)
        dq_parts, dqtot_parts, ddtx_parts = [], [], []
        for g in range(2):
            bg = xbc_v[:, SSD_IN + NST * g:SSD_IN + NST * (g + 1)]
            cg = xbc_v[:, SSD_IN + 2 * NST + NST * g:SSD_IN + 2 * NST + NST * (g + 1)]
            bg_t, cg_t = bg.T, cg.T
            gm = _dot(cg, bg_t)
            gm_t = _dot(bg, cg_t)
            dgm = jnp.zeros((CH, CH), F32)
            dgm_t = jnp.zeros((CH, CH), F32)
            dbg = jnp.zeros((CH, NST), F32)
            dcg = jnp.zeros((CH, NST), F32)
            for r in range(3):
                h = 3 * g + r
                lm, lm_t, eq, etot, dte = _head_decay(h, adt_all, q_all, inc, inc_t)
                eq_p, etot_p, dte_p = eq[:, :P], etot[:, :P], dte[:, :P]
                dt_p = dt_all[:, HP * h:HP * h + P]
                xs_h = xbc_v[:, P * h:P * (h + 1)]
                xh = xs_h * dt_p
                sm, sm_t = gm * lm, gm_t * lm_t
                dy_h = dyv[:, P * h:P * (h + 1)]
                hprev = hin_ref[P * h:P * (h + 1), :].T
                dho = dhs[:, P * h:P * (h + 1)]
                ds = _dot_nt(dy_h, xh)
                ds_t = _dot_nt(xh, dy_h)
                dx = _dot(sm_t, dy_h)
                edy = eq_p * dy_h
                yo = _dot(cg, hprev)
                dcg = dcg + _dot_nt(edy, hprev)
                dhin = _dot(cg_t, edy) + etot_p * dho
                zs = _dot(bg, dho)
                dx = dx + dte_p * zs
                wm = dte_p * xh * zs
                dbg = dbg + _dot_nt(xh * dte_p, dho)
                dgm = dgm + ds * lm
                dgm_t = dgm_t + ds_t * lm_t
                rs = jnp.sum(ds * sm - ds_t * sm_t, axis=1, keepdims=True) + jnp.sum(edy * yo - wm, axis=1, keepdims=True)
                dq_parts.append(jnp.broadcast_to(rs, (CH, HP)))
                dqtot_parts.append(total(hprev * dho) * etot + total(wm))
                dxbc_ref[:, P * h:P * (h + 1)] = dx * dt_p
                ddtx_parts.append(_dot01_r(dx * xs_h, ones_p, passes=2))
                dhs[:, P * h:P * (h + 1)] = dhin
            dcg = dcg + _dot(dgm, bg)
            dbg = dbg + _dot(dgm_t, cg)
            dxbc_ref[:, SSD_IN + NST * g:SSD_IN + NST * (g + 1)] = dbg
            dxbc_ref[:, SSD_IN + 2 * NST + NST * g:SSD_IN + 2 * NST + NST * (g + 1)] = dcg
        cat = lambda parts: jnp.concatenate(parts, axis=1)
        dadt_all = _dot01(inc_t, cat(dq_parts)) + cat(dqtot_parts)
        ddt_all = cat(ddtx_parts) + dadt_all * jnp.concatenate([a_all] * (CH // 8), axis=0)
        da_all = jnp.sum(dadt_all * dt_all, axis=0, keepdims=True)
        lane, lane1 = _iota((CH, HP), 1), _iota((1, HP), 1)
        ddt = jnp.zeros((CH, HP), F32)
        da = jnp.zeros((1, HP), F32)
        for h in range(H):
            ddt = ddt + jnp.where(lane == h, ddt_all[:, HP * h:HP * (h + 1)], 0.0)
            da = da + jnp.where(lane1 == h, da_all[:, HP * h:HP * (h + 1)], 0.0)
        ddt_ref[...] = ddt
        da_ref[d] += jnp.where(_iota((8, HP), 0) == 0, da, 0.0)

    def body(dyf_ref, dyb_ref, xf_ref, xb_ref, dtf_ref, dtb_ref, alog_ref, hf_ref, hb_ref,
             dxf_ref, dxb_ref, ddtf_ref, ddtb_ref, da_ref, dhs):
        @pl.when(pl.program_id(1) == 0)
        def _():
            dhs[...] = jnp.zeros(dhs.shape, F32)
            da_ref[...] = jnp.zeros(da_ref.shape, F32)

        chain(0, dyf_ref, xf_ref, dtf_ref, alog_ref, hf_ref, dxf_ref, ddtf_ref, da_ref, dhs)
        chain(1, dyb_ref, xb_ref, dtb_ref, alog_ref, hb_ref, dxb_ref, ddtb_ref, da_ref, dhs)

    cidx = lambda d: (lambda s: _chunk_index(d, nc - 1 - s, nc, nctc))
    specs = lambda d: dict(
        dy=pl.BlockSpec((None, CH, SSD_IN), lambda b, s: (b, cidx(d)(s), 0)),
        xbc=pl.BlockSpec((None, CH, XBC), lambda b, s: (b, cidx(d)(s), 0)),
        dt=pl.BlockSpec((None, None, CH, HP), lambda b, s: (b, d, cidx(d)(s), 0)),
        h=pl.BlockSpec((None, None, SSD_IN, NST), lambda b, s: (b, cidx(d)(s), 0, 0)),
        ddt=pl.BlockSpec((None, CH, HP), lambda b, s: (b, cidx(d)(s), 0)))
    f, r = specs(0), specs(1)
    bb = 2 * (2 * _nbytes((CH, XBC), F32) + 2 * _nbytes((CH, HP), F32) + _nbytes((CH, SSD_IN), F32) + _nbytes((SSD_IN, NST), F32))
    return pl.pallas_call(
        functools.partial(body), name="ssd_scan_bwd", grid=(B, nc),
        in_specs=[f["dy"], r["dy"], f["xbc"], r["xbc"], f["dt"], r["dt"], pl.BlockSpec((2, 8, HP), lambda b, s: (0, 0, 0)),
                  f["h"], r["h"]],
        out_specs=[f["xbc"], r["xbc"], f["ddt"], r["ddt"], pl.BlockSpec((None, 2, 8, HP), lambda b, s: (b, 0, 0, 0))],
        out_shape=[_out((B, T, XBC), F32)] * 2 + [_out((B, T, HP), F32)] * 2 + [_out((B, 2, 8, HP), F32)],
        scratch_shapes=[pltpu.VMEM((2, NST, SSD_IN), F32)],
        compiler_params=_params(("arbitrary",) * 2, bb, 16 << 20),
    )(*_hbm([dyt, dyt, xbc, xbc, dt2, dt2, alog2, hin_f, hin_b]))


def _ssd_prep_bwd(R, dxbc_f, dxbc_b, dxsk, ddt_f, ddt_b, xbc_raw, dt_raw, conv_w8, conv_b, dtb):
    B, T = R.B, R.T

    def body(dxf_ref, dxb_ref, dsk_ref, ddtf_ref, ddtb2_ref, raw_ref, pv_ref, nx_ref, dtr_ref, w_ref, b_ref, dtb_ref,
             dpre_ref, ddtr_ref, dcw_ref, dvec_ref, ddtb_ref):
        b, i = pl.program_id(0), pl.program_id(1)
        ext = R.ext(i, pv_ref, raw_ref[...], nx_ref)
        pre = _conv_pre(ext, w_ref, b_ref)
        dxbc = dxf_ref[...] + dxb_ref[...]
        skip = jnp.concatenate([dsk_ref[...], jnp.zeros((TM, XBC - SSD_IN), F32)], axis=1)
        dpre = (dxbc + skip) * _dsilu(pre)
        dpre_ref[...] = dpre
        first = jnp.logical_and(b == 0, i == 0)
        taps = {k: jnp.sum(dpre * _shift_rows(ext, k - 1)[8:8 + TM], axis=0, keepdims=True) for k in range(4)}
        _acc_rows(dcw_ref, first, taps)
        _acc_rows(dvec_ref, first, {0: jnp.sum(dpre, axis=0, keepdims=True)})
        ddt = ddtf_ref[...] + pltpu.roll(ddtb2_ref[...], H, axis=1)
        ddtr = ddt * _sigmoid(dtr_ref[...] + dtb_ref[...])
        ddtr = jnp.where(_iota((TM, HP), 1) < 2 * H, ddtr, 0.0)
        ddtr_ref[...] = ddtr.astype(BF)
        _acc_rows(ddtb_ref, first, {0: jnp.sum(ddtr, axis=0, keepdims=True)})

    return R.call("ssd_prep_bwd", body,
                  [(dxbc_f, R.row(XBC)), (dxbc_b, R.row(XBC)), (dxsk, R.row(SSD_IN)), (ddt_f, R.row(HP)), (ddt_b, R.row(HP)),
                   (xbc_raw, R.row(XBC)), (xbc_raw, R.prev8(XBC)),
                   (xbc_raw, R.next8(XBC)), (dt_raw, R.row(HP)), (conv_w8, R.const((8, XBC))), (conv_b, R.const((1, XBC))),
                   (dtb, R.const((1, HP)))],
                  [(_sds((B, T, XBC), F32), R.row(XBC)), (_sds((B, T, HP), BF), R.row(HP)), (_sds((8, XBC), F32), R.const((8, XBC))),
                   (_sds((8, XBC), F32), R.const((8, XBC))), (_sds((8, HP), F32), R.const((8, HP)))], extra=16 << 20)


def _in_bwd(R, dxm, x, mod, nw1, dz, dpre, dqa, dkva, dpool, dkr, ddtr, conv_w8, w_arr, latent_only=False):
    B, T = R.B, R.T
    dx_out = (_sds((B, T - CTX, D), F32), R.tgt(D)) if latent_only else (_sds((B, T, D), F32), R.row(D))

    xin = R.xin(x)

    def body(*refs):
        x_refs = refs[:len(xin)]
        (dxm_ref, mod_ref, nw_ref, dz_ref, dp_ref, dpp_ref, dpn_ref, dqa_ref, dkva_ref, dpool_ref, dkr_ref, ddt_ref,
         cw_ref, w_ref, dx_ref, dproj_ref, dmod_ref, dnw_ref) = refs[len(xin):]
        b, i = pl.program_id(0), pl.program_id(1)
        ext = R.ext(i, dpp_ref, dp_ref[...], dpn_ref)
        draw = (_row(cw_ref, 0) * _shift_rows(ext, 1)[8:8 + TM] + _row(cw_ref, 1) * ext[8:8 + TM]
                + _row(cw_ref, 2) * _shift_rows(ext, -1)[8:8 + TM] + _row(cw_ref, 3) * _shift_rows(ext, -2)[8:8 + TM])
        dproj_ref[:, A_Z:A_XBC] = dz_ref[...]
        dproj_ref[:, A_XBC:A_QA] = draw.astype(BF)
        dproj_ref[:, A_QA:A_KVA] = dqa_ref[...]
        dproj_ref[:, A_KVA:A_POOL] = dkva_ref[...]
        dproj_ref[:, A_POOL:A_KR] = dpool_ref[...]
        dproj_ref[:, A_KR:A_DT] = dkr_ref[...]
        dproj_ref[:, A_DT:PC] = ddt_ref[...]
        dh = _dot_nt(dproj_ref[...], w_ref[...])
        _, xn, r = _norm_mod(R.xval(i, x_refs), nw_ref[...], _row(mod_ref, 0), _row(mod_ref, 1))
        dx, dsh, dsc, dnw = _norm_mod_bwd(dh, xn, r, nw_ref[...], _row(mod_ref, 1))
        dx_ref[...] = dxm_ref[...] + dx
        _acc_rows(dmod_ref, R.first_of_stream(i), {0: dsh, 1: dsc})
        _acc_rows(dnw_ref, jnp.logical_and(b == 0, i == 0), {0: dnw})

    return R.call("in_bwd", body,
                  xin + [(dxm, R.row(D)), (mod, R.mod()), (nw1, R.const((1, D))), (dz, R.row(SSD_IN)), (dpre, R.row(XBC)),
                   (dpre, R.prev8(XBC)), (dpre, R.next8(XBC)), (dqa, R.row(QL)), (dkva, R.row(QL)), (dpool, R.row(PD)), (dkr, R.row(HP)),
                   (ddtr, R.row(HP)), (conv_w8, R.const((8, XBC))), (w_arr, R.const((D, PC)))],
                  [dx_out, (_sds((B, T, PC), BF), R.row(PC)), (_sds((B, 2, 8, D), F32), R.mod()),
                   (_sds((8, D), F32), R.const((8, D)))], extra=12 << 20)


def _adaln_fwd(cs, mod_w):
    L, _, C = mod_w.shape

    def body(c_ref, w_ref, o_ref):
        s = _silu(c_ref[...]).astype(BF)
        for l in range(L):
            o_ref[l] = jnp.dot(s, w_ref[l].astype(BF), preferred_element_type=F32)

    return pl.pallas_call(functools.partial(body), name="adaln_fwd", out_shape=_sds((L, 24, C), F32),
                          compiler_params=_params(None, _nbytes(mod_w.shape, F32) + _nbytes((L, 24, C), F32), 8 << 20))(cs, mod_w)


def _adaln_bwd(cs, dm, mod_w):
    L, _, C = mod_w.shape

    def body(c_ref, dm_ref, w_ref, gw_ref, gc_ref):
        c = c_ref[...]
        s = _silu(c).astype(BF)
        acc = jnp.zeros((24, D), F32)
        for l in range(L):
            dmb = dm_ref[l].astype(BF)
            gw_ref[l] = _dot_tn(s, dmb)
            acc = acc + _dot_nt(dmb, w_ref[l])
        gc_ref[...] = acc * _dsilu(c)

    return pl.pallas_call(functools.partial(body), name="adaln_bwd", out_shape=[_sds((L, D, C), F32), _sds((24, D), F32)],
                          compiler_params=_params(None, 2 * _nbytes(mod_w.shape, F32), 8 << 20))(cs, dm, mod_w)


def _sum_blocks(name, parts):
    Pn, Rr, C = parts.shape

    def body(p_ref, o_ref):
        acc = p_ref[0]
        for k in range(1, Pn):
            acc = acc + p_ref[k]
        o_ref[...] = acc

    return pl.pallas_call(functools.partial(body), name=name, out_shape=_sds((Rr, C), F32),
                          compiler_params=_params(None, _nbytes(parts.shape, F32), 4 << 20))(parts)


def _adamw(name, parts, w, m, v, rt):
    nseg = len(parts)
    Pn, rseg, C = parts[0].shape
    Rr = nseg * rseg
    tiles = rseg // rt
    assert rseg % rt == 0 and w.shape == (Rr, C)
    c1 = 1.0 - ADAM_B1 ** ADAM_STEP
    c2 = 1.0 - ADAM_B2 ** ADAM_STEP

    def body(*refs):
        p_refs = refs[:nseg]
        w_ref, m_ref, v_ref, g_ref, d_ref, nm_ref, nv_ref = refs[nseg:]
        i = pl.program_id(0)
        g = None
        for s, p_ref in enumerate(p_refs):
            gs = p_ref[0].astype(F32)
            for k in range(1, Pn):
                gs = gs + p_ref[k].astype(F32)
            g = gs if g is None else jnp.where(i >= s * tiles, gs, g)
        mn = ADAM_B1 * m_ref[...] + (1.0 - ADAM_B1) * g
        vn = ADAM_B2 * v_ref[...] + (1.0 - ADAM_B2) * jnp.square(g)
        g_ref[...] = g
        nm_ref[...] = mn
        nv_ref[...] = vn
        d_ref[...] = -ADAM_LR * ((mn / c1) / (jnp.sqrt(vn / c2) + ADAM_EPS) + ADAM_WD * w_ref[...])

    spec = pl.BlockSpec((rt, C), lambda i: (i, 0))
    pspec = lambda s: pl.BlockSpec((Pn, rt, C), lambda i: (0, jnp.clip(i - s * tiles, 0, tiles - 1), 0))
    bb = nseg * Pn * _nbytes((rt, C), parts[0].dtype) + 7 * _nbytes((rt, C), F32)
    return pl.pallas_call(
        functools.partial(body), name=name, grid=(Rr // rt,),
        in_specs=[pspec(s) for s in range(nseg)] + [spec, spec, spec],
        out_specs=[spec] * 4, out_shape=[_out((Rr, C), F32)] * 4,
        compiler_params=_params(("arbitrary",), bb, 4 << 20),
    )(*_hbm(list(parts) + [w, m, v]))


MESH = pl.DeviceIdType.MESH


def _my_pos():
    return lax.axis_index("x"), lax.axis_index("y"), lax.axis_index("c")


def _dev_index(x, y, c):
    return 4 * x + 2 * y + c


def _all_gather(name, shards):
    n = len(shards)

    def body(*refs):
        ins, outs = refs[:n], refs[n:2 * n]
        send_sems, recv_sems, local_sem = refs[2 * n:]
        x, y, c = _my_pos()
        me, sibling = (x, y, c), (x, y, 1 - c)
        chips = [(1 - x, y), (x, 1 - y), (1 - x, 1 - y)]

        def copy(t, k, block, to, src=None):
            slot = outs[t].at[_dev_index(*block)]
            return pltpu.make_async_remote_copy(
                src_ref=slot if src is None else src, dst_ref=slot,
                send_sem=send_sems.at[t, k], recv_sem=recv_sems.at[t, k], device_id=to, device_id_type=MESH)

        mine = [pltpu.make_async_copy(ins[t], outs[t].at[_dev_index(*me)], local_sem.at[t]) for t in range(n)]
        for cp in mine:
            cp.start()
        first = []
        for t in range(n):
            first.append(copy(t, 0, me, sibling, src=ins[t]))
            first += [copy(t, 1 + j, me, (*chip, c), src=ins[t]) for j, chip in enumerate(chips)]
        for cp in first:
            cp.start()
        passed = []
        for j, chip in enumerate(chips):
            for t in range(n):
                copy(t, 1 + j, (*chip, c), me).wait_recv()
                cp = copy(t, 4 + j, (*chip, c), sibling)
                cp.start()
                passed.append(cp)
        for t in range(n):
            copy(t, 0, sibling, me).wait_recv()
            for j, chip in enumerate(chips):
                copy(t, 4 + j, (*chip, 1 - c), me).wait_recv()
        for cp in first + passed:
            cp.wait_send()
        for cp in mine:
            cp.wait()

    return pl.pallas_call(
        functools.partial(body), name=name,
        in_specs=[ANY] * n, out_specs=[ANY] * n,
        out_shape=[_sds((NDEV,) + s.shape, s.dtype) for s in shards],
        scratch_shapes=[pltpu.SemaphoreType.DMA((n, 7)), pltpu.SemaphoreType.DMA((n, 7)), pltpu.SemaphoreType.DMA((n,))],
    )(*shards)


def _all_to_all(name, parts):
    n = len(parts)

    def body(*refs):
        ins, outs = refs[:n], refs[n:2 * n]
        send_sems, recv_sems, local_sem = refs[2 * n:]
        x, y, c = _my_pos()
        me = _dev_index(x, y, c)
        peers = [(x ^ ((k >> 2) & 1), y ^ ((k >> 1) & 1), c ^ (k & 1)) for k in range(1, NDEV)]
        mine = [pltpu.make_async_copy(ins[t].at[me], outs[t].at[me], local_sem.at[t]) for t in range(n)]
        for cp in mine:
            cp.start()
        sends = []
        for t in range(n):
            for k, peer in enumerate(peers):
                cp = pltpu.make_async_remote_copy(
                    src_ref=ins[t].at[_dev_index(*peer)], dst_ref=outs[t].at[me],
                    send_sem=send_sems.at[t, k], recv_sem=recv_sems.at[t, k], device_id=peer, device_id_type=MESH)
                cp.start()
                sends.append(cp)
        for t in range(n):
            for k, peer in enumerate(peers):
                slot = outs[t].at[_dev_index(*peer)]
                pltpu.make_async_remote_copy(
                    src_ref=slot, dst_ref=slot, send_sem=send_sems.at[t, k], recv_sem=recv_sems.at[t, k],
                    device_id=peer, device_id_type=MESH).wait_recv()
        for cp in sends:
            cp.wait_send()
        for cp in mine:
            cp.wait()

    return pl.pallas_call(
        functools.partial(body), name=name,
        in_specs=[ANY] * n, out_specs=[ANY] * n,
        out_shape=[_sds(p.shape, p.dtype) for p in parts],
        scratch_shapes=[pltpu.SemaphoreType.DMA((n, 7)), pltpu.SemaphoreType.DMA((n, 7)), pltpu.SemaphoreType.DMA((n,))],
    )(*parts)


SEM = pl.BlockSpec(memory_space=pltpu.SEMAPHORE)
IN_HBM = pl.BlockSpec(memory_space=pltpu.HBM)
DATAFLOW = pltpu.SideEffectType.DATAFLOW_SIDE_EFFECTING


def _flip_peers(x, y, c):
    return [(x ^ ((k >> 2) & 1), y ^ ((k >> 1) & 1), c ^ (k & 1)) for k in range(1, NDEV)]


def _split_copies(srcs, lands, send_sems, recv_sems, gather):
    x, y, c = _my_pos()
    me = _dev_index(x, y, c)
    out = []
    for t in range(len(srcs)):
        for k, peer in enumerate(_flip_peers(x, y, c)):
            p = _dev_index(*peer)
            src = srcs[t] if gather else srcs[t].at[p]
            sems = dict(send_sem=send_sems.at[7 * t + k], recv_sem=recv_sems.at[7 * t + k], device_id=peer, device_id_type=MESH)
            out.append((pltpu.make_async_remote_copy(src_ref=src, dst_ref=lands[t].at[me], **sems),
                        pltpu.make_async_remote_copy(src_ref=src, dst_ref=lands[t].at[p], **sems)))
    return out


def _exchange_start(name, collective_id, srcs, gather):
    n = len(srcs)
    lands = [lax.empty(((NDEV,) + s.shape) if gather else s.shape, s.dtype) for s in srcs]

    def body(*refs):
        src_refs, land_refs = refs[:n], refs[n:2 * n]
        send_sems, recv_sems = refs[2 * n], refs[2 * n + 1]
        token = refs[-1]
        barrier = pltpu.get_barrier_semaphore()
        for peer in _flip_peers(*_my_pos()):
            pl.semaphore_signal(barrier, inc=1, device_id=peer, device_id_type=MESH)
        pl.semaphore_wait(barrier, NDEV - 1)
        for send, _ in _split_copies(src_refs, land_refs, send_sems, recv_sems, gather):
            send.start()
        token[...] = jnp.zeros(token.shape, token.dtype)

    hbm = lambda a: pltpu.HBM(a.shape, a.dtype)
    res = pl.pallas_call(
        functools.partial(body), name=name,
        out_shape=[pltpu.SemaphoreType.DMA((7 * n,)), pltpu.SemaphoreType.DMA((7 * n,))] + [hbm(s) for s in srcs]
        + [hbm(a) for a in lands] + [_sds((8, HP), F32)],
        in_specs=[IN_HBM] * (2 * n), out_specs=[SEM, SEM] + [IN_HBM] * (2 * n) + [pl.BlockSpec(memory_space=pltpu.VMEM)],
        input_output_aliases={i: 2 + i for i in range(2 * n)},
        compiler_params=pltpu.CompilerParams(has_side_effects=DATAFLOW, collective_id=collective_id),
    )(*_hbm(list(srcs) + lands))
    return res[0], res[1], list(res[2:2 + n]), list(res[2 + n:2 + 2 * n]), res[-1]


def _exchange_wait(name, send_sems, recv_sems, srcs, lands, after, gather):
    n = len(srcs)

    def body(*refs):
        src_refs, land_refs = refs[:n], refs[n:2 * n]
        for _, recv in _split_copies(src_refs, land_refs, refs[2 * n], refs[2 * n + 1], gather):
            recv.wait_send()
            recv.wait_recv()

    hbm = lambda a: pltpu.HBM(a.shape, a.dtype)
    res = pl.pallas_call(
        functools.partial(body), name=name,
        out_shape=[hbm(s) for s in srcs] + [hbm(a) for a in lands],
        in_specs=[IN_HBM] * (2 * n) + [SEM, SEM, ANY], out_specs=[IN_HBM] * (2 * n),
        input_output_aliases={i: i for i in range(2 * n)},
        compiler_params=pltpu.CompilerParams(has_side_effects=DATAFLOW),
    )(*srcs, *lands, send_sems, recv_sems, after)
    return list(res[:n]), list(res[n:])


def _with_own(lands, own_blocks, me):
    out = []
    for land, own in zip(lands, own_blocks):
        out.append(lax.dynamic_update_slice(land, own[None], (me,) + (0,) * own.ndim))
    return out


def _arrange_w_in(w):
    z = lambda n: jnp.zeros((w.shape[0], n), w.dtype)
    return jnp.concatenate([w[:, 0:1280], w[:, 1292:1548], w[:, 1548:1804], w[:, 1836:2092],
                            z(64), w[:, 1804:1836], z(32), w[:, 1280:1292], z(HP - 2 * H)], axis=1)


def _unarrange_w_in(g):
    return jnp.concatenate([g[:, 0:1280], g[:, A_DT:A_DT + 2 * H], g[:, A_QA:A_KVA], g[:, A_KVA:A_POOL],
                            g[:, A_KR + NOPE:A_KR + NOPE + ROPE], g[:, A_POOL:A_KR]], axis=1)


def _pad_heads(w, width):
    k = w.shape[0]
    return jnp.pad(w.reshape(k, H, width), ((0, 0), (0, 0), (0, HP - width))).reshape(k, H * HP)


def _unpad_heads(g, width):
    k = g.shape[0]
    return g.reshape(k, H, HP)[:, :, :width].reshape(k, H * width)


def _arrange_w_out(w):
    att = jnp.pad(w[SSD_IN:2 * SSD_IN].reshape(H, VH, D), ((0, 0), (0, HP - VH), (0, 0))).reshape(QW, D)
    return jnp.concatenate([w[0:SSD_IN], att, w[2 * SSD_IN:]], axis=0)


def _unarrange_w_out(g):
    att = g[SSD_IN:SSD_IN + QW].reshape(H, HP, D)[:, :VH].reshape(SSD_IN, D)
    return jnp.concatenate([g[0:SSD_IN], att, g[SSD_IN + QW:]], axis=0)


def _rope_tables(T):
    n = T - CTX
    rows = n // GRID_W
    pairs = ROPE // 4
    inv = ROPE_THETA ** (-jnp.arange(pairs, dtype=F32) / pairs)
    ar = jnp.arange(rows, dtype=F32)[:, None] * inv
    ac = jnp.arange(GRID_W, dtype=F32)[:, None] * inv
    by_row = lambda a: jnp.repeat(a, GRID_W, axis=0)
    by_col = lambda a: jnp.tile(a, (rows, 1))
    cos = jnp.concatenate([by_row(jnp.cos(ar))] * 2 + [by_col(jnp.cos(ac))] * 2, axis=1)
    sin = jnp.concatenate([-by_row(jnp.sin(ar)), by_row(jnp.sin(ar)), -by_col(jnp.sin(ac)), by_col(jnp.sin(ac))], axis=1)
    ones, zeros = jnp.ones((n, NOPE), F32), jnp.zeros((n, NOPE), F32)
    cos = jnp.concatenate([ones, cos, ones[:, :HP - QK]], axis=1)
    sin = jnp.concatenate([zeros, sin, zeros[:, :HP - QK]], axis=1)
    return (jnp.concatenate([jnp.ones((CTX, HP), F32), cos], axis=0),
            jnp.concatenate([jnp.zeros((CTX, HP), F32), sin], axis=0))


def _lane_pad(v, n):
    return jnp.pad(v, (0, n - v.shape[0]))[None, :]


def _layer_weights(w_in, w_q_b, w_kv_b, conv_w, pool_w):
    kv = w_kv_b.reshape(QL, H, NOPE + VH)
    wbd = jnp.concatenate([jnp.pad(pool_w[g], ((0, 0), (64 * g, PD - 64 * (g + 1)))) for g in range(4)], axis=0)
    return dict(
        w_in=_arrange_w_in(w_in).astype(BF),
        wq=_pad_heads(w_q_b, QK).astype(BF),
        wk=_pad_heads(kv[:, :, :NOPE].reshape(QL, H * NOPE), NOPE).astype(BF),
        wv=_pad_heads(kv[:, :, NOPE:].reshape(QL, H * VH), VH).astype(BF),
        conv_w8=jnp.pad(conv_w, ((0, 4), (0, 0))), wbd=wbd)


def _layer_fwd(R, x, mod, lw, sp, late, cos, sin, head=None):
    B, T = R.B, R.T
    z, xbc_raw, qa, kva, pool_in, kr, dt_raw, h1 = _in_proj(R, x, mod, sp["nw1"], lw["w_in"])
    xbc, dt2 = _ssd_prep(R, xbc_raw, dt_raw, lw["conv_w8"], sp["conv_b"], sp["dtb"])
    yf, yb, hin_f, hin_b = _ssd_scan(B, T, xbc, dt2, sp["alog2"])
    qt, kt, q_t, k_t, v_t, cq, ckv = _mla_prep(R, qa, kva, kr, sp["qnw"], sp["kvnw"], lw["wq"], lw["wk"], lw["wv"], cos, sin)
    o, lse = _flash_fwd(B, T, q_t, kt, v_t)
    w_out, w1, w2 = late(o)
    w_out = _arrange_w_out(w_out).astype(BF)
    xmid, cat, mixbf = _out_proj(R, x, mod, yf, yb, xbc, z, o, pool_in, lw["wbd"], sp["pscale"], sp["dsk"], sp["snw"], w_out)
    xo, h2, ubf, ybf, *tail = _mlp_fwd(R, xmid, mod, sp["nw2"], w1, w2, head)
    if head is not None:
        xo = (xo, tail[0], tail[1])
    saved = dict(x=x, z=z, xbc_raw=xbc_raw, qa=qa, kva=kva, pool_in=pool_in, dt_raw=dt_raw, h1=h1, xbc=xbc, dt2=dt2,
                 yf=yf, yb=yb, hin_f=hin_f, hin_b=hin_b, qt=qt, kt=kt, q_t=q_t, k_t=k_t, v_t=v_t, cq=cq, ckv=ckv, o=o, lse=lse,
                 cat=cat, mixbf=mixbf, xmid=xmid, h2=h2, ubf=ubf, ybf=ybf, w_out=w_out, w1=w1, w2=w2)
    return xo, saved


def _layer_bwd(R, dxo, sv, mod, lw, sp, cos, sin, on_mlp=None, on_mid=None, latent_only=False):
    B, T = R.B, R.T
    dxm, du, abf, dyb, dmod_a, dnw2 = _mlp_bwd(R, dxo, sv["xmid"], sv["ubf"], sv["ybf"], mod, sp["nw2"], sv["w1"], sv["w2"])
    g_w1 = _tn_matmul("dw_mlp1", sv["h2"], du, 4 * FSH, sub=4)
    g_w2 = _tn_matmul("dw_mlp2", abf, dyb, D)[0].reshape(NDEV, FSH, D)
    snw = sp["snw"]
    tok = on_mlp(g_w1, g_w2) if on_mlp is not None else None
    if tok is not None:
        snw = snw + tok
    dmix, dz, dyt, dxsk, qs, dos, do_t, dpo, dmod_b, dvec_o = _out_bwd(R, dxm, mod, sv["mixbf"], sv["yf"], sv["yb"], sv["xbc"], sv["z"],
                                                                 sv["o"], sv["qt"], sv["lse"], sp["dsk"], snw, sv["w_out"])
    g_wout = _unarrange_w_out(_tn_matmul("dw_out", sv["cat"], dmix, D)[0])
    dpool_in, g_wbd, dpsc = _pool_bwd(R, dpo, sv["pool_in"], lw["wbd"], sp["pscale"])
    dqt, dkt, dvt = _flash_bwd(B, T, qs, sv["kt"], dos, sv["q_t"], sv["k_t"], sv["v_t"], do_t)
    dqp, dkvb, dqa, dkva, dkr, dnw_qk = _mla_bwd(R, dqt, dkt, dvt, sv["qa"], sv["kva"], sp["qnw"], sp["kvnw"],
                                                 lw["wq"], lw["wk"], lw["wv"], cos, sin)
    g_wq = _unpad_heads(_tn_matmul("dw_q", sv["cq"], dqp, QW)[0], QK)
    g_kv = _tn_matmul("dw_kv", sv["ckv"], dkvb, 2 * QW)[0]
    g_wk = _unpad_heads(g_kv[:, :QW], NOPE).reshape(QL, H, NOPE)
    g_wv = _unpad_heads(g_kv[:, QW:], VH).reshape(QL, H, VH)
    g_wkv = jnp.concatenate([g_wk, g_wv], axis=2).reshape(QL, H * (NOPE + VH))
    alog2 = sp["alog2"]
    tok = on_mid(g_wout, g_wq, g_wkv) if on_mid is not None else None
    if tok is not None:
        alog2 = alog2 + tok
    dxbc_f, dxbc_b, ddt_f, ddt_b, da = _ssd_scan_bwd(B, T, dyt, sv["xbc"], sv["dt2"], alog2, sv["hin_f"], sv["hin_b"])
    dpre, ddtr, dcw, dcb, ddtb = _ssd_prep_bwd(R, dxbc_f, dxbc_b, dxsk, ddt_f, ddt_b, sv["xbc_raw"], sv["dt_raw"], lw["conv_w8"],
                                                sp["conv_b"], sp["dtb"])
    dx, dproj, dmod_c, dnw1 = _in_bwd(R, dxm, sv["x"], mod, sp["nw1"], dz, dpre, dqa, dkva, dpool_in, dkr, ddtr,
                                      lw["conv_w8"], lw["w_in"], latent_only)
    g_win = _unarrange_w_in(jnp.concatenate(list(_tn_matmul("dw_in", sv["h1"], dproj, PC)), axis=1))
    a2 = -jnp.exp(sp["alog2"][:, 0, :H])
    small = dict(
        norm1_w=dnw1[0], norm2_w=dnw2[0], conv_w=dcw[0:4], conv_b=dcb[0], dt_bias=ddtb[0, :2 * H].reshape(2, H),
        a_log=jnp.sum(da[:, :, 0, :H], axis=0) * a2, ssd_d=jnp.sum(dvec_o[1].reshape(H, P), axis=1), ssd_norm_w=dvec_o[0],
        q_a_norm_w=dnw_qk[0], kv_a_norm_w=dnw_qk[1],
        pool_w=jnp.stack([g_wbd[64 * g:64 * (g + 1), 64 * g:64 * (g + 1)] for g in range(4)]), pool_scale=dpsc[0])
    big = dict(w_in=g_win, w_out=g_wout, w_q_b=g_wq, w_kv_b=g_wkv, w_mlp1=g_w1, w_mlp2=g_w2)
    return dx, big, small, dmod_a + dmod_b + dmod_c


def _small_params(l, norm1_w, norm2_w, conv_b, dt_bias, a_log, ssd_d, ssd_norm_w, q_a_norm_w, kv_a_norm_w, pool_scale):
    alog2 = jnp.broadcast_to(jnp.pad(a_log[l], ((0, 0), (0, HP - H)))[:, None, :], (2, 8, HP))
    return dict(nw1=norm1_w[l][None], nw2=norm2_w[l][None], conv_b=conv_b[l][None],
                dtb=_lane_pad(dt_bias[l].reshape(2 * H), HP), alog2=alog2,
                dsk=jnp.repeat(ssd_d[l], P)[None], snw=ssd_norm_w[l][None], qnw=q_a_norm_w[l][None],
                kvnw=kv_a_norm_w[l][None], pscale=pool_scale[l][None])


SMALL_NAMES = ["mod_b", "norm1_w", "norm2_w", "conv_b", "dt_bias", "a_log", "ssd_d", "ssd_norm_w", "q_a_norm_w",
               "kv_a_norm_w", "pool_w", "pool_scale", "final_norm_w"]


def _pack(arrs):
    rows = []
    for a in arrs:
        f = a.reshape(-1).astype(F32)
        n = -(-f.shape[0] // HP) * HP
        rows.append(jnp.pad(f, (0, n - f.shape[0])).reshape(-1, HP))
    out = jnp.concatenate(rows, axis=0)
    pad = (-out.shape[0]) % 8
    return jnp.pad(out, ((0, pad), (0, 0)))


def _unpack(pack, like):
    outs, r = [], 0
    for a in like:
        n = math.prod(a.shape)
        nr = -(-n // HP)
        outs.append(pack[r:r + nr].reshape(-1)[:n].reshape(a.shape))
        r += nr
    return outs


def _local_step(x, ctx, target, mods, full_of, small_w, on_grads=None, on_mlp=None, on_mid=None):
    B, N = x.shape[0], x.shape[1]
    T = CTX + N
    R = _Rows(B, T)
    cos, sin = _rope_tables(T)
    xu = (ctx, x)
    L = len(mods)
    lws, sps, saves = [], [], []
    for l in range(L):
        f = full_of(l, xu)
        lws.append(_layer_weights(f["w_in"], f["w_q_b"], f["w_kv_b"], f["conv_w"], small_w["pool_w"][l]))
        sps.append(_small_params(l, *[small_w[k] for k in ["norm1_w", "norm2_w", "conv_b", "dt_bias", "a_log", "ssd_d",
                                                          "ssd_norm_w", "q_a_norm_w", "kv_a_norm_w", "pool_scale"]]))
        head = (target, small_w["final_norm_w"][None]) if l == L - 1 else None
        xu, sv = _layer_fwd(R, xu, mods[l], lws[l], sps[l], f["late"], cos, sin, head)
        saves.append(sv)
    dx, loss8, dfnw = xu
    bigs, smalls, dmods = [None] * L, [None] * L, [None] * L
    for l in reversed(range(L)):
        hook = functools.partial(on_mlp, l) if on_mlp is not None else None
        hook2 = functools.partial(on_mid, l) if on_mid is not None else None
        dx, bigs[l], smalls[l], dmods[l] = _layer_bwd(R, dx, saves[l], mods[l], lws[l], sps[l], cos, sin, hook, hook2, l == 0)
        if on_grads is not None:
            bigs[l], tok = on_grads(l, bigs[l], dx)
            if tok is not None:
                sps[l - 1] = dict(sps[l - 1], nw2=sps[l - 1]["nw2"] + tok)
    return loss8[0, 0], dx, bigs, smalls, dfnw[0], dmods


def kernel(x, c, ctx, c_ctx, mod_w, mod_b, norm1_w, norm2_w, w_in, conv_w, conv_b, dt_bias, a_log, ssd_d, ssd_norm_w, q_a_norm_w, w_q_b, kv_a_norm_w, w_kv_b, pool_w, pool_scale, w_out, w_mlp1, w_mlp2, final_norm_w, loss_target, m_c_ctx, m_mod_w, m_mod_b, m_norm1_w, m_norm2_w, m_w_in, m_conv_w, m_conv_b, m_dt_bias, m_a_log, m_ssd_d, m_ssd_norm_w, m_q_a_norm_w, m_w_q_b, m_kv_a_norm_w, m_w_kv_b, m_pool_w, m_pool_scale, m_w_out, m_w_mlp1, m_w_mlp2, m_final_norm_w, v_c_ctx, v_mod_w, v_mod_b, v_norm1_w, v_norm2_w, v_w_in, v_conv_w, v_conv_b, v_dt_bias, v_a_log, v_ssd_d, v_ssd_norm_w, v_q_a_norm_w, v_w_q_b, v_kv_a_norm_w, v_w_kv_b, v_pool_w, v_pool_scale, v_w_out, v_w_mlp1, v_w_mlp2, v_final_norm_w):
    args = dict(locals())
    B = x.shape[0]
    L = mod_w.shape[0]
    me = _dev_index(*_my_pos())
    CS = mod_w.shape[2]

    big_names = ["w_in", "w_out", "w_q_b", "w_kv_b", "w_mlp1", "w_mlp2"]
    shards = {n: args[n].astype(BF) for n in big_names}
    early, late_names = ["w_in", "w_q_b", "w_kv_b"], ["w_out", "w_mlp1", "w_mlp2"]
    g0 = _all_gather("gather_weights", [c, conv_w] + [shards[n][0] for n in early])
    c_all, convw_all = g0[0], g0[1]
    gathered = {0: dict(zip(early, g0[2:]))}
    cs = jnp.concatenate([c_all.reshape(NDEV * B, D), c_ctx[None], jnp.zeros((24 - NDEV * B - 1, D), F32)], axis=0)
    m_loc = _adaln_fwd(cs, mod_w)
    m_all = _all_gather("gather_mod", [m_loc])[0]
    m_full = jnp.moveaxis(m_all, 0, 2).reshape(L, 24, NDEV * CS) + mod_b[:, None, :]
    pending = {}
    tok = jnp.zeros((), F32)
    res = _exchange_start("gather_start_0", 1, [shards[n][0] for n in late_names], gather=True)
    pending[0] = res[:4]
    tok = tok + res[4][0, 0]
    for l in range(1, L):
        res = _exchange_start("gather_start_%d" % l, 1 + l, [shards[n][l] for n in early + late_names], gather=True)
        pending[l] = res[:4]
        tok = tok + res[4][0, 0]
    mods = []
    for l in range(L):
        ex = lax.dynamic_slice(m_full[l], (me * B, 0), (B, 6 * D)).reshape(B, 6, D)
        cc = jnp.broadcast_to(m_full[l, NDEV * B].reshape(1, 6, D), (B, 6, D))
        mods.append(jnp.pad(jnp.stack([cc, ex], axis=1), ((0, 0), (0, 0), (0, 2), (0, 0))) + tok)

    def full_of(l, xu):
        if l > 0:
            own, lands = _exchange_wait("gather_wait_%d" % l, *pending.pop(l), xu, gather=True)
            gathered[l] = dict(zip(early + late_names, _with_own(lands, own, me)))

        def late(after):
            if l == 0:
                own, lands = _exchange_wait("gather_wait_0", *pending.pop(0), after, gather=True)
                gathered[0].update(zip(late_names, _with_own(lands, own, me)))
            gl = gathered[l]
            return gl["w_out"].reshape(D, D), gl["w_mlp1"], gl["w_mlp2"]

        g = gathered[l]
        return dict(
            w_in=g["w_in"].reshape(D, IN_COLS),
            w_q_b=jnp.moveaxis(g["w_q_b"], 0, 1).reshape(QL, H * QK),
            w_kv_b=jnp.moveaxis(g["w_kv_b"], 0, 1).reshape(QL, H * (NOPE + VH)),
            conv_w=jnp.moveaxis(convw_all[:, l], 0, 1).reshape(4, XBC), late=late)

    def grad_blocks(big, names):
        make = dict(
            w_in=lambda g: g.reshape(NDEV, D // NDEV, IN_COLS), w_out=lambda g: g.reshape(NDEV, D // NDEV, D),
            w_q_b=lambda g: jnp.moveaxis(g.reshape(QL, NDEV, -1), 1, 0),
            w_kv_b=lambda g: jnp.moveaxis(g.reshape(QL, NDEV, -1), 1, 0), w_mlp1=lambda g: g, w_mlp2=lambda g: g)
        return [make[n](big[n]).astype(BF) for n in names]

    sent, sent_mlp, sent_mid = {}, {}, {}
    rest_names, mid_names = ["w_in"], ["w_out", "w_q_b", "w_kv_b"]

    def on_mid(l, g_wout, g_wq, g_wkv):
        if l > 0:
            return None
        blocks_mid = grad_blocks(dict(w_out=g_wout, w_q_b=g_wq, w_kv_b=g_wkv), mid_names)
        res = _exchange_start("grads_mid_start_0", 2 + 2 * L, blocks_mid, gather=False)
        sent_mid[l] = res[:4]
        return res[4][0, 0]

    def on_mlp(l, g_w1, g_w2):
        if l > 0:
            return None
        res = _exchange_start("grads_mlp_start_0", 1 + 2 * L, [g_w1.astype(BF), g_w2.astype(BF)], gather=False)
        sent_mlp[l] = res[:4]
        return res[4][0, 0]

    def on_grads(l, big, dx):
        if l == 0:
            return grad_blocks(big, rest_names), None
        res = _exchange_start("grads_start_%d" % l, 1 + L + l, grad_blocks(big, big_names), gather=False)
        sent[l] = res[:4]
        return None, res[4][0, 0]

    small_w = {k: args[k] for k in SMALL_NAMES if k != "mod_b"}
    loss_part, grad_x, blocks, smalls, dfnw, dmods = _local_step(x, ctx, loss_target, mods, full_of, small_w, on_grads, on_mlp,
                                                                 on_mid)
    loss = lax.psum(loss_part, ("x", "y", "c"))

    dm_ex = jnp.stack([dmods[l][:, 1, :6].reshape(B, 6 * D) for l in range(L)])
    dm_cc = jnp.stack([jnp.sum(dmods[l][:, 0, :6], axis=0).reshape(6 * D) for l in range(L)])
    small_parts = dict(
        mod_b=jnp.sum(dm_ex, axis=1) + dm_cc,
        **{k: jnp.stack([smalls[l][k] for l in range(L)]) for k in SMALL_NAMES[1:-1]},
        final_norm_w=dfnw, conv_w=jnp.stack([smalls[l]["conv_w"] for l in range(L)]), dm_cc=dm_cc)
    adam_grads = [small_parts[k] for k in SMALL_NAMES]
    extras = [small_parts["conv_w"], dm_cc, dm_ex]
    pack = jnp.concatenate([_pack(adam_grads), _pack(extras)], axis=0)
    pack_all = _all_gather("gather_small_grads", [pack])[0]
    wpack = _pack([args[k] for k in SMALL_NAMES])
    mpack = _pack([args["m_" + k] for k in SMALL_NAMES])
    vpack = _pack([args["v_" + k] for k in SMALL_NAMES])
    n_adam = wpack.shape[0]
    res_small = _adamw("adamw_small", [pack_all[:, :n_adam]], wpack, mpack, vpack, n_adam)
    small_out = [_unpack(r, [args[k] for k in SMALL_NAMES]) for r in res_small]
    ext_all = pack_all[:, n_adam:]
    g_conv_full, dm_cc_tot, _ = _unpack(_sum_blocks("sum_small_grads", ext_all), extras)
    dm_ex_all = jnp.stack([_unpack(ext_all[k], extras)[2] for k in range(NDEV)], axis=1)
    dm_rows = jnp.concatenate([dm_ex_all.reshape(L, NDEV * B, 6 * D), dm_cc_tot[:, None, :],
                               jnp.zeros((L, 24 - NDEV * B - 1, 6 * D), F32)], axis=1)
    dm_loc = lax.dynamic_slice(dm_rows, (0, 0, me * CS), (L, 24, CS))
    g_modw, gc_part = _adaln_bwd(cs, dm_loc, mod_w)
    gc_all = _all_gather("gather_cctx_grad", [gc_part[NDEV * B:NDEV * B + 8]])[0]
    cpad = lambda a: jnp.pad(a[None], ((0, 7), (0, 0)))
    res_cc = _adamw("adamw_cctx", [gc_all], cpad(c_ctx), cpad(m_c_ctx), cpad(v_c_ctx), 8)
    cc_out = [r[0] for r in res_cc]

    def waited(name, handles, names):
        srcs, lands = _exchange_wait(name, *handles, grad_x, gather=False)
        own = [lax.dynamic_index_in_dim(s, me, 0, keepdims=False) for s in srcs]
        return dict(zip(names, _with_own(lands, own, me)))

    recv = {0: dict(zip(rest_names, _all_to_all("exchange_grads", blocks[0])))}
    recv[0].update(waited("grads_mlp_wait_0", sent_mlp.pop(0), ["w_mlp1", "w_mlp2"]))
    recv[0].update(waited("grads_mid_wait_0", sent_mid.pop(0), mid_names))
    for l in range(1, L):
        recv[l] = waited("grads_wait_%d" % l, sent.pop(l), big_names)
    big_out = {}
    for name in big_names:
        w = args[name]
        Rr, C = math.prod(w.shape[:-1]), w.shape[-1]
        rl = Rr // L
        rt = rl if rl * C <= (1 << 17) else rl // 4
        res = _adamw("adamw_" + name, [recv[l][name].reshape(NDEV, rl, C) for l in range(L)], w.reshape(Rr, C),
                     args["m_" + name].reshape(Rr, C), args["v_" + name].reshape(Rr, C), rt)
        big_out[name] = [r.reshape(w.shape) for r in res]
    res = _adamw("adamw_mod_w", [g_modw.reshape(1, L * D, CS)], mod_w.reshape(L * D, CS), m_mod_w.reshape(L * D, CS),
                 v_mod_w.reshape(L * D, CS), L * D // 8)
    big_out["mod_w"] = [r.reshape(mod_w.shape) for r in res]
    CW = conv_w.shape[2]
    g_conv = lax.dynamic_slice(g_conv_full, (0, 0, me * CW), (L, 4, CW))
    res = _adamw("adamw_conv_w", [g_conv.reshape(1, L * 4, CW)], conv_w.reshape(L * 4, CW), m_conv_w.reshape(L * 4, CW),
                 v_conv_w.reshape(L * 4, CW), L * 4)
    big_out["conv_w"] = [r.reshape(conv_w.shape) for r in res]

    weights = ["c_ctx", "mod_w", "mod_b", "norm1_w", "norm2_w", "w_in", "conv_w", "conv_b", "dt_bias", "a_log", "ssd_d",
               "ssd_norm_w", "q_a_norm_w", "w_q_b", "kv_a_norm_w", "w_kv_b", "pool_w", "pool_scale", "w_out", "w_mlp1",
               "w_mlp2", "final_norm_w"]
    outs = [loss, grad_x]
    for kind in range(4):
        for name in weights:
            if name == "c_ctx":
                outs.append(cc_out[kind])
            elif name in big_out:
                outs.append(big_out[name][kind])
            else:
                outs.append(small_out[kind][SMALL_NAMES.index(name)])
    return tuple(outs)
```
